```python
import math
import jax
import jax.numpy as jnp
from jax import lax
import numpy as np

D_MODEL = 1024
BATCH = 8
SEQ = 4096
DEPTH = 2


N_EVEN = (DEPTH + 1) // 2
N_ODD = DEPTH // 2
EPS = 1e-6

HG_HEADS = 4
HG_DIM = 128
HG_WIDTH = HG_HEADS * HG_DIM
HG_CHUNK = 64

MLA_HEADS = 4
MLA_Q_RANK = 256
MLA_KV_RANK = 128
MLA_NOPE = 128
MLA_ROPE = 64
MLA_V = 128
MLA_QK = MLA_NOPE + MLA_ROPE
MLA_WIDTH = MLA_HEADS * MLA_V
ATTN_BLOCK = 128
ROPE_BASE = 10000.0

IN_SPLITS = (HG_WIDTH, HG_WIDTH, HG_WIDTH, HG_WIDTH, MLA_Q_RANK, MLA_KV_RANK, MLA_ROPE)
IN_WIDTH = 4 * HG_WIDTH + MLA_Q_RANK + MLA_KV_RANK + MLA_ROPE
MIX_WIDTH = HG_WIDTH + MLA_WIDTH

S5_GROUP = 16
S5_GROUPS = D_MODEL // S5_GROUP
S5_STATE = 64
S5_CHUNK = 128
DT_MIN = 1e-3
DT_MAX = 1e-1

MEM_LEN = 256
XA_HEADS = 4
XA_DIM = D_MODEL // XA_HEADS

D_FF = 2816
CONV_W = 3

kernel_name = 'hybrid_hgrn2_mla_s5_memxattn_convffn'


def rms_norm(x, gain):
    xf = x.astype(jnp.float32)
    y = xf * lax.rsqrt(jnp.mean(xf * xf, axis=-1, keepdims=True) + EPS)
    return (y * gain.astype(jnp.float32)).astype(x.dtype)


def rope_tables(positions):
    inv_freq = 1.0 / (ROPE_BASE ** (jnp.arange(0, MLA_ROPE, 2, dtype=jnp.float32) / MLA_ROPE))
    ang = positions.astype(jnp.float32)[..., None] * inv_freq
    return jnp.cos(ang), jnp.sin(ang)


def apply_rope(x, cos, sin):
    xf = x.astype(jnp.float32)
    x1, x2 = jnp.split(xf, 2, axis=-1)
    c = cos[:, :, None, :]
    s = sin[:, :, None, :]
    return jnp.concatenate([x1 * c - x2 * s, x1 * s + x2 * c], axis=-1).astype(x.dtype)


def hgrn2_recurrence(q, f_logit, i_val, lb):
    bsz, seq, heads, dh = q.shape
    nc = seq // HG_CHUNK
    lbh = lb.astype(jnp.float32).reshape(heads, dh)
    f = lbh + (1.0 - lbh) * jax.nn.sigmoid(f_logit.astype(jnp.float32))
    log_f = jnp.log(f)
    k = 1.0 - f
    qf = jax.nn.silu(q.astype(jnp.float32))
    v = i_val.astype(jnp.float32)

    def to_chunks(t):
        return t.reshape(bsz, nc, HG_CHUNK, heads, dh).transpose(1, 0, 3, 2, 4)

    qc, kc, vc = to_chunks(qf), to_chunks(k), to_chunks(v)
    bc = jnp.cumsum(to_chunks(log_f), axis=3)
    causal = jnp.tril(jnp.ones((HG_CHUNK, HG_CHUNK), dtype=bool))[:, :, None]

    def chunk_step(state, inp):
        qt, kt, vt, bt = inp
        diff = bt[:, :, :, None, :] - bt[:, :, None, :, :]
        decay = jnp.exp(jnp.where(causal, diff, -jnp.inf))
        scores = jnp.einsum('bhtk,bhsk,bhtsk->bhts', qt, kt, decay)
        out = (jnp.einsum('bhts,bhsv->bhtv', scores, vt)
               + jnp.einsum('bhtk,bhkv->bhtv', qt * jnp.exp(bt), state))
        b_end = bt[:, :, -1:, :]
        new_state = (jnp.exp(b_end[:, :, 0, :])[..., None] * state
                     + jnp.einsum('bhsk,bhsv->bhkv', kt * jnp.exp(b_end - bt), vt))
        return new_state, out

    state0 = jnp.zeros((bsz, heads, dh, dh), jnp.float32)
    _, out = lax.scan(chunk_step, state0, (qc, kc, vc, bc))
    return out.transpose(1, 0, 3, 2, 4).reshape(bsz, seq, heads, dh)


def causal_block_attention(q, k, v, scale):
    bsz, seq, heads, dk = q.shape
    nb = seq // ATTN_BLOCK
    q_blocks = q.reshape(bsz, nb, ATTN_BLOCK, heads, dk).transpose(1, 0, 2, 3, 4)
    k_pos = jnp.arange(seq)

    def one_block(args):
        qb, b_idx = args
        s = jnp.einsum('bqhd,bkhd->bhqk', qb, k, preferred_element_type=jnp.float32) * scale
        q_pos = b_idx * ATTN_BLOCK + jnp.arange(ATTN_BLOCK)
        s = jnp.where(k_pos[None, :] <= q_pos[:, None], s, -jnp.inf)
        p = jax.nn.softmax(s, axis=-1).astype(v.dtype)
        return jnp.einsum('bhqk,bkhd->bqhd', p, v)

    out = lax.map(one_block, (q_blocks, jnp.arange(nb)))
    return out.transpose(1, 0, 2, 3, 4).reshape(bsz, seq, heads, v.shape[-1])


def hgrn2_mla_mixer(h, cos, sin, lb, w_in, hg_out_norm, q_a_norm, w_uq, kv_a_norm, w_ukv,
                    qn_nope, qn_rope, kn_nope, kn_rope, w_out):
    bsz, seq, _ = h.shape
    split_at = np.cumsum(IN_SPLITS)[:-1].tolist()
    q_hg, f_hg, i_hg, g_hg, c_q, c_kv, k_pe = jnp.split(h @ w_in, split_at, axis=-1)

    def as_heads(t):
        return t.reshape(bsz, seq, HG_HEADS, HG_DIM)
    o_hg = hgrn2_recurrence(as_heads(q_hg), as_heads(f_hg), as_heads(i_hg), lb).astype(h.dtype)
    o_hg = rms_norm(o_hg, hg_out_norm.reshape(HG_HEADS, HG_DIM)).reshape(bsz, seq, HG_WIDTH)
    o_hg = o_hg * jax.nn.silu(g_hg)

    q = (rms_norm(c_q, q_a_norm) @ w_uq).reshape(bsz, seq, MLA_HEADS, MLA_QK)
    kv = (rms_norm(c_kv, kv_a_norm) @ w_ukv).reshape(bsz, seq, MLA_HEADS, MLA_NOPE + MLA_V)
    q_nope = rms_norm(q[..., :MLA_NOPE], qn_nope)
    q_pe = apply_rope(rms_norm(q[..., MLA_NOPE:], qn_rope), cos, sin)
    k_nope = rms_norm(kv[..., :MLA_NOPE], kn_nope)
    v = kv[..., MLA_NOPE:]
    k_pe = apply_rope(rms_norm(k_pe, kn_rope)[:, :, None, :], cos, sin)
    k_pe = jnp.broadcast_to(k_pe, (bsz, seq, MLA_HEADS, MLA_ROPE))
    q_full = jnp.concatenate([q_nope, q_pe], axis=-1)
    k_full = jnp.concatenate([k_nope, k_pe], axis=-1)
    o_mla = causal_block_attention(q_full, k_full, v, MLA_QK ** -0.5).reshape(bsz, seq, MLA_WIDTH)

    return jnp.concatenate([o_hg, o_mla], axis=-1) @ w_out


def complex_affine_combine(e1, e2):
    a1r, a1i, b1r, b1i = e1
    a2r, a2i, b2r, b2i = e2
    ar = a2r * a1r - a2i * a1i
    ai = a2r * a1i + a2i * a1r
    br = a2r * b1r - a2i * b1i + b2r
    bi = a2r * b1i + a2i * b1r + b2i
    return ar, ai, br, bi


def s5_mixer(u, lam_re, lam_im, log_dt, b_re, b_im, c_re, c_im, d_skip, w_glu_a, w_glu_b):
    bsz, seq, _ = u.shape
    nc = seq // S5_CHUNK
    f32 = jnp.float32
    lr = lam_re.astype(f32)
    li = lam_im.astype(f32)
    dt = jnp.exp(log_dt.astype(f32))[:, None]
    mag = jnp.exp(lr * dt)
    ab_re = mag * jnp.cos(li * dt)
    ab_im = mag * jnp.sin(li * dt)
    den = lr * lr + li * li
    z_re = ((ab_re - 1.0) * lr + ab_im * li) / den
    z_im = (ab_im * lr - (ab_re - 1.0) * li) / den
    br = b_re.astype(f32)
    bi = b_im.astype(f32)
    bb_re = z_re[..., None] * br - z_im[..., None] * bi
    bb_im = z_re[..., None] * bi + z_im[..., None] * br
    cr = c_re.astype(f32)
    ci = c_im.astype(f32)
    uf = u.astype(f32)
    u_chunks = uf.reshape(bsz, nc, S5_CHUNK, S5_GROUPS, S5_GROUP).transpose(1, 0, 2, 3, 4)

    def chunk_step(carry, uc):
        hr0, hi0 = carry
        bu_re = jnp.einsum('bcgm,gpm->bcgp', uc, bb_re)
        bu_im = jnp.einsum('bcgm,gpm->bcgp', uc, bb_im)
        a_re = jnp.broadcast_to(ab_re, bu_re.shape)
        a_im = jnp.broadcast_to(ab_im, bu_im.shape)
        p_re, p_im, s_re, s_im = lax.associative_scan(
            complex_affine_combine, (a_re, a_im, bu_re, bu_im), axis=1)
        hr = s_re + p_re * hr0[:, None] - p_im * hi0[:, None]
        hi = s_im + p_re * hi0[:, None] + p_im * hr0[:, None]
        y = jnp.einsum('bcgp,gmp->bcgm', hr, cr) - jnp.einsum('bcgp,gmp->bcgm', hi, ci)
        return (hr[:, -1], hi[:, -1]), y

    init = (jnp.zeros((bsz, S5_GROUPS, S5_STATE), f32), jnp.zeros((bsz, S5_GROUPS, S5_STATE), f32))
    _, y = lax.scan(chunk_step, init, u_chunks)
    y = y.transpose(1, 0, 2, 3, 4).reshape(bsz, seq, D_MODEL) + d_skip.astype(f32) * uf
    y = jax.nn.gelu(y).astype(u.dtype)
    return (y @ w_glu_a) * jax.nn.sigmoid(y @ w_glu_b)


def memory_cross_attention(h, mem, mem_gain, wq, wk, wv, wo, q_gain, k_gain):
    bsz, seq, _ = h.shape
    m_len = mem.shape[1]
    m = rms_norm(mem, mem_gain)
    q = rms_norm((h @ wq).reshape(bsz, seq, XA_HEADS, XA_DIM), q_gain)
    k = rms_norm((m @ wk).reshape(bsz, m_len, XA_HEADS, XA_DIM), k_gain)
    v = (m @ wv).reshape(bsz, m_len, XA_HEADS, XA_DIM)
    s = jnp.einsum('bqhd,bkhd->bhqk', q, k, preferred_element_type=jnp.float32) * (XA_DIM ** -0.5)
    p = jax.nn.softmax(s, axis=-1).astype(v.dtype)
    o = jnp.einsum('bhqk,bkhd->bqhd', p, v).reshape(bsz, seq, D_MODEL)
    return o @ wo


def conv_gated_mlp(h, w_up, conv_w, conv_b, w_down):
    u = h @ w_up
    u = lax.conv_general_dilated(
        u, conv_w[:, None, :].astype(u.dtype), window_strides=(1,), padding=[(CONV_W - 1, 0)],
        dimension_numbers=('NWC', 'WIO', 'NWC'), feature_group_count=2 * D_FF) + conv_b
    gate, val = jnp.split(u, 2, axis=-1)
    return (jax.nn.silu(gate) * val) @ w_down


def _fwd_setup_inputs(seed: int = 0) -> dict:
    key = jax.random.key(seed)
    keys = jax.random.split(key, 48)
    counter = [0]

    def nxt():
        kk = keys[counter[0]]
        counter[0] += 1
        return kk

    def nrm(shape, scale):
        return scale * jax.random.normal(nxt(), shape, jnp.float32)

    def gain(shape):
        return 1.0 + nrm(shape, 0.02)

    d = D_MODEL
    inp = {}
    inp['x'] = nrm((BATCH, SEQ, d), 1.0)
    inp['mem'] = nrm((BATCH, MEM_LEN, d), 1.0)
    inp['positions'] = (jnp.arange(SEQ, dtype=jnp.int32)[None, :]
                        + jax.random.randint(nxt(), (BATCH, 1), 0, 1024, dtype=jnp.int32))
    inp['norm_mix'] = gain((DEPTH, d))
    inp['norm_xa'] = gain((DEPTH, d))
    inp['norm_mem'] = gain((DEPTH, d))
    inp['norm_ffn'] = gain((DEPTH, d))
    inp['xa_wq'] = nrm((DEPTH, d, d), d ** -0.5)
    inp['xa_wk'] = nrm((DEPTH, d, d), d ** -0.5)
    inp['xa_wv'] = nrm((DEPTH, d, d), d ** -0.5)
    inp['xa_wo'] = nrm((DEPTH, d, d), d ** -0.5)
    inp['xa_q_norm'] = gain((DEPTH, XA_DIM))
    inp['xa_k_norm'] = gain((DEPTH, XA_DIM))
    inp['ffn_w_up'] = nrm((DEPTH, d, 2 * D_FF), d ** -0.5)
    inp['ffn_conv_w'] = nrm((DEPTH, CONV_W, 2 * D_FF), CONV_W ** -0.5)
    inp['ffn_conv_b'] = nrm((DEPTH, 2 * D_FF), 0.01)
    inp['ffn_w_down'] = nrm((DEPTH, D_FF, d), D_FF ** -0.5)
    inp['hg_lb_logits'] = nrm((N_EVEN + 1, HG_WIDTH), 0.1)
    inp['mix_w_in'] = nrm((N_EVEN, d, IN_WIDTH), d ** -0.5)
    inp['hg_out_norm'] = gain((N_EVEN, HG_WIDTH))
    inp['mla_q_a_norm'] = gain((N_EVEN, MLA_Q_RANK))
    inp['mla_w_uq'] = nrm((N_EVEN, MLA_Q_RANK, MLA_HEADS * MLA_QK), MLA_Q_RANK ** -0.5)
    inp['mla_kv_a_norm'] = gain((N_EVEN, MLA_KV_RANK))
    inp['mla_w_ukv'] = nrm((N_EVEN, MLA_KV_RANK, MLA_HEADS * (MLA_NOPE + MLA_V)), MLA_KV_RANK ** -0.5)
    inp['mla_qn_nope'] = gain((N_EVEN, MLA_NOPE))
    inp['mla_qn_rope'] = gain((N_EVEN, MLA_ROPE))
    inp['mla_kn_nope'] = gain((N_EVEN, MLA_NOPE))
    inp['mla_kn_rope'] = gain((N_EVEN, MLA_ROPE))
    inp['mix_w_out'] = nrm((N_EVEN, MIX_WIDTH, d), MIX_WIDTH ** -0.5)
    inp['s5_lam_re'] = -0.5 + nrm((N_ODD, S5_GROUPS, S5_STATE), 0.01)
    inp['s5_lam_im'] = (math.pi * jnp.arange(S5_STATE, dtype=jnp.float32)[None, None, :]
                        + nrm((N_ODD, S5_GROUPS, S5_STATE), 0.01))
    inp['s5_log_dt'] = jax.random.uniform(nxt(), (N_ODD, S5_GROUPS), jnp.float32,
                                          math.log(DT_MIN), math.log(DT_MAX))
    inp['s5_b_re'] = nrm((N_ODD, S5_GROUPS, S5_STATE, S5_GROUP), (2 * S5_GROUP) ** -0.5)
    inp['s5_b_im'] = nrm((N_ODD, S5_GROUPS, S5_STATE, S5_GROUP), (2 * S5_GROUP) ** -0.5)
    inp['s5_c_re'] = nrm((N_ODD, S5_GROUPS, S5_GROUP, S5_STATE), S5_STATE ** -0.5)
    inp['s5_c_im'] = nrm((N_ODD, S5_GROUPS, S5_GROUP, S5_STATE), S5_STATE ** -0.5)
    inp['s5_d'] = nrm((N_ODD, d), 1.0)
    inp['s5_w_glu_a'] = nrm((N_ODD, d, d), d ** -0.5)
    inp['s5_w_glu_b'] = nrm((N_ODD, d, d), d ** -0.5)
    return inp


def _fwd_reference(x, mem, positions, norm_mix, norm_xa, norm_mem, norm_ffn,
              xa_wq, xa_wk, xa_wv, xa_wo, xa_q_norm, xa_k_norm,
              ffn_w_up, ffn_conv_w, ffn_conv_b, ffn_w_down,
              hg_lb_logits, mix_w_in, hg_out_norm, mla_q_a_norm, mla_w_uq, mla_kv_a_norm, mla_w_ukv,
              mla_qn_nope, mla_qn_rope, mla_kn_nope, mla_kn_rope, mix_w_out,
              s5_lam_re, s5_lam_im, s5_log_dt, s5_b_re, s5_b_im, s5_c_re, s5_c_im, s5_d,
              s5_w_glu_a, s5_w_glu_b):
    cos, sin = rope_tables(positions)
    lb_all = jnp.cumsum(jax.nn.softmax(hg_lb_logits.astype(jnp.float32), axis=0), axis=0)
    h = x
    for layer in range(DEPTH):
        j = layer // 2
        hn = rms_norm(h, norm_mix[layer])
        if layer % 2 == 0:
            mix = hgrn2_mla_mixer(hn, cos, sin, lb_all[j], mix_w_in[j], hg_out_norm[j],
                                  mla_q_a_norm[j], mla_w_uq[j], mla_kv_a_norm[j], mla_w_ukv[j],
                                  mla_qn_nope[j], mla_qn_rope[j], mla_kn_nope[j], mla_kn_rope[j],
                                  mix_w_out[j])
        else:
            mix = s5_mixer(hn, s5_lam_re[j], s5_lam_im[j], s5_log_dt[j], s5_b_re[j], s5_b_im[j],
                           s5_c_re[j], s5_c_im[j], s5_d[j], s5_w_glu_a[j], s5_w_glu_b[j])
        h = h + mix
        h = h + memory_cross_attention(rms_norm(h, norm_xa[layer]), mem, norm_mem[layer],
                                       xa_wq[layer], xa_wk[layer], xa_wv[layer], xa_wo[layer],
                                       xa_q_norm[layer], xa_k_norm[layer])
        h = h + conv_gated_mlp(rms_norm(h, norm_ffn[layer]), ffn_w_up[layer], ffn_conv_w[layer],
                               ffn_conv_b[layer], ffn_w_down[layer])
    return h


import jax as _jax
import jax.numpy as _jnp

TWIN_FORMAT = 'train_step'
FWD_PARAMS = ['x', 'mem', 'positions', 'norm_mix', 'norm_xa', 'norm_mem', 'norm_ffn', 'xa_wq', 'xa_wk', 'xa_wv', 'xa_wo', 'xa_q_norm', 'xa_k_norm', 'ffn_w_up', 'ffn_conv_w', 'ffn_conv_b', 'ffn_w_down', 'hg_lb_logits', 'mix_w_in', 'hg_out_norm', 'mla_q_a_norm', 'mla_w_uq', 'mla_kv_a_norm', 'mla_w_ukv', 'mla_qn_nope', 'mla_qn_rope', 'mla_kn_nope', 'mla_kn_rope', 'mix_w_out', 's5_lam_re', 's5_lam_im', 's5_log_dt', 's5_b_re', 's5_b_im', 's5_c_re', 's5_c_im', 's5_d', 's5_w_glu_a', 's5_w_glu_b']
TWIN_WEIGHTS = ['norm_mix', 'norm_xa', 'norm_mem', 'norm_ffn', 'xa_wq', 'xa_wk', 'xa_wv', 'xa_wo', 'xa_q_norm', 'xa_k_norm', 'ffn_w_up', 'ffn_conv_w', 'ffn_conv_b', 'ffn_w_down', 'hg_lb_logits', 'mix_w_in', 'hg_out_norm', 'mla_q_a_norm', 'mla_w_uq', 'mla_kv_a_norm', 'mla_w_ukv', 'mla_qn_nope', 'mla_qn_rope', 'mla_kn_nope', 'mla_kn_rope', 'mix_w_out', 's5_lam_re', 's5_lam_im', 's5_log_dt', 's5_b_re', 's5_b_im', 's5_c_re', 's5_c_im', 's5_d', 's5_w_glu_a', 's5_w_glu_b']
TWIN_DIFF_INPUT = 'x'
TWIN_INPUTS = ['x', 'mem', 'positions', 'norm_mix', 'norm_xa', 'norm_mem', 'norm_ffn', 'xa_wq', 'xa_wk', 'xa_wv', 'xa_wo', 'xa_q_norm', 'xa_k_norm', 'ffn_w_up', 'ffn_conv_w', 'ffn_conv_b', 'ffn_w_down', 'hg_lb_logits', 'mix_w_in', 'hg_out_norm', 'mla_q_a_norm', 'mla_w_uq', 'mla_kv_a_norm', 'mla_w_ukv', 'mla_qn_nope', 'mla_qn_rope', 'mla_kn_nope', 'mla_kn_rope', 'mix_w_out', 's5_lam_re', 's5_lam_im', 's5_log_dt', 's5_b_re', 's5_b_im', 's5_c_re', 's5_c_im', 's5_d', 's5_w_glu_a', 's5_w_glu_b', 'loss_target', 'm_norm_mix', 'm_norm_xa', 'm_norm_mem', 'm_norm_ffn', 'm_xa_wq', 'm_xa_wk', 'm_xa_wv', 'm_xa_wo', 'm_xa_q_norm', 'm_xa_k_norm', 'm_ffn_w_up', 'm_ffn_conv_w', 'm_ffn_conv_b', 'm_ffn_w_down', 'm_hg_lb_logits', 'm_mix_w_in', 'm_hg_out_norm', 'm_mla_q_a_norm', 'm_mla_w_uq', 'm_mla_kv_a_norm', 'm_mla_w_ukv', 'm_mla_qn_nope', 'm_mla_qn_rope', 'm_mla_kn_nope', 'm_mla_kn_rope', 'm_mix_w_out', 'm_s5_lam_re', 'm_s5_lam_im', 'm_s5_log_dt', 'm_s5_b_re', 'm_s5_b_im', 'm_s5_c_re', 'm_s5_c_im', 'm_s5_d', 'm_s5_w_glu_a', 'm_s5_w_glu_b', 'v_norm_mix', 'v_norm_xa', 'v_norm_mem', 'v_norm_ffn', 'v_xa_wq', 'v_xa_wk', 'v_xa_wv', 'v_xa_wo', 'v_xa_q_norm', 'v_xa_k_norm', 'v_ffn_w_up', 'v_ffn_conv_w', 'v_ffn_conv_b', 'v_ffn_w_down', 'v_hg_lb_logits', 'v_mix_w_in', 'v_hg_out_norm', 'v_mla_q_a_norm', 'v_mla_w_uq', 'v_mla_kv_a_norm', 'v_mla_w_ukv', 'v_mla_qn_nope', 'v_mla_qn_rope', 'v_mla_kn_nope', 'v_mla_kn_rope', 'v_mix_w_out', 'v_s5_lam_re', 'v_s5_lam_im', 'v_s5_log_dt', 'v_s5_b_re', 'v_s5_b_im', 'v_s5_c_re', 'v_s5_c_im', 'v_s5_d', 'v_s5_w_glu_a', 'v_s5_w_glu_b']
TWIN_OUTPUTS = ['loss', 'grad_x', 'grad_norm_mix', 'grad_norm_xa', 'grad_norm_mem', 'grad_norm_ffn', 'grad_xa_wq', 'grad_xa_wk', 'grad_xa_wv', 'grad_xa_wo', 'grad_xa_q_norm', 'grad_xa_k_norm', 'grad_ffn_w_up', 'grad_ffn_conv_w', 'grad_ffn_conv_b', 'grad_ffn_w_down', 'grad_hg_lb_logits', 'grad_mix_w_in', 'grad_hg_out_norm', 'grad_mla_q_a_norm', 'grad_mla_w_uq', 'grad_mla_kv_a_norm', 'grad_mla_w_ukv', 'grad_mla_qn_nope', 'grad_mla_qn_rope', 'grad_mla_kn_nope', 'grad_mla_kn_rope', 'grad_mix_w_out', 'grad_s5_lam_re', 'grad_s5_lam_im', 'grad_s5_log_dt', 'grad_s5_b_re', 'grad_s5_b_im', 'grad_s5_c_re', 'grad_s5_c_im', 'grad_s5_d', 'grad_s5_w_glu_a', 'grad_s5_w_glu_b', 'delta_norm_mix', 'delta_norm_xa', 'delta_norm_mem', 'delta_norm_ffn', 'delta_xa_wq', 'delta_xa_wk', 'delta_xa_wv', 'delta_xa_wo', 'delta_xa_q_norm', 'delta_xa_k_norm', 'delta_ffn_w_up', 'delta_ffn_conv_w', 'delta_ffn_conv_b', 'delta_ffn_w_down', 'delta_hg_lb_logits', 'delta_mix_w_in', 'delta_hg_out_norm', 'delta_mla_q_a_norm', 'delta_mla_w_uq', 'delta_mla_kv_a_norm', 'delta_mla_w_ukv', 'delta_mla_qn_nope', 'delta_mla_qn_rope', 'delta_mla_kn_nope', 'delta_mla_kn_rope', 'delta_mix_w_out', 'delta_s5_lam_re', 'delta_s5_lam_im', 'delta_s5_log_dt', 'delta_s5_b_re', 'delta_s5_b_im', 'delta_s5_c_re', 'delta_s5_c_im', 'delta_s5_d', 'delta_s5_w_glu_a', 'delta_s5_w_glu_b', 'new_m_norm_mix', 'new_m_norm_xa', 'new_m_norm_mem', 'new_m_norm_ffn', 'new_m_xa_wq', 'new_m_xa_wk', 'new_m_xa_wv', 'new_m_xa_wo', 'new_m_xa_q_norm', 'new_m_xa_k_norm', 'new_m_ffn_w_up', 'new_m_ffn_conv_w', 'new_m_ffn_conv_b', 'new_m_ffn_w_down', 'new_m_hg_lb_logits', 'new_m_mix_w_in', 'new_m_hg_out_norm', 'new_m_mla_q_a_norm', 'new_m_mla_w_uq', 'new_m_mla_kv_a_norm', 'new_m_mla_w_ukv', 'new_m_mla_qn_nope', 'new_m_mla_qn_rope', 'new_m_mla_kn_nope', 'new_m_mla_kn_rope', 'new_m_mix_w_out', 'new_m_s5_lam_re', 'new_m_s5_lam_im', 'new_m_s5_log_dt', 'new_m_s5_b_re', 'new_m_s5_b_im', 'new_m_s5_c_re', 'new_m_s5_c_im', 'new_m_s5_d', 'new_m_s5_w_glu_a', 'new_m_s5_w_glu_b', 'new_v_norm_mix', 'new_v_norm_xa', 'new_v_norm_mem', 'new_v_norm_ffn', 'new_v_xa_wq', 'new_v_xa_wk', 'new_v_xa_wv', 'new_v_xa_wo', 'new_v_xa_q_norm', 'new_v_xa_k_norm', 'new_v_ffn_w_up', 'new_v_ffn_conv_w', 'new_v_ffn_conv_b', 'new_v_ffn_w_down', 'new_v_hg_lb_logits', 'new_v_mix_w_in', 'new_v_hg_out_norm', 'new_v_mla_q_a_norm', 'new_v_mla_w_uq', 'new_v_mla_kv_a_norm', 'new_v_mla_w_ukv', 'new_v_mla_qn_nope', 'new_v_mla_qn_rope', 'new_v_mla_kn_nope', 'new_v_mla_kn_rope', 'new_v_mix_w_out', 'new_v_s5_lam_re', 'new_v_s5_lam_im', 'new_v_s5_log_dt', 'new_v_s5_b_re', 'new_v_s5_b_im', 'new_v_s5_c_re', 'new_v_s5_c_im', 'new_v_s5_d', 'new_v_s5_w_glu_a', 'new_v_s5_w_glu_b']
TWIN_LEAF_KINDS = {'loss': 'loss', 'grad_x': 'grad_x', 'grad_norm_mix': 'grad_w', 'grad_norm_xa': 'grad_w', 'grad_norm_mem': 'grad_w', 'grad_norm_ffn': 'grad_w', 'grad_xa_wq': 'grad_w', 'grad_xa_wk': 'grad_w', 'grad_xa_wv': 'grad_w', 'grad_xa_wo': 'grad_w', 'grad_xa_q_norm': 'grad_w', 'grad_xa_k_norm': 'grad_w', 'grad_ffn_w_up': 'grad_w', 'grad_ffn_conv_w': 'grad_w', 'grad_ffn_conv_b': 'grad_w', 'grad_ffn_w_down': 'grad_w', 'grad_hg_lb_logits': 'grad_w', 'grad_mix_w_in': 'grad_w', 'grad_hg_out_norm': 'grad_w', 'grad_mla_q_a_norm': 'grad_w', 'grad_mla_w_uq': 'grad_w', 'grad_mla_kv_a_norm': 'grad_w', 'grad_mla_w_ukv': 'grad_w', 'grad_mla_qn_nope': 'grad_w', 'grad_mla_qn_rope': 'grad_w', 'grad_mla_kn_nope': 'grad_w', 'grad_mla_kn_rope': 'grad_w', 'grad_mix_w_out': 'grad_w', 'grad_s5_lam_re': 'grad_w', 'grad_s5_lam_im': 'grad_w', 'grad_s5_log_dt': 'grad_w', 'grad_s5_b_re': 'grad_w', 'grad_s5_b_im': 'grad_w', 'grad_s5_c_re': 'grad_w', 'grad_s5_c_im': 'grad_w', 'grad_s5_d': 'grad_w', 'grad_s5_w_glu_a': 'grad_w', 'grad_s5_w_glu_b': 'grad_w', 'delta_norm_mix': 'delta_w', 'delta_norm_xa': 'delta_w', 'delta_norm_mem': 'delta_w', 'delta_norm_ffn': 'delta_w', 'delta_xa_wq': 'delta_w', 'delta_xa_wk': 'delta_w', 'delta_xa_wv': 'delta_w', 'delta_xa_wo': 'delta_w', 'delta_xa_q_norm': 'delta_w', 'delta_xa_k_norm': 'delta_w', 'delta_ffn_w_up': 'delta_w', 'delta_ffn_conv_w': 'delta_w', 'delta_ffn_conv_b': 'delta_w', 'delta_ffn_w_down': 'delta_w', 'delta_hg_lb_logits': 'delta_w', 'delta_mix_w_in': 'delta_w', 'delta_hg_out_norm': 'delta_w', 'delta_mla_q_a_norm': 'delta_w', 'delta_mla_w_uq': 'delta_w', 'delta_mla_kv_a_norm': 'delta_w', 'delta_mla_w_ukv': 'delta_w', 'delta_mla_qn_nope': 'delta_w', 'delta_mla_qn_rope': 'delta_w', 'delta_mla_kn_nope': 'delta_w', 'delta_mla_kn_rope': 'delta_w', 'delta_mix_w_out': 'delta_w', 'delta_s5_lam_re': 'delta_w', 'delta_s5_lam_im': 'delta_w', 'delta_s5_log_dt': 'delta_w', 'delta_s5_b_re': 'delta_w', 'delta_s5_b_im': 'delta_w', 'delta_s5_c_re': 'delta_w', 'delta_s5_c_im': 'delta_w', 'delta_s5_d': 'delta_w', 'delta_s5_w_glu_a': 'delta_w', 'delta_s5_w_glu_b': 'delta_w', 'new_m_norm_mix': 'new_m', 'new_m_norm_xa': 'new_m', 'new_m_norm_mem': 'new_m', 'new_m_norm_ffn': 'new_m', 'new_m_xa_wq': 'new_m', 'new_m_xa_wk': 'new_m', 'new_m_xa_wv': 'new_m', 'new_m_xa_wo': 'new_m', 'new_m_xa_q_norm': 'new_m', 'new_m_xa_k_norm': 'new_m', 'new_m_ffn_w_up': 'new_m', 'new_m_ffn_conv_w': 'new_m', 'new_m_ffn_conv_b': 'new_m', 'new_m_ffn_w_down': 'new_m', 'new_m_hg_lb_logits': 'new_m', 'new_m_mix_w_in': 'new_m', 'new_m_hg_out_norm': 'new_m', 'new_m_mla_q_a_norm': 'new_m', 'new_m_mla_w_uq': 'new_m', 'new_m_mla_kv_a_norm': 'new_m', 'new_m_mla_w_ukv': 'new_m', 'new_m_mla_qn_nope': 'new_m', 'new_m_mla_qn_rope': 'new_m', 'new_m_mla_kn_nope': 'new_m', 'new_m_mla_kn_rope': 'new_m', 'new_m_mix_w_out': 'new_m', 'new_m_s5_lam_re': 'new_m', 'new_m_s5_lam_im': 'new_m', 'new_m_s5_log_dt': 'new_m', 'new_m_s5_b_re': 'new_m', 'new_m_s5_b_im': 'new_m', 'new_m_s5_c_re': 'new_m', 'new_m_s5_c_im': 'new_m', 'new_m_s5_d': 'new_m', 'new_m_s5_w_glu_a': 'new_m', 'new_m_s5_w_glu_b': 'new_m', 'new_v_norm_mix': 'new_v', 'new_v_norm_xa': 'new_v', 'new_v_norm_mem': 'new_v', 'new_v_norm_ffn': 'new_v', 'new_v_xa_wq': 'new_v', 'new_v_xa_wk': 'new_v', 'new_v_xa_wv': 'new_v', 'new_v_xa_wo': 'new_v', 'new_v_xa_q_norm': 'new_v', 'new_v_xa_k_norm': 'new_v', 'new_v_ffn_w_up': 'new_v', 'new_v_ffn_conv_w': 'new_v', 'new_v_ffn_conv_b': 'new_v', 'new_v_ffn_w_down': 'new_v', 'new_v_hg_lb_logits': 'new_v', 'new_v_mix_w_in': 'new_v', 'new_v_hg_out_norm': 'new_v', 'new_v_mla_q_a_norm': 'new_v', 'new_v_mla_w_uq': 'new_v', 'new_v_mla_kv_a_norm': 'new_v', 'new_v_mla_w_ukv': 'new_v', 'new_v_mla_qn_nope': 'new_v', 'new_v_mla_qn_rope': 'new_v', 'new_v_mla_kn_nope': 'new_v', 'new_v_mla_kn_rope': 'new_v', 'new_v_mix_w_out': 'new_v', 'new_v_s5_lam_re': 'new_v', 'new_v_s5_lam_im': 'new_v', 'new_v_s5_log_dt': 'new_v', 'new_v_s5_b_re': 'new_v', 'new_v_s5_b_im': 'new_v', 'new_v_s5_c_re': 'new_v', 'new_v_s5_c_im': 'new_v', 'new_v_s5_d': 'new_v', 'new_v_s5_w_glu_a': 'new_v', 'new_v_s5_w_glu_b': 'new_v'}


def _forward(args):
    return _fwd_reference(*[args[k] for k in FWD_PARAMS])


def _output_shape():
    out = _jax.eval_shape(lambda: _forward(_fwd_setup_inputs(0)))
    return out.shape, out.dtype

N_MICROBATCH = 1
ADAM_LR = 0.001
ADAM_B1 = 0.9
ADAM_B2 = 0.999
ADAM_EPS = 1e-08
ADAM_WD = 0.01
ADAM_STEP = 10
PER_EXAMPLE_BATCH_AXIS = {'x': 0, 'mem': 0, 'positions': 0, 'loss_target': 0}
SHARED_INPUTS = []
_WEIGHT_DTYPES = {'norm_mix': _jnp.float32, 'norm_xa': _jnp.float32, 'norm_mem': _jnp.float32, 'norm_ffn': _jnp.float32, 'xa_wq': _jnp.float32, 'xa_wk': _jnp.float32, 'xa_wv': _jnp.float32, 'xa_wo': _jnp.float32, 'xa_q_norm': _jnp.float32, 'xa_k_norm': _jnp.float32, 'ffn_w_up': _jnp.float32, 'ffn_conv_w': _jnp.float32, 'ffn_conv_b': _jnp.float32, 'ffn_w_down': _jnp.float32, 'hg_lb_logits': _jnp.float32, 'mix_w_in': _jnp.float32, 'hg_out_norm': _jnp.float32, 'mla_q_a_norm': _jnp.float32, 'mla_w_uq': _jnp.float32, 'mla_kv_a_norm': _jnp.float32, 'mla_w_ukv': _jnp.float32, 'mla_qn_nope': _jnp.float32, 'mla_qn_rope': _jnp.float32, 'mla_kn_nope': _jnp.float32, 'mla_kn_rope': _jnp.float32, 'mix_w_out': _jnp.float32, 's5_lam_re': _jnp.float32, 's5_lam_im': _jnp.float32, 's5_log_dt': _jnp.float32, 's5_b_re': _jnp.float32, 's5_b_im': _jnp.float32, 's5_c_re': _jnp.float32, 's5_c_im': _jnp.float32, 's5_d': _jnp.float32, 's5_w_glu_a': _jnp.float32, 's5_w_glu_b': _jnp.float32}
MOMENT_SCALE = {'norm_mix': 7.796455e+00, 'norm_xa': 7.479997e-02, 'norm_mem': 4.923922e-01, 'norm_ffn': 2.525463e+01, 'xa_wq': 7.318518e-02, 'xa_wk': 7.316564e-02, 'xa_wv': 3.629433e-01, 'xa_wo': 3.467663e-01, 'xa_q_norm': 1.273450e+00, 'xa_k_norm': 1.268736e+00, 'ffn_w_up': 4.014301e-01, 'ffn_conv_w': 3.572628e+00, 'ffn_conv_b': 3.285003e+00, 'ffn_w_down': 3.828127e-01, 'hg_lb_logits': 3.892806e-02, 'mix_w_in': 3.000295e-01, 'hg_out_norm': 1.076507e+01, 'mla_q_a_norm': 1.762339e-01, 'mla_w_uq': 1.042737e-01, 'mla_kv_a_norm': 1.184837e+00, 'mla_w_ukv': 1.604460e-01, 'mla_qn_nope': 5.602998e-01, 'mla_qn_rope': 4.293870e-01, 'mla_kn_nope': 5.638407e-01, 'mla_kn_rope': 4.418940e-01, 'mix_w_out': 3.412490e-01, 's5_lam_re': 3.593188e-02, 's5_lam_im': 1.893825e-02, 's5_log_dt': 1.252592e+01, 's5_b_re': 1.758583e-02, 's5_b_im': 1.753040e-02, 's5_c_re': 1.971100e-02, 's5_c_im': 2.227926e-02, 's5_d': 5.203361e+00, 's5_w_glu_a': 1.813149e+00, 's5_w_glu_b': 6.532006e-01}


def _to_microbatches(a, axis):
    t = _jnp.moveaxis(a, axis, 0)
    t = t.reshape((N_MICROBATCH, t.shape[0] // N_MICROBATCH) + t.shape[1:])
    return _jnp.moveaxis(t, 1, axis + 1)


def setup_inputs(seed: int = 0) -> dict:
    inp = _fwd_setup_inputs(seed)
    key = _jax.random.fold_in(_jax.random.key(seed), 7919)
    shape, _ = _output_shape()
    out = dict(inp)
    out["loss_target"] = _jax.random.normal(_jax.random.fold_in(key, 0), shape, _jnp.float32)
    for i, name in enumerate(TWIN_WEIGHTS):
        w = inp[name].astype(_jnp.float32)
        if MOMENT_SCALE is None:
            s = _jnp.sqrt(_jnp.mean(_jnp.square(w)) + 1e-30)
        else:
            s = MOMENT_SCALE[name]
        km, kv = _jax.random.split(_jax.random.fold_in(key, i + 1))
        out[name] = w
        out["m_" + name] = s * _jax.random.normal(km, w.shape, _jnp.float32)
        out["v_" + name] = (s * s) * _jax.random.uniform(kv, w.shape, _jnp.float32, 0.5, 1.5)
    if N_MICROBATCH > 1:
        for name, axis in PER_EXAMPLE_BATCH_AXIS.items():
            out[name] = _to_microbatches(out[name], axis)
    return {'x': out['x'], 'mem': out['mem'], 'positions': out['positions'], 'norm_mix': out['norm_mix'], 'norm_xa': out['norm_xa'], 'norm_mem': out['norm_mem'], 'norm_ffn': out['norm_ffn'], 'xa_wq': out['xa_wq'], 'xa_wk': out['xa_wk'], 'xa_wv': out['xa_wv'], 'xa_wo': out['xa_wo'], 'xa_q_norm': out['xa_q_norm'], 'xa_k_norm': out['xa_k_norm'], 'ffn_w_up': out['ffn_w_up'], 'ffn_conv_w': out['ffn_conv_w'], 'ffn_conv_b': out['ffn_conv_b'], 'ffn_w_down': out['ffn_w_down'], 'hg_lb_logits': out['hg_lb_logits'], 'mix_w_in': out['mix_w_in'], 'hg_out_norm': out['hg_out_norm'], 'mla_q_a_norm': out['mla_q_a_norm'], 'mla_w_uq': out['mla_w_uq'], 'mla_kv_a_norm': out['mla_kv_a_norm'], 'mla_w_ukv': out['mla_w_ukv'], 'mla_qn_nope': out['mla_qn_nope'], 'mla_qn_rope': out['mla_qn_rope'], 'mla_kn_nope': out['mla_kn_nope'], 'mla_kn_rope': out['mla_kn_rope'], 'mix_w_out': out['mix_w_out'], 's5_lam_re': out['s5_lam_re'], 's5_lam_im': out['s5_lam_im'], 's5_log_dt': out['s5_log_dt'], 's5_b_re': out['s5_b_re'], 's5_b_im': out['s5_b_im'], 's5_c_re': out['s5_c_re'], 's5_c_im': out['s5_c_im'], 's5_d': out['s5_d'], 's5_w_glu_a': out['s5_w_glu_a'], 's5_w_glu_b': out['s5_w_glu_b'], 'loss_target': out['loss_target'], 'm_norm_mix': out['m_norm_mix'], 'm_norm_xa': out['m_norm_xa'], 'm_norm_mem': out['m_norm_mem'], 'm_norm_ffn': out['m_norm_ffn'], 'm_xa_wq': out['m_xa_wq'], 'm_xa_wk': out['m_xa_wk'], 'm_xa_wv': out['m_xa_wv'], 'm_xa_wo': out['m_xa_wo'], 'm_xa_q_norm': out['m_xa_q_norm'], 'm_xa_k_norm': out['m_xa_k_norm'], 'm_ffn_w_up': out['m_ffn_w_up'], 'm_ffn_conv_w': out['m_ffn_conv_w'], 'm_ffn_conv_b': out['m_ffn_conv_b'], 'm_ffn_w_down': out['m_ffn_w_down'], 'm_hg_lb_logits': out['m_hg_lb_logits'], 'm_mix_w_in': out['m_mix_w_in'], 'm_hg_out_norm': out['m_hg_out_norm'], 'm_mla_q_a_norm': out['m_mla_q_a_norm'], 'm_mla_w_uq': out['m_mla_w_uq'], 'm_mla_kv_a_norm': out['m_mla_kv_a_norm'], 'm_mla_w_ukv': out['m_mla_w_ukv'], 'm_mla_qn_nope': out['m_mla_qn_nope'], 'm_mla_qn_rope': out['m_mla_qn_rope'], 'm_mla_kn_nope': out['m_mla_kn_nope'], 'm_mla_kn_rope': out['m_mla_kn_rope'], 'm_mix_w_out': out['m_mix_w_out'], 'm_s5_lam_re': out['m_s5_lam_re'], 'm_s5_lam_im': out['m_s5_lam_im'], 'm_s5_log_dt': out['m_s5_log_dt'], 'm_s5_b_re': out['m_s5_b_re'], 'm_s5_b_im': out['m_s5_b_im'], 'm_s5_c_re': out['m_s5_c_re'], 'm_s5_c_im': out['m_s5_c_im'], 'm_s5_d': out['m_s5_d'], 'm_s5_w_glu_a': out['m_s5_w_glu_a'], 'm_s5_w_glu_b': out['m_s5_w_glu_b'], 'v_norm_mix': out['v_norm_mix'], 'v_norm_xa': out['v_norm_xa'], 'v_norm_mem': out['v_norm_mem'], 'v_norm_ffn': out['v_norm_ffn'], 'v_xa_wq': out['v_xa_wq'], 'v_xa_wk': out['v_xa_wk'], 'v_xa_wv': out['v_xa_wv'], 'v_xa_wo': out['v_xa_wo'], 'v_xa_q_norm': out['v_xa_q_norm'], 'v_xa_k_norm': out['v_xa_k_norm'], 'v_ffn_w_up': out['v_ffn_w_up'], 'v_ffn_conv_w': out['v_ffn_conv_w'], 'v_ffn_conv_b': out['v_ffn_conv_b'], 'v_ffn_w_down': out['v_ffn_w_down'], 'v_hg_lb_logits': out['v_hg_lb_logits'], 'v_mix_w_in': out['v_mix_w_in'], 'v_hg_out_norm': out['v_hg_out_norm'], 'v_mla_q_a_norm': out['v_mla_q_a_norm'], 'v_mla_w_uq': out['v_mla_w_uq'], 'v_mla_kv_a_norm': out['v_mla_kv_a_norm'], 'v_mla_w_ukv': out['v_mla_w_ukv'], 'v_mla_qn_nope': out['v_mla_qn_nope'], 'v_mla_qn_rope': out['v_mla_qn_rope'], 'v_mla_kn_nope': out['v_mla_kn_nope'], 'v_mla_kn_rope': out['v_mla_kn_rope'], 'v_mix_w_out': out['v_mix_w_out'], 'v_s5_lam_re': out['v_s5_lam_re'], 'v_s5_lam_im': out['v_s5_lam_im'], 'v_s5_log_dt': out['v_s5_log_dt'], 'v_s5_b_re': out['v_s5_b_re'], 'v_s5_b_im': out['v_s5_b_im'], 'v_s5_c_re': out['v_s5_c_re'], 'v_s5_c_im': out['v_s5_c_im'], 'v_s5_d': out['v_s5_d'], 'v_s5_w_glu_a': out['v_s5_w_glu_a'], 'v_s5_w_glu_b': out['v_s5_w_glu_b']}


def _loss(weights, diff, rest, loss_target):
    with _jax.named_scope("forward"):
        args = {**rest, TWIN_DIFF_INPUT: diff, **{k: w.astype(_WEIGHT_DTYPES[k]) for k, w in weights.items()}}
        y = _forward(args)
    with _jax.named_scope("loss_head"):
        err = _jnp.square(y.astype(_jnp.float32) - loss_target)
        return 0.5 * _jnp.sum(_jnp.mean(err, axis=-1)) if err.ndim else 0.5 * err


def _adamw(w, g, m, v):
    m = ADAM_B1 * m + (1.0 - ADAM_B1) * g
    v = ADAM_B2 * v + (1.0 - ADAM_B2) * _jnp.square(g)
    m_hat = m / (1.0 - ADAM_B1 ** ADAM_STEP)
    v_hat = v / (1.0 - ADAM_B2 ** ADAM_STEP)
    delta = -ADAM_LR * (m_hat / (_jnp.sqrt(v_hat) + ADAM_EPS) + ADAM_WD * w)
    return delta, m, v


def reference(x, mem, positions, norm_mix, norm_xa, norm_mem, norm_ffn, xa_wq, xa_wk, xa_wv, xa_wo, xa_q_norm, xa_k_norm, ffn_w_up, ffn_conv_w, ffn_conv_b, ffn_w_down, hg_lb_logits, mix_w_in, hg_out_norm, mla_q_a_norm, mla_w_uq, mla_kv_a_norm, mla_w_ukv, mla_qn_nope, mla_qn_rope, mla_kn_nope, mla_kn_rope, mix_w_out, s5_lam_re, s5_lam_im, s5_log_dt, s5_b_re, s5_b_im, s5_c_re, s5_c_im, s5_d, s5_w_glu_a, s5_w_glu_b, loss_target, m_norm_mix, m_norm_xa, m_norm_mem, m_norm_ffn, m_xa_wq, m_xa_wk, m_xa_wv, m_xa_wo, m_xa_q_norm, m_xa_k_norm, m_ffn_w_up, m_ffn_conv_w, m_ffn_conv_b, m_ffn_w_down, m_hg_lb_logits, m_mix_w_in, m_hg_out_norm, m_mla_q_a_norm, m_mla_w_uq, m_mla_kv_a_norm, m_mla_w_ukv, m_mla_qn_nope, m_mla_qn_rope, m_mla_kn_nope, m_mla_kn_rope, m_mix_w_out, m_s5_lam_re, m_s5_lam_im, m_s5_log_dt, m_s5_b_re, m_s5_b_im, m_s5_c_re, m_s5_c_im, m_s5_d, m_s5_w_glu_a, m_s5_w_glu_b, v_norm_mix, v_norm_xa, v_norm_mem, v_norm_ffn, v_xa_wq, v_xa_wk, v_xa_wv, v_xa_wo, v_xa_q_norm, v_xa_k_norm, v_ffn_w_up, v_ffn_conv_w, v_ffn_conv_b, v_ffn_w_down, v_hg_lb_logits, v_mix_w_in, v_hg_out_norm, v_mla_q_a_norm, v_mla_w_uq, v_mla_kv_a_norm, v_mla_w_ukv, v_mla_qn_nope, v_mla_qn_rope, v_mla_kn_nope, v_mla_kn_rope, v_mix_w_out, v_s5_lam_re, v_s5_lam_im, v_s5_log_dt, v_s5_b_re, v_s5_b_im, v_s5_c_re, v_s5_c_im, v_s5_d, v_s5_w_glu_a, v_s5_w_glu_b):
    given = dict(x=x, mem=mem, positions=positions, norm_mix=norm_mix, norm_xa=norm_xa, norm_mem=norm_mem, norm_ffn=norm_ffn, xa_wq=xa_wq, xa_wk=xa_wk, xa_wv=xa_wv, xa_wo=xa_wo, xa_q_norm=xa_q_norm, xa_k_norm=xa_k_norm, ffn_w_up=ffn_w_up, ffn_conv_w=ffn_conv_w, ffn_conv_b=ffn_conv_b, ffn_w_down=ffn_w_down, hg_lb_logits=hg_lb_logits, mix_w_in=mix_w_in, hg_out_norm=hg_out_norm, mla_q_a_norm=mla_q_a_norm, mla_w_uq=mla_w_uq, mla_kv_a_norm=mla_kv_a_norm, mla_w_ukv=mla_w_ukv, mla_qn_nope=mla_qn_nope, mla_qn_rope=mla_qn_rope, mla_kn_nope=mla_kn_nope, mla_kn_rope=mla_kn_rope, mix_w_out=mix_w_out, s5_lam_re=s5_lam_re, s5_lam_im=s5_lam_im, s5_log_dt=s5_log_dt, s5_b_re=s5_b_re, s5_b_im=s5_b_im, s5_c_re=s5_c_re, s5_c_im=s5_c_im, s5_d=s5_d, s5_w_glu_a=s5_w_glu_a, s5_w_glu_b=s5_w_glu_b, loss_target=loss_target, m_norm_mix=m_norm_mix, m_norm_xa=m_norm_xa, m_norm_mem=m_norm_mem, m_norm_ffn=m_norm_ffn, m_xa_wq=m_xa_wq, m_xa_wk=m_xa_wk, m_xa_wv=m_xa_wv, m_xa_wo=m_xa_wo, m_xa_q_norm=m_xa_q_norm, m_xa_k_norm=m_xa_k_norm, m_ffn_w_up=m_ffn_w_up, m_ffn_conv_w=m_ffn_conv_w, m_ffn_conv_b=m_ffn_conv_b, m_ffn_w_down=m_ffn_w_down, m_hg_lb_logits=m_hg_lb_logits, m_mix_w_in=m_mix_w_in, m_hg_out_norm=m_hg_out_norm, m_mla_q_a_norm=m_mla_q_a_norm, m_mla_w_uq=m_mla_w_uq, m_mla_kv_a_norm=m_mla_kv_a_norm, m_mla_w_ukv=m_mla_w_ukv, m_mla_qn_nope=m_mla_qn_nope, m_mla_qn_rope=m_mla_qn_rope, m_mla_kn_nope=m_mla_kn_nope, m_mla_kn_rope=m_mla_kn_rope, m_mix_w_out=m_mix_w_out, m_s5_lam_re=m_s5_lam_re, m_s5_lam_im=m_s5_lam_im, m_s5_log_dt=m_s5_log_dt, m_s5_b_re=m_s5_b_re, m_s5_b_im=m_s5_b_im, m_s5_c_re=m_s5_c_re, m_s5_c_im=m_s5_c_im, m_s5_d=m_s5_d, m_s5_w_glu_a=m_s5_w_glu_a, m_s5_w_glu_b=m_s5_w_glu_b, v_norm_mix=v_norm_mix, v_norm_xa=v_norm_xa, v_norm_mem=v_norm_mem, v_norm_ffn=v_norm_ffn, v_xa_wq=v_xa_wq, v_xa_wk=v_xa_wk, v_xa_wv=v_xa_wv, v_xa_wo=v_xa_wo, v_xa_q_norm=v_xa_q_norm, v_xa_k_norm=v_xa_k_norm, v_ffn_w_up=v_ffn_w_up, v_ffn_conv_w=v_ffn_conv_w, v_ffn_conv_b=v_ffn_conv_b, v_ffn_w_down=v_ffn_w_down, v_hg_lb_logits=v_hg_lb_logits, v_mix_w_in=v_mix_w_in, v_hg_out_norm=v_hg_out_norm, v_mla_q_a_norm=v_mla_q_a_norm, v_mla_w_uq=v_mla_w_uq, v_mla_kv_a_norm=v_mla_kv_a_norm, v_mla_w_ukv=v_mla_w_ukv, v_mla_qn_nope=v_mla_qn_nope, v_mla_qn_rope=v_mla_qn_rope, v_mla_kn_nope=v_mla_kn_nope, v_mla_kn_rope=v_mla_kn_rope, v_mix_w_out=v_mix_w_out, v_s5_lam_re=v_s5_lam_re, v_s5_lam_im=v_s5_lam_im, v_s5_log_dt=v_s5_log_dt, v_s5_b_re=v_s5_b_re, v_s5_b_im=v_s5_b_im, v_s5_c_re=v_s5_c_re, v_s5_c_im=v_s5_c_im, v_s5_d=v_s5_d, v_s5_w_glu_a=v_s5_w_glu_a, v_s5_w_glu_b=v_s5_w_glu_b)
    weights = {n: given[n] for n in TWIN_WEIGHTS}
    shared = {n: given[n] for n in SHARED_INPUTS}
    per_example = {n: given[n] for n in ['x', 'mem', 'positions']}
    grad_fn = _jax.value_and_grad(_loss, argnums=(0, 1))

    def one_microbatch(ex, loss_target):
        ex = dict(ex)
        diff = ex.pop(TWIN_DIFF_INPUT)
        return grad_fn(weights, diff, {**shared, **ex}, loss_target)

    if N_MICROBATCH == 1:
        loss, (grad_w, grad_x) = one_microbatch(per_example, given["loss_target"])
    else:
        def body(carry, xs):
            loss_sum, grad_sum = carry
            l_k, (gw_k, gx_k) = one_microbatch(xs[0], xs[1])
            with _jax.named_scope("update"):
                return (loss_sum + l_k, _jax.tree.map(_jnp.add, grad_sum, gw_k)), gx_k

        init = (_jnp.zeros((), _jnp.float32), _jax.tree.map(_jnp.zeros_like, weights))
        (loss, grad_w), grad_x = _jax.lax.scan(body, init, (per_example, given["loss_target"]))
    with _jax.named_scope("update"):
        delta_w, new_m, new_v = {}, {}, {}
        for n in TWIN_WEIGHTS:
            delta_w[n], new_m[n], new_v[n] = _adamw(weights[n], grad_w[n], given["m_" + n], given["v_" + n])
    return (loss, grad_x, *[grad_w[n] for n in TWIN_WEIGHTS], *[delta_w[n] for n in TWIN_WEIGHTS],
            *[new_m[n] for n in TWIN_WEIGHTS], *[new_v[n] for n in TWIN_WEIGHTS])
```

```python
import functools

import jax
import jax.numpy as jnp
import numpy as np
from jax import lax
from jax.experimental import pallas as pl
from jax.experimental.pallas import tpu as pltpu

f32 = jnp.float32
bf16 = jnp.bfloat16

EPS = 1e-6
N_DEV = 8
VMEM_LIMIT = 52 * 1024 * 1024

HG_HEADS = 4
HG_DIM = 128
HG_WIDTH = HG_HEADS * HG_DIM
HG_CHUNK = 64
HG_SUB = 16
MLA_HEADS = 4
MLA_Q_RANK = 256
MLA_KV_RANK = 128
MLA_NOPE = 128
MLA_ROPE = 64
MLA_V = 128
MLA_QK = MLA_NOPE + MLA_ROPE
MLA_QK_PAD = 256
ROPE_BASE = 10000.0
IN_WIDTH = 4 * HG_WIDTH + MLA_Q_RANK + MLA_KV_RANK + MLA_ROPE
IN_PAD = 2560
XA_HEADS = 4
XA_DIM = 256
S5_GROUP = 16
S5_GROUPS = 64
S5_STATE = 64
CONV_W = 3

ADAM_LR = 0.001
ADAM_B1 = 0.9
ADAM_B2 = 0.999
ADAM_EPS = 1e-08
ADAM_WD = 0.01
ADAM_STEP = 10

_NT = (((1,), (1,)), ((), ()))
_TN = (((0,), (0,)), ((), ()))
_NN = (((1,), (0,)), ((), ()))


def _pick(n, cands):
    for c in cands:
        if n % c == 0:
            return c
    return n


def _cparams(sem):
    return pltpu.CompilerParams(dimension_semantics=sem, vmem_limit_bytes=VMEM_LIMIT)


def _mm(a, b, *, name, ta=False, tb=False, out_dtype=f32, add=None):
    m, k = (a.shape[1], a.shape[0]) if ta else a.shape
    n = b.shape[0] if tb else b.shape[1]
    assert (b.shape[1] if tb else b.shape[0]) == k, (a.shape, b.shape, ta, tb)
    bm = _pick(m, (512, 256, 128))
    bn = _pick(n, (512, 256, 128))
    bk = _pick(k, (1024, 512, 256, 128))
    nk = k // bk
    dims = (((0 if ta else 1,), (1 if tb else 0,)), ((), ()))

    def body(*refs):
        if add is None:
            a_ref, b_ref, o_ref, acc_ref = refs
        else:
            a_ref, b_ref, add_ref, o_ref, acc_ref = refs
        kk = pl.program_id(2)

        @pl.when(kk == 0)
        def _():
            acc_ref[...] = jnp.zeros_like(acc_ref)

        acc_ref[...] += lax.dot_general(a_ref[...].astype(bf16), b_ref[...].astype(bf16), dims,
                                        preferred_element_type=f32)

        @pl.when(kk == nk - 1)
        def _():
            r = acc_ref[...]
            if add is not None:
                r = r + add_ref[...].astype(f32)
            o_ref[...] = r.astype(o_ref.dtype)

    a_spec = pl.BlockSpec((bk, bm), lambda i, j, q: (q, i)) if ta else pl.BlockSpec((bm, bk), lambda i, j, q: (i, q))
    b_spec = pl.BlockSpec((bn, bk), lambda i, j, q: (j, q)) if tb else pl.BlockSpec((bk, bn), lambda i, j, q: (q, j))
    in_specs = [a_spec, b_spec]
    args = [a, b]
    if add is not None:
        in_specs.append(pl.BlockSpec((bm, bn), lambda i, j, q: (i, j)))
        args.append(add)
    return pl.pallas_call(
        body, grid=(m // bm, n // bn, nk), in_specs=in_specs,
        out_specs=pl.BlockSpec((bm, bn), lambda i, j, q: (i, j)),
        out_shape=jax.ShapeDtypeStruct((m, n), out_dtype),
        scratch_shapes=[pltpu.VMEM((bm, bn), f32)],
        compiler_params=_cparams(("parallel", "parallel", "arbitrary")), name=name)(*args)


def _as_tuple(x):
    return tuple(x) if isinstance(x, (tuple, list)) else (x,)


def _full_spec(p):
    nd = p.ndim
    return pl.BlockSpec(p.shape, lambda i, _nd=nd: (0,) * _nd)


def _rows(fn, rows, params, outs, *, name, tile=256, accs=()):
    length = rows[0].shape[0]
    tile = min(tile, length)
    nr, npar, no = len(rows), len(params), len(outs)

    def body(*refs):
        r, p, o = refs[:nr], refs[nr:nr + npar], refs[nr + npar:]
        res = _as_tuple(fn(*[x[...].astype(f32) for x in r], *[x[...] for x in p]))
        for kk in range(no):
            o[kk][...] = res[kk].astype(o[kk].dtype)
        if accs:
            @pl.when(pl.program_id(0) == 0)
            def _():
                for kk in range(no, no + len(accs)):
                    o[kk][...] = jnp.zeros_like(o[kk])
            for kk in range(no, no + len(accs)):
                o[kk][...] += res[kk]

    in_specs = [pl.BlockSpec((tile, x.shape[1]), lambda i: (i, 0)) for x in rows] + [_full_spec(p) for p in params]
    out_specs = [pl.BlockSpec((tile, w), lambda i: (i, 0)) for w, _ in outs]
    out_shape = [jax.ShapeDtypeStruct((length, w), d) for w, d in outs]
    for s in accs:
        out_specs.append(pl.BlockSpec(s, lambda i, _nd=len(s): (0,) * _nd))
        out_shape.append(jax.ShapeDtypeStruct(s, f32))
    res = pl.pallas_call(body, grid=(length // tile,), in_specs=in_specs, out_specs=out_specs, out_shape=out_shape,
                         compiler_params=_cparams(("arbitrary",)), name=name)(*rows, *params)
    return res


def _rows_bwd(fn, rows, params, cts, *, name, rgrad, pgrad, tile=256, addends=None):
    addends = addends or {}
    length = rows[0].shape[0]
    tile = min(tile, length)
    nr, npar, nc = len(rows), len(params), len(cts)
    ridx = [i for i in range(nr) if rgrad[i] is not None]
    pidx = [i for i in range(npar) if pgrad[i]]
    aidx = sorted(addends)
    na = len(aidx)

    def body(*refs):
        r, p, c = refs[:nr], refs[nr:nr + npar], refs[nr + npar:nr + npar + nc]
        ad = refs[nr + npar + nc:nr + npar + nc + na]
        o = refs[nr + npar + nc + na:]
        rv = [x[...].astype(f32) for x in r]
        pv = [x[...] for x in p]
        cv = tuple(x[...].astype(f32) for x in c)

        def g(*d):
            rr, pp = list(rv), list(pv)
            for n_, i_ in enumerate(ridx):
                rr[i_] = d[n_]
            for n_, i_ in enumerate(pidx):
                pp[i_] = d[len(ridx) + n_]
            return _as_tuple(fn(*rr, *pp))

        _, vjp = jax.vjp(g, *[rv[i] for i in ridx], *[pv[i] for i in pidx])
        grads = vjp(cv)
        for n_, i_ in enumerate(ridx):
            val = grads[n_]
            if i_ in addends:
                val = val + ad[aidx.index(i_)][...].astype(f32)
            o[n_][...] = val.astype(o[n_].dtype)
        if pidx:
            @pl.when(pl.program_id(0) == 0)
            def _():
                for n_ in range(len(pidx)):
                    o[len(ridx) + n_][...] = jnp.zeros_like(o[len(ridx) + n_])
            for n_ in range(len(pidx)):
                o[len(ridx) + n_][...] += grads[len(ridx) + n_]

    row_spec = lambda x: pl.BlockSpec((tile, x.shape[1]), lambda i: (i, 0))
    in_specs = ([row_spec(x) for x in rows] + [_full_spec(p) for p in params] + [row_spec(x) for x in cts]
                + [row_spec(addends[i]) for i in aidx])
    out_specs = [row_spec(rows[i]) for i in ridx] + [_full_spec(params[i]) for i in pidx]
    out_shape = ([jax.ShapeDtypeStruct(rows[i].shape, rgrad[i]) for i in ridx]
                 + [jax.ShapeDtypeStruct(params[i].shape, f32) for i in pidx])
    res = pl.pallas_call(body, grid=(length // tile,), in_specs=in_specs, out_specs=out_specs, out_shape=out_shape,
                         compiler_params=_cparams(("arbitrary",)), name=name)(
        *rows, *params, *cts, *[addends[i] for i in aidx])
    return list(res[:len(ridx)]), list(res[len(ridx):])


def _rms(x, g):
    return x * lax.rsqrt(jnp.mean(x * x, axis=-1, keepdims=True) + EPS) * g


def _rms_twice(x, g):
    y = _rms(x, g)
    return y, y


def _silu(x):
    return x * jax.nn.sigmoid(x)


def _shift_down(x, s):
    rows = lax.broadcasted_iota(jnp.int32, x.shape, 0)
    return jnp.where(rows >= s, pltpu.roll(x, s, axis=0), 0.0)


def _shift_up(x, s):
    n = x.shape[0]
    rows = lax.broadcasted_iota(jnp.int32, x.shape, 0)
    return jnp.where(rows < n - s, pltpu.roll(x, n - s, axis=0), 0.0)


_CONV_COLS = 128


def _conv_gate_fwd(u, cw, cb, *, name):
    length, two_f = u.shape
    ff = two_f // 2
    nb = ff // _CONV_COLS

    def body(ug, uv, wg, wv, bg, bv, o):
        def conv(x_ref, w_ref, b_ref):
            x = x_ref[...]
            return (w_ref[2:3, :] * x + w_ref[1:2, :] * _shift_down(x, 1) + w_ref[0:1, :] * _shift_down(x, 2)
                    + b_ref[...])
        o[...] = (_silu(conv(ug, wg, bg)) * conv(uv, wv, bv)).astype(o.dtype)

    blk = lambda r, off: pl.BlockSpec((r, _CONV_COLS), lambda j, _o=off: (0, j + _o))
    return pl.pallas_call(
        body, grid=(nb,),
        in_specs=[blk(length, 0), blk(length, nb), blk(CONV_W, 0), blk(CONV_W, nb), blk(1, 0), blk(1, nb)],
        out_specs=blk(length, 0), out_shape=jax.ShapeDtypeStruct((length, ff), bf16),
        compiler_params=_cparams(("parallel",)), name=name)(u, u, cw, cw, cb, cb)


def _conv_gate_bwd(u, cw, cb, da, *, name):
    length, two_f = u.shape
    ff = two_f // 2
    nb = ff // _CONV_COLS

    def body(ug, uv, wg, wv, bg, bv, da_ref, dug, duv, dwg, dwv, dbg, dbv):
        def conv(x, w_ref, b_ref):
            x1, x2 = _shift_down(x, 1), _shift_down(x, 2)
            return w_ref[2:3, :] * x + w_ref[1:2, :] * x1 + w_ref[0:1, :] * x2 + b_ref[...], x1, x2

        xg, xv = ug[...], uv[...]
        g, xg1, xg2 = conv(xg, wg, bg)
        v, xv1, xv2 = conv(xv, wv, bv)
        d = da_ref[...].astype(f32)
        sg = jax.nn.sigmoid(g)
        dg = d * v * (sg * (1.0 + g * (1.0 - sg)))
        dv = d * (g * sg)

        def back(dy, x, x1, x2, w_ref, du_ref, dw_ref, db_ref):
            du_ref[...] = (w_ref[2:3, :] * dy + w_ref[1:2, :] * _shift_up(dy, 1)
                           + w_ref[0:1, :] * _shift_up(dy, 2)).astype(du_ref.dtype)
            dw_ref[2:3, :] = jnp.sum(dy * x, axis=0, keepdims=True)
            dw_ref[1:2, :] = jnp.sum(dy * x1, axis=0, keepdims=True)
            dw_ref[0:1, :] = jnp.sum(dy * x2, axis=0, keepdims=True)
            db_ref[...] = jnp.sum(dy, axis=0, keepdims=True)

        back(dg, xg, xg1, xg2, wg, dug, dwg, dbg)
        back(dv, xv, xv1, xv2, wv, duv, dwv, dbv)

    blk = lambda r, off: pl.BlockSpec((r, _CONV_COLS), lambda j, _o=off: (0, j + _o))
    sds = jax.ShapeDtypeStruct
    dug, duv, dwg, dwv, dbg, dbv = pl.pallas_call(
        body, grid=(nb,),
        in_specs=[blk(length, 0), blk(length, nb), blk(CONV_W, 0), blk(CONV_W, nb), blk(1, 0), blk(1, nb),
                  blk(length, 0)],
        out_specs=[blk(length, 0), blk(length, 0), blk(CONV_W, 0), blk(CONV_W, 0), blk(1, 0), blk(1, 0)],
        out_shape=[sds((length, ff), bf16), sds((length, ff), bf16), sds((CONV_W, ff), f32), sds((CONV_W, ff), f32),
                   sds((1, ff), f32), sds((1, ff), f32)],
        compiler_params=_cparams(("parallel",)), name=name)(u, u, cw, cw, cb, cb, da)
    return dug, duv, jnp.concatenate([dwg, dwv], axis=1), jnp.concatenate([dbg, dbv], axis=1)


def _ffn_fwd(h, w, tag):
    hf, = _rows(_rms, [h], [w["norm_ffn"]], [(h.shape[1], bf16)], name=f"ffn_norm_{tag}")
    u = _mm(hf, w["ffn_w_up"], name=f"ffn_up_{tag}")
    a = _conv_gate_fwd(u, w["ffn_conv_w"], w["ffn_conv_b"], name=f"ffn_conv_{tag}")
    out = _mm(a, w["ffn_w_down"], add=h, name=f"ffn_down_{tag}")
    return out, (h, hf, u, a)


def _ffn_bwd(dout, w, saved, tag):
    h, hf, u, a = saved
    ff = a.shape[1]
    da = _mm(dout, w["ffn_w_down"], tb=True, out_dtype=bf16, name=f"ffn_da_{tag}")
    g = {"ffn_w_down": _mm(a, dout, ta=True, name=f"ffn_dwdown_{tag}")}
    dug, duv, g["ffn_conv_w"], g["ffn_conv_b"] = _conv_gate_bwd(u, w["ffn_conv_w"], w["ffn_conv_b"], da,
                                                                name=f"ffn_dconv_{tag}")
    dhf = _mm(dug, w["ffn_w_up"][:, :ff], tb=True, name=f"ffn_dhf_g_{tag}")
    dhf = _mm(duv, w["ffn_w_up"][:, ff:], tb=True, add=dhf, out_dtype=bf16, name=f"ffn_dhf_v_{tag}")
    g["ffn_w_up"] = jnp.concatenate([_mm(hf, dug, ta=True, name=f"ffn_dwup_g_{tag}"),
                                     _mm(hf, duv, ta=True, name=f"ffn_dwup_v_{tag}")], axis=1)
    (dh,), (g["norm_ffn"],) = _rows_bwd(_rms, [h], [w["norm_ffn"]], [dhf], rgrad=[f32], pgrad=[True],
                                        addends={0: dout}, name=f"ffn_dnorm_{tag}")
    return dh, g


def _xattn_fn(qx, kx, vx, qg, kg):
    outs = []
    for hh in range(XA_HEADS):
        sl = slice(hh * XA_DIM, (hh + 1) * XA_DIM)
        q = _rms(qx[:, sl], qg).astype(bf16)
        k = _rms(kx[:, sl], kg).astype(bf16)
        s = lax.dot_general(q, k, _NT, preferred_element_type=f32) * (XA_DIM ** -0.5)
        s = s - jnp.max(s, axis=-1, keepdims=True)
        p = jnp.exp(s)
        p = p / jnp.sum(p, axis=-1, keepdims=True)
        outs.append(jnp.dot(p.astype(bf16), vx[:, sl].astype(bf16), preferred_element_type=f32))
    return jnp.concatenate(outs, axis=-1)


def _xattn_fwd(h, mem, w, tag):
    d = h.shape[1]
    hx, = _rows(_rms, [h], [w["norm_xa"]], [(d, bf16)], name=f"xa_norm_{tag}")
    qx = _mm(hx, w["xa_wq"], name=f"xa_q_{tag}")
    m, = _rows(_rms, [mem], [w["norm_mem"]], [(d, bf16)], name=f"xa_mnorm_{tag}")
    kx = _mm(m, w["xa_wk"], name=f"xa_k_{tag}")
    vx = _mm(m, w["xa_wv"], name=f"xa_v_{tag}")
    o, = _rows(_xattn_fn, [qx], [kx, vx, w["xa_q_norm"], w["xa_k_norm"]], [(d, bf16)], name=f"xa_attn_{tag}")
    out = _mm(o, w["xa_wo"], add=h, name=f"xa_o_{tag}")
    return out, (h, hx, qx, m, kx, vx, o)


def _xattn_bwd(dout, mem, w, saved, tag):
    h, hx, qx, m, kx, vx, o = saved
    g = {}
    do = _mm(dout, w["xa_wo"], tb=True, out_dtype=bf16, name=f"xa_do_{tag}")
    g["xa_wo"] = _mm(o, dout, ta=True, name=f"xa_dwo_{tag}")
    (dqx,), (dkx, dvx, g["xa_q_norm"], g["xa_k_norm"]) = _rows_bwd(
        _xattn_fn, [qx], [kx, vx, w["xa_q_norm"], w["xa_k_norm"]], [do], rgrad=[bf16], pgrad=[True] * 4,
        name=f"xa_dattn_{tag}")
    dhx = _mm(dqx, w["xa_wq"], tb=True, out_dtype=bf16, name=f"xa_dhx_{tag}")
    g["xa_wq"] = _mm(hx, dqx, ta=True, name=f"xa_dwq_{tag}")
    (dh,), (g["norm_xa"],) = _rows_bwd(_rms, [h], [w["norm_xa"]], [dhx], rgrad=[f32], pgrad=[True],
                                       addends={0: dout}, name=f"xa_dnorm_{tag}")
    dm = _mm(dkx, w["xa_wk"], tb=True, name=f"xa_dm_k_{tag}")
    dm = _mm(dvx, w["xa_wv"], tb=True, add=dm, name=f"xa_dm_v_{tag}")
    g["xa_wk"] = _mm(m, dkx, ta=True, name=f"xa_dwk_{tag}")
    g["xa_wv"] = _mm(m, dvx, ta=True, name=f"xa_dwv_{tag}")
    _, (g["norm_mem"],) = _rows_bwd(_rms, [mem], [w["norm_mem"]], [dm], rgrad=[None], pgrad=[True],
                                    name=f"xa_dmnorm_{tag}")
    return dh, g


_HG_GROUP = 4


def _hg_chunk(q, k, v, g, st):
    c = q.shape[0]
    tri = (lax.broadcasted_iota(jnp.int32, (c, c), 0) >= lax.broadcasted_iota(jnp.int32, (c, c), 1)).astype(f32)
    b = jnp.dot(tri, g, precision=lax.Precision.HIGHEST, preferred_element_type=f32)
    bend = jnp.sum(g, axis=0, keepdims=True)
    o_inter = lax.dot_general((q * jnp.exp(b)).astype(bf16), st.astype(bf16), _NT, preferred_element_type=f32)
    kd = k * jnp.exp(bend - b)
    st_new = st * jnp.exp(bend) + lax.dot_general(v.astype(bf16), kd.astype(bf16), _TN, preferred_element_type=f32)
    outs = []
    for i in range(c // HG_SUB):
        lo, n = HG_SUB * i, HG_SUB * (i + 1)
        ref = jnp.sum(g[:lo], axis=0, keepdims=True) if i else jnp.zeros((1, g.shape[1]), f32)
        qh = q[lo:n] * jnp.exp(b[lo:n] - ref)
        kh = k[:n] * jnp.exp(ref - b[:n])
        a = lax.dot_general(qh.astype(bf16), kh.astype(bf16), _NT, preferred_element_type=f32)
        keep = (lax.broadcasted_iota(jnp.int32, (HG_SUB, n), 1)
                <= lo + lax.broadcasted_iota(jnp.int32, (HG_SUB, n), 0))
        a = jnp.where(keep, a, 0.0)
        outs.append(jnp.dot(a.astype(bf16), v[:n].astype(bf16), preferred_element_type=f32))
    return jnp.concatenate(outs, axis=0) + o_inter, st_new


def _hg_fwd(q, k, v, g, *, name):
    length = q.shape[0]
    rows = _HG_GROUP * HG_CHUNK
    ng = length // rows
    nc = length // HG_CHUNK

    def body(q_ref, k_ref, v_ref, g_ref, o_ref, st_ref, state):
        @pl.when(pl.program_id(1) == 0)
        def _():
            state[...] = jnp.zeros_like(state)

        for ci in range(_HG_GROUP):
            sl = slice(ci * HG_CHUNK, (ci + 1) * HG_CHUNK)
            st = state[...]
            st_ref[0, ci] = st
            o, st_new = _hg_chunk(q_ref[sl, :], k_ref[sl, :], v_ref[sl, :], g_ref[sl, :], st)
            o_ref[sl, :] = o
            state[...] = st_new

    blk = pl.BlockSpec((rows, HG_DIM), lambda h, c: (c, h))
    return pl.pallas_call(
        body, grid=(HG_HEADS, ng), in_specs=[blk] * 4,
        out_specs=[blk, pl.BlockSpec((1, _HG_GROUP, HG_DIM, HG_DIM), lambda h, c: (h, c, 0, 0))],
        out_shape=[jax.ShapeDtypeStruct((length, HG_WIDTH), f32),
                   jax.ShapeDtypeStruct((HG_HEADS, nc, HG_DIM, HG_DIM), f32)],
        scratch_shapes=[pltpu.VMEM((HG_DIM, HG_DIM), f32)],
        compiler_params=_cparams(("parallel", "arbitrary")), name=name)(q, k, v, g)


def _hg_bwd(q, k, v, g, states, do, *, name):
    length = q.shape[0]
    rows = _HG_GROUP * HG_CHUNK
    ng = length // rows

    def body(q_ref, k_ref, v_ref, g_ref, st_ref, do_ref, dq_ref, dk_ref, dv_ref, dg_ref, dstate):
        @pl.when(pl.program_id(1) == 0)
        def _():
            dstate[...] = jnp.zeros_like(dstate)

        for ci in reversed(range(_HG_GROUP)):
            sl = slice(ci * HG_CHUNK, (ci + 1) * HG_CHUNK)
            _, vjp = jax.vjp(_hg_chunk, q_ref[sl, :], k_ref[sl, :], v_ref[sl, :], g_ref[sl, :], st_ref[0, ci])
            dq, dk, dv, dg, dst = vjp((do_ref[sl, :], dstate[...]))
            dq_ref[sl, :] = dq
            dk_ref[sl, :] = dk
            dv_ref[sl, :] = dv
            dg_ref[sl, :] = dg
            dstate[...] = dst

    blk = pl.BlockSpec((rows, HG_DIM), lambda h, c: (ng - 1 - c, h))
    sds = jax.ShapeDtypeStruct((length, HG_WIDTH), f32)
    return pl.pallas_call(
        body, grid=(HG_HEADS, ng),
        in_specs=[blk] * 4 + [pl.BlockSpec((1, _HG_GROUP, HG_DIM, HG_DIM), lambda h, c: (h, ng - 1 - c, 0, 0)), blk],
        out_specs=[blk] * 4, out_shape=[sds] * 4,
        scratch_shapes=[pltpu.VMEM((HG_DIM, HG_DIM), f32)],
        compiler_params=_cparams(("parallel", "arbitrary")), name=name)(q, k, v, g, states, do)


_ATT_BLK = 256
_ATT_SCALE = MLA_QK ** -0.5
_NEG = -1e30


def _att_mask(i, j, t):
    rows = i * t + lax.broadcasted_iota(jnp.int32, (t, t), 0)
    cols = j * t + lax.broadcasted_iota(jnp.int32, (t, t), 1)
    return cols <= rows


def _att_fwd(q, k, v, *, name):
    length = q.shape[0]
    t = min(_ATT_BLK, length)
    nq = length // t

    def body(q_ref, k_ref, v_ref, o_ref, lse_ref):
        i = pl.program_id(1)
        qb = q_ref[...]

        def step(j, carry):
            m, l, acc = carry
            off = pl.multiple_of(j * t, t)
            ks = k_ref[pl.ds(off, t), :]
            vs = v_ref[pl.ds(off, t), :]
            s = lax.dot_general(qb, ks, _NT, preferred_element_type=f32) * _ATT_SCALE
            s = jnp.where(_att_mask(i, j, t), s, _NEG)
            m_new = jnp.maximum(m, jnp.max(s, axis=-1, keepdims=True))
            alpha = jnp.exp(m - m_new)
            p = jnp.exp(s - m_new)
            l = alpha * l + jnp.sum(p, axis=-1, keepdims=True)
            acc = alpha * acc + jnp.dot(p.astype(bf16), vs, preferred_element_type=f32)
            return m_new, l, acc

        init = (jnp.full((t, 1), _NEG, f32), jnp.zeros((t, 1), f32), jnp.zeros((t, MLA_V), f32))
        m, l, acc = lax.fori_loop(0, i + 1, step, init)
        o_ref[...] = (acc / l).astype(o_ref.dtype)
        lse_ref[...] = jnp.broadcast_to(m + jnp.log(l), lse_ref.shape)

    return pl.pallas_call(
        body, grid=(MLA_HEADS, nq),
        in_specs=[pl.BlockSpec((t, MLA_QK_PAD), lambda h, i: (i, h)),
                  pl.BlockSpec((length, MLA_QK_PAD), lambda h, i: (0, h)),
                  pl.BlockSpec((length, MLA_V), lambda h, i: (0, h))],
        out_specs=[pl.BlockSpec((t, MLA_V), lambda h, i: (i, h))] * 2,
        out_shape=[jax.ShapeDtypeStruct((length, MLA_HEADS * MLA_V), bf16),
                   jax.ShapeDtypeStruct((length, MLA_HEADS * MLA_V), f32)],
        compiler_params=_cparams(("parallel", "arbitrary")), name=name)(q, k, v)


def _att_bwd(q, k, v, o, lse, do, *, name):
    length = q.shape[0]
    t = min(_ATT_BLK, length)
    nq = length // t

    def dq_body(q_ref, k_ref, v_ref, o_ref, lse_ref, do_ref, dq_ref, delta_ref):
        i = pl.program_id(1)
        qb = q_ref[...]
        dob = do_ref[...]
        lse_b = lse_ref[:, 0:1]
        delta = jnp.sum(dob.astype(f32) * o_ref[...].astype(f32), axis=-1, keepdims=True)

        def step(j, dq):
            off = pl.multiple_of(j * t, t)
            ks = k_ref[pl.ds(off, t), :]
            vs = v_ref[pl.ds(off, t), :]
            s = lax.dot_general(qb, ks, _NT, preferred_element_type=f32) * _ATT_SCALE
            p = jnp.where(_att_mask(i, j, t), jnp.exp(s - lse_b), 0.0)
            dp = lax.dot_general(dob, vs, _NT, preferred_element_type=f32)
            ds = p * (dp - delta) * _ATT_SCALE
            return dq + jnp.dot(ds.astype(bf16), ks, preferred_element_type=f32)

        dq = lax.fori_loop(0, i + 1, step, jnp.zeros((t, MLA_QK_PAD), f32))
        dq_ref[...] = dq.astype(dq_ref.dtype)
        delta_ref[...] = jnp.broadcast_to(delta, delta_ref.shape)

    qblk = pl.BlockSpec((t, MLA_QK_PAD), lambda h, i: (i, h))
    vblk = pl.BlockSpec((t, MLA_V), lambda h, i: (i, h))
    qfull = pl.BlockSpec((length, MLA_QK_PAD), lambda h, i: (0, h))
    vfull = pl.BlockSpec((length, MLA_V), lambda h, i: (0, h))
    dq, delta = pl.pallas_call(
        dq_body, grid=(MLA_HEADS, nq), in_specs=[qblk, qfull, vfull, vblk, vblk, vblk],
        out_specs=[qblk, vblk],
        out_shape=[jax.ShapeDtypeStruct(q.shape, bf16), jax.ShapeDtypeStruct(lse.shape, f32)],
        compiler_params=_cparams(("parallel", "arbitrary")), name=name + "_dq")(q, k, v, o, lse, do)

    def dkv_body(k_ref, v_ref, q_ref, do_ref, lse_ref, delta_ref, dk_ref, dv_ref):
        j = pl.program_id(1)
        kb = k_ref[...]
        vb = v_ref[...]

        def step(i, carry):
            dk, dv = carry
            off = pl.multiple_of(i * t, t)
            qs = q_ref[pl.ds(off, t), :]
            dos = do_ref[pl.ds(off, t), :]
            lse_i = lse_ref[pl.ds(off, t), 0:1]
            delta_i = delta_ref[pl.ds(off, t), 0:1]
            s = lax.dot_general(qs, kb, _NT, preferred_element_type=f32) * _ATT_SCALE
            p = jnp.where(_att_mask(i, j, t), jnp.exp(s - lse_i), 0.0)
            dv = dv + lax.dot_general(p.astype(bf16), dos, _TN, preferred_element_type=f32)
            dp = lax.dot_general(dos, vb, _NT, preferred_element_type=f32)
            ds = p * (dp - delta_i) * _ATT_SCALE
            dk = dk + lax.dot_general(ds.astype(bf16), qs, _TN, preferred_element_type=f32)
            return dk, dv

        dk, dv = lax.fori_loop(j, nq, step, (jnp.zeros((t, MLA_QK_PAD), f32), jnp.zeros((t, MLA_V), f32)))
        dk_ref[...] = dk.astype(dk_ref.dtype)
        dv_ref[...] = dv.astype(dv_ref.dtype)

    dk, dv = pl.pallas_call(
        dkv_body, grid=(MLA_HEADS, nq), in_specs=[qblk, vblk, qfull, vfull, vfull, vfull],
        out_specs=[qblk, vblk],
        out_shape=[jax.ShapeDtypeStruct(k.shape, bf16), jax.ShapeDtypeStruct(v.shape, bf16)],
        compiler_params=_cparams(("parallel", "arbitrary")), name=name + "_dkv")(k, v, q, do, lse, delta)
    return dq, dk, dv


_C_Q = 4 * HG_WIDTH
_C_KV = _C_Q + MLA_Q_RANK
_C_KPE = _C_KV + MLA_KV_RANK


def _rms_n(x, g, n):
    return x * lax.rsqrt(jnp.sum(x * x, axis=-1, keepdims=True) * (1.0 / n) + EPS) * g


def _mix_a(proj, l0, l1, q_a_norm, kv_a_norm):
    lb = jax.nn.sigmoid(l0 - l1)
    f = lb + (1.0 - lb) * jax.nn.sigmoid(proj[:, HG_WIDTH:2 * HG_WIDTH])
    qf = _silu(proj[:, :HG_WIDTH])
    v = proj[:, 2 * HG_WIDTH:3 * HG_WIDTH]
    cqn = _rms(proj[:, _C_Q:_C_KV], q_a_norm)
    ckvn = _rms(proj[:, _C_KV:_C_KPE], kv_a_norm)
    return qf, 1.0 - f, v, jnp.log(f), cqn, ckvn


def _mix_b(qraw, kvraw, proj, cos, sin, qn_nope, qn_rope, kn_nope, kn_rope, perm):
    def rope(x):
        return x * cos + jnp.dot(x, perm, precision=lax.Precision.HIGHEST, preferred_element_type=f32) * sin

    kpe = rope(_rms_n(proj[:, _C_KPE:], kn_rope, MLA_ROPE))
    qs, ks, vs = [], [], []
    for hh in range(MLA_HEADS):
        base = hh * MLA_QK_PAD
        qs.append(_rms(qraw[:, base:base + MLA_NOPE], qn_nope))
        qs.append(rope(_rms_n(qraw[:, base + MLA_NOPE:base + MLA_QK_PAD], qn_rope, MLA_ROPE)))
        ks.append(_rms(kvraw[:, base:base + MLA_NOPE], kn_nope))
        ks.append(kpe)
        vs.append(kvraw[:, base + MLA_NOPE:base + MLA_QK_PAD])
    return jnp.concatenate(qs, axis=-1), jnp.concatenate(ks, axis=-1), jnp.concatenate(vs, axis=-1)


def _mix_c(o_hg, proj, o_mla, hg_out_norm):
    parts = []
    for hh in range(HG_HEADS):
        sl = slice(hh * HG_DIM, (hh + 1) * HG_DIM)
        parts.append(_rms(o_hg[:, sl], hg_out_norm[:, sl]))
    o = jnp.concatenate(parts, axis=-1) * _silu(proj[:, 3 * HG_WIDTH:4 * HG_WIDTH])
    return jnp.concatenate([o, o_mla], axis=-1)


def _rope_perm():
    p = np.zeros((128, 128), np.float32)
    half = MLA_ROPE // 2
    for i in range(half):
        p[i + half, i] = -1.0
        p[i, i + half] = 1.0
    return jnp.asarray(p)


def _mixer_fwd(h, cos, sin, w, tag):
    d = h.shape[1]
    hn, = _rows(_rms, [h], [w["norm_mix"]], [(d, bf16)], name=f"mix_norm_{tag}")
    proj = _mm(hn, w["mix_w_in"], name=f"mix_in_{tag}")
    pa = [w["lb0"], w["lb1"], w["mla_q_a_norm"], w["mla_kv_a_norm"]]
    qf, kk, vv, logf, cqn, ckvn = _rows(
        _mix_a, [proj], pa, [(HG_WIDTH, f32)] * 4 + [(MLA_Q_RANK, bf16), (MLA_KV_RANK, bf16)], name=f"mix_a_{tag}")
    o_hg, states = _hg_fwd(qf, kk, vv, logf, name=f"hg_fwd_{tag}")
    qraw = _mm(cqn, w["mla_w_uq"], name=f"mla_uq_{tag}")
    kvraw = _mm(ckvn, w["mla_w_ukv"], name=f"mla_ukv_{tag}")
    pb = [w["mla_qn_nope"], w["mla_qn_rope"], w["mla_kn_nope"], w["mla_kn_rope"], w["rope_perm"]]
    qfull, kfull, vfull = _rows(_mix_b, [qraw, kvraw, proj, cos, sin], pb,
                                [(MLA_HEADS * MLA_QK_PAD, bf16)] * 2 + [(MLA_HEADS * MLA_V, bf16)],
                                name=f"mix_b_{tag}")
    o_mla, lse = _att_fwd(qfull, kfull, vfull, name=f"att_fwd_{tag}")
    mixin, = _rows(_mix_c, [o_hg, proj, o_mla], [w["hg_out_norm"]], [(d, bf16)], name=f"mix_c_{tag}")
    out = _mm(mixin, w["mix_w_out"], add=h, name=f"mix_out_{tag}")
    return out, (h, hn, proj, qf, kk, vv, logf, cqn, ckvn, o_hg, states, qraw, kvraw, qfull, kfull, vfull, o_mla,
                 lse, mixin)


def _mixer_bwd(dout, cos, sin, w, saved, tag):
    (h, hn, proj, qf, kk, vv, logf, cqn, ckvn, o_hg, states, qraw, kvraw, qfull, kfull, vfull, o_mla, lse,
     mixin) = saved
    g = {}
    dmixin = _mm(dout, w["mix_w_out"], tb=True, name=f"mix_dmixin_{tag}")
    g["mix_w_out"] = _mm(mixin, dout, ta=True, name=f"mix_dwout_{tag}")
    (do_hg, dproj_c, do_mla), (g["hg_out_norm"],) = _rows_bwd(
        _mix_c, [o_hg, proj, o_mla], [w["hg_out_norm"]], [dmixin], rgrad=[f32, f32, bf16], pgrad=[True],
        name=f"mix_dc_{tag}")
    dqfull, dkfull, dvfull = _att_bwd(qfull, kfull, vfull, o_mla, lse, do_mla, name=f"att_bwd_{tag}")
    pb = [w["mla_qn_nope"], w["mla_qn_rope"], w["mla_kn_nope"], w["mla_kn_rope"], w["rope_perm"]]
    (dqraw, dkvraw, dproj_b), pg = _rows_bwd(
        _mix_b, [qraw, kvraw, proj, cos, sin], pb, [dqfull, dkfull, dvfull],
        rgrad=[bf16, bf16, f32, None, None], pgrad=[True, True, True, True, False], addends={2: dproj_c},
        name=f"mix_db_{tag}")
    g["mla_qn_nope"], g["mla_qn_rope"], g["mla_kn_nope"], g["mla_kn_rope"] = pg
    dcqn = _mm(dqraw, w["mla_w_uq"], tb=True, name=f"mla_dcq_{tag}")
    g["mla_w_uq"] = _mm(cqn, dqraw, ta=True, name=f"mla_dwuq_{tag}")
    dckvn = _mm(dkvraw, w["mla_w_ukv"], tb=True, name=f"mla_dckv_{tag}")
    g["mla_w_ukv"] = _mm(ckvn, dkvraw, ta=True, name=f"mla_dwukv_{tag}")
    dqf, dkk, dvv, dlogf = _hg_bwd(qf, kk, vv, logf, states, do_hg, name=f"hg_bwd_{tag}")
    pa = [w["lb0"], w["lb1"], w["mla_q_a_norm"], w["mla_kv_a_norm"]]
    (dproj,), (g["lb0"], g["lb1"], g["mla_q_a_norm"], g["mla_kv_a_norm"]) = _rows_bwd(
        _mix_a, [proj], pa, [dqf, dkk, dvv, dlogf, dcqn, dckvn], rgrad=[bf16], pgrad=[True] * 4,
        addends={0: dproj_b}, name=f"mix_da_{tag}")
    dhn = _mm(dproj, w["mix_w_in"], tb=True, out_dtype=bf16, name=f"mix_dhn_{tag}")
    g["mix_w_in"] = _mm(hn, dproj, ta=True, name=f"mix_dwin_{tag}")
    (dh,), (g["norm_mix"],) = _rows_bwd(_rms, [h], [w["norm_mix"]], [dhn], rgrad=[f32], pgrad=[True],
                                        addends={0: dout}, name=f"mix_dnorm_{tag}")
    return dh, g


def _rope_tables(positions):
    inv_freq = 1.0 / (ROPE_BASE ** (jnp.arange(0, MLA_ROPE, 2, dtype=f32) / MLA_ROPE))
    ang = positions.astype(f32)[:, None] * inv_freq
    z = jnp.zeros((positions.shape[0], 128 - MLA_ROPE), f32)
    return (jnp.concatenate([jnp.cos(ang), jnp.cos(ang), z], axis=1),
            jnp.concatenate([jnp.sin(ang), jnp.sin(ang), z], axis=1))


def _pad_cols(a, n):
    return jnp.pad(a, ((0, 0), (0, n - a.shape[1])))


def _even_weights(p, j, layer, dt):
    w_uq = p["mla_w_uq"][j].reshape(MLA_Q_RANK, MLA_HEADS, MLA_QK)
    w_uq = jnp.pad(w_uq, ((0, 0), (0, 0), (0, MLA_QK_PAD - MLA_QK))).reshape(MLA_Q_RANK, MLA_HEADS * MLA_QK_PAD)
    return dict(
        norm_mix=p["norm_mix"][layer][None], mix_w_in=_pad_cols(p["mix_w_in"][j], IN_PAD).astype(dt),
        lb0=p["hg_lb_logits"][0][None], lb1=p["hg_lb_logits"][1][None],
        mla_q_a_norm=p["mla_q_a_norm"][j][None], mla_kv_a_norm=p["mla_kv_a_norm"][j][None],
        mla_w_uq=w_uq.astype(dt), mla_w_ukv=p["mla_w_ukv"][j].astype(dt),
        mla_qn_nope=p["mla_qn_nope"][j][None], mla_qn_rope=_pad_cols(p["mla_qn_rope"][j][None], 128),
        mla_kn_nope=p["mla_kn_nope"][j][None], mla_kn_rope=_pad_cols(p["mla_kn_rope"][j][None], 128),
        rope_perm=_rope_perm(), hg_out_norm=p["hg_out_norm"][j][None], mix_w_out=p["mix_w_out"][j].astype(dt))


def _even_grads(g):
    w_uq = g["mla_w_uq"].reshape(MLA_Q_RANK, MLA_HEADS, MLA_QK_PAD)[:, :, :MLA_QK].reshape(MLA_Q_RANK, -1)
    return dict(
        norm_mix=g["norm_mix"], mix_w_in=g["mix_w_in"][:, :IN_WIDTH][None],
        hg_lb_logits=jnp.concatenate([g["lb0"], g["lb1"]], axis=0),
        mla_q_a_norm=g["mla_q_a_norm"], mla_kv_a_norm=g["mla_kv_a_norm"], mla_w_uq=w_uq[None],
        mla_w_ukv=g["mla_w_ukv"][None], mla_qn_nope=g["mla_qn_nope"], mla_qn_rope=g["mla_qn_rope"][:, :MLA_ROPE],
        mla_kn_nope=g["mla_kn_nope"], mla_kn_rope=g["mla_kn_rope"][:, :MLA_ROPE],
        hg_out_norm=g["hg_out_norm"], mix_w_out=g["mix_w_out"][None])


_S5_NB = 8
_S5_BW = 1024
_S5_HALF = 512
_S5_UC = 128
_S5_TIME = 512


def _bd_mm(a, b3, *, name, tb=False, out_dtype=f32):
    length = a.shape[0]
    rows_b, cols_b = b3.shape[0] // _S5_NB, b3.shape[1]
    ka, n = (cols_b, rows_b) if tb else (rows_b, cols_b)
    bm = _pick(length, (512, 256, 128))
    dims = _NT if tb else _NN

    def body(a_ref, b_ref, o_ref):
        o_ref[...] = lax.dot_general(a_ref[...].astype(bf16), b_ref[...].astype(bf16), dims,
                                     preferred_element_type=f32).astype(o_ref.dtype)

    return pl.pallas_call(
        body, grid=(length // bm, _S5_NB),
        in_specs=[pl.BlockSpec((bm, ka), lambda i, j: (i, j)), pl.BlockSpec((rows_b, cols_b), lambda i, j: (j, 0))],
        out_specs=pl.BlockSpec((bm, n), lambda i, j: (i, j)),
        out_shape=jax.ShapeDtypeStruct((length, _S5_NB * n), out_dtype),
        compiler_params=_cparams(("parallel", "parallel")), name=name)(a, b3)


def _bd_mm_tn(a, c, *, name):
    length = a.shape[0]
    ka, n = a.shape[1] // _S5_NB, c.shape[1] // _S5_NB
    bk = _pick(length, (512, 256, 128))
    nk = length // bk

    def body(a_ref, c_ref, o_ref):
        @pl.when(pl.program_id(1) == 0)
        def _():
            o_ref[...] = jnp.zeros_like(o_ref)
        o_ref[...] += lax.dot_general(a_ref[...].astype(bf16), c_ref[...].astype(bf16), _TN,
                                      preferred_element_type=f32)

    return pl.pallas_call(
        body, grid=(_S5_NB, nk),
        in_specs=[pl.BlockSpec((bk, ka), lambda j, q: (q, j)), pl.BlockSpec((bk, n), lambda j, q: (q, j))],
        out_specs=pl.BlockSpec((ka, n), lambda j, q: (j, 0)),
        out_shape=jax.ShapeDtypeStruct((_S5_NB * ka, n), f32),
        compiler_params=_cparams(("parallel", "arbitrary")), name=name)(a, c)


def _cmul(ar, ai, br, bi):
    return ar * br - ai * bi, ar * bi + ai * br


def _pow_table(ar, ai, descending):
    rows = lax.broadcasted_iota(jnp.int32, (8, ar.shape[1]), 0)
    tr = jnp.zeros((8, ar.shape[1]), f32)
    ti = jnp.zeros((8, ar.shape[1]), f32)
    pr, pi_ = ar, ai
    for r in range(8):
        sel = rows == ((7 - r) if descending else r)
        tr = jnp.where(sel, pr, tr)
        ti = jnp.where(sel, pi_, ti)
        pr, pi_ = _cmul(pr, pi_, ar, ai)
    return tr, ti


def _s5_scan_fwd(a, x, *, name):
    length = x.shape[0]
    tc = min(_S5_TIME, length)
    hw = _S5_HALF

    def body(a_ref, x_ref, o_ref, carry):
        @pl.when(pl.program_id(1) == 0)
        def _():
            carry[...] = jnp.zeros_like(carry)

        ar, ai = a_ref[:, :hw], a_ref[:, hw:]
        xr, xi = x_ref[:, :hw], x_ref[:, hw:]
        row8 = lax.broadcasted_iota(jnp.int32, (tc, hw), 0) & 7
        mr, mi = ar, ai
        for s in (1, 2, 4):
            sr, si = pltpu.roll(xr, s, axis=0), pltpu.roll(xi, s, axis=0)
            pr, pi_ = _cmul(mr, mi, sr, si)
            ok = row8 >= s
            xr = xr + jnp.where(ok, pr, 0.0)
            xi = xi + jnp.where(ok, pi_, 0.0)
            mr, mi = _cmul(mr, mi, mr, mi)
        o_ref[:, :hw] = xr
        o_ref[:, hw:] = xi
        tr, ti = _pow_table(ar, ai, False)
        cr, ci = carry[:, :hw], carry[:, hw:]
        for i in range(tc // 8):
            sl = slice(8 * i, 8 * i + 8)
            pr, pi_ = _cmul(tr, ti, cr, ci)
            o_ref[sl, :hw] = o_ref[sl, :hw] + pr
            o_ref[sl, hw:] = o_ref[sl, hw:] + pi_
            cr, ci = o_ref[8 * i + 7:8 * i + 8, :hw], o_ref[8 * i + 7:8 * i + 8, hw:]
        carry[:, :hw] = cr
        carry[:, hw:] = ci

    return pl.pallas_call(
        body, grid=(_S5_NB, length // tc),
        in_specs=[pl.BlockSpec((1, _S5_BW), lambda j, t: (0, j)), pl.BlockSpec((tc, _S5_BW), lambda j, t: (t, j))],
        out_specs=pl.BlockSpec((tc, _S5_BW), lambda j, t: (t, j)),
        out_shape=jax.ShapeDtypeStruct(x.shape, f32),
        scratch_shapes=[pltpu.VMEM((1, _S5_BW), f32)],
        compiler_params=_cparams(("parallel", "arbitrary")), name=name)(a, x)


def _s5_scan_bwd(a, hs, d, *, name):
    length = d.shape[0]
    tc = min(_S5_TIME, length)
    nt = length // tc
    hw = _S5_HALF

    def body(a_ref, h_ref, d_ref, g_ref, da_ref, carry):
        @pl.when(pl.program_id(1) == 0)
        def _():
            carry[...] = jnp.zeros_like(carry)
            da_ref[...] = jnp.zeros_like(da_ref)

        ar, ai = a_ref[:, :hw], -a_ref[:, hw:]
        xr, xi = d_ref[:, :hw], d_ref[:, hw:]
        rows = lax.broadcasted_iota(jnp.int32, (tc, hw), 0)
        row8 = rows & 7
        mr, mi = ar, ai
        for s in (1, 2, 4):
            sr, si = pltpu.roll(xr, tc - s, axis=0), pltpu.roll(xi, tc - s, axis=0)
            pr, pi_ = _cmul(mr, mi, sr, si)
            ok = row8 < 8 - s
            xr = xr + jnp.where(ok, pr, 0.0)
            xi = xi + jnp.where(ok, pi_, 0.0)
            mr, mi = _cmul(mr, mi, mr, mi)
        g_ref[:, :hw] = xr
        g_ref[:, hw:] = xi
        tr, ti = _pow_table(ar, ai, True)
        cr0, ci0 = carry[:, :hw], carry[:, hw:]
        cr, ci = cr0, ci0
        for i in reversed(range(tc // 8)):
            sl = slice(8 * i, 8 * i + 8)
            pr, pi_ = _cmul(tr, ti, cr, ci)
            g_ref[sl, :hw] = g_ref[sl, :hw] + pr
            g_ref[sl, hw:] = g_ref[sl, hw:] + pi_
            cr, ci = g_ref[8 * i:8 * i + 1, :hw], g_ref[8 * i:8 * i + 1, hw:]
        carry[:, :hw] = cr
        carry[:, hw:] = ci
        last = rows == tc - 1
        gnr = jnp.where(last, cr0, pltpu.roll(g_ref[:, :hw], tc - 1, axis=0))
        gni = jnp.where(last, ci0, pltpu.roll(g_ref[:, hw:], tc - 1, axis=0))
        hr, hi = h_ref[:, :hw], h_ref[:, hw:]
        da_ref[:, :hw] += jnp.sum(hr * gnr + hi * gni, axis=0, keepdims=True)
        da_ref[:, hw:] += jnp.sum(hr * gni - hi * gnr, axis=0, keepdims=True)

    blk = pl.BlockSpec((tc, _S5_BW), lambda j, t: (nt - 1 - t, j))
    row = pl.BlockSpec((1, _S5_BW), lambda j, t: (0, j))
    return pl.pallas_call(
        body, grid=(_S5_NB, nt), in_specs=[row, blk, blk], out_specs=[blk, row],
        out_shape=[jax.ShapeDtypeStruct(d.shape, f32), jax.ShapeDtypeStruct((1, _S5_NB * _S5_BW), f32)],
        scratch_shapes=[pltpu.VMEM((1, _S5_BW), f32)],
        compiler_params=_cparams(("parallel", "arbitrary")), name=name)(a, hs, d)


def _s5_disc(lr, li, ldt, btr, bti, expand):
    dt = jnp.exp(ldt)
    mag = jnp.exp(lr * dt)
    abr = mag * jnp.cos(li * dt)
    abi = mag * jnp.sin(li * dt)
    den = lr * lr + li * li
    zr = ((abr - 1.0) * lr + abi * li) / den
    zi = (abi * lr - (abr - 1.0) * li) / den
    zr = jnp.dot(zr, expand, precision=lax.Precision.HIGHEST, preferred_element_type=f32)
    zi = jnp.dot(zi, expand, precision=lax.Precision.HIGHEST, preferred_element_type=f32)
    return abr, abi, zr * btr - zi * bti, zr * bti + zi * btr


def _s5_disc_fwd(args, *, name):
    def body(*refs):
        res = _s5_disc(*[r[...] for r in refs[:6]])
        for o, v in zip(refs[6:], res):
            o[...] = v

    sds = jax.ShapeDtypeStruct
    return pl.pallas_call(body, out_shape=[sds(args[0].shape, f32)] * 2 + [sds(args[3].shape, f32)] * 2,
                          name=name)(*args)


def _s5_disc_bwd(args, cts, *, name):
    def body(*refs):
        vals = [r[...] for r in refs[:6]]
        _, vjp = jax.vjp(lambda *d: _s5_disc(*d, vals[5]), *vals[:5])
        grads = vjp(tuple(r[...] for r in refs[6:10]))
        for o, v in zip(refs[10:], grads):
            o[...] = v

    return pl.pallas_call(body, out_shape=[jax.ShapeDtypeStruct(a.shape, f32) for a in args[:5]],
                          name=name)(*args, *cts)


def _gelu_tanh(x):
    return 0.5 * x * (1.0 + jnp.tanh(0.7978845608028654 * (x + 0.044715 * (x * x * x))))


def _s5_post(y, u, d_skip):
    return _gelu_tanh(y + d_skip * u)


def _s5_glu(ga, gb, h):
    return h + ga * jax.nn.sigmoid(gb)


def _s5_expand():
    e = np.zeros((S5_STATE, S5_GROUP * S5_STATE), np.float32)
    for m in range(S5_GROUP):
        e[np.arange(S5_STATE), m * S5_STATE + np.arange(S5_STATE)] = 1.0
    return jnp.asarray(e)


def _s5_pack_b(bbr, bbi):
    eye = jnp.eye(8, dtype=f32)

    def one(bb):
        b5 = bb.reshape(_S5_NB, 8, S5_GROUP, S5_STATE)
        return jnp.einsum("jgmp,gh->jgmhp", b5, eye).reshape(_S5_NB * _S5_UC, _S5_HALF)

    return jnp.concatenate([one(bbr), one(bbi)], axis=1)


def _s5_unpack_b(db3):
    def one(d):
        d5 = d.reshape(_S5_NB, 8, S5_GROUP, 8, S5_STATE)
        return jnp.einsum("jgmgp->jgmp", d5).reshape(S5_GROUPS, S5_GROUP * S5_STATE)

    return one(db3[:, :_S5_HALF]), one(db3[:, _S5_HALF:])


def _s5_pack_c(c_re, c_im):
    eye = jnp.eye(8, dtype=f32)

    def one(c):
        c4 = c.reshape(_S5_NB, 8, S5_GROUP, S5_STATE)
        return jnp.einsum("jgmp,hg->jhpgm", c4, eye).reshape(_S5_NB, _S5_HALF, _S5_UC)

    return jnp.concatenate([one(c_re), -one(c_im)], axis=1).reshape(_S5_NB * _S5_BW, _S5_UC)


def _s5_unpack_c(dc3):
    d = dc3.reshape(_S5_NB, 2, 8, S5_STATE, 8, S5_GROUP)
    dre = jnp.einsum("jgpgm->jgmp", d[:, 0]).reshape(S5_GROUPS, S5_GROUP, S5_STATE)
    dim = -jnp.einsum("jgpgm->jgmp", d[:, 1]).reshape(S5_GROUPS, S5_GROUP, S5_STATE)
    return dre, dim


def _s5_state_row(re, im):
    r = re.reshape(_S5_NB, 1, _S5_HALF)
    i = im.reshape(_S5_NB, 1, _S5_HALF)
    return jnp.concatenate([r, i], axis=2).reshape(1, _S5_NB * _S5_BW)


def _s5_unstate_row(row):
    r = row.reshape(_S5_NB, 2, 8, S5_STATE)
    return r[:, 0].reshape(S5_GROUPS, S5_STATE), r[:, 1].reshape(S5_GROUPS, S5_STATE)


def _s5_fwd(h, w, tag):
    d = h.shape[1]
    hn, = _rows(_rms, [h], [w["norm_mix"]], [(d, f32)], name=f"s5_norm_{tag}")
    disc_in = [w["s5_lam_re"], w["s5_lam_im"], w["s5_log_dt"], w["s5_bt_re"], w["s5_bt_im"], w["s5_expand"]]
    abr, abi, bbr, bbi = _s5_disc_fwd(disc_in, name=f"s5_disc_{tag}")
    a_row = _s5_state_row(abr, abi)
    b3 = _s5_pack_b(bbr, bbi)
    bu = _bd_mm(hn, b3, name=f"s5_bu_{tag}")
    hs = _s5_scan_fwd(a_row, bu, name=f"s5_scan_{tag}")
    y = _bd_mm(hs, w["s5_c3"], name=f"s5_y_{tag}")
    yg, = _rows(_s5_post, [y, hn], [w["s5_d"]], [(d, bf16)], name=f"s5_post_{tag}")
    ga = _mm(yg, w["s5_w_glu_a"], name=f"s5_glu_a_{tag}")
    gb = _mm(yg, w["s5_w_glu_b"], name=f"s5_glu_b_{tag}")
    out, = _rows(_s5_glu, [ga, gb, h], [], [(d, f32)], name=f"s5_glu_{tag}")
    return out, (h, hn, disc_in, a_row, b3, hs, y, yg, ga, gb)


def _s5_bwd(dout, w, saved, tag):
    h, hn, disc_in, a_row, b3, hs, y, yg, ga, gb = saved
    g = {}
    (dga, dgb), _ = _rows_bwd(_s5_glu, [ga, gb, h], [], [dout], rgrad=[bf16, bf16, None], pgrad=[],
                              name=f"s5_dglu_{tag}")
    dyg = _mm(dga, w["s5_w_glu_a"], tb=True, name=f"s5_dyg_a_{tag}")
    dyg = _mm(dgb, w["s5_w_glu_b"], tb=True, add=dyg, name=f"s5_dyg_b_{tag}")
    g["s5_w_glu_a"] = _mm(yg, dga, ta=True, name=f"s5_dwa_{tag}")
    g["s5_w_glu_b"] = _mm(yg, dgb, ta=True, name=f"s5_dwb_{tag}")
    (dy, du_skip), (g["s5_d"],) = _rows_bwd(_s5_post, [y, hn], [w["s5_d"]], [dyg], rgrad=[bf16, f32], pgrad=[True],
                                           name=f"s5_dpost_{tag}")
    dhs = _bd_mm(dy, w["s5_c3"], tb=True, name=f"s5_dhs_{tag}")
    dc3 = _bd_mm_tn(hs, dy, name=f"s5_dc_{tag}")
    gs, da_row = _s5_scan_bwd(a_row, hs, dhs, name=f"s5_dscan_{tag}")
    du = _bd_mm(gs, b3, tb=True, name=f"s5_du_{tag}")
    db3 = _bd_mm_tn(hn, gs, name=f"s5_db_{tag}")
    dabr, dabi = _s5_unstate_row(da_row)
    dbbr, dbbi = _s5_unpack_b(db3)
    g["s5_lam_re"], g["s5_lam_im"], g["s5_log_dt"], g["s5_bt_re"], g["s5_bt_im"] = _s5_disc_bwd(
        disc_in, [dabr, dabi, dbbr, dbbi], name=f"s5_ddisc_{tag}")
    g["s5_c_re"], g["s5_c_im"] = _s5_unpack_c(dc3)
    (dh,), (g["norm_mix"],) = _rows_bwd(_rms_twice, [h], [w["norm_mix"]], [du, du_skip], rgrad=[f32], pgrad=[True],
                                        addends={0: dout}, name=f"s5_dnorm_{tag}")
    return dh, g


def _odd_weights(p, j, layer, dt):
    tr = lambda b: b.transpose(0, 2, 1).reshape(S5_GROUPS, S5_GROUP * S5_STATE)
    return dict(
        norm_mix=p["norm_mix"][layer][None], s5_lam_re=p["s5_lam_re"][j], s5_lam_im=p["s5_lam_im"][j],
        s5_log_dt=p["s5_log_dt"][j][:, None], s5_bt_re=tr(p["s5_b_re"][j]), s5_bt_im=tr(p["s5_b_im"][j]),
        s5_expand=_s5_expand(), s5_c3=_s5_pack_c(p["s5_c_re"][j], p["s5_c_im"][j]).astype(dt),
        s5_d=p["s5_d"][j][None], s5_w_glu_a=p["s5_w_glu_a"][j].astype(dt), s5_w_glu_b=p["s5_w_glu_b"][j].astype(dt))


def _odd_grads(g):
    tr = lambda b: b.reshape(S5_GROUPS, S5_GROUP, S5_STATE).transpose(0, 2, 1)[None]
    return dict(
        norm_mix=g["norm_mix"], s5_lam_re=g["s5_lam_re"][None], s5_lam_im=g["s5_lam_im"][None],
        s5_log_dt=g["s5_log_dt"][:, 0][None], s5_b_re=tr(g["s5_bt_re"]), s5_b_im=tr(g["s5_bt_im"]),
        s5_c_re=g["s5_c_re"][None], s5_c_im=g["s5_c_im"][None], s5_d=g["s5_d"],
        s5_w_glu_a=g["s5_w_glu_a"][None], s5_w_glu_b=g["s5_w_glu_b"][None])


def _loss_fn(y, t):
    e = y - t
    part = jnp.sum(jnp.sum(e * e, axis=-1, keepdims=True), axis=0, keepdims=True) * (0.5 / y.shape[1])
    return e * (1.0 / y.shape[1]), part


def _layer_weights(p, layer, dt):
    return dict(
        norm_xa=p["norm_xa"][layer][None], norm_mem=p["norm_mem"][layer][None], norm_ffn=p["norm_ffn"][layer][None],
        xa_wq=p["xa_wq"][layer].astype(dt), xa_wk=p["xa_wk"][layer].astype(dt), xa_wv=p["xa_wv"][layer].astype(dt),
        xa_wo=p["xa_wo"][layer].astype(dt), xa_q_norm=p["xa_q_norm"][layer][None],
        xa_k_norm=p["xa_k_norm"][layer][None], ffn_w_up=p["ffn_w_up"][layer].astype(dt),
        ffn_conv_w=p["ffn_conv_w"][layer], ffn_conv_b=p["ffn_conv_b"][layer][None],
        ffn_w_down=p["ffn_w_down"][layer].astype(dt))


_PER_LAYER = ("norm_xa", "norm_mem", "norm_ffn", "xa_wq", "xa_wk", "xa_wv", "xa_wo", "xa_q_norm", "xa_k_norm",
              "ffn_w_up", "ffn_conv_w", "ffn_conv_b", "ffn_w_down")


def _local_step(x, mem, positions, target, p):
    cos, sin = _rope_tables(positions)
    we = _even_weights(p, 0, 0, bf16)
    wo = _odd_weights(p, 0, 1, bf16)
    wl = [_layer_weights(p, layer, bf16) for layer in range(2)]

    h, s_mix0 = _mixer_fwd(x, cos, sin, we, "l0")
    h, s_xa0 = _xattn_fwd(h, mem, wl[0], "l0")
    h, s_ff0 = _ffn_fwd(h, wl[0], "l0")
    h, s_mix1 = _s5_fwd(h, wo, "l1")
    h, s_xa1 = _xattn_fwd(h, mem, wl[1], "l1")
    h, s_ff1 = _ffn_fwd(h, wl[1], "l1")
    dh, loss = _rows(_loss_fn, [h, target], [], [(h.shape[1], f32)], accs=[(1, 1)], name="loss_head")

    gl = [{}, {}]
    dh, g = _ffn_bwd(dh, wl[1], s_ff1, "l1")
    gl[1].update(g)
    dh, g = _xattn_bwd(dh, mem, wl[1], s_xa1, "l1")
    gl[1].update(g)
    dh, g_odd = _s5_bwd(dh, wo, s_mix1, "l1")
    dh, g = _ffn_bwd(dh, wl[0], s_ff0, "l0")
    gl[0].update(g)
    dh, g = _xattn_bwd(dh, mem, wl[0], s_xa0, "l0")
    gl[0].update(g)
    dh, g_even = _mixer_bwd(dh, cos, sin, we, s_mix0, "l0")

    grads = {}
    for n in _PER_LAYER:
        a, b = gl[0][n], gl[1][n]
        grads[n] = jnp.concatenate([a, b], axis=0) if a.shape[0] == 1 else jnp.stack([a, b])
    ge, go = _even_grads(g_even), _odd_grads(g_odd)
    grads["norm_mix"] = jnp.concatenate([ge.pop("norm_mix"), go.pop("norm_mix")], axis=0)
    grads.update(ge)
    grads.update(go)
    return loss, dh, grads


_LANES = 1024
_ROW_PAD = 256


def _exchange(arrays, modes, *, name):
    na = len(arrays)

    def body(*refs):
        ins, outs = refs[:na], refs[na:2 * na]
        send_sems, recv_sems, local_sems = refs[2 * na:]
        x, y, c = lax.axis_index("x"), lax.axis_index("y"), lax.axis_index("c")
        me = 4 * x + 2 * y + c

        def peer(mask):
            px = 1 - x if mask & 4 else x
            py = 1 - y if mask & 2 else y
            pc = 1 - c if mask & 1 else c
            return (px, py, pc), 4 * px + 2 * py + pc

        def src(k, idx):
            return ins[k] if modes[k] == "g" else ins[k].at[idx]

        def remote(k, mask):
            dev, idx = peer(mask)
            s = k * (N_DEV - 1) + mask - 1
            send = pltpu.make_async_remote_copy(src_ref=src(k, idx), dst_ref=outs[k].at[me],
                                                send_sem=send_sems.at[s], recv_sem=recv_sems.at[s],
                                                device_id=dev, device_id_type=pl.DeviceIdType.MESH)
            recv = pltpu.make_async_remote_copy(src_ref=src(k, idx), dst_ref=outs[k].at[idx],
                                                send_sem=send_sems.at[s], recv_sem=recv_sems.at[s],
                                                device_id=dev, device_id_type=pl.DeviceIdType.MESH)
            return send, recv

        local = [pltpu.make_async_copy(src(k, me), outs[k].at[me], local_sems.at[k]) for k in range(na)]
        for cp in local:
            cp.start()
        copies = [remote(k, mask) for mask in (1, 2, 4, 3, 5, 6, 7) for k in range(na)]
        for send, _ in copies:
            send.start()
        for send, recv in copies:
            recv.wait_recv()
        for send, _ in copies:
            send.wait_send()
        for cp in local:
            cp.wait()

    hbm = pl.BlockSpec(memory_space=pltpu.HBM)
    out_shape = [jax.ShapeDtypeStruct((N_DEV,) + a.shape[-2:], a.dtype) for a in arrays]
    return pl.pallas_call(
        body, in_specs=[hbm] * na, out_specs=[hbm] * na, out_shape=out_shape,
        scratch_shapes=[pltpu.SemaphoreType.DMA((na * (N_DEV - 1),)), pltpu.SemaphoreType.DMA((na * (N_DEV - 1),)),
                        pltpu.SemaphoreType.DMA((na,))],
        compiler_params=pltpu.CompilerParams(has_side_effects=True), name=name)(*arrays)


def _sum_adam(slots, w, m, v, *, name):
    rows, cols = w.shape
    tr = _pick(rows, (256, 128, 64, 32, 16, 8))
    bc1 = 1.0 - ADAM_B1 ** ADAM_STEP
    bc2 = 1.0 - ADAM_B2 ** ADAM_STEP

    def body(s_ref, w_ref, m_ref, v_ref, g_ref, d_ref, nm_ref, nv_ref):
        g = s_ref[0]
        for k in range(1, N_DEV):
            g = g + s_ref[k]
        mm = ADAM_B1 * m_ref[...] + (1.0 - ADAM_B1) * g
        vv = ADAM_B2 * v_ref[...] + (1.0 - ADAM_B2) * (g * g)
        g_ref[...] = g
        nm_ref[...] = mm
        nv_ref[...] = vv
        d_ref[...] = -ADAM_LR * ((mm / bc1) / (jnp.sqrt(vv / bc2) + ADAM_EPS) + ADAM_WD * w_ref[...])

    blk = pl.BlockSpec((tr, cols), lambda i: (i, 0))
    sds = jax.ShapeDtypeStruct((rows, cols), f32)
    return pl.pallas_call(
        body, grid=(rows // tr,), in_specs=[pl.BlockSpec((N_DEV, tr, cols), lambda i: (0, i, 0)), blk, blk, blk],
        out_specs=[blk] * 4, out_shape=[sds] * 4, compiler_params=_cparams(("parallel",)), name=name)(slots, w, m, v)


_SHARDED = dict(xa_wq=1, xa_wk=1, xa_wv=1, xa_wo=1, ffn_w_up=2, ffn_conv_w=2, ffn_w_down=1, mix_w_in=2, mla_w_uq=2,
                mla_w_ukv=2, mix_w_out=1, s5_d=1, s5_w_glu_a=1, s5_w_glu_b=1)
_EXACT = ("ffn_conv_w", "s5_d")
_WEIGHTS = ("norm_mix", "norm_xa", "norm_mem", "norm_ffn", "xa_wq", "xa_wk", "xa_wv", "xa_wo", "xa_q_norm",
            "xa_k_norm", "ffn_w_up", "ffn_conv_w", "ffn_conv_b", "ffn_w_down", "hg_lb_logits", "mix_w_in",
            "hg_out_norm", "mla_q_a_norm", "mla_w_uq", "mla_kv_a_norm", "mla_w_ukv", "mla_qn_nope", "mla_qn_rope",
            "mla_kn_nope", "mla_kn_rope", "mix_w_out", "s5_lam_re", "s5_lam_im", "s5_log_dt", "s5_b_re", "s5_b_im",
            "s5_c_re", "s5_c_im", "s5_d", "s5_w_glu_a", "s5_w_glu_b")
_BIG = tuple(n for n in _WEIGHTS if n in _SHARDED and n not in _EXACT)
_SHARD_ORDER = tuple(n for n in _WEIGHTS if n in _SHARDED)
_REPL_ORDER = tuple(n for n in _WEIGHTS if n not in _SHARDED)


def _pack(parts, dtype, lead=None):
    nl = 0 if lead is None else 1
    flat = [a.astype(dtype).reshape(a.shape[:nl] + (-1,)) for a in parts]
    cat = jnp.concatenate(flat, axis=nl)
    n = cat.shape[nl]
    unit = _LANES * _ROW_PAD
    total = -(-n // unit) * unit
    cat = jnp.pad(cat, [(0, 0)] * nl + [(0, total - n)])
    return cat.reshape(cat.shape[:nl] + (total // _LANES, _LANES))


def _unpack(packed, shapes, lead=None):
    nl = 0 if lead is None else 1
    flat = packed.reshape(packed.shape[:nl] + (-1,))
    out, off = [], 0
    for s in shapes:
        n = int(np.prod(s))
        piece = flat[..., off:off + n] if nl else flat[off:off + n]
        out.append(piece.reshape(packed.shape[:nl] + tuple(s)))
        off += n
    return out


def _to_full(gathered, axis):
    g = jnp.moveaxis(gathered, 0, axis)
    s = g.shape
    return g.reshape(s[:axis] + (s[axis] * s[axis + 1],) + s[axis + 2:])


def _to_shards(full, axis):
    s = full.shape
    g = full.reshape(s[:axis] + (N_DEV, s[axis] // N_DEV) + s[axis + 1:])
    return jnp.moveaxis(g, axis, 0)


def _train_step(x, mem, positions, target, w, m, v):
    big_shapes = [w[n].shape for n in _BIG]
    exact_shapes = [w[n].shape for n in _EXACT]
    gathered_big, gathered_exact = _exchange(
        [_pack([w[n] for n in _BIG], bf16), _pack([w[n] for n in _EXACT], f32)], ["g", "g"], name="gather_weights")
    p = {n: w[n] for n in _REPL_ORDER}
    for n, a in zip(_BIG, _unpack(gathered_big, big_shapes, lead=True)):
        p[n] = _to_full(a, _SHARDED[n])
    for n, a in zip(_EXACT, _unpack(gathered_exact, exact_shapes, lead=True)):
        p[n] = _to_full(a, _SHARDED[n])

    loss, grad_x, grads = _local_step(x, mem, positions, target, p)

    shard_shapes = [w[n].shape for n in _SHARD_ORDER]
    repl_shapes = [w[n].shape for n in _REPL_ORDER]
    send_sh = _pack([_to_shards(grads[n], _SHARDED[n]) for n in _SHARD_ORDER], f32, lead=True)
    send_rp = _pack([grads[n].reshape(w[n].shape) for n in _REPL_ORDER], f32)
    slots_sh, slots_rp = _exchange([send_sh, send_rp], ["s", "g"], name="reduce_grads")
    pk = lambda d, order: _pack([d[n] for n in order], f32)
    res_sh = _sum_adam(slots_sh, pk(w, _SHARD_ORDER), pk(m, _SHARD_ORDER), pk(v, _SHARD_ORDER), name="adam_sharded")
    res_rp = _sum_adam(slots_rp, pk(w, _REPL_ORDER), pk(m, _REPL_ORDER), pk(v, _REPL_ORDER), name="adam_replicated")
    out = []
    for k in range(4):
        d = dict(zip(_SHARD_ORDER, _unpack(res_sh[k], shard_shapes)))
        d.update(zip(_REPL_ORDER, _unpack(res_rp[k], repl_shapes)))
        out.append(d)
    return loss, grad_x, out


_INPUTS = tuple("""x, mem, positions, norm_mix, norm_xa, norm_mem, norm_ffn, xa_wq, xa_wk, xa_wv, xa_wo, xa_q_norm, xa_k_norm, ffn_w_up, ffn_conv_w, ffn_conv_b, ffn_w_down, hg_lb_logits, mix_w_in, hg_out_norm, mla_q_a_norm, mla_w_uq, mla_kv_a_norm, mla_w_ukv, mla_qn_nope, mla_qn_rope, mla_kn_nope, mla_kn_rope, mix_w_out, s5_lam_re, s5_lam_im, s5_log_dt, s5_b_re, s5_b_im, s5_c_re, s5_c_im, s5_d, s5_w_glu_a, s5_w_glu_b, loss_target, m_norm_mix, m_norm_xa, m_norm_mem, m_norm_ffn, m_xa_wq, m_xa_wk, m_xa_wv, m_xa_wo, m_xa_q_norm, m_xa_k_norm, m_ffn_w_up, m_ffn_conv_w, m_ffn_conv_b, m_ffn_w_down, m_hg_lb_logits, m_mix_w_in, m_hg_out_norm, m_mla_q_a_norm, m_mla_w_uq, m_mla_kv_a_norm, m_mla_w_ukv, m_mla_qn_nope, m_mla_qn_rope, m_mla_kn_nope, m_mla_kn_rope, m_mix_w_out, m_s5_lam_re, m_s5_lam_im, m_s5_log_dt, m_s5_b_re, m_s5_b_im, m_s5_c_re, m_s5_c_im, m_s5_d, m_s5_w_glu_a, m_s5_w_glu_b, v_norm_mix, v_norm_xa, v_norm_mem, v_norm_ffn, v_xa_wq, v_xa_wk, v_xa_wv, v_xa_wo, v_xa_q_norm, v_xa_k_norm, v_ffn_w_up, v_ffn_conv_w, v_ffn_conv_b, v_ffn_w_down, v_hg_lb_logits, v_mix_w_in, v_hg_out_norm, v_mla_q_a_norm, v_mla_w_uq, v_mla_kv_a_norm, v_mla_w_ukv, v_mla_qn_nope, v_mla_qn_rope, v_mla_kn_nope, v_mla_kn_rope, v_mix_w_out, v_s5_lam_re, v_s5_lam_im, v_s5_log_dt, v_s5_b_re, v_s5_b_im, v_s5_c_re, v_s5_c_im, v_s5_d, v_s5_w_glu_a, v_s5_w_glu_b""".replace(" ", "").split(","))


def kernel(x, mem, positions, norm_mix, norm_xa, norm_mem, norm_ffn, xa_wq, xa_wk, xa_wv, xa_wo, xa_q_norm, xa_k_norm, ffn_w_up, ffn_conv_w, ffn_conv_b, ffn_w_down, hg_lb_logits, mix_w_in, hg_out_norm, mla_q_a_norm, mla_w_uq, mla_kv_a_norm, mla_w_ukv, mla_qn_nope, mla_qn_rope, mla_kn_nope, mla_kn_rope, mix_w_out, s5_lam_re, s5_lam_im, s5_log_dt, s5_b_re, s5_b_im, s5_c_re, s5_c_im, s5_d, s5_w_glu_a, s5_w_glu_b, loss_target, m_norm_mix, m_norm_xa, m_norm_mem, m_norm_ffn, m_xa_wq, m_xa_wk, m_xa_wv, m_xa_wo, m_xa_q_norm, m_xa_k_norm, m_ffn_w_up, m_ffn_conv_w, m_ffn_conv_b, m_ffn_w_down, m_hg_lb_logits, m_mix_w_in, m_hg_out_norm, m_mla_q_a_norm, m_mla_w_uq, m_mla_kv_a_norm, m_mla_w_ukv, m_mla_qn_nope, m_mla_qn_rope, m_mla_kn_nope, m_mla_kn_rope, m_mix_w_out, m_s5_lam_re, m_s5_lam_im, m_s5_log_dt, m_s5_b_re, m_s5_b_im, m_s5_c_re, m_s5_c_im, m_s5_d, m_s5_w_glu_a, m_s5_w_glu_b, v_norm_mix, v_norm_xa, v_norm_mem, v_norm_ffn, v_xa_wq, v_xa_wk, v_xa_wv, v_xa_wo, v_xa_q_norm, v_xa_k_norm, v_ffn_w_up, v_ffn_conv_w, v_ffn_conv_b, v_ffn_w_down, v_hg_lb_logits, v_mix_w_in, v_hg_out_norm, v_mla_q_a_norm, v_mla_w_uq, v_mla_kv_a_norm, v_mla_w_ukv, v_mla_qn_nope, v_mla_qn_rope, v_mla_kn_nope, v_mla_kn_rope, v_mix_w_out, v_s5_lam_re, v_s5_lam_im, v_s5_log_dt, v_s5_b_re, v_s5_b_im, v_s5_c_re, v_s5_c_im, v_s5_d, v_s5_w_glu_a, v_s5_w_glu_b):
    vals = dict(zip(_INPUTS, (x, mem, positions, norm_mix, norm_xa, norm_mem, norm_ffn, xa_wq, xa_wk, xa_wv, xa_wo, xa_q_norm, xa_k_norm, ffn_w_up, ffn_conv_w, ffn_conv_b, ffn_w_down, hg_lb_logits, mix_w_in, hg_out_norm, mla_q_a_norm, mla_w_uq, mla_kv_a_norm, mla_w_ukv, mla_qn_nope, mla_qn_rope, mla_kn_nope, mla_kn_rope, mix_w_out, s5_lam_re, s5_lam_im, s5_log_dt, s5_b_re, s5_b_im, s5_c_re, s5_c_im, s5_d, s5_w_glu_a, s5_w_glu_b, loss_target, m_norm_mix, m_norm_xa, m_norm_mem, m_norm_ffn, m_xa_wq, m_xa_wk, m_xa_wv, m_xa_wo, m_xa_q_norm, m_xa_k_norm, m_ffn_w_up, m_ffn_conv_w, m_ffn_conv_b, m_ffn_w_down, m_hg_lb_logits, m_mix_w_in, m_hg_out_norm, m_mla_q_a_norm, m_mla_w_uq, m_mla_kv_a_norm, m_mla_w_ukv, m_mla_qn_nope, m_mla_qn_rope, m_mla_kn_nope, m_mla_kn_rope, m_mix_w_out, m_s5_lam_re, m_s5_lam_im, m_s5_log_dt, m_s5_b_re, m_s5_b_im, m_s5_c_re, m_s5_c_im, m_s5_d, m_s5_w_glu_a, m_s5_w_glu_b, v_norm_mix, v_norm_xa, v_norm_mem, v_norm_ffn, v_xa_wq, v_xa_wk, v_xa_wv, v_xa_wo, v_xa_q_norm, v_xa_k_norm, v_ffn_w_up, v_ffn_conv_w, v_ffn_conv_b, v_ffn_w_down, v_hg_lb_logits, v_mix_w_in, v_hg_out_norm, v_mla_q_a_norm, v_mla_w_uq, v_mla_kv_a_norm, v_mla_w_ukv, v_mla_qn_nope, v_mla_qn_rope, v_mla_kn_nope, v_mla_kn_rope, v_mix_w_out, v_s5_lam_re, v_s5_lam_im, v_s5_log_dt, v_s5_b_re, v_s5_b_im, v_s5_c_re, v_s5_c_im, v_s5_d, v_s5_w_glu_a, v_s5_w_glu_b)))
    w = {n: vals[n] for n in _WEIGHTS}
    m = {n: vals["m_" + n] for n in _WEIGHTS}
    v = {n: vals["v_" + n] for n in _WEIGHTS}
    loss, grad_x, res = _train_step(vals["x"][0], vals["mem"][0], vals["positions"][0], vals["loss_target"][0],
                                    w, m, v)
    loss = lax.psum(loss[0, 0], ("x", "y", "c"))
    return (loss, grad_x[None], *[r[n] for r in res for n in _WEIGHTS])
```

```python
import functools

import jax
import jax.numpy as jnp
import numpy as np
from jax import lax
from jax.experimental import pallas as pl
from jax.experimental.pallas import tpu as pltpu

f32 = jnp.float32
bf16 = jnp.bfloat16

EPS = 1e-6
N_DEV = 8
VMEM_LIMIT = 52 * 1024 * 1024

HG_HEADS = 4
HG_DIM = 128
HG_WIDTH = HG_HEADS * HG_DIM
HG_CHUNK = 64
HG_SUB = 16
MLA_HEADS = 4
MLA_Q_RANK = 256
MLA_KV_RANK = 128
MLA_NOPE = 128
MLA_ROPE = 64
MLA_V = 128
MLA_QK = MLA_NOPE + MLA_ROPE
MLA_QK_PAD = 256
ROPE_BASE = 10000.0
IN_WIDTH = 4 * HG_WIDTH + MLA_Q_RANK + MLA_KV_RANK + MLA_ROPE
IN_PAD = 2560
XA_HEADS = 4
XA_DIM = 256
S5_GROUP = 16
S5_GROUPS = 64
S5_STATE = 64
CONV_W = 3

ADAM_LR = 0.001
ADAM_B1 = 0.9
ADAM_B2 = 0.999
ADAM_EPS = 1e-08
ADAM_WD = 0.01
ADAM_STEP = 10

_NT = (((1,), (1,)), ((), ()))
_TN = (((0,), (0,)), ((), ()))
_NN = (((1,), (0,)), ((), ()))


def _pick(n, cands):
    for c in cands:
        if n % c == 0:
            return c
    return n


def _cparams(sem):
    return pltpu.CompilerParams(dimension_semantics=sem, vmem_limit_bytes=VMEM_LIMIT)


_MM_BUDGET = 36 * 1024 * 1024


def _mm(a, b, *, name, ta=False, tb=False, out_dtype=f32, add=None, b2=None, kslab=None):
    m, k = (a.shape[1], a.shape[0]) if ta else a.shape
    nb = b.shape[0] if tb else b.shape[1]
    n = nb * (2 if b2 is not None else 1)
    slab, nslab = kslab if kslab is not None else (0, 1)
    assert (b.shape[1] // nslab if tb else b.shape[0]) == k, (a.shape, b.shape, ta, tb)
    assert b2 is None or (not tb and b2.shape == b.shape)
    isz = lambda x: jnp.dtype(x.dtype).itemsize
    bm = bn = None
    for cm, cn in ((512, 512), (512, 256), (256, 512), (256, 256), (256, 128), (128, 256), (128, 128)):
        if m % cm or nb % cn:
            continue
        need = 2 * (cm * k * isz(a) + cn * k * isz(b) * (2 if b2 is not None else 1)
                    + cm * cn * (jnp.dtype(out_dtype).itemsize + (4 if add is not None else 0)))
        if need <= _MM_BUDGET:
            bm, bn = cm, cn
            break
    assert bm is not None, (name, a.shape, b.shape)
    half = nb // bn
    dims = (((0 if ta else 1,), (1 if tb else 0,)), ((), ()))

    def body(*refs):
        refs = list(refs)
        a_ref, b_ref = refs[0], refs[1]
        b2_ref = refs.pop(2) if b2 is not None else None
        add_ref = refs[2] if add is not None else None
        o_ref = refs[-1]

        def run(rhs_ref):
            r = lax.dot_general(a_ref[...].astype(bf16), rhs_ref[...].astype(bf16), dims, preferred_element_type=f32)
            if add_ref is not None:
                r = r + add_ref[...].astype(f32)
            o_ref[...] = r.astype(o_ref.dtype)

        if b2_ref is None:
            run(b_ref)
        else:
            pl.when(pl.program_id(1) < half)(lambda: run(b_ref))
            pl.when(pl.program_id(1) >= half)(lambda: run(b2_ref))

    a_spec = pl.BlockSpec((k, bm), lambda i, j: (0, i)) if ta else pl.BlockSpec((bm, k), lambda i, j: (i, 0))
    if tb:
        b_spec = pl.BlockSpec((bn, k), lambda i, j: (j, slab))
    elif b2 is None:
        b_spec = pl.BlockSpec((k, bn), lambda i, j: (0, j))
    else:
        b_spec = pl.BlockSpec((k, bn), lambda i, j: (0, jnp.minimum(j, half - 1)))
    in_specs = [a_spec, b_spec]
    args = [a, b]
    if b2 is not None:
        in_specs.append(pl.BlockSpec((k, bn), lambda i, j: (0, jnp.maximum(j - half, 0))))
        args.append(b2)
    if add is not None:
        in_specs.append(pl.BlockSpec((bm, bn), lambda i, j: (i, j)))
        args.append(add)
    return pl.pallas_call(
        body, grid=(m // bm, n // bn), in_specs=in_specs,
        out_specs=pl.BlockSpec((bm, bn), lambda i, j: (i, j)),
        out_shape=jax.ShapeDtypeStruct((m, n), out_dtype),
        compiler_params=_cparams(("parallel", "parallel")), name=name)(*args)


def _as_tuple(x):
    return tuple(x) if isinstance(x, (tuple, list)) else (x,)


def _full_spec(p):
    nd = p.ndim
    return pl.BlockSpec(p.shape, lambda i, _nd=nd: (0,) * _nd)


def _rows(fn, rows, params, outs, *, name, tile=256, accs=()):
    length = rows[0].shape[0]
    tile = min(tile, length)
    nr, npar, no = len(rows), len(params), len(outs)

    def body(*refs):
        r, p, o = refs[:nr], refs[nr:nr + npar], refs[nr + npar:]
        res = _as_tuple(fn(*[x[...].astype(f32) for x in r], *[x[...] for x in p]))
        for kk in range(no):
            o[kk][...] = res[kk].astype(o[kk].dtype)
        if accs:
            @pl.when(pl.program_id(0) == 0)
            def _():
                for kk in range(no, no + len(accs)):
                    o[kk][...] = jnp.zeros_like(o[kk])
            for kk in range(no, no + len(accs)):
                o[kk][...] += res[kk]

    in_specs = [pl.BlockSpec((tile, x.shape[1]), lambda i: (i, 0)) for x in rows] + [_full_spec(p) for p in params]
    out_specs = [pl.BlockSpec((tile, w), lambda i: (i, 0)) for w, _ in outs]
    out_shape = [jax.ShapeDtypeStruct((length, w), d) for w, d in outs]
    for s in accs:
        out_specs.append(pl.BlockSpec(s, lambda i, _nd=len(s): (0,) * _nd))
        out_shape.append(jax.ShapeDtypeStruct(s, f32))
    res = pl.pallas_call(body, grid=(length // tile,), in_specs=in_specs, out_specs=out_specs, out_shape=out_shape,
                         compiler_params=_cparams(("arbitrary",)), name=name)(*rows, *params)
    return res


def _rows_bwd(fn, rows, params, cts, *, name, rgrad, pgrad, tile=256, addends=None):
    addends = addends or {}
    length = rows[0].shape[0]
    tile = min(tile, length)
    nr, npar, nc = len(rows), len(params), len(cts)
    ridx = [i for i in range(nr) if rgrad[i] is not None]
    pidx = [i for i in range(npar) if pgrad[i]]
    aidx = sorted(addends)
    na = len(aidx)

    def body(*refs):
        r, p, c = refs[:nr], refs[nr:nr + npar], refs[nr + npar:nr + npar + nc]
        ad = refs[nr + npar + nc:nr + npar + nc + na]
        o = refs[nr + npar + nc + na:]
        rv = [x[...].astype(f32) for x in r]
        pv = [x[...] for x in p]
        cv = tuple(x[...].astype(f32) for x in c)

        def g(*d):
            rr, pp = list(rv), list(pv)
            for n_, i_ in enumerate(ridx):
                rr[i_] = d[n_]
            for n_, i_ in enumerate(pidx):
                pp[i_] = d[len(ridx) + n_]
            return _as_tuple(fn(*rr, *pp))

        _, vjp = jax.vjp(g, *[rv[i] for i in ridx], *[pv[i] for i in pidx])
        grads = vjp(cv)
        for n_, i_ in enumerate(ridx):
            val = grads[n_]
            if i_ in addends:
                val = val + ad[aidx.index(i_)][...].astype(f32)
            o[n_][...] = val.astype(o[n_].dtype)
        if pidx:
            @pl.when(pl.program_id(0) == 0)
            def _():
                for n_ in range(len(pidx)):
                    o[len(ridx) + n_][...] = jnp.zeros_like(o[len(ridx) + n_])
            for n_ in range(len(pidx)):
                o[len(ridx) + n_][...] += grads[len(ridx) + n_]

    row_spec = lambda x: pl.BlockSpec((tile, x.shape[1]), lambda i: (i, 0))
    in_specs = ([row_spec(x) for x in rows] + [_full_spec(p) for p in params] + [row_spec(x) for x in cts]
                + [row_spec(addends[i]) for i in aidx])
    out_specs = [row_spec(rows[i]) for i in ridx] + [_full_spec(params[i]) for i in pidx]
    out_shape = ([jax.ShapeDtypeStruct(rows[i].shape, rgrad[i]) for i in ridx]
                 + [jax.ShapeDtypeStruct(params[i].shape, f32) for i in pidx])
    res = pl.pallas_call(body, grid=(length // tile,), in_specs=in_specs, out_specs=out_specs, out_shape=out_shape,
                         compiler_params=_cparams(("arbitrary",)), name=name)(
        *rows, *params, *cts, *[addends[i] for i in aidx])
    return list(res[:len(ridx)]), list(res[len(ridx):])


def _rms(x, g):
    return x * lax.rsqrt(jnp.mean(x * x, axis=-1, keepdims=True) + EPS) * g


def _rms_twice(x, g):
    y = _rms(x, g)
    return y, y


def _silu(x):
    return x * jax.nn.sigmoid(x)


def _shift_down(x, s):
    rows = lax.broadcasted_iota(jnp.int32, x.shape, 0)
    return jnp.where(rows >= s, pltpu.roll(x, s, axis=0), 0.0)


def _shift_up(x, s):
    n = x.shape[0]
    rows = lax.broadcasted_iota(jnp.int32, x.shape, 0)
    return jnp.where(rows < n - s, pltpu.roll(x, n - s, axis=0), 0.0)


_CONV_COLS = 128


def _conv_gate_fwd(u, cw, cb, *, name):
    length, two_f = u.shape
    ff = two_f // 2
    nb = ff // _CONV_COLS

    def body(ug, uv, wg, wv, bg, bv, o):
        def conv(x_ref, w_ref, b_ref):
            x = x_ref[...]
            return (w_ref[2:3, :] * x + w_ref[1:2, :] * _shift_down(x, 1) + w_ref[0:1, :] * _shift_down(x, 2)
                    + b_ref[...])
        o[...] = (_silu(conv(ug, wg, bg)) * conv(uv, wv, bv)).astype(o.dtype)

    blk = lambda r, off: pl.BlockSpec((r, _CONV_COLS), lambda j, _o=off: (0, j + _o))
    return pl.pallas_call(
        body, grid=(nb,),
        in_specs=[blk(length, 0), blk(length, nb), blk(CONV_W, 0), blk(CONV_W, nb), blk(1, 0), blk(1, nb)],
        out_specs=blk(length, 0), out_shape=jax.ShapeDtypeStruct((length, ff), bf16),
        compiler_params=_cparams(("parallel",)), name=name)(u, u, cw, cw, cb, cb)


def _conv_gate_bwd(u, cw, cb, da, *, name):
    length, two_f = u.shape
    ff = two_f // 2
    nb = ff // _CONV_COLS

    def body(ug, uv, wg, wv, bg, bv, da_ref, dug, duv, dwg, dwv, dbg, dbv):
        def conv(x, w_ref, b_ref):
            x1, x2 = _shift_down(x, 1), _shift_down(x, 2)
            return w_ref[2:3, :] * x + w_ref[1:2, :] * x1 + w_ref[0:1, :] * x2 + b_ref[...], x1, x2

        xg, xv = ug[...], uv[...]
        g, xg1, xg2 = conv(xg, wg, bg)
        v, xv1, xv2 = conv(xv, wv, bv)
        d = da_ref[...].astype(f32)
        sg = jax.nn.sigmoid(g)
        dg = d * v * (sg * (1.0 + g * (1.0 - sg)))
        dv = d * (g * sg)

        def back(dy, x, x1, x2, w_ref, du_ref, dw_ref, db_ref):
            du_ref[...] = (w_ref[2:3, :] * dy + w_ref[1:2, :] * _shift_up(dy, 1)
                           + w_ref[0:1, :] * _shift_up(dy, 2)).astype(du_ref.dtype)
            dw_ref[2:3, :] = jnp.sum(dy * x, axis=0, keepdims=True)
            dw_ref[1:2, :] = jnp.sum(dy * x1, axis=0, keepdims=True)
            dw_ref[0:1, :] = jnp.sum(dy * x2, axis=0, keepdims=True)
            db_ref[...] = jnp.sum(dy, axis=0, keepdims=True)

        back(dg, xg, xg1, xg2, wg, dug, dwg, dbg)
        back(dv, xv, xv1, xv2, wv, duv, dwv, dbv)

    blk = lambda r, off: pl.BlockSpec((r, _CONV_COLS), lambda j, _o=off: (0, j + _o))
    sds = jax.ShapeDtypeStruct
    dug, duv, dwg, dwv, dbg, dbv = pl.pallas_call(
        body, grid=(nb,),
        in_specs=[blk(length, 0), blk(length, nb), blk(CONV_W, 0), blk(CONV_W, nb), blk(1, 0), blk(1, nb),
                  blk(length, 0)],
        out_specs=[blk(length, 0), blk(length, 0), blk(CONV_W, 0), blk(CONV_W, 0), blk(1, 0), blk(1, 0)],
        out_shape=[sds((length, ff), bf16), sds((length, ff), bf16), sds((CONV_W, ff), f32), sds((CONV_W, ff), f32),
                   sds((1, ff), f32), sds((1, ff), f32)],
        compiler_params=_cparams(("parallel",)), name=name)(u, u, cw, cw, cb, cb, da)
    return dug, duv, jnp.concatenate([dwg, dwv], axis=1), jnp.concatenate([dbg, dbv], axis=1)


def _ffn_fwd(h, w, tag):
    hf, = _rows(_rms, [h], [w["norm_ffn"]], [(h.shape[1], bf16)], name=f"ffn_norm_{tag}")
    u = _mm(hf, w["ffn_w_up"], name=f"ffn_up_{tag}")
    a = _conv_gate_fwd(u, w["ffn_conv_w"], w["ffn_conv_b"], name=f"ffn_conv_{tag}")
    out = _mm(a, w["ffn_w_down"], add=h, name=f"ffn_down_{tag}")
    return out, (h, hf, u, a)


def _ffn_bwd(dout, w, saved, tag):
    h, hf, u, a = saved
    ff = a.shape[1]
    da = _mm(dout, w["ffn_w_down"], tb=True, out_dtype=bf16, name=f"ffn_da_{tag}")
    g = {"ffn_w_down": _mm(a, dout, ta=True, name=f"ffn_dwdown_{tag}")}
    dug, duv, g["ffn_conv_w"], g["ffn_conv_b"] = _conv_gate_bwd(u, w["ffn_conv_w"], w["ffn_conv_b"], da,
                                                                name=f"ffn_dconv_{tag}")
    dhf = _mm(dug, w["ffn_w_up"], tb=True, kslab=(0, 2), name=f"ffn_dhf_g_{tag}")
    dhf = _mm(duv, w["ffn_w_up"], tb=True, kslab=(1, 2), add=dhf, out_dtype=bf16, name=f"ffn_dhf_v_{tag}")
    g["ffn_w_up"] = _mm(hf, dug, ta=True, b2=duv, name=f"ffn_dwup_{tag}")
    (dh,), (g["norm_ffn"],) = _rows_bwd(_rms, [h], [w["norm_ffn"]], [dhf], rgrad=[f32], pgrad=[True],
                                        addends={0: dout}, name=f"ffn_dnorm_{tag}")
    return dh, g


def _xattn_fn(qx, kx, vx, qg, kg):
    outs = []
    for hh in range(XA_HEADS):
        sl = slice(hh * XA_DIM, (hh + 1) * XA_DIM)
        q = _rms(qx[:, sl], qg).astype(bf16)
        k = _rms(kx[:, sl], kg).astype(bf16)
        s = lax.dot_general(q, k, _NT, preferred_element_type=f32) * (XA_DIM ** -0.5)
        s = s - jnp.max(s, axis=-1, keepdims=True)
        p = jnp.exp(s)
        p = p / jnp.sum(p, axis=-1, keepdims=True)
        outs.append(jnp.dot(p.astype(bf16), vx[:, sl].astype(bf16), preferred_element_type=f32))
    return jnp.concatenate(outs, axis=-1)


def _xattn_fwd(h, mem, w, tag):
    d = h.shape[1]
    hx, = _rows(_rms, [h], [w["norm_xa"]], [(d, bf16)], name=f"xa_norm_{tag}")
    qx = _mm(hx, w["xa_wq"], name=f"xa_q_{tag}")
    m, = _rows(_rms, [mem], [w["norm_mem"]], [(d, bf16)], name=f"xa_mnorm_{tag}")
    kx = _mm(m, w["xa_wk"], name=f"xa_k_{tag}")
    vx = _mm(m, w["xa_wv"], name=f"xa_v_{tag}")
    o, = _rows(_xattn_fn, [qx], [kx, vx, w["xa_q_norm"], w["xa_k_norm"]], [(d, bf16)], name=f"xa_attn_{tag}")
    out = _mm(o, w["xa_wo"], add=h, name=f"xa_o_{tag}")
    return out, (h, hx, qx, m, kx, vx, o)


def _xattn_bwd(dout, mem, w, saved, tag):
    h, hx, qx, m, kx, vx, o = saved
    g = {}
    do = _mm(dout, w["xa_wo"], tb=True, out_dtype=bf16, name=f"xa_do_{tag}")
    g["xa_wo"] = _mm(o, dout, ta=True, name=f"xa_dwo_{tag}")
    (dqx,), (dkx, dvx, g["xa_q_norm"], g["xa_k_norm"]) = _rows_bwd(
        _xattn_fn, [qx], [kx, vx, w["xa_q_norm"], w["xa_k_norm"]], [do], rgrad=[bf16], pgrad=[True] * 4,
        name=f"xa_dattn_{tag}")
    dhx = _mm(dqx, w["xa_wq"], tb=True, out_dtype=bf16, name=f"xa_dhx_{tag}")
    g["xa_wq"] = _mm(hx, dqx, ta=True, name=f"xa_dwq_{tag}")
    (dh,), (g["norm_xa"],) = _rows_bwd(_rms, [h], [w["norm_xa"]], [dhx], rgrad=[f32], pgrad=[True],
                                       addends={0: dout}, name=f"xa_dnorm_{tag}")
    dm = _mm(dkx, w["xa_wk"], tb=True, name=f"xa_dm_k_{tag}")
    dm = _mm(dvx, w["xa_wv"], tb=True, add=dm, name=f"xa_dm_v_{tag}")
    g["xa_wk"] = _mm(m, dkx, ta=True, name=f"xa_dwk_{tag}")
    g["xa_wv"] = _mm(m, dvx, ta=True, name=f"xa_dwv_{tag}")
    _, (g["norm_mem"],) = _rows_bwd(_rms, [mem], [w["norm_mem"]], [dm], rgrad=[None], pgrad=[True],
                                    name=f"xa_dmnorm_{tag}")
    return dh, g


_HG_GROUP = 4


def _hg_chunk(q, k, v, g, st):
    c = q.shape[0]
    tri = (lax.broadcasted_iota(jnp.int32, (c, c), 0) >= lax.broadcasted_iota(jnp.int32, (c, c), 1)).astype(f32)
    b = jnp.dot(tri, g, precision=lax.Precision.HIGHEST, preferred_element_type=f32)
    bend = jnp.sum(g, axis=0, keepdims=True)
    o_inter = lax.dot_general((q * jnp.exp(b)).astype(bf16), st.astype(bf16), _NT, preferred_element_type=f32)
    kd = k * jnp.exp(bend - b)
    st_new = st * jnp.exp(bend) + lax.dot_general(v.astype(bf16), kd.astype(bf16), _TN, preferred_element_type=f32)
    outs = []
    for i in range(c // HG_SUB):
        lo, n = HG_SUB * i, HG_SUB * (i + 1)
        ref = jnp.sum(g[:lo], axis=0, keepdims=True) if i else jnp.zeros((1, g.shape[1]), f32)
        qh = q[lo:n] * jnp.exp(b[lo:n] - ref)
        kh = k[:n] * jnp.exp(ref - b[:n])
        a = lax.dot_general(qh.astype(bf16), kh.astype(bf16), _NT, preferred_element_type=f32)
        keep = (lax.broadcasted_iota(jnp.int32, (HG_SUB, n), 1)
                <= lo + lax.broadcasted_iota(jnp.int32, (HG_SUB, n), 0))
        a = jnp.where(keep, a, 0.0)
        outs.append(jnp.dot(a.astype(bf16), v[:n].astype(bf16), preferred_element_type=f32))
    return jnp.concatenate(outs, axis=0) + o_inter, st_new


def _hg_fwd(q, k, v, g, *, name):
    length = q.shape[0]
    rows = _HG_GROUP * HG_CHUNK
    ng = length // rows
    nc = length // HG_CHUNK

    def body(q_ref, k_ref, v_ref, g_ref, o_ref, st_ref, state):
        @pl.when(pl.program_id(1) == 0)
        def _():
            state[...] = jnp.zeros_like(state)

        for ci in range(_HG_GROUP):
            sl = slice(ci * HG_CHUNK, (ci + 1) * HG_CHUNK)
            st = state[...]
            st_ref[0, ci] = st
            o, st_new = _hg_chunk(q_ref[sl, :], k_ref[sl, :], v_ref[sl, :], g_ref[sl, :], st)
            o_ref[sl, :] = o
            state[...] = st_new

    blk = pl.BlockSpec((rows, HG_DIM), lambda h, c: (c, h))
    return pl.pallas_call(
        body, grid=(HG_HEADS, ng), in_specs=[blk] * 4,
        out_specs=[blk, pl.BlockSpec((1, _HG_GROUP, HG_DIM, HG_DIM), lambda h, c: (h, c, 0, 0))],
        out_shape=[jax.ShapeDtypeStruct((length, HG_WIDTH), f32),
                   jax.ShapeDtypeStruct((HG_HEADS, nc, HG_DIM, HG_DIM), f32)],
        scratch_shapes=[pltpu.VMEM((HG_DIM, HG_DIM), f32)],
        compiler_params=_cparams(("parallel", "arbitrary")), name=name)(q, k, v, g)


def _hg_bwd(q, k, v, g, states, do, *, name):
    length = q.shape[0]
    rows = _HG_GROUP * HG_CHUNK
    ng = length // rows

    def body(q_ref, k_ref, v_ref, g_ref, st_ref, do_ref, dq_ref, dk_ref, dv_ref, dg_ref, dstate):
        @pl.when(pl.program_id(1) == 0)
        def _():
            dstate[...] = jnp.zeros_like(dstate)

        for ci in reversed(range(_HG_GROUP)):
            sl = slice(ci * HG_CHUNK, (ci + 1) * HG_CHUNK)
            _, vjp = jax.vjp(_hg_chunk, q_ref[sl, :], k_ref[sl, :], v_ref[sl, :], g_ref[sl, :], st_ref[0, ci])
            dq, dk, dv, dg, dst = vjp((do_ref[sl, :], dstate[...]))
            dq_ref[sl, :] = dq
            dk_ref[sl, :] = dk
            dv_ref[sl, :] = dv
            dg_ref[sl, :] = dg
            dstate[...] = dst

    blk = pl.BlockSpec((rows, HG_DIM), lambda h, c: (ng - 1 - c, h))
    sds = jax.ShapeDtypeStruct((length, HG_WIDTH), f32)
    return pl.pallas_call(
        body, grid=(HG_HEADS, ng),
        in_specs=[blk] * 4 + [pl.BlockSpec((1, _HG_GROUP, HG_DIM, HG_DIM), lambda h, c: (h, ng - 1 - c, 0, 0)), blk],
        out_specs=[blk] * 4, out_shape=[sds] * 4,
        scratch_shapes=[pltpu.VMEM((HG_DIM, HG_DIM), f32)],
        compiler_params=_cparams(("parallel", "arbitrary")), name=name)(q, k, v, g, states, do)


_ATT_BLK = 256
_ATT_SCALE = MLA_QK ** -0.5
_NEG = -1e30


def _att_mask(i, j, t):
    rows = i * t + lax.broadcasted_iota(jnp.int32, (t, t), 0)
    cols = j * t + lax.broadcasted_iota(jnp.int32, (t, t), 1)
    return cols <= rows


def _att_fwd(q, k, v, *, name):
    length = q.shape[0]
    t = min(_ATT_BLK, length)
    nq = length // t

    def body(q_ref, k_ref, v_ref, o_ref, lse_ref):
        i = pl.program_id(1)
        qb = q_ref[...]

        def step(j, carry):
            m, l, acc = carry
            off = pl.multiple_of(j * t, t)
            ks = k_ref[pl.ds(off, t), :]
            vs = v_ref[pl.ds(off, t), :]
            s = lax.dot_general(qb, ks, _NT, preferred_element_type=f32) * _ATT_SCALE
            s = jnp.where(_att_mask(i, j, t), s, _NEG)
            m_new = jnp.maximum(m, jnp.max(s, axis=-1, keepdims=True))
            alpha = jnp.exp(m - m_new)
            p = jnp.exp(s - m_new)
            l = alpha * l + jnp.sum(p, axis=-1, keepdims=True)
            acc = alpha * acc + jnp.dot(p.astype(bf16), vs, preferred_element_type=f32)
            return m_new, l, acc

        init = (jnp.full((t, 1), _NEG, f32), jnp.zeros((t, 1), f32), jnp.zeros((t, MLA_V), f32))
        m, l, acc = lax.fori_loop(0, i + 1, step, init)
        o_ref[...] = (acc / l).astype(o_ref.dtype)
        lse_ref[...] = jnp.broadcast_to(m + jnp.log(l), lse_ref.shape)

    return pl.pallas_call(
        body, grid=(MLA_HEADS, nq),
        in_specs=[pl.BlockSpec((t, MLA_QK_PAD), lambda h, i: (i, h)),
                  pl.BlockSpec((length, MLA_QK_PAD), lambda h, i: (0, h)),
                  pl.BlockSpec((length, MLA_V), lambda h, i: (0, h))],
        out_specs=[pl.BlockSpec((t, MLA_V), lambda h, i: (i, h))] * 2,
        out_shape=[jax.ShapeDtypeStruct((length, MLA_HEADS * MLA_V), bf16),
                   jax.ShapeDtypeStruct((length, MLA_HEADS * MLA_V), f32)],
        compiler_params=_cparams(("parallel", "arbitrary")), name=name)(q, k, v)


def _att_bwd(q, k, v, o, lse, do, *, name):
    length = q.shape[0]
    t = min(_ATT_BLK, length)
    nq = length // t

    def dq_body(q_ref, k_ref, v_ref, o_ref, lse_ref, do_ref, dq_ref, delta_ref):
        i = pl.program_id(1)
        qb = q_ref[...]
        dob = do_ref[...]
        lse_b = lse_ref[:, 0:1]
        delta = jnp.sum(dob.astype(f32) * o_ref[...].astype(f32), axis=-1, keepdims=True)

        def step(j, dq):
            off = pl.multiple_of(j * t, t)
            ks = k_ref[pl.ds(off, t), :]
            vs = v_ref[pl.ds(off, t), :]
            s = lax.dot_general(qb, ks, _NT, preferred_element_type=f32) * _ATT_SCALE
            p = jnp.where(_att_mask(i, j, t), jnp.exp(s - lse_b), 0.0)
            dp = lax.dot_general(dob, vs, _NT, preferred_element_type=f32)
            ds = p * (dp - delta) * _ATT_SCALE
            return dq + jnp.dot(ds.astype(bf16), ks, preferred_element_type=f32)

        dq = lax.fori_loop(0, i + 1, step, jnp.zeros((t, MLA_QK_PAD), f32))
        dq_ref[...] = dq.astype(dq_ref.dtype)
        delta_ref[...] = jnp.broadcast_to(delta, delta_ref.shape)

    qblk = pl.BlockSpec((t, MLA_QK_PAD), lambda h, i: (i, h))
    vblk = pl.BlockSpec((t, MLA_V), lambda h, i: (i, h))
    qfull = pl.BlockSpec((length, MLA_QK_PAD), lambda h, i: (0, h))
    vfull = pl.BlockSpec((length, MLA_V), lambda h, i: (0, h))
    dq, delta = pl.pallas_call(
        dq_body, grid=(MLA_HEADS, nq), in_specs=[qblk, qfull, vfull, vblk, vblk, vblk],
        out_specs=[qblk, vblk],
        out_shape=[jax.ShapeDtypeStruct(q.shape, bf16), jax.ShapeDtypeStruct(lse.shape, f32)],
        compiler_params=_cparams(("parallel", "arbitrary")), name=name + "_dq")(q, k, v, o, lse, do)

    def dkv_body(k_ref, v_ref, q_ref, do_ref, lse_ref, delta_ref, dk_ref, dv_ref):
        j = pl.program_id(1)
        kb = k_ref[...]
        vb = v_ref[...]

        def step(i, carry):
            dk, dv = carry
            off = pl.multiple_of(i * t, t)
            qs = q_ref[pl.ds(off, t), :]
            dos = do_ref[pl.ds(off, t), :]
            lse_i = lse_ref[pl.ds(off, t), 0:1]
            delta_i = delta_ref[pl.ds(off, t), 0:1]
            s = lax.dot_general(qs, kb, _NT, preferred_element_type=f32) * _ATT_SCALE
            p = jnp.where(_att_mask(i, j, t), jnp.exp(s - lse_i), 0.0)
            dv = dv + lax.dot_general(p.astype(bf16), dos, _TN, preferred_element_type=f32)
            dp = lax.dot_general(dos, vb, _NT, preferred_element_type=f32)
            ds = p * (dp - delta_i) * _ATT_SCALE
            dk = dk + lax.dot_general(ds.astype(bf16), qs, _TN, preferred_element_type=f32)
            return dk, dv

        dk, dv = lax.fori_loop(j, nq, step, (jnp.zeros((t, MLA_QK_PAD), f32), jnp.zeros((t, MLA_V), f32)))
        dk_ref[...] = dk.astype(dk_ref.dtype)
        dv_ref[...] = dv.astype(dv_ref.dtype)

    dk, dv = pl.pallas_call(
        dkv_body, grid=(MLA_HEADS, nq), in_specs=[qblk, vblk, qfull, vfull, vfull, vfull],
        out_specs=[qblk, vblk],
        out_shape=[jax.ShapeDtypeStruct(k.shape, bf16), jax.ShapeDtypeStruct(v.shape, bf16)],
        compiler_params=_cparams(("parallel", "arbitrary")), name=name + "_dkv")(k, v, q, do, lse, delta)
    return dq, dk, dv


_C_Q = 4 * HG_WIDTH
_C_KV = _C_Q + MLA_Q_RANK
_C_KPE = _C_KV + MLA_KV_RANK


def _rms_n(x, g, n):
    return x * lax.rsqrt(jnp.sum(x * x, axis=-1, keepdims=True) * (1.0 / n) + EPS) * g


def _mix_a(proj, l0, l1, q_a_norm, kv_a_norm):
    lb = jax.nn.sigmoid(l0 - l1)
    f = lb + (1.0 - lb) * jax.nn.sigmoid(proj[:, HG_WIDTH:2 * HG_WIDTH])
    qf = _silu(proj[:, :HG_WIDTH])
    v = proj[:, 2 * HG_WIDTH:3 * HG_WIDTH]
    cqn = _rms(proj[:, _C_Q:_C_KV], q_a_norm)
    ckvn = _rms(proj[:, _C_KV:_C_KPE], kv_a_norm)
    return qf, 1.0 - f, v, jnp.log(f), cqn, ckvn


def _mix_b(qraw, kvraw, proj, cos, sin, qn_nope, qn_rope, kn_nope, kn_rope, perm):
    def rope(x):
        return x * cos + jnp.dot(x, perm, precision=lax.Precision.HIGHEST, preferred_element_type=f32) * sin

    kpe = rope(_rms_n(proj[:, _C_KPE:], kn_rope, MLA_ROPE))
    qs, ks, vs = [], [], []
    for hh in range(MLA_HEADS):
        base = hh * MLA_QK_PAD
        qs.append(_rms(qraw[:, base:base + MLA_NOPE], qn_nope))
        qs.append(rope(_rms_n(qraw[:, base + MLA_NOPE:base + MLA_QK_PAD], qn_rope, MLA_ROPE)))
        ks.append(_rms(kvraw[:, base:base + MLA_NOPE], kn_nope))
        ks.append(kpe)
        vs.append(kvraw[:, base + MLA_NOPE:base + MLA_QK_PAD])
    return jnp.concatenate(qs, axis=-1), jnp.concatenate(ks, axis=-1), jnp.concatenate(vs, axis=-1)


def _mix_c(o_hg, proj, o_mla, hg_out_norm):
    parts = []
    for hh in range(HG_HEADS):
        sl = slice(hh * HG_DIM, (hh + 1) * HG_DIM)
        parts.append(_rms(o_hg[:, sl], hg_out_norm[:, sl]))
    o = jnp.concatenate(parts, axis=-1) * _silu(proj[:, 3 * HG_WIDTH:4 * HG_WIDTH])
    return jnp.concatenate([o, o_mla], axis=-1)


def _rope_perm():
    p = np.zeros((128, 128), np.float32)
    half = MLA_ROPE // 2
    for i in range(half):
        p[i + half, i] = -1.0
        p[i, i + half] = 1.0
    return jnp.asarray(p)


def _mixer_fwd(h, cos, sin, w, tag):
    d = h.shape[1]
    hn, = _rows(_rms, [h], [w["norm_mix"]], [(d, bf16)], name=f"mix_norm_{tag}")
    proj = _mm(hn, w["mix_w_in"], name=f"mix_in_{tag}")
    pa = [w["lb0"], w["lb1"], w["mla_q_a_norm"], w["mla_kv_a_norm"]]
    qf, kk, vv, logf, cqn, ckvn = _rows(
        _mix_a, [proj], pa, [(HG_WIDTH, f32)] * 4 + [(MLA_Q_RANK, bf16), (MLA_KV_RANK, bf16)], name=f"mix_a_{tag}")
    o_hg, states = _hg_fwd(qf, kk, vv, logf, name=f"hg_fwd_{tag}")
    qraw = _mm(cqn, w["mla_w_uq"], name=f"mla_uq_{tag}")
    kvraw = _mm(ckvn, w["mla_w_ukv"], name=f"mla_ukv_{tag}")
    pb = [w["mla_qn_nope"], w["mla_qn_rope"], w["mla_kn_nope"], w["mla_kn_rope"], w["rope_perm"]]
    qfull, kfull, vfull = _rows(_mix_b, [qraw, kvraw, proj, cos, sin], pb,
                                [(MLA_HEADS * MLA_QK_PAD, bf16)] * 2 + [(MLA_HEADS * MLA_V, bf16)],
                                name=f"mix_b_{tag}")
    o_mla, lse = _att_fwd(qfull, kfull, vfull, name=f"att_fwd_{tag}")
    mixin, = _rows(_mix_c, [o_hg, proj, o_mla], [w["hg_out_norm"]], [(d, bf16)], name=f"mix_c_{tag}")
    out = _mm(mixin, w["mix_w_out"], add=h, name=f"mix_out_{tag}")
    return out, (h, hn, proj, qf, kk, vv, logf, cqn, ckvn, o_hg, states, qraw, kvraw, qfull, kfull, vfull, o_mla,
                 lse, mixin)


def _mixer_bwd(dout, cos, sin, w, saved, tag):
    (h, hn, proj, qf, kk, vv, logf, cqn, ckvn, o_hg, states, qraw, kvraw, qfull, kfull, vfull, o_mla, lse,
     mixin) = saved
    g = {}
    dmixin = _mm(dout, w["mix_w_out"], tb=True, name=f"mix_dmixin_{tag}")
    g["mix_w_out"] = _mm(mixin, dout, ta=True, name=f"mix_dwout_{tag}")
    (do_hg, dproj_c, do_mla), (g["hg_out_norm"],) = _rows_bwd(
        _mix_c, [o_hg, proj, o_mla], [w["hg_out_norm"]], [dmixin], rgrad=[f32, f32, bf16], pgrad=[True],
        name=f"mix_dc_{tag}")
    dqfull, dkfull, dvfull = _att_bwd(qfull, kfull, vfull, o_mla, lse, do_mla, name=f"att_bwd_{tag}")
    pb = [w["mla_qn_nope"], w["mla_qn_rope"], w["mla_kn_nope"], w["mla_kn_rope"], w["rope_perm"]]
    (dqraw, dkvraw, dproj_b), pg = _rows_bwd(
        _mix_b, [qraw, kvraw, proj, cos, sin], pb, [dqfull, dkfull, dvfull],
        rgrad=[bf16, bf16, f32, None, None], pgrad=[True, True, True, True, False], addends={2: dproj_c},
        name=f"mix_db_{tag}")
    g["mla_qn_nope"], g["mla_qn_rope"], g["mla_kn_nope"], g["mla_kn_rope"] = pg
    dcqn = _mm(dqraw, w["mla_w_uq"], tb=True, name=f"mla_dcq_{tag}")
    g["mla_w_uq"] = _mm(cqn, dqraw, ta=True, name=f"mla_dwuq_{tag}")
    dckvn = _mm(dkvraw, w["mla_w_ukv"], tb=True, name=f"mla_dckv_{tag}")
    g["mla_w_ukv"] = _mm(ckvn, dkvraw, ta=True, name=f"mla_dwukv_{tag}")
    dqf, dkk, dvv, dlogf = _hg_bwd(qf, kk, vv, logf, states, do_hg, name=f"hg_bwd_{tag}")
    pa = [w["lb0"], w["lb1"], w["mla_q_a_norm"], w["mla_kv_a_norm"]]
    (dproj,), (g["lb0"], g["lb1"], g["mla_q_a_norm"], g["mla_kv_a_norm"]) = _rows_bwd(
        _mix_a, [proj], pa, [dqf, dkk, dvv, dlogf, dcqn, dckvn], rgrad=[bf16], pgrad=[True] * 4,
        addends={0: dproj_b}, name=f"mix_da_{tag}")
    dhn = _mm(dproj, w["mix_w_in"], tb=True, out_dtype=bf16, name=f"mix_dhn_{tag}")
    g["mix_w_in"] = _mm(hn, dproj, ta=True, name=f"mix_dwin_{tag}")
    (dh,), (g["norm_mix"],) = _rows_bwd(_rms, [h], [w["norm_mix"]], [dhn], rgrad=[f32], pgrad=[True],
                                        addends={0: dout}, name=f"mix_dnorm_{tag}")
    return dh, g


def _rope_tables(positions):
    inv_freq = 1.0 / (ROPE_BASE ** (jnp.arange(0, MLA_ROPE, 2, dtype=f32) / MLA_ROPE))
    ang = positions.astype(f32)[:, None] * inv_freq
    z = jnp.zeros((positions.shape[0], 128 - MLA_ROPE), f32)
    return (jnp.concatenate([jnp.cos(ang), jnp.cos(ang), z], axis=1),
            jnp.concatenate([jnp.sin(ang), jnp.sin(ang), z], axis=1))


def _pad_cols(a, n):
    return jnp.pad(a, ((0, 0), (0, n - a.shape[1])))


def _even_weights(p, j, layer, dt):
    w_uq = p["mla_w_uq"][j].reshape(MLA_Q_RANK, MLA_HEADS, MLA_QK)
    w_uq = jnp.pad(w_uq, ((0, 0), (0, 0), (0, MLA_QK_PAD - MLA_QK))).reshape(MLA_Q_RANK, MLA_HEADS * MLA_QK_PAD)
    return dict(
        norm_mix=p["norm_mix"][layer][None], mix_w_in=_pad_cols(p["mix_w_in"][j], IN_PAD).astype(dt),
        lb0=p["hg_lb_logits"][0][None], lb1=p["hg_lb_logits"][1][None],
        mla_q_a_norm=p["mla_q_a_norm"][j][None], mla_kv_a_norm=p["mla_kv_a_norm"][j][None],
        mla_w_uq=w_uq.astype(dt), mla_w_ukv=p["mla_w_ukv"][j].astype(dt),
        mla_qn_nope=p["mla_qn_nope"][j][None], mla_qn_rope=_pad_cols(p["mla_qn_rope"][j][None], 128),
        mla_kn_nope=p["mla_kn_nope"][j][None], mla_kn_rope=_pad_cols(p["mla_kn_rope"][j][None], 128),
        rope_perm=_rope_perm(), hg_out_norm=p["hg_out_norm"][j][None], mix_w_out=p["mix_w_out"][j].astype(dt))


def _even_grads(g):
    w_uq = g["mla_w_uq"].reshape(MLA_Q_RANK, MLA_HEADS, MLA_QK_PAD)[:, :, :MLA_QK].reshape(MLA_Q_RANK, -1)
    return dict(
        norm_mix=g["norm_mix"], mix_w_in=g["mix_w_in"][:, :IN_WIDTH][None],
        hg_lb_logits=jnp.concatenate([g["lb0"], g["lb1"]], axis=0),
        mla_q_a_norm=g["mla_q_a_norm"], mla_kv_a_norm=g["mla_kv_a_norm"], mla_w_uq=w_uq[None],
        mla_w_ukv=g["mla_w_ukv"][None], mla_qn_nope=g["mla_qn_nope"], mla_qn_rope=g["mla_qn_rope"][:, :MLA_ROPE],
        mla_kn_nope=g["mla_kn_nope"], mla_kn_rope=g["mla_kn_rope"][:, :MLA_ROPE],
        hg_out_norm=g["hg_out_norm"], mix_w_out=g["mix_w_out"][None])


_S5_NB = 8
_S5_BW = 1024
_S5_HALF = 512
_S5_UC = 128
_S5_TIME = 512


def _bd_mm(a, b3, *, name, tb=False, out_dtype=f32):
    length = a.shape[0]
    rows_b, cols_b = b3.shape[0] // _S5_NB, b3.shape[1]
    ka, n = (cols_b, rows_b) if tb else (rows_b, cols_b)
    bm = _pick(length, (512, 256, 128))
    dims = _NT if tb else _NN

    def body(a_ref, b_ref, o_ref):
        o_ref[...] = lax.dot_general(a_ref[...].astype(bf16), b_ref[...].astype(bf16), dims,
                                     preferred_element_type=f32).astype(o_ref.dtype)

    return pl.pallas_call(
        body, grid=(length // bm, _S5_NB),
        in_specs=[pl.BlockSpec((bm, ka), lambda i, j: (i, j)), pl.BlockSpec((rows_b, cols_b), lambda i, j: (j, 0))],
        out_specs=pl.BlockSpec((bm, n), lambda i, j: (i, j)),
        out_shape=jax.ShapeDtypeStruct((length, _S5_NB * n), out_dtype),
        compiler_params=_cparams(("parallel", "parallel")), name=name)(a, b3)


def _bd_mm_tn(a, c, *, name):
    length = a.shape[0]
    ka, n = a.shape[1] // _S5_NB, c.shape[1] // _S5_NB
    bk = _pick(length, (512, 256, 128))
    nk = length // bk

    def body(a_ref, c_ref, o_ref):
        @pl.when(pl.program_id(1) == 0)
        def _():
            o_ref[...] = jnp.zeros_like(o_ref)
        o_ref[...] += lax.dot_general(a_ref[...].astype(bf16), c_ref[...].astype(bf16), _TN,
                                      preferred_element_type=f32)

    return pl.pallas_call(
        body, grid=(_S5_NB, nk),
        in_specs=[pl.BlockSpec((bk, ka), lambda j, q: (q, j)), pl.BlockSpec((bk, n), lambda j, q: (q, j))],
        out_specs=pl.BlockSpec((ka, n), lambda j, q: (j, 0)),
        out_shape=jax.ShapeDtypeStruct((_S5_NB * ka, n), f32),
        compiler_params=_cparams(("parallel", "arbitrary")), name=name)(a, c)


def _cmul(ar, ai, br, bi):
    return ar * br - ai * bi, ar * bi + ai * br


def _pow_table(ar, ai, descending):
    rows = lax.broadcasted_iota(jnp.int32, (8, ar.shape[1]), 0)
    tr = jnp.zeros((8, ar.shape[1]), f32)
    ti = jnp.zeros((8, ar.shape[1]), f32)
    pr, pi_ = ar, ai
    for r in range(8):
        sel = rows == ((7 - r) if descending else r)
        tr = jnp.where(sel, pr, tr)
        ti = jnp.where(sel, pi_, ti)
        pr, pi_ = _cmul(pr, pi_, ar, ai)
    return tr, ti


def _s5_scan_fwd(a, x, *, name):
    length = x.shape[0]
    tc = min(_S5_TIME, length)
    hw = _S5_HALF

    def body(a_ref, x_ref, o_ref, carry):
        @pl.when(pl.program_id(1) == 0)
        def _():
            carry[...] = jnp.zeros_like(carry)

        ar, ai = a_ref[:, :hw], a_ref[:, hw:]
        xr, xi = x_ref[:, :hw], x_ref[:, hw:]
        row8 = lax.broadcasted_iota(jnp.int32, (tc, hw), 0) & 7
        mr, mi = ar, ai
        for s in (1, 2, 4):
            sr, si = pltpu.roll(xr, s, axis=0), pltpu.roll(xi, s, axis=0)
            pr, pi_ = _cmul(mr, mi, sr, si)
            ok = row8 >= s
            xr = xr + jnp.where(ok, pr, 0.0)
            xi = xi + jnp.where(ok, pi_, 0.0)
            mr, mi = _cmul(mr, mi, mr, mi)
        o_ref[:, :hw] = xr
        o_ref[:, hw:] = xi
        tr, ti = _pow_table(ar, ai, False)
        cr, ci = carry[:, :hw], carry[:, hw:]
        for i in range(tc // 8):
            sl = slice(8 * i, 8 * i + 8)
            pr, pi_ = _cmul(tr, ti, cr, ci)
            o_ref[sl, :hw] = o_ref[sl, :hw] + pr
            o_ref[sl, hw:] = o_ref[sl, hw:] + pi_
            cr, ci = o_ref[8 * i + 7:8 * i + 8, :hw], o_ref[8 * i + 7:8 * i + 8, hw:]
        carry[:, :hw] = cr
        carry[:, hw:] = ci

    return pl.pallas_call(
        body, grid=(_S5_NB, length // tc),
        in_specs=[pl.BlockSpec((1, _S5_BW), lambda j, t: (0, j)), pl.BlockSpec((tc, _S5_BW), lambda j, t: (t, j))],
        out_specs=pl.BlockSpec((tc, _S5_BW), lambda j, t: (t, j)),
        out_shape=jax.ShapeDtypeStruct(x.shape, f32),
        scratch_shapes=[pltpu.VMEM((1, _S5_BW), f32)],
        compiler_params=_cparams(("parallel", "arbitrary")), name=name)(a, x)


def _s5_scan_bwd(a, hs, d, *, name):
    length = d.shape[0]
    tc = min(_S5_TIME, length)
    nt = length // tc
    hw = _S5_HALF

    def body(a_ref, h_ref, d_ref, g_ref, da_ref, carry):
        @pl.when(pl.program_id(1) == 0)
        def _():
            carry[...] = jnp.zeros_like(carry)
            da_ref[...] = jnp.zeros_like(da_ref)

        ar, ai = a_ref[:, :hw], -a_ref[:, hw:]
        xr, xi = d_ref[:, :hw], d_ref[:, hw:]
        rows = lax.broadcasted_iota(jnp.int32, (tc, hw), 0)
        row8 = rows & 7
        mr, mi = ar, ai
        for s in (1, 2, 4):
            sr, si = pltpu.roll(xr, tc - s, axis=0), pltpu.roll(xi, tc - s, axis=0)
            pr, pi_ = _cmul(mr, mi, sr, si)
            ok = row8 < 8 - s
            xr = xr + jnp.where(ok, pr, 0.0)
            xi = xi + jnp.where(ok, pi_, 0.0)
            mr, mi = _cmul(mr, mi, mr, mi)
        g_ref[:, :hw] = xr
        g_ref[:, hw:] = xi
        tr, ti = _pow_table(ar, ai, True)
        cr0, ci0 = carry[:, :hw], carry[:, hw:]
        cr, ci = cr0, ci0
        for i in reversed(range(tc // 8)):
            sl = slice(8 * i, 8 * i + 8)
            pr, pi_ = _cmul(tr, ti, cr, ci)
            g_ref[sl, :hw] = g_ref[sl, :hw] + pr
            g_ref[sl, hw:] = g_ref[sl, hw:] + pi_
            cr, ci = g_ref[8 * i:8 * i + 1, :hw], g_ref[8 * i:8 * i + 1, hw:]
        carry[:, :hw] = cr
        carry[:, hw:] = ci
        last = rows == tc - 1
        gnr = jnp.where(last, cr0, pltpu.roll(g_ref[:, :hw], tc - 1, axis=0))
        gni = jnp.where(last, ci0, pltpu.roll(g_ref[:, hw:], tc - 1, axis=0))
        hr, hi = h_ref[:, :hw], h_ref[:, hw:]
        da_ref[:, :hw] += jnp.sum(hr * gnr + hi * gni, axis=0, keepdims=True)
        da_ref[:, hw:] += jnp.sum(hr * gni - hi * gnr, axis=0, keepdims=True)

    blk = pl.BlockSpec((tc, _S5_BW), lambda j, t: (nt - 1 - t, j))
    row = pl.BlockSpec((1, _S5_BW), lambda j, t: (0, j))
    return pl.pallas_call(
        body, grid=(_S5_NB, nt), in_specs=[row, blk, blk], out_specs=[blk, row],
        out_shape=[jax.ShapeDtypeStruct(d.shape, f32), jax.ShapeDtypeStruct((1, _S5_NB * _S5_BW), f32)],
        scratch_shapes=[pltpu.VMEM((1, _S5_BW), f32)],
        compiler_params=_cparams(("parallel", "arbitrary")), name=name)(a, hs, d)


def _s5_disc(lr, li, ldt, btr, bti, expand):
    dt = jnp.exp(ldt)
    mag = jnp.exp(lr * dt)
    abr = mag * jnp.cos(li * dt)
    abi = mag * jnp.sin(li * dt)
    den = lr * lr + li * li
    zr = ((abr - 1.0) * lr + abi * li) / den
    zi = (abi * lr - (abr - 1.0) * li) / den
    zr = jnp.dot(zr, expand, precision=lax.Precision.HIGHEST, preferred_element_type=f32)
    zi = jnp.dot(zi, expand, precision=lax.Precision.HIGHEST, preferred_element_type=f32)
    return abr, abi, zr * btr - zi * bti, zr * bti + zi * btr


def _s5_disc_fwd(args, *, name):
    def body(*refs):
        res = _s5_disc(*[r[...] for r in refs[:6]])
        for o, v in zip(refs[6:], res):
            o[...] = v

    sds = jax.ShapeDtypeStruct
    return pl.pallas_call(body, out_shape=[sds(args[0].shape, f32)] * 2 + [sds(args[3].shape, f32)] * 2,
                          name=name)(*args)


def _s5_disc_bwd(args, cts, *, name):
    def body(*refs):
        vals = [r[...] for r in refs[:6]]
        _, vjp = jax.vjp(lambda *d: _s5_disc(*d, vals[5]), *vals[:5])
        grads = vjp(tuple(r[...] for r in refs[6:10]))
        for o, v in zip(refs[10:], grads):
            o[...] = v

    return pl.pallas_call(body, out_shape=[jax.ShapeDtypeStruct(a.shape, f32) for a in args[:5]],
                          name=name)(*args, *cts)


def _gelu_tanh(x):
    return 0.5 * x * (1.0 + jnp.tanh(0.7978845608028654 * (x + 0.044715 * (x * x * x))))


def _s5_post(y, u, d_skip):
    return _gelu_tanh(y + d_skip * u)


def _s5_glu(ga, gb, h):
    return h + ga * jax.nn.sigmoid(gb)


def _s5_expand():
    e = np.zeros((S5_STATE, S5_GROUP * S5_STATE), np.float32)
    for m in range(S5_GROUP):
        e[np.arange(S5_STATE), m * S5_STATE + np.arange(S5_STATE)] = 1.0
    return jnp.asarray(e)


def _s5_pack_b(bbr, bbi):
    eye = jnp.eye(8, dtype=f32)

    def one(bb):
        b5 = bb.reshape(_S5_NB, 8, S5_GROUP, S5_STATE)
        return jnp.einsum("jgmp,gh->jgmhp", b5, eye).reshape(_S5_NB * _S5_UC, _S5_HALF)

    return jnp.concatenate([one(bbr), one(bbi)], axis=1)


def _s5_unpack_b(db3):
    def one(d):
        d5 = d.reshape(_S5_NB, 8, S5_GROUP, 8, S5_STATE)
        return jnp.einsum("jgmgp->jgmp", d5).reshape(S5_GROUPS, S5_GROUP * S5_STATE)

    return one(db3[:, :_S5_HALF]), one(db3[:, _S5_HALF:])


def _s5_pack_c(c_re, c_im):
    eye = jnp.eye(8, dtype=f32)

    def one(c):
        c4 = c.reshape(_S5_NB, 8, S5_GROUP, S5_STATE)
        return jnp.einsum("jgmp,hg->jhpgm", c4, eye).reshape(_S5_NB, _S5_HALF, _S5_UC)

    return jnp.concatenate([one(c_re), -one(c_im)], axis=1).reshape(_S5_NB * _S5_BW, _S5_UC)


def _s5_unpack_c(dc3):
    d = dc3.reshape(_S5_NB, 2, 8, S5_STATE, 8, S5_GROUP)
    dre = jnp.einsum("jgpgm->jgmp", d[:, 0]).reshape(S5_GROUPS, S5_GROUP, S5_STATE)
    dim = -jnp.einsum("jgpgm->jgmp", d[:, 1]).reshape(S5_GROUPS, S5_GROUP, S5_STATE)
    return dre, dim


def _s5_state_row(re, im):
    r = re.reshape(_S5_NB, 1, _S5_HALF)
    i = im.reshape(_S5_NB, 1, _S5_HALF)
    return jnp.concatenate([r, i], axis=2).reshape(1, _S5_NB * _S5_BW)


def _s5_unstate_row(row):
    r = row.reshape(_S5_NB, 2, 8, S5_STATE)
    return r[:, 0].reshape(S5_GROUPS, S5_STATE), r[:, 1].reshape(S5_GROUPS, S5_STATE)


def _s5_fwd(h, w, tag):
    d = h.shape[1]
    hn, = _rows(_rms, [h], [w["norm_mix"]], [(d, f32)], name=f"s5_norm_{tag}")
    disc_in = [w["s5_lam_re"], w["s5_lam_im"], w["s5_log_dt"], w["s5_bt_re"], w["s5_bt_im"], w["s5_expand"]]
    abr, abi, bbr, bbi = _s5_disc_fwd(disc_in, name=f"s5_disc_{tag}")
    a_row = _s5_state_row(abr, abi)
    b3 = _s5_pack_b(bbr, bbi)
    bu = _bd_mm(hn, b3, name=f"s5_bu_{tag}")
    hs = _s5_scan_fwd(a_row, bu, name=f"s5_scan_{tag}")
    y = _bd_mm(hs, w["s5_c3"], name=f"s5_y_{tag}")
    yg, = _rows(_s5_post, [y, hn], [w["s5_d"]], [(d, bf16)], name=f"s5_post_{tag}")
    ga = _mm(yg, w["s5_w_glu_a"], name=f"s5_glu_a_{tag}")
    gb = _mm(yg, w["s5_w_glu_b"], name=f"s5_glu_b_{tag}")
    out, = _rows(_s5_glu, [ga, gb, h], [], [(d, f32)], name=f"s5_glu_{tag}")
    return out, (h, hn, disc_in, a_row, b3, hs, y, yg, ga, gb)


def _s5_bwd(dout, w, saved, tag):
    h, hn, disc_in, a_row, b3, hs, y, yg, ga, gb = saved
    g = {}
    (dga, dgb), _ = _rows_bwd(_s5_glu, [ga, gb, h], [], [dout], rgrad=[bf16, bf16, None], pgrad=[],
                              name=f"s5_dglu_{tag}")
    dyg = _mm(dga, w["s5_w_glu_a"], tb=True, name=f"s5_dyg_a_{tag}")
    dyg = _mm(dgb, w["s5_w_glu_b"], tb=True, add=dyg, name=f"s5_dyg_b_{tag}")
    g["s5_w_glu_a"] = _mm(yg, dga, ta=True, name=f"s5_dwa_{tag}")
    g["s5_w_glu_b"] = _mm(yg, dgb, ta=True, name=f"s5_dwb_{tag}")
    (dy, du_skip), (g["s5_d"],) = _rows_bwd(_s5_post, [y, hn], [w["s5_d"]], [dyg], rgrad=[bf16, f32], pgrad=[True],
                                           name=f"s5_dpost_{tag}")
    dhs = _bd_mm(dy, w["s5_c3"], tb=True, name=f"s5_dhs_{tag}")
    dc3 = _bd_mm_tn(hs, dy, name=f"s5_dc_{tag}")
    gs, da_row = _s5_scan_bwd(a_row, hs, dhs, name=f"s5_dscan_{tag}")
    du = _bd_mm(gs, b3, tb=True, name=f"s5_du_{tag}")
    db3 = _bd_mm_tn(hn, gs, name=f"s5_db_{tag}")
    dabr, dabi = _s5_unstate_row(da_row)
    dbbr, dbbi = _s5_unpack_b(db3)
    g["s5_lam_re"], g["s5_lam_im"], g["s5_log_dt"], g["s5_bt_re"], g["s5_bt_im"] = _s5_disc_bwd(
        disc_in, [dabr, dabi, dbbr, dbbi], name=f"s5_ddisc_{tag}")
    g["s5_c_re"], g["s5_c_im"] = _s5_unpack_c(dc3)
    (dh,), (g["norm_mix"],) = _rows_bwd(_rms_twice, [h], [w["norm_mix"]], [du, du_skip], rgrad=[f32], pgrad=[True],
                                        addends={0: dout}, name=f"s5_dnorm_{tag}")
    return dh, g


def _odd_weights(p, j, layer, dt):
    tr = lambda b: b.transpose(0, 2, 1).reshape(S5_GROUPS, S5_GROUP * S5_STATE)
    return dict(
        norm_mix=p["norm_mix"][layer][None], s5_lam_re=p["s5_lam_re"][j], s5_lam_im=p["s5_lam_im"][j],
        s5_log_dt=p["s5_log_dt"][j][:, None], s5_bt_re=tr(p["s5_b_re"][j]), s5_bt_im=tr(p["s5_b_im"][j]),
        s5_expand=_s5_expand(), s5_c3=_s5_pack_c(p["s5_c_re"][j], p["s5_c_im"][j]).astype(dt),
        s5_d=p["s5_d"][j][None], s5_w_glu_a=p["s5_w_glu_a"][j].astype(dt), s5_w_glu_b=p["s5_w_glu_b"][j].astype(dt))


def _odd_grads(g):
    tr = lambda b: b.reshape(S5_GROUPS, S5_GROUP, S5_STATE).transpose(0, 2, 1)[None]
    return dict(
        norm_mix=g["norm_mix"], s5_lam_re=g["s5_lam_re"][None], s5_lam_im=g["s5_lam_im"][None],
        s5_log_dt=g["s5_log_dt"][:, 0][None], s5_b_re=tr(g["s5_bt_re"]), s5_b_im=tr(g["s5_bt_im"]),
        s5_c_re=g["s5_c_re"][None], s5_c_im=g["s5_c_im"][None], s5_d=g["s5_d"],
        s5_w_glu_a=g["s5_w_glu_a"][None], s5_w_glu_b=g["s5_w_glu_b"][None])


def _loss_fn(y, t):
    e = y - t
    part = jnp.sum(jnp.sum(e * e, axis=-1, keepdims=True), axis=0, keepdims=True) * (0.5 / y.shape[1])
    return e * (1.0 / y.shape[1]), part


FF_SHARD = 352
FF_SHARD_PAD = 384


def _pad_groups(a, axis):
    axis %= a.ndim
    s = a.shape
    a = a.reshape(s[:axis] + (s[axis] // FF_SHARD, FF_SHARD) + s[axis + 1:])
    pad = [(0, 0)] * a.ndim
    pad[axis + 1] = (0, FF_SHARD_PAD - FF_SHARD)
    return jnp.pad(a, pad).reshape(s[:axis] + (s[axis] // FF_SHARD * FF_SHARD_PAD,) + s[axis + 1:])


def _unpad_groups(a, axis):
    axis %= a.ndim
    s = a.shape
    a = a.reshape(s[:axis] + (s[axis] // FF_SHARD_PAD, FF_SHARD_PAD) + s[axis + 1:])
    a = lax.slice_in_dim(a, 0, FF_SHARD, axis=axis + 1)
    return a.reshape(s[:axis] + (s[axis] // FF_SHARD_PAD * FF_SHARD,) + s[axis + 1:])


def _layer_weights(p, layer, dt):
    return dict(
        norm_xa=p["norm_xa"][layer][None], norm_mem=p["norm_mem"][layer][None], norm_ffn=p["norm_ffn"][layer][None],
        xa_wq=p["xa_wq"][layer].astype(dt), xa_wk=p["xa_wk"][layer].astype(dt), xa_wv=p["xa_wv"][layer].astype(dt),
        xa_wo=p["xa_wo"][layer].astype(dt), xa_q_norm=p["xa_q_norm"][layer][None],
        xa_k_norm=p["xa_k_norm"][layer][None], ffn_w_up=_pad_groups(p["ffn_w_up"][layer], 1).astype(dt),
        ffn_conv_w=_pad_groups(p["ffn_conv_w"][layer], 1), ffn_conv_b=_pad_groups(p["ffn_conv_b"][layer][None], 1),
        ffn_w_down=_pad_groups(p["ffn_w_down"][layer], 0).astype(dt))


_PER_LAYER = ("norm_xa", "norm_mem", "norm_ffn", "xa_wq", "xa_wk", "xa_wv", "xa_wo", "xa_q_norm", "xa_k_norm",
              "ffn_w_up", "ffn_conv_w", "ffn_conv_b", "ffn_w_down")
_FFN_PADDED = dict(ffn_w_up=1, ffn_conv_w=1, ffn_conv_b=1, ffn_w_down=0)


def _local_step(x, mem, positions, target, p):
    cos, sin = _rope_tables(positions)
    we = _even_weights(p, 0, 0, bf16)
    wo = _odd_weights(p, 0, 1, bf16)
    wl = [_layer_weights(p, layer, bf16) for layer in range(2)]
    loss, dh, g_even, g_odd, gl = _local_core(x, mem, cos, sin, target, we, wo, wl)
    grads = {}
    for n in _PER_LAYER:
        a, b = gl[0][n], gl[1][n]
        if n in _FFN_PADDED:
            a, b = _unpad_groups(a, _FFN_PADDED[n]), _unpad_groups(b, _FFN_PADDED[n])
        grads[n] = jnp.concatenate([a, b], axis=0) if a.shape[0] == 1 else jnp.stack([a, b])
    ge, go = _even_grads(g_even), _odd_grads(g_odd)
    grads["norm_mix"] = jnp.concatenate([ge.pop("norm_mix"), go.pop("norm_mix")], axis=0)
    grads.update(ge)
    grads.update(go)
    return loss, dh, grads


def _local_core(x, mem, cos, sin, target, we, wo, wl):
    h, s_mix0 = _mixer_fwd(x, cos, sin, we, "l0")
    h, s_xa0 = _xattn_fwd(h, mem, wl[0], "l0")
    h, s_ff0 = _ffn_fwd(h, wl[0], "l0")
    h, s_mix1 = _s5_fwd(h, wo, "l1")
    h, s_xa1 = _xattn_fwd(h, mem, wl[1], "l1")
    h, s_ff1 = _ffn_fwd(h, wl[1], "l1")
    dh, loss = _rows(_loss_fn, [h, target], [], [(h.shape[1], f32)], accs=[(1, 1)], name="loss_head")

    gl = [{}, {}]
    dh, g = _ffn_bwd(dh, wl[1], s_ff1, "l1")
    gl[1].update(g)
    dh, g = _xattn_bwd(dh, mem, wl[1], s_xa1, "l1")
    gl[1].update(g)
    dh, g_odd = _s5_bwd(dh, wo, s_mix1, "l1")
    dh, g = _ffn_bwd(dh, wl[0], s_ff0, "l0")
    gl[0].update(g)
    dh, g = _xattn_bwd(dh, mem, wl[0], s_xa0, "l0")
    gl[0].update(g)
    dh, g_even = _mixer_bwd(dh, cos, sin, we, s_mix0, "l0")
    return loss, dh, g_even, g_odd, gl


_LANES = 1024
_ROW_PAD = 256


_PEER_MASKS = (1, 2, 4, 3, 5, 6, 7)


def _mesh_place():
    x, y, c = lax.axis_index("x"), lax.axis_index("y"), lax.axis_index("c")

    def peer(mask):
        px = 1 - x if mask & 4 else x
        py = 1 - y if mask & 2 else y
        pc = 1 - c if mask & 1 else c
        return (px, py, pc), 4 * px + 2 * py + pc

    return 4 * x + 2 * y + c, peer


class _Exchange:
    def __init__(self, name):
        self.name = name
        self.srcs, self.shapes, self.items, self.where = [], [], [], {}

    def add(self, src, land_shape, src_at, dst_at, key):
        si = next((i for i, s in enumerate(self.srcs) if s is src), None)
        if si is None:
            self.srcs.append(src)
            si = len(self.srcs) - 1
        if key not in self.where:
            self.shapes.append(land_shape)
            self.where[key] = len(self.shapes) - 1
        self.items.append(dict(src=si, dst=self.where[key], src_at=src_at, dst_at=dst_at))

    def _copy(self, k, mask, ins, lands, send_sems, recv_sems, me, peer, arriving):
        it = self.items[k]
        dev, idx = peer(mask)
        s = k * (N_DEV - 1) + _PEER_MASKS.index(mask)
        return pltpu.make_async_remote_copy(
            src_ref=it["src_at"](ins[it["src"]], idx), dst_ref=it["dst_at"](lands[it["dst"]], idx if arriving else me),
            send_sem=send_sems.at[s], recv_sem=recv_sems.at[s], device_id=dev, device_id_type=pl.DeviceIdType.MESH)

    def fill(self):
        ns, nd, ni = len(self.srcs), len(self.shapes), len(self.items)

        def body(*refs):
            ins, outs, sems = refs[:ns], refs[ns:ns + nd], refs[ns + nd]
            me, _ = _mesh_place()
            cps = [pltpu.make_async_copy(it["src_at"](ins[it["src"]], me), it["dst_at"](outs[it["dst"]], me), sems.at[k])
                   for k, it in enumerate(self.items)]
            for cp in cps:
                cp.start()
            for cp in cps:
                cp.wait()

        hbm = pl.BlockSpec(memory_space=pltpu.HBM)
        return pl.pallas_call(body, in_specs=[hbm] * ns, out_specs=[hbm] * nd, out_shape=list(self.shapes),
                              scratch_shapes=[pltpu.SemaphoreType.DMA((ni,))], name=self.name + "_fill")(*self.srcs)

    def start(self, lands):
        ns, nd, ni = len(self.srcs), len(self.shapes), len(self.items)
        nsem = ni * (N_DEV - 1)

        def body(*refs):
            ins, land_refs = refs[:ns], refs[ns:ns + nd]
            send_sems, recv_sems = refs[ns + nd], refs[ns + nd + 1]
            me, peer = _mesh_place()
            for mask in _PEER_MASKS:
                for k in range(ni):
                    self._copy(k, mask, ins, land_refs, send_sems, recv_sems, me, peer, False).start()

        hbm = pl.BlockSpec(memory_space=pltpu.HBM)
        sem = pl.BlockSpec(memory_space=pltpu.SEMAPHORE)
        res = pl.pallas_call(
            body, in_specs=[hbm] * (ns + nd), out_specs=[sem, sem] + [hbm] * nd,
            out_shape=[pltpu.SemaphoreType.DMA((nsem,)), pltpu.SemaphoreType.DMA((nsem,))]
            + [pltpu.HBM(s.shape, s.dtype) for s in self.shapes],
            input_output_aliases={ns + j: 2 + j for j in range(nd)},
            compiler_params=pltpu.CompilerParams(has_side_effects=pltpu.SideEffectType.DATAFLOW_SIDE_EFFECTING),
            name=self.name + "_start")(*self.srcs, *lands)
        return res[0], res[1], list(res[2:])

    def wait(self, send_sems, recv_sems, lands, after):
        ns, nd, ni = len(self.srcs), len(self.shapes), len(self.items)

        def body(*refs):
            ins, land_refs = refs[:ns], refs[ns:ns + nd]
            ssem, rsem = refs[ns + nd], refs[ns + nd + 1]
            me, peer = _mesh_place()
            for mask in _PEER_MASKS:
                for k in range(ni):
                    cp = self._copy(k, mask, ins, land_refs, ssem, rsem, me, peer, True)
                    cp.wait_send()
                    cp.wait_recv()

        hbm = pl.BlockSpec(memory_space=pltpu.HBM)
        sem = pl.BlockSpec(memory_space=pltpu.SEMAPHORE)
        res = pl.pallas_call(
            body, in_specs=[hbm] * (ns + nd) + [sem, sem, pl.BlockSpec(memory_space=pl.ANY)],
            out_specs=[hbm] * nd, out_shape=[pltpu.HBM(s.shape, s.dtype) for s in self.shapes],
            input_output_aliases={ns + j: j for j in range(nd)},
            compiler_params=pltpu.CompilerParams(has_side_effects=pltpu.SideEffectType.DATAFLOW_SIDE_EFFECTING),
            name=self.name + "_wait")(*self.srcs, *lands, send_sems, recv_sems, after)
        return {k: res[i] for k, i in self.where.items()}

    def begin(self):
        return self.start(self.fill())

    def finish(self, state, after):
        return self.wait(*state, after)


def _rows_of(n):
    return lambda r, i: r.at[pl.ds(pl.multiple_of(i * n, n), n), :]


def _cols_of(n):
    return lambda r, i: r.at[:, pl.ds(pl.multiple_of(i * n, n), n)]


def _whole(r, i):
    return r


def _slot(r, i):
    return r.at[i]


def _at_layer(layer):
    return lambda r, i: r.at[layer]


def _slot_layer(layer):
    return lambda r, i: r.at[i, layer]


def _sum_adam(slots, w, m, v, *, name):
    rows, cols = w.shape
    tr = _pick(rows, (256, 128, 64, 32, 16, 8))
    bc1 = 1.0 - ADAM_B1 ** ADAM_STEP
    bc2 = 1.0 - ADAM_B2 ** ADAM_STEP

    def body(s_ref, w_ref, m_ref, v_ref, g_ref, d_ref, nm_ref, nv_ref):
        g = s_ref[0]
        for k in range(1, N_DEV):
            g = g + s_ref[k]
        mm = ADAM_B1 * m_ref[...] + (1.0 - ADAM_B1) * g
        vv = ADAM_B2 * v_ref[...] + (1.0 - ADAM_B2) * (g * g)
        g_ref[...] = g
        nm_ref[...] = mm
        nv_ref[...] = vv
        d_ref[...] = -ADAM_LR * ((mm / bc1) / (jnp.sqrt(vv / bc2) + ADAM_EPS) + ADAM_WD * w_ref[...])

    blk = pl.BlockSpec((tr, cols), lambda i: (i, 0))
    sds = jax.ShapeDtypeStruct((rows, cols), f32)
    return pl.pallas_call(
        body, grid=(rows // tr,), in_specs=[pl.BlockSpec((N_DEV, tr, cols), lambda i: (0, i, 0)), blk, blk, blk],
        out_specs=[blk] * 4, out_shape=[sds] * 4, compiler_params=_cparams(("parallel",)), name=name)(slots, w, m, v)


_SHARDED = dict(xa_wq=1, xa_wk=1, xa_wv=1, xa_wo=1, ffn_w_up=2, ffn_conv_w=2, ffn_w_down=1, mix_w_in=2, mla_w_uq=2,
                mla_w_ukv=2, mix_w_out=1, s5_d=1, s5_w_glu_a=1, s5_w_glu_b=1)
_EXACT = ("ffn_conv_w", "s5_d")
_WEIGHTS = ("norm_mix", "norm_xa", "norm_mem", "norm_ffn", "xa_wq", "xa_wk", "xa_wv", "xa_wo", "xa_q_norm",
            "xa_k_norm", "ffn_w_up", "ffn_conv_w", "ffn_conv_b", "ffn_w_down", "hg_lb_logits", "mix_w_in",
            "hg_out_norm", "mla_q_a_norm", "mla_w_uq", "mla_kv_a_norm", "mla_w_ukv", "mla_qn_nope", "mla_qn_rope",
            "mla_kn_nope", "mla_kn_rope", "mix_w_out", "s5_lam_re", "s5_lam_im", "s5_log_dt", "s5_b_re", "s5_b_im",
            "s5_c_re", "s5_c_im", "s5_d", "s5_w_glu_a", "s5_w_glu_b")
_BIG = tuple(n for n in _WEIGHTS if n in _SHARDED and n not in _EXACT)
_SHARD_ORDER = tuple(n for n in _WEIGHTS if n in _SHARDED)
_REPL_ORDER = tuple(n for n in _WEIGHTS if n not in _SHARDED)


def _pack(parts, dtype, lead=None):
    nl = 0 if lead is None else 1
    flat = [a.astype(dtype).reshape(a.shape[:nl] + (-1,)) for a in parts]
    cat = jnp.concatenate(flat, axis=nl)
    n = cat.shape[nl]
    unit = _LANES * _ROW_PAD
    total = -(-n // unit) * unit
    cat = jnp.pad(cat, [(0, 0)] * nl + [(0, total - n)])
    return cat.reshape(cat.shape[:nl] + (total // _LANES, _LANES))


def _unpack(packed, shapes, lead=None):
    nl = 0 if lead is None else 1
    flat = packed.reshape(packed.shape[:nl] + (-1,))
    out, off = [], 0
    for s in shapes:
        n = int(np.prod(s))
        piece = flat[..., off:off + n] if nl else flat[off:off + n]
        out.append(piece.reshape(packed.shape[:nl] + tuple(s)))
        off += n
    return out


def _to_full(gathered, axis):
    g = jnp.moveaxis(gathered, 0, axis)
    s = g.shape
    return g.reshape(s[:axis] + (s[axis] * s[axis + 1],) + s[axis + 2:])


def _to_shards(full, axis):
    s = full.shape
    g = full.reshape(s[:axis] + (N_DEV, s[axis] // N_DEV) + s[axis + 1:])
    return jnp.moveaxis(g, axis, 0)


_DIRECT_ROWS = ("xa_wq", "xa_wk", "xa_wv", "xa_wo", "mix_w_out", "s5_w_glu_a", "s5_w_glu_b")
_SMALL16 = ("mix_w_in", "mla_w_uq", "mla_w_ukv")
_SMALL_SHARDED = _SMALL16 + _EXACT
_SHARD_ROWS = 128


def _exchange_layout(d):
    out = dict(d)
    out["ffn_w_up"] = _pad_groups(d["ffn_w_up"], 2)
    out["ffn_conv_w"] = _pad_groups(d["ffn_conv_w"], 2)
    out["ffn_w_down"] = _pad_groups(d["ffn_w_down"], 1)
    return out


def _train_step(x, mem, positions, target, w, m, v):
    d_model = x.shape[1]
    we_, me_, ve_ = _exchange_layout(w), _exchange_layout(m), _exchange_layout(v)
    sds = jax.ShapeDtypeStruct

    matrices = _DIRECT_ROWS + ("ffn_w_up", "ffn_w_down")
    layer_mats = ("xa_wq", "xa_wk", "xa_wv", "xa_wo", "ffn_w_up", "ffn_w_down")
    shard16 = {n: we_[n].astype(bf16) for n in matrices}
    part_of = {n: _rows_of(_SHARD_ROWS) for n in _DIRECT_ROWS}
    part_of["ffn_w_up"] = _cols_of(we_["ffn_w_up"].shape[2])
    part_of["ffn_w_down"] = _rows_of(we_["ffn_w_down"].shape[1])
    part_shape = {n: we_[n].shape[1:] for n in matrices}

    def full_shape(n):
        r, c = part_shape[n]
        return (r, N_DEV * c) if n == "ffn_w_up" else (N_DEV * r, c)

    def gather(ex, n, layer):
        ex.add(shard16[n], sds(full_shape(n), bf16), _at_layer(layer), part_of[n], (n, layer))

    def scatter(ex, n, layer, grad):
        ex.add(grad, sds((N_DEV,) + part_shape[n], f32), part_of[n], _slot, (n, layer))

    small16 = _pack([we_[n] for n in _SMALL16], bf16)
    exact = _pack([we_[n] for n in _EXACT], f32)
    ga, gb, gc = _Exchange("gather_a"), _Exchange("gather_b"), _Exchange("gather_c")
    ga.add(small16, sds((N_DEV,) + small16.shape, bf16), _whole, _slot, "small16")
    ga.add(exact, sds((N_DEV,) + exact.shape, f32), _whole, _slot, "exact")
    gather(ga, "mix_w_out", 0)
    for n in layer_mats:
        gather(gb, n, 0)
    gather(gc, "s5_w_glu_a", 0)
    gather(gc, "s5_w_glu_b", 0)
    for n in layer_mats:
        gather(gc, n, 1)
    state_a, state_b, state_c = ga.begin(), gb.begin(), gc.begin()

    full = ga.finish(state_a, x)
    p = {n: w[n] for n in _REPL_ORDER}
    for n, a in zip(_SMALL16, _unpack(full["small16"], [we_[n].shape for n in _SMALL16], lead=True)):
        p[n] = _to_full(a, _SHARDED[n])
    conv_w, p["s5_d"] = [_to_full(a, _SHARDED[n]) for n, a in
                         zip(_EXACT, _unpack(full["exact"], [we_[n].shape for n in _EXACT], lead=True))]
    p["mix_w_out"] = full[("mix_w_out", 0)][None]
    cos, sin = _rope_tables(positions)
    we = _even_weights(p, 0, 0, bf16)
    conv_b = _pad_groups(w["ffn_conv_b"], 1)

    def layer_weights(layer):
        return dict(norm_xa=w["norm_xa"][layer][None], norm_mem=w["norm_mem"][layer][None],
                    norm_ffn=w["norm_ffn"][layer][None], xa_q_norm=w["xa_q_norm"][layer][None],
                    xa_k_norm=w["xa_k_norm"][layer][None], ffn_conv_w=conv_w[layer],
                    ffn_conv_b=conv_b[layer][None], **{n: full[(n, layer)] for n in layer_mats})

    h, s_mix0 = _mixer_fwd(x, cos, sin, we, "l0")
    full.update(gb.finish(state_b, h))
    wl = [layer_weights(0)]
    h, s_xa0 = _xattn_fwd(h, mem, wl[0], "l0")
    h, s_ff0 = _ffn_fwd(h, wl[0], "l0")
    full.update(gc.finish(state_c, h))
    wl.append(layer_weights(1))
    p["s5_w_glu_a"], p["s5_w_glu_b"] = full[("s5_w_glu_a", 0)][None], full[("s5_w_glu_b", 0)][None]
    wo = _odd_weights(p, 0, 1, bf16)
    h, s_mix1 = _s5_fwd(h, wo, "l1")
    h, s_xa1 = _xattn_fwd(h, mem, wl[1], "l1")
    h, s_ff1 = _ffn_fwd(h, wl[1], "l1")
    dh, loss = _rows(_loss_fn, [h, target], [], [(h.shape[1], f32)], accs=[(1, 1)], name="loss_head")

    gl = [{}, {}]
    reduces = []

    def reduce_start(name, entries):
        ex = _Exchange(name)
        for n, layer, grad in entries:
            scatter(ex, n, layer, grad)
        reduces.append((ex, ex.begin()))

    dh, gl[1] = _ffn_bwd(dh, wl[1], s_ff1, "l1")
    reduce_start("reduce_ffn1", [(n, 1, gl[1][n]) for n in ("ffn_w_up", "ffn_w_down")])
    dh, g = _xattn_bwd(dh, mem, wl[1], s_xa1, "l1")
    gl[1].update(g)
    reduce_start("reduce_xa1", [(n, 1, g[n]) for n in ("xa_wq", "xa_wk", "xa_wv", "xa_wo")])
    dh, g_odd = _s5_bwd(dh, wo, s_mix1, "l1")
    dh, gl[0] = _ffn_bwd(dh, wl[0], s_ff0, "l0")
    reduce_start("reduce_ffn0", [(n, 0, g_odd[n]) for n in ("s5_w_glu_a", "s5_w_glu_b")]
                 + [(n, 0, gl[0][n]) for n in ("ffn_w_up", "ffn_w_down")])
    dh, g = _xattn_bwd(dh, mem, wl[0], s_xa0, "l0")
    gl[0].update(g)
    reduce_start("reduce_xa0", [(n, 0, g[n]) for n in ("xa_wq", "xa_wk", "xa_wv", "xa_wo")])
    grad_x, g_even = _mixer_bwd(dh, cos, sin, we, s_mix0, "l0")

    ge, go = _even_grads(g_even), _odd_grads(g_odd)
    cat = lambda n: jnp.concatenate([gl[0][n], gl[1][n]], axis=0)
    rg = dict(ge)
    rg.update(go)
    rg["norm_mix"] = jnp.concatenate([ge["norm_mix"], go["norm_mix"]], axis=0)
    for n in ("norm_xa", "norm_mem", "norm_ffn", "xa_q_norm", "xa_k_norm"):
        rg[n] = cat(n)
    rg["ffn_conv_b"] = _unpad_groups(cat("ffn_conv_b"), 1)
    sg = dict(mix_w_in=ge["mix_w_in"], mla_w_uq=ge["mla_w_uq"], mla_w_ukv=ge["mla_w_ukv"], s5_d=go["s5_d"],
              ffn_conv_w=jnp.stack([gl[0]["ffn_conv_w"], gl[1]["ffn_conv_w"]]))

    send_small = _pack([_to_shards(sg[n], _SHARDED[n]) for n in _SMALL_SHARDED], f32, lead=True)
    send_repl = _pack([rg[n].reshape(w[n].shape) for n in _REPL_ORDER], f32)
    last = _Exchange("reduce_mix")
    scatter(last, "mix_w_out", 0, g_even["mix_w_out"])
    last.add(send_small, sds(send_small.shape, f32), _slot, _slot, "small")
    last.add(send_repl, sds((N_DEV,) + send_repl.shape, f32), _whole, _slot, "repl")
    reduces.append((last, last.begin()))
    slots = {}
    for ex, state in reduces:
        slots.update(ex.finish(state, grad_x))

    out = [{}, {}, {}, {}]
    unpad = dict(ffn_w_up=2, ffn_conv_w=2, ffn_w_down=1)
    for n in matrices:
        per_layer = [_sum_adam(slots[(n, layer)], we_[n][layer], me_[n][layer], ve_[n][layer], name=f"adam_{n}_{layer}")
                     for layer in range(we_[n].shape[0])]
        for k in range(4):
            r = jnp.stack([res[k] for res in per_layer])
            out[k][n] = _unpad_groups(r, unpad[n]) if n in unpad else r
    pk = lambda d, order: _pack([d[n] for n in order], f32)
    res_small = _sum_adam(slots["small"], pk(we_, _SMALL_SHARDED), pk(me_, _SMALL_SHARDED), pk(ve_, _SMALL_SHARDED),
                          name="adam_small")
    res_repl = _sum_adam(slots["repl"], pk(w, _REPL_ORDER), pk(m, _REPL_ORDER), pk(v, _REPL_ORDER), name="adam_repl")
    for k in range(4):
        for n, a in zip(_SMALL_SHARDED, _unpack(res_small[k], [we_[n].shape for n in _SMALL_SHARDED])):
            out[k][n] = _unpad_groups(a, unpad[n]) if n in unpad else a
        out[k].update(zip(_REPL_ORDER, _unpack(res_repl[k], [w[n].shape for n in _REPL_ORDER])))
    return loss, grad_x, out


_INPUTS = tuple("""x, mem, positions, norm_mix, norm_xa, norm_mem, norm_ffn, xa_wq, xa_wk, xa_wv, xa_wo, xa_q_norm, xa_k_norm, ffn_w_up, ffn_conv_w, ffn_conv_b, ffn_w_down, hg_lb_logits, mix_w_in, hg_out_norm, mla_q_a_norm, mla_w_uq, mla_kv_a_norm, mla_w_ukv, mla_qn_nope, mla_qn_rope, mla_kn_nope, mla_kn_rope, mix_w_out, s5_lam_re, s5_lam_im, s5_log_dt, s5_b_re, s5_b_im, s5_c_re, s5_c_im, s5_d, s5_w_glu_a, s5_w_glu_b, loss_target, m_norm_mix, m_norm_xa, m_norm_mem, m_norm_ffn, m_xa_wq, m_xa_wk, m_xa_wv, m_xa_wo, m_xa_q_norm, m_xa_k_norm, m_ffn_w_up, m_ffn_conv_w, m_ffn_conv_b, m_ffn_w_down, m_hg_lb_logits, m_mix_w_in, m_hg_out_norm, m_mla_q_a_norm, m_mla_w_uq, m_mla_kv_a_norm, m_mla_w_ukv, m_mla_qn_nope, m_mla_qn_rope, m_mla_kn_nope, m_mla_kn_rope, m_mix_w_out, m_s5_lam_re, m_s5_lam_im, m_s5_log_dt, m_s5_b_re, m_s5_b_im, m_s5_c_re, m_s5_c_im, m_s5_d, m_s5_w_glu_a, m_s5_w_glu_b, v_norm_mix, v_norm_xa, v_norm_mem, v_norm_ffn, v_xa_wq, v_xa_wk, v_xa_wv, v_xa_wo, v_xa_q_norm, v_xa_k_norm, v_ffn_w_up, v_ffn_conv_w, v_ffn_conv_b, v_ffn_w_down, v_hg_lb_logits, v_mix_w_in, v_hg_out_norm, v_mla_q_a_norm, v_mla_w_uq, v_mla_kv_a_norm, v_mla_w_ukv, v_mla_qn_nope, v_mla_qn_rope, v_mla_kn_nope, v_mla_kn_rope, v_mix_w_out, v_s5_lam_re, v_s5_lam_im, v_s5_log_dt, v_s5_b_re, v_s5_b_im, v_s5_c_re, v_s5_c_im, v_s5_d, v_s5_w_glu_a, v_s5_w_glu_b""".replace(" ", "").split(","))


def kernel(x, mem, positions, norm_mix, norm_xa, norm_mem, norm_ffn, xa_wq, xa_wk, xa_wv, xa_wo, xa_q_norm, xa_k_norm, ffn_w_up, ffn_conv_w, ffn_conv_b, ffn_w_down, hg_lb_logits, mix_w_in, hg_out_norm, mla_q_a_norm, mla_w_uq, mla_kv_a_norm, mla_w_ukv, mla_qn_nope, mla_qn_rope, mla_kn_nope, mla_kn_rope, mix_w_out, s5_lam_re, s5_lam_im, s5_log_dt, s5_b_re, s5_b_im, s5_c_re, s5_c_im, s5_d, s5_w_glu_a, s5_w_glu_b, loss_target, m_norm_mix, m_norm_xa, m_norm_mem, m_norm_ffn, m_xa_wq, m_xa_wk, m_xa_wv, m_xa_wo, m_xa_q_norm, m_xa_k_norm, m_ffn_w_up, m_ffn_conv_w, m_ffn_conv_b, m_ffn_w_down, m_hg_lb_logits, m_mix_w_in, m_hg_out_norm, m_mla_q_a_norm, m_mla_w_uq, m_mla_kv_a_norm, m_mla_w_ukv, m_mla_qn_nope, m_mla_qn_rope, m_mla_kn_nope, m_mla_kn_rope, m_mix_w_out, m_s5_lam_re, m_s5_lam_im, m_s5_log_dt, m_s5_b_re, m_s5_b_im, m_s5_c_re, m_s5_c_im, m_s5_d, m_s5_w_glu_a, m_s5_w_glu_b, v_norm_mix, v_norm_xa, v_norm_mem, v_norm_ffn, v_xa_wq, v_xa_wk, v_xa_wv, v_xa_wo, v_xa_q_norm, v_xa_k_norm, v_ffn_w_up, v_ffn_conv_w, v_ffn_conv_b, v_ffn_w_down, v_hg_lb_logits, v_mix_w_in, v_hg_out_norm, v_mla_q_a_norm, v_mla_w_uq, v_mla_kv_a_norm, v_mla_w_ukv, v_mla_qn_nope, v_mla_qn_rope, v_mla_kn_nope, v_mla_kn_rope, v_mix_w_out, v_s5_lam_re, v_s5_lam_im, v_s5_log_dt, v_s5_b_re, v_s5_b_im, v_s5_c_re, v_s5_c_im, v_s5_d, v_s5_w_glu_a, v_s5_w_glu_b):
    vals = dict(zip(_INPUTS, (x, mem, positions, norm_mix, norm_xa, norm_mem, norm_ffn, xa_wq, xa_wk, xa_wv, xa_wo, xa_q_norm, xa_k_norm, ffn_w_up, ffn_conv_w, ffn_conv_b, ffn_w_down, hg_lb_logits, mix_w_in, hg_out_norm, mla_q_a_norm, mla_w_uq, mla_kv_a_norm, mla_w_ukv, mla_qn_nope, mla_qn_rope, mla_kn_nope, mla_kn_rope, mix_w_out, s5_lam_re, s5_lam_im, s5_log_dt, s5_b_re, s5_b_im, s5_c_re, s5_c_im, s5_d, s5_w_glu_a, s5_w_glu_b, loss_target, m_norm_mix, m_norm_xa, m_norm_mem, m_norm_ffn, m_xa_wq, m_xa_wk, m_xa_wv, m_xa_wo, m_xa_q_norm, m_xa_k_norm, m_ffn_w_up, m_ffn_conv_w, m_ffn_conv_b, m_ffn_w_down, m_hg_lb_logits, m_mix_w_in, m_hg_out_norm, m_mla_q_a_norm, m_mla_w_uq, m_mla_kv_a_norm, m_mla_w_ukv, m_mla_qn_nope, m_mla_qn_rope, m_mla_kn_nope, m_mla_kn_rope, m_mix_w_out, m_s5_lam_re, m_s5_lam_im, m_s5_log_dt, m_s5_b_re, m_s5_b_im, m_s5_c_re, m_s5_c_im, m_s5_d, m_s5_w_glu_a, m_s5_w_glu_b, v_norm_mix, v_norm_xa, v_norm_mem, v_norm_ffn, v_xa_wq, v_xa_wk, v_xa_wv, v_xa_wo, v_xa_q_norm, v_xa_k_norm, v_ffn_w_up, v_ffn_conv_w, v_ffn_conv_b, v_ffn_w_down, v_hg_lb_logits, v_mix_w_in, v_hg_out_norm, v_mla_q_a_norm, v_mla_w_uq, v_mla_kv_a_norm, v_mla_w_ukv, v_mla_qn_nope, v_mla_qn_rope, v_mla_kn_nope, v_mla_kn_rope, v_mix_w_out, v_s5_lam_re, v_s5_lam_im, v_s5_log_dt, v_s5_b_re, v_s5_b_im, v_s5_c_re, v_s5_c_im, v_s5_d, v_s5_w_glu_a, v_s5_w_glu_b)))
    w = {n: vals[n] for n in _WEIGHTS}
    m = {n: vals["m_" + n] for n in _WEIGHTS}
    v = {n: vals["v_" + n] for n in _WEIGHTS}
    loss, grad_x, res = _train_step(vals["x"][0], vals["mem"][0], vals["positions"][0], vals["loss_target"][0],
                                    w, m, v)
    loss = lax.psum(loss[0, 0], ("x", "y", "c"))
    return (loss, grad_x[None], *[r[n] for r in res for n in _WEIGHTS])
```

```python
import functools

import jax
import jax.numpy as jnp
import numpy as np
from jax import lax
from jax.experimental import pallas as pl
from jax.experimental.pallas import tpu as pltpu

f32 = jnp.float32
bf16 = jnp.bfloat16

EPS = 1e-6
N_DEV = 8
VMEM_LIMIT = 52 * 1024 * 1024

HG_HEADS = 4
HG_DIM = 128
HG_WIDTH = HG_HEADS * HG_DIM
HG_CHUNK = 64
HG_SUB = 16
MLA_HEADS = 4
MLA_Q_RANK = 256
MLA_KV_RANK = 128
MLA_NOPE = 128
MLA_ROPE = 64
MLA_V = 128
MLA_QK = MLA_NOPE + MLA_ROPE
MLA_QK_PAD = 256
ROPE_BASE = 10000.0
IN_WIDTH = 4 * HG_WIDTH + MLA_Q_RANK + MLA_KV_RANK + MLA_ROPE
IN_PAD = 2560
XA_HEADS = 4
XA_DIM = 256
S5_GROUP = 16
S5_GROUPS = 64
S5_STATE = 64
CONV_W = 3

ADAM_LR = 0.001
ADAM_B1 = 0.9
ADAM_B2 = 0.999
ADAM_EPS = 1e-08
ADAM_WD = 0.01
ADAM_STEP = 10

_NT = (((1,), (1,)), ((), ()))
_TN = (((0,), (0,)), ((), ()))
_NN = (((1,), (0,)), ((), ()))


def _pick(n, cands):
    for c in cands:
        if n % c == 0:
            return c
    return n


def _cparams(sem):
    return pltpu.CompilerParams(dimension_semantics=sem, vmem_limit_bytes=VMEM_LIMIT)


_MM_BUDGET = 36 * 1024 * 1024


def _mm(a, b, *, name, ta=False, tb=False, out_dtype=f32, add=None, b2=None, kslab=None):
    m, k = (a.shape[1], a.shape[0]) if ta else a.shape
    nb = b.shape[0] if tb else b.shape[1]
    n = nb * (2 if b2 is not None else 1)
    slab, nslab = kslab if kslab is not None else (0, 1)
    assert (b.shape[1] // nslab if tb else b.shape[0]) == k, (a.shape, b.shape, ta, tb)
    assert b2 is None or (not tb and b2.shape == b.shape)
    isz = lambda x: jnp.dtype(x.dtype).itemsize
    bm = bn = None
    for cm, cn in ((512, 512), (512, 256), (256, 512), (256, 256), (256, 128), (128, 256), (128, 128)):
        if m % cm or nb % cn:
            continue
        need = 2 * (cm * k * isz(a) + cn * k * isz(b) * (2 if b2 is not None else 1)
                    + cm * cn * (jnp.dtype(out_dtype).itemsize + (4 if add is not None else 0)))
        if need <= _MM_BUDGET:
            bm, bn = cm, cn
            break
    assert bm is not None, (name, a.shape, b.shape)
    half = nb // bn
    dims = (((0 if ta else 1,), (1 if tb else 0,)), ((), ()))

    def body(*refs):
        refs = list(refs)
        a_ref, b_ref = refs[0], refs[1]
        b2_ref = refs.pop(2) if b2 is not None else None
        add_ref = refs[2] if add is not None else None
        o_ref = refs[-1]

        def run(rhs_ref):
            r = lax.dot_general(a_ref[...].astype(bf16), rhs_ref[...].astype(bf16), dims, preferred_element_type=f32)
            if add_ref is not None:
                r = r + add_ref[...].astype(f32)
            o_ref[...] = r.astype(o_ref.dtype)

        if b2_ref is None:
            run(b_ref)
        else:
            pl.when(pl.program_id(1) < half)(lambda: run(b_ref))
            pl.when(pl.program_id(1) >= half)(lambda: run(b2_ref))

    a_spec = pl.BlockSpec((k, bm), lambda i, j: (0, i)) if ta else pl.BlockSpec((bm, k), lambda i, j: (i, 0))
    if tb:
        b_spec = pl.BlockSpec((bn, k), lambda i, j: (j, slab))
    elif b2 is None:
        b_spec = pl.BlockSpec((k, bn), lambda i, j: (0, j))
    else:
        b_spec = pl.BlockSpec((k, bn), lambda i, j: (0, jnp.minimum(j, half - 1)))
    in_specs = [a_spec, b_spec]
    args = [a, b]
    if b2 is not None:
        in_specs.append(pl.BlockSpec((k, bn), lambda i, j: (0, jnp.maximum(j - half, 0))))
        args.append(b2)
    if add is not None:
        in_specs.append(pl.BlockSpec((bm, bn), lambda i, j: (i, j)))
        args.append(add)
    return pl.pallas_call(
        body, grid=(m // bm, n // bn), in_specs=in_specs,
        out_specs=pl.BlockSpec((bm, bn), lambda i, j: (i, j)),
        out_shape=jax.ShapeDtypeStruct((m, n), out_dtype),
        compiler_params=_cparams(("parallel", "parallel")), name=name)(*args)


def _as_tuple(x):
    return tuple(x) if isinstance(x, (tuple, list)) else (x,)


def _full_spec(p):
    nd = p.ndim
    return pl.BlockSpec(p.shape, lambda i, _nd=nd: (0,) * _nd)


def _rows(fn, rows, params, outs, *, name, tile=256, accs=()):
    length = rows[0].shape[0]
    tile = min(tile, length)
    nr, npar, no = len(rows), len(params), len(outs)

    def body(*refs):
        r, p, o = refs[:nr], refs[nr:nr + npar], refs[nr + npar:]
        res = _as_tuple(fn(*[x[...].astype(f32) for x in r], *[x[...] for x in p]))
        for kk in range(no):
            o[kk][...] = res[kk].astype(o[kk].dtype)
        if accs:
            @pl.when(pl.program_id(0) == 0)
            def _():
                for kk in range(no, no + len(accs)):
                    o[kk][...] = jnp.zeros_like(o[kk])
            for kk in range(no, no + len(accs)):
                o[kk][...] += res[kk]

    in_specs = [pl.BlockSpec((tile, x.shape[1]), lambda i: (i, 0)) for x in rows] + [_full_spec(p) for p in params]
    out_specs = [pl.BlockSpec((tile, w), lambda i: (i, 0)) for w, _ in outs]
    out_shape = [jax.ShapeDtypeStruct((length, w), d) for w, d in outs]
    for s in accs:
        out_specs.append(pl.BlockSpec(s, lambda i, _nd=len(s): (0,) * _nd))
        out_shape.append(jax.ShapeDtypeStruct(s, f32))
    res = pl.pallas_call(body, grid=(length // tile,), in_specs=in_specs, out_specs=out_specs, out_shape=out_shape,
                         compiler_params=_cparams(("arbitrary",)), name=name)(*rows, *params)
    return res


def _rows_bwd(fn, rows, params, cts, *, name, rgrad, pgrad, tile=256, addends=None):
    addends = addends or {}
    length = rows[0].shape[0]
    tile = min(tile, length)
    nr, npar, nc = len(rows), len(params), len(cts)
    ridx = [i for i in range(nr) if rgrad[i] is not None]
    pidx = [i for i in range(npar) if pgrad[i]]
    aidx = sorted(addends)
    na = len(aidx)

    def body(*refs):
        r, p, c = refs[:nr], refs[nr:nr + npar], refs[nr + npar:nr + npar + nc]
        ad = refs[nr + npar + nc:nr + npar + nc + na]
        o = refs[nr + npar + nc + na:]
        rv = [x[...].astype(f32) for x in r]
        pv = [x[...] for x in p]
        cv = tuple(x[...].astype(f32) for x in c)

        def g(*d):
            rr, pp = list(rv), list(pv)
            for n_, i_ in enumerate(ridx):
                rr[i_] = d[n_]
            for n_, i_ in enumerate(pidx):
                pp[i_] = d[len(ridx) + n_]
            return _as_tuple(fn(*rr, *pp))

        _, vjp = jax.vjp(g, *[rv[i] for i in ridx], *[pv[i] for i in pidx])
        grads = vjp(cv)
        for n_, i_ in enumerate(ridx):
            val = grads[n_]
            if i_ in addends:
                val = val + ad[aidx.index(i_)][...].astype(f32)
            o[n_][...] = val.astype(o[n_].dtype)
        if pidx:
            @pl.when(pl.program_id(0) == 0)
            def _():
                for n_ in range(len(pidx)):
                    o[len(ridx) + n_][...] = jnp.zeros_like(o[len(ridx) + n_])
            for n_ in range(len(pidx)):
                o[len(ridx) + n_][...] += grads[len(ridx) + n_]

    row_spec = lambda x: pl.BlockSpec((tile, x.shape[1]), lambda i: (i, 0))
    in_specs = ([row_spec(x) for x in rows] + [_full_spec(p) for p in params] + [row_spec(x) for x in cts]
                + [row_spec(addends[i]) for i in aidx])
    out_specs = [row_spec(rows[i]) for i in ridx] + [_full_spec(params[i]) for i in pidx]
    out_shape = ([jax.ShapeDtypeStruct(rows[i].shape, rgrad[i]) for i in ridx]
                 + [jax.ShapeDtypeStruct(params[i].shape, f32) for i in pidx])
    res = pl.pallas_call(body, grid=(length // tile,), in_specs=in_specs, out_specs=out_specs, out_shape=out_shape,
                         compiler_params=_cparams(("arbitrary",)), name=name)(
        *rows, *params, *cts, *[addends[i] for i in aidx])
    return list(res[:len(ridx)]), list(res[len(ridx):])


def _rms(x, g):
    return x * lax.rsqrt(jnp.mean(x * x, axis=-1, keepdims=True) + EPS) * g


def _rms_twice(x, g):
    y = _rms(x, g)
    return y, y


def _silu(x):
    return x * jax.nn.sigmoid(x)


def _shift_down(x, s):
    rows = lax.broadcasted_iota(jnp.int32, x.shape, 0)
    return jnp.where(rows >= s, pltpu.roll(x, s, axis=0), 0.0)


def _shift_up(x, s):
    n = x.shape[0]
    rows = lax.broadcasted_iota(jnp.int32, x.shape, 0)
    return jnp.where(rows < n - s, pltpu.roll(x, n - s, axis=0), 0.0)


_CONV_COLS = 128


def _conv_gate_fwd(u, cw, cb, *, name):
    length, two_f = u.shape
    ff = two_f // 2
    nb = ff // _CONV_COLS

    def body(ug, uv, wg, wv, bg, bv, o):
        def conv(x_ref, w_ref, b_ref):
            x = x_ref[...]
            return (w_ref[2:3, :] * x + w_ref[1:2, :] * _shift_down(x, 1) + w_ref[0:1, :] * _shift_down(x, 2)
                    + b_ref[...])
        o[...] = (_silu(conv(ug, wg, bg)) * conv(uv, wv, bv)).astype(o.dtype)

    blk = lambda r, off: pl.BlockSpec((r, _CONV_COLS), lambda j, _o=off: (0, j + _o))
    return pl.pallas_call(
        body, grid=(nb,),
        in_specs=[blk(length, 0), blk(length, nb), blk(CONV_W, 0), blk(CONV_W, nb), blk(1, 0), blk(1, nb)],
        out_specs=blk(length, 0), out_shape=jax.ShapeDtypeStruct((length, ff), bf16),
        compiler_params=_cparams(("parallel",)), name=name)(u, u, cw, cw, cb, cb)


def _conv_gate_bwd(u, cw, cb, da, *, name):
    length, two_f = u.shape
    ff = two_f // 2
    nb = ff // _CONV_COLS

    def body(ug, uv, wg, wv, bg, bv, da_ref, dug, duv, dwg, dwv, dbg, dbv):
        def conv(x, w_ref, b_ref):
            x1, x2 = _shift_down(x, 1), _shift_down(x, 2)
            return w_ref[2:3, :] * x + w_ref[1:2, :] * x1 + w_ref[0:1, :] * x2 + b_ref[...], x1, x2

        xg, xv = ug[...], uv[...]
        g, xg1, xg2 = conv(xg, wg, bg)
        v, xv1, xv2 = conv(xv, wv, bv)
        d = da_ref[...].astype(f32)
        sg = jax.nn.sigmoid(g)
        dg = d * v * (sg * (1.0 + g * (1.0 - sg)))
        dv = d * (g * sg)

        def back(dy, x, x1, x2, w_ref, du_ref, dw_ref, db_ref):
            du_ref[...] = (w_ref[2:3, :] * dy + w_ref[1:2, :] * _shift_up(dy, 1)
                           + w_ref[0:1, :] * _shift_up(dy, 2)).astype(du_ref.dtype)
            dw_ref[2:3, :] = jnp.sum(dy * x, axis=0, keepdims=True)
            dw_ref[1:2, :] = jnp.sum(dy * x1, axis=0, keepdims=True)
            dw_ref[0:1, :] = jnp.sum(dy * x2, axis=0, keepdims=True)
            db_ref[...] = jnp.sum(dy, axis=0, keepdims=True)

        back(dg, xg, xg1, xg2, wg, dug, dwg, dbg)
        back(dv, xv, xv1, xv2, wv, duv, dwv, dbv)

    blk = lambda r, off: pl.BlockSpec((r, _CONV_COLS), lambda j, _o=off: (0, j + _o))
    sds = jax.ShapeDtypeStruct
    dug, duv, dwg, dwv, dbg, dbv = pl.pallas_call(
        body, grid=(nb,),
        in_specs=[blk(length, 0), blk(length, nb), blk(CONV_W, 0), blk(CONV_W, nb), blk(1, 0), blk(1, nb),
                  blk(length, 0)],
        out_specs=[blk(length, 0), blk(length, 0), blk(CONV_W, 0), blk(CONV_W, 0), blk(1, 0), blk(1, 0)],
        out_shape=[sds((length, ff), bf16), sds((length, ff), bf16), sds((CONV_W, ff), f32), sds((CONV_W, ff), f32),
                   sds((1, ff), f32), sds((1, ff), f32)],
        compiler_params=_cparams(("parallel",)), name=name)(u, u, cw, cw, cb, cb, da)
    return dug, duv, jnp.concatenate([dwg, dwv], axis=1), jnp.concatenate([dbg, dbv], axis=1)


def _ffn_fwd(h, w, tag):
    hf, = _rows(_rms, [h], [w["norm_ffn"]], [(h.shape[1], bf16)], name=f"ffn_norm_{tag}")
    u = _mm(hf, w["ffn_w_up"], name=f"ffn_up_{tag}")
    a = _conv_gate_fwd(u, w["ffn_conv_w"], w["ffn_conv_b"], name=f"ffn_conv_{tag}")
    out = _mm(a, w["ffn_w_down"], add=h, name=f"ffn_down_{tag}")
    return out, (h, hf, u, a)


def _ffn_bwd(dout, w, saved, tag):
    h, hf, u, a = saved
    ff = a.shape[1]
    da = _mm(dout, w["ffn_w_down"], tb=True, out_dtype=bf16, name=f"ffn_da_{tag}")
    g = {"ffn_w_down": _mm(a, dout, ta=True, name=f"ffn_dwdown_{tag}")}
    dug, duv, g["ffn_conv_w"], g["ffn_conv_b"] = _conv_gate_bwd(u, w["ffn_conv_w"], w["ffn_conv_b"], da,
                                                                name=f"ffn_dconv_{tag}")
    dhf = _mm(dug, w["ffn_w_up"], tb=True, kslab=(0, 2), name=f"ffn_dhf_g_{tag}")
    dhf = _mm(duv, w["ffn_w_up"], tb=True, kslab=(1, 2), add=dhf, out_dtype=bf16, name=f"ffn_dhf_v_{tag}")
    g["ffn_w_up"] = _mm(hf, dug, ta=True, b2=duv, name=f"ffn_dwup_{tag}")
    (dh,), (g["norm_ffn"],) = _rows_bwd(_rms, [h], [w["norm_ffn"]], [dhf], rgrad=[f32], pgrad=[True],
                                        addends={0: dout}, name=f"ffn_dnorm_{tag}")
    return dh, g


def _xattn_fn(qx, kx, vx, qg, kg):
    outs = []
    for hh in range(XA_HEADS):
        sl = slice(hh * XA_DIM, (hh + 1) * XA_DIM)
        q = _rms(qx[:, sl], qg).astype(bf16)
        k = _rms(kx[:, sl], kg).astype(bf16)
        s = lax.dot_general(q, k, _NT, preferred_element_type=f32) * (XA_DIM ** -0.5)
        s = s - jnp.max(s, axis=-1, keepdims=True)
        p = jnp.exp(s)
        p = p / jnp.sum(p, axis=-1, keepdims=True)
        outs.append(jnp.dot(p.astype(bf16), vx[:, sl].astype(bf16), preferred_element_type=f32))
    return jnp.concatenate(outs, axis=-1)


def _xattn_fwd(h, mem, w, tag):
    d = h.shape[1]
    hx, = _rows(_rms, [h], [w["norm_xa"]], [(d, bf16)], name=f"xa_norm_{tag}")
    qx = _mm(hx, w["xa_wq"], name=f"xa_q_{tag}")
    m, = _rows(_rms, [mem], [w["norm_mem"]], [(d, bf16)], name=f"xa_mnorm_{tag}")
    kx = _mm(m, w["xa_wk"], name=f"xa_k_{tag}")
    vx = _mm(m, w["xa_wv"], name=f"xa_v_{tag}")
    o, = _rows(_xattn_fn, [qx], [kx, vx, w["xa_q_norm"], w["xa_k_norm"]], [(d, bf16)], name=f"xa_attn_{tag}")
    out = _mm(o, w["xa_wo"], add=h, name=f"xa_o_{tag}")
    return out, (h, hx, qx, m, kx, vx, o)


def _xattn_bwd(dout, mem, w, saved, tag):
    h, hx, qx, m, kx, vx, o = saved
    g = {}
    do = _mm(dout, w["xa_wo"], tb=True, out_dtype=bf16, name=f"xa_do_{tag}")
    g["xa_wo"] = _mm(o, dout, ta=True, name=f"xa_dwo_{tag}")
    (dqx,), (dkx, dvx, g["xa_q_norm"], g["xa_k_norm"]) = _rows_bwd(
        _xattn_fn, [qx], [kx, vx, w["xa_q_norm"], w["xa_k_norm"]], [do], rgrad=[bf16], pgrad=[True] * 4,
        name=f"xa_dattn_{tag}")
    dhx = _mm(dqx, w["xa_wq"], tb=True, out_dtype=bf16, name=f"xa_dhx_{tag}")
    g["xa_wq"] = _mm(hx, dqx, ta=True, name=f"xa_dwq_{tag}")
    (dh,), (g["norm_xa"],) = _rows_bwd(_rms, [h], [w["norm_xa"]], [dhx], rgrad=[f32], pgrad=[True],
                                       addends={0: dout}, name=f"xa_dnorm_{tag}")
    dm = _mm(dkx, w["xa_wk"], tb=True, name=f"xa_dm_k_{tag}")
    dm = _mm(dvx, w["xa_wv"], tb=True, add=dm, name=f"xa_dm_v_{tag}")
    g["xa_wk"] = _mm(m, dkx, ta=True, name=f"xa_dwk_{tag}")
    g["xa_wv"] = _mm(m, dvx, ta=True, name=f"xa_dwv_{tag}")
    _, (g["norm_mem"],) = _rows_bwd(_rms, [mem], [w["norm_mem"]], [dm], rgrad=[None], pgrad=[True],
                                    name=f"xa_dmnorm_{tag}")
    return dh, g


_HG_GROUP = 4


def _hg_chunk(q, k, v, g, st):
    c = q.shape[0]
    tri = (lax.broadcasted_iota(jnp.int32, (c, c), 0) >= lax.broadcasted_iota(jnp.int32, (c, c), 1)).astype(f32)
    b = jnp.dot(tri, g, precision=lax.Precision.HIGHEST, preferred_element_type=f32)
    bend = jnp.sum(g, axis=0, keepdims=True)
    o_inter = lax.dot_general((q * jnp.exp(b)).astype(bf16), st.astype(bf16), _NT, preferred_element_type=f32)
    kd = k * jnp.exp(bend - b)
    st_new = st * jnp.exp(bend) + lax.dot_general(v.astype(bf16), kd.astype(bf16), _TN, preferred_element_type=f32)
    outs = []
    for i in range(c // HG_SUB):
        lo, n = HG_SUB * i, HG_SUB * (i + 1)
        ref = jnp.sum(g[:lo], axis=0, keepdims=True) if i else jnp.zeros((1, g.shape[1]), f32)
        qh = q[lo:n] * jnp.exp(b[lo:n] - ref)
        kh = k[:n] * jnp.exp(ref - b[:n])
        a = lax.dot_general(qh.astype(bf16), kh.astype(bf16), _NT, preferred_element_type=f32)
        keep = (lax.broadcasted_iota(jnp.int32, (HG_SUB, n), 1)
                <= lo + lax.broadcasted_iota(jnp.int32, (HG_SUB, n), 0))
        a = jnp.where(keep, a, 0.0)
        outs.append(jnp.dot(a.astype(bf16), v[:n].astype(bf16), preferred_element_type=f32))
    return jnp.concatenate(outs, axis=0) + o_inter, st_new


def _hg_fwd(q, k, v, g, *, name):
    length = q.shape[0]
    rows = _HG_GROUP * HG_CHUNK
    ng = length // rows
    nc = length // HG_CHUNK

    def body(q_ref, k_ref, v_ref, g_ref, o_ref, st_ref, state):
        @pl.when(pl.program_id(1) == 0)
        def _():
            state[...] = jnp.zeros_like(state)

        for ci in range(_HG_GROUP):
            sl = slice(ci * HG_CHUNK, (ci + 1) * HG_CHUNK)
            st = state[...]
            st_ref[0, ci] = st
            o, st_new = _hg_chunk(q_ref[sl, :], k_ref[sl, :], v_ref[sl, :], g_ref[sl, :], st)
            o_ref[sl, :] = o
            state[...] = st_new

    blk = pl.BlockSpec((rows, HG_DIM), lambda h, c: (c, h))
    return pl.pallas_call(
        body, grid=(HG_HEADS, ng), in_specs=[blk] * 4,
        out_specs=[blk, pl.BlockSpec((1, _HG_GROUP, HG_DIM, HG_DIM), lambda h, c: (h, c, 0, 0))],
        out_shape=[jax.ShapeDtypeStruct((length, HG_WIDTH), f32),
                   jax.ShapeDtypeStruct((HG_HEADS, nc, HG_DIM, HG_DIM), f32)],
        scratch_shapes=[pltpu.VMEM((HG_DIM, HG_DIM), f32)],
        compiler_params=_cparams(("parallel", "arbitrary")), name=name)(q, k, v, g)


def _hg_bwd(q, k, v, g, states, do, *, name):
    length = q.shape[0]
    rows = _HG_GROUP * HG_CHUNK
    ng = length // rows

    def body(q_ref, k_ref, v_ref, g_ref, st_ref, do_ref, dq_ref, dk_ref, dv_ref, dg_ref, dstate):
        @pl.when(pl.program_id(1) == 0)
        def _():
            dstate[...] = jnp.zeros_like(dstate)

        for ci in reversed(range(_HG_GROUP)):
            sl = slice(ci * HG_CHUNK, (ci + 1) * HG_CHUNK)
            _, vjp = jax.vjp(_hg_chunk, q_ref[sl, :], k_ref[sl, :], v_ref[sl, :], g_ref[sl, :], st_ref[0, ci])
            dq, dk, dv, dg, dst = vjp((do_ref[sl, :], dstate[...]))
            dq_ref[sl, :] = dq
            dk_ref[sl, :] = dk
            dv_ref[sl, :] = dv
            dg_ref[sl, :] = dg
            dstate[...] = dst

    blk = pl.BlockSpec((rows, HG_DIM), lambda h, c: (ng - 1 - c, h))
    sds = jax.ShapeDtypeStruct((length, HG_WIDTH), f32)
    return pl.pallas_call(
        body, grid=(HG_HEADS, ng),
        in_specs=[blk] * 4 + [pl.BlockSpec((1, _HG_GROUP, HG_DIM, HG_DIM), lambda h, c: (h, ng - 1 - c, 0, 0)), blk],
        out_specs=[blk] * 4, out_shape=[sds] * 4,
        scratch_shapes=[pltpu.VMEM((HG_DIM, HG_DIM), f32)],
        compiler_params=_cparams(("parallel", "arbitrary")), name=name)(q, k, v, g, states, do)


_ATT_BLK = 256
_ATT_SCALE = MLA_QK ** -0.5
_NEG = -1e30


def _att_mask(i, j, t):
    rows = i * t + lax.broadcasted_iota(jnp.int32, (t, t), 0)
    cols = j * t + lax.broadcasted_iota(jnp.int32, (t, t), 1)
    return cols <= rows


def _att_fwd(q, k, v, *, name):
    length = q.shape[0]
    t = min(_ATT_BLK, length)
    nq = length // t

    def body(q_ref, k_ref, v_ref, o_ref, lse_ref):
        i = pl.program_id(1)
        qb = q_ref[...]

        def step(j, carry):
            m, l, acc = carry
            off = pl.multiple_of(j * t, t)
            ks = k_ref[pl.ds(off, t), :]
            vs = v_ref[pl.ds(off, t), :]
            s = lax.dot_general(qb, ks, _NT, preferred_element_type=f32) * _ATT_SCALE
            s = jnp.where(_att_mask(i, j, t), s, _NEG)
            m_new = jnp.maximum(m, jnp.max(s, axis=-1, keepdims=True))
            alpha = jnp.exp(m - m_new)
            p = jnp.exp(s - m_new)
            l = alpha * l + jnp.sum(p, axis=-1, keepdims=True)
            acc = alpha * acc + jnp.dot(p.astype(bf16), vs, preferred_element_type=f32)
            return m_new, l, acc

        init = (jnp.full((t, 1), _NEG, f32), jnp.zeros((t, 1), f32), jnp.zeros((t, MLA_V), f32))
        m, l, acc = lax.fori_loop(0, i + 1, step, init)
        o_ref[...] = (acc / l).astype(o_ref.dtype)
        lse_ref[...] = jnp.broadcast_to(m + jnp.log(l), lse_ref.shape)

    return pl.pallas_call(
        body, grid=(MLA_HEADS, nq),
        in_specs=[pl.BlockSpec((t, MLA_QK_PAD), lambda h, i: (i, h)),
                  pl.BlockSpec((length, MLA_QK_PAD), lambda h, i: (0, h)),
                  pl.BlockSpec((length, MLA_V), lambda h, i: (0, h))],
        out_specs=[pl.BlockSpec((t, MLA_V), lambda h, i: (i, h))] * 2,
        out_shape=[jax.ShapeDtypeStruct((length, MLA_HEADS * MLA_V), bf16),
                   jax.ShapeDtypeStruct((length, MLA_HEADS * MLA_V), f32)],
        compiler_params=_cparams(("parallel", "arbitrary")), name=name)(q, k, v)


def _att_bwd(q, k, v, o, lse, do, *, name):
    length = q.shape[0]
    t = min(_ATT_BLK, length)
    nq = length // t

    def dq_body(q_ref, k_ref, v_ref, o_ref, lse_ref, do_ref, dq_ref, delta_ref):
        i = pl.program_id(1)
        qb = q_ref[...]
        dob = do_ref[...]
        lse_b = lse_ref[:, 0:1]
        delta = jnp.sum(dob.astype(f32) * o_ref[...].astype(f32), axis=-1, keepdims=True)

        def step(j, dq):
            off = pl.multiple_of(j * t, t)
            ks = k_ref[pl.ds(off, t), :]
            vs = v_ref[pl.ds(off, t), :]
            s = lax.dot_general(qb, ks, _NT, preferred_element_type=f32) * _ATT_SCALE
            p = jnp.where(_att_mask(i, j, t), jnp.exp(s - lse_b), 0.0)
            dp = lax.dot_general(dob, vs, _NT, preferred_element_type=f32)
            ds = p * (dp - delta) * _ATT_SCALE
            return dq + jnp.dot(ds.astype(bf16), ks, preferred_element_type=f32)

        dq = lax.fori_loop(0, i + 1, step, jnp.zeros((t, MLA_QK_PAD), f32))
        dq_ref[...] = dq.astype(dq_ref.dtype)
        delta_ref[...] = jnp.broadcast_to(delta, delta_ref.shape)

    qblk = pl.BlockSpec((t, MLA_QK_PAD), lambda h, i: (i, h))
    vblk = pl.BlockSpec((t, MLA_V), lambda h, i: (i, h))
    qfull = pl.BlockSpec((length, MLA_QK_PAD), lambda h, i: (0, h))
    vfull = pl.BlockSpec((length, MLA_V), lambda h, i: (0, h))
    dq, delta = pl.pallas_call(
        dq_body, grid=(MLA_HEADS, nq), in_specs=[qblk, qfull, vfull, vblk, vblk, vblk],
        out_specs=[qblk, vblk],
        out_shape=[jax.ShapeDtypeStruct(q.shape, bf16), jax.ShapeDtypeStruct(lse.shape, f32)],
        compiler_params=_cparams(("parallel", "arbitrary")), name=name + "_dq")(q, k, v, o, lse, do)

    def dkv_body(k_ref, v_ref, q_ref, do_ref, lse_ref, delta_ref, dk_ref, dv_ref):
        j = pl.program_id(1)
        kb = k_ref[...]
        vb = v_ref[...]

        def step(i, carry):
            dk, dv = carry
            off = pl.multiple_of(i * t, t)
            qs = q_ref[pl.ds(off, t), :]
            dos = do_ref[pl.ds(off, t), :]
            lse_i = lse_ref[pl.ds(off, t), 0:1]
            delta_i = delta_ref[pl.ds(off, t), 0:1]
            s = lax.dot_general(qs, kb, _NT, preferred_element_type=f32) * _ATT_SCALE
            p = jnp.where(_att_mask(i, j, t), jnp.exp(s - lse_i), 0.0)
            dv = dv + lax.dot_general(p.astype(bf16), dos, _TN, preferred_element_type=f32)
            dp = lax.dot_general(dos, vb, _NT, preferred_element_type=f32)
            ds = p * (dp - delta_i) * _ATT_SCALE
            dk = dk + lax.dot_general(ds.astype(bf16), qs, _TN, preferred_element_type=f32)
            return dk, dv

        dk, dv = lax.fori_loop(j, nq, step, (jnp.zeros((t, MLA_QK_PAD), f32), jnp.zeros((t, MLA_V), f32)))
        dk_ref[...] = dk.astype(dk_ref.dtype)
        dv_ref[...] = dv.astype(dv_ref.dtype)

    dk, dv = pl.pallas_call(
        dkv_body, grid=(MLA_HEADS, nq), in_specs=[qblk, vblk, qfull, vfull, vfull, vfull],
        out_specs=[qblk, vblk],
        out_shape=[jax.ShapeDtypeStruct(k.shape, bf16), jax.ShapeDtypeStruct(v.shape, bf16)],
        compiler_params=_cparams(("parallel", "arbitrary")), name=name + "_dkv")(k, v, q, do, lse, delta)
    return dq, dk, dv


_C_Q = 4 * HG_WIDTH
_C_KV = _C_Q + MLA_Q_RANK
_C_KPE = _C_KV + MLA_KV_RANK


def _rms_n(x, g, n):
    return x * lax.rsqrt(jnp.sum(x * x, axis=-1, keepdims=True) * (1.0 / n) + EPS) * g


def _mix_a(proj, l0, l1, q_a_norm, kv_a_norm):
    lb = jax.nn.sigmoid(l0 - l1)
    f = lb + (1.0 - lb) * jax.nn.sigmoid(proj[:, HG_WIDTH:2 * HG_WIDTH])
    qf = _silu(proj[:, :HG_WIDTH])
    v = proj[:, 2 * HG_WIDTH:3 * HG_WIDTH]
    cqn = _rms(proj[:, _C_Q:_C_KV], q_a_norm)
    ckvn = _rms(proj[:, _C_KV:_C_KPE], kv_a_norm)
    return qf, 1.0 - f, v, jnp.log(f), cqn, ckvn


def _mix_b(qraw, kvraw, proj, cos, sin, qn_nope, qn_rope, kn_nope, kn_rope, perm):
    def rope(x):
        return x * cos + jnp.dot(x, perm, precision=lax.Precision.HIGHEST, preferred_element_type=f32) * sin

    kpe = rope(_rms_n(proj[:, _C_KPE:], kn_rope, MLA_ROPE))
    qs, ks, vs = [], [], []
    for hh in range(MLA_HEADS):
        base = hh * MLA_QK_PAD
        qs.append(_rms(qraw[:, base:base + MLA_NOPE], qn_nope))
        qs.append(rope(_rms_n(qraw[:, base + MLA_NOPE:base + MLA_QK_PAD], qn_rope, MLA_ROPE)))
        ks.append(_rms(kvraw[:, base:base + MLA_NOPE], kn_nope))
        ks.append(kpe)
        vs.append(kvraw[:, base + MLA_NOPE:base + MLA_QK_PAD])
    return jnp.concatenate(qs, axis=-1), jnp.concatenate(ks, axis=-1), jnp.concatenate(vs, axis=-1)


def _mix_c(o_hg, proj, o_mla, hg_out_norm):
    parts = []
    for hh in range(HG_HEADS):
        sl = slice(hh * HG_DIM, (hh + 1) * HG_DIM)
        parts.append(_rms(o_hg[:, sl], hg_out_norm[:, sl]))
    o = jnp.concatenate(parts, axis=-1) * _silu(proj[:, 3 * HG_WIDTH:4 * HG_WIDTH])
    return jnp.concatenate([o, o_mla], axis=-1)


def _rope_perm():
    p = np.zeros((128, 128), np.float32)
    half = MLA_ROPE // 2
    for i in range(half):
        p[i + half, i] = -1.0
        p[i, i + half] = 1.0
    return jnp.asarray(p)


def _mixer_fwd(h, cos, sin, w, tag):
    d = h.shape[1]
    hn, = _rows(_rms, [h], [w["norm_mix"]], [(d, bf16)], name=f"mix_norm_{tag}")
    proj = _mm(hn, w["mix_w_in"], name=f"mix_in_{tag}")
    pa = [w["lb0"], w["lb1"], w["mla_q_a_norm"], w["mla_kv_a_norm"]]
    qf, kk, vv, logf, cqn, ckvn = _rows(
        _mix_a, [proj], pa, [(HG_WIDTH, f32)] * 4 + [(MLA_Q_RANK, bf16), (MLA_KV_RANK, bf16)], name=f"mix_a_{tag}")
    o_hg, states = _hg_fwd(qf, kk, vv, logf, name=f"hg_fwd_{tag}")
    qraw = _mm(cqn, w["mla_w_uq"], name=f"mla_uq_{tag}")
    kvraw = _mm(ckvn, w["mla_w_ukv"], name=f"mla_ukv_{tag}")
    pb = [w["mla_qn_nope"], w["mla_qn_rope"], w["mla_kn_nope"], w["mla_kn_rope"], w["rope_perm"]]
    qfull, kfull, vfull = _rows(_mix_b, [qraw, kvraw, proj, cos, sin], pb,
                                [(MLA_HEADS * MLA_QK_PAD, bf16)] * 2 + [(MLA_HEADS * MLA_V, bf16)],
                                name=f"mix_b_{tag}")
    o_mla, lse = _att_fwd(qfull, kfull, vfull, name=f"att_fwd_{tag}")
    mixin, = _rows(_mix_c, [o_hg, proj, o_mla], [w["hg_out_norm"]], [(d, bf16)], name=f"mix_c_{tag}")
    out = _mm(mixin, w["mix_w_out"], add=h, name=f"mix_out_{tag}")
    return out, (h, hn, proj, qf, kk, vv, logf, cqn, ckvn, o_hg, states, qraw, kvraw, qfull, kfull, vfull, o_mla,
                 lse, mixin)


def _mixer_bwd(dout, cos, sin, w, saved, tag):
    (h, hn, proj, qf, kk, vv, logf, cqn, ckvn, o_hg, states, qraw, kvraw, qfull, kfull, vfull, o_mla, lse,
     mixin) = saved
    g = {}
    dmixin = _mm(dout, w["mix_w_out"], tb=True, name=f"mix_dmixin_{tag}")
    g["mix_w_out"] = _mm(mixin, dout, ta=True, name=f"mix_dwout_{tag}")
    (do_hg, dproj_c, do_mla), (g["hg_out_norm"],) = _rows_bwd(
        _mix_c, [o_hg, proj, o_mla], [w["hg_out_norm"]], [dmixin], rgrad=[f32, f32, bf16], pgrad=[True],
        name=f"mix_dc_{tag}")
    dqfull, dkfull, dvfull = _att_bwd(qfull, kfull, vfull, o_mla, lse, do_mla, name=f"att_bwd_{tag}")
    pb = [w["mla_qn_nope"], w["mla_qn_rope"], w["mla_kn_nope"], w["mla_kn_rope"], w["rope_perm"]]
    (dqraw, dkvraw, dproj_b), pg = _rows_bwd(
        _mix_b, [qraw, kvraw, proj, cos, sin], pb, [dqfull, dkfull, dvfull],
        rgrad=[bf16, bf16, f32, None, None], pgrad=[True, True, True, True, False], addends={2: dproj_c},
        name=f"mix_db_{tag}")
    g["mla_qn_nope"], g["mla_qn_rope"], g["mla_kn_nope"], g["mla_kn_rope"] = pg
    dcqn = _mm(dqraw, w["mla_w_uq"], tb=True, name=f"mla_dcq_{tag}")
    g["mla_w_uq"] = _mm(cqn, dqraw, ta=True, name=f"mla_dwuq_{tag}")
    dckvn = _mm(dkvraw, w["mla_w_ukv"], tb=True, name=f"mla_dckv_{tag}")
    g["mla_w_ukv"] = _mm(ckvn, dkvraw, ta=True, name=f"mla_dwukv_{tag}")
    dqf, dkk, dvv, dlogf = _hg_bwd(qf, kk, vv, logf, states, do_hg, name=f"hg_bwd_{tag}")
    pa = [w["lb0"], w["lb1"], w["mla_q_a_norm"], w["mla_kv_a_norm"]]
    (dproj,), (g["lb0"], g["lb1"], g["mla_q_a_norm"], g["mla_kv_a_norm"]) = _rows_bwd(
        _mix_a, [proj], pa, [dqf, dkk, dvv, dlogf, dcqn, dckvn], rgrad=[bf16], pgrad=[True] * 4,
        addends={0: dproj_b}, name=f"mix_da_{tag}")
    dhn = _mm(dproj, w["mix_w_in"], tb=True, out_dtype=bf16, name=f"mix_dhn_{tag}")
    g["mix_w_in"] = _mm(hn, dproj, ta=True, name=f"mix_dwin_{tag}")
    (dh,), (g["norm_mix"],) = _rows_bwd(_rms, [h], [w["norm_mix"]], [dhn], rgrad=[f32], pgrad=[True],
                                        addends={0: dout}, name=f"mix_dnorm_{tag}")
    return dh, g


def _rope_tables(positions):
    inv_freq = 1.0 / (ROPE_BASE ** (jnp.arange(0, MLA_ROPE, 2, dtype=f32) / MLA_ROPE))
    ang = positions.astype(f32)[:, None] * inv_freq
    z = jnp.zeros((positions.shape[0], 128 - MLA_ROPE), f32)
    return (jnp.concatenate([jnp.cos(ang), jnp.cos(ang), z], axis=1),
            jnp.concatenate([jnp.sin(ang), jnp.sin(ang), z], axis=1))


def _pad_cols(a, n):
    return jnp.pad(a, ((0, 0), (0, n - a.shape[1])))


def _even_weights(p, j, layer, dt):
    w_uq = p["mla_w_uq"][j].reshape(MLA_Q_RANK, MLA_HEADS, MLA_QK)
    w_uq = jnp.pad(w_uq, ((0, 0), (0, 0), (0, MLA_QK_PAD - MLA_QK))).reshape(MLA_Q_RANK, MLA_HEADS * MLA_QK_PAD)
    return dict(
        norm_mix=p["norm_mix"][layer][None], mix_w_in=_pad_cols(p["mix_w_in"][j], IN_PAD).astype(dt),
        lb0=p["hg_lb_logits"][0][None], lb1=p["hg_lb_logits"][1][None],
        mla_q_a_norm=p["mla_q_a_norm"][j][None], mla_kv_a_norm=p["mla_kv_a_norm"][j][None],
        mla_w_uq=w_uq.astype(dt), mla_w_ukv=p["mla_w_ukv"][j].astype(dt),
        mla_qn_nope=p["mla_qn_nope"][j][None], mla_qn_rope=_pad_cols(p["mla_qn_rope"][j][None], 128),
        mla_kn_nope=p["mla_kn_nope"][j][None], mla_kn_rope=_pad_cols(p["mla_kn_rope"][j][None], 128),
        rope_perm=_rope_perm(), hg_out_norm=p["hg_out_norm"][j][None], mix_w_out=p["mix_w_out"][j].astype(dt))


def _even_grads(g):
    w_uq = g["mla_w_uq"].reshape(MLA_Q_RANK, MLA_HEADS, MLA_QK_PAD)[:, :, :MLA_QK].reshape(MLA_Q_RANK, -1)
    return dict(
        norm_mix=g["norm_mix"], mix_w_in=g["mix_w_in"][:, :IN_WIDTH][None],
        hg_lb_logits=jnp.concatenate([g["lb0"], g["lb1"]], axis=0),
        mla_q_a_norm=g["mla_q_a_norm"], mla_kv_a_norm=g["mla_kv_a_norm"], mla_w_uq=w_uq[None],
        mla_w_ukv=g["mla_w_ukv"][None], mla_qn_nope=g["mla_qn_nope"], mla_qn_rope=g["mla_qn_rope"][:, :MLA_ROPE],
        mla_kn_nope=g["mla_kn_nope"], mla_kn_rope=g["mla_kn_rope"][:, :MLA_ROPE],
        hg_out_norm=g["hg_out_norm"], mix_w_out=g["mix_w_out"][None])


_S5_NB = 8
_S5_BW = 1024
_S5_HALF = 512
_S5_UC = 128
_S5_TIME = 512


def _bd_mm(a, b3, *, name, tb=False, out_dtype=f32):
    length = a.shape[0]
    rows_b, cols_b = b3.shape[0] // _S5_NB, b3.shape[1]
    ka, n = (cols_b, rows_b) if tb else (rows_b, cols_b)
    bm = _pick(length, (512, 256, 128))
    dims = _NT if tb else _NN

    def body(a_ref, b_ref, o_ref):
        o_ref[...] = lax.dot_general(a_ref[...].astype(bf16), b_ref[...].astype(bf16), dims,
                                     preferred_element_type=f32).astype(o_ref.dtype)

    return pl.pallas_call(
        body, grid=(length // bm, _S5_NB),
        in_specs=[pl.BlockSpec((bm, ka), lambda i, j: (i, j)), pl.BlockSpec((rows_b, cols_b), lambda i, j: (j, 0))],
        out_specs=pl.BlockSpec((bm, n), lambda i, j: (i, j)),
        out_shape=jax.ShapeDtypeStruct((length, _S5_NB * n), out_dtype),
        compiler_params=_cparams(("parallel", "parallel")), name=name)(a, b3)


def _bd_mm_tn(a, c, *, name):
    length = a.shape[0]
    ka, n = a.shape[1] // _S5_NB, c.shape[1] // _S5_NB
    bk = _pick(length, (512, 256, 128))
    nk = length // bk

    def body(a_ref, c_ref, o_ref):
        @pl.when(pl.program_id(1) == 0)
        def _():
            o_ref[...] = jnp.zeros_like(o_ref)
        o_ref[...] += lax.dot_general(a_ref[...].astype(bf16), c_ref[...].astype(bf16), _TN,
                                      preferred_element_type=f32)

    return pl.pallas_call(
        body, grid=(_S5_NB, nk),
        in_specs=[pl.BlockSpec((bk, ka), lambda j, q: (q, j)), pl.BlockSpec((bk, n), lambda j, q: (q, j))],
        out_specs=pl.BlockSpec((ka, n), lambda j, q: (j, 0)),
        out_shape=jax.ShapeDtypeStruct((_S5_NB * ka, n), f32),
        compiler_params=_cparams(("parallel", "arbitrary")), name=name)(a, c)


def _cmul(ar, ai, br, bi):
    return ar * br - ai * bi, ar * bi + ai * br


def _pow_table(ar, ai, descending):
    rows = lax.broadcasted_iota(jnp.int32, (8, ar.shape[1]), 0)
    tr = jnp.zeros((8, ar.shape[1]), f32)
    ti = jnp.zeros((8, ar.shape[1]), f32)
    pr, pi_ = ar, ai
    for r in range(8):
        sel = rows == ((7 - r) if descending else r)
        tr = jnp.where(sel, pr, tr)
        ti = jnp.where(sel, pi_, ti)
        pr, pi_ = _cmul(pr, pi_, ar, ai)
    return tr, ti


def _s5_scan_fwd(a, x, *, name):
    length = x.shape[0]
    tc = min(_S5_TIME, length)
    hw = _S5_HALF

    def body(a_ref, x_ref, o_ref, carry):
        @pl.when(pl.program_id(1) == 0)
        def _():
            carry[...] = jnp.zeros_like(carry)

        ar, ai = a_ref[:, :hw], a_ref[:, hw:]
        xr, xi = x_ref[:, :hw], x_ref[:, hw:]
        row8 = lax.broadcasted_iota(jnp.int32, (tc, hw), 0) & 7
        mr, mi = ar, ai
        for s in (1, 2, 4):
            sr, si = pltpu.roll(xr, s, axis=0), pltpu.roll(xi, s, axis=0)
            pr, pi_ = _cmul(mr, mi, sr, si)
            ok = row8 >= s
            xr = xr + jnp.where(ok, pr, 0.0)
            xi = xi + jnp.where(ok, pi_, 0.0)
            mr, mi = _cmul(mr, mi, mr, mi)
        o_ref[:, :hw] = xr
        o_ref[:, hw:] = xi
        tr, ti = _pow_table(ar, ai, False)
        cr, ci = carry[:, :hw], carry[:, hw:]
        for i in range(tc // 8):
            sl = slice(8 * i, 8 * i + 8)
            pr, pi_ = _cmul(tr, ti, cr, ci)
            o_ref[sl, :hw] = o_ref[sl, :hw] + pr
            o_ref[sl, hw:] = o_ref[sl, hw:] + pi_
            cr, ci = o_ref[8 * i + 7:8 * i + 8, :hw], o_ref[8 * i + 7:8 * i + 8, hw:]
        carry[:, :hw] = cr
        carry[:, hw:] = ci

    return pl.pallas_call(
        body, grid=(_S5_NB, length // tc),
        in_specs=[pl.BlockSpec((1, _S5_BW), lambda j, t: (0, j)), pl.BlockSpec((tc, _S5_BW), lambda j, t: (t, j))],
        out_specs=pl.BlockSpec((tc, _S5_BW), lambda j, t: (t, j)),
        out_shape=jax.ShapeDtypeStruct(x.shape, f32),
        scratch_shapes=[pltpu.VMEM((1, _S5_BW), f32)],
        compiler_params=_cparams(("parallel", "arbitrary")), name=name)(a, x)


def _s5_scan_bwd(a, hs, d, *, name):
    length = d.shape[0]
    tc = min(_S5_TIME, length)
    nt = length // tc
    hw = _S5_HALF

    def body(a_ref, h_ref, d_ref, g_ref, da_ref, carry):
        @pl.when(pl.program_id(1) == 0)
        def _():
            carry[...] = jnp.zeros_like(carry)
            da_ref[...] = jnp.zeros_like(da_ref)

        ar, ai = a_ref[:, :hw], -a_ref[:, hw:]
        xr, xi = d_ref[:, :hw], d_ref[:, hw:]
        rows = lax.broadcasted_iota(jnp.int32, (tc, hw), 0)
        row8 = rows & 7
        mr, mi = ar, ai
        for s in (1, 2, 4):
            sr, si = pltpu.roll(xr, tc - s, axis=0), pltpu.roll(xi, tc - s, axis=0)
            pr, pi_ = _cmul(mr, mi, sr, si)
            ok = row8 < 8 - s
            xr = xr + jnp.where(ok, pr, 0.0)
            xi = xi + jnp.where(ok, pi_, 0.0)
            mr, mi = _cmul(mr, mi, mr, mi)
        g_ref[:, :hw] = xr
        g_ref[:, hw:] = xi
        tr, ti = _pow_table(ar, ai, True)
        cr0, ci0 = carry[:, :hw], carry[:, hw:]
        cr, ci = cr0, ci0
        for i in reversed(range(tc // 8)):
            sl = slice(8 * i, 8 * i + 8)
            pr, pi_ = _cmul(tr, ti, cr, ci)
            g_ref[sl, :hw] = g_ref[sl, :hw] + pr
            g_ref[sl, hw:] = g_ref[sl, hw:] + pi_
            cr, ci = g_ref[8 * i:8 * i + 1, :hw], g_ref[8 * i:8 * i + 1, hw:]
        carry[:, :hw] = cr
        carry[:, hw:] = ci
        last = rows == tc - 1
        gnr = jnp.where(last, cr0, pltpu.roll(g_ref[:, :hw], tc - 1, axis=0))
        gni = jnp.where(last, ci0, pltpu.roll(g_ref[:, hw:], tc - 1, axis=0))
        hr, hi = h_ref[:, :hw], h_ref[:, hw:]
        da_ref[:, :hw] += jnp.sum(hr * gnr + hi * gni, axis=0, keepdims=True)
        da_ref[:, hw:] += jnp.sum(hr * gni - hi * gnr, axis=0, keepdims=True)

    blk = pl.BlockSpec((tc, _S5_BW), lambda j, t: (nt - 1 - t, j))
    row = pl.BlockSpec((1, _S5_BW), lambda j, t: (0, j))
    return pl.pallas_call(
        body, grid=(_S5_NB, nt), in_specs=[row, blk, blk], out_specs=[blk, row],
        out_shape=[jax.ShapeDtypeStruct(d.shape, f32), jax.ShapeDtypeStruct((1, _S5_NB * _S5_BW), f32)],
        scratch_shapes=[pltpu.VMEM((1, _S5_BW), f32)],
        compiler_params=_cparams(("parallel", "arbitrary")), name=name)(a, hs, d)


def _s5_disc(lr, li, ldt, btr, bti, expand):
    dt = jnp.exp(ldt)
    mag = jnp.exp(lr * dt)
    abr = mag * jnp.cos(li * dt)
    abi = mag * jnp.sin(li * dt)
    den = lr * lr + li * li
    zr = ((abr - 1.0) * lr + abi * li) / den
    zi = (abi * lr - (abr - 1.0) * li) / den
    zr = jnp.dot(zr, expand, precision=lax.Precision.HIGHEST, preferred_element_type=f32)
    zi = jnp.dot(zi, expand, precision=lax.Precision.HIGHEST, preferred_element_type=f32)
    return abr, abi, zr * btr - zi * bti, zr * bti + zi * btr


def _s5_disc_fwd(args, *, name):
    def body(*refs):
        res = _s5_disc(*[r[...] for r in refs[:6]])
        for o, v in zip(refs[6:], res):
            o[...] = v

    sds = jax.ShapeDtypeStruct
    return pl.pallas_call(body, out_shape=[sds(args[0].shape, f32)] * 2 + [sds(args[3].shape, f32)] * 2,
                          name=name)(*args)


def _s5_disc_bwd(args, cts, *, name):
    def body(*refs):
        vals = [r[...] for r in refs[:6]]
        _, vjp = jax.vjp(lambda *d: _s5_disc(*d, vals[5]), *vals[:5])
        grads = vjp(tuple(r[...] for r in refs[6:10]))
        for o, v in zip(refs[10:], grads):
            o[...] = v

    return pl.pallas_call(body, out_shape=[jax.ShapeDtypeStruct(a.shape, f32) for a in args[:5]],
                          name=name)(*args, *cts)


def _gelu_tanh(x):
    return 0.5 * x * (1.0 + jnp.tanh(0.7978845608028654 * (x + 0.044715 * (x * x * x))))


def _s5_post(y, u, d_skip):
    return _gelu_tanh(y + d_skip * u)


def _s5_glu(ga, gb, h):
    return h + ga * jax.nn.sigmoid(gb)


def _s5_expand():
    e = np.zeros((S5_STATE, S5_GROUP * S5_STATE), np.float32)
    for m in range(S5_GROUP):
        e[np.arange(S5_STATE), m * S5_STATE + np.arange(S5_STATE)] = 1.0
    return jnp.asarray(e)


def _s5_pack_b(bbr, bbi):
    eye = jnp.eye(8, dtype=f32)

    def one(bb):
        b5 = bb.reshape(_S5_NB, 8, S5_GROUP, S5_STATE)
        return jnp.einsum("jgmp,gh->jgmhp", b5, eye).reshape(_S5_NB * _S5_UC, _S5_HALF)

    return jnp.concatenate([one(bbr), one(bbi)], axis=1)


def _s5_unpack_b(db3):
    def one(d):
        d5 = d.reshape(_S5_NB, 8, S5_GROUP, 8, S5_STATE)
        return jnp.einsum("jgmgp->jgmp", d5).reshape(S5_GROUPS, S5_GROUP * S5_STATE)

    return one(db3[:, :_S5_HALF]), one(db3[:, _S5_HALF:])


def _s5_pack_c(c_re, c_im):
    eye = jnp.eye(8, dtype=f32)

    def one(c):
        c4 = c.reshape(_S5_NB, 8, S5_GROUP, S5_STATE)
        return jnp.einsum("jgmp,hg->jhpgm", c4, eye).reshape(_S5_NB, _S5_HALF, _S5_UC)

    return jnp.concatenate([one(c_re), -one(c_im)], axis=1).reshape(_S5_NB * _S5_BW, _S5_UC)


def _s5_unpack_c(dc3):
    d = dc3.reshape(_S5_NB, 2, 8, S5_STATE, 8, S5_GROUP)
    dre = jnp.einsum("jgpgm->jgmp", d[:, 0]).reshape(S5_GROUPS, S5_GROUP, S5_STATE)
    dim = -jnp.einsum("jgpgm->jgmp", d[:, 1]).reshape(S5_GROUPS, S5_GROUP, S5_STATE)
    return dre, dim


def _s5_state_row(re, im):
    r = re.reshape(_S5_NB, 1, _S5_HALF)
    i = im.reshape(_S5_NB, 1, _S5_HALF)
    return jnp.concatenate([r, i], axis=2).reshape(1, _S5_NB * _S5_BW)


def _s5_unstate_row(row):
    r = row.reshape(_S5_NB, 2, 8, S5_STATE)
    return r[:, 0].reshape(S5_GROUPS, S5_STATE), r[:, 1].reshape(S5_GROUPS, S5_STATE)


def _s5_fwd(h, w, tag):
    d = h.shape[1]
    hn, = _rows(_rms, [h], [w["norm_mix"]], [(d, f32)], name=f"s5_norm_{tag}")
    disc_in = [w["s5_lam_re"], w["s5_lam_im"], w["s5_log_dt"], w["s5_bt_re"], w["s5_bt_im"], w["s5_expand"]]
    abr, abi, bbr, bbi = _s5_disc_fwd(disc_in, name=f"s5_disc_{tag}")
    a_row = _s5_state_row(abr, abi)
    b3 = _s5_pack_b(bbr, bbi)
    bu = _bd_mm(hn, b3, name=f"s5_bu_{tag}")
    hs = _s5_scan_fwd(a_row, bu, name=f"s5_scan_{tag}")
    y = _bd_mm(hs, w["s5_c3"], name=f"s5_y_{tag}")
    yg, = _rows(_s5_post, [y, hn], [w["s5_d"]], [(d, bf16)], name=f"s5_post_{tag}")
    ga = _mm(yg, w["s5_w_glu_a"], name=f"s5_glu_a_{tag}")
    gb = _mm(yg, w["s5_w_glu_b"], name=f"s5_glu_b_{tag}")
    out, = _rows(_s5_glu, [ga, gb, h], [], [(d, f32)], name=f"s5_glu_{tag}")
    return out, (h, hn, disc_in, a_row, b3, hs, y, yg, ga, gb)


def _s5_bwd(dout, w, saved, tag):
    h, hn, disc_in, a_row, b3, hs, y, yg, ga, gb = saved
    g = {}
    (dga, dgb), _ = _rows_bwd(_s5_glu, [ga, gb, h], [], [dout], rgrad=[bf16, bf16, None], pgrad=[],
                              name=f"s5_dglu_{tag}")
    dyg = _mm(dga, w["s5_w_glu_a"], tb=True, name=f"s5_dyg_a_{tag}")
    dyg = _mm(dgb, w["s5_w_glu_b"], tb=True, add=dyg, name=f"s5_dyg_b_{tag}")
    g["s5_w_glu_a"] = _mm(yg, dga, ta=True, name=f"s5_dwa_{tag}")
    g["s5_w_glu_b"] = _mm(yg, dgb, ta=True, name=f"s5_dwb_{tag}")
    (dy, du_skip), (g["s5_d"],) = _rows_bwd(_s5_post, [y, hn], [w["s5_d"]], [dyg], rgrad=[bf16, f32], pgrad=[True],
                                           name=f"s5_dpost_{tag}")
    dhs = _bd_mm(dy, w["s5_c3"], tb=True, name=f"s5_dhs_{tag}")
    dc3 = _bd_mm_tn(hs, dy, name=f"s5_dc_{tag}")
    gs, da_row = _s5_scan_bwd(a_row, hs, dhs, name=f"s5_dscan_{tag}")
    du = _bd_mm(gs, b3, tb=True, name=f"s5_du_{tag}")
    db3 = _bd_mm_tn(hn, gs, name=f"s5_db_{tag}")
    dabr, dabi = _s5_unstate_row(da_row)
    dbbr, dbbi = _s5_unpack_b(db3)
    g["s5_lam_re"], g["s5_lam_im"], g["s5_log_dt"], g["s5_bt_re"], g["s5_bt_im"] = _s5_disc_bwd(
        disc_in, [dabr, dabi, dbbr, dbbi], name=f"s5_ddisc_{tag}")
    g["s5_c_re"], g["s5_c_im"] = _s5_unpack_c(dc3)
    (dh,), (g["norm_mix"],) = _rows_bwd(_rms_twice, [h], [w["norm_mix"]], [du, du_skip], rgrad=[f32], pgrad=[True],
                                        addends={0: dout}, name=f"s5_dnorm_{tag}")
    return dh, g


def _odd_weights(p, j, layer, dt):
    tr = lambda b: b.transpose(0, 2, 1).reshape(S5_GROUPS, S5_GROUP * S5_STATE)
    return dict(
        norm_mix=p["norm_mix"][layer][None], s5_lam_re=p["s5_lam_re"][j], s5_lam_im=p["s5_lam_im"][j],
        s5_log_dt=p["s5_log_dt"][j][:, None], s5_bt_re=tr(p["s5_b_re"][j]), s5_bt_im=tr(p["s5_b_im"][j]),
        s5_expand=_s5_expand(), s5_c3=_s5_pack_c(p["s5_c_re"][j], p["s5_c_im"][j]).astype(dt),
        s5_d=p["s5_d"][j][None], s5_w_glu_a=p["s5_w_glu_a"][j].astype(dt), s5_w_glu_b=p["s5_w_glu_b"][j].astype(dt))


def _odd_grads(g):
    tr = lambda b: b.reshape(S5_GROUPS, S5_GROUP, S5_STATE).transpose(0, 2, 1)[None]
    return dict(
        norm_mix=g["norm_mix"], s5_lam_re=g["s5_lam_re"][None], s5_lam_im=g["s5_lam_im"][None],
        s5_log_dt=g["s5_log_dt"][:, 0][None], s5_b_re=tr(g["s5_bt_re"]), s5_b_im=tr(g["s5_bt_im"]),
        s5_c_re=g["s5_c_re"][None], s5_c_im=g["s5_c_im"][None], s5_d=g["s5_d"],
        s5_w_glu_a=g["s5_w_glu_a"][None], s5_w_glu_b=g["s5_w_glu_b"][None])


def _loss_fn(y, t):
    e = y - t
    part = jnp.sum(jnp.sum(e * e, axis=-1, keepdims=True), axis=0, keepdims=True) * (0.5 / y.shape[1])
    return e * (1.0 / y.shape[1]), part


FF_SHARD = 352
FF_SHARD_PAD = 384


def _pad_groups(a, axis):
    axis %= a.ndim
    s = a.shape
    a = a.reshape(s[:axis] + (s[axis] // FF_SHARD, FF_SHARD) + s[axis + 1:])
    pad = [(0, 0)] * a.ndim
    pad[axis + 1] = (0, FF_SHARD_PAD - FF_SHARD)
    return jnp.pad(a, pad).reshape(s[:axis] + (s[axis] // FF_SHARD * FF_SHARD_PAD,) + s[axis + 1:])


def _unpad_groups(a, axis):
    axis %= a.ndim
    s = a.shape
    a = a.reshape(s[:axis] + (s[axis] // FF_SHARD_PAD, FF_SHARD_PAD) + s[axis + 1:])
    a = lax.slice_in_dim(a, 0, FF_SHARD, axis=axis + 1)
    return a.reshape(s[:axis] + (s[axis] // FF_SHARD_PAD * FF_SHARD,) + s[axis + 1:])


def _layer_weights(p, layer, dt):
    return dict(
        norm_xa=p["norm_xa"][layer][None], norm_mem=p["norm_mem"][layer][None], norm_ffn=p["norm_ffn"][layer][None],
        xa_wq=p["xa_wq"][layer].astype(dt), xa_wk=p["xa_wk"][layer].astype(dt), xa_wv=p["xa_wv"][layer].astype(dt),
        xa_wo=p["xa_wo"][layer].astype(dt), xa_q_norm=p["xa_q_norm"][layer][None],
        xa_k_norm=p["xa_k_norm"][layer][None], ffn_w_up=_pad_groups(p["ffn_w_up"][layer], 1).astype(dt),
        ffn_conv_w=_pad_groups(p["ffn_conv_w"][layer], 1), ffn_conv_b=_pad_groups(p["ffn_conv_b"][layer][None], 1),
        ffn_w_down=_pad_groups(p["ffn_w_down"][layer], 0).astype(dt))


_PER_LAYER = ("norm_xa", "norm_mem", "norm_ffn", "xa_wq", "xa_wk", "xa_wv", "xa_wo", "xa_q_norm", "xa_k_norm",
              "ffn_w_up", "ffn_conv_w", "ffn_conv_b", "ffn_w_down")
_FFN_PADDED = dict(ffn_w_up=1, ffn_conv_w=1, ffn_conv_b=1, ffn_w_down=0)


def _local_step(x, mem, positions, target, p):
    cos, sin = _rope_tables(positions)
    we = _even_weights(p, 0, 0, bf16)
    wo = _odd_weights(p, 0, 1, bf16)
    wl = [_layer_weights(p, layer, bf16) for layer in range(2)]
    loss, dh, g_even, g_odd, gl = _local_core(x, mem, cos, sin, target, we, wo, wl)
    grads = {}
    for n in _PER_LAYER:
        a, b = gl[0][n], gl[1][n]
        if n in _FFN_PADDED:
            a, b = _unpad_groups(a, _FFN_PADDED[n]), _unpad_groups(b, _FFN_PADDED[n])
        grads[n] = jnp.concatenate([a, b], axis=0) if a.shape[0] == 1 else jnp.stack([a, b])
    ge, go = _even_grads(g_even), _odd_grads(g_odd)
    grads["norm_mix"] = jnp.concatenate([ge.pop("norm_mix"), go.pop("norm_mix")], axis=0)
    grads.update(ge)
    grads.update(go)
    return loss, dh, grads


def _local_core(x, mem, cos, sin, target, we, wo, wl):
    h, s_mix0 = _mixer_fwd(x, cos, sin, we, "l0")
    h, s_xa0 = _xattn_fwd(h, mem, wl[0], "l0")
    h, s_ff0 = _ffn_fwd(h, wl[0], "l0")
    h, s_mix1 = _s5_fwd(h, wo, "l1")
    h, s_xa1 = _xattn_fwd(h, mem, wl[1], "l1")
    h, s_ff1 = _ffn_fwd(h, wl[1], "l1")
    dh, loss = _rows(_loss_fn, [h, target], [], [(h.shape[1], f32)], accs=[(1, 1)], name="loss_head")

    gl = [{}, {}]
    dh, g = _ffn_bwd(dh, wl[1], s_ff1, "l1")
    gl[1].update(g)
    dh, g = _xattn_bwd(dh, mem, wl[1], s_xa1, "l1")
    gl[1].update(g)
    dh, g_odd = _s5_bwd(dh, wo, s_mix1, "l1")
    dh, g = _ffn_bwd(dh, wl[0], s_ff0, "l0")
    gl[0].update(g)
    dh, g = _xattn_bwd(dh, mem, wl[0], s_xa0, "l0")
    gl[0].update(g)
    dh, g_even = _mixer_bwd(dh, cos, sin, we, s_mix0, "l0")
    return loss, dh, g_even, g_odd, gl


_LANES = 1024
_ROW_PAD = 256


_PEER_MASKS = (1, 2, 4, 3, 5, 6, 7)


def _mesh_place():
    x, y, c = lax.axis_index("x"), lax.axis_index("y"), lax.axis_index("c")

    def peer(mask):
        px = 1 - x if mask & 4 else x
        py = 1 - y if mask & 2 else y
        pc = 1 - c if mask & 1 else c
        return (px, py, pc), 4 * px + 2 * py + pc

    return 4 * x + 2 * y + c, peer


class _Exchange:
    def __init__(self, name):
        self.name = name
        self.srcs, self.shapes, self.items, self.where = [], [], [], {}

    def add(self, src, land_shape, src_at, dst_at, key):
        si = next((i for i, s in enumerate(self.srcs) if s is src), None)
        if si is None:
            self.srcs.append(src)
            si = len(self.srcs) - 1
        if key not in self.where:
            self.shapes.append(land_shape)
            self.where[key] = len(self.shapes) - 1
        self.items.append(dict(src=si, dst=self.where[key], src_at=src_at, dst_at=dst_at))

    def _copy(self, k, mask, ins, lands, send_sems, recv_sems, me, peer, arriving):
        it = self.items[k]
        dev, idx = peer(mask)
        s = k * (N_DEV - 1) + _PEER_MASKS.index(mask)
        return pltpu.make_async_remote_copy(
            src_ref=it["src_at"](ins[it["src"]], idx), dst_ref=it["dst_at"](lands[it["dst"]], idx if arriving else me),
            send_sem=send_sems.at[s], recv_sem=recv_sems.at[s], device_id=dev, device_id_type=pl.DeviceIdType.MESH)

    def fill(self):
        ns, nd, ni = len(self.srcs), len(self.shapes), len(self.items)

        def body(*refs):
            ins, outs, sems = refs[:ns], refs[ns:ns + nd], refs[ns + nd]
            me, _ = _mesh_place()
            cps = [pltpu.make_async_copy(it["src_at"](ins[it["src"]], me), it["dst_at"](outs[it["dst"]], me), sems.at[k])
                   for k, it in enumerate(self.items)]
            for cp in cps:
                cp.start()
            for cp in cps:
                cp.wait()

        hbm = pl.BlockSpec(memory_space=pltpu.HBM)
        return pl.pallas_call(body, in_specs=[hbm] * ns, out_specs=[hbm] * nd, out_shape=list(self.shapes),
                              scratch_shapes=[pltpu.SemaphoreType.DMA((ni,))], name=self.name + "_fill")(*self.srcs)

    def start(self, lands):
        ns, nd, ni = len(self.srcs), len(self.shapes), len(self.items)
        nsem = ni * (N_DEV - 1)

        def body(*refs):
            ins, land_refs = refs[:ns], refs[ns:ns + nd]
            send_sems, recv_sems, token = refs[ns + nd], refs[ns + nd + 1], refs[-1]
            me, peer = _mesh_place()
            for mask in _PEER_MASKS:
                for k in range(ni):
                    self._copy(k, mask, ins, land_refs, send_sems, recv_sems, me, peer, False).start()
            token[...] = jnp.zeros_like(token)

        hbm = pl.BlockSpec(memory_space=pltpu.HBM)
        sem = pl.BlockSpec(memory_space=pltpu.SEMAPHORE)
        res = pl.pallas_call(
            body, in_specs=[hbm] * (ns + nd),
            out_specs=[sem, sem] + [hbm] * nd + [pl.BlockSpec(memory_space=pltpu.VMEM)],
            out_shape=[pltpu.SemaphoreType.DMA((nsem,)), pltpu.SemaphoreType.DMA((nsem,))]
            + [pltpu.HBM(s.shape, s.dtype) for s in self.shapes] + [jax.ShapeDtypeStruct((8, 128), f32)],
            input_output_aliases={ns + j: 2 + j for j in range(nd)},
            compiler_params=pltpu.CompilerParams(has_side_effects=pltpu.SideEffectType.DATAFLOW_SIDE_EFFECTING),
            name=self.name + "_start")(*self.srcs, *lands)
        self.token = res[-1]
        return res[0], res[1], list(res[2:-1])

    def wait(self, send_sems, recv_sems, lands, after):
        ns, nd, ni = len(self.srcs), len(self.shapes), len(self.items)

        def body(*refs):
            ins, land_refs = refs[:ns], refs[ns:ns + nd]
            ssem, rsem = refs[ns + nd], refs[ns + nd + 1]
            me, peer = _mesh_place()
            for mask in _PEER_MASKS:
                for k in range(ni):
                    cp = self._copy(k, mask, ins, land_refs, ssem, rsem, me, peer, True)
                    cp.wait_send()
                    cp.wait_recv()

        hbm = pl.BlockSpec(memory_space=pltpu.HBM)
        sem = pl.BlockSpec(memory_space=pltpu.SEMAPHORE)
        res = pl.pallas_call(
            body, in_specs=[hbm] * (ns + nd) + [sem, sem, pl.BlockSpec(memory_space=pl.ANY)],
            out_specs=[hbm] * nd, out_shape=[pltpu.HBM(s.shape, s.dtype) for s in self.shapes],
            input_output_aliases={ns + j: j for j in range(nd)},
            compiler_params=pltpu.CompilerParams(has_side_effects=pltpu.SideEffectType.DATAFLOW_SIDE_EFFECTING),
            name=self.name + "_wait")(*self.srcs, *lands, send_sems, recv_sems, after)
        return {k: res[i] for k, i in self.where.items()}

    def begin(self, fill=True):
        lands = self.fill() if fill else [lax.empty(s.shape, s.dtype) for s in self.shapes]
        return self.start(lands)

    def finish(self, state, after):
        return self.wait(*state, after)


def _after(x, *tokens, name):
    def body(*refs):
        del refs

    anyspace = pl.BlockSpec(memory_space=pl.ANY)
    return pl.pallas_call(body, in_specs=[anyspace] * (1 + len(tokens)), out_specs=anyspace,
                          out_shape=jax.ShapeDtypeStruct(x.shape, x.dtype), input_output_aliases={0: 0},
                          name=name)(x, *tokens)


def _rows_of(n):
    return lambda r, i: r.at[pl.ds(pl.multiple_of(i * n, n), n), :]


def _cols_of(n):
    return lambda r, i: r.at[:, pl.ds(pl.multiple_of(i * n, n), n)]


def _whole(r, i):
    return r


def _slot(r, i):
    return r.at[i]


def _at_layer(layer):
    return lambda r, i: r.at[layer]


def _slot_layer(layer):
    return lambda r, i: r.at[i, layer]


def _sum_adam(me_index, slots, own, own_block, w, m, v, *, name):
    rows, cols = w.shape
    tr = _pick(rows, (256, 128, 64, 32, 16, 8))
    bc1 = 1.0 - ADAM_B1 ** ADAM_STEP
    bc2 = 1.0 - ADAM_B2 ** ADAM_STEP

    own_shape, own_map = own_block(tr)

    def body(me_ref, s_ref, own_ref, w_ref, m_ref, v_ref, g_ref, d_ref, nm_ref, nv_ref):
        mine = own_ref[0] if len(own_shape) == 3 else own_ref[...]
        me = me_ref[0]
        g = jnp.where(me == 0, mine, s_ref[0])
        for k in range(1, N_DEV):
            g = g + jnp.where(me == k, mine, s_ref[k])
        mm = ADAM_B1 * m_ref[...] + (1.0 - ADAM_B1) * g
        vv = ADAM_B2 * v_ref[...] + (1.0 - ADAM_B2) * (g * g)
        g_ref[...] = g
        nm_ref[...] = mm
        nv_ref[...] = vv
        d_ref[...] = -ADAM_LR * ((mm / bc1) / (jnp.sqrt(vv / bc2) + ADAM_EPS) + ADAM_WD * w_ref[...])

    blk = pl.BlockSpec((tr, cols), lambda i, me: (i, 0))
    sds = jax.ShapeDtypeStruct((rows, cols), f32)
    grid_spec = pltpu.PrefetchScalarGridSpec(
        num_scalar_prefetch=1, grid=(rows // tr,),
        in_specs=[pl.BlockSpec((N_DEV, tr, cols), lambda i, me: (0, i, 0)), pl.BlockSpec(own_shape, own_map),
                  blk, blk, blk],
        out_specs=[blk] * 4)
    return pl.pallas_call(body, grid_spec=grid_spec, out_shape=[sds] * 4, compiler_params=_cparams(("parallel",)),
                          name=name)(me_index, slots, own, w, m, v)


_SHARDED = dict(xa_wq=1, xa_wk=1, xa_wv=1, xa_wo=1, ffn_w_up=2, ffn_conv_w=2, ffn_w_down=1, mix_w_in=2, mla_w_uq=2,
                mla_w_ukv=2, mix_w_out=1, s5_d=1, s5_w_glu_a=1, s5_w_glu_b=1)
_EXACT = ("ffn_conv_w", "s5_d")
_WEIGHTS = ("norm_mix", "norm_xa", "norm_mem", "norm_ffn", "xa_wq", "xa_wk", "xa_wv", "xa_wo", "xa_q_norm",
            "xa_k_norm", "ffn_w_up", "ffn_conv_w", "ffn_conv_b", "ffn_w_down", "hg_lb_logits", "mix_w_in",
            "hg_out_norm", "mla_q_a_norm", "mla_w_uq", "mla_kv_a_norm", "mla_w_ukv", "mla_qn_nope", "mla_qn_rope",
            "mla_kn_nope", "mla_kn_rope", "mix_w_out", "s5_lam_re", "s5_lam_im", "s5_log_dt", "s5_b_re", "s5_b_im",
            "s5_c_re", "s5_c_im", "s5_d", "s5_w_glu_a", "s5_w_glu_b")
_BIG = tuple(n for n in _WEIGHTS if n in _SHARDED and n not in _EXACT)
_SHARD_ORDER = tuple(n for n in _WEIGHTS if n in _SHARDED)
_REPL_ORDER = tuple(n for n in _WEIGHTS if n not in _SHARDED)


def _pack(parts, dtype, lead=None):
    nl = 0 if lead is None else 1
    flat = [a.astype(dtype).reshape(a.shape[:nl] + (-1,)) for a in parts]
    cat = jnp.concatenate(flat, axis=nl)
    n = cat.shape[nl]
    unit = _LANES * _ROW_PAD
    total = -(-n // unit) * unit
    cat = jnp.pad(cat, [(0, 0)] * nl + [(0, total - n)])
    return cat.reshape(cat.shape[:nl] + (total // _LANES, _LANES))


def _unpack(packed, shapes, lead=None):
    nl = 0 if lead is None else 1
    flat = packed.reshape(packed.shape[:nl] + (-1,))
    out, off = [], 0
    for s in shapes:
        n = int(np.prod(s))
        piece = flat[..., off:off + n] if nl else flat[off:off + n]
        out.append(piece.reshape(packed.shape[:nl] + tuple(s)))
        off += n
    return out


def _to_full(gathered, axis):
    g = jnp.moveaxis(gathered, 0, axis)
    s = g.shape
    return g.reshape(s[:axis] + (s[axis] * s[axis + 1],) + s[axis + 2:])


def _to_shards(full, axis):
    s = full.shape
    g = full.reshape(s[:axis] + (N_DEV, s[axis] // N_DEV) + s[axis + 1:])
    return jnp.moveaxis(g, axis, 0)


_DIRECT_ROWS = ("xa_wq", "xa_wk", "xa_wv", "xa_wo", "mix_w_out", "s5_w_glu_a", "s5_w_glu_b")
_SMALL16 = ("mix_w_in", "mla_w_uq", "mla_w_ukv")
_SMALL_SHARDED = _SMALL16 + _EXACT
_SHARD_ROWS = 128


def _exchange_layout(d):
    out = dict(d)
    out["ffn_w_up"] = _pad_groups(d["ffn_w_up"], 2)
    out["ffn_conv_w"] = _pad_groups(d["ffn_conv_w"], 2)
    out["ffn_w_down"] = _pad_groups(d["ffn_w_down"], 1)
    return out


def _train_step(x, mem, positions, target, w, m, v):
    d_model = x.shape[1]
    we_, me_, ve_ = _exchange_layout(w), _exchange_layout(m), _exchange_layout(v)
    sds = jax.ShapeDtypeStruct

    matrices = _DIRECT_ROWS + ("ffn_w_up", "ffn_w_down")
    layer_mats = ("xa_wq", "xa_wk", "xa_wv", "xa_wo", "ffn_w_up", "ffn_w_down")
    shard16 = {n: we_[n].astype(bf16) for n in matrices}
    part_of = {n: _rows_of(_SHARD_ROWS) for n in _DIRECT_ROWS}
    part_of["ffn_w_up"] = _cols_of(we_["ffn_w_up"].shape[2])
    part_of["ffn_w_down"] = _rows_of(we_["ffn_w_down"].shape[1])
    part_shape = {n: we_[n].shape[1:] for n in matrices}

    def full_shape(n):
        r, c = part_shape[n]
        return (r, N_DEV * c) if n == "ffn_w_up" else (N_DEV * r, c)

    def gather(ex, n, layer):
        ex.add(shard16[n], sds(full_shape(n), bf16), _at_layer(layer), part_of[n], (n, layer))

    def scatter(ex, n, layer, grad):
        ex.add(grad, sds((N_DEV,) + part_shape[n], f32), part_of[n], _slot, (n, layer))

    small16 = _pack([we_[n] for n in _SMALL16], bf16)
    exact = _pack([we_[n] for n in _EXACT], f32)
    ga, gb, gc = _Exchange("gather_a"), _Exchange("gather_b"), _Exchange("gather_c")
    ga.add(small16, sds((N_DEV,) + small16.shape, bf16), _whole, _slot, "small16")
    ga.add(exact, sds((N_DEV,) + exact.shape, f32), _whole, _slot, "exact")
    gather(ga, "mix_w_out", 0)
    for n in layer_mats:
        gather(gb, n, 0)
    gather(gc, "s5_w_glu_a", 0)
    gather(gc, "s5_w_glu_b", 0)
    for n in layer_mats:
        gather(gc, n, 1)
    state_a, state_b, state_c = ga.begin(), gb.begin(), gc.begin()

    full = ga.finish(state_a, x)
    p = {n: w[n] for n in _REPL_ORDER}
    for n, a in zip(_SMALL16, _unpack(full["small16"], [we_[n].shape for n in _SMALL16], lead=True)):
        p[n] = _to_full(a, _SHARDED[n])
    conv_w, p["s5_d"] = [_to_full(a, _SHARDED[n]) for n, a in
                         zip(_EXACT, _unpack(full["exact"], [we_[n].shape for n in _EXACT], lead=True))]
    p["mix_w_out"] = full[("mix_w_out", 0)][None]
    cos, sin = _rope_tables(positions)
    we = _even_weights(p, 0, 0, bf16)
    we["norm_mix"] = _after(we["norm_mix"], ga.token, gb.token, gc.token, name="after_gather_starts")
    conv_b = _pad_groups(w["ffn_conv_b"], 1)

    def layer_weights(layer):
        return dict(norm_xa=w["norm_xa"][layer][None], norm_mem=w["norm_mem"][layer][None],
                    norm_ffn=w["norm_ffn"][layer][None], xa_q_norm=w["xa_q_norm"][layer][None],
                    xa_k_norm=w["xa_k_norm"][layer][None], ffn_conv_w=conv_w[layer],
                    ffn_conv_b=conv_b[layer][None], **{n: full[(n, layer)] for n in layer_mats})

    h, s_mix0 = _mixer_fwd(x, cos, sin, we, "l0")
    full.update(gb.finish(state_b, h))
    wl = [layer_weights(0)]
    h, s_xa0 = _xattn_fwd(h, mem, wl[0], "l0")
    h, s_ff0 = _ffn_fwd(h, wl[0], "l0")
    full.update(gc.finish(state_c, h))
    wl.append(layer_weights(1))
    p["s5_w_glu_a"], p["s5_w_glu_b"] = full[("s5_w_glu_a", 0)][None], full[("s5_w_glu_b", 0)][None]
    wo = _odd_weights(p, 0, 1, bf16)
    h, s_mix1 = _s5_fwd(h, wo, "l1")
    h, s_xa1 = _xattn_fwd(h, mem, wl[1], "l1")
    h, s_ff1 = _ffn_fwd(h, wl[1], "l1")
    dh, loss = _rows(_loss_fn, [h, target], [], [(h.shape[1], f32)], accs=[(1, 1)], name="loss_head")

    gl = [{}, {}]
    reduces = []

    own_grad = {}

    def reduce_start(name, entries, dh):
        ex = _Exchange(name)
        for n, layer, grad in entries:
            scatter(ex, n, layer, grad)
            own_grad[(n, layer)] = grad
        reduces.append((ex, ex.begin(fill=False)))
        return _after(dh, ex.token, name="after_" + name)

    dh, gl[1] = _ffn_bwd(dh, wl[1], s_ff1, "l1")
    dh = reduce_start("reduce_ffn1", [(n, 1, gl[1][n]) for n in ("ffn_w_up", "ffn_w_down")], dh)
    dh, g = _xattn_bwd(dh, mem, wl[1], s_xa1, "l1")
    gl[1].update(g)
    dh = reduce_start("reduce_xa1", [(n, 1, g[n]) for n in ("xa_wq", "xa_wk", "xa_wv", "xa_wo")], dh)
    dh, g_odd = _s5_bwd(dh, wo, s_mix1, "l1")
    dh, gl[0] = _ffn_bwd(dh, wl[0], s_ff0, "l0")
    dh = reduce_start("reduce_ffn0", [(n, 0, g_odd[n]) for n in ("s5_w_glu_a", "s5_w_glu_b")]
                      + [(n, 0, gl[0][n]) for n in ("ffn_w_up", "ffn_w_down")], dh)
    dh, g = _xattn_bwd(dh, mem, wl[0], s_xa0, "l0")
    gl[0].update(g)
    dh = reduce_start("reduce_xa0", [(n, 0, g[n]) for n in ("xa_wq", "xa_wk", "xa_wv", "xa_wo")], dh)
    grad_x, g_even = _mixer_bwd(dh, cos, sin, we, s_mix0, "l0")

    ge, go = _even_grads(g_even), _odd_grads(g_odd)
    cat = lambda n: jnp.concatenate([gl[0][n], gl[1][n]], axis=0)
    rg = dict(ge)
    rg.update(go)
    rg["norm_mix"] = jnp.concatenate([ge["norm_mix"], go["norm_mix"]], axis=0)
    for n in ("norm_xa", "norm_mem", "norm_ffn", "xa_q_norm", "xa_k_norm"):
        rg[n] = cat(n)
    rg["ffn_conv_b"] = _unpad_groups(cat("ffn_conv_b"), 1)
    sg = dict(mix_w_in=ge["mix_w_in"], mla_w_uq=ge["mla_w_uq"], mla_w_ukv=ge["mla_w_ukv"], s5_d=go["s5_d"],
              ffn_conv_w=jnp.stack([gl[0]["ffn_conv_w"], gl[1]["ffn_conv_w"]]))

    send_small = _pack([_to_shards(sg[n], _SHARDED[n]) for n in _SMALL_SHARDED], f32, lead=True)
    send_repl = _pack([rg[n].reshape(w[n].shape) for n in _REPL_ORDER], f32)
    last = _Exchange("reduce_mix")
    scatter(last, "mix_w_out", 0, g_even["mix_w_out"])
    own_grad[("mix_w_out", 0)] = g_even["mix_w_out"]
    last.add(send_small, sds(send_small.shape, f32), _slot, _slot, "small")
    last.add(send_repl, sds((N_DEV,) + send_repl.shape, f32), _whole, _slot, "repl")
    reduces.append((last, last.begin(fill=False)))
    slots = {}
    for ex, state in reduces:
        slots.update(ex.finish(state, grad_x))

    me_index = (4 * lax.axis_index("x") + 2 * lax.axis_index("y") + lax.axis_index("c")).astype(jnp.int32).reshape(1)

    def own_block(n):
        r, c = part_shape[n]
        if n == "ffn_w_up":
            return lambda tr: ((tr, c), lambda i, me: (i, me[0]))
        return lambda tr: ((tr, c), lambda i, me: (me[0] * (r // tr) + i, 0))

    out = [{}, {}, {}, {}]
    unpad = dict(ffn_w_up=2, ffn_conv_w=2, ffn_w_down=1)
    for n in matrices:
        per_layer = [_sum_adam(me_index, slots[(n, layer)], own_grad[(n, layer)], own_block(n), we_[n][layer],
                               me_[n][layer], ve_[n][layer], name=f"adam_{n}_{layer}")
                     for layer in range(we_[n].shape[0])]
        for k in range(4):
            r = jnp.stack([res[k] for res in per_layer])
            out[k][n] = _unpad_groups(r, unpad[n]) if n in unpad else r
    pk = lambda d, order: _pack([d[n] for n in order], f32)
    res_small = _sum_adam(me_index, slots["small"], send_small,
                          lambda tr: ((1, tr, _LANES), lambda i, me: (me[0], i, 0)),
                          pk(we_, _SMALL_SHARDED), pk(me_, _SMALL_SHARDED), pk(ve_, _SMALL_SHARDED), name="adam_small")
    res_repl = _sum_adam(me_index, slots["repl"], send_repl, lambda tr: ((tr, _LANES), lambda i, me: (i, 0)),
                         pk(w, _REPL_ORDER), pk(m, _REPL_ORDER), pk(v, _REPL_ORDER), name="adam_repl")
    for k in range(4):
        for n, a in zip(_SMALL_SHARDED, _unpack(res_small[k], [we_[n].shape for n in _SMALL_SHARDED])):
            out[k][n] = _unpad_groups(a, unpad[n]) if n in unpad else a
        out[k].update(zip(_REPL_ORDER, _unpack(res_repl[k], [w[n].shape for n in _REPL_ORDER])))
    return loss, grad_x, out


_INPUTS = tuple("""x, mem, positions, norm_mix, norm_xa, norm_mem, norm_ffn, xa_wq, xa_wk, xa_wv, xa_wo, xa_q_norm, xa_k_norm, ffn_w_up, ffn_conv_w, ffn_conv_b, ffn_w_down, hg_lb_logits, mix_w_in, hg_out_norm, mla_q_a_norm, mla_w_uq, mla_kv_a_norm, mla_w_ukv, mla_qn_nope, mla_qn_rope, mla_kn_nope, mla_kn_rope, mix_w_out, s5_lam_re, s5_lam_im, s5_log_dt, s5_b_re, s5_b_im, s5_c_re, s5_c_im, s5_d, s5_w_glu_a, s5_w_glu_b, loss_target, m_norm_mix, m_norm_xa, m_norm_mem, m_norm_ffn, m_xa_wq, m_xa_wk, m_xa_wv, m_xa_wo, m_xa_q_norm, m_xa_k_norm, m_ffn_w_up, m_ffn_conv_w, m_ffn_conv_b, m_ffn_w_down, m_hg_lb_logits, m_mix_w_in, m_hg_out_norm, m_mla_q_a_norm, m_mla_w_uq, m_mla_kv_a_norm, m_mla_w_ukv, m_mla_qn_nope, m_mla_qn_rope, m_mla_kn_nope, m_mla_kn_rope, m_mix_w_out, m_s5_lam_re, m_s5_lam_im, m_s5_log_dt, m_s5_b_re, m_s5_b_im, m_s5_c_re, m_s5_c_im, m_s5_d, m_s5_w_glu_a, m_s5_w_glu_b, v_norm_mix, v_norm_xa, v_norm_mem, v_norm_ffn, v_xa_wq, v_xa_wk, v_xa_wv, v_xa_wo, v_xa_q_norm, v_xa_k_norm, v_ffn_w_up, v_ffn_conv_w, v_ffn_conv_b, v_ffn_w_down, v_hg_lb_logits, v_mix_w_in, v_hg_out_norm, v_mla_q_a_norm, v_mla_w_uq, v_mla_kv_a_norm, v_mla_w_ukv, v_mla_qn_nope, v_mla_qn_rope, v_mla_kn_nope, v_mla_kn_rope, v_mix_w_out, v_s5_lam_re, v_s5_lam_im, v_s5_log_dt, v_s5_b_re, v_s5_b_im, v_s5_c_re, v_s5_c_im, v_s5_d, v_s5_w_glu_a, v_s5_w_glu_b""".replace(" ", "").split(","))


def kernel(x, mem, positions, norm_mix, norm_xa, norm_mem, norm_ffn, xa_wq, xa_wk, xa_wv, xa_wo, xa_q_norm, xa_k_norm, ffn_w_up, ffn_conv_w, ffn_conv_b, ffn_w_down, hg_lb_logits, mix_w_in, hg_out_norm, mla_q_a_norm, mla_w_uq, mla_kv_a_norm, mla_w_ukv, mla_qn_nope, mla_qn_rope, mla_kn_nope, mla_kn_rope, mix_w_out, s5_lam_re, s5_lam_im, s5_log_dt, s5_b_re, s5_b_im, s5_c_re, s5_c_im, s5_d, s5_w_glu_a, s5_w_glu_b, loss_target, m_norm_mix, m_norm_xa, m_norm_mem, m_norm_ffn, m_xa_wq, m_xa_wk, m_xa_wv, m_xa_wo, m_xa_q_norm, m_xa_k_norm, m_ffn_w_up, m_ffn_conv_w, m_ffn_conv_b, m_ffn_w_down, m_hg_lb_logits, m_mix_w_in, m_hg_out_norm, m_mla_q_a_norm, m_mla_w_uq, m_mla_kv_a_norm, m_mla_w_ukv, m_mla_qn_nope, m_mla_qn_rope, m_mla_kn_nope, m_mla_kn_rope, m_mix_w_out, m_s5_lam_re, m_s5_lam_im, m_s5_log_dt, m_s5_b_re, m_s5_b_im, m_s5_c_re, m_s5_c_im, m_s5_d, m_s5_w_glu_a, m_s5_w_glu_b, v_norm_mix, v_norm_xa, v_norm_mem, v_norm_ffn, v_xa_wq, v_xa_wk, v_xa_wv, v_xa_wo, v_xa_q_norm, v_xa_k_norm, v_ffn_w_up, v_ffn_conv_w, v_ffn_conv_b, v_ffn_w_down, v_hg_lb_logits, v_mix_w_in, v_hg_out_norm, v_mla_q_a_norm, v_mla_w_uq, v_mla_kv_a_norm, v_mla_w_ukv, v_mla_qn_nope, v_mla_qn_rope, v_mla_kn_nope, v_mla_kn_rope, v_mix_w_out, v_s5_lam_re, v_s5_lam_im, v_s5_log_dt, v_s5_b_re, v_s5_b_im, v_s5_c_re, v_s5_c_im, v_s5_d, v_s5_w_glu_a, v_s5_w_glu_b):
    vals = dict(zip(_INPUTS, (x, mem, positions, norm_mix, norm_xa, norm_mem, norm_ffn, xa_wq, xa_wk, xa_wv, xa_wo, xa_q_norm, xa_k_norm, ffn_w_up, ffn_conv_w, ffn_conv_b, ffn_w_down, hg_lb_logits, mix_w_in, hg_out_norm, mla_q_a_norm, mla_w_uq, mla_kv_a_norm, mla_w_ukv, mla_qn_nope, mla_qn_rope, mla_kn_nope, mla_kn_rope, mix_w_out, s5_lam_re, s5_lam_im, s5_log_dt, s5_b_re, s5_b_im, s5_c_re, s5_c_im, s5_d, s5_w_glu_a, s5_w_glu_b, loss_target, m_norm_mix, m_norm_xa, m_norm_mem, m_norm_ffn, m_xa_wq, m_xa_wk, m_xa_wv, m_xa_wo, m_xa_q_norm, m_xa_k_norm, m_ffn_w_up, m_ffn_conv_w, m_ffn_conv_b, m_ffn_w_down, m_hg_lb_logits, m_mix_w_in, m_hg_out_norm, m_mla_q_a_norm, m_mla_w_uq, m_mla_kv_a_norm, m_mla_w_ukv, m_mla_qn_nope, m_mla_qn_rope, m_mla_kn_nope, m_mla_kn_rope, m_mix_w_out, m_s5_lam_re, m_s5_lam_im, m_s5_log_dt, m_s5_b_re, m_s5_b_im, m_s5_c_re, m_s5_c_im, m_s5_d, m_s5_w_glu_a, m_s5_w_glu_b, v_norm_mix, v_norm_xa, v_norm_mem, v_norm_ffn, v_xa_wq, v_xa_wk, v_xa_wv, v_xa_wo, v_xa_q_norm, v_xa_k_norm, v_ffn_w_up, v_ffn_conv_w, v_ffn_conv_b, v_ffn_w_down, v_hg_lb_logits, v_mix_w_in, v_hg_out_norm, v_mla_q_a_norm, v_mla_w_uq, v_mla_kv_a_norm, v_mla_w_ukv, v_mla_qn_nope, v_mla_qn_rope, v_mla_kn_nope, v_mla_kn_rope, v_mix_w_out, v_s5_lam_re, v_s5_lam_im, v_s5_log_dt, v_s5_b_re, v_s5_b_im, v_s5_c_re, v_s5_c_im, v_s5_d, v_s5_w_glu_a, v_s5_w_glu_b)))
    w = {n: vals[n] for n in _WEIGHTS}
    m = {n: vals["m_" + n] for n in _WEIGHTS}
    v = {n: vals["v_" + n] for n in _WEIGHTS}
    loss, grad_x, res = _train_step(vals["x"][0], vals["mem"][0], vals["positions"][0], vals["loss_target"][0],
                                    w, m, v)
    loss = lax.psum(loss[0, 0], ("x", "y", "c"))
    return (loss, grad_x[None], *[r[n] for r in res for n in _WEIGHTS])
```

```python
import functools

import jax
import jax.numpy as jnp
import numpy as np
from jax import lax
from jax.experimental import pallas as pl
from jax.experimental.pallas import tpu as pltpu

f32 = jnp.float32
bf16 = jnp.bfloat16

EPS = 1e-6
N_DEV = 8
VMEM_LIMIT = 52 * 1024 * 1024

HG_HEADS = 4
HG_DIM = 128
HG_WIDTH = HG_HEADS * HG_DIM
HG_CHUNK = 64
HG_SUB = 16
MLA_HEADS = 4
MLA_Q_RANK = 256
MLA_KV_RANK = 128
MLA_NOPE = 128
MLA_ROPE = 64
MLA_V = 128
MLA_QK = MLA_NOPE + MLA_ROPE
MLA_QK_PAD = 256
ROPE_BASE = 10000.0
IN_WIDTH = 4 * HG_WIDTH + MLA_Q_RANK + MLA_KV_RANK + MLA_ROPE
IN_PAD = 2560
XA_HEADS = 4
XA_DIM = 256
S5_GROUP = 16
S5_GROUPS = 64
S5_STATE = 64
CONV_W = 3

ADAM_LR = 0.001
ADAM_B1 = 0.9
ADAM_B2 = 0.999
ADAM_EPS = 1e-08
ADAM_WD = 0.01
ADAM_STEP = 10

_NT = (((1,), (1,)), ((), ()))
_TN = (((0,), (0,)), ((), ()))
_NN = (((1,), (0,)), ((), ()))


def _pick(n, cands):
    for c in cands:
        if n % c == 0:
            return c
    return n


def _cparams(sem):
    return pltpu.CompilerParams(dimension_semantics=sem, vmem_limit_bytes=VMEM_LIMIT)


_MM_BUDGET = 36 * 1024 * 1024


def _mm(a, b, *, name, ta=False, tb=False, out_dtype=f32, add=None, b2=None, kslab=None):
    m, k = (a.shape[1], a.shape[0]) if ta else a.shape
    nb = b.shape[0] if tb else b.shape[1]
    n = nb * (2 if b2 is not None else 1)
    slab, nslab = kslab if kslab is not None else (0, 1)
    assert (b.shape[1] // nslab if tb else b.shape[0]) == k, (a.shape, b.shape, ta, tb)
    assert b2 is None or (not tb and b2.shape == b.shape)
    isz = lambda x: jnp.dtype(x.dtype).itemsize
    bm = bn = None
    for cm, cn in ((512, 512), (512, 256), (256, 512), (256, 256), (256, 128), (128, 256), (128, 128)):
        if m % cm or nb % cn:
            continue
        need = 2 * (cm * k * isz(a) + cn * k * isz(b) * (2 if b2 is not None else 1)
                    + cm * cn * (jnp.dtype(out_dtype).itemsize + (4 if add is not None else 0)))
        if need <= _MM_BUDGET:
            bm, bn = cm, cn
            break
    assert bm is not None, (name, a.shape, b.shape)
    half = nb // bn
    dims = (((0 if ta else 1,), (1 if tb else 0,)), ((), ()))

    def body(*refs):
        refs = list(refs)
        a_ref, b_ref = refs[0], refs[1]
        b2_ref = refs.pop(2) if b2 is not None else None
        add_ref = refs[2] if add is not None else None
        o_ref = refs[-1]

        def run(rhs_ref):
            r = lax.dot_general(a_ref[...].astype(bf16), rhs_ref[...].astype(bf16), dims, preferred_element_type=f32)
            if add_ref is not None:
                r = r + add_ref[...].astype(f32)
            o_ref[...] = r.astype(o_ref.dtype)

        if b2_ref is None:
            run(b_ref)
        else:
            pl.when(pl.program_id(1) < half)(lambda: run(b_ref))
            pl.when(pl.program_id(1) >= half)(lambda: run(b2_ref))

    a_spec = pl.BlockSpec((k, bm), lambda i, j: (0, i)) if ta else pl.BlockSpec((bm, k), lambda i, j: (i, 0))
    if tb:
        b_spec = pl.BlockSpec((bn, k), lambda i, j: (j, slab))
    elif b2 is None:
        b_spec = pl.BlockSpec((k, bn), lambda i, j: (0, j))
    else:
        b_spec = pl.BlockSpec((k, bn), lambda i, j: (0, jnp.minimum(j, half - 1)))
    in_specs = [a_spec, b_spec]
    args = [a, b]
    if b2 is not None:
        in_specs.append(pl.BlockSpec((k, bn), lambda i, j: (0, jnp.maximum(j - half, 0))))
        args.append(b2)
    if add is not None:
        in_specs.append(pl.BlockSpec((bm, bn), lambda i, j: (i, j)))
        args.append(add)
    return pl.pallas_call(
        body, grid=(m // bm, n // bn), in_specs=in_specs,
        out_specs=pl.BlockSpec((bm, bn), lambda i, j: (i, j)),
        out_shape=jax.ShapeDtypeStruct((m, n), out_dtype),
        compiler_params=_cparams(("parallel", "parallel")), name=name)(*args)


def _as_tuple(x):
    return tuple(x) if isinstance(x, (tuple, list)) else (x,)


def _full_spec(p):
    nd = p.ndim
    return pl.BlockSpec(p.shape, lambda i, _nd=nd: (0,) * _nd)


def _rows(fn, rows, params, outs, *, name, tile=256, accs=()):
    length = rows[0].shape[0]
    tile = min(tile, length)
    nr, npar, no = len(rows), len(params), len(outs)

    def body(*refs):
        r, p, o = refs[:nr], refs[nr:nr + npar], refs[nr + npar:]
        res = _as_tuple(fn(*[x[...].astype(f32) for x in r], *[x[...] for x in p]))
        for kk in range(no):
            o[kk][...] = res[kk].astype(o[kk].dtype)
        if accs:
            @pl.when(pl.program_id(0) == 0)
            def _():
                for kk in range(no, no + len(accs)):
                    o[kk][...] = jnp.zeros_like(o[kk])
            for kk in range(no, no + len(accs)):
                o[kk][...] += res[kk]

    in_specs = [pl.BlockSpec((tile, x.shape[1]), lambda i: (i, 0)) for x in rows] + [_full_spec(p) for p in params]
    out_specs = [pl.BlockSpec((tile, w), lambda i: (i, 0)) for w, _ in outs]
    out_shape = [jax.ShapeDtypeStruct((length, w), d) for w, d in outs]
    for s in accs:
        out_specs.append(pl.BlockSpec(s, lambda i, _nd=len(s): (0,) * _nd))
        out_shape.append(jax.ShapeDtypeStruct(s, f32))
    res = pl.pallas_call(body, grid=(length // tile,), in_specs=in_specs, out_specs=out_specs, out_shape=out_shape,
                         compiler_params=_cparams(("arbitrary",)), name=name)(*rows, *params)
    return res


def _rows_bwd(fn, rows, params, cts, *, name, rgrad, pgrad, tile=256, addends=None):
    addends = addends or {}
    length = rows[0].shape[0]
    tile = min(tile, length)
    nr, npar, nc = len(rows), len(params), len(cts)
    ridx = [i for i in range(nr) if rgrad[i] is not None]
    pidx = [i for i in range(npar) if pgrad[i]]
    aidx = sorted(addends)
    na = len(aidx)

    def body(*refs):
        r, p, c = refs[:nr], refs[nr:nr + npar], refs[nr + npar:nr + npar + nc]
        ad = refs[nr + npar + nc:nr + npar + nc + na]
        o = refs[nr + npar + nc + na:]
        rv = [x[...].astype(f32) for x in r]
        pv = [x[...] for x in p]
        cv = tuple(x[...].astype(f32) for x in c)

        def g(*d):
            rr, pp = list(rv), list(pv)
            for n_, i_ in enumerate(ridx):
                rr[i_] = d[n_]
            for n_, i_ in enumerate(pidx):
                pp[i_] = d[len(ridx) + n_]
            return _as_tuple(fn(*rr, *pp))

        _, vjp = jax.vjp(g, *[rv[i] for i in ridx], *[pv[i] for i in pidx])
        grads = vjp(cv)
        for n_, i_ in enumerate(ridx):
            val = grads[n_]
            if i_ in addends:
                val = val + ad[aidx.index(i_)][...].astype(f32)
            o[n_][...] = val.astype(o[n_].dtype)
        if pidx:
            @pl.when(pl.program_id(0) == 0)
            def _():
                for n_ in range(len(pidx)):
                    o[len(ridx) + n_][...] = jnp.zeros_like(o[len(ridx) + n_])
            for n_ in range(len(pidx)):
                o[len(ridx) + n_][...] += grads[len(ridx) + n_]

    row_spec = lambda x: pl.BlockSpec((tile, x.shape[1]), lambda i: (i, 0))
    in_specs = ([row_spec(x) for x in rows] + [_full_spec(p) for p in params] + [row_spec(x) for x in cts]
                + [row_spec(addends[i]) for i in aidx])
    out_specs = [row_spec(rows[i]) for i in ridx] + [_full_spec(params[i]) for i in pidx]
    out_shape = ([jax.ShapeDtypeStruct(rows[i].shape, rgrad[i]) for i in ridx]
                 + [jax.ShapeDtypeStruct(params[i].shape, f32) for i in pidx])
    res = pl.pallas_call(body, grid=(length // tile,), in_specs=in_specs, out_specs=out_specs, out_shape=out_shape,
                         compiler_params=_cparams(("arbitrary",)), name=name)(
        *rows, *params, *cts, *[addends[i] for i in aidx])
    return list(res[:len(ridx)]), list(res[len(ridx):])


def _rms(x, g):
    return x * lax.rsqrt(jnp.mean(x * x, axis=-1, keepdims=True) + EPS) * g


def _rms_twice(x, g):
    y = _rms(x, g)
    return y, y


def _silu(x):
    return x * jax.nn.sigmoid(x)


def _shift_down(x, s):
    rows = lax.broadcasted_iota(jnp.int32, x.shape, 0)
    return jnp.where(rows >= s, pltpu.roll(x, s, axis=0), 0.0)


def _shift_up(x, s):
    n = x.shape[0]
    rows = lax.broadcasted_iota(jnp.int32, x.shape, 0)
    return jnp.where(rows < n - s, pltpu.roll(x, n - s, axis=0), 0.0)


_CONV_COLS = 128


def _conv_gate_fwd(u, cw, cb, *, name):
    length, two_f = u.shape
    ff = two_f // 2
    nb = ff // _CONV_COLS

    def body(ug, uv, wg, wv, bg, bv, o):
        def conv(x_ref, w_ref, b_ref):
            x = x_ref[...]
            return (w_ref[2:3, :] * x + w_ref[1:2, :] * _shift_down(x, 1) + w_ref[0:1, :] * _shift_down(x, 2)
                    + b_ref[...])
        o[...] = (_silu(conv(ug, wg, bg)) * conv(uv, wv, bv)).astype(o.dtype)

    blk = lambda r, off: pl.BlockSpec((r, _CONV_COLS), lambda j, _o=off: (0, j + _o))
    return pl.pallas_call(
        body, grid=(nb,),
        in_specs=[blk(length, 0), blk(length, nb), blk(CONV_W, 0), blk(CONV_W, nb), blk(1, 0), blk(1, nb)],
        out_specs=blk(length, 0), out_shape=jax.ShapeDtypeStruct((length, ff), bf16),
        compiler_params=_cparams(("parallel",)), name=name)(u, u, cw, cw, cb, cb)


def _conv_gate_bwd(u, cw, cb, da, *, name):
    length, two_f = u.shape
    ff = two_f // 2
    nb = ff // _CONV_COLS

    def body(ug, uv, wg, wv, bg, bv, da_ref, dug, duv, dwg, dwv, dbg, dbv):
        def conv(x, w_ref, b_ref):
            x1, x2 = _shift_down(x, 1), _shift_down(x, 2)
            return w_ref[2:3, :] * x + w_ref[1:2, :] * x1 + w_ref[0:1, :] * x2 + b_ref[...], x1, x2

        xg, xv = ug[...], uv[...]
        g, xg1, xg2 = conv(xg, wg, bg)
        v, xv1, xv2 = conv(xv, wv, bv)
        d = da_ref[...].astype(f32)
        sg = jax.nn.sigmoid(g)
        dg = d * v * (sg * (1.0 + g * (1.0 - sg)))
        dv = d * (g * sg)

        def back(dy, x, x1, x2, w_ref, du_ref, dw_ref, db_ref):
            du_ref[...] = (w_ref[2:3, :] * dy + w_ref[1:2, :] * _shift_up(dy, 1)
                           + w_ref[0:1, :] * _shift_up(dy, 2)).astype(du_ref.dtype)
            dw_ref[2:3, :] = jnp.sum(dy * x, axis=0, keepdims=True)
            dw_ref[1:2, :] = jnp.sum(dy * x1, axis=0, keepdims=True)
            dw_ref[0:1, :] = jnp.sum(dy * x2, axis=0, keepdims=True)
            db_ref[...] = jnp.sum(dy, axis=0, keepdims=True)

        back(dg, xg, xg1, xg2, wg, dug, dwg, dbg)
        back(dv, xv, xv1, xv2, wv, duv, dwv, dbv)

    blk = lambda r, off: pl.BlockSpec((r, _CONV_COLS), lambda j, _o=off: (0, j + _o))
    sds = jax.ShapeDtypeStruct
    dug, duv, dwg, dwv, dbg, dbv = pl.pallas_call(
        body, grid=(nb,),
        in_specs=[blk(length, 0), blk(length, nb), blk(CONV_W, 0), blk(CONV_W, nb), blk(1, 0), blk(1, nb),
                  blk(length, 0)],
        out_specs=[blk(length, 0), blk(length, 0), blk(CONV_W, 0), blk(CONV_W, 0), blk(1, 0), blk(1, 0)],
        out_shape=[sds((length, ff), bf16), sds((length, ff), bf16), sds((CONV_W, ff), f32), sds((CONV_W, ff), f32),
                   sds((1, ff), f32), sds((1, ff), f32)],
        compiler_params=_cparams(("parallel",)), name=name)(u, u, cw, cw, cb, cb, da)
    return dug, duv, jnp.concatenate([dwg, dwv], axis=1), jnp.concatenate([dbg, dbv], axis=1)


def _ffn_fwd(h, w, tag):
    hf, = _rows(_rms, [h], [w["norm_ffn"]], [(h.shape[1], bf16)], name=f"ffn_norm_{tag}")
    u = _mm(hf, w["ffn_w_up"], name=f"ffn_up_{tag}")
    a = _conv_gate_fwd(u, w["ffn_conv_w"], w["ffn_conv_b"], name=f"ffn_conv_{tag}")
    out = _mm(a, w["ffn_w_down"], add=h, name=f"ffn_down_{tag}")
    return out, (h, hf, u, a)


def _ffn_bwd(dout, w, saved, tag):
    h, hf, u, a = saved
    ff = a.shape[1]
    da = _mm(dout, w["ffn_w_down"], tb=True, out_dtype=bf16, name=f"ffn_da_{tag}")
    g = {"ffn_w_down": _mm(a, dout, ta=True, name=f"ffn_dwdown_{tag}")}
    dug, duv, g["ffn_conv_w"], g["ffn_conv_b"] = _conv_gate_bwd(u, w["ffn_conv_w"], w["ffn_conv_b"], da,
                                                                name=f"ffn_dconv_{tag}")
    dhf = _mm(dug, w["ffn_w_up"], tb=True, kslab=(0, 2), name=f"ffn_dhf_g_{tag}")
    dhf = _mm(duv, w["ffn_w_up"], tb=True, kslab=(1, 2), add=dhf, out_dtype=bf16, name=f"ffn_dhf_v_{tag}")
    g["ffn_w_up"] = _mm(hf, dug, ta=True, b2=duv, name=f"ffn_dwup_{tag}")
    (dh,), (g["norm_ffn"],) = _rows_bwd(_rms, [h], [w["norm_ffn"]], [dhf], rgrad=[f32], pgrad=[True],
                                        addends={0: dout}, name=f"ffn_dnorm_{tag}")
    return dh, g


def _xattn_fn(qx, kx, vx, qg, kg):
    outs = []
    for hh in range(XA_HEADS):
        sl = slice(hh * XA_DIM, (hh + 1) * XA_DIM)
        q = _rms(qx[:, sl], qg).astype(bf16)
        k = _rms(kx[:, sl], kg).astype(bf16)
        s = lax.dot_general(q, k, _NT, preferred_element_type=f32) * (XA_DIM ** -0.5)
        s = s - jnp.max(s, axis=-1, keepdims=True)
        p = jnp.exp(s)
        p = p / jnp.sum(p, axis=-1, keepdims=True)
        outs.append(jnp.dot(p.astype(bf16), vx[:, sl].astype(bf16), preferred_element_type=f32))
    return jnp.concatenate(outs, axis=-1)


def _xattn_fwd(h, mem, w, tag):
    d = h.shape[1]
    hx, = _rows(_rms, [h], [w["norm_xa"]], [(d, bf16)], name=f"xa_norm_{tag}")
    qx = _mm(hx, w["xa_wq"], name=f"xa_q_{tag}")
    m, = _rows(_rms, [mem], [w["norm_mem"]], [(d, bf16)], name=f"xa_mnorm_{tag}")
    kx = _mm(m, w["xa_wk"], name=f"xa_k_{tag}")
    vx = _mm(m, w["xa_wv"], name=f"xa_v_{tag}")
    o, = _rows(_xattn_fn, [qx], [kx, vx, w["xa_q_norm"], w["xa_k_norm"]], [(d, bf16)], name=f"xa_attn_{tag}")
    out = _mm(o, w["xa_wo"], add=h, name=f"xa_o_{tag}")
    return out, (h, hx, qx, m, kx, vx, o)


def _xattn_bwd(dout, mem, w, saved, tag):
    h, hx, qx, m, kx, vx, o = saved
    g = {}
    do = _mm(dout, w["xa_wo"], tb=True, out_dtype=bf16, name=f"xa_do_{tag}")
    g["xa_wo"] = _mm(o, dout, ta=True, name=f"xa_dwo_{tag}")
    (dqx,), (dkx, dvx, g["xa_q_norm"], g["xa_k_norm"]) = _rows_bwd(
        _xattn_fn, [qx], [kx, vx, w["xa_q_norm"], w["xa_k_norm"]], [do], rgrad=[bf16], pgrad=[True] * 4,
        name=f"xa_dattn_{tag}")
    dhx = _mm(dqx, w["xa_wq"], tb=True, out_dtype=bf16, name=f"xa_dhx_{tag}")
    g["xa_wq"] = _mm(hx, dqx, ta=True, name=f"xa_dwq_{tag}")
    (dh,), (g["norm_xa"],) = _rows_bwd(_rms, [h], [w["norm_xa"]], [dhx], rgrad=[f32], pgrad=[True],
                                       addends={0: dout}, name=f"xa_dnorm_{tag}")
    dm = _mm(dkx, w["xa_wk"], tb=True, name=f"xa_dm_k_{tag}")
    dm = _mm(dvx, w["xa_wv"], tb=True, add=dm, name=f"xa_dm_v_{tag}")
    g["xa_wk"] = _mm(m, dkx, ta=True, name=f"xa_dwk_{tag}")
    g["xa_wv"] = _mm(m, dvx, ta=True, name=f"xa_dwv_{tag}")
    _, (g["norm_mem"],) = _rows_bwd(_rms, [mem], [w["norm_mem"]], [dm], rgrad=[None], pgrad=[True],
                                    name=f"xa_dmnorm_{tag}")
    return dh, g


_HG_GROUP = 4


def _hg_chunk(q, k, v, g, st):
    c = q.shape[0]
    tri = (lax.broadcasted_iota(jnp.int32, (c, c), 0) >= lax.broadcasted_iota(jnp.int32, (c, c), 1)).astype(f32)
    b = jnp.dot(tri, g, precision=lax.Precision.HIGHEST, preferred_element_type=f32)
    bend = jnp.sum(g, axis=0, keepdims=True)
    o_inter = lax.dot_general((q * jnp.exp(b)).astype(bf16), st.astype(bf16), _NT, preferred_element_type=f32)
    kd = k * jnp.exp(bend - b)
    st_new = st * jnp.exp(bend) + lax.dot_general(v.astype(bf16), kd.astype(bf16), _TN, preferred_element_type=f32)
    outs = []
    for i in range(c // HG_SUB):
        lo, n = HG_SUB * i, HG_SUB * (i + 1)
        ref = jnp.sum(g[:lo], axis=0, keepdims=True) if i else jnp.zeros((1, g.shape[1]), f32)
        qh = q[lo:n] * jnp.exp(b[lo:n] - ref)
        kh = k[:n] * jnp.exp(ref - b[:n])
        a = lax.dot_general(qh.astype(bf16), kh.astype(bf16), _NT, preferred_element_type=f32)
        keep = (lax.broadcasted_iota(jnp.int32, (HG_SUB, n), 1)
                <= lo + lax.broadcasted_iota(jnp.int32, (HG_SUB, n), 0))
        a = jnp.where(keep, a, 0.0)
        outs.append(jnp.dot(a.astype(bf16), v[:n].astype(bf16), preferred_element_type=f32))
    return jnp.concatenate(outs, axis=0) + o_inter, st_new


def _hg_fwd(q, k, v, g, *, name):
    length = q.shape[0]
    rows = _HG_GROUP * HG_CHUNK
    ng = length // rows
    nc = length // HG_CHUNK

    def body(q_ref, k_ref, v_ref, g_ref, o_ref, st_ref, state):
        @pl.when(pl.program_id(1) == 0)
        def _():
            state[...] = jnp.zeros_like(state)

        for ci in range(_HG_GROUP):
            sl = slice(ci * HG_CHUNK, (ci + 1) * HG_CHUNK)
            st = state[...]
            st_ref[0, ci] = st
            o, st_new = _hg_chunk(q_ref[sl, :], k_ref[sl, :], v_ref[sl, :], g_ref[sl, :], st)
            o_ref[sl, :] = o
            state[...] = st_new

    blk = pl.BlockSpec((rows, HG_DIM), lambda h, c: (c, h))
    return pl.pallas_call(
        body, grid=(HG_HEADS, ng), in_specs=[blk] * 4,
        out_specs=[blk, pl.BlockSpec((1, _HG_GROUP, HG_DIM, HG_DIM), lambda h, c: (h, c, 0, 0))],
        out_shape=[jax.ShapeDtypeStruct((length, HG_WIDTH), f32),
                   jax.ShapeDtypeStruct((HG_HEADS, nc, HG_DIM, HG_DIM), f32)],
        scratch_shapes=[pltpu.VMEM((HG_DIM, HG_DIM), f32)],
        compiler_params=_cparams(("parallel", "arbitrary")), name=name)(q, k, v, g)


def _hg_bwd(q, k, v, g, states, do, *, name):
    length = q.shape[0]
    rows = _HG_GROUP * HG_CHUNK
    ng = length // rows

    def body(q_ref, k_ref, v_ref, g_ref, st_ref, do_ref, dq_ref, dk_ref, dv_ref, dg_ref, dstate):
        @pl.when(pl.program_id(1) == 0)
        def _():
            dstate[...] = jnp.zeros_like(dstate)

        for ci in reversed(range(_HG_GROUP)):
            sl = slice(ci * HG_CHUNK, (ci + 1) * HG_CHUNK)
            _, vjp = jax.vjp(_hg_chunk, q_ref[sl, :], k_ref[sl, :], v_ref[sl, :], g_ref[sl, :], st_ref[0, ci])
            dq, dk, dv, dg, dst = vjp((do_ref[sl, :], dstate[...]))
            dq_ref[sl, :] = dq
            dk_ref[sl, :] = dk
            dv_ref[sl, :] = dv
            dg_ref[sl, :] = dg
            dstate[...] = dst

    blk = pl.BlockSpec((rows, HG_DIM), lambda h, c: (ng - 1 - c, h))
    sds = jax.ShapeDtypeStruct((length, HG_WIDTH), f32)
    return pl.pallas_call(
        body, grid=(HG_HEADS, ng),
        in_specs=[blk] * 4 + [pl.BlockSpec((1, _HG_GROUP, HG_DIM, HG_DIM), lambda h, c: (h, ng - 1 - c, 0, 0)), blk],
        out_specs=[blk] * 4, out_shape=[sds] * 4,
        scratch_shapes=[pltpu.VMEM((HG_DIM, HG_DIM), f32)],
        compiler_params=_cparams(("parallel", "arbitrary")), name=name)(q, k, v, g, states, do)


_ATT_BLK = 256
_ATT_SCALE = MLA_QK ** -0.5
_NEG = -1e30


def _att_mask(i, j, t):
    rows = i * t + lax.broadcasted_iota(jnp.int32, (t, t), 0)
    cols = j * t + lax.broadcasted_iota(jnp.int32, (t, t), 1)
    return cols <= rows


def _att_fwd(q, k, v, *, name):
    length = q.shape[0]
    t = min(_ATT_BLK, length)
    nq = length // t

    def body(q_ref, k_ref, v_ref, o_ref, lse_ref):
        i = pl.program_id(1)
        qb = q_ref[...]

        def step(j, carry):
            m, l, acc = carry
            off = pl.multiple_of(j * t, t)
            ks = k_ref[pl.ds(off, t), :]
            vs = v_ref[pl.ds(off, t), :]
            s = lax.dot_general(qb, ks, _NT, preferred_element_type=f32) * _ATT_SCALE
            s = jnp.where(_att_mask(i, j, t), s, _NEG)
            m_new = jnp.maximum(m, jnp.max(s, axis=-1, keepdims=True))
            alpha = jnp.exp(m - m_new)
            p = jnp.exp(s - m_new)
            l = alpha * l + jnp.sum(p, axis=-1, keepdims=True)
            acc = alpha * acc + jnp.dot(p.astype(bf16), vs, preferred_element_type=f32)
            return m_new, l, acc

        init = (jnp.full((t, 1), _NEG, f32), jnp.zeros((t, 1), f32), jnp.zeros((t, MLA_V), f32))
        m, l, acc = lax.fori_loop(0, i + 1, step, init)
        o_ref[...] = (acc / l).astype(o_ref.dtype)
        lse_ref[...] = jnp.broadcast_to(m + jnp.log(l), lse_ref.shape)

    return pl.pallas_call(
        body, grid=(MLA_HEADS, nq),
        in_specs=[pl.BlockSpec((t, MLA_QK_PAD), lambda h, i: (i, h)),
                  pl.BlockSpec((length, MLA_QK_PAD), lambda h, i: (0, h)),
                  pl.BlockSpec((length, MLA_V), lambda h, i: (0, h))],
        out_specs=[pl.BlockSpec((t, MLA_V), lambda h, i: (i, h))] * 2,
        out_shape=[jax.ShapeDtypeStruct((length, MLA_HEADS * MLA_V), bf16),
                   jax.ShapeDtypeStruct((length, MLA_HEADS * MLA_V), f32)],
        compiler_params=_cparams(("parallel", "arbitrary")), name=name)(q, k, v)


def _att_bwd(q, k, v, o, lse, do, *, name):
    length = q.shape[0]
    t = min(_ATT_BLK, length)
    nq = length // t

    def dq_body(q_ref, k_ref, v_ref, o_ref, lse_ref, do_ref, dq_ref, delta_ref):
        i = pl.program_id(1)
        qb = q_ref[...]
        dob = do_ref[...]
        lse_b = lse_ref[:, 0:1]
        delta = jnp.sum(dob.astype(f32) * o_ref[...].astype(f32), axis=-1, keepdims=True)

        def step(j, dq):
            off = pl.multiple_of(j * t, t)
            ks = k_ref[pl.ds(off, t), :]
            vs = v_ref[pl.ds(off, t), :]
            s = lax.dot_general(qb, ks, _NT, preferred_element_type=f32) * _ATT_SCALE
            p = jnp.where(_att_mask(i, j, t), jnp.exp(s - lse_b), 0.0)
            dp = lax.dot_general(dob, vs, _NT, preferred_element_type=f32)
            ds = p * (dp - delta) * _ATT_SCALE
            return dq + jnp.dot(ds.astype(bf16), ks, preferred_element_type=f32)

        dq = lax.fori_loop(0, i + 1, step, jnp.zeros((t, MLA_QK_PAD), f32))
        dq_ref[...] = dq.astype(dq_ref.dtype)
        delta_ref[...] = jnp.broadcast_to(delta, delta_ref.shape)

    qblk = pl.BlockSpec((t, MLA_QK_PAD), lambda h, i: (i, h))
    vblk = pl.BlockSpec((t, MLA_V), lambda h, i: (i, h))
    qfull = pl.BlockSpec((length, MLA_QK_PAD), lambda h, i: (0, h))
    vfull = pl.BlockSpec((length, MLA_V), lambda h, i: (0, h))
    dq, delta = pl.pallas_call(
        dq_body, grid=(MLA_HEADS, nq), in_specs=[qblk, qfull, vfull, vblk, vblk, vblk],
        out_specs=[qblk, vblk],
        out_shape=[jax.ShapeDtypeStruct(q.shape, bf16), jax.ShapeDtypeStruct(lse.shape, f32)],
        compiler_params=_cparams(("parallel", "arbitrary")), name=name + "_dq")(q, k, v, o, lse, do)

    def dkv_body(k_ref, v_ref, q_ref, do_ref, lse_ref, delta_ref, dk_ref, dv_ref):
        j = pl.program_id(1)
        kb = k_ref[...]
        vb = v_ref[...]

        def step(i, carry):
            dk, dv = carry
            off = pl.multiple_of(i * t, t)
            qs = q_ref[pl.ds(off, t), :]
            dos = do_ref[pl.ds(off, t), :]
            lse_i = lse_ref[pl.ds(off, t), 0:1]
            delta_i = delta_ref[pl.ds(off, t), 0:1]
            s = lax.dot_general(qs, kb, _NT, preferred_element_type=f32) * _ATT_SCALE
            p = jnp.where(_att_mask(i, j, t), jnp.exp(s - lse_i), 0.0)
            dv = dv + lax.dot_general(p.astype(bf16), dos, _TN, preferred_element_type=f32)
            dp = lax.dot_general(dos, vb, _NT, preferred_element_type=f32)
            ds = p * (dp - delta_i) * _ATT_SCALE
            dk = dk + lax.dot_general(ds.astype(bf16), qs, _TN, preferred_element_type=f32)
            return dk, dv

        dk, dv = lax.fori_loop(j, nq, step, (jnp.zeros((t, MLA_QK_PAD), f32), jnp.zeros((t, MLA_V), f32)))
        dk_ref[...] = dk.astype(dk_ref.dtype)
        dv_ref[...] = dv.astype(dv_ref.dtype)

    dk, dv = pl.pallas_call(
        dkv_body, grid=(MLA_HEADS, nq), in_specs=[qblk, vblk, qfull, vfull, vfull, vfull],
        out_specs=[qblk, vblk],
        out_shape=[jax.ShapeDtypeStruct(k.shape, bf16), jax.ShapeDtypeStruct(v.shape, bf16)],
        compiler_params=_cparams(("parallel", "arbitrary")), name=name + "_dkv")(k, v, q, do, lse, delta)
    return dq, dk, dv


_C_Q = 4 * HG_WIDTH
_C_KV = _C_Q + MLA_Q_RANK
_C_KPE = _C_KV + MLA_KV_RANK


def _rms_n(x, g, n):
    return x * lax.rsqrt(jnp.sum(x * x, axis=-1, keepdims=True) * (1.0 / n) + EPS) * g


def _mix_a(proj, l0, l1, q_a_norm, kv_a_norm):
    lb = jax.nn.sigmoid(l0 - l1)
    f = lb + (1.0 - lb) * jax.nn.sigmoid(proj[:, HG_WIDTH:2 * HG_WIDTH])
    qf = _silu(proj[:, :HG_WIDTH])
    v = proj[:, 2 * HG_WIDTH:3 * HG_WIDTH]
    cqn = _rms(proj[:, _C_Q:_C_KV], q_a_norm)
    ckvn = _rms(proj[:, _C_KV:_C_KPE], kv_a_norm)
    return qf, 1.0 - f, v, jnp.log(f), cqn, ckvn


def _mix_b(qraw, kvraw, proj, cos, sin, qn_nope, qn_rope, kn_nope, kn_rope, perm):
    def rope(x):
        return x * cos + jnp.dot(x, perm, precision=lax.Precision.HIGHEST, preferred_element_type=f32) * sin

    kpe = rope(_rms_n(proj[:, _C_KPE:], kn_rope, MLA_ROPE))
    qs, ks, vs = [], [], []
    for hh in range(MLA_HEADS):
        base = hh * MLA_QK_PAD
        qs.append(_rms(qraw[:, base:base + MLA_NOPE], qn_nope))
        qs.append(rope(_rms_n(qraw[:, base + MLA_NOPE:base + MLA_QK_PAD], qn_rope, MLA_ROPE)))
        ks.append(_rms(kvraw[:, base:base + MLA_NOPE], kn_nope))
        ks.append(kpe)
        vs.append(kvraw[:, base + MLA_NOPE:base + MLA_QK_PAD])
    return jnp.concatenate(qs, axis=-1), jnp.concatenate(ks, axis=-1), jnp.concatenate(vs, axis=-1)


def _mix_c(o_hg, proj, o_mla, hg_out_norm):
    parts = []
    for hh in range(HG_HEADS):
        sl = slice(hh * HG_DIM, (hh + 1) * HG_DIM)
        parts.append(_rms(o_hg[:, sl], hg_out_norm[:, sl]))
    o = jnp.concatenate(parts, axis=-1) * _silu(proj[:, 3 * HG_WIDTH:4 * HG_WIDTH])
    return jnp.concatenate([o, o_mla], axis=-1)


def _rope_perm():
    p = np.zeros((128, 128), np.float32)
    half = MLA_ROPE // 2
    for i in range(half):
        p[i + half, i] = -1.0
        p[i, i + half] = 1.0
    return jnp.asarray(p)


def _mixer_fwd(h, cos, sin, w, tag):
    d = h.shape[1]
    hn, = _rows(_rms, [h], [w["norm_mix"]], [(d, bf16)], name=f"mix_norm_{tag}")
    proj = _mm(hn, w["mix_w_in"], name=f"mix_in_{tag}")
    pa = [w["lb0"], w["lb1"], w["mla_q_a_norm"], w["mla_kv_a_norm"]]
    qf, kk, vv, logf, cqn, ckvn = _rows(
        _mix_a, [proj], pa, [(HG_WIDTH, f32)] * 4 + [(MLA_Q_RANK, bf16), (MLA_KV_RANK, bf16)], name=f"mix_a_{tag}")
    o_hg, states = _hg_fwd(qf, kk, vv, logf, name=f"hg_fwd_{tag}")
    qraw = _mm(cqn, w["mla_w_uq"], name=f"mla_uq_{tag}")
    kvraw = _mm(ckvn, w["mla_w_ukv"], name=f"mla_ukv_{tag}")
    pb = [w["mla_qn_nope"], w["mla_qn_rope"], w["mla_kn_nope"], w["mla_kn_rope"], w["rope_perm"]]
    qfull, kfull, vfull = _rows(_mix_b, [qraw, kvraw, proj, cos, sin], pb,
                                [(MLA_HEADS * MLA_QK_PAD, bf16)] * 2 + [(MLA_HEADS * MLA_V, bf16)],
                                name=f"mix_b_{tag}")
    o_mla, lse = _att_fwd(qfull, kfull, vfull, name=f"att_fwd_{tag}")
    mixin, = _rows(_mix_c, [o_hg, proj, o_mla], [w["hg_out_norm"]], [(d, bf16)], name=f"mix_c_{tag}")
    out = _mm(mixin, w["mix_w_out"], add=h, name=f"mix_out_{tag}")
    return out, (h, hn, proj, qf, kk, vv, logf, cqn, ckvn, o_hg, states, qraw, kvraw, qfull, kfull, vfull, o_mla,
                 lse, mixin)


def _mixer_bwd(dout, cos, sin, w, saved, tag, on_w_out=None):
    (h, hn, proj, qf, kk, vv, logf, cqn, ckvn, o_hg, states, qraw, kvraw, qfull, kfull, vfull, o_mla, lse,
     mixin) = saved
    g = {}
    dmixin = _mm(dout, w["mix_w_out"], tb=True, name=f"mix_dmixin_{tag}")
    g["mix_w_out"] = _mm(mixin, dout, ta=True, name=f"mix_dwout_{tag}")
    if on_w_out is not None:
        dmixin = on_w_out(g["mix_w_out"], dmixin)
    (do_hg, dproj_c, do_mla), (g["hg_out_norm"],) = _rows_bwd(
        _mix_c, [o_hg, proj, o_mla], [w["hg_out_norm"]], [dmixin], rgrad=[f32, f32, bf16], pgrad=[True],
        name=f"mix_dc_{tag}")
    dqfull, dkfull, dvfull = _att_bwd(qfull, kfull, vfull, o_mla, lse, do_mla, name=f"att_bwd_{tag}")
    pb = [w["mla_qn_nope"], w["mla_qn_rope"], w["mla_kn_nope"], w["mla_kn_rope"], w["rope_perm"]]
    (dqraw, dkvraw, dproj_b), pg = _rows_bwd(
        _mix_b, [qraw, kvraw, proj, cos, sin], pb, [dqfull, dkfull, dvfull],
        rgrad=[bf16, bf16, f32, None, None], pgrad=[True, True, True, True, False], addends={2: dproj_c},
        name=f"mix_db_{tag}")
    g["mla_qn_nope"], g["mla_qn_rope"], g["mla_kn_nope"], g["mla_kn_rope"] = pg
    dcqn = _mm(dqraw, w["mla_w_uq"], tb=True, name=f"mla_dcq_{tag}")
    g["mla_w_uq"] = _mm(cqn, dqraw, ta=True, name=f"mla_dwuq_{tag}")
    dckvn = _mm(dkvraw, w["mla_w_ukv"], tb=True, name=f"mla_dckv_{tag}")
    g["mla_w_ukv"] = _mm(ckvn, dkvraw, ta=True, name=f"mla_dwukv_{tag}")
    dqf, dkk, dvv, dlogf = _hg_bwd(qf, kk, vv, logf, states, do_hg, name=f"hg_bwd_{tag}")
    pa = [w["lb0"], w["lb1"], w["mla_q_a_norm"], w["mla_kv_a_norm"]]
    (dproj,), (g["lb0"], g["lb1"], g["mla_q_a_norm"], g["mla_kv_a_norm"]) = _rows_bwd(
        _mix_a, [proj], pa, [dqf, dkk, dvv, dlogf, dcqn, dckvn], rgrad=[bf16], pgrad=[True] * 4,
        addends={0: dproj_b}, name=f"mix_da_{tag}")
    dhn = _mm(dproj, w["mix_w_in"], tb=True, out_dtype=bf16, name=f"mix_dhn_{tag}")
    g["mix_w_in"] = _mm(hn, dproj, ta=True, name=f"mix_dwin_{tag}")
    (dh,), (g["norm_mix"],) = _rows_bwd(_rms, [h], [w["norm_mix"]], [dhn], rgrad=[f32], pgrad=[True],
                                        addends={0: dout}, name=f"mix_dnorm_{tag}")
    return dh, g


def _rope_tables(positions):
    inv_freq = 1.0 / (ROPE_BASE ** (jnp.arange(0, MLA_ROPE, 2, dtype=f32) / MLA_ROPE))
    ang = positions.astype(f32)[:, None] * inv_freq
    z = jnp.zeros((positions.shape[0], 128 - MLA_ROPE), f32)
    return (jnp.concatenate([jnp.cos(ang), jnp.cos(ang), z], axis=1),
            jnp.concatenate([jnp.sin(ang), jnp.sin(ang), z], axis=1))


def _pad_cols(a, n):
    return jnp.pad(a, ((0, 0), (0, n - a.shape[1])))


def _even_weights(p, j, layer, dt):
    w_uq = p["mla_w_uq"][j].reshape(MLA_Q_RANK, MLA_HEADS, MLA_QK)
    w_uq = jnp.pad(w_uq, ((0, 0), (0, 0), (0, MLA_QK_PAD - MLA_QK))).reshape(MLA_Q_RANK, MLA_HEADS * MLA_QK_PAD)
    return dict(
        norm_mix=p["norm_mix"][layer][None], mix_w_in=_pad_cols(p["mix_w_in"][j], IN_PAD).astype(dt),
        lb0=p["hg_lb_logits"][0][None], lb1=p["hg_lb_logits"][1][None],
        mla_q_a_norm=p["mla_q_a_norm"][j][None], mla_kv_a_norm=p["mla_kv_a_norm"][j][None],
        mla_w_uq=w_uq.astype(dt), mla_w_ukv=p["mla_w_ukv"][j].astype(dt),
        mla_qn_nope=p["mla_qn_nope"][j][None], mla_qn_rope=_pad_cols(p["mla_qn_rope"][j][None], 128),
        mla_kn_nope=p["mla_kn_nope"][j][None], mla_kn_rope=_pad_cols(p["mla_kn_rope"][j][None], 128),
        rope_perm=_rope_perm(), hg_out_norm=p["hg_out_norm"][j][None], mix_w_out=p["mix_w_out"][j].astype(dt))


def _even_grads(g):
    w_uq = g["mla_w_uq"].reshape(MLA_Q_RANK, MLA_HEADS, MLA_QK_PAD)[:, :, :MLA_QK].reshape(MLA_Q_RANK, -1)
    return dict(
        norm_mix=g["norm_mix"], mix_w_in=g["mix_w_in"][:, :IN_WIDTH][None],
        hg_lb_logits=jnp.concatenate([g["lb0"], g["lb1"]], axis=0),
        mla_q_a_norm=g["mla_q_a_norm"], mla_kv_a_norm=g["mla_kv_a_norm"], mla_w_uq=w_uq[None],
        mla_w_ukv=g["mla_w_ukv"][None], mla_qn_nope=g["mla_qn_nope"], mla_qn_rope=g["mla_qn_rope"][:, :MLA_ROPE],
        mla_kn_nope=g["mla_kn_nope"], mla_kn_rope=g["mla_kn_rope"][:, :MLA_ROPE],
        hg_out_norm=g["hg_out_norm"], mix_w_out=g["mix_w_out"][None])


_S5_NB = 8
_S5_BW = 1024
_S5_HALF = 512
_S5_UC = 128
_S5_TIME = 512


def _bd_mm(a, b3, *, name, tb=False, out_dtype=f32):
    length = a.shape[0]
    rows_b, cols_b = b3.shape[0] // _S5_NB, b3.shape[1]
    ka, n = (cols_b, rows_b) if tb else (rows_b, cols_b)
    bm = _pick(length, (512, 256, 128))
    dims = _NT if tb else _NN

    def body(a_ref, b_ref, o_ref):
        o_ref[...] = lax.dot_general(a_ref[...].astype(bf16), b_ref[...].astype(bf16), dims,
                                     preferred_element_type=f32).astype(o_ref.dtype)

    return pl.pallas_call(
        body, grid=(length // bm, _S5_NB),
        in_specs=[pl.BlockSpec((bm, ka), lambda i, j: (i, j)), pl.BlockSpec((rows_b, cols_b), lambda i, j: (j, 0))],
        out_specs=pl.BlockSpec((bm, n), lambda i, j: (i, j)),
        out_shape=jax.ShapeDtypeStruct((length, _S5_NB * n), out_dtype),
        compiler_params=_cparams(("parallel", "parallel")), name=name)(a, b3)


def _bd_mm_tn(a, c, *, name):
    length = a.shape[0]
    ka, n = a.shape[1] // _S5_NB, c.shape[1] // _S5_NB
    bk = _pick(length, (512, 256, 128))
    nk = length // bk

    def body(a_ref, c_ref, o_ref):
        @pl.when(pl.program_id(1) == 0)
        def _():
            o_ref[...] = jnp.zeros_like(o_ref)
        o_ref[...] += lax.dot_general(a_ref[...].astype(bf16), c_ref[...].astype(bf16), _TN,
                                      preferred_element_type=f32)

    return pl.pallas_call(
        body, grid=(_S5_NB, nk),
        in_specs=[pl.BlockSpec((bk, ka), lambda j, q: (q, j)), pl.BlockSpec((bk, n), lambda j, q: (q, j))],
        out_specs=pl.BlockSpec((ka, n), lambda j, q: (j, 0)),
        out_shape=jax.ShapeDtypeStruct((_S5_NB * ka, n), f32),
        compiler_params=_cparams(("parallel", "arbitrary")), name=name)(a, c)


def _cmul(ar, ai, br, bi):
    return ar * br - ai * bi, ar * bi + ai * br


def _pow_table(ar, ai, descending):
    rows = lax.broadcasted_iota(jnp.int32, (8, ar.shape[1]), 0)
    tr = jnp.zeros((8, ar.shape[1]), f32)
    ti = jnp.zeros((8, ar.shape[1]), f32)
    pr, pi_ = ar, ai
    for r in range(8):
        sel = rows == ((7 - r) if descending else r)
        tr = jnp.where(sel, pr, tr)
        ti = jnp.where(sel, pi_, ti)
        pr, pi_ = _cmul(pr, pi_, ar, ai)
    return tr, ti


def _s5_scan_fwd(a, x, *, name):
    length = x.shape[0]
    tc = min(_S5_TIME, length)
    hw = _S5_HALF

    def body(a_ref, x_ref, o_ref, carry):
        @pl.when(pl.program_id(1) == 0)
        def _():
            carry[...] = jnp.zeros_like(carry)

        ar, ai = a_ref[:, :hw], a_ref[:, hw:]
        xr, xi = x_ref[:, :hw], x_ref[:, hw:]
        row8 = lax.broadcasted_iota(jnp.int32, (tc, hw), 0) & 7
        mr, mi = ar, ai
        for s in (1, 2, 4):
            sr, si = pltpu.roll(xr, s, axis=0), pltpu.roll(xi, s, axis=0)
            pr, pi_ = _cmul(mr, mi, sr, si)
            ok = row8 >= s
            xr = xr + jnp.where(ok, pr, 0.0)
            xi = xi + jnp.where(ok, pi_, 0.0)
            mr, mi = _cmul(mr, mi, mr, mi)
        o_ref[:, :hw] = xr
        o_ref[:, hw:] = xi
        tr, ti = _pow_table(ar, ai, False)
        cr, ci = carry[:, :hw], carry[:, hw:]
        for i in range(tc // 8):
            sl = slice(8 * i, 8 * i + 8)
            pr, pi_ = _cmul(tr, ti, cr, ci)
            o_ref[sl, :hw] = o_ref[sl, :hw] + pr
            o_ref[sl, hw:] = o_ref[sl, hw:] + pi_
            cr, ci = o_ref[8 * i + 7:8 * i + 8, :hw], o_ref[8 * i + 7:8 * i + 8, hw:]
        carry[:, :hw] = cr
        carry[:, hw:] = ci

    return pl.pallas_call(
        body, grid=(_S5_NB, length // tc),
        in_specs=[pl.BlockSpec((1, _S5_BW), lambda j, t: (0, j)), pl.BlockSpec((tc, _S5_BW), lambda j, t: (t, j))],
        out_specs=pl.BlockSpec((tc, _S5_BW), lambda j, t: (t, j)),
        out_shape=jax.ShapeDtypeStruct(x.shape, f32),
        scratch_shapes=[pltpu.VMEM((1, _S5_BW), f32)],
        compiler_params=_cparams(("parallel", "arbitrary")), name=name)(a, x)


def _s5_scan_bwd(a, hs, d, *, name):
    length = d.shape[0]
    tc = min(_S5_TIME, length)
    nt = length // tc
    hw = _S5_HALF

    def body(a_ref, h_ref, d_ref, g_ref, da_ref, carry):
        @pl.when(pl.program_id(1) == 0)
        def _():
            carry[...] = jnp.zeros_like(carry)
            da_ref[...] = jnp.zeros_like(da_ref)

        ar, ai = a_ref[:, :hw], -a_ref[:, hw:]
        xr, xi = d_ref[:, :hw], d_ref[:, hw:]
        rows = lax.broadcasted_iota(jnp.int32, (tc, hw), 0)
        row8 = rows & 7
        mr, mi = ar, ai
        for s in (1, 2, 4):
            sr, si = pltpu.roll(xr, tc - s, axis=0), pltpu.roll(xi, tc - s, axis=0)
            pr, pi_ = _cmul(mr, mi, sr, si)
            ok = row8 < 8 - s
            xr = xr + jnp.where(ok, pr, 0.0)
            xi = xi + jnp.where(ok, pi_, 0.0)
            mr, mi = _cmul(mr, mi, mr, mi)
        g_ref[:, :hw] = xr
        g_ref[:, hw:] = xi
        tr, ti = _pow_table(ar, ai, True)
        cr0, ci0 = carry[:, :hw], carry[:, hw:]
        cr, ci = cr0, ci0
        for i in reversed(range(tc // 8)):
            sl = slice(8 * i, 8 * i + 8)
            pr, pi_ = _cmul(tr, ti, cr, ci)
            g_ref[sl, :hw] = g_ref[sl, :hw] + pr
            g_ref[sl, hw:] = g_ref[sl, hw:] + pi_
            cr, ci = g_ref[8 * i:8 * i + 1, :hw], g_ref[8 * i:8 * i + 1, hw:]
        carry[:, :hw] = cr
        carry[:, hw:] = ci
        last = rows == tc - 1
        gnr = jnp.where(last, cr0, pltpu.roll(g_ref[:, :hw], tc - 1, axis=0))
        gni = jnp.where(last, ci0, pltpu.roll(g_ref[:, hw:], tc - 1, axis=0))
        hr, hi = h_ref[:, :hw], h_ref[:, hw:]
        da_ref[:, :hw] += jnp.sum(hr * gnr + hi * gni, axis=0, keepdims=True)
        da_ref[:, hw:] += jnp.sum(hr * gni - hi * gnr, axis=0, keepdims=True)

    blk = pl.BlockSpec((tc, _S5_BW), lambda j, t: (nt - 1 - t, j))
    row = pl.BlockSpec((1, _S5_BW), lambda j, t: (0, j))
    return pl.pallas_call(
        body, grid=(_S5_NB, nt), in_specs=[row, blk, blk], out_specs=[blk, row],
        out_shape=[jax.ShapeDtypeStruct(d.shape, f32), jax.ShapeDtypeStruct((1, _S5_NB * _S5_BW), f32)],
        scratch_shapes=[pltpu.VMEM((1, _S5_BW), f32)],
        compiler_params=_cparams(("parallel", "arbitrary")), name=name)(a, hs, d)


def _s5_disc(lr, li, ldt, btr, bti, expand):
    dt = jnp.exp(ldt)
    mag = jnp.exp(lr * dt)
    abr = mag * jnp.cos(li * dt)
    abi = mag * jnp.sin(li * dt)
    den = lr * lr + li * li
    zr = ((abr - 1.0) * lr + abi * li) / den
    zi = (abi * lr - (abr - 1.0) * li) / den
    zr = jnp.dot(zr, expand, precision=lax.Precision.HIGHEST, preferred_element_type=f32)
    zi = jnp.dot(zi, expand, precision=lax.Precision.HIGHEST, preferred_element_type=f32)
    return abr, abi, zr * btr - zi * bti, zr * bti + zi * btr


def _s5_disc_fwd(args, *, name):
    def body(*refs):
        res = _s5_disc(*[r[...] for r in refs[:6]])
        for o, v in zip(refs[6:], res):
            o[...] = v

    sds = jax.ShapeDtypeStruct
    return pl.pallas_call(body, out_shape=[sds(args[0].shape, f32)] * 2 + [sds(args[3].shape, f32)] * 2,
                          name=name)(*args)


def _s5_disc_bwd(args, cts, *, name):
    def body(*refs):
        vals = [r[...] for r in refs[:6]]
        _, vjp = jax.vjp(lambda *d: _s5_disc(*d, vals[5]), *vals[:5])
        grads = vjp(tuple(r[...] for r in refs[6:10]))
        for o, v in zip(refs[10:], grads):
            o[...] = v

    return pl.pallas_call(body, out_shape=[jax.ShapeDtypeStruct(a.shape, f32) for a in args[:5]],
                          name=name)(*args, *cts)


def _gelu_tanh(x):
    return 0.5 * x * (1.0 + jnp.tanh(0.7978845608028654 * (x + 0.044715 * (x * x * x))))


def _s5_post(y, u, d_skip):
    return _gelu_tanh(y + d_skip * u)


def _s5_glu(ga, gb, h):
    return h + ga * jax.nn.sigmoid(gb)


def _s5_expand():
    e = np.zeros((S5_STATE, S5_GROUP * S5_STATE), np.float32)
    for m in range(S5_GROUP):
        e[np.arange(S5_STATE), m * S5_STATE + np.arange(S5_STATE)] = 1.0
    return jnp.asarray(e)


def _s5_pack_b(bbr, bbi):
    eye = jnp.eye(8, dtype=f32)

    def one(bb):
        b5 = bb.reshape(_S5_NB, 8, S5_GROUP, S5_STATE)
        return jnp.einsum("jgmp,gh->jgmhp", b5, eye).reshape(_S5_NB * _S5_UC, _S5_HALF)

    return jnp.concatenate([one(bbr), one(bbi)], axis=1)


def _s5_unpack_b(db3):
    def one(d):
        d5 = d.reshape(_S5_NB, 8, S5_GROUP, 8, S5_STATE)
        return jnp.einsum("jgmgp->jgmp", d5).reshape(S5_GROUPS, S5_GROUP * S5_STATE)

    return one(db3[:, :_S5_HALF]), one(db3[:, _S5_HALF:])


def _s5_pack_c(c_re, c_im):
    eye = jnp.eye(8, dtype=f32)

    def one(c):
        c4 = c.reshape(_S5_NB, 8, S5_GROUP, S5_STATE)
        return jnp.einsum("jgmp,hg->jhpgm", c4, eye).reshape(_S5_NB, _S5_HALF, _S5_UC)

    return jnp.concatenate([one(c_re), -one(c_im)], axis=1).reshape(_S5_NB * _S5_BW, _S5_UC)


def _s5_unpack_c(dc3):
    d = dc3.reshape(_S5_NB, 2, 8, S5_STATE, 8, S5_GROUP)
    dre = jnp.einsum("jgpgm->jgmp", d[:, 0]).reshape(S5_GROUPS, S5_GROUP, S5_STATE)
    dim = -jnp.einsum("jgpgm->jgmp", d[:, 1]).reshape(S5_GROUPS, S5_GROUP, S5_STATE)
    return dre, dim


def _s5_state_row(re, im):
    r = re.reshape(_S5_NB, 1, _S5_HALF)
    i = im.reshape(_S5_NB, 1, _S5_HALF)
    return jnp.concatenate([r, i], axis=2).reshape(1, _S5_NB * _S5_BW)


def _s5_unstate_row(row):
    r = row.reshape(_S5_NB, 2, 8, S5_STATE)
    return r[:, 0].reshape(S5_GROUPS, S5_STATE), r[:, 1].reshape(S5_GROUPS, S5_STATE)


def _s5_fwd(h, w, tag):
    d = h.shape[1]
    hn, = _rows(_rms, [h], [w["norm_mix"]], [(d, f32)], name=f"s5_norm_{tag}")
    disc_in = [w["s5_lam_re"], w["s5_lam_im"], w["s5_log_dt"], w["s5_bt_re"], w["s5_bt_im"], w["s5_expand"]]
    abr, abi, bbr, bbi = _s5_disc_fwd(disc_in, name=f"s5_disc_{tag}")
    a_row = _s5_state_row(abr, abi)
    b3 = _s5_pack_b(bbr, bbi)
    bu = _bd_mm(hn, b3, name=f"s5_bu_{tag}")
    hs = _s5_scan_fwd(a_row, bu, name=f"s5_scan_{tag}")
    y = _bd_mm(hs, w["s5_c3"], name=f"s5_y_{tag}")
    yg, = _rows(_s5_post, [y, hn], [w["s5_d"]], [(d, bf16)], name=f"s5_post_{tag}")
    ga = _mm(yg, w["s5_w_glu_a"], name=f"s5_glu_a_{tag}")
    gb = _mm(yg, w["s5_w_glu_b"], name=f"s5_glu_b_{tag}")
    out, = _rows(_s5_glu, [ga, gb, h], [], [(d, f32)], name=f"s5_glu_{tag}")
    return out, (h, hn, disc_in, a_row, b3, hs, y, yg, ga, gb)


def _s5_bwd(dout, w, saved, tag):
    h, hn, disc_in, a_row, b3, hs, y, yg, ga, gb = saved
    g = {}
    (dga, dgb), _ = _rows_bwd(_s5_glu, [ga, gb, h], [], [dout], rgrad=[bf16, bf16, None], pgrad=[],
                              name=f"s5_dglu_{tag}")
    dyg = _mm(dga, w["s5_w_glu_a"], tb=True, name=f"s5_dyg_a_{tag}")
    dyg = _mm(dgb, w["s5_w_glu_b"], tb=True, add=dyg, name=f"s5_dyg_b_{tag}")
    g["s5_w_glu_a"] = _mm(yg, dga, ta=True, name=f"s5_dwa_{tag}")
    g["s5_w_glu_b"] = _mm(yg, dgb, ta=True, name=f"s5_dwb_{tag}")
    (dy, du_skip), (g["s5_d"],) = _rows_bwd(_s5_post, [y, hn], [w["s5_d"]], [dyg], rgrad=[bf16, f32], pgrad=[True],
                                           name=f"s5_dpost_{tag}")
    dhs = _bd_mm(dy, w["s5_c3"], tb=True, name=f"s5_dhs_{tag}")
    dc3 = _bd_mm_tn(hs, dy, name=f"s5_dc_{tag}")
    gs, da_row = _s5_scan_bwd(a_row, hs, dhs, name=f"s5_dscan_{tag}")
    du = _bd_mm(gs, b3, tb=True, name=f"s5_du_{tag}")
    db3 = _bd_mm_tn(hn, gs, name=f"s5_db_{tag}")
    dabr, dabi = _s5_unstate_row(da_row)
    dbbr, dbbi = _s5_unpack_b(db3)
    g["s5_lam_re"], g["s5_lam_im"], g["s5_log_dt"], g["s5_bt_re"], g["s5_bt_im"] = _s5_disc_bwd(
        disc_in, [dabr, dabi, dbbr, dbbi], name=f"s5_ddisc_{tag}")
    g["s5_c_re"], g["s5_c_im"] = _s5_unpack_c(dc3)
    (dh,), (g["norm_mix"],) = _rows_bwd(_rms_twice, [h], [w["norm_mix"]], [du, du_skip], rgrad=[f32], pgrad=[True],
                                        addends={0: dout}, name=f"s5_dnorm_{tag}")
    return dh, g


def _odd_weights(p, j, layer, dt):
    tr = lambda b: b.transpose(0, 2, 1).reshape(S5_GROUPS, S5_GROUP * S5_STATE)
    return dict(
        norm_mix=p["norm_mix"][layer][None], s5_lam_re=p["s5_lam_re"][j], s5_lam_im=p["s5_lam_im"][j],
        s5_log_dt=p["s5_log_dt"][j][:, None], s5_bt_re=tr(p["s5_b_re"][j]), s5_bt_im=tr(p["s5_b_im"][j]),
        s5_expand=_s5_expand(), s5_c3=_s5_pack_c(p["s5_c_re"][j], p["s5_c_im"][j]).astype(dt),
        s5_d=p["s5_d"][j][None], s5_w_glu_a=p["s5_w_glu_a"][j].astype(dt), s5_w_glu_b=p["s5_w_glu_b"][j].astype(dt))


def _odd_grads(g):
    tr = lambda b: b.reshape(S5_GROUPS, S5_GROUP, S5_STATE).transpose(0, 2, 1)[None]
    return dict(
        norm_mix=g["norm_mix"], s5_lam_re=g["s5_lam_re"][None], s5_lam_im=g["s5_lam_im"][None],
        s5_log_dt=g["s5_log_dt"][:, 0][None], s5_b_re=tr(g["s5_bt_re"]), s5_b_im=tr(g["s5_bt_im"]),
        s5_c_re=g["s5_c_re"][None], s5_c_im=g["s5_c_im"][None], s5_d=g["s5_d"],
        s5_w_glu_a=g["s5_w_glu_a"][None], s5_w_glu_b=g["s5_w_glu_b"][None])


def _loss_fn(y, t):
    e = y - t
    part = jnp.sum(jnp.sum(e * e, axis=-1, keepdims=True), axis=0, keepdims=True) * (0.5 / y.shape[1])
    return e * (1.0 / y.shape[1]), part


FF_SHARD = 352
FF_SHARD_PAD = 384


def _pad_groups(a, axis):
    axis %= a.ndim
    s = a.shape
    a = a.reshape(s[:axis] + (s[axis] // FF_SHARD, FF_SHARD) + s[axis + 1:])
    pad = [(0, 0)] * a.ndim
    pad[axis + 1] = (0, FF_SHARD_PAD - FF_SHARD)
    return jnp.pad(a, pad).reshape(s[:axis] + (s[axis] // FF_SHARD * FF_SHARD_PAD,) + s[axis + 1:])


def _unpad_groups(a, axis):
    axis %= a.ndim
    s = a.shape
    a = a.reshape(s[:axis] + (s[axis] // FF_SHARD_PAD, FF_SHARD_PAD) + s[axis + 1:])
    a = lax.slice_in_dim(a, 0, FF_SHARD, axis=axis + 1)
    return a.reshape(s[:axis] + (s[axis] // FF_SHARD_PAD * FF_SHARD,) + s[axis + 1:])


def _layer_weights(p, layer, dt):
    return dict(
        norm_xa=p["norm_xa"][layer][None], norm_mem=p["norm_mem"][layer][None], norm_ffn=p["norm_ffn"][layer][None],
        xa_wq=p["xa_wq"][layer].astype(dt), xa_wk=p["xa_wk"][layer].astype(dt), xa_wv=p["xa_wv"][layer].astype(dt),
        xa_wo=p["xa_wo"][layer].astype(dt), xa_q_norm=p["xa_q_norm"][layer][None],
        xa_k_norm=p["xa_k_norm"][layer][None], ffn_w_up=_pad_groups(p["ffn_w_up"][layer], 1).astype(dt),
        ffn_conv_w=_pad_groups(p["ffn_conv_w"][layer], 1), ffn_conv_b=_pad_groups(p["ffn_conv_b"][layer][None], 1),
        ffn_w_down=_pad_groups(p["ffn_w_down"][layer], 0).astype(dt))


_PER_LAYER = ("norm_xa", "norm_mem", "norm_ffn", "xa_wq", "xa_wk", "xa_wv", "xa_wo", "xa_q_norm", "xa_k_norm",
              "ffn_w_up", "ffn_conv_w", "ffn_conv_b", "ffn_w_down")
_FFN_PADDED = dict(ffn_w_up=1, ffn_conv_w=1, ffn_conv_b=1, ffn_w_down=0)


def _local_step(x, mem, positions, target, p):
    cos, sin = _rope_tables(positions)
    we = _even_weights(p, 0, 0, bf16)
    wo = _odd_weights(p, 0, 1, bf16)
    wl = [_layer_weights(p, layer, bf16) for layer in range(2)]
    loss, dh, g_even, g_odd, gl = _local_core(x, mem, cos, sin, target, we, wo, wl)
    grads = {}
    for n in _PER_LAYER:
        a, b = gl[0][n], gl[1][n]
        if n in _FFN_PADDED:
            a, b = _unpad_groups(a, _FFN_PADDED[n]), _unpad_groups(b, _FFN_PADDED[n])
        grads[n] = jnp.concatenate([a, b], axis=0) if a.shape[0] == 1 else jnp.stack([a, b])
    ge, go = _even_grads(g_even), _odd_grads(g_odd)
    grads["norm_mix"] = jnp.concatenate([ge.pop("norm_mix"), go.pop("norm_mix")], axis=0)
    grads.update(ge)
    grads.update(go)
    return loss, dh, grads


def _local_core(x, mem, cos, sin, target, we, wo, wl):
    h, s_mix0 = _mixer_fwd(x, cos, sin, we, "l0")
    h, s_xa0 = _xattn_fwd(h, mem, wl[0], "l0")
    h, s_ff0 = _ffn_fwd(h, wl[0], "l0")
    h, s_mix1 = _s5_fwd(h, wo, "l1")
    h, s_xa1 = _xattn_fwd(h, mem, wl[1], "l1")
    h, s_ff1 = _ffn_fwd(h, wl[1], "l1")
    dh, loss = _rows(_loss_fn, [h, target], [], [(h.shape[1], f32)], accs=[(1, 1)], name="loss_head")

    gl = [{}, {}]
    dh, g = _ffn_bwd(dh, wl[1], s_ff1, "l1")
    gl[1].update(g)
    dh, g = _xattn_bwd(dh, mem, wl[1], s_xa1, "l1")
    gl[1].update(g)
    dh, g_odd = _s5_bwd(dh, wo, s_mix1, "l1")
    dh, g = _ffn_bwd(dh, wl[0], s_ff0, "l0")
    gl[0].update(g)
    dh, g = _xattn_bwd(dh, mem, wl[0], s_xa0, "l0")
    gl[0].update(g)
    dh, g_even = _mixer_bwd(dh, cos, sin, we, s_mix0, "l0")
    return loss, dh, g_even, g_odd, gl


_LANES = 1024
_ROW_PAD = 256


_PEER_MASKS = (1, 2, 4, 3, 5, 6, 7)


def _mesh_place():
    x, y, c = lax.axis_index("x"), lax.axis_index("y"), lax.axis_index("c")

    def peer(mask):
        px = 1 - x if mask & 4 else x
        py = 1 - y if mask & 2 else y
        pc = 1 - c if mask & 1 else c
        return (px, py, pc), 4 * px + 2 * py + pc

    return 4 * x + 2 * y + c, peer


class _Exchange:
    def __init__(self, name):
        self.name = name
        self.srcs, self.shapes, self.items, self.where = [], [], [], {}

    def add(self, src, land_shape, src_at, dst_at, key):
        si = next((i for i, s in enumerate(self.srcs) if s is src), None)
        if si is None:
            self.srcs.append(src)
            si = len(self.srcs) - 1
        if key not in self.where:
            self.shapes.append(land_shape)
            self.where[key] = len(self.shapes) - 1
        self.items.append(dict(src=si, dst=self.where[key], src_at=src_at, dst_at=dst_at))

    def _copy(self, k, mask, ins, lands, send_sems, recv_sems, me, peer, arriving):
        it = self.items[k]
        dev, idx = peer(mask)
        s = k * (N_DEV - 1) + _PEER_MASKS.index(mask)
        return pltpu.make_async_remote_copy(
            src_ref=it["src_at"](ins[it["src"]], idx), dst_ref=it["dst_at"](lands[it["dst"]], idx if arriving else me),
            send_sem=send_sems.at[s], recv_sem=recv_sems.at[s], device_id=dev, device_id_type=pl.DeviceIdType.MESH)

    def _own_copy(self, k, ins, lands, own_sems, me):
        it = self.items[k]
        return pltpu.make_async_copy(it["src_at"](ins[it["src"]], me), it["dst_at"](lands[it["dst"]], me), own_sems.at[k])

    def begin(self, own):
        ns, nd, ni = len(self.srcs), len(self.shapes), len(self.items)
        nsem = ni * (N_DEV - 1)
        self.own = own

        nq = 3 if own else 2

        def body(*refs):
            ins, land_refs = refs[:ns], refs[ns:ns + nd]
            sems, token = refs[ns + nd:ns + nd + nq], refs[-1]
            me, peer = _mesh_place()
            for mask in _PEER_MASKS:
                for k in range(ni):
                    self._copy(k, mask, ins, land_refs, sems[0], sems[1], me, peer, False).start()
            if own:
                for k in range(ni):
                    self._own_copy(k, ins, land_refs, sems[2], me).start()
            token[...] = jnp.zeros_like(token)

        hbm = pl.BlockSpec(memory_space=pltpu.HBM)
        sem = pl.BlockSpec(memory_space=pltpu.SEMAPHORE)
        lands = [lax.empty(s.shape, s.dtype) for s in self.shapes]
        sem_shapes = [pltpu.SemaphoreType.DMA((nsem,)), pltpu.SemaphoreType.DMA((nsem,)), pltpu.SemaphoreType.DMA((ni,))]
        res = pl.pallas_call(
            body, in_specs=[hbm] * (ns + nd),
            out_specs=[sem] * nq + [hbm] * nd + [pl.BlockSpec(memory_space=pltpu.VMEM)],
            out_shape=sem_shapes[:nq] + [pltpu.HBM(s.shape, s.dtype) for s in self.shapes]
            + [jax.ShapeDtypeStruct((8, 128), f32)],
            input_output_aliases={ns + j: nq + j for j in range(nd)},
            compiler_params=pltpu.CompilerParams(has_side_effects=pltpu.SideEffectType.DATAFLOW_SIDE_EFFECTING),
            name=self.name + "_start")(*self.srcs, *lands)
        self.token = res[-1]
        return list(res[:nq]), list(res[nq:-1])

    def finish(self, state, after):
        sems, lands = state
        nq = len(sems)
        after = list(after) if isinstance(after, (list, tuple)) else [after]
        ns, nd, ni = len(self.srcs), len(self.shapes), len(self.items)

        def body(*refs):
            ins, land_refs = refs[:ns], refs[ns:ns + nd]
            sem_refs = refs[ns + nd:ns + nd + nq]
            me, peer = _mesh_place()
            for mask in _PEER_MASKS:
                for k in range(ni):
                    cp = self._copy(k, mask, ins, land_refs, sem_refs[0], sem_refs[1], me, peer, True)
                    cp.wait_send()
                    cp.wait_recv()
            if self.own:
                for k in range(ni):
                    self._own_copy(k, ins, land_refs, sem_refs[2], me).wait()

        hbm = pl.BlockSpec(memory_space=pltpu.HBM)
        sem = pl.BlockSpec(memory_space=pltpu.SEMAPHORE)
        res = pl.pallas_call(
            body, in_specs=[hbm] * (ns + nd) + [sem] * nq + [pl.BlockSpec(memory_space=pl.ANY)] * len(after),
            out_specs=[hbm] * nd, out_shape=[pltpu.HBM(s.shape, s.dtype) for s in self.shapes],
            input_output_aliases={ns + j: j for j in range(nd)},
            compiler_params=pltpu.CompilerParams(has_side_effects=pltpu.SideEffectType.DATAFLOW_SIDE_EFFECTING),
            name=self.name + "_wait")(*self.srcs, *lands, *sems, *after)
        return {k: res[i] for k, i in self.where.items()}


def _after(x, *tokens, name):
    def body(*refs):
        del refs

    anyspace = pl.BlockSpec(memory_space=pl.ANY)
    return pl.pallas_call(body, in_specs=[anyspace] * (1 + len(tokens)), out_specs=anyspace,
                          out_shape=jax.ShapeDtypeStruct(x.shape, x.dtype), input_output_aliases={0: 0},
                          name=name)(x, *tokens)


def _rows_of(n):
    return lambda r, i: r.at[pl.ds(pl.multiple_of(i * n, n), n), :]


def _cols_of(n):
    return lambda r, i: r.at[:, pl.ds(pl.multiple_of(i * n, n), n)]


def _whole(r, i):
    return r


def _slot(r, i):
    return r.at[i]


def _at_layer(layer):
    return lambda r, i: r.at[layer]


def _slot_layer(layer):
    return lambda r, i: r.at[i, layer]


def _sum_adam(me_index, slots, own, own_block, w, m, v, *, name):
    rows, cols = w.shape
    tr = _pick(rows, (256, 128, 64, 32, 16, 8))
    bc1 = 1.0 - ADAM_B1 ** ADAM_STEP
    bc2 = 1.0 - ADAM_B2 ** ADAM_STEP

    own_shape, own_map = own_block(tr)

    def body(me_ref, s_ref, own_ref, w_ref, m_ref, v_ref, g_ref, d_ref, nm_ref, nv_ref):
        mine = own_ref[0] if len(own_shape) == 3 else own_ref[...]
        me = me_ref[0]
        g = jnp.where(me == 0, mine, s_ref[0])
        for k in range(1, N_DEV):
            g = g + jnp.where(me == k, mine, s_ref[k])
        mm = ADAM_B1 * m_ref[...] + (1.0 - ADAM_B1) * g
        vv = ADAM_B2 * v_ref[...] + (1.0 - ADAM_B2) * (g * g)
        g_ref[...] = g
        nm_ref[...] = mm
        nv_ref[...] = vv
        d_ref[...] = -ADAM_LR * ((mm / bc1) / (jnp.sqrt(vv / bc2) + ADAM_EPS) + ADAM_WD * w_ref[...])

    blk = pl.BlockSpec((tr, cols), lambda i, me: (i, 0))
    sds = jax.ShapeDtypeStruct((rows, cols), f32)
    grid_spec = pltpu.PrefetchScalarGridSpec(
        num_scalar_prefetch=1, grid=(rows // tr,),
        in_specs=[pl.BlockSpec((N_DEV, tr, cols), lambda i, me: (0, i, 0)), pl.BlockSpec(own_shape, own_map),
                  blk, blk, blk],
        out_specs=[blk] * 4)
    return pl.pallas_call(body, grid_spec=grid_spec, out_shape=[sds] * 4, compiler_params=_cparams(("parallel",)),
                          name=name)(me_index, slots, own, w, m, v)


_SHARDED = dict(xa_wq=1, xa_wk=1, xa_wv=1, xa_wo=1, ffn_w_up=2, ffn_conv_w=2, ffn_w_down=1, mix_w_in=2, mla_w_uq=2,
                mla_w_ukv=2, mix_w_out=1, s5_d=1, s5_w_glu_a=1, s5_w_glu_b=1)
_EXACT = ("ffn_conv_w", "s5_d")
_WEIGHTS = ("norm_mix", "norm_xa", "norm_mem", "norm_ffn", "xa_wq", "xa_wk", "xa_wv", "xa_wo", "xa_q_norm",
            "xa_k_norm", "ffn_w_up", "ffn_conv_w", "ffn_conv_b", "ffn_w_down", "hg_lb_logits", "mix_w_in",
            "hg_out_norm", "mla_q_a_norm", "mla_w_uq", "mla_kv_a_norm", "mla_w_ukv", "mla_qn_nope", "mla_qn_rope",
            "mla_kn_nope", "mla_kn_rope", "mix_w_out", "s5_lam_re", "s5_lam_im", "s5_log_dt", "s5_b_re", "s5_b_im",
            "s5_c_re", "s5_c_im", "s5_d", "s5_w_glu_a", "s5_w_glu_b")
_BIG = tuple(n for n in _WEIGHTS if n in _SHARDED and n not in _EXACT)
_SHARD_ORDER = tuple(n for n in _WEIGHTS if n in _SHARDED)
_REPL_ORDER = tuple(n for n in _WEIGHTS if n not in _SHARDED)
_REPL_EARLY = tuple(n for n in _REPL_ORDER if n.startswith("s5_"))
_REPL_LATE = tuple(n for n in _REPL_ORDER if n not in _REPL_EARLY)


def _pack(parts, dtype, lead=None):
    nl = 0 if lead is None else 1
    flat = [a.astype(dtype).reshape(a.shape[:nl] + (-1,)) for a in parts]
    cat = jnp.concatenate(flat, axis=nl)
    n = cat.shape[nl]
    unit = _LANES * _ROW_PAD
    total = -(-n // unit) * unit
    cat = jnp.pad(cat, [(0, 0)] * nl + [(0, total - n)])
    return cat.reshape(cat.shape[:nl] + (total // _LANES, _LANES))


def _unpack(packed, shapes, lead=None):
    nl = 0 if lead is None else 1
    flat = packed.reshape(packed.shape[:nl] + (-1,))
    out, off = [], 0
    for s in shapes:
        n = int(np.prod(s))
        piece = flat[..., off:off + n] if nl else flat[off:off + n]
        out.append(piece.reshape(packed.shape[:nl] + tuple(s)))
        off += n
    return out


def _to_full(gathered, axis):
    g = jnp.moveaxis(gathered, 0, axis)
    s = g.shape
    return g.reshape(s[:axis] + (s[axis] * s[axis + 1],) + s[axis + 2:])


def _to_shards(full, axis):
    s = full.shape
    g = full.reshape(s[:axis] + (N_DEV, s[axis] // N_DEV) + s[axis + 1:])
    return jnp.moveaxis(g, axis, 0)


_DIRECT_ROWS = ("xa_wq", "xa_wk", "xa_wv", "xa_wo", "mix_w_out", "s5_w_glu_a", "s5_w_glu_b")
_SMALL16 = ("mix_w_in", "mla_w_uq", "mla_w_ukv")
_SMALL_SHARDED = _SMALL16 + _EXACT
_SHARD_ROWS = 128


def _exchange_layout(d):
    out = dict(d)
    out["ffn_w_up"] = _pad_groups(d["ffn_w_up"], 2)
    out["ffn_conv_w"] = _pad_groups(d["ffn_conv_w"], 2)
    out["ffn_w_down"] = _pad_groups(d["ffn_w_down"], 1)
    return out


def _train_step(x, mem, positions, target, w, m, v):
    d_model = x.shape[1]
    we_, me_, ve_ = _exchange_layout(w), _exchange_layout(m), _exchange_layout(v)
    sds = jax.ShapeDtypeStruct

    matrices = _DIRECT_ROWS + ("ffn_w_up", "ffn_w_down")
    layer_mats = ("xa_wq", "xa_wk", "xa_wv", "xa_wo", "ffn_w_up", "ffn_w_down")
    shard16 = {n: we_[n].astype(bf16) for n in matrices}
    part_of = {n: _rows_of(_SHARD_ROWS) for n in _DIRECT_ROWS}
    part_of["ffn_w_up"] = _cols_of(we_["ffn_w_up"].shape[2])
    part_of["ffn_w_down"] = _rows_of(we_["ffn_w_down"].shape[1])
    part_shape = {n: we_[n].shape[1:] for n in matrices}

    def full_shape(n):
        r, c = part_shape[n]
        return (r, N_DEV * c) if n == "ffn_w_up" else (N_DEV * r, c)

    def gather(ex, n, layer):
        ex.add(shard16[n], sds(full_shape(n), bf16), _at_layer(layer), part_of[n], (n, layer))

    def scatter(ex, n, layer, grad):
        ex.add(grad, sds((N_DEV,) + part_shape[n], f32), part_of[n], _slot, (n, layer))

    small16 = _pack([we_[n] for n in _SMALL16], bf16)
    exact = _pack([we_[n] for n in _EXACT], f32)
    ga, gb, gc = _Exchange("gather_a"), _Exchange("gather_b"), _Exchange("gather_c")
    ga.add(small16, sds((N_DEV,) + small16.shape, bf16), _whole, _slot, "small16")
    ga.add(exact, sds((N_DEV,) + exact.shape, f32), _whole, _slot, "exact")
    gather(ga, "mix_w_out", 0)
    for n in layer_mats:
        gather(gb, n, 0)
    gather(gc, "s5_w_glu_a", 0)
    gather(gc, "s5_w_glu_b", 0)
    for n in layer_mats:
        gather(gc, n, 1)
    state_a, state_b, state_c = ga.begin(True), gb.begin(True), gc.begin(True)

    full = ga.finish(state_a, [gb.token, gc.token])
    p = {n: w[n] for n in _REPL_ORDER}
    for n, a in zip(_SMALL16, _unpack(full["small16"], [we_[n].shape for n in _SMALL16], lead=True)):
        p[n] = _to_full(a, _SHARDED[n])
    conv_w, p["s5_d"] = [_to_full(a, _SHARDED[n]) for n, a in
                         zip(_EXACT, _unpack(full["exact"], [we_[n].shape for n in _EXACT], lead=True))]
    p["mix_w_out"] = full[("mix_w_out", 0)][None]
    cos, sin = _rope_tables(positions)
    we = _even_weights(p, 0, 0, bf16)
    we["norm_mix"] = _after(we["norm_mix"], ga.token, gb.token, gc.token, name="after_gather_starts")
    conv_b = _pad_groups(w["ffn_conv_b"], 1)

    def layer_weights(layer):
        return dict(norm_xa=w["norm_xa"][layer][None], norm_mem=w["norm_mem"][layer][None],
                    norm_ffn=w["norm_ffn"][layer][None], xa_q_norm=w["xa_q_norm"][layer][None],
                    xa_k_norm=w["xa_k_norm"][layer][None], ffn_conv_w=conv_w[layer],
                    ffn_conv_b=conv_b[layer][None], **{n: full[(n, layer)] for n in layer_mats})

    h, s_mix0 = _mixer_fwd(x, cos, sin, we, "l0")
    full.update(gb.finish(state_b, h))
    wl = [layer_weights(0)]
    h, s_xa0 = _xattn_fwd(h, mem, wl[0], "l0")
    h, s_ff0 = _ffn_fwd(h, wl[0], "l0")
    full.update(gc.finish(state_c, h))
    wl.append(layer_weights(1))
    p["s5_w_glu_a"], p["s5_w_glu_b"] = full[("s5_w_glu_a", 0)][None], full[("s5_w_glu_b", 0)][None]
    wo = _odd_weights(p, 0, 1, bf16)
    h, s_mix1 = _s5_fwd(h, wo, "l1")
    h, s_xa1 = _xattn_fwd(h, mem, wl[1], "l1")
    h, s_ff1 = _ffn_fwd(h, wl[1], "l1")
    dh, loss = _rows(_loss_fn, [h, target], [], [(h.shape[1], f32)], accs=[(1, 1)], name="loss_head")

    gl = [{}, {}]
    reduces = []

    own_grad = {}

    def reduce_start(name, entries, dh):
        ex = _Exchange(name)
        for n, layer, grad in entries.get("matrices", ()):
            scatter(ex, n, layer, grad)
            own_grad[(n, layer)] = grad
        for key, src, shape, src_at in entries.get("packs", ()):
            ex.add(src, shape, src_at, _slot, key)
        reduces.append((ex, ex.begin(False)))
        return _after(dh, ex.token, name="after_" + name)

    dh, gl[1] = _ffn_bwd(dh, wl[1], s_ff1, "l1")
    dh = reduce_start("reduce_ffn1", dict(matrices=[(n, 1, gl[1][n]) for n in ("ffn_w_up", "ffn_w_down")]), dh)
    dh, g = _xattn_bwd(dh, mem, wl[1], s_xa1, "l1")
    gl[1].update(g)
    dh = reduce_start("reduce_xa1", dict(matrices=[(n, 1, g[n]) for n in ("xa_wq", "xa_wk", "xa_wv", "xa_wo")]), dh)
    dh, g_odd = _s5_bwd(dh, wo, s_mix1, "l1")
    go = _odd_grads(g_odd)
    dh, gl[0] = _ffn_bwd(dh, wl[0], s_ff0, "l0")
    send_early = _pack([go[n].reshape(w[n].shape) for n in _REPL_EARLY], f32)
    dh = reduce_start("reduce_ffn0", dict(
        matrices=[(n, 0, g_odd[n]) for n in ("s5_w_glu_a", "s5_w_glu_b")]
        + [(n, 0, gl[0][n]) for n in ("ffn_w_up", "ffn_w_down")],
        packs=[("repl_early", send_early, sds((N_DEV,) + send_early.shape, f32), _whole)]), dh)
    dh, g = _xattn_bwd(dh, mem, wl[0], s_xa0, "l0")
    gl[0].update(g)
    dh = reduce_start("reduce_xa0", dict(matrices=[(n, 0, g[n]) for n in ("xa_wq", "xa_wk", "xa_wv", "xa_wo")]), dh)
    grad_x, g_even = _mixer_bwd(
        dh, cos, sin, we, s_mix0, "l0",
        on_w_out=lambda grad, dmixin: reduce_start("reduce_w_out", dict(matrices=[("mix_w_out", 0, grad)]), dmixin))

    ge = _even_grads(g_even)
    cat = lambda n: jnp.concatenate([gl[0][n], gl[1][n]], axis=0)
    rg = dict(ge)
    rg["norm_mix"] = jnp.concatenate([ge["norm_mix"], go["norm_mix"]], axis=0)
    for n in ("norm_xa", "norm_mem", "norm_ffn", "xa_q_norm", "xa_k_norm"):
        rg[n] = cat(n)
    rg["ffn_conv_b"] = _unpad_groups(cat("ffn_conv_b"), 1)
    sg = dict(mix_w_in=ge["mix_w_in"], mla_w_uq=ge["mla_w_uq"], mla_w_ukv=ge["mla_w_ukv"], s5_d=go["s5_d"],
              ffn_conv_w=jnp.stack([gl[0]["ffn_conv_w"], gl[1]["ffn_conv_w"]]))
    send_small = _pack([_to_shards(sg[n], _SHARDED[n]) for n in _SMALL_SHARDED], f32, lead=True)
    send_late = _pack([rg[n].reshape(w[n].shape) for n in _REPL_LATE], f32)
    last = _Exchange("reduce_last")
    last.add(send_small, sds(send_small.shape, f32), _slot, _slot, "small")
    last.add(send_late, sds((N_DEV,) + send_late.shape, f32), _whole, _slot, "repl_late")
    state_last = last.begin(False)
    slots = {}
    for ex, state in reduces:
        slots.update(ex.finish(state, [grad_x, last.token]))

    me_index = (4 * lax.axis_index("x") + 2 * lax.axis_index("y") + lax.axis_index("c")).astype(jnp.int32).reshape(1)

    def own_block(n):
        r, c = part_shape[n]
        if n == "ffn_w_up":
            return lambda tr: ((tr, c), lambda i, me: (i, me[0]))
        return lambda tr: ((tr, c), lambda i, me: (me[0] * (r // tr) + i, 0))

    out = [{}, {}, {}, {}]
    unpad = dict(ffn_w_up=2, ffn_conv_w=2, ffn_w_down=1)
    done = []
    for n in matrices:
        per_layer = [_sum_adam(me_index, slots[(n, layer)], own_grad[(n, layer)], own_block(n), we_[n][layer],
                               me_[n][layer], ve_[n][layer], name=f"adam_{n}_{layer}")
                     for layer in range(we_[n].shape[0])]
        done += [res[0] for res in per_layer]
        for k in range(4):
            r = jnp.stack([res[k] for res in per_layer])
            out[k][n] = _unpad_groups(r, unpad[n]) if n in unpad else r
    pk = lambda d, order: _pack([d[n] for n in order], f32)
    whole_rows = lambda tr: ((tr, _LANES), lambda i, me: (i, 0))
    res_early = _sum_adam(me_index, slots["repl_early"], send_early, whole_rows, pk(w, _REPL_EARLY), pk(m, _REPL_EARLY),
                          pk(v, _REPL_EARLY), name="adam_repl_early")
    slots = last.finish(state_last, done + [res_early[0]])
    res_small = _sum_adam(me_index, slots["small"], send_small,
                          lambda tr: ((1, tr, _LANES), lambda i, me: (me[0], i, 0)),
                          pk(we_, _SMALL_SHARDED), pk(me_, _SMALL_SHARDED), pk(ve_, _SMALL_SHARDED), name="adam_small")
    res_late = _sum_adam(me_index, slots["repl_late"], send_late, whole_rows, pk(w, _REPL_LATE), pk(m, _REPL_LATE),
                         pk(v, _REPL_LATE), name="adam_repl_late")
    for k in range(4):
        for n, a in zip(_SMALL_SHARDED, _unpack(res_small[k], [we_[n].shape for n in _SMALL_SHARDED])):
            out[k][n] = _unpad_groups(a, unpad[n]) if n in unpad else a
        out[k].update(zip(_REPL_EARLY, _unpack(res_early[k], [w[n].shape for n in _REPL_EARLY])))
        out[k].update(zip(_REPL_LATE, _unpack(res_late[k], [w[n].shape for n in _REPL_LATE])))
    return loss, grad_x, out


_INPUTS = tuple("""x, mem, positions, norm_mix, norm_xa, norm_mem, norm_ffn, xa_wq, xa_wk, xa_wv, xa_wo, xa_q_norm, xa_k_norm, ffn_w_up, ffn_conv_w, ffn_conv_b, ffn_w_down, hg_lb_logits, mix_w_in, hg_out_norm, mla_q_a_norm, mla_w_uq, mla_kv_a_norm, mla_w_ukv, mla_qn_nope, mla_qn_rope, mla_kn_nope, mla_kn_rope, mix_w_out, s5_lam_re, s5_lam_im, s5_log_dt, s5_b_re, s5_b_im, s5_c_re, s5_c_im, s5_d, s5_w_glu_a, s5_w_glu_b, loss_target, m_norm_mix, m_norm_xa, m_norm_mem, m_norm_ffn, m_xa_wq, m_xa_wk, m_xa_wv, m_xa_wo, m_xa_q_norm, m_xa_k_norm, m_ffn_w_up, m_ffn_conv_w, m_ffn_conv_b, m_ffn_w_down, m_hg_lb_logits, m_mix_w_in, m_hg_out_norm, m_mla_q_a_norm, m_mla_w_uq, m_mla_kv_a_norm, m_mla_w_ukv, m_mla_qn_nope, m_mla_qn_rope, m_mla_kn_nope, m_mla_kn_rope, m_mix_w_out, m_s5_lam_re, m_s5_lam_im, m_s5_log_dt, m_s5_b_re, m_s5_b_im, m_s5_c_re, m_s5_c_im, m_s5_d, m_s5_w_glu_a, m_s5_w_glu_b, v_norm_mix, v_norm_xa, v_norm_mem, v_norm_ffn, v_xa_wq, v_xa_wk, v_xa_wv, v_xa_wo, v_xa_q_norm, v_xa_k_norm, v_ffn_w_up, v_ffn_conv_w, v_ffn_conv_b, v_ffn_w_down, v_hg_lb_logits, v_mix_w_in, v_hg_out_norm, v_mla_q_a_norm, v_mla_w_uq, v_mla_kv_a_norm, v_mla_w_ukv, v_mla_qn_nope, v_mla_qn_rope, v_mla_kn_nope, v_mla_kn_rope, v_mix_w_out, v_s5_lam_re, v_s5_lam_im, v_s5_log_dt, v_s5_b_re, v_s5_b_im, v_s5_c_re, v_s5_c_im, v_s5_d, v_s5_w_glu_a, v_s5_w_glu_b""".replace(" ", "").split(","))


def kernel(x, mem, positions, norm_mix, norm_xa, norm_mem, norm_ffn, xa_wq, xa_wk, xa_wv, xa_wo, xa_q_norm, xa_k_norm, ffn_w_up, ffn_conv_w, ffn_conv_b, ffn_w_down, hg_lb_logits, mix_w_in, hg_out_norm, mla_q_a_norm, mla_w_uq, mla_kv_a_norm, mla_w_ukv, mla_qn_nope, mla_qn_rope, mla_kn_nope, mla_kn_rope, mix_w_out, s5_lam_re, s5_lam_im, s5_log_dt, s5_b_re, s5_b_im, s5_c_re, s5_c_im, s5_d, s5_w_glu_a, s5_w_glu_b, loss_target, m_norm_mix, m_norm_xa, m_norm_mem, m_norm_ffn, m_xa_wq, m_xa_wk, m_xa_wv, m_xa_wo, m_xa_q_norm, m_xa_k_norm, m_ffn_w_up, m_ffn_conv_w, m_ffn_conv_b, m_ffn_w_down, m_hg_lb_logits, m_mix_w_in, m_hg_out_norm, m_mla_q_a_norm, m_mla_w_uq, m_mla_kv_a_norm, m_mla_w_ukv, m_mla_qn_nope, m_mla_qn_rope, m_mla_kn_nope, m_mla_kn_rope, m_mix_w_out, m_s5_lam_re, m_s5_lam_im, m_s5_log_dt, m_s5_b_re, m_s5_b_im, m_s5_c_re, m_s5_c_im, m_s5_d, m_s5_w_glu_a, m_s5_w_glu_b, v_norm_mix, v_norm_xa, v_norm_mem, v_norm_ffn, v_xa_wq, v_xa_wk, v_xa_wv, v_xa_wo, v_xa_q_norm, v_xa_k_norm, v_ffn_w_up, v_ffn_conv_w, v_ffn_conv_b, v_ffn_w_down, v_hg_lb_logits, v_mix_w_in, v_hg_out_norm, v_mla_q_a_norm, v_mla_w_uq, v_mla_kv_a_norm, v_mla_w_ukv, v_mla_qn_nope, v_mla_qn_rope, v_mla_kn_nope, v_mla_kn_rope, v_mix_w_out, v_s5_lam_re, v_s5_lam_im, v_s5_log_dt, v_s5_b_re, v_s5_b_im, v_s5_c_re, v_s5_c_im, v_s5_d, v_s5_w_glu_a, v_s5_w_glu_b):
    vals = dict(zip(_INPUTS, (x, mem, positions, norm_mix, norm_xa, norm_mem, norm_ffn, xa_wq, xa_wk, xa_wv, xa_wo, xa_q_norm, xa_k_norm, ffn_w_up, ffn_conv_w, ffn_conv_b, ffn_w_down, hg_lb_logits, mix_w_in, hg_out_norm, mla_q_a_norm, mla_w_uq, mla_kv_a_norm, mla_w_ukv, mla_qn_nope, mla_qn_rope, mla_kn_nope, mla_kn_rope, mix_w_out, s5_lam_re, s5_lam_im, s5_log_dt, s5_b_re, s5_b_im, s5_c_re, s5_c_im, s5_d, s5_w_glu_a, s5_w_glu_b, loss_target, m_norm_mix, m_norm_xa, m_norm_mem, m_norm_ffn, m_xa_wq, m_xa_wk, m_xa_wv, m_xa_wo, m_xa_q_norm, m_xa_k_norm, m_ffn_w_up, m_ffn_conv_w, m_ffn_conv_b, m_ffn_w_down, m_hg_lb_logits, m_mix_w_in, m_hg_out_norm, m_mla_q_a_norm, m_mla_w_uq, m_mla_kv_a_norm, m_mla_w_ukv, m_mla_qn_nope, m_mla_qn_rope, m_mla_kn_nope, m_mla_kn_rope, m_mix_w_out, m_s5_lam_re, m_s5_lam_im, m_s5_log_dt, m_s5_b_re, m_s5_b_im, m_s5_c_re, m_s5_c_im, m_s5_d, m_s5_w_glu_a, m_s5_w_glu_b, v_norm_mix, v_norm_xa, v_norm_mem, v_norm_ffn, v_xa_wq, v_xa_wk, v_xa_wv, v_xa_wo, v_xa_q_norm, v_xa_k_norm, v_ffn_w_up, v_ffn_conv_w, v_ffn_conv_b, v_ffn_w_down, v_hg_lb_logits, v_mix_w_in, v_hg_out_norm, v_mla_q_a_norm, v_mla_w_uq, v_mla_kv_a_norm, v_mla_w_ukv, v_mla_qn_nope, v_mla_qn_rope, v_mla_kn_nope, v_mla_kn_rope, v_mix_w_out, v_s5_lam_re, v_s5_lam_im, v_s5_log_dt, v_s5_b_re, v_s5_b_im, v_s5_c_re, v_s5_c_im, v_s5_d, v_s5_w_glu_a, v_s5_w_glu_b)))
    w = {n: vals[n] for n in _WEIGHTS}
    m = {n: vals["m_" + n] for n in _WEIGHTS}
    v = {n: vals["v_" + n] for n in _WEIGHTS}
    loss, grad_x, res = _train_step(vals["x"][0], vals["mem"][0], vals["positions"][0], vals["loss_target"][0],
                                    w, m, v)
    loss = lax.psum(loss[0, 0], ("x", "y", "c"))
    return (loss, grad_x[None], *[r[n] for r in res for n in _WEIGHTS])
```

```python
import functools

import jax
import jax.numpy as jnp
import numpy as np
from jax import lax
from jax.experimental import pallas as pl
from jax.experimental.pallas import tpu as pltpu

f32 = jnp.float32
bf16 = jnp.bfloat16

EPS = 1e-6
N_DEV = 8
VMEM_LIMIT = 52 * 1024 * 1024

HG_HEADS = 4
HG_DIM = 128
HG_WIDTH = HG_HEADS * HG_DIM
HG_CHUNK = 64
HG_SUB = 16
MLA_HEADS = 4
MLA_Q_RANK = 256
MLA_KV_RANK = 128
MLA_NOPE = 128
MLA_ROPE = 64
MLA_V = 128
MLA_QK = MLA_NOPE + MLA_ROPE
MLA_QK_PAD = 256
ROPE_BASE = 10000.0
IN_WIDTH = 4 * HG_WIDTH + MLA_Q_RANK + MLA_KV_RANK + MLA_ROPE
IN_PAD = 2560
XA_HEADS = 4
XA_DIM = 256
S5_GROUP = 16
S5_GROUPS = 64
S5_STATE = 64
CONV_W = 3

ADAM_LR = 0.001
ADAM_B1 = 0.9
ADAM_B2 = 0.999
ADAM_EPS = 1e-08
ADAM_WD = 0.01
ADAM_STEP = 10

_NT = (((1,), (1,)), ((), ()))
_TN = (((0,), (0,)), ((), ()))
_NN = (((1,), (0,)), ((), ()))


def _pick(n, cands):
    for c in cands:
        if n % c == 0:
            return c
    return n


def _cparams(sem):
    return pltpu.CompilerParams(dimension_semantics=sem, vmem_limit_bytes=VMEM_LIMIT)


_MM_BUDGET = 36 * 1024 * 1024


def _mm(a, b, *, name, ta=False, tb=False, out_dtype=f32, add=None, b2=None, kslab=None):
    m, k = (a.shape[1], a.shape[0]) if ta else a.shape
    nb = b.shape[0] if tb else b.shape[1]
    n = nb * (2 if b2 is not None else 1)
    slab, nslab = kslab if kslab is not None else (0, 1)
    assert (b.shape[1] // nslab if tb else b.shape[0]) == k, (a.shape, b.shape, ta, tb)
    assert b2 is None or (not tb and b2.shape == b.shape)
    isz = lambda x: jnp.dtype(x.dtype).itemsize
    bm = bn = None
    for cm, cn in ((512, 512), (512, 256), (256, 512), (256, 256), (256, 128), (128, 256), (128, 128)):
        if m % cm or nb % cn:
            continue
        need = 2 * (cm * k * isz(a) + cn * k * isz(b) * (2 if b2 is not None else 1)
                    + cm * cn * (jnp.dtype(out_dtype).itemsize + (4 if add is not None else 0)))
        if need <= _MM_BUDGET:
            bm, bn = cm, cn
            break
    assert bm is not None, (name, a.shape, b.shape)
    half = nb // bn
    dims = (((0 if ta else 1,), (1 if tb else 0,)), ((), ()))

    def body(*refs):
        refs = list(refs)
        a_ref, b_ref = refs[0], refs[1]
        b2_ref = refs.pop(2) if b2 is not None else None
        add_ref = refs[2] if add is not None else None
        o_ref = refs[-1]

        def run(rhs_ref):
            r = lax.dot_general(a_ref[...].astype(bf16), rhs_ref[...].astype(bf16), dims, preferred_element_type=f32)
            if add_ref is not None:
                r = r + add_ref[...].astype(f32)
            o_ref[...] = r.astype(o_ref.dtype)

        if b2_ref is None:
            run(b_ref)
        else:
            pl.when(pl.program_id(1) < half)(lambda: run(b_ref))
            pl.when(pl.program_id(1) >= half)(lambda: run(b2_ref))

    a_spec = pl.BlockSpec((k, bm), lambda i, j: (0, i)) if ta else pl.BlockSpec((bm, k), lambda i, j: (i, 0))
    if tb:
        b_spec = pl.BlockSpec((bn, k), lambda i, j: (j, slab))
    elif b2 is None:
        b_spec = pl.BlockSpec((k, bn), lambda i, j: (0, j))
    else:
        b_spec = pl.BlockSpec((k, bn), lambda i, j: (0, jnp.minimum(j, half - 1)))
    in_specs = [a_spec, b_spec]
    args = [a, b]
    if b2 is not None:
        in_specs.append(pl.BlockSpec((k, bn), lambda i, j: (0, jnp.maximum(j - half, 0))))
        args.append(b2)
    if add is not None:
        in_specs.append(pl.BlockSpec((bm, bn), lambda i, j: (i, j)))
        args.append(add)
    return pl.pallas_call(
        body, grid=(m // bm, n // bn), in_specs=in_specs,
        out_specs=pl.BlockSpec((bm, bn), lambda i, j: (i, j)),
        out_shape=jax.ShapeDtypeStruct((m, n), out_dtype),
        compiler_params=_cparams(("parallel", "parallel")), name=name)(*args)


def _as_tuple(x):
    return tuple(x) if isinstance(x, (tuple, list)) else (x,)


def _full_spec(p):
    nd = p.ndim
    return pl.BlockSpec(p.shape, lambda i, _nd=nd: (0,) * _nd)


def _rows(fn, rows, params, outs, *, name, tile=256, accs=()):
    length = rows[0].shape[0]
    tile = min(tile, length)
    nr, npar, no = len(rows), len(params), len(outs)

    def body(*refs):
        r, p, o = refs[:nr], refs[nr:nr + npar], refs[nr + npar:]
        res = _as_tuple(fn(*[x[...].astype(f32) for x in r], *[x[...] for x in p]))
        for kk in range(no):
            o[kk][...] = res[kk].astype(o[kk].dtype)
        if accs:
            @pl.when(pl.program_id(0) == 0)
            def _():
                for kk in range(no, no + len(accs)):
                    o[kk][...] = jnp.zeros_like(o[kk])
            for kk in range(no, no + len(accs)):
                o[kk][...] += res[kk]

    in_specs = [pl.BlockSpec((tile, x.shape[1]), lambda i: (i, 0)) for x in rows] + [_full_spec(p) for p in params]
    out_specs = [pl.BlockSpec((tile, w), lambda i: (i, 0)) for w, _ in outs]
    out_shape = [jax.ShapeDtypeStruct((length, w), d) for w, d in outs]
    for s in accs:
        out_specs.append(pl.BlockSpec(s, lambda i, _nd=len(s): (0,) * _nd))
        out_shape.append(jax.ShapeDtypeStruct(s, f32))
    res = pl.pallas_call(body, grid=(length // tile,), in_specs=in_specs, out_specs=out_specs, out_shape=out_shape,
                         compiler_params=_cparams(("arbitrary",)), name=name)(*rows, *params)
    return res


def _rows_bwd(fn, rows, params, cts, *, name, rgrad, pgrad, tile=256, addends=None):
    addends = addends or {}
    length = rows[0].shape[0]
    tile = min(tile, length)
    nr, npar, nc = len(rows), len(params), len(cts)
    ridx = [i for i in range(nr) if rgrad[i] is not None]
    pidx = [i for i in range(npar) if pgrad[i]]
    aidx = sorted(addends)
    na = len(aidx)

    def body(*refs):
        r, p, c = refs[:nr], refs[nr:nr + npar], refs[nr + npar:nr + npar + nc]
        ad = refs[nr + npar + nc:nr + npar + nc + na]
        o = refs[nr + npar + nc + na:]
        rv = [x[...].astype(f32) for x in r]
        pv = [x[...] for x in p]
        cv = tuple(x[...].astype(f32) for x in c)

        def g(*d):
            rr, pp = list(rv), list(pv)
            for n_, i_ in enumerate(ridx):
                rr[i_] = d[n_]
            for n_, i_ in enumerate(pidx):
                pp[i_] = d[len(ridx) + n_]
            return _as_tuple(fn(*rr, *pp))

        _, vjp = jax.vjp(g, *[rv[i] for i in ridx], *[pv[i] for i in pidx])
        grads = vjp(cv)
        for n_, i_ in enumerate(ridx):
            val = grads[n_]
            if i_ in addends:
                val = val + ad[aidx.index(i_)][...].astype(f32)
            o[n_][...] = val.astype(o[n_].dtype)
        if pidx:
            @pl.when(pl.program_id(0) == 0)
            def _():
                for n_ in range(len(pidx)):
                    o[len(ridx) + n_][...] = jnp.zeros_like(o[len(ridx) + n_])
            for n_ in range(len(pidx)):
                o[len(ridx) + n_][...] += grads[len(ridx) + n_]

    row_spec = lambda x: pl.BlockSpec((tile, x.shape[1]), lambda i: (i, 0))
    in_specs = ([row_spec(x) for x in rows] + [_full_spec(p) for p in params] + [row_spec(x) for x in cts]
                + [row_spec(addends[i]) for i in aidx])
    out_specs = [row_spec(rows[i]) for i in ridx] + [_full_spec(params[i]) for i in pidx]
    out_shape = ([jax.ShapeDtypeStruct(rows[i].shape, rgrad[i]) for i in ridx]
                 + [jax.ShapeDtypeStruct(params[i].shape, f32) for i in pidx])
    res = pl.pallas_call(body, grid=(length // tile,), in_specs=in_specs, out_specs=out_specs, out_shape=out_shape,
                         compiler_params=_cparams(("arbitrary",)), name=name)(
        *rows, *params, *cts, *[addends[i] for i in aidx])
    return list(res[:len(ridx)]), list(res[len(ridx):])


def _rms(x, g):
    return x * lax.rsqrt(jnp.mean(x * x, axis=-1, keepdims=True) + EPS) * g


def _rms_twice(x, g):
    y = _rms(x, g)
    return y, y


def _silu(x):
    return x * jax.nn.sigmoid(x)


def _shift_down(x, s):
    rows = lax.broadcasted_iota(jnp.int32, x.shape, 0)
    return jnp.where(rows >= s, pltpu.roll(x, s, axis=0), 0.0)


def _shift_up(x, s):
    n = x.shape[0]
    rows = lax.broadcasted_iota(jnp.int32, x.shape, 0)
    return jnp.where(rows < n - s, pltpu.roll(x, n - s, axis=0), 0.0)


_CONV_COLS = 128


def _conv_gate_fwd(u, cw, cb, *, name):
    length, two_f = u.shape
    ff = two_f // 2
    nb = ff // _CONV_COLS

    def body(ug, uv, wg, wv, bg, bv, o):
        def conv(x_ref, w_ref, b_ref):
            x = x_ref[...].astype(f32)
            return (w_ref[2:3, :] * x + w_ref[1:2, :] * _shift_down(x, 1) + w_ref[0:1, :] * _shift_down(x, 2)
                    + b_ref[...])
        o[...] = (_silu(conv(ug, wg, bg)) * conv(uv, wv, bv)).astype(o.dtype)

    blk = lambda r, off: pl.BlockSpec((r, _CONV_COLS), lambda j, _o=off: (0, j + _o))
    return pl.pallas_call(
        body, grid=(nb,),
        in_specs=[blk(length, 0), blk(length, nb), blk(CONV_W, 0), blk(CONV_W, nb), blk(1, 0), blk(1, nb)],
        out_specs=blk(length, 0), out_shape=jax.ShapeDtypeStruct((length, ff), bf16),
        compiler_params=_cparams(("parallel",)), name=name)(u, u, cw, cw, cb, cb)


def _conv_gate_bwd(u, cw, cb, da, *, name):
    length, two_f = u.shape
    ff = two_f // 2
    nb = ff // _CONV_COLS

    def body(ug, uv, wg, wv, bg, bv, da_ref, dug, duv, dwg, dwv, dbg, dbv):
        def conv(x, w_ref, b_ref):
            x1, x2 = _shift_down(x, 1), _shift_down(x, 2)
            return w_ref[2:3, :] * x + w_ref[1:2, :] * x1 + w_ref[0:1, :] * x2 + b_ref[...], x1, x2

        xg, xv = ug[...].astype(f32), uv[...].astype(f32)
        g, xg1, xg2 = conv(xg, wg, bg)
        v, xv1, xv2 = conv(xv, wv, bv)
        d = da_ref[...].astype(f32)
        sg = jax.nn.sigmoid(g)
        dg = d * v * (sg * (1.0 + g * (1.0 - sg)))
        dv = d * (g * sg)

        def back(dy, x, x1, x2, w_ref, du_ref, dw_ref, db_ref):
            du_ref[...] = (w_ref[2:3, :] * dy + w_ref[1:2, :] * _shift_up(dy, 1)
                           + w_ref[0:1, :] * _shift_up(dy, 2)).astype(du_ref.dtype)
            dw_ref[2:3, :] = jnp.sum(dy * x, axis=0, keepdims=True)
            dw_ref[1:2, :] = jnp.sum(dy * x1, axis=0, keepdims=True)
            dw_ref[0:1, :] = jnp.sum(dy * x2, axis=0, keepdims=True)
            db_ref[...] = jnp.sum(dy, axis=0, keepdims=True)

        back(dg, xg, xg1, xg2, wg, dug, dwg, dbg)
        back(dv, xv, xv1, xv2, wv, duv, dwv, dbv)

    blk = lambda r, off: pl.BlockSpec((r, _CONV_COLS), lambda j, _o=off: (0, j + _o))
    sds = jax.ShapeDtypeStruct
    dug, duv, dwg, dwv, dbg, dbv = pl.pallas_call(
        body, grid=(nb,),
        in_specs=[blk(length, 0), blk(length, nb), blk(CONV_W, 0), blk(CONV_W, nb), blk(1, 0), blk(1, nb),
                  blk(length, 0)],
        out_specs=[blk(length, 0), blk(length, 0), blk(CONV_W, 0), blk(CONV_W, 0), blk(1, 0), blk(1, 0)],
        out_shape=[sds((length, ff), bf16), sds((length, ff), bf16), sds((CONV_W, ff), f32), sds((CONV_W, ff), f32),
                   sds((1, ff), f32), sds((1, ff), f32)],
        compiler_params=_cparams(("parallel",)), name=name)(u, u, cw, cw, cb, cb, da)
    return dug, duv, jnp.concatenate([dwg, dwv], axis=1), jnp.concatenate([dbg, dbv], axis=1)


def _ffn_fwd(h, w, tag):
    hf, = _rows(_rms, [h], [w["norm_ffn"]], [(h.shape[1], bf16)], name=f"ffn_norm_{tag}")
    u = _mm(hf, w["ffn_w_up"], out_dtype=bf16, name=f"ffn_up_{tag}")
    a = _conv_gate_fwd(u, w["ffn_conv_w"], w["ffn_conv_b"], name=f"ffn_conv_{tag}")
    out = _mm(a, w["ffn_w_down"], add=h, name=f"ffn_down_{tag}")
    return out, (h, hf, u, a)


def _ffn_bwd(dout, w, saved, tag):
    h, hf, u, a = saved
    ff = a.shape[1]
    da = _mm(dout, w["ffn_w_down"], tb=True, out_dtype=bf16, name=f"ffn_da_{tag}")
    g = {"ffn_w_down": _mm(a, dout, ta=True, name=f"ffn_dwdown_{tag}")}
    dug, duv, g["ffn_conv_w"], g["ffn_conv_b"] = _conv_gate_bwd(u, w["ffn_conv_w"], w["ffn_conv_b"], da,
                                                                name=f"ffn_dconv_{tag}")
    dhf = _mm(dug, w["ffn_w_up"], tb=True, kslab=(0, 2), name=f"ffn_dhf_g_{tag}")
    dhf = _mm(duv, w["ffn_w_up"], tb=True, kslab=(1, 2), add=dhf, out_dtype=bf16, name=f"ffn_dhf_v_{tag}")
    g["ffn_w_up"] = _mm(hf, dug, ta=True, b2=duv, name=f"ffn_dwup_{tag}")
    (dh,), (g["norm_ffn"],) = _rows_bwd(_rms, [h], [w["norm_ffn"]], [dhf], rgrad=[f32], pgrad=[True],
                                        addends={0: dout}, name=f"ffn_dnorm_{tag}")
    return dh, g


def _xattn_fn(qx, kx, vx, qg, kg):
    outs = []
    for hh in range(XA_HEADS):
        sl = slice(hh * XA_DIM, (hh + 1) * XA_DIM)
        q = _rms(qx[:, sl], qg).astype(bf16)
        k = _rms(kx[:, sl], kg).astype(bf16)
        s = lax.dot_general(q, k, _NT, preferred_element_type=f32) * (XA_DIM ** -0.5)
        s = s - jnp.max(s, axis=-1, keepdims=True)
        p = jnp.exp(s)
        p = p / jnp.sum(p, axis=-1, keepdims=True)
        outs.append(jnp.dot(p.astype(bf16), vx[:, sl].astype(bf16), preferred_element_type=f32))
    return jnp.concatenate(outs, axis=-1)


def _xattn_fwd(h, mem, w, tag):
    d = h.shape[1]
    hx, = _rows(_rms, [h], [w["norm_xa"]], [(d, bf16)], name=f"xa_norm_{tag}")
    qx = _mm(hx, w["xa_wq"], name=f"xa_q_{tag}")
    m, = _rows(_rms, [mem], [w["norm_mem"]], [(d, bf16)], name=f"xa_mnorm_{tag}")
    kx = _mm(m, w["xa_wk"], name=f"xa_k_{tag}")
    vx = _mm(m, w["xa_wv"], name=f"xa_v_{tag}")
    o, = _rows(_xattn_fn, [qx], [kx, vx, w["xa_q_norm"], w["xa_k_norm"]], [(d, bf16)], name=f"xa_attn_{tag}")
    out = _mm(o, w["xa_wo"], add=h, name=f"xa_o_{tag}")
    return out, (h, hx, qx, m, kx, vx, o)


def _xattn_bwd(dout, mem, w, saved, tag):
    h, hx, qx, m, kx, vx, o = saved
    g = {}
    do = _mm(dout, w["xa_wo"], tb=True, out_dtype=bf16, name=f"xa_do_{tag}")
    g["xa_wo"] = _mm(o, dout, ta=True, name=f"xa_dwo_{tag}")
    (dqx,), (dkx, dvx, g["xa_q_norm"], g["xa_k_norm"]) = _rows_bwd(
        _xattn_fn, [qx], [kx, vx, w["xa_q_norm"], w["xa_k_norm"]], [do], rgrad=[bf16], pgrad=[True] * 4,
        name=f"xa_dattn_{tag}")
    dhx = _mm(dqx, w["xa_wq"], tb=True, out_dtype=bf16, name=f"xa_dhx_{tag}")
    g["xa_wq"] = _mm(hx, dqx, ta=True, name=f"xa_dwq_{tag}")
    (dh,), (g["norm_xa"],) = _rows_bwd(_rms, [h], [w["norm_xa"]], [dhx], rgrad=[f32], pgrad=[True],
                                       addends={0: dout}, name=f"xa_dnorm_{tag}")
    dm = _mm(dkx, w["xa_wk"], tb=True, name=f"xa_dm_k_{tag}")
    dm = _mm(dvx, w["xa_wv"], tb=True, add=dm, name=f"xa_dm_v_{tag}")
    g["xa_wk"] = _mm(m, dkx, ta=True, name=f"xa_dwk_{tag}")
    g["xa_wv"] = _mm(m, dvx, ta=True, name=f"xa_dwv_{tag}")
    _, (g["norm_mem"],) = _rows_bwd(_rms, [mem], [w["norm_mem"]], [dm], rgrad=[None], pgrad=[True],
                                    name=f"xa_dmnorm_{tag}")
    return dh, g


_HG_GROUP = 4


def _hg_chunk(q, k, v, g, st):
    c = q.shape[0]
    tri = (lax.broadcasted_iota(jnp.int32, (c, c), 0) >= lax.broadcasted_iota(jnp.int32, (c, c), 1)).astype(f32)
    b = jnp.dot(tri, g, precision=lax.Precision.HIGHEST, preferred_element_type=f32)
    bend = jnp.sum(g, axis=0, keepdims=True)
    o_inter = lax.dot_general((q * jnp.exp(b)).astype(bf16), st.astype(bf16), _NT, preferred_element_type=f32)
    kd = k * jnp.exp(bend - b)
    st_new = st * jnp.exp(bend) + lax.dot_general(v.astype(bf16), kd.astype(bf16), _TN, preferred_element_type=f32)
    outs = []
    for i in range(c // HG_SUB):
        lo, n = HG_SUB * i, HG_SUB * (i + 1)
        ref = jnp.sum(g[:lo], axis=0, keepdims=True) if i else jnp.zeros((1, g.shape[1]), f32)
        qh = q[lo:n] * jnp.exp(b[lo:n] - ref)
        kh = k[:n] * jnp.exp(ref - b[:n])
        a = lax.dot_general(qh.astype(bf16), kh.astype(bf16), _NT, preferred_element_type=f32)
        keep = (lax.broadcasted_iota(jnp.int32, (HG_SUB, n), 1)
                <= lo + lax.broadcasted_iota(jnp.int32, (HG_SUB, n), 0))
        a = jnp.where(keep, a, 0.0)
        outs.append(jnp.dot(a.astype(bf16), v[:n].astype(bf16), preferred_element_type=f32))
    return jnp.concatenate(outs, axis=0) + o_inter, st_new


def _hg_fwd(q, k, v, g, *, name):
    length = q.shape[0]
    rows = _HG_GROUP * HG_CHUNK
    ng = length // rows
    nc = length // HG_CHUNK

    def body(q_ref, k_ref, v_ref, g_ref, o_ref, st_ref, state):
        @pl.when(pl.program_id(0) == 0)
        def _():
            state[...] = jnp.zeros_like(state)

        for ci in range(_HG_GROUP):
            sl = slice(ci * HG_CHUNK, (ci + 1) * HG_CHUNK)
            for h in range(HG_HEADS):
                hs = slice(h * HG_DIM, (h + 1) * HG_DIM)
                st = state[h]
                st_ref[h, ci] = st
                o, st_new = _hg_chunk(q_ref[sl, hs], k_ref[sl, hs], v_ref[sl, hs], g_ref[sl, hs], st)
                o_ref[sl, hs] = o
                state[h] = st_new

    blk = pl.BlockSpec((rows, HG_WIDTH), lambda c: (c, 0))
    return pl.pallas_call(
        body, grid=(ng,), in_specs=[blk] * 4,
        out_specs=[blk, pl.BlockSpec((HG_HEADS, _HG_GROUP, HG_DIM, HG_DIM), lambda c: (0, c, 0, 0))],
        out_shape=[jax.ShapeDtypeStruct((length, HG_WIDTH), f32),
                   jax.ShapeDtypeStruct((HG_HEADS, nc, HG_DIM, HG_DIM), f32)],
        scratch_shapes=[pltpu.VMEM((HG_HEADS, HG_DIM, HG_DIM), f32)],
        compiler_params=_cparams(("arbitrary",)), name=name)(q, k, v, g)


def _hg_bwd(q, k, v, g, states, do, *, name):
    length = q.shape[0]
    rows = _HG_GROUP * HG_CHUNK
    ng = length // rows

    def body(q_ref, k_ref, v_ref, g_ref, st_ref, do_ref, dq_ref, dk_ref, dv_ref, dg_ref, dstate):
        @pl.when(pl.program_id(0) == 0)
        def _():
            dstate[...] = jnp.zeros_like(dstate)

        for ci in reversed(range(_HG_GROUP)):
            sl = slice(ci * HG_CHUNK, (ci + 1) * HG_CHUNK)
            for h in range(HG_HEADS):
                hs = slice(h * HG_DIM, (h + 1) * HG_DIM)
                _, vjp = jax.vjp(_hg_chunk, q_ref[sl, hs], k_ref[sl, hs], v_ref[sl, hs], g_ref[sl, hs], st_ref[h, ci])
                dq, dk, dv, dg, dst = vjp((do_ref[sl, hs], dstate[h]))
                dq_ref[sl, hs] = dq
                dk_ref[sl, hs] = dk
                dv_ref[sl, hs] = dv
                dg_ref[sl, hs] = dg
                dstate[h] = dst

    blk = pl.BlockSpec((rows, HG_WIDTH), lambda c: (ng - 1 - c, 0))
    sds = jax.ShapeDtypeStruct((length, HG_WIDTH), f32)
    return pl.pallas_call(
        body, grid=(ng,),
        in_specs=[blk] * 4 + [pl.BlockSpec((HG_HEADS, _HG_GROUP, HG_DIM, HG_DIM), lambda c: (0, ng - 1 - c, 0, 0)), blk],
        out_specs=[blk] * 4, out_shape=[sds] * 4,
        scratch_shapes=[pltpu.VMEM((HG_HEADS, HG_DIM, HG_DIM), f32)],
        compiler_params=_cparams(("arbitrary",)), name=name)(q, k, v, g, states, do)


_ATT_BLK = 256
_ATT_SCALE = MLA_QK ** -0.5
_NEG = -1e30


def _att_mask(i, j, t):
    rows = i * t + lax.broadcasted_iota(jnp.int32, (t, t), 0)
    cols = j * t + lax.broadcasted_iota(jnp.int32, (t, t), 1)
    return cols <= rows


def _att_fwd(q, k, v, *, name):
    length = q.shape[0]
    t = min(_ATT_BLK, length)
    nq = length // t
    qw, vw = MLA_QK_PAD, MLA_V
    heads = range(MLA_HEADS)

    def body(q_ref, k_ref, v_ref, o_ref, lse_ref):
        i = pl.program_id(0)
        qbs = [q_ref[:, h * qw:(h + 1) * qw] for h in heads]

        def step(j, carry):
            off = pl.multiple_of(j * t, t)
            keep = _att_mask(i, j, t)
            out = []
            for h in heads:
                m, l, acc = carry[h]
                ks = k_ref[pl.ds(off, t), h * qw:(h + 1) * qw]
                vs = v_ref[pl.ds(off, t), h * vw:(h + 1) * vw]
                s = lax.dot_general(qbs[h], ks, _NT, preferred_element_type=f32) * _ATT_SCALE
                s = jnp.where(keep, s, _NEG)
                m_new = jnp.maximum(m, jnp.max(s, axis=-1, keepdims=True))
                alpha = jnp.exp(m - m_new)
                p = jnp.exp(s - m_new)
                l = alpha * l + jnp.sum(p, axis=-1, keepdims=True)
                acc = alpha * acc + jnp.dot(p.astype(bf16), vs, preferred_element_type=f32)
                out.append((m_new, l, acc))
            return tuple(out)

        init = tuple((jnp.full((t, 1), _NEG, f32), jnp.zeros((t, 1), f32), jnp.zeros((t, vw), f32)) for _ in heads)
        res = lax.fori_loop(0, i + 1, step, init)
        for h in heads:
            m, l, acc = res[h]
            o_ref[:, h * vw:(h + 1) * vw] = (acc / l).astype(o_ref.dtype)
            lse_ref[:, h * vw:(h + 1) * vw] = jnp.broadcast_to(m + jnp.log(l), (t, vw))

    return pl.pallas_call(
        body, grid=(nq,),
        in_specs=[pl.BlockSpec((t, q.shape[1]), lambda i: (i, 0)), pl.BlockSpec(k.shape, lambda i: (0, 0)),
                  pl.BlockSpec(v.shape, lambda i: (0, 0))],
        out_specs=[pl.BlockSpec((t, v.shape[1]), lambda i: (i, 0))] * 2,
        out_shape=[jax.ShapeDtypeStruct(v.shape, bf16), jax.ShapeDtypeStruct(v.shape, f32)],
        compiler_params=_cparams(("arbitrary",)), name=name)(q, k, v)


def _att_bwd(q, k, v, o, lse, do, *, name):
    length = q.shape[0]
    t = min(_ATT_BLK, length)
    nq = length // t
    qw, vw = MLA_QK_PAD, MLA_V
    heads = range(MLA_HEADS)

    def dq_body(q_ref, k_ref, v_ref, o_ref, lse_ref, do_ref, dq_ref, delta_ref):
        i = pl.program_id(0)
        qbs = [q_ref[:, h * qw:(h + 1) * qw] for h in heads]
        dobs = [do_ref[:, h * vw:(h + 1) * vw] for h in heads]
        lses = [lse_ref[:, h * vw:h * vw + 1] for h in heads]
        deltas = [jnp.sum(dobs[h].astype(f32) * o_ref[:, h * vw:(h + 1) * vw].astype(f32), axis=-1, keepdims=True)
                  for h in heads]

        def step(j, dqs):
            off = pl.multiple_of(j * t, t)
            keep = _att_mask(i, j, t)
            out = []
            for h in heads:
                ks = k_ref[pl.ds(off, t), h * qw:(h + 1) * qw]
                vs = v_ref[pl.ds(off, t), h * vw:(h + 1) * vw]
                s = lax.dot_general(qbs[h], ks, _NT, preferred_element_type=f32) * _ATT_SCALE
                p = jnp.where(keep, jnp.exp(s - lses[h]), 0.0)
                dp = lax.dot_general(dobs[h], vs, _NT, preferred_element_type=f32)
                ds = p * (dp - deltas[h]) * _ATT_SCALE
                out.append(dqs[h] + jnp.dot(ds.astype(bf16), ks, preferred_element_type=f32))
            return tuple(out)

        dqs = lax.fori_loop(0, i + 1, step, tuple(jnp.zeros((t, qw), f32) for _ in heads))
        for h in heads:
            dq_ref[:, h * qw:(h + 1) * qw] = dqs[h].astype(dq_ref.dtype)
            delta_ref[:, h * vw:(h + 1) * vw] = jnp.broadcast_to(deltas[h], (t, vw))

    qblk = pl.BlockSpec((t, q.shape[1]), lambda i: (i, 0))
    vblk = pl.BlockSpec((t, v.shape[1]), lambda i: (i, 0))
    qfull = pl.BlockSpec(q.shape, lambda i: (0, 0))
    vfull = pl.BlockSpec(v.shape, lambda i: (0, 0))
    dq, delta = pl.pallas_call(
        dq_body, grid=(nq,), in_specs=[qblk, qfull, vfull, vblk, vblk, vblk], out_specs=[qblk, vblk],
        out_shape=[jax.ShapeDtypeStruct(q.shape, bf16), jax.ShapeDtypeStruct(lse.shape, f32)],
        compiler_params=_cparams(("arbitrary",)), name=name + "_dq")(q, k, v, o, lse, do)

    def dkv_body(k_ref, v_ref, q_ref, do_ref, lse_ref, delta_ref, dk_ref, dv_ref):
        j = pl.program_id(0)
        kbs = [k_ref[:, h * qw:(h + 1) * qw] for h in heads]
        vbs = [v_ref[:, h * vw:(h + 1) * vw] for h in heads]

        def step(i, carry):
            off = pl.multiple_of(i * t, t)
            keep = _att_mask(i, j, t)
            out = []
            for h in heads:
                dk, dv = carry[h]
                qs = q_ref[pl.ds(off, t), h * qw:(h + 1) * qw]
                dos = do_ref[pl.ds(off, t), h * vw:(h + 1) * vw]
                lse_i = lse_ref[pl.ds(off, t), h * vw:h * vw + 1]
                delta_i = delta_ref[pl.ds(off, t), h * vw:h * vw + 1]
                s = lax.dot_general(qs, kbs[h], _NT, preferred_element_type=f32) * _ATT_SCALE
                p = jnp.where(keep, jnp.exp(s - lse_i), 0.0)
                dv = dv + lax.dot_general(p.astype(bf16), dos, _TN, preferred_element_type=f32)
                dp = lax.dot_general(dos, vbs[h], _NT, preferred_element_type=f32)
                ds = p * (dp - delta_i) * _ATT_SCALE
                dk = dk + lax.dot_general(ds.astype(bf16), qs, _TN, preferred_element_type=f32)
                out.append((dk, dv))
            return tuple(out)

        res = lax.fori_loop(j, nq, step, tuple((jnp.zeros((t, qw), f32), jnp.zeros((t, vw), f32)) for _ in heads))
        for h in heads:
            dk_ref[:, h * qw:(h + 1) * qw] = res[h][0].astype(dk_ref.dtype)
            dv_ref[:, h * vw:(h + 1) * vw] = res[h][1].astype(dv_ref.dtype)

    dk, dv = pl.pallas_call(
        dkv_body, grid=(nq,), in_specs=[qblk, vblk, qfull, vfull, vfull, vfull], out_specs=[qblk, vblk],
        out_shape=[jax.ShapeDtypeStruct(k.shape, bf16), jax.ShapeDtypeStruct(v.shape, bf16)],
        compiler_params=_cparams(("arbitrary",)), name=name + "_dkv")(k, v, q, do, lse, delta)
    return dq, dk, dv


_C_Q = 4 * HG_WIDTH
_C_KV = _C_Q + MLA_Q_RANK
_C_KPE = _C_KV + MLA_KV_RANK


def _rms_n(x, g, n):
    return x * lax.rsqrt(jnp.sum(x * x, axis=-1, keepdims=True) * (1.0 / n) + EPS) * g


def _mix_a(proj, l0, l1, q_a_norm, kv_a_norm):
    lb = jax.nn.sigmoid(l0 - l1)
    f = lb + (1.0 - lb) * jax.nn.sigmoid(proj[:, HG_WIDTH:2 * HG_WIDTH])
    qf = _silu(proj[:, :HG_WIDTH])
    v = proj[:, 2 * HG_WIDTH:3 * HG_WIDTH]
    cqn = _rms(proj[:, _C_Q:_C_KV], q_a_norm)
    ckvn = _rms(proj[:, _C_KV:_C_KPE], kv_a_norm)
    return qf, 1.0 - f, v, jnp.log(f), cqn, ckvn


def _mix_b(qraw, kvraw, proj, cos, sin, qn_nope, qn_rope, kn_nope, kn_rope, perm):
    def rope(x):
        return x * cos + jnp.dot(x, perm, precision=lax.Precision.HIGHEST, preferred_element_type=f32) * sin

    kpe = rope(_rms_n(proj[:, _C_KPE:], kn_rope, MLA_ROPE))
    qs, ks, vs = [], [], []
    for hh in range(MLA_HEADS):
        base = hh * MLA_QK_PAD
        qs.append(_rms(qraw[:, base:base + MLA_NOPE], qn_nope))
        qs.append(rope(_rms_n(qraw[:, base + MLA_NOPE:base + MLA_QK_PAD], qn_rope, MLA_ROPE)))
        ks.append(_rms(kvraw[:, base:base + MLA_NOPE], kn_nope))
        ks.append(kpe)
        vs.append(kvraw[:, base + MLA_NOPE:base + MLA_QK_PAD])
    return jnp.concatenate(qs, axis=-1), jnp.concatenate(ks, axis=-1), jnp.concatenate(vs, axis=-1)


def _mix_c(o_hg, proj, o_mla, hg_out_norm):
    parts = []
    for hh in range(HG_HEADS):
        sl = slice(hh * HG_DIM, (hh + 1) * HG_DIM)
        parts.append(_rms(o_hg[:, sl], hg_out_norm[:, sl]))
    o = jnp.concatenate(parts, axis=-1) * _silu(proj[:, 3 * HG_WIDTH:4 * HG_WIDTH])
    return jnp.concatenate([o, o_mla], axis=-1)


def _rope_perm():
    p = np.zeros((128, 128), np.float32)
    half = MLA_ROPE // 2
    for i in range(half):
        p[i + half, i] = -1.0
        p[i, i + half] = 1.0
    return jnp.asarray(p)


def _mixer_fwd(h, cos, sin, w, tag):
    d = h.shape[1]
    hn, = _rows(_rms, [h], [w["norm_mix"]], [(d, bf16)], name=f"mix_norm_{tag}")
    proj = _mm(hn, w["mix_w_in"], name=f"mix_in_{tag}")
    pa = [w["lb0"], w["lb1"], w["mla_q_a_norm"], w["mla_kv_a_norm"]]
    qf, kk, vv, logf, cqn, ckvn = _rows(
        _mix_a, [proj], pa, [(HG_WIDTH, f32)] * 4 + [(MLA_Q_RANK, bf16), (MLA_KV_RANK, bf16)], name=f"mix_a_{tag}")
    o_hg, states = _hg_fwd(qf, kk, vv, logf, name=f"hg_fwd_{tag}")
    qraw = _mm(cqn, w["mla_w_uq"], name=f"mla_uq_{tag}")
    kvraw = _mm(ckvn, w["mla_w_ukv"], name=f"mla_ukv_{tag}")
    pb = [w["mla_qn_nope"], w["mla_qn_rope"], w["mla_kn_nope"], w["mla_kn_rope"], w["rope_perm"]]
    qfull, kfull, vfull = _rows(_mix_b, [qraw, kvraw, proj, cos, sin], pb,
                                [(MLA_HEADS * MLA_QK_PAD, bf16)] * 2 + [(MLA_HEADS * MLA_V, bf16)],
                                name=f"mix_b_{tag}")
    o_mla, lse = _att_fwd(qfull, kfull, vfull, name=f"att_fwd_{tag}")
    mixin, = _rows(_mix_c, [o_hg, proj, o_mla], [w["hg_out_norm"]], [(d, bf16)], name=f"mix_c_{tag}")
    out = _mm(mixin, w["mix_w_out"], add=h, name=f"mix_out_{tag}")
    return out, (h, hn, proj, qf, kk, vv, logf, cqn, ckvn, o_hg, states, qraw, kvraw, qfull, kfull, vfull, o_mla,
                 lse, mixin)


def _mixer_bwd(dout, cos, sin, w, saved, tag, on_w_out=None):
    (h, hn, proj, qf, kk, vv, logf, cqn, ckvn, o_hg, states, qraw, kvraw, qfull, kfull, vfull, o_mla, lse,
     mixin) = saved
    g = {}
    dmixin = _mm(dout, w["mix_w_out"], tb=True, name=f"mix_dmixin_{tag}")
    g["mix_w_out"] = _mm(mixin, dout, ta=True, name=f"mix_dwout_{tag}")
    if on_w_out is not None:
        dmixin = on_w_out(g["mix_w_out"], dmixin)
    (do_hg, dproj_c, do_mla), (g["hg_out_norm"],) = _rows_bwd(
        _mix_c, [o_hg, proj, o_mla], [w["hg_out_norm"]], [dmixin], rgrad=[f32, f32, bf16], pgrad=[True],
        name=f"mix_dc_{tag}")
    dqfull, dkfull, dvfull = _att_bwd(qfull, kfull, vfull, o_mla, lse, do_mla, name=f"att_bwd_{tag}")
    pb = [w["mla_qn_nope"], w["mla_qn_rope"], w["mla_kn_nope"], w["mla_kn_rope"], w["rope_perm"]]
    (dqraw, dkvraw, dproj_b), pg = _rows_bwd(
        _mix_b, [qraw, kvraw, proj, cos, sin], pb, [dqfull, dkfull, dvfull],
        rgrad=[bf16, bf16, f32, None, None], pgrad=[True, True, True, True, False], addends={2: dproj_c},
        name=f"mix_db_{tag}")
    g["mla_qn_nope"], g["mla_qn_rope"], g["mla_kn_nope"], g["mla_kn_rope"] = pg
    dcqn = _mm(dqraw, w["mla_w_uq"], tb=True, name=f"mla_dcq_{tag}")
    g["mla_w_uq"] = _mm(cqn, dqraw, ta=True, name=f"mla_dwuq_{tag}")
    dckvn = _mm(dkvraw, w["mla_w_ukv"], tb=True, name=f"mla_dckv_{tag}")
    g["mla_w_ukv"] = _mm(ckvn, dkvraw, ta=True, name=f"mla_dwukv_{tag}")
    dqf, dkk, dvv, dlogf = _hg_bwd(qf, kk, vv, logf, states, do_hg, name=f"hg_bwd_{tag}")
    pa = [w["lb0"], w["lb1"], w["mla_q_a_norm"], w["mla_kv_a_norm"]]
    (dproj,), (g["lb0"], g["lb1"], g["mla_q_a_norm"], g["mla_kv_a_norm"]) = _rows_bwd(
        _mix_a, [proj], pa, [dqf, dkk, dvv, dlogf, dcqn, dckvn], rgrad=[bf16], pgrad=[True] * 4,
        addends={0: dproj_b}, name=f"mix_da_{tag}")
    dhn = _mm(dproj, w["mix_w_in"], tb=True, out_dtype=bf16, name=f"mix_dhn_{tag}")
    g["mix_w_in"] = _mm(hn, dproj, ta=True, name=f"mix_dwin_{tag}")
    (dh,), (g["norm_mix"],) = _rows_bwd(_rms, [h], [w["norm_mix"]], [dhn], rgrad=[f32], pgrad=[True],
                                        addends={0: dout}, name=f"mix_dnorm_{tag}")
    return dh, g


def _rope_tables(positions):
    inv_freq = 1.0 / (ROPE_BASE ** (jnp.arange(0, MLA_ROPE, 2, dtype=f32) / MLA_ROPE))
    ang = positions.astype(f32)[:, None] * inv_freq
    z = jnp.zeros((positions.shape[0], 128 - MLA_ROPE), f32)
    return (jnp.concatenate([jnp.cos(ang), jnp.cos(ang), z], axis=1),
            jnp.concatenate([jnp.sin(ang), jnp.sin(ang), z], axis=1))


def _pad_cols(a, n):
    return jnp.pad(a, ((0, 0), (0, n - a.shape[1])))


def _even_weights(p, j, layer, dt):
    w_uq = p["mla_w_uq"][j].reshape(MLA_Q_RANK, MLA_HEADS, MLA_QK)
    w_uq = jnp.pad(w_uq, ((0, 0), (0, 0), (0, MLA_QK_PAD - MLA_QK))).reshape(MLA_Q_RANK, MLA_HEADS * MLA_QK_PAD)
    return dict(
        norm_mix=p["norm_mix"][layer][None], mix_w_in=_pad_cols(p["mix_w_in"][j], IN_PAD).astype(dt),
        lb0=p["hg_lb_logits"][0][None], lb1=p["hg_lb_logits"][1][None],
        mla_q_a_norm=p["mla_q_a_norm"][j][None], mla_kv_a_norm=p["mla_kv_a_norm"][j][None],
        mla_w_uq=w_uq.astype(dt), mla_w_ukv=p["mla_w_ukv"][j].astype(dt),
        mla_qn_nope=p["mla_qn_nope"][j][None], mla_qn_rope=_pad_cols(p["mla_qn_rope"][j][None], 128),
        mla_kn_nope=p["mla_kn_nope"][j][None], mla_kn_rope=_pad_cols(p["mla_kn_rope"][j][None], 128),
        rope_perm=_rope_perm(), hg_out_norm=p["hg_out_norm"][j][None], mix_w_out=p["mix_w_out"][j].astype(dt))


def _even_grads(g):
    w_uq = g["mla_w_uq"].reshape(MLA_Q_RANK, MLA_HEADS, MLA_QK_PAD)[:, :, :MLA_QK].reshape(MLA_Q_RANK, -1)
    return dict(
        norm_mix=g["norm_mix"], mix_w_in=g["mix_w_in"][:, :IN_WIDTH][None],
        hg_lb_logits=jnp.concatenate([g["lb0"], g["lb1"]], axis=0),
        mla_q_a_norm=g["mla_q_a_norm"], mla_kv_a_norm=g["mla_kv_a_norm"], mla_w_uq=w_uq[None],
        mla_w_ukv=g["mla_w_ukv"][None], mla_qn_nope=g["mla_qn_nope"], mla_qn_rope=g["mla_qn_rope"][:, :MLA_ROPE],
        mla_kn_nope=g["mla_kn_nope"], mla_kn_rope=g["mla_kn_rope"][:, :MLA_ROPE],
        hg_out_norm=g["hg_out_norm"], mix_w_out=g["mix_w_out"][None])


_S5_NB = 8
_S5_BW = 1024
_S5_HALF = 512
_S5_UC = 128
_S5_TIME = 512


def _bd_mm(a, b3, *, name, tb=False, out_dtype=f32):
    length = a.shape[0]
    rows_b, cols_b = b3.shape[0] // _S5_NB, b3.shape[1]
    ka, n = (cols_b, rows_b) if tb else (rows_b, cols_b)
    bm = _pick(length, (512, 256, 128))
    dims = _NT if tb else _NN

    def body(a_ref, b_ref, o_ref):
        o_ref[...] = lax.dot_general(a_ref[...].astype(bf16), b_ref[...].astype(bf16), dims,
                                     preferred_element_type=f32).astype(o_ref.dtype)

    return pl.pallas_call(
        body, grid=(length // bm, _S5_NB),
        in_specs=[pl.BlockSpec((bm, ka), lambda i, j: (i, j)), pl.BlockSpec((rows_b, cols_b), lambda i, j: (j, 0))],
        out_specs=pl.BlockSpec((bm, n), lambda i, j: (i, j)),
        out_shape=jax.ShapeDtypeStruct((length, _S5_NB * n), out_dtype),
        compiler_params=_cparams(("parallel", "parallel")), name=name)(a, b3)


def _bd_mm_tn(a, c, *, name):
    length = a.shape[0]
    ka, n = a.shape[1] // _S5_NB, c.shape[1] // _S5_NB
    bk = _pick(length, (512, 256, 128))
    nk = length // bk

    def body(a_ref, c_ref, o_ref):
        @pl.when(pl.program_id(1) == 0)
        def _():
            o_ref[...] = jnp.zeros_like(o_ref)
        o_ref[...] += lax.dot_general(a_ref[...].astype(bf16), c_ref[...].astype(bf16), _TN,
                                      preferred_element_type=f32)

    return pl.pallas_call(
        body, grid=(_S5_NB, nk),
        in_specs=[pl.BlockSpec((bk, ka), lambda j, q: (q, j)), pl.BlockSpec((bk, n), lambda j, q: (q, j))],
        out_specs=pl.BlockSpec((ka, n), lambda j, q: (j, 0)),
        out_shape=jax.ShapeDtypeStruct((_S5_NB * ka, n), f32),
        compiler_params=_cparams(("parallel", "arbitrary")), name=name)(a, c)


def _cmul(ar, ai, br, bi):
    return ar * br - ai * bi, ar * bi + ai * br


def _pow_table(ar, ai, descending):
    rows = lax.broadcasted_iota(jnp.int32, (8, ar.shape[1]), 0)
    tr = jnp.zeros((8, ar.shape[1]), f32)
    ti = jnp.zeros((8, ar.shape[1]), f32)
    pr, pi_ = ar, ai
    for r in range(8):
        sel = rows == ((7 - r) if descending else r)
        tr = jnp.where(sel, pr, tr)
        ti = jnp.where(sel, pi_, ti)
        pr, pi_ = _cmul(pr, pi_, ar, ai)
    return tr, ti


def _s5_scan_fwd(a, x, *, name):
    length = x.shape[0]
    tc = min(_S5_TIME, length)
    hw = _S5_HALF

    def body(a_ref, x_ref, o_ref, carry):
        @pl.when(pl.program_id(1) == 0)
        def _():
            carry[...] = jnp.zeros_like(carry)

        ar, ai = a_ref[:, :hw], a_ref[:, hw:]
        xr, xi = x_ref[:, :hw], x_ref[:, hw:]
        row8 = lax.broadcasted_iota(jnp.int32, (tc, hw), 0) & 7
        mr, mi = ar, ai
        for s in (1, 2, 4):
            sr, si = pltpu.roll(xr, s, axis=0), pltpu.roll(xi, s, axis=0)
            pr, pi_ = _cmul(mr, mi, sr, si)
            ok = row8 >= s
            xr = xr + jnp.where(ok, pr, 0.0)
            xi = xi + jnp.where(ok, pi_, 0.0)
            mr, mi = _cmul(mr, mi, mr, mi)
        o_ref[:, :hw] = xr
        o_ref[:, hw:] = xi
        tr, ti = _pow_table(ar, ai, False)
        cr, ci = carry[:, :hw], carry[:, hw:]
        for i in range(tc // 8):
            sl = slice(8 * i, 8 * i + 8)
            pr, pi_ = _cmul(tr, ti, cr, ci)
            o_ref[sl, :hw] = o_ref[sl, :hw] + pr
            o_ref[sl, hw:] = o_ref[sl, hw:] + pi_
            cr, ci = o_ref[8 * i + 7:8 * i + 8, :hw], o_ref[8 * i + 7:8 * i + 8, hw:]
        carry[:, :hw] = cr
        carry[:, hw:] = ci

    return pl.pallas_call(
        body, grid=(_S5_NB, length // tc),
        in_specs=[pl.BlockSpec((1, _S5_BW), lambda j, t: (0, j)), pl.BlockSpec((tc, _S5_BW), lambda j, t: (t, j))],
        out_specs=pl.BlockSpec((tc, _S5_BW), lambda j, t: (t, j)),
        out_shape=jax.ShapeDtypeStruct(x.shape, f32),
        scratch_shapes=[pltpu.VMEM((1, _S5_BW), f32)],
        compiler_params=_cparams(("parallel", "arbitrary")), name=name)(a, x)


def _s5_scan_bwd(a, hs, d, *, name):
    length = d.shape[0]
    tc = min(_S5_TIME, length)
    nt = length // tc
    hw = _S5_HALF

    def body(a_ref, h_ref, d_ref, g_ref, da_ref, carry):
        @pl.when(pl.program_id(1) == 0)
        def _():
            carry[...] = jnp.zeros_like(carry)
            da_ref[...] = jnp.zeros_like(da_ref)

        ar, ai = a_ref[:, :hw], -a_ref[:, hw:]
        xr, xi = d_ref[:, :hw], d_ref[:, hw:]
        rows = lax.broadcasted_iota(jnp.int32, (tc, hw), 0)
        row8 = rows & 7
        mr, mi = ar, ai
        for s in (1, 2, 4):
            sr, si = pltpu.roll(xr, tc - s, axis=0), pltpu.roll(xi, tc - s, axis=0)
            pr, pi_ = _cmul(mr, mi, sr, si)
            ok = row8 < 8 - s
            xr = xr + jnp.where(ok, pr, 0.0)
            xi = xi + jnp.where(ok, pi_, 0.0)
            mr, mi = _cmul(mr, mi, mr, mi)
        g_ref[:, :hw] = xr
        g_ref[:, hw:] = xi
        tr, ti = _pow_table(ar, ai, True)
        cr0, ci0 = carry[:, :hw], carry[:, hw:]
        cr, ci = cr0, ci0
        for i in reversed(range(tc // 8)):
            sl = slice(8 * i, 8 * i + 8)
            pr, pi_ = _cmul(tr, ti, cr, ci)
            g_ref[sl, :hw] = g_ref[sl, :hw] + pr
            g_ref[sl, hw:] = g_ref[sl, hw:] + pi_
            cr, ci = g_ref[8 * i:8 * i + 1, :hw], g_ref[8 * i:8 * i + 1, hw:]
        carry[:, :hw] = cr
        carry[:, hw:] = ci
        last = rows == tc - 1
        gnr = jnp.where(last, cr0, pltpu.roll(g_ref[:, :hw], tc - 1, axis=0))
        gni = jnp.where(last, ci0, pltpu.roll(g_ref[:, hw:], tc - 1, axis=0))
        hr, hi = h_ref[:, :hw], h_ref[:, hw:]
        da_ref[:, :hw] += jnp.sum(hr * gnr + hi * gni, axis=0, keepdims=True)
        da_ref[:, hw:] += jnp.sum(hr * gni - hi * gnr, axis=0, keepdims=True)

    blk = pl.BlockSpec((tc, _S5_BW), lambda j, t: (nt - 1 - t, j))
    row = pl.BlockSpec((1, _S5_BW), lambda j, t: (0, j))
    return pl.pallas_call(
        body, grid=(_S5_NB, nt), in_specs=[row, blk, blk], out_specs=[blk, row],
        out_shape=[jax.ShapeDtypeStruct(d.shape, f32), jax.ShapeDtypeStruct((1, _S5_NB * _S5_BW), f32)],
        scratch_shapes=[pltpu.VMEM((1, _S5_BW), f32)],
        compiler_params=_cparams(("parallel", "arbitrary")), name=name)(a, hs, d)


def _s5_disc(lr, li, ldt, btr, bti, expand):
    dt = jnp.exp(ldt)
    mag = jnp.exp(lr * dt)
    abr = mag * jnp.cos(li * dt)
    abi = mag * jnp.sin(li * dt)
    den = lr * lr + li * li
    zr = ((abr - 1.0) * lr + abi * li) / den
    zi = (abi * lr - (abr - 1.0) * li) / den
    zr = jnp.dot(zr, expand, precision=lax.Precision.HIGHEST, preferred_element_type=f32)
    zi = jnp.dot(zi, expand, precision=lax.Precision.HIGHEST, preferred_element_type=f32)
    return abr, abi, zr * btr - zi * bti, zr * bti + zi * btr


def _s5_disc_fwd(args, *, name):
    def body(*refs):
        res = _s5_disc(*[r[...] for r in refs[:6]])
        for o, v in zip(refs[6:], res):
            o[...] = v

    sds = jax.ShapeDtypeStruct
    return pl.pallas_call(body, out_shape=[sds(args[0].shape, f32)] * 2 + [sds(args[3].shape, f32)] * 2,
                          name=name)(*args)


def _s5_disc_bwd(args, cts, *, name):
    def body(*refs):
        vals = [r[...] for r in refs[:6]]
        _, vjp = jax.vjp(lambda *d: _s5_disc(*d, vals[5]), *vals[:5])
        grads = vjp(tuple(r[...] for r in refs[6:10]))
        for o, v in zip(refs[10:], grads):
            o[...] = v

    return pl.pallas_call(body, out_shape=[jax.ShapeDtypeStruct(a.shape, f32) for a in args[:5]],
                          name=name)(*args, *cts)


def _gelu_tanh(x):
    return 0.5 * x * (1.0 + jnp.tanh(0.7978845608028654 * (x + 0.044715 * (x * x * x))))


def _s5_post(y, u, d_skip):
    return _gelu_tanh(y + d_skip * u)


def _s5_glu(ga, gb, h):
    return h + ga * jax.nn.sigmoid(gb)


def _s5_expand():
    e = np.zeros((S5_STATE, S5_GROUP * S5_STATE), np.float32)
    for m in range(S5_GROUP):
        e[np.arange(S5_STATE), m * S5_STATE + np.arange(S5_STATE)] = 1.0
    return jnp.asarray(e)


def _s5_pack_b(bbr, bbi):
    eye = jnp.eye(8, dtype=f32)

    def one(bb):
        b5 = bb.reshape(_S5_NB, 8, S5_GROUP, S5_STATE)
        return jnp.einsum("jgmp,gh->jgmhp", b5, eye).reshape(_S5_NB * _S5_UC, _S5_HALF)

    return jnp.concatenate([one(bbr), one(bbi)], axis=1)


def _s5_unpack_b(db3):
    def one(d):
        d5 = d.reshape(_S5_NB, 8, S5_GROUP, 8, S5_STATE)
        return jnp.einsum("jgmgp->jgmp", d5).reshape(S5_GROUPS, S5_GROUP * S5_STATE)

    return one(db3[:, :_S5_HALF]), one(db3[:, _S5_HALF:])


def _s5_pack_c(c_re, c_im):
    eye = jnp.eye(8, dtype=f32)

    def one(c):
        c4 = c.reshape(_S5_NB, 8, S5_GROUP, S5_STATE)
        return jnp.einsum("jgmp,hg->jhpgm", c4, eye).reshape(_S5_NB, _S5_HALF, _S5_UC)

    return jnp.concatenate([one(c_re), -one(c_im)], axis=1).reshape(_S5_NB * _S5_BW, _S5_UC)


def _s5_unpack_c(dc3):
    d = dc3.reshape(_S5_NB, 2, 8, S5_STATE, 8, S5_GROUP)
    dre = jnp.einsum("jgpgm->jgmp", d[:, 0]).reshape(S5_GROUPS, S5_GROUP, S5_STATE)
    dim = -jnp.einsum("jgpgm->jgmp", d[:, 1]).reshape(S5_GROUPS, S5_GROUP, S5_STATE)
    return dre, dim


def _s5_state_row(re, im):
    r = re.reshape(_S5_NB, 1, _S5_HALF)
    i = im.reshape(_S5_NB, 1, _S5_HALF)
    return jnp.concatenate([r, i], axis=2).reshape(1, _S5_NB * _S5_BW)


def _s5_unstate_row(row):
    r = row.reshape(_S5_NB, 2, 8, S5_STATE)
    return r[:, 0].reshape(S5_GROUPS, S5_STATE), r[:, 1].reshape(S5_GROUPS, S5_STATE)


def _s5_fwd(h, w, tag):
    d = h.shape[1]
    hn, = _rows(_rms, [h], [w["norm_mix"]], [(d, f32)], name=f"s5_norm_{tag}")
    disc_in = [w["s5_lam_re"], w["s5_lam_im"], w["s5_log_dt"], w["s5_bt_re"], w["s5_bt_im"], w["s5_expand"]]
    abr, abi, bbr, bbi = _s5_disc_fwd(disc_in, name=f"s5_disc_{tag}")
    a_row = _s5_state_row(abr, abi)
    b3 = _s5_pack_b(bbr, bbi)
    bu = _bd_mm(hn, b3, name=f"s5_bu_{tag}")
    hs = _s5_scan_fwd(a_row, bu, name=f"s5_scan_{tag}")
    y = _bd_mm(hs, w["s5_c3"], name=f"s5_y_{tag}")
    yg, = _rows(_s5_post, [y, hn], [w["s5_d"]], [(d, bf16)], name=f"s5_post_{tag}")
    ga = _mm(yg, w["s5_w_glu_a"], name=f"s5_glu_a_{tag}")
    gb = _mm(yg, w["s5_w_glu_b"], name=f"s5_glu_b_{tag}")
    out, = _rows(_s5_glu, [ga, gb, h], [], [(d, f32)], name=f"s5_glu_{tag}")
    return out, (h, hn, disc_in, a_row, b3, hs, y, yg, ga, gb)


def _s5_bwd(dout, w, saved, tag):
    h, hn, disc_in, a_row, b3, hs, y, yg, ga, gb = saved
    g = {}
    (dga, dgb), _ = _rows_bwd(_s5_glu, [ga, gb, h], [], [dout], rgrad=[bf16, bf16, None], pgrad=[],
                              name=f"s5_dglu_{tag}")
    dyg = _mm(dga, w["s5_w_glu_a"], tb=True, name=f"s5_dyg_a_{tag}")
    dyg = _mm(dgb, w["s5_w_glu_b"], tb=True, add=dyg, name=f"s5_dyg_b_{tag}")
    g["s5_w_glu_a"] = _mm(yg, dga, ta=True, name=f"s5_dwa_{tag}")
    g["s5_w_glu_b"] = _mm(yg, dgb, ta=True, name=f"s5_dwb_{tag}")
    (dy, du_skip), (g["s5_d"],) = _rows_bwd(_s5_post, [y, hn], [w["s5_d"]], [dyg], rgrad=[bf16, f32], pgrad=[True],
                                           name=f"s5_dpost_{tag}")
    dhs = _bd_mm(dy, w["s5_c3"], tb=True, name=f"s5_dhs_{tag}")
    dc3 = _bd_mm_tn(hs, dy, name=f"s5_dc_{tag}")
    gs, da_row = _s5_scan_bwd(a_row, hs, dhs, name=f"s5_dscan_{tag}")
    du = _bd_mm(gs, b3, tb=True, name=f"s5_du_{tag}")
    db3 = _bd_mm_tn(hn, gs, name=f"s5_db_{tag}")
    dabr, dabi = _s5_unstate_row(da_row)
    dbbr, dbbi = _s5_unpack_b(db3)
    g["s5_lam_re"], g["s5_lam_im"], g["s5_log_dt"], g["s5_bt_re"], g["s5_bt_im"] = _s5_disc_bwd(
        disc_in, [dabr, dabi, dbbr, dbbi], name=f"s5_ddisc_{tag}")
    g["s5_c_re"], g["s5_c_im"] = _s5_unpack_c(dc3)
    (dh,), (g["norm_mix"],) = _rows_bwd(_rms_twice, [h], [w["norm_mix"]], [du, du_skip], rgrad=[f32], pgrad=[True],
                                        addends={0: dout}, name=f"s5_dnorm_{tag}")
    return dh, g


def _odd_weights(p, j, layer, dt):
    tr = lambda b: b.transpose(0, 2, 1).reshape(S5_GROUPS, S5_GROUP * S5_STATE)
    return dict(
        norm_mix=p["norm_mix"][layer][None], s5_lam_re=p["s5_lam_re"][j], s5_lam_im=p["s5_lam_im"][j],
        s5_log_dt=p["s5_log_dt"][j][:, None], s5_bt_re=tr(p["s5_b_re"][j]), s5_bt_im=tr(p["s5_b_im"][j]),
        s5_expand=_s5_expand(), s5_c3=_s5_pack_c(p["s5_c_re"][j], p["s5_c_im"][j]).astype(dt),
        s5_d=p["s5_d"][j][None], s5_w_glu_a=p["s5_w_glu_a"][j].astype(dt), s5_w_glu_b=p["s5_w_glu_b"][j].astype(dt))


def _odd_grads(g):
    tr = lambda b: b.reshape(S5_GROUPS, S5_GROUP, S5_STATE).transpose(0, 2, 1)[None]
    return dict(
        norm_mix=g["norm_mix"], s5_lam_re=g["s5_lam_re"][None], s5_lam_im=g["s5_lam_im"][None],
        s5_log_dt=g["s5_log_dt"][:, 0][None], s5_b_re=tr(g["s5_bt_re"]), s5_b_im=tr(g["s5_bt_im"]),
        s5_c_re=g["s5_c_re"][None], s5_c_im=g["s5_c_im"][None], s5_d=g["s5_d"],
        s5_w_glu_a=g["s5_w_glu_a"][None], s5_w_glu_b=g["s5_w_glu_b"][None])


def _loss_fn(y, t):
    e = y - t
    part = jnp.sum(jnp.sum(e * e, axis=-1, keepdims=True), axis=0, keepdims=True) * (0.5 / y.shape[1])
    return e * (1.0 / y.shape[1]), part


FF_SHARD = 352
FF_SHARD_PAD = 384


def _pad_groups(a, axis):
    axis %= a.ndim
    s = a.shape
    a = a.reshape(s[:axis] + (s[axis] // FF_SHARD, FF_SHARD) + s[axis + 1:])
    pad = [(0, 0)] * a.ndim
    pad[axis + 1] = (0, FF_SHARD_PAD - FF_SHARD)
    return jnp.pad(a, pad).reshape(s[:axis] + (s[axis] // FF_SHARD * FF_SHARD_PAD,) + s[axis + 1:])


def _unpad_groups(a, axis):
    axis %= a.ndim
    s = a.shape
    a = a.reshape(s[:axis] + (s[axis] // FF_SHARD_PAD, FF_SHARD_PAD) + s[axis + 1:])
    a = lax.slice_in_dim(a, 0, FF_SHARD, axis=axis + 1)
    return a.reshape(s[:axis] + (s[axis] // FF_SHARD_PAD * FF_SHARD,) + s[axis + 1:])


def _layer_weights(p, layer, dt):
    return dict(
        norm_xa=p["norm_xa"][layer][None], norm_mem=p["norm_mem"][layer][None], norm_ffn=p["norm_ffn"][layer][None],
        xa_wq=p["xa_wq"][layer].astype(dt), xa_wk=p["xa_wk"][layer].astype(dt), xa_wv=p["xa_wv"][layer].astype(dt),
        xa_wo=p["xa_wo"][layer].astype(dt), xa_q_norm=p["xa_q_norm"][layer][None],
        xa_k_norm=p["xa_k_norm"][layer][None], ffn_w_up=_pad_groups(p["ffn_w_up"][layer], 1).astype(dt),
        ffn_conv_w=_pad_groups(p["ffn_conv_w"][layer], 1), ffn_conv_b=_pad_groups(p["ffn_conv_b"][layer][None], 1),
        ffn_w_down=_pad_groups(p["ffn_w_down"][layer], 0).astype(dt))


_PER_LAYER = ("norm_xa", "norm_mem", "norm_ffn", "xa_wq", "xa_wk", "xa_wv", "xa_wo", "xa_q_norm", "xa_k_norm",
              "ffn_w_up", "ffn_conv_w", "ffn_conv_b", "ffn_w_down")
_FFN_PADDED = dict(ffn_w_up=1, ffn_conv_w=1, ffn_conv_b=1, ffn_w_down=0)


def _local_step(x, mem, positions, target, p):
    cos, sin = _rope_tables(positions)
    we = _even_weights(p, 0, 0, bf16)
    wo = _odd_weights(p, 0, 1, bf16)
    wl = [_layer_weights(p, layer, bf16) for layer in range(2)]
    loss, dh, g_even, g_odd, gl = _local_core(x, mem, cos, sin, target, we, wo, wl)
    grads = {}
    for n in _PER_LAYER:
        a, b = gl[0][n], gl[1][n]
        if n in _FFN_PADDED:
            a, b = _unpad_groups(a, _FFN_PADDED[n]), _unpad_groups(b, _FFN_PADDED[n])
        grads[n] = jnp.concatenate([a, b], axis=0) if a.shape[0] == 1 else jnp.stack([a, b])
    ge, go = _even_grads(g_even), _odd_grads(g_odd)
    grads["norm_mix"] = jnp.concatenate([ge.pop("norm_mix"), go.pop("norm_mix")], axis=0)
    grads.update(ge)
    grads.update(go)
    return loss, dh, grads


def _local_core(x, mem, cos, sin, target, we, wo, wl):
    h, s_mix0 = _mixer_fwd(x, cos, sin, we, "l0")
    h, s_xa0 = _xattn_fwd(h, mem, wl[0], "l0")
    h, s_ff0 = _ffn_fwd(h, wl[0], "l0")
    h, s_mix1 = _s5_fwd(h, wo, "l1")
    h, s_xa1 = _xattn_fwd(h, mem, wl[1], "l1")
    h, s_ff1 = _ffn_fwd(h, wl[1], "l1")
    dh, loss = _rows(_loss_fn, [h, target], [], [(h.shape[1], f32)], accs=[(1, 1)], name="loss_head")

    gl = [{}, {}]
    dh, g = _ffn_bwd(dh, wl[1], s_ff1, "l1")
    gl[1].update(g)
    dh, g = _xattn_bwd(dh, mem, wl[1], s_xa1, "l1")
    gl[1].update(g)
    dh, g_odd = _s5_bwd(dh, wo, s_mix1, "l1")
    dh, g = _ffn_bwd(dh, wl[0], s_ff0, "l0")
    gl[0].update(g)
    dh, g = _xattn_bwd(dh, mem, wl[0], s_xa0, "l0")
    gl[0].update(g)
    dh, g_even = _mixer_bwd(dh, cos, sin, we, s_mix0, "l0")
    return loss, dh, g_even, g_odd, gl


_LANES = 1024
_ROW_PAD = 256


_PEER_MASKS = (1, 2, 4, 3, 5, 6, 7)


def _mesh_place():
    x, y, c = lax.axis_index("x"), lax.axis_index("y"), lax.axis_index("c")

    def peer(mask):
        px = 1 - x if mask & 4 else x
        py = 1 - y if mask & 2 else y
        pc = 1 - c if mask & 1 else c
        return (px, py, pc), 4 * px + 2 * py + pc

    return 4 * x + 2 * y + c, peer


class _Exchange:
    def __init__(self, name):
        self.name = name
        self.srcs, self.shapes, self.items, self.where = [], [], [], {}

    def add(self, src, land_shape, src_at, dst_at, key):
        si = next((i for i, s in enumerate(self.srcs) if s is src), None)
        if si is None:
            self.srcs.append(src)
            si = len(self.srcs) - 1
        if key not in self.where:
            self.shapes.append(land_shape)
            self.where[key] = len(self.shapes) - 1
        self.items.append(dict(src=si, dst=self.where[key], src_at=src_at, dst_at=dst_at))

    def _copy(self, k, mask, ins, lands, send_sems, recv_sems, me, peer, arriving):
        it = self.items[k]
        dev, idx = peer(mask)
        s = k * (N_DEV - 1) + _PEER_MASKS.index(mask)
        return pltpu.make_async_remote_copy(
            src_ref=it["src_at"](ins[it["src"]], idx), dst_ref=it["dst_at"](lands[it["dst"]], idx if arriving else me),
            send_sem=send_sems.at[s], recv_sem=recv_sems.at[s], device_id=dev, device_id_type=pl.DeviceIdType.MESH)

    def _own_copy(self, k, ins, lands, own_sems, me):
        it = self.items[k]
        return pltpu.make_async_copy(it["src_at"](ins[it["src"]], me), it["dst_at"](lands[it["dst"]], me), own_sems.at[k])

    def begin(self, own):
        ns, nd, ni = len(self.srcs), len(self.shapes), len(self.items)
        nsem = ni * (N_DEV - 1)
        self.own = own

        nq = 3 if own else 2

        def body(*refs):
            ins, land_refs = refs[:ns], refs[ns:ns + nd]
            sems, token = refs[ns + nd:ns + nd + nq], refs[-1]
            me, peer = _mesh_place()
            for mask in _PEER_MASKS:
                for k in range(ni):
                    self._copy(k, mask, ins, land_refs, sems[0], sems[1], me, peer, False).start()
            if own:
                for k in range(ni):
                    self._own_copy(k, ins, land_refs, sems[2], me).start()
            token[...] = jnp.zeros_like(token)

        hbm = pl.BlockSpec(memory_space=pltpu.HBM)
        sem = pl.BlockSpec(memory_space=pltpu.SEMAPHORE)
        lands = [lax.empty(s.shape, s.dtype) for s in self.shapes]
        sem_shapes = [pltpu.SemaphoreType.DMA((nsem,)), pltpu.SemaphoreType.DMA((nsem,)), pltpu.SemaphoreType.DMA((ni,))]
        res = pl.pallas_call(
            body, in_specs=[hbm] * (ns + nd),
            out_specs=[sem] * nq + [hbm] * nd + [pl.BlockSpec(memory_space=pltpu.VMEM)],
            out_shape=sem_shapes[:nq] + [pltpu.HBM(s.shape, s.dtype) for s in self.shapes]
            + [jax.ShapeDtypeStruct((8, 128), f32)],
            input_output_aliases={ns + j: nq + j for j in range(nd)},
            compiler_params=pltpu.CompilerParams(has_side_effects=pltpu.SideEffectType.DATAFLOW_SIDE_EFFECTING),
            name=self.name + "_start")(*self.srcs, *lands)
        self.token = res[-1]
        return list(res[:nq]), list(res[nq:-1])

    def finish(self, state, after):
        sems, lands = state
        nq = len(sems)
        after = list(after) if isinstance(after, (list, tuple)) else [after]
        ns, nd, ni = len(self.srcs), len(self.shapes), len(self.items)

        def body(*refs):
            ins, land_refs = refs[:ns], refs[ns:ns + nd]
            sem_refs = refs[ns + nd:ns + nd + nq]
            me, peer = _mesh_place()
            for mask in _PEER_MASKS:
                for k in range(ni):
                    cp = self._copy(k, mask, ins, land_refs, sem_refs[0], sem_refs[1], me, peer, True)
                    cp.wait_send()
                    cp.wait_recv()
            if self.own:
                for k in range(ni):
                    self._own_copy(k, ins, land_refs, sem_refs[2], me).wait()

        hbm = pl.BlockSpec(memory_space=pltpu.HBM)
        sem = pl.BlockSpec(memory_space=pltpu.SEMAPHORE)
        res = pl.pallas_call(
            body, in_specs=[hbm] * (ns + nd) + [sem] * nq + [pl.BlockSpec(memory_space=pl.ANY)] * len(after),
            out_specs=[hbm] * nd, out_shape=[pltpu.HBM(s.shape, s.dtype) for s in self.shapes],
            input_output_aliases={ns + j: j for j in range(nd)},
            compiler_params=pltpu.CompilerParams(has_side_effects=pltpu.SideEffectType.DATAFLOW_SIDE_EFFECTING),
            name=self.name + "_wait")(*self.srcs, *lands, *sems, *after)
        return {k: res[i] for k, i in self.where.items()}


def _after(x, *tokens, name):
    def body(*refs):
        del refs

    anyspace = pl.BlockSpec(memory_space=pl.ANY)
    return pl.pallas_call(body, in_specs=[anyspace] * (1 + len(tokens)), out_specs=anyspace,
                          out_shape=jax.ShapeDtypeStruct(x.shape, x.dtype), input_output_aliases={0: 0},
                          name=name)(x, *tokens)


def _rows_of(n):
    return lambda r, i: r.at[pl.ds(pl.multiple_of(i * n, n), n), :]


def _cols_of(n):
    return lambda r, i: r.at[:, pl.ds(pl.multiple_of(i * n, n), n)]


def _whole(r, i):
    return r


def _slot(r, i):
    return r.at[i]


def _at_layer(layer):
    return lambda r, i: r.at[layer]


def _slot_layer(layer):
    return lambda r, i: r.at[i, layer]


def _sum_adam(me_index, slots, own, own_block, w, m, v, *, name):
    rows, cols = w.shape
    tr = _pick(rows, (256, 128, 64, 32, 16, 8))
    bc1 = 1.0 - ADAM_B1 ** ADAM_STEP
    bc2 = 1.0 - ADAM_B2 ** ADAM_STEP

    own_shape, own_map = own_block(tr)

    def body(me_ref, s_ref, own_ref, w_ref, m_ref, v_ref, g_ref, d_ref, nm_ref, nv_ref):
        mine = own_ref[0] if len(own_shape) == 3 else own_ref[...]
        me = me_ref[0]
        g = jnp.where(me == 0, mine, s_ref[0])
        for k in range(1, N_DEV):
            g = g + jnp.where(me == k, mine, s_ref[k])
        mm = ADAM_B1 * m_ref[...] + (1.0 - ADAM_B1) * g
        vv = ADAM_B2 * v_ref[...] + (1.0 - ADAM_B2) * (g * g)
        g_ref[...] = g
        nm_ref[...] = mm
        nv_ref[...] = vv
        d_ref[...] = -ADAM_LR * ((mm / bc1) / (jnp.sqrt(vv / bc2) + ADAM_EPS) + ADAM_WD * w_ref[...])

    blk = pl.BlockSpec((tr, cols), lambda i, me: (i, 0))
    sds = jax.ShapeDtypeStruct((rows, cols), f32)
    grid_spec = pltpu.PrefetchScalarGridSpec(
        num_scalar_prefetch=1, grid=(rows // tr,),
        in_specs=[pl.BlockSpec((N_DEV, tr, cols), lambda i, me: (0, i, 0)), pl.BlockSpec(own_shape, own_map),
                  blk, blk, blk],
        out_specs=[blk] * 4)
    return pl.pallas_call(body, grid_spec=grid_spec, out_shape=[sds] * 4, compiler_params=_cparams(("parallel",)),
                          name=name)(me_index, slots, own, w, m, v)


_SHARDED = dict(xa_wq=1, xa_wk=1, xa_wv=1, xa_wo=1, ffn_w_up=2, ffn_conv_w=2, ffn_w_down=1, mix_w_in=2, mla_w_uq=2,
                mla_w_ukv=2, mix_w_out=1, s5_d=1, s5_w_glu_a=1, s5_w_glu_b=1)
_EXACT = ("ffn_conv_w", "s5_d")
_WEIGHTS = ("norm_mix", "norm_xa", "norm_mem", "norm_ffn", "xa_wq", "xa_wk", "xa_wv", "xa_wo", "xa_q_norm",
            "xa_k_norm", "ffn_w_up", "ffn_conv_w", "ffn_conv_b", "ffn_w_down", "hg_lb_logits", "mix_w_in",
            "hg_out_norm", "mla_q_a_norm", "mla_w_uq", "mla_kv_a_norm", "mla_w_ukv", "mla_qn_nope", "mla_qn_rope",
            "mla_kn_nope", "mla_kn_rope", "mix_w_out", "s5_lam_re", "s5_lam_im", "s5_log_dt", "s5_b_re", "s5_b_im",
            "s5_c_re", "s5_c_im", "s5_d", "s5_w_glu_a", "s5_w_glu_b")
_BIG = tuple(n for n in _WEIGHTS if n in _SHARDED and n not in _EXACT)
_SHARD_ORDER = tuple(n for n in _WEIGHTS if n in _SHARDED)
_REPL_ORDER = tuple(n for n in _WEIGHTS if n not in _SHARDED)
_REPL_EARLY = tuple(n for n in _REPL_ORDER if n.startswith("s5_"))
_REPL_LATE = tuple(n for n in _REPL_ORDER if n not in _REPL_EARLY)


def _pack(parts, dtype, lead=None):
    nl = 0 if lead is None else 1
    flat = [a.astype(dtype).reshape(a.shape[:nl] + (-1,)) for a in parts]
    cat = jnp.concatenate(flat, axis=nl)
    n = cat.shape[nl]
    unit = _LANES * _ROW_PAD
    total = -(-n // unit) * unit
    cat = jnp.pad(cat, [(0, 0)] * nl + [(0, total - n)])
    return cat.reshape(cat.shape[:nl] + (total // _LANES, _LANES))


def _unpack(packed, shapes, lead=None):
    nl = 0 if lead is None else 1
    flat = packed.reshape(packed.shape[:nl] + (-1,))
    out, off = [], 0
    for s in shapes:
        n = int(np.prod(s))
        piece = flat[..., off:off + n] if nl else flat[off:off + n]
        out.append(piece.reshape(packed.shape[:nl] + tuple(s)))
        off += n
    return out


def _to_full(gathered, axis):
    g = jnp.moveaxis(gathered, 0, axis)
    s = g.shape
    return g.reshape(s[:axis] + (s[axis] * s[axis + 1],) + s[axis + 2:])


def _to_shards(full, axis):
    s = full.shape
    g = full.reshape(s[:axis] + (N_DEV, s[axis] // N_DEV) + s[axis + 1:])
    return jnp.moveaxis(g, axis, 0)


_DIRECT_ROWS = ("xa_wq", "xa_wk", "xa_wv", "xa_wo", "mix_w_out", "s5_w_glu_a", "s5_w_glu_b")
_SMALL16 = ("mix_w_in", "mla_w_uq", "mla_w_ukv")
_SMALL_SHARDED = _SMALL16 + _EXACT
_SHARD_ROWS = 128


def _exchange_layout(d):
    out = dict(d)
    out["ffn_w_up"] = _pad_groups(d["ffn_w_up"], 2)
    out["ffn_conv_w"] = _pad_groups(d["ffn_conv_w"], 2)
    out["ffn_w_down"] = _pad_groups(d["ffn_w_down"], 1)
    return out


def _train_step(x, mem, positions, target, w, m, v):
    d_model = x.shape[1]
    we_, me_, ve_ = _exchange_layout(w), _exchange_layout(m), _exchange_layout(v)
    sds = jax.ShapeDtypeStruct

    matrices = _DIRECT_ROWS + ("ffn_w_up", "ffn_w_down")
    layer_mats = ("xa_wq", "xa_wk", "xa_wv", "xa_wo", "ffn_w_up", "ffn_w_down")
    shard16 = {n: we_[n].astype(bf16) for n in matrices}
    part_of = {n: _rows_of(_SHARD_ROWS) for n in _DIRECT_ROWS}
    part_of["ffn_w_up"] = _cols_of(we_["ffn_w_up"].shape[2])
    part_of["ffn_w_down"] = _rows_of(we_["ffn_w_down"].shape[1])
    part_shape = {n: we_[n].shape[1:] for n in matrices}

    def full_shape(n):
        r, c = part_shape[n]
        return (r, N_DEV * c) if n == "ffn_w_up" else (N_DEV * r, c)

    def gather(ex, n, layer):
        ex.add(shard16[n], sds(full_shape(n), bf16), _at_layer(layer), part_of[n], (n, layer))

    def scatter(ex, n, layer, grad):
        ex.add(grad, sds((N_DEV,) + part_shape[n], f32), part_of[n], _slot, (n, layer))

    small16 = _pack([we_[n] for n in _SMALL16], bf16)
    exact = _pack([we_[n] for n in _EXACT], f32)
    ga, gb, gc = _Exchange("gather_a"), _Exchange("gather_b"), _Exchange("gather_c")
    ga.add(small16, sds((N_DEV,) + small16.shape, bf16), _whole, _slot, "small16")
    ga.add(exact, sds((N_DEV,) + exact.shape, f32), _whole, _slot, "exact")
    gather(ga, "mix_w_out", 0)
    for n in layer_mats:
        gather(gb, n, 0)
    gather(gc, "s5_w_glu_a", 0)
    gather(gc, "s5_w_glu_b", 0)
    for n in layer_mats:
        gather(gc, n, 1)
    state_a, state_b, state_c = ga.begin(True), gb.begin(True), gc.begin(True)

    full = ga.finish(state_a, [gb.token, gc.token])
    p = {n: w[n] for n in _REPL_ORDER}
    for n, a in zip(_SMALL16, _unpack(full["small16"], [we_[n].shape for n in _SMALL16], lead=True)):
        p[n] = _to_full(a, _SHARDED[n])
    conv_w, p["s5_d"] = [_to_full(a, _SHARDED[n]) for n, a in
                         zip(_EXACT, _unpack(full["exact"], [we_[n].shape for n in _EXACT], lead=True))]
    p["mix_w_out"] = full[("mix_w_out", 0)][None]
    cos, sin = _rope_tables(positions)
    we = _even_weights(p, 0, 0, bf16)
    we["norm_mix"] = _after(we["norm_mix"], ga.token, gb.token, gc.token, name="after_gather_starts")
    conv_b = _pad_groups(w["ffn_conv_b"], 1)

    def layer_weights(layer):
        return dict(norm_xa=w["norm_xa"][layer][None], norm_mem=w["norm_mem"][layer][None],
                    norm_ffn=w["norm_ffn"][layer][None], xa_q_norm=w["xa_q_norm"][layer][None],
                    xa_k_norm=w["xa_k_norm"][layer][None], ffn_conv_w=conv_w[layer],
                    ffn_conv_b=conv_b[layer][None], **{n: full[(n, layer)] for n in layer_mats})

    h, s_mix0 = _mixer_fwd(x, cos, sin, we, "l0")
    full.update(gb.finish(state_b, h))
    wl = [layer_weights(0)]
    h, s_xa0 = _xattn_fwd(h, mem, wl[0], "l0")
    h, s_ff0 = _ffn_fwd(h, wl[0], "l0")
    full.update(gc.finish(state_c, h))
    wl.append(layer_weights(1))
    p["s5_w_glu_a"], p["s5_w_glu_b"] = full[("s5_w_glu_a", 0)][None], full[("s5_w_glu_b", 0)][None]
    wo = _odd_weights(p, 0, 1, bf16)
    h, s_mix1 = _s5_fwd(h, wo, "l1")
    h, s_xa1 = _xattn_fwd(h, mem, wl[1], "l1")
    h, s_ff1 = _ffn_fwd(h, wl[1], "l1")
    dh, loss = _rows(_loss_fn, [h, target], [], [(h.shape[1], f32)], accs=[(1, 1)], name="loss_head")

    gl = [{}, {}]
    reduces = []

    own_grad = {}

    def reduce_start(name, entries, dh):
        ex = _Exchange(name)
        for n, layer, grad in entries.get("matrices", ()):
            scatter(ex, n, layer, grad)
            own_grad[(n, layer)] = grad
        for key, src, shape, src_at in entries.get("packs", ()):
            ex.add(src, shape, src_at, _slot, key)
        reduces.append((ex, ex.begin(False)))
        return _after(dh, ex.token, name="after_" + name)

    dh, gl[1] = _ffn_bwd(dh, wl[1], s_ff1, "l1")
    dh = reduce_start("reduce_ffn1", dict(matrices=[(n, 1, gl[1][n]) for n in ("ffn_w_up", "ffn_w_down")]), dh)
    dh, g = _xattn_bwd(dh, mem, wl[1], s_xa1, "l1")
    gl[1].update(g)
    dh = reduce_start("reduce_xa1", dict(matrices=[(n, 1, g[n]) for n in ("xa_wq", "xa_wk", "xa_wv", "xa_wo")]), dh)
    dh, g_odd = _s5_bwd(dh, wo, s_mix1, "l1")
    go = _odd_grads(g_odd)
    dh, gl[0] = _ffn_bwd(dh, wl[0], s_ff0, "l0")
    send_early = _pack([go[n].reshape(w[n].shape) for n in _REPL_EARLY], f32)
    dh = reduce_start("reduce_ffn0", dict(
        matrices=[(n, 0, g_odd[n]) for n in ("s5_w_glu_a", "s5_w_glu_b")]
        + [(n, 0, gl[0][n]) for n in ("ffn_w_up", "ffn_w_down")],
        packs=[("repl_early", send_early, sds((N_DEV,) + send_early.shape, f32), _whole)]), dh)
    dh, g = _xattn_bwd(dh, mem, wl[0], s_xa0, "l0")
    gl[0].update(g)
    dh = reduce_start("reduce_xa0", dict(matrices=[(n, 0, g[n]) for n in ("xa_wq", "xa_wk", "xa_wv", "xa_wo")]), dh)
    grad_x, g_even = _mixer_bwd(
        dh, cos, sin, we, s_mix0, "l0",
        on_w_out=lambda grad, dmixin: reduce_start("reduce_w_out", dict(matrices=[("mix_w_out", 0, grad)]), dmixin))

    ge = _even_grads(g_even)
    cat = lambda n: jnp.concatenate([gl[0][n], gl[1][n]], axis=0)
    rg = dict(ge)
    rg["norm_mix"] = jnp.concatenate([ge["norm_mix"], go["norm_mix"]], axis=0)
    for n in ("norm_xa", "norm_mem", "norm_ffn", "xa_q_norm", "xa_k_norm"):
        rg[n] = cat(n)
    rg["ffn_conv_b"] = _unpad_groups(cat("ffn_conv_b"), 1)
    sg = dict(mix_w_in=ge["mix_w_in"], mla_w_uq=ge["mla_w_uq"], mla_w_ukv=ge["mla_w_ukv"], s5_d=go["s5_d"],
              ffn_conv_w=jnp.stack([gl[0]["ffn_conv_w"], gl[1]["ffn_conv_w"]]))
    send_small = _pack([_to_shards(sg[n], _SHARDED[n]) for n in _SMALL_SHARDED], f32, lead=True)
    send_late = _pack([rg[n].reshape(w[n].shape) for n in _REPL_LATE], f32)
    last = _Exchange("reduce_last")
    last.add(send_small, sds(send_small.shape, f32), _slot, _slot, "small")
    last.add(send_late, sds((N_DEV,) + send_late.shape, f32), _whole, _slot, "repl_late")
    state_last = last.begin(False)
    slots = {}
    for ex, state in reduces:
        slots.update(ex.finish(state, [grad_x, last.token]))

    me_index = (4 * lax.axis_index("x") + 2 * lax.axis_index("y") + lax.axis_index("c")).astype(jnp.int32).reshape(1)

    def own_block(n):
        r, c = part_shape[n]
        if n == "ffn_w_up":
            return lambda tr: ((tr, c), lambda i, me: (i, me[0]))
        return lambda tr: ((tr, c), lambda i, me: (me[0] * (r // tr) + i, 0))

    out = [{}, {}, {}, {}]
    unpad = dict(ffn_w_up=2, ffn_conv_w=2, ffn_w_down=1)
    done = []
    for n in matrices:
        per_layer = [_sum_adam(me_index, slots[(n, layer)], own_grad[(n, layer)], own_block(n), we_[n][layer],
                               me_[n][layer], ve_[n][layer], name=f"adam_{n}_{layer}")
                     for layer in range(we_[n].shape[0])]
        done += [res[0] for res in per_layer]
        for k in range(4):
            r = jnp.stack([res[k] for res in per_layer])
            out[k][n] = _unpad_groups(r, unpad[n]) if n in unpad else r
    pk = lambda d, order: _pack([d[n] for n in order], f32)
    whole_rows = lambda tr: ((tr, _LANES), lambda i, me: (i, 0))
    res_early = _sum_adam(me_index, slots["repl_early"], send_early, whole_rows, pk(w, _REPL_EARLY), pk(m, _REPL_EARLY),
                          pk(v, _REPL_EARLY), name="adam_repl_early")
    slots = last.finish(state_last, done + [res_early[0]])
    res_small = _sum_adam(me_index, slots["small"], send_small,
                          lambda tr: ((1, tr, _LANES), lambda i, me: (me[0], i, 0)),
                          pk(we_, _SMALL_SHARDED), pk(me_, _SMALL_SHARDED), pk(ve_, _SMALL_SHARDED), name="adam_small")
    res_late = _sum_adam(me_index, slots["repl_late"], send_late, whole_rows, pk(w, _REPL_LATE), pk(m, _REPL_LATE),
                         pk(v, _REPL_LATE), name="adam_repl_late")
    for k in range(4):
        for n, a in zip(_SMALL_SHARDED, _unpack(res_small[k], [we_[n].shape for n in _SMALL_SHARDED])):
            out[k][n] = _unpad_groups(a, unpad[n]) if n in unpad else a
        out[k].update(zip(_REPL_EARLY, _unpack(res_early[k], [w[n].shape for n in _REPL_EARLY])))
        out[k].update(zip(_REPL_LATE, _unpack(res_late[k], [w[n].shape for n in _REPL_LATE])))
    return loss, grad_x, out


_INPUTS = tuple("""x, mem, positions, norm_mix, norm_xa, norm_mem, norm_ffn, xa_wq, xa_wk, xa_wv, xa_wo, xa_q_norm, xa_k_norm, ffn_w_up, ffn_conv_w, ffn_conv_b, ffn_w_down, hg_lb_logits, mix_w_in, hg_out_norm, mla_q_a_norm, mla_w_uq, mla_kv_a_norm, mla_w_ukv, mla_qn_nope, mla_qn_rope, mla_kn_nope, mla_kn_rope, mix_w_out, s5_lam_re, s5_lam_im, s5_log_dt, s5_b_re, s5_b_im, s5_c_re, s5_c_im, s5_d, s5_w_glu_a, s5_w_glu_b, loss_target, m_norm_mix, m_norm_xa, m_norm_mem, m_norm_ffn, m_xa_wq, m_xa_wk, m_xa_wv, m_xa_wo, m_xa_q_norm, m_xa_k_norm, m_ffn_w_up, m_ffn_conv_w, m_ffn_conv_b, m_ffn_w_down, m_hg_lb_logits, m_mix_w_in, m_hg_out_norm, m_mla_q_a_norm, m_mla_w_uq, m_mla_kv_a_norm, m_mla_w_ukv, m_mla_qn_nope, m_mla_qn_rope, m_mla_kn_nope, m_mla_kn_rope, m_mix_w_out, m_s5_lam_re, m_s5_lam_im, m_s5_log_dt, m_s5_b_re, m_s5_b_im, m_s5_c_re, m_s5_c_im, m_s5_d, m_s5_w_glu_a, m_s5_w_glu_b, v_norm_mix, v_norm_xa, v_norm_mem, v_norm_ffn, v_xa_wq, v_xa_wk, v_xa_wv, v_xa_wo, v_xa_q_norm, v_xa_k_norm, v_ffn_w_up, v_ffn_conv_w, v_ffn_conv_b, v_ffn_w_down, v_hg_lb_logits, v_mix_w_in, v_hg_out_norm, v_mla_q_a_norm, v_mla_w_uq, v_mla_kv_a_norm, v_mla_w_ukv, v_mla_qn_nope, v_mla_qn_rope, v_mla_kn_nope, v_mla_kn_rope, v_mix_w_out, v_s5_lam_re, v_s5_lam_im, v_s5_log_dt, v_s5_b_re, v_s5_b_im, v_s5_c_re, v_s5_c_im, v_s5_d, v_s5_w_glu_a, v_s5_w_glu_b""".replace(" ", "").split(","))


def kernel(x, mem, positions, norm_mix, norm_xa, norm_mem, norm_ffn, xa_wq, xa_wk, xa_wv, xa_wo, xa_q_norm, xa_k_norm, ffn_w_up, ffn_conv_w, ffn_conv_b, ffn_w_down, hg_lb_logits, mix_w_in, hg_out_norm, mla_q_a_norm, mla_w_uq, mla_kv_a_norm, mla_w_ukv, mla_qn_nope, mla_qn_rope, mla_kn_nope, mla_kn_rope, mix_w_out, s5_lam_re, s5_lam_im, s5_log_dt, s5_b_re, s5_b_im, s5_c_re, s5_c_im, s5_d, s5_w_glu_a, s5_w_glu_b, loss_target, m_norm_mix, m_norm_xa, m_norm_mem, m_norm_ffn, m_xa_wq, m_xa_wk, m_xa_wv, m_xa_wo, m_xa_q_norm, m_xa_k_norm, m_ffn_w_up, m_ffn_conv_w, m_ffn_conv_b, m_ffn_w_down, m_hg_lb_logits, m_mix_w_in, m_hg_out_norm, m_mla_q_a_norm, m_mla_w_uq, m_mla_kv_a_norm, m_mla_w_ukv, m_mla_qn_nope, m_mla_qn_rope, m_mla_kn_nope, m_mla_kn_rope, m_mix_w_out, m_s5_lam_re, m_s5_lam_im, m_s5_log_dt, m_s5_b_re, m_s5_b_im, m_s5_c_re, m_s5_c_im, m_s5_d, m_s5_w_glu_a, m_s5_w_glu_b, v_norm_mix, v_norm_xa, v_norm_mem, v_norm_ffn, v_xa_wq, v_xa_wk, v_xa_wv, v_xa_wo, v_xa_q_norm, v_xa_k_norm, v_ffn_w_up, v_ffn_conv_w, v_ffn_conv_b, v_ffn_w_down, v_hg_lb_logits, v_mix_w_in, v_hg_out_norm, v_mla_q_a_norm, v_mla_w_uq, v_mla_kv_a_norm, v_mla_w_ukv, v_mla_qn_nope, v_mla_qn_rope, v_mla_kn_nope, v_mla_kn_rope, v_mix_w_out, v_s5_lam_re, v_s5_lam_im, v_s5_log_dt, v_s5_b_re, v_s5_b_im, v_s5_c_re, v_s5_c_im, v_s5_d, v_s5_w_glu_a, v_s5_w_glu_b):
    vals = dict(zip(_INPUTS, (x, mem, positions, norm_mix, norm_xa, norm_mem, norm_ffn, xa_wq, xa_wk, xa_wv, xa_wo, xa_q_norm, xa_k_norm, ffn_w_up, ffn_conv_w, ffn_conv_b, ffn_w_down, hg_lb_logits, mix_w_in, hg_out_norm, mla_q_a_norm, mla_w_uq, mla_kv_a_norm, mla_w_ukv, mla_qn_nope, mla_qn_rope, mla_kn_nope, mla_kn_rope, mix_w_out, s5_lam_re, s5_lam_im, s5_log_dt, s5_b_re, s5_b_im, s5_c_re, s5_c_im, s5_d, s5_w_glu_a, s5_w_glu_b, loss_target, m_norm_mix, m_norm_xa, m_norm_mem, m_norm_ffn, m_xa_wq, m_xa_wk, m_xa_wv, m_xa_wo, m_xa_q_norm, m_xa_k_norm, m_ffn_w_up, m_ffn_conv_w, m_ffn_conv_b, m_ffn_w_down, m_hg_lb_logits, m_mix_w_in, m_hg_out_norm, m_mla_q_a_norm, m_mla_w_uq, m_mla_kv_a_norm, m_mla_w_ukv, m_mla_qn_nope, m_mla_qn_rope, m_mla_kn_nope, m_mla_kn_rope, m_mix_w_out, m_s5_lam_re, m_s5_lam_im, m_s5_log_dt, m_s5_b_re, m_s5_b_im, m_s5_c_re, m_s5_c_im, m_s5_d, m_s5_w_glu_a, m_s5_w_glu_b, v_norm_mix, v_norm_xa, v_norm_mem, v_norm_ffn, v_xa_wq, v_xa_wk, v_xa_wv, v_xa_wo, v_xa_q_norm, v_xa_k_norm, v_ffn_w_up, v_ffn_conv_w, v_ffn_conv_b, v_ffn_w_down, v_hg_lb_logits, v_mix_w_in, v_hg_out_norm, v_mla_q_a_norm, v_mla_w_uq, v_mla_kv_a_norm, v_mla_w_ukv, v_mla_qn_nope, v_mla_qn_rope, v_mla_kn_nope, v_mla_kn_rope, v_mix_w_out, v_s5_lam_re, v_s5_lam_im, v_s5_log_dt, v_s5_b_re, v_s5_b_im, v_s5_c_re, v_s5_c_im, v_s5_d, v_s5_w_glu_a, v_s5_w_glu_b)))
    w = {n: vals[n] for n in _WEIGHTS}
    m = {n: vals["m_" + n] for n in _WEIGHTS}
    v = {n: vals["v_" + n] for n in _WEIGHTS}
    loss, grad_x, res = _train_step(vals["x"][0], vals["mem"][0], vals["positions"][0], vals["loss_target"][0],
                                    w, m, v)
    loss = lax.psum(loss[0, 0], ("x", "y", "c"))
    return (loss, grad_x[None], *[r[n] for r in res for n in _WEIGHTS])
```

```python
import functools

import jax
import jax.numpy as jnp
import numpy as np
from jax import lax
from jax.experimental import pallas as pl
from jax.experimental.pallas import tpu as pltpu

f32 = jnp.float32
bf16 = jnp.bfloat16

EPS = 1e-6
N_DEV = 8
VMEM_LIMIT = 52 * 1024 * 1024

HG_HEADS = 4
HG_DIM = 128
HG_WIDTH = HG_HEADS * HG_DIM
HG_CHUNK = 64
HG_SUB = 16
MLA_HEADS = 4
MLA_Q_RANK = 256
MLA_KV_RANK = 128
MLA_NOPE = 128
MLA_ROPE = 64
MLA_V = 128
MLA_QK = MLA_NOPE + MLA_ROPE
MLA_QK_PAD = 256
ROPE_BASE = 10000.0
IN_WIDTH = 4 * HG_WIDTH + MLA_Q_RANK + MLA_KV_RANK + MLA_ROPE
IN_PAD = 2560
XA_HEADS = 4
XA_DIM = 256
S5_GROUP = 16
S5_GROUPS = 64
S5_STATE = 64
CONV_W = 3

ADAM_LR = 0.001
ADAM_B1 = 0.9
ADAM_B2 = 0.999
ADAM_EPS = 1e-08
ADAM_WD = 0.01
ADAM_STEP = 10

_NT = (((1,), (1,)), ((), ()))
_TN = (((0,), (0,)), ((), ()))
_NN = (((1,), (0,)), ((), ()))


def _pick(n, cands):
    for c in cands:
        if n % c == 0:
            return c
    return n


def _cparams(sem):
    return pltpu.CompilerParams(dimension_semantics=sem, vmem_limit_bytes=VMEM_LIMIT)


_MM_BUDGET = 36 * 1024 * 1024


def _mm(a, b, *, name, ta=False, tb=False, out_dtype=f32, add=None, b2=None, kslab=None):
    m, k = (a.shape[1], a.shape[0]) if ta else a.shape
    nb = b.shape[0] if tb else b.shape[1]
    n = nb * (2 if b2 is not None else 1)
    slab, nslab = kslab if kslab is not None else (0, 1)
    assert (b.shape[1] // nslab if tb else b.shape[0]) == k, (a.shape, b.shape, ta, tb)
    assert b2 is None or (not tb and b2.shape == b.shape)
    isz = lambda x: jnp.dtype(x.dtype).itemsize
    bm = bn = None
    for cm, cn in ((512, 512), (512, 256), (256, 512), (256, 256), (256, 128), (128, 256), (128, 128)):
        if m % cm or nb % cn:
            continue
        need = 2 * (cm * k * isz(a) + cn * k * isz(b) * (2 if b2 is not None else 1)
                    + cm * cn * (jnp.dtype(out_dtype).itemsize + (4 if add is not None else 0)))
        if need <= _MM_BUDGET:
            bm, bn = cm, cn
            break
    assert bm is not None, (name, a.shape, b.shape)
    half = nb // bn
    dims = (((0 if ta else 1,), (1 if tb else 0,)), ((), ()))

    def body(*refs):
        refs = list(refs)
        a_ref, b_ref = refs[0], refs[1]
        b2_ref = refs.pop(2) if b2 is not None else None
        add_ref = refs[2] if add is not None else None
        o_ref = refs[-1]

        def run(rhs_ref):
            r = lax.dot_general(a_ref[...].astype(bf16), rhs_ref[...].astype(bf16), dims, preferred_element_type=f32)
            if add_ref is not None:
                r = r + add_ref[...].astype(f32)
            o_ref[...] = r.astype(o_ref.dtype)

        if b2_ref is None:
            run(b_ref)
        else:
            pl.when(pl.program_id(1) < half)(lambda: run(b_ref))
            pl.when(pl.program_id(1) >= half)(lambda: run(b2_ref))

    a_spec = pl.BlockSpec((k, bm), lambda i, j: (0, i)) if ta else pl.BlockSpec((bm, k), lambda i, j: (i, 0))
    if tb:
        b_spec = pl.BlockSpec((bn, k), lambda i, j: (j, slab))
    elif b2 is None:
        b_spec = pl.BlockSpec((k, bn), lambda i, j: (0, j))
    else:
        b_spec = pl.BlockSpec((k, bn), lambda i, j: (0, jnp.minimum(j, half - 1)))
    in_specs = [a_spec, b_spec]
    args = [a, b]
    if b2 is not None:
        in_specs.append(pl.BlockSpec((k, bn), lambda i, j: (0, jnp.maximum(j - half, 0))))
        args.append(b2)
    if add is not None:
        in_specs.append(pl.BlockSpec((bm, bn), lambda i, j: (i, j)))
        args.append(add)
    return pl.pallas_call(
        body, grid=(m // bm, n // bn), in_specs=in_specs,
        out_specs=pl.BlockSpec((bm, bn), lambda i, j: (i, j)),
        out_shape=jax.ShapeDtypeStruct((m, n), out_dtype),
        compiler_params=_cparams(("parallel", "parallel")), name=name)(*args)


def _as_tuple(x):
    return tuple(x) if isinstance(x, (tuple, list)) else (x,)


def _full_spec(p):
    nd = p.ndim
    return pl.BlockSpec(p.shape, lambda i, _nd=nd: (0,) * _nd)


def _rows(fn, rows, params, outs, *, name, tile=256, accs=()):
    length = rows[0].shape[0]
    tile = min(tile, length)
    nr, npar, no = len(rows), len(params), len(outs)

    def body(*refs):
        r, p, o = refs[:nr], refs[nr:nr + npar], refs[nr + npar:]
        res = _as_tuple(fn(*[x[...].astype(f32) for x in r], *[x[...] for x in p]))
        for kk in range(no):
            o[kk][...] = res[kk].astype(o[kk].dtype)
        if accs:
            @pl.when(pl.program_id(0) == 0)
            def _():
                for kk in range(no, no + len(accs)):
                    o[kk][...] = jnp.zeros_like(o[kk])
            for kk in range(no, no + len(accs)):
                o[kk][...] += res[kk]

    in_specs = [pl.BlockSpec((tile, x.shape[1]), lambda i: (i, 0)) for x in rows] + [_full_spec(p) for p in params]
    out_specs = [pl.BlockSpec((tile, w), lambda i: (i, 0)) for w, _ in outs]
    out_shape = [jax.ShapeDtypeStruct((length, w), d) for w, d in outs]
    for s in accs:
        out_specs.append(pl.BlockSpec(s, lambda i, _nd=len(s): (0,) * _nd))
        out_shape.append(jax.ShapeDtypeStruct(s, f32))
    res = pl.pallas_call(body, grid=(length // tile,), in_specs=in_specs, out_specs=out_specs, out_shape=out_shape,
                         compiler_params=_cparams(("arbitrary",)), name=name)(*rows, *params)
    return res


def _rows_bwd(fn, rows, params, cts, *, name, rgrad, pgrad, tile=256, addends=None):
    addends = addends or {}
    length = rows[0].shape[0]
    tile = min(tile, length)
    nr, npar, nc = len(rows), len(params), len(cts)
    ridx = [i for i in range(nr) if rgrad[i] is not None]
    pidx = [i for i in range(npar) if pgrad[i]]
    aidx = sorted(addends)
    na = len(aidx)

    def body(*refs):
        r, p, c = refs[:nr], refs[nr:nr + npar], refs[nr + npar:nr + npar + nc]
        ad = refs[nr + npar + nc:nr + npar + nc + na]
        o = refs[nr + npar + nc + na:]
        rv = [x[...].astype(f32) for x in r]
        pv = [x[...] for x in p]
        cv = tuple(x[...].astype(f32) for x in c)

        def g(*d):
            rr, pp = list(rv), list(pv)
            for n_, i_ in enumerate(ridx):
                rr[i_] = d[n_]
            for n_, i_ in enumerate(pidx):
                pp[i_] = d[len(ridx) + n_]
            return _as_tuple(fn(*rr, *pp))

        _, vjp = jax.vjp(g, *[rv[i] for i in ridx], *[pv[i] for i in pidx])
        grads = vjp(cv)
        for n_, i_ in enumerate(ridx):
            val = grads[n_]
            if i_ in addends:
                val = val + ad[aidx.index(i_)][...].astype(f32)
            o[n_][...] = val.astype(o[n_].dtype)
        if pidx:
            @pl.when(pl.program_id(0) == 0)
            def _():
                for n_ in range(len(pidx)):
                    o[len(ridx) + n_][...] = jnp.zeros_like(o[len(ridx) + n_])
            for n_ in range(len(pidx)):
                o[len(ridx) + n_][...] += grads[len(ridx) + n_]

    row_spec = lambda x: pl.BlockSpec((tile, x.shape[1]), lambda i: (i, 0))
    in_specs = ([row_spec(x) for x in rows] + [_full_spec(p) for p in params] + [row_spec(x) for x in cts]
                + [row_spec(addends[i]) for i in aidx])
    out_specs = [row_spec(rows[i]) for i in ridx] + [_full_spec(params[i]) for i in pidx]
    out_shape = ([jax.ShapeDtypeStruct(rows[i].shape, rgrad[i]) for i in ridx]
                 + [jax.ShapeDtypeStruct(params[i].shape, f32) for i in pidx])
    res = pl.pallas_call(body, grid=(length // tile,), in_specs=in_specs, out_specs=out_specs, out_shape=out_shape,
                         compiler_params=_cparams(("arbitrary",)), name=name)(
        *rows, *params, *cts, *[addends[i] for i in aidx])
    return list(res[:len(ridx)]), list(res[len(ridx):])


def _rms(x, g):
    return x * lax.rsqrt(jnp.mean(x * x, axis=-1, keepdims=True) + EPS) * g


def _rms_twice(x, g):
    y = _rms(x, g)
    return y, y


def _silu(x):
    return x * jax.nn.sigmoid(x)


def _shift_down(x, s):
    rows = lax.broadcasted_iota(jnp.int32, x.shape, 0)
    return jnp.where(rows >= s, pltpu.roll(x, s, axis=0), 0.0)


def _shift_up(x, s):
    n = x.shape[0]
    rows = lax.broadcasted_iota(jnp.int32, x.shape, 0)
    return jnp.where(rows < n - s, pltpu.roll(x, n - s, axis=0), 0.0)


_CONV_COLS = 128


def _conv_gate_fwd(u, cw, cb, *, name):
    length, two_f = u.shape
    ff = two_f // 2
    nb = ff // _CONV_COLS

    def body(ug, uv, wg, wv, bg, bv, o):
        def conv(x_ref, w_ref, b_ref):
            x = x_ref[...].astype(f32)
            return (w_ref[2:3, :] * x + w_ref[1:2, :] * _shift_down(x, 1) + w_ref[0:1, :] * _shift_down(x, 2)
                    + b_ref[...])
        o[...] = (_silu(conv(ug, wg, bg)) * conv(uv, wv, bv)).astype(o.dtype)

    blk = lambda r, off: pl.BlockSpec((r, _CONV_COLS), lambda j, _o=off: (0, j + _o))
    return pl.pallas_call(
        body, grid=(nb,),
        in_specs=[blk(length, 0), blk(length, nb), blk(CONV_W, 0), blk(CONV_W, nb), blk(1, 0), blk(1, nb)],
        out_specs=blk(length, 0), out_shape=jax.ShapeDtypeStruct((length, ff), bf16),
        compiler_params=_cparams(("parallel",)), name=name)(u, u, cw, cw, cb, cb)


def _conv_gate_bwd(u, cw, cb, da, *, name):
    length, two_f = u.shape
    ff = two_f // 2
    nb = ff // _CONV_COLS

    def body(ug, uv, wg, wv, bg, bv, da_ref, dug, duv, dwg, dwv, dbg, dbv):
        def conv(x, w_ref, b_ref):
            x1, x2 = _shift_down(x, 1), _shift_down(x, 2)
            return w_ref[2:3, :] * x + w_ref[1:2, :] * x1 + w_ref[0:1, :] * x2 + b_ref[...], x1, x2

        xg, xv = ug[...].astype(f32), uv[...].astype(f32)
        g, xg1, xg2 = conv(xg, wg, bg)
        v, xv1, xv2 = conv(xv, wv, bv)
        d = da_ref[...].astype(f32)
        sg = jax.nn.sigmoid(g)
        dg = d * v * (sg * (1.0 + g * (1.0 - sg)))
        dv = d * (g * sg)

        def back(dy, x, x1, x2, w_ref, du_ref, dw_ref, db_ref):
            du_ref[...] = (w_ref[2:3, :] * dy + w_ref[1:2, :] * _shift_up(dy, 1)
                           + w_ref[0:1, :] * _shift_up(dy, 2)).astype(du_ref.dtype)
            dw_ref[2:3, :] = jnp.sum(dy * x, axis=0, keepdims=True)
            dw_ref[1:2, :] = jnp.sum(dy * x1, axis=0, keepdims=True)
            dw_ref[0:1, :] = jnp.sum(dy * x2, axis=0, keepdims=True)
            db_ref[...] = jnp.sum(dy, axis=0, keepdims=True)

        back(dg, xg, xg1, xg2, wg, dug, dwg, dbg)
        back(dv, xv, xv1, xv2, wv, duv, dwv, dbv)

    blk = lambda r, off: pl.BlockSpec((r, _CONV_COLS), lambda j, _o=off: (0, j + _o))
    sds = jax.ShapeDtypeStruct
    dug, duv, dwg, dwv, dbg, dbv = pl.pallas_call(
        body, grid=(nb,),
        in_specs=[blk(length, 0), blk(length, nb), blk(CONV_W, 0), blk(CONV_W, nb), blk(1, 0), blk(1, nb),
                  blk(length, 0)],
        out_specs=[blk(length, 0), blk(length, 0), blk(CONV_W, 0), blk(CONV_W, 0), blk(1, 0), blk(1, 0)],
        out_shape=[sds((length, ff), bf16), sds((length, ff), bf16), sds((CONV_W, ff), f32), sds((CONV_W, ff), f32),
                   sds((1, ff), f32), sds((1, ff), f32)],
        compiler_params=_cparams(("parallel",)), name=name)(u, u, cw, cw, cb, cb, da)
    return dug, duv, jnp.concatenate([dwg, dwv], axis=1), jnp.concatenate([dbg, dbv], axis=1)


def _ffn_fwd(h, w, tag):
    hf, = _rows(_rms, [h], [w["norm_ffn"]], [(h.shape[1], bf16)], name=f"ffn_norm_{tag}")
    u = _mm(hf, w["ffn_w_up"], out_dtype=bf16, name=f"ffn_up_{tag}")
    a = _conv_gate_fwd(u, w["ffn_conv_w"], w["ffn_conv_b"], name=f"ffn_conv_{tag}")
    out = _mm(a, w["ffn_w_down"], add=h, name=f"ffn_down_{tag}")
    return out, (h, hf, u, a)


def _ffn_bwd(dout, w, saved, tag):
    h, hf, u, a = saved
    ff = a.shape[1]
    da = _mm(dout, w["ffn_w_down"], tb=True, out_dtype=bf16, name=f"ffn_da_{tag}")
    g = {"ffn_w_down": _mm(a, dout, ta=True, name=f"ffn_dwdown_{tag}")}
    dug, duv, g["ffn_conv_w"], g["ffn_conv_b"] = _conv_gate_bwd(u, w["ffn_conv_w"], w["ffn_conv_b"], da,
                                                                name=f"ffn_dconv_{tag}")
    dhf = _mm(dug, w["ffn_w_up"], tb=True, kslab=(0, 2), name=f"ffn_dhf_g_{tag}")
    dhf = _mm(duv, w["ffn_w_up"], tb=True, kslab=(1, 2), add=dhf, out_dtype=bf16, name=f"ffn_dhf_v_{tag}")
    g["ffn_w_up"] = _mm(hf, dug, ta=True, b2=duv, name=f"ffn_dwup_{tag}")
    (dh,), (g["norm_ffn"],) = _rows_bwd(_rms, [h], [w["norm_ffn"]], [dhf], rgrad=[f32], pgrad=[True],
                                        addends={0: dout}, name=f"ffn_dnorm_{tag}")
    return dh, g


def _xattn_fn(qx, kx, vx, qg, kg):
    outs = []
    for hh in range(XA_HEADS):
        sl = slice(hh * XA_DIM, (hh + 1) * XA_DIM)
        q = _rms(qx[:, sl], qg).astype(bf16)
        k = _rms(kx[:, sl], kg).astype(bf16)
        s = lax.dot_general(q, k, _NT, preferred_element_type=f32) * (XA_DIM ** -0.5)
        s = s - jnp.max(s, axis=-1, keepdims=True)
        p = jnp.exp(s)
        p = p / jnp.sum(p, axis=-1, keepdims=True)
        outs.append(jnp.dot(p.astype(bf16), vx[:, sl].astype(bf16), preferred_element_type=f32))
    return jnp.concatenate(outs, axis=-1)


def _xattn_fwd(h, mem, w, tag):
    d = h.shape[1]
    hx, = _rows(_rms, [h], [w["norm_xa"]], [(d, bf16)], name=f"xa_norm_{tag}")
    qx = _mm(hx, w["xa_wq"], name=f"xa_q_{tag}")
    m, = _rows(_rms, [mem], [w["norm_mem"]], [(d, bf16)], name=f"xa_mnorm_{tag}")
    kx = _mm(m, w["xa_wk"], name=f"xa_k_{tag}")
    vx = _mm(m, w["xa_wv"], name=f"xa_v_{tag}")
    o, = _rows(_xattn_fn, [qx], [kx, vx, w["xa_q_norm"], w["xa_k_norm"]], [(d, bf16)], name=f"xa_attn_{tag}")
    out = _mm(o, w["xa_wo"], add=h, name=f"xa_o_{tag}")
    return out, (h, hx, qx, m, kx, vx, o)


def _xattn_bwd(dout, mem, w, saved, tag):
    h, hx, qx, m, kx, vx, o = saved
    g = {}
    do = _mm(dout, w["xa_wo"], tb=True, out_dtype=bf16, name=f"xa_do_{tag}")
    g["xa_wo"] = _mm(o, dout, ta=True, name=f"xa_dwo_{tag}")
    (dqx,), (dkx, dvx, g["xa_q_norm"], g["xa_k_norm"]) = _rows_bwd(
        _xattn_fn, [qx], [kx, vx, w["xa_q_norm"], w["xa_k_norm"]], [do], rgrad=[bf16], pgrad=[True] * 4,
        name=f"xa_dattn_{tag}")
    dhx = _mm(dqx, w["xa_wq"], tb=True, out_dtype=bf16, name=f"xa_dhx_{tag}")
    g["xa_wq"] = _mm(hx, dqx, ta=True, name=f"xa_dwq_{tag}")
    (dh,), (g["norm_xa"],) = _rows_bwd(_rms, [h], [w["norm_xa"]], [dhx], rgrad=[f32], pgrad=[True],
                                       addends={0: dout}, name=f"xa_dnorm_{tag}")
    dm = _mm(dkx, w["xa_wk"], tb=True, name=f"xa_dm_k_{tag}")
    dm = _mm(dvx, w["xa_wv"], tb=True, add=dm, name=f"xa_dm_v_{tag}")
    g["xa_wk"] = _mm(m, dkx, ta=True, name=f"xa_dwk_{tag}")
    g["xa_wv"] = _mm(m, dvx, ta=True, name=f"xa_dwv_{tag}")
    _, (g["norm_mem"],) = _rows_bwd(_rms, [mem], [w["norm_mem"]], [dm], rgrad=[None], pgrad=[True],
                                    name=f"xa_dmnorm_{tag}")
    return dh, g


_HG_GROUP = 4


def _hg_chunk(q, k, v, g, *sts):
    c = q.shape[0]
    heads = [slice(h * HG_DIM, (h + 1) * HG_DIM) for h in range(len(sts))]
    tri = (lax.broadcasted_iota(jnp.int32, (c, c), 0) >= lax.broadcasted_iota(jnp.int32, (c, c), 1)).astype(f32)
    b = jnp.dot(tri, g, precision=lax.Precision.HIGHEST, preferred_element_type=f32)
    bend = jnp.sum(g, axis=0, keepdims=True)
    qe = (q * jnp.exp(b)).astype(bf16)
    kd = (k * jnp.exp(bend - b)).astype(bf16)
    vb = v.astype(bf16)
    decay = jnp.exp(bend)
    o_inter = [lax.dot_general(qe[:, hs], st.astype(bf16), _NT, preferred_element_type=f32) for hs, st in zip(heads, sts)]
    new = [st * decay[:, hs] + lax.dot_general(vb[:, hs], kd[:, hs], _TN, preferred_element_type=f32)
           for hs, st in zip(heads, sts)]
    outs = []
    for i in range(c // HG_SUB):
        lo, n = HG_SUB * i, HG_SUB * (i + 1)
        ref = jnp.sum(g[:lo], axis=0, keepdims=True) if i else jnp.zeros((1, g.shape[1]), f32)
        qh = (q[lo:n] * jnp.exp(b[lo:n] - ref)).astype(bf16)
        kh = (k[:n] * jnp.exp(ref - b[:n])).astype(bf16)
        keep = (lax.broadcasted_iota(jnp.int32, (HG_SUB, n), 1)
                <= lo + lax.broadcasted_iota(jnp.int32, (HG_SUB, n), 0))
        scores = [lax.dot_general(qh[:, hs], kh[:, hs], _NT, preferred_element_type=f32) for hs in heads]
        scores = [jnp.where(keep, a, 0.0).astype(bf16) for a in scores]
        outs.append(jnp.concatenate([jnp.dot(a, vb[:n, hs], preferred_element_type=f32)
                                     for a, hs in zip(scores, heads)], axis=1))
    return (jnp.concatenate(outs, axis=0) + jnp.concatenate(o_inter, axis=1), *new)


def _hg_fwd(q, k, v, g, *, name):
    length = q.shape[0]
    rows = _HG_GROUP * HG_CHUNK
    ng = length // rows
    nc = length // HG_CHUNK

    def body(q_ref, k_ref, v_ref, g_ref, o_ref, st_ref, state):
        @pl.when(pl.program_id(0) == 0)
        def _():
            state[...] = jnp.zeros_like(state)

        states = [state[h] for h in range(HG_HEADS)]
        for ci in range(_HG_GROUP):
            sl = slice(ci * HG_CHUNK, (ci + 1) * HG_CHUNK)
            for h in range(HG_HEADS):
                st_ref[h, ci] = states[h]
            o, *states = _hg_chunk(q_ref[sl, :], k_ref[sl, :], v_ref[sl, :], g_ref[sl, :], *states)
            o_ref[sl, :] = o
        for h in range(HG_HEADS):
            state[h] = states[h]

    blk = pl.BlockSpec((rows, HG_WIDTH), lambda c: (c, 0))
    return pl.pallas_call(
        body, grid=(ng,), in_specs=[blk] * 4,
        out_specs=[blk, pl.BlockSpec((HG_HEADS, _HG_GROUP, HG_DIM, HG_DIM), lambda c: (0, c, 0, 0))],
        out_shape=[jax.ShapeDtypeStruct((length, HG_WIDTH), f32),
                   jax.ShapeDtypeStruct((HG_HEADS, nc, HG_DIM, HG_DIM), f32)],
        scratch_shapes=[pltpu.VMEM((HG_HEADS, HG_DIM, HG_DIM), f32)],
        compiler_params=_cparams(("arbitrary",)), name=name)(q, k, v, g)


def _hg_bwd(q, k, v, g, states, do, *, name):
    length = q.shape[0]
    rows = _HG_GROUP * HG_CHUNK
    ng = length // rows

    def body(q_ref, k_ref, v_ref, g_ref, st_ref, do_ref, dq_ref, dk_ref, dv_ref, dg_ref, dstate):
        @pl.when(pl.program_id(0) == 0)
        def _():
            dstate[...] = jnp.zeros_like(dstate)

        dstates = [dstate[h] for h in range(HG_HEADS)]
        for ci in reversed(range(_HG_GROUP)):
            sl = slice(ci * HG_CHUNK, (ci + 1) * HG_CHUNK)
            _, vjp = jax.vjp(_hg_chunk, q_ref[sl, :], k_ref[sl, :], v_ref[sl, :], g_ref[sl, :],
                             *[st_ref[h, ci] for h in range(HG_HEADS)])
            dq, dk, dv, dg, *dstates = vjp((do_ref[sl, :], *dstates))
            dq_ref[sl, :] = dq
            dk_ref[sl, :] = dk
            dv_ref[sl, :] = dv
            dg_ref[sl, :] = dg
        for h in range(HG_HEADS):
            dstate[h] = dstates[h]

    blk = pl.BlockSpec((rows, HG_WIDTH), lambda c: (ng - 1 - c, 0))
    sds = jax.ShapeDtypeStruct((length, HG_WIDTH), f32)
    return pl.pallas_call(
        body, grid=(ng,),
        in_specs=[blk] * 4 + [pl.BlockSpec((HG_HEADS, _HG_GROUP, HG_DIM, HG_DIM), lambda c: (0, ng - 1 - c, 0, 0)), blk],
        out_specs=[blk] * 4, out_shape=[sds] * 4,
        scratch_shapes=[pltpu.VMEM((HG_HEADS, HG_DIM, HG_DIM), f32)],
        compiler_params=_cparams(("arbitrary",)), name=name)(q, k, v, g, states, do)


_ATT_BLK = 256
_ATT_SCALE = MLA_QK ** -0.5
_NEG = -1e30


def _att_mask(i, j, t):
    rows = i * t + lax.broadcasted_iota(jnp.int32, (t, t), 0)
    cols = j * t + lax.broadcasted_iota(jnp.int32, (t, t), 1)
    return cols <= rows


def _att_fwd(q, k, v, *, name):
    length = q.shape[0]
    t = min(_ATT_BLK, length)
    nq = length // t
    qw, vw = MLA_QK_PAD, MLA_V
    heads = range(MLA_HEADS)

    def body(q_ref, k_ref, v_ref, o_ref, lse_ref):
        i = pl.program_id(0)
        qbs = [q_ref[:, h * qw:(h + 1) * qw] for h in heads]

        def step(j, carry):
            off = pl.multiple_of(j * t, t)
            keep = _att_mask(i, j, t)
            out = []
            for h in heads:
                m, l, acc = carry[h]
                ks = k_ref[pl.ds(off, t), h * qw:(h + 1) * qw]
                vs = v_ref[pl.ds(off, t), h * vw:(h + 1) * vw]
                s = lax.dot_general(qbs[h], ks, _NT, preferred_element_type=f32) * _ATT_SCALE
                s = jnp.where(keep, s, _NEG)
                m_new = jnp.maximum(m, jnp.max(s, axis=-1, keepdims=True))
                alpha = jnp.exp(m - m_new)
                p = jnp.exp(s - m_new)
                l = alpha * l + jnp.sum(p, axis=-1, keepdims=True)
                acc = alpha * acc + jnp.dot(p.astype(bf16), vs, preferred_element_type=f32)
                out.append((m_new, l, acc))
            return tuple(out)

        init = tuple((jnp.full((t, 1), _NEG, f32), jnp.zeros((t, 1), f32), jnp.zeros((t, vw), f32)) for _ in heads)
        res = lax.fori_loop(0, i + 1, step, init)
        for h in heads:
            m, l, acc = res[h]
            o_ref[:, h * vw:(h + 1) * vw] = (acc / l).astype(o_ref.dtype)
            lse_ref[:, h * vw:(h + 1) * vw] = jnp.broadcast_to(m + jnp.log(l), (t, vw))

    return pl.pallas_call(
        body, grid=(nq,),
        in_specs=[pl.BlockSpec((t, q.shape[1]), lambda i: (i, 0)), pl.BlockSpec(k.shape, lambda i: (0, 0)),
                  pl.BlockSpec(v.shape, lambda i: (0, 0))],
        out_specs=[pl.BlockSpec((t, v.shape[1]), lambda i: (i, 0))] * 2,
        out_shape=[jax.ShapeDtypeStruct(v.shape, bf16), jax.ShapeDtypeStruct(v.shape, f32)],
        compiler_params=_cparams(("arbitrary",)), name=name)(q, k, v)


def _att_bwd(q, k, v, o, lse, do, *, name):
    length = q.shape[0]
    t = min(_ATT_BLK, length)
    nq = length // t
    qw, vw = MLA_QK_PAD, MLA_V
    heads = range(MLA_HEADS)

    def dq_body(q_ref, k_ref, v_ref, o_ref, lse_ref, do_ref, dq_ref, delta_ref):
        i = pl.program_id(0)
        qbs = [q_ref[:, h * qw:(h + 1) * qw] for h in heads]
        dobs = [do_ref[:, h * vw:(h + 1) * vw] for h in heads]
        lses = [lse_ref[:, h * vw:h * vw + 1] for h in heads]
        deltas = [jnp.sum(dobs[h].astype(f32) * o_ref[:, h * vw:(h + 1) * vw].astype(f32), axis=-1, keepdims=True)
                  for h in heads]

        def step(j, dqs):
            off = pl.multiple_of(j * t, t)
            keep = _att_mask(i, j, t)
            out = []
            for h in heads:
                ks = k_ref[pl.ds(off, t), h * qw:(h + 1) * qw]
                vs = v_ref[pl.ds(off, t), h * vw:(h + 1) * vw]
                s = lax.dot_general(qbs[h], ks, _NT, preferred_element_type=f32) * _ATT_SCALE
                p = jnp.where(keep, jnp.exp(s - lses[h]), 0.0)
                dp = lax.dot_general(dobs[h], vs, _NT, preferred_element_type=f32)
                ds = p * (dp - deltas[h]) * _ATT_SCALE
                out.append(dqs[h] + jnp.dot(ds.astype(bf16), ks, preferred_element_type=f32))
            return tuple(out)

        dqs = lax.fori_loop(0, i + 1, step, tuple(jnp.zeros((t, qw), f32) for _ in heads))
        for h in heads:
            dq_ref[:, h * qw:(h + 1) * qw] = dqs[h].astype(dq_ref.dtype)
            delta_ref[:, h * vw:(h + 1) * vw] = jnp.broadcast_to(deltas[h], (t, vw))

    qblk = pl.BlockSpec((t, q.shape[1]), lambda i: (i, 0))
    vblk = pl.BlockSpec((t, v.shape[1]), lambda i: (i, 0))
    qfull = pl.BlockSpec(q.shape, lambda i: (0, 0))
    vfull = pl.BlockSpec(v.shape, lambda i: (0, 0))
    dq, delta = pl.pallas_call(
        dq_body, grid=(nq,), in_specs=[qblk, qfull, vfull, vblk, vblk, vblk], out_specs=[qblk, vblk],
        out_shape=[jax.ShapeDtypeStruct(q.shape, bf16), jax.ShapeDtypeStruct(lse.shape, f32)],
        compiler_params=_cparams(("arbitrary",)), name=name + "_dq")(q, k, v, o, lse, do)

    def dkv_body(k_ref, v_ref, q_ref, do_ref, lse_ref, delta_ref, dk_ref, dv_ref):
        j = pl.program_id(0)
        kbs = [k_ref[:, h * qw:(h + 1) * qw] for h in heads]
        vbs = [v_ref[:, h * vw:(h + 1) * vw] for h in heads]

        def step(i, carry):
            off = pl.multiple_of(i * t, t)
            keep = _att_mask(i, j, t)
            out = []
            for h in heads:
                dk, dv = carry[h]
                qs = q_ref[pl.ds(off, t), h * qw:(h + 1) * qw]
                dos = do_ref[pl.ds(off, t), h * vw:(h + 1) * vw]
                lse_i = lse_ref[pl.ds(off, t), h * vw:h * vw + 1]
                delta_i = delta_ref[pl.ds(off, t), h * vw:h * vw + 1]
                s = lax.dot_general(qs, kbs[h], _NT, preferred_element_type=f32) * _ATT_SCALE
                p = jnp.where(keep, jnp.exp(s - lse_i), 0.0)
                dv = dv + lax.dot_general(p.astype(bf16), dos, _TN, preferred_element_type=f32)
                dp = lax.dot_general(dos, vbs[h], _NT, preferred_element_type=f32)
                ds = p * (dp - delta_i) * _ATT_SCALE
                dk = dk + lax.dot_general(ds.astype(bf16), qs, _TN, preferred_element_type=f32)
                out.append((dk, dv))
            return tuple(out)

        res = lax.fori_loop(j, nq, step, tuple((jnp.zeros((t, qw), f32), jnp.zeros((t, vw), f32)) for _ in heads))
        for h in heads:
            dk_ref[:, h * qw:(h + 1) * qw] = res[h][0].astype(dk_ref.dtype)
            dv_ref[:, h * vw:(h + 1) * vw] = res[h][1].astype(dv_ref.dtype)

    dk, dv = pl.pallas_call(
        dkv_body, grid=(nq,), in_specs=[qblk, vblk, qfull, vfull, vfull, vfull], out_specs=[qblk, vblk],
        out_shape=[jax.ShapeDtypeStruct(k.shape, bf16), jax.ShapeDtypeStruct(v.shape, bf16)],
        compiler_params=_cparams(("arbitrary",)), name=name + "_dkv")(k, v, q, do, lse, delta)
    return dq, dk, dv


_C_Q = 4 * HG_WIDTH
_C_KV = _C_Q + MLA_Q_RANK
_C_KPE = _C_KV + MLA_KV_RANK


def _rms_n(x, g, n):
    return x * lax.rsqrt(jnp.sum(x * x, axis=-1, keepdims=True) * (1.0 / n) + EPS) * g


def _mix_a(proj, l0, l1, q_a_norm, kv_a_norm):
    lb = jax.nn.sigmoid(l0 - l1)
    f = lb + (1.0 - lb) * jax.nn.sigmoid(proj[:, HG_WIDTH:2 * HG_WIDTH])
    qf = _silu(proj[:, :HG_WIDTH])
    v = proj[:, 2 * HG_WIDTH:3 * HG_WIDTH]
    cqn = _rms(proj[:, _C_Q:_C_KV], q_a_norm)
    ckvn = _rms(proj[:, _C_KV:_C_KPE], kv_a_norm)
    return qf, 1.0 - f, v, jnp.log(f), cqn, ckvn


def _mix_b(qraw, kvraw, proj, cos, sin, qn_nope, qn_rope, kn_nope, kn_rope, perm):
    def rope(x):
        return x * cos + jnp.dot(x, perm, precision=lax.Precision.HIGHEST, preferred_element_type=f32) * sin

    kpe = rope(_rms_n(proj[:, _C_KPE:], kn_rope, MLA_ROPE))
    qs, ks, vs = [], [], []
    for hh in range(MLA_HEADS):
        base = hh * MLA_QK_PAD
        qs.append(_rms(qraw[:, base:base + MLA_NOPE], qn_nope))
        qs.append(rope(_rms_n(qraw[:, base + MLA_NOPE:base + MLA_QK_PAD], qn_rope, MLA_ROPE)))
        ks.append(_rms(kvraw[:, base:base + MLA_NOPE], kn_nope))
        ks.append(kpe)
        vs.append(kvraw[:, base + MLA_NOPE:base + MLA_QK_PAD])
    return jnp.concatenate(qs, axis=-1), jnp.concatenate(ks, axis=-1), jnp.concatenate(vs, axis=-1)


def _mix_c(o_hg, proj, o_mla, hg_out_norm):
    parts = []
    for hh in range(HG_HEADS):
        sl = slice(hh * HG_DIM, (hh + 1) * HG_DIM)
        parts.append(_rms(o_hg[:, sl], hg_out_norm[:, sl]))
    o = jnp.concatenate(parts, axis=-1) * _silu(proj[:, 3 * HG_WIDTH:4 * HG_WIDTH])
    return jnp.concatenate([o, o_mla], axis=-1)


def _rope_perm():
    p = np.zeros((128, 128), np.float32)
    half = MLA_ROPE // 2
    for i in range(half):
        p[i + half, i] = -1.0
        p[i, i + half] = 1.0
    return jnp.asarray(p)


def _mixer_fwd(h, cos, sin, w, tag):
    d = h.shape[1]
    hn, = _rows(_rms, [h], [w["norm_mix"]], [(d, bf16)], name=f"mix_norm_{tag}")
    proj = _mm(hn, w["mix_w_in"], name=f"mix_in_{tag}")
    pa = [w["lb0"], w["lb1"], w["mla_q_a_norm"], w["mla_kv_a_norm"]]
    qf, kk, vv, logf, cqn, ckvn = _rows(
        _mix_a, [proj], pa, [(HG_WIDTH, f32)] * 4 + [(MLA_Q_RANK, bf16), (MLA_KV_RANK, bf16)], name=f"mix_a_{tag}")
    o_hg, states = _hg_fwd(qf, kk, vv, logf, name=f"hg_fwd_{tag}")
    qraw = _mm(cqn, w["mla_w_uq"], name=f"mla_uq_{tag}")
    kvraw = _mm(ckvn, w["mla_w_ukv"], name=f"mla_ukv_{tag}")
    pb = [w["mla_qn_nope"], w["mla_qn_rope"], w["mla_kn_nope"], w["mla_kn_rope"], w["rope_perm"]]
    qfull, kfull, vfull = _rows(_mix_b, [qraw, kvraw, proj, cos, sin], pb,
                                [(MLA_HEADS * MLA_QK_PAD, bf16)] * 2 + [(MLA_HEADS * MLA_V, bf16)],
                                name=f"mix_b_{tag}")
    o_mla, lse = _att_fwd(qfull, kfull, vfull, name=f"att_fwd_{tag}")
    mixin, = _rows(_mix_c, [o_hg, proj, o_mla], [w["hg_out_norm"]], [(d, bf16)], name=f"mix_c_{tag}")
    out = _mm(mixin, w["mix_w_out"], add=h, name=f"mix_out_{tag}")
    return out, (h, hn, proj, qf, kk, vv, logf, cqn, ckvn, o_hg, states, qraw, kvraw, qfull, kfull, vfull, o_mla,
                 lse, mixin)


def _mixer_bwd(dout, cos, sin, w, saved, tag, on_w_out=None):
    (h, hn, proj, qf, kk, vv, logf, cqn, ckvn, o_hg, states, qraw, kvraw, qfull, kfull, vfull, o_mla, lse,
     mixin) = saved
    g = {}
    dmixin = _mm(dout, w["mix_w_out"], tb=True, name=f"mix_dmixin_{tag}")
    g["mix_w_out"] = _mm(mixin, dout, ta=True, name=f"mix_dwout_{tag}")
    if on_w_out is not None:
        dmixin = on_w_out(g["mix_w_out"], dmixin)
    (do_hg, dproj_c, do_mla), (g["hg_out_norm"],) = _rows_bwd(
        _mix_c, [o_hg, proj, o_mla], [w["hg_out_norm"]], [dmixin], rgrad=[f32, f32, bf16], pgrad=[True],
        name=f"mix_dc_{tag}")
    dqfull, dkfull, dvfull = _att_bwd(qfull, kfull, vfull, o_mla, lse, do_mla, name=f"att_bwd_{tag}")
    pb = [w["mla_qn_nope"], w["mla_qn_rope"], w["mla_kn_nope"], w["mla_kn_rope"], w["rope_perm"]]
    (dqraw, dkvraw, dproj_b), pg = _rows_bwd(
        _mix_b, [qraw, kvraw, proj, cos, sin], pb, [dqfull, dkfull, dvfull],
        rgrad=[bf16, bf16, f32, None, None], pgrad=[True, True, True, True, False], addends={2: dproj_c},
        name=f"mix_db_{tag}")
    g["mla_qn_nope"], g["mla_qn_rope"], g["mla_kn_nope"], g["mla_kn_rope"] = pg
    dcqn = _mm(dqraw, w["mla_w_uq"], tb=True, name=f"mla_dcq_{tag}")
    g["mla_w_uq"] = _mm(cqn, dqraw, ta=True, name=f"mla_dwuq_{tag}")
    dckvn = _mm(dkvraw, w["mla_w_ukv"], tb=True, name=f"mla_dckv_{tag}")
    g["mla_w_ukv"] = _mm(ckvn, dkvraw, ta=True, name=f"mla_dwukv_{tag}")
    dqf, dkk, dvv, dlogf = _hg_bwd(qf, kk, vv, logf, states, do_hg, name=f"hg_bwd_{tag}")
    pa = [w["lb0"], w["lb1"], w["mla_q_a_norm"], w["mla_kv_a_norm"]]
    (dproj,), (g["lb0"], g["lb1"], g["mla_q_a_norm"], g["mla_kv_a_norm"]) = _rows_bwd(
        _mix_a, [proj], pa, [dqf, dkk, dvv, dlogf, dcqn, dckvn], rgrad=[bf16], pgrad=[True] * 4,
        addends={0: dproj_b}, name=f"mix_da_{tag}")
    dhn = _mm(dproj, w["mix_w_in"], tb=True, out_dtype=bf16, name=f"mix_dhn_{tag}")
    g["mix_w_in"] = _mm(hn, dproj, ta=True, name=f"mix_dwin_{tag}")
    (dh,), (g["norm_mix"],) = _rows_bwd(_rms, [h], [w["norm_mix"]], [dhn], rgrad=[f32], pgrad=[True],
                                        addends={0: dout}, name=f"mix_dnorm_{tag}")
    return dh, g


def _rope_tables(positions):
    inv_freq = 1.0 / (ROPE_BASE ** (jnp.arange(0, MLA_ROPE, 2, dtype=f32) / MLA_ROPE))
    ang = positions.astype(f32)[:, None] * inv_freq
    z = jnp.zeros((positions.shape[0], 128 - MLA_ROPE), f32)
    return (jnp.concatenate([jnp.cos(ang), jnp.cos(ang), z], axis=1),
            jnp.concatenate([jnp.sin(ang), jnp.sin(ang), z], axis=1))


def _pad_cols(a, n):
    return jnp.pad(a, ((0, 0), (0, n - a.shape[1])))


def _even_weights(p, j, layer, dt):
    w_uq = p["mla_w_uq"][j].reshape(MLA_Q_RANK, MLA_HEADS, MLA_QK)
    w_uq = jnp.pad(w_uq, ((0, 0), (0, 0), (0, MLA_QK_PAD - MLA_QK))).reshape(MLA_Q_RANK, MLA_HEADS * MLA_QK_PAD)
    return dict(
        norm_mix=p["norm_mix"][layer][None], mix_w_in=_pad_cols(p["mix_w_in"][j], IN_PAD).astype(dt),
        lb0=p["hg_lb_logits"][0][None], lb1=p["hg_lb_logits"][1][None],
        mla_q_a_norm=p["mla_q_a_norm"][j][None], mla_kv_a_norm=p["mla_kv_a_norm"][j][None],
        mla_w_uq=w_uq.astype(dt), mla_w_ukv=p["mla_w_ukv"][j].astype(dt),
        mla_qn_nope=p["mla_qn_nope"][j][None], mla_qn_rope=_pad_cols(p["mla_qn_rope"][j][None], 128),
        mla_kn_nope=p["mla_kn_nope"][j][None], mla_kn_rope=_pad_cols(p["mla_kn_rope"][j][None], 128),
        rope_perm=_rope_perm(), hg_out_norm=p["hg_out_norm"][j][None], mix_w_out=p["mix_w_out"][j].astype(dt))


def _even_grads(g):
    w_uq = g["mla_w_uq"].reshape(MLA_Q_RANK, MLA_HEADS, MLA_QK_PAD)[:, :, :MLA_QK].reshape(MLA_Q_RANK, -1)
    return dict(
        norm_mix=g["norm_mix"], mix_w_in=g["mix_w_in"][:, :IN_WIDTH][None],
        hg_lb_logits=jnp.concatenate([g["lb0"], g["lb1"]], axis=0),
        mla_q_a_norm=g["mla_q_a_norm"], mla_kv_a_norm=g["mla_kv_a_norm"], mla_w_uq=w_uq[None],
        mla_w_ukv=g["mla_w_ukv"][None], mla_qn_nope=g["mla_qn_nope"], mla_qn_rope=g["mla_qn_rope"][:, :MLA_ROPE],
        mla_kn_nope=g["mla_kn_nope"], mla_kn_rope=g["mla_kn_rope"][:, :MLA_ROPE],
        hg_out_norm=g["hg_out_norm"], mix_w_out=g["mix_w_out"][None])


_S5_NB = 8
_S5_BW = 1024
_S5_HALF = 512
_S5_UC = 128
_S5_TIME = 512


def _bd_mm(a, b3, *, name, tb=False, out_dtype=f32):
    length = a.shape[0]
    rows_b, cols_b = b3.shape[0] // _S5_NB, b3.shape[1]
    ka, n = (cols_b, rows_b) if tb else (rows_b, cols_b)
    bm = _pick(length, (512, 256, 128))
    dims = _NT if tb else _NN

    def body(a_ref, b_ref, o_ref):
        o_ref[...] = lax.dot_general(a_ref[...].astype(bf16), b_ref[...].astype(bf16), dims,
                                     preferred_element_type=f32).astype(o_ref.dtype)

    return pl.pallas_call(
        body, grid=(length // bm, _S5_NB),
        in_specs=[pl.BlockSpec((bm, ka), lambda i, j: (i, j)), pl.BlockSpec((rows_b, cols_b), lambda i, j: (j, 0))],
        out_specs=pl.BlockSpec((bm, n), lambda i, j: (i, j)),
        out_shape=jax.ShapeDtypeStruct((length, _S5_NB * n), out_dtype),
        compiler_params=_cparams(("parallel", "parallel")), name=name)(a, b3)


def _bd_mm_tn(a, c, *, name):
    length = a.shape[0]
    ka, n = a.shape[1] // _S5_NB, c.shape[1] // _S5_NB
    bk = _pick(length, (512, 256, 128))
    nk = length // bk

    def body(a_ref, c_ref, o_ref):
        @pl.when(pl.program_id(1) == 0)
        def _():
            o_ref[...] = jnp.zeros_like(o_ref)
        o_ref[...] += lax.dot_general(a_ref[...].astype(bf16), c_ref[...].astype(bf16), _TN,
                                      preferred_element_type=f32)

    return pl.pallas_call(
        body, grid=(_S5_NB, nk),
        in_specs=[pl.BlockSpec((bk, ka), lambda j, q: (q, j)), pl.BlockSpec((bk, n), lambda j, q: (q, j))],
        out_specs=pl.BlockSpec((ka, n), lambda j, q: (j, 0)),
        out_shape=jax.ShapeDtypeStruct((_S5_NB * ka, n), f32),
        compiler_params=_cparams(("parallel", "arbitrary")), name=name)(a, c)


def _cmul(ar, ai, br, bi):
    return ar * br - ai * bi, ar * bi + ai * br


def _pow_table(ar, ai, descending):
    rows = lax.broadcasted_iota(jnp.int32, (8, ar.shape[1]), 0)
    tr = jnp.zeros((8, ar.shape[1]), f32)
    ti = jnp.zeros((8, ar.shape[1]), f32)
    pr, pi_ = ar, ai
    for r in range(8):
        sel = rows == ((7 - r) if descending else r)
        tr = jnp.where(sel, pr, tr)
        ti = jnp.where(sel, pi_, ti)
        pr, pi_ = _cmul(pr, pi_, ar, ai)
    return tr, ti


def _s5_scan_fwd(a, x, *, name):
    length = x.shape[0]
    tc = min(_S5_TIME, length)
    hw = _S5_HALF

    def body(a_ref, x_ref, o_ref, carry):
        @pl.when(pl.program_id(1) == 0)
        def _():
            carry[...] = jnp.zeros_like(carry)

        ar, ai = a_ref[:, :hw], a_ref[:, hw:]
        xr, xi = x_ref[:, :hw], x_ref[:, hw:]
        row8 = lax.broadcasted_iota(jnp.int32, (tc, hw), 0) & 7
        mr, mi = ar, ai
        for s in (1, 2, 4):
            sr, si = pltpu.roll(xr, s, axis=0), pltpu.roll(xi, s, axis=0)
            pr, pi_ = _cmul(mr, mi, sr, si)
            ok = row8 >= s
            xr = xr + jnp.where(ok, pr, 0.0)
            xi = xi + jnp.where(ok, pi_, 0.0)
            mr, mi = _cmul(mr, mi, mr, mi)
        o_ref[:, :hw] = xr
        o_ref[:, hw:] = xi
        tr, ti = _pow_table(ar, ai, False)
        cr, ci = carry[:, :hw], carry[:, hw:]
        for i in range(tc // 8):
            sl = slice(8 * i, 8 * i + 8)
            pr, pi_ = _cmul(tr, ti, cr, ci)
            o_ref[sl, :hw] = o_ref[sl, :hw] + pr
            o_ref[sl, hw:] = o_ref[sl, hw:] + pi_
            cr, ci = o_ref[8 * i + 7:8 * i + 8, :hw], o_ref[8 * i + 7:8 * i + 8, hw:]
        carry[:, :hw] = cr
        carry[:, hw:] = ci

    return pl.pallas_call(
        body, grid=(_S5_NB, length // tc),
        in_specs=[pl.BlockSpec((1, _S5_BW), lambda j, t: (0, j)), pl.BlockSpec((tc, _S5_BW), lambda j, t: (t, j))],
        out_specs=pl.BlockSpec((tc, _S5_BW), lambda j, t: (t, j)),
        out_shape=jax.ShapeDtypeStruct(x.shape, f32),
        scratch_shapes=[pltpu.VMEM((1, _S5_BW), f32)],
        compiler_params=_cparams(("parallel", "arbitrary")), name=name)(a, x)


def _s5_scan_bwd(a, hs, d, *, name):
    length = d.shape[0]
    tc = min(_S5_TIME, length)
    nt = length // tc
    hw = _S5_HALF

    def body(a_ref, h_ref, d_ref, g_ref, da_ref, carry):
        @pl.when(pl.program_id(1) == 0)
        def _():
            carry[...] = jnp.zeros_like(carry)
            da_ref[...] = jnp.zeros_like(da_ref)

        ar, ai = a_ref[:, :hw], -a_ref[:, hw:]
        xr, xi = d_ref[:, :hw], d_ref[:, hw:]
        rows = lax.broadcasted_iota(jnp.int32, (tc, hw), 0)
        row8 = rows & 7
        mr, mi = ar, ai
        for s in (1, 2, 4):
            sr, si = pltpu.roll(xr, tc - s, axis=0), pltpu.roll(xi, tc - s, axis=0)
            pr, pi_ = _cmul(mr, mi, sr, si)
            ok = row8 < 8 - s
            xr = xr + jnp.where(ok, pr, 0.0)
            xi = xi + jnp.where(ok, pi_, 0.0)
            mr, mi = _cmul(mr, mi, mr, mi)
        g_ref[:, :hw] = xr
        g_ref[:, hw:] = xi
        tr, ti = _pow_table(ar, ai, True)
        cr0, ci0 = carry[:, :hw], carry[:, hw:]
        cr, ci = cr0, ci0
        for i in reversed(range(tc // 8)):
            sl = slice(8 * i, 8 * i + 8)
            pr, pi_ = _cmul(tr, ti, cr, ci)
            g_ref[sl, :hw] = g_ref[sl, :hw] + pr
            g_ref[sl, hw:] = g_ref[sl, hw:] + pi_
            cr, ci = g_ref[8 * i:8 * i + 1, :hw], g_ref[8 * i:8 * i + 1, hw:]
        carry[:, :hw] = cr
        carry[:, hw:] = ci
        last = rows == tc - 1
        gnr = jnp.where(last, cr0, pltpu.roll(g_ref[:, :hw], tc - 1, axis=0))
        gni = jnp.where(last, ci0, pltpu.roll(g_ref[:, hw:], tc - 1, axis=0))
        hr, hi = h_ref[:, :hw], h_ref[:, hw:]
        da_ref[:, :hw] += jnp.sum(hr * gnr + hi * gni, axis=0, keepdims=True)
        da_ref[:, hw:] += jnp.sum(hr * gni - hi * gnr, axis=0, keepdims=True)

    blk = pl.BlockSpec((tc, _S5_BW), lambda j, t: (nt - 1 - t, j))
    row = pl.BlockSpec((1, _S5_BW), lambda j, t: (0, j))
    return pl.pallas_call(
        body, grid=(_S5_NB, nt), in_specs=[row, blk, blk], out_specs=[blk, row],
        out_shape=[jax.ShapeDtypeStruct(d.shape, f32), jax.ShapeDtypeStruct((1, _S5_NB * _S5_BW), f32)],
        scratch_shapes=[pltpu.VMEM((1, _S5_BW), f32)],
        compiler_params=_cparams(("parallel", "arbitrary")), name=name)(a, hs, d)


def _s5_tile_scan(work, carry, ar, ai, tc, reverse, per_tile=None):
    hw = _S5_HALF
    row8 = lax.broadcasted_iota(jnp.int32, (8, hw), 0)
    powers = [(ar, ai)]
    for _ in range(2):
        powers.append(_cmul(*powers[-1], *powers[-1]))
    tr, ti = _pow_table(ar, ai, reverse)
    cr, ci = carry[:, :hw], carry[:, hw:]
    tiles = range(tc // 8)
    for i in (reversed(tiles) if reverse else tiles):
        sl = slice(8 * i, 8 * i + 8)
        xr, xi = work[sl, :hw], work[sl, hw:]
        for (mr, mi), s in zip(powers, (1, 2, 4)):
            shift = 8 - s if reverse else s
            pr, pi_ = _cmul(mr, mi, pltpu.roll(xr, shift, axis=0), pltpu.roll(xi, shift, axis=0))
            ok = (row8 < 8 - s) if reverse else (row8 >= s)
            xr = xr + jnp.where(ok, pr, 0.0)
            xi = xi + jnp.where(ok, pi_, 0.0)
        pr, pi_ = _cmul(tr, ti, cr, ci)
        xr, xi = xr + pr, xi + pi_
        work[sl, :hw] = xr
        work[sl, hw:] = xi
        if per_tile is not None:
            per_tile(sl, xr, xi, cr, ci)
        edge = 8 * i if reverse else 8 * i + 7
        cr, ci = work[edge:edge + 1, :hw], work[edge:edge + 1, hw:]
    carry[:, :hw] = cr
    carry[:, hw:] = ci


def _s5_core_fwd(a, hn, b3, c3, *, name):
    length = hn.shape[0]
    tc = min(_S5_TIME, length)

    def body(a_ref, hn_ref, b_ref, c_ref, hs_ref, y_ref, work, carry):
        @pl.when(pl.program_id(1) == 0)
        def _():
            carry[...] = jnp.zeros_like(carry)

        work[...] = jnp.dot(hn_ref[...].astype(bf16), b_ref[...], preferred_element_type=f32)
        _s5_tile_scan(work, carry, a_ref[:, :_S5_HALF], a_ref[:, _S5_HALF:], tc, False)
        hs = work[...].astype(bf16)
        hs_ref[...] = hs
        y_ref[...] = jnp.dot(hs, c_ref[...], preferred_element_type=f32)

    return pl.pallas_call(
        body, grid=(_S5_NB, length // tc),
        in_specs=[pl.BlockSpec((1, _S5_BW), lambda j, t: (0, j)), pl.BlockSpec((tc, _S5_UC), lambda j, t: (t, j)),
                  pl.BlockSpec((_S5_UC, _S5_BW), lambda j, t: (j, 0)), pl.BlockSpec((_S5_BW, _S5_UC), lambda j, t: (j, 0))],
        out_specs=[pl.BlockSpec((tc, _S5_BW), lambda j, t: (t, j)), pl.BlockSpec((tc, _S5_UC), lambda j, t: (t, j))],
        out_shape=[jax.ShapeDtypeStruct((length, _S5_NB * _S5_BW), bf16),
                   jax.ShapeDtypeStruct((length, _S5_NB * _S5_UC), f32)],
        scratch_shapes=[pltpu.VMEM((tc, _S5_BW), f32), pltpu.VMEM((1, _S5_BW), f32)],
        compiler_params=_cparams(("parallel", "arbitrary")), name=name)(a, hn, b3, c3)


def _s5_core_bwd(a, dy, c3, hs, hn, b3, *, name):
    length = hn.shape[0]
    tc = min(_S5_TIME, length)
    nt = length // tc
    hw = _S5_HALF

    def body(a_ref, dy_ref, c_ref, hs_ref, hn_ref, b_ref, du_ref, db_ref, dc_ref, da_ref, work, carry, acc):
        @pl.when(pl.program_id(1) == 0)
        def _():
            carry[...] = jnp.zeros_like(carry)
            db_ref[...] = jnp.zeros_like(db_ref)
            dc_ref[...] = jnp.zeros_like(dc_ref)
            da_ref[...] = jnp.zeros_like(da_ref)

        dyb = dy_ref[...].astype(bf16)
        work[...] = lax.dot_general(dyb, c_ref[...], _NT, preferred_element_type=f32)
        acc[...] = jnp.zeros_like(acc)
        row8 = lax.broadcasted_iota(jnp.int32, (8, hw), 0)

        def grad_a(sl, gr, gi, cr, ci):
            gnr = jnp.where(row8 == 7, cr, pltpu.roll(gr, 7, axis=0))
            gni = jnp.where(row8 == 7, ci, pltpu.roll(gi, 7, axis=0))
            hr, hi = hs_ref[sl, :hw].astype(f32), hs_ref[sl, hw:].astype(f32)
            acc[:, :hw] += hr * gnr + hi * gni
            acc[:, hw:] += hr * gni - hi * gnr

        _s5_tile_scan(work, carry, a_ref[:, :hw], -a_ref[:, hw:], tc, True, grad_a)
        da_ref[...] += jnp.sum(acc[...], axis=0, keepdims=True)
        g = work[...].astype(bf16)
        du_ref[...] = lax.dot_general(g, b_ref[...], _NT, preferred_element_type=f32)
        db_ref[...] += lax.dot_general(hn_ref[...].astype(bf16), g, _TN, preferred_element_type=f32)
        dc_ref[...] += lax.dot_general(hs_ref[...], dyb, _TN, preferred_element_type=f32)

    rev = lambda j, t: (nt - 1 - t, j)
    return pl.pallas_call(
        body, grid=(_S5_NB, nt),
        in_specs=[pl.BlockSpec((1, _S5_BW), lambda j, t: (0, j)), pl.BlockSpec((tc, _S5_UC), rev),
                  pl.BlockSpec((_S5_BW, _S5_UC), lambda j, t: (j, 0)), pl.BlockSpec((tc, _S5_BW), rev),
                  pl.BlockSpec((tc, _S5_UC), rev), pl.BlockSpec((_S5_UC, _S5_BW), lambda j, t: (j, 0))],
        out_specs=[pl.BlockSpec((tc, _S5_UC), rev), pl.BlockSpec((_S5_UC, _S5_BW), lambda j, t: (j, 0)),
                   pl.BlockSpec((_S5_BW, _S5_UC), lambda j, t: (j, 0)), pl.BlockSpec((1, _S5_BW), lambda j, t: (0, j))],
        out_shape=[jax.ShapeDtypeStruct((length, _S5_NB * _S5_UC), f32),
                   jax.ShapeDtypeStruct((_S5_NB * _S5_UC, _S5_BW), f32),
                   jax.ShapeDtypeStruct((_S5_NB * _S5_BW, _S5_UC), f32),
                   jax.ShapeDtypeStruct((1, _S5_NB * _S5_BW), f32)],
        scratch_shapes=[pltpu.VMEM((tc, _S5_BW), f32), pltpu.VMEM((1, _S5_BW), f32), pltpu.VMEM((8, _S5_BW), f32)],
        compiler_params=_cparams(("parallel", "arbitrary")), name=name)(a, dy, c3, hs, hn, b3)


def _s5_disc(lr, li, ldt, btr, bti, expand):
    dt = jnp.exp(ldt)
    mag = jnp.exp(lr * dt)
    abr = mag * jnp.cos(li * dt)
    abi = mag * jnp.sin(li * dt)
    den = lr * lr + li * li
    zr = ((abr - 1.0) * lr + abi * li) / den
    zi = (abi * lr - (abr - 1.0) * li) / den
    zr = jnp.dot(zr, expand, precision=lax.Precision.HIGHEST, preferred_element_type=f32)
    zi = jnp.dot(zi, expand, precision=lax.Precision.HIGHEST, preferred_element_type=f32)
    return abr, abi, zr * btr - zi * bti, zr * bti + zi * btr


def _s5_disc_fwd(args, *, name):
    def body(*refs):
        res = _s5_disc(*[r[...] for r in refs[:6]])
        for o, v in zip(refs[6:], res):
            o[...] = v

    sds = jax.ShapeDtypeStruct
    return pl.pallas_call(body, out_shape=[sds(args[0].shape, f32)] * 2 + [sds(args[3].shape, f32)] * 2,
                          name=name)(*args)


def _s5_disc_bwd(args, cts, *, name):
    def body(*refs):
        vals = [r[...] for r in refs[:6]]
        _, vjp = jax.vjp(lambda *d: _s5_disc(*d, vals[5]), *vals[:5])
        grads = vjp(tuple(r[...] for r in refs[6:10]))
        for o, v in zip(refs[10:], grads):
            o[...] = v

    return pl.pallas_call(body, out_shape=[jax.ShapeDtypeStruct(a.shape, f32) for a in args[:5]],
                          name=name)(*args, *cts)


def _gelu_tanh(x):
    return 0.5 * x * (1.0 + jnp.tanh(0.7978845608028654 * (x + 0.044715 * (x * x * x))))


def _s5_post(y, u, d_skip):
    return _gelu_tanh(y + d_skip * u)


def _s5_glu(ga, gb, h):
    return h + ga * jax.nn.sigmoid(gb)


def _s5_expand():
    e = np.zeros((S5_STATE, S5_GROUP * S5_STATE), np.float32)
    for m in range(S5_GROUP):
        e[np.arange(S5_STATE), m * S5_STATE + np.arange(S5_STATE)] = 1.0
    return jnp.asarray(e)


def _s5_pack_b(bbr, bbi):
    eye = jnp.eye(8, dtype=f32)

    def one(bb):
        b5 = bb.reshape(_S5_NB, 8, S5_GROUP, S5_STATE)
        return jnp.einsum("jgmp,gh->jgmhp", b5, eye).reshape(_S5_NB * _S5_UC, _S5_HALF)

    return jnp.concatenate([one(bbr), one(bbi)], axis=1)


def _s5_unpack_b(db3):
    def one(d):
        d5 = d.reshape(_S5_NB, 8, S5_GROUP, 8, S5_STATE)
        return jnp.einsum("jgmgp->jgmp", d5).reshape(S5_GROUPS, S5_GROUP * S5_STATE)

    return one(db3[:, :_S5_HALF]), one(db3[:, _S5_HALF:])


def _s5_pack_c(c_re, c_im):
    eye = jnp.eye(8, dtype=f32)

    def one(c):
        c4 = c.reshape(_S5_NB, 8, S5_GROUP, S5_STATE)
        return jnp.einsum("jgmp,hg->jhpgm", c4, eye).reshape(_S5_NB, _S5_HALF, _S5_UC)

    return jnp.concatenate([one(c_re), -one(c_im)], axis=1).reshape(_S5_NB * _S5_BW, _S5_UC)


def _s5_unpack_c(dc3):
    d = dc3.reshape(_S5_NB, 2, 8, S5_STATE, 8, S5_GROUP)
    dre = jnp.einsum("jgpgm->jgmp", d[:, 0]).reshape(S5_GROUPS, S5_GROUP, S5_STATE)
    dim = -jnp.einsum("jgpgm->jgmp", d[:, 1]).reshape(S5_GROUPS, S5_GROUP, S5_STATE)
    return dre, dim


def _s5_state_row(re, im):
    r = re.reshape(_S5_NB, 1, _S5_HALF)
    i = im.reshape(_S5_NB, 1, _S5_HALF)
    return jnp.concatenate([r, i], axis=2).reshape(1, _S5_NB * _S5_BW)


def _s5_unstate_row(row):
    r = row.reshape(_S5_NB, 2, 8, S5_STATE)
    return r[:, 0].reshape(S5_GROUPS, S5_STATE), r[:, 1].reshape(S5_GROUPS, S5_STATE)


def _s5_fwd(h, w, tag):
    d = h.shape[1]
    hn, = _rows(_rms, [h], [w["norm_mix"]], [(d, f32)], name=f"s5_norm_{tag}")
    disc_in = [w["s5_lam_re"], w["s5_lam_im"], w["s5_log_dt"], w["s5_bt_re"], w["s5_bt_im"], w["s5_expand"]]
    abr, abi, bbr, bbi = _s5_disc_fwd(disc_in, name=f"s5_disc_{tag}")
    a_row = _s5_state_row(abr, abi)
    b3 = _s5_pack_b(bbr, bbi).astype(bf16)
    hs, y = _s5_core_fwd(a_row, hn, b3, w["s5_c3"], name=f"s5_core_{tag}")
    yg, = _rows(_s5_post, [y, hn], [w["s5_d"]], [(d, bf16)], name=f"s5_post_{tag}")
    ga = _mm(yg, w["s5_w_glu_a"], name=f"s5_glu_a_{tag}")
    gb = _mm(yg, w["s5_w_glu_b"], name=f"s5_glu_b_{tag}")
    out, = _rows(_s5_glu, [ga, gb, h], [], [(d, f32)], name=f"s5_glu_{tag}")
    return out, (h, hn, disc_in, a_row, b3, hs, y, yg, ga, gb)


def _s5_bwd(dout, w, saved, tag):
    h, hn, disc_in, a_row, b3, hs, y, yg, ga, gb = saved
    g = {}
    (dga, dgb), _ = _rows_bwd(_s5_glu, [ga, gb, h], [], [dout], rgrad=[bf16, bf16, None], pgrad=[],
                              name=f"s5_dglu_{tag}")
    dyg = _mm(dga, w["s5_w_glu_a"], tb=True, name=f"s5_dyg_a_{tag}")
    dyg = _mm(dgb, w["s5_w_glu_b"], tb=True, add=dyg, name=f"s5_dyg_b_{tag}")
    g["s5_w_glu_a"] = _mm(yg, dga, ta=True, name=f"s5_dwa_{tag}")
    g["s5_w_glu_b"] = _mm(yg, dgb, ta=True, name=f"s5_dwb_{tag}")
    (dy, du_skip), (g["s5_d"],) = _rows_bwd(_s5_post, [y, hn], [w["s5_d"]], [dyg], rgrad=[bf16, f32], pgrad=[True],
                                           name=f"s5_dpost_{tag}")
    du, db3, dc3, da_row = _s5_core_bwd(a_row, dy, w["s5_c3"], hs, hn, b3, name=f"s5_dcore_{tag}")
    dabr, dabi = _s5_unstate_row(da_row)
    dbbr, dbbi = _s5_unpack_b(db3)
    g["s5_lam_re"], g["s5_lam_im"], g["s5_log_dt"], g["s5_bt_re"], g["s5_bt_im"] = _s5_disc_bwd(
        disc_in, [dabr, dabi, dbbr, dbbi], name=f"s5_ddisc_{tag}")
    g["s5_c_re"], g["s5_c_im"] = _s5_unpack_c(dc3)
    (dh,), (g["norm_mix"],) = _rows_bwd(_rms_twice, [h], [w["norm_mix"]], [du, du_skip], rgrad=[f32], pgrad=[True],
                                        addends={0: dout}, name=f"s5_dnorm_{tag}")
    return dh, g


def _odd_weights(p, j, layer, dt):
    tr = lambda b: b.transpose(0, 2, 1).reshape(S5_GROUPS, S5_GROUP * S5_STATE)
    return dict(
        norm_mix=p["norm_mix"][layer][None], s5_lam_re=p["s5_lam_re"][j], s5_lam_im=p["s5_lam_im"][j],
        s5_log_dt=p["s5_log_dt"][j][:, None], s5_bt_re=tr(p["s5_b_re"][j]), s5_bt_im=tr(p["s5_b_im"][j]),
        s5_expand=_s5_expand(), s5_c3=_s5_pack_c(p["s5_c_re"][j], p["s5_c_im"][j]).astype(dt),
        s5_d=p["s5_d"][j][None], s5_w_glu_a=p["s5_w_glu_a"][j].astype(dt), s5_w_glu_b=p["s5_w_glu_b"][j].astype(dt))


def _odd_grads(g):
    tr = lambda b: b.reshape(S5_GROUPS, S5_GROUP, S5_STATE).transpose(0, 2, 1)[None]
    return dict(
        norm_mix=g["norm_mix"], s5_lam_re=g["s5_lam_re"][None], s5_lam_im=g["s5_lam_im"][None],
        s5_log_dt=g["s5_log_dt"][:, 0][None], s5_b_re=tr(g["s5_bt_re"]), s5_b_im=tr(g["s5_bt_im"]),
        s5_c_re=g["s5_c_re"][None], s5_c_im=g["s5_c_im"][None], s5_d=g["s5_d"],
        s5_w_glu_a=g["s5_w_glu_a"][None], s5_w_glu_b=g["s5_w_glu_b"][None])


def _loss_fn(y, t):
    e = y - t
    part = jnp.sum(jnp.sum(e * e, axis=-1, keepdims=True), axis=0, keepdims=True) * (0.5 / y.shape[1])
    return e * (1.0 / y.shape[1]), part


FF_SHARD = 352
FF_SHARD_PAD = 384


def _pad_groups(a, axis):
    axis %= a.ndim
    s = a.shape
    a = a.reshape(s[:axis] + (s[axis] // FF_SHARD, FF_SHARD) + s[axis + 1:])
    pad = [(0, 0)] * a.ndim
    pad[axis + 1] = (0, FF_SHARD_PAD - FF_SHARD)
    return jnp.pad(a, pad).reshape(s[:axis] + (s[axis] // FF_SHARD * FF_SHARD_PAD,) + s[axis + 1:])


def _unpad_groups(a, axis):
    axis %= a.ndim
    s = a.shape
    a = a.reshape(s[:axis] + (s[axis] // FF_SHARD_PAD, FF_SHARD_PAD) + s[axis + 1:])
    a = lax.slice_in_dim(a, 0, FF_SHARD, axis=axis + 1)
    return a.reshape(s[:axis] + (s[axis] // FF_SHARD_PAD * FF_SHARD,) + s[axis + 1:])


def _layer_weights(p, layer, dt):
    return dict(
        norm_xa=p["norm_xa"][layer][None], norm_mem=p["norm_mem"][layer][None], norm_ffn=p["norm_ffn"][layer][None],
        xa_wq=p["xa_wq"][layer].astype(dt), xa_wk=p["xa_wk"][layer].astype(dt), xa_wv=p["xa_wv"][layer].astype(dt),
        xa_wo=p["xa_wo"][layer].astype(dt), xa_q_norm=p["xa_q_norm"][layer][None],
        xa_k_norm=p["xa_k_norm"][layer][None], ffn_w_up=_pad_groups(p["ffn_w_up"][layer], 1).astype(dt),
        ffn_conv_w=_pad_groups(p["ffn_conv_w"][layer], 1), ffn_conv_b=_pad_groups(p["ffn_conv_b"][layer][None], 1),
        ffn_w_down=_pad_groups(p["ffn_w_down"][layer], 0).astype(dt))


_PER_LAYER = ("norm_xa", "norm_mem", "norm_ffn", "xa_wq", "xa_wk", "xa_wv", "xa_wo", "xa_q_norm", "xa_k_norm",
              "ffn_w_up", "ffn_conv_w", "ffn_conv_b", "ffn_w_down")
_FFN_PADDED = dict(ffn_w_up=1, ffn_conv_w=1, ffn_conv_b=1, ffn_w_down=0)


def _local_step(x, mem, positions, target, p):
    cos, sin = _rope_tables(positions)
    we = _even_weights(p, 0, 0, bf16)
    wo = _odd_weights(p, 0, 1, bf16)
    wl = [_layer_weights(p, layer, bf16) for layer in range(2)]
    loss, dh, g_even, g_odd, gl = _local_core(x, mem, cos, sin, target, we, wo, wl)
    grads = {}
    for n in _PER_LAYER:
        a, b = gl[0][n], gl[1][n]
        if n in _FFN_PADDED:
            a, b = _unpad_groups(a, _FFN_PADDED[n]), _unpad_groups(b, _FFN_PADDED[n])
        grads[n] = jnp.concatenate([a, b], axis=0) if a.shape[0] == 1 else jnp.stack([a, b])
    ge, go = _even_grads(g_even), _odd_grads(g_odd)
    grads["norm_mix"] = jnp.concatenate([ge.pop("norm_mix"), go.pop("norm_mix")], axis=0)
    grads.update(ge)
    grads.update(go)
    return loss, dh, grads


def _local_core(x, mem, cos, sin, target, we, wo, wl):
    h, s_mix0 = _mixer_fwd(x, cos, sin, we, "l0")
    h, s_xa0 = _xattn_fwd(h, mem, wl[0], "l0")
    h, s_ff0 = _ffn_fwd(h, wl[0], "l0")
    h, s_mix1 = _s5_fwd(h, wo, "l1")
    h, s_xa1 = _xattn_fwd(h, mem, wl[1], "l1")
    h, s_ff1 = _ffn_fwd(h, wl[1], "l1")
    dh, loss = _rows(_loss_fn, [h, target], [], [(h.shape[1], f32)], accs=[(1, 1)], name="loss_head")

    gl = [{}, {}]
    dh, g = _ffn_bwd(dh, wl[1], s_ff1, "l1")
    gl[1].update(g)
    dh, g = _xattn_bwd(dh, mem, wl[1], s_xa1, "l1")
    gl[1].update(g)
    dh, g_odd = _s5_bwd(dh, wo, s_mix1, "l1")
    dh, g = _ffn_bwd(dh, wl[0], s_ff0, "l0")
    gl[0].update(g)
    dh, g = _xattn_bwd(dh, mem, wl[0], s_xa0, "l0")
    gl[0].update(g)
    dh, g_even = _mixer_bwd(dh, cos, sin, we, s_mix0, "l0")
    return loss, dh, g_even, g_odd, gl


_LANES = 1024
_ROW_PAD = 256


_PEER_MASKS = (1, 2, 4, 3, 5, 6, 7)


def _mesh_place():
    x, y, c = lax.axis_index("x"), lax.axis_index("y"), lax.axis_index("c")

    def peer(mask):
        px = 1 - x if mask & 4 else x
        py = 1 - y if mask & 2 else y
        pc = 1 - c if mask & 1 else c
        return (px, py, pc), 4 * px + 2 * py + pc

    return 4 * x + 2 * y + c, peer


class _Exchange:
    def __init__(self, name):
        self.name = name
        self.srcs, self.shapes, self.items, self.where = [], [], [], {}

    def add(self, src, land_shape, src_at, dst_at, key):
        si = next((i for i, s in enumerate(self.srcs) if s is src), None)
        if si is None:
            self.srcs.append(src)
            si = len(self.srcs) - 1
        if key not in self.where:
            self.shapes.append(land_shape)
            self.where[key] = len(self.shapes) - 1
        self.items.append(dict(src=si, dst=self.where[key], src_at=src_at, dst_at=dst_at))

    def _copy(self, k, mask, ins, lands, send_sems, recv_sems, me, peer, arriving):
        it = self.items[k]
        dev, idx = peer(mask)
        s = k * (N_DEV - 1) + _PEER_MASKS.index(mask)
        return pltpu.make_async_remote_copy(
            src_ref=it["src_at"](ins[it["src"]], idx), dst_ref=it["dst_at"](lands[it["dst"]], idx if arriving else me),
            send_sem=send_sems.at[s], recv_sem=recv_sems.at[s], device_id=dev, device_id_type=pl.DeviceIdType.MESH)

    def _own_copy(self, k, ins, lands, own_sems, me):
        it = self.items[k]
        return pltpu.make_async_copy(it["src_at"](ins[it["src"]], me), it["dst_at"](lands[it["dst"]], me), own_sems.at[k])

    def begin(self, own):
        ns, nd, ni = len(self.srcs), len(self.shapes), len(self.items)
        nsem = ni * (N_DEV - 1)
        self.own = own

        nq = 3 if own else 2

        def body(*refs):
            ins, land_refs = refs[:ns], refs[ns:ns + nd]
            sems, token = refs[ns + nd:ns + nd + nq], refs[-1]
            me, peer = _mesh_place()
            for mask in _PEER_MASKS:
                for k in range(ni):
                    self._copy(k, mask, ins, land_refs, sems[0], sems[1], me, peer, False).start()
            if own:
                for k in range(ni):
                    self._own_copy(k, ins, land_refs, sems[2], me).start()
            token[...] = jnp.zeros_like(token)

        hbm = pl.BlockSpec(memory_space=pltpu.HBM)
        sem = pl.BlockSpec(memory_space=pltpu.SEMAPHORE)
        lands = [lax.empty(s.shape, s.dtype) for s in self.shapes]
        sem_shapes = [pltpu.SemaphoreType.DMA((nsem,)), pltpu.SemaphoreType.DMA((nsem,)), pltpu.SemaphoreType.DMA((ni,))]
        res = pl.pallas_call(
            body, in_specs=[hbm] * (ns + nd),
            out_specs=[sem] * nq + [hbm] * nd + [pl.BlockSpec(memory_space=pltpu.VMEM)],
            out_shape=sem_shapes[:nq] + [pltpu.HBM(s.shape, s.dtype) for s in self.shapes]
            + [jax.ShapeDtypeStruct((8, 128), f32)],
            input_output_aliases={ns + j: nq + j for j in range(nd)},
            compiler_params=pltpu.CompilerParams(has_side_effects=pltpu.SideEffectType.DATAFLOW_SIDE_EFFECTING),
            name=self.name + "_start")(*self.srcs, *lands)
        self.token = res[-1]
        return list(res[:nq]), list(res[nq:-1])

    def finish(self, state, after):
        sems, lands = state
        nq = len(sems)
        after = list(after) if isinstance(after, (list, tuple)) else [after]
        ns, nd, ni = len(self.srcs), len(self.shapes), len(self.items)

        def body(*refs):
            ins, land_refs = refs[:ns], refs[ns:ns + nd]
            sem_refs = refs[ns + nd:ns + nd + nq]
            me, peer = _mesh_place()
            for mask in _PEER_MASKS:
                for k in range(ni):
                    cp = self._copy(k, mask, ins, land_refs, sem_refs[0], sem_refs[1], me, peer, True)
                    cp.wait_send()
                    cp.wait_recv()
            if self.own:
                for k in range(ni):
                    self._own_copy(k, ins, land_refs, sem_refs[2], me).wait()

        hbm = pl.BlockSpec(memory_space=pltpu.HBM)
        sem = pl.BlockSpec(memory_space=pltpu.SEMAPHORE)
        res = pl.pallas_call(
            body, in_specs=[hbm] * (ns + nd) + [sem] * nq + [pl.BlockSpec(memory_space=pl.ANY)] * len(after),
            out_specs=[hbm] * nd, out_shape=[pltpu.HBM(s.shape, s.dtype) for s in self.shapes],
            input_output_aliases={ns + j: j for j in range(nd)},
            compiler_params=pltpu.CompilerParams(has_side_effects=pltpu.SideEffectType.DATAFLOW_SIDE_EFFECTING),
            name=self.name + "_wait")(*self.srcs, *lands, *sems, *after)
        return {k: res[i] for k, i in self.where.items()}


def _after(x, *tokens, name):
    def body(*refs):
        del refs

    anyspace = pl.BlockSpec(memory_space=pl.ANY)
    return pl.pallas_call(body, in_specs=[anyspace] * (1 + len(tokens)), out_specs=anyspace,
                          out_shape=jax.ShapeDtypeStruct(x.shape, x.dtype), input_output_aliases={0: 0},
                          name=name)(x, *tokens)


def _rows_of(n):
    return lambda r, i: r.at[pl.ds(pl.multiple_of(i * n, n), n), :]


def _cols_of(n):
    return lambda r, i: r.at[:, pl.ds(pl.multiple_of(i * n, n), n)]


def _whole(r, i):
    return r


def _slot(r, i):
    return r.at[i]


def _at_layer(layer):
    return lambda r, i: r.at[layer]


def _slot_layer(layer):
    return lambda r, i: r.at[i, layer]


def _sum_adam(me_index, slots, own, own_block, w, m, v, *, name):
    rows, cols = w.shape
    tr = _pick(rows, (256, 128, 64, 32, 16, 8))
    bc1 = 1.0 - ADAM_B1 ** ADAM_STEP
    bc2 = 1.0 - ADAM_B2 ** ADAM_STEP

    own_shape, own_map = own_block(tr)

    def body(me_ref, s_ref, own_ref, w_ref, m_ref, v_ref, g_ref, d_ref, nm_ref, nv_ref):
        mine = own_ref[0] if len(own_shape) == 3 else own_ref[...]
        me = me_ref[0]
        g = jnp.where(me == 0, mine, s_ref[0])
        for k in range(1, N_DEV):
            g = g + jnp.where(me == k, mine, s_ref[k])
        mm = ADAM_B1 * m_ref[...] + (1.0 - ADAM_B1) * g
        vv = ADAM_B2 * v_ref[...] + (1.0 - ADAM_B2) * (g * g)
        g_ref[...] = g
        nm_ref[...] = mm
        nv_ref[...] = vv
        d_ref[...] = -ADAM_LR * ((mm / bc1) / (jnp.sqrt(vv / bc2) + ADAM_EPS) + ADAM_WD * w_ref[...])

    blk = pl.BlockSpec((tr, cols), lambda i, me: (i, 0))
    sds = jax.ShapeDtypeStruct((rows, cols), f32)
    grid_spec = pltpu.PrefetchScalarGridSpec(
        num_scalar_prefetch=1, grid=(rows // tr,),
        in_specs=[pl.BlockSpec((N_DEV, tr, cols), lambda i, me: (0, i, 0)), pl.BlockSpec(own_shape, own_map),
                  blk, blk, blk],
        out_specs=[blk] * 4)
    return pl.pallas_call(body, grid_spec=grid_spec, out_shape=[sds] * 4, compiler_params=_cparams(("parallel",)),
                          name=name)(me_index, slots, own, w, m, v)


_SHARDED = dict(xa_wq=1, xa_wk=1, xa_wv=1, xa_wo=1, ffn_w_up=2, ffn_conv_w=2, ffn_w_down=1, mix_w_in=2, mla_w_uq=2,
                mla_w_ukv=2, mix_w_out=1, s5_d=1, s5_w_glu_a=1, s5_w_glu_b=1)
_EXACT = ("ffn_conv_w", "s5_d")
_WEIGHTS = ("norm_mix", "norm_xa", "norm_mem", "norm_ffn", "xa_wq", "xa_wk", "xa_wv", "xa_wo", "xa_q_norm",
            "xa_k_norm", "ffn_w_up", "ffn_conv_w", "ffn_conv_b", "ffn_w_down", "hg_lb_logits", "mix_w_in",
            "hg_out_norm", "mla_q_a_norm", "mla_w_uq", "mla_kv_a_norm", "mla_w_ukv", "mla_qn_nope", "mla_qn_rope",
            "mla_kn_nope", "mla_kn_rope", "mix_w_out", "s5_lam_re", "s5_lam_im", "s5_log_dt", "s5_b_re", "s5_b_im",
            "s5_c_re", "s5_c_im", "s5_d", "s5_w_glu_a", "s5_w_glu_b")
_BIG = tuple(n for n in _WEIGHTS if n in _SHARDED and n not in _EXACT)
_SHARD_ORDER = tuple(n for n in _WEIGHTS if n in _SHARDED)
_REPL_ORDER = tuple(n for n in _WEIGHTS if n not in _SHARDED)
_REPL_EARLY = tuple(n for n in _REPL_ORDER if n.startswith("s5_"))
_REPL_LATE = tuple(n for n in _REPL_ORDER if n not in _REPL_EARLY)


def _pack(parts, dtype, lead=None):
    nl = 0 if lead is None else 1
    flat = [a.astype(dtype).reshape(a.shape[:nl] + (-1,)) for a in parts]
    cat = jnp.concatenate(flat, axis=nl)
    n = cat.shape[nl]
    unit = _LANES * _ROW_PAD
    total = -(-n // unit) * unit
    cat = jnp.pad(cat, [(0, 0)] * nl + [(0, total - n)])
    return cat.reshape(cat.shape[:nl] + (total // _LANES, _LANES))


def _unpack(packed, shapes, lead=None):
    nl = 0 if lead is None else 1
    flat = packed.reshape(packed.shape[:nl] + (-1,))
    out, off = [], 0
    for s in shapes:
        n = int(np.prod(s))
        piece = flat[..., off:off + n] if nl else flat[off:off + n]
        out.append(piece.reshape(packed.shape[:nl] + tuple(s)))
        off += n
    return out


def _to_full(gathered, axis):
    g = jnp.moveaxis(gathered, 0, axis)
    s = g.shape
    return g.reshape(s[:axis] + (s[axis] * s[axis + 1],) + s[axis + 2:])


def _to_shards(full, axis):
    s = full.shape
    g = full.reshape(s[:axis] + (N_DEV, s[axis] // N_DEV) + s[axis + 1:])
    return jnp.moveaxis(g, axis, 0)


_DIRECT_ROWS = ("xa_wq", "xa_wk", "xa_wv", "xa_wo", "mix_w_out", "s5_w_glu_a", "s5_w_glu_b")
_SMALL16 = ("mix_w_in", "mla_w_uq", "mla_w_ukv")
_SMALL_SHARDED = _SMALL16 + _EXACT
_SHARD_ROWS = 128


def _exchange_layout(d):
    out = dict(d)
    out["ffn_w_up"] = _pad_groups(d["ffn_w_up"], 2)
    out["ffn_conv_w"] = _pad_groups(d["ffn_conv_w"], 2)
    out["ffn_w_down"] = _pad_groups(d["ffn_w_down"], 1)
    return out


def _train_step(x, mem, positions, target, w, m, v):
    d_model = x.shape[1]
    we_, me_, ve_ = _exchange_layout(w), _exchange_layout(m), _exchange_layout(v)
    sds = jax.ShapeDtypeStruct

    matrices = _DIRECT_ROWS + ("ffn_w_up", "ffn_w_down")
    layer_mats = ("xa_wq", "xa_wk", "xa_wv", "xa_wo", "ffn_w_up", "ffn_w_down")
    shard16 = {n: we_[n].astype(bf16) for n in matrices}
    part_of = {n: _rows_of(_SHARD_ROWS) for n in _DIRECT_ROWS}
    part_of["ffn_w_up"] = _cols_of(we_["ffn_w_up"].shape[2])
    part_of["ffn_w_down"] = _rows_of(we_["ffn_w_down"].shape[1])
    part_shape = {n: we_[n].shape[1:] for n in matrices}

    def full_shape(n):
        r, c = part_shape[n]
        return (r, N_DEV * c) if n == "ffn_w_up" else (N_DEV * r, c)

    def gather(ex, n, layer):
        ex.add(shard16[n], sds(full_shape(n), bf16), _at_layer(layer), part_of[n], (n, layer))

    def scatter(ex, n, layer, grad):
        ex.add(grad, sds((N_DEV,) + part_shape[n], f32), part_of[n], _slot, (n, layer))

    small16 = _pack([we_[n] for n in _SMALL16], bf16)
    exact = _pack([we_[n] for n in _EXACT], f32)
    ga, gb, gc = _Exchange("gather_a"), _Exchange("gather_b"), _Exchange("gather_c")
    ga.add(small16, sds((N_DEV,) + small16.shape, bf16), _whole, _slot, "small16")
    ga.add(exact, sds((N_DEV,) + exact.shape, f32), _whole, _slot, "exact")
    gather(ga, "mix_w_out", 0)
    for n in layer_mats:
        gather(gb, n, 0)
    gather(gc, "s5_w_glu_a", 0)
    gather(gc, "s5_w_glu_b", 0)
    for n in layer_mats:
        gather(gc, n, 1)
    state_a, state_b, state_c = ga.begin(True), gb.begin(True), gc.begin(True)

    full = ga.finish(state_a, [gb.token, gc.token])
    p = {n: w[n] for n in _REPL_ORDER}
    for n, a in zip(_SMALL16, _unpack(full["small16"], [we_[n].shape for n in _SMALL16], lead=True)):
        p[n] = _to_full(a, _SHARDED[n])
    conv_w, p["s5_d"] = [_to_full(a, _SHARDED[n]) for n, a in
                         zip(_EXACT, _unpack(full["exact"], [we_[n].shape for n in _EXACT], lead=True))]
    p["mix_w_out"] = full[("mix_w_out", 0)][None]
    cos, sin = _rope_tables(positions)
    we = _even_weights(p, 0, 0, bf16)
    we["norm_mix"] = _after(we["norm_mix"], ga.token, gb.token, gc.token, name="after_gather_starts")
    conv_b = _pad_groups(w["ffn_conv_b"], 1)

    def layer_weights(layer):
        return dict(norm_xa=w["norm_xa"][layer][None], norm_mem=w["norm_mem"][layer][None],
                    norm_ffn=w["norm_ffn"][layer][None], xa_q_norm=w["xa_q_norm"][layer][None],
                    xa_k_norm=w["xa_k_norm"][layer][None], ffn_conv_w=conv_w[layer],
                    ffn_conv_b=conv_b[layer][None], **{n: full[(n, layer)] for n in layer_mats})

    h, s_mix0 = _mixer_fwd(x, cos, sin, we, "l0")
    full.update(gb.finish(state_b, h))
    wl = [layer_weights(0)]
    h, s_xa0 = _xattn_fwd(h, mem, wl[0], "l0")
    h, s_ff0 = _ffn_fwd(h, wl[0], "l0")
    full.update(gc.finish(state_c, h))
    wl.append(layer_weights(1))
    p["s5_w_glu_a"], p["s5_w_glu_b"] = full[("s5_w_glu_a", 0)][None], full[("s5_w_glu_b", 0)][None]
    wo = _odd_weights(p, 0, 1, bf16)
    h, s_mix1 = _s5_fwd(h, wo, "l1")
    h, s_xa1 = _xattn_fwd(h, mem, wl[1], "l1")
    h, s_ff1 = _ffn_fwd(h, wl[1], "l1")
    dh, loss = _rows(_loss_fn, [h, target], [], [(h.shape[1], f32)], accs=[(1, 1)], name="loss_head")

    gl = [{}, {}]
    reduces = []

    own_grad = {}

    def reduce_start(name, entries, dh):
        ex = _Exchange(name)
        for n, layer, grad in entries.get("matrices", ()):
            scatter(ex, n, layer, grad)
            own_grad[(n, layer)] = grad
        for key, src, shape, src_at in entries.get("packs", ()):
            ex.add(src, shape, src_at, _slot, key)
        reduces.append((ex, ex.begin(False)))
        return _after(dh, ex.token, name="after_" + name)

    dh, gl[1] = _ffn_bwd(dh, wl[1], s_ff1, "l1")
    dh = reduce_start("reduce_ffn1", dict(matrices=[(n, 1, gl[1][n]) for n in ("ffn_w_up", "ffn_w_down")]), dh)
    dh, g = _xattn_bwd(dh, mem, wl[1], s_xa1, "l1")
    gl[1].update(g)
    dh = reduce_start("reduce_xa1", dict(matrices=[(n, 1, g[n]) for n in ("xa_wq", "xa_wk", "xa_wv", "xa_wo")]), dh)
    dh, g_odd = _s5_bwd(dh, wo, s_mix1, "l1")
    go = _odd_grads(g_odd)
    dh, gl[0] = _ffn_bwd(dh, wl[0], s_ff0, "l0")
    send_early = _pack([go[n].reshape(w[n].shape) for n in _REPL_EARLY], f32)
    dh = reduce_start("reduce_ffn0", dict(
        matrices=[(n, 0, g_odd[n]) for n in ("s5_w_glu_a", "s5_w_glu_b")]
        + [(n, 0, gl[0][n]) for n in ("ffn_w_up", "ffn_w_down")],
        packs=[("repl_early", send_early, sds((N_DEV,) + send_early.shape, f32), _whole)]), dh)
    dh, g = _xattn_bwd(dh, mem, wl[0], s_xa0, "l0")
    gl[0].update(g)
    dh = reduce_start("reduce_xa0", dict(matrices=[(n, 0, g[n]) for n in ("xa_wq", "xa_wk", "xa_wv", "xa_wo")]), dh)
    grad_x, g_even = _mixer_bwd(
        dh, cos, sin, we, s_mix0, "l0",
        on_w_out=lambda grad, dmixin: reduce_start("reduce_w_out", dict(matrices=[("mix_w_out", 0, grad)]), dmixin))

    ge = _even_grads(g_even)
    cat = lambda n: jnp.concatenate([gl[0][n], gl[1][n]], axis=0)
    rg = dict(ge)
    rg["norm_mix"] = jnp.concatenate([ge["norm_mix"], go["norm_mix"]], axis=0)
    for n in ("norm_xa", "norm_mem", "norm_ffn", "xa_q_norm", "xa_k_norm"):
        rg[n] = cat(n)
    rg["ffn_conv_b"] = _unpad_groups(cat("ffn_conv_b"), 1)
    sg = dict(mix_w_in=ge["mix_w_in"], mla_w_uq=ge["mla_w_uq"], mla_w_ukv=ge["mla_w_ukv"], s5_d=go["s5_d"],
              ffn_conv_w=jnp.stack([gl[0]["ffn_conv_w"], gl[1]["ffn_conv_w"]]))
    send_small = _pack([_to_shards(sg[n], _SHARDED[n]) for n in _SMALL_SHARDED], f32, lead=True)
    send_late = _pack([rg[n].reshape(w[n].shape) for n in _REPL_LATE], f32)
    last = _Exchange("reduce_last")
    last.add(send_small, sds(send_small.shape, f32), _slot, _slot, "small")
    last.add(send_late, sds((N_DEV,) + send_late.shape, f32), _whole, _slot, "repl_late")
    state_last = last.begin(False)
    slots = {}
    for ex, state in reduces:
        slots.update(ex.finish(state, [grad_x, last.token]))

    me_index = (4 * lax.axis_index("x") + 2 * lax.axis_index("y") + lax.axis_index("c")).astype(jnp.int32).reshape(1)

    def own_block(n):
        r, c = part_shape[n]
        if n == "ffn_w_up":
            return lambda tr: ((tr, c), lambda i, me: (i, me[0]))
        return lambda tr: ((tr, c), lambda i, me: (me[0] * (r // tr) + i, 0))

    out = [{}, {}, {}, {}]
    unpad = dict(ffn_w_up=2, ffn_conv_w=2, ffn_w_down=1)
    done = []
    for n in matrices:
        per_layer = [_sum_adam(me_index, slots[(n, layer)], own_grad[(n, layer)], own_block(n), we_[n][layer],
                               me_[n][layer], ve_[n][layer], name=f"adam_{n}_{layer}")
                     for layer in range(we_[n].shape[0])]
        done += [res[0] for res in per_layer]
        for k in range(4):
            r = jnp.stack([res[k] for res in per_layer])
            out[k][n] = _unpad_groups(r, unpad[n]) if n in unpad else r
    pk = lambda d, order: _pack([d[n] for n in order], f32)
    whole_rows = lambda tr: ((tr, _LANES), lambda i, me: (i, 0))
    res_early = _sum_adam(me_index, slots["repl_early"], send_early, whole_rows, pk(w, _REPL_EARLY), pk(m, _REPL_EARLY),
                          pk(v, _REPL_EARLY), name="adam_repl_early")
    slots = last.finish(state_last, done + [res_early[0]])
    res_small = _sum_adam(me_index, slots["small"], send_small,
                          lambda tr: ((1, tr, _LANES), lambda i, me: (me[0], i, 0)),
                          pk(we_, _SMALL_SHARDED), pk(me_, _SMALL_SHARDED), pk(ve_, _SMALL_SHARDED), name="adam_small")
    res_late = _sum_adam(me_index, slots["repl_late"], send_late, whole_rows, pk(w, _REPL_LATE), pk(m, _REPL_LATE),
                         pk(v, _REPL_LATE), name="adam_repl_late")
    for k in range(4):
        for n, a in zip(_SMALL_SHARDED, _unpack(res_small[k], [we_[n].shape for n in _SMALL_SHARDED])):
            out[k][n] = _unpad_groups(a, unpad[n]) if n in unpad else a
        out[k].update(zip(_REPL_EARLY, _unpack(res_early[k], [w[n].shape for n in _REPL_EARLY])))
        out[k].update(zip(_REPL_LATE, _unpack(res_late[k], [w[n].shape for n in _REPL_LATE])))
    return loss, grad_x, out


_INPUTS = tuple("""x, mem, positions, norm_mix, norm_xa, norm_mem, norm_ffn, xa_wq, xa_wk, xa_wv, xa_wo, xa_q_norm, xa_k_norm, ffn_w_up, ffn_conv_w, ffn_conv_b, ffn_w_down, hg_lb_logits, mix_w_in, hg_out_norm, mla_q_a_norm, mla_w_uq, mla_kv_a_norm, mla_w_ukv, mla_qn_nope, mla_qn_rope, mla_kn_nope, mla_kn_rope, mix_w_out, s5_lam_re, s5_lam_im, s5_log_dt, s5_b_re, s5_b_im, s5_c_re, s5_c_im, s5_d, s5_w_glu_a, s5_w_glu_b, loss_target, m_norm_mix, m_norm_xa, m_norm_mem, m_norm_ffn, m_xa_wq, m_xa_wk, m_xa_wv, m_xa_wo, m_xa_q_norm, m_xa_k_norm, m_ffn_w_up, m_ffn_conv_w, m_ffn_conv_b, m_ffn_w_down, m_hg_lb_logits, m_mix_w_in, m_hg_out_norm, m_mla_q_a_norm, m_mla_w_uq, m_mla_kv_a_norm, m_mla_w_ukv, m_mla_qn_nope, m_mla_qn_rope, m_mla_kn_nope, m_mla_kn_rope, m_mix_w_out, m_s5_lam_re, m_s5_lam_im, m_s5_log_dt, m_s5_b_re, m_s5_b_im, m_s5_c_re, m_s5_c_im, m_s5_d, m_s5_w_glu_a, m_s5_w_glu_b, v_norm_mix, v_norm_xa, v_norm_mem, v_norm_ffn, v_xa_wq, v_xa_wk, v_xa_wv, v_xa_wo, v_xa_q_norm, v_xa_k_norm, v_ffn_w_up, v_ffn_conv_w, v_ffn_conv_b, v_ffn_w_down, v_hg_lb_logits, v_mix_w_in, v_hg_out_norm, v_mla_q_a_norm, v_mla_w_uq, v_mla_kv_a_norm, v_mla_w_ukv, v_mla_qn_nope, v_mla_qn_rope, v_mla_kn_nope, v_mla_kn_rope, v_mix_w_out, v_s5_lam_re, v_s5_lam_im, v_s5_log_dt, v_s5_b_re, v_s5_b_im, v_s5_c_re, v_s5_c_im, v_s5_d, v_s5_w_glu_a, v_s5_w_glu_b""".replace(" ", "").split(","))


def kernel(x, mem, positions, norm_mix, norm_xa, norm_mem, norm_ffn, xa_wq, xa_wk, xa_wv, xa_wo, xa_q_norm, xa_k_norm, ffn_w_up, ffn_conv_w, ffn_conv_b, ffn_w_down, hg_lb_logits, mix_w_in, hg_out_norm, mla_q_a_norm, mla_w_uq, mla_kv_a_norm, mla_w_ukv, mla_qn_nope, mla_qn_rope, mla_kn_nope, mla_kn_rope, mix_w_out, s5_lam_re, s5_lam_im, s5_log_dt, s5_b_re, s5_b_im, s5_c_re, s5_c_im, s5_d, s5_w_glu_a, s5_w_glu_b, loss_target, m_norm_mix, m_norm_xa, m_norm_mem, m_norm_ffn, m_xa_wq, m_xa_wk, m_xa_wv, m_xa_wo, m_xa_q_norm, m_xa_k_norm, m_ffn_w_up, m_ffn_conv_w, m_ffn_conv_b, m_ffn_w_down, m_hg_lb_logits, m_mix_w_in, m_hg_out_norm, m_mla_q_a_norm, m_mla_w_uq, m_mla_kv_a_norm, m_mla_w_ukv, m_mla_qn_nope, m_mla_qn_rope, m_mla_kn_nope, m_mla_kn_rope, m_mix_w_out, m_s5_lam_re, m_s5_lam_im, m_s5_log_dt, m_s5_b_re, m_s5_b_im, m_s5_c_re, m_s5_c_im, m_s5_d, m_s5_w_glu_a, m_s5_w_glu_b, v_norm_mix, v_norm_xa, v_norm_mem, v_norm_ffn, v_xa_wq, v_xa_wk, v_xa_wv, v_xa_wo, v_xa_q_norm, v_xa_k_norm, v_ffn_w_up, v_ffn_conv_w, v_ffn_conv_b, v_ffn_w_down, v_hg_lb_logits, v_mix_w_in, v_hg_out_norm, v_mla_q_a_norm, v_mla_w_uq, v_mla_kv_a_norm, v_mla_w_ukv, v_mla_qn_nope, v_mla_qn_rope, v_mla_kn_nope, v_mla_kn_rope, v_mix_w_out, v_s5_lam_re, v_s5_lam_im, v_s5_log_dt, v_s5_b_re, v_s5_b_im, v_s5_c_re, v_s5_c_im, v_s5_d, v_s5_w_glu_a, v_s5_w_glu_b):
    vals = dict(zip(_INPUTS, (x, mem, positions, norm_mix, norm_xa, norm_mem, norm_ffn, xa_wq, xa_wk, xa_wv, xa_wo, xa_q_norm, xa_k_norm, ffn_w_up, ffn_conv_w, ffn_conv_b, ffn_w_down, hg_lb_logits, mix_w_in, hg_out_norm, mla_q_a_norm, mla_w_uq, mla_kv_a_norm, mla_w_ukv, mla_qn_nope, mla_qn_rope, mla_kn_nope, mla_kn_rope, mix_w_out, s5_lam_re, s5_lam_im, s5_log_dt, s5_b_re, s5_b_im, s5_c_re, s5_c_im, s5_d, s5_w_glu_a, s5_w_glu_b, loss_target, m_norm_mix, m_norm_xa, m_norm_mem, m_norm_ffn, m_xa_wq, m_xa_wk, m_xa_wv, m_xa_wo, m_xa_q_norm, m_xa_k_norm, m_ffn_w_up, m_ffn_conv_w, m_ffn_conv_b, m_ffn_w_down, m_hg_lb_logits, m_mix_w_in, m_hg_out_norm, m_mla_q_a_norm, m_mla_w_uq, m_mla_kv_a_norm, m_mla_w_ukv, m_mla_qn_nope, m_mla_qn_rope, m_mla_kn_nope, m_mla_kn_rope, m_mix_w_out, m_s5_lam_re, m_s5_lam_im, m_s5_log_dt, m_s5_b_re, m_s5_b_im, m_s5_c_re, m_s5_c_im, m_s5_d, m_s5_w_glu_a, m_s5_w_glu_b, v_norm_mix, v_norm_xa, v_norm_mem, v_norm_ffn, v_xa_wq, v_xa_wk, v_xa_wv, v_xa_wo, v_xa_q_norm, v_xa_k_norm, v_ffn_w_up, v_ffn_conv_w, v_ffn_conv_b, v_ffn_w_down, v_hg_lb_logits, v_mix_w_in, v_hg_out_norm, v_mla_q_a_norm, v_mla_w_uq, v_mla_kv_a_norm, v_mla_w_ukv, v_mla_qn_nope, v_mla_qn_rope, v_mla_kn_nope, v_mla_kn_rope, v_mix_w_out, v_s5_lam_re, v_s5_lam_im, v_s5_log_dt, v_s5_b_re, v_s5_b_im, v_s5_c_re, v_s5_c_im, v_s5_d, v_s5_w_glu_a, v_s5_w_glu_b)))
    w = {n: vals[n] for n in _WEIGHTS}
    m = {n: vals["m_" + n] for n in _WEIGHTS}
    v = {n: vals["v_" + n] for n in _WEIGHTS}
    loss, grad_x, res = _train_step(vals["x"][0], vals["mem"][0], vals["positions"][0], vals["loss_target"][0],
                                    w, m, v)
    loss = lax.psum(loss[0, 0], ("x", "y", "c"))
    return (loss, grad_x[None], *[r[n] for r in res for n in _WEIGHTS])
```

```python
import functools

import jax
import jax.numpy as jnp
import numpy as np
from jax import lax
from jax.experimental import pallas as pl
from jax.experimental.pallas import tpu as pltpu

f32 = jnp.float32
bf16 = jnp.bfloat16

EPS = 1e-6
N_DEV = 8
VMEM_LIMIT = 52 * 1024 * 1024

HG_HEADS = 4
HG_DIM = 128
HG_WIDTH = HG_HEADS * HG_DIM
HG_CHUNK = 64
HG_SUB = 16
MLA_HEADS = 4
MLA_Q_RANK = 256
MLA_KV_RANK = 128
MLA_NOPE = 128
MLA_ROPE = 64
MLA_V = 128
MLA_QK = MLA_NOPE + MLA_ROPE
MLA_QK_PAD = 256
ROPE_BASE = 10000.0
IN_WIDTH = 4 * HG_WIDTH + MLA_Q_RANK + MLA_KV_RANK + MLA_ROPE
IN_PAD = 2560
XA_HEADS = 4
XA_DIM = 256
S5_GROUP = 16
S5_GROUPS = 64
S5_STATE = 64
CONV_W = 3

ADAM_LR = 0.001
ADAM_B1 = 0.9
ADAM_B2 = 0.999
ADAM_EPS = 1e-08
ADAM_WD = 0.01
ADAM_STEP = 10

_NT = (((1,), (1,)), ((), ()))
_TN = (((0,), (0,)), ((), ()))
_NN = (((1,), (0,)), ((), ()))


def _pick(n, cands):
    for c in cands:
        if n % c == 0:
            return c
    return n


def _cparams(sem):
    return pltpu.CompilerParams(dimension_semantics=sem, vmem_limit_bytes=VMEM_LIMIT)


_MM_BUDGET = 36 * 1024 * 1024
_MM_TILES = ((1024, 1024), (1024, 512), (512, 1024), (512, 512), (512, 256), (256, 512), (256, 256), (256, 128),
             (128, 256), (128, 128))


def _mm(a, b, *, name, ta=False, tb=False, out_dtype=f32, add=None, b2=None, kslab=None):
    m, k = (a.shape[1], a.shape[0]) if ta else a.shape
    nb = b.shape[0] if tb else b.shape[1]
    n = nb * (2 if b2 is not None else 1)
    slab, nslab = kslab if kslab is not None else (0, 1)
    assert (b.shape[1] // nslab if tb else b.shape[0]) == k, (a.shape, b.shape, ta, tb)
    assert b2 is None or (not tb and b2.shape == b.shape)
    isz = lambda x: jnp.dtype(x.dtype).itemsize
    bm = bn = None
    for cm, cn in _MM_TILES:
        if m % cm or nb % cn:
            continue
        need = 2 * (cm * k * isz(a) + cn * k * isz(b) * (2 if b2 is not None else 1)
                    + cm * cn * (jnp.dtype(out_dtype).itemsize + (4 if add is not None else 0)))
        if need <= _MM_BUDGET:
            bm, bn = cm, cn
            break
    assert bm is not None, (name, a.shape, b.shape)
    half = nb // bn
    dims = (((0 if ta else 1,), (1 if tb else 0,)), ((), ()))

    def body(*refs):
        refs = list(refs)
        a_ref, b_ref = refs[0], refs[1]
        b2_ref = refs.pop(2) if b2 is not None else None
        add_ref = refs[2] if add is not None else None
        o_ref = refs[-1]

        def run(rhs_ref):
            r = lax.dot_general(a_ref[...].astype(bf16), rhs_ref[...].astype(bf16), dims, preferred_element_type=f32)
            if add_ref is not None:
                r = r + add_ref[...].astype(f32)
            o_ref[...] = r.astype(o_ref.dtype)

        if b2_ref is None:
            run(b_ref)
        else:
            pl.when(pl.program_id(1) < half)(lambda: run(b_ref))
            pl.when(pl.program_id(1) >= half)(lambda: run(b2_ref))

    a_spec = pl.BlockSpec((k, bm), lambda i, j: (0, i)) if ta else pl.BlockSpec((bm, k), lambda i, j: (i, 0))
    if tb:
        b_spec = pl.BlockSpec((bn, k), lambda i, j: (j, slab))
    elif b2 is None:
        b_spec = pl.BlockSpec((k, bn), lambda i, j: (0, j))
    else:
        b_spec = pl.BlockSpec((k, bn), lambda i, j: (0, jnp.minimum(j, half - 1)))
    in_specs = [a_spec, b_spec]
    args = [a, b]
    if b2 is not None:
        in_specs.append(pl.BlockSpec((k, bn), lambda i, j: (0, jnp.maximum(j - half, 0))))
        args.append(b2)
    if add is not None:
        in_specs.append(pl.BlockSpec((bm, bn), lambda i, j: (i, j)))
        args.append(add)
    return pl.pallas_call(
        body, grid=(m // bm, n // bn), in_specs=in_specs,
        out_specs=pl.BlockSpec((bm, bn), lambda i, j: (i, j)),
        out_shape=jax.ShapeDtypeStruct((m, n), out_dtype),
        compiler_params=_cparams(("parallel", "parallel")), name=name)(*args)


def _as_tuple(x):
    return tuple(x) if isinstance(x, (tuple, list)) else (x,)


def _full_spec(p):
    nd = p.ndim
    return pl.BlockSpec(p.shape, lambda i, _nd=nd: (0,) * _nd)


def _rows(fn, rows, params, outs, *, name, tile=256, accs=()):
    length = rows[0].shape[0]
    tile = min(tile, length)
    nr, npar, no = len(rows), len(params), len(outs)

    def body(*refs):
        r, p, o = refs[:nr], refs[nr:nr + npar], refs[nr + npar:]
        res = _as_tuple(fn(*[x[...].astype(f32) for x in r], *[x[...] for x in p]))
        for kk in range(no):
            o[kk][...] = res[kk].astype(o[kk].dtype)
        if accs:
            @pl.when(pl.program_id(0) == 0)
            def _():
                for kk in range(no, no + len(accs)):
                    o[kk][...] = jnp.zeros_like(o[kk])
            for kk in range(no, no + len(accs)):
                o[kk][...] += res[kk]

    in_specs = [pl.BlockSpec((tile, x.shape[1]), lambda i: (i, 0)) for x in rows] + [_full_spec(p) for p in params]
    out_specs = [pl.BlockSpec((tile, w), lambda i: (i, 0)) for w, _ in outs]
    out_shape = [jax.ShapeDtypeStruct((length, w), d) for w, d in outs]
    for s in accs:
        out_specs.append(pl.BlockSpec(s, lambda i, _nd=len(s): (0,) * _nd))
        out_shape.append(jax.ShapeDtypeStruct(s, f32))
    res = pl.pallas_call(body, grid=(length // tile,), in_specs=in_specs, out_specs=out_specs, out_shape=out_shape,
                         compiler_params=_cparams(("arbitrary",)), name=name)(*rows, *params)
    return res


def _rows_bwd(fn, rows, params, cts, *, name, rgrad, pgrad, tile=256, addends=None):
    addends = addends or {}
    length = rows[0].shape[0]
    tile = min(tile, length)
    nr, npar, nc = len(rows), len(params), len(cts)
    ridx = [i for i in range(nr) if rgrad[i] is not None]
    pidx = [i for i in range(npar) if pgrad[i]]
    aidx = sorted(addends)
    na = len(aidx)

    def body(*refs):
        r, p, c = refs[:nr], refs[nr:nr + npar], refs[nr + npar:nr + npar + nc]
        ad = refs[nr + npar + nc:nr + npar + nc + na]
        o = refs[nr + npar + nc + na:]
        rv = [x[...].astype(f32) for x in r]
        pv = [x[...] for x in p]
        cv = tuple(x[...].astype(f32) for x in c)

        def g(*d):
            rr, pp = list(rv), list(pv)
            for n_, i_ in enumerate(ridx):
                rr[i_] = d[n_]
            for n_, i_ in enumerate(pidx):
                pp[i_] = d[len(ridx) + n_]
            return _as_tuple(fn(*rr, *pp))

        _, vjp = jax.vjp(g, *[rv[i] for i in ridx], *[pv[i] for i in pidx])
        grads = vjp(cv)
        for n_, i_ in enumerate(ridx):
            val = grads[n_]
            if i_ in addends:
                val = val + ad[aidx.index(i_)][...].astype(f32)
            o[n_][...] = val.astype(o[n_].dtype)
        if pidx:
            @pl.when(pl.program_id(0) == 0)
            def _():
                for n_ in range(len(pidx)):
                    o[len(ridx) + n_][...] = jnp.zeros_like(o[len(ridx) + n_])
            for n_ in range(len(pidx)):
                o[len(ridx) + n_][...] += grads[len(ridx) + n_]

    row_spec = lambda x: pl.BlockSpec((tile, x.shape[1]), lambda i: (i, 0))
    in_specs = ([row_spec(x) for x in rows] + [_full_spec(p) for p in params] + [row_spec(x) for x in cts]
                + [row_spec(addends[i]) for i in aidx])
    out_specs = [row_spec(rows[i]) for i in ridx] + [_full_spec(params[i]) for i in pidx]
    out_shape = ([jax.ShapeDtypeStruct(rows[i].shape, rgrad[i]) for i in ridx]
                 + [jax.ShapeDtypeStruct(params[i].shape, f32) for i in pidx])
    res = pl.pallas_call(body, grid=(length // tile,), in_specs=in_specs, out_specs=out_specs, out_shape=out_shape,
                         compiler_params=_cparams(("arbitrary",)), name=name)(
        *rows, *params, *cts, *[addends[i] for i in aidx])
    return list(res[:len(ridx)]), list(res[len(ridx):])


def _rms(x, g):
    return x * lax.rsqrt(jnp.mean(x * x, axis=-1, keepdims=True) + EPS) * g


def _rms_twice(x, g):
    y = _rms(x, g)
    return y, y


def _silu(x):
    return x * jax.nn.sigmoid(x)


def _shift_down(x, s):
    rows = lax.broadcasted_iota(jnp.int32, x.shape, 0)
    return jnp.where(rows >= s, pltpu.roll(x, s, axis=0), 0.0)


def _shift_up(x, s):
    n = x.shape[0]
    rows = lax.broadcasted_iota(jnp.int32, x.shape, 0)
    return jnp.where(rows < n - s, pltpu.roll(x, n - s, axis=0), 0.0)


_CONV_COLS = 128


def _conv_gate_fwd(u, cw, cb, *, name):
    length, two_f = u.shape
    ff = two_f // 2
    nb = ff // _CONV_COLS

    def body(ug, uv, wg, wv, bg, bv, o):
        def conv(x_ref, w_ref, b_ref):
            x = x_ref[...].astype(f32)
            return (w_ref[2:3, :] * x + w_ref[1:2, :] * _shift_down(x, 1) + w_ref[0:1, :] * _shift_down(x, 2)
                    + b_ref[...])
        o[...] = (_silu(conv(ug, wg, bg)) * conv(uv, wv, bv)).astype(o.dtype)

    blk = lambda r, off: pl.BlockSpec((r, _CONV_COLS), lambda j, _o=off: (0, j + _o))
    return pl.pallas_call(
        body, grid=(nb,),
        in_specs=[blk(length, 0), blk(length, nb), blk(CONV_W, 0), blk(CONV_W, nb), blk(1, 0), blk(1, nb)],
        out_specs=blk(length, 0), out_shape=jax.ShapeDtypeStruct((length, ff), bf16),
        compiler_params=_cparams(("parallel",)), name=name)(u, u, cw, cw, cb, cb)


def _conv_gate_bwd(u, cw, cb, da, *, name):
    length, two_f = u.shape
    ff = two_f // 2
    nb = ff // _CONV_COLS

    def body(ug, uv, wg, wv, bg, bv, da_ref, dug, duv, dwg, dwv, dbg, dbv):
        def conv(x, w_ref, b_ref):
            x1, x2 = _shift_down(x, 1), _shift_down(x, 2)
            return w_ref[2:3, :] * x + w_ref[1:2, :] * x1 + w_ref[0:1, :] * x2 + b_ref[...], x1, x2

        xg, xv = ug[...].astype(f32), uv[...].astype(f32)
        g, xg1, xg2 = conv(xg, wg, bg)
        v, xv1, xv2 = conv(xv, wv, bv)
        d = da_ref[...].astype(f32)
        sg = jax.nn.sigmoid(g)
        dg = d * v * (sg * (1.0 + g * (1.0 - sg)))
        dv = d * (g * sg)

        def back(dy, x, x1, x2, w_ref, du_ref, dw_ref, db_ref):
            du_ref[...] = (w_ref[2:3, :] * dy + w_ref[1:2, :] * _shift_up(dy, 1)
                           + w_ref[0:1, :] * _shift_up(dy, 2)).astype(du_ref.dtype)
            dw_ref[2:3, :] = jnp.sum(dy * x, axis=0, keepdims=True)
            dw_ref[1:2, :] = jnp.sum(dy * x1, axis=0, keepdims=True)
            dw_ref[0:1, :] = jnp.sum(dy * x2, axis=0, keepdims=True)
            db_ref[...] = jnp.sum(dy, axis=0, keepdims=True)

        back(dg, xg, xg1, xg2, wg, dug, dwg, dbg)
        back(dv, xv, xv1, xv2, wv, duv, dwv, dbv)

    blk = lambda r, off: pl.BlockSpec((r, _CONV_COLS), lambda j, _o=off: (0, j + _o))
    sds = jax.ShapeDtypeStruct
    dug, duv, dwg, dwv, dbg, dbv = pl.pallas_call(
        body, grid=(nb,),
        in_specs=[blk(length, 0), blk(length, nb), blk(CONV_W, 0), blk(CONV_W, nb), blk(1, 0), blk(1, nb),
                  blk(length, 0)],
        out_specs=[blk(length, 0), blk(length, 0), blk(CONV_W, 0), blk(CONV_W, 0), blk(1, 0), blk(1, 0)],
        out_shape=[sds((length, ff), bf16), sds((length, ff), bf16), sds((CONV_W, ff), f32), sds((CONV_W, ff), f32),
                   sds((1, ff), f32), sds((1, ff), f32)],
        compiler_params=_cparams(("parallel",)), name=name)(u, u, cw, cw, cb, cb, da)
    return dug, duv, jnp.concatenate([dwg, dwv], axis=1), jnp.concatenate([dbg, dbv], axis=1)


def _ffn_fwd(h, w, tag):
    hf, = _rows(_rms, [h], [w["norm_ffn"]], [(h.shape[1], bf16)], name=f"ffn_norm_{tag}")
    u = _mm(hf, w["ffn_w_up"], out_dtype=bf16, name=f"ffn_up_{tag}")
    a = _conv_gate_fwd(u, w["ffn_conv_w"], w["ffn_conv_b"], name=f"ffn_conv_{tag}")
    out = _mm(a, w["ffn_w_down"], add=h, name=f"ffn_down_{tag}")
    return out, (h, hf, u, a)


def _ffn_bwd(dout, w, saved, tag):
    h, hf, u, a = saved
    ff = a.shape[1]
    da = _mm(dout, w["ffn_w_down"], tb=True, out_dtype=bf16, name=f"ffn_da_{tag}")
    g = {"ffn_w_down": _mm(a, dout, ta=True, name=f"ffn_dwdown_{tag}")}
    dug, duv, g["ffn_conv_w"], g["ffn_conv_b"] = _conv_gate_bwd(u, w["ffn_conv_w"], w["ffn_conv_b"], da,
                                                                name=f"ffn_dconv_{tag}")
    dhf = _mm(dug, w["ffn_w_up"], tb=True, kslab=(0, 2), name=f"ffn_dhf_g_{tag}")
    dhf = _mm(duv, w["ffn_w_up"], tb=True, kslab=(1, 2), add=dhf, out_dtype=bf16, name=f"ffn_dhf_v_{tag}")
    g["ffn_w_up"] = _mm(hf, dug, ta=True, b2=duv, name=f"ffn_dwup_{tag}")
    (dh,), (g["norm_ffn"],) = _rows_bwd(_rms, [h], [w["norm_ffn"]], [dhf], rgrad=[f32], pgrad=[True],
                                        addends={0: dout}, name=f"ffn_dnorm_{tag}")
    return dh, g


def _xattn_fn(qx, kx, vx, qg, kg):
    outs = []
    for hh in range(XA_HEADS):
        sl = slice(hh * XA_DIM, (hh + 1) * XA_DIM)
        q = _rms(qx[:, sl], qg).astype(bf16)
        k = _rms(kx[:, sl], kg).astype(bf16)
        s = lax.dot_general(q, k, _NT, preferred_element_type=f32) * (XA_DIM ** -0.5)
        s = s - jnp.max(s, axis=-1, keepdims=True)
        p = jnp.exp(s)
        p = p / jnp.sum(p, axis=-1, keepdims=True)
        outs.append(jnp.dot(p.astype(bf16), vx[:, sl].astype(bf16), preferred_element_type=f32))
    return jnp.concatenate(outs, axis=-1)


def _xattn_fwd(h, mem, w, tag):
    d = h.shape[1]
    hx, = _rows(_rms, [h], [w["norm_xa"]], [(d, bf16)], name=f"xa_norm_{tag}")
    qx = _mm(hx, w["xa_wq"], name=f"xa_q_{tag}")
    m, = _rows(_rms, [mem], [w["norm_mem"]], [(d, bf16)], name=f"xa_mnorm_{tag}")
    kx = _mm(m, w["xa_wk"], name=f"xa_k_{tag}")
    vx = _mm(m, w["xa_wv"], name=f"xa_v_{tag}")
    o, = _rows(_xattn_fn, [qx], [kx, vx, w["xa_q_norm"], w["xa_k_norm"]], [(d, bf16)], tile=512,
               name=f"xa_attn_{tag}")
    out = _mm(o, w["xa_wo"], add=h, name=f"xa_o_{tag}")
    return out, (h, hx, qx, m, kx, vx, o)


def _xattn_bwd(dout, mem, w, saved, tag):
    h, hx, qx, m, kx, vx, o = saved
    g = {}
    do = _mm(dout, w["xa_wo"], tb=True, out_dtype=bf16, name=f"xa_do_{tag}")
    g["xa_wo"] = _mm(o, dout, ta=True, name=f"xa_dwo_{tag}")
    (dqx,), (dkx, dvx, g["xa_q_norm"], g["xa_k_norm"]) = _rows_bwd(
        _xattn_fn, [qx], [kx, vx, w["xa_q_norm"], w["xa_k_norm"]], [do], rgrad=[bf16], pgrad=[True] * 4,
        tile=512, name=f"xa_dattn_{tag}")
    dhx = _mm(dqx, w["xa_wq"], tb=True, out_dtype=bf16, name=f"xa_dhx_{tag}")
    g["xa_wq"] = _mm(hx, dqx, ta=True, name=f"xa_dwq_{tag}")
    (dh,), (g["norm_xa"],) = _rows_bwd(_rms, [h], [w["norm_xa"]], [dhx], rgrad=[f32], pgrad=[True],
                                       addends={0: dout}, name=f"xa_dnorm_{tag}")
    dm = _mm(dkx, w["xa_wk"], tb=True, name=f"xa_dm_k_{tag}")
    dm = _mm(dvx, w["xa_wv"], tb=True, add=dm, name=f"xa_dm_v_{tag}")
    g["xa_wk"] = _mm(m, dkx, ta=True, name=f"xa_dwk_{tag}")
    g["xa_wv"] = _mm(m, dvx, ta=True, name=f"xa_dwv_{tag}")
    _, (g["norm_mem"],) = _rows_bwd(_rms, [mem], [w["norm_mem"]], [dm], rgrad=[None], pgrad=[True],
                                    name=f"xa_dmnorm_{tag}")
    return dh, g


_HG_GROUP = 4


def _hg_chunk(q, k, v, g, *sts):
    c = q.shape[0]
    heads = [slice(h * HG_DIM, (h + 1) * HG_DIM) for h in range(len(sts))]
    tri = (lax.broadcasted_iota(jnp.int32, (c, c), 0) >= lax.broadcasted_iota(jnp.int32, (c, c), 1)).astype(f32)
    b = jnp.dot(tri, g, precision=lax.Precision.HIGHEST, preferred_element_type=f32)
    bend = jnp.sum(g, axis=0, keepdims=True)
    qe = (q * jnp.exp(b)).astype(bf16)
    kd = (k * jnp.exp(bend - b)).astype(bf16)
    vb = v.astype(bf16)
    decay = jnp.exp(bend)
    o_inter = [lax.dot_general(qe[:, hs], st.astype(bf16), _NT, preferred_element_type=f32) for hs, st in zip(heads, sts)]
    new = [st * decay[:, hs] + lax.dot_general(vb[:, hs], kd[:, hs], _TN, preferred_element_type=f32)
           for hs, st in zip(heads, sts)]
    outs = []
    for i in range(c // HG_SUB):
        lo, n = HG_SUB * i, HG_SUB * (i + 1)
        ref = jnp.sum(g[:lo], axis=0, keepdims=True) if i else jnp.zeros((1, g.shape[1]), f32)
        qh = (q[lo:n] * jnp.exp(b[lo:n] - ref)).astype(bf16)
        kh = (k[:n] * jnp.exp(ref - b[:n])).astype(bf16)
        keep = (lax.broadcasted_iota(jnp.int32, (HG_SUB, n), 1)
                <= lo + lax.broadcasted_iota(jnp.int32, (HG_SUB, n), 0))
        scores = [lax.dot_general(qh[:, hs], kh[:, hs], _NT, preferred_element_type=f32) for hs in heads]
        scores = [jnp.where(keep, a, 0.0).astype(bf16) for a in scores]
        outs.append(jnp.concatenate([jnp.dot(a, vb[:n, hs], preferred_element_type=f32)
                                     for a, hs in zip(scores, heads)], axis=1))
    return (jnp.concatenate(outs, axis=0) + jnp.concatenate(o_inter, axis=1), *new)


def _hg_fwd(q, k, v, g, *, name):
    length = q.shape[0]
    rows = _HG_GROUP * HG_CHUNK
    ng = length // rows
    nc = length // HG_CHUNK

    def body(q_ref, k_ref, v_ref, g_ref, o_ref, st_ref, state):
        @pl.when(pl.program_id(0) == 0)
        def _():
            state[...] = jnp.zeros_like(state)

        states = [state[h] for h in range(HG_HEADS)]
        for ci in range(_HG_GROUP):
            sl = slice(ci * HG_CHUNK, (ci + 1) * HG_CHUNK)
            for h in range(HG_HEADS):
                st_ref[h, ci] = states[h]
            o, *states = _hg_chunk(q_ref[sl, :], k_ref[sl, :], v_ref[sl, :], g_ref[sl, :], *states)
            o_ref[sl, :] = o
        for h in range(HG_HEADS):
            state[h] = states[h]

    blk = pl.BlockSpec((rows, HG_WIDTH), lambda c: (c, 0))
    return pl.pallas_call(
        body, grid=(ng,), in_specs=[blk] * 4,
        out_specs=[blk, pl.BlockSpec((HG_HEADS, _HG_GROUP, HG_DIM, HG_DIM), lambda c: (0, c, 0, 0))],
        out_shape=[jax.ShapeDtypeStruct((length, HG_WIDTH), f32),
                   jax.ShapeDtypeStruct((HG_HEADS, nc, HG_DIM, HG_DIM), f32)],
        scratch_shapes=[pltpu.VMEM((HG_HEADS, HG_DIM, HG_DIM), f32)],
        compiler_params=_cparams(("arbitrary",)), name=name)(q, k, v, g)


def _hg_bwd(q, k, v, g, states, do, *, name):
    length = q.shape[0]
    rows = _HG_GROUP * HG_CHUNK
    ng = length // rows

    def body(q_ref, k_ref, v_ref, g_ref, st_ref, do_ref, dq_ref, dk_ref, dv_ref, dg_ref, dstate):
        @pl.when(pl.program_id(0) == 0)
        def _():
            dstate[...] = jnp.zeros_like(dstate)

        dstates = [dstate[h] for h in range(HG_HEADS)]
        for ci in reversed(range(_HG_GROUP)):
            sl = slice(ci * HG_CHUNK, (ci + 1) * HG_CHUNK)
            _, vjp = jax.vjp(_hg_chunk, q_ref[sl, :], k_ref[sl, :], v_ref[sl, :], g_ref[sl, :],
                             *[st_ref[h, ci] for h in range(HG_HEADS)])
            dq, dk, dv, dg, *dstates = vjp((do_ref[sl, :], *dstates))
            dq_ref[sl, :] = dq
            dk_ref[sl, :] = dk
            dv_ref[sl, :] = dv
            dg_ref[sl, :] = dg
        for h in range(HG_HEADS):
            dstate[h] = dstates[h]

    blk = pl.BlockSpec((rows, HG_WIDTH), lambda c: (ng - 1 - c, 0))
    sds = jax.ShapeDtypeStruct((length, HG_WIDTH), f32)
    return pl.pallas_call(
        body, grid=(ng,),
        in_specs=[blk] * 4 + [pl.BlockSpec((HG_HEADS, _HG_GROUP, HG_DIM, HG_DIM), lambda c: (0, ng - 1 - c, 0, 0)), blk],
        out_specs=[blk] * 4, out_shape=[sds] * 4,
        scratch_shapes=[pltpu.VMEM((HG_HEADS, HG_DIM, HG_DIM), f32)],
        compiler_params=_cparams(("arbitrary",)), name=name)(q, k, v, g, states, do)


_ATT_BLK = 256
_ATT_SCALE = MLA_QK ** -0.5
_NEG = -1e30


def _att_mask(i, j, t):
    rows = i * t + lax.broadcasted_iota(jnp.int32, (t, t), 0)
    cols = j * t + lax.broadcasted_iota(jnp.int32, (t, t), 1)
    return cols <= rows


def _att_fwd(q, k, v, *, name):
    length = q.shape[0]
    t = min(_ATT_BLK, length)
    nq = length // t
    qw, vw = MLA_QK_PAD, MLA_V
    heads = range(MLA_HEADS)

    def body(q_ref, k_ref, v_ref, o_ref, lse_ref):
        i = pl.program_id(0)
        qbs = [q_ref[:, h * qw:(h + 1) * qw] for h in heads]

        def step(j, carry, diagonal=False):
            off = pl.multiple_of(j * t, t)
            out = []
            for h in heads:
                m, l, acc = carry[h]
                ks = k_ref[pl.ds(off, t), h * qw:(h + 1) * qw]
                vs = v_ref[pl.ds(off, t), h * vw:(h + 1) * vw]
                s = lax.dot_general(qbs[h], ks, _NT, preferred_element_type=f32) * _ATT_SCALE
                if diagonal:
                    s = jnp.where(_att_mask(i, j, t), s, _NEG)
                m_new = jnp.maximum(m, jnp.max(s, axis=-1, keepdims=True))
                alpha = jnp.exp(m - m_new)
                p = jnp.exp(s - m_new)
                l = alpha * l + jnp.sum(p, axis=-1, keepdims=True)
                acc = alpha * acc + jnp.dot(p.astype(bf16), vs, preferred_element_type=f32)
                out.append((m_new, l, acc))
            return tuple(out)

        init = tuple((jnp.full((t, 1), _NEG, f32), jnp.zeros((t, 1), f32), jnp.zeros((t, vw), f32)) for _ in heads)
        res = step(i, lax.fori_loop(0, i, step, init), diagonal=True)
        for h in heads:
            m, l, acc = res[h]
            o_ref[:, h * vw:(h + 1) * vw] = (acc / l).astype(o_ref.dtype)
            lse_ref[:, h * vw:(h + 1) * vw] = jnp.broadcast_to(m + jnp.log(l), (t, vw))

    return pl.pallas_call(
        body, grid=(nq,),
        in_specs=[pl.BlockSpec((t, q.shape[1]), lambda i: (i, 0)), pl.BlockSpec(k.shape, lambda i: (0, 0)),
                  pl.BlockSpec(v.shape, lambda i: (0, 0))],
        out_specs=[pl.BlockSpec((t, v.shape[1]), lambda i: (i, 0))] * 2,
        out_shape=[jax.ShapeDtypeStruct(v.shape, bf16), jax.ShapeDtypeStruct(v.shape, f32)],
        compiler_params=_cparams(("arbitrary",)), name=name)(q, k, v)


def _att_bwd(q, k, v, o, lse, do, *, name):
    length = q.shape[0]
    t = min(_ATT_BLK, length)
    nq = length // t
    qw, vw = MLA_QK_PAD, MLA_V
    heads = range(MLA_HEADS)

    def dq_body(q_ref, k_ref, v_ref, o_ref, lse_ref, do_ref, dq_ref, delta_ref):
        i = pl.program_id(0)
        qbs = [q_ref[:, h * qw:(h + 1) * qw] for h in heads]
        dobs = [do_ref[:, h * vw:(h + 1) * vw] for h in heads]
        lses = [lse_ref[:, h * vw:h * vw + 1] for h in heads]
        deltas = [jnp.sum(dobs[h].astype(f32) * o_ref[:, h * vw:(h + 1) * vw].astype(f32), axis=-1, keepdims=True)
                  for h in heads]

        def step(j, dqs, diagonal=False):
            off = pl.multiple_of(j * t, t)
            out = []
            for h in heads:
                ks = k_ref[pl.ds(off, t), h * qw:(h + 1) * qw]
                vs = v_ref[pl.ds(off, t), h * vw:(h + 1) * vw]
                s = lax.dot_general(qbs[h], ks, _NT, preferred_element_type=f32) * _ATT_SCALE
                p = jnp.exp(s - lses[h])
                if diagonal:
                    p = jnp.where(_att_mask(i, j, t), p, 0.0)
                dp = lax.dot_general(dobs[h], vs, _NT, preferred_element_type=f32)
                ds = p * (dp - deltas[h]) * _ATT_SCALE
                out.append(dqs[h] + jnp.dot(ds.astype(bf16), ks, preferred_element_type=f32))
            return tuple(out)

        dqs = step(i, lax.fori_loop(0, i, step, tuple(jnp.zeros((t, qw), f32) for _ in heads)), diagonal=True)
        for h in heads:
            dq_ref[:, h * qw:(h + 1) * qw] = dqs[h].astype(dq_ref.dtype)
            delta_ref[:, h * vw:(h + 1) * vw] = jnp.broadcast_to(deltas[h], (t, vw))

    qblk = pl.BlockSpec((t, q.shape[1]), lambda i: (i, 0))
    vblk = pl.BlockSpec((t, v.shape[1]), lambda i: (i, 0))
    qfull = pl.BlockSpec(q.shape, lambda i: (0, 0))
    vfull = pl.BlockSpec(v.shape, lambda i: (0, 0))
    dq, delta = pl.pallas_call(
        dq_body, grid=(nq,), in_specs=[qblk, qfull, vfull, vblk, vblk, vblk], out_specs=[qblk, vblk],
        out_shape=[jax.ShapeDtypeStruct(q.shape, bf16), jax.ShapeDtypeStruct(lse.shape, f32)],
        compiler_params=_cparams(("arbitrary",)), name=name + "_dq")(q, k, v, o, lse, do)

    def dkv_body(k_ref, v_ref, q_ref, do_ref, lse_ref, delta_ref, dk_ref, dv_ref):
        j = pl.program_id(0)
        kbs = [k_ref[:, h * qw:(h + 1) * qw] for h in heads]
        vbs = [v_ref[:, h * vw:(h + 1) * vw] for h in heads]

        def step(i, carry, diagonal=False):
            off = pl.multiple_of(i * t, t)
            out = []
            for h in heads:
                dk, dv = carry[h]
                qs = q_ref[pl.ds(off, t), h * qw:(h + 1) * qw]
                dos = do_ref[pl.ds(off, t), h * vw:(h + 1) * vw]
                lse_i = lse_ref[pl.ds(off, t), h * vw:h * vw + 1]
                delta_i = delta_ref[pl.ds(off, t), h * vw:h * vw + 1]
                s = lax.dot_general(qs, kbs[h], _NT, preferred_element_type=f32) * _ATT_SCALE
                p = jnp.exp(s - lse_i)
                if diagonal:
                    p = jnp.where(_att_mask(i, j, t), p, 0.0)
                dv = dv + lax.dot_general(p.astype(bf16), dos, _TN, preferred_element_type=f32)
                dp = lax.dot_general(dos, vbs[h], _NT, preferred_element_type=f32)
                ds = p * (dp - delta_i) * _ATT_SCALE
                dk = dk + lax.dot_general(ds.astype(bf16), qs, _TN, preferred_element_type=f32)
                out.append((dk, dv))
            return tuple(out)

        first = step(j, tuple((jnp.zeros((t, qw), f32), jnp.zeros((t, vw), f32)) for _ in heads), diagonal=True)
        res = lax.fori_loop(j + 1, nq, step, first)
        for h in heads:
            dk_ref[:, h * qw:(h + 1) * qw] = res[h][0].astype(dk_ref.dtype)
            dv_ref[:, h * vw:(h + 1) * vw] = res[h][1].astype(dv_ref.dtype)

    dk, dv = pl.pallas_call(
        dkv_body, grid=(nq,), in_specs=[qblk, vblk, qfull, vfull, vfull, vfull], out_specs=[qblk, vblk],
        out_shape=[jax.ShapeDtypeStruct(k.shape, bf16), jax.ShapeDtypeStruct(v.shape, bf16)],
        compiler_params=_cparams(("arbitrary",)), name=name + "_dkv")(k, v, q, do, lse, delta)
    return dq, dk, dv


_C_Q = 4 * HG_WIDTH
_C_KV = _C_Q + MLA_Q_RANK
_C_KPE = _C_KV + MLA_KV_RANK


def _rms_n(x, g, n):
    return x * lax.rsqrt(jnp.sum(x * x, axis=-1, keepdims=True) * (1.0 / n) + EPS) * g


def _mix_a(proj, l0, l1, q_a_norm, kv_a_norm):
    lb = jax.nn.sigmoid(l0 - l1)
    f = lb + (1.0 - lb) * jax.nn.sigmoid(proj[:, HG_WIDTH:2 * HG_WIDTH])
    qf = _silu(proj[:, :HG_WIDTH])
    v = proj[:, 2 * HG_WIDTH:3 * HG_WIDTH]
    cqn = _rms(proj[:, _C_Q:_C_KV], q_a_norm)
    ckvn = _rms(proj[:, _C_KV:_C_KPE], kv_a_norm)
    return qf, 1.0 - f, v, jnp.log(f), cqn, ckvn


def _mix_b(qraw, kvraw, proj, cos, sin, qn_nope, qn_rope, kn_nope, kn_rope, perm):
    def rope(x):
        return x * cos + jnp.dot(x, perm, precision=lax.Precision.HIGHEST, preferred_element_type=f32) * sin

    kpe = rope(_rms_n(proj[:, _C_KPE:], kn_rope, MLA_ROPE))
    qs, ks, vs = [], [], []
    for hh in range(MLA_HEADS):
        base = hh * MLA_QK_PAD
        qs.append(_rms(qraw[:, base:base + MLA_NOPE], qn_nope))
        qs.append(rope(_rms_n(qraw[:, base + MLA_NOPE:base + MLA_QK_PAD], qn_rope, MLA_ROPE)))
        ks.append(_rms(kvraw[:, base:base + MLA_NOPE], kn_nope))
        ks.append(kpe)
        vs.append(kvraw[:, base + MLA_NOPE:base + MLA_QK_PAD])
    return jnp.concatenate(qs, axis=-1), jnp.concatenate(ks, axis=-1), jnp.concatenate(vs, axis=-1)


def _mix_c(o_hg, proj, o_mla, hg_out_norm):
    parts = []
    for hh in range(HG_HEADS):
        sl = slice(hh * HG_DIM, (hh + 1) * HG_DIM)
        parts.append(_rms(o_hg[:, sl], hg_out_norm[:, sl]))
    o = jnp.concatenate(parts, axis=-1) * _silu(proj[:, 3 * HG_WIDTH:4 * HG_WIDTH])
    return jnp.concatenate([o, o_mla], axis=-1)


def _rope_perm():
    p = np.zeros((128, 128), np.float32)
    half = MLA_ROPE // 2
    for i in range(half):
        p[i + half, i] = -1.0
        p[i, i + half] = 1.0
    return jnp.asarray(p)


def _mixer_fwd(h, cos, sin, w, tag):
    d = h.shape[1]
    hn, = _rows(_rms, [h], [w["norm_mix"]], [(d, bf16)], name=f"mix_norm_{tag}")
    proj = _mm(hn, w["mix_w_in"], name=f"mix_in_{tag}")
    pa = [w["lb0"], w["lb1"], w["mla_q_a_norm"], w["mla_kv_a_norm"]]
    qf, kk, vv, logf, cqn, ckvn = _rows(
        _mix_a, [proj], pa, [(HG_WIDTH, f32)] * 4 + [(MLA_Q_RANK, bf16), (MLA_KV_RANK, bf16)], name=f"mix_a_{tag}")
    o_hg, states = _hg_fwd(qf, kk, vv, logf, name=f"hg_fwd_{tag}")
    qraw = _mm(cqn, w["mla_w_uq"], name=f"mla_uq_{tag}")
    kvraw = _mm(ckvn, w["mla_w_ukv"], name=f"mla_ukv_{tag}")
    pb = [w["mla_qn_nope"], w["mla_qn_rope"], w["mla_kn_nope"], w["mla_kn_rope"], w["rope_perm"]]
    qfull, kfull, vfull = _rows(_mix_b, [qraw, kvraw, proj, cos, sin], pb,
                                [(MLA_HEADS * MLA_QK_PAD, bf16)] * 2 + [(MLA_HEADS * MLA_V, bf16)],
                                name=f"mix_b_{tag}")
    o_mla, lse = _att_fwd(qfull, kfull, vfull, name=f"att_fwd_{tag}")
    mixin, = _rows(_mix_c, [o_hg, proj, o_mla], [w["hg_out_norm"]], [(d, bf16)], name=f"mix_c_{tag}")
    out = _mm(mixin, w["mix_w_out"], add=h, name=f"mix_out_{tag}")
    return out, (h, hn, proj, qf, kk, vv, logf, cqn, ckvn, o_hg, states, qraw, kvraw, qfull, kfull, vfull, o_mla,
                 lse, mixin)


def _mixer_bwd(dout, cos, sin, w, saved, tag, on_w_out=None):
    (h, hn, proj, qf, kk, vv, logf, cqn, ckvn, o_hg, states, qraw, kvraw, qfull, kfull, vfull, o_mla, lse,
     mixin) = saved
    g = {}
    dmixin = _mm(dout, w["mix_w_out"], tb=True, name=f"mix_dmixin_{tag}")
    g["mix_w_out"] = _mm(mixin, dout, ta=True, name=f"mix_dwout_{tag}")
    if on_w_out is not None:
        dmixin = on_w_out(g["mix_w_out"], dmixin)
    (do_hg, dproj_c, do_mla), (g["hg_out_norm"],) = _rows_bwd(
        _mix_c, [o_hg, proj, o_mla], [w["hg_out_norm"]], [dmixin], rgrad=[f32, f32, bf16], pgrad=[True],
        name=f"mix_dc_{tag}")
    dqfull, dkfull, dvfull = _att_bwd(qfull, kfull, vfull, o_mla, lse, do_mla, name=f"att_bwd_{tag}")
    pb = [w["mla_qn_nope"], w["mla_qn_rope"], w["mla_kn_nope"], w["mla_kn_rope"], w["rope_perm"]]
    (dqraw, dkvraw, dproj_b), pg = _rows_bwd(
        _mix_b, [qraw, kvraw, proj, cos, sin], pb, [dqfull, dkfull, dvfull],
        rgrad=[bf16, bf16, f32, None, None], pgrad=[True, True, True, True, False], addends={2: dproj_c},
        name=f"mix_db_{tag}")
    g["mla_qn_nope"], g["mla_qn_rope"], g["mla_kn_nope"], g["mla_kn_rope"] = pg
    dcqn = _mm(dqraw, w["mla_w_uq"], tb=True, name=f"mla_dcq_{tag}")
    g["mla_w_uq"] = _mm(cqn, dqraw, ta=True, name=f"mla_dwuq_{tag}")
    dckvn = _mm(dkvraw, w["mla_w_ukv"], tb=True, name=f"mla_dckv_{tag}")
    g["mla_w_ukv"] = _mm(ckvn, dkvraw, ta=True, name=f"mla_dwukv_{tag}")
    dqf, dkk, dvv, dlogf = _hg_bwd(qf, kk, vv, logf, states, do_hg, name=f"hg_bwd_{tag}")
    pa = [w["lb0"], w["lb1"], w["mla_q_a_norm"], w["mla_kv_a_norm"]]
    (dproj,), (g["lb0"], g["lb1"], g["mla_q_a_norm"], g["mla_kv_a_norm"]) = _rows_bwd(
        _mix_a, [proj], pa, [dqf, dkk, dvv, dlogf, dcqn, dckvn], rgrad=[bf16], pgrad=[True] * 4,
        addends={0: dproj_b}, name=f"mix_da_{tag}")
    dhn = _mm(dproj, w["mix_w_in"], tb=True, out_dtype=bf16, name=f"mix_dhn_{tag}")
    g["mix_w_in"] = _mm(hn, dproj, ta=True, name=f"mix_dwin_{tag}")
    (dh,), (g["norm_mix"],) = _rows_bwd(_rms, [h], [w["norm_mix"]], [dhn], rgrad=[f32], pgrad=[True],
                                        addends={0: dout}, name=f"mix_dnorm_{tag}")
    return dh, g


def _rope_tables(positions):
    inv_freq = 1.0 / (ROPE_BASE ** (jnp.arange(0, MLA_ROPE, 2, dtype=f32) / MLA_ROPE))
    ang = positions.astype(f32)[:, None] * inv_freq
    z = jnp.zeros((positions.shape[0], 128 - MLA_ROPE), f32)
    return (jnp.concatenate([jnp.cos(ang), jnp.cos(ang), z], axis=1),
            jnp.concatenate([jnp.sin(ang), jnp.sin(ang), z], axis=1))


def _pad_cols(a, n):
    return jnp.pad(a, ((0, 0), (0, n - a.shape[1])))


def _even_weights(p, j, layer, dt):
    w_uq = p["mla_w_uq"][j].reshape(MLA_Q_RANK, MLA_HEADS, MLA_QK)
    w_uq = jnp.pad(w_uq, ((0, 0), (0, 0), (0, MLA_QK_PAD - MLA_QK))).reshape(MLA_Q_RANK, MLA_HEADS * MLA_QK_PAD)
    return dict(
        norm_mix=p["norm_mix"][layer][None], mix_w_in=_pad_cols(p["mix_w_in"][j], IN_PAD).astype(dt),
        lb0=p["hg_lb_logits"][0][None], lb1=p["hg_lb_logits"][1][None],
        mla_q_a_norm=p["mla_q_a_norm"][j][None], mla_kv_a_norm=p["mla_kv_a_norm"][j][None],
        mla_w_uq=w_uq.astype(dt), mla_w_ukv=p["mla_w_ukv"][j].astype(dt),
        mla_qn_nope=p["mla_qn_nope"][j][None], mla_qn_rope=_pad_cols(p["mla_qn_rope"][j][None], 128),
        mla_kn_nope=p["mla_kn_nope"][j][None], mla_kn_rope=_pad_cols(p["mla_kn_rope"][j][None], 128),
        rope_perm=_rope_perm(), hg_out_norm=p["hg_out_norm"][j][None], mix_w_out=p["mix_w_out"][j].astype(dt))


def _even_grads(g):
    w_uq = g["mla_w_uq"].reshape(MLA_Q_RANK, MLA_HEADS, MLA_QK_PAD)[:, :, :MLA_QK].reshape(MLA_Q_RANK, -1)
    return dict(
        norm_mix=g["norm_mix"], mix_w_in=g["mix_w_in"][:, :IN_WIDTH][None],
        hg_lb_logits=jnp.concatenate([g["lb0"], g["lb1"]], axis=0),
        mla_q_a_norm=g["mla_q_a_norm"], mla_kv_a_norm=g["mla_kv_a_norm"], mla_w_uq=w_uq[None],
        mla_w_ukv=g["mla_w_ukv"][None], mla_qn_nope=g["mla_qn_nope"], mla_qn_rope=g["mla_qn_rope"][:, :MLA_ROPE],
        mla_kn_nope=g["mla_kn_nope"], mla_kn_rope=g["mla_kn_rope"][:, :MLA_ROPE],
        hg_out_norm=g["hg_out_norm"], mix_w_out=g["mix_w_out"][None])


_S5_NB = 8
_S5_BW = 1024
_S5_HALF = 512
_S5_UC = 128
_S5_TIME = 512


def _bd_mm(a, b3, *, name, tb=False, out_dtype=f32):
    length = a.shape[0]
    rows_b, cols_b = b3.shape[0] // _S5_NB, b3.shape[1]
    ka, n = (cols_b, rows_b) if tb else (rows_b, cols_b)
    bm = _pick(length, (512, 256, 128))
    dims = _NT if tb else _NN

    def body(a_ref, b_ref, o_ref):
        o_ref[...] = lax.dot_general(a_ref[...].astype(bf16), b_ref[...].astype(bf16), dims,
                                     preferred_element_type=f32).astype(o_ref.dtype)

    return pl.pallas_call(
        body, grid=(length // bm, _S5_NB),
        in_specs=[pl.BlockSpec((bm, ka), lambda i, j: (i, j)), pl.BlockSpec((rows_b, cols_b), lambda i, j: (j, 0))],
        out_specs=pl.BlockSpec((bm, n), lambda i, j: (i, j)),
        out_shape=jax.ShapeDtypeStruct((length, _S5_NB * n), out_dtype),
        compiler_params=_cparams(("parallel", "parallel")), name=name)(a, b3)


def _bd_mm_tn(a, c, *, name):
    length = a.shape[0]
    ka, n = a.shape[1] // _S5_NB, c.shape[1] // _S5_NB
    bk = _pick(length, (512, 256, 128))
    nk = length // bk

    def body(a_ref, c_ref, o_ref):
        @pl.when(pl.program_id(1) == 0)
        def _():
            o_ref[...] = jnp.zeros_like(o_ref)
        o_ref[...] += lax.dot_general(a_ref[...].astype(bf16), c_ref[...].astype(bf16), _TN,
                                      preferred_element_type=f32)

    return pl.pallas_call(
        body, grid=(_S5_NB, nk),
        in_specs=[pl.BlockSpec((bk, ka), lambda j, q: (q, j)), pl.BlockSpec((bk, n), lambda j, q: (q, j))],
        out_specs=pl.BlockSpec((ka, n), lambda j, q: (j, 0)),
        out_shape=jax.ShapeDtypeStruct((_S5_NB * ka, n), f32),
        compiler_params=_cparams(("parallel", "arbitrary")), name=name)(a, c)


def _cmul(ar, ai, br, bi):
    return ar * br - ai * bi, ar * bi + ai * br


def _pow_table(ar, ai, descending):
    rows = lax.broadcasted_iota(jnp.int32, (8, ar.shape[1]), 0)
    tr = jnp.zeros((8, ar.shape[1]), f32)
    ti = jnp.zeros((8, ar.shape[1]), f32)
    pr, pi_ = ar, ai
    for r in range(8):
        sel = rows == ((7 - r) if descending else r)
        tr = jnp.where(sel, pr, tr)
        ti = jnp.where(sel, pi_, ti)
        pr, pi_ = _cmul(pr, pi_, ar, ai)
    return tr, ti


def _s5_scan_fwd(a, x, *, name):
    length = x.shape[0]
    tc = min(_S5_TIME, length)
    hw = _S5_HALF

    def body(a_ref, x_ref, o_ref, carry):
        @pl.when(pl.program_id(1) == 0)
        def _():
            carry[...] = jnp.zeros_like(carry)

        ar, ai = a_ref[:, :hw], a_ref[:, hw:]
        xr, xi = x_ref[:, :hw], x_ref[:, hw:]
        row8 = lax.broadcasted_iota(jnp.int32, (tc, hw), 0) & 7
        mr, mi = ar, ai
        for s in (1, 2, 4):
            sr, si = pltpu.roll(xr, s, axis=0), pltpu.roll(xi, s, axis=0)
            pr, pi_ = _cmul(mr, mi, sr, si)
            ok = row8 >= s
            xr = xr + jnp.where(ok, pr, 0.0)
            xi = xi + jnp.where(ok, pi_, 0.0)
            mr, mi = _cmul(mr, mi, mr, mi)
        o_ref[:, :hw] = xr
        o_ref[:, hw:] = xi
        tr, ti = _pow_table(ar, ai, False)
        cr, ci = carry[:, :hw], carry[:, hw:]
        for i in range(tc // 8):
            sl = slice(8 * i, 8 * i + 8)
            pr, pi_ = _cmul(tr, ti, cr, ci)
            o_ref[sl, :hw] = o_ref[sl, :hw] + pr
            o_ref[sl, hw:] = o_ref[sl, hw:] + pi_
            cr, ci = o_ref[8 * i + 7:8 * i + 8, :hw], o_ref[8 * i + 7:8 * i + 8, hw:]
        carry[:, :hw] = cr
        carry[:, hw:] = ci

    return pl.pallas_call(
        body, grid=(_S5_NB, length // tc),
        in_specs=[pl.BlockSpec((1, _S5_BW), lambda j, t: (0, j)), pl.BlockSpec((tc, _S5_BW), lambda j, t: (t, j))],
        out_specs=pl.BlockSpec((tc, _S5_BW), lambda j, t: (t, j)),
        out_shape=jax.ShapeDtypeStruct(x.shape, f32),
        scratch_shapes=[pltpu.VMEM((1, _S5_BW), f32)],
        compiler_params=_cparams(("parallel", "arbitrary")), name=name)(a, x)


def _s5_scan_bwd(a, hs, d, *, name):
    length = d.shape[0]
    tc = min(_S5_TIME, length)
    nt = length // tc
    hw = _S5_HALF

    def body(a_ref, h_ref, d_ref, g_ref, da_ref, carry):
        @pl.when(pl.program_id(1) == 0)
        def _():
            carry[...] = jnp.zeros_like(carry)
            da_ref[...] = jnp.zeros_like(da_ref)

        ar, ai = a_ref[:, :hw], -a_ref[:, hw:]
        xr, xi = d_ref[:, :hw], d_ref[:, hw:]
        rows = lax.broadcasted_iota(jnp.int32, (tc, hw), 0)
        row8 = rows & 7
        mr, mi = ar, ai
        for s in (1, 2, 4):
            sr, si = pltpu.roll(xr, tc - s, axis=0), pltpu.roll(xi, tc - s, axis=0)
            pr, pi_ = _cmul(mr, mi, sr, si)
            ok = row8 < 8 - s
            xr = xr + jnp.where(ok, pr, 0.0)
            xi = xi + jnp.where(ok, pi_, 0.0)
            mr, mi = _cmul(mr, mi, mr, mi)
        g_ref[:, :hw] = xr
        g_ref[:, hw:] = xi
        tr, ti = _pow_table(ar, ai, True)
        cr0, ci0 = carry[:, :hw], carry[:, hw:]
        cr, ci = cr0, ci0
        for i in reversed(range(tc // 8)):
            sl = slice(8 * i, 8 * i + 8)
            pr, pi_ = _cmul(tr, ti, cr, ci)
            g_ref[sl, :hw] = g_ref[sl, :hw] + pr
            g_ref[sl, hw:] = g_ref[sl, hw:] + pi_
            cr, ci = g_ref[8 * i:8 * i + 1, :hw], g_ref[8 * i:8 * i + 1, hw:]
        carry[:, :hw] = cr
        carry[:, hw:] = ci
        last = rows == tc - 1
        gnr = jnp.where(last, cr0, pltpu.roll(g_ref[:, :hw], tc - 1, axis=0))
        gni = jnp.where(last, ci0, pltpu.roll(g_ref[:, hw:], tc - 1, axis=0))
        hr, hi = h_ref[:, :hw], h_ref[:, hw:]
        da_ref[:, :hw] += jnp.sum(hr * gnr + hi * gni, axis=0, keepdims=True)
        da_ref[:, hw:] += jnp.sum(hr * gni - hi * gnr, axis=0, keepdims=True)

    blk = pl.BlockSpec((tc, _S5_BW), lambda j, t: (nt - 1 - t, j))
    row = pl.BlockSpec((1, _S5_BW), lambda j, t: (0, j))
    return pl.pallas_call(
        body, grid=(_S5_NB, nt), in_specs=[row, blk, blk], out_specs=[blk, row],
        out_shape=[jax.ShapeDtypeStruct(d.shape, f32), jax.ShapeDtypeStruct((1, _S5_NB * _S5_BW), f32)],
        scratch_shapes=[pltpu.VMEM((1, _S5_BW), f32)],
        compiler_params=_cparams(("parallel", "arbitrary")), name=name)(a, hs, d)


def _s5_tile_scan(work, carry, ar, ai, tc, reverse, per_tile=None):
    hw = _S5_HALF
    row8 = lax.broadcasted_iota(jnp.int32, (8, hw), 0)
    powers = [(ar, ai)]
    for _ in range(2):
        powers.append(_cmul(*powers[-1], *powers[-1]))
    tr, ti = _pow_table(ar, ai, reverse)
    cr, ci = carry[:, :hw], carry[:, hw:]
    tiles = range(tc // 8)
    for i in (reversed(tiles) if reverse else tiles):
        sl = slice(8 * i, 8 * i + 8)
        xr, xi = work[sl, :hw], work[sl, hw:]
        for (mr, mi), s in zip(powers, (1, 2, 4)):
            shift = 8 - s if reverse else s
            pr, pi_ = _cmul(mr, mi, pltpu.roll(xr, shift, axis=0), pltpu.roll(xi, shift, axis=0))
            ok = (row8 < 8 - s) if reverse else (row8 >= s)
            xr = xr + jnp.where(ok, pr, 0.0)
            xi = xi + jnp.where(ok, pi_, 0.0)
        pr, pi_ = _cmul(tr, ti, cr, ci)
        xr, xi = xr + pr, xi + pi_
        work[sl, :hw] = xr
        work[sl, hw:] = xi
        if per_tile is not None:
            per_tile(sl, xr, xi, cr, ci)
        edge = 8 * i if reverse else 8 * i + 7
        cr, ci = work[edge:edge + 1, :hw], work[edge:edge + 1, hw:]
    carry[:, :hw] = cr
    carry[:, hw:] = ci


def _s5_core_fwd(a, hn, b3, c3, *, name):
    length = hn.shape[0]
    tc = min(_S5_TIME, length)

    def body(a_ref, hn_ref, b_ref, c_ref, hs_ref, y_ref, work, carry):
        @pl.when(pl.program_id(1) == 0)
        def _():
            carry[...] = jnp.zeros_like(carry)

        work[...] = jnp.dot(hn_ref[...].astype(bf16), b_ref[...], preferred_element_type=f32)
        _s5_tile_scan(work, carry, a_ref[:, :_S5_HALF], a_ref[:, _S5_HALF:], tc, False)
        hs = work[...].astype(bf16)
        hs_ref[...] = hs
        y_ref[...] = jnp.dot(hs, c_ref[...], preferred_element_type=f32)

    return pl.pallas_call(
        body, grid=(_S5_NB, length // tc),
        in_specs=[pl.BlockSpec((1, _S5_BW), lambda j, t: (0, j)), pl.BlockSpec((tc, _S5_UC), lambda j, t: (t, j)),
                  pl.BlockSpec((_S5_UC, _S5_BW), lambda j, t: (j, 0)), pl.BlockSpec((_S5_BW, _S5_UC), lambda j, t: (j, 0))],
        out_specs=[pl.BlockSpec((tc, _S5_BW), lambda j, t: (t, j)), pl.BlockSpec((tc, _S5_UC), lambda j, t: (t, j))],
        out_shape=[jax.ShapeDtypeStruct((length, _S5_NB * _S5_BW), bf16),
                   jax.ShapeDtypeStruct((length, _S5_NB * _S5_UC), f32)],
        scratch_shapes=[pltpu.VMEM((tc, _S5_BW), f32), pltpu.VMEM((1, _S5_BW), f32)],
        compiler_params=_cparams(("parallel", "arbitrary")), name=name)(a, hn, b3, c3)


def _s5_core_bwd(a, dy, c3, hs, hn, b3, *, name):
    length = hn.shape[0]
    tc = min(_S5_TIME, length)
    nt = length // tc
    hw = _S5_HALF

    def body(a_ref, dy_ref, c_ref, hs_ref, hn_ref, b_ref, du_ref, db_ref, dc_ref, da_ref, work, carry, acc):
        @pl.when(pl.program_id(1) == 0)
        def _():
            carry[...] = jnp.zeros_like(carry)
            db_ref[...] = jnp.zeros_like(db_ref)
            dc_ref[...] = jnp.zeros_like(dc_ref)
            da_ref[...] = jnp.zeros_like(da_ref)

        dyb = dy_ref[...].astype(bf16)
        work[...] = lax.dot_general(dyb, c_ref[...], _NT, preferred_element_type=f32)
        acc[...] = jnp.zeros_like(acc)
        row8 = lax.broadcasted_iota(jnp.int32, (8, hw), 0)

        def grad_a(sl, gr, gi, cr, ci):
            gnr = jnp.where(row8 == 7, cr, pltpu.roll(gr, 7, axis=0))
            gni = jnp.where(row8 == 7, ci, pltpu.roll(gi, 7, axis=0))
            hr, hi = hs_ref[sl, :hw].astype(f32), hs_ref[sl, hw:].astype(f32)
            acc[:, :hw] += hr * gnr + hi * gni
            acc[:, hw:] += hr * gni - hi * gnr

        _s5_tile_scan(work, carry, a_ref[:, :hw], -a_ref[:, hw:], tc, True, grad_a)
        da_ref[...] += jnp.sum(acc[...], axis=0, keepdims=True)
        g = work[...].astype(bf16)
        du_ref[...] = lax.dot_general(g, b_ref[...], _NT, preferred_element_type=f32)
        db_ref[...] += lax.dot_general(hn_ref[...].astype(bf16), g, _TN, preferred_element_type=f32)
        dc_ref[...] += lax.dot_general(hs_ref[...], dyb, _TN, preferred_element_type=f32)

    rev = lambda j, t: (nt - 1 - t, j)
    return pl.pallas_call(
        body, grid=(_S5_NB, nt),
        in_specs=[pl.BlockSpec((1, _S5_BW), lambda j, t: (0, j)), pl.BlockSpec((tc, _S5_UC), rev),
                  pl.BlockSpec((_S5_BW, _S5_UC), lambda j, t: (j, 0)), pl.BlockSpec((tc, _S5_BW), rev),
                  pl.BlockSpec((tc, _S5_UC), rev), pl.BlockSpec((_S5_UC, _S5_BW), lambda j, t: (j, 0))],
        out_specs=[pl.BlockSpec((tc, _S5_UC), rev), pl.BlockSpec((_S5_UC, _S5_BW), lambda j, t: (j, 0)),
                   pl.BlockSpec((_S5_BW, _S5_UC), lambda j, t: (j, 0)), pl.BlockSpec((1, _S5_BW), lambda j, t: (0, j))],
        out_shape=[jax.ShapeDtypeStruct((length, _S5_NB * _S5_UC), f32),
                   jax.ShapeDtypeStruct((_S5_NB * _S5_UC, _S5_BW), f32),
                   jax.ShapeDtypeStruct((_S5_NB * _S5_BW, _S5_UC), f32),
                   jax.ShapeDtypeStruct((1, _S5_NB * _S5_BW), f32)],
        scratch_shapes=[pltpu.VMEM((tc, _S5_BW), f32), pltpu.VMEM((1, _S5_BW), f32), pltpu.VMEM((8, _S5_BW), f32)],
        compiler_params=_cparams(("parallel", "arbitrary")), name=name)(a, dy, c3, hs, hn, b3)


def _s5_disc(lr, li, ldt, btr, bti, expand):
    dt = jnp.exp(ldt)
    mag = jnp.exp(lr * dt)
    abr = mag * jnp.cos(li * dt)
    abi = mag * jnp.sin(li * dt)
    den = lr * lr + li * li
    zr = ((abr - 1.0) * lr + abi * li) / den
    zi = (abi * lr - (abr - 1.0) * li) / den
    zr = jnp.dot(zr, expand, precision=lax.Precision.HIGHEST, preferred_element_type=f32)
    zi = jnp.dot(zi, expand, precision=lax.Precision.HIGHEST, preferred_element_type=f32)
    return abr, abi, zr * btr - zi * bti, zr * bti + zi * btr


def _s5_disc_fwd(args, *, name):
    def body(*refs):
        res = _s5_disc(*[r[...] for r in refs[:6]])
        for o, v in zip(refs[6:], res):
            o[...] = v

    sds = jax.ShapeDtypeStruct
    return pl.pallas_call(body, out_shape=[sds(args[0].shape, f32)] * 2 + [sds(args[3].shape, f32)] * 2,
                          name=name)(*args)


def _s5_disc_bwd(args, cts, *, name):
    def body(*refs):
        vals = [r[...] for r in refs[:6]]
        _, vjp = jax.vjp(lambda *d: _s5_disc(*d, vals[5]), *vals[:5])
        grads = vjp(tuple(r[...] for r in refs[6:10]))
        for o, v in zip(refs[10:], grads):
            o[...] = v

    return pl.pallas_call(body, out_shape=[jax.ShapeDtypeStruct(a.shape, f32) for a in args[:5]],
                          name=name)(*args, *cts)


def _gelu_tanh(x):
    return 0.5 * x * (1.0 + jnp.tanh(0.7978845608028654 * (x + 0.044715 * (x * x * x))))


def _s5_post(y, u, d_skip):
    return _gelu_tanh(y + d_skip * u)


def _s5_glu(ga, gb, h):
    return h + ga * jax.nn.sigmoid(gb)


def _s5_expand():
    e = np.zeros((S5_STATE, S5_GROUP * S5_STATE), np.float32)
    for m in range(S5_GROUP):
        e[np.arange(S5_STATE), m * S5_STATE + np.arange(S5_STATE)] = 1.0
    return jnp.asarray(e)


def _s5_pack_b(bbr, bbi):
    eye = jnp.eye(8, dtype=f32)

    def one(bb):
        b5 = bb.reshape(_S5_NB, 8, S5_GROUP, S5_STATE)
        return jnp.einsum("jgmp,gh->jgmhp", b5, eye).reshape(_S5_NB * _S5_UC, _S5_HALF)

    return jnp.concatenate([one(bbr), one(bbi)], axis=1)


def _s5_unpack_b(db3):
    def one(d):
        d5 = d.reshape(_S5_NB, 8, S5_GROUP, 8, S5_STATE)
        return jnp.einsum("jgmgp->jgmp", d5).reshape(S5_GROUPS, S5_GROUP * S5_STATE)

    return one(db3[:, :_S5_HALF]), one(db3[:, _S5_HALF:])


def _s5_pack_c(c_re, c_im):
    eye = jnp.eye(8, dtype=f32)

    def one(c):
        c4 = c.reshape(_S5_NB, 8, S5_GROUP, S5_STATE)
        return jnp.einsum("jgmp,hg->jhpgm", c4, eye).reshape(_S5_NB, _S5_HALF, _S5_UC)

    return jnp.concatenate([one(c_re), -one(c_im)], axis=1).reshape(_S5_NB * _S5_BW, _S5_UC)


def _s5_unpack_c(dc3):
    d = dc3.reshape(_S5_NB, 2, 8, S5_STATE, 8, S5_GROUP)
    dre = jnp.einsum("jgpgm->jgmp", d[:, 0]).reshape(S5_GROUPS, S5_GROUP, S5_STATE)
    dim = -jnp.einsum("jgpgm->jgmp", d[:, 1]).reshape(S5_GROUPS, S5_GROUP, S5_STATE)
    return dre, dim


def _s5_state_row(re, im):
    r = re.reshape(_S5_NB, 1, _S5_HALF)
    i = im.reshape(_S5_NB, 1, _S5_HALF)
    return jnp.concatenate([r, i], axis=2).reshape(1, _S5_NB * _S5_BW)


def _s5_unstate_row(row):
    r = row.reshape(_S5_NB, 2, 8, S5_STATE)
    return r[:, 0].reshape(S5_GROUPS, S5_STATE), r[:, 1].reshape(S5_GROUPS, S5_STATE)


def _s5_fwd(h, w, tag):
    d = h.shape[1]
    hn, = _rows(_rms, [h], [w["norm_mix"]], [(d, f32)], name=f"s5_norm_{tag}")
    disc_in = [w["s5_lam_re"], w["s5_lam_im"], w["s5_log_dt"], w["s5_bt_re"], w["s5_bt_im"], w["s5_expand"]]
    abr, abi, bbr, bbi = _s5_disc_fwd(disc_in, name=f"s5_disc_{tag}")
    a_row = _s5_state_row(abr, abi)
    b3 = _s5_pack_b(bbr, bbi).astype(bf16)
    hs, y = _s5_core_fwd(a_row, hn, b3, w["s5_c3"], name=f"s5_core_{tag}")
    yg, = _rows(_s5_post, [y, hn], [w["s5_d"]], [(d, bf16)], name=f"s5_post_{tag}")
    ga = _mm(yg, w["s5_w_glu_a"], name=f"s5_glu_a_{tag}")
    gb = _mm(yg, w["s5_w_glu_b"], name=f"s5_glu_b_{tag}")
    out, = _rows(_s5_glu, [ga, gb, h], [], [(d, f32)], name=f"s5_glu_{tag}")
    return out, (h, hn, disc_in, a_row, b3, hs, y, yg, ga, gb)


def _s5_bwd(dout, w, saved, tag):
    h, hn, disc_in, a_row, b3, hs, y, yg, ga, gb = saved
    g = {}
    (dga, dgb), _ = _rows_bwd(_s5_glu, [ga, gb, h], [], [dout], rgrad=[bf16, bf16, None], pgrad=[],
                              name=f"s5_dglu_{tag}")
    dyg = _mm(dga, w["s5_w_glu_a"], tb=True, name=f"s5_dyg_a_{tag}")
    dyg = _mm(dgb, w["s5_w_glu_b"], tb=True, add=dyg, name=f"s5_dyg_b_{tag}")
    g["s5_w_glu_a"] = _mm(yg, dga, ta=True, name=f"s5_dwa_{tag}")
    g["s5_w_glu_b"] = _mm(yg, dgb, ta=True, name=f"s5_dwb_{tag}")
    (dy, du_skip), (g["s5_d"],) = _rows_bwd(_s5_post, [y, hn], [w["s5_d"]], [dyg], rgrad=[bf16, f32], pgrad=[True],
                                           name=f"s5_dpost_{tag}")
    du, db3, dc3, da_row = _s5_core_bwd(a_row, dy, w["s5_c3"], hs, hn, b3, name=f"s5_dcore_{tag}")
    dabr, dabi = _s5_unstate_row(da_row)
    dbbr, dbbi = _s5_unpack_b(db3)
    g["s5_lam_re"], g["s5_lam_im"], g["s5_log_dt"], g["s5_bt_re"], g["s5_bt_im"] = _s5_disc_bwd(
        disc_in, [dabr, dabi, dbbr, dbbi], name=f"s5_ddisc_{tag}")
    g["s5_c_re"], g["s5_c_im"] = _s5_unpack_c(dc3)
    (dh,), (g["norm_mix"],) = _rows_bwd(_rms_twice, [h], [w["norm_mix"]], [du, du_skip], rgrad=[f32], pgrad=[True],
                                        addends={0: dout}, name=f"s5_dnorm_{tag}")
    return dh, g


def _odd_weights(p, j, layer, dt):
    tr = lambda b: b.transpose(0, 2, 1).reshape(S5_GROUPS, S5_GROUP * S5_STATE)
    return dict(
        norm_mix=p["norm_mix"][layer][None], s5_lam_re=p["s5_lam_re"][j], s5_lam_im=p["s5_lam_im"][j],
        s5_log_dt=p["s5_log_dt"][j][:, None], s5_bt_re=tr(p["s5_b_re"][j]), s5_bt_im=tr(p["s5_b_im"][j]),
        s5_expand=_s5_expand(), s5_c3=_s5_pack_c(p["s5_c_re"][j], p["s5_c_im"][j]).astype(dt),
        s5_d=p["s5_d"][j][None], s5_w_glu_a=p["s5_w_glu_a"][j].astype(dt), s5_w_glu_b=p["s5_w_glu_b"][j].astype(dt))


def _odd_grads(g):
    tr = lambda b: b.reshape(S5_GROUPS, S5_GROUP, S5_STATE).transpose(0, 2, 1)[None]
    return dict(
        norm_mix=g["norm_mix"], s5_lam_re=g["s5_lam_re"][None], s5_lam_im=g["s5_lam_im"][None],
        s5_log_dt=g["s5_log_dt"][:, 0][None], s5_b_re=tr(g["s5_bt_re"]), s5_b_im=tr(g["s5_bt_im"]),
        s5_c_re=g["s5_c_re"][None], s5_c_im=g["s5_c_im"][None], s5_d=g["s5_d"],
        s5_w_glu_a=g["s5_w_glu_a"][None], s5_w_glu_b=g["s5_w_glu_b"][None])


def _loss_fn(y, t):
    e = y - t
    part = jnp.sum(jnp.sum(e * e, axis=-1, keepdims=True), axis=0, keepdims=True) * (0.5 / y.shape[1])
    return e * (1.0 / y.shape[1]), part


FF_SHARD = 352
FF_SHARD_PAD = 384


def _pad_groups(a, axis):
    axis %= a.ndim
    s = a.shape
    a = a.reshape(s[:axis] + (s[axis] // FF_SHARD, FF_SHARD) + s[axis + 1:])
    pad = [(0, 0)] * a.ndim
    pad[axis + 1] = (0, FF_SHARD_PAD - FF_SHARD)
    return jnp.pad(a, pad).reshape(s[:axis] + (s[axis] // FF_SHARD * FF_SHARD_PAD,) + s[axis + 1:])


def _unpad_groups(a, axis):
    axis %= a.ndim
    s = a.shape
    a = a.reshape(s[:axis] + (s[axis] // FF_SHARD_PAD, FF_SHARD_PAD) + s[axis + 1:])
    a = lax.slice_in_dim(a, 0, FF_SHARD, axis=axis + 1)
    return a.reshape(s[:axis] + (s[axis] // FF_SHARD_PAD * FF_SHARD,) + s[axis + 1:])


def _layer_weights(p, layer, dt):
    return dict(
        norm_xa=p["norm_xa"][layer][None], norm_mem=p["norm_mem"][layer][None], norm_ffn=p["norm_ffn"][layer][None],
        xa_wq=p["xa_wq"][layer].astype(dt), xa_wk=p["xa_wk"][layer].astype(dt), xa_wv=p["xa_wv"][layer].astype(dt),
        xa_wo=p["xa_wo"][layer].astype(dt), xa_q_norm=p["xa_q_norm"][layer][None],
        xa_k_norm=p["xa_k_norm"][layer][None], ffn_w_up=_pad_groups(p["ffn_w_up"][layer], 1).astype(dt),
        ffn_conv_w=_pad_groups(p["ffn_conv_w"][layer], 1), ffn_conv_b=_pad_groups(p["ffn_conv_b"][layer][None], 1),
        ffn_w_down=_pad_groups(p["ffn_w_down"][layer], 0).astype(dt))


_PER_LAYER = ("norm_xa", "norm_mem", "norm_ffn", "xa_wq", "xa_wk", "xa_wv", "xa_wo", "xa_q_norm", "xa_k_norm",
              "ffn_w_up", "ffn_conv_w", "ffn_conv_b", "ffn_w_down")
_FFN_PADDED = dict(ffn_w_up=1, ffn_conv_w=1, ffn_conv_b=1, ffn_w_down=0)


def _local_step(x, mem, positions, target, p):
    cos, sin = _rope_tables(positions)
    we = _even_weights(p, 0, 0, bf16)
    wo = _odd_weights(p, 0, 1, bf16)
    wl = [_layer_weights(p, layer, bf16) for layer in range(2)]
    loss, dh, g_even, g_odd, gl = _local_core(x, mem, cos, sin, target, we, wo, wl)
    grads = {}
    for n in _PER_LAYER:
        a, b = gl[0][n], gl[1][n]
        if n in _FFN_PADDED:
            a, b = _unpad_groups(a, _FFN_PADDED[n]), _unpad_groups(b, _FFN_PADDED[n])
        grads[n] = jnp.concatenate([a, b], axis=0) if a.shape[0] == 1 else jnp.stack([a, b])
    ge, go = _even_grads(g_even), _odd_grads(g_odd)
    grads["norm_mix"] = jnp.concatenate([ge.pop("norm_mix"), go.pop("norm_mix")], axis=0)
    grads.update(ge)
    grads.update(go)
    return loss, dh, grads


def _local_core(x, mem, cos, sin, target, we, wo, wl):
    h, s_mix0 = _mixer_fwd(x, cos, sin, we, "l0")
    h, s_xa0 = _xattn_fwd(h, mem, wl[0], "l0")
    h, s_ff0 = _ffn_fwd(h, wl[0], "l0")
    h, s_mix1 = _s5_fwd(h, wo, "l1")
    h, s_xa1 = _xattn_fwd(h, mem, wl[1], "l1")
    h, s_ff1 = _ffn_fwd(h, wl[1], "l1")
    dh, loss = _rows(_loss_fn, [h, target], [], [(h.shape[1], f32)], accs=[(1, 1)], name="loss_head")

    gl = [{}, {}]
    dh, g = _ffn_bwd(dh, wl[1], s_ff1, "l1")
    gl[1].update(g)
    dh, g = _xattn_bwd(dh, mem, wl[1], s_xa1, "l1")
    gl[1].update(g)
    dh, g_odd = _s5_bwd(dh, wo, s_mix1, "l1")
    dh, g = _ffn_bwd(dh, wl[0], s_ff0, "l0")
    gl[0].update(g)
    dh, g = _xattn_bwd(dh, mem, wl[0], s_xa0, "l0")
    gl[0].update(g)
    dh, g_even = _mixer_bwd(dh, cos, sin, we, s_mix0, "l0")
    return loss, dh, g_even, g_odd, gl


_LANES = 1024
_ROW_PAD = 256


_PEER_MASKS = (1, 2, 4, 3, 5, 6, 7)


def _mesh_place():
    x, y, c = lax.axis_index("x"), lax.axis_index("y"), lax.axis_index("c")

    def peer(mask):
        px = 1 - x if mask & 4 else x
        py = 1 - y if mask & 2 else y
        pc = 1 - c if mask & 1 else c
        return (px, py, pc), 4 * px + 2 * py + pc

    return 4 * x + 2 * y + c, peer


class _Exchange:
    def __init__(self, name):
        self.name = name
        self.srcs, self.shapes, self.items, self.where = [], [], [], {}

    def add(self, src, land_shape, src_at, dst_at, key):
        si = next((i for i, s in enumerate(self.srcs) if s is src), None)
        if si is None:
            self.srcs.append(src)
            si = len(self.srcs) - 1
        if key not in self.where:
            self.shapes.append(land_shape)
            self.where[key] = len(self.shapes) - 1
        self.items.append(dict(src=si, dst=self.where[key], src_at=src_at, dst_at=dst_at))

    def _copy(self, k, mask, ins, lands, send_sems, recv_sems, me, peer, arriving):
        it = self.items[k]
        dev, idx = peer(mask)
        s = k * (N_DEV - 1) + _PEER_MASKS.index(mask)
        return pltpu.make_async_remote_copy(
            src_ref=it["src_at"](ins[it["src"]], idx), dst_ref=it["dst_at"](lands[it["dst"]], idx if arriving else me),
            send_sem=send_sems.at[s], recv_sem=recv_sems.at[s], device_id=dev, device_id_type=pl.DeviceIdType.MESH)

    def _own_copy(self, k, ins, lands, own_sems, me):
        it = self.items[k]
        return pltpu.make_async_copy(it["src_at"](ins[it["src"]], me), it["dst_at"](lands[it["dst"]], me), own_sems.at[k])

    def begin(self, own):
        ns, nd, ni = len(self.srcs), len(self.shapes), len(self.items)
        nsem = ni * (N_DEV - 1)
        self.own = own

        nq = 3 if own else 2

        def body(*refs):
            ins, land_refs = refs[:ns], refs[ns:ns + nd]
            sems, token = refs[ns + nd:ns + nd + nq], refs[-1]
            me, peer = _mesh_place()
            for mask in _PEER_MASKS:
                for k in range(ni):
                    self._copy(k, mask, ins, land_refs, sems[0], sems[1], me, peer, False).start()
            if own:
                for k in range(ni):
                    self._own_copy(k, ins, land_refs, sems[2], me).start()
            token[...] = jnp.zeros_like(token)

        hbm = pl.BlockSpec(memory_space=pltpu.HBM)
        sem = pl.BlockSpec(memory_space=pltpu.SEMAPHORE)
        lands = [lax.empty(s.shape, s.dtype) for s in self.shapes]
        sem_shapes = [pltpu.SemaphoreType.DMA((nsem,)), pltpu.SemaphoreType.DMA((nsem,)), pltpu.SemaphoreType.DMA((ni,))]
        res = pl.pallas_call(
            body, in_specs=[hbm] * (ns + nd),
            out_specs=[sem] * nq + [hbm] * nd + [pl.BlockSpec(memory_space=pltpu.VMEM)],
            out_shape=sem_shapes[:nq] + [pltpu.HBM(s.shape, s.dtype) for s in self.shapes]
            + [jax.ShapeDtypeStruct((8, 128), f32)],
            input_output_aliases={ns + j: nq + j for j in range(nd)},
            compiler_params=pltpu.CompilerParams(has_side_effects=pltpu.SideEffectType.DATAFLOW_SIDE_EFFECTING),
            name=self.name + "_start")(*self.srcs, *lands)
        self.token = res[-1]
        return list(res[:nq]), list(res[nq:-1])

    def finish(self, state, after):
        sems, lands = state
        nq = len(sems)
        after = list(after) if isinstance(after, (list, tuple)) else [after]
        ns, nd, ni = len(self.srcs), len(self.shapes), len(self.items)

        def body(*refs):
            ins, land_refs = refs[:ns], refs[ns:ns + nd]
            sem_refs = refs[ns + nd:ns + nd + nq]
            me, peer = _mesh_place()
            for mask in _PEER_MASKS:
                for k in range(ni):
                    cp = self._copy(k, mask, ins, land_refs, sem_refs[0], sem_refs[1], me, peer, True)
                    cp.wait_send()
                    cp.wait_recv()
            if self.own:
                for k in range(ni):
                    self._own_copy(k, ins, land_refs, sem_refs[2], me).wait()

        hbm = pl.BlockSpec(memory_space=pltpu.HBM)
        sem = pl.BlockSpec(memory_space=pltpu.SEMAPHORE)
        res = pl.pallas_call(
            body, in_specs=[hbm] * (ns + nd) + [sem] * nq + [pl.BlockSpec(memory_space=pl.ANY)] * len(after),
            out_specs=[hbm] * nd, out_shape=[pltpu.HBM(s.shape, s.dtype) for s in self.shapes],
            input_output_aliases={ns + j: j for j in range(nd)},
            compiler_params=pltpu.CompilerParams(has_side_effects=pltpu.SideEffectType.DATAFLOW_SIDE_EFFECTING),
            name=self.name + "_wait")(*self.srcs, *lands, *sems, *after)
        return {k: res[i] for k, i in self.where.items()}


def _after(x, *tokens, name):
    def body(*refs):
        del refs

    anyspace = pl.BlockSpec(memory_space=pl.ANY)
    return pl.pallas_call(body, in_specs=[anyspace] * (1 + len(tokens)), out_specs=anyspace,
                          out_shape=jax.ShapeDtypeStruct(x.shape, x.dtype), input_output_aliases={0: 0},
                          name=name)(x, *tokens)


def _rows_of(n):
    return lambda r, i: r.at[pl.ds(pl.multiple_of(i * n, n), n), :]


def _cols_of(n):
    return lambda r, i: r.at[:, pl.ds(pl.multiple_of(i * n, n), n)]


def _whole(r, i):
    return r


def _slot(r, i):
    return r.at[i]


def _at_layer(layer):
    return lambda r, i: r.at[layer]


def _slot_layer(layer):
    return lambda r, i: r.at[i, layer]


def _sum_adam(me_index, slots, own, own_block, w, m, v, *, name):
    rows, cols = w.shape
    tr = _pick(rows, (256, 128, 64, 32, 16, 8))
    bc1 = 1.0 - ADAM_B1 ** ADAM_STEP
    bc2 = 1.0 - ADAM_B2 ** ADAM_STEP

    own_shape, own_map = own_block(tr)

    def body(me_ref, s_ref, own_ref, w_ref, m_ref, v_ref, g_ref, d_ref, nm_ref, nv_ref):
        mine = own_ref[0] if len(own_shape) == 3 else own_ref[...]
        me = me_ref[0]
        g = jnp.where(me == 0, mine, s_ref[0])
        for k in range(1, N_DEV):
            g = g + jnp.where(me == k, mine, s_ref[k])
        mm = ADAM_B1 * m_ref[...] + (1.0 - ADAM_B1) * g
        vv = ADAM_B2 * v_ref[...] + (1.0 - ADAM_B2) * (g * g)
        g_ref[...] = g
        nm_ref[...] = mm
        nv_ref[...] = vv
        d_ref[...] = -ADAM_LR * ((mm / bc1) / (jnp.sqrt(vv / bc2) + ADAM_EPS) + ADAM_WD * w_ref[...])

    blk = pl.BlockSpec((tr, cols), lambda i, me: (i, 0))
    sds = jax.ShapeDtypeStruct((rows, cols), f32)
    grid_spec = pltpu.PrefetchScalarGridSpec(
        num_scalar_prefetch=1, grid=(rows // tr,),
        in_specs=[pl.BlockSpec((N_DEV, tr, cols), lambda i, me: (0, i, 0)), pl.BlockSpec(own_shape, own_map),
                  blk, blk, blk],
        out_specs=[blk] * 4)
    return pl.pallas_call(body, grid_spec=grid_spec, out_shape=[sds] * 4, compiler_params=_cparams(("parallel",)),
                          name=name)(me_index, slots, own, w, m, v)


_SHARDED = dict(xa_wq=1, xa_wk=1, xa_wv=1, xa_wo=1, ffn_w_up=2, ffn_conv_w=2, ffn_w_down=1, mix_w_in=2, mla_w_uq=2,
                mla_w_ukv=2, mix_w_out=1, s5_d=1, s5_w_glu_a=1, s5_w_glu_b=1)
_EXACT = ("ffn_conv_w", "s5_d")
_WEIGHTS = ("norm_mix", "norm_xa", "norm_mem", "norm_ffn", "xa_wq", "xa_wk", "xa_wv", "xa_wo", "xa_q_norm",
            "xa_k_norm", "ffn_w_up", "ffn_conv_w", "ffn_conv_b", "ffn_w_down", "hg_lb_logits", "mix_w_in",
            "hg_out_norm", "mla_q_a_norm", "mla_w_uq", "mla_kv_a_norm", "mla_w_ukv", "mla_qn_nope", "mla_qn_rope",
            "mla_kn_nope", "mla_kn_rope", "mix_w_out", "s5_lam_re", "s5_lam_im", "s5_log_dt", "s5_b_re", "s5_b_im",
            "s5_c_re", "s5_c_im", "s5_d", "s5_w_glu_a", "s5_w_glu_b")
_BIG = tuple(n for n in _WEIGHTS if n in _SHARDED and n not in _EXACT)
_SHARD_ORDER = tuple(n for n in _WEIGHTS if n in _SHARDED)
_REPL_ORDER = tuple(n for n in _WEIGHTS if n not in _SHARDED)
_REPL_EARLY = tuple(n for n in _REPL_ORDER if n.startswith("s5_"))
_REPL_LATE = tuple(n for n in _REPL_ORDER if n not in _REPL_EARLY)


def _pack(parts, dtype, lead=None):
    nl = 0 if lead is None else 1
    flat = [a.astype(dtype).reshape(a.shape[:nl] + (-1,)) for a in parts]
    cat = jnp.concatenate(flat, axis=nl)
    n = cat.shape[nl]
    unit = _LANES * _ROW_PAD
    total = -(-n // unit) * unit
    cat = jnp.pad(cat, [(0, 0)] * nl + [(0, total - n)])
    return cat.reshape(cat.shape[:nl] + (total // _LANES, _LANES))


def _unpack(packed, shapes, lead=None):
    nl = 0 if lead is None else 1
    flat = packed.reshape(packed.shape[:nl] + (-1,))
    out, off = [], 0
    for s in shapes:
        n = int(np.prod(s))
        piece = flat[..., off:off + n] if nl else flat[off:off + n]
        out.append(piece.reshape(packed.shape[:nl] + tuple(s)))
        off += n
    return out


def _to_full(gathered, axis):
    g = jnp.moveaxis(gathered, 0, axis)
    s = g.shape
    return g.reshape(s[:axis] + (s[axis] * s[axis + 1],) + s[axis + 2:])


def _to_shards(full, axis):
    s = full.shape
    g = full.reshape(s[:axis] + (N_DEV, s[axis] // N_DEV) + s[axis + 1:])
    return jnp.moveaxis(g, axis, 0)


_DIRECT_ROWS = ("xa_wq", "xa_wk", "xa_wv", "xa_wo", "mix_w_out", "s5_w_glu_a", "s5_w_glu_b")
_SMALL16 = ("mix_w_in", "mla_w_uq", "mla_w_ukv")
_SMALL_SHARDED = _SMALL16 + _EXACT
_SHARD_ROWS = 128


def _exchange_layout(d):
    out = dict(d)
    out["ffn_w_up"] = _pad_groups(d["ffn_w_up"], 2)
    out["ffn_conv_w"] = _pad_groups(d["ffn_conv_w"], 2)
    out["ffn_w_down"] = _pad_groups(d["ffn_w_down"], 1)
    return out


def _train_step(x, mem, positions, target, w, m, v):
    d_model = x.shape[1]
    we_, me_, ve_ = _exchange_layout(w), _exchange_layout(m), _exchange_layout(v)
    sds = jax.ShapeDtypeStruct

    matrices = _DIRECT_ROWS + ("ffn_w_up", "ffn_w_down")
    layer_mats = ("xa_wq", "xa_wk", "xa_wv", "xa_wo", "ffn_w_up", "ffn_w_down")
    shard16 = {n: we_[n].astype(bf16) for n in matrices}
    part_of = {n: _rows_of(_SHARD_ROWS) for n in _DIRECT_ROWS}
    part_of["ffn_w_up"] = _cols_of(we_["ffn_w_up"].shape[2])
    part_of["ffn_w_down"] = _rows_of(we_["ffn_w_down"].shape[1])
    part_shape = {n: we_[n].shape[1:] for n in matrices}

    def full_shape(n):
        r, c = part_shape[n]
        return (r, N_DEV * c) if n == "ffn_w_up" else (N_DEV * r, c)

    def gather(ex, n, layer):
        ex.add(shard16[n], sds(full_shape(n), bf16), _at_layer(layer), part_of[n], (n, layer))

    def scatter(ex, n, layer, grad):
        ex.add(grad, sds((N_DEV,) + part_shape[n], f32), part_of[n], _slot, (n, layer))

    small16 = _pack([we_[n] for n in _SMALL16], bf16)
    exact = _pack([we_[n] for n in _EXACT], f32)
    ga, gb, gc = _Exchange("gather_a"), _Exchange("gather_b"), _Exchange("gather_c")
    ga.add(small16, sds((N_DEV,) + small16.shape, bf16), _whole, _slot, "small16")
    ga.add(exact, sds((N_DEV,) + exact.shape, f32), _whole, _slot, "exact")
    gather(ga, "mix_w_out", 0)
    for n in layer_mats:
        gather(gb, n, 0)
    gather(gc, "s5_w_glu_a", 0)
    gather(gc, "s5_w_glu_b", 0)
    for n in layer_mats:
        gather(gc, n, 1)
    state_a, state_b, state_c = ga.begin(True), gb.begin(True), gc.begin(True)

    full = ga.finish(state_a, [gb.token, gc.token])
    p = {n: w[n] for n in _REPL_ORDER}
    for n, a in zip(_SMALL16, _unpack(full["small16"], [we_[n].shape for n in _SMALL16], lead=True)):
        p[n] = _to_full(a, _SHARDED[n])
    conv_w, p["s5_d"] = [_to_full(a, _SHARDED[n]) for n, a in
                         zip(_EXACT, _unpack(full["exact"], [we_[n].shape for n in _EXACT], lead=True))]
    p["mix_w_out"] = full[("mix_w_out", 0)][None]
    cos, sin = _rope_tables(positions)
    we = _even_weights(p, 0, 0, bf16)
    we["norm_mix"] = _after(we["norm_mix"], ga.token, gb.token, gc.token, name="after_gather_starts")
    conv_b = _pad_groups(w["ffn_conv_b"], 1)

    def layer_weights(layer):
        return dict(norm_xa=w["norm_xa"][layer][None], norm_mem=w["norm_mem"][layer][None],
                    norm_ffn=w["norm_ffn"][layer][None], xa_q_norm=w["xa_q_norm"][layer][None],
                    xa_k_norm=w["xa_k_norm"][layer][None], ffn_conv_w=conv_w[layer],
                    ffn_conv_b=conv_b[layer][None], **{n: full[(n, layer)] for n in layer_mats})

    h, s_mix0 = _mixer_fwd(x, cos, sin, we, "l0")
    full.update(gb.finish(state_b, h))
    wl = [layer_weights(0)]
    h, s_xa0 = _xattn_fwd(h, mem, wl[0], "l0")
    h, s_ff0 = _ffn_fwd(h, wl[0], "l0")
    full.update(gc.finish(state_c, h))
    wl.append(layer_weights(1))
    p["s5_w_glu_a"], p["s5_w_glu_b"] = full[("s5_w_glu_a", 0)][None], full[("s5_w_glu_b", 0)][None]
    wo = _odd_weights(p, 0, 1, bf16)
    h, s_mix1 = _s5_fwd(h, wo, "l1")
    h, s_xa1 = _xattn_fwd(h, mem, wl[1], "l1")
    h, s_ff1 = _ffn_fwd(h, wl[1], "l1")
    dh, loss = _rows(_loss_fn, [h, target], [], [(h.shape[1], f32)], accs=[(1, 1)], name="loss_head")

    gl = [{}, {}]
    reduces = []

    own_grad = {}

    def reduce_start(name, entries, dh):
        ex = _Exchange(name)
        for n, layer, grad in entries.get("matrices", ()):
            scatter(ex, n, layer, grad)
            own_grad[(n, layer)] = grad
        for key, src, shape, src_at in entries.get("packs", ()):
            ex.add(src, shape, src_at, _slot, key)
        reduces.append((ex, ex.begin(False)))
        return _after(dh, ex.token, name="after_" + name)

    dh, gl[1] = _ffn_bwd(dh, wl[1], s_ff1, "l1")
    dh = reduce_start("reduce_ffn1", dict(matrices=[(n, 1, gl[1][n]) for n in ("ffn_w_up", "ffn_w_down")]), dh)
    dh, g = _xattn_bwd(dh, mem, wl[1], s_xa1, "l1")
    gl[1].update(g)
    dh = reduce_start("reduce_xa1", dict(matrices=[(n, 1, g[n]) for n in ("xa_wq", "xa_wk", "xa_wv", "xa_wo")]), dh)
    dh, g_odd = _s5_bwd(dh, wo, s_mix1, "l1")
    go = _odd_grads(g_odd)
    dh, gl[0] = _ffn_bwd(dh, wl[0], s_ff0, "l0")
    send_early = _pack([go[n].reshape(w[n].shape) for n in _REPL_EARLY], f32)
    dh = reduce_start("reduce_ffn0", dict(
        matrices=[(n, 0, g_odd[n]) for n in ("s5_w_glu_a", "s5_w_glu_b")]
        + [(n, 0, gl[0][n]) for n in ("ffn_w_up", "ffn_w_down")],
        packs=[("repl_early", send_early, sds((N_DEV,) + send_early.shape, f32), _whole)]), dh)
    dh, g = _xattn_bwd(dh, mem, wl[0], s_xa0, "l0")
    gl[0].update(g)
    dh = reduce_start("reduce_xa0", dict(matrices=[(n, 0, g[n]) for n in ("xa_wq", "xa_wk", "xa_wv", "xa_wo")]), dh)
    grad_x, g_even = _mixer_bwd(
        dh, cos, sin, we, s_mix0, "l0",
        on_w_out=lambda grad, dmixin: reduce_start("reduce_w_out", dict(matrices=[("mix_w_out", 0, grad)]), dmixin))

    ge = _even_grads(g_even)
    cat = lambda n: jnp.concatenate([gl[0][n], gl[1][n]], axis=0)
    rg = dict(ge)
    rg["norm_mix"] = jnp.concatenate([ge["norm_mix"], go["norm_mix"]], axis=0)
    for n in ("norm_xa", "norm_mem", "norm_ffn", "xa_q_norm", "xa_k_norm"):
        rg[n] = cat(n)
    rg["ffn_conv_b"] = _unpad_groups(cat("ffn_conv_b"), 1)
    sg = dict(mix_w_in=ge["mix_w_in"], mla_w_uq=ge["mla_w_uq"], mla_w_ukv=ge["mla_w_ukv"], s5_d=go["s5_d"],
              ffn_conv_w=jnp.stack([gl[0]["ffn_conv_w"], gl[1]["ffn_conv_w"]]))
    send_small = _pack([_to_shards(sg[n], _SHARDED[n]) for n in _SMALL_SHARDED], f32, lead=True)
    send_late = _pack([rg[n].reshape(w[n].shape) for n in _REPL_LATE], f32)
    last = _Exchange("reduce_last")
    last.add(send_small, sds(send_small.shape, f32), _slot, _slot, "small")
    last.add(send_late, sds((N_DEV,) + send_late.shape, f32), _whole, _slot, "repl_late")
    state_last = last.begin(False)
    slots = {}
    for ex, state in reduces:
        slots.update(ex.finish(state, [grad_x, last.token]))

    me_index = (4 * lax.axis_index("x") + 2 * lax.axis_index("y") + lax.axis_index("c")).astype(jnp.int32).reshape(1)

    def own_block(n):
        r, c = part_shape[n]
        if n == "ffn_w_up":
            return lambda tr: ((tr, c), lambda i, me: (i, me[0]))
        return lambda tr: ((tr, c), lambda i, me: (me[0] * (r // tr) + i, 0))

    out = [{}, {}, {}, {}]
    unpad = dict(ffn_w_up=2, ffn_conv_w=2, ffn_w_down=1)
    done = []
    for n in matrices:
        per_layer = [_sum_adam(me_index, slots[(n, layer)], own_grad[(n, layer)], own_block(n), we_[n][layer],
                               me_[n][layer], ve_[n][layer], name=f"adam_{n}_{layer}")
                     for layer in range(we_[n].shape[0])]
        done += [res[0] for res in per_layer]
        for k in range(4):
            r = jnp.stack([res[k] for res in per_layer])
            out[k][n] = _unpad_groups(r, unpad[n]) if n in unpad else r
    pk = lambda d, order: _pack([d[n] for n in order], f32)
    whole_rows = lambda tr: ((tr, _LANES), lambda i, me: (i, 0))
    res_early = _sum_adam(me_index, slots["repl_early"], send_early, whole_rows, pk(w, _REPL_EARLY), pk(m, _REPL_EARLY),
                          pk(v, _REPL_EARLY), name="adam_repl_early")
    slots = last.finish(state_last, done + [res_early[0]])
    res_small = _sum_adam(me_index, slots["small"], send_small,
                          lambda tr: ((1, tr, _LANES), lambda i, me: (me[0], i, 0)),
                          pk(we_, _SMALL_SHARDED), pk(me_, _SMALL_SHARDED), pk(ve_, _SMALL_SHARDED), name="adam_small")
    res_late = _sum_adam(me_index, slots["repl_late"], send_late, whole_rows, pk(w, _REPL_LATE), pk(m, _REPL_LATE),
                         pk(v, _REPL_LATE), name="adam_repl_late")
    for k in range(4):
        for n, a in zip(_SMALL_SHARDED, _unpack(res_small[k], [we_[n].shape for n in _SMALL_SHARDED])):
            out[k][n] = _unpad_groups(a, unpad[n]) if n in unpad else a
        out[k].update(zip(_REPL_EARLY, _unpack(res_early[k], [w[n].shape for n in _REPL_EARLY])))
        out[k].update(zip(_REPL_LATE, _unpack(res_late[k], [w[n].shape for n in _REPL_LATE])))
    return loss, grad_x, out


_INPUTS = tuple("""x, mem, positions, norm_mix, norm_xa, norm_mem, norm_ffn, xa_wq, xa_wk, xa_wv, xa_wo, xa_q_norm, xa_k_norm, ffn_w_up, ffn_conv_w, ffn_conv_b, ffn_w_down, hg_lb_logits, mix_w_in, hg_out_norm, mla_q_a_norm, mla_w_uq, mla_kv_a_norm, mla_w_ukv, mla_qn_nope, mla_qn_rope, mla_kn_nope, mla_kn_rope, mix_w_out, s5_lam_re, s5_lam_im, s5_log_dt, s5_b_re, s5_b_im, s5_c_re, s5_c_im, s5_d, s5_w_glu_a, s5_w_glu_b, loss_target, m_norm_mix, m_norm_xa, m_norm_mem, m_norm_ffn, m_xa_wq, m_xa_wk, m_xa_wv, m_xa_wo, m_xa_q_norm, m_xa_k_norm, m_ffn_w_up, m_ffn_conv_w, m_ffn_conv_b, m_ffn_w_down, m_hg_lb_logits, m_mix_w_in, m_hg_out_norm, m_mla_q_a_norm, m_mla_w_uq, m_mla_kv_a_norm, m_mla_w_ukv, m_mla_qn_nope, m_mla_qn_rope, m_mla_kn_nope, m_mla_kn_rope, m_mix_w_out, m_s5_lam_re, m_s5_lam_im, m_s5_log_dt, m_s5_b_re, m_s5_b_im, m_s5_c_re, m_s5_c_im, m_s5_d, m_s5_w_glu_a, m_s5_w_glu_b, v_norm_mix, v_norm_xa, v_norm_mem, v_norm_ffn, v_xa_wq, v_xa_wk, v_xa_wv, v_xa_wo, v_xa_q_norm, v_xa_k_norm, v_ffn_w_up, v_ffn_conv_w, v_ffn_conv_b, v_ffn_w_down, v_hg_lb_logits, v_mix_w_in, v_hg_out_norm, v_mla_q_a_norm, v_mla_w_uq, v_mla_kv_a_norm, v_mla_w_ukv, v_mla_qn_nope, v_mla_qn_rope, v_mla_kn_nope, v_mla_kn_rope, v_mix_w_out, v_s5_lam_re, v_s5_lam_im, v_s5_log_dt, v_s5_b_re, v_s5_b_im, v_s5_c_re, v_s5_c_im, v_s5_d, v_s5_w_glu_a, v_s5_w_glu_b""".replace(" ", "").split(","))


def kernel(x, mem, positions, norm_mix, norm_xa, norm_mem, norm_ffn, xa_wq, xa_wk, xa_wv, xa_wo, xa_q_norm, xa_k_norm, ffn_w_up, ffn_conv_w, ffn_conv_b, ffn_w_down, hg_lb_logits, mix_w_in, hg_out_norm, mla_q_a_norm, mla_w_uq, mla_kv_a_norm, mla_w_ukv, mla_qn_nope, mla_qn_rope, mla_kn_nope, mla_kn_rope, mix_w_out, s5_lam_re, s5_lam_im, s5_log_dt, s5_b_re, s5_b_im, s5_c_re, s5_c_im, s5_d, s5_w_glu_a, s5_w_glu_b, loss_target, m_norm_mix, m_norm_xa, m_norm_mem, m_norm_ffn, m_xa_wq, m_xa_wk, m_xa_wv, m_xa_wo, m_xa_q_norm, m_xa_k_norm, m_ffn_w_up, m_ffn_conv_w, m_ffn_conv_b, m_ffn_w_down, m_hg_lb_logits, m_mix_w_in, m_hg_out_norm, m_mla_q_a_norm, m_mla_w_uq, m_mla_kv_a_norm, m_mla_w_ukv, m_mla_qn_nope, m_mla_qn_rope, m_mla_kn_nope, m_mla_kn_rope, m_mix_w_out, m_s5_lam_re, m_s5_lam_im, m_s5_log_dt, m_s5_b_re, m_s5_b_im, m_s5_c_re, m_s5_c_im, m_s5_d, m_s5_w_glu_a, m_s5_w_glu_b, v_norm_mix, v_norm_xa, v_norm_mem, v_norm_ffn, v_xa_wq, v_xa_wk, v_xa_wv, v_xa_wo, v_xa_q_norm, v_xa_k_norm, v_ffn_w_up, v_ffn_conv_w, v_ffn_conv_b, v_ffn_w_down, v_hg_lb_logits, v_mix_w_in, v_hg_out_norm, v_mla_q_a_norm, v_mla_w_uq, v_mla_kv_a_norm, v_mla_w_ukv, v_mla_qn_nope, v_mla_qn_rope, v_mla_kn_nope, v_mla_kn_rope, v_mix_w_out, v_s5_lam_re, v_s5_lam_im, v_s5_log_dt, v_s5_b_re, v_s5_b_im, v_s5_c_re, v_s5_c_im, v_s5_d, v_s5_w_glu_a, v_s5_w_glu_b):
    vals = dict(zip(_INPUTS, (x, mem, positions, norm_mix, norm_xa, norm_mem, norm_ffn, xa_wq, xa_wk, xa_wv, xa_wo, xa_q_norm, xa_k_norm, ffn_w_up, ffn_conv_w, ffn_conv_b, ffn_w_down, hg_lb_logits, mix_w_in, hg_out_norm, mla_q_a_norm, mla_w_uq, mla_kv_a_norm, mla_w_ukv, mla_qn_nope, mla_qn_rope, mla_kn_nope, mla_kn_rope, mix_w_out, s5_lam_re, s5_lam_im, s5_log_dt, s5_b_re, s5_b_im, s5_c_re, s5_c_im, s5_d, s5_w_glu_a, s5_w_glu_b, loss_target, m_norm_mix, m_norm_xa, m_norm_mem, m_norm_ffn, m_xa_wq, m_xa_wk, m_xa_wv, m_xa_wo, m_xa_q_norm, m_xa_k_norm, m_ffn_w_up, m_ffn_conv_w, m_ffn_conv_b, m_ffn_w_down, m_hg_lb_logits, m_mix_w_in, m_hg_out_norm, m_mla_q_a_norm, m_mla_w_uq, m_mla_kv_a_norm, m_mla_w_ukv, m_mla_qn_nope, m_mla_qn_rope, m_mla_kn_nope, m_mla_kn_rope, m_mix_w_out, m_s5_lam_re, m_s5_lam_im, m_s5_log_dt, m_s5_b_re, m_s5_b_im, m_s5_c_re, m_s5_c_im, m_s5_d, m_s5_w_glu_a, m_s5_w_glu_b, v_norm_mix, v_norm_xa, v_norm_mem, v_norm_ffn, v_xa_wq, v_xa_wk, v_xa_wv, v_xa_wo, v_xa_q_norm, v_xa_k_norm, v_ffn_w_up, v_ffn_conv_w, v_ffn_conv_b, v_ffn_w_down, v_hg_lb_logits, v_mix_w_in, v_hg_out_norm, v_mla_q_a_norm, v_mla_w_uq, v_mla_kv_a_norm, v_mla_w_ukv, v_mla_qn_nope, v_mla_qn_rope, v_mla_kn_nope, v_mla_kn_rope, v_mix_w_out, v_s5_lam_re, v_s5_lam_im, v_s5_log_dt, v_s5_b_re, v_s5_b_im, v_s5_c_re, v_s5_c_im, v_s5_d, v_s5_w_glu_a, v_s5_w_glu_b)))
    w = {n: vals[n] for n in _WEIGHTS}
    m = {n: vals["m_" + n] for n in _WEIGHTS}
    v = {n: vals["v_" + n] for n in _WEIGHTS}
    loss, grad_x, res = _train_step(vals["x"][0], vals["mem"][0], vals["positions"][0], vals["loss_target"][0],
                                    w, m, v)
    loss = lax.psum(loss[0, 0], ("x", "y", "c"))
    return (loss, grad_x[None], *[r[n] for r in res for n in _WEIGHTS])
```

```python
import functools

import jax
import jax.numpy as jnp
import numpy as np
from jax import lax
from jax.experimental import pallas as pl
from jax.experimental.pallas import tpu as pltpu

f32 = jnp.float32
bf16 = jnp.bfloat16

EPS = 1e-6
N_DEV = 8
VMEM_LIMIT = 52 * 1024 * 1024

HG_HEADS = 4
HG_DIM = 128
HG_WIDTH = HG_HEADS * HG_DIM
HG_CHUNK = 64
HG_SUB = 16
MLA_HEADS = 4
MLA_Q_RANK = 256
MLA_KV_RANK = 128
MLA_NOPE = 128
MLA_ROPE = 64
MLA_V = 128
MLA_QK = MLA_NOPE + MLA_ROPE
MLA_QK_PAD = 256
ROPE_BASE = 10000.0
IN_WIDTH = 4 * HG_WIDTH + MLA_Q_RANK + MLA_KV_RANK + MLA_ROPE
IN_PAD = 2560
XA_HEADS = 4
XA_DIM = 256
S5_GROUP = 16
S5_GROUPS = 64
S5_STATE = 64
CONV_W = 3

ADAM_LR = 0.001
ADAM_B1 = 0.9
ADAM_B2 = 0.999
ADAM_EPS = 1e-08
ADAM_WD = 0.01
ADAM_STEP = 10

_NT = (((1,), (1,)), ((), ()))
_TN = (((0,), (0,)), ((), ()))
_NN = (((1,), (0,)), ((), ()))


def _pick(n, cands):
    for c in cands:
        if n % c == 0:
            return c
    return n


def _cparams(sem):
    return pltpu.CompilerParams(dimension_semantics=sem, vmem_limit_bytes=VMEM_LIMIT)


_MM_BUDGET = 36 * 1024 * 1024
_MM_TILES = ((1024, 1024), (1024, 512), (512, 1024), (512, 512), (512, 256), (256, 512), (256, 256), (256, 128),
             (128, 256), (128, 128))


def _mm(a, b, *, name, ta=False, tb=False, out_dtype=f32, add=None, b2=None, kslab=None):
    m, k = (a.shape[1], a.shape[0]) if ta else a.shape
    nb = b.shape[0] if tb else b.shape[1]
    n = nb * (2 if b2 is not None else 1)
    slab, nslab = kslab if kslab is not None else (0, 1)
    assert (b.shape[1] // nslab if tb else b.shape[0]) == k, (a.shape, b.shape, ta, tb)
    assert b2 is None or (not tb and b2.shape == b.shape)
    isz = lambda x: jnp.dtype(x.dtype).itemsize
    bm = bn = None
    for cm, cn in _MM_TILES:
        if m % cm or nb % cn:
            continue
        need = 2 * (cm * k * isz(a) + cn * k * isz(b) * (2 if b2 is not None else 1)
                    + cm * cn * (jnp.dtype(out_dtype).itemsize + (4 if add is not None else 0)))
        if need <= _MM_BUDGET:
            bm, bn = cm, cn
            break
    assert bm is not None, (name, a.shape, b.shape)
    half = nb // bn
    dims = (((0 if ta else 1,), (1 if tb else 0,)), ((), ()))

    def body(*refs):
        refs = list(refs)
        a_ref, b_ref = refs[0], refs[1]
        b2_ref = refs.pop(2) if b2 is not None else None
        add_ref = refs[2] if add is not None else None
        o_ref = refs[-1]

        def run(rhs_ref):
            r = lax.dot_general(a_ref[...].astype(bf16), rhs_ref[...].astype(bf16), dims, preferred_element_type=f32)
            if add_ref is not None:
                r = r + add_ref[...].astype(f32)
            o_ref[...] = r.astype(o_ref.dtype)

        if b2_ref is None:
            run(b_ref)
        else:
            pl.when(pl.program_id(1) < half)(lambda: run(b_ref))
            pl.when(pl.program_id(1) >= half)(lambda: run(b2_ref))

    a_spec = pl.BlockSpec((k, bm), lambda i, j: (0, i)) if ta else pl.BlockSpec((bm, k), lambda i, j: (i, 0))
    if tb:
        b_spec = pl.BlockSpec((bn, k), lambda i, j: (j, slab))
    elif b2 is None:
        b_spec = pl.BlockSpec((k, bn), lambda i, j: (0, j))
    else:
        b_spec = pl.BlockSpec((k, bn), lambda i, j: (0, jnp.minimum(j, half - 1)))
    in_specs = [a_spec, b_spec]
    args = [a, b]
    if b2 is not None:
        in_specs.append(pl.BlockSpec((k, bn), lambda i, j: (0, jnp.maximum(j - half, 0))))
        args.append(b2)
    if add is not None:
        in_specs.append(pl.BlockSpec((bm, bn), lambda i, j: (i, j)))
        args.append(add)
    return pl.pallas_call(
        body, grid=(m // bm, n // bn), in_specs=in_specs,
        out_specs=pl.BlockSpec((bm, bn), lambda i, j: (i, j)),
        out_shape=jax.ShapeDtypeStruct((m, n), out_dtype),
        compiler_params=_cparams(("parallel", "parallel")), name=name)(*args)


def _as_tuple(x):
    return tuple(x) if isinstance(x, (tuple, list)) else (x,)


def _full_spec(p):
    nd = p.ndim
    return pl.BlockSpec(p.shape, lambda i, _nd=nd: (0,) * _nd)


def _rows(fn, rows, params, outs, *, name, tile=256, accs=()):
    length = rows[0].shape[0]
    tile = min(tile, length)
    nr, npar, no = len(rows), len(params), len(outs)

    def body(*refs):
        r, p, o = refs[:nr], refs[nr:nr + npar], refs[nr + npar:]
        res = _as_tuple(fn(*[x[...].astype(f32) for x in r], *[x[...] for x in p]))
        for kk in range(no):
            o[kk][...] = res[kk].astype(o[kk].dtype)
        if accs:
            @pl.when(pl.program_id(0) == 0)
            def _():
                for kk in range(no, no + len(accs)):
                    o[kk][...] = jnp.zeros_like(o[kk])
            for kk in range(no, no + len(accs)):
                o[kk][...] += res[kk]

    in_specs = [pl.BlockSpec((tile, x.shape[1]), lambda i: (i, 0)) for x in rows] + [_full_spec(p) for p in params]
    out_specs = [pl.BlockSpec((tile, w), lambda i: (i, 0)) for w, _ in outs]
    out_shape = [jax.ShapeDtypeStruct((length, w), d) for w, d in outs]
    for s in accs:
        out_specs.append(pl.BlockSpec(s, lambda i, _nd=len(s): (0,) * _nd))
        out_shape.append(jax.ShapeDtypeStruct(s, f32))
    res = pl.pallas_call(body, grid=(length // tile,), in_specs=in_specs, out_specs=out_specs, out_shape=out_shape,
                         compiler_params=_cparams(("arbitrary",)), name=name)(*rows, *params)
    return res


def _rows_bwd(fn, rows, params, cts, *, name, rgrad, pgrad, tile=256, addends=None):
    addends = addends or {}
    length = rows[0].shape[0]
    tile = min(tile, length)
    nr, npar, nc = len(rows), len(params), len(cts)
    ridx = [i for i in range(nr) if rgrad[i] is not None]
    pidx = [i for i in range(npar) if pgrad[i]]
    aidx = sorted(addends)
    na = len(aidx)

    def body(*refs):
        r, p, c = refs[:nr], refs[nr:nr + npar], refs[nr + npar:nr + npar + nc]
        ad = refs[nr + npar + nc:nr + npar + nc + na]
        o = refs[nr + npar + nc + na:]
        rv = [x[...].astype(f32) for x in r]
        pv = [x[...] for x in p]
        cv = tuple(x[...].astype(f32) for x in c)

        def g(*d):
            rr, pp = list(rv), list(pv)
            for n_, i_ in enumerate(ridx):
                rr[i_] = d[n_]
            for n_, i_ in enumerate(pidx):
                pp[i_] = d[len(ridx) + n_]
            return _as_tuple(fn(*rr, *pp))

        _, vjp = jax.vjp(g, *[rv[i] for i in ridx], *[pv[i] for i in pidx])
        grads = vjp(cv)
        for n_, i_ in enumerate(ridx):
            val = grads[n_]
            if i_ in addends:
                val = val + ad[aidx.index(i_)][...].astype(f32)
            o[n_][...] = val.astype(o[n_].dtype)
        if pidx:
            @pl.when(pl.program_id(0) == 0)
            def _():
                for n_ in range(len(pidx)):
                    o[len(ridx) + n_][...] = jnp.zeros_like(o[len(ridx) + n_])
            for n_ in range(len(pidx)):
                o[len(ridx) + n_][...] += grads[len(ridx) + n_]

    row_spec = lambda x: pl.BlockSpec((tile, x.shape[1]), lambda i: (i, 0))
    in_specs = ([row_spec(x) for x in rows] + [_full_spec(p) for p in params] + [row_spec(x) for x in cts]
                + [row_spec(addends[i]) for i in aidx])
    out_specs = [row_spec(rows[i]) for i in ridx] + [_full_spec(params[i]) for i in pidx]
    out_shape = ([jax.ShapeDtypeStruct(rows[i].shape, rgrad[i]) for i in ridx]
                 + [jax.ShapeDtypeStruct(params[i].shape, f32) for i in pidx])
    res = pl.pallas_call(body, grid=(length // tile,), in_specs=in_specs, out_specs=out_specs, out_shape=out_shape,
                         compiler_params=_cparams(("arbitrary",)), name=name)(
        *rows, *params, *cts, *[addends[i] for i in aidx])
    return list(res[:len(ridx)]), list(res[len(ridx):])


def _rms(x, g):
    return x * lax.rsqrt(jnp.mean(x * x, axis=-1, keepdims=True) + EPS) * g


def _rms_twice(x, g):
    y = _rms(x, g)
    return y, y


def _silu(x):
    return x * jax.nn.sigmoid(x)


def _shift_down(x, s):
    rows = lax.broadcasted_iota(jnp.int32, x.shape, 0)
    return jnp.where(rows >= s, pltpu.roll(x, s, axis=0), 0.0)


def _shift_up(x, s):
    n = x.shape[0]
    rows = lax.broadcasted_iota(jnp.int32, x.shape, 0)
    return jnp.where(rows < n - s, pltpu.roll(x, n - s, axis=0), 0.0)


_CONV_COLS = 128


def _conv_gate_fwd(u, cw, cb, *, name):
    length, two_f = u.shape
    ff = two_f // 2
    nb = ff // _CONV_COLS

    def body(ug, uv, wg, wv, bg, bv, o):
        def conv(x_ref, w_ref, b_ref):
            x = x_ref[...].astype(f32)
            return (w_ref[2:3, :] * x + w_ref[1:2, :] * _shift_down(x, 1) + w_ref[0:1, :] * _shift_down(x, 2)
                    + b_ref[...])
        o[...] = (_silu(conv(ug, wg, bg)) * conv(uv, wv, bv)).astype(o.dtype)

    blk = lambda r, off: pl.BlockSpec((r, _CONV_COLS), lambda j, _o=off: (0, j + _o))
    return pl.pallas_call(
        body, grid=(nb,),
        in_specs=[blk(length, 0), blk(length, nb), blk(CONV_W, 0), blk(CONV_W, nb), blk(1, 0), blk(1, nb)],
        out_specs=blk(length, 0), out_shape=jax.ShapeDtypeStruct((length, ff), bf16),
        compiler_params=_cparams(("parallel",)), name=name)(u, u, cw, cw, cb, cb)


def _conv_gate_bwd(u, cw, cb, da, *, name):
    length, two_f = u.shape
    ff = two_f // 2
    nb = ff // _CONV_COLS

    def body(ug, uv, wg, wv, bg, bv, da_ref, dug, duv, dwg, dwv, dbg, dbv):
        def conv(x, w_ref, b_ref):
            x1, x2 = _shift_down(x, 1), _shift_down(x, 2)
            return w_ref[2:3, :] * x + w_ref[1:2, :] * x1 + w_ref[0:1, :] * x2 + b_ref[...], x1, x2

        xg, xv = ug[...].astype(f32), uv[...].astype(f32)
        g, xg1, xg2 = conv(xg, wg, bg)
        v, xv1, xv2 = conv(xv, wv, bv)
        d = da_ref[...].astype(f32)
        sg = jax.nn.sigmoid(g)
        dg = d * v * (sg * (1.0 + g * (1.0 - sg)))
        dv = d * (g * sg)

        def back(dy, x, x1, x2, w_ref, du_ref, dw_ref, db_ref):
            du_ref[...] = (w_ref[2:3, :] * dy + w_ref[1:2, :] * _shift_up(dy, 1)
                           + w_ref[0:1, :] * _shift_up(dy, 2)).astype(du_ref.dtype)
            dw_ref[2:3, :] = jnp.sum(dy * x, axis=0, keepdims=True)
            dw_ref[1:2, :] = jnp.sum(dy * x1, axis=0, keepdims=True)
            dw_ref[0:1, :] = jnp.sum(dy * x2, axis=0, keepdims=True)
            db_ref[...] = jnp.sum(dy, axis=0, keepdims=True)

        back(dg, xg, xg1, xg2, wg, dug, dwg, dbg)
        back(dv, xv, xv1, xv2, wv, duv, dwv, dbv)

    blk = lambda r, off: pl.BlockSpec((r, _CONV_COLS), lambda j, _o=off: (0, j + _o))
    sds = jax.ShapeDtypeStruct
    dug, duv, dwg, dwv, dbg, dbv = pl.pallas_call(
        body, grid=(nb,),
        in_specs=[blk(length, 0), blk(length, nb), blk(CONV_W, 0), blk(CONV_W, nb), blk(1, 0), blk(1, nb),
                  blk(length, 0)],
        out_specs=[blk(length, 0), blk(length, 0), blk(CONV_W, 0), blk(CONV_W, 0), blk(1, 0), blk(1, 0)],
        out_shape=[sds((length, ff), bf16), sds((length, ff), bf16), sds((CONV_W, ff), f32), sds((CONV_W, ff), f32),
                   sds((1, ff), f32), sds((1, ff), f32)],
        compiler_params=_cparams(("parallel",)), name=name)(u, u, cw, cw, cb, cb, da)
    return dug, duv, jnp.concatenate([dwg, dwv], axis=1), jnp.concatenate([dbg, dbv], axis=1)


def _ffn_fwd(h, w, tag):
    hf, = _rows(_rms, [h], [w["norm_ffn"]], [(h.shape[1], bf16)], name=f"ffn_norm_{tag}")
    u = _mm(hf, w["ffn_w_up"], out_dtype=bf16, name=f"ffn_up_{tag}")
    a = _conv_gate_fwd(u, w["ffn_conv_w"], w["ffn_conv_b"], name=f"ffn_conv_{tag}")
    out = _mm(a, w["ffn_w_down"], add=h, name=f"ffn_down_{tag}")
    return out, (h, hf, u, a)


def _ffn_bwd(dout, w, saved, tag):
    h, hf, u, a = saved
    ff = a.shape[1]
    da = _mm(dout, w["ffn_w_down"], tb=True, out_dtype=bf16, name=f"ffn_da_{tag}")
    g = {"ffn_w_down": _mm(a, dout, ta=True, name=f"ffn_dwdown_{tag}")}
    dug, duv, g["ffn_conv_w"], g["ffn_conv_b"] = _conv_gate_bwd(u, w["ffn_conv_w"], w["ffn_conv_b"], da,
                                                                name=f"ffn_dconv_{tag}")
    dhf = _mm(dug, w["ffn_w_up"], tb=True, kslab=(0, 2), name=f"ffn_dhf_g_{tag}")
    dhf = _mm(duv, w["ffn_w_up"], tb=True, kslab=(1, 2), add=dhf, out_dtype=bf16, name=f"ffn_dhf_v_{tag}")
    g["ffn_w_up"] = _mm(hf, dug, ta=True, b2=duv, name=f"ffn_dwup_{tag}")
    (dh,), (g["norm_ffn"],) = _rows_bwd(_rms, [h], [w["norm_ffn"]], [dhf], rgrad=[f32], pgrad=[True],
                                        addends={0: dout}, name=f"ffn_dnorm_{tag}")
    return dh, g


def _xattn_fn(qx, kx, vx, qg, kg):
    outs = []
    for hh in range(XA_HEADS):
        sl = slice(hh * XA_DIM, (hh + 1) * XA_DIM)
        q = _rms(qx[:, sl], qg).astype(bf16)
        k = _rms(kx[:, sl], kg).astype(bf16)
        s = lax.dot_general(q, k, _NT, preferred_element_type=f32) * (XA_DIM ** -0.5)
        s = s - jnp.max(s, axis=-1, keepdims=True)
        p = jnp.exp(s)
        p = p / jnp.sum(p, axis=-1, keepdims=True)
        outs.append(jnp.dot(p.astype(bf16), vx[:, sl].astype(bf16), preferred_element_type=f32))
    return jnp.concatenate(outs, axis=-1)


def _xattn_fwd(h, mem, w, tag):
    d = h.shape[1]
    hx, = _rows(_rms, [h], [w["norm_xa"]], [(d, bf16)], name=f"xa_norm_{tag}")
    qx = _mm(hx, w["xa_wq"], name=f"xa_q_{tag}")
    m, = _rows(_rms, [mem], [w["norm_mem"]], [(d, bf16)], name=f"xa_mnorm_{tag}")
    kx = _mm(m, w["xa_wk"], name=f"xa_k_{tag}")
    vx = _mm(m, w["xa_wv"], name=f"xa_v_{tag}")
    o, = _rows(_xattn_fn, [qx], [kx, vx, w["xa_q_norm"], w["xa_k_norm"]], [(d, bf16)], tile=512,
               name=f"xa_attn_{tag}")
    out = _mm(o, w["xa_wo"], add=h, name=f"xa_o_{tag}")
    return out, (h, hx, qx, m, kx, vx, o)


def _xattn_bwd(dout, mem, w, saved, tag):
    h, hx, qx, m, kx, vx, o = saved
    g = {}
    do = _mm(dout, w["xa_wo"], tb=True, out_dtype=bf16, name=f"xa_do_{tag}")
    g["xa_wo"] = _mm(o, dout, ta=True, name=f"xa_dwo_{tag}")
    (dqx,), (dkx, dvx, g["xa_q_norm"], g["xa_k_norm"]) = _rows_bwd(
        _xattn_fn, [qx], [kx, vx, w["xa_q_norm"], w["xa_k_norm"]], [do], rgrad=[bf16], pgrad=[True] * 4,
        tile=512, name=f"xa_dattn_{tag}")
    dhx = _mm(dqx, w["xa_wq"], tb=True, out_dtype=bf16, name=f"xa_dhx_{tag}")
    g["xa_wq"] = _mm(hx, dqx, ta=True, name=f"xa_dwq_{tag}")
    (dh,), (g["norm_xa"],) = _rows_bwd(_rms, [h], [w["norm_xa"]], [dhx], rgrad=[f32], pgrad=[True],
                                       addends={0: dout}, name=f"xa_dnorm_{tag}")
    dm = _mm(dkx, w["xa_wk"], tb=True, name=f"xa_dm_k_{tag}")
    dm = _mm(dvx, w["xa_wv"], tb=True, add=dm, name=f"xa_dm_v_{tag}")
    g["xa_wk"] = _mm(m, dkx, ta=True, name=f"xa_dwk_{tag}")
    g["xa_wv"] = _mm(m, dvx, ta=True, name=f"xa_dwv_{tag}")
    _, (g["norm_mem"],) = _rows_bwd(_rms, [mem], [w["norm_mem"]], [dm], rgrad=[None], pgrad=[True],
                                    name=f"xa_dmnorm_{tag}")
    return dh, g


_HG_GROUP = 4


def _hg_chunk(q, k, v, g, *sts):
    c = q.shape[0]
    heads = [slice(h * HG_DIM, (h + 1) * HG_DIM) for h in range(len(sts))]
    tri = (lax.broadcasted_iota(jnp.int32, (c, c), 0) >= lax.broadcasted_iota(jnp.int32, (c, c), 1)).astype(f32)
    b = jnp.dot(tri, g, precision=lax.Precision.HIGHEST, preferred_element_type=f32)
    bend = jnp.sum(g, axis=0, keepdims=True)
    qe = (q * jnp.exp(b)).astype(bf16)
    kd = (k * jnp.exp(bend - b)).astype(bf16)
    vb = v.astype(bf16)
    decay = jnp.exp(bend)
    o_inter = [lax.dot_general(qe[:, hs], st.astype(bf16), _NT, preferred_element_type=f32) for hs, st in zip(heads, sts)]
    new = [st * decay[:, hs] + lax.dot_general(vb[:, hs], kd[:, hs], _TN, preferred_element_type=f32)
           for hs, st in zip(heads, sts)]
    outs = []
    for i in range(c // HG_SUB):
        lo, n = HG_SUB * i, HG_SUB * (i + 1)
        ref = jnp.sum(g[:lo], axis=0, keepdims=True) if i else jnp.zeros((1, g.shape[1]), f32)
        qh = (q[lo:n] * jnp.exp(b[lo:n] - ref)).astype(bf16)
        kh = (k[:n] * jnp.exp(ref - b[:n])).astype(bf16)
        keep = (lax.broadcasted_iota(jnp.int32, (HG_SUB, n), 1)
                <= lo + lax.broadcasted_iota(jnp.int32, (HG_SUB, n), 0))
        scores = [lax.dot_general(qh[:, hs], kh[:, hs], _NT, preferred_element_type=f32) for hs in heads]
        scores = [jnp.where(keep, a, 0.0).astype(bf16) for a in scores]
        outs.append(jnp.concatenate([jnp.dot(a, vb[:n, hs], preferred_element_type=f32)
                                     for a, hs in zip(scores, heads)], axis=1))
    return (jnp.concatenate(outs, axis=0) + jnp.concatenate(o_inter, axis=1), *new)


def _hg_fwd(q, k, v, g, *, name):
    length = q.shape[0]
    rows = _HG_GROUP * HG_CHUNK
    ng = length // rows
    nc = length // HG_CHUNK

    def body(q_ref, k_ref, v_ref, g_ref, o_ref, st_ref, state):
        @pl.when(pl.program_id(0) == 0)
        def _():
            state[...] = jnp.zeros_like(state)

        states = [state[h] for h in range(HG_HEADS)]
        for ci in range(_HG_GROUP):
            sl = slice(ci * HG_CHUNK, (ci + 1) * HG_CHUNK)
            for h in range(HG_HEADS):
                st_ref[h, ci] = states[h]
            o, *states = _hg_chunk(q_ref[sl, :], k_ref[sl, :], v_ref[sl, :], g_ref[sl, :], *states)
            o_ref[sl, :] = o
        for h in range(HG_HEADS):
            state[h] = states[h]

    blk = pl.BlockSpec((rows, HG_WIDTH), lambda c: (c, 0))
    return pl.pallas_call(
        body, grid=(ng,), in_specs=[blk] * 4,
        out_specs=[blk, pl.BlockSpec((HG_HEADS, _HG_GROUP, HG_DIM, HG_DIM), lambda c: (0, c, 0, 0))],
        out_shape=[jax.ShapeDtypeStruct((length, HG_WIDTH), f32),
                   jax.ShapeDtypeStruct((HG_HEADS, nc, HG_DIM, HG_DIM), f32)],
        scratch_shapes=[pltpu.VMEM((HG_HEADS, HG_DIM, HG_DIM), f32)],
        compiler_params=_cparams(("arbitrary",)), name=name)(q, k, v, g)


def _hg_bwd(q, k, v, g, states, do, *, name):
    length = q.shape[0]
    rows = _HG_GROUP * HG_CHUNK
    ng = length // rows

    def body(q_ref, k_ref, v_ref, g_ref, st_ref, do_ref, dq_ref, dk_ref, dv_ref, dg_ref, dstate):
        @pl.when(pl.program_id(0) == 0)
        def _():
            dstate[...] = jnp.zeros_like(dstate)

        dstates = [dstate[h] for h in range(HG_HEADS)]
        for ci in reversed(range(_HG_GROUP)):
            sl = slice(ci * HG_CHUNK, (ci + 1) * HG_CHUNK)
            _, vjp = jax.vjp(_hg_chunk, q_ref[sl, :], k_ref[sl, :], v_ref[sl, :], g_ref[sl, :],
                             *[st_ref[h, ci] for h in range(HG_HEADS)])
            dq, dk, dv, dg, *dstates = vjp((do_ref[sl, :], *dstates))
            dq_ref[sl, :] = dq
            dk_ref[sl, :] = dk
            dv_ref[sl, :] = dv
            dg_ref[sl, :] = dg
        for h in range(HG_HEADS):
            dstate[h] = dstates[h]

    blk = pl.BlockSpec((rows, HG_WIDTH), lambda c: (ng - 1 - c, 0))
    sds = jax.ShapeDtypeStruct((length, HG_WIDTH), f32)
    return pl.pallas_call(
        body, grid=(ng,),
        in_specs=[blk] * 4 + [pl.BlockSpec((HG_HEADS, _HG_GROUP, HG_DIM, HG_DIM), lambda c: (0, ng - 1 - c, 0, 0)), blk],
        out_specs=[blk] * 4, out_shape=[sds] * 4,
        scratch_shapes=[pltpu.VMEM((HG_HEADS, HG_DIM, HG_DIM), f32)],
        compiler_params=_cparams(("arbitrary",)), name=name)(q, k, v, g, states, do)


_ATT_BLK = 256
_ATT_SCALE = MLA_QK ** -0.5
_NEG = -1e30


def _att_mask(i, j, t):
    rows = i * t + lax.broadcasted_iota(jnp.int32, (t, t), 0)
    cols = j * t + lax.broadcasted_iota(jnp.int32, (t, t), 1)
    return cols <= rows


def _att_fwd(q, k, v, *, name):
    length = q.shape[0]
    t = min(_ATT_BLK, length)
    nq = length // t
    qw, vw = MLA_QK_PAD, MLA_V
    heads = range(MLA_HEADS)

    def body(q_ref, k_ref, v_ref, o_ref, lse_ref):
        i = pl.program_id(0)
        qbs = [q_ref[:, h * qw:(h + 1) * qw] for h in heads]

        def step(j, carry, diagonal=False):
            off = pl.multiple_of(j * t, t)
            out = []
            for h in heads:
                m, l, acc = carry[h]
                ks = k_ref[pl.ds(off, t), h * qw:(h + 1) * qw]
                vs = v_ref[pl.ds(off, t), h * vw:(h + 1) * vw]
                s = lax.dot_general(qbs[h], ks, _NT, preferred_element_type=f32) * _ATT_SCALE
                if diagonal:
                    s = jnp.where(_att_mask(i, j, t), s, _NEG)
                m_new = jnp.maximum(m, jnp.max(s, axis=-1, keepdims=True))
                alpha = jnp.exp(m - m_new)
                p = jnp.exp(s - m_new)
                l = alpha * l + jnp.sum(p, axis=-1, keepdims=True)
                acc = alpha * acc + jnp.dot(p.astype(bf16), vs, preferred_element_type=f32)
                out.append((m_new, l, acc))
            return tuple(out)

        init = tuple((jnp.full((t, 1), _NEG, f32), jnp.zeros((t, 1), f32), jnp.zeros((t, vw), f32)) for _ in heads)
        res = step(i, lax.fori_loop(0, i, step, init), diagonal=True)
        for h in heads:
            m, l, acc = res[h]
            o_ref[:, h * vw:(h + 1) * vw] = (acc / l).astype(o_ref.dtype)
            lse_ref[:, h * vw:(h + 1) * vw] = jnp.broadcast_to(m + jnp.log(l), (t, vw))

    return pl.pallas_call(
        body, grid=(nq,),
        in_specs=[pl.BlockSpec((t, q.shape[1]), lambda i: (i, 0)), pl.BlockSpec(k.shape, lambda i: (0, 0)),
                  pl.BlockSpec(v.shape, lambda i: (0, 0))],
        out_specs=[pl.BlockSpec((t, v.shape[1]), lambda i: (i, 0))] * 2,
        out_shape=[jax.ShapeDtypeStruct(v.shape, bf16), jax.ShapeDtypeStruct(v.shape, f32)],
        compiler_params=_cparams(("arbitrary",)), name=name)(q, k, v)


def _att_bwd(q, k, v, o, lse, do, *, name):
    length = q.shape[0]
    t = min(_ATT_BLK, length)
    nq = length // t
    qw, vw = MLA_QK_PAD, MLA_V
    heads = range(MLA_HEADS)

    def dq_body(q_ref, k_ref, v_ref, o_ref, lse_ref, do_ref, dq_ref, delta_ref):
        i = pl.program_id(0)
        qbs = [q_ref[:, h * qw:(h + 1) * qw] for h in heads]
        dobs = [do_ref[:, h * vw:(h + 1) * vw] for h in heads]
        lses = [lse_ref[:, h * vw:h * vw + 1] for h in heads]
        deltas = [jnp.sum(dobs[h].astype(f32) * o_ref[:, h * vw:(h + 1) * vw].astype(f32), axis=-1, keepdims=True)
                  for h in heads]

        def step(j, dqs, diagonal=False):
            off = pl.multiple_of(j * t, t)
            out = []
            for h in heads:
                ks = k_ref[pl.ds(off, t), h * qw:(h + 1) * qw]
                vs = v_ref[pl.ds(off, t), h * vw:(h + 1) * vw]
                s = lax.dot_general(qbs[h], ks, _NT, preferred_element_type=f32) * _ATT_SCALE
                p = jnp.exp(s - lses[h])
                if diagonal:
                    p = jnp.where(_att_mask(i, j, t), p, 0.0)
                dp = lax.dot_general(dobs[h], vs, _NT, preferred_element_type=f32)
                ds = p * (dp - deltas[h]) * _ATT_SCALE
                out.append(dqs[h] + jnp.dot(ds.astype(bf16), ks, preferred_element_type=f32))
            return tuple(out)

        dqs = step(i, lax.fori_loop(0, i, step, tuple(jnp.zeros((t, qw), f32) for _ in heads)), diagonal=True)
        for h in heads:
            dq_ref[:, h * qw:(h + 1) * qw] = dqs[h].astype(dq_ref.dtype)
            delta_ref[:, h * vw:(h + 1) * vw] = jnp.broadcast_to(deltas[h], (t, vw))

    qblk = pl.BlockSpec((t, q.shape[1]), lambda i: (i, 0))
    vblk = pl.BlockSpec((t, v.shape[1]), lambda i: (i, 0))
    qfull = pl.BlockSpec(q.shape, lambda i: (0, 0))
    vfull = pl.BlockSpec(v.shape, lambda i: (0, 0))
    dq, delta = pl.pallas_call(
        dq_body, grid=(nq,), in_specs=[qblk, qfull, vfull, vblk, vblk, vblk], out_specs=[qblk, vblk],
        out_shape=[jax.ShapeDtypeStruct(q.shape, bf16), jax.ShapeDtypeStruct(lse.shape, f32)],
        compiler_params=_cparams(("arbitrary",)), name=name + "_dq")(q, k, v, o, lse, do)

    def dkv_body(k_ref, v_ref, q_ref, do_ref, lse_ref, delta_ref, dk_ref, dv_ref):
        j = pl.program_id(0)
        kbs = [k_ref[:, h * qw:(h + 1) * qw] for h in heads]
        vbs = [v_ref[:, h * vw:(h + 1) * vw] for h in heads]

        def step(i, carry, diagonal=False):
            off = pl.multiple_of(i * t, t)
            out = []
            for h in heads:
                dk, dv = carry[h]
                qs = q_ref[pl.ds(off, t), h * qw:(h + 1) * qw]
                dos = do_ref[pl.ds(off, t), h * vw:(h + 1) * vw]
                lse_i = lse_ref[pl.ds(off, t), h * vw:h * vw + 1]
                delta_i = delta_ref[pl.ds(off, t), h * vw:h * vw + 1]
                s = lax.dot_general(qs, kbs[h], _NT, preferred_element_type=f32) * _ATT_SCALE
                p = jnp.exp(s - lse_i)
                if diagonal:
                    p = jnp.where(_att_mask(i, j, t), p, 0.0)
                dv = dv + lax.dot_general(p.astype(bf16), dos, _TN, preferred_element_type=f32)
                dp = lax.dot_general(dos, vbs[h], _NT, preferred_element_type=f32)
                ds = p * (dp - delta_i) * _ATT_SCALE
                dk = dk + lax.dot_general(ds.astype(bf16), qs, _TN, preferred_element_type=f32)
                out.append((dk, dv))
            return tuple(out)

        first = step(j, tuple((jnp.zeros((t, qw), f32), jnp.zeros((t, vw), f32)) for _ in heads), diagonal=True)
        res = lax.fori_loop(j + 1, nq, step, first)
        for h in heads:
            dk_ref[:, h * qw:(h + 1) * qw] = res[h][0].astype(dk_ref.dtype)
            dv_ref[:, h * vw:(h + 1) * vw] = res[h][1].astype(dv_ref.dtype)

    dk, dv = pl.pallas_call(
        dkv_body, grid=(nq,), in_specs=[qblk, vblk, qfull, vfull, vfull, vfull], out_specs=[qblk, vblk],
        out_shape=[jax.ShapeDtypeStruct(k.shape, bf16), jax.ShapeDtypeStruct(v.shape, bf16)],
        compiler_params=_cparams(("arbitrary",)), name=name + "_dkv")(k, v, q, do, lse, delta)
    return dq, dk, dv


_C_Q = 4 * HG_WIDTH
_C_KV = _C_Q + MLA_Q_RANK
_C_KPE = _C_KV + MLA_KV_RANK


def _rms_n(x, g, n):
    return x * lax.rsqrt(jnp.sum(x * x, axis=-1, keepdims=True) * (1.0 / n) + EPS) * g


def _mix_a(proj, l0, l1, q_a_norm, kv_a_norm):
    lb = jax.nn.sigmoid(l0 - l1)
    f = lb + (1.0 - lb) * jax.nn.sigmoid(proj[:, HG_WIDTH:2 * HG_WIDTH])
    qf = _silu(proj[:, :HG_WIDTH])
    v = proj[:, 2 * HG_WIDTH:3 * HG_WIDTH]
    cqn = _rms(proj[:, _C_Q:_C_KV], q_a_norm)
    ckvn = _rms(proj[:, _C_KV:_C_KPE], kv_a_norm)
    return qf, 1.0 - f, v, jnp.log(f), cqn, ckvn


def _mix_b(qraw, kvraw, proj, cos, sin, qn_nope, qn_rope, kn_nope, kn_rope, perm):
    def rope(x):
        return x * cos + jnp.dot(x, perm, precision=lax.Precision.HIGHEST, preferred_element_type=f32) * sin

    kpe = rope(_rms_n(proj[:, _C_KPE:], kn_rope, MLA_ROPE))
    qs, ks, vs = [], [], []
    for hh in range(MLA_HEADS):
        base = hh * MLA_QK_PAD
        qs.append(_rms(qraw[:, base:base + MLA_NOPE], qn_nope))
        qs.append(rope(_rms_n(qraw[:, base + MLA_NOPE:base + MLA_QK_PAD], qn_rope, MLA_ROPE)))
        ks.append(_rms(kvraw[:, base:base + MLA_NOPE], kn_nope))
        ks.append(kpe)
        vs.append(kvraw[:, base + MLA_NOPE:base + MLA_QK_PAD])
    return jnp.concatenate(qs, axis=-1), jnp.concatenate(ks, axis=-1), jnp.concatenate(vs, axis=-1)


def _mix_c(o_hg, proj, o_mla, hg_out_norm):
    parts = []
    for hh in range(HG_HEADS):
        sl = slice(hh * HG_DIM, (hh + 1) * HG_DIM)
        parts.append(_rms(o_hg[:, sl], hg_out_norm[:, sl]))
    o = jnp.concatenate(parts, axis=-1) * _silu(proj[:, 3 * HG_WIDTH:4 * HG_WIDTH])
    return jnp.concatenate([o, o_mla], axis=-1)


def _rope_perm():
    p = np.zeros((128, 128), np.float32)
    half = MLA_ROPE // 2
    for i in range(half):
        p[i + half, i] = -1.0
        p[i, i + half] = 1.0
    return jnp.asarray(p)


def _mixer_fwd(h, cos, sin, w, tag):
    d = h.shape[1]
    hn, = _rows(_rms, [h], [w["norm_mix"]], [(d, bf16)], name=f"mix_norm_{tag}")
    proj = _mm(hn, w["mix_w_in"], name=f"mix_in_{tag}")
    pa = [w["lb0"], w["lb1"], w["mla_q_a_norm"], w["mla_kv_a_norm"]]
    qf, kk, vv, logf, cqn, ckvn = _rows(
        _mix_a, [proj], pa, [(HG_WIDTH, f32)] * 4 + [(MLA_Q_RANK, bf16), (MLA_KV_RANK, bf16)], name=f"mix_a_{tag}")
    o_hg, states = _hg_fwd(qf, kk, vv, logf, name=f"hg_fwd_{tag}")
    qraw = _mm(cqn, w["mla_w_uq"], name=f"mla_uq_{tag}")
    kvraw = _mm(ckvn, w["mla_w_ukv"], name=f"mla_ukv_{tag}")
    pb = [w["mla_qn_nope"], w["mla_qn_rope"], w["mla_kn_nope"], w["mla_kn_rope"], w["rope_perm"]]
    qfull, kfull, vfull = _rows(_mix_b, [qraw, kvraw, proj, cos, sin], pb,
                                [(MLA_HEADS * MLA_QK_PAD, bf16)] * 2 + [(MLA_HEADS * MLA_V, bf16)],
                                name=f"mix_b_{tag}")
    o_mla, lse = _att_fwd(qfull, kfull, vfull, name=f"att_fwd_{tag}")
    mixin, = _rows(_mix_c, [o_hg, proj, o_mla], [w["hg_out_norm"]], [(d, bf16)], name=f"mix_c_{tag}")
    out = _mm(mixin, w["mix_w_out"], add=h, name=f"mix_out_{tag}")
    return out, (h, hn, proj, qf, kk, vv, logf, cqn, ckvn, o_hg, states, qraw, kvraw, qfull, kfull, vfull, o_mla,
                 lse, mixin)


def _mixer_bwd(dout, cos, sin, w, saved, tag, on_w_out=None):
    (h, hn, proj, qf, kk, vv, logf, cqn, ckvn, o_hg, states, qraw, kvraw, qfull, kfull, vfull, o_mla, lse,
     mixin) = saved
    g = {}
    dmixin = _mm(dout, w["mix_w_out"], tb=True, name=f"mix_dmixin_{tag}")
    g["mix_w_out"] = _mm(mixin, dout, ta=True, name=f"mix_dwout_{tag}")
    if on_w_out is not None:
        dmixin = on_w_out(g["mix_w_out"], dmixin)
    (do_hg, dproj_c, do_mla), (g["hg_out_norm"],) = _rows_bwd(
        _mix_c, [o_hg, proj, o_mla], [w["hg_out_norm"]], [dmixin], rgrad=[f32, f32, bf16], pgrad=[True],
        name=f"mix_dc_{tag}")
    dqfull, dkfull, dvfull = _att_bwd(qfull, kfull, vfull, o_mla, lse, do_mla, name=f"att_bwd_{tag}")
    pb = [w["mla_qn_nope"], w["mla_qn_rope"], w["mla_kn_nope"], w["mla_kn_rope"], w["rope_perm"]]
    (dqraw, dkvraw, dproj_b), pg = _rows_bwd(
        _mix_b, [qraw, kvraw, proj, cos, sin], pb, [dqfull, dkfull, dvfull],
        rgrad=[bf16, bf16, f32, None, None], pgrad=[True, True, True, True, False], addends={2: dproj_c},
        name=f"mix_db_{tag}")
    g["mla_qn_nope"], g["mla_qn_rope"], g["mla_kn_nope"], g["mla_kn_rope"] = pg
    dcqn = _mm(dqraw, w["mla_w_uq"], tb=True, name=f"mla_dcq_{tag}")
    g["mla_w_uq"] = _mm(cqn, dqraw, ta=True, name=f"mla_dwuq_{tag}")
    dckvn = _mm(dkvraw, w["mla_w_ukv"], tb=True, name=f"mla_dckv_{tag}")
    g["mla_w_ukv"] = _mm(ckvn, dkvraw, ta=True, name=f"mla_dwukv_{tag}")
    dqf, dkk, dvv, dlogf = _hg_bwd(qf, kk, vv, logf, states, do_hg, name=f"hg_bwd_{tag}")
    pa = [w["lb0"], w["lb1"], w["mla_q_a_norm"], w["mla_kv_a_norm"]]
    (dproj,), (g["lb0"], g["lb1"], g["mla_q_a_norm"], g["mla_kv_a_norm"]) = _rows_bwd(
        _mix_a, [proj], pa, [dqf, dkk, dvv, dlogf, dcqn, dckvn], rgrad=[bf16], pgrad=[True] * 4,
        addends={0: dproj_b}, name=f"mix_da_{tag}")
    dhn = _mm(dproj, w["mix_w_in"], tb=True, out_dtype=bf16, name=f"mix_dhn_{tag}")
    g["mix_w_in_t"] = _mm(dproj, hn, ta=True, name=f"mix_dwin_{tag}")
    (dh,), (g["norm_mix"],) = _rows_bwd(_rms, [h], [w["norm_mix"]], [dhn], rgrad=[f32], pgrad=[True],
                                        addends={0: dout}, name=f"mix_dnorm_{tag}")
    return dh, g


def _rope_tables(positions):
    inv_freq = 1.0 / (ROPE_BASE ** (jnp.arange(0, MLA_ROPE, 2, dtype=f32) / MLA_ROPE))
    ang = positions.astype(f32)[:, None] * inv_freq
    z = jnp.zeros((positions.shape[0], 128 - MLA_ROPE), f32)
    return (jnp.concatenate([jnp.cos(ang), jnp.cos(ang), z], axis=1),
            jnp.concatenate([jnp.sin(ang), jnp.sin(ang), z], axis=1))


def _pad_cols(a, n):
    return jnp.pad(a, ((0, 0), (0, n - a.shape[1])))


def _even_weights(p, j, layer, dt):
    w_uq = p["mla_w_uq"][j].reshape(MLA_Q_RANK, MLA_HEADS, MLA_QK)
    w_uq = jnp.pad(w_uq, ((0, 0), (0, 0), (0, MLA_QK_PAD - MLA_QK))).reshape(MLA_Q_RANK, MLA_HEADS * MLA_QK_PAD)
    return dict(
        norm_mix=p["norm_mix"][layer][None], mix_w_in=_pad_cols(p["mix_w_in"][j], IN_PAD).astype(dt),
        lb0=p["hg_lb_logits"][0][None], lb1=p["hg_lb_logits"][1][None],
        mla_q_a_norm=p["mla_q_a_norm"][j][None], mla_kv_a_norm=p["mla_kv_a_norm"][j][None],
        mla_w_uq=w_uq.astype(dt), mla_w_ukv=p["mla_w_ukv"][j].astype(dt),
        mla_qn_nope=p["mla_qn_nope"][j][None], mla_qn_rope=_pad_cols(p["mla_qn_rope"][j][None], 128),
        mla_kn_nope=p["mla_kn_nope"][j][None], mla_kn_rope=_pad_cols(p["mla_kn_rope"][j][None], 128),
        rope_perm=_rope_perm(), hg_out_norm=p["hg_out_norm"][j][None], mix_w_out=p["mix_w_out"][j].astype(dt))


def _even_grads(g):
    w_uq = g["mla_w_uq"].reshape(MLA_Q_RANK, MLA_HEADS, MLA_QK_PAD)[:, :, :MLA_QK].reshape(MLA_Q_RANK, -1)
    return dict(
        norm_mix=g["norm_mix"], mix_w_in=g["mix_w_in_t"][:IN_WIDTH].T[None],
        hg_lb_logits=jnp.concatenate([g["lb0"], g["lb1"]], axis=0),
        mla_q_a_norm=g["mla_q_a_norm"], mla_kv_a_norm=g["mla_kv_a_norm"], mla_w_uq=w_uq[None],
        mla_w_ukv=g["mla_w_ukv"][None], mla_qn_nope=g["mla_qn_nope"], mla_qn_rope=g["mla_qn_rope"][:, :MLA_ROPE],
        mla_kn_nope=g["mla_kn_nope"], mla_kn_rope=g["mla_kn_rope"][:, :MLA_ROPE],
        hg_out_norm=g["hg_out_norm"], mix_w_out=g["mix_w_out"][None])


_S5_NB = 8
_S5_BW = 1024
_S5_HALF = 512
_S5_UC = 128
_S5_TIME = 512


def _bd_mm(a, b3, *, name, tb=False, out_dtype=f32):
    length = a.shape[0]
    rows_b, cols_b = b3.shape[0] // _S5_NB, b3.shape[1]
    ka, n = (cols_b, rows_b) if tb else (rows_b, cols_b)
    bm = _pick(length, (512, 256, 128))
    dims = _NT if tb else _NN

    def body(a_ref, b_ref, o_ref):
        o_ref[...] = lax.dot_general(a_ref[...].astype(bf16), b_ref[...].astype(bf16), dims,
                                     preferred_element_type=f32).astype(o_ref.dtype)

    return pl.pallas_call(
        body, grid=(length // bm, _S5_NB),
        in_specs=[pl.BlockSpec((bm, ka), lambda i, j: (i, j)), pl.BlockSpec((rows_b, cols_b), lambda i, j: (j, 0))],
        out_specs=pl.BlockSpec((bm, n), lambda i, j: (i, j)),
        out_shape=jax.ShapeDtypeStruct((length, _S5_NB * n), out_dtype),
        compiler_params=_cparams(("parallel", "parallel")), name=name)(a, b3)


def _bd_mm_tn(a, c, *, name):
    length = a.shape[0]
    ka, n = a.shape[1] // _S5_NB, c.shape[1] // _S5_NB
    bk = _pick(length, (512, 256, 128))
    nk = length // bk

    def body(a_ref, c_ref, o_ref):
        @pl.when(pl.program_id(1) == 0)
        def _():
            o_ref[...] = jnp.zeros_like(o_ref)
        o_ref[...] += lax.dot_general(a_ref[...].astype(bf16), c_ref[...].astype(bf16), _TN,
                                      preferred_element_type=f32)

    return pl.pallas_call(
        body, grid=(_S5_NB, nk),
        in_specs=[pl.BlockSpec((bk, ka), lambda j, q: (q, j)), pl.BlockSpec((bk, n), lambda j, q: (q, j))],
        out_specs=pl.BlockSpec((ka, n), lambda j, q: (j, 0)),
        out_shape=jax.ShapeDtypeStruct((_S5_NB * ka, n), f32),
        compiler_params=_cparams(("parallel", "arbitrary")), name=name)(a, c)


def _cmul(ar, ai, br, bi):
    return ar * br - ai * bi, ar * bi + ai * br


def _pow_table(ar, ai, descending):
    rows = lax.broadcasted_iota(jnp.int32, (8, ar.shape[1]), 0)
    tr = jnp.zeros((8, ar.shape[1]), f32)
    ti = jnp.zeros((8, ar.shape[1]), f32)
    pr, pi_ = ar, ai
    for r in range(8):
        sel = rows == ((7 - r) if descending else r)
        tr = jnp.where(sel, pr, tr)
        ti = jnp.where(sel, pi_, ti)
        pr, pi_ = _cmul(pr, pi_, ar, ai)
    return tr, ti


def _s5_scan_fwd(a, x, *, name):
    length = x.shape[0]
    tc = min(_S5_TIME, length)
    hw = _S5_HALF

    def body(a_ref, x_ref, o_ref, carry):
        @pl.when(pl.program_id(1) == 0)
        def _():
            carry[...] = jnp.zeros_like(carry)

        ar, ai = a_ref[:, :hw], a_ref[:, hw:]
        xr, xi = x_ref[:, :hw], x_ref[:, hw:]
        row8 = lax.broadcasted_iota(jnp.int32, (tc, hw), 0) & 7
        mr, mi = ar, ai
        for s in (1, 2, 4):
            sr, si = pltpu.roll(xr, s, axis=0), pltpu.roll(xi, s, axis=0)
            pr, pi_ = _cmul(mr, mi, sr, si)
            ok = row8 >= s
            xr = xr + jnp.where(ok, pr, 0.0)
            xi = xi + jnp.where(ok, pi_, 0.0)
            mr, mi = _cmul(mr, mi, mr, mi)
        o_ref[:, :hw] = xr
        o_ref[:, hw:] = xi
        tr, ti = _pow_table(ar, ai, False)
        cr, ci = carry[:, :hw], carry[:, hw:]
        for i in range(tc // 8):
            sl = slice(8 * i, 8 * i + 8)
            pr, pi_ = _cmul(tr, ti, cr, ci)
            o_ref[sl, :hw] = o_ref[sl, :hw] + pr
            o_ref[sl, hw:] = o_ref[sl, hw:] + pi_
            cr, ci = o_ref[8 * i + 7:8 * i + 8, :hw], o_ref[8 * i + 7:8 * i + 8, hw:]
        carry[:, :hw] = cr
        carry[:, hw:] = ci

    return pl.pallas_call(
        body, grid=(_S5_NB, length // tc),
        in_specs=[pl.BlockSpec((1, _S5_BW), lambda j, t: (0, j)), pl.BlockSpec((tc, _S5_BW), lambda j, t: (t, j))],
        out_specs=pl.BlockSpec((tc, _S5_BW), lambda j, t: (t, j)),
        out_shape=jax.ShapeDtypeStruct(x.shape, f32),
        scratch_shapes=[pltpu.VMEM((1, _S5_BW), f32)],
        compiler_params=_cparams(("parallel", "arbitrary")), name=name)(a, x)


def _s5_scan_bwd(a, hs, d, *, name):
    length = d.shape[0]
    tc = min(_S5_TIME, length)
    nt = length // tc
    hw = _S5_HALF

    def body(a_ref, h_ref, d_ref, g_ref, da_ref, carry):
        @pl.when(pl.program_id(1) == 0)
        def _():
            carry[...] = jnp.zeros_like(carry)
            da_ref[...] = jnp.zeros_like(da_ref)

        ar, ai = a_ref[:, :hw], -a_ref[:, hw:]
        xr, xi = d_ref[:, :hw], d_ref[:, hw:]
        rows = lax.broadcasted_iota(jnp.int32, (tc, hw), 0)
        row8 = rows & 7
        mr, mi = ar, ai
        for s in (1, 2, 4):
            sr, si = pltpu.roll(xr, tc - s, axis=0), pltpu.roll(xi, tc - s, axis=0)
            pr, pi_ = _cmul(mr, mi, sr, si)
            ok = row8 < 8 - s
            xr = xr + jnp.where(ok, pr, 0.0)
            xi = xi + jnp.where(ok, pi_, 0.0)
            mr, mi = _cmul(mr, mi, mr, mi)
        g_ref[:, :hw] = xr
        g_ref[:, hw:] = xi
        tr, ti = _pow_table(ar, ai, True)
        cr0, ci0 = carry[:, :hw], carry[:, hw:]
        cr, ci = cr0, ci0
        for i in reversed(range(tc // 8)):
            sl = slice(8 * i, 8 * i + 8)
            pr, pi_ = _cmul(tr, ti, cr, ci)
            g_ref[sl, :hw] = g_ref[sl, :hw] + pr
            g_ref[sl, hw:] = g_ref[sl, hw:] + pi_
            cr, ci = g_ref[8 * i:8 * i + 1, :hw], g_ref[8 * i:8 * i + 1, hw:]
        carry[:, :hw] = cr
        carry[:, hw:] = ci
        last = rows == tc - 1
        gnr = jnp.where(last, cr0, pltpu.roll(g_ref[:, :hw], tc - 1, axis=0))
        gni = jnp.where(last, ci0, pltpu.roll(g_ref[:, hw:], tc - 1, axis=0))
        hr, hi = h_ref[:, :hw], h_ref[:, hw:]
        da_ref[:, :hw] += jnp.sum(hr * gnr + hi * gni, axis=0, keepdims=True)
        da_ref[:, hw:] += jnp.sum(hr * gni - hi * gnr, axis=0, keepdims=True)

    blk = pl.BlockSpec((tc, _S5_BW), lambda j, t: (nt - 1 - t, j))
    row = pl.BlockSpec((1, _S5_BW), lambda j, t: (0, j))
    return pl.pallas_call(
        body, grid=(_S5_NB, nt), in_specs=[row, blk, blk], out_specs=[blk, row],
        out_shape=[jax.ShapeDtypeStruct(d.shape, f32), jax.ShapeDtypeStruct((1, _S5_NB * _S5_BW), f32)],
        scratch_shapes=[pltpu.VMEM((1, _S5_BW), f32)],
        compiler_params=_cparams(("parallel", "arbitrary")), name=name)(a, hs, d)


def _s5_tile_scan(work, carry, ar, ai, tc, reverse, per_tile=None):
    hw = _S5_HALF
    row8 = lax.broadcasted_iota(jnp.int32, (8, hw), 0)
    powers = [(ar, ai)]
    for _ in range(2):
        powers.append(_cmul(*powers[-1], *powers[-1]))
    tr, ti = _pow_table(ar, ai, reverse)
    cr, ci = carry[:, :hw], carry[:, hw:]
    tiles = range(tc // 8)
    for i in (reversed(tiles) if reverse else tiles):
        sl = slice(8 * i, 8 * i + 8)
        xr, xi = work[sl, :hw], work[sl, hw:]
        for (mr, mi), s in zip(powers, (1, 2, 4)):
            shift = 8 - s if reverse else s
            pr, pi_ = _cmul(mr, mi, pltpu.roll(xr, shift, axis=0), pltpu.roll(xi, shift, axis=0))
            ok = (row8 < 8 - s) if reverse else (row8 >= s)
            xr = xr + jnp.where(ok, pr, 0.0)
            xi = xi + jnp.where(ok, pi_, 0.0)
        pr, pi_ = _cmul(tr, ti, cr, ci)
        xr, xi = xr + pr, xi + pi_
        work[sl, :hw] = xr
        work[sl, hw:] = xi
        if per_tile is not None:
            per_tile(sl, xr, xi, cr, ci)
        edge = 8 * i if reverse else 8 * i + 7
        cr, ci = work[edge:edge + 1, :hw], work[edge:edge + 1, hw:]
    carry[:, :hw] = cr
    carry[:, hw:] = ci


def _s5_core_fwd(a, hn, b3, c3, *, name):
    length = hn.shape[0]
    tc = min(_S5_TIME, length)

    def body(a_ref, hn_ref, b_ref, c_ref, hs_ref, y_ref, work, carry):
        @pl.when(pl.program_id(1) == 0)
        def _():
            carry[...] = jnp.zeros_like(carry)

        work[...] = jnp.dot(hn_ref[...].astype(bf16), b_ref[...], preferred_element_type=f32)
        _s5_tile_scan(work, carry, a_ref[:, :_S5_HALF], a_ref[:, _S5_HALF:], tc, False)
        hs = work[...].astype(bf16)
        hs_ref[...] = hs
        y_ref[...] = jnp.dot(hs, c_ref[...], preferred_element_type=f32)

    return pl.pallas_call(
        body, grid=(_S5_NB, length // tc),
        in_specs=[pl.BlockSpec((1, _S5_BW), lambda j, t: (0, j)), pl.BlockSpec((tc, _S5_UC), lambda j, t: (t, j)),
                  pl.BlockSpec((_S5_UC, _S5_BW), lambda j, t: (j, 0)), pl.BlockSpec((_S5_BW, _S5_UC), lambda j, t: (j, 0))],
        out_specs=[pl.BlockSpec((tc, _S5_BW), lambda j, t: (t, j)), pl.BlockSpec((tc, _S5_UC), lambda j, t: (t, j))],
        out_shape=[jax.ShapeDtypeStruct((length, _S5_NB * _S5_BW), bf16),
                   jax.ShapeDtypeStruct((length, _S5_NB * _S5_UC), f32)],
        scratch_shapes=[pltpu.VMEM((tc, _S5_BW), f32), pltpu.VMEM((1, _S5_BW), f32)],
        compiler_params=_cparams(("parallel", "arbitrary")), name=name)(a, hn, b3, c3)


def _s5_core_bwd(a, dy, c3, hs, hn, b3, *, name):
    length = hn.shape[0]
    tc = min(_S5_TIME, length)
    nt = length // tc
    hw = _S5_HALF

    def body(a_ref, dy_ref, c_ref, hs_ref, hn_ref, b_ref, du_ref, db_ref, dc_ref, da_ref, work, carry, acc):
        @pl.when(pl.program_id(1) == 0)
        def _():
            carry[...] = jnp.zeros_like(carry)
            db_ref[...] = jnp.zeros_like(db_ref)
            dc_ref[...] = jnp.zeros_like(dc_ref)
            da_ref[...] = jnp.zeros_like(da_ref)

        dyb = dy_ref[...].astype(bf16)
        work[...] = lax.dot_general(dyb, c_ref[...], _NT, preferred_element_type=f32)
        acc[...] = jnp.zeros_like(acc)
        row8 = lax.broadcasted_iota(jnp.int32, (8, hw), 0)

        def grad_a(sl, gr, gi, cr, ci):
            gnr = jnp.where(row8 == 7, cr, pltpu.roll(gr, 7, axis=0))
            gni = jnp.where(row8 == 7, ci, pltpu.roll(gi, 7, axis=0))
            hr, hi = hs_ref[sl, :hw].astype(f32), hs_ref[sl, hw:].astype(f32)
            acc[:, :hw] += hr * gnr + hi * gni
            acc[:, hw:] += hr * gni - hi * gnr

        _s5_tile_scan(work, carry, a_ref[:, :hw], -a_ref[:, hw:], tc, True, grad_a)
        da_ref[...] += jnp.sum(acc[...], axis=0, keepdims=True)
        g = work[...].astype(bf16)
        du_ref[...] = lax.dot_general(g, b_ref[...], _NT, preferred_element_type=f32)
        db_ref[...] += lax.dot_general(hn_ref[...].astype(bf16), g, _TN, preferred_element_type=f32)
        dc_ref[...] += lax.dot_general(hs_ref[...], dyb, _TN, preferred_element_type=f32)

    rev = lambda j, t: (nt - 1 - t, j)
    return pl.pallas_call(
        body, grid=(_S5_NB, nt),
        in_specs=[pl.BlockSpec((1, _S5_BW), lambda j, t: (0, j)), pl.BlockSpec((tc, _S5_UC), rev),
                  pl.BlockSpec((_S5_BW, _S5_UC), lambda j, t: (j, 0)), pl.BlockSpec((tc, _S5_BW), rev),
                  pl.BlockSpec((tc, _S5_UC), rev), pl.BlockSpec((_S5_UC, _S5_BW), lambda j, t: (j, 0))],
        out_specs=[pl.BlockSpec((tc, _S5_UC), rev), pl.BlockSpec((_S5_UC, _S5_BW), lambda j, t: (j, 0)),
                   pl.BlockSpec((_S5_BW, _S5_UC), lambda j, t: (j, 0)), pl.BlockSpec((1, _S5_BW), lambda j, t: (0, j))],
        out_shape=[jax.ShapeDtypeStruct((length, _S5_NB * _S5_UC), f32),
                   jax.ShapeDtypeStruct((_S5_NB * _S5_UC, _S5_BW), f32),
                   jax.ShapeDtypeStruct((_S5_NB * _S5_BW, _S5_UC), f32),
                   jax.ShapeDtypeStruct((1, _S5_NB * _S5_BW), f32)],
        scratch_shapes=[pltpu.VMEM((tc, _S5_BW), f32), pltpu.VMEM((1, _S5_BW), f32), pltpu.VMEM((8, _S5_BW), f32)],
        compiler_params=_cparams(("parallel", "arbitrary")), name=name)(a, dy, c3, hs, hn, b3)


def _s5_disc(lr, li, ldt, btr, bti, expand):
    dt = jnp.exp(ldt)
    mag = jnp.exp(lr * dt)
    abr = mag * jnp.cos(li * dt)
    abi = mag * jnp.sin(li * dt)
    den = lr * lr + li * li
    zr = ((abr - 1.0) * lr + abi * li) / den
    zi = (abi * lr - (abr - 1.0) * li) / den
    zr = jnp.dot(zr, expand, precision=lax.Precision.HIGHEST, preferred_element_type=f32)
    zi = jnp.dot(zi, expand, precision=lax.Precision.HIGHEST, preferred_element_type=f32)
    return abr, abi, zr * btr - zi * bti, zr * bti + zi * btr


def _s5_disc_fwd(args, *, name):
    def body(*refs):
        res = _s5_disc(*[r[...] for r in refs[:6]])
        for o, v in zip(refs[6:], res):
            o[...] = v

    sds = jax.ShapeDtypeStruct
    return pl.pallas_call(body, out_shape=[sds(args[0].shape, f32)] * 2 + [sds(args[3].shape, f32)] * 2,
                          name=name)(*args)


def _s5_disc_bwd(args, cts, *, name):
    def body(*refs):
        vals = [r[...] for r in refs[:6]]
        _, vjp = jax.vjp(lambda *d: _s5_disc(*d, vals[5]), *vals[:5])
        grads = vjp(tuple(r[...] for r in refs[6:10]))
        for o, v in zip(refs[10:], grads):
            o[...] = v

    return pl.pallas_call(body, out_shape=[jax.ShapeDtypeStruct(a.shape, f32) for a in args[:5]],
                          name=name)(*args, *cts)


def _gelu_tanh(x):
    return 0.5 * x * (1.0 + jnp.tanh(0.7978845608028654 * (x + 0.044715 * (x * x * x))))


def _s5_post(y, u, d_skip):
    return _gelu_tanh(y + d_skip * u)


def _s5_glu(ga, gb, h):
    return h + ga * jax.nn.sigmoid(gb)


def _s5_expand():
    e = np.zeros((S5_STATE, S5_GROUP * S5_STATE), np.float32)
    for m in range(S5_GROUP):
        e[np.arange(S5_STATE), m * S5_STATE + np.arange(S5_STATE)] = 1.0
    return jnp.asarray(e)


def _s5_pack_b(bbr, bbi):
    eye = jnp.eye(8, dtype=f32)

    def one(bb):
        b5 = bb.reshape(_S5_NB, 8, S5_GROUP, S5_STATE)
        return jnp.einsum("jgmp,gh->jgmhp", b5, eye).reshape(_S5_NB * _S5_UC, _S5_HALF)

    return jnp.concatenate([one(bbr), one(bbi)], axis=1)


def _s5_unpack_b(db3):
    def one(d):
        d5 = d.reshape(_S5_NB, 8, S5_GROUP, 8, S5_STATE)
        return jnp.einsum("jgmgp->jgmp", d5).reshape(S5_GROUPS, S5_GROUP * S5_STATE)

    return one(db3[:, :_S5_HALF]), one(db3[:, _S5_HALF:])


def _s5_pack_c(c_re, c_im):
    eye = jnp.eye(8, dtype=f32)

    def one(c):
        c4 = c.reshape(_S5_NB, 8, S5_GROUP, S5_STATE)
        return jnp.einsum("jgmp,hg->jhpgm", c4, eye).reshape(_S5_NB, _S5_HALF, _S5_UC)

    return jnp.concatenate([one(c_re), -one(c_im)], axis=1).reshape(_S5_NB * _S5_BW, _S5_UC)


def _s5_unpack_c(dc3):
    d = dc3.reshape(_S5_NB, 2, 8, S5_STATE, 8, S5_GROUP)
    dre = jnp.einsum("jgpgm->jgmp", d[:, 0]).reshape(S5_GROUPS, S5_GROUP, S5_STATE)
    dim = -jnp.einsum("jgpgm->jgmp", d[:, 1]).reshape(S5_GROUPS, S5_GROUP, S5_STATE)
    return dre, dim


def _s5_state_row(re, im):
    r = re.reshape(_S5_NB, 1, _S5_HALF)
    i = im.reshape(_S5_NB, 1, _S5_HALF)
    return jnp.concatenate([r, i], axis=2).reshape(1, _S5_NB * _S5_BW)


def _s5_unstate_row(row):
    r = row.reshape(_S5_NB, 2, 8, S5_STATE)
    return r[:, 0].reshape(S5_GROUPS, S5_STATE), r[:, 1].reshape(S5_GROUPS, S5_STATE)


def _s5_fwd(h, w, tag):
    d = h.shape[1]
    hn, = _rows(_rms, [h], [w["norm_mix"]], [(d, f32)], name=f"s5_norm_{tag}")
    disc_in = [w["s5_lam_re"], w["s5_lam_im"], w["s5_log_dt"], w["s5_bt_re"], w["s5_bt_im"], w["s5_expand"]]
    abr, abi, bbr, bbi = _s5_disc_fwd(disc_in, name=f"s5_disc_{tag}")
    a_row = _s5_state_row(abr, abi)
    b3 = _s5_pack_b(bbr, bbi).astype(bf16)
    hs, y = _s5_core_fwd(a_row, hn, b3, w["s5_c3"], name=f"s5_core_{tag}")
    yg, = _rows(_s5_post, [y, hn], [w["s5_d"]], [(d, bf16)], name=f"s5_post_{tag}")
    ga = _mm(yg, w["s5_w_glu_a"], name=f"s5_glu_a_{tag}")
    gb = _mm(yg, w["s5_w_glu_b"], name=f"s5_glu_b_{tag}")
    out, = _rows(_s5_glu, [ga, gb, h], [], [(d, f32)], name=f"s5_glu_{tag}")
    return out, (h, hn, disc_in, a_row, b3, hs, y, yg, ga, gb)


def _s5_bwd(dout, w, saved, tag):
    h, hn, disc_in, a_row, b3, hs, y, yg, ga, gb = saved
    g = {}
    (dga, dgb), _ = _rows_bwd(_s5_glu, [ga, gb, h], [], [dout], rgrad=[bf16, bf16, None], pgrad=[],
                              name=f"s5_dglu_{tag}")
    dyg = _mm(dga, w["s5_w_glu_a"], tb=True, name=f"s5_dyg_a_{tag}")
    dyg = _mm(dgb, w["s5_w_glu_b"], tb=True, add=dyg, name=f"s5_dyg_b_{tag}")
    g["s5_w_glu_a"] = _mm(yg, dga, ta=True, name=f"s5_dwa_{tag}")
    g["s5_w_glu_b"] = _mm(yg, dgb, ta=True, name=f"s5_dwb_{tag}")
    (dy, du_skip), (g["s5_d"],) = _rows_bwd(_s5_post, [y, hn], [w["s5_d"]], [dyg], rgrad=[bf16, f32], pgrad=[True],
                                           name=f"s5_dpost_{tag}")
    du, db3, dc3, da_row = _s5_core_bwd(a_row, dy, w["s5_c3"], hs, hn, b3, name=f"s5_dcore_{tag}")
    dabr, dabi = _s5_unstate_row(da_row)
    dbbr, dbbi = _s5_unpack_b(db3)
    g["s5_lam_re"], g["s5_lam_im"], g["s5_log_dt"], g["s5_bt_re"], g["s5_bt_im"] = _s5_disc_bwd(
        disc_in, [dabr, dabi, dbbr, dbbi], name=f"s5_ddisc_{tag}")
    g["s5_c_re"], g["s5_c_im"] = _s5_unpack_c(dc3)
    (dh,), (g["norm_mix"],) = _rows_bwd(_rms_twice, [h], [w["norm_mix"]], [du, du_skip], rgrad=[f32], pgrad=[True],
                                        addends={0: dout}, name=f"s5_dnorm_{tag}")
    return dh, g


def _odd_weights(p, j, layer, dt):
    tr = lambda b: b.transpose(0, 2, 1).reshape(S5_GROUPS, S5_GROUP * S5_STATE)
    return dict(
        norm_mix=p["norm_mix"][layer][None], s5_lam_re=p["s5_lam_re"][j], s5_lam_im=p["s5_lam_im"][j],
        s5_log_dt=p["s5_log_dt"][j][:, None], s5_bt_re=tr(p["s5_b_re"][j]), s5_bt_im=tr(p["s5_b_im"][j]),
        s5_expand=_s5_expand(), s5_c3=_s5_pack_c(p["s5_c_re"][j], p["s5_c_im"][j]).astype(dt),
        s5_d=p["s5_d"][j][None], s5_w_glu_a=p["s5_w_glu_a"][j].astype(dt), s5_w_glu_b=p["s5_w_glu_b"][j].astype(dt))


def _odd_grads(g):
    tr = lambda b: b.reshape(S5_GROUPS, S5_GROUP, S5_STATE).transpose(0, 2, 1)[None]
    return dict(
        norm_mix=g["norm_mix"], s5_lam_re=g["s5_lam_re"][None], s5_lam_im=g["s5_lam_im"][None],
        s5_log_dt=g["s5_log_dt"][:, 0][None], s5_b_re=tr(g["s5_bt_re"]), s5_b_im=tr(g["s5_bt_im"]),
        s5_c_re=g["s5_c_re"][None], s5_c_im=g["s5_c_im"][None], s5_d=g["s5_d"],
        s5_w_glu_a=g["s5_w_glu_a"][None], s5_w_glu_b=g["s5_w_glu_b"][None])


def _loss_fn(y, t):
    e = y - t
    part = jnp.sum(jnp.sum(e * e, axis=-1, keepdims=True), axis=0, keepdims=True) * (0.5 / y.shape[1])
    return e * (1.0 / y.shape[1]), part


FF_SHARD = 352
FF_SHARD_PAD = 384


def _pad_groups(a, axis):
    axis %= a.ndim
    s = a.shape
    a = a.reshape(s[:axis] + (s[axis] // FF_SHARD, FF_SHARD) + s[axis + 1:])
    pad = [(0, 0)] * a.ndim
    pad[axis + 1] = (0, FF_SHARD_PAD - FF_SHARD)
    return jnp.pad(a, pad).reshape(s[:axis] + (s[axis] // FF_SHARD * FF_SHARD_PAD,) + s[axis + 1:])


def _unpad_groups(a, axis):
    axis %= a.ndim
    s = a.shape
    a = a.reshape(s[:axis] + (s[axis] // FF_SHARD_PAD, FF_SHARD_PAD) + s[axis + 1:])
    a = lax.slice_in_dim(a, 0, FF_SHARD, axis=axis + 1)
    return a.reshape(s[:axis] + (s[axis] // FF_SHARD_PAD * FF_SHARD,) + s[axis + 1:])


def _layer_weights(p, layer, dt):
    return dict(
        norm_xa=p["norm_xa"][layer][None], norm_mem=p["norm_mem"][layer][None], norm_ffn=p["norm_ffn"][layer][None],
        xa_wq=p["xa_wq"][layer].astype(dt), xa_wk=p["xa_wk"][layer].astype(dt), xa_wv=p["xa_wv"][layer].astype(dt),
        xa_wo=p["xa_wo"][layer].astype(dt), xa_q_norm=p["xa_q_norm"][layer][None],
        xa_k_norm=p["xa_k_norm"][layer][None], ffn_w_up=_pad_groups(p["ffn_w_up"][layer], 1).astype(dt),
        ffn_conv_w=_pad_groups(p["ffn_conv_w"][layer], 1), ffn_conv_b=_pad_groups(p["ffn_conv_b"][layer][None], 1),
        ffn_w_down=_pad_groups(p["ffn_w_down"][layer], 0).astype(dt))


_PER_LAYER = ("norm_xa", "norm_mem", "norm_ffn", "xa_wq", "xa_wk", "xa_wv", "xa_wo", "xa_q_norm", "xa_k_norm",
              "ffn_w_up", "ffn_conv_w", "ffn_conv_b", "ffn_w_down")
_FFN_PADDED = dict(ffn_w_up=1, ffn_conv_w=1, ffn_conv_b=1, ffn_w_down=0)


def _local_step(x, mem, positions, target, p):
    cos, sin = _rope_tables(positions)
    we = _even_weights(p, 0, 0, bf16)
    wo = _odd_weights(p, 0, 1, bf16)
    wl = [_layer_weights(p, layer, bf16) for layer in range(2)]
    loss, dh, g_even, g_odd, gl = _local_core(x, mem, cos, sin, target, we, wo, wl)
    grads = {}
    for n in _PER_LAYER:
        a, b = gl[0][n], gl[1][n]
        if n in _FFN_PADDED:
            a, b = _unpad_groups(a, _FFN_PADDED[n]), _unpad_groups(b, _FFN_PADDED[n])
        grads[n] = jnp.concatenate([a, b], axis=0) if a.shape[0] == 1 else jnp.stack([a, b])
    ge, go = _even_grads(g_even), _odd_grads(g_odd)
    grads["norm_mix"] = jnp.concatenate([ge.pop("norm_mix"), go.pop("norm_mix")], axis=0)
    grads.update(ge)
    grads.update(go)
    return loss, dh, grads


def _local_core(x, mem, cos, sin, target, we, wo, wl):
    h, s_mix0 = _mixer_fwd(x, cos, sin, we, "l0")
    h, s_xa0 = _xattn_fwd(h, mem, wl[0], "l0")
    h, s_ff0 = _ffn_fwd(h, wl[0], "l0")
    h, s_mix1 = _s5_fwd(h, wo, "l1")
    h, s_xa1 = _xattn_fwd(h, mem, wl[1], "l1")
    h, s_ff1 = _ffn_fwd(h, wl[1], "l1")
    dh, loss = _rows(_loss_fn, [h, target], [], [(h.shape[1], f32)], accs=[(1, 1)], name="loss_head")

    gl = [{}, {}]
    dh, g = _ffn_bwd(dh, wl[1], s_ff1, "l1")
    gl[1].update(g)
    dh, g = _xattn_bwd(dh, mem, wl[1], s_xa1, "l1")
    gl[1].update(g)
    dh, g_odd = _s5_bwd(dh, wo, s_mix1, "l1")
    dh, g = _ffn_bwd(dh, wl[0], s_ff0, "l0")
    gl[0].update(g)
    dh, g = _xattn_bwd(dh, mem, wl[0], s_xa0, "l0")
    gl[0].update(g)
    dh, g_even = _mixer_bwd(dh, cos, sin, we, s_mix0, "l0")
    return loss, dh, g_even, g_odd, gl


_LANES = 1024
_ROW_PAD = 256


_PEER_MASKS = (1, 2, 4, 3, 5, 6, 7)


def _mesh_place():
    x, y, c = lax.axis_index("x"), lax.axis_index("y"), lax.axis_index("c")

    def peer(mask):
        px = 1 - x if mask & 4 else x
        py = 1 - y if mask & 2 else y
        pc = 1 - c if mask & 1 else c
        return (px, py, pc), 4 * px + 2 * py + pc

    return 4 * x + 2 * y + c, peer


class _Exchange:
    def __init__(self, name):
        self.name = name
        self.srcs, self.shapes, self.items, self.where = [], [], [], {}

    def add(self, src, land_shape, src_at, dst_at, key):
        si = next((i for i, s in enumerate(self.srcs) if s is src), None)
        if si is None:
            self.srcs.append(src)
            si = len(self.srcs) - 1
        if key not in self.where:
            self.shapes.append(land_shape)
            self.where[key] = len(self.shapes) - 1
        self.items.append(dict(src=si, dst=self.where[key], src_at=src_at, dst_at=dst_at))

    def _copy(self, k, mask, ins, lands, send_sems, recv_sems, me, peer, arriving):
        it = self.items[k]
        dev, idx = peer(mask)
        s = k * (N_DEV - 1) + _PEER_MASKS.index(mask)
        return pltpu.make_async_remote_copy(
            src_ref=it["src_at"](ins[it["src"]], idx), dst_ref=it["dst_at"](lands[it["dst"]], idx if arriving else me),
            send_sem=send_sems.at[s], recv_sem=recv_sems.at[s], device_id=dev, device_id_type=pl.DeviceIdType.MESH)

    def _own_copy(self, k, ins, lands, own_sems, me):
        it = self.items[k]
        return pltpu.make_async_copy(it["src_at"](ins[it["src"]], me), it["dst_at"](lands[it["dst"]], me), own_sems.at[k])

    def begin(self, own):
        ns, nd, ni = len(self.srcs), len(self.shapes), len(self.items)
        nsem = ni * (N_DEV - 1)
        self.own = own

        nq = 3 if own else 2

        def body(*refs):
            ins, land_refs = refs[:ns], refs[ns:ns + nd]
            sems, token = refs[ns + nd:ns + nd + nq], refs[-1]
            me, peer = _mesh_place()
            for mask in _PEER_MASKS:
                for k in range(ni):
                    self._copy(k, mask, ins, land_refs, sems[0], sems[1], me, peer, False).start()
            if own:
                for k in range(ni):
                    self._own_copy(k, ins, land_refs, sems[2], me).start()
            token[...] = jnp.zeros_like(token)

        hbm = pl.BlockSpec(memory_space=pltpu.HBM)
        sem = pl.BlockSpec(memory_space=pltpu.SEMAPHORE)
        lands = [lax.empty(s.shape, s.dtype) for s in self.shapes]
        sem_shapes = [pltpu.SemaphoreType.DMA((nsem,)), pltpu.SemaphoreType.DMA((nsem,)), pltpu.SemaphoreType.DMA((ni,))]
        res = pl.pallas_call(
            body, in_specs=[hbm] * (ns + nd),
            out_specs=[sem] * nq + [hbm] * nd + [pl.BlockSpec(memory_space=pltpu.VMEM)],
            out_shape=sem_shapes[:nq] + [pltpu.HBM(s.shape, s.dtype) for s in self.shapes]
            + [jax.ShapeDtypeStruct((8, 128), f32)],
            input_output_aliases={ns + j: nq + j for j in range(nd)},
            compiler_params=pltpu.CompilerParams(has_side_effects=pltpu.SideEffectType.DATAFLOW_SIDE_EFFECTING),
            name=self.name + "_start")(*self.srcs, *lands)
        self.token = res[-1]
        return list(res[:nq]), list(res[nq:-1])

    def finish(self, state, after):
        sems, lands = state
        nq = len(sems)
        after = list(after) if isinstance(after, (list, tuple)) else [after]
        ns, nd, ni = len(self.srcs), len(self.shapes), len(self.items)

        def body(*refs):
            ins, land_refs = refs[:ns], refs[ns:ns + nd]
            sem_refs = refs[ns + nd:ns + nd + nq]
            me, peer = _mesh_place()
            for mask in _PEER_MASKS:
                for k in range(ni):
                    cp = self._copy(k, mask, ins, land_refs, sem_refs[0], sem_refs[1], me, peer, True)
                    cp.wait_send()
                    cp.wait_recv()
            if self.own:
                for k in range(ni):
                    self._own_copy(k, ins, land_refs, sem_refs[2], me).wait()

        hbm = pl.BlockSpec(memory_space=pltpu.HBM)
        sem = pl.BlockSpec(memory_space=pltpu.SEMAPHORE)
        res = pl.pallas_call(
            body, in_specs=[hbm] * (ns + nd) + [sem] * nq + [pl.BlockSpec(memory_space=pl.ANY)] * len(after),
            out_specs=[hbm] * nd, out_shape=[pltpu.HBM(s.shape, s.dtype) for s in self.shapes],
            input_output_aliases={ns + j: j for j in range(nd)},
            compiler_params=pltpu.CompilerParams(has_side_effects=pltpu.SideEffectType.DATAFLOW_SIDE_EFFECTING),
            name=self.name + "_wait")(*self.srcs, *lands, *sems, *after)
        return {k: res[i] for k, i in self.where.items()}


def _after(x, *tokens, name):
    def body(*refs):
        del refs

    anyspace = pl.BlockSpec(memory_space=pl.ANY)
    return pl.pallas_call(body, in_specs=[anyspace] * (1 + len(tokens)), out_specs=anyspace,
                          out_shape=jax.ShapeDtypeStruct(x.shape, x.dtype), input_output_aliases={0: 0},
                          name=name)(x, *tokens)


def _rows_of(n):
    return lambda r, i: r.at[pl.ds(pl.multiple_of(i * n, n), n), :]


def _cols_of(n):
    return lambda r, i: r.at[:, pl.ds(pl.multiple_of(i * n, n), n)]


def _whole(r, i):
    return r


def _slot(r, i):
    return r.at[i]


def _at_layer(layer):
    return lambda r, i: r.at[layer]


def _slot_layer(layer):
    return lambda r, i: r.at[i, layer]


def _sum_adam(me_index, slots, owns, own_block, w, m, v, *, name):
    layers, rows, cols = w.shape
    tr = _pick(rows, (256, 128, 104, 64, 32, 16, 8))
    bc1 = 1.0 - ADAM_B1 ** ADAM_STEP
    bc2 = 1.0 - ADAM_B2 ** ADAM_STEP
    own_shape, own_map = own_block(tr)
    nl = len(slots)
    assert nl == layers and len(owns) == layers

    def body(me_ref, *refs):
        s_refs, own_refs = refs[:nl], refs[nl:2 * nl]
        w_ref, m_ref, v_ref, g_ref, d_ref, nm_ref, nv_ref = refs[2 * nl:]
        me = me_ref[0]

        def run(s_ref, own_ref):
            mine = own_ref[0] if len(own_shape) == 3 else own_ref[...]
            g = jnp.where(me == 0, mine, s_ref[0])
            for k in range(1, N_DEV):
                g = g + jnp.where(me == k, mine, s_ref[k])
            mm = ADAM_B1 * m_ref[0] + (1.0 - ADAM_B1) * g
            vv = ADAM_B2 * v_ref[0] + (1.0 - ADAM_B2) * (g * g)
            g_ref[0] = g
            nm_ref[0] = mm
            nv_ref[0] = vv
            d_ref[0] = -ADAM_LR * ((mm / bc1) / (jnp.sqrt(vv / bc2) + ADAM_EPS) + ADAM_WD * w_ref[0])

        for layer in range(nl):
            pl.when(pl.program_id(0) == layer)(functools.partial(run, s_refs[layer], own_refs[layer]))

    def of_layer(layer, index_map):
        return lambda lyr, i, me: index_map(jnp.where(lyr == layer, i, 0), me)

    blk = pl.BlockSpec((1, tr, cols), lambda lyr, i, me: (lyr, i, 0))
    sds = jax.ShapeDtypeStruct((layers, rows, cols), f32)
    grid_spec = pltpu.PrefetchScalarGridSpec(
        num_scalar_prefetch=1, grid=(layers, rows // tr),
        in_specs=[pl.BlockSpec((N_DEV, tr, cols), of_layer(layer, lambda i, me: (0, i, 0))) for layer in range(nl)]
        + [pl.BlockSpec(own_shape, of_layer(layer, own_map)) for layer in range(nl)] + [blk, blk, blk],
        out_specs=[blk] * 4)
    return pl.pallas_call(body, grid_spec=grid_spec, out_shape=[sds] * 4,
                          compiler_params=_cparams(("arbitrary", "arbitrary")),
                          name=name)(me_index, *slots, *owns, w, m, v)


_SHARDED = dict(xa_wq=1, xa_wk=1, xa_wv=1, xa_wo=1, ffn_w_up=2, ffn_conv_w=2, ffn_w_down=1, mix_w_in=2, mla_w_uq=2,
                mla_w_ukv=2, mix_w_out=1, s5_d=1, s5_w_glu_a=1, s5_w_glu_b=1)
_EXACT = ("ffn_conv_w", "s5_d")
_WEIGHTS = ("norm_mix", "norm_xa", "norm_mem", "norm_ffn", "xa_wq", "xa_wk", "xa_wv", "xa_wo", "xa_q_norm",
            "xa_k_norm", "ffn_w_up", "ffn_conv_w", "ffn_conv_b", "ffn_w_down", "hg_lb_logits", "mix_w_in",
            "hg_out_norm", "mla_q_a_norm", "mla_w_uq", "mla_kv_a_norm", "mla_w_ukv", "mla_qn_nope", "mla_qn_rope",
            "mla_kn_nope", "mla_kn_rope", "mix_w_out", "s5_lam_re", "s5_lam_im", "s5_log_dt", "s5_b_re", "s5_b_im",
            "s5_c_re", "s5_c_im", "s5_d", "s5_w_glu_a", "s5_w_glu_b")
_BIG = tuple(n for n in _WEIGHTS if n in _SHARDED and n not in _EXACT)
_SHARD_ORDER = tuple(n for n in _WEIGHTS if n in _SHARDED)
_REPL_ORDER = tuple(n for n in _WEIGHTS if n not in _SHARDED)
_REPL_EARLY = tuple(n for n in _REPL_ORDER if n.startswith("s5_"))
_REPL_LATE = tuple(n for n in _REPL_ORDER if n not in _REPL_EARLY)


def _pack(parts, dtype, lead=None):
    nl = 0 if lead is None else 1
    flat = [a.astype(dtype).reshape(a.shape[:nl] + (-1,)) for a in parts]
    cat = jnp.concatenate(flat, axis=nl)
    n = cat.shape[nl]
    unit = _LANES * _ROW_PAD
    total = -(-n // unit) * unit
    cat = jnp.pad(cat, [(0, 0)] * nl + [(0, total - n)])
    return cat.reshape(cat.shape[:nl] + (total // _LANES, _LANES))


def _unpack(packed, shapes, lead=None):
    nl = 0 if lead is None else 1
    flat = packed.reshape(packed.shape[:nl] + (-1,))
    out, off = [], 0
    for s in shapes:
        n = int(np.prod(s))
        piece = flat[..., off:off + n] if nl else flat[off:off + n]
        out.append(piece.reshape(packed.shape[:nl] + tuple(s)))
        off += n
    return out


def _to_full(gathered, axis):
    g = jnp.moveaxis(gathered, 0, axis)
    s = g.shape
    return g.reshape(s[:axis] + (s[axis] * s[axis + 1],) + s[axis + 2:])


def _to_shards(full, axis):
    s = full.shape
    g = full.reshape(s[:axis] + (N_DEV, s[axis] // N_DEV) + s[axis + 1:])
    return jnp.moveaxis(g, axis, 0)


_DIRECT_ROWS = ("xa_wq", "xa_wk", "xa_wv", "xa_wo", "mix_w_out", "s5_w_glu_a", "s5_w_glu_b")
_SMALL16 = ("mix_w_in", "mla_w_uq", "mla_w_ukv")
_SMALL_SHARDED = ("mla_w_uq", "mla_w_ukv") + _EXACT
_SHARD_ROWS = 128


def _exchange_layout(d):
    out = dict(d)
    out["ffn_w_up"] = _pad_groups(d["ffn_w_up"], 2)
    out["ffn_conv_w"] = _pad_groups(d["ffn_conv_w"], 2)
    out["ffn_w_down"] = _pad_groups(d["ffn_w_down"], 1)
    return out


def _train_step(x, mem, positions, target, w, m, v):
    d_model = x.shape[1]
    we_, me_, ve_ = _exchange_layout(w), _exchange_layout(m), _exchange_layout(v)
    sds = jax.ShapeDtypeStruct

    matrices = _DIRECT_ROWS + ("ffn_w_up", "ffn_w_down")
    layer_mats = ("xa_wq", "xa_wk", "xa_wv", "xa_wo", "ffn_w_up", "ffn_w_down")
    shard16 = {n: we_[n].astype(bf16) for n in matrices}
    part_of = {n: _rows_of(_SHARD_ROWS) for n in _DIRECT_ROWS}
    part_of["ffn_w_up"] = _cols_of(we_["ffn_w_up"].shape[2])
    part_of["ffn_w_down"] = _rows_of(we_["ffn_w_down"].shape[1])
    part_shape = {n: we_[n].shape[1:] for n in matrices}

    def full_shape(n):
        r, c = part_shape[n]
        return (r, N_DEV * c) if n == "ffn_w_up" else (N_DEV * r, c)

    def gather(ex, n, layer):
        ex.add(shard16[n], sds(full_shape(n), bf16), _at_layer(layer), part_of[n], (n, layer))

    def scatter(ex, n, layer, grad):
        ex.add(grad, sds((N_DEV,) + part_shape[n], f32), part_of[n], _slot, (n, layer))

    small16 = _pack([we_[n] for n in _SMALL16], bf16)
    exact = _pack([we_[n] for n in _EXACT], f32)
    ga, gb, gc = _Exchange("gather_a"), _Exchange("gather_b"), _Exchange("gather_c")
    ga.add(small16, sds((N_DEV,) + small16.shape, bf16), _whole, _slot, "small16")
    ga.add(exact, sds((N_DEV,) + exact.shape, f32), _whole, _slot, "exact")
    gather(ga, "mix_w_out", 0)
    for n in layer_mats:
        gather(gb, n, 0)
    gather(gc, "s5_w_glu_a", 0)
    gather(gc, "s5_w_glu_b", 0)
    for n in layer_mats:
        gather(gc, n, 1)
    state_a, state_b, state_c = ga.begin(True), gb.begin(True), gc.begin(True)

    full = ga.finish(state_a, [gb.token, gc.token])
    p = {n: w[n] for n in _REPL_ORDER}
    for n, a in zip(_SMALL16, _unpack(full["small16"], [we_[n].shape for n in _SMALL16], lead=True)):
        p[n] = _to_full(a, _SHARDED[n])
    conv_w, p["s5_d"] = [_to_full(a, _SHARDED[n]) for n, a in
                         zip(_EXACT, _unpack(full["exact"], [we_[n].shape for n in _EXACT], lead=True))]
    p["mix_w_out"] = full[("mix_w_out", 0)][None]
    cos, sin = _rope_tables(positions)
    we = _even_weights(p, 0, 0, bf16)
    we["norm_mix"] = _after(we["norm_mix"], ga.token, gb.token, gc.token, name="after_gather_starts")
    conv_b = _pad_groups(w["ffn_conv_b"], 1)

    def layer_weights(layer):
        return dict(norm_xa=w["norm_xa"][layer][None], norm_mem=w["norm_mem"][layer][None],
                    norm_ffn=w["norm_ffn"][layer][None], xa_q_norm=w["xa_q_norm"][layer][None],
                    xa_k_norm=w["xa_k_norm"][layer][None], ffn_conv_w=conv_w[layer],
                    ffn_conv_b=conv_b[layer][None], **{n: full[(n, layer)] for n in layer_mats})

    h, s_mix0 = _mixer_fwd(x, cos, sin, we, "l0")
    full.update(gb.finish(state_b, h))
    wl = [layer_weights(0)]
    h, s_xa0 = _xattn_fwd(h, mem, wl[0], "l0")
    h, s_ff0 = _ffn_fwd(h, wl[0], "l0")
    full.update(gc.finish(state_c, h))
    wl.append(layer_weights(1))
    p["s5_w_glu_a"], p["s5_w_glu_b"] = full[("s5_w_glu_a", 0)][None], full[("s5_w_glu_b", 0)][None]
    wo = _odd_weights(p, 0, 1, bf16)
    h, s_mix1 = _s5_fwd(h, wo, "l1")
    h, s_xa1 = _xattn_fwd(h, mem, wl[1], "l1")
    h, s_ff1 = _ffn_fwd(h, wl[1], "l1")
    dh, loss = _rows(_loss_fn, [h, target], [], [(h.shape[1], f32)], accs=[(1, 1)], name="loss_head")

    gl = [{}, {}]
    reduces = []

    own_grad = {}

    def reduce_start(name, entries, dh):
        ex = _Exchange(name)
        for n, layer, grad in entries.get("matrices", ()):
            scatter(ex, n, layer, grad)
            own_grad[(n, layer)] = grad
        for key, src, shape, src_at in entries.get("packs", ()):
            ex.add(src, shape, src_at, _slot, key)
        reduces.append((ex, ex.begin(False)))
        return _after(dh, ex.token, name="after_" + name)

    dh, gl[1] = _ffn_bwd(dh, wl[1], s_ff1, "l1")
    dh = reduce_start("reduce_ffn1", dict(matrices=[(n, 1, gl[1][n]) for n in ("ffn_w_up", "ffn_w_down")]), dh)
    dh, g = _xattn_bwd(dh, mem, wl[1], s_xa1, "l1")
    gl[1].update(g)
    dh = reduce_start("reduce_xa1", dict(matrices=[(n, 1, g[n]) for n in ("xa_wq", "xa_wk", "xa_wv", "xa_wo")]), dh)
    dh, g_odd = _s5_bwd(dh, wo, s_mix1, "l1")
    go = _odd_grads(g_odd)
    dh, gl[0] = _ffn_bwd(dh, wl[0], s_ff0, "l0")
    send_early = _pack([go[n].reshape(w[n].shape) for n in _REPL_EARLY], f32)
    dh = reduce_start("reduce_ffn0", dict(
        matrices=[(n, 0, g_odd[n]) for n in ("s5_w_glu_a", "s5_w_glu_b")]
        + [(n, 0, gl[0][n]) for n in ("ffn_w_up", "ffn_w_down")],
        packs=[("repl_early", send_early, sds((N_DEV,) + send_early.shape, f32), _whole)]), dh)
    dh, g = _xattn_bwd(dh, mem, wl[0], s_xa0, "l0")
    gl[0].update(g)
    dh = reduce_start("reduce_xa0", dict(matrices=[(n, 0, g[n]) for n in ("xa_wq", "xa_wk", "xa_wv", "xa_wo")]), dh)
    grad_x, g_even = _mixer_bwd(
        dh, cos, sin, we, s_mix0, "l0",
        on_w_out=lambda grad, dmixin: reduce_start("reduce_w_out", dict(matrices=[("mix_w_out", 0, grad)]), dmixin))

    ge = _even_grads(g_even)
    cat = lambda n: jnp.concatenate([gl[0][n], gl[1][n]], axis=0)
    rg = dict(ge)
    rg["norm_mix"] = jnp.concatenate([ge["norm_mix"], go["norm_mix"]], axis=0)
    for n in ("norm_xa", "norm_mem", "norm_ffn", "xa_q_norm", "xa_k_norm"):
        rg[n] = cat(n)
    rg["ffn_conv_b"] = _unpad_groups(cat("ffn_conv_b"), 1)
    sg = dict(mla_w_uq=ge["mla_w_uq"], mla_w_ukv=ge["mla_w_ukv"], s5_d=go["s5_d"],
              ffn_conv_w=jnp.stack([gl[0]["ffn_conv_w"], gl[1]["ffn_conv_w"]]))
    send_small = _pack([_to_shards(sg[n], _SHARDED[n]) for n in _SMALL_SHARDED], f32, lead=True)
    send_late = _pack([rg[n].reshape(w[n].shape) for n in _REPL_LATE], f32)
    w_in_rows = w["mix_w_in"].shape[2]
    last = _Exchange("reduce_last")
    last.add(g_even["mix_w_in_t"], sds((N_DEV, w_in_rows, d_model), f32), _rows_of(w_in_rows), _slot, "mix_w_in")
    last.add(send_small, sds(send_small.shape, f32), _slot, _slot, "small")
    last.add(send_late, sds((N_DEV,) + send_late.shape, f32), _whole, _slot, "repl_late")
    state_last = last.begin(False)
    slots = {}
    for ex, state in reduces:
        slots.update(ex.finish(state, [grad_x, last.token]))

    me_index = (4 * lax.axis_index("x") + 2 * lax.axis_index("y") + lax.axis_index("c")).astype(jnp.int32).reshape(1)

    def rows_block(r, c):
        return lambda tr: ((tr, c), lambda i, me: (me[0] * (r // tr) + i, 0))

    def own_block(n):
        r, c = part_shape[n]
        if n == "ffn_w_up":
            return lambda tr: ((tr, c), lambda i, me: (i, me[0]))
        return rows_block(r, c)

    out = [{}, {}, {}, {}]
    unpad = dict(ffn_w_up=2, ffn_conv_w=2, ffn_w_down=1)
    for n in matrices:
        layers = range(we_[n].shape[0])
        res = _sum_adam(me_index, [slots[(n, layer)] for layer in layers], [own_grad[(n, layer)] for layer in layers],
                        own_block(n), we_[n], me_[n], ve_[n], name=f"adam_{n}")
        for k in range(4):
            out[k][n] = _unpad_groups(res[k], unpad[n]) if n in unpad else res[k]
    pk = lambda d, order: _pack([d[n] for n in order], f32)[None]
    whole_rows = lambda tr: ((tr, _LANES), lambda i, me: (i, 0))
    res_early = _sum_adam(me_index, [slots["repl_early"]], [send_early], whole_rows, pk(w, _REPL_EARLY),
                          pk(m, _REPL_EARLY), pk(v, _REPL_EARLY), name="adam_repl_early")
    for k in range(4):
        out[k].update(zip(_REPL_EARLY, _unpack(res_early[k][0], [w[n].shape for n in _REPL_EARLY])))
    done = [out[k][n] for k in range(4) for n in matrices + _REPL_EARLY]
    slots = last.finish(state_last, done)
    transposed = lambda d: jnp.swapaxes(d["mix_w_in"], 1, 2)
    res_w_in = _sum_adam(me_index, [slots["mix_w_in"]], [g_even["mix_w_in_t"]], rows_block(w_in_rows, d_model),
                         transposed(w), transposed(m), transposed(v), name="adam_mix_w_in")
    res_small = _sum_adam(me_index, [slots["small"]], [send_small],
                          lambda tr: ((1, tr, _LANES), lambda i, me: (me[0], i, 0)),
                          pk(we_, _SMALL_SHARDED), pk(me_, _SMALL_SHARDED), pk(ve_, _SMALL_SHARDED), name="adam_small")
    res_late = _sum_adam(me_index, [slots["repl_late"]], [send_late], whole_rows, pk(w, _REPL_LATE), pk(m, _REPL_LATE),
                         pk(v, _REPL_LATE), name="adam_repl_late")
    for k in range(4):
        out[k]["mix_w_in"] = jnp.swapaxes(res_w_in[k], 1, 2)
        for n, a in zip(_SMALL_SHARDED, _unpack(res_small[k][0], [we_[n].shape for n in _SMALL_SHARDED])):
            out[k][n] = _unpad_groups(a, unpad[n]) if n in unpad else a
        out[k].update(zip(_REPL_LATE, _unpack(res_late[k][0], [w[n].shape for n in _REPL_LATE])))
    return loss, grad_x, out


_INPUTS = tuple("""x, mem, positions, norm_mix, norm_xa, norm_mem, norm_ffn, xa_wq, xa_wk, xa_wv, xa_wo, xa_q_norm, xa_k_norm, ffn_w_up, ffn_conv_w, ffn_conv_b, ffn_w_down, hg_lb_logits, mix_w_in, hg_out_norm, mla_q_a_norm, mla_w_uq, mla_kv_a_norm, mla_w_ukv, mla_qn_nope, mla_qn_rope, mla_kn_nope, mla_kn_rope, mix_w_out, s5_lam_re, s5_lam_im, s5_log_dt, s5_b_re, s5_b_im, s5_c_re, s5_c_im, s5_d, s5_w_glu_a, s5_w_glu_b, loss_target, m_norm_mix, m_norm_xa, m_norm_mem, m_norm_ffn, m_xa_wq, m_xa_wk, m_xa_wv, m_xa_wo, m_xa_q_norm, m_xa_k_norm, m_ffn_w_up, m_ffn_conv_w, m_ffn_conv_b, m_ffn_w_down, m_hg_lb_logits, m_mix_w_in, m_hg_out_norm, m_mla_q_a_norm, m_mla_w_uq, m_mla_kv_a_norm, m_mla_w_ukv, m_mla_qn_nope, m_mla_qn_rope, m_mla_kn_nope, m_mla_kn_rope, m_mix_w_out, m_s5_lam_re, m_s5_lam_im, m_s5_log_dt, m_s5_b_re, m_s5_b_im, m_s5_c_re, m_s5_c_im, m_s5_d, m_s5_w_glu_a, m_s5_w_glu_b, v_norm_mix, v_norm_xa, v_norm_mem, v_norm_ffn, v_xa_wq, v_xa_wk, v_xa_wv, v_xa_wo, v_xa_q_norm, v_xa_k_norm, v_ffn_w_up, v_ffn_conv_w, v_ffn_conv_b, v_ffn_w_down, v_hg_lb_logits, v_mix_w_in, v_hg_out_norm, v_mla_q_a_norm, v_mla_w_uq, v_mla_kv_a_norm, v_mla_w_ukv, v_mla_qn_nope, v_mla_qn_rope, v_mla_kn_nope, v_mla_kn_rope, v_mix_w_out, v_s5_lam_re, v_s5_lam_im, v_s5_log_dt, v_s5_b_re, v_s5_b_im, v_s5_c_re, v_s5_c_im, v_s5_d, v_s5_w_glu_a, v_s5_w_glu_b""".replace(" ", "").split(","))


def kernel(x, mem, positions, norm_mix, norm_xa, norm_mem, norm_ffn, xa_wq, xa_wk, xa_wv, xa_wo, xa_q_norm, xa_k_norm, ffn_w_up, ffn_conv_w, ffn_conv_b, ffn_w_down, hg_lb_logits, mix_w_in, hg_out_norm, mla_q_a_norm, mla_w_uq, mla_kv_a_norm, mla_w_ukv, mla_qn_nope, mla_qn_rope, mla_kn_nope, mla_kn_rope, mix_w_out, s5_lam_re, s5_lam_im, s5_log_dt, s5_b_re, s5_b_im, s5_c_re, s5_c_im, s5_d, s5_w_glu_a, s5_w_glu_b, loss_target, m_norm_mix, m_norm_xa, m_norm_mem, m_norm_ffn, m_xa_wq, m_xa_wk, m_xa_wv, m_xa_wo, m_xa_q_norm, m_xa_k_norm, m_ffn_w_up, m_ffn_conv_w, m_ffn_conv_b, m_ffn_w_down, m_hg_lb_logits, m_mix_w_in, m_hg_out_norm, m_mla_q_a_norm, m_mla_w_uq, m_mla_kv_a_norm, m_mla_w_ukv, m_mla_qn_nope, m_mla_qn_rope, m_mla_kn_nope, m_mla_kn_rope, m_mix_w_out, m_s5_lam_re, m_s5_lam_im, m_s5_log_dt, m_s5_b_re, m_s5_b_im, m_s5_c_re, m_s5_c_im, m_s5_d, m_s5_w_glu_a, m_s5_w_glu_b, v_norm_mix, v_norm_xa, v_norm_mem, v_norm_ffn, v_xa_wq, v_xa_wk, v_xa_wv, v_xa_wo, v_xa_q_norm, v_xa_k_norm, v_ffn_w_up, v_ffn_conv_w, v_ffn_conv_b, v_ffn_w_down, v_hg_lb_logits, v_mix_w_in, v_hg_out_norm, v_mla_q_a_norm, v_mla_w_uq, v_mla_kv_a_norm, v_mla_w_ukv, v_mla_qn_nope, v_mla_qn_rope, v_mla_kn_nope, v_mla_kn_rope, v_mix_w_out, v_s5_lam_re, v_s5_lam_im, v_s5_log_dt, v_s5_b_re, v_s5_b_im, v_s5_c_re, v_s5_c_im, v_s5_d, v_s5_w_glu_a, v_s5_w_glu_b):
    vals = dict(zip(_INPUTS, (x, mem, positions, norm_mix, norm_xa, norm_mem, norm_ffn, xa_wq, xa_wk, xa_wv, xa_wo, xa_q_norm, xa_k_norm, ffn_w_up, ffn_conv_w, ffn_conv_b, ffn_w_down, hg_lb_logits, mix_w_in, hg_out_norm, mla_q_a_norm, mla_w_uq, mla_kv_a_norm, mla_w_ukv, mla_qn_nope, mla_qn_rope, mla_kn_nope, mla_kn_rope, mix_w_out, s5_lam_re, s5_lam_im, s5_log_dt, s5_b_re, s5_b_im, s5_c_re, s5_c_im, s5_d, s5_w_glu_a, s5_w_glu_b, loss_target, m_norm_mix, m_norm_xa, m_norm_mem, m_norm_ffn, m_xa_wq, m_xa_wk, m_xa_wv, m_xa_wo, m_xa_q_norm, m_xa_k_norm, m_ffn_w_up, m_ffn_conv_w, m_ffn_conv_b, m_ffn_w_down, m_hg_lb_logits, m_mix_w_in, m_hg_out_norm, m_mla_q_a_norm, m_mla_w_uq, m_mla_kv_a_norm, m_mla_w_ukv, m_mla_qn_nope, m_mla_qn_rope, m_mla_kn_nope, m_mla_kn_rope, m_mix_w_out, m_s5_lam_re, m_s5_lam_im, m_s5_log_dt, m_s5_b_re, m_s5_b_im, m_s5_c_re, m_s5_c_im, m_s5_d, m_s5_w_glu_a, m_s5_w_glu_b, v_norm_mix, v_norm_xa, v_norm_mem, v_norm_ffn, v_xa_wq, v_xa_wk, v_xa_wv, v_xa_wo, v_xa_q_norm, v_xa_k_norm, v_ffn_w_up, v_ffn_conv_w, v_ffn_conv_b, v_ffn_w_down, v_hg_lb_logits, v_mix_w_in, v_hg_out_norm, v_mla_q_a_norm, v_mla_w_uq, v_mla_kv_a_norm, v_mla_w_ukv, v_mla_qn_nope, v_mla_qn_rope, v_mla_kn_nope, v_mla_kn_rope, v_mix_w_out, v_s5_lam_re, v_s5_lam_im, v_s5_log_dt, v_s5_b_re, v_s5_b_im, v_s5_c_re, v_s5_c_im, v_s5_d, v_s5_w_glu_a, v_s5_w_glu_b)))
    w = {n: vals[n] for n in _WEIGHTS}
    m = {n: vals["m_" + n] for n in _WEIGHTS}
    v = {n: vals["v_" + n] for n in _WEIGHTS}
    loss, grad_x, res = _train_step(vals["x"][0], vals["mem"][0], vals["positions"][0], vals["loss_target"][0],
                                    w, m, v)
    loss = lax.psum(loss[0, 0], ("x", "y", "c"))
    return (loss, grad_x[None], *[r[n] for r in res for n in _WEIGHTS])
```

```python
import functools

import jax
import jax.numpy as jnp
import numpy as np
from jax import lax
from jax.experimental import pallas as pl
from jax.experimental.pallas import tpu as pltpu

f32 = jnp.float32
bf16 = jnp.bfloat16

EPS = 1e-6
N_DEV = 8
VMEM_LIMIT = 52 * 1024 * 1024

HG_HEADS = 4
HG_DIM = 128
HG_WIDTH = HG_HEADS * HG_DIM
HG_CHUNK = 64
HG_SUB = 16
MLA_HEADS = 4
MLA_Q_RANK = 256
MLA_KV_RANK = 128
MLA_NOPE = 128
MLA_ROPE = 64
MLA_V = 128
MLA_QK = MLA_NOPE + MLA_ROPE
MLA_QK_PAD = 256
ROPE_BASE = 10000.0
IN_WIDTH = 4 * HG_WIDTH + MLA_Q_RANK + MLA_KV_RANK + MLA_ROPE
IN_PAD = 2560
XA_HEADS = 4
XA_DIM = 256
S5_GROUP = 16
S5_GROUPS = 64
S5_STATE = 64
CONV_W = 3

ADAM_LR = 0.001
ADAM_B1 = 0.9
ADAM_B2 = 0.999
ADAM_EPS = 1e-08
ADAM_WD = 0.01
ADAM_STEP = 10

_NT = (((1,), (1,)), ((), ()))
_TN = (((0,), (0,)), ((), ()))
_NN = (((1,), (0,)), ((), ()))


def _pick(n, cands):
    for c in cands:
        if n % c == 0:
            return c
    return n


def _cparams(sem):
    return pltpu.CompilerParams(dimension_semantics=sem, vmem_limit_bytes=VMEM_LIMIT)


_MM_BUDGET = 36 * 1024 * 1024
_MM_TILES = ((1024, 1024), (1024, 512), (512, 1024), (512, 512), (512, 256), (256, 512), (256, 256), (256, 128),
             (128, 256), (128, 128))


def _mm(a, b, *, name, ta=False, tb=False, out_dtype=f32, add=None, b2=None, kslab=None):
    m, k = (a.shape[1], a.shape[0]) if ta else a.shape
    nb = b.shape[0] if tb else b.shape[1]
    n = nb * (2 if b2 is not None else 1)
    slab, nslab = kslab if kslab is not None else (0, 1)
    assert (b.shape[1] // nslab if tb else b.shape[0]) == k, (a.shape, b.shape, ta, tb)
    assert b2 is None or (not tb and b2.shape == b.shape)
    isz = lambda x: jnp.dtype(x.dtype).itemsize
    bm = bn = None
    for cm, cn in _MM_TILES:
        if m % cm or nb % cn:
            continue
        need = 2 * (cm * k * isz(a) + cn * k * isz(b) * (2 if b2 is not None else 1)
                    + cm * cn * (jnp.dtype(out_dtype).itemsize + (4 if add is not None else 0)))
        if need <= _MM_BUDGET:
            bm, bn = cm, cn
            break
    assert bm is not None, (name, a.shape, b.shape)
    half = nb // bn
    dims = (((0 if ta else 1,), (1 if tb else 0,)), ((), ()))

    def body(*refs):
        refs = list(refs)
        a_ref, b_ref = refs[0], refs[1]
        b2_ref = refs.pop(2) if b2 is not None else None
        add_ref = refs[2] if add is not None else None
        o_ref = refs[-1]

        def run(rhs_ref):
            r = lax.dot_general(a_ref[...].astype(bf16), rhs_ref[...].astype(bf16), dims, preferred_element_type=f32)
            if add_ref is not None:
                r = r + add_ref[...].astype(f32)
            o_ref[...] = r.astype(o_ref.dtype)

        if b2_ref is None:
            run(b_ref)
        else:
            pl.when(pl.program_id(1) < half)(lambda: run(b_ref))
            pl.when(pl.program_id(1) >= half)(lambda: run(b2_ref))

    a_spec = pl.BlockSpec((k, bm), lambda i, j: (0, i)) if ta else pl.BlockSpec((bm, k), lambda i, j: (i, 0))
    if tb:
        b_spec = pl.BlockSpec((bn, k), lambda i, j: (j, slab))
    elif b2 is None:
        b_spec = pl.BlockSpec((k, bn), lambda i, j: (0, j))
    else:
        b_spec = pl.BlockSpec((k, bn), lambda i, j: (0, jnp.minimum(j, half - 1)))
    in_specs = [a_spec, b_spec]
    args = [a, b]
    if b2 is not None:
        in_specs.append(pl.BlockSpec((k, bn), lambda i, j: (0, jnp.maximum(j - half, 0))))
        args.append(b2)
    if add is not None:
        in_specs.append(pl.BlockSpec((bm, bn), lambda i, j: (i, j)))
        args.append(add)
    return pl.pallas_call(
        body, grid=(m // bm, n // bn), in_specs=in_specs,
        out_specs=pl.BlockSpec((bm, bn), lambda i, j: (i, j)),
        out_shape=jax.ShapeDtypeStruct((m, n), out_dtype),
        compiler_params=_cparams(("parallel", "parallel")), name=name)(*args)


def _as_tuple(x):
    return tuple(x) if isinstance(x, (tuple, list)) else (x,)


def _full_spec(p):
    nd = p.ndim
    return pl.BlockSpec(p.shape, lambda i, _nd=nd: (0,) * _nd)


def _rows(fn, rows, params, outs, *, name, tile=256, accs=()):
    length = rows[0].shape[0]
    tile = min(tile, length)
    nr, npar, no = len(rows), len(params), len(outs)

    def body(*refs):
        r, p, o = refs[:nr], refs[nr:nr + npar], refs[nr + npar:]
        res = _as_tuple(fn(*[x[...].astype(f32) for x in r], *[x[...] for x in p]))
        for kk in range(no):
            o[kk][...] = res[kk].astype(o[kk].dtype)
        if accs:
            @pl.when(pl.program_id(0) == 0)
            def _():
                for kk in range(no, no + len(accs)):
                    o[kk][...] = jnp.zeros_like(o[kk])
            for kk in range(no, no + len(accs)):
                o[kk][...] += res[kk]

    in_specs = [pl.BlockSpec((tile, x.shape[1]), lambda i: (i, 0)) for x in rows] + [_full_spec(p) for p in params]
    out_specs = [pl.BlockSpec((tile, w), lambda i: (i, 0)) for w, _ in outs]
    out_shape = [jax.ShapeDtypeStruct((length, w), d) for w, d in outs]
    for s in accs:
        out_specs.append(pl.BlockSpec(s, lambda i, _nd=len(s): (0,) * _nd))
        out_shape.append(jax.ShapeDtypeStruct(s, f32))
    res = pl.pallas_call(body, grid=(length // tile,), in_specs=in_specs, out_specs=out_specs, out_shape=out_shape,
                         compiler_params=_cparams(("arbitrary",)), name=name)(*rows, *params)
    return res


def _rows_bwd(fn, rows, params, cts, *, name, rgrad, pgrad, tile=256, addends=None):
    addends = addends or {}
    length = rows[0].shape[0]
    tile = min(tile, length)
    nr, npar, nc = len(rows), len(params), len(cts)
    ridx = [i for i in range(nr) if rgrad[i] is not None]
    pidx = [i for i in range(npar) if pgrad[i]]
    aidx = sorted(addends)
    na = len(aidx)

    def body(*refs):
        r, p, c = refs[:nr], refs[nr:nr + npar], refs[nr + npar:nr + npar + nc]
        ad = refs[nr + npar + nc:nr + npar + nc + na]
        o = refs[nr + npar + nc + na:]
        rv = [x[...].astype(f32) for x in r]
        pv = [x[...] for x in p]
        cv = tuple(x[...].astype(f32) for x in c)

        def g(*d):
            rr, pp = list(rv), list(pv)
            for n_, i_ in enumerate(ridx):
                rr[i_] = d[n_]
            for n_, i_ in enumerate(pidx):
                pp[i_] = d[len(ridx) + n_]
            return _as_tuple(fn(*rr, *pp))

        _, vjp = jax.vjp(g, *[rv[i] for i in ridx], *[pv[i] for i in pidx])
        grads = vjp(cv)
        for n_, i_ in enumerate(ridx):
            val = grads[n_]
            if i_ in addends:
                val = val + ad[aidx.index(i_)][...].astype(f32)
            o[n_][...] = val.astype(o[n_].dtype)
        if pidx:
            @pl.when(pl.program_id(0) == 0)
            def _():
                for n_ in range(len(pidx)):
                    o[len(ridx) + n_][...] = jnp.zeros_like(o[len(ridx) + n_])
            for n_ in range(len(pidx)):
                o[len(ridx) + n_][...] += grads[len(ridx) + n_]

    row_spec = lambda x: pl.BlockSpec((tile, x.shape[1]), lambda i: (i, 0))
    in_specs = ([row_spec(x) for x in rows] + [_full_spec(p) for p in params] + [row_spec(x) for x in cts]
                + [row_spec(addends[i]) for i in aidx])
    out_specs = [row_spec(rows[i]) for i in ridx] + [_full_spec(params[i]) for i in pidx]
    out_shape = ([jax.ShapeDtypeStruct(rows[i].shape, rgrad[i]) for i in ridx]
                 + [jax.ShapeDtypeStruct(params[i].shape, f32) for i in pidx])
    res = pl.pallas_call(body, grid=(length // tile,), in_specs=in_specs, out_specs=out_specs, out_shape=out_shape,
                         compiler_params=_cparams(("arbitrary",)), name=name)(
        *rows, *params, *cts, *[addends[i] for i in aidx])
    return list(res[:len(ridx)]), list(res[len(ridx):])


def _rms(x, g):
    return x * lax.rsqrt(jnp.mean(x * x, axis=-1, keepdims=True) + EPS) * g


def _rms_twice(x, g):
    y = _rms(x, g)
    return y, y


def _silu(x):
    return x * jax.nn.sigmoid(x)


def _shift_down(x, s):
    rows = lax.broadcasted_iota(jnp.int32, x.shape, 0)
    return jnp.where(rows >= s, pltpu.roll(x, s, axis=0), 0.0)


def _shift_up(x, s):
    n = x.shape[0]
    rows = lax.broadcasted_iota(jnp.int32, x.shape, 0)
    return jnp.where(rows < n - s, pltpu.roll(x, n - s, axis=0), 0.0)


_CONV_COLS = 128


def _conv_gate_fwd(u, cw, cb, *, name):
    length, two_f = u.shape
    ff = two_f // 2
    nb = ff // _CONV_COLS

    def body(ug, uv, wg, wv, bg, bv, o):
        def conv(x_ref, w_ref, b_ref):
            x = x_ref[...].astype(f32)
            return (w_ref[2:3, :] * x + w_ref[1:2, :] * _shift_down(x, 1) + w_ref[0:1, :] * _shift_down(x, 2)
                    + b_ref[...])
        o[...] = (_silu(conv(ug, wg, bg)) * conv(uv, wv, bv)).astype(o.dtype)

    blk = lambda r, off: pl.BlockSpec((r, _CONV_COLS), lambda j, _o=off: (0, j + _o))
    return pl.pallas_call(
        body, grid=(nb,),
        in_specs=[blk(length, 0), blk(length, nb), blk(CONV_W, 0), blk(CONV_W, nb), blk(1, 0), blk(1, nb)],
        out_specs=blk(length, 0), out_shape=jax.ShapeDtypeStruct((length, ff), bf16),
        compiler_params=_cparams(("parallel",)), name=name)(u, u, cw, cw, cb, cb)


def _conv_gate_bwd(u, cw, cb, da, *, name):
    length, two_f = u.shape
    ff = two_f // 2
    nb = ff // _CONV_COLS

    def body(ug, uv, wg, wv, bg, bv, da_ref, dug, duv, dwg, dwv, dbg, dbv):
        def conv(x, w_ref, b_ref):
            x1, x2 = _shift_down(x, 1), _shift_down(x, 2)
            return w_ref[2:3, :] * x + w_ref[1:2, :] * x1 + w_ref[0:1, :] * x2 + b_ref[...], x1, x2

        xg, xv = ug[...].astype(f32), uv[...].astype(f32)
        g, xg1, xg2 = conv(xg, wg, bg)
        v, xv1, xv2 = conv(xv, wv, bv)
        d = da_ref[...].astype(f32)
        sg = jax.nn.sigmoid(g)
        dg = d * v * (sg * (1.0 + g * (1.0 - sg)))
        dv = d * (g * sg)

        def back(dy, x, x1, x2, w_ref, du_ref, dw_ref, db_ref):
            du_ref[...] = (w_ref[2:3, :] * dy + w_ref[1:2, :] * _shift_up(dy, 1)
                           + w_ref[0:1, :] * _shift_up(dy, 2)).astype(du_ref.dtype)
            dw_ref[2:3, :] = jnp.sum(dy * x, axis=0, keepdims=True)
            dw_ref[1:2, :] = jnp.sum(dy * x1, axis=0, keepdims=True)
            dw_ref[0:1, :] = jnp.sum(dy * x2, axis=0, keepdims=True)
            db_ref[...] = jnp.sum(dy, axis=0, keepdims=True)

        back(dg, xg, xg1, xg2, wg, dug, dwg, dbg)
        back(dv, xv, xv1, xv2, wv, duv, dwv, dbv)

    blk = lambda r, off: pl.BlockSpec((r, _CONV_COLS), lambda j, _o=off: (0, j + _o))
    sds = jax.ShapeDtypeStruct
    dug, duv, dwg, dwv, dbg, dbv = pl.pallas_call(
        body, grid=(nb,),
        in_specs=[blk(length, 0), blk(length, nb), blk(CONV_W, 0), blk(CONV_W, nb), blk(1, 0), blk(1, nb),
                  blk(length, 0)],
        out_specs=[blk(length, 0), blk(length, 0), blk(CONV_W, 0), blk(CONV_W, 0), blk(1, 0), blk(1, 0)],
        out_shape=[sds((length, ff), bf16), sds((length, ff), bf16), sds((CONV_W, ff), f32), sds((CONV_W, ff), f32),
                   sds((1, ff), f32), sds((1, ff), f32)],
        compiler_params=_cparams(("parallel",)), name=name)(u, u, cw, cw, cb, cb, da)
    return dug, duv, jnp.concatenate([dwg, dwv], axis=1), jnp.concatenate([dbg, dbv], axis=1)


def _ffn_fwd(h, w, tag):
    hf, = _rows(_rms, [h], [w["norm_ffn"]], [(h.shape[1], bf16)], name=f"ffn_norm_{tag}")
    u = _mm(hf, w["ffn_w_up"], out_dtype=bf16, name=f"ffn_up_{tag}")
    a = _conv_gate_fwd(u, w["ffn_conv_w"], w["ffn_conv_b"], name=f"ffn_conv_{tag}")
    out = _mm(a, w["ffn_w_down"], add=h, name=f"ffn_down_{tag}")
    return out, (h, hf, u, a)


def _ffn_bwd(dout, w, saved, tag):
    h, hf, u, a = saved
    ff = a.shape[1]
    da = _mm(dout, w["ffn_w_down"], tb=True, out_dtype=bf16, name=f"ffn_da_{tag}")
    g = {"ffn_w_down": _mm(a, dout, ta=True, name=f"ffn_dwdown_{tag}")}
    dug, duv, g["ffn_conv_w"], g["ffn_conv_b"] = _conv_gate_bwd(u, w["ffn_conv_w"], w["ffn_conv_b"], da,
                                                                name=f"ffn_dconv_{tag}")
    dhf = _mm(dug, w["ffn_w_up"], tb=True, kslab=(0, 2), name=f"ffn_dhf_g_{tag}")
    dhf = _mm(duv, w["ffn_w_up"], tb=True, kslab=(1, 2), add=dhf, out_dtype=bf16, name=f"ffn_dhf_v_{tag}")
    g["ffn_w_up"] = _mm(hf, dug, ta=True, b2=duv, name=f"ffn_dwup_{tag}")
    (dh,), (g["norm_ffn"],) = _rows_bwd(_rms, [h], [w["norm_ffn"]], [dhf], rgrad=[f32], pgrad=[True],
                                        addends={0: dout}, name=f"ffn_dnorm_{tag}")
    return dh, g


def _xattn_fn(qx, kx, vx, qg, kg):
    outs = []
    for hh in range(XA_HEADS):
        sl = slice(hh * XA_DIM, (hh + 1) * XA_DIM)
        q = _rms(qx[:, sl], qg).astype(bf16)
        k = _rms(kx[:, sl], kg).astype(bf16)
        s = lax.dot_general(q, k, _NT, preferred_element_type=f32) * (XA_DIM ** -0.5)
        s = s - jnp.max(s, axis=-1, keepdims=True)
        p = jnp.exp(s)
        p = p / jnp.sum(p, axis=-1, keepdims=True)
        outs.append(jnp.dot(p.astype(bf16), vx[:, sl].astype(bf16), preferred_element_type=f32))
    return jnp.concatenate(outs, axis=-1)


def _xattn_fwd(h, mem, w, tag):
    d = h.shape[1]
    hx, = _rows(_rms, [h], [w["norm_xa"]], [(d, bf16)], name=f"xa_norm_{tag}")
    qx = _mm(hx, w["xa_wq"], name=f"xa_q_{tag}")
    m, = _rows(_rms, [mem], [w["norm_mem"]], [(d, bf16)], name=f"xa_mnorm_{tag}")
    kx = _mm(m, w["xa_wk"], name=f"xa_k_{tag}")
    vx = _mm(m, w["xa_wv"], name=f"xa_v_{tag}")
    o, = _rows(_xattn_fn, [qx], [kx, vx, w["xa_q_norm"], w["xa_k_norm"]], [(d, bf16)], tile=512,
               name=f"xa_attn_{tag}")
    out = _mm(o, w["xa_wo"], add=h, name=f"xa_o_{tag}")
    return out, (h, hx, qx, m, kx, vx, o)


def _xattn_bwd(dout, mem, w, saved, tag):
    h, hx, qx, m, kx, vx, o = saved
    g = {}
    do = _mm(dout, w["xa_wo"], tb=True, out_dtype=bf16, name=f"xa_do_{tag}")
    g["xa_wo"] = _mm(o, dout, ta=True, name=f"xa_dwo_{tag}")
    (dqx,), (dkx, dvx, g["xa_q_norm"], g["xa_k_norm"]) = _rows_bwd(
        _xattn_fn, [qx], [kx, vx, w["xa_q_norm"], w["xa_k_norm"]], [do], rgrad=[bf16], pgrad=[True] * 4,
        tile=512, name=f"xa_dattn_{tag}")
    dhx = _mm(dqx, w["xa_wq"], tb=True, out_dtype=bf16, name=f"xa_dhx_{tag}")
    g["xa_wq"] = _mm(hx, dqx, ta=True, name=f"xa_dwq_{tag}")
    (dh,), (g["norm_xa"],) = _rows_bwd(_rms, [h], [w["norm_xa"]], [dhx], rgrad=[f32], pgrad=[True],
                                       addends={0: dout}, name=f"xa_dnorm_{tag}")
    dm = _mm(dkx, w["xa_wk"], tb=True, name=f"xa_dm_k_{tag}")
    dm = _mm(dvx, w["xa_wv"], tb=True, add=dm, name=f"xa_dm_v_{tag}")
    g["xa_wk"] = _mm(m, dkx, ta=True, name=f"xa_dwk_{tag}")
    g["xa_wv"] = _mm(m, dvx, ta=True, name=f"xa_dwv_{tag}")
    _, (g["norm_mem"],) = _rows_bwd(_rms, [mem], [w["norm_mem"]], [dm], rgrad=[None], pgrad=[True],
                                    name=f"xa_dmnorm_{tag}")
    return dh, g


_HG_GROUP = 4


def _hg_chunk(q, k, v, g, *sts):
    c = q.shape[0]
    heads = [slice(h * HG_DIM, (h + 1) * HG_DIM) for h in range(len(sts))]
    tri = (lax.broadcasted_iota(jnp.int32, (c, c), 0) >= lax.broadcasted_iota(jnp.int32, (c, c), 1)).astype(f32)
    b = jnp.dot(tri, g, precision=lax.Precision.HIGHEST, preferred_element_type=f32)
    bend = jnp.sum(g, axis=0, keepdims=True)
    qe = (q * jnp.exp(b)).astype(bf16)
    kd = (k * jnp.exp(bend - b)).astype(bf16)
    vb = v.astype(bf16)
    decay = jnp.exp(bend)
    o_inter = [lax.dot_general(qe[:, hs], st.astype(bf16), _NT, preferred_element_type=f32) for hs, st in zip(heads, sts)]
    new = [st * decay[:, hs] + lax.dot_general(vb[:, hs], kd[:, hs], _TN, preferred_element_type=f32)
           for hs, st in zip(heads, sts)]
    outs = []
    for i in range(c // HG_SUB):
        lo, n = HG_SUB * i, HG_SUB * (i + 1)
        ref = jnp.sum(g[:lo], axis=0, keepdims=True) if i else jnp.zeros((1, g.shape[1]), f32)
        qh = (q[lo:n] * jnp.exp(b[lo:n] - ref)).astype(bf16)
        kh = (k[:n] * jnp.exp(ref - b[:n])).astype(bf16)
        keep = (lax.broadcasted_iota(jnp.int32, (HG_SUB, n), 1)
                <= lo + lax.broadcasted_iota(jnp.int32, (HG_SUB, n), 0))
        scores = [lax.dot_general(qh[:, hs], kh[:, hs], _NT, preferred_element_type=f32) for hs in heads]
        scores = [jnp.where(keep, a, 0.0).astype(bf16) for a in scores]
        outs.append(jnp.concatenate([jnp.dot(a, vb[:n, hs], preferred_element_type=f32)
                                     for a, hs in zip(scores, heads)], axis=1))
    return (jnp.concatenate(outs, axis=0) + jnp.concatenate(o_inter, axis=1), *new)


def _hg_fwd(q, k, v, g, *, name):
    length = q.shape[0]
    rows = _HG_GROUP * HG_CHUNK
    ng = length // rows
    nc = length // HG_CHUNK

    def body(q_ref, k_ref, v_ref, g_ref, o_ref, st_ref, state):
        @pl.when(pl.program_id(0) == 0)
        def _():
            state[...] = jnp.zeros_like(state)

        states = [state[h] for h in range(HG_HEADS)]
        for ci in range(_HG_GROUP):
            sl = slice(ci * HG_CHUNK, (ci + 1) * HG_CHUNK)
            for h in range(HG_HEADS):
                st_ref[h, ci] = states[h]
            o, *states = _hg_chunk(q_ref[sl, :], k_ref[sl, :], v_ref[sl, :], g_ref[sl, :], *states)
            o_ref[sl, :] = o
        for h in range(HG_HEADS):
            state[h] = states[h]

    blk = pl.BlockSpec((rows, HG_WIDTH), lambda c: (c, 0))
    return pl.pallas_call(
        body, grid=(ng,), in_specs=[blk] * 4,
        out_specs=[blk, pl.BlockSpec((HG_HEADS, _HG_GROUP, HG_DIM, HG_DIM), lambda c: (0, c, 0, 0))],
        out_shape=[jax.ShapeDtypeStruct((length, HG_WIDTH), f32),
                   jax.ShapeDtypeStruct((HG_HEADS, nc, HG_DIM, HG_DIM), f32)],
        scratch_shapes=[pltpu.VMEM((HG_HEADS, HG_DIM, HG_DIM), f32)],
        compiler_params=_cparams(("arbitrary",)), name=name)(q, k, v, g)


def _hg_bwd(q, k, v, g, states, do, *, name):
    length = q.shape[0]
    rows = _HG_GROUP * HG_CHUNK
    ng = length // rows

    def body(q_ref, k_ref, v_ref, g_ref, st_ref, do_ref, dq_ref, dk_ref, dv_ref, dg_ref, dstate):
        @pl.when(pl.program_id(0) == 0)
        def _():
            dstate[...] = jnp.zeros_like(dstate)

        dstates = [dstate[h] for h in range(HG_HEADS)]
        for ci in reversed(range(_HG_GROUP)):
            sl = slice(ci * HG_CHUNK, (ci + 1) * HG_CHUNK)
            _, vjp = jax.vjp(_hg_chunk, q_ref[sl, :], k_ref[sl, :], v_ref[sl, :], g_ref[sl, :],
                             *[st_ref[h, ci] for h in range(HG_HEADS)])
            dq, dk, dv, dg, *dstates = vjp((do_ref[sl, :], *dstates))
            dq_ref[sl, :] = dq
            dk_ref[sl, :] = dk
            dv_ref[sl, :] = dv
            dg_ref[sl, :] = dg
        for h in range(HG_HEADS):
            dstate[h] = dstates[h]

    blk = pl.BlockSpec((rows, HG_WIDTH), lambda c: (ng - 1 - c, 0))
    sds = jax.ShapeDtypeStruct((length, HG_WIDTH), f32)
    return pl.pallas_call(
        body, grid=(ng,),
        in_specs=[blk] * 4 + [pl.BlockSpec((HG_HEADS, _HG_GROUP, HG_DIM, HG_DIM), lambda c: (0, ng - 1 - c, 0, 0)), blk],
        out_specs=[blk] * 4, out_shape=[sds] * 4,
        scratch_shapes=[pltpu.VMEM((HG_HEADS, HG_DIM, HG_DIM), f32)],
        compiler_params=_cparams(("arbitrary",)), name=name)(q, k, v, g, states, do)


_ATT_BLK = 256
_ATT_SCALE = MLA_QK ** -0.5
_NEG = -1e30


def _att_mask(i, j, t):
    rows = i * t + lax.broadcasted_iota(jnp.int32, (t, t), 0)
    cols = j * t + lax.broadcasted_iota(jnp.int32, (t, t), 1)
    return cols <= rows


def _att_fwd(q, k, v, *, name):
    length = q.shape[0]
    t = min(_ATT_BLK, length)
    nq = length // t
    qw, vw = MLA_QK_PAD, MLA_V
    heads = range(MLA_HEADS)

    def body(q_ref, k_ref, v_ref, o_ref, lse_ref):
        i = pl.program_id(0)
        qbs = [q_ref[:, h * qw:(h + 1) * qw] for h in heads]

        def step(j, carry, diagonal=False):
            off = pl.multiple_of(j * t, t)
            out = []
            for h in heads:
                m, l, acc = carry[h]
                ks = k_ref[pl.ds(off, t), h * qw:(h + 1) * qw]
                vs = v_ref[pl.ds(off, t), h * vw:(h + 1) * vw]
                s = lax.dot_general(qbs[h], ks, _NT, preferred_element_type=f32) * _ATT_SCALE
                if diagonal:
                    s = jnp.where(_att_mask(i, j, t), s, _NEG)
                m_new = jnp.maximum(m, jnp.max(s, axis=-1, keepdims=True))
                alpha = jnp.exp(m - m_new)
                p = jnp.exp(s - m_new)
                l = alpha * l + jnp.sum(p, axis=-1, keepdims=True)
                acc = alpha * acc + jnp.dot(p.astype(bf16), vs, preferred_element_type=f32)
                out.append((m_new, l, acc))
            return tuple(out)

        init = tuple((jnp.full((t, 1), _NEG, f32), jnp.zeros((t, 1), f32), jnp.zeros((t, vw), f32)) for _ in heads)
        res = step(i, lax.fori_loop(0, i, step, init), diagonal=True)
        for h in heads:
            m, l, acc = res[h]
            o_ref[:, h * vw:(h + 1) * vw] = (acc / l).astype(o_ref.dtype)
            lse_ref[:, h * vw:(h + 1) * vw] = jnp.broadcast_to(m + jnp.log(l), (t, vw))

    return pl.pallas_call(
        body, grid=(nq,),
        in_specs=[pl.BlockSpec((t, q.shape[1]), lambda i: (i, 0)), pl.BlockSpec(k.shape, lambda i: (0, 0)),
                  pl.BlockSpec(v.shape, lambda i: (0, 0))],
        out_specs=[pl.BlockSpec((t, v.shape[1]), lambda i: (i, 0))] * 2,
        out_shape=[jax.ShapeDtypeStruct(v.shape, bf16), jax.ShapeDtypeStruct(v.shape, f32)],
        compiler_params=_cparams(("arbitrary",)), name=name)(q, k, v)


def _att_bwd(q, k, v, o, lse, do, *, name):
    length = q.shape[0]
    t = min(_ATT_BLK, length)
    nq = length // t
    qw, vw = MLA_QK_PAD, MLA_V
    heads = range(MLA_HEADS)

    def dq_body(q_ref, k_ref, v_ref, o_ref, lse_ref, do_ref, dq_ref, delta_ref):
        i = pl.program_id(0)
        qbs = [q_ref[:, h * qw:(h + 1) * qw] for h in heads]
        dobs = [do_ref[:, h * vw:(h + 1) * vw] for h in heads]
        lses = [lse_ref[:, h * vw:h * vw + 1] for h in heads]
        deltas = [jnp.sum(dobs[h].astype(f32) * o_ref[:, h * vw:(h + 1) * vw].astype(f32), axis=-1, keepdims=True)
                  for h in heads]

        def step(j, dqs, diagonal=False):
            off = pl.multiple_of(j * t, t)
            out = []
            for h in heads:
                ks = k_ref[pl.ds(off, t), h * qw:(h + 1) * qw]
                vs = v_ref[pl.ds(off, t), h * vw:(h + 1) * vw]
                s = lax.dot_general(qbs[h], ks, _NT, preferred_element_type=f32) * _ATT_SCALE
                p = jnp.exp(s - lses[h])
                if diagonal:
                    p = jnp.where(_att_mask(i, j, t), p, 0.0)
                dp = lax.dot_general(dobs[h], vs, _NT, preferred_element_type=f32)
                ds = p * (dp - deltas[h]) * _ATT_SCALE
                out.append(dqs[h] + jnp.dot(ds.astype(bf16), ks, preferred_element_type=f32))
            return tuple(out)

        dqs = step(i, lax.fori_loop(0, i, step, tuple(jnp.zeros((t, qw), f32) for _ in heads)), diagonal=True)
        for h in heads:
            dq_ref[:, h * qw:(h + 1) * qw] = dqs[h].astype(dq_ref.dtype)
            delta_ref[:, h * vw:(h + 1) * vw] = jnp.broadcast_to(deltas[h], (t, vw))

    qblk = pl.BlockSpec((t, q.shape[1]), lambda i: (i, 0))
    vblk = pl.BlockSpec((t, v.shape[1]), lambda i: (i, 0))
    qfull = pl.BlockSpec(q.shape, lambda i: (0, 0))
    vfull = pl.BlockSpec(v.shape, lambda i: (0, 0))
    dq, delta = pl.pallas_call(
        dq_body, grid=(nq,), in_specs=[qblk, qfull, vfull, vblk, vblk, vblk], out_specs=[qblk, vblk],
        out_shape=[jax.ShapeDtypeStruct(q.shape, bf16), jax.ShapeDtypeStruct(lse.shape, f32)],
        compiler_params=_cparams(("arbitrary",)), name=name + "_dq")(q, k, v, o, lse, do)

    def dkv_body(k_ref, v_ref, q_ref, do_ref, lse_ref, delta_ref, dk_ref, dv_ref):
        j = pl.program_id(0)
        kbs = [k_ref[:, h * qw:(h + 1) * qw] for h in heads]
        vbs = [v_ref[:, h * vw:(h + 1) * vw] for h in heads]

        def step(i, carry, diagonal=False):
            off = pl.multiple_of(i * t, t)
            out = []
            for h in heads:
                dk, dv = carry[h]
                qs = q_ref[pl.ds(off, t), h * qw:(h + 1) * qw]
                dos = do_ref[pl.ds(off, t), h * vw:(h + 1) * vw]
                lse_i = lse_ref[pl.ds(off, t), h * vw:h * vw + 1]
                delta_i = delta_ref[pl.ds(off, t), h * vw:h * vw + 1]
                s = lax.dot_general(qs, kbs[h], _NT, preferred_element_type=f32) * _ATT_SCALE
                p = jnp.exp(s - lse_i)
                if diagonal:
                    p = jnp.where(_att_mask(i, j, t), p, 0.0)
                dv = dv + lax.dot_general(p.astype(bf16), dos, _TN, preferred_element_type=f32)
                dp = lax.dot_general(dos, vbs[h], _NT, preferred_element_type=f32)
                ds = p * (dp - delta_i) * _ATT_SCALE
                dk = dk + lax.dot_general(ds.astype(bf16), qs, _TN, preferred_element_type=f32)
                out.append((dk, dv))
            return tuple(out)

        first = step(j, tuple((jnp.zeros((t, qw), f32), jnp.zeros((t, vw), f32)) for _ in heads), diagonal=True)
        res = lax.fori_loop(j + 1, nq, step, first)
        for h in heads:
            dk_ref[:, h * qw:(h + 1) * qw] = res[h][0].astype(dk_ref.dtype)
            dv_ref[:, h * vw:(h + 1) * vw] = res[h][1].astype(dv_ref.dtype)

    dk, dv = pl.pallas_call(
        dkv_body, grid=(nq,), in_specs=[qblk, vblk, qfull, vfull, vfull, vfull], out_specs=[qblk, vblk],
        out_shape=[jax.ShapeDtypeStruct(k.shape, bf16), jax.ShapeDtypeStruct(v.shape, bf16)],
        compiler_params=_cparams(("arbitrary",)), name=name + "_dkv")(k, v, q, do, lse, delta)
    return dq, dk, dv


_C_Q = 4 * HG_WIDTH
_C_KV = _C_Q + MLA_Q_RANK
_C_KPE = _C_KV + MLA_KV_RANK


def _rms_n(x, g, n):
    return x * lax.rsqrt(jnp.sum(x * x, axis=-1, keepdims=True) * (1.0 / n) + EPS) * g


def _mix_a(proj, l0, l1, q_a_norm, kv_a_norm):
    lb = jax.nn.sigmoid(l0 - l1)
    f = lb + (1.0 - lb) * jax.nn.sigmoid(proj[:, HG_WIDTH:2 * HG_WIDTH])
    qf = _silu(proj[:, :HG_WIDTH])
    v = proj[:, 2 * HG_WIDTH:3 * HG_WIDTH]
    cqn = _rms(proj[:, _C_Q:_C_KV], q_a_norm)
    ckvn = _rms(proj[:, _C_KV:_C_KPE], kv_a_norm)
    return qf, 1.0 - f, v, jnp.log(f), cqn, ckvn


def _mix_b(qraw, kvraw, proj, cos, sin, qn_nope, qn_rope, kn_nope, kn_rope, perm):
    def rope(x):
        return x * cos + jnp.dot(x, perm, precision=lax.Precision.HIGHEST, preferred_element_type=f32) * sin

    kpe = rope(_rms_n(proj[:, _C_KPE:], kn_rope, MLA_ROPE))
    qs, ks, vs = [], [], []
    for hh in range(MLA_HEADS):
        base = hh * MLA_QK_PAD
        qs.append(_rms(qraw[:, base:base + MLA_NOPE], qn_nope))
        qs.append(rope(_rms_n(qraw[:, base + MLA_NOPE:base + MLA_QK_PAD], qn_rope, MLA_ROPE)))
        ks.append(_rms(kvraw[:, base:base + MLA_NOPE], kn_nope))
        ks.append(kpe)
        vs.append(kvraw[:, base + MLA_NOPE:base + MLA_QK_PAD])
    return jnp.concatenate(qs, axis=-1), jnp.concatenate(ks, axis=-1), jnp.concatenate(vs, axis=-1)


def _mix_c(o_hg, proj, o_mla, hg_out_norm):
    parts = []
    for hh in range(HG_HEADS):
        sl = slice(hh * HG_DIM, (hh + 1) * HG_DIM)
        parts.append(_rms(o_hg[:, sl], hg_out_norm[:, sl]))
    o = jnp.concatenate(parts, axis=-1) * _silu(proj[:, 3 * HG_WIDTH:4 * HG_WIDTH])
    return jnp.concatenate([o, o_mla], axis=-1)


def _rope_perm():
    p = np.zeros((128, 128), np.float32)
    half = MLA_ROPE // 2
    for i in range(half):
        p[i + half, i] = -1.0
        p[i, i + half] = 1.0
    return jnp.asarray(p)


def _mixer_fwd(h, cos, sin, w, tag):
    d = h.shape[1]
    hn, = _rows(_rms, [h], [w["norm_mix"]], [(d, bf16)], name=f"mix_norm_{tag}")
    proj = _mm(hn, w["mix_w_in"], name=f"mix_in_{tag}")
    pa = [w["lb0"], w["lb1"], w["mla_q_a_norm"], w["mla_kv_a_norm"]]
    qf, kk, vv, logf, cqn, ckvn = _rows(
        _mix_a, [proj], pa, [(HG_WIDTH, f32)] * 4 + [(MLA_Q_RANK, bf16), (MLA_KV_RANK, bf16)], name=f"mix_a_{tag}")
    o_hg, states = _hg_fwd(qf, kk, vv, logf, name=f"hg_fwd_{tag}")
    qraw = _mm(cqn, w["mla_w_uq"], name=f"mla_uq_{tag}")
    kvraw = _mm(ckvn, w["mla_w_ukv"], name=f"mla_ukv_{tag}")
    pb = [w["mla_qn_nope"], w["mla_qn_rope"], w["mla_kn_nope"], w["mla_kn_rope"], w["rope_perm"]]
    qfull, kfull, vfull = _rows(_mix_b, [qraw, kvraw, proj, cos, sin], pb,
                                [(MLA_HEADS * MLA_QK_PAD, bf16)] * 2 + [(MLA_HEADS * MLA_V, bf16)],
                                name=f"mix_b_{tag}")
    o_mla, lse = _att_fwd(qfull, kfull, vfull, name=f"att_fwd_{tag}")
    mixin, = _rows(_mix_c, [o_hg, proj, o_mla], [w["hg_out_norm"]], [(d, bf16)], name=f"mix_c_{tag}")
    if callable(w["mix_w_out"]):
        w["mix_w_out"] = w["mix_w_out"](mixin)
    out = _mm(mixin, w["mix_w_out"], add=h, name=f"mix_out_{tag}")
    return out, (h, hn, proj, qf, kk, vv, logf, cqn, ckvn, o_hg, states, qraw, kvraw, qfull, kfull, vfull, o_mla,
                 lse, mixin)


def _mixer_bwd(dout, cos, sin, w, saved, tag, on_w_out=None):
    (h, hn, proj, qf, kk, vv, logf, cqn, ckvn, o_hg, states, qraw, kvraw, qfull, kfull, vfull, o_mla, lse,
     mixin) = saved
    g = {}
    dmixin = _mm(dout, w["mix_w_out"], tb=True, name=f"mix_dmixin_{tag}")
    g["mix_w_out"] = _mm(mixin, dout, ta=True, name=f"mix_dwout_{tag}")
    if on_w_out is not None:
        dmixin = on_w_out(g["mix_w_out"], dmixin)
    (do_hg, dproj_c, do_mla), (g["hg_out_norm"],) = _rows_bwd(
        _mix_c, [o_hg, proj, o_mla], [w["hg_out_norm"]], [dmixin], rgrad=[f32, f32, bf16], pgrad=[True],
        name=f"mix_dc_{tag}")
    dqfull, dkfull, dvfull = _att_bwd(qfull, kfull, vfull, o_mla, lse, do_mla, name=f"att_bwd_{tag}")
    pb = [w["mla_qn_nope"], w["mla_qn_rope"], w["mla_kn_nope"], w["mla_kn_rope"], w["rope_perm"]]
    (dqraw, dkvraw, dproj_b), pg = _rows_bwd(
        _mix_b, [qraw, kvraw, proj, cos, sin], pb, [dqfull, dkfull, dvfull],
        rgrad=[bf16, bf16, f32, None, None], pgrad=[True, True, True, True, False], addends={2: dproj_c},
        name=f"mix_db_{tag}")
    g["mla_qn_nope"], g["mla_qn_rope"], g["mla_kn_nope"], g["mla_kn_rope"] = pg
    dcqn = _mm(dqraw, w["mla_w_uq"], tb=True, name=f"mla_dcq_{tag}")
    g["mla_w_uq"] = _mm(cqn, dqraw, ta=True, name=f"mla_dwuq_{tag}")
    dckvn = _mm(dkvraw, w["mla_w_ukv"], tb=True, name=f"mla_dckv_{tag}")
    g["mla_w_ukv"] = _mm(ckvn, dkvraw, ta=True, name=f"mla_dwukv_{tag}")
    dqf, dkk, dvv, dlogf = _hg_bwd(qf, kk, vv, logf, states, do_hg, name=f"hg_bwd_{tag}")
    pa = [w["lb0"], w["lb1"], w["mla_q_a_norm"], w["mla_kv_a_norm"]]
    (dproj,), (g["lb0"], g["lb1"], g["mla_q_a_norm"], g["mla_kv_a_norm"]) = _rows_bwd(
        _mix_a, [proj], pa, [dqf, dkk, dvv, dlogf, dcqn, dckvn], rgrad=[bf16], pgrad=[True] * 4,
        addends={0: dproj_b}, name=f"mix_da_{tag}")
    dhn = _mm(dproj, w["mix_w_in"], tb=True, out_dtype=bf16, name=f"mix_dhn_{tag}")
    g["mix_w_in_t"] = _mm(dproj, hn, ta=True, name=f"mix_dwin_{tag}")
    (dh,), (g["norm_mix"],) = _rows_bwd(_rms, [h], [w["norm_mix"]], [dhn], rgrad=[f32], pgrad=[True],
                                        addends={0: dout}, name=f"mix_dnorm_{tag}")
    return dh, g


def _rope_tables(positions):
    inv_freq = 1.0 / (ROPE_BASE ** (jnp.arange(0, MLA_ROPE, 2, dtype=f32) / MLA_ROPE))
    ang = positions.astype(f32)[:, None] * inv_freq
    z = jnp.zeros((positions.shape[0], 128 - MLA_ROPE), f32)
    return (jnp.concatenate([jnp.cos(ang), jnp.cos(ang), z], axis=1),
            jnp.concatenate([jnp.sin(ang), jnp.sin(ang), z], axis=1))


def _pad_cols(a, n):
    return jnp.pad(a, ((0, 0), (0, n - a.shape[1])))


def _even_weights(p, j, layer, dt):
    w_uq = p["mla_w_uq"][j].reshape(MLA_Q_RANK, MLA_HEADS, MLA_QK)
    w_uq = jnp.pad(w_uq, ((0, 0), (0, 0), (0, MLA_QK_PAD - MLA_QK))).reshape(MLA_Q_RANK, MLA_HEADS * MLA_QK_PAD)
    return dict(
        norm_mix=p["norm_mix"][layer][None], mix_w_in=_pad_cols(p["mix_w_in"][j], IN_PAD).astype(dt),
        lb0=p["hg_lb_logits"][0][None], lb1=p["hg_lb_logits"][1][None],
        mla_q_a_norm=p["mla_q_a_norm"][j][None], mla_kv_a_norm=p["mla_kv_a_norm"][j][None],
        mla_w_uq=w_uq.astype(dt), mla_w_ukv=p["mla_w_ukv"][j].astype(dt),
        mla_qn_nope=p["mla_qn_nope"][j][None], mla_qn_rope=_pad_cols(p["mla_qn_rope"][j][None], 128),
        mla_kn_nope=p["mla_kn_nope"][j][None], mla_kn_rope=_pad_cols(p["mla_kn_rope"][j][None], 128),
        rope_perm=_rope_perm(), hg_out_norm=p["hg_out_norm"][j][None],
        mix_w_out=p["mix_w_out"][j].astype(dt) if "mix_w_out" in p else None)


def _even_grads(g):
    w_uq = g["mla_w_uq"].reshape(MLA_Q_RANK, MLA_HEADS, MLA_QK_PAD)[:, :, :MLA_QK].reshape(MLA_Q_RANK, -1)
    return dict(
        norm_mix=g["norm_mix"], mix_w_in=g["mix_w_in_t"][:IN_WIDTH].T[None],
        hg_lb_logits=jnp.concatenate([g["lb0"], g["lb1"]], axis=0),
        mla_q_a_norm=g["mla_q_a_norm"], mla_kv_a_norm=g["mla_kv_a_norm"], mla_w_uq=w_uq[None],
        mla_w_ukv=g["mla_w_ukv"][None], mla_qn_nope=g["mla_qn_nope"], mla_qn_rope=g["mla_qn_rope"][:, :MLA_ROPE],
        mla_kn_nope=g["mla_kn_nope"], mla_kn_rope=g["mla_kn_rope"][:, :MLA_ROPE],
        hg_out_norm=g["hg_out_norm"], mix_w_out=g["mix_w_out"][None])


_S5_NB = 8
_S5_BW = 1024
_S5_HALF = 512
_S5_UC = 128
_S5_TIME = 512


def _bd_mm(a, b3, *, name, tb=False, out_dtype=f32):
    length = a.shape[0]
    rows_b, cols_b = b3.shape[0] // _S5_NB, b3.shape[1]
    ka, n = (cols_b, rows_b) if tb else (rows_b, cols_b)
    bm = _pick(length, (512, 256, 128))
    dims = _NT if tb else _NN

    def body(a_ref, b_ref, o_ref):
        o_ref[...] = lax.dot_general(a_ref[...].astype(bf16), b_ref[...].astype(bf16), dims,
                                     preferred_element_type=f32).astype(o_ref.dtype)

    return pl.pallas_call(
        body, grid=(length // bm, _S5_NB),
        in_specs=[pl.BlockSpec((bm, ka), lambda i, j: (i, j)), pl.BlockSpec((rows_b, cols_b), lambda i, j: (j, 0))],
        out_specs=pl.BlockSpec((bm, n), lambda i, j: (i, j)),
        out_shape=jax.ShapeDtypeStruct((length, _S5_NB * n), out_dtype),
        compiler_params=_cparams(("parallel", "parallel")), name=name)(a, b3)


def _bd_mm_tn(a, c, *, name):
    length = a.shape[0]
    ka, n = a.shape[1] // _S5_NB, c.shape[1] // _S5_NB
    bk = _pick(length, (512, 256, 128))
    nk = length // bk

    def body(a_ref, c_ref, o_ref):
        @pl.when(pl.program_id(1) == 0)
        def _():
            o_ref[...] = jnp.zeros_like(o_ref)
        o_ref[...] += lax.dot_general(a_ref[...].astype(bf16), c_ref[...].astype(bf16), _TN,
                                      preferred_element_type=f32)

    return pl.pallas_call(
        body, grid=(_S5_NB, nk),
        in_specs=[pl.BlockSpec((bk, ka), lambda j, q: (q, j)), pl.BlockSpec((bk, n), lambda j, q: (q, j))],
        out_specs=pl.BlockSpec((ka, n), lambda j, q: (j, 0)),
        out_shape=jax.ShapeDtypeStruct((_S5_NB * ka, n), f32),
        compiler_params=_cparams(("parallel", "arbitrary")), name=name)(a, c)


def _cmul(ar, ai, br, bi):
    return ar * br - ai * bi, ar * bi + ai * br


def _pow_table(ar, ai, descending):
    rows = lax.broadcasted_iota(jnp.int32, (8, ar.shape[1]), 0)
    tr = jnp.zeros((8, ar.shape[1]), f32)
    ti = jnp.zeros((8, ar.shape[1]), f32)
    pr, pi_ = ar, ai
    for r in range(8):
        sel = rows == ((7 - r) if descending else r)
        tr = jnp.where(sel, pr, tr)
        ti = jnp.where(sel, pi_, ti)
        pr, pi_ = _cmul(pr, pi_, ar, ai)
    return tr, ti


def _s5_scan_fwd(a, x, *, name):
    length = x.shape[0]
    tc = min(_S5_TIME, length)
    hw = _S5_HALF

    def body(a_ref, x_ref, o_ref, carry):
        @pl.when(pl.program_id(1) == 0)
        def _():
            carry[...] = jnp.zeros_like(carry)

        ar, ai = a_ref[:, :hw], a_ref[:, hw:]
        xr, xi = x_ref[:, :hw], x_ref[:, hw:]
        row8 = lax.broadcasted_iota(jnp.int32, (tc, hw), 0) & 7
        mr, mi = ar, ai
        for s in (1, 2, 4):
            sr, si = pltpu.roll(xr, s, axis=0), pltpu.roll(xi, s, axis=0)
            pr, pi_ = _cmul(mr, mi, sr, si)
            ok = row8 >= s
            xr = xr + jnp.where(ok, pr, 0.0)
            xi = xi + jnp.where(ok, pi_, 0.0)
            mr, mi = _cmul(mr, mi, mr, mi)
        o_ref[:, :hw] = xr
        o_ref[:, hw:] = xi
        tr, ti = _pow_table(ar, ai, False)
        cr, ci = carry[:, :hw], carry[:, hw:]
        for i in range(tc // 8):
            sl = slice(8 * i, 8 * i + 8)
            pr, pi_ = _cmul(tr, ti, cr, ci)
            o_ref[sl, :hw] = o_ref[sl, :hw] + pr
            o_ref[sl, hw:] = o_ref[sl, hw:] + pi_
            cr, ci = o_ref[8 * i + 7:8 * i + 8, :hw], o_ref[8 * i + 7:8 * i + 8, hw:]
        carry[:, :hw] = cr
        carry[:, hw:] = ci

    return pl.pallas_call(
        body, grid=(_S5_NB, length // tc),
        in_specs=[pl.BlockSpec((1, _S5_BW), lambda j, t: (0, j)), pl.BlockSpec((tc, _S5_BW), lambda j, t: (t, j))],
        out_specs=pl.BlockSpec((tc, _S5_BW), lambda j, t: (t, j)),
        out_shape=jax.ShapeDtypeStruct(x.shape, f32),
        scratch_shapes=[pltpu.VMEM((1, _S5_BW), f32)],
        compiler_params=_cparams(("parallel", "arbitrary")), name=name)(a, x)


def _s5_scan_bwd(a, hs, d, *, name):
    length = d.shape[0]
    tc = min(_S5_TIME, length)
    nt = length // tc
    hw = _S5_HALF

    def body(a_ref, h_ref, d_ref, g_ref, da_ref, carry):
        @pl.when(pl.program_id(1) == 0)
        def _():
            carry[...] = jnp.zeros_like(carry)
            da_ref[...] = jnp.zeros_like(da_ref)

        ar, ai = a_ref[:, :hw], -a_ref[:, hw:]
        xr, xi = d_ref[:, :hw], d_ref[:, hw:]
        rows = lax.broadcasted_iota(jnp.int32, (tc, hw), 0)
        row8 = rows & 7
        mr, mi = ar, ai
        for s in (1, 2, 4):
            sr, si = pltpu.roll(xr, tc - s, axis=0), pltpu.roll(xi, tc - s, axis=0)
            pr, pi_ = _cmul(mr, mi, sr, si)
            ok = row8 < 8 - s
            xr = xr + jnp.where(ok, pr, 0.0)
            xi = xi + jnp.where(ok, pi_, 0.0)
            mr, mi = _cmul(mr, mi, mr, mi)
        g_ref[:, :hw] = xr
        g_ref[:, hw:] = xi
        tr, ti = _pow_table(ar, ai, True)
        cr0, ci0 = carry[:, :hw], carry[:, hw:]
        cr, ci = cr0, ci0
        for i in reversed(range(tc // 8)):
            sl = slice(8 * i, 8 * i + 8)
            pr, pi_ = _cmul(tr, ti, cr, ci)
            g_ref[sl, :hw] = g_ref[sl, :hw] + pr
            g_ref[sl, hw:] = g_ref[sl, hw:] + pi_
            cr, ci = g_ref[8 * i:8 * i + 1, :hw], g_ref[8 * i:8 * i + 1, hw:]
        carry[:, :hw] = cr
        carry[:, hw:] = ci
        last = rows == tc - 1
        gnr = jnp.where(last, cr0, pltpu.roll(g_ref[:, :hw], tc - 1, axis=0))
        gni = jnp.where(last, ci0, pltpu.roll(g_ref[:, hw:], tc - 1, axis=0))
        hr, hi = h_ref[:, :hw], h_ref[:, hw:]
        da_ref[:, :hw] += jnp.sum(hr * gnr + hi * gni, axis=0, keepdims=True)
        da_ref[:, hw:] += jnp.sum(hr * gni - hi * gnr, axis=0, keepdims=True)

    blk = pl.BlockSpec((tc, _S5_BW), lambda j, t: (nt - 1 - t, j))
    row = pl.BlockSpec((1, _S5_BW), lambda j, t: (0, j))
    return pl.pallas_call(
        body, grid=(_S5_NB, nt), in_specs=[row, blk, blk], out_specs=[blk, row],
        out_shape=[jax.ShapeDtypeStruct(d.shape, f32), jax.ShapeDtypeStruct((1, _S5_NB * _S5_BW), f32)],
        scratch_shapes=[pltpu.VMEM((1, _S5_BW), f32)],
        compiler_params=_cparams(("parallel", "arbitrary")), name=name)(a, hs, d)


def _s5_tile_scan(work, carry, ar, ai, tc, reverse, per_tile=None):
    hw = _S5_HALF
    row8 = lax.broadcasted_iota(jnp.int32, (8, hw), 0)
    powers = [(ar, ai)]
    for _ in range(2):
        powers.append(_cmul(*powers[-1], *powers[-1]))
    steps = []
    for (mr, mi), s in zip(powers, (1, 2, 4)):
        ok = (row8 < 8 - s) if reverse else (row8 >= s)
        steps.append((jnp.where(ok, mr, 0.0), jnp.where(ok, mi, 0.0), 8 - s if reverse else s))
    tr, ti = _pow_table(ar, ai, reverse)
    cr, ci = carry[:, :hw], carry[:, hw:]
    tiles = range(tc // 8)
    for i in (reversed(tiles) if reverse else tiles):
        sl = slice(8 * i, 8 * i + 8)
        xr, xi = work[sl, :hw], work[sl, hw:]
        for mr, mi, shift in steps:
            pr, pi_ = _cmul(mr, mi, pltpu.roll(xr, shift, axis=0), pltpu.roll(xi, shift, axis=0))
            xr, xi = xr + pr, xi + pi_
        pr, pi_ = _cmul(tr, ti, cr, ci)
        xr, xi = xr + pr, xi + pi_
        work[sl, :hw] = xr
        work[sl, hw:] = xi
        if per_tile is not None:
            per_tile(sl, xr, xi, cr, ci)
        edge = 8 * i if reverse else 8 * i + 7
        cr, ci = work[edge:edge + 1, :hw], work[edge:edge + 1, hw:]
    carry[:, :hw] = cr
    carry[:, hw:] = ci


def _s5_core_fwd(a, hn, b3, c3, *, name):
    length = hn.shape[0]
    tc = min(_S5_TIME, length)

    def body(a_ref, hn_ref, b_ref, c_ref, hs_ref, y_ref, work, carry):
        @pl.when(pl.program_id(1) == 0)
        def _():
            carry[...] = jnp.zeros_like(carry)

        work[...] = jnp.dot(hn_ref[...].astype(bf16), b_ref[...], preferred_element_type=f32)
        _s5_tile_scan(work, carry, a_ref[:, :_S5_HALF], a_ref[:, _S5_HALF:], tc, False)
        hs = work[...].astype(bf16)
        hs_ref[...] = hs
        y_ref[...] = jnp.dot(hs, c_ref[...], preferred_element_type=f32)

    return pl.pallas_call(
        body, grid=(_S5_NB, length // tc),
        in_specs=[pl.BlockSpec((1, _S5_BW), lambda j, t: (0, j)), pl.BlockSpec((tc, _S5_UC), lambda j, t: (t, j)),
                  pl.BlockSpec((_S5_UC, _S5_BW), lambda j, t: (j, 0)), pl.BlockSpec((_S5_BW, _S5_UC), lambda j, t: (j, 0))],
        out_specs=[pl.BlockSpec((tc, _S5_BW), lambda j, t: (t, j)), pl.BlockSpec((tc, _S5_UC), lambda j, t: (t, j))],
        out_shape=[jax.ShapeDtypeStruct((length, _S5_NB * _S5_BW), bf16),
                   jax.ShapeDtypeStruct((length, _S5_NB * _S5_UC), f32)],
        scratch_shapes=[pltpu.VMEM((tc, _S5_BW), f32), pltpu.VMEM((1, _S5_BW), f32)],
        compiler_params=_cparams(("parallel", "arbitrary")), name=name)(a, hn, b3, c3)


def _s5_core_bwd(a, dy, c3, hs, hn, b3, *, name):
    length = hn.shape[0]
    tc = min(_S5_TIME, length)
    nt = length // tc
    hw = _S5_HALF

    def body(a_ref, dy_ref, c_ref, hs_ref, hn_ref, b_ref, du_ref, db_ref, dc_ref, da_ref, work, carry, acc):
        @pl.when(pl.program_id(1) == 0)
        def _():
            carry[...] = jnp.zeros_like(carry)
            db_ref[...] = jnp.zeros_like(db_ref)
            dc_ref[...] = jnp.zeros_like(dc_ref)
            da_ref[...] = jnp.zeros_like(da_ref)

        dyb = dy_ref[...].astype(bf16)
        work[...] = lax.dot_general(dyb, c_ref[...], _NT, preferred_element_type=f32)
        acc[...] = jnp.zeros_like(acc)
        row8 = lax.broadcasted_iota(jnp.int32, (8, hw), 0)

        def grad_a(sl, gr, gi, cr, ci):
            gnr = jnp.where(row8 == 7, cr, pltpu.roll(gr, 7, axis=0))
            gni = jnp.where(row8 == 7, ci, pltpu.roll(gi, 7, axis=0))
            hr, hi = hs_ref[sl, :hw].astype(f32), hs_ref[sl, hw:].astype(f32)
            acc[:, :hw] += hr * gnr + hi * gni
            acc[:, hw:] += hr * gni - hi * gnr

        _s5_tile_scan(work, carry, a_ref[:, :hw], -a_ref[:, hw:], tc, True, grad_a)
        da_ref[...] += jnp.sum(acc[...], axis=0, keepdims=True)
        g = work[...].astype(bf16)
        du_ref[...] = lax.dot_general(g, b_ref[...], _NT, preferred_element_type=f32)
        db_ref[...] += lax.dot_general(hn_ref[...].astype(bf16), g, _TN, preferred_element_type=f32)
        dc_ref[...] += lax.dot_general(hs_ref[...], dyb, _TN, preferred_element_type=f32)

    rev = lambda j, t: (nt - 1 - t, j)
    return pl.pallas_call(
        body, grid=(_S5_NB, nt),
        in_specs=[pl.BlockSpec((1, _S5_BW), lambda j, t: (0, j)), pl.BlockSpec((tc, _S5_UC), rev),
                  pl.BlockSpec((_S5_BW, _S5_UC), lambda j, t: (j, 0)), pl.BlockSpec((tc, _S5_BW), rev),
                  pl.BlockSpec((tc, _S5_UC), rev), pl.BlockSpec((_S5_UC, _S5_BW), lambda j, t: (j, 0))],
        out_specs=[pl.BlockSpec((tc, _S5_UC), rev), pl.BlockSpec((_S5_UC, _S5_BW), lambda j, t: (j, 0)),
                   pl.BlockSpec((_S5_BW, _S5_UC), lambda j, t: (j, 0)), pl.BlockSpec((1, _S5_BW), lambda j, t: (0, j))],
        out_shape=[jax.ShapeDtypeStruct((length, _S5_NB * _S5_UC), f32),
                   jax.ShapeDtypeStruct((_S5_NB * _S5_UC, _S5_BW), f32),
                   jax.ShapeDtypeStruct((_S5_NB * _S5_BW, _S5_UC), f32),
                   jax.ShapeDtypeStruct((1, _S5_NB * _S5_BW), f32)],
        scratch_shapes=[pltpu.VMEM((tc, _S5_BW), f32), pltpu.VMEM((1, _S5_BW), f32), pltpu.VMEM((8, _S5_BW), f32)],
        compiler_params=_cparams(("parallel", "arbitrary")), name=name)(a, dy, c3, hs, hn, b3)


def _s5_disc(lr, li, ldt, btr, bti, expand):
    dt = jnp.exp(ldt)
    mag = jnp.exp(lr * dt)
    abr = mag * jnp.cos(li * dt)
    abi = mag * jnp.sin(li * dt)
    den = lr * lr + li * li
    zr = ((abr - 1.0) * lr + abi * li) / den
    zi = (abi * lr - (abr - 1.0) * li) / den
    zr = jnp.dot(zr, expand, precision=lax.Precision.HIGHEST, preferred_element_type=f32)
    zi = jnp.dot(zi, expand, precision=lax.Precision.HIGHEST, preferred_element_type=f32)
    return abr, abi, zr * btr - zi * bti, zr * bti + zi * btr


def _s5_disc_fwd(args, *, name):
    def body(*refs):
        res = _s5_disc(*[r[...] for r in refs[:6]])
        for o, v in zip(refs[6:], res):
            o[...] = v

    sds = jax.ShapeDtypeStruct
    return pl.pallas_call(body, out_shape=[sds(args[0].shape, f32)] * 2 + [sds(args[3].shape, f32)] * 2,
                          name=name)(*args)


def _s5_disc_bwd(args, cts, *, name):
    def body(*refs):
        vals = [r[...] for r in refs[:6]]
        _, vjp = jax.vjp(lambda *d: _s5_disc(*d, vals[5]), *vals[:5])
        grads = vjp(tuple(r[...] for r in refs[6:10]))
        for o, v in zip(refs[10:], grads):
            o[...] = v

    return pl.pallas_call(body, out_shape=[jax.ShapeDtypeStruct(a.shape, f32) for a in args[:5]],
                          name=name)(*args, *cts)


def _gelu_tanh(x):
    return 0.5 * x * (1.0 + jnp.tanh(0.7978845608028654 * (x + 0.044715 * (x * x * x))))


def _s5_post(y, u, d_skip):
    return _gelu_tanh(y + d_skip * u)


def _s5_glu(ga, gb, h):
    return h + ga * jax.nn.sigmoid(gb)


def _s5_expand():
    e = np.zeros((S5_STATE, S5_GROUP * S5_STATE), np.float32)
    for m in range(S5_GROUP):
        e[np.arange(S5_STATE), m * S5_STATE + np.arange(S5_STATE)] = 1.0
    return jnp.asarray(e)


def _s5_pack_b(bbr, bbi):
    eye = jnp.eye(8, dtype=f32)

    def one(bb):
        b5 = bb.reshape(_S5_NB, 8, S5_GROUP, S5_STATE)
        return jnp.einsum("jgmp,gh->jgmhp", b5, eye).reshape(_S5_NB * _S5_UC, _S5_HALF)

    return jnp.concatenate([one(bbr), one(bbi)], axis=1)


def _s5_unpack_b(db3):
    def one(d):
        d5 = d.reshape(_S5_NB, 8, S5_GROUP, 8, S5_STATE)
        return jnp.einsum("jgmgp->jgmp", d5).reshape(S5_GROUPS, S5_GROUP * S5_STATE)

    return one(db3[:, :_S5_HALF]), one(db3[:, _S5_HALF:])


def _s5_pack_c(c_re, c_im):
    eye = jnp.eye(8, dtype=f32)

    def one(c):
        c4 = c.reshape(_S5_NB, 8, S5_GROUP, S5_STATE)
        return jnp.einsum("jgmp,hg->jhpgm", c4, eye).reshape(_S5_NB, _S5_HALF, _S5_UC)

    return jnp.concatenate([one(c_re), -one(c_im)], axis=1).reshape(_S5_NB * _S5_BW, _S5_UC)


def _s5_unpack_c(dc3):
    d = dc3.reshape(_S5_NB, 2, 8, S5_STATE, 8, S5_GROUP)
    dre = jnp.einsum("jgpgm->jgmp", d[:, 0]).reshape(S5_GROUPS, S5_GROUP, S5_STATE)
    dim = -jnp.einsum("jgpgm->jgmp", d[:, 1]).reshape(S5_GROUPS, S5_GROUP, S5_STATE)
    return dre, dim


def _s5_state_row(re, im):
    r = re.reshape(_S5_NB, 1, _S5_HALF)
    i = im.reshape(_S5_NB, 1, _S5_HALF)
    return jnp.concatenate([r, i], axis=2).reshape(1, _S5_NB * _S5_BW)


def _s5_unstate_row(row):
    r = row.reshape(_S5_NB, 2, 8, S5_STATE)
    return r[:, 0].reshape(S5_GROUPS, S5_STATE), r[:, 1].reshape(S5_GROUPS, S5_STATE)


def _s5_fwd(h, w, tag):
    d = h.shape[1]
    hn, = _rows(_rms, [h], [w["norm_mix"]], [(d, f32)], name=f"s5_norm_{tag}")
    disc_in = [w["s5_lam_re"], w["s5_lam_im"], w["s5_log_dt"], w["s5_bt_re"], w["s5_bt_im"], w["s5_expand"]]
    abr, abi, bbr, bbi = _s5_disc_fwd(disc_in, name=f"s5_disc_{tag}")
    a_row = _s5_state_row(abr, abi)
    b3 = _s5_pack_b(bbr, bbi).astype(bf16)
    hs, y = _s5_core_fwd(a_row, hn, b3, w["s5_c3"], name=f"s5_core_{tag}")
    yg, = _rows(_s5_post, [y, hn], [w["s5_d"]], [(d, bf16)], name=f"s5_post_{tag}")
    ga = _mm(yg, w["s5_w_glu_a"], name=f"s5_glu_a_{tag}")
    gb = _mm(yg, w["s5_w_glu_b"], name=f"s5_glu_b_{tag}")
    out, = _rows(_s5_glu, [ga, gb, h], [], [(d, f32)], name=f"s5_glu_{tag}")
    return out, (h, hn, disc_in, a_row, b3, hs, y, yg, ga, gb)


def _s5_bwd(dout, w, saved, tag):
    h, hn, disc_in, a_row, b3, hs, y, yg, ga, gb = saved
    g = {}
    (dga, dgb), _ = _rows_bwd(_s5_glu, [ga, gb, h], [], [dout], rgrad=[bf16, bf16, None], pgrad=[],
                              name=f"s5_dglu_{tag}")
    dyg = _mm(dga, w["s5_w_glu_a"], tb=True, name=f"s5_dyg_a_{tag}")
    dyg = _mm(dgb, w["s5_w_glu_b"], tb=True, add=dyg, name=f"s5_dyg_b_{tag}")
    g["s5_w_glu_a"] = _mm(yg, dga, ta=True, name=f"s5_dwa_{tag}")
    g["s5_w_glu_b"] = _mm(yg, dgb, ta=True, name=f"s5_dwb_{tag}")
    (dy, du_skip), (g["s5_d"],) = _rows_bwd(_s5_post, [y, hn], [w["s5_d"]], [dyg], rgrad=[bf16, f32], pgrad=[True],
                                           name=f"s5_dpost_{tag}")
    du, db3, dc3, da_row = _s5_core_bwd(a_row, dy, w["s5_c3"], hs, hn, b3, name=f"s5_dcore_{tag}")
    dabr, dabi = _s5_unstate_row(da_row)
    dbbr, dbbi = _s5_unpack_b(db3)
    g["s5_lam_re"], g["s5_lam_im"], g["s5_log_dt"], g["s5_bt_re"], g["s5_bt_im"] = _s5_disc_bwd(
        disc_in, [dabr, dabi, dbbr, dbbi], name=f"s5_ddisc_{tag}")
    g["s5_c_re"], g["s5_c_im"] = _s5_unpack_c(dc3)
    (dh,), (g["norm_mix"],) = _rows_bwd(_rms_twice, [h], [w["norm_mix"]], [du, du_skip], rgrad=[f32], pgrad=[True],
                                        addends={0: dout}, name=f"s5_dnorm_{tag}")
    return dh, g


def _odd_weights(p, j, layer, dt):
    tr = lambda b: b.transpose(0, 2, 1).reshape(S5_GROUPS, S5_GROUP * S5_STATE)
    return dict(
        norm_mix=p["norm_mix"][layer][None], s5_lam_re=p["s5_lam_re"][j], s5_lam_im=p["s5_lam_im"][j],
        s5_log_dt=p["s5_log_dt"][j][:, None], s5_bt_re=tr(p["s5_b_re"][j]), s5_bt_im=tr(p["s5_b_im"][j]),
        s5_expand=_s5_expand(), s5_c3=_s5_pack_c(p["s5_c_re"][j], p["s5_c_im"][j]).astype(dt),
        s5_d=p["s5_d"][j][None], s5_w_glu_a=p["s5_w_glu_a"][j].astype(dt), s5_w_glu_b=p["s5_w_glu_b"][j].astype(dt))


def _odd_grads(g):
    tr = lambda b: b.reshape(S5_GROUPS, S5_GROUP, S5_STATE).transpose(0, 2, 1)[None]
    return dict(
        norm_mix=g["norm_mix"], s5_lam_re=g["s5_lam_re"][None], s5_lam_im=g["s5_lam_im"][None],
        s5_log_dt=g["s5_log_dt"][:, 0][None], s5_b_re=tr(g["s5_bt_re"]), s5_b_im=tr(g["s5_bt_im"]),
        s5_c_re=g["s5_c_re"][None], s5_c_im=g["s5_c_im"][None], s5_d=g["s5_d"],
        s5_w_glu_a=g["s5_w_glu_a"][None], s5_w_glu_b=g["s5_w_glu_b"][None])


def _loss_fn(y, t):
    e = y - t
    part = jnp.sum(jnp.sum(e * e, axis=-1, keepdims=True), axis=0, keepdims=True) * (0.5 / y.shape[1])
    return e * (1.0 / y.shape[1]), part


FF_SHARD = 352
FF_SHARD_PAD = 384


def _pad_groups(a, axis):
    axis %= a.ndim
    zeros = jnp.zeros(a.shape[:axis] + (FF_SHARD_PAD - FF_SHARD,) + a.shape[axis + 1:], a.dtype)
    pieces = []
    for g in range(a.shape[axis] // FF_SHARD):
        pieces += [lax.slice_in_dim(a, g * FF_SHARD, (g + 1) * FF_SHARD, axis=axis), zeros]
    return jnp.concatenate(pieces, axis=axis)


def _unpad_groups(a, axis):
    axis %= a.ndim
    pieces = [lax.slice_in_dim(a, g * FF_SHARD_PAD, g * FF_SHARD_PAD + FF_SHARD, axis=axis)
              for g in range(a.shape[axis] // FF_SHARD_PAD)]
    return pieces[0] if len(pieces) == 1 else jnp.concatenate(pieces, axis=axis)


def _layer_weights(p, layer, dt):
    return dict(
        norm_xa=p["norm_xa"][layer][None], norm_mem=p["norm_mem"][layer][None], norm_ffn=p["norm_ffn"][layer][None],
        xa_wq=p["xa_wq"][layer].astype(dt), xa_wk=p["xa_wk"][layer].astype(dt), xa_wv=p["xa_wv"][layer].astype(dt),
        xa_wo=p["xa_wo"][layer].astype(dt), xa_q_norm=p["xa_q_norm"][layer][None],
        xa_k_norm=p["xa_k_norm"][layer][None], ffn_w_up=_pad_groups(p["ffn_w_up"][layer], 1).astype(dt),
        ffn_conv_w=_pad_groups(p["ffn_conv_w"][layer], 1), ffn_conv_b=_pad_groups(p["ffn_conv_b"][layer][None], 1),
        ffn_w_down=_pad_groups(p["ffn_w_down"][layer], 0).astype(dt))


_PER_LAYER = ("norm_xa", "norm_mem", "norm_ffn", "xa_wq", "xa_wk", "xa_wv", "xa_wo", "xa_q_norm", "xa_k_norm",
              "ffn_w_up", "ffn_conv_w", "ffn_conv_b", "ffn_w_down")
_FFN_PADDED = dict(ffn_w_up=1, ffn_conv_w=1, ffn_conv_b=1, ffn_w_down=0)


def _local_step(x, mem, positions, target, p):
    cos, sin = _rope_tables(positions)
    we = _even_weights(p, 0, 0, bf16)
    wo = _odd_weights(p, 0, 1, bf16)
    wl = [_layer_weights(p, layer, bf16) for layer in range(2)]
    loss, dh, g_even, g_odd, gl = _local_core(x, mem, cos, sin, target, we, wo, wl)
    grads = {}
    for n in _PER_LAYER:
        a, b = gl[0][n], gl[1][n]
        if n in _FFN_PADDED:
            a, b = _unpad_groups(a, _FFN_PADDED[n]), _unpad_groups(b, _FFN_PADDED[n])
        grads[n] = jnp.concatenate([a, b], axis=0) if a.shape[0] == 1 else jnp.stack([a, b])
    ge, go = _even_grads(g_even), _odd_grads(g_odd)
    grads["norm_mix"] = jnp.concatenate([ge.pop("norm_mix"), go.pop("norm_mix")], axis=0)
    grads.update(ge)
    grads.update(go)
    return loss, dh, grads


def _local_core(x, mem, cos, sin, target, we, wo, wl):
    h, s_mix0 = _mixer_fwd(x, cos, sin, we, "l0")
    h, s_xa0 = _xattn_fwd(h, mem, wl[0], "l0")
    h, s_ff0 = _ffn_fwd(h, wl[0], "l0")
    h, s_mix1 = _s5_fwd(h, wo, "l1")
    h, s_xa1 = _xattn_fwd(h, mem, wl[1], "l1")
    h, s_ff1 = _ffn_fwd(h, wl[1], "l1")
    dh, loss = _rows(_loss_fn, [h, target], [], [(h.shape[1], f32)], accs=[(1, 1)], name="loss_head")

    gl = [{}, {}]
    dh, g = _ffn_bwd(dh, wl[1], s_ff1, "l1")
    gl[1].update(g)
    dh, g = _xattn_bwd(dh, mem, wl[1], s_xa1, "l1")
    gl[1].update(g)
    dh, g_odd = _s5_bwd(dh, wo, s_mix1, "l1")
    dh, g = _ffn_bwd(dh, wl[0], s_ff0, "l0")
    gl[0].update(g)
    dh, g = _xattn_bwd(dh, mem, wl[0], s_xa0, "l0")
    gl[0].update(g)
    dh, g_even = _mixer_bwd(dh, cos, sin, we, s_mix0, "l0")
    return loss, dh, g_even, g_odd, gl


_LANES = 1024
_ROW_PAD = 256


_PEER_MASKS = (1, 2, 4, 3, 5, 6, 7)


def _mesh_place():
    x, y, c = lax.axis_index("x"), lax.axis_index("y"), lax.axis_index("c")

    def peer(mask):
        px = 1 - x if mask & 4 else x
        py = 1 - y if mask & 2 else y
        pc = 1 - c if mask & 1 else c
        return (px, py, pc), 4 * px + 2 * py + pc

    return 4 * x + 2 * y + c, peer


class _Exchange:
    def __init__(self, name):
        self.name = name
        self.srcs, self.shapes, self.items, self.where = [], [], [], {}

    def add(self, src, land_shape, src_at, dst_at, key):
        si = next((i for i, s in enumerate(self.srcs) if s is src), None)
        if si is None:
            self.srcs.append(src)
            si = len(self.srcs) - 1
        if key not in self.where:
            self.shapes.append(land_shape)
            self.where[key] = len(self.shapes) - 1
        self.items.append(dict(src=si, dst=self.where[key], src_at=src_at, dst_at=dst_at))

    def _copy(self, k, mask, ins, lands, send_sems, recv_sems, me, peer, arriving):
        it = self.items[k]
        dev, idx = peer(mask)
        s = k * (N_DEV - 1) + _PEER_MASKS.index(mask)
        return pltpu.make_async_remote_copy(
            src_ref=it["src_at"](ins[it["src"]], idx), dst_ref=it["dst_at"](lands[it["dst"]], idx if arriving else me),
            send_sem=send_sems.at[s], recv_sem=recv_sems.at[s], device_id=dev, device_id_type=pl.DeviceIdType.MESH)

    def _own_copy(self, k, ins, lands, own_sems, me):
        it = self.items[k]
        return pltpu.make_async_copy(it["src_at"](ins[it["src"]], me), it["dst_at"](lands[it["dst"]], me), own_sems.at[k])

    def begin(self, own):
        ns, nd, ni = len(self.srcs), len(self.shapes), len(self.items)
        nsem = ni * (N_DEV - 1)
        self.own = own

        nq = 3 if own else 2

        def body(*refs):
            ins, land_refs = refs[:ns], refs[ns:ns + nd]
            sems, token = refs[ns + nd:ns + nd + nq], refs[-1]
            me, peer = _mesh_place()
            for mask in _PEER_MASKS:
                for k in range(ni):
                    self._copy(k, mask, ins, land_refs, sems[0], sems[1], me, peer, False).start()
            if own:
                for k in range(ni):
                    self._own_copy(k, ins, land_refs, sems[2], me).start()
            token[...] = jnp.zeros_like(token)

        hbm = pl.BlockSpec(memory_space=pltpu.HBM)
        sem = pl.BlockSpec(memory_space=pltpu.SEMAPHORE)
        lands = [lax.empty(s.shape, s.dtype) for s in self.shapes]
        sem_shapes = [pltpu.SemaphoreType.DMA((nsem,)), pltpu.SemaphoreType.DMA((nsem,)), pltpu.SemaphoreType.DMA((ni,))]
        res = pl.pallas_call(
            body, in_specs=[hbm] * (ns + nd),
            out_specs=[sem] * nq + [hbm] * nd + [pl.BlockSpec(memory_space=pltpu.VMEM)],
            out_shape=sem_shapes[:nq] + [pltpu.HBM(s.shape, s.dtype) for s in self.shapes]
            + [jax.ShapeDtypeStruct((8, 128), f32)],
            input_output_aliases={ns + j: nq + j for j in range(nd)},
            compiler_params=pltpu.CompilerParams(has_side_effects=pltpu.SideEffectType.DATAFLOW_SIDE_EFFECTING),
            name=self.name + "_start")(*self.srcs, *lands)
        self.token = res[-1]
        return list(res[:nq]), list(res[nq:-1])

    def finish(self, state, after):
        sems, lands = state
        nq = len(sems)
        after = list(after) if isinstance(after, (list, tuple)) else [after]
        ns, nd, ni = len(self.srcs), len(self.shapes), len(self.items)

        def body(*refs):
            ins, land_refs = refs[:ns], refs[ns:ns + nd]
            sem_refs = refs[ns + nd:ns + nd + nq]
            me, peer = _mesh_place()
            for mask in _PEER_MASKS:
                for k in range(ni):
                    cp = self._copy(k, mask, ins, land_refs, sem_refs[0], sem_refs[1], me, peer, True)
                    cp.wait_send()
                    cp.wait_recv()
            if self.own:
                for k in range(ni):
                    self._own_copy(k, ins, land_refs, sem_refs[2], me).wait()

        hbm = pl.BlockSpec(memory_space=pltpu.HBM)
        sem = pl.BlockSpec(memory_space=pltpu.SEMAPHORE)
        res = pl.pallas_call(
            body, in_specs=[hbm] * (ns + nd) + [sem] * nq + [pl.BlockSpec(memory_space=pl.ANY)] * len(after),
            out_specs=[hbm] * nd, out_shape=[pltpu.HBM(s.shape, s.dtype) for s in self.shapes],
            input_output_aliases={ns + j: j for j in range(nd)},
            compiler_params=pltpu.CompilerParams(has_side_effects=pltpu.SideEffectType.DATAFLOW_SIDE_EFFECTING),
            name=self.name + "_wait")(*self.srcs, *lands, *sems, *after)
        return {k: res[i] for k, i in self.where.items()}


def _after(x, *tokens, name):
    def body(*refs):
        del refs

    anyspace = pl.BlockSpec(memory_space=pl.ANY)
    return pl.pallas_call(body, in_specs=[anyspace] * (1 + len(tokens)), out_specs=anyspace,
                          out_shape=jax.ShapeDtypeStruct(x.shape, x.dtype), input_output_aliases={0: 0},
                          name=name)(x, *tokens)


def _rows_of(n):
    return lambda r, i: r.at[pl.ds(pl.multiple_of(i * n, n), n), :]


def _cols_of(n):
    return lambda r, i: r.at[:, pl.ds(pl.multiple_of(i * n, n), n)]


def _whole(r, i):
    return r


def _slot(r, i):
    return r.at[i]


def _at_layer(layer):
    return lambda r, i: r.at[layer]


def _slot_layer(layer):
    return lambda r, i: r.at[i, layer]


def _sum_adam(me_index, slots, owns, own_block, w, m, v, *, name):
    layers, rows, cols = w.shape
    tr = _pick(rows, (256, 128, 104, 64, 32, 16, 8))
    bc1 = 1.0 - ADAM_B1 ** ADAM_STEP
    bc2 = 1.0 - ADAM_B2 ** ADAM_STEP
    own_shape, own_map = own_block(tr)
    nl = len(slots)
    assert nl == layers and len(owns) == layers

    def body(me_ref, *refs):
        s_refs, own_refs = refs[:nl], refs[nl:2 * nl]
        w_ref, m_ref, v_ref, g_ref, d_ref, nm_ref, nv_ref = refs[2 * nl:]
        me = me_ref[0]

        def run(s_ref, own_ref):
            mine = own_ref[0] if len(own_shape) == 3 else own_ref[...]
            g = jnp.where(me == 0, mine, s_ref[0])
            for k in range(1, N_DEV):
                g = g + jnp.where(me == k, mine, s_ref[k])
            mm = ADAM_B1 * m_ref[0] + (1.0 - ADAM_B1) * g
            vv = ADAM_B2 * v_ref[0] + (1.0 - ADAM_B2) * (g * g)
            g_ref[0] = g
            nm_ref[0] = mm
            nv_ref[0] = vv
            d_ref[0] = -ADAM_LR * ((mm / bc1) / (jnp.sqrt(vv / bc2) + ADAM_EPS) + ADAM_WD * w_ref[0])

        for layer in range(nl):
            pl.when(pl.program_id(0) == layer)(functools.partial(run, s_refs[layer], own_refs[layer]))

    def of_layer(layer, index_map):
        return lambda lyr, i, me: index_map(jnp.where(lyr == layer, i, 0), me)

    blk = pl.BlockSpec((1, tr, cols), lambda lyr, i, me: (lyr, i, 0))
    sds = jax.ShapeDtypeStruct((layers, rows, cols), f32)
    grid_spec = pltpu.PrefetchScalarGridSpec(
        num_scalar_prefetch=1, grid=(layers, rows // tr),
        in_specs=[pl.BlockSpec((N_DEV, tr, cols), of_layer(layer, lambda i, me: (0, i, 0))) for layer in range(nl)]
        + [pl.BlockSpec(own_shape, of_layer(layer, own_map)) for layer in range(nl)] + [blk, blk, blk],
        out_specs=[blk] * 4)
    return pl.pallas_call(body, grid_spec=grid_spec, out_shape=[sds] * 4,
                          compiler_params=_cparams(("arbitrary", "arbitrary")),
                          name=name)(me_index, *slots, *owns, w, m, v)


_SHARDED = dict(xa_wq=1, xa_wk=1, xa_wv=1, xa_wo=1, ffn_w_up=2, ffn_conv_w=2, ffn_w_down=1, mix_w_in=2, mla_w_uq=2,
                mla_w_ukv=2, mix_w_out=1, s5_d=1, s5_w_glu_a=1, s5_w_glu_b=1)
_EXACT = ("ffn_conv_w", "s5_d")
_WEIGHTS = ("norm_mix", "norm_xa", "norm_mem", "norm_ffn", "xa_wq", "xa_wk", "xa_wv", "xa_wo", "xa_q_norm",
            "xa_k_norm", "ffn_w_up", "ffn_conv_w", "ffn_conv_b", "ffn_w_down", "hg_lb_logits", "mix_w_in",
            "hg_out_norm", "mla_q_a_norm", "mla_w_uq", "mla_kv_a_norm", "mla_w_ukv", "mla_qn_nope", "mla_qn_rope",
            "mla_kn_nope", "mla_kn_rope", "mix_w_out", "s5_lam_re", "s5_lam_im", "s5_log_dt", "s5_b_re", "s5_b_im",
            "s5_c_re", "s5_c_im", "s5_d", "s5_w_glu_a", "s5_w_glu_b")
_BIG = tuple(n for n in _WEIGHTS if n in _SHARDED and n not in _EXACT)
_SHARD_ORDER = tuple(n for n in _WEIGHTS if n in _SHARDED)
_REPL_ORDER = tuple(n for n in _WEIGHTS if n not in _SHARDED)
_REPL_EARLY = tuple(n for n in _REPL_ORDER if n.startswith("s5_"))
_REPL_LATE = tuple(n for n in _REPL_ORDER if n not in _REPL_EARLY)


def _pack(parts, dtype, lead=None):
    nl = 0 if lead is None else 1
    flat = [a.astype(dtype).reshape(a.shape[:nl] + (-1,)) for a in parts]
    cat = jnp.concatenate(flat, axis=nl)
    n = cat.shape[nl]
    unit = _LANES * _ROW_PAD
    total = -(-n // unit) * unit
    cat = jnp.pad(cat, [(0, 0)] * nl + [(0, total - n)])
    return cat.reshape(cat.shape[:nl] + (total // _LANES, _LANES))


def _unpack(packed, shapes, lead=None):
    nl = 0 if lead is None else 1
    flat = packed.reshape(packed.shape[:nl] + (-1,))
    out, off = [], 0
    for s in shapes:
        n = int(np.prod(s))
        piece = flat[..., off:off + n] if nl else flat[off:off + n]
        out.append(piece.reshape(packed.shape[:nl] + tuple(s)))
        off += n
    return out


def _to_full(gathered, axis):
    g = jnp.moveaxis(gathered, 0, axis)
    s = g.shape
    return g.reshape(s[:axis] + (s[axis] * s[axis + 1],) + s[axis + 2:])


def _to_shards(full, axis):
    s = full.shape
    g = full.reshape(s[:axis] + (N_DEV, s[axis] // N_DEV) + s[axis + 1:])
    return jnp.moveaxis(g, axis, 0)


_DIRECT_ROWS = ("xa_wq", "xa_wk", "xa_wv", "xa_wo", "mix_w_out", "s5_w_glu_a", "s5_w_glu_b")
_SMALL16 = ("mix_w_in", "mla_w_uq", "mla_w_ukv")
_SMALL_SHARDED = ("mla_w_uq", "mla_w_ukv") + _EXACT
_SHARD_ROWS = 128


def _exchange_layout(d):
    out = dict(d)
    out["ffn_w_up"] = _pad_groups(d["ffn_w_up"], 2)
    out["ffn_conv_w"] = _pad_groups(d["ffn_conv_w"], 2)
    out["ffn_w_down"] = _pad_groups(d["ffn_w_down"], 1)
    return out


def _train_step(x, mem, positions, target, w, m, v):
    d_model = x.shape[1]
    we_, me_, ve_ = _exchange_layout(w), _exchange_layout(m), _exchange_layout(v)
    sds = jax.ShapeDtypeStruct

    matrices = _DIRECT_ROWS + ("ffn_w_up", "ffn_w_down")
    layer_mats = ("xa_wq", "xa_wk", "xa_wv", "xa_wo", "ffn_w_up", "ffn_w_down")
    shard16 = {n: we_[n].astype(bf16) for n in matrices}
    part_of = {n: _rows_of(_SHARD_ROWS) for n in _DIRECT_ROWS}
    part_of["ffn_w_up"] = _cols_of(we_["ffn_w_up"].shape[2])
    part_of["ffn_w_down"] = _rows_of(we_["ffn_w_down"].shape[1])
    part_shape = {n: we_[n].shape[1:] for n in matrices}

    def full_shape(n):
        r, c = part_shape[n]
        return (r, N_DEV * c) if n == "ffn_w_up" else (N_DEV * r, c)

    def gather(ex, n, layer):
        ex.add(shard16[n], sds(full_shape(n), bf16), _at_layer(layer), part_of[n], (n, layer))

    def scatter(ex, n, layer, grad):
        ex.add(grad, sds((N_DEV,) + part_shape[n], f32), part_of[n], _slot, (n, layer))

    small16 = _pack([we_[n] for n in _SMALL16], bf16)
    exact = _pack([we_[n] for n in _EXACT], f32)
    ga, ga1, gb, gc = _Exchange("gather_a"), _Exchange("gather_a1"), _Exchange("gather_b"), _Exchange("gather_c")
    ga.add(small16, sds((N_DEV,) + small16.shape, bf16), _whole, _slot, "small16")
    ga1.add(exact, sds((N_DEV,) + exact.shape, f32), _whole, _slot, "exact")
    gather(ga1, "mix_w_out", 0)
    for n in layer_mats:
        gather(gb, n, 0)
    gather(gc, "s5_w_glu_a", 0)
    gather(gc, "s5_w_glu_b", 0)
    for n in layer_mats:
        gather(gc, n, 1)
    state_a, state_a1, state_b, state_c = ga.begin(True), ga1.begin(True), gb.begin(True), gc.begin(True)

    full = ga.finish(state_a, [ga1.token, gb.token, gc.token])
    p = {n: w[n] for n in _REPL_ORDER}
    for n, a in zip(_SMALL16, _unpack(full["small16"], [we_[n].shape for n in _SMALL16], lead=True)):
        p[n] = _to_full(a, _SHARDED[n])
    cos, sin = _rope_tables(positions)
    we = _even_weights(p, 0, 0, bf16)
    we["norm_mix"] = _after(we["norm_mix"], ga.token, ga1.token, gb.token, gc.token, name="after_gather_starts")

    def late_mix_w_out(mixin):
        full.update(ga1.finish(state_a1, [mixin]))
        return full[("mix_w_out", 0)]

    we["mix_w_out"] = late_mix_w_out
    h, s_mix0 = _mixer_fwd(x, cos, sin, we, "l0")
    conv_w, p["s5_d"] = [_to_full(a, _SHARDED[n]) for n, a in
                         zip(_EXACT, _unpack(full["exact"], [we_[n].shape for n in _EXACT], lead=True))]
    conv_b = _pad_groups(w["ffn_conv_b"], 1)

    def layer_weights(layer):
        return dict(norm_xa=w["norm_xa"][layer][None], norm_mem=w["norm_mem"][layer][None],
                    norm_ffn=w["norm_ffn"][layer][None], xa_q_norm=w["xa_q_norm"][layer][None],
                    xa_k_norm=w["xa_k_norm"][layer][None], ffn_conv_w=conv_w[layer],
                    ffn_conv_b=conv_b[layer][None], **{n: full[(n, layer)] for n in layer_mats})

    full.update(gb.finish(state_b, h))
    wl = [layer_weights(0)]
    h, s_xa0 = _xattn_fwd(h, mem, wl[0], "l0")
    h, s_ff0 = _ffn_fwd(h, wl[0], "l0")
    full.update(gc.finish(state_c, h))
    wl.append(layer_weights(1))
    p["s5_w_glu_a"], p["s5_w_glu_b"] = full[("s5_w_glu_a", 0)][None], full[("s5_w_glu_b", 0)][None]
    wo = _odd_weights(p, 0, 1, bf16)
    h, s_mix1 = _s5_fwd(h, wo, "l1")
    h, s_xa1 = _xattn_fwd(h, mem, wl[1], "l1")
    h, s_ff1 = _ffn_fwd(h, wl[1], "l1")
    dh, loss = _rows(_loss_fn, [h, target], [], [(h.shape[1], f32)], accs=[(1, 1)], name="loss_head")

    gl = [{}, {}]
    reduces = []

    own_grad = {}

    def reduce_start(name, entries, dh):
        ex = _Exchange(name)
        for n, layer, grad in entries.get("matrices", ()):
            scatter(ex, n, layer, grad)
            own_grad[(n, layer)] = grad
        for key, src, shape, src_at in entries.get("packs", ()):
            ex.add(src, shape, src_at, _slot, key)
        reduces.append((ex, ex.begin(False)))
        return _after(dh, ex.token, name="after_" + name)

    dh, gl[1] = _ffn_bwd(dh, wl[1], s_ff1, "l1")
    dh = reduce_start("reduce_ffn1", dict(matrices=[(n, 1, gl[1][n]) for n in ("ffn_w_up", "ffn_w_down")]), dh)
    dh, g = _xattn_bwd(dh, mem, wl[1], s_xa1, "l1")
    gl[1].update(g)
    dh = reduce_start("reduce_xa1", dict(matrices=[(n, 1, g[n]) for n in ("xa_wq", "xa_wk", "xa_wv", "xa_wo")]), dh)
    dh, g_odd = _s5_bwd(dh, wo, s_mix1, "l1")
    go = _odd_grads(g_odd)
    dh, gl[0] = _ffn_bwd(dh, wl[0], s_ff0, "l0")
    send_early = _pack([go[n].reshape(w[n].shape) for n in _REPL_EARLY], f32)
    dh = reduce_start("reduce_ffn0", dict(
        matrices=[(n, 0, g_odd[n]) for n in ("s5_w_glu_a", "s5_w_glu_b")]
        + [(n, 0, gl[0][n]) for n in ("ffn_w_up", "ffn_w_down")],
        packs=[("repl_early", send_early, sds((N_DEV,) + send_early.shape, f32), _whole)]), dh)
    dh, g = _xattn_bwd(dh, mem, wl[0], s_xa0, "l0")
    gl[0].update(g)
    dh = reduce_start("reduce_xa0", dict(matrices=[(n, 0, g[n]) for n in ("xa_wq", "xa_wk", "xa_wv", "xa_wo")]), dh)
    grad_x, g_even = _mixer_bwd(
        dh, cos, sin, we, s_mix0, "l0",
        on_w_out=lambda grad, dmixin: reduce_start("reduce_w_out", dict(matrices=[("mix_w_out", 0, grad)]), dmixin))

    ge = _even_grads(g_even)
    cat = lambda n: jnp.concatenate([gl[0][n], gl[1][n]], axis=0)
    rg = dict(ge)
    rg["norm_mix"] = jnp.concatenate([ge["norm_mix"], go["norm_mix"]], axis=0)
    for n in ("norm_xa", "norm_mem", "norm_ffn", "xa_q_norm", "xa_k_norm"):
        rg[n] = cat(n)
    rg["ffn_conv_b"] = _unpad_groups(cat("ffn_conv_b"), 1)
    sg = dict(mla_w_uq=ge["mla_w_uq"], mla_w_ukv=ge["mla_w_ukv"], s5_d=go["s5_d"],
              ffn_conv_w=jnp.stack([gl[0]["ffn_conv_w"], gl[1]["ffn_conv_w"]]))
    send_small = _pack([_to_shards(sg[n], _SHARDED[n]) for n in _SMALL_SHARDED], f32, lead=True)
    send_late = _pack([rg[n].reshape(w[n].shape) for n in _REPL_LATE], f32)
    w_in_rows = w["mix_w_in"].shape[2]
    last = _Exchange("reduce_last")
    last.add(g_even["mix_w_in_t"], sds((N_DEV, w_in_rows, d_model), f32), _rows_of(w_in_rows), _slot, "mix_w_in")
    last.add(send_small, sds(send_small.shape, f32), _slot, _slot, "small")
    last.add(send_late, sds((N_DEV,) + send_late.shape, f32), _whole, _slot, "repl_late")
    state_last = last.begin(False)
    slots = {}
    for ex, state in reduces:
        slots.update(ex.finish(state, [grad_x, last.token]))

    me_index = (4 * lax.axis_index("x") + 2 * lax.axis_index("y") + lax.axis_index("c")).astype(jnp.int32).reshape(1)

    def rows_block(r, c):
        return lambda tr: ((tr, c), lambda i, me: (me[0] * (r // tr) + i, 0))

    def own_block(n):
        r, c = part_shape[n]
        if n == "ffn_w_up":
            return lambda tr: ((tr, c), lambda i, me: (i, me[0]))
        return rows_block(r, c)

    out = [{}, {}, {}, {}]
    unpad = dict(ffn_w_up=2, ffn_conv_w=2, ffn_w_down=1)
    for n in matrices:
        layers = range(we_[n].shape[0])
        res = _sum_adam(me_index, [slots[(n, layer)] for layer in layers], [own_grad[(n, layer)] for layer in layers],
                        own_block(n), we_[n], me_[n], ve_[n], name=f"adam_{n}")
        for k in range(4):
            out[k][n] = _unpad_groups(res[k], unpad[n]) if n in unpad else res[k]
    pk = lambda d, order: _pack([d[n] for n in order], f32)[None]
    whole_rows = lambda tr: ((tr, _LANES), lambda i, me: (i, 0))
    res_early = _sum_adam(me_index, [slots["repl_early"]], [send_early], whole_rows, pk(w, _REPL_EARLY),
                          pk(m, _REPL_EARLY), pk(v, _REPL_EARLY), name="adam_repl_early")
    for k in range(4):
        out[k].update(zip(_REPL_EARLY, _unpack(res_early[k][0], [w[n].shape for n in _REPL_EARLY])))
    done = [out[k][n] for k in range(4) for n in matrices + _REPL_EARLY]
    slots = last.finish(state_last, done)
    transposed = lambda d: jnp.swapaxes(d["mix_w_in"], 1, 2)
    res_w_in = _sum_adam(me_index, [slots["mix_w_in"]], [g_even["mix_w_in_t"]], rows_block(w_in_rows, d_model),
                         transposed(w), transposed(m), transposed(v), name="adam_mix_w_in")
    res_small = _sum_adam(me_index, [slots["small"]], [send_small],
                          lambda tr: ((1, tr, _LANES), lambda i, me: (me[0], i, 0)),
                          pk(we_, _SMALL_SHARDED), pk(me_, _SMALL_SHARDED), pk(ve_, _SMALL_SHARDED), name="adam_small")
    res_late = _sum_adam(me_index, [slots["repl_late"]], [send_late], whole_rows, pk(w, _REPL_LATE), pk(m, _REPL_LATE),
                         pk(v, _REPL_LATE), name="adam_repl_late")
    for k in range(4):
        out[k]["mix_w_in"] = jnp.swapaxes(res_w_in[k], 1, 2)
        for n, a in zip(_SMALL_SHARDED, _unpack(res_small[k][0], [we_[n].shape for n in _SMALL_SHARDED])):
            out[k][n] = _unpad_groups(a, unpad[n]) if n in unpad else a
        out[k].update(zip(_REPL_LATE, _unpack(res_late[k][0], [w[n].shape for n in _REPL_LATE])))
    return loss, grad_x, out


_INPUTS = tuple("""x, mem, positions, norm_mix, norm_xa, norm_mem, norm_ffn, xa_wq, xa_wk, xa_wv, xa_wo, xa_q_norm, xa_k_norm, ffn_w_up, ffn_conv_w, ffn_conv_b, ffn_w_down, hg_lb_logits, mix_w_in, hg_out_norm, mla_q_a_norm, mla_w_uq, mla_kv_a_norm, mla_w_ukv, mla_qn_nope, mla_qn_rope, mla_kn_nope, mla_kn_rope, mix_w_out, s5_lam_re, s5_lam_im, s5_log_dt, s5_b_re, s5_b_im, s5_c_re, s5_c_im, s5_d, s5_w_glu_a, s5_w_glu_b, loss_target, m_norm_mix, m_norm_xa, m_norm_mem, m_norm_ffn, m_xa_wq, m_xa_wk, m_xa_wv, m_xa_wo, m_xa_q_norm, m_xa_k_norm, m_ffn_w_up, m_ffn_conv_w, m_ffn_conv_b, m_ffn_w_down, m_hg_lb_logits, m_mix_w_in, m_hg_out_norm, m_mla_q_a_norm, m_mla_w_uq, m_mla_kv_a_norm, m_mla_w_ukv, m_mla_qn_nope, m_mla_qn_rope, m_mla_kn_nope, m_mla_kn_rope, m_mix_w_out, m_s5_lam_re, m_s5_lam_im, m_s5_log_dt, m_s5_b_re, m_s5_b_im, m_s5_c_re, m_s5_c_im, m_s5_d, m_s5_w_glu_a, m_s5_w_glu_b, v_norm_mix, v_norm_xa, v_norm_mem, v_norm_ffn, v_xa_wq, v_xa_wk, v_xa_wv, v_xa_wo, v_xa_q_norm, v_xa_k_norm, v_ffn_w_up, v_ffn_conv_w, v_ffn_conv_b, v_ffn_w_down, v_hg_lb_logits, v_mix_w_in, v_hg_out_norm, v_mla_q_a_norm, v_mla_w_uq, v_mla_kv_a_norm, v_mla_w_ukv, v_mla_qn_nope, v_mla_qn_rope, v_mla_kn_nope, v_mla_kn_rope, v_mix_w_out, v_s5_lam_re, v_s5_lam_im, v_s5_log_dt, v_s5_b_re, v_s5_b_im, v_s5_c_re, v_s5_c_im, v_s5_d, v_s5_w_glu_a, v_s5_w_glu_b""".replace(" ", "").split(","))


def kernel(x, mem, positions, norm_mix, norm_xa, norm_mem, norm_ffn, xa_wq, xa_wk, xa_wv, xa_wo, xa_q_norm, xa_k_norm, ffn_w_up, ffn_conv_w, ffn_conv_b, ffn_w_down, hg_lb_logits, mix_w_in, hg_out_norm, mla_q_a_norm, mla_w_uq, mla_kv_a_norm, mla_w_ukv, mla_qn_nope, mla_qn_rope, mla_kn_nope, mla_kn_rope, mix_w_out, s5_lam_re, s5_lam_im, s5_log_dt, s5_b_re, s5_b_im, s5_c_re, s5_c_im, s5_d, s5_w_glu_a, s5_w_glu_b, loss_target, m_norm_mix, m_norm_xa, m_norm_mem, m_norm_ffn, m_xa_wq, m_xa_wk, m_xa_wv, m_xa_wo, m_xa_q_norm, m_xa_k_norm, m_ffn_w_up, m_ffn_conv_w, m_ffn_conv_b, m_ffn_w_down, m_hg_lb_logits, m_mix_w_in, m_hg_out_norm, m_mla_q_a_norm, m_mla_w_uq, m_mla_kv_a_norm, m_mla_w_ukv, m_mla_qn_nope, m_mla_qn_rope, m_mla_kn_nope, m_mla_kn_rope, m_mix_w_out, m_s5_lam_re, m_s5_lam_im, m_s5_log_dt, m_s5_b_re, m_s5_b_im, m_s5_c_re, m_s5_c_im, m_s5_d, m_s5_w_glu_a, m_s5_w_glu_b, v_norm_mix, v_norm_xa, v_norm_mem, v_norm_ffn, v_xa_wq, v_xa_wk, v_xa_wv, v_xa_wo, v_xa_q_norm, v_xa_k_norm, v_ffn_w_up, v_ffn_conv_w, v_ffn_conv_b, v_ffn_w_down, v_hg_lb_logits, v_mix_w_in, v_hg_out_norm, v_mla_q_a_norm, v_mla_w_uq, v_mla_kv_a_norm, v_mla_w_ukv, v_mla_qn_nope, v_mla_qn_rope, v_mla_kn_nope, v_mla_kn_rope, v_mix_w_out, v_s5_lam_re, v_s5_lam_im, v_s5_log_dt, v_s5_b_re, v_s5_b_im, v_s5_c_re, v_s5_c_im, v_s5_d, v_s5_w_glu_a, v_s5_w_glu_b):
    vals = dict(zip(_INPUTS, (x, mem, positions, norm_mix, norm_xa, norm_mem, norm_ffn, xa_wq, xa_wk, xa_wv, xa_wo, xa_q_norm, xa_k_norm, ffn_w_up, ffn_conv_w, ffn_conv_b, ffn_w_down, hg_lb_logits, mix_w_in, hg_out_norm, mla_q_a_norm, mla_w_uq, mla_kv_a_norm, mla_w_ukv, mla_qn_nope, mla_qn_rope, mla_kn_nope, mla_kn_rope, mix_w_out, s5_lam_re, s5_lam_im, s5_log_dt, s5_b_re, s5_b_im, s5_c_re, s5_c_im, s5_d, s5_w_glu_a, s5_w_glu_b, loss_target, m_norm_mix, m_norm_xa, m_norm_mem, m_norm_ffn, m_xa_wq, m_xa_wk, m_xa_wv, m_xa_wo, m_xa_q_norm, m_xa_k_norm, m_ffn_w_up, m_ffn_conv_w, m_ffn_conv_b, m_ffn_w_down, m_hg_lb_logits, m_mix_w_in, m_hg_out_norm, m_mla_q_a_norm, m_mla_w_uq, m_mla_kv_a_norm, m_mla_w_ukv, m_mla_qn_nope, m_mla_qn_rope, m_mla_kn_nope, m_mla_kn_rope, m_mix_w_out, m_s5_lam_re, m_s5_lam_im, m_s5_log_dt, m_s5_b_re, m_s5_b_im, m_s5_c_re, m_s5_c_im, m_s5_d, m_s5_w_glu_a, m_s5_w_glu_b, v_norm_mix, v_norm_xa, v_norm_mem, v_norm_ffn, v_xa_wq, v_xa_wk, v_xa_wv, v_xa_wo, v_xa_q_norm, v_xa_k_norm, v_ffn_w_up, v_ffn_conv_w, v_ffn_conv_b, v_ffn_w_down, v_hg_lb_logits, v_mix_w_in, v_hg_out_norm, v_mla_q_a_norm, v_mla_w_uq, v_mla_kv_a_norm, v_mla_w_ukv, v_mla_qn_nope, v_mla_qn_rope, v_mla_kn_nope, v_mla_kn_rope, v_mix_w_out, v_s5_lam_re, v_s5_lam_im, v_s5_log_dt, v_s5_b_re, v_s5_b_im, v_s5_c_re, v_s5_c_im, v_s5_d, v_s5_w_glu_a, v_s5_w_glu_b)))
    w = {n: vals[n] for n in _WEIGHTS}
    m = {n: vals["m_" + n] for n in _WEIGHTS}
    v = {n: vals["v_" + n] for n in _WEIGHTS}
    loss, grad_x, res = _train_step(vals["x"][0], vals["mem"][0], vals["positions"][0], vals["loss_target"][0],
                                    w, m, v)
    loss = lax.psum(loss[0, 0], ("x", "y", "c"))
    return (loss, grad_x[None], *[r[n] for r in res for n in _WEIGHTS])
```

```python
import functools

import jax
import jax.numpy as jnp
import numpy as np
from jax import lax
from jax.experimental import pallas as pl
from jax.experimental.pallas import tpu as pltpu

f32 = jnp.float32
bf16 = jnp.bfloat16

EPS = 1e-6
N_DEV = 8
VMEM_LIMIT = 52 * 1024 * 1024

HG_HEADS = 4
HG_DIM = 128
HG_WIDTH = HG_HEADS * HG_DIM
HG_CHUNK = 64
HG_SUB = 16
MLA_HEADS = 4
MLA_Q_RANK = 256
MLA_KV_RANK = 128
MLA_NOPE = 128
MLA_ROPE = 64
MLA_V = 128
MLA_QK = MLA_NOPE + MLA_ROPE
MLA_QK_PAD = 256
ROPE_BASE = 10000.0
IN_WIDTH = 4 * HG_WIDTH + MLA_Q_RANK + MLA_KV_RANK + MLA_ROPE
IN_PAD = 2560
XA_HEADS = 4
XA_DIM = 256
S5_GROUP = 16
S5_GROUPS = 64
S5_STATE = 64
CONV_W = 3

ADAM_LR = 0.001
ADAM_B1 = 0.9
ADAM_B2 = 0.999
ADAM_EPS = 1e-08
ADAM_WD = 0.01
ADAM_STEP = 10

_NT = (((1,), (1,)), ((), ()))
_TN = (((0,), (0,)), ((), ()))
_NN = (((1,), (0,)), ((), ()))


def _pick(n, cands):
    for c in cands:
        if n % c == 0:
            return c
    return n


def _cparams(sem):
    return pltpu.CompilerParams(dimension_semantics=sem, vmem_limit_bytes=VMEM_LIMIT)


_MM_BUDGET = 36 * 1024 * 1024
_MM_TILES = ((1024, 1024), (1024, 512), (512, 1024), (512, 512), (512, 256), (256, 512), (256, 256), (256, 128),
             (128, 256), (128, 128))


def _mm(a, b, *, name, ta=False, tb=False, out_dtype=f32, add=None, b2=None, kslab=None):
    m, k = (a.shape[1], a.shape[0]) if ta else a.shape
    nb = b.shape[0] if tb else b.shape[1]
    n = nb * (2 if b2 is not None else 1)
    slab, nslab = kslab if kslab is not None else (0, 1)
    assert (b.shape[1] // nslab if tb else b.shape[0]) == k, (a.shape, b.shape, ta, tb)
    assert b2 is None or (not tb and b2.shape == b.shape)
    isz = lambda x: jnp.dtype(x.dtype).itemsize
    bm = bn = None
    for cm, cn in _MM_TILES:
        if m % cm or nb % cn:
            continue
        need = 2 * (cm * k * isz(a) + cn * k * isz(b) * (2 if b2 is not None else 1)
                    + cm * cn * (jnp.dtype(out_dtype).itemsize + (4 if add is not None else 0)))
        if need <= _MM_BUDGET:
            bm, bn = cm, cn
            break
    assert bm is not None, (name, a.shape, b.shape)
    half = nb // bn
    dims = (((0 if ta else 1,), (1 if tb else 0,)), ((), ()))

    def body(*refs):
        refs = list(refs)
        a_ref, b_ref = refs[0], refs[1]
        b2_ref = refs.pop(2) if b2 is not None else None
        add_ref = refs[2] if add is not None else None
        o_ref = refs[-1]

        def run(rhs_ref):
            r = lax.dot_general(a_ref[...].astype(bf16), rhs_ref[...].astype(bf16), dims, preferred_element_type=f32)
            if add_ref is not None:
                r = r + add_ref[...].astype(f32)
            o_ref[...] = r.astype(o_ref.dtype)

        if b2_ref is None:
            run(b_ref)
        else:
            pl.when(pl.program_id(1) < half)(lambda: run(b_ref))
            pl.when(pl.program_id(1) >= half)(lambda: run(b2_ref))

    a_spec = pl.BlockSpec((k, bm), lambda i, j: (0, i)) if ta else pl.BlockSpec((bm, k), lambda i, j: (i, 0))
    if tb:
        b_spec = pl.BlockSpec((bn, k), lambda i, j: (j, slab))
    elif b2 is None:
        b_spec = pl.BlockSpec((k, bn), lambda i, j: (0, j))
    else:
        b_spec = pl.BlockSpec((k, bn), lambda i, j: (0, jnp.minimum(j, half - 1)))
    in_specs = [a_spec, b_spec]
    args = [a, b]
    if b2 is not None:
        in_specs.append(pl.BlockSpec((k, bn), lambda i, j: (0, jnp.maximum(j - half, 0))))
        args.append(b2)
    if add is not None:
        in_specs.append(pl.BlockSpec((bm, bn), lambda i, j: (i, j)))
        args.append(add)
    return pl.pallas_call(
        body, grid=(m // bm, n // bn), in_specs=in_specs,
        out_specs=pl.BlockSpec((bm, bn), lambda i, j: (i, j)),
        out_shape=jax.ShapeDtypeStruct((m, n), out_dtype),
        compiler_params=_cparams(("parallel", "parallel")), name=name)(*args)


def _as_tuple(x):
    return tuple(x) if isinstance(x, (tuple, list)) else (x,)


def _full_spec(p):
    nd = p.ndim
    return pl.BlockSpec(p.shape, lambda i, _nd=nd: (0,) * _nd)


def _window(a, start, width):
    assert start % width == 0 and width % 128 == 0
    return (a, start // width, width)


def _row_array(x):
    return x[0] if isinstance(x, tuple) else x


def _row_shape(x):
    return (x[0].shape[0], x[2]) if isinstance(x, tuple) else x.shape


def _row_spec(x, tile):
    if isinstance(x, tuple):
        return pl.BlockSpec((tile, x[2]), lambda i, _b=x[1]: (i, _b))
    return pl.BlockSpec((tile, x.shape[1]), lambda i: (i, 0))


def _rows(fn, rows, params, outs, *, name, tile=256, accs=()):
    length = _row_shape(rows[0])[0]
    tile = min(tile, length)
    nr, npar, no = len(rows), len(params), len(outs)

    def body(*refs):
        r, p, o = refs[:nr], refs[nr:nr + npar], refs[nr + npar:]
        res = _as_tuple(fn(*[x[...].astype(f32) for x in r], *[x[...] for x in p]))
        for kk in range(no):
            o[kk][...] = res[kk].astype(o[kk].dtype)
        if accs:
            @pl.when(pl.program_id(0) == 0)
            def _():
                for kk in range(no, no + len(accs)):
                    o[kk][...] = jnp.zeros_like(o[kk])
            for kk in range(no, no + len(accs)):
                o[kk][...] += res[kk]

    in_specs = [_row_spec(x, tile) for x in rows] + [_full_spec(p) for p in params]
    out_specs = [pl.BlockSpec((tile, w), lambda i: (i, 0)) for w, _ in outs]
    out_shape = [jax.ShapeDtypeStruct((length, w), d) for w, d in outs]
    for s in accs:
        out_specs.append(pl.BlockSpec(s, lambda i, _nd=len(s): (0,) * _nd))
        out_shape.append(jax.ShapeDtypeStruct(s, f32))
    res = pl.pallas_call(body, grid=(length // tile,), in_specs=in_specs, out_specs=out_specs, out_shape=out_shape,
                         compiler_params=_cparams(("arbitrary",)), name=name)(*[_row_array(x) for x in rows], *params)
    return res


def _rows_bwd(fn, rows, params, cts, *, name, rgrad, pgrad, tile=256, addends=None):
    addends = {i: (a if isinstance(a, list) else [(a, 0)]) for i, a in (addends or {}).items()}
    length = _row_shape(rows[0])[0]
    tile = min(tile, length)
    nr, npar, nc = len(rows), len(params), len(cts)
    ridx = [i for i in range(nr) if rgrad[i] is not None]
    pidx = [i for i in range(npar) if pgrad[i]]
    flat_addends = [(i, a, off) for i in sorted(addends) for a, off in addends[i]]
    na = len(flat_addends)

    def body(*refs):
        r, p, c = refs[:nr], refs[nr:nr + npar], refs[nr + npar:nr + npar + nc]
        ad = refs[nr + npar + nc:nr + npar + nc + na]
        o = refs[nr + npar + nc + na:]
        rv = [x[...].astype(f32) for x in r]
        pv = [x[...] for x in p]
        cv = tuple(x[...].astype(f32) for x in c)

        def g(*d):
            rr, pp = list(rv), list(pv)
            for n_, i_ in enumerate(ridx):
                rr[i_] = d[n_]
            for n_, i_ in enumerate(pidx):
                pp[i_] = d[len(ridx) + n_]
            return _as_tuple(fn(*rr, *pp))

        _, vjp = jax.vjp(g, *[rv[i] for i in ridx], *[pv[i] for i in pidx])
        grads = vjp(cv)
        for n_, i_ in enumerate(ridx):
            val = grads[n_]
            for k_, (j_, a_, off) in enumerate(flat_addends):
                if j_ == i_:
                    extra = ad[k_][...].astype(f32)
                    if extra.shape[1] != val.shape[1]:
                        extra = jnp.pad(extra, ((0, 0), (off, val.shape[1] - off - extra.shape[1])))
                    val = val + extra
            o[n_][...] = val.astype(o[n_].dtype)
        if pidx:
            @pl.when(pl.program_id(0) == 0)
            def _():
                for n_ in range(len(pidx)):
                    o[len(ridx) + n_][...] = jnp.zeros_like(o[len(ridx) + n_])
            for n_ in range(len(pidx)):
                o[len(ridx) + n_][...] += grads[len(ridx) + n_]

    plain = lambda shape: pl.BlockSpec((tile, shape[1]), lambda i: (i, 0))
    in_specs = ([_row_spec(x, tile) for x in rows] + [_full_spec(p) for p in params] + [plain(x.shape) for x in cts]
                + [plain(a.shape) for _, a, _ in flat_addends])
    out_specs = [plain(_row_shape(rows[i])) for i in ridx] + [_full_spec(params[i]) for i in pidx]
    out_shape = ([jax.ShapeDtypeStruct(_row_shape(rows[i]), rgrad[i]) for i in ridx]
                 + [jax.ShapeDtypeStruct(params[i].shape, f32) for i in pidx])
    res = pl.pallas_call(body, grid=(length // tile,), in_specs=in_specs, out_specs=out_specs, out_shape=out_shape,
                         compiler_params=_cparams(("arbitrary",)), name=name)(
        *[_row_array(x) for x in rows], *params, *cts, *[a for _, a, _ in flat_addends])
    return list(res[:len(ridx)]), list(res[len(ridx):])


def _rms(x, g):
    return x * lax.rsqrt(jnp.mean(x * x, axis=-1, keepdims=True) + EPS) * g


def _rms_twice(x, g):
    y = _rms(x, g)
    return y, y


def _silu(x):
    return x * jax.nn.sigmoid(x)


def _shift_down(x, s):
    rows = lax.broadcasted_iota(jnp.int32, x.shape, 0)
    return jnp.where(rows >= s, pltpu.roll(x, s, axis=0), 0.0)


def _shift_up(x, s):
    n = x.shape[0]
    rows = lax.broadcasted_iota(jnp.int32, x.shape, 0)
    return jnp.where(rows < n - s, pltpu.roll(x, n - s, axis=0), 0.0)


_CONV_COLS = 128


def _conv_gate_fwd(u, cw, cb, *, name):
    length, two_f = u.shape
    ff = two_f // 2
    nb = ff // _CONV_COLS

    def body(ug, uv, wg, wv, bg, bv, o):
        def conv(x_ref, w_ref, b_ref):
            x = x_ref[...].astype(f32)
            return (w_ref[2:3, :] * x + w_ref[1:2, :] * _shift_down(x, 1) + w_ref[0:1, :] * _shift_down(x, 2)
                    + b_ref[...])
        o[...] = (_silu(conv(ug, wg, bg)) * conv(uv, wv, bv)).astype(o.dtype)

    blk = lambda r, off: pl.BlockSpec((r, _CONV_COLS), lambda j, _o=off: (0, j + _o))
    return pl.pallas_call(
        body, grid=(nb,),
        in_specs=[blk(length, 0), blk(length, nb), blk(CONV_W, 0), blk(CONV_W, nb), blk(1, 0), blk(1, nb)],
        out_specs=blk(length, 0), out_shape=jax.ShapeDtypeStruct((length, ff), bf16),
        compiler_params=_cparams(("parallel",)), name=name)(u, u, cw, cw, cb, cb)


def _conv_gate_bwd(u, cw, cb, da, *, name):
    length, two_f = u.shape
    ff = two_f // 2
    nb = ff // _CONV_COLS

    def body(ug, uv, wg, wv, bg, bv, da_ref, dug, duv, dwg, dwv, dbg, dbv):
        def conv(x, w_ref, b_ref):
            x1, x2 = _shift_down(x, 1), _shift_down(x, 2)
            return w_ref[2:3, :] * x + w_ref[1:2, :] * x1 + w_ref[0:1, :] * x2 + b_ref[...], x1, x2

        xg, xv = ug[...].astype(f32), uv[...].astype(f32)
        g, xg1, xg2 = conv(xg, wg, bg)
        v, xv1, xv2 = conv(xv, wv, bv)
        d = da_ref[...].astype(f32)
        sg = jax.nn.sigmoid(g)
        dg = d * v * (sg * (1.0 + g * (1.0 - sg)))
        dv = d * (g * sg)

        def back(dy, x, x1, x2, w_ref, du_ref, dw_ref, db_ref):
            du_ref[...] = (w_ref[2:3, :] * dy + w_ref[1:2, :] * _shift_up(dy, 1)
                           + w_ref[0:1, :] * _shift_up(dy, 2)).astype(du_ref.dtype)
            dw_ref[2:3, :] = jnp.sum(dy * x, axis=0, keepdims=True)
            dw_ref[1:2, :] = jnp.sum(dy * x1, axis=0, keepdims=True)
            dw_ref[0:1, :] = jnp.sum(dy * x2, axis=0, keepdims=True)
            db_ref[...] = jnp.sum(dy, axis=0, keepdims=True)

        back(dg, xg, xg1, xg2, wg, dug, dwg, dbg)
        back(dv, xv, xv1, xv2, wv, duv, dwv, dbv)

    blk = lambda r, off: pl.BlockSpec((r, _CONV_COLS), lambda j, _o=off: (0, j + _o))
    sds = jax.ShapeDtypeStruct
    dug, duv, dwg, dwv, dbg, dbv = pl.pallas_call(
        body, grid=(nb,),
        in_specs=[blk(length, 0), blk(length, nb), blk(CONV_W, 0), blk(CONV_W, nb), blk(1, 0), blk(1, nb),
                  blk(length, 0)],
        out_specs=[blk(length, 0), blk(length, 0), blk(CONV_W, 0), blk(CONV_W, 0), blk(1, 0), blk(1, 0)],
        out_shape=[sds((length, ff), bf16), sds((length, ff), bf16), sds((CONV_W, ff), f32), sds((CONV_W, ff), f32),
                   sds((1, ff), f32), sds((1, ff), f32)],
        compiler_params=_cparams(("parallel",)), name=name)(u, u, cw, cw, cb, cb, da)
    return dug, duv, jnp.concatenate([dwg, dwv], axis=1), jnp.concatenate([dbg, dbv], axis=1)


def _ffn_fwd(h, w, tag):
    hf, = _rows(_rms, [h], [w["norm_ffn"]], [(h.shape[1], bf16)], name=f"ffn_norm_{tag}")
    u = _mm(hf, w["ffn_w_up"], out_dtype=bf16, name=f"ffn_up_{tag}")
    a = _conv_gate_fwd(u, w["ffn_conv_w"], w["ffn_conv_b"], name=f"ffn_conv_{tag}")
    out = _mm(a, w["ffn_w_down"], add=h, name=f"ffn_down_{tag}")
    return out, (h, hf, u, a)


def _ffn_bwd(dout, w, saved, tag):
    h, hf, u, a = saved
    ff = a.shape[1]
    da = _mm(dout, w["ffn_w_down"], tb=True, out_dtype=bf16, name=f"ffn_da_{tag}")
    g = {"ffn_w_down": _mm(a, dout, ta=True, name=f"ffn_dwdown_{tag}")}
    dug, duv, g["ffn_conv_w"], g["ffn_conv_b"] = _conv_gate_bwd(u, w["ffn_conv_w"], w["ffn_conv_b"], da,
                                                                name=f"ffn_dconv_{tag}")
    dhf = _mm(dug, w["ffn_w_up"], tb=True, kslab=(0, 2), name=f"ffn_dhf_g_{tag}")
    dhf = _mm(duv, w["ffn_w_up"], tb=True, kslab=(1, 2), add=dhf, out_dtype=bf16, name=f"ffn_dhf_v_{tag}")
    g["ffn_w_up"] = _mm(hf, dug, ta=True, b2=duv, name=f"ffn_dwup_{tag}")
    (dh,), (g["norm_ffn"],) = _rows_bwd(_rms, [h], [w["norm_ffn"]], [dhf], rgrad=[f32], pgrad=[True],
                                        addends={0: dout}, name=f"ffn_dnorm_{tag}")
    return dh, g


def _xattn_fn(qx, kx, vx, qg, kg):
    outs = []
    for hh in range(XA_HEADS):
        sl = slice(hh * XA_DIM, (hh + 1) * XA_DIM)
        q = _rms(qx[:, sl], qg).astype(bf16)
        k = _rms(kx[:, sl], kg).astype(bf16)
        s = lax.dot_general(q, k, _NT, preferred_element_type=f32) * (XA_DIM ** -0.5)
        s = s - jnp.max(s, axis=-1, keepdims=True)
        p = jnp.exp(s)
        p = p / jnp.sum(p, axis=-1, keepdims=True)
        outs.append(jnp.dot(p.astype(bf16), vx[:, sl].astype(bf16), preferred_element_type=f32))
    return jnp.concatenate(outs, axis=-1)


def _xattn_fwd(h, mem, w, tag):
    d = h.shape[1]
    hx, = _rows(_rms, [h], [w["norm_xa"]], [(d, bf16)], name=f"xa_norm_{tag}")
    qx = _mm(hx, w["xa_wq"], name=f"xa_q_{tag}")
    m, = _rows(_rms, [mem], [w["norm_mem"]], [(d, bf16)], name=f"xa_mnorm_{tag}")
    kx = _mm(m, w["xa_wk"], name=f"xa_k_{tag}")
    vx = _mm(m, w["xa_wv"], name=f"xa_v_{tag}")
    o, = _rows(_xattn_fn, [qx], [kx, vx, w["xa_q_norm"], w["xa_k_norm"]], [(d, bf16)], tile=512,
               name=f"xa_attn_{tag}")
    out = _mm(o, w["xa_wo"], add=h, name=f"xa_o_{tag}")
    return out, (h, hx, qx, m, kx, vx, o)


def _xattn_bwd(dout, mem, w, saved, tag):
    h, hx, qx, m, kx, vx, o = saved
    g = {}
    do = _mm(dout, w["xa_wo"], tb=True, out_dtype=bf16, name=f"xa_do_{tag}")
    g["xa_wo"] = _mm(o, dout, ta=True, name=f"xa_dwo_{tag}")
    (dqx,), (dkx, dvx, g["xa_q_norm"], g["xa_k_norm"]) = _rows_bwd(
        _xattn_fn, [qx], [kx, vx, w["xa_q_norm"], w["xa_k_norm"]], [do], rgrad=[bf16], pgrad=[True] * 4,
        tile=512, name=f"xa_dattn_{tag}")
    dhx = _mm(dqx, w["xa_wq"], tb=True, out_dtype=bf16, name=f"xa_dhx_{tag}")
    g["xa_wq"] = _mm(hx, dqx, ta=True, name=f"xa_dwq_{tag}")
    (dh,), (g["norm_xa"],) = _rows_bwd(_rms, [h], [w["norm_xa"]], [dhx], rgrad=[f32], pgrad=[True],
                                       addends={0: dout}, name=f"xa_dnorm_{tag}")
    dm = _mm(dkx, w["xa_wk"], tb=True, name=f"xa_dm_k_{tag}")
    dm = _mm(dvx, w["xa_wv"], tb=True, add=dm, name=f"xa_dm_v_{tag}")
    g["xa_wk"] = _mm(m, dkx, ta=True, name=f"xa_dwk_{tag}")
    g["xa_wv"] = _mm(m, dvx, ta=True, name=f"xa_dwv_{tag}")
    _, (g["norm_mem"],) = _rows_bwd(_rms, [mem], [w["norm_mem"]], [dm], rgrad=[None], pgrad=[True],
                                    name=f"xa_dmnorm_{tag}")
    return dh, g


_HG_GROUP = 4


def _hg_chunk(q, k, v, g, *sts):
    c = q.shape[0]
    heads = [slice(h * HG_DIM, (h + 1) * HG_DIM) for h in range(len(sts))]
    tri = (lax.broadcasted_iota(jnp.int32, (c, c), 0) >= lax.broadcasted_iota(jnp.int32, (c, c), 1)).astype(f32)
    b = jnp.dot(tri, g, precision=lax.Precision.HIGHEST, preferred_element_type=f32)
    bend = jnp.sum(g, axis=0, keepdims=True)
    qe = (q * jnp.exp(b)).astype(bf16)
    kd = (k * jnp.exp(bend - b)).astype(bf16)
    vb = v.astype(bf16)
    decay = jnp.exp(bend)
    o_inter = [lax.dot_general(qe[:, hs], st.astype(bf16), _NT, preferred_element_type=f32) for hs, st in zip(heads, sts)]
    new = [st * decay[:, hs] + lax.dot_general(vb[:, hs], kd[:, hs], _TN, preferred_element_type=f32)
           for hs, st in zip(heads, sts)]
    outs = []
    for i in range(c // HG_SUB):
        lo, n = HG_SUB * i, HG_SUB * (i + 1)
        ref = jnp.sum(g[:lo], axis=0, keepdims=True) if i else jnp.zeros((1, g.shape[1]), f32)
        qh = (q[lo:n] * jnp.exp(b[lo:n] - ref)).astype(bf16)
        kh = (k[:n] * jnp.exp(ref - b[:n])).astype(bf16)
        keep = (lax.broadcasted_iota(jnp.int32, (HG_SUB, n), 1)
                <= lo + lax.broadcasted_iota(jnp.int32, (HG_SUB, n), 0))
        scores = [lax.dot_general(qh[:, hs], kh[:, hs], _NT, preferred_element_type=f32) for hs in heads]
        scores = [jnp.where(keep, a, 0.0).astype(bf16) for a in scores]
        outs.append(jnp.concatenate([jnp.dot(a, vb[:n, hs], preferred_element_type=f32)
                                     for a, hs in zip(scores, heads)], axis=1))
    return (jnp.concatenate(outs, axis=0) + jnp.concatenate(o_inter, axis=1), *new)


def _hg_fwd(q, k, v, g, *, name):
    length = q.shape[0]
    rows = _HG_GROUP * HG_CHUNK
    ng = length // rows
    nc = length // HG_CHUNK

    def body(q_ref, k_ref, v_ref, g_ref, o_ref, st_ref, state):
        @pl.when(pl.program_id(0) == 0)
        def _():
            state[...] = jnp.zeros_like(state)

        states = [state[h] for h in range(HG_HEADS)]
        for ci in range(_HG_GROUP):
            sl = slice(ci * HG_CHUNK, (ci + 1) * HG_CHUNK)
            for h in range(HG_HEADS):
                st_ref[h, ci] = states[h]
            o, *states = _hg_chunk(q_ref[sl, :], k_ref[sl, :], v_ref[sl, :], g_ref[sl, :], *states)
            o_ref[sl, :] = o
        for h in range(HG_HEADS):
            state[h] = states[h]

    blk = pl.BlockSpec((rows, HG_WIDTH), lambda c: (c, 0))
    return pl.pallas_call(
        body, grid=(ng,), in_specs=[blk] * 4,
        out_specs=[blk, pl.BlockSpec((HG_HEADS, _HG_GROUP, HG_DIM, HG_DIM), lambda c: (0, c, 0, 0))],
        out_shape=[jax.ShapeDtypeStruct((length, HG_WIDTH), f32),
                   jax.ShapeDtypeStruct((HG_HEADS, nc, HG_DIM, HG_DIM), f32)],
        scratch_shapes=[pltpu.VMEM((HG_HEADS, HG_DIM, HG_DIM), f32)],
        compiler_params=_cparams(("arbitrary",)), name=name)(q, k, v, g)


def _hg_bwd(q, k, v, g, states, do, *, name):
    length = q.shape[0]
    rows = _HG_GROUP * HG_CHUNK
    ng = length // rows

    def body(q_ref, k_ref, v_ref, g_ref, st_ref, do_ref, dq_ref, dk_ref, dv_ref, dg_ref, dstate):
        @pl.when(pl.program_id(0) == 0)
        def _():
            dstate[...] = jnp.zeros_like(dstate)

        dstates = [dstate[h] for h in range(HG_HEADS)]
        for ci in reversed(range(_HG_GROUP)):
            sl = slice(ci * HG_CHUNK, (ci + 1) * HG_CHUNK)
            _, vjp = jax.vjp(_hg_chunk, q_ref[sl, :], k_ref[sl, :], v_ref[sl, :], g_ref[sl, :],
                             *[st_ref[h, ci] for h in range(HG_HEADS)])
            dq, dk, dv, dg, *dstates = vjp((do_ref[sl, :], *dstates))
            dq_ref[sl, :] = dq
            dk_ref[sl, :] = dk
            dv_ref[sl, :] = dv
            dg_ref[sl, :] = dg
        for h in range(HG_HEADS):
            dstate[h] = dstates[h]

    blk = pl.BlockSpec((rows, HG_WIDTH), lambda c: (ng - 1 - c, 0))
    sds = jax.ShapeDtypeStruct((length, HG_WIDTH), f32)
    return pl.pallas_call(
        body, grid=(ng,),
        in_specs=[blk] * 4 + [pl.BlockSpec((HG_HEADS, _HG_GROUP, HG_DIM, HG_DIM), lambda c: (0, ng - 1 - c, 0, 0)), blk],
        out_specs=[blk] * 4, out_shape=[sds] * 4,
        scratch_shapes=[pltpu.VMEM((HG_HEADS, HG_DIM, HG_DIM), f32)],
        compiler_params=_cparams(("arbitrary",)), name=name)(q, k, v, g, states, do)


_ATT_BLK = 256
_ATT_SCALE = MLA_QK ** -0.5
_NEG = -1e30


def _att_mask(i, j, t):
    rows = i * t + lax.broadcasted_iota(jnp.int32, (t, t), 0)
    cols = j * t + lax.broadcasted_iota(jnp.int32, (t, t), 1)
    return cols <= rows


def _att_fwd(q, k, v, *, name):
    length = q.shape[0]
    t = min(_ATT_BLK, length)
    nq = length // t
    qw, vw = MLA_QK_PAD, MLA_V
    heads = range(MLA_HEADS)

    def body(q_ref, k_ref, v_ref, o_ref, lse_ref):
        i = pl.program_id(0)
        qbs = [q_ref[:, h * qw:(h + 1) * qw] for h in heads]

        def step(j, carry, diagonal=False):
            off = pl.multiple_of(j * t, t)
            out = []
            for h in heads:
                m, l, acc = carry[h]
                ks = k_ref[pl.ds(off, t), h * qw:(h + 1) * qw]
                vs = v_ref[pl.ds(off, t), h * vw:(h + 1) * vw]
                s = lax.dot_general(qbs[h], ks, _NT, preferred_element_type=f32) * _ATT_SCALE
                if diagonal:
                    s = jnp.where(_att_mask(i, j, t), s, _NEG)
                m_new = jnp.maximum(m, jnp.max(s, axis=-1, keepdims=True))
                alpha = jnp.exp(m - m_new)
                p = jnp.exp(s - m_new)
                l = alpha * l + jnp.sum(p, axis=-1, keepdims=True)
                acc = alpha * acc + jnp.dot(p.astype(bf16), vs, preferred_element_type=f32)
                out.append((m_new, l, acc))
            return tuple(out)

        init = tuple((jnp.full((t, 1), _NEG, f32), jnp.zeros((t, 1), f32), jnp.zeros((t, vw), f32)) for _ in heads)
        res = step(i, lax.fori_loop(0, i, step, init), diagonal=True)
        for h in heads:
            m, l, acc = res[h]
            o_ref[:, h * vw:(h + 1) * vw] = (acc / l).astype(o_ref.dtype)
            lse_ref[:, h * vw:(h + 1) * vw] = jnp.broadcast_to(m + jnp.log(l), (t, vw))

    return pl.pallas_call(
        body, grid=(nq,),
        in_specs=[pl.BlockSpec((t, q.shape[1]), lambda i: (i, 0)), pl.BlockSpec(k.shape, lambda i: (0, 0)),
                  pl.BlockSpec(v.shape, lambda i: (0, 0))],
        out_specs=[pl.BlockSpec((t, v.shape[1]), lambda i: (i, 0))] * 2,
        out_shape=[jax.ShapeDtypeStruct(v.shape, bf16), jax.ShapeDtypeStruct(v.shape, f32)],
        compiler_params=_cparams(("arbitrary",)), name=name)(q, k, v)


def _att_bwd(q, k, v, o, lse, do, *, name):
    length = q.shape[0]
    t = min(_ATT_BLK, length)
    nq = length // t
    qw, vw = MLA_QK_PAD, MLA_V
    heads = range(MLA_HEADS)

    def dq_body(q_ref, k_ref, v_ref, o_ref, lse_ref, do_ref, dq_ref, delta_ref):
        i = pl.program_id(0)
        qbs = [q_ref[:, h * qw:(h + 1) * qw] for h in heads]
        dobs = [do_ref[:, h * vw:(h + 1) * vw] for h in heads]
        lses = [lse_ref[:, h * vw:h * vw + 1] for h in heads]
        deltas = [jnp.sum(dobs[h].astype(f32) * o_ref[:, h * vw:(h + 1) * vw].astype(f32), axis=-1, keepdims=True)
                  for h in heads]

        def step(j, dqs, diagonal=False):
            off = pl.multiple_of(j * t, t)
            out = []
            for h in heads:
                ks = k_ref[pl.ds(off, t), h * qw:(h + 1) * qw]
                vs = v_ref[pl.ds(off, t), h * vw:(h + 1) * vw]
                s = lax.dot_general(qbs[h], ks, _NT, preferred_element_type=f32) * _ATT_SCALE
                p = jnp.exp(s - lses[h])
                if diagonal:
                    p = jnp.where(_att_mask(i, j, t), p, 0.0)
                dp = lax.dot_general(dobs[h], vs, _NT, preferred_element_type=f32)
                ds = p * (dp - deltas[h]) * _ATT_SCALE
                out.append(dqs[h] + jnp.dot(ds.astype(bf16), ks, preferred_element_type=f32))
            return tuple(out)

        dqs = step(i, lax.fori_loop(0, i, step, tuple(jnp.zeros((t, qw), f32) for _ in heads)), diagonal=True)
        for h in heads:
            dq_ref[:, h * qw:(h + 1) * qw] = dqs[h].astype(dq_ref.dtype)
            delta_ref[:, h * vw:(h + 1) * vw] = jnp.broadcast_to(deltas[h], (t, vw))

    qblk = pl.BlockSpec((t, q.shape[1]), lambda i: (i, 0))
    vblk = pl.BlockSpec((t, v.shape[1]), lambda i: (i, 0))
    qfull = pl.BlockSpec(q.shape, lambda i: (0, 0))
    vfull = pl.BlockSpec(v.shape, lambda i: (0, 0))
    dq, delta = pl.pallas_call(
        dq_body, grid=(nq,), in_specs=[qblk, qfull, vfull, vblk, vblk, vblk], out_specs=[qblk, vblk],
        out_shape=[jax.ShapeDtypeStruct(q.shape, bf16), jax.ShapeDtypeStruct(lse.shape, f32)],
        compiler_params=_cparams(("arbitrary",)), name=name + "_dq")(q, k, v, o, lse, do)

    def dkv_body(k_ref, v_ref, q_ref, do_ref, lse_ref, delta_ref, dk_ref, dv_ref):
        j = pl.program_id(0)
        kbs = [k_ref[:, h * qw:(h + 1) * qw] for h in heads]
        vbs = [v_ref[:, h * vw:(h + 1) * vw] for h in heads]

        def step(i, carry, diagonal=False):
            off = pl.multiple_of(i * t, t)
            out = []
            for h in heads:
                dk, dv = carry[h]
                qs = q_ref[pl.ds(off, t), h * qw:(h + 1) * qw]
                dos = do_ref[pl.ds(off, t), h * vw:(h + 1) * vw]
                lse_i = lse_ref[pl.ds(off, t), h * vw:h * vw + 1]
                delta_i = delta_ref[pl.ds(off, t), h * vw:h * vw + 1]
                s = lax.dot_general(qs, kbs[h], _NT, preferred_element_type=f32) * _ATT_SCALE
                p = jnp.exp(s - lse_i)
                if diagonal:
                    p = jnp.where(_att_mask(i, j, t), p, 0.0)
                dv = dv + lax.dot_general(p.astype(bf16), dos, _TN, preferred_element_type=f32)
                dp = lax.dot_general(dos, vbs[h], _NT, preferred_element_type=f32)
                ds = p * (dp - delta_i) * _ATT_SCALE
                dk = dk + lax.dot_general(ds.astype(bf16), qs, _TN, preferred_element_type=f32)
                out.append((dk, dv))
            return tuple(out)

        first = step(j, tuple((jnp.zeros((t, qw), f32), jnp.zeros((t, vw), f32)) for _ in heads), diagonal=True)
        res = lax.fori_loop(j + 1, nq, step, first)
        for h in heads:
            dk_ref[:, h * qw:(h + 1) * qw] = res[h][0].astype(dk_ref.dtype)
            dv_ref[:, h * vw:(h + 1) * vw] = res[h][1].astype(dv_ref.dtype)

    dk, dv = pl.pallas_call(
        dkv_body, grid=(nq,), in_specs=[qblk, vblk, qfull, vfull, vfull, vfull], out_specs=[qblk, vblk],
        out_shape=[jax.ShapeDtypeStruct(k.shape, bf16), jax.ShapeDtypeStruct(v.shape, bf16)],
        compiler_params=_cparams(("arbitrary",)), name=name + "_dkv")(k, v, q, do, lse, delta)
    return dq, dk, dv


_C_Q = 4 * HG_WIDTH
_C_KV = _C_Q + MLA_Q_RANK
_C_KPE = _C_KV + MLA_KV_RANK


def _rms_n(x, g, n):
    return x * lax.rsqrt(jnp.sum(x * x, axis=-1, keepdims=True) * (1.0 / n) + EPS) * g


def _mix_a(proj, l0, l1, q_a_norm, kv_a_norm):
    lb = jax.nn.sigmoid(l0 - l1)
    f = lb + (1.0 - lb) * jax.nn.sigmoid(proj[:, HG_WIDTH:2 * HG_WIDTH])
    qf = _silu(proj[:, :HG_WIDTH])
    v = proj[:, 2 * HG_WIDTH:3 * HG_WIDTH]
    cqn = _rms(proj[:, _C_Q:_C_KV], q_a_norm)
    ckvn = _rms(proj[:, _C_KV:_C_KPE], kv_a_norm)
    return qf, 1.0 - f, v, jnp.log(f), cqn, ckvn


def _mix_b(qraw, kvraw, kpe_raw, cos, sin, qn_nope, qn_rope, kn_nope, kn_rope, perm):
    def rope(x):
        return x * cos + jnp.dot(x, perm, precision=lax.Precision.HIGHEST, preferred_element_type=f32) * sin

    kpe = rope(_rms_n(kpe_raw, kn_rope, MLA_ROPE))
    qs, ks, vs = [], [], []
    for hh in range(MLA_HEADS):
        base = hh * MLA_QK_PAD
        qs.append(_rms(qraw[:, base:base + MLA_NOPE], qn_nope))
        qs.append(rope(_rms_n(qraw[:, base + MLA_NOPE:base + MLA_QK_PAD], qn_rope, MLA_ROPE)))
        ks.append(_rms(kvraw[:, base:base + MLA_NOPE], kn_nope))
        ks.append(kpe)
        vs.append(kvraw[:, base + MLA_NOPE:base + MLA_QK_PAD])
    return jnp.concatenate(qs, axis=-1), jnp.concatenate(ks, axis=-1), jnp.concatenate(vs, axis=-1)


def _mix_c(o_hg, gate, o_mla, hg_out_norm):
    parts = []
    for hh in range(HG_HEADS):
        sl = slice(hh * HG_DIM, (hh + 1) * HG_DIM)
        parts.append(_rms(o_hg[:, sl], hg_out_norm[:, sl]))
    o = jnp.concatenate(parts, axis=-1) * _silu(gate)
    return jnp.concatenate([o, o_mla], axis=-1)


def _rope_perm():
    p = np.zeros((128, 128), np.float32)
    half = MLA_ROPE // 2
    for i in range(half):
        p[i + half, i] = -1.0
        p[i, i + half] = 1.0
    return jnp.asarray(p)


def _mixer_fwd(h, cos, sin, w, tag):
    d = h.shape[1]
    hn, = _rows(_rms, [h], [w["norm_mix"]], [(d, bf16)], name=f"mix_norm_{tag}")
    proj = _mm(hn, w["mix_w_in"], name=f"mix_in_{tag}")
    pa = [w["lb0"], w["lb1"], w["mla_q_a_norm"], w["mla_kv_a_norm"]]
    qf, kk, vv, logf, cqn, ckvn = _rows(
        _mix_a, [proj], pa, [(HG_WIDTH, f32)] * 4 + [(MLA_Q_RANK, bf16), (MLA_KV_RANK, bf16)], name=f"mix_a_{tag}")
    o_hg, states = _hg_fwd(qf, kk, vv, logf, name=f"hg_fwd_{tag}")
    qraw = _mm(cqn, w["mla_w_uq"], name=f"mla_uq_{tag}")
    kvraw = _mm(ckvn, w["mla_w_ukv"], name=f"mla_ukv_{tag}")
    pb = [w["mla_qn_nope"], w["mla_qn_rope"], w["mla_kn_nope"], w["mla_kn_rope"], w["rope_perm"]]
    kpe_raw, gate = _window(proj, _C_KPE, IN_PAD - _C_KPE), _window(proj, 3 * HG_WIDTH, HG_WIDTH)
    qfull, kfull, vfull = _rows(_mix_b, [qraw, kvraw, kpe_raw, cos, sin], pb,
                                [(MLA_HEADS * MLA_QK_PAD, bf16)] * 2 + [(MLA_HEADS * MLA_V, bf16)],
                                name=f"mix_b_{tag}")
    o_mla, lse = _att_fwd(qfull, kfull, vfull, name=f"att_fwd_{tag}")
    mixin, = _rows(_mix_c, [o_hg, gate, o_mla], [w["hg_out_norm"]], [(d, bf16)], name=f"mix_c_{tag}")
    if callable(w["mix_w_out"]):
        w["mix_w_out"] = w["mix_w_out"](mixin)
    out = _mm(mixin, w["mix_w_out"], add=h, name=f"mix_out_{tag}")
    return out, (h, hn, proj, qf, kk, vv, logf, cqn, ckvn, o_hg, states, qraw, kvraw, qfull, kfull, vfull, o_mla,
                 lse, mixin)


def _mixer_bwd(dout, cos, sin, w, saved, tag, on_w_out=None):
    (h, hn, proj, qf, kk, vv, logf, cqn, ckvn, o_hg, states, qraw, kvraw, qfull, kfull, vfull, o_mla, lse,
     mixin) = saved
    g = {}
    dmixin = _mm(dout, w["mix_w_out"], tb=True, name=f"mix_dmixin_{tag}")
    g["mix_w_out"] = _mm(mixin, dout, ta=True, name=f"mix_dwout_{tag}")
    if on_w_out is not None:
        dmixin = on_w_out(g["mix_w_out"], dmixin)
    kpe_raw, gate = _window(proj, _C_KPE, IN_PAD - _C_KPE), _window(proj, 3 * HG_WIDTH, HG_WIDTH)
    (do_hg, dgate, do_mla), (g["hg_out_norm"],) = _rows_bwd(
        _mix_c, [o_hg, gate, o_mla], [w["hg_out_norm"]], [dmixin], rgrad=[f32, f32, bf16], pgrad=[True],
        name=f"mix_dc_{tag}")
    dqfull, dkfull, dvfull = _att_bwd(qfull, kfull, vfull, o_mla, lse, do_mla, name=f"att_bwd_{tag}")
    pb = [w["mla_qn_nope"], w["mla_qn_rope"], w["mla_kn_nope"], w["mla_kn_rope"], w["rope_perm"]]
    (dqraw, dkvraw, dkpe_raw), pg = _rows_bwd(
        _mix_b, [qraw, kvraw, kpe_raw, cos, sin], pb, [dqfull, dkfull, dvfull],
        rgrad=[bf16, bf16, f32, None, None], pgrad=[True, True, True, True, False],
        name=f"mix_db_{tag}")
    g["mla_qn_nope"], g["mla_qn_rope"], g["mla_kn_nope"], g["mla_kn_rope"] = pg
    dcqn = _mm(dqraw, w["mla_w_uq"], tb=True, name=f"mla_dcq_{tag}")
    g["mla_w_uq"] = _mm(cqn, dqraw, ta=True, name=f"mla_dwuq_{tag}")
    dckvn = _mm(dkvraw, w["mla_w_ukv"], tb=True, name=f"mla_dckv_{tag}")
    g["mla_w_ukv"] = _mm(ckvn, dkvraw, ta=True, name=f"mla_dwukv_{tag}")
    dqf, dkk, dvv, dlogf = _hg_bwd(qf, kk, vv, logf, states, do_hg, name=f"hg_bwd_{tag}")
    pa = [w["lb0"], w["lb1"], w["mla_q_a_norm"], w["mla_kv_a_norm"]]
    (dproj,), (g["lb0"], g["lb1"], g["mla_q_a_norm"], g["mla_kv_a_norm"]) = _rows_bwd(
        _mix_a, [proj], pa, [dqf, dkk, dvv, dlogf, dcqn, dckvn], rgrad=[bf16], pgrad=[True] * 4,
        addends={0: [(dgate, 3 * HG_WIDTH), (dkpe_raw, _C_KPE)]}, name=f"mix_da_{tag}")
    dhn = _mm(dproj, w["mix_w_in"], tb=True, out_dtype=bf16, name=f"mix_dhn_{tag}")
    g["mix_w_in_t"] = _mm(dproj, hn, ta=True, name=f"mix_dwin_{tag}")
    (dh,), (g["norm_mix"],) = _rows_bwd(_rms, [h], [w["norm_mix"]], [dhn], rgrad=[f32], pgrad=[True],
                                        addends={0: dout}, name=f"mix_dnorm_{tag}")
    return dh, g


def _rope_tables(positions):
    inv_freq = 1.0 / (ROPE_BASE ** (jnp.arange(0, MLA_ROPE, 2, dtype=f32) / MLA_ROPE))
    ang = positions.astype(f32)[:, None] * inv_freq
    z = jnp.zeros((positions.shape[0], 128 - MLA_ROPE), f32)
    return (jnp.concatenate([jnp.cos(ang), jnp.cos(ang), z], axis=1),
            jnp.concatenate([jnp.sin(ang), jnp.sin(ang), z], axis=1))


def _pad_cols(a, n):
    return jnp.pad(a, ((0, 0), (0, n - a.shape[1])))


def _even_weights(p, j, layer, dt):
    w_uq = p["mla_w_uq"][j].reshape(MLA_Q_RANK, MLA_HEADS, MLA_QK)
    w_uq = jnp.pad(w_uq, ((0, 0), (0, 0), (0, MLA_QK_PAD - MLA_QK))).reshape(MLA_Q_RANK, MLA_HEADS * MLA_QK_PAD)
    return dict(
        norm_mix=p["norm_mix"][layer][None], mix_w_in=_pad_cols(p["mix_w_in"][j], IN_PAD).astype(dt),
        lb0=p["hg_lb_logits"][0][None], lb1=p["hg_lb_logits"][1][None],
        mla_q_a_norm=p["mla_q_a_norm"][j][None], mla_kv_a_norm=p["mla_kv_a_norm"][j][None],
        mla_w_uq=w_uq.astype(dt), mla_w_ukv=p["mla_w_ukv"][j].astype(dt),
        mla_qn_nope=p["mla_qn_nope"][j][None], mla_qn_rope=_pad_cols(p["mla_qn_rope"][j][None], 128),
        mla_kn_nope=p["mla_kn_nope"][j][None], mla_kn_rope=_pad_cols(p["mla_kn_rope"][j][None], 128),
        rope_perm=_rope_perm(), hg_out_norm=p["hg_out_norm"][j][None],
        mix_w_out=p["mix_w_out"][j].astype(dt) if "mix_w_out" in p else None)


def _even_grads(g):
    w_uq = g["mla_w_uq"].reshape(MLA_Q_RANK, MLA_HEADS, MLA_QK_PAD)[:, :, :MLA_QK].reshape(MLA_Q_RANK, -1)
    return dict(
        norm_mix=g["norm_mix"], mix_w_in=g["mix_w_in_t"][:IN_WIDTH].T[None],
        hg_lb_logits=jnp.concatenate([g["lb0"], g["lb1"]], axis=0),
        mla_q_a_norm=g["mla_q_a_norm"], mla_kv_a_norm=g["mla_kv_a_norm"], mla_w_uq=w_uq[None],
        mla_w_ukv=g["mla_w_ukv"][None], mla_qn_nope=g["mla_qn_nope"], mla_qn_rope=g["mla_qn_rope"][:, :MLA_ROPE],
        mla_kn_nope=g["mla_kn_nope"], mla_kn_rope=g["mla_kn_rope"][:, :MLA_ROPE],
        hg_out_norm=g["hg_out_norm"], mix_w_out=g["mix_w_out"][None])


_S5_NB = 8
_S5_BW = 1024
_S5_HALF = 512
_S5_UC = 128
_S5_TIME = 512


def _bd_mm(a, b3, *, name, tb=False, out_dtype=f32):
    length = a.shape[0]
    rows_b, cols_b = b3.shape[0] // _S5_NB, b3.shape[1]
    ka, n = (cols_b, rows_b) if tb else (rows_b, cols_b)
    bm = _pick(length, (512, 256, 128))
    dims = _NT if tb else _NN

    def body(a_ref, b_ref, o_ref):
        o_ref[...] = lax.dot_general(a_ref[...].astype(bf16), b_ref[...].astype(bf16), dims,
                                     preferred_element_type=f32).astype(o_ref.dtype)

    return pl.pallas_call(
        body, grid=(length // bm, _S5_NB),
        in_specs=[pl.BlockSpec((bm, ka), lambda i, j: (i, j)), pl.BlockSpec((rows_b, cols_b), lambda i, j: (j, 0))],
        out_specs=pl.BlockSpec((bm, n), lambda i, j: (i, j)),
        out_shape=jax.ShapeDtypeStruct((length, _S5_NB * n), out_dtype),
        compiler_params=_cparams(("parallel", "parallel")), name=name)(a, b3)


def _bd_mm_tn(a, c, *, name):
    length = a.shape[0]
    ka, n = a.shape[1] // _S5_NB, c.shape[1] // _S5_NB
    bk = _pick(length, (512, 256, 128))
    nk = length // bk

    def body(a_ref, c_ref, o_ref):
        @pl.when(pl.program_id(1) == 0)
        def _():
            o_ref[...] = jnp.zeros_like(o_ref)
        o_ref[...] += lax.dot_general(a_ref[...].astype(bf16), c_ref[...].astype(bf16), _TN,
                                      preferred_element_type=f32)

    return pl.pallas_call(
        body, grid=(_S5_NB, nk),
        in_specs=[pl.BlockSpec((bk, ka), lambda j, q: (q, j)), pl.BlockSpec((bk, n), lambda j, q: (q, j))],
        out_specs=pl.BlockSpec((ka, n), lambda j, q: (j, 0)),
        out_shape=jax.ShapeDtypeStruct((_S5_NB * ka, n), f32),
        compiler_params=_cparams(("parallel", "arbitrary")), name=name)(a, c)


def _cmul(ar, ai, br, bi):
    return ar * br - ai * bi, ar * bi + ai * br


def _pow_table(ar, ai, descending):
    rows = lax.broadcasted_iota(jnp.int32, (8, ar.shape[1]), 0)
    tr = jnp.zeros((8, ar.shape[1]), f32)
    ti = jnp.zeros((8, ar.shape[1]), f32)
    pr, pi_ = ar, ai
    for r in range(8):
        sel = rows == ((7 - r) if descending else r)
        tr = jnp.where(sel, pr, tr)
        ti = jnp.where(sel, pi_, ti)
        pr, pi_ = _cmul(pr, pi_, ar, ai)
    return tr, ti


def _s5_scan_fwd(a, x, *, name):
    length = x.shape[0]
    tc = min(_S5_TIME, length)
    hw = _S5_HALF

    def body(a_ref, x_ref, o_ref, carry):
        @pl.when(pl.program_id(1) == 0)
        def _():
            carry[...] = jnp.zeros_like(carry)

        ar, ai = a_ref[:, :hw], a_ref[:, hw:]
        xr, xi = x_ref[:, :hw], x_ref[:, hw:]
        row8 = lax.broadcasted_iota(jnp.int32, (tc, hw), 0) & 7
        mr, mi = ar, ai
        for s in (1, 2, 4):
            sr, si = pltpu.roll(xr, s, axis=0), pltpu.roll(xi, s, axis=0)
            pr, pi_ = _cmul(mr, mi, sr, si)
            ok = row8 >= s
            xr = xr + jnp.where(ok, pr, 0.0)
            xi = xi + jnp.where(ok, pi_, 0.0)
            mr, mi = _cmul(mr, mi, mr, mi)
        o_ref[:, :hw] = xr
        o_ref[:, hw:] = xi
        tr, ti = _pow_table(ar, ai, False)
        cr, ci = carry[:, :hw], carry[:, hw:]
        for i in range(tc // 8):
            sl = slice(8 * i, 8 * i + 8)
            pr, pi_ = _cmul(tr, ti, cr, ci)
            o_ref[sl, :hw] = o_ref[sl, :hw] + pr
            o_ref[sl, hw:] = o_ref[sl, hw:] + pi_
            cr, ci = o_ref[8 * i + 7:8 * i + 8, :hw], o_ref[8 * i + 7:8 * i + 8, hw:]
        carry[:, :hw] = cr
        carry[:, hw:] = ci

    return pl.pallas_call(
        body, grid=(_S5_NB, length // tc),
        in_specs=[pl.BlockSpec((1, _S5_BW), lambda j, t: (0, j)), pl.BlockSpec((tc, _S5_BW), lambda j, t: (t, j))],
        out_specs=pl.BlockSpec((tc, _S5_BW), lambda j, t: (t, j)),
        out_shape=jax.ShapeDtypeStruct(x.shape, f32),
        scratch_shapes=[pltpu.VMEM((1, _S5_BW), f32)],
        compiler_params=_cparams(("parallel", "arbitrary")), name=name)(a, x)


def _s5_scan_bwd(a, hs, d, *, name):
    length = d.shape[0]
    tc = min(_S5_TIME, length)
    nt = length // tc
    hw = _S5_HALF

    def body(a_ref, h_ref, d_ref, g_ref, da_ref, carry):
        @pl.when(pl.program_id(1) == 0)
        def _():
            carry[...] = jnp.zeros_like(carry)
            da_ref[...] = jnp.zeros_like(da_ref)

        ar, ai = a_ref[:, :hw], -a_ref[:, hw:]
        xr, xi = d_ref[:, :hw], d_ref[:, hw:]
        rows = lax.broadcasted_iota(jnp.int32, (tc, hw), 0)
        row8 = rows & 7
        mr, mi = ar, ai
        for s in (1, 2, 4):
            sr, si = pltpu.roll(xr, tc - s, axis=0), pltpu.roll(xi, tc - s, axis=0)
            pr, pi_ = _cmul(mr, mi, sr, si)
            ok = row8 < 8 - s
            xr = xr + jnp.where(ok, pr, 0.0)
            xi = xi + jnp.where(ok, pi_, 0.0)
            mr, mi = _cmul(mr, mi, mr, mi)
        g_ref[:, :hw] = xr
        g_ref[:, hw:] = xi
        tr, ti = _pow_table(ar, ai, True)
        cr0, ci0 = carry[:, :hw], carry[:, hw:]
        cr, ci = cr0, ci0
        for i in reversed(range(tc // 8)):
            sl = slice(8 * i, 8 * i + 8)
            pr, pi_ = _cmul(tr, ti, cr, ci)
            g_ref[sl, :hw] = g_ref[sl, :hw] + pr
            g_ref[sl, hw:] = g_ref[sl, hw:] + pi_
            cr, ci = g_ref[8 * i:8 * i + 1, :hw], g_ref[8 * i:8 * i + 1, hw:]
        carry[:, :hw] = cr
        carry[:, hw:] = ci
        last = rows == tc - 1
        gnr = jnp.where(last, cr0, pltpu.roll(g_ref[:, :hw], tc - 1, axis=0))
        gni = jnp.where(last, ci0, pltpu.roll(g_ref[:, hw:], tc - 1, axis=0))
        hr, hi = h_ref[:, :hw], h_ref[:, hw:]
        da_ref[:, :hw] += jnp.sum(hr * gnr + hi * gni, axis=0, keepdims=True)
        da_ref[:, hw:] += jnp.sum(hr * gni - hi * gnr, axis=0, keepdims=True)

    blk = pl.BlockSpec((tc, _S5_BW), lambda j, t: (nt - 1 - t, j))
    row = pl.BlockSpec((1, _S5_BW), lambda j, t: (0, j))
    return pl.pallas_call(
        body, grid=(_S5_NB, nt), in_specs=[row, blk, blk], out_specs=[blk, row],
        out_shape=[jax.ShapeDtypeStruct(d.shape, f32), jax.ShapeDtypeStruct((1, _S5_NB * _S5_BW), f32)],
        scratch_shapes=[pltpu.VMEM((1, _S5_BW), f32)],
        compiler_params=_cparams(("parallel", "arbitrary")), name=name)(a, hs, d)


def _s5_tile_scan(work, carry, ar, ai, tc, reverse, per_tile=None):
    hw = _S5_HALF
    row8 = lax.broadcasted_iota(jnp.int32, (8, hw), 0)
    powers = [(ar, ai)]
    for _ in range(2):
        powers.append(_cmul(*powers[-1], *powers[-1]))
    steps = []
    for (mr, mi), s in zip(powers, (1, 2, 4)):
        ok = (row8 < 8 - s) if reverse else (row8 >= s)
        steps.append((jnp.where(ok, mr, 0.0), jnp.where(ok, mi, 0.0), 8 - s if reverse else s))
    tr, ti = _pow_table(ar, ai, reverse)
    cr, ci = carry[:, :hw], carry[:, hw:]
    tiles = range(tc // 8)
    for i in (reversed(tiles) if reverse else tiles):
        sl = slice(8 * i, 8 * i + 8)
        xr, xi = work[sl, :hw], work[sl, hw:]
        for mr, mi, shift in steps:
            pr, pi_ = _cmul(mr, mi, pltpu.roll(xr, shift, axis=0), pltpu.roll(xi, shift, axis=0))
            xr, xi = xr + pr, xi + pi_
        pr, pi_ = _cmul(tr, ti, cr, ci)
        xr, xi = xr + pr, xi + pi_
        work[sl, :hw] = xr
        work[sl, hw:] = xi
        if per_tile is not None:
            per_tile(sl, xr, xi, cr, ci)
        edge = 8 * i if reverse else 8 * i + 7
        cr, ci = work[edge:edge + 1, :hw], work[edge:edge + 1, hw:]
    carry[:, :hw] = cr
    carry[:, hw:] = ci


def _s5_core_fwd(a, hn, b3, c3, *, name):
    length = hn.shape[0]
    tc = min(_S5_TIME, length)

    def body(a_ref, hn_ref, b_ref, c_ref, hs_ref, y_ref, work, carry):
        @pl.when(pl.program_id(1) == 0)
        def _():
            carry[...] = jnp.zeros_like(carry)

        work[...] = jnp.dot(hn_ref[...].astype(bf16), b_ref[...], preferred_element_type=f32)
        _s5_tile_scan(work, carry, a_ref[:, :_S5_HALF], a_ref[:, _S5_HALF:], tc, False)
        hs = work[...].astype(bf16)
        hs_ref[...] = hs
        y_ref[...] = jnp.dot(hs, c_ref[...], preferred_element_type=f32)

    return pl.pallas_call(
        body, grid=(_S5_NB, length // tc),
        in_specs=[pl.BlockSpec((1, _S5_BW), lambda j, t: (0, j)), pl.BlockSpec((tc, _S5_UC), lambda j, t: (t, j)),
                  pl.BlockSpec((_S5_UC, _S5_BW), lambda j, t: (j, 0)), pl.BlockSpec((_S5_BW, _S5_UC), lambda j, t: (j, 0))],
        out_specs=[pl.BlockSpec((tc, _S5_BW), lambda j, t: (t, j)), pl.BlockSpec((tc, _S5_UC), lambda j, t: (t, j))],
        out_shape=[jax.ShapeDtypeStruct((length, _S5_NB * _S5_BW), bf16),
                   jax.ShapeDtypeStruct((length, _S5_NB * _S5_UC), f32)],
        scratch_shapes=[pltpu.VMEM((tc, _S5_BW), f32), pltpu.VMEM((1, _S5_BW), f32)],
        compiler_params=_cparams(("parallel", "arbitrary")), name=name)(a, hn, b3, c3)


def _s5_core_bwd(a, dy, c3, hs, hn, b3, *, name):
    length = hn.shape[0]
    tc = min(_S5_TIME, length)
    nt = length // tc
    hw = _S5_HALF

    def body(a_ref, dy_ref, c_ref, hs_ref, hn_ref, b_ref, du_ref, db_ref, dc_ref, da_ref, work, carry, acc):
        @pl.when(pl.program_id(1) == 0)
        def _():
            carry[...] = jnp.zeros_like(carry)
            db_ref[...] = jnp.zeros_like(db_ref)
            dc_ref[...] = jnp.zeros_like(dc_ref)
            da_ref[...] = jnp.zeros_like(da_ref)

        dyb = dy_ref[...].astype(bf16)
        work[...] = lax.dot_general(dyb, c_ref[...], _NT, preferred_element_type=f32)
        acc[...] = jnp.zeros_like(acc)
        row8 = lax.broadcasted_iota(jnp.int32, (8, hw), 0)

        def grad_a(sl, gr, gi, cr, ci):
            gnr = jnp.where(row8 == 7, cr, pltpu.roll(gr, 7, axis=0))
            gni = jnp.where(row8 == 7, ci, pltpu.roll(gi, 7, axis=0))
            hr, hi = hs_ref[sl, :hw].astype(f32), hs_ref[sl, hw:].astype(f32)
            acc[:, :hw] += hr * gnr + hi * gni
            acc[:, hw:] += hr * gni - hi * gnr

        _s5_tile_scan(work, carry, a_ref[:, :hw], -a_ref[:, hw:], tc, True, grad_a)
        da_ref[...] += jnp.sum(acc[...], axis=0, keepdims=True)
        g = work[...].astype(bf16)
        du_ref[...] = lax.dot_general(g, b_ref[...], _NT, preferred_element_type=f32)
        db_ref[...] += lax.dot_general(hn_ref[...].astype(bf16), g, _TN, preferred_element_type=f32)
        dc_ref[...] += lax.dot_general(hs_ref[...], dyb, _TN, preferred_element_type=f32)

    rev = lambda j, t: (nt - 1 - t, j)
    return pl.pallas_call(
        body, grid=(_S5_NB, nt),
        in_specs=[pl.BlockSpec((1, _S5_BW), lambda j, t: (0, j)), pl.BlockSpec((tc, _S5_UC), rev),
                  pl.BlockSpec((_S5_BW, _S5_UC), lambda j, t: (j, 0)), pl.BlockSpec((tc, _S5_BW), rev),
                  pl.BlockSpec((tc, _S5_UC), rev), pl.BlockSpec((_S5_UC, _S5_BW), lambda j, t: (j, 0))],
        out_specs=[pl.BlockSpec((tc, _S5_UC), rev), pl.BlockSpec((_S5_UC, _S5_BW), lambda j, t: (j, 0)),
                   pl.BlockSpec((_S5_BW, _S5_UC), lambda j, t: (j, 0)), pl.BlockSpec((1, _S5_BW), lambda j, t: (0, j))],
        out_shape=[jax.ShapeDtypeStruct((length, _S5_NB * _S5_UC), f32),
                   jax.ShapeDtypeStruct((_S5_NB * _S5_UC, _S5_BW), f32),
                   jax.ShapeDtypeStruct((_S5_NB * _S5_BW, _S5_UC), f32),
                   jax.ShapeDtypeStruct((1, _S5_NB * _S5_BW), f32)],
        scratch_shapes=[pltpu.VMEM((tc, _S5_BW), f32), pltpu.VMEM((1, _S5_BW), f32), pltpu.VMEM((8, _S5_BW), f32)],
        compiler_params=_cparams(("parallel", "arbitrary")), name=name)(a, dy, c3, hs, hn, b3)


def _s5_disc(lr, li, ldt, btr, bti, expand):
    dt = jnp.exp(ldt)
    mag = jnp.exp(lr * dt)
    abr = mag * jnp.cos(li * dt)
    abi = mag * jnp.sin(li * dt)
    den = lr * lr + li * li
    zr = ((abr - 1.0) * lr + abi * li) / den
    zi = (abi * lr - (abr - 1.0) * li) / den
    zr = jnp.dot(zr, expand, precision=lax.Precision.HIGHEST, preferred_element_type=f32)
    zi = jnp.dot(zi, expand, precision=lax.Precision.HIGHEST, preferred_element_type=f32)
    return abr, abi, zr * btr - zi * bti, zr * bti + zi * btr


def _s5_disc_fwd(args, *, name):
    def body(*refs):
        res = _s5_disc(*[r[...] for r in refs[:6]])
        for o, v in zip(refs[6:], res):
            o[...] = v

    sds = jax.ShapeDtypeStruct
    return pl.pallas_call(body, out_shape=[sds(args[0].shape, f32)] * 2 + [sds(args[3].shape, f32)] * 2,
                          name=name)(*args)


def _s5_disc_bwd(args, cts, *, name):
    def body(*refs):
        vals = [r[...] for r in refs[:6]]
        _, vjp = jax.vjp(lambda *d: _s5_disc(*d, vals[5]), *vals[:5])
        grads = vjp(tuple(r[...] for r in refs[6:10]))
        for o, v in zip(refs[10:], grads):
            o[...] = v

    return pl.pallas_call(body, out_shape=[jax.ShapeDtypeStruct(a.shape, f32) for a in args[:5]],
                          name=name)(*args, *cts)


def _gelu_tanh(x):
    return 0.5 * x * (1.0 + jnp.tanh(0.7978845608028654 * (x + 0.044715 * (x * x * x))))


def _s5_post(y, u, d_skip):
    return _gelu_tanh(y + d_skip * u)


def _s5_glu(ga, gb, h):
    return h + ga * jax.nn.sigmoid(gb)


def _s5_expand():
    e = np.zeros((S5_STATE, S5_GROUP * S5_STATE), np.float32)
    for m in range(S5_GROUP):
        e[np.arange(S5_STATE), m * S5_STATE + np.arange(S5_STATE)] = 1.0
    return jnp.asarray(e)


def _s5_pack_b(bbr, bbi):
    eye = jnp.eye(8, dtype=f32)

    def one(bb):
        b5 = bb.reshape(_S5_NB, 8, S5_GROUP, S5_STATE)
        return jnp.einsum("jgmp,gh->jgmhp", b5, eye).reshape(_S5_NB * _S5_UC, _S5_HALF)

    return jnp.concatenate([one(bbr), one(bbi)], axis=1)


def _s5_unpack_b(db3):
    def one(d):
        d5 = d.reshape(_S5_NB, 8, S5_GROUP, 8, S5_STATE)
        return jnp.einsum("jgmgp->jgmp", d5).reshape(S5_GROUPS, S5_GROUP * S5_STATE)

    return one(db3[:, :_S5_HALF]), one(db3[:, _S5_HALF:])


def _s5_pack_c(c_re, c_im):
    eye = jnp.eye(8, dtype=f32)

    def one(c):
        c4 = c.reshape(_S5_NB, 8, S5_GROUP, S5_STATE)
        return jnp.einsum("jgmp,hg->jhpgm", c4, eye).reshape(_S5_NB, _S5_HALF, _S5_UC)

    return jnp.concatenate([one(c_re), -one(c_im)], axis=1).reshape(_S5_NB * _S5_BW, _S5_UC)


def _s5_unpack_c(dc3):
    d = dc3.reshape(_S5_NB, 2, 8, S5_STATE, 8, S5_GROUP)
    dre = jnp.einsum("jgpgm->jgmp", d[:, 0]).reshape(S5_GROUPS, S5_GROUP, S5_STATE)
    dim = -jnp.einsum("jgpgm->jgmp", d[:, 1]).reshape(S5_GROUPS, S5_GROUP, S5_STATE)
    return dre, dim


def _s5_state_row(re, im):
    r = re.reshape(_S5_NB, 1, _S5_HALF)
    i = im.reshape(_S5_NB, 1, _S5_HALF)
    return jnp.concatenate([r, i], axis=2).reshape(1, _S5_NB * _S5_BW)


def _s5_unstate_row(row):
    r = row.reshape(_S5_NB, 2, 8, S5_STATE)
    return r[:, 0].reshape(S5_GROUPS, S5_STATE), r[:, 1].reshape(S5_GROUPS, S5_STATE)


def _s5_fwd(h, w, tag):
    d = h.shape[1]
    hn, = _rows(_rms, [h], [w["norm_mix"]], [(d, f32)], name=f"s5_norm_{tag}")
    disc_in = [w["s5_lam_re"], w["s5_lam_im"], w["s5_log_dt"], w["s5_bt_re"], w["s5_bt_im"], w["s5_expand"]]
    abr, abi, bbr, bbi = _s5_disc_fwd(disc_in, name=f"s5_disc_{tag}")
    a_row = _s5_state_row(abr, abi)
    b3 = _s5_pack_b(bbr, bbi).astype(bf16)
    hs, y = _s5_core_fwd(a_row, hn, b3, w["s5_c3"], name=f"s5_core_{tag}")
    yg, = _rows(_s5_post, [y, hn], [w["s5_d"]], [(d, bf16)], name=f"s5_post_{tag}")
    ga = _mm(yg, w["s5_w_glu_a"], name=f"s5_glu_a_{tag}")
    gb = _mm(yg, w["s5_w_glu_b"], name=f"s5_glu_b_{tag}")
    out, = _rows(_s5_glu, [ga, gb, h], [], [(d, f32)], name=f"s5_glu_{tag}")
    return out, (h, hn, disc_in, a_row, b3, hs, y, yg, ga, gb)


def _s5_bwd(dout, w, saved, tag):
    h, hn, disc_in, a_row, b3, hs, y, yg, ga, gb = saved
    g = {}
    (dga, dgb), _ = _rows_bwd(_s5_glu, [ga, gb, h], [], [dout], rgrad=[bf16, bf16, None], pgrad=[],
                              name=f"s5_dglu_{tag}")
    dyg = _mm(dga, w["s5_w_glu_a"], tb=True, name=f"s5_dyg_a_{tag}")
    dyg = _mm(dgb, w["s5_w_glu_b"], tb=True, add=dyg, name=f"s5_dyg_b_{tag}")
    g["s5_w_glu_a"] = _mm(yg, dga, ta=True, name=f"s5_dwa_{tag}")
    g["s5_w_glu_b"] = _mm(yg, dgb, ta=True, name=f"s5_dwb_{tag}")
    (dy, du_skip), (g["s5_d"],) = _rows_bwd(_s5_post, [y, hn], [w["s5_d"]], [dyg], rgrad=[bf16, f32], pgrad=[True],
                                           name=f"s5_dpost_{tag}")
    du, db3, dc3, da_row = _s5_core_bwd(a_row, dy, w["s5_c3"], hs, hn, b3, name=f"s5_dcore_{tag}")
    dabr, dabi = _s5_unstate_row(da_row)
    dbbr, dbbi = _s5_unpack_b(db3)
    g["s5_lam_re"], g["s5_lam_im"], g["s5_log_dt"], g["s5_bt_re"], g["s5_bt_im"] = _s5_disc_bwd(
        disc_in, [dabr, dabi, dbbr, dbbi], name=f"s5_ddisc_{tag}")
    g["s5_c_re"], g["s5_c_im"] = _s5_unpack_c(dc3)
    (dh,), (g["norm_mix"],) = _rows_bwd(_rms_twice, [h], [w["norm_mix"]], [du, du_skip], rgrad=[f32], pgrad=[True],
                                        addends={0: dout}, name=f"s5_dnorm_{tag}")
    return dh, g


def _odd_weights(p, j, layer, dt):
    tr = lambda b: b.transpose(0, 2, 1).reshape(S5_GROUPS, S5_GROUP * S5_STATE)
    return dict(
        norm_mix=p["norm_mix"][layer][None], s5_lam_re=p["s5_lam_re"][j], s5_lam_im=p["s5_lam_im"][j],
        s5_log_dt=p["s5_log_dt"][j][:, None], s5_bt_re=tr(p["s5_b_re"][j]), s5_bt_im=tr(p["s5_b_im"][j]),
        s5_expand=_s5_expand(), s5_c3=_s5_pack_c(p["s5_c_re"][j], p["s5_c_im"][j]).astype(dt),
        **{n: (p[n][j][None] if n == "s5_d" else p[n][j].astype(dt))
           for n in ("s5_d", "s5_w_glu_a", "s5_w_glu_b") if n in p})


def _odd_grads(g):
    tr = lambda b: b.reshape(S5_GROUPS, S5_GROUP, S5_STATE).transpose(0, 2, 1)[None]
    return dict(
        norm_mix=g["norm_mix"], s5_lam_re=g["s5_lam_re"][None], s5_lam_im=g["s5_lam_im"][None],
        s5_log_dt=g["s5_log_dt"][:, 0][None], s5_b_re=tr(g["s5_bt_re"]), s5_b_im=tr(g["s5_bt_im"]),
        s5_c_re=g["s5_c_re"][None], s5_c_im=g["s5_c_im"][None], s5_d=g["s5_d"],
        s5_w_glu_a=g["s5_w_glu_a"][None], s5_w_glu_b=g["s5_w_glu_b"][None])


def _loss_fn(y, t):
    e = y - t
    part = jnp.sum(jnp.sum(e * e, axis=-1, keepdims=True), axis=0, keepdims=True) * (0.5 / y.shape[1])
    return e * (1.0 / y.shape[1]), part


FF_SHARD = 352
FF_SHARD_PAD = 384


def _pad_groups(a, axis):
    axis %= a.ndim
    zeros = jnp.zeros(a.shape[:axis] + (FF_SHARD_PAD - FF_SHARD,) + a.shape[axis + 1:], a.dtype)
    pieces = []
    for g in range(a.shape[axis] // FF_SHARD):
        pieces += [lax.slice_in_dim(a, g * FF_SHARD, (g + 1) * FF_SHARD, axis=axis), zeros]
    return jnp.concatenate(pieces, axis=axis)


def _unpad_groups(a, axis):
    axis %= a.ndim
    pieces = [lax.slice_in_dim(a, g * FF_SHARD_PAD, g * FF_SHARD_PAD + FF_SHARD, axis=axis)
              for g in range(a.shape[axis] // FF_SHARD_PAD)]
    return pieces[0] if len(pieces) == 1 else jnp.concatenate(pieces, axis=axis)


def _layer_weights(p, layer, dt):
    return dict(
        norm_xa=p["norm_xa"][layer][None], norm_mem=p["norm_mem"][layer][None], norm_ffn=p["norm_ffn"][layer][None],
        xa_wq=p["xa_wq"][layer].astype(dt), xa_wk=p["xa_wk"][layer].astype(dt), xa_wv=p["xa_wv"][layer].astype(dt),
        xa_wo=p["xa_wo"][layer].astype(dt), xa_q_norm=p["xa_q_norm"][layer][None],
        xa_k_norm=p["xa_k_norm"][layer][None], ffn_w_up=_pad_groups(p["ffn_w_up"][layer], 1).astype(dt),
        ffn_conv_w=_pad_groups(p["ffn_conv_w"][layer], 1), ffn_conv_b=_pad_groups(p["ffn_conv_b"][layer][None], 1),
        ffn_w_down=_pad_groups(p["ffn_w_down"][layer], 0).astype(dt))


_PER_LAYER = ("norm_xa", "norm_mem", "norm_ffn", "xa_wq", "xa_wk", "xa_wv", "xa_wo", "xa_q_norm", "xa_k_norm",
              "ffn_w_up", "ffn_conv_w", "ffn_conv_b", "ffn_w_down")
_FFN_PADDED = dict(ffn_w_up=1, ffn_conv_w=1, ffn_conv_b=1, ffn_w_down=0)


def _local_step(x, mem, positions, target, p):
    cos, sin = _rope_tables(positions)
    we = _even_weights(p, 0, 0, bf16)
    wo = _odd_weights(p, 0, 1, bf16)
    wl = [_layer_weights(p, layer, bf16) for layer in range(2)]
    loss, dh, g_even, g_odd, gl = _local_core(x, mem, cos, sin, target, we, wo, wl)
    grads = {}
    for n in _PER_LAYER:
        a, b = gl[0][n], gl[1][n]
        if n in _FFN_PADDED:
            a, b = _unpad_groups(a, _FFN_PADDED[n]), _unpad_groups(b, _FFN_PADDED[n])
        grads[n] = jnp.concatenate([a, b], axis=0) if a.shape[0] == 1 else jnp.stack([a, b])
    ge, go = _even_grads(g_even), _odd_grads(g_odd)
    grads["norm_mix"] = jnp.concatenate([ge.pop("norm_mix"), go.pop("norm_mix")], axis=0)
    grads.update(ge)
    grads.update(go)
    return loss, dh, grads


def _local_core(x, mem, cos, sin, target, we, wo, wl):
    h, s_mix0 = _mixer_fwd(x, cos, sin, we, "l0")
    h, s_xa0 = _xattn_fwd(h, mem, wl[0], "l0")
    h, s_ff0 = _ffn_fwd(h, wl[0], "l0")
    h, s_mix1 = _s5_fwd(h, wo, "l1")
    h, s_xa1 = _xattn_fwd(h, mem, wl[1], "l1")
    h, s_ff1 = _ffn_fwd(h, wl[1], "l1")
    dh, loss = _rows(_loss_fn, [h, target], [], [(h.shape[1], f32)], accs=[(1, 1)], name="loss_head")

    gl = [{}, {}]
    dh, g = _ffn_bwd(dh, wl[1], s_ff1, "l1")
    gl[1].update(g)
    dh, g = _xattn_bwd(dh, mem, wl[1], s_xa1, "l1")
    gl[1].update(g)
    dh, g_odd = _s5_bwd(dh, wo, s_mix1, "l1")
    dh, g = _ffn_bwd(dh, wl[0], s_ff0, "l0")
    gl[0].update(g)
    dh, g = _xattn_bwd(dh, mem, wl[0], s_xa0, "l0")
    gl[0].update(g)
    dh, g_even = _mixer_bwd(dh, cos, sin, we, s_mix0, "l0")
    return loss, dh, g_even, g_odd, gl


_LANES = 1024
_ROW_PAD = 16


_PEER_MASKS = (1, 2, 4, 3, 5, 6, 7)


def _mesh_place():
    x, y, c = lax.axis_index("x"), lax.axis_index("y"), lax.axis_index("c")

    def peer(mask):
        px = 1 - x if mask & 4 else x
        py = 1 - y if mask & 2 else y
        pc = 1 - c if mask & 1 else c
        return (px, py, pc), 4 * px + 2 * py + pc

    return 4 * x + 2 * y + c, peer


class _Exchange:
    def __init__(self, name):
        self.name = name
        self.srcs, self.shapes, self.items, self.where = [], [], [], {}

    def add(self, src, land_shape, src_at, dst_at, key):
        si = next((i for i, s in enumerate(self.srcs) if s is src), None)
        if si is None:
            self.srcs.append(src)
            si = len(self.srcs) - 1
        if key not in self.where:
            self.shapes.append(land_shape)
            self.where[key] = len(self.shapes) - 1
        self.items.append(dict(src=si, dst=self.where[key], src_at=src_at, dst_at=dst_at))

    def _copy(self, k, mask, ins, lands, send_sems, recv_sems, me, peer, arriving):
        it = self.items[k]
        dev, idx = peer(mask)
        s = k * (N_DEV - 1) + _PEER_MASKS.index(mask)
        return pltpu.make_async_remote_copy(
            src_ref=it["src_at"](ins[it["src"]], idx), dst_ref=it["dst_at"](lands[it["dst"]], idx if arriving else me),
            send_sem=send_sems.at[s], recv_sem=recv_sems.at[s], device_id=dev, device_id_type=pl.DeviceIdType.MESH)

    def _own_copy(self, k, ins, lands, own_sems, me):
        it = self.items[k]
        return pltpu.make_async_copy(it["src_at"](ins[it["src"]], me), it["dst_at"](lands[it["dst"]], me), own_sems.at[k])

    def begin(self, own):
        ns, nd, ni = len(self.srcs), len(self.shapes), len(self.items)
        nsem = ni * (N_DEV - 1)
        self.own = own

        nq = 3 if own else 2

        def body(*refs):
            ins, land_refs = refs[:ns], refs[ns:ns + nd]
            sems, token = refs[ns + nd:ns + nd + nq], refs[-1]
            me, peer = _mesh_place()
            for mask in _PEER_MASKS:
                for k in range(ni):
                    self._copy(k, mask, ins, land_refs, sems[0], sems[1], me, peer, False).start()
            if own:
                for k in range(ni):
                    self._own_copy(k, ins, land_refs, sems[2], me).start()
            token[...] = jnp.zeros_like(token)

        hbm = pl.BlockSpec(memory_space=pltpu.HBM)
        sem = pl.BlockSpec(memory_space=pltpu.SEMAPHORE)
        lands = [lax.empty(s.shape, s.dtype) for s in self.shapes]
        sem_shapes = [pltpu.SemaphoreType.DMA((nsem,)), pltpu.SemaphoreType.DMA((nsem,)), pltpu.SemaphoreType.DMA((ni,))]
        res = pl.pallas_call(
            body, in_specs=[hbm] * (ns + nd),
            out_specs=[sem] * nq + [hbm] * nd + [pl.BlockSpec(memory_space=pltpu.VMEM)],
            out_shape=sem_shapes[:nq] + [pltpu.HBM(s.shape, s.dtype) for s in self.shapes]
            + [jax.ShapeDtypeStruct((8, 128), f32)],
            input_output_aliases={ns + j: nq + j for j in range(nd)},
            compiler_params=pltpu.CompilerParams(has_side_effects=pltpu.SideEffectType.DATAFLOW_SIDE_EFFECTING),
            name=self.name + "_start")(*self.srcs, *lands)
        self.token = res[-1]
        return list(res[:nq]), list(res[nq:-1])

    def finish(self, state, after):
        sems, lands = state
        nq = len(sems)
        after = list(after) if isinstance(after, (list, tuple)) else [after]
        ns, nd, ni = len(self.srcs), len(self.shapes), len(self.items)

        def body(*refs):
            ins, land_refs = refs[:ns], refs[ns:ns + nd]
            sem_refs = refs[ns + nd:ns + nd + nq]
            me, peer = _mesh_place()
            for mask in _PEER_MASKS:
                for k in range(ni):
                    cp = self._copy(k, mask, ins, land_refs, sem_refs[0], sem_refs[1], me, peer, True)
                    cp.wait_send()
                    cp.wait_recv()
            if self.own:
                for k in range(ni):
                    self._own_copy(k, ins, land_refs, sem_refs[2], me).wait()

        hbm = pl.BlockSpec(memory_space=pltpu.HBM)
        sem = pl.BlockSpec(memory_space=pltpu.SEMAPHORE)
        res = pl.pallas_call(
            body, in_specs=[hbm] * (ns + nd) + [sem] * nq + [pl.BlockSpec(memory_space=pl.ANY)] * len(after),
            out_specs=[hbm] * nd, out_shape=[pltpu.HBM(s.shape, s.dtype) for s in self.shapes],
            input_output_aliases={ns + j: j for j in range(nd)},
            compiler_params=pltpu.CompilerParams(has_side_effects=pltpu.SideEffectType.DATAFLOW_SIDE_EFFECTING),
            name=self.name + "_wait")(*self.srcs, *lands, *sems, *after)
        return {k: res[i] for k, i in self.where.items()}


def _after(x, *tokens, name):
    def body(*refs):
        del refs

    anyspace = pl.BlockSpec(memory_space=pl.ANY)
    return pl.pallas_call(body, in_specs=[anyspace] * (1 + len(tokens)), out_specs=anyspace,
                          out_shape=jax.ShapeDtypeStruct(x.shape, x.dtype), input_output_aliases={0: 0},
                          name=name)(x, *tokens)


def _rows_of(n):
    return lambda r, i: r.at[pl.ds(pl.multiple_of(i * n, n), n), :]


def _cols_of(n):
    return lambda r, i: r.at[:, pl.ds(pl.multiple_of(i * n, n), n)]


def _whole(r, i):
    return r


def _slot(r, i):
    return r.at[i]


def _at_layer(layer):
    return lambda r, i: r.at[layer]


def _slot_layer(layer):
    return lambda r, i: r.at[i, layer]


def _sum_adam(me_index, slots, owns, own_block, w, m, v, *, name):
    layers, rows, cols = w.shape
    tr = _pick(rows, (256, 128, 104, 64, 32, 16, 8))
    bc1 = 1.0 - ADAM_B1 ** ADAM_STEP
    bc2 = 1.0 - ADAM_B2 ** ADAM_STEP
    own_shape, own_map = own_block(tr)
    nl = len(slots)
    assert nl == layers and len(owns) == layers

    def body(me_ref, *refs):
        s_refs, own_refs = refs[:nl], refs[nl:2 * nl]
        w_ref, m_ref, v_ref, g_ref, d_ref, nm_ref, nv_ref = refs[2 * nl:]
        me = me_ref[0]

        def run(s_ref, own_ref):
            mine = own_ref[0] if len(own_shape) == 3 else own_ref[...]
            g = jnp.where(me == 0, mine, s_ref[0])
            for k in range(1, N_DEV):
                g = g + jnp.where(me == k, mine, s_ref[k])
            mm = ADAM_B1 * m_ref[0] + (1.0 - ADAM_B1) * g
            vv = ADAM_B2 * v_ref[0] + (1.0 - ADAM_B2) * (g * g)
            g_ref[0] = g
            nm_ref[0] = mm
            nv_ref[0] = vv
            d_ref[0] = -ADAM_LR * ((mm / bc1) / (jnp.sqrt(vv / bc2) + ADAM_EPS) + ADAM_WD * w_ref[0])

        for layer in range(nl):
            pl.when(pl.program_id(0) == layer)(functools.partial(run, s_refs[layer], own_refs[layer]))

    def of_layer(layer, index_map):
        return lambda lyr, i, me: index_map(jnp.where(lyr == layer, i, 0), me)

    blk = pl.BlockSpec((1, tr, cols), lambda lyr, i, me: (lyr, i, 0))
    sds = jax.ShapeDtypeStruct((layers, rows, cols), f32)
    grid_spec = pltpu.PrefetchScalarGridSpec(
        num_scalar_prefetch=1, grid=(layers, rows // tr),
        in_specs=[pl.BlockSpec((N_DEV, tr, cols), of_layer(layer, lambda i, me: (0, i, 0))) for layer in range(nl)]
        + [pl.BlockSpec(own_shape, of_layer(layer, own_map)) for layer in range(nl)] + [blk, blk, blk],
        out_specs=[blk] * 4)
    return pl.pallas_call(body, grid_spec=grid_spec, out_shape=[sds] * 4,
                          compiler_params=_cparams(("arbitrary", "arbitrary")),
                          name=name)(me_index, *slots, *owns, w, m, v)


_SHARDED = dict(xa_wq=1, xa_wk=1, xa_wv=1, xa_wo=1, ffn_w_up=2, ffn_conv_w=2, ffn_w_down=1, mix_w_in=2, mla_w_uq=2,
                mla_w_ukv=2, mix_w_out=1, s5_d=1, s5_w_glu_a=1, s5_w_glu_b=1)
_EXACT = ("ffn_conv_w", "s5_d")
_WEIGHTS = ("norm_mix", "norm_xa", "norm_mem", "norm_ffn", "xa_wq", "xa_wk", "xa_wv", "xa_wo", "xa_q_norm",
            "xa_k_norm", "ffn_w_up", "ffn_conv_w", "ffn_conv_b", "ffn_w_down", "hg_lb_logits", "mix_w_in",
            "hg_out_norm", "mla_q_a_norm", "mla_w_uq", "mla_kv_a_norm", "mla_w_ukv", "mla_qn_nope", "mla_qn_rope",
            "mla_kn_nope", "mla_kn_rope", "mix_w_out", "s5_lam_re", "s5_lam_im", "s5_log_dt", "s5_b_re", "s5_b_im",
            "s5_c_re", "s5_c_im", "s5_d", "s5_w_glu_a", "s5_w_glu_b")
_BIG = tuple(n for n in _WEIGHTS if n in _SHARDED and n not in _EXACT)
_SHARD_ORDER = tuple(n for n in _WEIGHTS if n in _SHARDED)
_REPL_ORDER = tuple(n for n in _WEIGHTS if n not in _SHARDED)
_REPL_EARLY = tuple(n for n in _REPL_ORDER if n.startswith("s5_"))
_REPL_LATE = tuple(n for n in _REPL_ORDER if n not in _REPL_EARLY)


def _pack(parts, dtype, lead=None):
    nl = 0 if lead is None else 1
    flat = [a.astype(dtype).reshape(a.shape[:nl] + (-1,)) for a in parts]
    cat = jnp.concatenate(flat, axis=nl)
    n = cat.shape[nl]
    unit = _LANES * _ROW_PAD
    total = -(-n // unit) * unit
    cat = jnp.pad(cat, [(0, 0)] * nl + [(0, total - n)])
    return cat.reshape(cat.shape[:nl] + (total // _LANES, _LANES))


def _unpack(packed, shapes, lead=None):
    nl = 0 if lead is None else 1
    flat = packed.reshape(packed.shape[:nl] + (-1,))
    out, off = [], 0
    for s in shapes:
        n = int(np.prod(s))
        piece = flat[..., off:off + n] if nl else flat[off:off + n]
        out.append(piece.reshape(packed.shape[:nl] + tuple(s)))
        off += n
    return out


def _to_full(gathered, axis):
    g = jnp.moveaxis(gathered, 0, axis)
    s = g.shape
    return g.reshape(s[:axis] + (s[axis] * s[axis + 1],) + s[axis + 2:])


def _to_shards(full, axis):
    s = full.shape
    g = full.reshape(s[:axis] + (N_DEV, s[axis] // N_DEV) + s[axis + 1:])
    return jnp.moveaxis(g, axis, 0)


_DIRECT_ROWS = ("xa_wq", "xa_wk", "xa_wv", "xa_wo", "mix_w_out", "s5_w_glu_a", "s5_w_glu_b")
_SMALL16 = ("mix_w_in", "mla_w_uq", "mla_w_ukv")
_SMALL_SHARDED = ("mla_w_uq", "mla_w_ukv") + _EXACT
_SHARD_ROWS = 128


def _exchange_layout(d):
    out = dict(d)
    out["ffn_w_up"] = _pad_groups(d["ffn_w_up"], 2)
    out["ffn_conv_w"] = _pad_groups(d["ffn_conv_w"], 2)
    out["ffn_w_down"] = _pad_groups(d["ffn_w_down"], 1)
    return out


def _train_step(x, mem, positions, target, w, m, v):
    d_model = x.shape[1]
    we_, me_, ve_ = _exchange_layout(w), _exchange_layout(m), _exchange_layout(v)
    sds = jax.ShapeDtypeStruct

    matrices = _DIRECT_ROWS + ("ffn_w_up", "ffn_w_down")
    layer_mats = ("xa_wq", "xa_wk", "xa_wv", "xa_wo", "ffn_w_up", "ffn_w_down")
    shard16 = {n: we_[n].astype(bf16) for n in matrices}
    part_of = {n: _rows_of(_SHARD_ROWS) for n in _DIRECT_ROWS}
    part_of["ffn_w_up"] = _cols_of(we_["ffn_w_up"].shape[2])
    part_of["ffn_w_down"] = _rows_of(we_["ffn_w_down"].shape[1])
    part_shape = {n: we_[n].shape[1:] for n in matrices}

    def full_shape(n):
        r, c = part_shape[n]
        return (r, N_DEV * c) if n == "ffn_w_up" else (N_DEV * r, c)

    def gather(ex, n, layer):
        ex.add(shard16[n], sds(full_shape(n), bf16), _at_layer(layer), part_of[n], (n, layer))

    def scatter(ex, n, layer, grad):
        ex.add(grad, sds((N_DEV,) + part_shape[n], f32), part_of[n], _slot, (n, layer))

    small16 = _pack([we_[n] for n in _SMALL16], bf16)
    exact = _pack([we_[n] for n in _EXACT], f32)
    ga, ga1, gb, gc = _Exchange("gather_a"), _Exchange("gather_a1"), _Exchange("gather_b"), _Exchange("gather_c")
    ga.add(small16, sds((N_DEV,) + small16.shape, bf16), _whole, _slot, "small16")
    ga1.add(exact, sds((N_DEV,) + exact.shape, f32), _whole, _slot, "exact")
    gather(ga1, "mix_w_out", 0)
    for n in layer_mats:
        gather(gb, n, 0)
    gather(gc, "s5_w_glu_a", 0)
    gather(gc, "s5_w_glu_b", 0)
    for n in layer_mats:
        gather(gc, n, 1)
    state_a, state_a1, state_b, state_c = ga.begin(True), ga1.begin(True), gb.begin(True), gc.begin(True)

    p = {n: w[n] for n in _REPL_ORDER}
    cos, sin = _rope_tables(positions)
    wo = _odd_weights(p, 0, 1, bf16)
    conv_b = _pad_groups(w["ffn_conv_b"], 1)
    prepared = [cos, sin, conv_b, wo["s5_c3"], wo["s5_bt_re"], wo["s5_bt_im"]]
    full = ga.finish(state_a, [ga1.token, gb.token, gc.token] + prepared)
    for n, a in zip(_SMALL16, _unpack(full["small16"], [we_[n].shape for n in _SMALL16], lead=True)):
        p[n] = _to_full(a, _SHARDED[n])
    we = _even_weights(p, 0, 0, bf16)
    we["norm_mix"] = _after(we["norm_mix"], ga.token, ga1.token, gb.token, gc.token, name="after_gather_starts")

    def late_mix_w_out(mixin):
        full.update(ga1.finish(state_a1, [mixin]))
        return full[("mix_w_out", 0)]

    we["mix_w_out"] = late_mix_w_out
    h, s_mix0 = _mixer_fwd(x, cos, sin, we, "l0")
    conv_w, s5_d = [_to_full(a, _SHARDED[n]) for n, a in
                    zip(_EXACT, _unpack(full["exact"], [we_[n].shape for n in _EXACT], lead=True))]

    def layer_weights(layer):
        return dict(norm_xa=w["norm_xa"][layer][None], norm_mem=w["norm_mem"][layer][None],
                    norm_ffn=w["norm_ffn"][layer][None], xa_q_norm=w["xa_q_norm"][layer][None],
                    xa_k_norm=w["xa_k_norm"][layer][None], ffn_conv_w=conv_w[layer],
                    ffn_conv_b=conv_b[layer][None], **{n: full[(n, layer)] for n in layer_mats})

    full.update(gb.finish(state_b, h))
    wl = [layer_weights(0)]
    h, s_xa0 = _xattn_fwd(h, mem, wl[0], "l0")
    h, s_ff0 = _ffn_fwd(h, wl[0], "l0")
    full.update(gc.finish(state_c, h))
    wl.append(layer_weights(1))
    wo.update(s5_d=s5_d, s5_w_glu_a=full[("s5_w_glu_a", 0)], s5_w_glu_b=full[("s5_w_glu_b", 0)])
    h, s_mix1 = _s5_fwd(h, wo, "l1")
    h, s_xa1 = _xattn_fwd(h, mem, wl[1], "l1")
    h, s_ff1 = _ffn_fwd(h, wl[1], "l1")
    dh, loss = _rows(_loss_fn, [h, target], [], [(h.shape[1], f32)], accs=[(1, 1)], name="loss_head")

    gl = [{}, {}]
    reduces = []

    own_grad = {}

    def reduce_start(name, entries, dh):
        ex = _Exchange(name)
        for n, layer, grad in entries.get("matrices", ()):
            scatter(ex, n, layer, grad)
            own_grad[(n, layer)] = grad
        for key, src, shape, src_at in entries.get("packs", ()):
            ex.add(src, shape, src_at, _slot, key)
        reduces.append((ex, ex.begin(False)))
        return _after(dh, ex.token, name="after_" + name)

    dh, gl[1] = _ffn_bwd(dh, wl[1], s_ff1, "l1")
    dh = reduce_start("reduce_ffn1", dict(matrices=[(n, 1, gl[1][n]) for n in ("ffn_w_up", "ffn_w_down")]), dh)
    dh, g = _xattn_bwd(dh, mem, wl[1], s_xa1, "l1")
    gl[1].update(g)
    dh = reduce_start("reduce_xa1", dict(matrices=[(n, 1, g[n]) for n in ("xa_wq", "xa_wk", "xa_wv", "xa_wo")]), dh)
    dh, g_odd = _s5_bwd(dh, wo, s_mix1, "l1")
    go = _odd_grads(g_odd)
    dh, gl[0] = _ffn_bwd(dh, wl[0], s_ff0, "l0")
    send_early = _pack([go[n].reshape(w[n].shape) for n in _REPL_EARLY], f32)
    dh = reduce_start("reduce_ffn0", dict(
        matrices=[(n, 0, g_odd[n]) for n in ("s5_w_glu_a", "s5_w_glu_b")]
        + [(n, 0, gl[0][n]) for n in ("ffn_w_up", "ffn_w_down")],
        packs=[("repl_early", send_early, sds((N_DEV,) + send_early.shape, f32), _whole)]), dh)
    dh, g = _xattn_bwd(dh, mem, wl[0], s_xa0, "l0")
    gl[0].update(g)
    dh = reduce_start("reduce_xa0", dict(matrices=[(n, 0, g[n]) for n in ("xa_wq", "xa_wk", "xa_wv", "xa_wo")]), dh)
    grad_x, g_even = _mixer_bwd(
        dh, cos, sin, we, s_mix0, "l0",
        on_w_out=lambda grad, dmixin: reduce_start("reduce_w_out", dict(matrices=[("mix_w_out", 0, grad)]), dmixin))

    ge = _even_grads(g_even)
    cat = lambda n: jnp.concatenate([gl[0][n], gl[1][n]], axis=0)
    rg = dict(ge)
    rg["norm_mix"] = jnp.concatenate([ge["norm_mix"], go["norm_mix"]], axis=0)
    for n in ("norm_xa", "norm_mem", "norm_ffn", "xa_q_norm", "xa_k_norm"):
        rg[n] = cat(n)
    rg["ffn_conv_b"] = _unpad_groups(cat("ffn_conv_b"), 1)
    sg = dict(mla_w_uq=ge["mla_w_uq"], mla_w_ukv=ge["mla_w_ukv"], s5_d=go["s5_d"],
              ffn_conv_w=jnp.stack([gl[0]["ffn_conv_w"], gl[1]["ffn_conv_w"]]))
    send_small = _pack([_to_shards(sg[n], _SHARDED[n]) for n in _SMALL_SHARDED], f32, lead=True)
    send_late = _pack([rg[n].reshape(w[n].shape) for n in _REPL_LATE], f32)
    w_in_rows = w["mix_w_in"].shape[2]
    last = _Exchange("reduce_last")
    last.add(g_even["mix_w_in_t"], sds((N_DEV, w_in_rows, d_model), f32), _rows_of(w_in_rows), _slot, "mix_w_in")
    last.add(send_small, sds(send_small.shape, f32), _slot, _slot, "small")
    last.add(send_late, sds((N_DEV,) + send_late.shape, f32), _whole, _slot, "repl_late")
    state_last = last.begin(False)
    slots = {}
    for ex, state in reduces:
        slots.update(ex.finish(state, [grad_x, last.token]))

    me_index = (4 * lax.axis_index("x") + 2 * lax.axis_index("y") + lax.axis_index("c")).astype(jnp.int32).reshape(1)

    def rows_block(r, c):
        return lambda tr: ((tr, c), lambda i, me: (me[0] * (r // tr) + i, 0))

    def own_block(n):
        r, c = part_shape[n]
        if n == "ffn_w_up":
            return lambda tr: ((tr, c), lambda i, me: (i, me[0]))
        return rows_block(r, c)

    out = [{}, {}, {}, {}]
    unpad = dict(ffn_w_up=2, ffn_conv_w=2, ffn_w_down=1)
    for n in matrices:
        layers = range(we_[n].shape[0])
        res = _sum_adam(me_index, [slots[(n, layer)] for layer in layers], [own_grad[(n, layer)] for layer in layers],
                        own_block(n), we_[n], me_[n], ve_[n], name=f"adam_{n}")
        for k in range(4):
            out[k][n] = _unpad_groups(res[k], unpad[n]) if n in unpad else res[k]
    pk = lambda d, order: _pack([d[n] for n in order], f32)[None]
    whole_rows = lambda tr: ((tr, _LANES), lambda i, me: (i, 0))
    res_early = _sum_adam(me_index, [slots["repl_early"]], [send_early], whole_rows, pk(w, _REPL_EARLY),
                          pk(m, _REPL_EARLY), pk(v, _REPL_EARLY), name="adam_repl_early")
    for k in range(4):
        out[k].update(zip(_REPL_EARLY, _unpack(res_early[k][0], [w[n].shape for n in _REPL_EARLY])))
    done = [out[k][n] for k in range(4) for n in matrices + _REPL_EARLY]
    slots = last.finish(state_last, done)
    transposed = lambda d: jnp.swapaxes(d["mix_w_in"], 1, 2)
    res_w_in = _sum_adam(me_index, [slots["mix_w_in"]], [g_even["mix_w_in_t"]], rows_block(w_in_rows, d_model),
                         transposed(w), transposed(m), transposed(v), name="adam_mix_w_in")
    res_small = _sum_adam(me_index, [slots["small"]], [send_small],
                          lambda tr: ((1, tr, _LANES), lambda i, me: (me[0], i, 0)),
                          pk(we_, _SMALL_SHARDED), pk(me_, _SMALL_SHARDED), pk(ve_, _SMALL_SHARDED), name="adam_small")
    res_late = _sum_adam(me_index, [slots["repl_late"]], [send_late], whole_rows, pk(w, _REPL_LATE), pk(m, _REPL_LATE),
                         pk(v, _REPL_LATE), name="adam_repl_late")
    for k in range(4):
        out[k]["mix_w_in"] = jnp.swapaxes(res_w_in[k], 1, 2)
        for n, a in zip(_SMALL_SHARDED, _unpack(res_small[k][0], [we_[n].shape for n in _SMALL_SHARDED])):
            out[k][n] = _unpad_groups(a, unpad[n]) if n in unpad else a
        out[k].update(zip(_REPL_LATE, _unpack(res_late[k][0], [w[n].shape for n in _REPL_LATE])))
    return loss, grad_x, out


_INPUTS = tuple("""x, mem, positions, norm_mix, norm_xa, norm_mem, norm_ffn, xa_wq, xa_wk, xa_wv, xa_wo, xa_q_norm, xa_k_norm, ffn_w_up, ffn_conv_w, ffn_conv_b, ffn_w_down, hg_lb_logits, mix_w_in, hg_out_norm, mla_q_a_norm, mla_w_uq, mla_kv_a_norm, mla_w_ukv, mla_qn_nope, mla_qn_rope, mla_kn_nope, mla_kn_rope, mix_w_out, s5_lam_re, s5_lam_im, s5_log_dt, s5_b_re, s5_b_im, s5_c_re, s5_c_im, s5_d, s5_w_glu_a, s5_w_glu_b, loss_target, m_norm_mix, m_norm_xa, m_norm_mem, m_norm_ffn, m_xa_wq, m_xa_wk, m_xa_wv, m_xa_wo, m_xa_q_norm, m_xa_k_norm, m_ffn_w_up, m_ffn_conv_w, m_ffn_conv_b, m_ffn_w_down, m_hg_lb_logits, m_mix_w_in, m_hg_out_norm, m_mla_q_a_norm, m_mla_w_uq, m_mla_kv_a_norm, m_mla_w_ukv, m_mla_qn_nope, m_mla_qn_rope, m_mla_kn_nope, m_mla_kn_rope, m_mix_w_out, m_s5_lam_re, m_s5_lam_im, m_s5_log_dt, m_s5_b_re, m_s5_b_im, m_s5_c_re, m_s5_c_im, m_s5_d, m_s5_w_glu_a, m_s5_w_glu_b, v_norm_mix, v_norm_xa, v_norm_mem, v_norm_ffn, v_xa_wq, v_xa_wk, v_xa_wv, v_xa_wo, v_xa_q_norm, v_xa_k_norm, v_ffn_w_up, v_ffn_conv_w, v_ffn_conv_b, v_ffn_w_down, v_hg_lb_logits, v_mix_w_in, v_hg_out_norm, v_mla_q_a_norm, v_mla_w_uq, v_mla_kv_a_norm, v_mla_w_ukv, v_mla_qn_nope, v_mla_qn_rope, v_mla_kn_nope, v_mla_kn_rope, v_mix_w_out, v_s5_lam_re, v_s5_lam_im, v_s5_log_dt, v_s5_b_re, v_s5_b_im, v_s5_c_re, v_s5_c_im, v_s5_d, v_s5_w_glu_a, v_s5_w_glu_b""".replace(" ", "").split(","))


def kernel(x, mem, positions, norm_mix, norm_xa, norm_mem, norm_ffn, xa_wq, xa_wk, xa_wv, xa_wo, xa_q_norm, xa_k_norm, ffn_w_up, ffn_conv_w, ffn_conv_b, ffn_w_down, hg_lb_logits, mix_w_in, hg_out_norm, mla_q_a_norm, mla_w_uq, mla_kv_a_norm, mla_w_ukv, mla_qn_nope, mla_qn_rope, mla_kn_nope, mla_kn_rope, mix_w_out, s5_lam_re, s5_lam_im, s5_log_dt, s5_b_re, s5_b_im, s5_c_re, s5_c_im, s5_d, s5_w_glu_a, s5_w_glu_b, loss_target, m_norm_mix, m_norm_xa, m_norm_mem, m_norm_ffn, m_xa_wq, m_xa_wk, m_xa_wv, m_xa_wo, m_xa_q_norm, m_xa_k_norm, m_ffn_w_up, m_ffn_conv_w, m_ffn_conv_b, m_ffn_w_down, m_hg_lb_logits, m_mix_w_in, m_hg_out_norm, m_mla_q_a_norm, m_mla_w_uq, m_mla_kv_a_norm, m_mla_w_ukv, m_mla_qn_nope, m_mla_qn_rope, m_mla_kn_nope, m_mla_kn_rope, m_mix_w_out, m_s5_lam_re, m_s5_lam_im, m_s5_log_dt, m_s5_b_re, m_s5_b_im, m_s5_c_re, m_s5_c_im, m_s5_d, m_s5_w_glu_a, m_s5_w_glu_b, v_norm_mix, v_norm_xa, v_norm_mem, v_norm_ffn, v_xa_wq, v_xa_wk, v_xa_wv, v_xa_wo, v_xa_q_norm, v_xa_k_norm, v_ffn_w_up, v_ffn_conv_w, v_ffn_conv_b, v_ffn_w_down, v_hg_lb_logits, v_mix_w_in, v_hg_out_norm, v_mla_q_a_norm, v_mla_w_uq, v_mla_kv_a_norm, v_mla_w_ukv, v_mla_qn_nope, v_mla_qn_rope, v_mla_kn_nope, v_mla_kn_rope, v_mix_w_out, v_s5_lam_re, v_s5_lam_im, v_s5_log_dt, v_s5_b_re, v_s5_b_im, v_s5_c_re, v_s5_c_im, v_s5_d, v_s5_w_glu_a, v_s5_w_glu_b):
    vals = dict(zip(_INPUTS, (x, mem, positions, norm_mix, norm_xa, norm_mem, norm_ffn, xa_wq, xa_wk, xa_wv, xa_wo, xa_q_norm, xa_k_norm, ffn_w_up, ffn_conv_w, ffn_conv_b, ffn_w_down, hg_lb_logits, mix_w_in, hg_out_norm, mla_q_a_norm, mla_w_uq, mla_kv_a_norm, mla_w_ukv, mla_qn_nope, mla_qn_rope, mla_kn_nope, mla_kn_rope, mix_w_out, s5_lam_re, s5_lam_im, s5_log_dt, s5_b_re, s5_b_im, s5_c_re, s5_c_im, s5_d, s5_w_glu_a, s5_w_glu_b, loss_target, m_norm_mix, m_norm_xa, m_norm_mem, m_norm_ffn, m_xa_wq, m_xa_wk, m_xa_wv, m_xa_wo, m_xa_q_norm, m_xa_k_norm, m_ffn_w_up, m_ffn_conv_w, m_ffn_conv_b, m_ffn_w_down, m_hg_lb_logits, m_mix_w_in, m_hg_out_norm, m_mla_q_a_norm, m_mla_w_uq, m_mla_kv_a_norm, m_mla_w_ukv, m_mla_qn_nope, m_mla_qn_rope, m_mla_kn_nope, m_mla_kn_rope, m_mix_w_out, m_s5_lam_re, m_s5_lam_im, m_s5_log_dt, m_s5_b_re, m_s5_b_im, m_s5_c_re, m_s5_c_im, m_s5_d, m_s5_w_glu_a, m_s5_w_glu_b, v_norm_mix, v_norm_xa, v_norm_mem, v_norm_ffn, v_xa_wq, v_xa_wk, v_xa_wv, v_xa_wo, v_xa_q_norm, v_xa_k_norm, v_ffn_w_up, v_ffn_conv_w, v_ffn_conv_b, v_ffn_w_down, v_hg_lb_logits, v_mix_w_in, v_hg_out_norm, v_mla_q_a_norm, v_mla_w_uq, v_mla_kv_a_norm, v_mla_w_ukv, v_mla_qn_nope, v_mla_qn_rope, v_mla_kn_nope, v_mla_kn_rope, v_mix_w_out, v_s5_lam_re, v_s5_lam_im, v_s5_log_dt, v_s5_b_re, v_s5_b_im, v_s5_c_re, v_s5_c_im, v_s5_d, v_s5_w_glu_a, v_s5_w_glu_b)))
    w = {n: vals[n] for n in _WEIGHTS}
    m = {n: vals["m_" + n] for n in _WEIGHTS}
    v = {n: vals["v_" + n] for n in _WEIGHTS}
    loss, grad_x, res = _train_step(vals["x"][0], vals["mem"][0], vals["positions"][0], vals["loss_target"][0],
                                    w, m, v)
    loss = lax.psum(loss[0, 0], ("x", "y", "c"))
    return (loss, grad_x[None], *[r[n] for r in res for n in _WEIGHTS])
```

```python
import functools

import jax
import jax.numpy as jnp
import numpy as np
from jax import lax
from jax.experimental import pallas as pl
from jax.experimental.pallas import tpu as pltpu

f32 = jnp.float32
bf16 = jnp.bfloat16

EPS = 1e-6
N_DEV = 8
VMEM_LIMIT = 52 * 1024 * 1024

HG_HEADS = 4
HG_DIM = 128
HG_WIDTH = HG_HEADS * HG_DIM
HG_CHUNK = 64
HG_SUB = 16
MLA_HEADS = 4
MLA_Q_RANK = 256
MLA_KV_RANK = 128
MLA_NOPE = 128
MLA_ROPE = 64
MLA_V = 128
MLA_QK = MLA_NOPE + MLA_ROPE
MLA_QK_PAD = 256
ROPE_BASE = 10000.0
IN_WIDTH = 4 * HG_WIDTH + MLA_Q_RANK + MLA_KV_RANK + MLA_ROPE
IN_PAD = 2560
XA_HEADS = 4
XA_DIM = 256
S5_GROUP = 16
S5_GROUPS = 64
S5_STATE = 64
CONV_W = 3

ADAM_LR = 0.001
ADAM_B1 = 0.9
ADAM_B2 = 0.999
ADAM_EPS = 1e-08
ADAM_WD = 0.01
ADAM_STEP = 10

_NT = (((1,), (1,)), ((), ()))
_TN = (((0,), (0,)), ((), ()))
_NN = (((1,), (0,)), ((), ()))


def _pick(n, cands):
    for c in cands:
        if n % c == 0:
            return c
    return n


def _cparams(sem):
    return pltpu.CompilerParams(dimension_semantics=sem, vmem_limit_bytes=VMEM_LIMIT)


_MM_BUDGET = 36 * 1024 * 1024
_MM_TILES = ((1024, 1024), (1024, 512), (512, 1024), (512, 512), (512, 256), (256, 512), (256, 256), (256, 128),
             (128, 256), (128, 128))


def _mm(a, b, *, name, ta=False, tb=False, out_dtype=f32, add=None, b2=None, kslab=None):
    m, k = (a.shape[1], a.shape[0]) if ta else a.shape
    nb = b.shape[0] if tb else b.shape[1]
    n = nb * (2 if b2 is not None else 1)
    slab, nslab = kslab if kslab is not None else (0, 1)
    assert (b.shape[1] // nslab if tb else b.shape[0]) == k, (a.shape, b.shape, ta, tb)
    assert b2 is None or (not tb and b2.shape == b.shape)
    isz = lambda x: jnp.dtype(x.dtype).itemsize
    bm = bn = None
    for cm, cn in _MM_TILES:
        if m % cm or nb % cn:
            continue
        need = 2 * (cm * k * isz(a) + cn * k * isz(b) * (2 if b2 is not None else 1)
                    + cm * cn * (jnp.dtype(out_dtype).itemsize + (4 if add is not None else 0)))
        if need <= _MM_BUDGET:
            bm, bn = cm, cn
            break
    assert bm is not None, (name, a.shape, b.shape)
    half = nb // bn
    dims = (((0 if ta else 1,), (1 if tb else 0,)), ((), ()))

    def body(*refs):
        refs = list(refs)
        a_ref, b_ref = refs[0], refs[1]
        b2_ref = refs.pop(2) if b2 is not None else None
        add_ref = refs[2] if add is not None else None
        o_ref = refs[-1]

        def run(rhs_ref):
            r = lax.dot_general(a_ref[...].astype(bf16), rhs_ref[...].astype(bf16), dims, preferred_element_type=f32)
            if add_ref is not None:
                r = r + add_ref[...].astype(f32)
            o_ref[...] = r.astype(o_ref.dtype)

        if b2_ref is None:
            run(b_ref)
        else:
            pl.when(pl.program_id(1) < half)(lambda: run(b_ref))
            pl.when(pl.program_id(1) >= half)(lambda: run(b2_ref))

    a_spec = pl.BlockSpec((k, bm), lambda i, j: (0, i)) if ta else pl.BlockSpec((bm, k), lambda i, j: (i, 0))
    if tb:
        b_spec = pl.BlockSpec((bn, k), lambda i, j: (j, slab))
    elif b2 is None:
        b_spec = pl.BlockSpec((k, bn), lambda i, j: (0, j))
    else:
        b_spec = pl.BlockSpec((k, bn), lambda i, j: (0, jnp.minimum(j, half - 1)))
    in_specs = [a_spec, b_spec]
    args = [a, b]
    if b2 is not None:
        in_specs.append(pl.BlockSpec((k, bn), lambda i, j: (0, jnp.maximum(j - half, 0))))
        args.append(b2)
    if add is not None:
        in_specs.append(pl.BlockSpec((bm, bn), lambda i, j: (i, j)))
        args.append(add)
    return pl.pallas_call(
        body, grid=(m // bm, n // bn), in_specs=in_specs,
        out_specs=pl.BlockSpec((bm, bn), lambda i, j: (i, j)),
        out_shape=jax.ShapeDtypeStruct((m, n), out_dtype),
        compiler_params=_cparams(("parallel", "parallel")), name=name)(*args)


def _as_tuple(x):
    return tuple(x) if isinstance(x, (tuple, list)) else (x,)


def _full_spec(p):
    nd = p.ndim
    return pl.BlockSpec(p.shape, lambda i, _nd=nd: (0,) * _nd)


def _window(a, start, width):
    assert start % width == 0 and width % 128 == 0
    return (a, start // width, width)


def _row_array(x):
    return x[0] if isinstance(x, tuple) else x


def _row_shape(x):
    return (x[0].shape[0], x[2]) if isinstance(x, tuple) else x.shape


def _row_spec(x, tile):
    if isinstance(x, tuple):
        return pl.BlockSpec((tile, x[2]), lambda i, _b=x[1]: (i, _b))
    return pl.BlockSpec((tile, x.shape[1]), lambda i: (i, 0))


def _rows(fn, rows, params, outs, *, name, tile=256, accs=()):
    length = _row_shape(rows[0])[0]
    tile = min(tile, length)
    nr, npar, no = len(rows), len(params), len(outs)

    def body(*refs):
        r, p, o = refs[:nr], refs[nr:nr + npar], refs[nr + npar:]
        res = _as_tuple(fn(*[x[...].astype(f32) for x in r], *[x[...] for x in p]))
        for kk in range(no):
            o[kk][...] = res[kk].astype(o[kk].dtype)
        if accs:
            @pl.when(pl.program_id(0) == 0)
            def _():
                for kk in range(no, no + len(accs)):
                    o[kk][...] = jnp.zeros_like(o[kk])
            for kk in range(no, no + len(accs)):
                o[kk][...] += res[kk]

    in_specs = [_row_spec(x, tile) for x in rows] + [_full_spec(p) for p in params]
    out_specs = [pl.BlockSpec((tile, w), lambda i: (i, 0)) for w, _ in outs]
    out_shape = [jax.ShapeDtypeStruct((length, w), d) for w, d in outs]
    for s in accs:
        out_specs.append(pl.BlockSpec(s, lambda i, _nd=len(s): (0,) * _nd))
        out_shape.append(jax.ShapeDtypeStruct(s, f32))
    res = pl.pallas_call(body, grid=(length // tile,), in_specs=in_specs, out_specs=out_specs, out_shape=out_shape,
                         compiler_params=_cparams(("arbitrary",)), name=name)(*[_row_array(x) for x in rows], *params)
    return res


def _rows_bwd(fn, rows, params, cts, *, name, rgrad, pgrad, tile=256, addends=None):
    addends = {i: (a if isinstance(a, list) else [(a, 0)]) for i, a in (addends or {}).items()}
    length = _row_shape(rows[0])[0]
    tile = min(tile, length)
    nr, npar, nc = len(rows), len(params), len(cts)
    ridx = [i for i in range(nr) if rgrad[i] is not None]
    pidx = [i for i in range(npar) if pgrad[i]]
    flat_addends = [(i, a, off) for i in sorted(addends) for a, off in addends[i]]
    na = len(flat_addends)

    def body(*refs):
        r, p, c = refs[:nr], refs[nr:nr + npar], refs[nr + npar:nr + npar + nc]
        ad = refs[nr + npar + nc:nr + npar + nc + na]
        o = refs[nr + npar + nc + na:]
        rv = [x[...].astype(f32) for x in r]
        pv = [x[...] for x in p]
        cv = tuple(x[...].astype(f32) for x in c)

        def g(*d):
            rr, pp = list(rv), list(pv)
            for n_, i_ in enumerate(ridx):
                rr[i_] = d[n_]
            for n_, i_ in enumerate(pidx):
                pp[i_] = d[len(ridx) + n_]
            return _as_tuple(fn(*rr, *pp))

        _, vjp = jax.vjp(g, *[rv[i] for i in ridx], *[pv[i] for i in pidx])
        grads = vjp(cv)
        for n_, i_ in enumerate(ridx):
            val = grads[n_]
            for k_, (j_, a_, off) in enumerate(flat_addends):
                if j_ == i_:
                    extra = ad[k_][...].astype(f32)
                    if extra.shape[1] != val.shape[1]:
                        extra = jnp.pad(extra, ((0, 0), (off, val.shape[1] - off - extra.shape[1])))
                    val = val + extra
            o[n_][...] = val.astype(o[n_].dtype)
        if pidx:
            @pl.when(pl.program_id(0) == 0)
            def _():
                for n_ in range(len(pidx)):
                    o[len(ridx) + n_][...] = jnp.zeros_like(o[len(ridx) + n_])
            for n_ in range(len(pidx)):
                o[len(ridx) + n_][...] += grads[len(ridx) + n_]

    plain = lambda shape: pl.BlockSpec((tile, shape[1]), lambda i: (i, 0))
    in_specs = ([_row_spec(x, tile) for x in rows] + [_full_spec(p) for p in params] + [plain(x.shape) for x in cts]
                + [plain(a.shape) for _, a, _ in flat_addends])
    out_specs = [plain(_row_shape(rows[i])) for i in ridx] + [_full_spec(params[i]) for i in pidx]
    out_shape = ([jax.ShapeDtypeStruct(_row_shape(rows[i]), rgrad[i]) for i in ridx]
                 + [jax.ShapeDtypeStruct(params[i].shape, f32) for i in pidx])
    res = pl.pallas_call(body, grid=(length // tile,), in_specs=in_specs, out_specs=out_specs, out_shape=out_shape,
                         compiler_params=_cparams(("arbitrary",)), name=name)(
        *[_row_array(x) for x in rows], *params, *cts, *[a for _, a, _ in flat_addends])
    return list(res[:len(ridx)]), list(res[len(ridx):])


def _rms(x, g):
    return x * lax.rsqrt(jnp.mean(x * x, axis=-1, keepdims=True) + EPS) * g


def _rms_twice(x, g):
    y = _rms(x, g)
    return y, y


def _silu(x):
    return x * jax.nn.sigmoid(x)


def _shift_down(x, s):
    rows = lax.broadcasted_iota(jnp.int32, x.shape, 0)
    return jnp.where(rows >= s, pltpu.roll(x, s, axis=0), 0.0)


def _shift_up(x, s):
    n = x.shape[0]
    rows = lax.broadcasted_iota(jnp.int32, x.shape, 0)
    return jnp.where(rows < n - s, pltpu.roll(x, n - s, axis=0), 0.0)


_CONV_COLS = 128


def _conv_gate_fwd(u, cw, cb, *, name):
    length, two_f = u.shape
    ff = two_f // 2
    nb = ff // _CONV_COLS

    def body(ug, uv, wg, wv, bg, bv, o):
        def conv(x_ref, w_ref, b_ref):
            x = x_ref[...].astype(f32)
            return (w_ref[2:3, :] * x + w_ref[1:2, :] * _shift_down(x, 1) + w_ref[0:1, :] * _shift_down(x, 2)
                    + b_ref[...])
        o[...] = (_silu(conv(ug, wg, bg)) * conv(uv, wv, bv)).astype(o.dtype)

    blk = lambda r, off: pl.BlockSpec((r, _CONV_COLS), lambda j, _o=off: (0, j + _o))
    return pl.pallas_call(
        body, grid=(nb,),
        in_specs=[blk(length, 0), blk(length, nb), blk(CONV_W, 0), blk(CONV_W, nb), blk(1, 0), blk(1, nb)],
        out_specs=blk(length, 0), out_shape=jax.ShapeDtypeStruct((length, ff), bf16),
        compiler_params=_cparams(("parallel",)), name=name)(u, u, cw, cw, cb, cb)


def _conv_gate_bwd(u, cw, cb, da, *, name):
    length, two_f = u.shape
    ff = two_f // 2
    nb = ff // _CONV_COLS

    def body(ug, uv, wg, wv, bg, bv, da_ref, dug, duv, dwg, dwv, dbg, dbv):
        def conv(x, w_ref, b_ref):
            x1, x2 = _shift_down(x, 1), _shift_down(x, 2)
            return w_ref[2:3, :] * x + w_ref[1:2, :] * x1 + w_ref[0:1, :] * x2 + b_ref[...], x1, x2

        xg, xv = ug[...].astype(f32), uv[...].astype(f32)
        g, xg1, xg2 = conv(xg, wg, bg)
        v, xv1, xv2 = conv(xv, wv, bv)
        d = da_ref[...].astype(f32)
        sg = jax.nn.sigmoid(g)
        dg = d * v * (sg * (1.0 + g * (1.0 - sg)))
        dv = d * (g * sg)

        def back(dy, x, x1, x2, w_ref, du_ref, dw_ref, db_ref):
            du_ref[...] = (w_ref[2:3, :] * dy + w_ref[1:2, :] * _shift_up(dy, 1)
                           + w_ref[0:1, :] * _shift_up(dy, 2)).astype(du_ref.dtype)
            dw_ref[2:3, :] = jnp.sum(dy * x, axis=0, keepdims=True)
            dw_ref[1:2, :] = jnp.sum(dy * x1, axis=0, keepdims=True)
            dw_ref[0:1, :] = jnp.sum(dy * x2, axis=0, keepdims=True)
            db_ref[...] = jnp.sum(dy, axis=0, keepdims=True)

        back(dg, xg, xg1, xg2, wg, dug, dwg, dbg)
        back(dv, xv, xv1, xv2, wv, duv, dwv, dbv)

    blk = lambda r, off: pl.BlockSpec((r, _CONV_COLS), lambda j, _o=off: (0, j + _o))
    sds = jax.ShapeDtypeStruct
    dug, duv, dwg, dwv, dbg, dbv = pl.pallas_call(
        body, grid=(nb,),
        in_specs=[blk(length, 0), blk(length, nb), blk(CONV_W, 0), blk(CONV_W, nb), blk(1, 0), blk(1, nb),
                  blk(length, 0)],
        out_specs=[blk(length, 0), blk(length, 0), blk(CONV_W, 0), blk(CONV_W, 0), blk(1, 0), blk(1, 0)],
        out_shape=[sds((length, ff), bf16), sds((length, ff), bf16), sds((CONV_W, ff), f32), sds((CONV_W, ff), f32),
                   sds((1, ff), f32), sds((1, ff), f32)],
        compiler_params=_cparams(("parallel",)), name=name)(u, u, cw, cw, cb, cb, da)
    return dug, duv, jnp.concatenate([dwg, dwv], axis=1), jnp.concatenate([dbg, dbv], axis=1)


def _ffn_fwd(h, w, tag):
    hf, = _rows(_rms, [h], [w["norm_ffn"]], [(h.shape[1], bf16)], name=f"ffn_norm_{tag}")
    u = _mm(hf, w["ffn_w_up"], out_dtype=bf16, name=f"ffn_up_{tag}")
    a = _conv_gate_fwd(u, w["ffn_conv_w"], w["ffn_conv_b"], name=f"ffn_conv_{tag}")
    out = _mm(a, w["ffn_w_down"], add=h, name=f"ffn_down_{tag}")
    return out, (h, hf, u, a)


def _ffn_bwd(dout, w, saved, tag):
    h, hf, u, a = saved
    ff = a.shape[1]
    da = _mm(dout, w["ffn_w_down"], tb=True, out_dtype=bf16, name=f"ffn_da_{tag}")
    g = {"ffn_w_down": _mm(a, dout, ta=True, name=f"ffn_dwdown_{tag}")}
    dug, duv, g["ffn_conv_w"], g["ffn_conv_b"] = _conv_gate_bwd(u, w["ffn_conv_w"], w["ffn_conv_b"], da,
                                                                name=f"ffn_dconv_{tag}")
    dhf = _mm(dug, w["ffn_w_up"], tb=True, kslab=(0, 2), name=f"ffn_dhf_g_{tag}")
    dhf = _mm(duv, w["ffn_w_up"], tb=True, kslab=(1, 2), add=dhf, out_dtype=bf16, name=f"ffn_dhf_v_{tag}")
    g["ffn_w_up"] = _mm(hf, dug, ta=True, b2=duv, name=f"ffn_dwup_{tag}")
    (dh,), (g["norm_ffn"],) = _rows_bwd(_rms, [h], [w["norm_ffn"]], [dhf], rgrad=[f32], pgrad=[True],
                                        addends={0: dout}, name=f"ffn_dnorm_{tag}")
    return dh, g


def _xattn_fn(qx, kx, vx, qg, kg):
    outs = []
    for hh in range(XA_HEADS):
        sl = slice(hh * XA_DIM, (hh + 1) * XA_DIM)
        q = _rms(qx[:, sl], qg).astype(bf16)
        k = _rms(kx[:, sl], kg).astype(bf16)
        s = lax.dot_general(q, k, _NT, preferred_element_type=f32) * (XA_DIM ** -0.5)
        s = s - jnp.max(s, axis=-1, keepdims=True)
        p = jnp.exp(s)
        p = p / jnp.sum(p, axis=-1, keepdims=True)
        outs.append(jnp.dot(p.astype(bf16), vx[:, sl].astype(bf16), preferred_element_type=f32))
    return jnp.concatenate(outs, axis=-1)


def _xattn_fwd(h, mem, w, tag):
    d = h.shape[1]
    hx, = _rows(_rms, [h], [w["norm_xa"]], [(d, bf16)], name=f"xa_norm_{tag}")
    qx = _mm(hx, w["xa_wq"], name=f"xa_q_{tag}")
    m, = _rows(_rms, [mem], [w["norm_mem"]], [(d, bf16)], name=f"xa_mnorm_{tag}")
    kx = _mm(m, w["xa_wk"], name=f"xa_k_{tag}")
    vx = _mm(m, w["xa_wv"], name=f"xa_v_{tag}")
    o, = _rows(_xattn_fn, [qx], [kx, vx, w["xa_q_norm"], w["xa_k_norm"]], [(d, bf16)], tile=512,
               name=f"xa_attn_{tag}")
    out = _mm(o, w["xa_wo"], add=h, name=f"xa_o_{tag}")
    return out, (h, hx, qx, m, kx, vx, o)


def _xattn_bwd(dout, mem, w, saved, tag):
    h, hx, qx, m, kx, vx, o = saved
    g = {}
    do = _mm(dout, w["xa_wo"], tb=True, out_dtype=bf16, name=f"xa_do_{tag}")
    g["xa_wo"] = _mm(o, dout, ta=True, name=f"xa_dwo_{tag}")
    (dqx,), (dkx, dvx, g["xa_q_norm"], g["xa_k_norm"]) = _rows_bwd(
        _xattn_fn, [qx], [kx, vx, w["xa_q_norm"], w["xa_k_norm"]], [do], rgrad=[bf16], pgrad=[True] * 4,
        tile=512, name=f"xa_dattn_{tag}")
    dhx = _mm(dqx, w["xa_wq"], tb=True, out_dtype=bf16, name=f"xa_dhx_{tag}")
    g["xa_wq"] = _mm(hx, dqx, ta=True, name=f"xa_dwq_{tag}")
    (dh,), (g["norm_xa"],) = _rows_bwd(_rms, [h], [w["norm_xa"]], [dhx], rgrad=[f32], pgrad=[True],
                                       addends={0: dout}, name=f"xa_dnorm_{tag}")
    dm = _mm(dkx, w["xa_wk"], tb=True, name=f"xa_dm_k_{tag}")
    dm = _mm(dvx, w["xa_wv"], tb=True, add=dm, name=f"xa_dm_v_{tag}")
    g["xa_wk"] = _mm(m, dkx, ta=True, name=f"xa_dwk_{tag}")
    g["xa_wv"] = _mm(m, dvx, ta=True, name=f"xa_dwv_{tag}")
    _, (g["norm_mem"],) = _rows_bwd(_rms, [mem], [w["norm_mem"]], [dm], rgrad=[None], pgrad=[True],
                                    name=f"xa_dmnorm_{tag}")
    return dh, g


_HG_GROUP = 4


def _hg_chunk(q, k, v, g, *sts):
    c = q.shape[0]
    heads = [slice(h * HG_DIM, (h + 1) * HG_DIM) for h in range(len(sts))]
    tri = (lax.broadcasted_iota(jnp.int32, (c, c), 0) >= lax.broadcasted_iota(jnp.int32, (c, c), 1)).astype(f32)
    b = jnp.dot(tri, g, precision=lax.Precision.HIGHEST, preferred_element_type=f32)
    bend = jnp.sum(g, axis=0, keepdims=True)
    qe = (q * jnp.exp(b)).astype(bf16)
    kd = (k * jnp.exp(bend - b)).astype(bf16)
    vb = v.astype(bf16)
    decay = jnp.exp(bend)
    o_inter = [lax.dot_general(qe[:, hs], st.astype(bf16), _NT, preferred_element_type=f32) for hs, st in zip(heads, sts)]
    new = [st * decay[:, hs] + lax.dot_general(vb[:, hs], kd[:, hs], _TN, preferred_element_type=f32)
           for hs, st in zip(heads, sts)]
    outs = []
    for i in range(c // HG_SUB):
        lo, n = HG_SUB * i, HG_SUB * (i + 1)
        ref = jnp.sum(g[:lo], axis=0, keepdims=True) if i else jnp.zeros((1, g.shape[1]), f32)
        qh = (q[lo:n] * jnp.exp(b[lo:n] - ref)).astype(bf16)
        kh = (k[:n] * jnp.exp(ref - b[:n])).astype(bf16)
        keep = (lax.broadcasted_iota(jnp.int32, (HG_SUB, n), 1)
                <= lo + lax.broadcasted_iota(jnp.int32, (HG_SUB, n), 0))
        scores = [lax.dot_general(qh[:, hs], kh[:, hs], _NT, preferred_element_type=f32) for hs in heads]
        scores = [jnp.where(keep, a, 0.0).astype(bf16) for a in scores]
        outs.append(jnp.concatenate([jnp.dot(a, vb[:n, hs], preferred_element_type=f32)
                                     for a, hs in zip(scores, heads)], axis=1))
    return (jnp.concatenate(outs, axis=0) + jnp.concatenate(o_inter, axis=1), *new)


def _hg_fwd(q, k, v, g, *, name):
    length = q.shape[0]
    rows = _HG_GROUP * HG_CHUNK
    ng = length // rows
    nc = length // HG_CHUNK

    def body(q_ref, k_ref, v_ref, g_ref, o_ref, st_ref, state):
        @pl.when(pl.program_id(0) == 0)
        def _():
            state[...] = jnp.zeros_like(state)

        states = [state[h] for h in range(HG_HEADS)]
        for ci in range(_HG_GROUP):
            sl = slice(ci * HG_CHUNK, (ci + 1) * HG_CHUNK)
            for h in range(HG_HEADS):
                st_ref[h, ci] = states[h]
            o, *states = _hg_chunk(q_ref[sl, :], k_ref[sl, :], v_ref[sl, :], g_ref[sl, :], *states)
            o_ref[sl, :] = o
        for h in range(HG_HEADS):
            state[h] = states[h]

    blk = pl.BlockSpec((rows, HG_WIDTH), lambda c: (c, 0))
    return pl.pallas_call(
        body, grid=(ng,), in_specs=[blk] * 4,
        out_specs=[blk, pl.BlockSpec((HG_HEADS, _HG_GROUP, HG_DIM, HG_DIM), lambda c: (0, c, 0, 0))],
        out_shape=[jax.ShapeDtypeStruct((length, HG_WIDTH), f32),
                   jax.ShapeDtypeStruct((HG_HEADS, nc, HG_DIM, HG_DIM), f32)],
        scratch_shapes=[pltpu.VMEM((HG_HEADS, HG_DIM, HG_DIM), f32)],
        compiler_params=_cparams(("arbitrary",)), name=name)(q, k, v, g)


def _hg_bwd(q, k, v, g, states, do, *, name):
    length = q.shape[0]
    rows = _HG_GROUP * HG_CHUNK
    ng = length // rows

    def body(q_ref, k_ref, v_ref, g_ref, st_ref, do_ref, dq_ref, dk_ref, dv_ref, dg_ref, dstate):
        @pl.when(pl.program_id(0) == 0)
        def _():
            dstate[...] = jnp.zeros_like(dstate)

        dstates = [dstate[h] for h in range(HG_HEADS)]
        for ci in reversed(range(_HG_GROUP)):
            sl = slice(ci * HG_CHUNK, (ci + 1) * HG_CHUNK)
            _, vjp = jax.vjp(_hg_chunk, q_ref[sl, :], k_ref[sl, :], v_ref[sl, :], g_ref[sl, :],
                             *[st_ref[h, ci] for h in range(HG_HEADS)])
            dq, dk, dv, dg, *dstates = vjp((do_ref[sl, :], *dstates))
            dq_ref[sl, :] = dq
            dk_ref[sl, :] = dk
            dv_ref[sl, :] = dv
            dg_ref[sl, :] = dg
        for h in range(HG_HEADS):
            dstate[h] = dstates[h]

    blk = pl.BlockSpec((rows, HG_WIDTH), lambda c: (ng - 1 - c, 0))
    sds = jax.ShapeDtypeStruct((length, HG_WIDTH), f32)
    return pl.pallas_call(
        body, grid=(ng,),
        in_specs=[blk] * 4 + [pl.BlockSpec((HG_HEADS, _HG_GROUP, HG_DIM, HG_DIM), lambda c: (0, ng - 1 - c, 0, 0)), blk],
        out_specs=[blk] * 4, out_shape=[sds] * 4,
        scratch_shapes=[pltpu.VMEM((HG_HEADS, HG_DIM, HG_DIM), f32)],
        compiler_params=_cparams(("arbitrary",)), name=name)(q, k, v, g, states, do)


_ATT_BLK = 512
_ATT_SCALE = MLA_QK ** -0.5
_NEG = -1e30


def _att_mask(i, j, t):
    rows = i * t + lax.broadcasted_iota(jnp.int32, (t, t), 0)
    cols = j * t + lax.broadcasted_iota(jnp.int32, (t, t), 1)
    return cols <= rows


def _att_fwd(q, k, v, *, name):
    length = q.shape[0]
    t = min(_ATT_BLK, length)
    nq = length // t
    qw, vw = MLA_QK_PAD, MLA_V
    heads = range(MLA_HEADS)

    def body(q_ref, k_ref, v_ref, o_ref, lse_ref):
        i = pl.program_id(0)
        qbs = [q_ref[:, h * qw:(h + 1) * qw] for h in heads]

        def step(j, carry, diagonal=False):
            off = pl.multiple_of(j * t, t)
            out = []
            for h in heads:
                m, l, acc = carry[h]
                ks = k_ref[pl.ds(off, t), h * qw:(h + 1) * qw]
                vs = v_ref[pl.ds(off, t), h * vw:(h + 1) * vw]
                s = lax.dot_general(qbs[h], ks, _NT, preferred_element_type=f32) * _ATT_SCALE
                if diagonal:
                    s = jnp.where(_att_mask(i, j, t), s, _NEG)
                m_new = jnp.maximum(m, jnp.max(s, axis=-1, keepdims=True))
                alpha = jnp.exp(m - m_new)
                p = jnp.exp(s - m_new)
                l = alpha * l + jnp.sum(p, axis=-1, keepdims=True)
                acc = alpha * acc + jnp.dot(p.astype(bf16), vs, preferred_element_type=f32)
                out.append((m_new, l, acc))
            return tuple(out)

        init = tuple((jnp.full((t, 1), _NEG, f32), jnp.zeros((t, 1), f32), jnp.zeros((t, vw), f32)) for _ in heads)
        res = step(i, lax.fori_loop(0, i, step, init), diagonal=True)
        for h in heads:
            m, l, acc = res[h]
            o_ref[:, h * vw:(h + 1) * vw] = (acc / l).astype(o_ref.dtype)
            lse_ref[:, h * vw:(h + 1) * vw] = jnp.broadcast_to(m + jnp.log(l), (t, vw))

    return pl.pallas_call(
        body, grid=(nq,),
        in_specs=[pl.BlockSpec((t, q.shape[1]), lambda i: (i, 0)), pl.BlockSpec(k.shape, lambda i: (0, 0)),
                  pl.BlockSpec(v.shape, lambda i: (0, 0))],
        out_specs=[pl.BlockSpec((t, v.shape[1]), lambda i: (i, 0))] * 2,
        out_shape=[jax.ShapeDtypeStruct(v.shape, bf16), jax.ShapeDtypeStruct(v.shape, f32)],
        compiler_params=_cparams(("arbitrary",)), name=name)(q, k, v)


def _att_bwd(q, k, v, o, lse, do, *, name):
    length = q.shape[0]
    t = min(_ATT_BLK, length)
    nq = length // t
    qw, vw = MLA_QK_PAD, MLA_V
    heads = range(MLA_HEADS)

    def dq_body(q_ref, k_ref, v_ref, o_ref, lse_ref, do_ref, dq_ref, delta_ref):
        i = pl.program_id(0)
        qbs = [q_ref[:, h * qw:(h + 1) * qw] for h in heads]
        dobs = [do_ref[:, h * vw:(h + 1) * vw] for h in heads]
        lses = [lse_ref[:, h * vw:h * vw + 1] for h in heads]
        deltas = [jnp.sum(dobs[h].astype(f32) * o_ref[:, h * vw:(h + 1) * vw].astype(f32), axis=-1, keepdims=True)
                  for h in heads]

        def step(j, dqs, diagonal=False):
            off = pl.multiple_of(j * t, t)
            out = []
            for h in heads:
                ks = k_ref[pl.ds(off, t), h * qw:(h + 1) * qw]
                vs = v_ref[pl.ds(off, t), h * vw:(h + 1) * vw]
                s = lax.dot_general(qbs[h], ks, _NT, preferred_element_type=f32) * _ATT_SCALE
                p = jnp.exp(s - lses[h])
                if diagonal:
                    p = jnp.where(_att_mask(i, j, t), p, 0.0)
                dp = lax.dot_general(dobs[h], vs, _NT, preferred_element_type=f32)
                ds = p * (dp - deltas[h]) * _ATT_SCALE
                out.append(dqs[h] + jnp.dot(ds.astype(bf16), ks, preferred_element_type=f32))
            return tuple(out)

        dqs = step(i, lax.fori_loop(0, i, step, tuple(jnp.zeros((t, qw), f32) for _ in heads)), diagonal=True)
        for h in heads:
            dq_ref[:, h * qw:(h + 1) * qw] = dqs[h].astype(dq_ref.dtype)
            delta_ref[:, h * vw:(h + 1) * vw] = jnp.broadcast_to(deltas[h], (t, vw))

    qblk = pl.BlockSpec((t, q.shape[1]), lambda i: (i, 0))
    vblk = pl.BlockSpec((t, v.shape[1]), lambda i: (i, 0))
    qfull = pl.BlockSpec(q.shape, lambda i: (0, 0))
    vfull = pl.BlockSpec(v.shape, lambda i: (0, 0))
    dq, delta = pl.pallas_call(
        dq_body, grid=(nq,), in_specs=[qblk, qfull, vfull, vblk, vblk, vblk], out_specs=[qblk, vblk],
        out_shape=[jax.ShapeDtypeStruct(q.shape, bf16), jax.ShapeDtypeStruct(lse.shape, f32)],
        compiler_params=_cparams(("arbitrary",)), name=name + "_dq")(q, k, v, o, lse, do)

    def dkv_body(k_ref, v_ref, q_ref, do_ref, lse_ref, delta_ref, dk_ref, dv_ref):
        j = pl.program_id(0)
        kbs = [k_ref[:, h * qw:(h + 1) * qw] for h in heads]
        vbs = [v_ref[:, h * vw:(h + 1) * vw] for h in heads]

        def step(i, carry, diagonal=False):
            off = pl.multiple_of(i * t, t)
            out = []
            for h in heads:
                dk, dv = carry[h]
                qs = q_ref[pl.ds(off, t), h * qw:(h + 1) * qw]
                dos = do_ref[pl.ds(off, t), h * vw:(h + 1) * vw]
                lse_i = lse_ref[pl.ds(off, t), h * vw:h * vw + 1]
                delta_i = delta_ref[pl.ds(off, t), h * vw:h * vw + 1]
                s = lax.dot_general(qs, kbs[h], _NT, preferred_element_type=f32) * _ATT_SCALE
                p = jnp.exp(s - lse_i)
                if diagonal:
                    p = jnp.where(_att_mask(i, j, t), p, 0.0)
                dv = dv + lax.dot_general(p.astype(bf16), dos, _TN, preferred_element_type=f32)
                dp = lax.dot_general(dos, vbs[h], _NT, preferred_element_type=f32)
                ds = p * (dp - delta_i) * _ATT_SCALE
                dk = dk + lax.dot_general(ds.astype(bf16), qs, _TN, preferred_element_type=f32)
                out.append((dk, dv))
            return tuple(out)

        first = step(j, tuple((jnp.zeros((t, qw), f32), jnp.zeros((t, vw), f32)) for _ in heads), diagonal=True)
        res = lax.fori_loop(j + 1, nq, step, first)
        for h in heads:
            dk_ref[:, h * qw:(h + 1) * qw] = res[h][0].astype(dk_ref.dtype)
            dv_ref[:, h * vw:(h + 1) * vw] = res[h][1].astype(dv_ref.dtype)

    dk, dv = pl.pallas_call(
        dkv_body, grid=(nq,), in_specs=[qblk, vblk, qfull, vfull, vfull, vfull], out_specs=[qblk, vblk],
        out_shape=[jax.ShapeDtypeStruct(k.shape, bf16), jax.ShapeDtypeStruct(v.shape, bf16)],
        compiler_params=_cparams(("arbitrary",)), name=name + "_dkv")(k, v, q, do, lse, delta)
    return dq, dk, dv


_C_Q = 4 * HG_WIDTH
_C_KV = _C_Q + MLA_Q_RANK
_C_KPE = _C_KV + MLA_KV_RANK


def _rms_n(x, g, n):
    return x * lax.rsqrt(jnp.sum(x * x, axis=-1, keepdims=True) * (1.0 / n) + EPS) * g


def _mix_a(proj, l0, l1, q_a_norm, kv_a_norm):
    lb = jax.nn.sigmoid(l0 - l1)
    f = lb + (1.0 - lb) * jax.nn.sigmoid(proj[:, HG_WIDTH:2 * HG_WIDTH])
    qf = _silu(proj[:, :HG_WIDTH])
    v = proj[:, 2 * HG_WIDTH:3 * HG_WIDTH]
    cqn = _rms(proj[:, _C_Q:_C_KV], q_a_norm)
    ckvn = _rms(proj[:, _C_KV:_C_KPE], kv_a_norm)
    return qf, 1.0 - f, v, jnp.log(f), cqn, ckvn


def _mix_b(qraw, kvraw, kpe_raw, cos, sin, qn_nope, qn_rope, kn_nope, kn_rope, perm):
    def rope(x):
        return x * cos + jnp.dot(x, perm, precision=lax.Precision.HIGHEST, preferred_element_type=f32) * sin

    kpe = rope(_rms_n(kpe_raw, kn_rope, MLA_ROPE))
    qs, ks, vs = [], [], []
    for hh in range(MLA_HEADS):
        base = hh * MLA_QK_PAD
        qs.append(_rms(qraw[:, base:base + MLA_NOPE], qn_nope))
        qs.append(rope(_rms_n(qraw[:, base + MLA_NOPE:base + MLA_QK_PAD], qn_rope, MLA_ROPE)))
        ks.append(_rms(kvraw[:, base:base + MLA_NOPE], kn_nope))
        ks.append(kpe)
        vs.append(kvraw[:, base + MLA_NOPE:base + MLA_QK_PAD])
    return jnp.concatenate(qs, axis=-1), jnp.concatenate(ks, axis=-1), jnp.concatenate(vs, axis=-1)


def _mix_c(o_hg, gate, o_mla, hg_out_norm):
    parts = []
    for hh in range(HG_HEADS):
        sl = slice(hh * HG_DIM, (hh + 1) * HG_DIM)
        parts.append(_rms(o_hg[:, sl], hg_out_norm[:, sl]))
    o = jnp.concatenate(parts, axis=-1) * _silu(gate)
    return jnp.concatenate([o, o_mla], axis=-1)


def _rope_perm():
    p = np.zeros((128, 128), np.float32)
    half = MLA_ROPE // 2
    for i in range(half):
        p[i + half, i] = -1.0
        p[i, i + half] = 1.0
    return jnp.asarray(p)


def _mixer_fwd(h, cos, sin, w, tag):
    d = h.shape[1]
    hn, = _rows(_rms, [h], [w["norm_mix"]], [(d, bf16)], name=f"mix_norm_{tag}")
    proj = _mm(hn, w["mix_w_in"], name=f"mix_in_{tag}")
    pa = [w["lb0"], w["lb1"], w["mla_q_a_norm"], w["mla_kv_a_norm"]]
    qf, kk, vv, logf, cqn, ckvn = _rows(
        _mix_a, [proj], pa, [(HG_WIDTH, f32)] * 4 + [(MLA_Q_RANK, bf16), (MLA_KV_RANK, bf16)], name=f"mix_a_{tag}")
    o_hg, states = _hg_fwd(qf, kk, vv, logf, name=f"hg_fwd_{tag}")
    qraw = _mm(cqn, w["mla_w_uq"], name=f"mla_uq_{tag}")
    kvraw = _mm(ckvn, w["mla_w_ukv"], name=f"mla_ukv_{tag}")
    pb = [w["mla_qn_nope"], w["mla_qn_rope"], w["mla_kn_nope"], w["mla_kn_rope"], w["rope_perm"]]
    kpe_raw, gate = _window(proj, _C_KPE, IN_PAD - _C_KPE), _window(proj, 3 * HG_WIDTH, HG_WIDTH)
    qfull, kfull, vfull = _rows(_mix_b, [qraw, kvraw, kpe_raw, cos, sin], pb,
                                [(MLA_HEADS * MLA_QK_PAD, bf16)] * 2 + [(MLA_HEADS * MLA_V, bf16)],
                                name=f"mix_b_{tag}")
    o_mla, lse = _att_fwd(qfull, kfull, vfull, name=f"att_fwd_{tag}")
    mixin, = _rows(_mix_c, [o_hg, gate, o_mla], [w["hg_out_norm"]], [(d, bf16)], name=f"mix_c_{tag}")
    if callable(w["mix_w_out"]):
        w["mix_w_out"] = w["mix_w_out"](mixin)
    out = _mm(mixin, w["mix_w_out"], add=h, name=f"mix_out_{tag}")
    return out, (h, hn, proj, qf, kk, vv, logf, cqn, ckvn, o_hg, states, qraw, kvraw, qfull, kfull, vfull, o_mla,
                 lse, mixin)


def _mixer_bwd(dout, cos, sin, w, saved, tag, on_w_out=None):
    (h, hn, proj, qf, kk, vv, logf, cqn, ckvn, o_hg, states, qraw, kvraw, qfull, kfull, vfull, o_mla, lse,
     mixin) = saved
    g = {}
    dmixin = _mm(dout, w["mix_w_out"], tb=True, name=f"mix_dmixin_{tag}")
    g["mix_w_out"] = _mm(mixin, dout, ta=True, name=f"mix_dwout_{tag}")
    if on_w_out is not None:
        dmixin = on_w_out(g["mix_w_out"], dmixin)
    kpe_raw, gate = _window(proj, _C_KPE, IN_PAD - _C_KPE), _window(proj, 3 * HG_WIDTH, HG_WIDTH)
    (do_hg, dgate, do_mla), (g["hg_out_norm"],) = _rows_bwd(
        _mix_c, [o_hg, gate, o_mla], [w["hg_out_norm"]], [dmixin], rgrad=[f32, f32, bf16], pgrad=[True],
        name=f"mix_dc_{tag}")
    dqfull, dkfull, dvfull = _att_bwd(qfull, kfull, vfull, o_mla, lse, do_mla, name=f"att_bwd_{tag}")
    pb = [w["mla_qn_nope"], w["mla_qn_rope"], w["mla_kn_nope"], w["mla_kn_rope"], w["rope_perm"]]
    (dqraw, dkvraw, dkpe_raw), pg = _rows_bwd(
        _mix_b, [qraw, kvraw, kpe_raw, cos, sin], pb, [dqfull, dkfull, dvfull],
        rgrad=[bf16, bf16, f32, None, None], pgrad=[True, True, True, True, False],
        name=f"mix_db_{tag}")
    g["mla_qn_nope"], g["mla_qn_rope"], g["mla_kn_nope"], g["mla_kn_rope"] = pg
    dcqn = _mm(dqraw, w["mla_w_uq"], tb=True, name=f"mla_dcq_{tag}")
    g["mla_w_uq"] = _mm(cqn, dqraw, ta=True, name=f"mla_dwuq_{tag}")
    dckvn = _mm(dkvraw, w["mla_w_ukv"], tb=True, name=f"mla_dckv_{tag}")
    g["mla_w_ukv"] = _mm(ckvn, dkvraw, ta=True, name=f"mla_dwukv_{tag}")
    dqf, dkk, dvv, dlogf = _hg_bwd(qf, kk, vv, logf, states, do_hg, name=f"hg_bwd_{tag}")
    pa = [w["lb0"], w["lb1"], w["mla_q_a_norm"], w["mla_kv_a_norm"]]
    (dproj,), (g["lb0"], g["lb1"], g["mla_q_a_norm"], g["mla_kv_a_norm"]) = _rows_bwd(
        _mix_a, [proj], pa, [dqf, dkk, dvv, dlogf, dcqn, dckvn], rgrad=[bf16], pgrad=[True] * 4,
        addends={0: [(dgate, 3 * HG_WIDTH), (dkpe_raw, _C_KPE)]}, name=f"mix_da_{tag}")
    dhn = _mm(dproj, w["mix_w_in"], tb=True, out_dtype=bf16, name=f"mix_dhn_{tag}")
    g["mix_w_in_t"] = _mm(dproj, hn, ta=True, name=f"mix_dwin_{tag}")
    (dh,), (g["norm_mix"],) = _rows_bwd(_rms, [h], [w["norm_mix"]], [dhn], rgrad=[f32], pgrad=[True],
                                        addends={0: dout}, name=f"mix_dnorm_{tag}")
    return dh, g


def _rope_tables(positions):
    inv_freq = 1.0 / (ROPE_BASE ** (jnp.arange(0, MLA_ROPE, 2, dtype=f32) / MLA_ROPE))
    ang = positions.astype(f32)[:, None] * inv_freq
    z = jnp.zeros((positions.shape[0], 128 - MLA_ROPE), f32)
    return (jnp.concatenate([jnp.cos(ang), jnp.cos(ang), z], axis=1),
            jnp.concatenate([jnp.sin(ang), jnp.sin(ang), z], axis=1))


def _pad_cols(a, n):
    return jnp.pad(a, ((0, 0), (0, n - a.shape[1])))


def _even_weights(p, j, layer, dt):
    w_uq = p["mla_w_uq"][j].reshape(MLA_Q_RANK, MLA_HEADS, MLA_QK)
    w_uq = jnp.pad(w_uq, ((0, 0), (0, 0), (0, MLA_QK_PAD - MLA_QK))).reshape(MLA_Q_RANK, MLA_HEADS * MLA_QK_PAD)
    return dict(
        norm_mix=p["norm_mix"][layer][None], mix_w_in=_pad_cols(p["mix_w_in"][j], IN_PAD).astype(dt),
        lb0=p["hg_lb_logits"][0][None], lb1=p["hg_lb_logits"][1][None],
        mla_q_a_norm=p["mla_q_a_norm"][j][None], mla_kv_a_norm=p["mla_kv_a_norm"][j][None],
        mla_w_uq=w_uq.astype(dt), mla_w_ukv=p["mla_w_ukv"][j].astype(dt),
        mla_qn_nope=p["mla_qn_nope"][j][None], mla_qn_rope=_pad_cols(p["mla_qn_rope"][j][None], 128),
        mla_kn_nope=p["mla_kn_nope"][j][None], mla_kn_rope=_pad_cols(p["mla_kn_rope"][j][None], 128),
        rope_perm=_rope_perm(), hg_out_norm=p["hg_out_norm"][j][None],
        mix_w_out=p["mix_w_out"][j].astype(dt) if "mix_w_out" in p else None)


def _even_grads(g):
    w_uq = g["mla_w_uq"].reshape(MLA_Q_RANK, MLA_HEADS, MLA_QK_PAD)[:, :, :MLA_QK].reshape(MLA_Q_RANK, -1)
    return dict(
        norm_mix=g["norm_mix"], mix_w_in=g["mix_w_in_t"][:IN_WIDTH].T[None],
        hg_lb_logits=jnp.concatenate([g["lb0"], g["lb1"]], axis=0),
        mla_q_a_norm=g["mla_q_a_norm"], mla_kv_a_norm=g["mla_kv_a_norm"], mla_w_uq=w_uq[None],
        mla_w_ukv=g["mla_w_ukv"][None], mla_qn_nope=g["mla_qn_nope"], mla_qn_rope=g["mla_qn_rope"][:, :MLA_ROPE],
        mla_kn_nope=g["mla_kn_nope"], mla_kn_rope=g["mla_kn_rope"][:, :MLA_ROPE],
        hg_out_norm=g["hg_out_norm"], mix_w_out=g["mix_w_out"][None])


_S5_NB = 8
_S5_BW = 1024
_S5_HALF = 512
_S5_UC = 128
_S5_TIME = 512


def _cmul(ar, ai, br, bi):
    return ar * br - ai * bi, ar * bi + ai * br


def _pow_table(ar, ai, descending):
    rows = lax.broadcasted_iota(jnp.int32, (8, ar.shape[1]), 0)
    tr = jnp.zeros((8, ar.shape[1]), f32)
    ti = jnp.zeros((8, ar.shape[1]), f32)
    pr, pi_ = ar, ai
    for r in range(8):
        sel = rows == ((7 - r) if descending else r)
        tr = jnp.where(sel, pr, tr)
        ti = jnp.where(sel, pi_, ti)
        pr, pi_ = _cmul(pr, pi_, ar, ai)
    return tr, ti


def _s5_tile_scan(work, carry, ar, ai, tc, reverse, per_tile=None):
    hw = _S5_HALF
    row8 = lax.broadcasted_iota(jnp.int32, (8, hw), 0)
    powers = [(ar, ai)]
    for _ in range(2):
        powers.append(_cmul(*powers[-1], *powers[-1]))
    steps = []
    for (mr, mi), s in zip(powers, (1, 2, 4)):
        ok = (row8 < 8 - s) if reverse else (row8 >= s)
        steps.append((jnp.where(ok, mr, 0.0), jnp.where(ok, mi, 0.0), 8 - s if reverse else s))
    tr, ti = _pow_table(ar, ai, reverse)
    cr, ci = carry[:, :hw], carry[:, hw:]
    tiles = range(tc // 8)
    for i in (reversed(tiles) if reverse else tiles):
        sl = slice(8 * i, 8 * i + 8)
        xr, xi = work[sl, :hw], work[sl, hw:]
        for mr, mi, shift in steps:
            pr, pi_ = _cmul(mr, mi, pltpu.roll(xr, shift, axis=0), pltpu.roll(xi, shift, axis=0))
            xr, xi = xr + pr, xi + pi_
        pr, pi_ = _cmul(tr, ti, cr, ci)
        xr, xi = xr + pr, xi + pi_
        work[sl, :hw] = xr
        work[sl, hw:] = xi
        if per_tile is not None:
            per_tile(sl, xr, xi, cr, ci)
        edge = 8 * i if reverse else 8 * i + 7
        cr, ci = work[edge:edge + 1, :hw], work[edge:edge + 1, hw:]
    carry[:, :hw] = cr
    carry[:, hw:] = ci


def _s5_core_fwd(a, hn, b3, c3, *, name):
    length = hn.shape[0]
    tc = min(_S5_TIME, length)

    def body(a_ref, hn_ref, b_ref, c_ref, hs_ref, y_ref, work, carry):
        @pl.when(pl.program_id(1) == 0)
        def _():
            carry[...] = jnp.zeros_like(carry)

        work[...] = jnp.dot(hn_ref[...].astype(bf16), b_ref[...], preferred_element_type=f32)
        _s5_tile_scan(work, carry, a_ref[:, :_S5_HALF], a_ref[:, _S5_HALF:], tc, False)
        hs = work[...].astype(bf16)
        hs_ref[...] = hs
        y_ref[...] = jnp.dot(hs, c_ref[...], preferred_element_type=f32)

    return pl.pallas_call(
        body, grid=(_S5_NB, length // tc),
        in_specs=[pl.BlockSpec((1, _S5_BW), lambda j, t: (0, j)), pl.BlockSpec((tc, _S5_UC), lambda j, t: (t, j)),
                  pl.BlockSpec((_S5_UC, _S5_BW), lambda j, t: (j, 0)), pl.BlockSpec((_S5_BW, _S5_UC), lambda j, t: (j, 0))],
        out_specs=[pl.BlockSpec((tc, _S5_BW), lambda j, t: (t, j)), pl.BlockSpec((tc, _S5_UC), lambda j, t: (t, j))],
        out_shape=[jax.ShapeDtypeStruct((length, _S5_NB * _S5_BW), bf16),
                   jax.ShapeDtypeStruct((length, _S5_NB * _S5_UC), f32)],
        scratch_shapes=[pltpu.VMEM((tc, _S5_BW), f32), pltpu.VMEM((1, _S5_BW), f32)],
        compiler_params=_cparams(("parallel", "arbitrary")), name=name)(a, hn, b3, c3)


def _s5_core_bwd(a, dy, c3, hs, hn, b3, *, name):
    length = hn.shape[0]
    tc = min(_S5_TIME, length)
    nt = length // tc
    hw = _S5_HALF

    def body(a_ref, dy_ref, c_ref, hs_ref, hn_ref, b_ref, du_ref, db_ref, dc_ref, da_ref, work, carry, acc):
        @pl.when(pl.program_id(1) == 0)
        def _():
            carry[...] = jnp.zeros_like(carry)
            db_ref[...] = jnp.zeros_like(db_ref)
            dc_ref[...] = jnp.zeros_like(dc_ref)
            da_ref[...] = jnp.zeros_like(da_ref)

        dyb = dy_ref[...].astype(bf16)
        work[...] = lax.dot_general(dyb, c_ref[...], _NT, preferred_element_type=f32)
        acc[...] = jnp.zeros_like(acc)
        row8 = lax.broadcasted_iota(jnp.int32, (8, hw), 0)

        def grad_a(sl, gr, gi, cr, ci):
            gnr = jnp.where(row8 == 7, cr, pltpu.roll(gr, 7, axis=0))
            gni = jnp.where(row8 == 7, ci, pltpu.roll(gi, 7, axis=0))
            hr, hi = hs_ref[sl, :hw].astype(f32), hs_ref[sl, hw:].astype(f32)
            acc[:, :hw] += hr * gnr + hi * gni
            acc[:, hw:] += hr * gni - hi * gnr

        _s5_tile_scan(work, carry, a_ref[:, :hw], -a_ref[:, hw:], tc, True, grad_a)
        da_ref[...] += jnp.sum(acc[...], axis=0, keepdims=True)
        g = work[...].astype(bf16)
        du_ref[...] = lax.dot_general(g, b_ref[...], _NT, preferred_element_type=f32)
        db_ref[...] += lax.dot_general(hn_ref[...].astype(bf16), g, _TN, preferred_element_type=f32)
        dc_ref[...] += lax.dot_general(hs_ref[...], dyb, _TN, preferred_element_type=f32)

    rev = lambda j, t: (nt - 1 - t, j)
    return pl.pallas_call(
        body, grid=(_S5_NB, nt),
        in_specs=[pl.BlockSpec((1, _S5_BW), lambda j, t: (0, j)), pl.BlockSpec((tc, _S5_UC), rev),
                  pl.BlockSpec((_S5_BW, _S5_UC), lambda j, t: (j, 0)), pl.BlockSpec((tc, _S5_BW), rev),
                  pl.BlockSpec((tc, _S5_UC), rev), pl.BlockSpec((_S5_UC, _S5_BW), lambda j, t: (j, 0))],
        out_specs=[pl.BlockSpec((tc, _S5_UC), rev), pl.BlockSpec((_S5_UC, _S5_BW), lambda j, t: (j, 0)),
                   pl.BlockSpec((_S5_BW, _S5_UC), lambda j, t: (j, 0)), pl.BlockSpec((1, _S5_BW), lambda j, t: (0, j))],
        out_shape=[jax.ShapeDtypeStruct((length, _S5_NB * _S5_UC), f32),
                   jax.ShapeDtypeStruct((_S5_NB * _S5_UC, _S5_BW), f32),
                   jax.ShapeDtypeStruct((_S5_NB * _S5_BW, _S5_UC), f32),
                   jax.ShapeDtypeStruct((1, _S5_NB * _S5_BW), f32)],
        scratch_shapes=[pltpu.VMEM((tc, _S5_BW), f32), pltpu.VMEM((1, _S5_BW), f32), pltpu.VMEM((8, _S5_BW), f32)],
        compiler_params=_cparams(("parallel", "arbitrary")), name=name)(a, dy, c3, hs, hn, b3)


def _s5_disc(lr, li, ldt, btr, bti, expand):
    dt = jnp.exp(ldt)
    mag = jnp.exp(lr * dt)
    abr = mag * jnp.cos(li * dt)
    abi = mag * jnp.sin(li * dt)
    den = lr * lr + li * li
    zr = ((abr - 1.0) * lr + abi * li) / den
    zi = (abi * lr - (abr - 1.0) * li) / den
    zr = jnp.dot(zr, expand, precision=lax.Precision.HIGHEST, preferred_element_type=f32)
    zi = jnp.dot(zi, expand, precision=lax.Precision.HIGHEST, preferred_element_type=f32)
    return abr, abi, zr * btr - zi * bti, zr * bti + zi * btr


def _s5_disc_fwd(args, *, name):
    def body(*refs):
        res = _s5_disc(*[r[...] for r in refs[:6]])
        for o, v in zip(refs[6:], res):
            o[...] = v

    sds = jax.ShapeDtypeStruct
    return pl.pallas_call(body, out_shape=[sds(args[0].shape, f32)] * 2 + [sds(args[3].shape, f32)] * 2,
                          name=name)(*args)


def _s5_disc_bwd(args, cts, *, name):
    def body(*refs):
        vals = [r[...] for r in refs[:6]]
        _, vjp = jax.vjp(lambda *d: _s5_disc(*d, vals[5]), *vals[:5])
        grads = vjp(tuple(r[...] for r in refs[6:10]))
        for o, v in zip(refs[10:], grads):
            o[...] = v

    return pl.pallas_call(body, out_shape=[jax.ShapeDtypeStruct(a.shape, f32) for a in args[:5]],
                          name=name)(*args, *cts)


def _gelu_tanh(x):
    return 0.5 * x * (1.0 + jnp.tanh(0.7978845608028654 * (x + 0.044715 * (x * x * x))))


def _s5_post(y, u, d_skip):
    return _gelu_tanh(y + d_skip * u)


def _s5_glu(ga, gb, h):
    return h + ga * jax.nn.sigmoid(gb)


def _s5_expand():
    e = np.zeros((S5_STATE, S5_GROUP * S5_STATE), np.float32)
    for m in range(S5_GROUP):
        e[np.arange(S5_STATE), m * S5_STATE + np.arange(S5_STATE)] = 1.0
    return jnp.asarray(e)


def _s5_pack_b(bbr, bbi):
    eye = jnp.eye(8, dtype=f32)

    def one(bb):
        b5 = bb.reshape(_S5_NB, 8, S5_GROUP, S5_STATE)
        return jnp.einsum("jgmp,gh->jgmhp", b5, eye).reshape(_S5_NB * _S5_UC, _S5_HALF)

    return jnp.concatenate([one(bbr), one(bbi)], axis=1)


def _s5_unpack_b(db3):
    def one(d):
        d5 = d.reshape(_S5_NB, 8, S5_GROUP, 8, S5_STATE)
        return jnp.einsum("jgmgp->jgmp", d5).reshape(S5_GROUPS, S5_GROUP * S5_STATE)

    return one(db3[:, :_S5_HALF]), one(db3[:, _S5_HALF:])


def _s5_pack_c(c_re, c_im):
    eye = jnp.eye(8, dtype=f32)

    def one(c):
        c4 = c.reshape(_S5_NB, 8, S5_GROUP, S5_STATE)
        return jnp.einsum("jgmp,hg->jhpgm", c4, eye).reshape(_S5_NB, _S5_HALF, _S5_UC)

    return jnp.concatenate([one(c_re), -one(c_im)], axis=1).reshape(_S5_NB * _S5_BW, _S5_UC)


def _s5_unpack_c(dc3):
    d = dc3.reshape(_S5_NB, 2, 8, S5_STATE, 8, S5_GROUP)
    dre = jnp.einsum("jgpgm->jgmp", d[:, 0]).reshape(S5_GROUPS, S5_GROUP, S5_STATE)
    dim = -jnp.einsum("jgpgm->jgmp", d[:, 1]).reshape(S5_GROUPS, S5_GROUP, S5_STATE)
    return dre, dim


def _s5_state_row(re, im):
    r = re.reshape(_S5_NB, 1, _S5_HALF)
    i = im.reshape(_S5_NB, 1, _S5_HALF)
    return jnp.concatenate([r, i], axis=2).reshape(1, _S5_NB * _S5_BW)


def _s5_unstate_row(row):
    r = row.reshape(_S5_NB, 2, 8, S5_STATE)
    return r[:, 0].reshape(S5_GROUPS, S5_STATE), r[:, 1].reshape(S5_GROUPS, S5_STATE)


def _s5_fwd(h, w, tag):
    d = h.shape[1]
    hn, = _rows(_rms, [h], [w["norm_mix"]], [(d, f32)], name=f"s5_norm_{tag}")
    disc_in = [w["s5_lam_re"], w["s5_lam_im"], w["s5_log_dt"], w["s5_bt_re"], w["s5_bt_im"], w["s5_expand"]]
    abr, abi, bbr, bbi = _s5_disc_fwd(disc_in, name=f"s5_disc_{tag}")
    a_row = _s5_state_row(abr, abi)
    b3 = _s5_pack_b(bbr, bbi).astype(bf16)
    hs, y = _s5_core_fwd(a_row, hn, b3, w["s5_c3"], name=f"s5_core_{tag}")
    yg, = _rows(_s5_post, [y, hn], [w["s5_d"]], [(d, bf16)], name=f"s5_post_{tag}")
    ga = _mm(yg, w["s5_w_glu_a"], name=f"s5_glu_a_{tag}")
    gb = _mm(yg, w["s5_w_glu_b"], name=f"s5_glu_b_{tag}")
    out, = _rows(_s5_glu, [ga, gb, h], [], [(d, f32)], name=f"s5_glu_{tag}")
    return out, (h, hn, disc_in, a_row, b3, hs, y, yg, ga, gb)


def _s5_bwd(dout, w, saved, tag):
    h, hn, disc_in, a_row, b3, hs, y, yg, ga, gb = saved
    g = {}
    (dga, dgb), _ = _rows_bwd(_s5_glu, [ga, gb, h], [], [dout], rgrad=[bf16, bf16, None], pgrad=[],
                              name=f"s5_dglu_{tag}")
    dyg = _mm(dga, w["s5_w_glu_a"], tb=True, name=f"s5_dyg_a_{tag}")
    dyg = _mm(dgb, w["s5_w_glu_b"], tb=True, add=dyg, name=f"s5_dyg_b_{tag}")
    g["s5_w_glu_a"] = _mm(yg, dga, ta=True, name=f"s5_dwa_{tag}")
    g["s5_w_glu_b"] = _mm(yg, dgb, ta=True, name=f"s5_dwb_{tag}")
    (dy, du_skip), (g["s5_d"],) = _rows_bwd(_s5_post, [y, hn], [w["s5_d"]], [dyg], rgrad=[bf16, f32], pgrad=[True],
                                           name=f"s5_dpost_{tag}")
    du, db3, dc3, da_row = _s5_core_bwd(a_row, dy, w["s5_c3"], hs, hn, b3, name=f"s5_dcore_{tag}")
    dabr, dabi = _s5_unstate_row(da_row)
    dbbr, dbbi = _s5_unpack_b(db3)
    g["s5_lam_re"], g["s5_lam_im"], g["s5_log_dt"], g["s5_bt_re"], g["s5_bt_im"] = _s5_disc_bwd(
        disc_in, [dabr, dabi, dbbr, dbbi], name=f"s5_ddisc_{tag}")
    g["s5_c_re"], g["s5_c_im"] = _s5_unpack_c(dc3)
    (dh,), (g["norm_mix"],) = _rows_bwd(_rms_twice, [h], [w["norm_mix"]], [du, du_skip], rgrad=[f32], pgrad=[True],
                                        addends={0: dout}, name=f"s5_dnorm_{tag}")
    return dh, g


def _odd_weights(p, j, layer, dt):
    tr = lambda b: b.transpose(0, 2, 1).reshape(S5_GROUPS, S5_GROUP * S5_STATE)
    return dict(
        norm_mix=p["norm_mix"][layer][None], s5_lam_re=p["s5_lam_re"][j], s5_lam_im=p["s5_lam_im"][j],
        s5_log_dt=p["s5_log_dt"][j][:, None], s5_bt_re=tr(p["s5_b_re"][j]), s5_bt_im=tr(p["s5_b_im"][j]),
        s5_expand=_s5_expand(), s5_c3=_s5_pack_c(p["s5_c_re"][j], p["s5_c_im"][j]).astype(dt),
        **{n: (p[n][j][None] if n == "s5_d" else p[n][j].astype(dt))
           for n in ("s5_d", "s5_w_glu_a", "s5_w_glu_b") if n in p})


def _odd_grads(g):
    tr = lambda b: b.reshape(S5_GROUPS, S5_GROUP, S5_STATE).transpose(0, 2, 1)[None]
    return dict(
        norm_mix=g["norm_mix"], s5_lam_re=g["s5_lam_re"][None], s5_lam_im=g["s5_lam_im"][None],
        s5_log_dt=g["s5_log_dt"][:, 0][None], s5_b_re=tr(g["s5_bt_re"]), s5_b_im=tr(g["s5_bt_im"]),
        s5_c_re=g["s5_c_re"][None], s5_c_im=g["s5_c_im"][None], s5_d=g["s5_d"],
        s5_w_glu_a=g["s5_w_glu_a"][None], s5_w_glu_b=g["s5_w_glu_b"][None])


def _loss_fn(y, t):
    e = y - t
    part = jnp.sum(jnp.sum(e * e, axis=-1, keepdims=True), axis=0, keepdims=True) * (0.5 / y.shape[1])
    return e * (1.0 / y.shape[1]), part


FF_SHARD = 352
FF_SHARD_PAD = 384


def _pad_groups(a, axis):
    axis %= a.ndim
    zeros = jnp.zeros(a.shape[:axis] + (FF_SHARD_PAD - FF_SHARD,) + a.shape[axis + 1:], a.dtype)
    pieces = []
    for g in range(a.shape[axis] // FF_SHARD):
        pieces += [lax.slice_in_dim(a, g * FF_SHARD, (g + 1) * FF_SHARD, axis=axis), zeros]
    return jnp.concatenate(pieces, axis=axis)


def _unpad_groups(a, axis):
    axis %= a.ndim
    pieces = [lax.slice_in_dim(a, g * FF_SHARD_PAD, g * FF_SHARD_PAD + FF_SHARD, axis=axis)
              for g in range(a.shape[axis] // FF_SHARD_PAD)]
    return pieces[0] if len(pieces) == 1 else jnp.concatenate(pieces, axis=axis)


def _layer_weights(p, layer, dt):
    return dict(
        norm_xa=p["norm_xa"][layer][None], norm_mem=p["norm_mem"][layer][None], norm_ffn=p["norm_ffn"][layer][None],
        xa_wq=p["xa_wq"][layer].astype(dt), xa_wk=p["xa_wk"][layer].astype(dt), xa_wv=p["xa_wv"][layer].astype(dt),
        xa_wo=p["xa_wo"][layer].astype(dt), xa_q_norm=p["xa_q_norm"][layer][None],
        xa_k_norm=p["xa_k_norm"][layer][None], ffn_w_up=_pad_groups(p["ffn_w_up"][layer], 1).astype(dt),
        ffn_conv_w=_pad_groups(p["ffn_conv_w"][layer], 1), ffn_conv_b=_pad_groups(p["ffn_conv_b"][layer][None], 1),
        ffn_w_down=_pad_groups(p["ffn_w_down"][layer], 0).astype(dt))


_PER_LAYER = ("norm_xa", "norm_mem", "norm_ffn", "xa_wq", "xa_wk", "xa_wv", "xa_wo", "xa_q_norm", "xa_k_norm",
              "ffn_w_up", "ffn_conv_w", "ffn_conv_b", "ffn_w_down")
_FFN_PADDED = dict(ffn_w_up=1, ffn_conv_w=1, ffn_conv_b=1, ffn_w_down=0)


def _local_step(x, mem, positions, target, p):
    cos, sin = _rope_tables(positions)
    we = _even_weights(p, 0, 0, bf16)
    wo = _odd_weights(p, 0, 1, bf16)
    wl = [_layer_weights(p, layer, bf16) for layer in range(2)]
    loss, dh, g_even, g_odd, gl = _local_core(x, mem, cos, sin, target, we, wo, wl)
    grads = {}
    for n in _PER_LAYER:
        a, b = gl[0][n], gl[1][n]
        if n in _FFN_PADDED:
            a, b = _unpad_groups(a, _FFN_PADDED[n]), _unpad_groups(b, _FFN_PADDED[n])
        grads[n] = jnp.concatenate([a, b], axis=0) if a.shape[0] == 1 else jnp.stack([a, b])
    ge, go = _even_grads(g_even), _odd_grads(g_odd)
    grads["norm_mix"] = jnp.concatenate([ge.pop("norm_mix"), go.pop("norm_mix")], axis=0)
    grads.update(ge)
    grads.update(go)
    return loss, dh, grads


def _local_core(x, mem, cos, sin, target, we, wo, wl):
    h, s_mix0 = _mixer_fwd(x, cos, sin, we, "l0")
    h, s_xa0 = _xattn_fwd(h, mem, wl[0], "l0")
    h, s_ff0 = _ffn_fwd(h, wl[0], "l0")
    h, s_mix1 = _s5_fwd(h, wo, "l1")
    h, s_xa1 = _xattn_fwd(h, mem, wl[1], "l1")
    h, s_ff1 = _ffn_fwd(h, wl[1], "l1")
    dh, loss = _rows(_loss_fn, [h, target], [], [(h.shape[1], f32)], accs=[(1, 1)], name="loss_head")

    gl = [{}, {}]
    dh, g = _ffn_bwd(dh, wl[1], s_ff1, "l1")
    gl[1].update(g)
    dh, g = _xattn_bwd(dh, mem, wl[1], s_xa1, "l1")
    gl[1].update(g)
    dh, g_odd = _s5_bwd(dh, wo, s_mix1, "l1")
    dh, g = _ffn_bwd(dh, wl[0], s_ff0, "l0")
    gl[0].update(g)
    dh, g = _xattn_bwd(dh, mem, wl[0], s_xa0, "l0")
    gl[0].update(g)
    dh, g_even = _mixer_bwd(dh, cos, sin, we, s_mix0, "l0")
    return loss, dh, g_even, g_odd, gl


_LANES = 1024
_ROW_PAD = 16


_PEER_MASKS = (1, 2, 4, 3, 5, 6, 7)


def _mesh_place():
    x, y, c = lax.axis_index("x"), lax.axis_index("y"), lax.axis_index("c")

    def peer(mask):
        px = 1 - x if mask & 4 else x
        py = 1 - y if mask & 2 else y
        pc = 1 - c if mask & 1 else c
        return (px, py, pc), 4 * px + 2 * py + pc

    return 4 * x + 2 * y + c, peer


class _Exchange:
    def __init__(self, name):
        self.name = name
        self.srcs, self.shapes, self.items, self.where = [], [], [], {}

    def add(self, src, land_shape, src_at, dst_at, key):
        si = next((i for i, s in enumerate(self.srcs) if s is src), None)
        if si is None:
            self.srcs.append(src)
            si = len(self.srcs) - 1
        if key not in self.where:
            self.shapes.append(land_shape)
            self.where[key] = len(self.shapes) - 1
        self.items.append(dict(src=si, dst=self.where[key], src_at=src_at, dst_at=dst_at))

    def _copy(self, k, mask, ins, lands, send_sems, recv_sems, me, peer, arriving):
        it = self.items[k]
        dev, idx = peer(mask)
        s = k * (N_DEV - 1) + _PEER_MASKS.index(mask)
        return pltpu.make_async_remote_copy(
            src_ref=it["src_at"](ins[it["src"]], idx), dst_ref=it["dst_at"](lands[it["dst"]], idx if arriving else me),
            send_sem=send_sems.at[s], recv_sem=recv_sems.at[s], device_id=dev, device_id_type=pl.DeviceIdType.MESH)

    def _own_copy(self, k, ins, lands, own_sems, me):
        it = self.items[k]
        return pltpu.make_async_copy(it["src_at"](ins[it["src"]], me), it["dst_at"](lands[it["dst"]], me), own_sems.at[k])

    def begin(self, own):
        ns, nd, ni = len(self.srcs), len(self.shapes), len(self.items)
        nsem = ni * (N_DEV - 1)
        self.own = own

        nq = 3 if own else 2

        def body(*refs):
            ins, land_refs = refs[:ns], refs[ns:ns + nd]
            sems, token = refs[ns + nd:ns + nd + nq], refs[-1]
            me, peer = _mesh_place()
            for mask in _PEER_MASKS:
                for k in range(ni):
                    self._copy(k, mask, ins, land_refs, sems[0], sems[1], me, peer, False).start()
            if own:
                for k in range(ni):
                    self._own_copy(k, ins, land_refs, sems[2], me).start()
            token[...] = jnp.zeros_like(token)

        hbm = pl.BlockSpec(memory_space=pltpu.HBM)
        sem = pl.BlockSpec(memory_space=pltpu.SEMAPHORE)
        lands = [lax.empty(s.shape, s.dtype) for s in self.shapes]
        sem_shapes = [pltpu.SemaphoreType.DMA((nsem,)), pltpu.SemaphoreType.DMA((nsem,)), pltpu.SemaphoreType.DMA((ni,))]
        res = pl.pallas_call(
            body, in_specs=[hbm] * (ns + nd),
            out_specs=[sem] * nq + [hbm] * nd + [pl.BlockSpec(memory_space=pltpu.VMEM)],
            out_shape=sem_shapes[:nq] + [pltpu.HBM(s.shape, s.dtype) for s in self.shapes]
            + [jax.ShapeDtypeStruct((8, 128), f32)],
            input_output_aliases={ns + j: nq + j for j in range(nd)},
            compiler_params=pltpu.CompilerParams(has_side_effects=pltpu.SideEffectType.DATAFLOW_SIDE_EFFECTING),
            name=self.name + "_start")(*self.srcs, *lands)
        self.token = res[-1]
        return list(res[:nq]), list(res[nq:-1])

    def finish(self, state, after):
        sems, lands = state
        nq = len(sems)
        after = list(after) if isinstance(after, (list, tuple)) else [after]
        ns, nd, ni = len(self.srcs), len(self.shapes), len(self.items)

        def body(*refs):
            ins, land_refs = refs[:ns], refs[ns:ns + nd]
            sem_refs = refs[ns + nd:ns + nd + nq]
            me, peer = _mesh_place()
            for mask in _PEER_MASKS:
                for k in range(ni):
                    cp = self._copy(k, mask, ins, land_refs, sem_refs[0], sem_refs[1], me, peer, True)
                    cp.wait_send()
                    cp.wait_recv()
            if self.own:
                for k in range(ni):
                    self._own_copy(k, ins, land_refs, sem_refs[2], me).wait()

        hbm = pl.BlockSpec(memory_space=pltpu.HBM)
        sem = pl.BlockSpec(memory_space=pltpu.SEMAPHORE)
        res = pl.pallas_call(
            body, in_specs=[hbm] * (ns + nd) + [sem] * nq + [pl.BlockSpec(memory_space=pl.ANY)] * len(after),
            out_specs=[hbm] * nd, out_shape=[pltpu.HBM(s.shape, s.dtype) for s in self.shapes],
            input_output_aliases={ns + j: j for j in range(nd)},
            compiler_params=pltpu.CompilerParams(has_side_effects=pltpu.SideEffectType.DATAFLOW_SIDE_EFFECTING),
            name=self.name + "_wait")(*self.srcs, *lands, *sems, *after)
        return {k: res[i] for k, i in self.where.items()}


def _after(x, *tokens, name):
    def body(*refs):
        del refs

    anyspace = pl.BlockSpec(memory_space=pl.ANY)
    return pl.pallas_call(body, in_specs=[anyspace] * (1 + len(tokens)), out_specs=anyspace,
                          out_shape=jax.ShapeDtypeStruct(x.shape, x.dtype), input_output_aliases={0: 0},
                          name=name)(x, *tokens)


def _rows_of(n):
    return lambda r, i: r.at[pl.ds(pl.multiple_of(i * n, n), n), :]


def _cols_of(n):
    return lambda r, i: r.at[:, pl.ds(pl.multiple_of(i * n, n), n)]


def _whole(r, i):
    return r


def _slot(r, i):
    return r.at[i]


def _at_layer(layer):
    return lambda r, i: r.at[layer]


def _sum_adam(me_index, slots, owns, own_block, w, m, v, *, name):
    layers, rows, cols = w.shape
    tr = _pick(rows, (256, 128, 104, 64, 32, 16, 8))
    bc1 = 1.0 - ADAM_B1 ** ADAM_STEP
    bc2 = 1.0 - ADAM_B2 ** ADAM_STEP
    own_shape, own_map = own_block(tr)
    nl = len(slots)
    assert nl == layers and len(owns) == layers

    def body(me_ref, *refs):
        s_refs, own_refs = refs[:nl], refs[nl:2 * nl]
        w_ref, m_ref, v_ref, g_ref, d_ref, nm_ref, nv_ref = refs[2 * nl:]
        me = me_ref[0]

        def run(s_ref, own_ref):
            mine = own_ref[0] if len(own_shape) == 3 else own_ref[...]
            g = jnp.where(me == 0, mine, s_ref[0])
            for k in range(1, N_DEV):
                g = g + jnp.where(me == k, mine, s_ref[k])
            mm = ADAM_B1 * m_ref[0] + (1.0 - ADAM_B1) * g
            vv = ADAM_B2 * v_ref[0] + (1.0 - ADAM_B2) * (g * g)
            g_ref[0] = g
            nm_ref[0] = mm
            nv_ref[0] = vv
            d_ref[0] = -ADAM_LR * ((mm / bc1) / (jnp.sqrt(vv / bc2) + ADAM_EPS) + ADAM_WD * w_ref[0])

        for layer in range(nl):
            pl.when(pl.program_id(0) == layer)(functools.partial(run, s_refs[layer], own_refs[layer]))

    def of_layer(layer, index_map):
        return lambda lyr, i, me: index_map(jnp.where(lyr == layer, i, 0), me)

    blk = pl.BlockSpec((1, tr, cols), lambda lyr, i, me: (lyr, i, 0))
    sds = jax.ShapeDtypeStruct((layers, rows, cols), f32)
    grid_spec = pltpu.PrefetchScalarGridSpec(
        num_scalar_prefetch=1, grid=(layers, rows // tr),
        in_specs=[pl.BlockSpec((N_DEV, tr, cols), of_layer(layer, lambda i, me: (0, i, 0))) for layer in range(nl)]
        + [pl.BlockSpec(own_shape, of_layer(layer, own_map)) for layer in range(nl)] + [blk, blk, blk],
        out_specs=[blk] * 4)
    return pl.pallas_call(body, grid_spec=grid_spec, out_shape=[sds] * 4,
                          compiler_params=_cparams(("arbitrary", "arbitrary")),
                          name=name)(me_index, *slots, *owns, w, m, v)


_SHARDED = dict(xa_wq=1, xa_wk=1, xa_wv=1, xa_wo=1, ffn_w_up=2, ffn_conv_w=2, ffn_w_down=1, mix_w_in=2, mla_w_uq=2,
                mla_w_ukv=2, mix_w_out=1, s5_d=1, s5_w_glu_a=1, s5_w_glu_b=1)
_EXACT = ("ffn_conv_w", "s5_d")
_WEIGHTS = ("norm_mix", "norm_xa", "norm_mem", "norm_ffn", "xa_wq", "xa_wk", "xa_wv", "xa_wo", "xa_q_norm",
            "xa_k_norm", "ffn_w_up", "ffn_conv_w", "ffn_conv_b", "ffn_w_down", "hg_lb_logits", "mix_w_in",
            "hg_out_norm", "mla_q_a_norm", "mla_w_uq", "mla_kv_a_norm", "mla_w_ukv", "mla_qn_nope", "mla_qn_rope",
            "mla_kn_nope", "mla_kn_rope", "mix_w_out", "s5_lam_re", "s5_lam_im", "s5_log_dt", "s5_b_re", "s5_b_im",
            "s5_c_re", "s5_c_im", "s5_d", "s5_w_glu_a", "s5_w_glu_b")
_BIG = tuple(n for n in _WEIGHTS if n in _SHARDED and n not in _EXACT)
_SHARD_ORDER = tuple(n for n in _WEIGHTS if n in _SHARDED)
_REPL_ORDER = tuple(n for n in _WEIGHTS if n not in _SHARDED)
_REPL_EARLY = tuple(n for n in _REPL_ORDER if n.startswith("s5_"))
_REPL_LATE = tuple(n for n in _REPL_ORDER if n not in _REPL_EARLY)


def _pack(parts, dtype, lead=None):
    nl = 0 if lead is None else 1
    flat = [a.astype(dtype).reshape(a.shape[:nl] + (-1,)) for a in parts]
    cat = jnp.concatenate(flat, axis=nl)
    n = cat.shape[nl]
    unit = _LANES * _ROW_PAD
    total = -(-n // unit) * unit
    cat = jnp.pad(cat, [(0, 0)] * nl + [(0, total - n)])
    return cat.reshape(cat.shape[:nl] + (total // _LANES, _LANES))


def _unpack(packed, shapes, lead=None):
    nl = 0 if lead is None else 1
    flat = packed.reshape(packed.shape[:nl] + (-1,))
    out, off = [], 0
    for s in shapes:
        n = int(np.prod(s))
        piece = flat[..., off:off + n] if nl else flat[off:off + n]
        out.append(piece.reshape(packed.shape[:nl] + tuple(s)))
        off += n
    return out


def _to_full(gathered, axis):
    g = jnp.moveaxis(gathered, 0, axis)
    s = g.shape
    return g.reshape(s[:axis] + (s[axis] * s[axis + 1],) + s[axis + 2:])


def _to_shards(full, axis):
    s = full.shape
    g = full.reshape(s[:axis] + (N_DEV, s[axis] // N_DEV) + s[axis + 1:])
    return jnp.moveaxis(g, axis, 0)


_DIRECT_ROWS = ("xa_wq", "xa_wk", "xa_wv", "xa_wo", "mix_w_out", "s5_w_glu_a", "s5_w_glu_b")
_SMALL16 = ("mix_w_in", "mla_w_uq", "mla_w_ukv")
_SMALL_SHARDED = ("mla_w_uq", "mla_w_ukv") + _EXACT
_SHARD_ROWS = 128


def _exchange_layout(d):
    out = dict(d)
    out["ffn_w_up"] = _pad_groups(d["ffn_w_up"], 2)
    out["ffn_conv_w"] = _pad_groups(d["ffn_conv_w"], 2)
    out["ffn_w_down"] = _pad_groups(d["ffn_w_down"], 1)
    return out


def _train_step(x, mem, positions, target, w, m, v):
    d_model = x.shape[1]
    we_, me_, ve_ = _exchange_layout(w), _exchange_layout(m), _exchange_layout(v)
    sds = jax.ShapeDtypeStruct

    matrices = _DIRECT_ROWS + ("ffn_w_up", "ffn_w_down")
    layer_mats = ("xa_wq", "xa_wk", "xa_wv", "xa_wo", "ffn_w_up", "ffn_w_down")
    shard16 = {n: we_[n].astype(bf16) for n in matrices}
    part_of = {n: _rows_of(_SHARD_ROWS) for n in _DIRECT_ROWS}
    part_of["ffn_w_up"] = _cols_of(we_["ffn_w_up"].shape[2])
    part_of["ffn_w_down"] = _rows_of(we_["ffn_w_down"].shape[1])
    part_shape = {n: we_[n].shape[1:] for n in matrices}

    def full_shape(n):
        r, c = part_shape[n]
        return (r, N_DEV * c) if n == "ffn_w_up" else (N_DEV * r, c)

    def gather(ex, n, layer):
        ex.add(shard16[n], sds(full_shape(n), bf16), _at_layer(layer), part_of[n], (n, layer))

    def scatter(ex, n, layer, grad):
        ex.add(grad, sds((N_DEV,) + part_shape[n], f32), part_of[n], _slot, (n, layer))

    small16 = _pack([we_[n] for n in _SMALL16], bf16)
    exact = _pack([we_[n] for n in _EXACT], f32)
    ga, ga1, gb, gc = _Exchange("gather_a"), _Exchange("gather_a1"), _Exchange("gather_b"), _Exchange("gather_c")
    ga.add(small16, sds((N_DEV,) + small16.shape, bf16), _whole, _slot, "small16")
    ga1.add(exact, sds((N_DEV,) + exact.shape, f32), _whole, _slot, "exact")
    gather(ga1, "mix_w_out", 0)
    for n in layer_mats:
        gather(gb, n, 0)
    gather(gc, "s5_w_glu_a", 0)
    gather(gc, "s5_w_glu_b", 0)
    for n in layer_mats:
        gather(gc, n, 1)
    state_a, state_a1, state_b, state_c = ga.begin(True), ga1.begin(True), gb.begin(True), gc.begin(True)

    p = {n: w[n] for n in _REPL_ORDER}
    cos, sin = _rope_tables(positions)
    wo = _odd_weights(p, 0, 1, bf16)
    conv_b = _pad_groups(w["ffn_conv_b"], 1)
    prepared = [cos, sin, conv_b, wo["s5_c3"], wo["s5_bt_re"], wo["s5_bt_im"]]
    full = ga.finish(state_a, [ga1.token, gb.token, gc.token] + prepared)
    for n, a in zip(_SMALL16, _unpack(full["small16"], [we_[n].shape for n in _SMALL16], lead=True)):
        p[n] = _to_full(a, _SHARDED[n])
    we = _even_weights(p, 0, 0, bf16)
    we["norm_mix"] = _after(we["norm_mix"], ga.token, ga1.token, gb.token, gc.token, name="after_gather_starts")

    def late_mix_w_out(mixin):
        full.update(ga1.finish(state_a1, [mixin]))
        return full[("mix_w_out", 0)]

    we["mix_w_out"] = late_mix_w_out
    h, s_mix0 = _mixer_fwd(x, cos, sin, we, "l0")
    conv_w, s5_d = [_to_full(a, _SHARDED[n]) for n, a in
                    zip(_EXACT, _unpack(full["exact"], [we_[n].shape for n in _EXACT], lead=True))]

    def layer_weights(layer):
        return dict(norm_xa=w["norm_xa"][layer][None], norm_mem=w["norm_mem"][layer][None],
                    norm_ffn=w["norm_ffn"][layer][None], xa_q_norm=w["xa_q_norm"][layer][None],
                    xa_k_norm=w["xa_k_norm"][layer][None], ffn_conv_w=conv_w[layer],
                    ffn_conv_b=conv_b[layer][None], **{n: full[(n, layer)] for n in layer_mats})

    full.update(gb.finish(state_b, h))
    wl = [layer_weights(0)]
    h, s_xa0 = _xattn_fwd(h, mem, wl[0], "l0")
    h, s_ff0 = _ffn_fwd(h, wl[0], "l0")
    full.update(gc.finish(state_c, h))
    wl.append(layer_weights(1))
    wo.update(s5_d=s5_d, s5_w_glu_a=full[("s5_w_glu_a", 0)], s5_w_glu_b=full[("s5_w_glu_b", 0)])
    h, s_mix1 = _s5_fwd(h, wo, "l1")
    h, s_xa1 = _xattn_fwd(h, mem, wl[1], "l1")
    h, s_ff1 = _ffn_fwd(h, wl[1], "l1")
    dh, loss = _rows(_loss_fn, [h, target], [], [(h.shape[1], f32)], accs=[(1, 1)], name="loss_head")

    gl = [{}, {}]
    reduces = []

    own_grad = {}

    def reduce_start(name, entries, dh):
        ex = _Exchange(name)
        for n, layer, grad in entries.get("matrices", ()):
            scatter(ex, n, layer, grad)
            own_grad[(n, layer)] = grad
        for key, src, shape, src_at in entries.get("packs", ()):
            ex.add(src, shape, src_at, _slot, key)
        reduces.append((ex, ex.begin(False)))
        return _after(dh, ex.token, name="after_" + name)

    dh, gl[1] = _ffn_bwd(dh, wl[1], s_ff1, "l1")
    dh = reduce_start("reduce_ffn1", dict(matrices=[(n, 1, gl[1][n]) for n in ("ffn_w_up", "ffn_w_down")]), dh)
    dh, g = _xattn_bwd(dh, mem, wl[1], s_xa1, "l1")
    gl[1].update(g)
    dh = reduce_start("reduce_xa1", dict(matrices=[(n, 1, g[n]) for n in ("xa_wq", "xa_wk", "xa_wv", "xa_wo")]), dh)
    dh, g_odd = _s5_bwd(dh, wo, s_mix1, "l1")
    go = _odd_grads(g_odd)
    dh, gl[0] = _ffn_bwd(dh, wl[0], s_ff0, "l0")
    send_early = _pack([go[n].reshape(w[n].shape) for n in _REPL_EARLY], f32)
    dh = reduce_start("reduce_ffn0", dict(
        matrices=[(n, 0, g_odd[n]) for n in ("s5_w_glu_a", "s5_w_glu_b")]
        + [(n, 0, gl[0][n]) for n in ("ffn_w_up", "ffn_w_down")],
        packs=[("repl_early", send_early, sds((N_DEV,) + send_early.shape, f32), _whole)]), dh)
    dh, g = _xattn_bwd(dh, mem, wl[0], s_xa0, "l0")
    gl[0].update(g)
    dh = reduce_start("reduce_xa0", dict(matrices=[(n, 0, g[n]) for n in ("xa_wq", "xa_wk", "xa_wv", "xa_wo")]), dh)
    grad_x, g_even = _mixer_bwd(
        dh, cos, sin, we, s_mix0, "l0",
        on_w_out=lambda grad, dmixin: reduce_start("reduce_w_out", dict(matrices=[("mix_w_out", 0, grad)]), dmixin))

    ge = _even_grads(g_even)
    cat = lambda n: jnp.concatenate([gl[0][n], gl[1][n]], axis=0)
    rg = dict(ge)
    rg["norm_mix"] = jnp.concatenate([ge["norm_mix"], go["norm_mix"]], axis=0)
    for n in ("norm_xa", "norm_mem", "norm_ffn", "xa_q_norm", "xa_k_norm"):
        rg[n] = cat(n)
    rg["ffn_conv_b"] = _unpad_groups(cat("ffn_conv_b"), 1)
    sg = dict(mla_w_uq=ge["mla_w_uq"], mla_w_ukv=ge["mla_w_ukv"], s5_d=go["s5_d"],
              ffn_conv_w=jnp.stack([gl[0]["ffn_conv_w"], gl[1]["ffn_conv_w"]]))
    send_small = _pack([_to_shards(sg[n], _SHARDED[n]) for n in _SMALL_SHARDED], f32, lead=True)
    send_late = _pack([rg[n].reshape(w[n].shape) for n in _REPL_LATE], f32)
    w_in_rows = w["mix_w_in"].shape[2]
    last = _Exchange("reduce_last")
    last.add(g_even["mix_w_in_t"], sds((N_DEV, w_in_rows, d_model), f32), _rows_of(w_in_rows), _slot, "mix_w_in")
    last.add(send_small, sds(send_small.shape, f32), _slot, _slot, "small")
    last.add(send_late, sds((N_DEV,) + send_late.shape, f32), _whole, _slot, "repl_late")
    state_last = last.begin(False)
    slots = {}
    for ex, state in reduces:
        slots.update(ex.finish(state, [grad_x, last.token]))

    me_index = (4 * lax.axis_index("x") + 2 * lax.axis_index("y") + lax.axis_index("c")).astype(jnp.int32).reshape(1)

    def rows_block(r, c):
        return lambda tr: ((tr, c), lambda i, me: (me[0] * (r // tr) + i, 0))

    def own_block(n):
        r, c = part_shape[n]
        if n == "ffn_w_up":
            return lambda tr: ((tr, c), lambda i, me: (i, me[0]))
        return rows_block(r, c)

    out = [{}, {}, {}, {}]
    unpad = dict(ffn_w_up=2, ffn_conv_w=2, ffn_w_down=1)
    for n in matrices:
        layers = range(we_[n].shape[0])
        res = _sum_adam(me_index, [slots[(n, layer)] for layer in layers], [own_grad[(n, layer)] for layer in layers],
                        own_block(n), we_[n], me_[n], ve_[n], name=f"adam_{n}")
        for k in range(4):
            out[k][n] = _unpad_groups(res[k], unpad[n]) if n in unpad else res[k]
    pk = lambda d, order: _pack([d[n] for n in order], f32)[None]
    whole_rows = lambda tr: ((tr, _LANES), lambda i, me: (i, 0))
    res_early = _sum_adam(me_index, [slots["repl_early"]], [send_early], whole_rows, pk(w, _REPL_EARLY),
                          pk(m, _REPL_EARLY), pk(v, _REPL_EARLY), name="adam_repl_early")
    for k in range(4):
        out[k].update(zip(_REPL_EARLY, _unpack(res_early[k][0], [w[n].shape for n in _REPL_EARLY])))
    done = [out[k][n] for k in range(4) for n in matrices + _REPL_EARLY]
    slots = last.finish(state_last, done)
    transposed = lambda d: jnp.swapaxes(d["mix_w_in"], 1, 2)
    res_w_in = _sum_adam(me_index, [slots["mix_w_in"]], [g_even["mix_w_in_t"]], rows_block(w_in_rows, d_model),
                         transposed(w), transposed(m), transposed(v), name="adam_mix_w_in")
    res_small = _sum_adam(me_index, [slots["small"]], [send_small],
                          lambda tr: ((1, tr, _LANES), lambda i, me: (me[0], i, 0)),
                          pk(we_, _SMALL_SHARDED), pk(me_, _SMALL_SHARDED), pk(ve_, _SMALL_SHARDED), name="adam_small")
    res_late = _sum_adam(me_index, [slots["repl_late"]], [send_late], whole_rows, pk(w, _REPL_LATE), pk(m, _REPL_LATE),
                         pk(v, _REPL_LATE), name="adam_repl_late")
    for k in range(4):
        out[k]["mix_w_in"] = jnp.swapaxes(res_w_in[k], 1, 2)
        for n, a in zip(_SMALL_SHARDED, _unpack(res_small[k][0], [we_[n].shape for n in _SMALL_SHARDED])):
            out[k][n] = _unpad_groups(a, unpad[n]) if n in unpad else a
        out[k].update(zip(_REPL_LATE, _unpack(res_late[k][0], [w[n].shape for n in _REPL_LATE])))
    return loss, grad_x, out


_INPUTS = tuple("""x, mem, positions, norm_mix, norm_xa, norm_mem, norm_ffn, xa_wq, xa_wk, xa_wv, xa_wo, xa_q_norm, xa_k_norm, ffn_w_up, ffn_conv_w, ffn_conv_b, ffn_w_down, hg_lb_logits, mix_w_in, hg_out_norm, mla_q_a_norm, mla_w_uq, mla_kv_a_norm, mla_w_ukv, mla_qn_nope, mla_qn_rope, mla_kn_nope, mla_kn_rope, mix_w_out, s5_lam_re, s5_lam_im, s5_log_dt, s5_b_re, s5_b_im, s5_c_re, s5_c_im, s5_d, s5_w_glu_a, s5_w_glu_b, loss_target, m_norm_mix, m_norm_xa, m_norm_mem, m_norm_ffn, m_xa_wq, m_xa_wk, m_xa_wv, m_xa_wo, m_xa_q_norm, m_xa_k_norm, m_ffn_w_up, m_ffn_conv_w, m_ffn_conv_b, m_ffn_w_down, m_hg_lb_logits, m_mix_w_in, m_hg_out_norm, m_mla_q_a_norm, m_mla_w_uq, m_mla_kv_a_norm, m_mla_w_ukv, m_mla_qn_nope, m_mla_qn_rope, m_mla_kn_nope, m_mla_kn_rope, m_mix_w_out, m_s5_lam_re, m_s5_lam_im, m_s5_log_dt, m_s5_b_re, m_s5_b_im, m_s5_c_re, m_s5_c_im, m_s5_d, m_s5_w_glu_a, m_s5_w_glu_b, v_norm_mix, v_norm_xa, v_norm_mem, v_norm_ffn, v_xa_wq, v_xa_wk, v_xa_wv, v_xa_wo, v_xa_q_norm, v_xa_k_norm, v_ffn_w_up, v_ffn_conv_w, v_ffn_conv_b, v_ffn_w_down, v_hg_lb_logits, v_mix_w_in, v_hg_out_norm, v_mla_q_a_norm, v_mla_w_uq, v_mla_kv_a_norm, v_mla_w_ukv, v_mla_qn_nope, v_mla_qn_rope, v_mla_kn_nope, v_mla_kn_rope, v_mix_w_out, v_s5_lam_re, v_s5_lam_im, v_s5_log_dt, v_s5_b_re, v_s5_b_im, v_s5_c_re, v_s5_c_im, v_s5_d, v_s5_w_glu_a, v_s5_w_glu_b""".replace(" ", "").split(","))


def kernel(x, mem, positions, norm_mix, norm_xa, norm_mem, norm_ffn, xa_wq, xa_wk, xa_wv, xa_wo, xa_q_norm, xa_k_norm, ffn_w_up, ffn_conv_w, ffn_conv_b, ffn_w_down, hg_lb_logits, mix_w_in, hg_out_norm, mla_q_a_norm, mla_w_uq, mla_kv_a_norm, mla_w_ukv, mla_qn_nope, mla_qn_rope, mla_kn_nope, mla_kn_rope, mix_w_out, s5_lam_re, s5_lam_im, s5_log_dt, s5_b_re, s5_b_im, s5_c_re, s5_c_im, s5_d, s5_w_glu_a, s5_w_glu_b, loss_target, m_norm_mix, m_norm_xa, m_norm_mem, m_norm_ffn, m_xa_wq, m_xa_wk, m_xa_wv, m_xa_wo, m_xa_q_norm, m_xa_k_norm, m_ffn_w_up, m_ffn_conv_w, m_ffn_conv_b, m_ffn_w_down, m_hg_lb_logits, m_mix_w_in, m_hg_out_norm, m_mla_q_a_norm, m_mla_w_uq, m_mla_kv_a_norm, m_mla_w_ukv, m_mla_qn_nope, m_mla_qn_rope, m_mla_kn_nope, m_mla_kn_rope, m_mix_w_out, m_s5_lam_re, m_s5_lam_im, m_s5_log_dt, m_s5_b_re, m_s5_b_im, m_s5_c_re, m_s5_c_im, m_s5_d, m_s5_w_glu_a, m_s5_w_glu_b, v_norm_mix, v_norm_xa, v_norm_mem, v_norm_ffn, v_xa_wq, v_xa_wk, v_xa_wv, v_xa_wo, v_xa_q_norm, v_xa_k_norm, v_ffn_w_up, v_ffn_conv_w, v_ffn_conv_b, v_ffn_w_down, v_hg_lb_logits, v_mix_w_in, v_hg_out_norm, v_mla_q_a_norm, v_mla_w_uq, v_mla_kv_a_norm, v_mla_w_ukv, v_mla_qn_nope, v_mla_qn_rope, v_mla_kn_nope, v_mla_kn_rope, v_mix_w_out, v_s5_lam_re, v_s5_lam_im, v_s5_log_dt, v_s5_b_re, v_s5_b_im, v_s5_c_re, v_s5_c_im, v_s5_d, v_s5_w_glu_a, v_s5_w_glu_b):
    vals = dict(zip(_INPUTS, (x, mem, positions, norm_mix, norm_xa, norm_mem, norm_ffn, xa_wq, xa_wk, xa_wv, xa_wo, xa_q_norm, xa_k_norm, ffn_w_up, ffn_conv_w, ffn_conv_b, ffn_w_down, hg_lb_logits, mix_w_in, hg_out_norm, mla_q_a_norm, mla_w_uq, mla_kv_a_norm, mla_w_ukv, mla_qn_nope, mla_qn_rope, mla_kn_nope, mla_kn_rope, mix_w_out, s5_lam_re, s5_lam_im, s5_log_dt, s5_b_re, s5_b_im, s5_c_re, s5_c_im, s5_d, s5_w_glu_a, s5_w_glu_b, loss_target, m_norm_mix, m_norm_xa, m_norm_mem, m_norm_ffn, m_xa_wq, m_xa_wk, m_xa_wv, m_xa_wo, m_xa_q_norm, m_xa_k_norm, m_ffn_w_up, m_ffn_conv_w, m_ffn_conv_b, m_ffn_w_down, m_hg_lb_logits, m_mix_w_in, m_hg_out_norm, m_mla_q_a_norm, m_mla_w_uq, m_mla_kv_a_norm, m_mla_w_ukv, m_mla_qn_nope, m_mla_qn_rope, m_mla_kn_nope, m_mla_kn_rope, m_mix_w_out, m_s5_lam_re, m_s5_lam_im, m_s5_log_dt, m_s5_b_re, m_s5_b_im, m_s5_c_re, m_s5_c_im, m_s5_d, m_s5_w_glu_a, m_s5_w_glu_b, v_norm_mix, v_norm_xa, v_norm_mem, v_norm_ffn, v_xa_wq, v_xa_wk, v_xa_wv, v_xa_wo, v_xa_q_norm, v_xa_k_norm, v_ffn_w_up, v_ffn_conv_w, v_ffn_conv_b, v_ffn_w_down, v_hg_lb_logits, v_mix_w_in, v_hg_out_norm, v_mla_q_a_norm, v_mla_w_uq, v_mla_kv_a_norm, v_mla_w_ukv, v_mla_qn_nope, v_mla_qn_rope, v_mla_kn_nope, v_mla_kn_rope, v_mix_w_out, v_s5_lam_re, v_s5_lam_im, v_s5_log_dt, v_s5_b_re, v_s5_b_im, v_s5_c_re, v_s5_c_im, v_s5_d, v_s5_w_glu_a, v_s5_w_glu_b)))
    w = {n: vals[n] for n in _WEIGHTS}
    m = {n: vals["m_" + n] for n in _WEIGHTS}
    v = {n: vals["v_" + n] for n in _WEIGHTS}
    loss, grad_x, res = _train_step(vals["x"][0], vals["mem"][0], vals["positions"][0], vals["loss_target"][0],
                                    w, m, v)
    loss = lax.psum(loss[0, 0], ("x", "y", "c"))
    return (loss, grad_x[None], *[r[n] for r in res for n in _WEIGHTS])
```

```python
import functools

import jax
import jax.numpy as jnp
import numpy as np
from jax import lax
from jax.experimental import pallas as pl
from jax.experimental.pallas import tpu as pltpu

f32 = jnp.float32
bf16 = jnp.bfloat16

EPS = 1e-6
N_DEV = 8
VMEM_LIMIT = 52 * 1024 * 1024

HG_HEADS = 4
HG_DIM = 128
HG_WIDTH = HG_HEADS * HG_DIM
HG_CHUNK = 64
HG_SUB = 16
MLA_HEADS = 4
MLA_Q_RANK = 256
MLA_KV_RANK = 128
MLA_NOPE = 128
MLA_ROPE = 64
MLA_V = 128
MLA_QK = MLA_NOPE + MLA_ROPE
MLA_QK_PAD = 256
ROPE_BASE = 10000.0
IN_WIDTH = 4 * HG_WIDTH + MLA_Q_RANK + MLA_KV_RANK + MLA_ROPE
IN_PAD = 2560
XA_HEADS = 4
XA_DIM = 256
S5_GROUP = 16
S5_GROUPS = 64
S5_STATE = 64
CONV_W = 3

ADAM_LR = 0.001
ADAM_B1 = 0.9
ADAM_B2 = 0.999
ADAM_EPS = 1e-08
ADAM_WD = 0.01
ADAM_STEP = 10

_NT = (((1,), (1,)), ((), ()))
_TN = (((0,), (0,)), ((), ()))
_NN = (((1,), (0,)), ((), ()))


def _pick(n, cands):
    for c in cands:
        if n % c == 0:
            return c
    return n


def _cparams(sem):
    return pltpu.CompilerParams(dimension_semantics=sem, vmem_limit_bytes=VMEM_LIMIT)


_MM_BUDGET = 36 * 1024 * 1024
_MM_TILES = ((1024, 1024), (1024, 512), (512, 1024), (512, 512), (512, 256), (256, 512), (256, 256), (256, 128),
             (128, 256), (128, 128))


def _mm(a, b, *, name, ta=False, tb=False, out_dtype=f32, add=None, b2=None, kslab=None, norm=None):
    m, k = (a.shape[1], a.shape[0]) if ta else a.shape
    nb = b.shape[0] if tb else b.shape[1]
    n = nb * (2 if b2 is not None else 1)
    slab, nslab = kslab if kslab is not None else (0, 1)
    assert (b.shape[1] // nslab if tb else b.shape[0]) == k, (a.shape, b.shape, ta, tb)
    assert b2 is None or (not tb and b2.shape == b.shape)
    assert norm is None or b2 is None
    isz = lambda x: jnp.dtype(x.dtype).itemsize
    bm = bn = None
    for cm, cn in _MM_TILES:
        if m % cm or nb % cn or (norm is not None and cn != n):
            continue
        need = 2 * (cm * k * isz(a) + cn * k * isz(b) * (2 if b2 is not None else 1)
                    + cm * cn * (jnp.dtype(out_dtype).itemsize + (4 if add is not None else 0)
                                 + (2 if norm is not None else 0)))
        if need <= _MM_BUDGET:
            bm, bn = cm, cn
            break
    assert bm is not None, (name, a.shape, b.shape)
    half = nb // bn
    dims = (((0 if ta else 1,), (1 if tb else 0,)), ((), ()))

    def body(*refs):
        refs = list(refs)
        a_ref, b_ref = refs[0], refs[1]
        b2_ref = refs.pop(2) if b2 is not None else None
        add_ref = refs.pop(2) if add is not None else None
        gain_ref = refs.pop(2) if norm is not None else None
        o_ref = refs[2]
        normed_ref = refs[3] if norm is not None else None

        def run(rhs_ref):
            r = lax.dot_general(a_ref[...].astype(bf16), rhs_ref[...].astype(bf16), dims, preferred_element_type=f32)
            if add_ref is not None:
                r = r + add_ref[...].astype(f32)
            o_ref[...] = r.astype(o_ref.dtype)
            if normed_ref is not None:
                normed_ref[...] = _rms(r, gain_ref[...]).astype(normed_ref.dtype)

        if b2_ref is None:
            run(b_ref)
        else:
            pl.when(pl.program_id(1) < half)(lambda: run(b_ref))
            pl.when(pl.program_id(1) >= half)(lambda: run(b2_ref))

    a_spec = pl.BlockSpec((k, bm), lambda i, j: (0, i)) if ta else pl.BlockSpec((bm, k), lambda i, j: (i, 0))
    if tb:
        b_spec = pl.BlockSpec((bn, k), lambda i, j: (j, slab))
    elif b2 is None:
        b_spec = pl.BlockSpec((k, bn), lambda i, j: (0, j))
    else:
        b_spec = pl.BlockSpec((k, bn), lambda i, j: (0, jnp.minimum(j, half - 1)))
    in_specs = [a_spec, b_spec]
    args = [a, b]
    if b2 is not None:
        in_specs.append(pl.BlockSpec((k, bn), lambda i, j: (0, jnp.maximum(j - half, 0))))
        args.append(b2)
    if add is not None:
        in_specs.append(pl.BlockSpec((bm, bn), lambda i, j: (i, j)))
        args.append(add)
    out_blk = pl.BlockSpec((bm, bn), lambda i, j: (i, j))
    out_specs, out_shape = out_blk, jax.ShapeDtypeStruct((m, n), out_dtype)
    if norm is not None:
        in_specs.append(pl.BlockSpec((1, bn), lambda i, j: (0, j)))
        args.append(norm)
        out_specs, out_shape = [out_blk, out_blk], [out_shape, jax.ShapeDtypeStruct((m, n), bf16)]
    return pl.pallas_call(
        body, grid=(m // bm, n // bn), in_specs=in_specs, out_specs=out_specs, out_shape=out_shape,
        compiler_params=_cparams(("parallel", "parallel")), name=name)(*args)


def _as_tuple(x):
    return tuple(x) if isinstance(x, (tuple, list)) else (x,)


def _full_spec(p):
    nd = p.ndim
    return pl.BlockSpec(p.shape, lambda i, _nd=nd: (0,) * _nd)


def _window(a, start, width):
    assert start % width == 0 and width % 128 == 0
    return (a, start // width, width)


def _row_array(x):
    return x[0] if isinstance(x, tuple) else x


def _row_shape(x):
    return (x[0].shape[0], x[2]) if isinstance(x, tuple) else x.shape


def _row_spec(x, tile):
    if isinstance(x, tuple):
        return pl.BlockSpec((tile, x[2]), lambda i, _b=x[1]: (i, _b))
    return pl.BlockSpec((tile, x.shape[1]), lambda i: (i, 0))


def _rows(fn, rows, params, outs, *, name, tile=256, accs=()):
    length = _row_shape(rows[0])[0]
    tile = min(tile, length)
    nr, npar, no = len(rows), len(params), len(outs)

    def body(*refs):
        r, p, o = refs[:nr], refs[nr:nr + npar], refs[nr + npar:]
        res = _as_tuple(fn(*[x[...].astype(f32) for x in r], *[x[...] for x in p]))
        for kk in range(no):
            o[kk][...] = res[kk].astype(o[kk].dtype)
        if accs:
            @pl.when(pl.program_id(0) == 0)
            def _():
                for kk in range(no, no + len(accs)):
                    o[kk][...] = jnp.zeros_like(o[kk])
            for kk in range(no, no + len(accs)):
                o[kk][...] += res[kk]

    in_specs = [_row_spec(x, tile) for x in rows] + [_full_spec(p) for p in params]
    out_specs = [pl.BlockSpec((tile, w), lambda i: (i, 0)) for w, _ in outs]
    out_shape = [jax.ShapeDtypeStruct((length, w), d) for w, d in outs]
    for s in accs:
        out_specs.append(pl.BlockSpec(s, lambda i, _nd=len(s): (0,) * _nd))
        out_shape.append(jax.ShapeDtypeStruct(s, f32))
    res = pl.pallas_call(body, grid=(length // tile,), in_specs=in_specs, out_specs=out_specs, out_shape=out_shape,
                         compiler_params=_cparams(("arbitrary",)), name=name)(*[_row_array(x) for x in rows], *params)
    return res


def _rows_bwd(fn, rows, params, cts, *, name, rgrad, pgrad, tile=256, addends=None):
    addends = {i: (a if isinstance(a, list) else [(a, 0)]) for i, a in (addends or {}).items()}
    length = _row_shape(rows[0])[0]
    tile = min(tile, length)
    nr, npar, nc = len(rows), len(params), len(cts)
    ridx = [i for i in range(nr) if rgrad[i] is not None]
    pidx = [i for i in range(npar) if pgrad[i]]
    flat_addends = [(i, a, off) for i in sorted(addends) for a, off in addends[i]]
    na = len(flat_addends)

    def body(*refs):
        r, p, c = refs[:nr], refs[nr:nr + npar], refs[nr + npar:nr + npar + nc]
        ad = refs[nr + npar + nc:nr + npar + nc + na]
        o = refs[nr + npar + nc + na:]
        rv = [x[...].astype(f32) for x in r]
        pv = [x[...] for x in p]
        cv = tuple(x[...].astype(f32) for x in c)

        def g(*d):
            rr, pp = list(rv), list(pv)
            for n_, i_ in enumerate(ridx):
                rr[i_] = d[n_]
            for n_, i_ in enumerate(pidx):
                pp[i_] = d[len(ridx) + n_]
            return _as_tuple(fn(*rr, *pp))

        _, vjp = jax.vjp(g, *[rv[i] for i in ridx], *[pv[i] for i in pidx])
        grads = vjp(cv)
        for n_, i_ in enumerate(ridx):
            val = grads[n_]
            for k_, (j_, a_, off) in enumerate(flat_addends):
                if j_ == i_:
                    extra = ad[k_][...].astype(f32)
                    if extra.shape[1] != val.shape[1]:
                        extra = jnp.pad(extra, ((0, 0), (off, val.shape[1] - off - extra.shape[1])))
                    val = val + extra
            o[n_][...] = val.astype(o[n_].dtype)
        if pidx:
            @pl.when(pl.program_id(0) == 0)
            def _():
                for n_ in range(len(pidx)):
                    o[len(ridx) + n_][...] = jnp.zeros_like(o[len(ridx) + n_])
            for n_ in range(len(pidx)):
                o[len(ridx) + n_][...] += grads[len(ridx) + n_]

    plain = lambda shape: pl.BlockSpec((tile, shape[1]), lambda i: (i, 0))
    in_specs = ([_row_spec(x, tile) for x in rows] + [_full_spec(p) for p in params] + [plain(x.shape) for x in cts]
                + [plain(a.shape) for _, a, _ in flat_addends])
    out_specs = [plain(_row_shape(rows[i])) for i in ridx] + [_full_spec(params[i]) for i in pidx]
    out_shape = ([jax.ShapeDtypeStruct(_row_shape(rows[i]), rgrad[i]) for i in ridx]
                 + [jax.ShapeDtypeStruct(params[i].shape, f32) for i in pidx])
    res = pl.pallas_call(body, grid=(length // tile,), in_specs=in_specs, out_specs=out_specs, out_shape=out_shape,
                         compiler_params=_cparams(("arbitrary",)), name=name)(
        *[_row_array(x) for x in rows], *params, *cts, *[a for _, a, _ in flat_addends])
    return list(res[:len(ridx)]), list(res[len(ridx):])


def _rms(x, g):
    return x * lax.rsqrt(jnp.mean(x * x, axis=-1, keepdims=True) + EPS) * g


def _rms_twice(x, g):
    y = _rms(x, g)
    return y, y


def _silu(x):
    return x * jax.nn.sigmoid(x)


def _shift_down(x, s):
    rows = lax.broadcasted_iota(jnp.int32, x.shape, 0)
    return jnp.where(rows >= s, pltpu.roll(x, s, axis=0), 0.0)


def _shift_up(x, s):
    n = x.shape[0]
    rows = lax.broadcasted_iota(jnp.int32, x.shape, 0)
    return jnp.where(rows < n - s, pltpu.roll(x, n - s, axis=0), 0.0)


_CONV_COLS = 128


def _conv_gate_fwd(u, cw, cb, *, name):
    length, two_f = u.shape
    ff = two_f // 2
    nb = ff // _CONV_COLS

    def body(ug, uv, wg, wv, bg, bv, o):
        def conv(x_ref, w_ref, b_ref):
            x = x_ref[...].astype(f32)
            return (w_ref[2:3, :] * x + w_ref[1:2, :] * _shift_down(x, 1) + w_ref[0:1, :] * _shift_down(x, 2)
                    + b_ref[...])
        o[...] = (_silu(conv(ug, wg, bg)) * conv(uv, wv, bv)).astype(o.dtype)

    blk = lambda r, off: pl.BlockSpec((r, _CONV_COLS), lambda j, _o=off: (0, j + _o))
    return pl.pallas_call(
        body, grid=(nb,),
        in_specs=[blk(length, 0), blk(length, nb), blk(CONV_W, 0), blk(CONV_W, nb), blk(1, 0), blk(1, nb)],
        out_specs=blk(length, 0), out_shape=jax.ShapeDtypeStruct((length, ff), bf16),
        compiler_params=_cparams(("parallel",)), name=name)(u, u, cw, cw, cb, cb)


def _conv_gate_bwd(u, cw, cb, da, *, name):
    length, two_f = u.shape
    ff = two_f // 2
    nb = ff // _CONV_COLS

    def body(ug, uv, wg, wv, bg, bv, da_ref, dug, duv, dwg, dwv, dbg, dbv):
        def conv(x, w_ref, b_ref):
            x1, x2 = _shift_down(x, 1), _shift_down(x, 2)
            return w_ref[2:3, :] * x + w_ref[1:2, :] * x1 + w_ref[0:1, :] * x2 + b_ref[...], x1, x2

        xg, xv = ug[...].astype(f32), uv[...].astype(f32)
        g, xg1, xg2 = conv(xg, wg, bg)
        v, xv1, xv2 = conv(xv, wv, bv)
        d = da_ref[...].astype(f32)
        sg = jax.nn.sigmoid(g)
        dg = d * v * (sg * (1.0 + g * (1.0 - sg)))
        dv = d * (g * sg)

        def back(dy, x, x1, x2, w_ref, du_ref, dw_ref, db_ref):
            du_ref[...] = (w_ref[2:3, :] * dy + w_ref[1:2, :] * _shift_up(dy, 1)
                           + w_ref[0:1, :] * _shift_up(dy, 2)).astype(du_ref.dtype)
            dw_ref[2:3, :] = jnp.sum(dy * x, axis=0, keepdims=True)
            dw_ref[1:2, :] = jnp.sum(dy * x1, axis=0, keepdims=True)
            dw_ref[0:1, :] = jnp.sum(dy * x2, axis=0, keepdims=True)
            db_ref[...] = jnp.sum(dy, axis=0, keepdims=True)

        back(dg, xg, xg1, xg2, wg, dug, dwg, dbg)
        back(dv, xv, xv1, xv2, wv, duv, dwv, dbv)

    blk = lambda r, off: pl.BlockSpec((r, _CONV_COLS), lambda j, _o=off: (0, j + _o))
    sds = jax.ShapeDtypeStruct
    dug, duv, dwg, dwv, dbg, dbv = pl.pallas_call(
        body, grid=(nb,),
        in_specs=[blk(length, 0), blk(length, nb), blk(CONV_W, 0), blk(CONV_W, nb), blk(1, 0), blk(1, nb),
                  blk(length, 0)],
        out_specs=[blk(length, 0), blk(length, 0), blk(CONV_W, 0), blk(CONV_W, 0), blk(1, 0), blk(1, 0)],
        out_shape=[sds((length, ff), bf16), sds((length, ff), bf16), sds((CONV_W, ff), f32), sds((CONV_W, ff), f32),
                   sds((1, ff), f32), sds((1, ff), f32)],
        compiler_params=_cparams(("parallel",)), name=name)(u, u, cw, cw, cb, cb, da)
    return dug, duv, jnp.concatenate([dwg, dwv], axis=1), jnp.concatenate([dbg, dbv], axis=1)


def _ffn_fwd(h, w, tag, hf=None):
    if hf is None:
        hf, = _rows(_rms, [h], [w["norm_ffn"]], [(h.shape[1], bf16)], name=f"ffn_norm_{tag}")
    u = _mm(hf, w["ffn_w_up"], out_dtype=bf16, name=f"ffn_up_{tag}")
    a = _conv_gate_fwd(u, w["ffn_conv_w"], w["ffn_conv_b"], name=f"ffn_conv_{tag}")
    out = _mm(a, w["ffn_w_down"], add=h, name=f"ffn_down_{tag}")
    return out, (h, hf, u, a)


def _ffn_bwd(dout, w, saved, tag):
    h, hf, u, a = saved
    ff = a.shape[1]
    da = _mm(dout, w["ffn_w_down"], tb=True, out_dtype=bf16, name=f"ffn_da_{tag}")
    g = {"ffn_w_down": _mm(a, dout, ta=True, name=f"ffn_dwdown_{tag}")}
    dug, duv, g["ffn_conv_w"], g["ffn_conv_b"] = _conv_gate_bwd(u, w["ffn_conv_w"], w["ffn_conv_b"], da,
                                                                name=f"ffn_dconv_{tag}")
    dhf = _mm(dug, w["ffn_w_up"], tb=True, kslab=(0, 2), name=f"ffn_dhf_g_{tag}")
    dhf = _mm(duv, w["ffn_w_up"], tb=True, kslab=(1, 2), add=dhf, out_dtype=bf16, name=f"ffn_dhf_v_{tag}")
    g["ffn_w_up"] = _mm(hf, dug, ta=True, b2=duv, name=f"ffn_dwup_{tag}")
    (dh,), (g["norm_ffn"],) = _rows_bwd(_rms, [h], [w["norm_ffn"]], [dhf], rgrad=[f32], pgrad=[True],
                                        addends={0: dout}, name=f"ffn_dnorm_{tag}")
    return dh, g


def _xattn_fn(qx, kx, vx, qg, kg):
    outs = []
    for hh in range(XA_HEADS):
        sl = slice(hh * XA_DIM, (hh + 1) * XA_DIM)
        q = _rms(qx[:, sl], qg).astype(bf16)
        k = _rms(kx[:, sl], kg).astype(bf16)
        s = lax.dot_general(q, k, _NT, preferred_element_type=f32) * (XA_DIM ** -0.5)
        s = s - jnp.max(s, axis=-1, keepdims=True)
        p = jnp.exp(s)
        p = p / jnp.sum(p, axis=-1, keepdims=True)
        outs.append(jnp.dot(p.astype(bf16), vx[:, sl].astype(bf16), preferred_element_type=f32))
    return jnp.concatenate(outs, axis=-1)


def _xattn_fwd(h, mem, w, tag, hx=None):
    d = h.shape[1]
    if hx is None:
        hx, = _rows(_rms, [h], [w["norm_xa"]], [(d, bf16)], name=f"xa_norm_{tag}")
    qx = _mm(hx, w["xa_wq"], name=f"xa_q_{tag}")
    m, = _rows(_rms, [mem], [w["norm_mem"]], [(d, bf16)], name=f"xa_mnorm_{tag}")
    kx = _mm(m, w["xa_wk"], name=f"xa_k_{tag}")
    vx = _mm(m, w["xa_wv"], name=f"xa_v_{tag}")
    o, = _rows(_xattn_fn, [qx], [kx, vx, w["xa_q_norm"], w["xa_k_norm"]], [(d, bf16)], tile=1024,
               name=f"xa_attn_{tag}")
    out, hf = _mm(o, w["xa_wo"], add=h, norm=w["norm_ffn"], name=f"xa_o_{tag}")
    return out, (h, hx, qx, m, kx, vx, o), hf


def _xattn_bwd(dout, mem, w, saved, tag):
    h, hx, qx, m, kx, vx, o = saved
    g = {}
    do = _mm(dout, w["xa_wo"], tb=True, out_dtype=bf16, name=f"xa_do_{tag}")
    g["xa_wo"] = _mm(o, dout, ta=True, name=f"xa_dwo_{tag}")
    (dqx,), (dkx, dvx, g["xa_q_norm"], g["xa_k_norm"]) = _rows_bwd(
        _xattn_fn, [qx], [kx, vx, w["xa_q_norm"], w["xa_k_norm"]], [do], rgrad=[bf16], pgrad=[True] * 4,
        tile=1024, name=f"xa_dattn_{tag}")
    dhx = _mm(dqx, w["xa_wq"], tb=True, out_dtype=bf16, name=f"xa_dhx_{tag}")
    g["xa_wq"] = _mm(hx, dqx, ta=True, name=f"xa_dwq_{tag}")
    (dh,), (g["norm_xa"],) = _rows_bwd(_rms, [h], [w["norm_xa"]], [dhx], rgrad=[f32], pgrad=[True],
                                       addends={0: dout}, name=f"xa_dnorm_{tag}")
    dm = _mm(dkx, w["xa_wk"], tb=True, name=f"xa_dm_k_{tag}")
    dm = _mm(dvx, w["xa_wv"], tb=True, add=dm, name=f"xa_dm_v_{tag}")
    g["xa_wk"] = _mm(m, dkx, ta=True, name=f"xa_dwk_{tag}")
    g["xa_wv"] = _mm(m, dvx, ta=True, name=f"xa_dwv_{tag}")
    _, (g["norm_mem"],) = _rows_bwd(_rms, [mem], [w["norm_mem"]], [dm], rgrad=[None], pgrad=[True],
                                    name=f"xa_dmnorm_{tag}")
    return dh, g


_HG_GROUP = 4


def _hg_chunk(q, k, v, g, *sts):
    c = q.shape[0]
    heads = [slice(h * HG_DIM, (h + 1) * HG_DIM) for h in range(len(sts))]
    tri = (lax.broadcasted_iota(jnp.int32, (c, c), 0) >= lax.broadcasted_iota(jnp.int32, (c, c), 1)).astype(f32)
    b = jnp.dot(tri, g, precision=lax.Precision.HIGHEST, preferred_element_type=f32)
    bend = jnp.sum(g, axis=0, keepdims=True)
    qe = (q * jnp.exp(b)).astype(bf16)
    kd = (k * jnp.exp(bend - b)).astype(bf16)
    vb = v.astype(bf16)
    decay = jnp.exp(bend)
    o_inter = [lax.dot_general(qe[:, hs], st.astype(bf16), _NT, preferred_element_type=f32) for hs, st in zip(heads, sts)]
    new = [st * decay[:, hs] + lax.dot_general(vb[:, hs], kd[:, hs], _TN, preferred_element_type=f32)
           for hs, st in zip(heads, sts)]
    outs = []
    for i in range(c // HG_SUB):
        lo, n = HG_SUB * i, HG_SUB * (i + 1)
        ref = jnp.sum(g[:lo], axis=0, keepdims=True) if i else jnp.zeros((1, g.shape[1]), f32)
        qh = (q[lo:n] * jnp.exp(b[lo:n] - ref)).astype(bf16)
        kh = (k[:n] * jnp.exp(ref - b[:n])).astype(bf16)
        keep = (lax.broadcasted_iota(jnp.int32, (HG_SUB, n), 1)
                <= lo + lax.broadcasted_iota(jnp.int32, (HG_SUB, n), 0))
        scores = [lax.dot_general(qh[:, hs], kh[:, hs], _NT, preferred_element_type=f32) for hs in heads]
        scores = [jnp.where(keep, a, 0.0).astype(bf16) for a in scores]
        outs.append(jnp.concatenate([jnp.dot(a, vb[:n, hs], preferred_element_type=f32)
                                     for a, hs in zip(scores, heads)], axis=1))
    return (jnp.concatenate(outs, axis=0) + jnp.concatenate(o_inter, axis=1), *new)


def _hg_fwd(q, k, v, g, *, name):
    length = q.shape[0]
    rows = _HG_GROUP * HG_CHUNK
    ng = length // rows
    nc = length // HG_CHUNK

    def body(q_ref, k_ref, v_ref, g_ref, o_ref, st_ref, state):
        @pl.when(pl.program_id(0) == 0)
        def _():
            state[...] = jnp.zeros_like(state)

        states = [state[h] for h in range(HG_HEADS)]
        for ci in range(_HG_GROUP):
            sl = slice(ci * HG_CHUNK, (ci + 1) * HG_CHUNK)
            for h in range(HG_HEADS):
                st_ref[h, ci] = states[h]
            o, *states = _hg_chunk(q_ref[sl, :], k_ref[sl, :], v_ref[sl, :], g_ref[sl, :], *states)
            o_ref[sl, :] = o
        for h in range(HG_HEADS):
            state[h] = states[h]

    blk = pl.BlockSpec((rows, HG_WIDTH), lambda c: (c, 0))
    return pl.pallas_call(
        body, grid=(ng,), in_specs=[blk] * 4,
        out_specs=[blk, pl.BlockSpec((HG_HEADS, _HG_GROUP, HG_DIM, HG_DIM), lambda c: (0, c, 0, 0))],
        out_shape=[jax.ShapeDtypeStruct((length, HG_WIDTH), f32),
                   jax.ShapeDtypeStruct((HG_HEADS, nc, HG_DIM, HG_DIM), f32)],
        scratch_shapes=[pltpu.VMEM((HG_HEADS, HG_DIM, HG_DIM), f32)],
        compiler_params=_cparams(("arbitrary",)), name=name)(q, k, v, g)


def _hg_bwd(q, k, v, g, states, do, *, name):
    length = q.shape[0]
    rows = _HG_GROUP * HG_CHUNK
    ng = length // rows

    def body(q_ref, k_ref, v_ref, g_ref, st_ref, do_ref, dq_ref, dk_ref, dv_ref, dg_ref, dstate):
        @pl.when(pl.program_id(0) == 0)
        def _():
            dstate[...] = jnp.zeros_like(dstate)

        dstates = [dstate[h] for h in range(HG_HEADS)]
        for ci in reversed(range(_HG_GROUP)):
            sl = slice(ci * HG_CHUNK, (ci + 1) * HG_CHUNK)
            _, vjp = jax.vjp(_hg_chunk, q_ref[sl, :], k_ref[sl, :], v_ref[sl, :], g_ref[sl, :],
                             *[st_ref[h, ci] for h in range(HG_HEADS)])
            dq, dk, dv, dg, *dstates = vjp((do_ref[sl, :], *dstates))
            dq_ref[sl, :] = dq
            dk_ref[sl, :] = dk
            dv_ref[sl, :] = dv
            dg_ref[sl, :] = dg
        for h in range(HG_HEADS):
            dstate[h] = dstates[h]

    blk = pl.BlockSpec((rows, HG_WIDTH), lambda c: (ng - 1 - c, 0))
    sds = jax.ShapeDtypeStruct((length, HG_WIDTH), f32)
    return pl.pallas_call(
        body, grid=(ng,),
        in_specs=[blk] * 4 + [pl.BlockSpec((HG_HEADS, _HG_GROUP, HG_DIM, HG_DIM), lambda c: (0, ng - 1 - c, 0, 0)), blk],
        out_specs=[blk] * 4, out_shape=[sds] * 4,
        scratch_shapes=[pltpu.VMEM((HG_HEADS, HG_DIM, HG_DIM), f32)],
        compiler_params=_cparams(("arbitrary",)), name=name)(q, k, v, g, states, do)


_ATT_BLK = 512
_ATT_SCALE = MLA_QK ** -0.5
_NEG = -1e30


def _att_mask(i, j, t):
    rows = i * t + lax.broadcasted_iota(jnp.int32, (t, t), 0)
    cols = j * t + lax.broadcasted_iota(jnp.int32, (t, t), 1)
    return cols <= rows


def _att_fwd(q, k, v, *, name):
    length = q.shape[0]
    t = min(_ATT_BLK, length)
    nq = length // t
    qw, vw = MLA_QK_PAD, MLA_V
    heads = range(MLA_HEADS)

    def body(q_ref, k_ref, v_ref, o_ref, lse_ref):
        i = pl.program_id(0)
        qbs = [q_ref[:, h * qw:(h + 1) * qw] for h in heads]

        def step(j, carry, diagonal=False):
            off = pl.multiple_of(j * t, t)
            out = []
            for h in heads:
                m, l, acc = carry[h]
                ks = k_ref[pl.ds(off, t), h * qw:(h + 1) * qw]
                vs = v_ref[pl.ds(off, t), h * vw:(h + 1) * vw]
                s = lax.dot_general(qbs[h], ks, _NT, preferred_element_type=f32) * _ATT_SCALE
                if diagonal:
                    s = jnp.where(_att_mask(i, j, t), s, _NEG)
                m_new = jnp.maximum(m, jnp.max(s, axis=-1, keepdims=True))
                alpha = jnp.exp(m - m_new)
                p = jnp.exp(s - m_new)
                l = alpha * l + jnp.sum(p, axis=-1, keepdims=True)
                acc = alpha * acc + jnp.dot(p.astype(bf16), vs, preferred_element_type=f32)
                out.append((m_new, l, acc))
            return tuple(out)

        init = tuple((jnp.full((t, 1), _NEG, f32), jnp.zeros((t, 1), f32), jnp.zeros((t, vw), f32)) for _ in heads)
        res = step(i, lax.fori_loop(0, i, step, init), diagonal=True)
        for h in heads:
            m, l, acc = res[h]
            o_ref[:, h * vw:(h + 1) * vw] = (acc / l).astype(o_ref.dtype)
            lse_ref[:, h * vw:(h + 1) * vw] = jnp.broadcast_to(m + jnp.log(l), (t, vw))

    return pl.pallas_call(
        body, grid=(nq,),
        in_specs=[pl.BlockSpec((t, q.shape[1]), lambda i: (i, 0)), pl.BlockSpec(k.shape, lambda i: (0, 0)),
                  pl.BlockSpec(v.shape, lambda i: (0, 0))],
        out_specs=[pl.BlockSpec((t, v.shape[1]), lambda i: (i, 0))] * 2,
        out_shape=[jax.ShapeDtypeStruct(v.shape, bf16), jax.ShapeDtypeStruct(v.shape, f32)],
        compiler_params=_cparams(("arbitrary",)), name=name)(q, k, v)


def _att_bwd(q, k, v, o, lse, do, *, name):
    length = q.shape[0]
    t = min(_ATT_BLK, length)
    nq = length // t
    qw, vw = MLA_QK_PAD, MLA_V
    heads = range(MLA_HEADS)

    def dq_body(q_ref, k_ref, v_ref, o_ref, lse_ref, do_ref, dq_ref, delta_ref):
        i = pl.program_id(0)
        qbs = [q_ref[:, h * qw:(h + 1) * qw] for h in heads]
        dobs = [do_ref[:, h * vw:(h + 1) * vw] for h in heads]
        lses = [lse_ref[:, h * vw:h * vw + 1] for h in heads]
        deltas = [jnp.sum(dobs[h].astype(f32) * o_ref[:, h * vw:(h + 1) * vw].astype(f32), axis=-1, keepdims=True)
                  for h in heads]

        def step(j, dqs, diagonal=False):
            off = pl.multiple_of(j * t, t)
            out = []
            for h in heads:
                ks = k_ref[pl.ds(off, t), h * qw:(h + 1) * qw]
                vs = v_ref[pl.ds(off, t), h * vw:(h + 1) * vw]
                s = lax.dot_general(qbs[h], ks, _NT, preferred_element_type=f32) * _ATT_SCALE
                p = jnp.exp(s - lses[h])
                if diagonal:
                    p = jnp.where(_att_mask(i, j, t), p, 0.0)
                dp = lax.dot_general(dobs[h], vs, _NT, preferred_element_type=f32)
                ds = p * (dp - deltas[h]) * _ATT_SCALE
                out.append(dqs[h] + jnp.dot(ds.astype(bf16), ks, preferred_element_type=f32))
            return tuple(out)

        dqs = step(i, lax.fori_loop(0, i, step, tuple(jnp.zeros((t, qw), f32) for _ in heads)), diagonal=True)
        for h in heads:
            dq_ref[:, h * qw:(h + 1) * qw] = dqs[h].astype(dq_ref.dtype)
            delta_ref[:, h * vw:(h + 1) * vw] = jnp.broadcast_to(deltas[h], (t, vw))

    qblk = pl.BlockSpec((t, q.shape[1]), lambda i: (i, 0))
    vblk = pl.BlockSpec((t, v.shape[1]), lambda i: (i, 0))
    qfull = pl.BlockSpec(q.shape, lambda i: (0, 0))
    vfull = pl.BlockSpec(v.shape, lambda i: (0, 0))
    dq, delta = pl.pallas_call(
        dq_body, grid=(nq,), in_specs=[qblk, qfull, vfull, vblk, vblk, vblk], out_specs=[qblk, vblk],
        out_shape=[jax.ShapeDtypeStruct(q.shape, bf16), jax.ShapeDtypeStruct(lse.shape, f32)],
        compiler_params=_cparams(("arbitrary",)), name=name + "_dq")(q, k, v, o, lse, do)

    def dkv_body(k_ref, v_ref, q_ref, do_ref, lse_ref, delta_ref, dk_ref, dv_ref):
        j = pl.program_id(0)
        kbs = [k_ref[:, h * qw:(h + 1) * qw] for h in heads]
        vbs = [v_ref[:, h * vw:(h + 1) * vw] for h in heads]

        def step(i, carry, diagonal=False):
            off = pl.multiple_of(i * t, t)
            out = []
            for h in heads:
                dk, dv = carry[h]
                qs = q_ref[pl.ds(off, t), h * qw:(h + 1) * qw]
                dos = do_ref[pl.ds(off, t), h * vw:(h + 1) * vw]
                lse_i = lse_ref[pl.ds(off, t), h * vw:h * vw + 1]
                delta_i = delta_ref[pl.ds(off, t), h * vw:h * vw + 1]
                s = lax.dot_general(qs, kbs[h], _NT, preferred_element_type=f32) * _ATT_SCALE
                p = jnp.exp(s - lse_i)
                if diagonal:
                    p = jnp.where(_att_mask(i, j, t), p, 0.0)
                dv = dv + lax.dot_general(p.astype(bf16), dos, _TN, preferred_element_type=f32)
                dp = lax.dot_general(dos, vbs[h], _NT, preferred_element_type=f32)
                ds = p * (dp - delta_i) * _ATT_SCALE
                dk = dk + lax.dot_general(ds.astype(bf16), qs, _TN, preferred_element_type=f32)
                out.append((dk, dv))
            return tuple(out)

        first = step(j, tuple((jnp.zeros((t, qw), f32), jnp.zeros((t, vw), f32)) for _ in heads), diagonal=True)
        res = lax.fori_loop(j + 1, nq, step, first)
        for h in heads:
            dk_ref[:, h * qw:(h + 1) * qw] = res[h][0].astype(dk_ref.dtype)
            dv_ref[:, h * vw:(h + 1) * vw] = res[h][1].astype(dv_ref.dtype)

    dk, dv = pl.pallas_call(
        dkv_body, grid=(nq,), in_specs=[qblk, vblk, qfull, vfull, vfull, vfull], out_specs=[qblk, vblk],
        out_shape=[jax.ShapeDtypeStruct(k.shape, bf16), jax.ShapeDtypeStruct(v.shape, bf16)],
        compiler_params=_cparams(("arbitrary",)), name=name + "_dkv")(k, v, q, do, lse, delta)
    return dq, dk, dv


_C_Q = 4 * HG_WIDTH
_C_KV = _C_Q + MLA_Q_RANK
_C_KPE = _C_KV + MLA_KV_RANK


def _rms_n(x, g, n):
    return x * lax.rsqrt(jnp.sum(x * x, axis=-1, keepdims=True) * (1.0 / n) + EPS) * g


def _mix_a(proj, l0, l1, q_a_norm, kv_a_norm):
    lb = jax.nn.sigmoid(l0 - l1)
    f = lb + (1.0 - lb) * jax.nn.sigmoid(proj[:, HG_WIDTH:2 * HG_WIDTH])
    qf = _silu(proj[:, :HG_WIDTH])
    v = proj[:, 2 * HG_WIDTH:3 * HG_WIDTH]
    cqn = _rms(proj[:, _C_Q:_C_KV], q_a_norm)
    ckvn = _rms(proj[:, _C_KV:_C_KPE], kv_a_norm)
    return qf, 1.0 - f, v, jnp.log(f), cqn, ckvn


def _mix_b(qraw, kvraw, kpe_raw, cos, sin, qn_nope, qn_rope, kn_nope, kn_rope, perm):
    def rope(x):
        return x * cos + jnp.dot(x, perm, precision=lax.Precision.HIGHEST, preferred_element_type=f32) * sin

    kpe = rope(_rms_n(kpe_raw, kn_rope, MLA_ROPE))
    qs, ks, vs = [], [], []
    for hh in range(MLA_HEADS):
        base = hh * MLA_QK_PAD
        qs.append(_rms(qraw[:, base:base + MLA_NOPE], qn_nope))
        qs.append(rope(_rms_n(qraw[:, base + MLA_NOPE:base + MLA_QK_PAD], qn_rope, MLA_ROPE)))
        ks.append(_rms(kvraw[:, base:base + MLA_NOPE], kn_nope))
        ks.append(kpe)
        vs.append(kvraw[:, base + MLA_NOPE:base + MLA_QK_PAD])
    return jnp.concatenate(qs, axis=-1), jnp.concatenate(ks, axis=-1), jnp.concatenate(vs, axis=-1)


def _mix_c(o_hg, gate, o_mla, hg_out_norm):
    parts = []
    for hh in range(HG_HEADS):
        sl = slice(hh * HG_DIM, (hh + 1) * HG_DIM)
        parts.append(_rms(o_hg[:, sl], hg_out_norm[:, sl]))
    o = jnp.concatenate(parts, axis=-1) * _silu(gate)
    return jnp.concatenate([o, o_mla], axis=-1)


def _rope_perm():
    p = np.zeros((128, 128), np.float32)
    half = MLA_ROPE // 2
    for i in range(half):
        p[i + half, i] = -1.0
        p[i, i + half] = 1.0
    return jnp.asarray(p)


def _mixer_fwd(h, cos, sin, w, tag, next_gain):
    d = h.shape[1]
    hn, = _rows(_rms, [h], [w["norm_mix"]], [(d, bf16)], name=f"mix_norm_{tag}")
    proj = _mm(hn, w["mix_w_in"], name=f"mix_in_{tag}")
    pa = [w["lb0"], w["lb1"], w["mla_q_a_norm"], w["mla_kv_a_norm"]]
    qf, kk, vv, logf, cqn, ckvn = _rows(
        _mix_a, [proj], pa, [(HG_WIDTH, f32)] * 4 + [(MLA_Q_RANK, bf16), (MLA_KV_RANK, bf16)], name=f"mix_a_{tag}")
    o_hg, states = _hg_fwd(qf, kk, vv, logf, name=f"hg_fwd_{tag}")
    qraw = _mm(cqn, w["mla_w_uq"], name=f"mla_uq_{tag}")
    kvraw = _mm(ckvn, w["mla_w_ukv"], name=f"mla_ukv_{tag}")
    pb = [w["mla_qn_nope"], w["mla_qn_rope"], w["mla_kn_nope"], w["mla_kn_rope"], w["rope_perm"]]
    kpe_raw, gate = _window(proj, _C_KPE, IN_PAD - _C_KPE), _window(proj, 3 * HG_WIDTH, HG_WIDTH)
    qfull, kfull, vfull = _rows(_mix_b, [qraw, kvraw, kpe_raw, cos, sin], pb,
                                [(MLA_HEADS * MLA_QK_PAD, bf16)] * 2 + [(MLA_HEADS * MLA_V, bf16)],
                                name=f"mix_b_{tag}")
    o_mla, lse = _att_fwd(qfull, kfull, vfull, name=f"att_fwd_{tag}")
    mixin, = _rows(_mix_c, [o_hg, gate, o_mla], [w["hg_out_norm"]], [(d, bf16)], name=f"mix_c_{tag}")
    if callable(w["mix_w_out"]):
        w["mix_w_out"] = w["mix_w_out"](mixin)
    out, normed = _mm(mixin, w["mix_w_out"], add=h, norm=next_gain, name=f"mix_out_{tag}")
    return out, (h, hn, proj, qf, kk, vv, logf, cqn, ckvn, o_hg, states, qraw, kvraw, qfull, kfull, vfull, o_mla,
                 lse, mixin), normed


def _mixer_bwd(dout, cos, sin, w, saved, tag, on_w_out=None):
    (h, hn, proj, qf, kk, vv, logf, cqn, ckvn, o_hg, states, qraw, kvraw, qfull, kfull, vfull, o_mla, lse,
     mixin) = saved
    g = {}
    dmixin = _mm(dout, w["mix_w_out"], tb=True, name=f"mix_dmixin_{tag}")
    g["mix_w_out"] = _mm(mixin, dout, ta=True, name=f"mix_dwout_{tag}")
    if on_w_out is not None:
        dmixin = on_w_out(g["mix_w_out"], dmixin)
    kpe_raw, gate = _window(proj, _C_KPE, IN_PAD - _C_KPE), _window(proj, 3 * HG_WIDTH, HG_WIDTH)
    (do_hg, dgate, do_mla), (g["hg_out_norm"],) = _rows_bwd(
        _mix_c, [o_hg, gate, o_mla], [w["hg_out_norm"]], [dmixin], rgrad=[f32, f32, bf16], pgrad=[True],
        name=f"mix_dc_{tag}")
    dqfull, dkfull, dvfull = _att_bwd(qfull, kfull, vfull, o_mla, lse, do_mla, name=f"att_bwd_{tag}")
    pb = [w["mla_qn_nope"], w["mla_qn_rope"], w["mla_kn_nope"], w["mla_kn_rope"], w["rope_perm"]]
    (dqraw, dkvraw, dkpe_raw), pg = _rows_bwd(
        _mix_b, [qraw, kvraw, kpe_raw, cos, sin], pb, [dqfull, dkfull, dvfull],
        rgrad=[bf16, bf16, f32, None, None], pgrad=[True, True, True, True, False],
        name=f"mix_db_{tag}")
    g["mla_qn_nope"], g["mla_qn_rope"], g["mla_kn_nope"], g["mla_kn_rope"] = pg
    dcqn = _mm(dqraw, w["mla_w_uq"], tb=True, name=f"mla_dcq_{tag}")
    g["mla_w_uq"] = _mm(cqn, dqraw, ta=True, name=f"mla_dwuq_{tag}")
    dckvn = _mm(dkvraw, w["mla_w_ukv"], tb=True, name=f"mla_dckv_{tag}")
    g["mla_w_ukv"] = _mm(ckvn, dkvraw, ta=True, name=f"mla_dwukv_{tag}")
    dqf, dkk, dvv, dlogf = _hg_bwd(qf, kk, vv, logf, states, do_hg, name=f"hg_bwd_{tag}")
    pa = [w["lb0"], w["lb1"], w["mla_q_a_norm"], w["mla_kv_a_norm"]]
    (dproj,), (g["lb0"], g["lb1"], g["mla_q_a_norm"], g["mla_kv_a_norm"]) = _rows_bwd(
        _mix_a, [proj], pa, [dqf, dkk, dvv, dlogf, dcqn, dckvn], rgrad=[bf16], pgrad=[True] * 4,
        addends={0: [(dgate, 3 * HG_WIDTH), (dkpe_raw, _C_KPE)]}, name=f"mix_da_{tag}")
    dhn = _mm(dproj, w["mix_w_in"], tb=True, out_dtype=bf16, name=f"mix_dhn_{tag}")
    g["mix_w_in_t"] = _mm(dproj, hn, ta=True, name=f"mix_dwin_{tag}")
    (dh,), (g["norm_mix"],) = _rows_bwd(_rms, [h], [w["norm_mix"]], [dhn], rgrad=[f32], pgrad=[True],
                                        addends={0: dout}, name=f"mix_dnorm_{tag}")
    return dh, g


def _rope_tables(positions):
    inv_freq = 1.0 / (ROPE_BASE ** (jnp.arange(0, MLA_ROPE, 2, dtype=f32) / MLA_ROPE))
    ang = positions.astype(f32)[:, None] * inv_freq
    z = jnp.zeros((positions.shape[0], 128 - MLA_ROPE), f32)
    return (jnp.concatenate([jnp.cos(ang), jnp.cos(ang), z], axis=1),
            jnp.concatenate([jnp.sin(ang), jnp.sin(ang), z], axis=1))


def _pad_cols(a, n):
    return jnp.pad(a, ((0, 0), (0, n - a.shape[1])))


def _even_weights(p, j, layer, dt):
    w_uq = p["mla_w_uq"][j].reshape(MLA_Q_RANK, MLA_HEADS, MLA_QK)
    w_uq = jnp.pad(w_uq, ((0, 0), (0, 0), (0, MLA_QK_PAD - MLA_QK))).reshape(MLA_Q_RANK, MLA_HEADS * MLA_QK_PAD)
    return dict(
        norm_mix=p["norm_mix"][layer][None], mix_w_in=_pad_cols(p["mix_w_in"][j], IN_PAD).astype(dt),
        lb0=p["hg_lb_logits"][0][None], lb1=p["hg_lb_logits"][1][None],
        mla_q_a_norm=p["mla_q_a_norm"][j][None], mla_kv_a_norm=p["mla_kv_a_norm"][j][None],
        mla_w_uq=w_uq.astype(dt), mla_w_ukv=p["mla_w_ukv"][j].astype(dt),
        mla_qn_nope=p["mla_qn_nope"][j][None], mla_qn_rope=_pad_cols(p["mla_qn_rope"][j][None], 128),
        mla_kn_nope=p["mla_kn_nope"][j][None], mla_kn_rope=_pad_cols(p["mla_kn_rope"][j][None], 128),
        rope_perm=_rope_perm(), hg_out_norm=p["hg_out_norm"][j][None],
        mix_w_out=p["mix_w_out"][j].astype(dt) if "mix_w_out" in p else None)


def _even_grads(g):
    w_uq = g["mla_w_uq"].reshape(MLA_Q_RANK, MLA_HEADS, MLA_QK_PAD)[:, :, :MLA_QK].reshape(MLA_Q_RANK, -1)
    return dict(
        norm_mix=g["norm_mix"], mix_w_in=g["mix_w_in_t"][:IN_WIDTH].T[None],
        hg_lb_logits=jnp.concatenate([g["lb0"], g["lb1"]], axis=0),
        mla_q_a_norm=g["mla_q_a_norm"], mla_kv_a_norm=g["mla_kv_a_norm"], mla_w_uq=w_uq[None],
        mla_w_ukv=g["mla_w_ukv"][None], mla_qn_nope=g["mla_qn_nope"], mla_qn_rope=g["mla_qn_rope"][:, :MLA_ROPE],
        mla_kn_nope=g["mla_kn_nope"], mla_kn_rope=g["mla_kn_rope"][:, :MLA_ROPE],
        hg_out_norm=g["hg_out_norm"], mix_w_out=g["mix_w_out"][None])


_S5_NB = 8
_S5_BW = 1024
_S5_HALF = 512
_S5_UC = 128
_S5_TIME = 512


def _cmul(ar, ai, br, bi):
    return ar * br - ai * bi, ar * bi + ai * br


def _pow_table(ar, ai, descending):
    rows = lax.broadcasted_iota(jnp.int32, (8, ar.shape[1]), 0)
    tr = jnp.zeros((8, ar.shape[1]), f32)
    ti = jnp.zeros((8, ar.shape[1]), f32)
    pr, pi_ = ar, ai
    for r in range(8):
        sel = rows == ((7 - r) if descending else r)
        tr = jnp.where(sel, pr, tr)
        ti = jnp.where(sel, pi_, ti)
        pr, pi_ = _cmul(pr, pi_, ar, ai)
    return tr, ti


def _s5_tile_scan(work, carry, ar, ai, tc, reverse, per_tile=None):
    hw = _S5_HALF
    row8 = lax.broadcasted_iota(jnp.int32, (8, hw), 0)
    powers = [(ar, ai)]
    for _ in range(2):
        powers.append(_cmul(*powers[-1], *powers[-1]))
    steps = []
    for (mr, mi), s in zip(powers, (1, 2, 4)):
        ok = (row8 < 8 - s) if reverse else (row8 >= s)
        steps.append((jnp.where(ok, mr, 0.0), jnp.where(ok, mi, 0.0), 8 - s if reverse else s))
    tr, ti = _pow_table(ar, ai, reverse)
    cr, ci = carry[:, :hw], carry[:, hw:]
    tiles = range(tc // 8)
    for i in (reversed(tiles) if reverse else tiles):
        sl = slice(8 * i, 8 * i + 8)
        xr, xi = work[sl, :hw], work[sl, hw:]
        for mr, mi, shift in steps:
            pr, pi_ = _cmul(mr, mi, pltpu.roll(xr, shift, axis=0), pltpu.roll(xi, shift, axis=0))
            xr, xi = xr + pr, xi + pi_
        pr, pi_ = _cmul(tr, ti, cr, ci)
        xr, xi = xr + pr, xi + pi_
        work[sl, :hw] = xr
        work[sl, hw:] = xi
        if per_tile is not None:
            per_tile(sl, xr, xi, cr, ci)
        edge = 8 * i if reverse else 8 * i + 7
        cr, ci = work[edge:edge + 1, :hw], work[edge:edge + 1, hw:]
    carry[:, :hw] = cr
    carry[:, hw:] = ci


def _s5_core_fwd(a, hn, b3, c3, *, name):
    length = hn.shape[0]
    tc = min(_S5_TIME, length)

    def body(a_ref, hn_ref, b_ref, c_ref, hs_ref, y_ref, work, carry):
        @pl.when(pl.program_id(1) == 0)
        def _():
            carry[...] = jnp.zeros_like(carry)

        work[...] = jnp.dot(hn_ref[...].astype(bf16), b_ref[...], preferred_element_type=f32)
        _s5_tile_scan(work, carry, a_ref[:, :_S5_HALF], a_ref[:, _S5_HALF:], tc, False)
        hs = work[...].astype(bf16)
        hs_ref[...] = hs
        y_ref[...] = jnp.dot(hs, c_ref[...], preferred_element_type=f32)

    return pl.pallas_call(
        body, grid=(_S5_NB, length // tc),
        in_specs=[pl.BlockSpec((1, _S5_BW), lambda j, t: (0, j)), pl.BlockSpec((tc, _S5_UC), lambda j, t: (t, j)),
                  pl.BlockSpec((_S5_UC, _S5_BW), lambda j, t: (j, 0)), pl.BlockSpec((_S5_BW, _S5_UC), lambda j, t: (j, 0))],
        out_specs=[pl.BlockSpec((tc, _S5_BW), lambda j, t: (t, j)), pl.BlockSpec((tc, _S5_UC), lambda j, t: (t, j))],
        out_shape=[jax.ShapeDtypeStruct((length, _S5_NB * _S5_BW), bf16),
                   jax.ShapeDtypeStruct((length, _S5_NB * _S5_UC), f32)],
        scratch_shapes=[pltpu.VMEM((tc, _S5_BW), f32), pltpu.VMEM((1, _S5_BW), f32)],
        compiler_params=_cparams(("parallel", "arbitrary")), name=name)(a, hn, b3, c3)


def _s5_core_bwd(a, dy, c3, hs, hn, b3, *, name):
    length = hn.shape[0]
    tc = min(_S5_TIME, length)
    nt = length // tc
    hw = _S5_HALF

    def body(a_ref, dy_ref, c_ref, hs_ref, hn_ref, b_ref, du_ref, db_ref, dc_ref, da_ref, work, carry, acc):
        @pl.when(pl.program_id(1) == 0)
        def _():
            carry[...] = jnp.zeros_like(carry)
            db_ref[...] = jnp.zeros_like(db_ref)
            dc_ref[...] = jnp.zeros_like(dc_ref)
            da_ref[...] = jnp.zeros_like(da_ref)

        dyb = dy_ref[...].astype(bf16)
        work[...] = lax.dot_general(dyb, c_ref[...], _NT, preferred_element_type=f32)
        acc[...] = jnp.zeros_like(acc)
        row8 = lax.broadcasted_iota(jnp.int32, (8, hw), 0)

        def grad_a(sl, gr, gi, cr, ci):
            gnr = jnp.where(row8 == 7, cr, pltpu.roll(gr, 7, axis=0))
            gni = jnp.where(row8 == 7, ci, pltpu.roll(gi, 7, axis=0))
            hr, hi = hs_ref[sl, :hw].astype(f32), hs_ref[sl, hw:].astype(f32)
            acc[:, :hw] += hr * gnr + hi * gni
            acc[:, hw:] += hr * gni - hi * gnr

        _s5_tile_scan(work, carry, a_ref[:, :hw], -a_ref[:, hw:], tc, True, grad_a)
        da_ref[...] += jnp.sum(acc[...], axis=0, keepdims=True)
        g = work[...].astype(bf16)
        du_ref[...] = lax.dot_general(g, b_ref[...], _NT, preferred_element_type=f32)
        db_ref[...] += lax.dot_general(hn_ref[...].astype(bf16), g, _TN, preferred_element_type=f32)
        dc_ref[...] += lax.dot_general(hs_ref[...], dyb, _TN, preferred_element_type=f32)

    rev = lambda j, t: (nt - 1 - t, j)
    return pl.pallas_call(
        body, grid=(_S5_NB, nt),
        in_specs=[pl.BlockSpec((1, _S5_BW), lambda j, t: (0, j)), pl.BlockSpec((tc, _S5_UC), rev),
                  pl.BlockSpec((_S5_BW, _S5_UC), lambda j, t: (j, 0)), pl.BlockSpec((tc, _S5_BW), rev),
                  pl.BlockSpec((tc, _S5_UC), rev), pl.BlockSpec((_S5_UC, _S5_BW), lambda j, t: (j, 0))],
        out_specs=[pl.BlockSpec((tc, _S5_UC), rev), pl.BlockSpec((_S5_UC, _S5_BW), lambda j, t: (j, 0)),
                   pl.BlockSpec((_S5_BW, _S5_UC), lambda j, t: (j, 0)), pl.BlockSpec((1, _S5_BW), lambda j, t: (0, j))],
        out_shape=[jax.ShapeDtypeStruct((length, _S5_NB * _S5_UC), f32),
                   jax.ShapeDtypeStruct((_S5_NB * _S5_UC, _S5_BW), f32),
                   jax.ShapeDtypeStruct((_S5_NB * _S5_BW, _S5_UC), f32),
                   jax.ShapeDtypeStruct((1, _S5_NB * _S5_BW), f32)],
        scratch_shapes=[pltpu.VMEM((tc, _S5_BW), f32), pltpu.VMEM((1, _S5_BW), f32), pltpu.VMEM((8, _S5_BW), f32)],
        compiler_params=_cparams(("parallel", "arbitrary")), name=name)(a, dy, c3, hs, hn, b3)


def _s5_disc(lr, li, ldt, btr, bti, expand):
    dt = jnp.exp(ldt)
    mag = jnp.exp(lr * dt)
    abr = mag * jnp.cos(li * dt)
    abi = mag * jnp.sin(li * dt)
    den = lr * lr + li * li
    zr = ((abr - 1.0) * lr + abi * li) / den
    zi = (abi * lr - (abr - 1.0) * li) / den
    zr = jnp.dot(zr, expand, precision=lax.Precision.HIGHEST, preferred_element_type=f32)
    zi = jnp.dot(zi, expand, precision=lax.Precision.HIGHEST, preferred_element_type=f32)
    return abr, abi, zr * btr - zi * bti, zr * bti + zi * btr


def _s5_disc_fwd(args, *, name):
    def body(*refs):
        res = _s5_disc(*[r[...] for r in refs[:6]])
        for o, v in zip(refs[6:], res):
            o[...] = v

    sds = jax.ShapeDtypeStruct
    return pl.pallas_call(body, out_shape=[sds(args[0].shape, f32)] * 2 + [sds(args[3].shape, f32)] * 2,
                          name=name)(*args)


def _s5_disc_bwd(args, cts, *, name):
    def body(*refs):
        vals = [r[...] for r in refs[:6]]
        _, vjp = jax.vjp(lambda *d: _s5_disc(*d, vals[5]), *vals[:5])
        grads = vjp(tuple(r[...] for r in refs[6:10]))
        for o, v in zip(refs[10:], grads):
            o[...] = v

    return pl.pallas_call(body, out_shape=[jax.ShapeDtypeStruct(a.shape, f32) for a in args[:5]],
                          name=name)(*args, *cts)


def _gelu_tanh(x):
    return 0.5 * x * (1.0 + jnp.tanh(0.7978845608028654 * (x + 0.044715 * (x * x * x))))


def _s5_post(y, u, d_skip):
    return _gelu_tanh(y + d_skip * u)


def _s5_glu(ga, gb, h):
    return h + ga * jax.nn.sigmoid(gb)


def _s5_glu_norm(ga, gb, h, next_gain):
    out = _s5_glu(ga, gb, h)
    return out, _rms(out, next_gain)


def _s5_expand():
    e = np.zeros((S5_STATE, S5_GROUP * S5_STATE), np.float32)
    for m in range(S5_GROUP):
        e[np.arange(S5_STATE), m * S5_STATE + np.arange(S5_STATE)] = 1.0
    return jnp.asarray(e)


def _s5_pack_b(bbr, bbi):
    eye = jnp.eye(8, dtype=f32)

    def one(bb):
        b5 = bb.reshape(_S5_NB, 8, S5_GROUP, S5_STATE)
        return jnp.einsum("jgmp,gh->jgmhp", b5, eye).reshape(_S5_NB * _S5_UC, _S5_HALF)

    return jnp.concatenate([one(bbr), one(bbi)], axis=1)


def _s5_unpack_b(db3):
    def one(d):
        d5 = d.reshape(_S5_NB, 8, S5_GROUP, 8, S5_STATE)
        return jnp.einsum("jgmgp->jgmp", d5).reshape(S5_GROUPS, S5_GROUP * S5_STATE)

    return one(db3[:, :_S5_HALF]), one(db3[:, _S5_HALF:])


def _s5_pack_c(c_re, c_im):
    eye = jnp.eye(8, dtype=f32)

    def one(c):
        c4 = c.reshape(_S5_NB, 8, S5_GROUP, S5_STATE)
        return jnp.einsum("jgmp,hg->jhpgm", c4, eye).reshape(_S5_NB, _S5_HALF, _S5_UC)

    return jnp.concatenate([one(c_re), -one(c_im)], axis=1).reshape(_S5_NB * _S5_BW, _S5_UC)


def _s5_unpack_c(dc3):
    d = dc3.reshape(_S5_NB, 2, 8, S5_STATE, 8, S5_GROUP)
    dre = jnp.einsum("jgpgm->jgmp", d[:, 0]).reshape(S5_GROUPS, S5_GROUP, S5_STATE)
    dim = -jnp.einsum("jgpgm->jgmp", d[:, 1]).reshape(S5_GROUPS, S5_GROUP, S5_STATE)
    return dre, dim


def _s5_state_row(re, im):
    r = re.reshape(_S5_NB, 1, _S5_HALF)
    i = im.reshape(_S5_NB, 1, _S5_HALF)
    return jnp.concatenate([r, i], axis=2).reshape(1, _S5_NB * _S5_BW)


def _s5_unstate_row(row):
    r = row.reshape(_S5_NB, 2, 8, S5_STATE)
    return r[:, 0].reshape(S5_GROUPS, S5_STATE), r[:, 1].reshape(S5_GROUPS, S5_STATE)


def _s5_fwd(h, w, tag, next_gain):
    d = h.shape[1]
    hn, = _rows(_rms, [h], [w["norm_mix"]], [(d, f32)], name=f"s5_norm_{tag}")
    disc_in = [w["s5_lam_re"], w["s5_lam_im"], w["s5_log_dt"], w["s5_bt_re"], w["s5_bt_im"], w["s5_expand"]]
    abr, abi, bbr, bbi = _s5_disc_fwd(disc_in, name=f"s5_disc_{tag}")
    a_row = _s5_state_row(abr, abi)
    b3 = _s5_pack_b(bbr, bbi).astype(bf16)
    hs, y = _s5_core_fwd(a_row, hn, b3, w["s5_c3"], name=f"s5_core_{tag}")
    yg, = _rows(_s5_post, [y, hn], [w["s5_d"]], [(d, bf16)], name=f"s5_post_{tag}")
    ga = _mm(yg, w["s5_w_glu_a"], name=f"s5_glu_a_{tag}")
    gb = _mm(yg, w["s5_w_glu_b"], name=f"s5_glu_b_{tag}")
    out, normed = _rows(_s5_glu_norm, [ga, gb, h], [next_gain], [(d, f32), (d, bf16)], name=f"s5_glu_{tag}")
    return out, (h, hn, disc_in, a_row, b3, hs, y, yg, ga, gb), normed


def _s5_bwd(dout, w, saved, tag):
    h, hn, disc_in, a_row, b3, hs, y, yg, ga, gb = saved
    g = {}
    (dga, dgb), _ = _rows_bwd(_s5_glu, [ga, gb, h], [], [dout], rgrad=[bf16, bf16, None], pgrad=[],
                              name=f"s5_dglu_{tag}")
    dyg = _mm(dga, w["s5_w_glu_a"], tb=True, name=f"s5_dyg_a_{tag}")
    dyg = _mm(dgb, w["s5_w_glu_b"], tb=True, add=dyg, name=f"s5_dyg_b_{tag}")
    g["s5_w_glu_a"] = _mm(yg, dga, ta=True, name=f"s5_dwa_{tag}")
    g["s5_w_glu_b"] = _mm(yg, dgb, ta=True, name=f"s5_dwb_{tag}")
    (dy, du_skip), (g["s5_d"],) = _rows_bwd(_s5_post, [y, hn], [w["s5_d"]], [dyg], rgrad=[bf16, f32], pgrad=[True],
                                           name=f"s5_dpost_{tag}")
    du, db3, dc3, da_row = _s5_core_bwd(a_row, dy, w["s5_c3"], hs, hn, b3, name=f"s5_dcore_{tag}")
    dabr, dabi = _s5_unstate_row(da_row)
    dbbr, dbbi = _s5_unpack_b(db3)
    g["s5_lam_re"], g["s5_lam_im"], g["s5_log_dt"], g["s5_bt_re"], g["s5_bt_im"] = _s5_disc_bwd(
        disc_in, [dabr, dabi, dbbr, dbbi], name=f"s5_ddisc_{tag}")
    g["s5_c_re"], g["s5_c_im"] = _s5_unpack_c(dc3)
    (dh,), (g["norm_mix"],) = _rows_bwd(_rms_twice, [h], [w["norm_mix"]], [du, du_skip], rgrad=[f32], pgrad=[True],
                                        addends={0: dout}, name=f"s5_dnorm_{tag}")
    return dh, g


def _odd_weights(p, j, layer, dt):
    tr = lambda b: b.transpose(0, 2, 1).reshape(S5_GROUPS, S5_GROUP * S5_STATE)
    return dict(
        norm_mix=p["norm_mix"][layer][None], s5_lam_re=p["s5_lam_re"][j], s5_lam_im=p["s5_lam_im"][j],
        s5_log_dt=p["s5_log_dt"][j][:, None], s5_bt_re=tr(p["s5_b_re"][j]), s5_bt_im=tr(p["s5_b_im"][j]),
        s5_expand=_s5_expand(), s5_c3=_s5_pack_c(p["s5_c_re"][j], p["s5_c_im"][j]).astype(dt),
        **{n: (p[n][j][None] if n == "s5_d" else p[n][j].astype(dt))
           for n in ("s5_d", "s5_w_glu_a", "s5_w_glu_b") if n in p})


def _odd_grads(g):
    tr = lambda b: b.reshape(S5_GROUPS, S5_GROUP, S5_STATE).transpose(0, 2, 1)[None]
    return dict(
        norm_mix=g["norm_mix"], s5_lam_re=g["s5_lam_re"][None], s5_lam_im=g["s5_lam_im"][None],
        s5_log_dt=g["s5_log_dt"][:, 0][None], s5_b_re=tr(g["s5_bt_re"]), s5_b_im=tr(g["s5_bt_im"]),
        s5_c_re=g["s5_c_re"][None], s5_c_im=g["s5_c_im"][None], s5_d=g["s5_d"],
        s5_w_glu_a=g["s5_w_glu_a"][None], s5_w_glu_b=g["s5_w_glu_b"][None])


def _loss_fn(y, t):
    e = y - t
    part = jnp.sum(jnp.sum(e * e, axis=-1, keepdims=True), axis=0, keepdims=True) * (0.5 / y.shape[1])
    return e * (1.0 / y.shape[1]), part


FF_SHARD = 352
FF_SHARD_PAD = 384


def _pad_groups(a, axis):
    axis %= a.ndim
    zeros = jnp.zeros(a.shape[:axis] + (FF_SHARD_PAD - FF_SHARD,) + a.shape[axis + 1:], a.dtype)
    pieces = []
    for g in range(a.shape[axis] // FF_SHARD):
        pieces += [lax.slice_in_dim(a, g * FF_SHARD, (g + 1) * FF_SHARD, axis=axis), zeros]
    return jnp.concatenate(pieces, axis=axis)


def _unpad_groups(a, axis):
    axis %= a.ndim
    pieces = [lax.slice_in_dim(a, g * FF_SHARD_PAD, g * FF_SHARD_PAD + FF_SHARD, axis=axis)
              for g in range(a.shape[axis] // FF_SHARD_PAD)]
    return pieces[0] if len(pieces) == 1 else jnp.concatenate(pieces, axis=axis)


def _layer_weights(p, layer, dt):
    return dict(
        norm_xa=p["norm_xa"][layer][None], norm_mem=p["norm_mem"][layer][None], norm_ffn=p["norm_ffn"][layer][None],
        xa_wq=p["xa_wq"][layer].astype(dt), xa_wk=p["xa_wk"][layer].astype(dt), xa_wv=p["xa_wv"][layer].astype(dt),
        xa_wo=p["xa_wo"][layer].astype(dt), xa_q_norm=p["xa_q_norm"][layer][None],
        xa_k_norm=p["xa_k_norm"][layer][None], ffn_w_up=_pad_groups(p["ffn_w_up"][layer], 1).astype(dt),
        ffn_conv_w=_pad_groups(p["ffn_conv_w"][layer], 1), ffn_conv_b=_pad_groups(p["ffn_conv_b"][layer][None], 1),
        ffn_w_down=_pad_groups(p["ffn_w_down"][layer], 0).astype(dt))


_PER_LAYER = ("norm_xa", "norm_mem", "norm_ffn", "xa_wq", "xa_wk", "xa_wv", "xa_wo", "xa_q_norm", "xa_k_norm",
              "ffn_w_up", "ffn_conv_w", "ffn_conv_b", "ffn_w_down")
_FFN_PADDED = dict(ffn_w_up=1, ffn_conv_w=1, ffn_conv_b=1, ffn_w_down=0)


def _local_step(x, mem, positions, target, p):
    cos, sin = _rope_tables(positions)
    we = _even_weights(p, 0, 0, bf16)
    wo = _odd_weights(p, 0, 1, bf16)
    wl = [_layer_weights(p, layer, bf16) for layer in range(2)]
    loss, dh, g_even, g_odd, gl = _local_core(x, mem, cos, sin, target, we, wo, wl)
    grads = {}
    for n in _PER_LAYER:
        a, b = gl[0][n], gl[1][n]
        if n in _FFN_PADDED:
            a, b = _unpad_groups(a, _FFN_PADDED[n]), _unpad_groups(b, _FFN_PADDED[n])
        grads[n] = jnp.concatenate([a, b], axis=0) if a.shape[0] == 1 else jnp.stack([a, b])
    ge, go = _even_grads(g_even), _odd_grads(g_odd)
    grads["norm_mix"] = jnp.concatenate([ge.pop("norm_mix"), go.pop("norm_mix")], axis=0)
    grads.update(ge)
    grads.update(go)
    return loss, dh, grads


def _local_core(x, mem, cos, sin, target, we, wo, wl):
    h, s_mix0, hx = _mixer_fwd(x, cos, sin, we, "l0", wl[0]["norm_xa"])
    h, s_xa0, hf = _xattn_fwd(h, mem, wl[0], "l0", hx)
    h, s_ff0 = _ffn_fwd(h, wl[0], "l0", hf)
    h, s_mix1, hx = _s5_fwd(h, wo, "l1", wl[1]["norm_xa"])
    h, s_xa1, hf = _xattn_fwd(h, mem, wl[1], "l1", hx)
    h, s_ff1 = _ffn_fwd(h, wl[1], "l1", hf)
    dh, loss = _rows(_loss_fn, [h, target], [], [(h.shape[1], f32)], accs=[(1, 1)], name="loss_head")

    gl = [{}, {}]
    dh, g = _ffn_bwd(dh, wl[1], s_ff1, "l1")
    gl[1].update(g)
    dh, g = _xattn_bwd(dh, mem, wl[1], s_xa1, "l1")
    gl[1].update(g)
    dh, g_odd = _s5_bwd(dh, wo, s_mix1, "l1")
    dh, g = _ffn_bwd(dh, wl[0], s_ff0, "l0")
    gl[0].update(g)
    dh, g = _xattn_bwd(dh, mem, wl[0], s_xa0, "l0")
    gl[0].update(g)
    dh, g_even = _mixer_bwd(dh, cos, sin, we, s_mix0, "l0")
    return loss, dh, g_even, g_odd, gl


_LANES = 1024
_ROW_PAD = 16


_PEER_MASKS = (1, 2, 4, 3, 5, 6, 7)


def _mesh_place():
    x, y, c = lax.axis_index("x"), lax.axis_index("y"), lax.axis_index("c")

    def peer(mask):
        px = 1 - x if mask & 4 else x
        py = 1 - y if mask & 2 else y
        pc = 1 - c if mask & 1 else c
        return (px, py, pc), 4 * px + 2 * py + pc

    return 4 * x + 2 * y + c, peer


class _Exchange:
    def __init__(self, name):
        self.name = name
        self.srcs, self.shapes, self.items, self.where = [], [], [], {}

    def add(self, src, land_shape, src_at, dst_at, key):
        si = next((i for i, s in enumerate(self.srcs) if s is src), None)
        if si is None:
            self.srcs.append(src)
            si = len(self.srcs) - 1
        if key not in self.where:
            self.shapes.append(land_shape)
            self.where[key] = len(self.shapes) - 1
        self.items.append(dict(src=si, dst=self.where[key], src_at=src_at, dst_at=dst_at))

    def _copy(self, k, mask, ins, lands, send_sems, recv_sems, me, peer, arriving):
        it = self.items[k]
        dev, idx = peer(mask)
        s = k * (N_DEV - 1) + _PEER_MASKS.index(mask)
        return pltpu.make_async_remote_copy(
            src_ref=it["src_at"](ins[it["src"]], idx), dst_ref=it["dst_at"](lands[it["dst"]], idx if arriving else me),
            send_sem=send_sems.at[s], recv_sem=recv_sems.at[s], device_id=dev, device_id_type=pl.DeviceIdType.MESH)

    def _own_copy(self, k, ins, lands, own_sems, me):
        it = self.items[k]
        return pltpu.make_async_copy(it["src_at"](ins[it["src"]], me), it["dst_at"](lands[it["dst"]], me), own_sems.at[k])

    def begin(self, own):
        ns, nd, ni = len(self.srcs), len(self.shapes), len(self.items)
        nsem = ni * (N_DEV - 1)
        self.own = own

        nq = 3 if own else 2

        def body(*refs):
            ins, land_refs = refs[:ns], refs[ns:ns + nd]
            sems, token = refs[ns + nd:ns + nd + nq], refs[-1]
            me, peer = _mesh_place()
            for mask in _PEER_MASKS:
                for k in range(ni):
                    self._copy(k, mask, ins, land_refs, sems[0], sems[1], me, peer, False).start()
            if own:
                for k in range(ni):
                    self._own_copy(k, ins, land_refs, sems[2], me).start()
            token[...] = jnp.zeros_like(token)

        hbm = pl.BlockSpec(memory_space=pltpu.HBM)
        sem = pl.BlockSpec(memory_space=pltpu.SEMAPHORE)
        lands = [lax.empty(s.shape, s.dtype) for s in self.shapes]
        sem_shapes = [pltpu.SemaphoreType.DMA((nsem,)), pltpu.SemaphoreType.DMA((nsem,)), pltpu.SemaphoreType.DMA((ni,))]
        res = pl.pallas_call(
            body, in_specs=[hbm] * (ns + nd),
            out_specs=[sem] * nq + [hbm] * nd + [pl.BlockSpec(memory_space=pltpu.VMEM)],
            out_shape=sem_shapes[:nq] + [pltpu.HBM(s.shape, s.dtype) for s in self.shapes]
            + [jax.ShapeDtypeStruct((8, 128), f32)],
            input_output_aliases={ns + j: nq + j for j in range(nd)},
            compiler_params=pltpu.CompilerParams(has_side_effects=pltpu.SideEffectType.DATAFLOW_SIDE_EFFECTING),
            name=self.name + "_start")(*self.srcs, *lands)
        self.token = res[-1]
        return list(res[:nq]), list(res[nq:-1])

    def finish(self, state, after):
        sems, lands = state
        nq = len(sems)
        after = list(after) if isinstance(after, (list, tuple)) else [after]
        ns, nd, ni = len(self.srcs), len(self.shapes), len(self.items)

        def body(*refs):
            ins, land_refs = refs[:ns], refs[ns:ns + nd]
            sem_refs = refs[ns + nd:ns + nd + nq]
            me, peer = _mesh_place()
            for mask in _PEER_MASKS:
                for k in range(ni):
                    cp = self._copy(k, mask, ins, land_refs, sem_refs[0], sem_refs[1], me, peer, True)
                    cp.wait_send()
                    cp.wait_recv()
            if self.own:
                for k in range(ni):
                    self._own_copy(k, ins, land_refs, sem_refs[2], me).wait()

        hbm = pl.BlockSpec(memory_space=pltpu.HBM)
        sem = pl.BlockSpec(memory_space=pltpu.SEMAPHORE)
        res = pl.pallas_call(
            body, in_specs=[hbm] * (ns + nd) + [sem] * nq + [pl.BlockSpec(memory_space=pl.ANY)] * len(after),
            out_specs=[hbm] * nd, out_shape=[pltpu.HBM(s.shape, s.dtype) for s in self.shapes],
            input_output_aliases={ns + j: j for j in range(nd)},
            compiler_params=pltpu.CompilerParams(has_side_effects=pltpu.SideEffectType.DATAFLOW_SIDE_EFFECTING),
            name=self.name + "_wait")(*self.srcs, *lands, *sems, *after)
        return {k: res[i] for k, i in self.where.items()}


def _after(x, *tokens, name):
    def body(*refs):
        del refs

    anyspace = pl.BlockSpec(memory_space=pl.ANY)
    return pl.pallas_call(body, in_specs=[anyspace] * (1 + len(tokens)), out_specs=anyspace,
                          out_shape=jax.ShapeDtypeStruct(x.shape, x.dtype), input_output_aliases={0: 0},
                          name=name)(x, *tokens)


def _rows_of(n):
    return lambda r, i: r.at[pl.ds(pl.multiple_of(i * n, n), n), :]


def _cols_of(n):
    return lambda r, i: r.at[:, pl.ds(pl.multiple_of(i * n, n), n)]


def _whole(r, i):
    return r


def _slot(r, i):
    return r.at[i]


def _at_layer(layer):
    return lambda r, i: r.at[layer]


def _sum_adam(me_index, slots, owns, own_block, w, m, v, *, name):
    layers, rows, cols = w.shape
    tr = _pick(rows, (256, 128, 104, 64, 32, 16, 8))
    bc1 = 1.0 - ADAM_B1 ** ADAM_STEP
    bc2 = 1.0 - ADAM_B2 ** ADAM_STEP
    own_shape, own_map = own_block(tr)
    nl = len(slots)
    assert nl == layers and len(owns) == layers

    def body(me_ref, *refs):
        s_refs, own_refs = refs[:nl], refs[nl:2 * nl]
        w_ref, m_ref, v_ref, g_ref, d_ref, nm_ref, nv_ref = refs[2 * nl:]
        me = me_ref[0]

        def run(s_ref, own_ref):
            mine = own_ref[0] if len(own_shape) == 3 else own_ref[...]
            g = jnp.where(me == 0, mine, s_ref[0])
            for k in range(1, N_DEV):
                g = g + jnp.where(me == k, mine, s_ref[k])
            mm = ADAM_B1 * m_ref[0] + (1.0 - ADAM_B1) * g
            vv = ADAM_B2 * v_ref[0] + (1.0 - ADAM_B2) * (g * g)
            g_ref[0] = g
            nm_ref[0] = mm
            nv_ref[0] = vv
            d_ref[0] = -ADAM_LR * ((mm / bc1) / (jnp.sqrt(vv / bc2) + ADAM_EPS) + ADAM_WD * w_ref[0])

        for layer in range(nl):
            pl.when(pl.program_id(0) == layer)(functools.partial(run, s_refs[layer], own_refs[layer]))

    def of_layer(layer, index_map):
        return lambda lyr, i, me: index_map(jnp.where(lyr == layer, i, 0), me)

    blk = pl.BlockSpec((1, tr, cols), lambda lyr, i, me: (lyr, i, 0))
    sds = jax.ShapeDtypeStruct((layers, rows, cols), f32)
    grid_spec = pltpu.PrefetchScalarGridSpec(
        num_scalar_prefetch=1, grid=(layers, rows // tr),
        in_specs=[pl.BlockSpec((N_DEV, tr, cols), of_layer(layer, lambda i, me: (0, i, 0))) for layer in range(nl)]
        + [pl.BlockSpec(own_shape, of_layer(layer, own_map)) for layer in range(nl)] + [blk, blk, blk],
        out_specs=[blk] * 4)
    return pl.pallas_call(body, grid_spec=grid_spec, out_shape=[sds] * 4,
                          compiler_params=_cparams(("arbitrary", "arbitrary")),
                          name=name)(me_index, *slots, *owns, w, m, v)


_SHARDED = dict(xa_wq=1, xa_wk=1, xa_wv=1, xa_wo=1, ffn_w_up=2, ffn_conv_w=2, ffn_w_down=1, mix_w_in=2, mla_w_uq=2,
                mla_w_ukv=2, mix_w_out=1, s5_d=1, s5_w_glu_a=1, s5_w_glu_b=1)
_EXACT = ("ffn_conv_w", "s5_d")
_WEIGHTS = ("norm_mix", "norm_xa", "norm_mem", "norm_ffn", "xa_wq", "xa_wk", "xa_wv", "xa_wo", "xa_q_norm",
            "xa_k_norm", "ffn_w_up", "ffn_conv_w", "ffn_conv_b", "ffn_w_down", "hg_lb_logits", "mix_w_in",
            "hg_out_norm", "mla_q_a_norm", "mla_w_uq", "mla_kv_a_norm", "mla_w_ukv", "mla_qn_nope", "mla_qn_rope",
            "mla_kn_nope", "mla_kn_rope", "mix_w_out", "s5_lam_re", "s5_lam_im", "s5_log_dt", "s5_b_re", "s5_b_im",
            "s5_c_re", "s5_c_im", "s5_d", "s5_w_glu_a", "s5_w_glu_b")
_BIG = tuple(n for n in _WEIGHTS if n in _SHARDED and n not in _EXACT)
_SHARD_ORDER = tuple(n for n in _WEIGHTS if n in _SHARDED)
_REPL_ORDER = tuple(n for n in _WEIGHTS if n not in _SHARDED)
_REPL_EARLY = tuple(n for n in _REPL_ORDER if n.startswith("s5_"))
_REPL_LATE = tuple(n for n in _REPL_ORDER if n not in _REPL_EARLY)


def _pack(parts, dtype, lead=None):
    nl = 0 if lead is None else 1
    flat = [a.astype(dtype).reshape(a.shape[:nl] + (-1,)) for a in parts]
    cat = jnp.concatenate(flat, axis=nl)
    n = cat.shape[nl]
    unit = _LANES * _ROW_PAD
    total = -(-n // unit) * unit
    cat = jnp.pad(cat, [(0, 0)] * nl + [(0, total - n)])
    return cat.reshape(cat.shape[:nl] + (total // _LANES, _LANES))


def _unpack(packed, shapes, lead=None):
    nl = 0 if lead is None else 1
    flat = packed.reshape(packed.shape[:nl] + (-1,))
    out, off = [], 0
    for s in shapes:
        n = int(np.prod(s))
        piece = flat[..., off:off + n] if nl else flat[off:off + n]
        out.append(piece.reshape(packed.shape[:nl] + tuple(s)))
        off += n
    return out


def _to_full(gathered, axis):
    g = jnp.moveaxis(gathered, 0, axis)
    s = g.shape
    return g.reshape(s[:axis] + (s[axis] * s[axis + 1],) + s[axis + 2:])


def _to_shards(full, axis):
    s = full.shape
    g = full.reshape(s[:axis] + (N_DEV, s[axis] // N_DEV) + s[axis + 1:])
    return jnp.moveaxis(g, axis, 0)


_DIRECT_ROWS = ("xa_wq", "xa_wk", "xa_wv", "xa_wo", "mix_w_out", "s5_w_glu_a", "s5_w_glu_b")
_SMALL16 = ("mix_w_in", "mla_w_uq", "mla_w_ukv")
_SMALL_SHARDED = ("mla_w_uq", "mla_w_ukv") + _EXACT
_SHARD_ROWS = 128


def _exchange_layout(d):
    out = dict(d)
    out["ffn_w_up"] = _pad_groups(d["ffn_w_up"], 2)
    out["ffn_conv_w"] = _pad_groups(d["ffn_conv_w"], 2)
    out["ffn_w_down"] = _pad_groups(d["ffn_w_down"], 1)
    return out


def _train_step(x, mem, positions, target, w, m, v):
    d_model = x.shape[1]
    we_, me_, ve_ = _exchange_layout(w), _exchange_layout(m), _exchange_layout(v)
    sds = jax.ShapeDtypeStruct

    matrices = _DIRECT_ROWS + ("ffn_w_up", "ffn_w_down")
    layer_mats = ("xa_wq", "xa_wk", "xa_wv", "xa_wo", "ffn_w_up", "ffn_w_down")
    shard16 = {n: we_[n].astype(bf16) for n in matrices}
    part_of = {n: _rows_of(_SHARD_ROWS) for n in _DIRECT_ROWS}
    part_of["ffn_w_up"] = _cols_of(we_["ffn_w_up"].shape[2])
    part_of["ffn_w_down"] = _rows_of(we_["ffn_w_down"].shape[1])
    part_shape = {n: we_[n].shape[1:] for n in matrices}

    def full_shape(n):
        r, c = part_shape[n]
        return (r, N_DEV * c) if n == "ffn_w_up" else (N_DEV * r, c)

    def gather(ex, n, layer):
        ex.add(shard16[n], sds(full_shape(n), bf16), _at_layer(layer), part_of[n], (n, layer))

    def scatter(ex, n, layer, grad):
        ex.add(grad, sds((N_DEV,) + part_shape[n], f32), part_of[n], _slot, (n, layer))

    small16 = _pack([we_[n] for n in _SMALL16], bf16)
    exact = _pack([we_[n] for n in _EXACT], f32)
    ga, ga1, gb, gc = _Exchange("gather_a"), _Exchange("gather_a1"), _Exchange("gather_b"), _Exchange("gather_c")
    ga.add(small16, sds((N_DEV,) + small16.shape, bf16), _whole, _slot, "small16")
    ga1.add(exact, sds((N_DEV,) + exact.shape, f32), _whole, _slot, "exact")
    gather(ga1, "mix_w_out", 0)
    for n in layer_mats:
        gather(gb, n, 0)
    gather(gc, "s5_w_glu_a", 0)
    gather(gc, "s5_w_glu_b", 0)
    for n in layer_mats:
        gather(gc, n, 1)
    state_a, state_a1, state_b, state_c = ga.begin(True), ga1.begin(True), gb.begin(True), gc.begin(True)

    p = {n: w[n] for n in _REPL_ORDER}
    cos, sin = _rope_tables(positions)
    wo = _odd_weights(p, 0, 1, bf16)
    conv_b = _pad_groups(w["ffn_conv_b"], 1)
    prepared = [cos, sin, conv_b, wo["s5_c3"], wo["s5_bt_re"], wo["s5_bt_im"]]
    full = ga.finish(state_a, [ga1.token, gb.token, gc.token] + prepared)
    for n, a in zip(_SMALL16, _unpack(full["small16"], [we_[n].shape for n in _SMALL16], lead=True)):
        p[n] = _to_full(a, _SHARDED[n])
    we = _even_weights(p, 0, 0, bf16)
    we["norm_mix"] = _after(we["norm_mix"], ga.token, ga1.token, gb.token, gc.token, name="after_gather_starts")

    def late_mix_w_out(mixin):
        full.update(ga1.finish(state_a1, [mixin]))
        return full[("mix_w_out", 0)]

    we["mix_w_out"] = late_mix_w_out
    h, s_mix0, hx = _mixer_fwd(x, cos, sin, we, "l0", w["norm_xa"][0][None])
    conv_w, s5_d = [_to_full(a, _SHARDED[n]) for n, a in
                    zip(_EXACT, _unpack(full["exact"], [we_[n].shape for n in _EXACT], lead=True))]

    def layer_weights(layer):
        return dict(norm_xa=w["norm_xa"][layer][None], norm_mem=w["norm_mem"][layer][None],
                    norm_ffn=w["norm_ffn"][layer][None], xa_q_norm=w["xa_q_norm"][layer][None],
                    xa_k_norm=w["xa_k_norm"][layer][None], ffn_conv_w=conv_w[layer],
                    ffn_conv_b=conv_b[layer][None], **{n: full[(n, layer)] for n in layer_mats})

    full.update(gb.finish(state_b, h))
    wl = [layer_weights(0)]
    h, s_xa0, hf = _xattn_fwd(h, mem, wl[0], "l0", hx)
    h, s_ff0 = _ffn_fwd(h, wl[0], "l0", hf)
    full.update(gc.finish(state_c, h))
    wl.append(layer_weights(1))
    wo.update(s5_d=s5_d, s5_w_glu_a=full[("s5_w_glu_a", 0)], s5_w_glu_b=full[("s5_w_glu_b", 0)])
    h, s_mix1, hx = _s5_fwd(h, wo, "l1", wl[1]["norm_xa"])
    h, s_xa1, hf = _xattn_fwd(h, mem, wl[1], "l1", hx)
    h, s_ff1 = _ffn_fwd(h, wl[1], "l1", hf)
    dh, loss = _rows(_loss_fn, [h, target], [], [(h.shape[1], f32)], accs=[(1, 1)], name="loss_head")

    gl = [{}, {}]
    reduces = []

    own_grad = {}

    def reduce_start(name, entries, dh):
        ex = _Exchange(name)
        for n, layer, grad in entries.get("matrices", ()):
            scatter(ex, n, layer, grad)
            own_grad[(n, layer)] = grad
        for key, src, shape, src_at in entries.get("packs", ()):
            ex.add(src, shape, src_at, _slot, key)
        reduces.append((ex, ex.begin(False)))
        return _after(dh, ex.token, name="after_" + name)

    dh, gl[1] = _ffn_bwd(dh, wl[1], s_ff1, "l1")
    dh = reduce_start("reduce_ffn1", dict(matrices=[(n, 1, gl[1][n]) for n in ("ffn_w_up", "ffn_w_down")]), dh)
    dh, g = _xattn_bwd(dh, mem, wl[1], s_xa1, "l1")
    gl[1].update(g)
    dh = reduce_start("reduce_xa1", dict(matrices=[(n, 1, g[n]) for n in ("xa_wq", "xa_wk", "xa_wv", "xa_wo")]), dh)
    dh, g_odd = _s5_bwd(dh, wo, s_mix1, "l1")
    go = _odd_grads(g_odd)
    dh, gl[0] = _ffn_bwd(dh, wl[0], s_ff0, "l0")
    send_early = _pack([go[n].reshape(w[n].shape) for n in _REPL_EARLY], f32)
    dh = reduce_start("reduce_ffn0", dict(
        matrices=[(n, 0, g_odd[n]) for n in ("s5_w_glu_a", "s5_w_glu_b")]
        + [(n, 0, gl[0][n]) for n in ("ffn_w_up", "ffn_w_down")],
        packs=[("repl_early", send_early, sds((N_DEV,) + send_early.shape, f32), _whole)]), dh)
    dh, g = _xattn_bwd(dh, mem, wl[0], s_xa0, "l0")
    gl[0].update(g)
    dh = reduce_start("reduce_xa0", dict(matrices=[(n, 0, g[n]) for n in ("xa_wq", "xa_wk", "xa_wv", "xa_wo")]), dh)
    grad_x, g_even = _mixer_bwd(
        dh, cos, sin, we, s_mix0, "l0",
        on_w_out=lambda grad, dmixin: reduce_start("reduce_w_out", dict(matrices=[("mix_w_out", 0, grad)]), dmixin))

    ge = _even_grads(g_even)
    cat = lambda n: jnp.concatenate([gl[0][n], gl[1][n]], axis=0)
    rg = dict(ge)
    rg["norm_mix"] = jnp.concatenate([ge["norm_mix"], go["norm_mix"]], axis=0)
    for n in ("norm_xa", "norm_mem", "norm_ffn", "xa_q_norm", "xa_k_norm"):
        rg[n] = cat(n)
    rg["ffn_conv_b"] = _unpad_groups(cat("ffn_conv_b"), 1)
    sg = dict(mla_w_uq=ge["mla_w_uq"], mla_w_ukv=ge["mla_w_ukv"], s5_d=go["s5_d"],
              ffn_conv_w=jnp.stack([gl[0]["ffn_conv_w"], gl[1]["ffn_conv_w"]]))
    send_small = _pack([_to_shards(sg[n], _SHARDED[n]) for n in _SMALL_SHARDED], f32, lead=True)
    send_late = _pack([rg[n].reshape(w[n].shape) for n in _REPL_LATE], f32)
    w_in_rows = w["mix_w_in"].shape[2]
    last = _Exchange("reduce_last")
    last.add(g_even["mix_w_in_t"], sds((N_DEV, w_in_rows, d_model), f32), _rows_of(w_in_rows), _slot, "mix_w_in")
    last.add(send_small, sds(send_small.shape, f32), _slot, _slot, "small")
    last.add(send_late, sds((N_DEV,) + send_late.shape, f32), _whole, _slot, "repl_late")
    state_last = last.begin(False)
    slots = {}
    for ex, state in reduces:
        slots.update(ex.finish(state, [grad_x, last.token]))

    me_index = (4 * lax.axis_index("x") + 2 * lax.axis_index("y") + lax.axis_index("c")).astype(jnp.int32).reshape(1)

    def rows_block(r, c):
        return lambda tr: ((tr, c), lambda i, me: (me[0] * (r // tr) + i, 0))

    def own_block(n):
        r, c = part_shape[n]
        if n == "ffn_w_up":
            return lambda tr: ((tr, c), lambda i, me: (i, me[0]))
        return rows_block(r, c)

    out = [{}, {}, {}, {}]
    unpad = dict(ffn_w_up=2, ffn_conv_w=2, ffn_w_down=1)
    for n in matrices:
        layers = range(we_[n].shape[0])
        res = _sum_adam(me_index, [slots[(n, layer)] for layer in layers], [own_grad[(n, layer)] for layer in layers],
                        own_block(n), we_[n], me_[n], ve_[n], name=f"adam_{n}")
        for k in range(4):
            out[k][n] = _unpad_groups(res[k], unpad[n]) if n in unpad else res[k]
    pk = lambda d, order: _pack([d[n] for n in order], f32)[None]
    whole_rows = lambda tr: ((tr, _LANES), lambda i, me: (i, 0))
    res_early = _sum_adam(me_index, [slots["repl_early"]], [send_early], whole_rows, pk(w, _REPL_EARLY),
                          pk(m, _REPL_EARLY), pk(v, _REPL_EARLY), name="adam_repl_early")
    for k in range(4):
        out[k].update(zip(_REPL_EARLY, _unpack(res_early[k][0], [w[n].shape for n in _REPL_EARLY])))
    done = [out[k][n] for k in range(4) for n in matrices + _REPL_EARLY]
    slots = last.finish(state_last, done)
    transposed = lambda d: jnp.swapaxes(d["mix_w_in"], 1, 2)
    res_w_in = _sum_adam(me_index, [slots["mix_w_in"]], [g_even["mix_w_in_t"]], rows_block(w_in_rows, d_model),
                         transposed(w), transposed(m), transposed(v), name="adam_mix_w_in")
    res_small = _sum_adam(me_index, [slots["small"]], [send_small],
                          lambda tr: ((1, tr, _LANES), lambda i, me: (me[0], i, 0)),
                          pk(we_, _SMALL_SHARDED), pk(me_, _SMALL_SHARDED), pk(ve_, _SMALL_SHARDED), name="adam_small")
    res_late = _sum_adam(me_index, [slots["repl_late"]], [send_late], whole_rows, pk(w, _REPL_LATE), pk(m, _REPL_LATE),
                         pk(v, _REPL_LATE), name="adam_repl_late")
    for k in range(4):
        out[k]["mix_w_in"] = jnp.swapaxes(res_w_in[k], 1, 2)
        for n, a in zip(_SMALL_SHARDED, _unpack(res_small[k][0], [we_[n].shape for n in _SMALL_SHARDED])):
            out[k][n] = _unpad_groups(a, unpad[n]) if n in unpad else a
        out[k].update(zip(_REPL_LATE, _unpack(res_late[k][0], [w[n].shape for n in _REPL_LATE])))
    return loss, grad_x, out


_INPUTS = tuple("""x, mem, positions, norm_mix, norm_xa, norm_mem, norm_ffn, xa_wq, xa_wk, xa_wv, xa_wo, xa_q_norm, xa_k_norm, ffn_w_up, ffn_conv_w, ffn_conv_b, ffn_w_down, hg_lb_logits, mix_w_in, hg_out_norm, mla_q_a_norm, mla_w_uq, mla_kv_a_norm, mla_w_ukv, mla_qn_nope, mla_qn_rope, mla_kn_nope, mla_kn_rope, mix_w_out, s5_lam_re, s5_lam_im, s5_log_dt, s5_b_re, s5_b_im, s5_c_re, s5_c_im, s5_d, s5_w_glu_a, s5_w_glu_b, loss_target, m_norm_mix, m_norm_xa, m_norm_mem, m_norm_ffn, m_xa_wq, m_xa_wk, m_xa_wv, m_xa_wo, m_xa_q_norm, m_xa_k_norm, m_ffn_w_up, m_ffn_conv_w, m_ffn_conv_b, m_ffn_w_down, m_hg_lb_logits, m_mix_w_in, m_hg_out_norm, m_mla_q_a_norm, m_mla_w_uq, m_mla_kv_a_norm, m_mla_w_ukv, m_mla_qn_nope, m_mla_qn_rope, m_mla_kn_nope, m_mla_kn_rope, m_mix_w_out, m_s5_lam_re, m_s5_lam_im, m_s5_log_dt, m_s5_b_re, m_s5_b_im, m_s5_c_re, m_s5_c_im, m_s5_d, m_s5_w_glu_a, m_s5_w_glu_b, v_norm_mix, v_norm_xa, v_norm_mem, v_norm_ffn, v_xa_wq, v_xa_wk, v_xa_wv, v_xa_wo, v_xa_q_norm, v_xa_k_norm, v_ffn_w_up, v_ffn_conv_w, v_ffn_conv_b, v_ffn_w_down, v_hg_lb_logits, v_mix_w_in, v_hg_out_norm, v_mla_q_a_norm, v_mla_w_uq, v_mla_kv_a_norm, v_mla_w_ukv, v_mla_qn_nope, v_mla_qn_rope, v_mla_kn_nope, v_mla_kn_rope, v_mix_w_out, v_s5_lam_re, v_s5_lam_im, v_s5_log_dt, v_s5_b_re, v_s5_b_im, v_s5_c_re, v_s5_c_im, v_s5_d, v_s5_w_glu_a, v_s5_w_glu_b""".replace(" ", "").split(","))


def kernel(x, mem, positions, norm_mix, norm_xa, norm_mem, norm_ffn, xa_wq, xa_wk, xa_wv, xa_wo, xa_q_norm, xa_k_norm, ffn_w_up, ffn_conv_w, ffn_conv_b, ffn_w_down, hg_lb_logits, mix_w_in, hg_out_norm, mla_q_a_norm, mla_w_uq, mla_kv_a_norm, mla_w_ukv, mla_qn_nope, mla_qn_rope, mla_kn_nope, mla_kn_rope, mix_w_out, s5_lam_re, s5_lam_im, s5_log_dt, s5_b_re, s5_b_im, s5_c_re, s5_c_im, s5_d, s5_w_glu_a, s5_w_glu_b, loss_target, m_norm_mix, m_norm_xa, m_norm_mem, m_norm_ffn, m_xa_wq, m_xa_wk, m_xa_wv, m_xa_wo, m_xa_q_norm, m_xa_k_norm, m_ffn_w_up, m_ffn_conv_w, m_ffn_conv_b, m_ffn_w_down, m_hg_lb_logits, m_mix_w_in, m_hg_out_norm, m_mla_q_a_norm, m_mla_w_uq, m_mla_kv_a_norm, m_mla_w_ukv, m_mla_qn_nope, m_mla_qn_rope, m_mla_kn_nope, m_mla_kn_rope, m_mix_w_out, m_s5_lam_re, m_s5_lam_im, m_s5_log_dt, m_s5_b_re, m_s5_b_im, m_s5_c_re, m_s5_c_im, m_s5_d, m_s5_w_glu_a, m_s5_w_glu_b, v_norm_mix, v_norm_xa, v_norm_mem, v_norm_ffn, v_xa_wq, v_xa_wk, v_xa_wv, v_xa_wo, v_xa_q_norm, v_xa_k_norm, v_ffn_w_up, v_ffn_conv_w, v_ffn_conv_b, v_ffn_w_down, v_hg_lb_logits, v_mix_w_in, v_hg_out_norm, v_mla_q_a_norm, v_mla_w_uq, v_mla_kv_a_norm, v_mla_w_ukv, v_mla_qn_nope, v_mla_qn_rope, v_mla_kn_nope, v_mla_kn_rope, v_mix_w_out, v_s5_lam_re, v_s5_lam_im, v_s5_log_dt, v_s5_b_re, v_s5_b_im, v_s5_c_re, v_s5_c_im, v_s5_d, v_s5_w_glu_a, v_s5_w_glu_b):
    vals = dict(zip(_INPUTS, (x, mem, positions, norm_mix, norm_xa, norm_mem, norm_ffn, xa_wq, xa_wk, xa_wv, xa_wo, xa_q_norm, xa_k_norm, ffn_w_up, ffn_conv_w, ffn_conv_b, ffn_w_down, hg_lb_logits, mix_w_in, hg_out_norm, mla_q_a_norm, mla_w_uq, mla_kv_a_norm, mla_w_ukv, mla_qn_nope, mla_qn_rope, mla_kn_nope, mla_kn_rope, mix_w_out, s5_lam_re, s5_lam_im, s5_log_dt, s5_b_re, s5_b_im, s5_c_re, s5_c_im, s5_d, s5_w_glu_a, s5_w_glu_b, loss_target, m_norm_mix, m_norm_xa, m_norm_mem, m_norm_ffn, m_xa_wq, m_xa_wk, m_xa_wv, m_xa_wo, m_xa_q_norm, m_xa_k_norm, m_ffn_w_up, m_ffn_conv_w, m_ffn_conv_b, m_ffn_w_down, m_hg_lb_logits, m_mix_w_in, m_hg_out_norm, m_mla_q_a_norm, m_mla_w_uq, m_mla_kv_a_norm, m_mla_w_ukv, m_mla_qn_nope, m_mla_qn_rope, m_mla_kn_nope, m_mla_kn_rope, m_mix_w_out, m_s5_lam_re, m_s5_lam_im, m_s5_log_dt, m_s5_b_re, m_s5_b_im, m_s5_c_re, m_s5_c_im, m_s5_d, m_s5_w_glu_a, m_s5_w_glu_b, v_norm_mix, v_norm_xa, v_norm_mem, v_norm_ffn, v_xa_wq, v_xa_wk, v_xa_wv, v_xa_wo, v_xa_q_norm, v_xa_k_norm, v_ffn_w_up, v_ffn_conv_w, v_ffn_conv_b, v_ffn_w_down, v_hg_lb_logits, v_mix_w_in, v_hg_out_norm, v_mla_q_a_norm, v_mla_w_uq, v_mla_kv_a_norm, v_mla_w_ukv, v_mla_qn_nope, v_mla_qn_rope, v_mla_kn_nope, v_mla_kn_rope, v_mix_w_out, v_s5_lam_re, v_s5_lam_im, v_s5_log_dt, v_s5_b_re, v_s5_b_im, v_s5_c_re, v_s5_c_im, v_s5_d, v_s5_w_glu_a, v_s5_w_glu_b)))
    w = {n: vals[n] for n in _WEIGHTS}
    m = {n: vals["m_" + n] for n in _WEIGHTS}
    v = {n: vals["v_" + n] for n in _WEIGHTS}
    loss, grad_x, res = _train_step(vals["x"][0], vals["mem"][0], vals["positions"][0], vals["loss_target"][0],
                                    w, m, v)
    loss = lax.psum(loss[0, 0], ("x", "y", "c"))
    return (loss, grad_x[None], *[r[n] for r in res for n in _WEIGHTS])
```

```python
import functools

import jax
import jax.numpy as jnp
import numpy as np
from jax import lax
from jax.experimental import pallas as pl
from jax.experimental.pallas import tpu as pltpu

f32 = jnp.float32
bf16 = jnp.bfloat16

EPS = 1e-6
N_DEV = 8
VMEM_LIMIT = 52 * 1024 * 1024

HG_HEADS = 4
HG_DIM = 128
HG_WIDTH = HG_HEADS * HG_DIM
HG_CHUNK = 64
HG_SUB = 16
MLA_HEADS = 4
MLA_Q_RANK = 256
MLA_KV_RANK = 128
MLA_NOPE = 128
MLA_ROPE = 64
MLA_V = 128
MLA_QK = MLA_NOPE + MLA_ROPE
MLA_QK_PAD = 256
ROPE_BASE = 10000.0
IN_WIDTH = 4 * HG_WIDTH + MLA_Q_RANK + MLA_KV_RANK + MLA_ROPE
IN_PAD = 2560
XA_HEADS = 4
XA_DIM = 256
S5_GROUP = 16
S5_GROUPS = 64
S5_STATE = 64
CONV_W = 3

ADAM_LR = 0.001
ADAM_B1 = 0.9
ADAM_B2 = 0.999
ADAM_EPS = 1e-08
ADAM_WD = 0.01
ADAM_STEP = 10

_NT = (((1,), (1,)), ((), ()))
_TN = (((0,), (0,)), ((), ()))
_NN = (((1,), (0,)), ((), ()))


def _pick(n, cands):
    for c in cands:
        if n % c == 0:
            return c
    return n


def _cparams(sem):
    return pltpu.CompilerParams(dimension_semantics=sem, vmem_limit_bytes=VMEM_LIMIT)


_MM_BUDGET = 36 * 1024 * 1024
_MM_TILES = ((1024, 1024), (1024, 512), (512, 1024), (512, 512), (512, 256), (256, 512), (256, 256), (256, 128),
             (128, 256), (128, 128))


def _mm(a, b, *, name, ta=False, tb=False, out_dtype=f32, add=None, b2=None, kslab=None, norm=None, rms_bwd=None):
    m, k = (a.shape[1], a.shape[0]) if ta else a.shape
    nb = b.shape[0] if tb else b.shape[1]
    n = nb * (2 if b2 is not None else 1)
    slab, nslab = kslab if kslab is not None else (0, 1)
    assert (b.shape[1] // nslab if tb else b.shape[0]) == k, (a.shape, b.shape, ta, tb)
    assert b2 is None or (not tb and b2.shape == b.shape)
    full_rows = norm is not None or rms_bwd is not None
    assert not (full_rows and b2 is not None) and not (norm is not None and rms_bwd is not None)
    isz = lambda x: jnp.dtype(x.dtype).itemsize
    bm = bn = None
    for cm, cn in _MM_TILES:
        if m % cm or nb % cn or (full_rows and cn != n):
            continue
        need = 2 * (cm * k * isz(a) + cn * k * isz(b) * (2 if b2 is not None else 1)
                    + cm * cn * (jnp.dtype(out_dtype).itemsize + (4 if add is not None else 0)
                                 + (2 if norm is not None else 0) + (8 if rms_bwd is not None else 0)))
        if need <= _MM_BUDGET:
            bm, bn = cm, cn
            break
    assert bm is not None, (name, a.shape, b.shape)
    half = nb // bn
    dims = (((0 if ta else 1,), (1 if tb else 0,)), ((), ()))

    def body(*refs):
        refs = list(refs)
        a_ref, b_ref = refs[0], refs[1]
        b2_ref = refs.pop(2) if b2 is not None else None
        add_ref = refs.pop(2) if add is not None else None
        gain_ref = refs.pop(2) if norm is not None else None
        x_ref, xgain_ref, through_ref = (refs.pop(2), refs.pop(2), refs.pop(2)) if rms_bwd is not None else (None,) * 3
        o_ref = refs[2]
        extra_ref = refs[3] if (norm is not None or rms_bwd is not None) else None

        def run(rhs_ref):
            r = lax.dot_general(a_ref[...].astype(bf16), rhs_ref[...].astype(bf16), dims, preferred_element_type=f32)
            if add_ref is not None:
                r = r + add_ref[...].astype(f32)
            if rms_bwd is None:
                o_ref[...] = r.astype(o_ref.dtype)
            else:
                _, vjp = jax.vjp(_rms, x_ref[...], xgain_ref[...])
                dx, dgain = vjp(r)
                o_ref[...] = (through_ref[...] + dx).astype(o_ref.dtype)

                @pl.when(pl.program_id(0) == 0)
                def _():
                    extra_ref[...] = jnp.zeros_like(extra_ref)
                extra_ref[...] += dgain
            if norm is not None:
                extra_ref[...] = _rms(r, gain_ref[...]).astype(extra_ref.dtype)

        if b2_ref is None:
            run(b_ref)
        else:
            pl.when(pl.program_id(1) < half)(lambda: run(b_ref))
            pl.when(pl.program_id(1) >= half)(lambda: run(b2_ref))

    a_spec = pl.BlockSpec((k, bm), lambda i, j: (0, i)) if ta else pl.BlockSpec((bm, k), lambda i, j: (i, 0))
    if tb:
        b_spec = pl.BlockSpec((bn, k), lambda i, j: (j, slab))
    elif b2 is None:
        b_spec = pl.BlockSpec((k, bn), lambda i, j: (0, j))
    else:
        b_spec = pl.BlockSpec((k, bn), lambda i, j: (0, jnp.minimum(j, half - 1)))
    in_specs = [a_spec, b_spec]
    args = [a, b]
    if b2 is not None:
        in_specs.append(pl.BlockSpec((k, bn), lambda i, j: (0, jnp.maximum(j - half, 0))))
        args.append(b2)
    if add is not None:
        in_specs.append(pl.BlockSpec((bm, bn), lambda i, j: (i, j)))
        args.append(add)
    out_blk = pl.BlockSpec((bm, bn), lambda i, j: (i, j))
    out_specs, out_shape = out_blk, jax.ShapeDtypeStruct((m, n), out_dtype)
    row_blk = pl.BlockSpec((1, bn), lambda i, j: (0, j))
    if norm is not None:
        in_specs.append(row_blk)
        args.append(norm)
        out_specs, out_shape = [out_blk, out_blk], [out_shape, jax.ShapeDtypeStruct((m, n), bf16)]
    if rms_bwd is not None:
        in_specs += [out_blk, row_blk, out_blk]
        args += list(rms_bwd)
        out_specs, out_shape = [out_blk, row_blk], [out_shape, jax.ShapeDtypeStruct((1, n), f32)]
    return pl.pallas_call(
        body, grid=(m // bm, n // bn), in_specs=in_specs, out_specs=out_specs, out_shape=out_shape,
        compiler_params=_cparams(("arbitrary" if rms_bwd is not None else "parallel", "parallel")),
        name=name)(*args)


def _as_tuple(x):
    return tuple(x) if isinstance(x, (tuple, list)) else (x,)


def _full_spec(p):
    nd = p.ndim
    return pl.BlockSpec(p.shape, lambda i, _nd=nd: (0,) * _nd)


def _window(a, start, width):
    assert start % width == 0 and width % 128 == 0
    return (a, start // width, width)


def _row_array(x):
    return x[0] if isinstance(x, tuple) else x


def _row_shape(x):
    return (x[0].shape[0], x[2]) if isinstance(x, tuple) else x.shape


def _row_spec(x, tile):
    if isinstance(x, tuple):
        return pl.BlockSpec((tile, x[2]), lambda i, _b=x[1]: (i, _b))
    return pl.BlockSpec((tile, x.shape[1]), lambda i: (i, 0))


def _rows(fn, rows, params, outs, *, name, tile=256, accs=()):
    length = _row_shape(rows[0])[0]
    tile = min(tile, length)
    nr, npar, no = len(rows), len(params), len(outs)

    def body(*refs):
        r, p, o = refs[:nr], refs[nr:nr + npar], refs[nr + npar:]
        res = _as_tuple(fn(*[x[...].astype(f32) for x in r], *[x[...] for x in p]))
        for kk in range(no):
            o[kk][...] = res[kk].astype(o[kk].dtype)
        if accs:
            @pl.when(pl.program_id(0) == 0)
            def _():
                for kk in range(no, no + len(accs)):
                    o[kk][...] = jnp.zeros_like(o[kk])
            for kk in range(no, no + len(accs)):
                o[kk][...] += res[kk]

    in_specs = [_row_spec(x, tile) for x in rows] + [_full_spec(p) for p in params]
    out_specs = [pl.BlockSpec((tile, w), lambda i: (i, 0)) for w, _ in outs]
    out_shape = [jax.ShapeDtypeStruct((length, w), d) for w, d in outs]
    for s in accs:
        out_specs.append(pl.BlockSpec(s, lambda i, _nd=len(s): (0,) * _nd))
        out_shape.append(jax.ShapeDtypeStruct(s, f32))
    res = pl.pallas_call(body, grid=(length // tile,), in_specs=in_specs, out_specs=out_specs, out_shape=out_shape,
                         compiler_params=_cparams(("arbitrary",)), name=name)(*[_row_array(x) for x in rows], *params)
    return res


def _rows_bwd(fn, rows, params, cts, *, name, rgrad, pgrad, tile=256, addends=None):
    addends = {i: (a if isinstance(a, list) else [(a, 0)]) for i, a in (addends or {}).items()}
    length = _row_shape(rows[0])[0]
    tile = min(tile, length)
    nr, npar, nc = len(rows), len(params), len(cts)
    ridx = [i for i in range(nr) if rgrad[i] is not None]
    pidx = [i for i in range(npar) if pgrad[i]]
    flat_addends = [(i, a, off) for i in sorted(addends) for a, off in addends[i]]
    na = len(flat_addends)

    def body(*refs):
        r, p, c = refs[:nr], refs[nr:nr + npar], refs[nr + npar:nr + npar + nc]
        ad = refs[nr + npar + nc:nr + npar + nc + na]
        o = refs[nr + npar + nc + na:]
        rv = [x[...].astype(f32) for x in r]
        pv = [x[...] for x in p]
        cv = tuple(x[...].astype(f32) for x in c)

        def g(*d):
            rr, pp = list(rv), list(pv)
            for n_, i_ in enumerate(ridx):
                rr[i_] = d[n_]
            for n_, i_ in enumerate(pidx):
                pp[i_] = d[len(ridx) + n_]
            return _as_tuple(fn(*rr, *pp))

        _, vjp = jax.vjp(g, *[rv[i] for i in ridx], *[pv[i] for i in pidx])
        grads = vjp(cv)
        for n_, i_ in enumerate(ridx):
            val = grads[n_]
            for k_, (j_, a_, off) in enumerate(flat_addends):
                if j_ == i_:
                    extra = ad[k_][...].astype(f32)
                    if extra.shape[1] != val.shape[1]:
                        extra = jnp.pad(extra, ((0, 0), (off, val.shape[1] - off - extra.shape[1])))
                    val = val + extra
            o[n_][...] = val.astype(o[n_].dtype)
        if pidx:
            @pl.when(pl.program_id(0) == 0)
            def _():
                for n_ in range(len(pidx)):
                    o[len(ridx) + n_][...] = jnp.zeros_like(o[len(ridx) + n_])
            for n_ in range(len(pidx)):
                o[len(ridx) + n_][...] += grads[len(ridx) + n_]

    plain = lambda shape: pl.BlockSpec((tile, shape[1]), lambda i: (i, 0))
    in_specs = ([_row_spec(x, tile) for x in rows] + [_full_spec(p) for p in params] + [plain(x.shape) for x in cts]
                + [plain(a.shape) for _, a, _ in flat_addends])
    out_specs = [plain(_row_shape(rows[i])) for i in ridx] + [_full_spec(params[i]) for i in pidx]
    out_shape = ([jax.ShapeDtypeStruct(_row_shape(rows[i]), rgrad[i]) for i in ridx]
                 + [jax.ShapeDtypeStruct(params[i].shape, f32) for i in pidx])
    res = pl.pallas_call(body, grid=(length // tile,), in_specs=in_specs, out_specs=out_specs, out_shape=out_shape,
                         compiler_params=_cparams(("arbitrary",)), name=name)(
        *[_row_array(x) for x in rows], *params, *cts, *[a for _, a, _ in flat_addends])
    return list(res[:len(ridx)]), list(res[len(ridx):])


def _rms(x, g):
    return x * lax.rsqrt(jnp.mean(x * x, axis=-1, keepdims=True) + EPS) * g


def _rms_twice(x, g):
    y = _rms(x, g)
    return y, y


def _silu(x):
    return x * jax.nn.sigmoid(x)


def _shift_down(x, s):
    rows = lax.broadcasted_iota(jnp.int32, x.shape, 0)
    return jnp.where(rows >= s, pltpu.roll(x, s, axis=0), 0.0)


def _shift_up(x, s):
    n = x.shape[0]
    rows = lax.broadcasted_iota(jnp.int32, x.shape, 0)
    return jnp.where(rows < n - s, pltpu.roll(x, n - s, axis=0), 0.0)


_CONV_COLS = 128


def _conv_gate_fwd(u, cw, cb, *, name):
    length, two_f = u.shape
    ff = two_f // 2
    nb = ff // _CONV_COLS

    def body(ug, uv, wg, wv, bg, bv, o):
        def conv(x_ref, w_ref, b_ref):
            x = x_ref[...].astype(f32)
            return (w_ref[2:3, :] * x + w_ref[1:2, :] * _shift_down(x, 1) + w_ref[0:1, :] * _shift_down(x, 2)
                    + b_ref[...])
        o[...] = (_silu(conv(ug, wg, bg)) * conv(uv, wv, bv)).astype(o.dtype)

    blk = lambda r, off: pl.BlockSpec((r, _CONV_COLS), lambda j, _o=off: (0, j + _o))
    return pl.pallas_call(
        body, grid=(nb,),
        in_specs=[blk(length, 0), blk(length, nb), blk(CONV_W, 0), blk(CONV_W, nb), blk(1, 0), blk(1, nb)],
        out_specs=blk(length, 0), out_shape=jax.ShapeDtypeStruct((length, ff), bf16),
        compiler_params=_cparams(("parallel",)), name=name)(u, u, cw, cw, cb, cb)


def _conv_gate_bwd(u, cw, cb, da, *, name):
    length, two_f = u.shape
    ff = two_f // 2
    nb = ff // _CONV_COLS

    def body(ug, uv, wg, wv, bg, bv, da_ref, dug, duv, dwg, dwv, dbg, dbv):
        def conv(x, w_ref, b_ref):
            x1, x2 = _shift_down(x, 1), _shift_down(x, 2)
            return w_ref[2:3, :] * x + w_ref[1:2, :] * x1 + w_ref[0:1, :] * x2 + b_ref[...], x1, x2

        xg, xv = ug[...].astype(f32), uv[...].astype(f32)
        g, xg1, xg2 = conv(xg, wg, bg)
        v, xv1, xv2 = conv(xv, wv, bv)
        d = da_ref[...].astype(f32)
        sg = jax.nn.sigmoid(g)
        dg = d * v * (sg * (1.0 + g * (1.0 - sg)))
        dv = d * (g * sg)

        def back(dy, x, x1, x2, w_ref, du_ref, dw_ref, db_ref):
            du_ref[...] = (w_ref[2:3, :] * dy + w_ref[1:2, :] * _shift_up(dy, 1)
                           + w_ref[0:1, :] * _shift_up(dy, 2)).astype(du_ref.dtype)
            dw_ref[2:3, :] = jnp.sum(dy * x, axis=0, keepdims=True)
            dw_ref[1:2, :] = jnp.sum(dy * x1, axis=0, keepdims=True)
            dw_ref[0:1, :] = jnp.sum(dy * x2, axis=0, keepdims=True)
            db_ref[...] = jnp.sum(dy, axis=0, keepdims=True)

        back(dg, xg, xg1, xg2, wg, dug, dwg, dbg)
        back(dv, xv, xv1, xv2, wv, duv, dwv, dbv)

    blk = lambda r, off: pl.BlockSpec((r, _CONV_COLS), lambda j, _o=off: (0, j + _o))
    sds = jax.ShapeDtypeStruct
    dug, duv, dwg, dwv, dbg, dbv = pl.pallas_call(
        body, grid=(nb,),
        in_specs=[blk(length, 0), blk(length, nb), blk(CONV_W, 0), blk(CONV_W, nb), blk(1, 0), blk(1, nb),
                  blk(length, 0)],
        out_specs=[blk(length, 0), blk(length, 0), blk(CONV_W, 0), blk(CONV_W, 0), blk(1, 0), blk(1, 0)],
        out_shape=[sds((length, ff), bf16), sds((length, ff), bf16), sds((CONV_W, ff), f32), sds((CONV_W, ff), f32),
                   sds((1, ff), f32), sds((1, ff), f32)],
        compiler_params=_cparams(("parallel",)), name=name)(u, u, cw, cw, cb, cb, da)
    return dug, duv, jnp.concatenate([dwg, dwv], axis=1), jnp.concatenate([dbg, dbv], axis=1)


def _ffn_fwd(h, w, tag, hf=None):
    if hf is None:
        hf, = _rows(_rms, [h], [w["norm_ffn"]], [(h.shape[1], bf16)], name=f"ffn_norm_{tag}")
    u = _mm(hf, w["ffn_w_up"], out_dtype=bf16, name=f"ffn_up_{tag}")
    a = _conv_gate_fwd(u, w["ffn_conv_w"], w["ffn_conv_b"], name=f"ffn_conv_{tag}")
    out = _mm(a, w["ffn_w_down"], add=h, name=f"ffn_down_{tag}")
    return out, (h, hf, u, a)


def _ffn_bwd(dout, w, saved, tag):
    h, hf, u, a = saved
    ff = a.shape[1]
    da = _mm(dout, w["ffn_w_down"], tb=True, out_dtype=bf16, name=f"ffn_da_{tag}")
    g = {"ffn_w_down": _mm(a, dout, ta=True, name=f"ffn_dwdown_{tag}")}
    dug, duv, g["ffn_conv_w"], g["ffn_conv_b"] = _conv_gate_bwd(u, w["ffn_conv_w"], w["ffn_conv_b"], da,
                                                                name=f"ffn_dconv_{tag}")
    dhf = _mm(dug, w["ffn_w_up"], tb=True, kslab=(0, 2), name=f"ffn_dhf_g_{tag}")
    dh, g["norm_ffn"] = _mm(duv, w["ffn_w_up"], tb=True, kslab=(1, 2), add=dhf, rms_bwd=(h, w["norm_ffn"], dout),
                            name=f"ffn_dhf_v_{tag}")
    g["ffn_w_up"] = _mm(hf, dug, ta=True, b2=duv, name=f"ffn_dwup_{tag}")
    return dh, g


def _xattn_fn(qx, kx, vx, qg, kg):
    outs = []
    for hh in range(XA_HEADS):
        sl = slice(hh * XA_DIM, (hh + 1) * XA_DIM)
        q = _rms(qx[:, sl], qg).astype(bf16)
        k = _rms(kx[:, sl], kg).astype(bf16)
        s = lax.dot_general(q, k, _NT, preferred_element_type=f32) * (XA_DIM ** -0.5)
        s = s - jnp.max(s, axis=-1, keepdims=True)
        p = jnp.exp(s)
        p = p / jnp.sum(p, axis=-1, keepdims=True)
        outs.append(jnp.dot(p.astype(bf16), vx[:, sl].astype(bf16), preferred_element_type=f32))
    return jnp.concatenate(outs, axis=-1)


def _xattn_fwd(h, mem, w, tag, hx=None):
    d = h.shape[1]
    if hx is None:
        hx, = _rows(_rms, [h], [w["norm_xa"]], [(d, bf16)], name=f"xa_norm_{tag}")
    qx = _mm(hx, w["xa_wq"], name=f"xa_q_{tag}")
    m, = _rows(_rms, [mem], [w["norm_mem"]], [(d, bf16)], name=f"xa_mnorm_{tag}")
    kx = _mm(m, w["xa_wk"], name=f"xa_k_{tag}")
    vx = _mm(m, w["xa_wv"], name=f"xa_v_{tag}")
    o, = _rows(_xattn_fn, [qx], [kx, vx, w["xa_q_norm"], w["xa_k_norm"]], [(d, bf16)], tile=1024,
               name=f"xa_attn_{tag}")
    out, hf = _mm(o, w["xa_wo"], add=h, norm=w["norm_ffn"], name=f"xa_o_{tag}")
    return out, (h, hx, qx, m, kx, vx, o), hf


def _xattn_bwd(dout, mem, w, saved, tag):
    h, hx, qx, m, kx, vx, o = saved
    g = {}
    do = _mm(dout, w["xa_wo"], tb=True, out_dtype=bf16, name=f"xa_do_{tag}")
    g["xa_wo"] = _mm(o, dout, ta=True, name=f"xa_dwo_{tag}")
    (dqx,), (dkx, dvx, g["xa_q_norm"], g["xa_k_norm"]) = _rows_bwd(
        _xattn_fn, [qx], [kx, vx, w["xa_q_norm"], w["xa_k_norm"]], [do], rgrad=[bf16], pgrad=[True] * 4,
        tile=1024, name=f"xa_dattn_{tag}")
    dh, g["norm_xa"] = _mm(dqx, w["xa_wq"], tb=True, rms_bwd=(h, w["norm_xa"], dout), name=f"xa_dhx_{tag}")
    g["xa_wq"] = _mm(hx, dqx, ta=True, name=f"xa_dwq_{tag}")
    dm = _mm(dkx, w["xa_wk"], tb=True, name=f"xa_dm_k_{tag}")
    dm = _mm(dvx, w["xa_wv"], tb=True, add=dm, name=f"xa_dm_v_{tag}")
    g["xa_wk"] = _mm(m, dkx, ta=True, name=f"xa_dwk_{tag}")
    g["xa_wv"] = _mm(m, dvx, ta=True, name=f"xa_dwv_{tag}")
    _, (g["norm_mem"],) = _rows_bwd(_rms, [mem], [w["norm_mem"]], [dm], rgrad=[None], pgrad=[True],
                                    name=f"xa_dmnorm_{tag}")
    return dh, g


_HG_GROUP = 4


def _hg_chunk(q, k, v, g, *sts):
    c = q.shape[0]
    heads = [slice(h * HG_DIM, (h + 1) * HG_DIM) for h in range(len(sts))]
    tri = (lax.broadcasted_iota(jnp.int32, (c, c), 0) >= lax.broadcasted_iota(jnp.int32, (c, c), 1)).astype(f32)
    b = jnp.dot(tri, g, precision=lax.Precision.HIGHEST, preferred_element_type=f32)
    bend = jnp.sum(g, axis=0, keepdims=True)
    qe = (q * jnp.exp(b)).astype(bf16)
    kd = (k * jnp.exp(bend - b)).astype(bf16)
    vb = v.astype(bf16)
    decay = jnp.exp(bend)
    o_inter = [lax.dot_general(qe[:, hs], st.astype(bf16), _NT, preferred_element_type=f32) for hs, st in zip(heads, sts)]
    new = [st * decay[:, hs] + lax.dot_general(vb[:, hs], kd[:, hs], _TN, preferred_element_type=f32)
           for hs, st in zip(heads, sts)]
    outs = []
    for i in range(c // HG_SUB):
        lo, n = HG_SUB * i, HG_SUB * (i + 1)
        ref = jnp.sum(g[:lo], axis=0, keepdims=True) if i else jnp.zeros((1, g.shape[1]), f32)
        qh = (q[lo:n] * jnp.exp(b[lo:n] - ref)).astype(bf16)
        kh = (k[:n] * jnp.exp(ref - b[:n])).astype(bf16)
        keep = (lax.broadcasted_iota(jnp.int32, (HG_SUB, n), 1)
                <= lo + lax.broadcasted_iota(jnp.int32, (HG_SUB, n), 0))
        scores = [lax.dot_general(qh[:, hs], kh[:, hs], _NT, preferred_element_type=f32) for hs in heads]
        scores = [jnp.where(keep, a, 0.0).astype(bf16) for a in scores]
        outs.append(jnp.concatenate([jnp.dot(a, vb[:n, hs], preferred_element_type=f32)
                                     for a, hs in zip(scores, heads)], axis=1))
    return (jnp.concatenate(outs, axis=0) + jnp.concatenate(o_inter, axis=1), *new)


def _hg_fwd(q, k, v, g, *, name):
    length = q.shape[0]
    rows = _HG_GROUP * HG_CHUNK
    ng = length // rows
    nc = length // HG_CHUNK

    def body(q_ref, k_ref, v_ref, g_ref, o_ref, st_ref, state):
        @pl.when(pl.program_id(0) == 0)
        def _():
            state[...] = jnp.zeros_like(state)

        states = [state[h] for h in range(HG_HEADS)]
        for ci in range(_HG_GROUP):
            sl = slice(ci * HG_CHUNK, (ci + 1) * HG_CHUNK)
            for h in range(HG_HEADS):
                st_ref[h, ci] = states[h]
            o, *states = _hg_chunk(q_ref[sl, :], k_ref[sl, :], v_ref[sl, :], g_ref[sl, :], *states)
            o_ref[sl, :] = o
        for h in range(HG_HEADS):
            state[h] = states[h]

    blk = pl.BlockSpec((rows, HG_WIDTH), lambda c: (c, 0))
    return pl.pallas_call(
        body, grid=(ng,), in_specs=[blk] * 4,
        out_specs=[blk, pl.BlockSpec((HG_HEADS, _HG_GROUP, HG_DIM, HG_DIM), lambda c: (0, c, 0, 0))],
        out_shape=[jax.ShapeDtypeStruct((length, HG_WIDTH), f32),
                   jax.ShapeDtypeStruct((HG_HEADS, nc, HG_DIM, HG_DIM), f32)],
        scratch_shapes=[pltpu.VMEM((HG_HEADS, HG_DIM, HG_DIM), f32)],
        compiler_params=_cparams(("arbitrary",)), name=name)(q, k, v, g)


def _hg_bwd(q, k, v, g, states, do, *, name):
    length = q.shape[0]
    rows = _HG_GROUP * HG_CHUNK
    ng = length // rows

    def body(q_ref, k_ref, v_ref, g_ref, st_ref, do_ref, dq_ref, dk_ref, dv_ref, dg_ref, dstate):
        @pl.when(pl.program_id(0) == 0)
        def _():
            dstate[...] = jnp.zeros_like(dstate)

        dstates = [dstate[h] for h in range(HG_HEADS)]
        for ci in reversed(range(_HG_GROUP)):
            sl = slice(ci * HG_CHUNK, (ci + 1) * HG_CHUNK)
            _, vjp = jax.vjp(_hg_chunk, q_ref[sl, :], k_ref[sl, :], v_ref[sl, :], g_ref[sl, :],
                             *[st_ref[h, ci] for h in range(HG_HEADS)])
            dq, dk, dv, dg, *dstates = vjp((do_ref[sl, :], *dstates))
            dq_ref[sl, :] = dq
            dk_ref[sl, :] = dk
            dv_ref[sl, :] = dv
            dg_ref[sl, :] = dg
        for h in range(HG_HEADS):
            dstate[h] = dstates[h]

    blk = pl.BlockSpec((rows, HG_WIDTH), lambda c: (ng - 1 - c, 0))
    sds = jax.ShapeDtypeStruct((length, HG_WIDTH), f32)
    return pl.pallas_call(
        body, grid=(ng,),
        in_specs=[blk] * 4 + [pl.BlockSpec((HG_HEADS, _HG_GROUP, HG_DIM, HG_DIM), lambda c: (0, ng - 1 - c, 0, 0)), blk],
        out_specs=[blk] * 4, out_shape=[sds] * 4,
        scratch_shapes=[pltpu.VMEM((HG_HEADS, HG_DIM, HG_DIM), f32)],
        compiler_params=_cparams(("arbitrary",)), name=name)(q, k, v, g, states, do)


_ATT_BLK = 512
_ATT_SCALE = MLA_QK ** -0.5
_NEG = -1e30


def _att_mask(i, j, t):
    rows = i * t + lax.broadcasted_iota(jnp.int32, (t, t), 0)
    cols = j * t + lax.broadcasted_iota(jnp.int32, (t, t), 1)
    return cols <= rows


def _att_fwd(q, k, v, *, name):
    length = q.shape[0]
    t = min(_ATT_BLK, length)
    nq = length // t
    qw, vw = MLA_QK_PAD, MLA_V
    heads = range(MLA_HEADS)

    def body(q_ref, k_ref, v_ref, o_ref, lse_ref):
        i = pl.program_id(0)
        qbs = [q_ref[:, h * qw:(h + 1) * qw] for h in heads]

        def step(j, carry, diagonal=False):
            off = pl.multiple_of(j * t, t)
            out = []
            for h in heads:
                m, l, acc = carry[h]
                ks = k_ref[pl.ds(off, t), h * qw:(h + 1) * qw]
                vs = v_ref[pl.ds(off, t), h * vw:(h + 1) * vw]
                s = lax.dot_general(qbs[h], ks, _NT, preferred_element_type=f32) * _ATT_SCALE
                if diagonal:
                    s = jnp.where(_att_mask(i, j, t), s, _NEG)
                m_new = jnp.maximum(m, jnp.max(s, axis=-1, keepdims=True))
                alpha = jnp.exp(m - m_new)
                p = jnp.exp(s - m_new)
                l = alpha * l + jnp.sum(p, axis=-1, keepdims=True)
                acc = alpha * acc + jnp.dot(p.astype(bf16), vs, preferred_element_type=f32)
                out.append((m_new, l, acc))
            return tuple(out)

        init = tuple((jnp.full((t, 1), _NEG, f32), jnp.zeros((t, 1), f32), jnp.zeros((t, vw), f32)) for _ in heads)
        res = step(i, lax.fori_loop(0, i, step, init), diagonal=True)
        for h in heads:
            m, l, acc = res[h]
            o_ref[:, h * vw:(h + 1) * vw] = (acc / l).astype(o_ref.dtype)
            lse_ref[:, h * vw:(h + 1) * vw] = jnp.broadcast_to(m + jnp.log(l), (t, vw))

    return pl.pallas_call(
        body, grid=(nq,),
        in_specs=[pl.BlockSpec((t, q.shape[1]), lambda i: (i, 0)), pl.BlockSpec(k.shape, lambda i: (0, 0)),
                  pl.BlockSpec(v.shape, lambda i: (0, 0))],
        out_specs=[pl.BlockSpec((t, v.shape[1]), lambda i: (i, 0))] * 2,
        out_shape=[jax.ShapeDtypeStruct(v.shape, bf16), jax.ShapeDtypeStruct(v.shape, f32)],
        compiler_params=_cparams(("arbitrary",)), name=name)(q, k, v)


def _att_bwd(q, k, v, o, lse, do, *, name):
    length = q.shape[0]
    t = min(_ATT_BLK, length)
    nq = length // t
    qw, vw = MLA_QK_PAD, MLA_V
    heads = range(MLA_HEADS)

    def dq_body(q_ref, k_ref, v_ref, o_ref, lse_ref, do_ref, dq_ref, delta_ref):
        i = pl.program_id(0)
        qbs = [q_ref[:, h * qw:(h + 1) * qw] for h in heads]
        dobs = [do_ref[:, h * vw:(h + 1) * vw] for h in heads]
        lses = [lse_ref[:, h * vw:h * vw + 1] for h in heads]
        deltas = [jnp.sum(dobs[h].astype(f32) * o_ref[:, h * vw:(h + 1) * vw].astype(f32), axis=-1, keepdims=True)
                  for h in heads]

        def step(j, dqs, diagonal=False):
            off = pl.multiple_of(j * t, t)
            out = []
            for h in heads:
                ks = k_ref[pl.ds(off, t), h * qw:(h + 1) * qw]
                vs = v_ref[pl.ds(off, t), h * vw:(h + 1) * vw]
                s = lax.dot_general(qbs[h], ks, _NT, preferred_element_type=f32) * _ATT_SCALE
                p = jnp.exp(s - lses[h])
                if diagonal:
                    p = jnp.where(_att_mask(i, j, t), p, 0.0)
                dp = lax.dot_general(dobs[h], vs, _NT, preferred_element_type=f32)
                ds = p * (dp - deltas[h]) * _ATT_SCALE
                out.append(dqs[h] + jnp.dot(ds.astype(bf16), ks, preferred_element_type=f32))
            return tuple(out)

        dqs = step(i, lax.fori_loop(0, i, step, tuple(jnp.zeros((t, qw), f32) for _ in heads)), diagonal=True)
        for h in heads:
            dq_ref[:, h * qw:(h + 1) * qw] = dqs[h].astype(dq_ref.dtype)
            delta_ref[:, h * vw:(h + 1) * vw] = jnp.broadcast_to(deltas[h], (t, vw))

    qblk = pl.BlockSpec((t, q.shape[1]), lambda i: (i, 0))
    vblk = pl.BlockSpec((t, v.shape[1]), lambda i: (i, 0))
    qfull = pl.BlockSpec(q.shape, lambda i: (0, 0))
    vfull = pl.BlockSpec(v.shape, lambda i: (0, 0))
    dq, delta = pl.pallas_call(
        dq_body, grid=(nq,), in_specs=[qblk, qfull, vfull, vblk, vblk, vblk], out_specs=[qblk, vblk],
        out_shape=[jax.ShapeDtypeStruct(q.shape, bf16), jax.ShapeDtypeStruct(lse.shape, f32)],
        compiler_params=_cparams(("arbitrary",)), name=name + "_dq")(q, k, v, o, lse, do)

    def dkv_body(k_ref, v_ref, q_ref, do_ref, lse_ref, delta_ref, dk_ref, dv_ref):
        j = pl.program_id(0)
        kbs = [k_ref[:, h * qw:(h + 1) * qw] for h in heads]
        vbs = [v_ref[:, h * vw:(h + 1) * vw] for h in heads]

        def step(i, carry, diagonal=False):
            off = pl.multiple_of(i * t, t)
            out = []
            for h in heads:
                dk, dv = carry[h]
                qs = q_ref[pl.ds(off, t), h * qw:(h + 1) * qw]
                dos = do_ref[pl.ds(off, t), h * vw:(h + 1) * vw]
                lse_i = lse_ref[pl.ds(off, t), h * vw:h * vw + 1]
                delta_i = delta_ref[pl.ds(off, t), h * vw:h * vw + 1]
                s = lax.dot_general(qs, kbs[h], _NT, preferred_element_type=f32) * _ATT_SCALE
                p = jnp.exp(s - lse_i)
                if diagonal:
                    p = jnp.where(_att_mask(i, j, t), p, 0.0)
                dv = dv + lax.dot_general(p.astype(bf16), dos, _TN, preferred_element_type=f32)
                dp = lax.dot_general(dos, vbs[h], _NT, preferred_element_type=f32)
                ds = p * (dp - delta_i) * _ATT_SCALE
                dk = dk + lax.dot_general(ds.astype(bf16), qs, _TN, preferred_element_type=f32)
                out.append((dk, dv))
            return tuple(out)

        first = step(j, tuple((jnp.zeros((t, qw), f32), jnp.zeros((t, vw), f32)) for _ in heads), diagonal=True)
        res = lax.fori_loop(j + 1, nq, step, first)
        for h in heads:
            dk_ref[:, h * qw:(h + 1) * qw] = res[h][0].astype(dk_ref.dtype)
            dv_ref[:, h * vw:(h + 1) * vw] = res[h][1].astype(dv_ref.dtype)

    dk, dv = pl.pallas_call(
        dkv_body, grid=(nq,), in_specs=[qblk, vblk, qfull, vfull, vfull, vfull], out_specs=[qblk, vblk],
        out_shape=[jax.ShapeDtypeStruct(k.shape, bf16), jax.ShapeDtypeStruct(v.shape, bf16)],
        compiler_params=_cparams(("arbitrary",)), name=name + "_dkv")(k, v, q, do, lse, delta)
    return dq, dk, dv


_C_Q = 4 * HG_WIDTH
_C_KV = _C_Q + MLA_Q_RANK
_C_KPE = _C_KV + MLA_KV_RANK


def _rms_n(x, g, n):
    return x * lax.rsqrt(jnp.sum(x * x, axis=-1, keepdims=True) * (1.0 / n) + EPS) * g


def _mix_a(proj, l0, l1, q_a_norm, kv_a_norm):
    lb = jax.nn.sigmoid(l0 - l1)
    f = lb + (1.0 - lb) * jax.nn.sigmoid(proj[:, HG_WIDTH:2 * HG_WIDTH])
    qf = _silu(proj[:, :HG_WIDTH])
    v = proj[:, 2 * HG_WIDTH:3 * HG_WIDTH]
    cqn = _rms(proj[:, _C_Q:_C_KV], q_a_norm)
    ckvn = _rms(proj[:, _C_KV:_C_KPE], kv_a_norm)
    return qf, 1.0 - f, v, jnp.log(f), cqn, ckvn


def _mix_b(qraw, kvraw, kpe_raw, cos, sin, qn_nope, qn_rope, kn_nope, kn_rope, perm):
    def rope(x):
        return x * cos + jnp.dot(x, perm, precision=lax.Precision.HIGHEST, preferred_element_type=f32) * sin

    kpe = rope(_rms_n(kpe_raw, kn_rope, MLA_ROPE))
    qs, ks, vs = [], [], []
    for hh in range(MLA_HEADS):
        base = hh * MLA_QK_PAD
        qs.append(_rms(qraw[:, base:base + MLA_NOPE], qn_nope))
        qs.append(rope(_rms_n(qraw[:, base + MLA_NOPE:base + MLA_QK_PAD], qn_rope, MLA_ROPE)))
        ks.append(_rms(kvraw[:, base:base + MLA_NOPE], kn_nope))
        ks.append(kpe)
        vs.append(kvraw[:, base + MLA_NOPE:base + MLA_QK_PAD])
    return jnp.concatenate(qs, axis=-1), jnp.concatenate(ks, axis=-1), jnp.concatenate(vs, axis=-1)


def _mix_c(o_hg, gate, o_mla, hg_out_norm):
    parts = []
    for hh in range(HG_HEADS):
        sl = slice(hh * HG_DIM, (hh + 1) * HG_DIM)
        parts.append(_rms(o_hg[:, sl], hg_out_norm[:, sl]))
    o = jnp.concatenate(parts, axis=-1) * _silu(gate)
    return jnp.concatenate([o, o_mla], axis=-1)


def _rope_perm():
    p = np.zeros((128, 128), np.float32)
    half = MLA_ROPE // 2
    for i in range(half):
        p[i + half, i] = -1.0
        p[i, i + half] = 1.0
    return jnp.asarray(p)


def _mixer_fwd(h, cos, sin, w, tag, next_gain):
    d = h.shape[1]
    hn, = _rows(_rms, [h], [w["norm_mix"]], [(d, bf16)], name=f"mix_norm_{tag}")
    proj = _mm(hn, w["mix_w_in"], name=f"mix_in_{tag}")
    pa = [w["lb0"], w["lb1"], w["mla_q_a_norm"], w["mla_kv_a_norm"]]
    qf, kk, vv, logf, cqn, ckvn = _rows(
        _mix_a, [proj], pa, [(HG_WIDTH, f32)] * 4 + [(MLA_Q_RANK, bf16), (MLA_KV_RANK, bf16)], name=f"mix_a_{tag}")
    o_hg, states = _hg_fwd(qf, kk, vv, logf, name=f"hg_fwd_{tag}")
    qraw = _mm(cqn, w["mla_w_uq"], name=f"mla_uq_{tag}")
    kvraw = _mm(ckvn, w["mla_w_ukv"], name=f"mla_ukv_{tag}")
    pb = [w["mla_qn_nope"], w["mla_qn_rope"], w["mla_kn_nope"], w["mla_kn_rope"], w["rope_perm"]]
    kpe_raw, gate = _window(proj, _C_KPE, IN_PAD - _C_KPE), _window(proj, 3 * HG_WIDTH, HG_WIDTH)
    qfull, kfull, vfull = _rows(_mix_b, [qraw, kvraw, kpe_raw, cos, sin], pb,
                                [(MLA_HEADS * MLA_QK_PAD, bf16)] * 2 + [(MLA_HEADS * MLA_V, bf16)],
                                name=f"mix_b_{tag}")
    o_mla, lse = _att_fwd(qfull, kfull, vfull, name=f"att_fwd_{tag}")
    mixin, = _rows(_mix_c, [o_hg, gate, o_mla], [w["hg_out_norm"]], [(d, bf16)], name=f"mix_c_{tag}")
    if callable(w["mix_w_out"]):
        w["mix_w_out"] = w["mix_w_out"](mixin)
    out, normed = _mm(mixin, w["mix_w_out"], add=h, norm=next_gain, name=f"mix_out_{tag}")
    return out, (h, hn, proj, qf, kk, vv, logf, cqn, ckvn, o_hg, states, qraw, kvraw, qfull, kfull, vfull, o_mla,
                 lse, mixin), normed


def _mixer_bwd(dout, cos, sin, w, saved, tag, on_w_out=None):
    (h, hn, proj, qf, kk, vv, logf, cqn, ckvn, o_hg, states, qraw, kvraw, qfull, kfull, vfull, o_mla, lse,
     mixin) = saved
    g = {}
    dmixin = _mm(dout, w["mix_w_out"], tb=True, name=f"mix_dmixin_{tag}")
    g["mix_w_out"] = _mm(mixin, dout, ta=True, name=f"mix_dwout_{tag}")
    if on_w_out is not None:
        dmixin = on_w_out(g["mix_w_out"], dmixin)
    kpe_raw, gate = _window(proj, _C_KPE, IN_PAD - _C_KPE), _window(proj, 3 * HG_WIDTH, HG_WIDTH)
    (do_hg, dgate, do_mla), (g["hg_out_norm"],) = _rows_bwd(
        _mix_c, [o_hg, gate, o_mla], [w["hg_out_norm"]], [dmixin], rgrad=[f32, f32, bf16], pgrad=[True],
        name=f"mix_dc_{tag}")
    dqfull, dkfull, dvfull = _att_bwd(qfull, kfull, vfull, o_mla, lse, do_mla, name=f"att_bwd_{tag}")
    pb = [w["mla_qn_nope"], w["mla_qn_rope"], w["mla_kn_nope"], w["mla_kn_rope"], w["rope_perm"]]
    (dqraw, dkvraw, dkpe_raw), pg = _rows_bwd(
        _mix_b, [qraw, kvraw, kpe_raw, cos, sin], pb, [dqfull, dkfull, dvfull],
        rgrad=[bf16, bf16, f32, None, None], pgrad=[True, True, True, True, False],
        name=f"mix_db_{tag}")
    g["mla_qn_nope"], g["mla_qn_rope"], g["mla_kn_nope"], g["mla_kn_rope"] = pg
    dcqn = _mm(dqraw, w["mla_w_uq"], tb=True, name=f"mla_dcq_{tag}")
    g["mla_w_uq"] = _mm(cqn, dqraw, ta=True, name=f"mla_dwuq_{tag}")
    dckvn = _mm(dkvraw, w["mla_w_ukv"], tb=True, name=f"mla_dckv_{tag}")
    g["mla_w_ukv"] = _mm(ckvn, dkvraw, ta=True, name=f"mla_dwukv_{tag}")
    dqf, dkk, dvv, dlogf = _hg_bwd(qf, kk, vv, logf, states, do_hg, name=f"hg_bwd_{tag}")
    pa = [w["lb0"], w["lb1"], w["mla_q_a_norm"], w["mla_kv_a_norm"]]
    (dproj,), (g["lb0"], g["lb1"], g["mla_q_a_norm"], g["mla_kv_a_norm"]) = _rows_bwd(
        _mix_a, [proj], pa, [dqf, dkk, dvv, dlogf, dcqn, dckvn], rgrad=[bf16], pgrad=[True] * 4,
        addends={0: [(dgate, 3 * HG_WIDTH), (dkpe_raw, _C_KPE)]}, name=f"mix_da_{tag}")
    dh, g["norm_mix"] = _mm(dproj, w["mix_w_in"], tb=True, rms_bwd=(h, w["norm_mix"], dout), name=f"mix_dhn_{tag}")
    g["mix_w_in_t"] = _mm(dproj, hn, ta=True, name=f"mix_dwin_{tag}")
    return dh, g


def _rope_tables(positions):
    inv_freq = 1.0 / (ROPE_BASE ** (jnp.arange(0, MLA_ROPE, 2, dtype=f32) / MLA_ROPE))
    ang = positions.astype(f32)[:, None] * inv_freq
    z = jnp.zeros((positions.shape[0], 128 - MLA_ROPE), f32)
    return (jnp.concatenate([jnp.cos(ang), jnp.cos(ang), z], axis=1),
            jnp.concatenate([jnp.sin(ang), jnp.sin(ang), z], axis=1))


def _pad_cols(a, n):
    return jnp.pad(a, ((0, 0), (0, n - a.shape[1])))


def _even_weights(p, j, layer, dt):
    w_uq = p["mla_w_uq"][j].reshape(MLA_Q_RANK, MLA_HEADS, MLA_QK)
    w_uq = jnp.pad(w_uq, ((0, 0), (0, 0), (0, MLA_QK_PAD - MLA_QK))).reshape(MLA_Q_RANK, MLA_HEADS * MLA_QK_PAD)
    return dict(
        norm_mix=p["norm_mix"][layer][None], mix_w_in=_pad_cols(p["mix_w_in"][j], IN_PAD).astype(dt),
        lb0=p["hg_lb_logits"][0][None], lb1=p["hg_lb_logits"][1][None],
        mla_q_a_norm=p["mla_q_a_norm"][j][None], mla_kv_a_norm=p["mla_kv_a_norm"][j][None],
        mla_w_uq=w_uq.astype(dt), mla_w_ukv=p["mla_w_ukv"][j].astype(dt),
        mla_qn_nope=p["mla_qn_nope"][j][None], mla_qn_rope=_pad_cols(p["mla_qn_rope"][j][None], 128),
        mla_kn_nope=p["mla_kn_nope"][j][None], mla_kn_rope=_pad_cols(p["mla_kn_rope"][j][None], 128),
        rope_perm=_rope_perm(), hg_out_norm=p["hg_out_norm"][j][None],
        mix_w_out=p["mix_w_out"][j].astype(dt) if "mix_w_out" in p else None)


def _even_grads(g):
    w_uq = g["mla_w_uq"].reshape(MLA_Q_RANK, MLA_HEADS, MLA_QK_PAD)[:, :, :MLA_QK].reshape(MLA_Q_RANK, -1)
    return dict(
        norm_mix=g["norm_mix"], mix_w_in=g["mix_w_in_t"][:IN_WIDTH].T[None],
        hg_lb_logits=jnp.concatenate([g["lb0"], g["lb1"]], axis=0),
        mla_q_a_norm=g["mla_q_a_norm"], mla_kv_a_norm=g["mla_kv_a_norm"], mla_w_uq=w_uq[None],
        mla_w_ukv=g["mla_w_ukv"][None], mla_qn_nope=g["mla_qn_nope"], mla_qn_rope=g["mla_qn_rope"][:, :MLA_ROPE],
        mla_kn_nope=g["mla_kn_nope"], mla_kn_rope=g["mla_kn_rope"][:, :MLA_ROPE],
        hg_out_norm=g["hg_out_norm"], mix_w_out=g["mix_w_out"][None])


_S5_NB = 8
_S5_BW = 1024
_S5_HALF = 512
_S5_UC = 128
_S5_TIME = 512


def _cmul(ar, ai, br, bi):
    return ar * br - ai * bi, ar * bi + ai * br


def _pow_table(ar, ai, descending):
    rows = lax.broadcasted_iota(jnp.int32, (8, ar.shape[1]), 0)
    tr = jnp.zeros((8, ar.shape[1]), f32)
    ti = jnp.zeros((8, ar.shape[1]), f32)
    pr, pi_ = ar, ai
    for r in range(8):
        sel = rows == ((7 - r) if descending else r)
        tr = jnp.where(sel, pr, tr)
        ti = jnp.where(sel, pi_, ti)
        pr, pi_ = _cmul(pr, pi_, ar, ai)
    return tr, ti


def _s5_tile_scan(work, carry, ar, ai, tc, reverse, per_tile=None):
    hw = _S5_HALF
    row8 = lax.broadcasted_iota(jnp.int32, (8, hw), 0)
    powers = [(ar, ai)]
    for _ in range(2):
        powers.append(_cmul(*powers[-1], *powers[-1]))
    steps = []
    for (mr, mi), s in zip(powers, (1, 2, 4)):
        ok = (row8 < 8 - s) if reverse else (row8 >= s)
        steps.append((jnp.where(ok, mr, 0.0), jnp.where(ok, mi, 0.0), 8 - s if reverse else s))
    tr, ti = _pow_table(ar, ai, reverse)
    cr, ci = carry[:, :hw], carry[:, hw:]
    tiles = range(tc // 8)
    for i in (reversed(tiles) if reverse else tiles):
        sl = slice(8 * i, 8 * i + 8)
        xr, xi = work[sl, :hw], work[sl, hw:]
        for mr, mi, shift in steps:
            pr, pi_ = _cmul(mr, mi, pltpu.roll(xr, shift, axis=0), pltpu.roll(xi, shift, axis=0))
            xr, xi = xr + pr, xi + pi_
        pr, pi_ = _cmul(tr, ti, cr, ci)
        xr, xi = xr + pr, xi + pi_
        work[sl, :hw] = xr
        work[sl, hw:] = xi
        if per_tile is not None:
            per_tile(sl, xr, xi, cr, ci)
        edge = 8 * i if reverse else 8 * i + 7
        cr, ci = work[edge:edge + 1, :hw], work[edge:edge + 1, hw:]
    carry[:, :hw] = cr
    carry[:, hw:] = ci


def _s5_core_fwd(a, hn, b3, c3, *, name):
    length = hn.shape[0]
    tc = min(_S5_TIME, length)

    def body(a_ref, hn_ref, b_ref, c_ref, hs_ref, y_ref, work, carry):
        @pl.when(pl.program_id(1) == 0)
        def _():
            carry[...] = jnp.zeros_like(carry)

        work[...] = jnp.dot(hn_ref[...].astype(bf16), b_ref[...], preferred_element_type=f32)
        _s5_tile_scan(work, carry, a_ref[:, :_S5_HALF], a_ref[:, _S5_HALF:], tc, False)
        hs = work[...].astype(bf16)
        hs_ref[...] = hs
        y_ref[...] = jnp.dot(hs, c_ref[...], preferred_element_type=f32)

    return pl.pallas_call(
        body, grid=(_S5_NB, length // tc),
        in_specs=[pl.BlockSpec((1, _S5_BW), lambda j, t: (0, j)), pl.BlockSpec((tc, _S5_UC), lambda j, t: (t, j)),
                  pl.BlockSpec((_S5_UC, _S5_BW), lambda j, t: (j, 0)), pl.BlockSpec((_S5_BW, _S5_UC), lambda j, t: (j, 0))],
        out_specs=[pl.BlockSpec((tc, _S5_BW), lambda j, t: (t, j)), pl.BlockSpec((tc, _S5_UC), lambda j, t: (t, j))],
        out_shape=[jax.ShapeDtypeStruct((length, _S5_NB * _S5_BW), bf16),
                   jax.ShapeDtypeStruct((length, _S5_NB * _S5_UC), f32)],
        scratch_shapes=[pltpu.VMEM((tc, _S5_BW), f32), pltpu.VMEM((1, _S5_BW), f32)],
        compiler_params=_cparams(("parallel", "arbitrary")), name=name)(a, hn, b3, c3)


def _s5_core_bwd(a, dy, c3, hs, hn, b3, *, name):
    length = hn.shape[0]
    tc = min(_S5_TIME, length)
    nt = length // tc
    hw = _S5_HALF

    def body(a_ref, dy_ref, c_ref, hs_ref, hn_ref, b_ref, du_ref, db_ref, dc_ref, da_ref, work, carry, acc):
        @pl.when(pl.program_id(1) == 0)
        def _():
            carry[...] = jnp.zeros_like(carry)
            db_ref[...] = jnp.zeros_like(db_ref)
            dc_ref[...] = jnp.zeros_like(dc_ref)
            da_ref[...] = jnp.zeros_like(da_ref)

        dyb = dy_ref[...].astype(bf16)
        work[...] = lax.dot_general(dyb, c_ref[...], _NT, preferred_element_type=f32)
        acc[...] = jnp.zeros_like(acc)
        row8 = lax.broadcasted_iota(jnp.int32, (8, hw), 0)

        def grad_a(sl, gr, gi, cr, ci):
            gnr = jnp.where(row8 == 7, cr, pltpu.roll(gr, 7, axis=0))
            gni = jnp.where(row8 == 7, ci, pltpu.roll(gi, 7, axis=0))
            hr, hi = hs_ref[sl, :hw].astype(f32), hs_ref[sl, hw:].astype(f32)
            acc[:, :hw] += hr * gnr + hi * gni
            acc[:, hw:] += hr * gni - hi * gnr

        _s5_tile_scan(work, carry, a_ref[:, :hw], -a_ref[:, hw:], tc, True, grad_a)
        da_ref[...] += jnp.sum(acc[...], axis=0, keepdims=True)
        g = work[...].astype(bf16)
        du_ref[...] = lax.dot_general(g, b_ref[...], _NT, preferred_element_type=f32)
        db_ref[...] += lax.dot_general(hn_ref[...].astype(bf16), g, _TN, preferred_element_type=f32)
        dc_ref[...] += lax.dot_general(hs_ref[...], dyb, _TN, preferred_element_type=f32)

    rev = lambda j, t: (nt - 1 - t, j)
    return pl.pallas_call(
        body, grid=(_S5_NB, nt),
        in_specs=[pl.BlockSpec((1, _S5_BW), lambda j, t: (0, j)), pl.BlockSpec((tc, _S5_UC), rev),
                  pl.BlockSpec((_S5_BW, _S5_UC), lambda j, t: (j, 0)), pl.BlockSpec((tc, _S5_BW), rev),
                  pl.BlockSpec((tc, _S5_UC), rev), pl.BlockSpec((_S5_UC, _S5_BW), lambda j, t: (j, 0))],
        out_specs=[pl.BlockSpec((tc, _S5_UC), rev), pl.BlockSpec((_S5_UC, _S5_BW), lambda j, t: (j, 0)),
                   pl.BlockSpec((_S5_BW, _S5_UC), lambda j, t: (j, 0)), pl.BlockSpec((1, _S5_BW), lambda j, t: (0, j))],
        out_shape=[jax.ShapeDtypeStruct((length, _S5_NB * _S5_UC), f32),
                   jax.ShapeDtypeStruct((_S5_NB * _S5_UC, _S5_BW), f32),
                   jax.ShapeDtypeStruct((_S5_NB * _S5_BW, _S5_UC), f32),
                   jax.ShapeDtypeStruct((1, _S5_NB * _S5_BW), f32)],
        scratch_shapes=[pltpu.VMEM((tc, _S5_BW), f32), pltpu.VMEM((1, _S5_BW), f32), pltpu.VMEM((8, _S5_BW), f32)],
        compiler_params=_cparams(("parallel", "arbitrary")), name=name)(a, dy, c3, hs, hn, b3)


def _s5_disc(lr, li, ldt, btr, bti, expand):
    dt = jnp.exp(ldt)
    mag = jnp.exp(lr * dt)
    abr = mag * jnp.cos(li * dt)
    abi = mag * jnp.sin(li * dt)
    den = lr * lr + li * li
    zr = ((abr - 1.0) * lr + abi * li) / den
    zi = (abi * lr - (abr - 1.0) * li) / den
    zr = jnp.dot(zr, expand, precision=lax.Precision.HIGHEST, preferred_element_type=f32)
    zi = jnp.dot(zi, expand, precision=lax.Precision.HIGHEST, preferred_element_type=f32)
    return abr, abi, zr * btr - zi * bti, zr * bti + zi * btr


def _s5_disc_fwd(args, *, name):
    def body(*refs):
        res = _s5_disc(*[r[...] for r in refs[:6]])
        for o, v in zip(refs[6:], res):
            o[...] = v

    sds = jax.ShapeDtypeStruct
    return pl.pallas_call(body, out_shape=[sds(args[0].shape, f32)] * 2 + [sds(args[3].shape, f32)] * 2,
                          name=name)(*args)


def _s5_disc_bwd(args, cts, *, name):
    def body(*refs):
        vals = [r[...] for r in refs[:6]]
        _, vjp = jax.vjp(lambda *d: _s5_disc(*d, vals[5]), *vals[:5])
        grads = vjp(tuple(r[...] for r in refs[6:10]))
        for o, v in zip(refs[10:], grads):
            o[...] = v

    return pl.pallas_call(body, out_shape=[jax.ShapeDtypeStruct(a.shape, f32) for a in args[:5]],
                          name=name)(*args, *cts)


def _gelu_tanh(x):
    return 0.5 * x * (1.0 + jnp.tanh(0.7978845608028654 * (x + 0.044715 * (x * x * x))))


def _s5_post(y, u, d_skip):
    return _gelu_tanh(y + d_skip * u)


def _s5_glu(ga, gb, h):
    return h + ga * jax.nn.sigmoid(gb)


def _s5_glu_norm(ga, gb, h, next_gain):
    out = _s5_glu(ga, gb, h)
    return out, _rms(out, next_gain)


def _s5_expand():
    e = np.zeros((S5_STATE, S5_GROUP * S5_STATE), np.float32)
    for m in range(S5_GROUP):
        e[np.arange(S5_STATE), m * S5_STATE + np.arange(S5_STATE)] = 1.0
    return jnp.asarray(e)


def _s5_pack_b(bbr, bbi):
    eye = jnp.eye(8, dtype=f32)

    def one(bb):
        b5 = bb.reshape(_S5_NB, 8, S5_GROUP, S5_STATE)
        return jnp.einsum("jgmp,gh->jgmhp", b5, eye).reshape(_S5_NB * _S5_UC, _S5_HALF)

    return jnp.concatenate([one(bbr), one(bbi)], axis=1)


def _s5_unpack_b(db3):
    def one(d):
        d5 = d.reshape(_S5_NB, 8, S5_GROUP, 8, S5_STATE)
        return jnp.einsum("jgmgp->jgmp", d5).reshape(S5_GROUPS, S5_GROUP * S5_STATE)

    return one(db3[:, :_S5_HALF]), one(db3[:, _S5_HALF:])


def _s5_pack_c(c_re, c_im):
    eye = jnp.eye(8, dtype=f32)

    def one(c):
        c4 = c.reshape(_S5_NB, 8, S5_GROUP, S5_STATE)
        return jnp.einsum("jgmp,hg->jhpgm", c4, eye).reshape(_S5_NB, _S5_HALF, _S5_UC)

    return jnp.concatenate([one(c_re), -one(c_im)], axis=1).reshape(_S5_NB * _S5_BW, _S5_UC)


def _s5_unpack_c(dc3):
    d = dc3.reshape(_S5_NB, 2, 8, S5_STATE, 8, S5_GROUP)
    dre = jnp.einsum("jgpgm->jgmp", d[:, 0]).reshape(S5_GROUPS, S5_GROUP, S5_STATE)
    dim = -jnp.einsum("jgpgm->jgmp", d[:, 1]).reshape(S5_GROUPS, S5_GROUP, S5_STATE)
    return dre, dim


def _s5_state_row(re, im):
    r = re.reshape(_S5_NB, 1, _S5_HALF)
    i = im.reshape(_S5_NB, 1, _S5_HALF)
    return jnp.concatenate([r, i], axis=2).reshape(1, _S5_NB * _S5_BW)


def _s5_unstate_row(row):
    r = row.reshape(_S5_NB, 2, 8, S5_STATE)
    return r[:, 0].reshape(S5_GROUPS, S5_STATE), r[:, 1].reshape(S5_GROUPS, S5_STATE)


def _s5_fwd(h, w, tag, next_gain):
    d = h.shape[1]
    hn, = _rows(_rms, [h], [w["norm_mix"]], [(d, f32)], name=f"s5_norm_{tag}")
    disc_in = [w["s5_lam_re"], w["s5_lam_im"], w["s5_log_dt"], w["s5_bt_re"], w["s5_bt_im"], w["s5_expand"]]
    abr, abi, bbr, bbi = _s5_disc_fwd(disc_in, name=f"s5_disc_{tag}")
    a_row = _s5_state_row(abr, abi)
    b3 = _s5_pack_b(bbr, bbi).astype(bf16)
    hs, y = _s5_core_fwd(a_row, hn, b3, w["s5_c3"], name=f"s5_core_{tag}")
    yg, = _rows(_s5_post, [y, hn], [w["s5_d"]], [(d, bf16)], name=f"s5_post_{tag}")
    ga = _mm(yg, w["s5_w_glu_a"], name=f"s5_glu_a_{tag}")
    gb = _mm(yg, w["s5_w_glu_b"], name=f"s5_glu_b_{tag}")
    out, normed = _rows(_s5_glu_norm, [ga, gb, h], [next_gain], [(d, f32), (d, bf16)], name=f"s5_glu_{tag}")
    return out, (h, hn, disc_in, a_row, b3, hs, y, yg, ga, gb), normed


def _s5_bwd(dout, w, saved, tag):
    h, hn, disc_in, a_row, b3, hs, y, yg, ga, gb = saved
    g = {}
    (dga, dgb), _ = _rows_bwd(_s5_glu, [ga, gb, h], [], [dout], rgrad=[bf16, bf16, None], pgrad=[],
                              name=f"s5_dglu_{tag}")
    dyg = _mm(dga, w["s5_w_glu_a"], tb=True, name=f"s5_dyg_a_{tag}")
    dyg = _mm(dgb, w["s5_w_glu_b"], tb=True, add=dyg, name=f"s5_dyg_b_{tag}")
    g["s5_w_glu_a"] = _mm(yg, dga, ta=True, name=f"s5_dwa_{tag}")
    g["s5_w_glu_b"] = _mm(yg, dgb, ta=True, name=f"s5_dwb_{tag}")
    (dy, du_skip), (g["s5_d"],) = _rows_bwd(_s5_post, [y, hn], [w["s5_d"]], [dyg], rgrad=[bf16, f32], pgrad=[True],
                                           name=f"s5_dpost_{tag}")
    du, db3, dc3, da_row = _s5_core_bwd(a_row, dy, w["s5_c3"], hs, hn, b3, name=f"s5_dcore_{tag}")
    dabr, dabi = _s5_unstate_row(da_row)
    dbbr, dbbi = _s5_unpack_b(db3)
    g["s5_lam_re"], g["s5_lam_im"], g["s5_log_dt"], g["s5_bt_re"], g["s5_bt_im"] = _s5_disc_bwd(
        disc_in, [dabr, dabi, dbbr, dbbi], name=f"s5_ddisc_{tag}")
    g["s5_c_re"], g["s5_c_im"] = _s5_unpack_c(dc3)
    (dh,), (g["norm_mix"],) = _rows_bwd(_rms_twice, [h], [w["norm_mix"]], [du, du_skip], rgrad=[f32], pgrad=[True],
                                        addends={0: dout}, name=f"s5_dnorm_{tag}")
    return dh, g


def _odd_weights(p, j, layer, dt):
    tr = lambda b: b.transpose(0, 2, 1).reshape(S5_GROUPS, S5_GROUP * S5_STATE)
    return dict(
        norm_mix=p["norm_mix"][layer][None], s5_lam_re=p["s5_lam_re"][j], s5_lam_im=p["s5_lam_im"][j],
        s5_log_dt=p["s5_log_dt"][j][:, None], s5_bt_re=tr(p["s5_b_re"][j]), s5_bt_im=tr(p["s5_b_im"][j]),
        s5_expand=_s5_expand(), s5_c3=_s5_pack_c(p["s5_c_re"][j], p["s5_c_im"][j]).astype(dt),
        **{n: (p[n][j][None] if n == "s5_d" else p[n][j].astype(dt))
           for n in ("s5_d", "s5_w_glu_a", "s5_w_glu_b") if n in p})


def _odd_grads(g):
    tr = lambda b: b.reshape(S5_GROUPS, S5_GROUP, S5_STATE).transpose(0, 2, 1)[None]
    return dict(
        norm_mix=g["norm_mix"], s5_lam_re=g["s5_lam_re"][None], s5_lam_im=g["s5_lam_im"][None],
        s5_log_dt=g["s5_log_dt"][:, 0][None], s5_b_re=tr(g["s5_bt_re"]), s5_b_im=tr(g["s5_bt_im"]),
        s5_c_re=g["s5_c_re"][None], s5_c_im=g["s5_c_im"][None], s5_d=g["s5_d"],
        s5_w_glu_a=g["s5_w_glu_a"][None], s5_w_glu_b=g["s5_w_glu_b"][None])


def _loss_fn(y, t):
    e = y - t
    part = jnp.sum(jnp.sum(e * e, axis=-1, keepdims=True), axis=0, keepdims=True) * (0.5 / y.shape[1])
    return e * (1.0 / y.shape[1]), part


FF_SHARD = 352
FF_SHARD_PAD = 384


def _pad_groups(a, axis):
    axis %= a.ndim
    zeros = jnp.zeros(a.shape[:axis] + (FF_SHARD_PAD - FF_SHARD,) + a.shape[axis + 1:], a.dtype)
    pieces = []
    for g in range(a.shape[axis] // FF_SHARD):
        pieces += [lax.slice_in_dim(a, g * FF_SHARD, (g + 1) * FF_SHARD, axis=axis), zeros]
    return jnp.concatenate(pieces, axis=axis)


def _unpad_groups(a, axis):
    axis %= a.ndim
    pieces = [lax.slice_in_dim(a, g * FF_SHARD_PAD, g * FF_SHARD_PAD + FF_SHARD, axis=axis)
              for g in range(a.shape[axis] // FF_SHARD_PAD)]
    return pieces[0] if len(pieces) == 1 else jnp.concatenate(pieces, axis=axis)


def _layer_weights(p, layer, dt):
    return dict(
        norm_xa=p["norm_xa"][layer][None], norm_mem=p["norm_mem"][layer][None], norm_ffn=p["norm_ffn"][layer][None],
        xa_wq=p["xa_wq"][layer].astype(dt), xa_wk=p["xa_wk"][layer].astype(dt), xa_wv=p["xa_wv"][layer].astype(dt),
        xa_wo=p["xa_wo"][layer].astype(dt), xa_q_norm=p["xa_q_norm"][layer][None],
        xa_k_norm=p["xa_k_norm"][layer][None], ffn_w_up=_pad_groups(p["ffn_w_up"][layer], 1).astype(dt),
        ffn_conv_w=_pad_groups(p["ffn_conv_w"][layer], 1), ffn_conv_b=_pad_groups(p["ffn_conv_b"][layer][None], 1),
        ffn_w_down=_pad_groups(p["ffn_w_down"][layer], 0).astype(dt))


_PER_LAYER = ("norm_xa", "norm_mem", "norm_ffn", "xa_wq", "xa_wk", "xa_wv", "xa_wo", "xa_q_norm", "xa_k_norm",
              "ffn_w_up", "ffn_conv_w", "ffn_conv_b", "ffn_w_down")
_FFN_PADDED = dict(ffn_w_up=1, ffn_conv_w=1, ffn_conv_b=1, ffn_w_down=0)


def _local_step(x, mem, positions, target, p):
    cos, sin = _rope_tables(positions)
    we = _even_weights(p, 0, 0, bf16)
    wo = _odd_weights(p, 0, 1, bf16)
    wl = [_layer_weights(p, layer, bf16) for layer in range(2)]
    loss, dh, g_even, g_odd, gl = _local_core(x, mem, cos, sin, target, we, wo, wl)
    grads = {}
    for n in _PER_LAYER:
        a, b = gl[0][n], gl[1][n]
        if n in _FFN_PADDED:
            a, b = _unpad_groups(a, _FFN_PADDED[n]), _unpad_groups(b, _FFN_PADDED[n])
        grads[n] = jnp.concatenate([a, b], axis=0) if a.shape[0] == 1 else jnp.stack([a, b])
    ge, go = _even_grads(g_even), _odd_grads(g_odd)
    grads["norm_mix"] = jnp.concatenate([ge.pop("norm_mix"), go.pop("norm_mix")], axis=0)
    grads.update(ge)
    grads.update(go)
    return loss, dh, grads


def _local_core(x, mem, cos, sin, target, we, wo, wl):
    h, s_mix0, hx = _mixer_fwd(x, cos, sin, we, "l0", wl[0]["norm_xa"])
    h, s_xa0, hf = _xattn_fwd(h, mem, wl[0], "l0", hx)
    h, s_ff0 = _ffn_fwd(h, wl[0], "l0", hf)
    h, s_mix1, hx = _s5_fwd(h, wo, "l1", wl[1]["norm_xa"])
    h, s_xa1, hf = _xattn_fwd(h, mem, wl[1], "l1", hx)
    h, s_ff1 = _ffn_fwd(h, wl[1], "l1", hf)
    dh, loss = _rows(_loss_fn, [h, target], [], [(h.shape[1], f32)], accs=[(1, 1)], name="loss_head")

    gl = [{}, {}]
    dh, g = _ffn_bwd(dh, wl[1], s_ff1, "l1")
    gl[1].update(g)
    dh, g = _xattn_bwd(dh, mem, wl[1], s_xa1, "l1")
    gl[1].update(g)
    dh, g_odd = _s5_bwd(dh, wo, s_mix1, "l1")
    dh, g = _ffn_bwd(dh, wl[0], s_ff0, "l0")
    gl[0].update(g)
    dh, g = _xattn_bwd(dh, mem, wl[0], s_xa0, "l0")
    gl[0].update(g)
    dh, g_even = _mixer_bwd(dh, cos, sin, we, s_mix0, "l0")
    return loss, dh, g_even, g_odd, gl


_LANES = 1024
_ROW_PAD = 16


_PEER_MASKS = (1, 2, 4, 3, 5, 6, 7)


def _mesh_place():
    x, y, c = lax.axis_index("x"), lax.axis_index("y"), lax.axis_index("c")

    def peer(mask):
        px = 1 - x if mask & 4 else x
        py = 1 - y if mask & 2 else y
        pc = 1 - c if mask & 1 else c
        return (px, py, pc), 4 * px + 2 * py + pc

    return 4 * x + 2 * y + c, peer


class _Exchange:
    def __init__(self, name):
        self.name = name
        self.srcs, self.shapes, self.items, self.where = [], [], [], {}

    def add(self, src, land_shape, src_at, dst_at, key):
        si = next((i for i, s in enumerate(self.srcs) if s is src), None)
        if si is None:
            self.srcs.append(src)
            si = len(self.srcs) - 1
        if key not in self.where:
            self.shapes.append(land_shape)
            self.where[key] = len(self.shapes) - 1
        self.items.append(dict(src=si, dst=self.where[key], src_at=src_at, dst_at=dst_at))

    def _copy(self, k, mask, ins, lands, send_sems, recv_sems, me, peer, arriving):
        it = self.items[k]
        dev, idx = peer(mask)
        s = k * (N_DEV - 1) + _PEER_MASKS.index(mask)
        return pltpu.make_async_remote_copy(
            src_ref=it["src_at"](ins[it["src"]], idx), dst_ref=it["dst_at"](lands[it["dst"]], idx if arriving else me),
            send_sem=send_sems.at[s], recv_sem=recv_sems.at[s], device_id=dev, device_id_type=pl.DeviceIdType.MESH)

    def _own_copy(self, k, ins, lands, own_sems, me):
        it = self.items[k]
        return pltpu.make_async_copy(it["src_at"](ins[it["src"]], me), it["dst_at"](lands[it["dst"]], me), own_sems.at[k])

    def begin(self, own):
        ns, nd, ni = len(self.srcs), len(self.shapes), len(self.items)
        nsem = ni * (N_DEV - 1)
        self.own = own

        nq = 3 if own else 2

        def body(*refs):
            ins, land_refs = refs[:ns], refs[ns:ns + nd]
            sems, token = refs[ns + nd:ns + nd + nq], refs[-1]
            me, peer = _mesh_place()
            for mask in _PEER_MASKS:
                for k in range(ni):
                    self._copy(k, mask, ins, land_refs, sems[0], sems[1], me, peer, False).start()
            if own:
                for k in range(ni):
                    self._own_copy(k, ins, land_refs, sems[2], me).start()
            token[...] = jnp.zeros_like(token)

        hbm = pl.BlockSpec(memory_space=pltpu.HBM)
        sem = pl.BlockSpec(memory_space=pltpu.SEMAPHORE)
        lands = [lax.empty(s.shape, s.dtype) for s in self.shapes]
        sem_shapes = [pltpu.SemaphoreType.DMA((nsem,)), pltpu.SemaphoreType.DMA((nsem,)), pltpu.SemaphoreType.DMA((ni,))]
        res = pl.pallas_call(
            body, in_specs=[hbm] * (ns + nd),
            out_specs=[sem] * nq + [hbm] * nd + [pl.BlockSpec(memory_space=pltpu.VMEM)],
            out_shape=sem_shapes[:nq] + [pltpu.HBM(s.shape, s.dtype) for s in self.shapes]
            + [jax.ShapeDtypeStruct((8, 128), f32)],
            input_output_aliases={ns + j: nq + j for j in range(nd)},
            compiler_params=pltpu.CompilerParams(has_side_effects=pltpu.SideEffectType.DATAFLOW_SIDE_EFFECTING),
            name=self.name + "_start")(*self.srcs, *lands)
        self.token = res[-1]
        return list(res[:nq]), list(res[nq:-1])

    def finish(self, state, after):
        sems, lands = state
        nq = len(sems)
        after = list(after) if isinstance(after, (list, tuple)) else [after]
        ns, nd, ni = len(self.srcs), len(self.shapes), len(self.items)

        def body(*refs):
            ins, land_refs = refs[:ns], refs[ns:ns + nd]
            sem_refs = refs[ns + nd:ns + nd + nq]
            me, peer = _mesh_place()
            for mask in _PEER_MASKS:
                for k in range(ni):
                    cp = self._copy(k, mask, ins, land_refs, sem_refs[0], sem_refs[1], me, peer, True)
                    cp.wait_send()
                    cp.wait_recv()
            if self.own:
                for k in range(ni):
                    self._own_copy(k, ins, land_refs, sem_refs[2], me).wait()

        hbm = pl.BlockSpec(memory_space=pltpu.HBM)
        sem = pl.BlockSpec(memory_space=pltpu.SEMAPHORE)
        res = pl.pallas_call(
            body, in_specs=[hbm] * (ns + nd) + [sem] * nq + [pl.BlockSpec(memory_space=pl.ANY)] * len(after),
            out_specs=[hbm] * nd, out_shape=[pltpu.HBM(s.shape, s.dtype) for s in self.shapes],
            input_output_aliases={ns + j: j for j in range(nd)},
            compiler_params=pltpu.CompilerParams(has_side_effects=pltpu.SideEffectType.DATAFLOW_SIDE_EFFECTING),
            name=self.name + "_wait")(*self.srcs, *lands, *sems, *after)
        return {k: res[i] for k, i in self.where.items()}


def _after(x, *tokens, name):
    def body(*refs):
        del refs

    anyspace = pl.BlockSpec(memory_space=pl.ANY)
    return pl.pallas_call(body, in_specs=[anyspace] * (1 + len(tokens)), out_specs=anyspace,
                          out_shape=jax.ShapeDtypeStruct(x.shape, x.dtype), input_output_aliases={0: 0},
                          name=name)(x, *tokens)


def _rows_of(n):
    return lambda r, i: r.at[pl.ds(pl.multiple_of(i * n, n), n), :]


def _cols_of(n):
    return lambda r, i: r.at[:, pl.ds(pl.multiple_of(i * n, n), n)]


def _whole(r, i):
    return r


def _slot(r, i):
    return r.at[i]


def _at_layer(layer):
    return lambda r, i: r.at[layer]


def _sum_adam(me_index, slots, owns, own_block, w, m, v, *, name):
    layers, rows, cols = w.shape
    tr = _pick(rows, (256, 128, 104, 64, 32, 16, 8))
    bc1 = 1.0 - ADAM_B1 ** ADAM_STEP
    bc2 = 1.0 - ADAM_B2 ** ADAM_STEP
    own_shape, own_map = own_block(tr)
    nl = len(slots)
    assert nl == layers and len(owns) == layers

    def body(me_ref, *refs):
        s_refs, own_refs = refs[:nl], refs[nl:2 * nl]
        w_ref, m_ref, v_ref, g_ref, d_ref, nm_ref, nv_ref = refs[2 * nl:]
        me = me_ref[0]

        def run(s_ref, own_ref):
            mine = own_ref[0] if len(own_shape) == 3 else own_ref[...]
            g = jnp.where(me == 0, mine, s_ref[0])
            for k in range(1, N_DEV):
                g = g + jnp.where(me == k, mine, s_ref[k])
            mm = ADAM_B1 * m_ref[0] + (1.0 - ADAM_B1) * g
            vv = ADAM_B2 * v_ref[0] + (1.0 - ADAM_B2) * (g * g)
            g_ref[0] = g
            nm_ref[0] = mm
            nv_ref[0] = vv
            d_ref[0] = -ADAM_LR * ((mm / bc1) / (jnp.sqrt(vv / bc2) + ADAM_EPS) + ADAM_WD * w_ref[0])

        for layer in range(nl):
            pl.when(pl.program_id(0) == layer)(functools.partial(run, s_refs[layer], own_refs[layer]))

    def of_layer(layer, index_map):
        return lambda lyr, i, me: index_map(jnp.where(lyr == layer, i, 0), me)

    blk = pl.BlockSpec((1, tr, cols), lambda lyr, i, me: (lyr, i, 0))
    sds = jax.ShapeDtypeStruct((layers, rows, cols), f32)
    grid_spec = pltpu.PrefetchScalarGridSpec(
        num_scalar_prefetch=1, grid=(layers, rows // tr),
        in_specs=[pl.BlockSpec((N_DEV, tr, cols), of_layer(layer, lambda i, me: (0, i, 0))) for layer in range(nl)]
        + [pl.BlockSpec(own_shape, of_layer(layer, own_map)) for layer in range(nl)] + [blk, blk, blk],
        out_specs=[blk] * 4)
    return pl.pallas_call(body, grid_spec=grid_spec, out_shape=[sds] * 4,
                          compiler_params=_cparams(("arbitrary", "arbitrary")),
                          name=name)(me_index, *slots, *owns, w, m, v)


_SHARDED = dict(xa_wq=1, xa_wk=1, xa_wv=1, xa_wo=1, ffn_w_up=2, ffn_conv_w=2, ffn_w_down=1, mix_w_in=2, mla_w_uq=2,
                mla_w_ukv=2, mix_w_out=1, s5_d=1, s5_w_glu_a=1, s5_w_glu_b=1)
_EXACT = ("ffn_conv_w", "s5_d")
_WEIGHTS = ("norm_mix", "norm_xa", "norm_mem", "norm_ffn", "xa_wq", "xa_wk", "xa_wv", "xa_wo", "xa_q_norm",
            "xa_k_norm", "ffn_w_up", "ffn_conv_w", "ffn_conv_b", "ffn_w_down", "hg_lb_logits", "mix_w_in",
            "hg_out_norm", "mla_q_a_norm", "mla_w_uq", "mla_kv_a_norm", "mla_w_ukv", "mla_qn_nope", "mla_qn_rope",
            "mla_kn_nope", "mla_kn_rope", "mix_w_out", "s5_lam_re", "s5_lam_im", "s5_log_dt", "s5_b_re", "s5_b_im",
            "s5_c_re", "s5_c_im", "s5_d", "s5_w_glu_a", "s5_w_glu_b")
_BIG = tuple(n for n in _WEIGHTS if n in _SHARDED and n not in _EXACT)
_SHARD_ORDER = tuple(n for n in _WEIGHTS if n in _SHARDED)
_REPL_ORDER = tuple(n for n in _WEIGHTS if n not in _SHARDED)
_REPL_EARLY = tuple(n for n in _REPL_ORDER if n.startswith("s5_"))
_REPL_LATE = tuple(n for n in _REPL_ORDER if n not in _REPL_EARLY)


def _pack(parts, dtype, lead=None):
    nl = 0 if lead is None else 1
    flat = [a.astype(dtype).reshape(a.shape[:nl] + (-1,)) for a in parts]
    cat = jnp.concatenate(flat, axis=nl)
    n = cat.shape[nl]
    unit = _LANES * _ROW_PAD
    total = -(-n // unit) * unit
    cat = jnp.pad(cat, [(0, 0)] * nl + [(0, total - n)])
    return cat.reshape(cat.shape[:nl] + (total // _LANES, _LANES))


def _unpack(packed, shapes, lead=None):
    nl = 0 if lead is None else 1
    flat = packed.reshape(packed.shape[:nl] + (-1,))
    out, off = [], 0
    for s in shapes:
        n = int(np.prod(s))
        piece = flat[..., off:off + n] if nl else flat[off:off + n]
        out.append(piece.reshape(packed.shape[:nl] + tuple(s)))
        off += n
    return out


def _to_full(gathered, axis):
    g = jnp.moveaxis(gathered, 0, axis)
    s = g.shape
    return g.reshape(s[:axis] + (s[axis] * s[axis + 1],) + s[axis + 2:])


def _to_shards(full, axis):
    s = full.shape
    g = full.reshape(s[:axis] + (N_DEV, s[axis] // N_DEV) + s[axis + 1:])
    return jnp.moveaxis(g, axis, 0)


_DIRECT_ROWS = ("xa_wq", "xa_wk", "xa_wv", "xa_wo", "mix_w_out", "s5_w_glu_a", "s5_w_glu_b")
_SMALL16 = ("mix_w_in", "mla_w_uq", "mla_w_ukv")
_SMALL_SHARDED = ("mla_w_uq", "mla_w_ukv") + _EXACT
_SHARD_ROWS = 128


def _exchange_layout(d):
    out = dict(d)
    out["ffn_w_up"] = _pad_groups(d["ffn_w_up"], 2)
    out["ffn_conv_w"] = _pad_groups(d["ffn_conv_w"], 2)
    out["ffn_w_down"] = _pad_groups(d["ffn_w_down"], 1)
    return out


def _train_step(x, mem, positions, target, w, m, v):
    d_model = x.shape[1]
    we_, me_, ve_ = _exchange_layout(w), _exchange_layout(m), _exchange_layout(v)
    sds = jax.ShapeDtypeStruct

    matrices = _DIRECT_ROWS + ("ffn_w_up", "ffn_w_down")
    layer_mats = ("xa_wq", "xa_wk", "xa_wv", "xa_wo", "ffn_w_up", "ffn_w_down")
    shard16 = {n: we_[n].astype(bf16) for n in matrices}
    part_of = {n: _rows_of(_SHARD_ROWS) for n in _DIRECT_ROWS}
    part_of["ffn_w_up"] = _cols_of(we_["ffn_w_up"].shape[2])
    part_of["ffn_w_down"] = _rows_of(we_["ffn_w_down"].shape[1])
    part_shape = {n: we_[n].shape[1:] for n in matrices}

    def full_shape(n):
        r, c = part_shape[n]
        return (r, N_DEV * c) if n == "ffn_w_up" else (N_DEV * r, c)

    def gather(ex, n, layer):
        ex.add(shard16[n], sds(full_shape(n), bf16), _at_layer(layer), part_of[n], (n, layer))

    def scatter(ex, n, layer, grad):
        ex.add(grad, sds((N_DEV,) + part_shape[n], f32), part_of[n], _slot, (n, layer))

    small16 = _pack([we_[n] for n in _SMALL16], bf16)
    exact = _pack([we_[n] for n in _EXACT], f32)
    ga, ga1, gb, gc = _Exchange("gather_a"), _Exchange("gather_a1"), _Exchange("gather_b"), _Exchange("gather_c")
    ga.add(small16, sds((N_DEV,) + small16.shape, bf16), _whole, _slot, "small16")
    ga1.add(exact, sds((N_DEV,) + exact.shape, f32), _whole, _slot, "exact")
    gather(ga1, "mix_w_out", 0)
    for n in layer_mats:
        gather(gb, n, 0)
    gather(gc, "s5_w_glu_a", 0)
    gather(gc, "s5_w_glu_b", 0)
    for n in layer_mats:
        gather(gc, n, 1)
    state_a, state_a1, state_b, state_c = ga.begin(True), ga1.begin(True), gb.begin(True), gc.begin(True)

    p = {n: w[n] for n in _REPL_ORDER}
    cos, sin = _rope_tables(positions)
    wo = _odd_weights(p, 0, 1, bf16)
    conv_b = _pad_groups(w["ffn_conv_b"], 1)
    prepared = [cos, sin, conv_b, wo["s5_c3"], wo["s5_bt_re"], wo["s5_bt_im"]]
    full = ga.finish(state_a, [ga1.token, gb.token, gc.token] + prepared)
    for n, a in zip(_SMALL16, _unpack(full["small16"], [we_[n].shape for n in _SMALL16], lead=True)):
        p[n] = _to_full(a, _SHARDED[n])
    we = _even_weights(p, 0, 0, bf16)
    we["norm_mix"] = _after(we["norm_mix"], ga.token, ga1.token, gb.token, gc.token, name="after_gather_starts")

    def late_mix_w_out(mixin):
        full.update(ga1.finish(state_a1, [mixin]))
        return full[("mix_w_out", 0)]

    we["mix_w_out"] = late_mix_w_out
    h, s_mix0, hx = _mixer_fwd(x, cos, sin, we, "l0", w["norm_xa"][0][None])
    conv_w, s5_d = [_to_full(a, _SHARDED[n]) for n, a in
                    zip(_EXACT, _unpack(full["exact"], [we_[n].shape for n in _EXACT], lead=True))]

    def layer_weights(layer):
        return dict(norm_xa=w["norm_xa"][layer][None], norm_mem=w["norm_mem"][layer][None],
                    norm_ffn=w["norm_ffn"][layer][None], xa_q_norm=w["xa_q_norm"][layer][None],
                    xa_k_norm=w["xa_k_norm"][layer][None], ffn_conv_w=conv_w[layer],
                    ffn_conv_b=conv_b[layer][None], **{n: full[(n, layer)] for n in layer_mats})

    full.update(gb.finish(state_b, h))
    wl = [layer_weights(0)]
    h, s_xa0, hf = _xattn_fwd(h, mem, wl[0], "l0", hx)
    h, s_ff0 = _ffn_fwd(h, wl[0], "l0", hf)
    full.update(gc.finish(state_c, h))
    wl.append(layer_weights(1))
    wo.update(s5_d=s5_d, s5_w_glu_a=full[("s5_w_glu_a", 0)], s5_w_glu_b=full[("s5_w_glu_b", 0)])
    h, s_mix1, hx = _s5_fwd(h, wo, "l1", wl[1]["norm_xa"])
    h, s_xa1, hf = _xattn_fwd(h, mem, wl[1], "l1", hx)
    h, s_ff1 = _ffn_fwd(h, wl[1], "l1", hf)
    dh, loss = _rows(_loss_fn, [h, target], [], [(h.shape[1], f32)], accs=[(1, 1)], name="loss_head")

    gl = [{}, {}]
    reduces = []

    own_grad = {}

    def reduce_start(name, entries, dh):
        ex = _Exchange(name)
        for n, layer, grad in entries.get("matrices", ()):
            scatter(ex, n, layer, grad)
            own_grad[(n, layer)] = grad
        for key, src, shape, src_at in entries.get("packs", ()):
            ex.add(src, shape, src_at, _slot, key)
        reduces.append((ex, ex.begin(False)))
        return _after(dh, ex.token, name="after_" + name)

    dh, gl[1] = _ffn_bwd(dh, wl[1], s_ff1, "l1")
    dh = reduce_start("reduce_ffn1", dict(matrices=[(n, 1, gl[1][n]) for n in ("ffn_w_up", "ffn_w_down")]), dh)
    dh, g = _xattn_bwd(dh, mem, wl[1], s_xa1, "l1")
    gl[1].update(g)
    dh = reduce_start("reduce_xa1", dict(matrices=[(n, 1, g[n]) for n in ("xa_wq", "xa_wk", "xa_wv", "xa_wo")]), dh)
    dh, g_odd = _s5_bwd(dh, wo, s_mix1, "l1")
    go = _odd_grads(g_odd)
    dh, gl[0] = _ffn_bwd(dh, wl[0], s_ff0, "l0")
    send_early = _pack([go[n].reshape(w[n].shape) for n in _REPL_EARLY], f32)
    dh = reduce_start("reduce_ffn0", dict(
        matrices=[(n, 0, g_odd[n]) for n in ("s5_w_glu_a", "s5_w_glu_b")]
        + [(n, 0, gl[0][n]) for n in ("ffn_w_up", "ffn_w_down")],
        packs=[("repl_early", send_early, sds((N_DEV,) + send_early.shape, f32), _whole)]), dh)
    dh, g = _xattn_bwd(dh, mem, wl[0], s_xa0, "l0")
    gl[0].update(g)
    dh = reduce_start("reduce_xa0", dict(matrices=[(n, 0, g[n]) for n in ("xa_wq", "xa_wk", "xa_wv", "xa_wo")]), dh)
    grad_x, g_even = _mixer_bwd(
        dh, cos, sin, we, s_mix0, "l0",
        on_w_out=lambda grad, dmixin: reduce_start("reduce_w_out", dict(matrices=[("mix_w_out", 0, grad)]), dmixin))

    ge = _even_grads(g_even)
    cat = lambda n: jnp.concatenate([gl[0][n], gl[1][n]], axis=0)
    rg = dict(ge)
    rg["norm_mix"] = jnp.concatenate([ge["norm_mix"], go["norm_mix"]], axis=0)
    for n in ("norm_xa", "norm_mem", "norm_ffn", "xa_q_norm", "xa_k_norm"):
        rg[n] = cat(n)
    rg["ffn_conv_b"] = _unpad_groups(cat("ffn_conv_b"), 1)
    sg = dict(mla_w_uq=ge["mla_w_uq"], mla_w_ukv=ge["mla_w_ukv"], s5_d=go["s5_d"],
              ffn_conv_w=jnp.stack([gl[0]["ffn_conv_w"], gl[1]["ffn_conv_w"]]))
    send_small = _pack([_to_shards(sg[n], _SHARDED[n]) for n in _SMALL_SHARDED], f32, lead=True)
    send_late = _pack([rg[n].reshape(w[n].shape) for n in _REPL_LATE], f32)
    w_in_rows = w["mix_w_in"].shape[2]
    last = _Exchange("reduce_last")
    last.add(g_even["mix_w_in_t"], sds((N_DEV, w_in_rows, d_model), f32), _rows_of(w_in_rows), _slot, "mix_w_in")
    last.add(send_small, sds(send_small.shape, f32), _slot, _slot, "small")
    last.add(send_late, sds((N_DEV,) + send_late.shape, f32), _whole, _slot, "repl_late")
    state_last = last.begin(False)
    slots = {}
    for ex, state in reduces:
        slots.update(ex.finish(state, [grad_x, last.token]))

    me_index = (4 * lax.axis_index("x") + 2 * lax.axis_index("y") + lax.axis_index("c")).astype(jnp.int32).reshape(1)

    def rows_block(r, c):
        return lambda tr: ((tr, c), lambda i, me: (me[0] * (r // tr) + i, 0))

    def own_block(n):
        r, c = part_shape[n]
        if n == "ffn_w_up":
            return lambda tr: ((tr, c), lambda i, me: (i, me[0]))
        return rows_block(r, c)

    out = [{}, {}, {}, {}]
    unpad = dict(ffn_w_up=2, ffn_conv_w=2, ffn_w_down=1)
    for n in matrices:
        layers = range(we_[n].shape[0])
        res = _sum_adam(me_index, [slots[(n, layer)] for layer in layers], [own_grad[(n, layer)] for layer in layers],
                        own_block(n), we_[n], me_[n], ve_[n], name=f"adam_{n}")
        for k in range(4):
            out[k][n] = _unpad_groups(res[k], unpad[n]) if n in unpad else res[k]
    pk = lambda d, order: _pack([d[n] for n in order], f32)[None]
    whole_rows = lambda tr: ((tr, _LANES), lambda i, me: (i, 0))
    res_early = _sum_adam(me_index, [slots["repl_early"]], [send_early], whole_rows, pk(w, _REPL_EARLY),
                          pk(m, _REPL_EARLY), pk(v, _REPL_EARLY), name="adam_repl_early")
    for k in range(4):
        out[k].update(zip(_REPL_EARLY, _unpack(res_early[k][0], [w[n].shape for n in _REPL_EARLY])))
    done = [out[k][n] for k in range(4) for n in matrices + _REPL_EARLY]
    slots = last.finish(state_last, done)
    transposed = lambda d: jnp.swapaxes(d["mix_w_in"], 1, 2)
    res_w_in = _sum_adam(me_index, [slots["mix_w_in"]], [g_even["mix_w_in_t"]], rows_block(w_in_rows, d_model),
                         transposed(w), transposed(m), transposed(v), name="adam_mix_w_in")
    res_small = _sum_adam(me_index, [slots["small"]], [send_small],
                          lambda tr: ((1, tr, _LANES), lambda i, me: (me[0], i, 0)),
                          pk(we_, _SMALL_SHARDED), pk(me_, _SMALL_SHARDED), pk(ve_, _SMALL_SHARDED), name="adam_small")
    res_late = _sum_adam(me_index, [slots["repl_late"]], [send_late], whole_rows, pk(w, _REPL_LATE), pk(m, _REPL_LATE),
                         pk(v, _REPL_LATE), name="adam_repl_late")
    for k in range(4):
        out[k]["mix_w_in"] = jnp.swapaxes(res_w_in[k], 1, 2)
        for n, a in zip(_SMALL_SHARDED, _unpack(res_small[k][0], [we_[n].shape for n in _SMALL_SHARDED])):
            out[k][n] = _unpad_groups(a, unpad[n]) if n in unpad else a
        out[k].update(zip(_REPL_LATE, _unpack(res_late[k][0], [w[n].shape for n in _REPL_LATE])))
    return loss, grad_x, out


_INPUTS = tuple("""x, mem, positions, norm_mix, norm_xa, norm_mem, norm_ffn, xa_wq, xa_wk, xa_wv, xa_wo, xa_q_norm, xa_k_norm, ffn_w_up, ffn_conv_w, ffn_conv_b, ffn_w_down, hg_lb_logits, mix_w_in, hg_out_norm, mla_q_a_norm, mla_w_uq, mla_kv_a_norm, mla_w_ukv, mla_qn_nope, mla_qn_rope, mla_kn_nope, mla_kn_rope, mix_w_out, s5_lam_re, s5_lam_im, s5_log_dt, s5_b_re, s5_b_im, s5_c_re, s5_c_im, s5_d, s5_w_glu_a, s5_w_glu_b, loss_target, m_norm_mix, m_norm_xa, m_norm_mem, m_norm_ffn, m_xa_wq, m_xa_wk, m_xa_wv, m_xa_wo, m_xa_q_norm, m_xa_k_norm, m_ffn_w_up, m_ffn_conv_w, m_ffn_conv_b, m_ffn_w_down, m_hg_lb_logits, m_mix_w_in, m_hg_out_norm, m_mla_q_a_norm, m_mla_w_uq, m_mla_kv_a_norm, m_mla_w_ukv, m_mla_qn_nope, m_mla_qn_rope, m_mla_kn_nope, m_mla_kn_rope, m_mix_w_out, m_s5_lam_re, m_s5_lam_im, m_s5_log_dt, m_s5_b_re, m_s5_b_im, m_s5_c_re, m_s5_c_im, m_s5_d, m_s5_w_glu_a, m_s5_w_glu_b, v_norm_mix, v_norm_xa, v_norm_mem, v_norm_ffn, v_xa_wq, v_xa_wk, v_xa_wv, v_xa_wo, v_xa_q_norm, v_xa_k_norm, v_ffn_w_up, v_ffn_conv_w, v_ffn_conv_b, v_ffn_w_down, v_hg_lb_logits, v_mix_w_in, v_hg_out_norm, v_mla_q_a_norm, v_mla_w_uq, v_mla_kv_a_norm, v_mla_w_ukv, v_mla_qn_nope, v_mla_qn_rope, v_mla_kn_nope, v_mla_kn_rope, v_mix_w_out, v_s5_lam_re, v_s5_lam_im, v_s5_log_dt, v_s5_b_re, v_s5_b_im, v_s5_c_re, v_s5_c_im, v_s5_d, v_s5_w_glu_a, v_s5_w_glu_b""".replace(" ", "").split(","))


def kernel(x, mem, positions, norm_mix, norm_xa, norm_mem, norm_ffn, xa_wq, xa_wk, xa_wv, xa_wo, xa_q_norm, xa_k_norm, ffn_w_up, ffn_conv_w, ffn_conv_b, ffn_w_down, hg_lb_logits, mix_w_in, hg_out_norm, mla_q_a_norm, mla_w_uq, mla_kv_a_norm, mla_w_ukv, mla_qn_nope, mla_qn_rope, mla_kn_nope, mla_kn_rope, mix_w_out, s5_lam_re, s5_lam_im, s5_log_dt, s5_b_re, s5_b_im, s5_c_re, s5_c_im, s5_d, s5_w_glu_a, s5_w_glu_b, loss_target, m_norm_mix, m_norm_xa, m_norm_mem, m_norm_ffn, m_xa_wq, m_xa_wk, m_xa_wv, m_xa_wo, m_xa_q_norm, m_xa_k_norm, m_ffn_w_up, m_ffn_conv_w, m_ffn_conv_b, m_ffn_w_down, m_hg_lb_logits, m_mix_w_in, m_hg_out_norm, m_mla_q_a_norm, m_mla_w_uq, m_mla_kv_a_norm, m_mla_w_ukv, m_mla_qn_nope, m_mla_qn_rope, m_mla_kn_nope, m_mla_kn_rope, m_mix_w_out, m_s5_lam_re, m_s5_lam_im, m_s5_log_dt, m_s5_b_re, m_s5_b_im, m_s5_c_re, m_s5_c_im, m_s5_d, m_s5_w_glu_a, m_s5_w_glu_b, v_norm_mix, v_norm_xa, v_norm_mem, v_norm_ffn, v_xa_wq, v_xa_wk, v_xa_wv, v_xa_wo, v_xa_q_norm, v_xa_k_norm, v_ffn_w_up, v_ffn_conv_w, v_ffn_conv_b, v_ffn_w_down, v_hg_lb_logits, v_mix_w_in, v_hg_out_norm, v_mla_q_a_norm, v_mla_w_uq, v_mla_kv_a_norm, v_mla_w_ukv, v_mla_qn_nope, v_mla_qn_rope, v_mla_kn_nope, v_mla_kn_rope, v_mix_w_out, v_s5_lam_re, v_s5_lam_im, v_s5_log_dt, v_s5_b_re, v_s5_b_im, v_s5_c_re, v_s5_c_im, v_s5_d, v_s5_w_glu_a, v_s5_w_glu_b):
    vals = dict(zip(_INPUTS, (x, mem, positions, norm_mix, norm_xa, norm_mem, norm_ffn, xa_wq, xa_wk, xa_wv, xa_wo, xa_q_norm, xa_k_norm, ffn_w_up, ffn_conv_w, ffn_conv_b, ffn_w_down, hg_lb_logits, mix_w_in, hg_out_norm, mla_q_a_norm, mla_w_uq, mla_kv_a_norm, mla_w_ukv, mla_qn_nope, mla_qn_rope, mla_kn_nope, mla_kn_rope, mix_w_out, s5_lam_re, s5_lam_im, s5_log_dt, s5_b_re, s5_b_im, s5_c_re, s5_c_im, s5_d, s5_w_glu_a, s5_w_glu_b, loss_target, m_norm_mix, m_norm_xa, m_norm_mem, m_norm_ffn, m_xa_wq, m_xa_wk, m_xa_wv, m_xa_wo, m_xa_q_norm, m_xa_k_norm, m_ffn_w_up, m_ffn_conv_w, m_ffn_conv_b, m_ffn_w_down, m_hg_lb_logits, m_mix_w_in, m_hg_out_norm, m_mla_q_a_norm, m_mla_w_uq, m_mla_kv_a_norm, m_mla_w_ukv, m_mla_qn_nope, m_mla_qn_rope, m_mla_kn_nope, m_mla_kn_rope, m_mix_w_out, m_s5_lam_re, m_s5_lam_im, m_s5_log_dt, m_s5_b_re, m_s5_b_im, m_s5_c_re, m_s5_c_im, m_s5_d, m_s5_w_glu_a, m_s5_w_glu_b, v_norm_mix, v_norm_xa, v_norm_mem, v_norm_ffn, v_xa_wq, v_xa_wk, v_xa_wv, v_xa_wo, v_xa_q_norm, v_xa_k_norm, v_ffn_w_up, v_ffn_conv_w, v_ffn_conv_b, v_ffn_w_down, v_hg_lb_logits, v_mix_w_in, v_hg_out_norm, v_mla_q_a_norm, v_mla_w_uq, v_mla_kv_a_norm, v_mla_w_ukv, v_mla_qn_nope, v_mla_qn_rope, v_mla_kn_nope, v_mla_kn_rope, v_mix_w_out, v_s5_lam_re, v_s5_lam_im, v_s5_log_dt, v_s5_b_re, v_s5_b_im, v_s5_c_re, v_s5_c_im, v_s5_d, v_s5_w_glu_a, v_s5_w_glu_b)))
    w = {n: vals[n] for n in _WEIGHTS}
    m = {n: vals["m_" + n] for n in _WEIGHTS}
    v = {n: vals["v_" + n] for n in _WEIGHTS}
    loss, grad_x, res = _train_step(vals["x"][0], vals["mem"][0], vals["positions"][0], vals["loss_target"][0],
                                    w, m, v)
    loss = lax.psum(loss[0, 0], ("x", "y", "c"))
    return (loss, grad_x[None], *[r[n] for r in res for n in _WEIGHTS])
```

```python
import functools

import jax
import jax.numpy as jnp
import numpy as np
from jax import lax
from jax.experimental import pallas as pl
from jax.experimental.pallas import tpu as pltpu

f32 = jnp.float32
bf16 = jnp.bfloat16

EPS = 1e-6
N_DEV = 8
VMEM_LIMIT = 52 * 1024 * 1024

HG_HEADS = 4
HG_DIM = 128
HG_WIDTH = HG_HEADS * HG_DIM
HG_CHUNK = 64
HG_SUB = 16
MLA_HEADS = 4
MLA_Q_RANK = 256
MLA_KV_RANK = 128
MLA_NOPE = 128
MLA_ROPE = 64
MLA_V = 128
MLA_QK = MLA_NOPE + MLA_ROPE
MLA_QK_PAD = 256
ROPE_BASE = 10000.0
IN_WIDTH = 4 * HG_WIDTH + MLA_Q_RANK + MLA_KV_RANK + MLA_ROPE
IN_PAD = 2560
XA_HEADS = 4
XA_DIM = 256
S5_GROUP = 16
S5_GROUPS = 64
S5_STATE = 64
CONV_W = 3

ADAM_LR = 0.001
ADAM_B1 = 0.9
ADAM_B2 = 0.999
ADAM_EPS = 1e-08
ADAM_WD = 0.01
ADAM_STEP = 10

_NT = (((1,), (1,)), ((), ()))
_TN = (((0,), (0,)), ((), ()))
_NN = (((1,), (0,)), ((), ()))


def _pick(n, cands):
    for c in cands:
        if n % c == 0:
            return c
    return n


def _cparams(sem):
    return pltpu.CompilerParams(dimension_semantics=sem, vmem_limit_bytes=VMEM_LIMIT)


_MM_BUDGET = 36 * 1024 * 1024
_MM_TILES = ((1024, 1024), (1024, 512), (512, 1024), (512, 512), (512, 256), (256, 512), (256, 256), (256, 128),
             (128, 256), (128, 128))


def _mm(a, b, *, name, ta=False, tb=False, out_dtype=f32, add=None, b2=None, kslab=None, norm=None, rms_bwd=None):
    m, k = (a.shape[1], a.shape[0]) if ta else a.shape
    nb = b.shape[0] if tb else b.shape[1]
    n = nb * (2 if b2 is not None else 1)
    slab, nslab = kslab if kslab is not None else (0, 1)
    assert (b.shape[1] // nslab if tb else b.shape[0]) == k, (a.shape, b.shape, ta, tb)
    assert b2 is None or (not tb and b2.shape == b.shape)
    full_rows = norm is not None or rms_bwd is not None
    assert not (full_rows and b2 is not None) and not (norm is not None and rms_bwd is not None)
    isz = lambda x: jnp.dtype(x.dtype).itemsize
    bm = bn = None
    for cm, cn in _MM_TILES:
        if m % cm or nb % cn or (full_rows and cn != n):
            continue
        need = 2 * (cm * k * isz(a) + cn * k * isz(b) * (2 if b2 is not None else 1)
                    + cm * cn * (jnp.dtype(out_dtype).itemsize + (4 if add is not None else 0)
                                 + (2 if norm is not None else 0) + (8 if rms_bwd is not None else 0)))
        if need <= _MM_BUDGET:
            bm, bn = cm, cn
            break
    assert bm is not None, (name, a.shape, b.shape)
    half = nb // bn
    dims = (((0 if ta else 1,), (1 if tb else 0,)), ((), ()))

    def body(*refs):
        refs = list(refs)
        a_ref, b_ref = refs[0], refs[1]
        b2_ref = refs.pop(2) if b2 is not None else None
        add_ref = refs.pop(2) if add is not None else None
        gain_ref = refs.pop(2) if norm is not None else None
        x_ref, xgain_ref, through_ref = (refs.pop(2), refs.pop(2), refs.pop(2)) if rms_bwd is not None else (None,) * 3
        o_ref = refs[2]
        extra_ref = refs[3] if (norm is not None or rms_bwd is not None) else None

        def run(rhs_ref):
            r = lax.dot_general(a_ref[...].astype(bf16), rhs_ref[...].astype(bf16), dims, preferred_element_type=f32)
            if add_ref is not None:
                r = r + add_ref[...].astype(f32)
            if rms_bwd is None:
                o_ref[...] = r.astype(o_ref.dtype)
            else:
                _, vjp = jax.vjp(_rms, x_ref[...], xgain_ref[...])
                dx, dgain = vjp(r)
                o_ref[...] = (through_ref[...] + dx).astype(o_ref.dtype)

                @pl.when(pl.program_id(0) == 0)
                def _():
                    extra_ref[...] = jnp.zeros_like(extra_ref)
                extra_ref[...] += dgain
            if norm is not None:
                extra_ref[...] = _rms(r, gain_ref[...]).astype(extra_ref.dtype)

        if b2_ref is None:
            run(b_ref)
        else:
            pl.when(pl.program_id(1) < half)(lambda: run(b_ref))
            pl.when(pl.program_id(1) >= half)(lambda: run(b2_ref))

    a_spec = pl.BlockSpec((k, bm), lambda i, j: (0, i)) if ta else pl.BlockSpec((bm, k), lambda i, j: (i, 0))
    if tb:
        b_spec = pl.BlockSpec((bn, k), lambda i, j: (j, slab))
    elif b2 is None:
        b_spec = pl.BlockSpec((k, bn), lambda i, j: (0, j))
    else:
        b_spec = pl.BlockSpec((k, bn), lambda i, j: (0, jnp.minimum(j, half - 1)))
    in_specs = [a_spec, b_spec]
    args = [a, b]
    if b2 is not None:
        in_specs.append(pl.BlockSpec((k, bn), lambda i, j: (0, jnp.maximum(j - half, 0))))
        args.append(b2)
    if add is not None:
        in_specs.append(pl.BlockSpec((bm, bn), lambda i, j: (i, j)))
        args.append(add)
    out_blk = pl.BlockSpec((bm, bn), lambda i, j: (i, j))
    out_specs, out_shape = out_blk, jax.ShapeDtypeStruct((m, n), out_dtype)
    row_blk = pl.BlockSpec((1, bn), lambda i, j: (0, j))
    if norm is not None:
        in_specs.append(row_blk)
        args.append(norm)
        out_specs, out_shape = [out_blk, out_blk], [out_shape, jax.ShapeDtypeStruct((m, n), bf16)]
    if rms_bwd is not None:
        in_specs += [out_blk, row_blk, out_blk]
        args += list(rms_bwd)
        out_specs, out_shape = [out_blk, row_blk], [out_shape, jax.ShapeDtypeStruct((1, n), f32)]
    return pl.pallas_call(
        body, grid=(m // bm, n // bn), in_specs=in_specs, out_specs=out_specs, out_shape=out_shape,
        compiler_params=_cparams(("arbitrary" if rms_bwd is not None else "parallel", "parallel")),
        name=name)(*args)


def _as_tuple(x):
    return tuple(x) if isinstance(x, (tuple, list)) else (x,)


def _full_spec(p):
    nd = p.ndim
    return pl.BlockSpec(p.shape, lambda i, _nd=nd: (0,) * _nd)


def _window(a, start, width):
    assert start % width == 0 and width % 128 == 0
    return (a, start // width, width)


def _row_array(x):
    return x[0] if isinstance(x, tuple) else x


def _row_shape(x):
    return (x[0].shape[0], x[2]) if isinstance(x, tuple) else x.shape


def _row_spec(x, tile):
    if isinstance(x, tuple):
        return pl.BlockSpec((tile, x[2]), lambda i, _b=x[1]: (i, _b))
    return pl.BlockSpec((tile, x.shape[1]), lambda i: (i, 0))


def _rows(fn, rows, params, outs, *, name, tile=256, accs=()):
    length = _row_shape(rows[0])[0]
    tile = min(tile, length)
    nr, npar, no = len(rows), len(params), len(outs)

    def body(*refs):
        r, p, o = refs[:nr], refs[nr:nr + npar], refs[nr + npar:]
        res = _as_tuple(fn(*[x[...].astype(f32) for x in r], *[x[...] for x in p]))
        for kk in range(no):
            o[kk][...] = res[kk].astype(o[kk].dtype)
        if accs:
            @pl.when(pl.program_id(0) == 0)
            def _():
                for kk in range(no, no + len(accs)):
                    o[kk][...] = jnp.zeros_like(o[kk])
            for kk in range(no, no + len(accs)):
                o[kk][...] += res[kk]

    in_specs = [_row_spec(x, tile) for x in rows] + [_full_spec(p) for p in params]
    out_specs = [pl.BlockSpec((tile, w), lambda i: (i, 0)) for w, _ in outs]
    out_shape = [jax.ShapeDtypeStruct((length, w), d) for w, d in outs]
    for s in accs:
        out_specs.append(pl.BlockSpec(s, lambda i, _nd=len(s): (0,) * _nd))
        out_shape.append(jax.ShapeDtypeStruct(s, f32))
    res = pl.pallas_call(body, grid=(length // tile,), in_specs=in_specs, out_specs=out_specs, out_shape=out_shape,
                         compiler_params=_cparams(("arbitrary",)), name=name)(*[_row_array(x) for x in rows], *params)
    return res


def _rows_bwd(fn, rows, params, cts, *, name, rgrad, pgrad, tile=256, addends=None):
    addends = {i: (a if isinstance(a, list) else [(a, 0)]) for i, a in (addends or {}).items()}
    length = _row_shape(rows[0])[0]
    tile = min(tile, length)
    nr, npar, nc = len(rows), len(params), len(cts)
    ridx = [i for i in range(nr) if rgrad[i] is not None]
    pidx = [i for i in range(npar) if pgrad[i]]
    flat_addends = [(i, a, off) for i in sorted(addends) for a, off in addends[i]]
    na = len(flat_addends)

    def body(*refs):
        r, p, c = refs[:nr], refs[nr:nr + npar], refs[nr + npar:nr + npar + nc]
        ad = refs[nr + npar + nc:nr + npar + nc + na]
        o = refs[nr + npar + nc + na:]
        rv = [x[...].astype(f32) for x in r]
        pv = [x[...] for x in p]
        cv = tuple(x[...].astype(f32) for x in c)

        def g(*d):
            rr, pp = list(rv), list(pv)
            for n_, i_ in enumerate(ridx):
                rr[i_] = d[n_]
            for n_, i_ in enumerate(pidx):
                pp[i_] = d[len(ridx) + n_]
            return _as_tuple(fn(*rr, *pp))

        _, vjp = jax.vjp(g, *[rv[i] for i in ridx], *[pv[i] for i in pidx])
        grads = vjp(cv)
        for n_, i_ in enumerate(ridx):
            val = grads[n_]
            for k_, (j_, a_, off) in enumerate(flat_addends):
                if j_ == i_:
                    extra = ad[k_][...].astype(f32)
                    if extra.shape[1] != val.shape[1]:
                        extra = jnp.pad(extra, ((0, 0), (off, val.shape[1] - off - extra.shape[1])))
                    val = val + extra
            o[n_][...] = val.astype(o[n_].dtype)
        if pidx:
            @pl.when(pl.program_id(0) == 0)
            def _():
                for n_ in range(len(pidx)):
                    o[len(ridx) + n_][...] = jnp.zeros_like(o[len(ridx) + n_])
            for n_ in range(len(pidx)):
                o[len(ridx) + n_][...] += grads[len(ridx) + n_]

    plain = lambda shape: pl.BlockSpec((tile, shape[1]), lambda i: (i, 0))
    in_specs = ([_row_spec(x, tile) for x in rows] + [_full_spec(p) for p in params] + [plain(x.shape) for x in cts]
                + [plain(a.shape) for _, a, _ in flat_addends])
    out_specs = [plain(_row_shape(rows[i])) for i in ridx] + [_full_spec(params[i]) for i in pidx]
    out_shape = ([jax.ShapeDtypeStruct(_row_shape(rows[i]), rgrad[i]) for i in ridx]
                 + [jax.ShapeDtypeStruct(params[i].shape, f32) for i in pidx])
    res = pl.pallas_call(body, grid=(length // tile,), in_specs=in_specs, out_specs=out_specs, out_shape=out_shape,
                         compiler_params=_cparams(("arbitrary",)), name=name)(
        *[_row_array(x) for x in rows], *params, *cts, *[a for _, a, _ in flat_addends])
    return list(res[:len(ridx)]), list(res[len(ridx):])


def _rms(x, g):
    return x * lax.rsqrt(jnp.mean(x * x, axis=-1, keepdims=True) + EPS) * g


def _rms_twice(x, g):
    y = _rms(x, g)
    return y, y


def _silu(x):
    return x * jax.nn.sigmoid(x)


def _shift_down(x, s):
    rows = lax.broadcasted_iota(jnp.int32, x.shape, 0)
    return jnp.where(rows >= s, pltpu.roll(x, s, axis=0), 0.0)


def _shift_up(x, s):
    n = x.shape[0]
    rows = lax.broadcasted_iota(jnp.int32, x.shape, 0)
    return jnp.where(rows < n - s, pltpu.roll(x, n - s, axis=0), 0.0)


_CONV_COLS = 128


_UP_COLS = 512


def _ffn_up_conv(hf, w_up, cw, cb, *, name):
    length, d = hf.shape
    ff = w_up.shape[1] // 2
    bm = _pick(length, (1024, 512, 256))
    bn = _UP_COLS
    nj = ff // bn

    def body(hf_ref, wg_ref, wv_ref, cwg, cwv, cbg, cbv, ug_ref, uv_ref, a_ref, halo_g, halo_v):
        i, j = pl.program_id(0), pl.program_id(1)
        x = hf_ref[...]
        rows = lax.broadcasted_iota(jnp.int32, (bm, bn), 0)
        pad = jnp.zeros((bm - 8, bn), f32)

        def half(w_ref, cw_ref, cb_ref, u_ref, halo):
            u = jnp.dot(x, w_ref[...], preferred_element_type=f32).astype(bf16)
            u_ref[...] = u
            u = u.astype(f32)
            prev = jnp.where(i == 0, 0.0, halo[j])
            x1 = jnp.where(rows >= 1, pltpu.roll(u, 1, axis=0), jnp.concatenate([pltpu.roll(prev, 1, axis=0), pad]))
            x2 = jnp.where(rows >= 2, pltpu.roll(u, 2, axis=0), jnp.concatenate([pltpu.roll(prev, 2, axis=0), pad]))
            halo[j] = u[bm - 8:, :]
            return cw_ref[2:3, :] * u + cw_ref[1:2, :] * x1 + cw_ref[0:1, :] * x2 + cb_ref[...]

        g = half(wg_ref, cwg, cbg, ug_ref, halo_g)
        v = half(wv_ref, cwv, cbv, uv_ref, halo_v)
        a_ref[...] = (_silu(g) * v).astype(a_ref.dtype)

    col = lambda r, off: pl.BlockSpec((r, bn), lambda i, j, _o=off: (0, j + _o))
    out_blk = pl.BlockSpec((bm, bn), lambda i, j: (i, j))
    sds = jax.ShapeDtypeStruct((length, ff), bf16)
    return pl.pallas_call(
        body, grid=(length // bm, nj),
        in_specs=[pl.BlockSpec((bm, d), lambda i, j: (i, 0)), col(d, 0), col(d, nj), col(CONV_W, 0), col(CONV_W, nj),
                  col(1, 0), col(1, nj)],
        out_specs=[out_blk] * 3, out_shape=[sds] * 3,
        scratch_shapes=[pltpu.VMEM((nj, 8, bn), f32), pltpu.VMEM((nj, 8, bn), f32)],
        compiler_params=_cparams(("arbitrary", "arbitrary")), name=name)(hf, w_up, w_up, cw, cw, cb, cb)


def _conv_gate_bwd(u_g, u_v, cw, cb, da, *, name):
    length, ff = u_g.shape
    nb = ff // _CONV_COLS

    def body(ug, uv, wg, wv, bg, bv, da_ref, dug, duv, dwg, dwv, dbg, dbv):
        def conv(x, w_ref, b_ref):
            x1, x2 = _shift_down(x, 1), _shift_down(x, 2)
            return w_ref[2:3, :] * x + w_ref[1:2, :] * x1 + w_ref[0:1, :] * x2 + b_ref[...], x1, x2

        xg, xv = ug[...].astype(f32), uv[...].astype(f32)
        g, xg1, xg2 = conv(xg, wg, bg)
        v, xv1, xv2 = conv(xv, wv, bv)
        d = da_ref[...].astype(f32)
        sg = jax.nn.sigmoid(g)
        dg = d * v * (sg * (1.0 + g * (1.0 - sg)))
        dv = d * (g * sg)

        def back(dy, x, x1, x2, w_ref, du_ref, dw_ref, db_ref):
            du_ref[...] = (w_ref[2:3, :] * dy + w_ref[1:2, :] * _shift_up(dy, 1)
                           + w_ref[0:1, :] * _shift_up(dy, 2)).astype(du_ref.dtype)
            dw_ref[2:3, :] = jnp.sum(dy * x, axis=0, keepdims=True)
            dw_ref[1:2, :] = jnp.sum(dy * x1, axis=0, keepdims=True)
            dw_ref[0:1, :] = jnp.sum(dy * x2, axis=0, keepdims=True)
            db_ref[...] = jnp.sum(dy, axis=0, keepdims=True)

        back(dg, xg, xg1, xg2, wg, dug, dwg, dbg)
        back(dv, xv, xv1, xv2, wv, duv, dwv, dbv)

    blk = lambda r, off: pl.BlockSpec((r, _CONV_COLS), lambda j, _o=off: (0, j + _o))
    sds = jax.ShapeDtypeStruct
    dug, duv, dwg, dwv, dbg, dbv = pl.pallas_call(
        body, grid=(nb,),
        in_specs=[blk(length, 0), blk(length, 0), blk(CONV_W, 0), blk(CONV_W, nb), blk(1, 0), blk(1, nb),
                  blk(length, 0)],
        out_specs=[blk(length, 0), blk(length, 0), blk(CONV_W, 0), blk(CONV_W, 0), blk(1, 0), blk(1, 0)],
        out_shape=[sds((length, ff), bf16), sds((length, ff), bf16), sds((CONV_W, ff), f32), sds((CONV_W, ff), f32),
                   sds((1, ff), f32), sds((1, ff), f32)],
        compiler_params=_cparams(("parallel",)), name=name)(u_g, u_v, cw, cw, cb, cb, da)
    return dug, duv, jnp.concatenate([dwg, dwv], axis=1), jnp.concatenate([dbg, dbv], axis=1)


def _ffn_fwd(h, w, tag, hf=None):
    if hf is None:
        hf, = _rows(_rms, [h], [w["norm_ffn"]], [(h.shape[1], bf16)], name=f"ffn_norm_{tag}")
    u_g, u_v, a = _ffn_up_conv(hf, w["ffn_w_up"], w["ffn_conv_w"], w["ffn_conv_b"], name=f"ffn_up_{tag}")
    out = _mm(a, w["ffn_w_down"], add=h, name=f"ffn_down_{tag}")
    return out, (h, hf, u_g, u_v, a)


def _ffn_bwd(dout, w, saved, tag):
    h, hf, u_g, u_v, a = saved
    da = _mm(dout, w["ffn_w_down"], tb=True, out_dtype=bf16, name=f"ffn_da_{tag}")
    g = {"ffn_w_down": _mm(a, dout, ta=True, name=f"ffn_dwdown_{tag}")}
    dug, duv, g["ffn_conv_w"], g["ffn_conv_b"] = _conv_gate_bwd(u_g, u_v, w["ffn_conv_w"], w["ffn_conv_b"], da,
                                                                name=f"ffn_dconv_{tag}")
    dhf = _mm(dug, w["ffn_w_up"], tb=True, kslab=(0, 2), name=f"ffn_dhf_g_{tag}")
    dh, g["norm_ffn"] = _mm(duv, w["ffn_w_up"], tb=True, kslab=(1, 2), add=dhf, rms_bwd=(h, w["norm_ffn"], dout),
                            name=f"ffn_dhf_v_{tag}")
    g["ffn_w_up"] = _mm(hf, dug, ta=True, b2=duv, name=f"ffn_dwup_{tag}")
    return dh, g


def _xattn_fn(qx, kx, vx, qg, kg):
    outs = []
    for hh in range(XA_HEADS):
        sl = slice(hh * XA_DIM, (hh + 1) * XA_DIM)
        q = _rms(qx[:, sl], qg).astype(bf16)
        k = _rms(kx[:, sl], kg).astype(bf16)
        s = lax.dot_general(q, k, _NT, preferred_element_type=f32) * (XA_DIM ** -0.5)
        s = s - jnp.max(s, axis=-1, keepdims=True)
        p = jnp.exp(s)
        p = p / jnp.sum(p, axis=-1, keepdims=True)
        outs.append(jnp.dot(p.astype(bf16), vx[:, sl].astype(bf16), preferred_element_type=f32))
    return jnp.concatenate(outs, axis=-1)


def _xattn_fwd(h, mem, w, tag, hx=None):
    d = h.shape[1]
    if hx is None:
        hx, = _rows(_rms, [h], [w["norm_xa"]], [(d, bf16)], name=f"xa_norm_{tag}")
    qx = _mm(hx, w["xa_wq"], name=f"xa_q_{tag}")
    m, = _rows(_rms, [mem], [w["norm_mem"]], [(d, bf16)], name=f"xa_mnorm_{tag}")
    kx = _mm(m, w["xa_wk"], name=f"xa_k_{tag}")
    vx = _mm(m, w["xa_wv"], name=f"xa_v_{tag}")
    o, = _rows(_xattn_fn, [qx], [kx, vx, w["xa_q_norm"], w["xa_k_norm"]], [(d, bf16)], tile=1024,
               name=f"xa_attn_{tag}")
    out, hf = _mm(o, w["xa_wo"], add=h, norm=w["norm_ffn"], name=f"xa_o_{tag}")
    return out, (h, hx, qx, m, kx, vx, o), hf


def _xattn_bwd(dout, mem, w, saved, tag):
    h, hx, qx, m, kx, vx, o = saved
    g = {}
    do = _mm(dout, w["xa_wo"], tb=True, out_dtype=bf16, name=f"xa_do_{tag}")
    g["xa_wo"] = _mm(o, dout, ta=True, name=f"xa_dwo_{tag}")
    (dqx,), (dkx, dvx, g["xa_q_norm"], g["xa_k_norm"]) = _rows_bwd(
        _xattn_fn, [qx], [kx, vx, w["xa_q_norm"], w["xa_k_norm"]], [do], rgrad=[bf16], pgrad=[True] * 4,
        tile=1024, name=f"xa_dattn_{tag}")
    dh, g["norm_xa"] = _mm(dqx, w["xa_wq"], tb=True, rms_bwd=(h, w["norm_xa"], dout), name=f"xa_dhx_{tag}")
    g["xa_wq"] = _mm(hx, dqx, ta=True, name=f"xa_dwq_{tag}")
    dm = _mm(dkx, w["xa_wk"], tb=True, name=f"xa_dm_k_{tag}")
    dm = _mm(dvx, w["xa_wv"], tb=True, add=dm, name=f"xa_dm_v_{tag}")
    g["xa_wk"] = _mm(m, dkx, ta=True, name=f"xa_dwk_{tag}")
    g["xa_wv"] = _mm(m, dvx, ta=True, name=f"xa_dwv_{tag}")
    _, (g["norm_mem"],) = _rows_bwd(_rms, [mem], [w["norm_mem"]], [dm], rgrad=[None], pgrad=[True],
                                    name=f"xa_dmnorm_{tag}")
    return dh, g


_HG_GROUP = 4


def _hg_chunk(q, k, v, g, *sts):
    c = q.shape[0]
    heads = [slice(h * HG_DIM, (h + 1) * HG_DIM) for h in range(len(sts))]
    tri = (lax.broadcasted_iota(jnp.int32, (c, c), 0) >= lax.broadcasted_iota(jnp.int32, (c, c), 1)).astype(f32)
    b = jnp.dot(tri, g, precision=lax.Precision.HIGHEST, preferred_element_type=f32)
    bend = jnp.sum(g, axis=0, keepdims=True)
    qe = (q * jnp.exp(b)).astype(bf16)
    kd = (k * jnp.exp(bend - b)).astype(bf16)
    vb = v.astype(bf16)
    decay = jnp.exp(bend)
    o_inter = [lax.dot_general(qe[:, hs], st.astype(bf16), _NT, preferred_element_type=f32) for hs, st in zip(heads, sts)]
    new = [st * decay[:, hs] + lax.dot_general(vb[:, hs], kd[:, hs], _TN, preferred_element_type=f32)
           for hs, st in zip(heads, sts)]
    outs = []
    for i in range(c // HG_SUB):
        lo, n = HG_SUB * i, HG_SUB * (i + 1)
        ref = jnp.sum(g[:lo], axis=0, keepdims=True) if i else jnp.zeros((1, g.shape[1]), f32)
        qh = (q[lo:n] * jnp.exp(b[lo:n] - ref)).astype(bf16)
        kh = (k[:n] * jnp.exp(ref - b[:n])).astype(bf16)
        keep = (lax.broadcasted_iota(jnp.int32, (HG_SUB, n), 1)
                <= lo + lax.broadcasted_iota(jnp.int32, (HG_SUB, n), 0))
        scores = [lax.dot_general(qh[:, hs], kh[:, hs], _NT, preferred_element_type=f32) for hs in heads]
        scores = [jnp.where(keep, a, 0.0).astype(bf16) for a in scores]
        outs.append(jnp.concatenate([jnp.dot(a, vb[:n, hs], preferred_element_type=f32)
                                     for a, hs in zip(scores, heads)], axis=1))
    return (jnp.concatenate(outs, axis=0) + jnp.concatenate(o_inter, axis=1), *new)


def _hg_fwd(q, k, v, g, *, name):
    length = q.shape[0]
    rows = _HG_GROUP * HG_CHUNK
    ng = length // rows
    nc = length // HG_CHUNK

    def body(q_ref, k_ref, v_ref, g_ref, o_ref, st_ref, state):
        @pl.when(pl.program_id(0) == 0)
        def _():
            state[...] = jnp.zeros_like(state)

        states = [state[h] for h in range(HG_HEADS)]
        for ci in range(_HG_GROUP):
            sl = slice(ci * HG_CHUNK, (ci + 1) * HG_CHUNK)
            for h in range(HG_HEADS):
                st_ref[h, ci] = states[h]
            o, *states = _hg_chunk(q_ref[sl, :], k_ref[sl, :], v_ref[sl, :], g_ref[sl, :], *states)
            o_ref[sl, :] = o
        for h in range(HG_HEADS):
            state[h] = states[h]

    blk = pl.BlockSpec((rows, HG_WIDTH), lambda c: (c, 0))
    return pl.pallas_call(
        body, grid=(ng,), in_specs=[blk] * 4,
        out_specs=[blk, pl.BlockSpec((HG_HEADS, _HG_GROUP, HG_DIM, HG_DIM), lambda c: (0, c, 0, 0))],
        out_shape=[jax.ShapeDtypeStruct((length, HG_WIDTH), f32),
                   jax.ShapeDtypeStruct((HG_HEADS, nc, HG_DIM, HG_DIM), f32)],
        scratch_shapes=[pltpu.VMEM((HG_HEADS, HG_DIM, HG_DIM), f32)],
        compiler_params=_cparams(("arbitrary",)), name=name)(q, k, v, g)


def _hg_bwd(q, k, v, g, states, do, *, name):
    length = q.shape[0]
    rows = _HG_GROUP * HG_CHUNK
    ng = length // rows

    def body(q_ref, k_ref, v_ref, g_ref, st_ref, do_ref, dq_ref, dk_ref, dv_ref, dg_ref, dstate):
        @pl.when(pl.program_id(0) == 0)
        def _():
            dstate[...] = jnp.zeros_like(dstate)

        dstates = [dstate[h] for h in range(HG_HEADS)]
        for ci in reversed(range(_HG_GROUP)):
            sl = slice(ci * HG_CHUNK, (ci + 1) * HG_CHUNK)
            _, vjp = jax.vjp(_hg_chunk, q_ref[sl, :], k_ref[sl, :], v_ref[sl, :], g_ref[sl, :],
                             *[st_ref[h, ci] for h in range(HG_HEADS)])
            dq, dk, dv, dg, *dstates = vjp((do_ref[sl, :], *dstates))
            dq_ref[sl, :] = dq
            dk_ref[sl, :] = dk
            dv_ref[sl, :] = dv
            dg_ref[sl, :] = dg
        for h in range(HG_HEADS):
            dstate[h] = dstates[h]

    blk = pl.BlockSpec((rows, HG_WIDTH), lambda c: (ng - 1 - c, 0))
    sds = jax.ShapeDtypeStruct((length, HG_WIDTH), f32)
    return pl.pallas_call(
        body, grid=(ng,),
        in_specs=[blk] * 4 + [pl.BlockSpec((HG_HEADS, _HG_GROUP, HG_DIM, HG_DIM), lambda c: (0, ng - 1 - c, 0, 0)), blk],
        out_specs=[blk] * 4, out_shape=[sds] * 4,
        scratch_shapes=[pltpu.VMEM((HG_HEADS, HG_DIM, HG_DIM), f32)],
        compiler_params=_cparams(("arbitrary",)), name=name)(q, k, v, g, states, do)


_ATT_BLK = 512
_ATT_SCALE = MLA_QK ** -0.5
_NEG = -1e30


def _att_mask(i, j, t):
    rows = i * t + lax.broadcasted_iota(jnp.int32, (t, t), 0)
    cols = j * t + lax.broadcasted_iota(jnp.int32, (t, t), 1)
    return cols <= rows


def _att_fwd(q, k, v, *, name):
    length = q.shape[0]
    t = min(_ATT_BLK, length)
    nq = length // t
    qw, vw = MLA_QK_PAD, MLA_V
    heads = range(MLA_HEADS)

    def body(q_ref, k_ref, v_ref, o_ref, lse_ref):
        i = pl.program_id(0)
        qbs = [q_ref[:, h * qw:(h + 1) * qw] for h in heads]

        def step(j, carry, diagonal=False):
            off = pl.multiple_of(j * t, t)
            out = []
            for h in heads:
                m, l, acc = carry[h]
                ks = k_ref[pl.ds(off, t), h * qw:(h + 1) * qw]
                vs = v_ref[pl.ds(off, t), h * vw:(h + 1) * vw]
                s = lax.dot_general(qbs[h], ks, _NT, preferred_element_type=f32) * _ATT_SCALE
                if diagonal:
                    s = jnp.where(_att_mask(i, j, t), s, _NEG)
                m_new = jnp.maximum(m, jnp.max(s, axis=-1, keepdims=True))
                alpha = jnp.exp(m - m_new)
                p = jnp.exp(s - m_new)
                l = alpha * l + jnp.sum(p, axis=-1, keepdims=True)
                acc = alpha * acc + jnp.dot(p.astype(bf16), vs, preferred_element_type=f32)
                out.append((m_new, l, acc))
            return tuple(out)

        init = tuple((jnp.full((t, 1), _NEG, f32), jnp.zeros((t, 1), f32), jnp.zeros((t, vw), f32)) for _ in heads)
        res = step(i, lax.fori_loop(0, i, step, init), diagonal=True)
        for h in heads:
            m, l, acc = res[h]
            o_ref[:, h * vw:(h + 1) * vw] = (acc / l).astype(o_ref.dtype)
            lse_ref[:, h * vw:(h + 1) * vw] = jnp.broadcast_to(m + jnp.log(l), (t, vw))

    return pl.pallas_call(
        body, grid=(nq,),
        in_specs=[pl.BlockSpec((t, q.shape[1]), lambda i: (i, 0)), pl.BlockSpec(k.shape, lambda i: (0, 0)),
                  pl.BlockSpec(v.shape, lambda i: (0, 0))],
        out_specs=[pl.BlockSpec((t, v.shape[1]), lambda i: (i, 0))] * 2,
        out_shape=[jax.ShapeDtypeStruct(v.shape, bf16), jax.ShapeDtypeStruct(v.shape, f32)],
        compiler_params=_cparams(("arbitrary",)), name=name)(q, k, v)


def _att_bwd(q, k, v, o, lse, do, *, name):
    length = q.shape[0]
    t = min(_ATT_BLK, length)
    nq = length // t
    qw, vw = MLA_QK_PAD, MLA_V
    heads = range(MLA_HEADS)

    def dq_body(q_ref, k_ref, v_ref, o_ref, lse_ref, do_ref, dq_ref, delta_ref):
        i = pl.program_id(0)
        qbs = [q_ref[:, h * qw:(h + 1) * qw] for h in heads]
        dobs = [do_ref[:, h * vw:(h + 1) * vw] for h in heads]
        lses = [lse_ref[:, h * vw:h * vw + 1] for h in heads]
        deltas = [jnp.sum(dobs[h].astype(f32) * o_ref[:, h * vw:(h + 1) * vw].astype(f32), axis=-1, keepdims=True)
                  for h in heads]

        def step(j, dqs, diagonal=False):
            off = pl.multiple_of(j * t, t)
            out = []
            for h in heads:
                ks = k_ref[pl.ds(off, t), h * qw:(h + 1) * qw]
                vs = v_ref[pl.ds(off, t), h * vw:(h + 1) * vw]
                s = lax.dot_general(qbs[h], ks, _NT, preferred_element_type=f32) * _ATT_SCALE
                p = jnp.exp(s - lses[h])
                if diagonal:
                    p = jnp.where(_att_mask(i, j, t), p, 0.0)
                dp = lax.dot_general(dobs[h], vs, _NT, preferred_element_type=f32)
                ds = p * (dp - deltas[h]) * _ATT_SCALE
                out.append(dqs[h] + jnp.dot(ds.astype(bf16), ks, preferred_element_type=f32))
            return tuple(out)

        dqs = step(i, lax.fori_loop(0, i, step, tuple(jnp.zeros((t, qw), f32) for _ in heads)), diagonal=True)
        for h in heads:
            dq_ref[:, h * qw:(h + 1) * qw] = dqs[h].astype(dq_ref.dtype)
            delta_ref[:, h * vw:(h + 1) * vw] = jnp.broadcast_to(deltas[h], (t, vw))

    qblk = pl.BlockSpec((t, q.shape[1]), lambda i: (i, 0))
    vblk = pl.BlockSpec((t, v.shape[1]), lambda i: (i, 0))
    qfull = pl.BlockSpec(q.shape, lambda i: (0, 0))
    vfull = pl.BlockSpec(v.shape, lambda i: (0, 0))
    dq, delta = pl.pallas_call(
        dq_body, grid=(nq,), in_specs=[qblk, qfull, vfull, vblk, vblk, vblk], out_specs=[qblk, vblk],
        out_shape=[jax.ShapeDtypeStruct(q.shape, bf16), jax.ShapeDtypeStruct(lse.shape, f32)],
        compiler_params=_cparams(("arbitrary",)), name=name + "_dq")(q, k, v, o, lse, do)

    def dkv_body(k_ref, v_ref, q_ref, do_ref, lse_ref, delta_ref, dk_ref, dv_ref):
        j = pl.program_id(0)
        kbs = [k_ref[:, h * qw:(h + 1) * qw] for h in heads]
        vbs = [v_ref[:, h * vw:(h + 1) * vw] for h in heads]

        def step(i, carry, diagonal=False):
            off = pl.multiple_of(i * t, t)
            out = []
            for h in heads:
                dk, dv = carry[h]
                qs = q_ref[pl.ds(off, t), h * qw:(h + 1) * qw]
                dos = do_ref[pl.ds(off, t), h * vw:(h + 1) * vw]
                lse_i = lse_ref[pl.ds(off, t), h * vw:h * vw + 1]
                delta_i = delta_ref[pl.ds(off, t), h * vw:h * vw + 1]
                s = lax.dot_general(qs, kbs[h], _NT, preferred_element_type=f32) * _ATT_SCALE
                p = jnp.exp(s - lse_i)
                if diagonal:
                    p = jnp.where(_att_mask(i, j, t), p, 0.0)
                dv = dv + lax.dot_general(p.astype(bf16), dos, _TN, preferred_element_type=f32)
                dp = lax.dot_general(dos, vbs[h], _NT, preferred_element_type=f32)
                ds = p * (dp - delta_i) * _ATT_SCALE
                dk = dk + lax.dot_general(ds.astype(bf16), qs, _TN, preferred_element_type=f32)
                out.append((dk, dv))
            return tuple(out)

        first = step(j, tuple((jnp.zeros((t, qw), f32), jnp.zeros((t, vw), f32)) for _ in heads), diagonal=True)
        res = lax.fori_loop(j + 1, nq, step, first)
        for h in heads:
            dk_ref[:, h * qw:(h + 1) * qw] = res[h][0].astype(dk_ref.dtype)
            dv_ref[:, h * vw:(h + 1) * vw] = res[h][1].astype(dv_ref.dtype)

    dk, dv = pl.pallas_call(
        dkv_body, grid=(nq,), in_specs=[qblk, vblk, qfull, vfull, vfull, vfull], out_specs=[qblk, vblk],
        out_shape=[jax.ShapeDtypeStruct(k.shape, bf16), jax.ShapeDtypeStruct(v.shape, bf16)],
        compiler_params=_cparams(("arbitrary",)), name=name + "_dkv")(k, v, q, do, lse, delta)
    return dq, dk, dv


_C_Q = 4 * HG_WIDTH
_C_KV = _C_Q + MLA_Q_RANK
_C_KPE = _C_KV + MLA_KV_RANK


def _rms_n(x, g, n):
    return x * lax.rsqrt(jnp.sum(x * x, axis=-1, keepdims=True) * (1.0 / n) + EPS) * g


def _mix_a(proj, l0, l1, q_a_norm, kv_a_norm):
    lb = jax.nn.sigmoid(l0 - l1)
    f = lb + (1.0 - lb) * jax.nn.sigmoid(proj[:, HG_WIDTH:2 * HG_WIDTH])
    qf = _silu(proj[:, :HG_WIDTH])
    v = proj[:, 2 * HG_WIDTH:3 * HG_WIDTH]
    cqn = _rms(proj[:, _C_Q:_C_KV], q_a_norm)
    ckvn = _rms(proj[:, _C_KV:_C_KPE], kv_a_norm)
    return qf, 1.0 - f, v, jnp.log(f), cqn, ckvn


def _mix_b(qraw, kvraw, kpe_raw, cos, sin, qn_nope, qn_rope, kn_nope, kn_rope, perm):
    def rope(x):
        return x * cos + jnp.dot(x, perm, precision=lax.Precision.HIGHEST, preferred_element_type=f32) * sin

    kpe = rope(_rms_n(kpe_raw, kn_rope, MLA_ROPE))
    qs, ks, vs = [], [], []
    for hh in range(MLA_HEADS):
        base = hh * MLA_QK_PAD
        qs.append(_rms(qraw[:, base:base + MLA_NOPE], qn_nope))
        qs.append(rope(_rms_n(qraw[:, base + MLA_NOPE:base + MLA_QK_PAD], qn_rope, MLA_ROPE)))
        ks.append(_rms(kvraw[:, base:base + MLA_NOPE], kn_nope))
        ks.append(kpe)
        vs.append(kvraw[:, base + MLA_NOPE:base + MLA_QK_PAD])
    return jnp.concatenate(qs, axis=-1), jnp.concatenate(ks, axis=-1), jnp.concatenate(vs, axis=-1)


def _mix_c(o_hg, gate, o_mla, hg_out_norm):
    parts = []
    for hh in range(HG_HEADS):
        sl = slice(hh * HG_DIM, (hh + 1) * HG_DIM)
        parts.append(_rms(o_hg[:, sl], hg_out_norm[:, sl]))
    o = jnp.concatenate(parts, axis=-1) * _silu(gate)
    return jnp.concatenate([o, o_mla], axis=-1)


def _rope_perm():
    p = np.zeros((128, 128), np.float32)
    half = MLA_ROPE // 2
    for i in range(half):
        p[i + half, i] = -1.0
        p[i, i + half] = 1.0
    return jnp.asarray(p)


def _mixer_fwd(h, cos, sin, w, tag, next_gain):
    d = h.shape[1]
    hn, = _rows(_rms, [h], [w["norm_mix"]], [(d, bf16)], name=f"mix_norm_{tag}")
    proj = _mm(hn, w["mix_w_in"], name=f"mix_in_{tag}")
    pa = [w["lb0"], w["lb1"], w["mla_q_a_norm"], w["mla_kv_a_norm"]]
    qf, kk, vv, logf, cqn, ckvn = _rows(
        _mix_a, [proj], pa, [(HG_WIDTH, f32)] * 4 + [(MLA_Q_RANK, bf16), (MLA_KV_RANK, bf16)], name=f"mix_a_{tag}")
    o_hg, states = _hg_fwd(qf, kk, vv, logf, name=f"hg_fwd_{tag}")
    qraw = _mm(cqn, w["mla_w_uq"], name=f"mla_uq_{tag}")
    kvraw = _mm(ckvn, w["mla_w_ukv"], name=f"mla_ukv_{tag}")
    pb = [w["mla_qn_nope"], w["mla_qn_rope"], w["mla_kn_nope"], w["mla_kn_rope"], w["rope_perm"]]
    kpe_raw, gate = _window(proj, _C_KPE, IN_PAD - _C_KPE), _window(proj, 3 * HG_WIDTH, HG_WIDTH)
    qfull, kfull, vfull = _rows(_mix_b, [qraw, kvraw, kpe_raw, cos, sin], pb,
                                [(MLA_HEADS * MLA_QK_PAD, bf16)] * 2 + [(MLA_HEADS * MLA_V, bf16)],
                                name=f"mix_b_{tag}")
    o_mla, lse = _att_fwd(qfull, kfull, vfull, name=f"att_fwd_{tag}")
    mixin, = _rows(_mix_c, [o_hg, gate, o_mla], [w["hg_out_norm"]], [(d, bf16)], name=f"mix_c_{tag}")
    if callable(w["mix_w_out"]):
        w["mix_w_out"] = w["mix_w_out"](mixin)
    out, normed = _mm(mixin, w["mix_w_out"], add=h, norm=next_gain, name=f"mix_out_{tag}")
    return out, (h, hn, proj, qf, kk, vv, logf, cqn, ckvn, o_hg, states, qraw, kvraw, qfull, kfull, vfull, o_mla,
                 lse, mixin), normed


def _mixer_bwd(dout, cos, sin, w, saved, tag, on_w_out=None):
    (h, hn, proj, qf, kk, vv, logf, cqn, ckvn, o_hg, states, qraw, kvraw, qfull, kfull, vfull, o_mla, lse,
     mixin) = saved
    g = {}
    dmixin = _mm(dout, w["mix_w_out"], tb=True, name=f"mix_dmixin_{tag}")
    g["mix_w_out"] = _mm(mixin, dout, ta=True, name=f"mix_dwout_{tag}")
    if on_w_out is not None:
        dmixin = on_w_out(g["mix_w_out"], dmixin)
    kpe_raw, gate = _window(proj, _C_KPE, IN_PAD - _C_KPE), _window(proj, 3 * HG_WIDTH, HG_WIDTH)
    (do_hg, dgate, do_mla), (g["hg_out_norm"],) = _rows_bwd(
        _mix_c, [o_hg, gate, o_mla], [w["hg_out_norm"]], [dmixin], rgrad=[f32, f32, bf16], pgrad=[True],
        name=f"mix_dc_{tag}")
    dqfull, dkfull, dvfull = _att_bwd(qfull, kfull, vfull, o_mla, lse, do_mla, name=f"att_bwd_{tag}")
    pb = [w["mla_qn_nope"], w["mla_qn_rope"], w["mla_kn_nope"], w["mla_kn_rope"], w["rope_perm"]]
    (dqraw, dkvraw, dkpe_raw), pg = _rows_bwd(
        _mix_b, [qraw, kvraw, kpe_raw, cos, sin], pb, [dqfull, dkfull, dvfull],
        rgrad=[bf16, bf16, f32, None, None], pgrad=[True, True, True, True, False],
        name=f"mix_db_{tag}")
    g["mla_qn_nope"], g["mla_qn_rope"], g["mla_kn_nope"], g["mla_kn_rope"] = pg
    dcqn = _mm(dqraw, w["mla_w_uq"], tb=True, name=f"mla_dcq_{tag}")
    g["mla_w_uq"] = _mm(cqn, dqraw, ta=True, name=f"mla_dwuq_{tag}")
    dckvn = _mm(dkvraw, w["mla_w_ukv"], tb=True, name=f"mla_dckv_{tag}")
    g["mla_w_ukv"] = _mm(ckvn, dkvraw, ta=True, name=f"mla_dwukv_{tag}")
    dqf, dkk, dvv, dlogf = _hg_bwd(qf, kk, vv, logf, states, do_hg, name=f"hg_bwd_{tag}")
    pa = [w["lb0"], w["lb1"], w["mla_q_a_norm"], w["mla_kv_a_norm"]]
    (dproj,), (g["lb0"], g["lb1"], g["mla_q_a_norm"], g["mla_kv_a_norm"]) = _rows_bwd(
        _mix_a, [proj], pa, [dqf, dkk, dvv, dlogf, dcqn, dckvn], rgrad=[bf16], pgrad=[True] * 4,
        addends={0: [(dgate, 3 * HG_WIDTH), (dkpe_raw, _C_KPE)]}, name=f"mix_da_{tag}")
    dh, g["norm_mix"] = _mm(dproj, w["mix_w_in"], tb=True, rms_bwd=(h, w["norm_mix"], dout), name=f"mix_dhn_{tag}")
    g["mix_w_in_t"] = _mm(dproj, hn, ta=True, name=f"mix_dwin_{tag}")
    return dh, g


def _rope_tables(positions):
    inv_freq = 1.0 / (ROPE_BASE ** (jnp.arange(0, MLA_ROPE, 2, dtype=f32) / MLA_ROPE))
    ang = positions.astype(f32)[:, None] * inv_freq
    z = jnp.zeros((positions.shape[0], 128 - MLA_ROPE), f32)
    return (jnp.concatenate([jnp.cos(ang), jnp.cos(ang), z], axis=1),
            jnp.concatenate([jnp.sin(ang), jnp.sin(ang), z], axis=1))


def _pad_cols(a, n):
    return jnp.pad(a, ((0, 0), (0, n - a.shape[1])))


def _even_weights(p, j, layer, dt):
    w_uq = p["mla_w_uq"][j].reshape(MLA_Q_RANK, MLA_HEADS, MLA_QK)
    w_uq = jnp.pad(w_uq, ((0, 0), (0, 0), (0, MLA_QK_PAD - MLA_QK))).reshape(MLA_Q_RANK, MLA_HEADS * MLA_QK_PAD)
    return dict(
        norm_mix=p["norm_mix"][layer][None], mix_w_in=_pad_cols(p["mix_w_in"][j], IN_PAD).astype(dt),
        lb0=p["hg_lb_logits"][0][None], lb1=p["hg_lb_logits"][1][None],
        mla_q_a_norm=p["mla_q_a_norm"][j][None], mla_kv_a_norm=p["mla_kv_a_norm"][j][None],
        mla_w_uq=w_uq.astype(dt), mla_w_ukv=p["mla_w_ukv"][j].astype(dt),
        mla_qn_nope=p["mla_qn_nope"][j][None], mla_qn_rope=_pad_cols(p["mla_qn_rope"][j][None], 128),
        mla_kn_nope=p["mla_kn_nope"][j][None], mla_kn_rope=_pad_cols(p["mla_kn_rope"][j][None], 128),
        rope_perm=_rope_perm(), hg_out_norm=p["hg_out_norm"][j][None],
        mix_w_out=p["mix_w_out"][j].astype(dt) if "mix_w_out" in p else None)


def _even_grads(g):
    w_uq = g["mla_w_uq"].reshape(MLA_Q_RANK, MLA_HEADS, MLA_QK_PAD)[:, :, :MLA_QK].reshape(MLA_Q_RANK, -1)
    return dict(
        norm_mix=g["norm_mix"], mix_w_in=g["mix_w_in_t"][:IN_WIDTH].T[None],
        hg_lb_logits=jnp.concatenate([g["lb0"], g["lb1"]], axis=0),
        mla_q_a_norm=g["mla_q_a_norm"], mla_kv_a_norm=g["mla_kv_a_norm"], mla_w_uq=w_uq[None],
        mla_w_ukv=g["mla_w_ukv"][None], mla_qn_nope=g["mla_qn_nope"], mla_qn_rope=g["mla_qn_rope"][:, :MLA_ROPE],
        mla_kn_nope=g["mla_kn_nope"], mla_kn_rope=g["mla_kn_rope"][:, :MLA_ROPE],
        hg_out_norm=g["hg_out_norm"], mix_w_out=g["mix_w_out"][None])


_S5_NB = 8
_S5_BW = 1024
_S5_HALF = 512
_S5_UC = 128
_S5_TIME = 512


def _cmul(ar, ai, br, bi):
    return ar * br - ai * bi, ar * bi + ai * br


def _pow_table(ar, ai, descending):
    rows = lax.broadcasted_iota(jnp.int32, (8, ar.shape[1]), 0)
    tr = jnp.zeros((8, ar.shape[1]), f32)
    ti = jnp.zeros((8, ar.shape[1]), f32)
    pr, pi_ = ar, ai
    for r in range(8):
        sel = rows == ((7 - r) if descending else r)
        tr = jnp.where(sel, pr, tr)
        ti = jnp.where(sel, pi_, ti)
        pr, pi_ = _cmul(pr, pi_, ar, ai)
    return tr, ti


def _s5_tile_scan(work, carry, ar, ai, tc, reverse, per_tile=None):
    hw = _S5_HALF
    row8 = lax.broadcasted_iota(jnp.int32, (8, hw), 0)
    powers = [(ar, ai)]
    for _ in range(2):
        powers.append(_cmul(*powers[-1], *powers[-1]))
    steps = []
    for (mr, mi), s in zip(powers, (1, 2, 4)):
        ok = (row8 < 8 - s) if reverse else (row8 >= s)
        steps.append((jnp.where(ok, mr, 0.0), jnp.where(ok, mi, 0.0), 8 - s if reverse else s))
    tr, ti = _pow_table(ar, ai, reverse)
    cr, ci = carry[:, :hw], carry[:, hw:]
    tiles = range(tc // 8)
    for i in (reversed(tiles) if reverse else tiles):
        sl = slice(8 * i, 8 * i + 8)
        xr, xi = work[sl, :hw], work[sl, hw:]
        for mr, mi, shift in steps:
            pr, pi_ = _cmul(mr, mi, pltpu.roll(xr, shift, axis=0), pltpu.roll(xi, shift, axis=0))
            xr, xi = xr + pr, xi + pi_
        pr, pi_ = _cmul(tr, ti, cr, ci)
        xr, xi = xr + pr, xi + pi_
        work[sl, :hw] = xr
        work[sl, hw:] = xi
        if per_tile is not None:
            per_tile(sl, xr, xi, cr, ci)
        edge = 8 * i if reverse else 8 * i + 7
        cr, ci = work[edge:edge + 1, :hw], work[edge:edge + 1, hw:]
    carry[:, :hw] = cr
    carry[:, hw:] = ci


def _s5_core_fwd(a, hn, b3, c3, *, name):
    length = hn.shape[0]
    tc = min(_S5_TIME, length)

    def body(a_ref, hn_ref, b_ref, c_ref, hs_ref, y_ref, work, carry):
        @pl.when(pl.program_id(1) == 0)
        def _():
            carry[...] = jnp.zeros_like(carry)

        work[...] = jnp.dot(hn_ref[...].astype(bf16), b_ref[...], preferred_element_type=f32)
        _s5_tile_scan(work, carry, a_ref[:, :_S5_HALF], a_ref[:, _S5_HALF:], tc, False)
        hs = work[...].astype(bf16)
        hs_ref[...] = hs
        y_ref[...] = jnp.dot(hs, c_ref[...], preferred_element_type=f32)

    return pl.pallas_call(
        body, grid=(_S5_NB, length // tc),
        in_specs=[pl.BlockSpec((1, _S5_BW), lambda j, t: (0, j)), pl.BlockSpec((tc, _S5_UC), lambda j, t: (t, j)),
                  pl.BlockSpec((_S5_UC, _S5_BW), lambda j, t: (j, 0)), pl.BlockSpec((_S5_BW, _S5_UC), lambda j, t: (j, 0))],
        out_specs=[pl.BlockSpec((tc, _S5_BW), lambda j, t: (t, j)), pl.BlockSpec((tc, _S5_UC), lambda j, t: (t, j))],
        out_shape=[jax.ShapeDtypeStruct((length, _S5_NB * _S5_BW), bf16),
                   jax.ShapeDtypeStruct((length, _S5_NB * _S5_UC), f32)],
        scratch_shapes=[pltpu.VMEM((tc, _S5_BW), f32), pltpu.VMEM((1, _S5_BW), f32)],
        compiler_params=_cparams(("parallel", "arbitrary")), name=name)(a, hn, b3, c3)


def _s5_core_bwd(a, dy, c3, hs, hn, b3, *, name):
    length = hn.shape[0]
    tc = min(_S5_TIME, length)
    nt = length // tc
    hw = _S5_HALF

    def body(a_ref, dy_ref, c_ref, hs_ref, hn_ref, b_ref, du_ref, db_ref, dc_ref, da_ref, work, carry, acc):
        @pl.when(pl.program_id(1) == 0)
        def _():
            carry[...] = jnp.zeros_like(carry)
            db_ref[...] = jnp.zeros_like(db_ref)
            dc_ref[...] = jnp.zeros_like(dc_ref)
            da_ref[...] = jnp.zeros_like(da_ref)

        dyb = dy_ref[...].astype(bf16)
        work[...] = lax.dot_general(dyb, c_ref[...], _NT, preferred_element_type=f32)
        acc[...] = jnp.zeros_like(acc)
        row8 = lax.broadcasted_iota(jnp.int32, (8, hw), 0)

        def grad_a(sl, gr, gi, cr, ci):
            gnr = jnp.where(row8 == 7, cr, pltpu.roll(gr, 7, axis=0))
            gni = jnp.where(row8 == 7, ci, pltpu.roll(gi, 7, axis=0))
            hr, hi = hs_ref[sl, :hw].astype(f32), hs_ref[sl, hw:].astype(f32)
            acc[:, :hw] += hr * gnr + hi * gni
            acc[:, hw:] += hr * gni - hi * gnr

        _s5_tile_scan(work, carry, a_ref[:, :hw], -a_ref[:, hw:], tc, True, grad_a)
        da_ref[...] += jnp.sum(acc[...], axis=0, keepdims=True)
        g = work[...].astype(bf16)
        du_ref[...] = lax.dot_general(g, b_ref[...], _NT, preferred_element_type=f32)
        db_ref[...] += lax.dot_general(hn_ref[...].astype(bf16), g, _TN, preferred_element_type=f32)
        dc_ref[...] += lax.dot_general(hs_ref[...], dyb, _TN, preferred_element_type=f32)

    rev = lambda j, t: (nt - 1 - t, j)
    return pl.pallas_call(
        body, grid=(_S5_NB, nt),
        in_specs=[pl.BlockSpec((1, _S5_BW), lambda j, t: (0, j)), pl.BlockSpec((tc, _S5_UC), rev),
                  pl.BlockSpec((_S5_BW, _S5_UC), lambda j, t: (j, 0)), pl.BlockSpec((tc, _S5_BW), rev),
                  pl.BlockSpec((tc, _S5_UC), rev), pl.BlockSpec((_S5_UC, _S5_BW), lambda j, t: (j, 0))],
        out_specs=[pl.BlockSpec((tc, _S5_UC), rev), pl.BlockSpec((_S5_UC, _S5_BW), lambda j, t: (j, 0)),
                   pl.BlockSpec((_S5_BW, _S5_UC), lambda j, t: (j, 0)), pl.BlockSpec((1, _S5_BW), lambda j, t: (0, j))],
        out_shape=[jax.ShapeDtypeStruct((length, _S5_NB * _S5_UC), f32),
                   jax.ShapeDtypeStruct((_S5_NB * _S5_UC, _S5_BW), f32),
                   jax.ShapeDtypeStruct((_S5_NB * _S5_BW, _S5_UC), f32),
                   jax.ShapeDtypeStruct((1, _S5_NB * _S5_BW), f32)],
        scratch_shapes=[pltpu.VMEM((tc, _S5_BW), f32), pltpu.VMEM((1, _S5_BW), f32), pltpu.VMEM((8, _S5_BW), f32)],
        compiler_params=_cparams(("parallel", "arbitrary")), name=name)(a, dy, c3, hs, hn, b3)


def _s5_disc(lr, li, ldt, btr, bti, expand):
    dt = jnp.exp(ldt)
    mag = jnp.exp(lr * dt)
    abr = mag * jnp.cos(li * dt)
    abi = mag * jnp.sin(li * dt)
    den = lr * lr + li * li
    zr = ((abr - 1.0) * lr + abi * li) / den
    zi = (abi * lr - (abr - 1.0) * li) / den
    zr = jnp.dot(zr, expand, precision=lax.Precision.HIGHEST, preferred_element_type=f32)
    zi = jnp.dot(zi, expand, precision=lax.Precision.HIGHEST, preferred_element_type=f32)
    return abr, abi, zr * btr - zi * bti, zr * bti + zi * btr


def _s5_disc_fwd(args, *, name):
    def body(*refs):
        res = _s5_disc(*[r[...] for r in refs[:6]])
        for o, v in zip(refs[6:], res):
            o[...] = v

    sds = jax.ShapeDtypeStruct
    return pl.pallas_call(body, out_shape=[sds(args[0].shape, f32)] * 2 + [sds(args[3].shape, f32)] * 2,
                          name=name)(*args)


def _s5_disc_bwd(args, cts, *, name):
    def body(*refs):
        vals = [r[...] for r in refs[:6]]
        _, vjp = jax.vjp(lambda *d: _s5_disc(*d, vals[5]), *vals[:5])
        grads = vjp(tuple(r[...] for r in refs[6:10]))
        for o, v in zip(refs[10:], grads):
            o[...] = v

    return pl.pallas_call(body, out_shape=[jax.ShapeDtypeStruct(a.shape, f32) for a in args[:5]],
                          name=name)(*args, *cts)


def _gelu_tanh(x):
    return 0.5 * x * (1.0 + jnp.tanh(0.7978845608028654 * (x + 0.044715 * (x * x * x))))


def _s5_post(y, u, d_skip):
    return _gelu_tanh(y + d_skip * u)


def _s5_glu(ga, gb, h):
    return h + ga * jax.nn.sigmoid(gb)


def _s5_glu_norm(ga, gb, h, next_gain):
    out = _s5_glu(ga, gb, h)
    return out, _rms(out, next_gain)


def _s5_expand():
    e = np.zeros((S5_STATE, S5_GROUP * S5_STATE), np.float32)
    for m in range(S5_GROUP):
        e[np.arange(S5_STATE), m * S5_STATE + np.arange(S5_STATE)] = 1.0
    return jnp.asarray(e)


def _s5_pack_b(bbr, bbi):
    eye = jnp.eye(8, dtype=f32)

    def one(bb):
        b5 = bb.reshape(_S5_NB, 8, S5_GROUP, S5_STATE)
        return jnp.einsum("jgmp,gh->jgmhp", b5, eye).reshape(_S5_NB * _S5_UC, _S5_HALF)

    return jnp.concatenate([one(bbr), one(bbi)], axis=1)


def _s5_unpack_b(db3):
    def one(d):
        d5 = d.reshape(_S5_NB, 8, S5_GROUP, 8, S5_STATE)
        return jnp.einsum("jgmgp->jgmp", d5).reshape(S5_GROUPS, S5_GROUP * S5_STATE)

    return one(db3[:, :_S5_HALF]), one(db3[:, _S5_HALF:])


def _s5_pack_c(c_re, c_im):
    eye = jnp.eye(8, dtype=f32)

    def one(c):
        c4 = c.reshape(_S5_NB, 8, S5_GROUP, S5_STATE)
        return jnp.einsum("jgmp,hg->jhpgm", c4, eye).reshape(_S5_NB, _S5_HALF, _S5_UC)

    return jnp.concatenate([one(c_re), -one(c_im)], axis=1).reshape(_S5_NB * _S5_BW, _S5_UC)


def _s5_unpack_c(dc3):
    d = dc3.reshape(_S5_NB, 2, 8, S5_STATE, 8, S5_GROUP)
    dre = jnp.einsum("jgpgm->jgmp", d[:, 0]).reshape(S5_GROUPS, S5_GROUP, S5_STATE)
    dim = -jnp.einsum("jgpgm->jgmp", d[:, 1]).reshape(S5_GROUPS, S5_GROUP, S5_STATE)
    return dre, dim


def _s5_state_row(re, im):
    r = re.reshape(_S5_NB, 1, _S5_HALF)
    i = im.reshape(_S5_NB, 1, _S5_HALF)
    return jnp.concatenate([r, i], axis=2).reshape(1, _S5_NB * _S5_BW)


def _s5_unstate_row(row):
    r = row.reshape(_S5_NB, 2, 8, S5_STATE)
    return r[:, 0].reshape(S5_GROUPS, S5_STATE), r[:, 1].reshape(S5_GROUPS, S5_STATE)


def _s5_fwd(h, w, tag, next_gain):
    d = h.shape[1]
    hn, = _rows(_rms, [h], [w["norm_mix"]], [(d, f32)], name=f"s5_norm_{tag}")
    disc_in = [w["s5_lam_re"], w["s5_lam_im"], w["s5_log_dt"], w["s5_bt_re"], w["s5_bt_im"], w["s5_expand"]]
    abr, abi, bbr, bbi = _s5_disc_fwd(disc_in, name=f"s5_disc_{tag}")
    a_row = _s5_state_row(abr, abi)
    b3 = _s5_pack_b(bbr, bbi).astype(bf16)
    hs, y = _s5_core_fwd(a_row, hn, b3, w["s5_c3"], name=f"s5_core_{tag}")
    yg, = _rows(_s5_post, [y, hn], [w["s5_d"]], [(d, bf16)], name=f"s5_post_{tag}")
    ga = _mm(yg, w["s5_w_glu_a"], name=f"s5_glu_a_{tag}")
    gb = _mm(yg, w["s5_w_glu_b"], name=f"s5_glu_b_{tag}")
    out, normed = _rows(_s5_glu_norm, [ga, gb, h], [next_gain], [(d, f32), (d, bf16)], name=f"s5_glu_{tag}")
    return out, (h, hn, disc_in, a_row, b3, hs, y, yg, ga, gb), normed


def _s5_bwd(dout, w, saved, tag):
    h, hn, disc_in, a_row, b3, hs, y, yg, ga, gb = saved
    g = {}
    (dga, dgb), _ = _rows_bwd(_s5_glu, [ga, gb, h], [], [dout], rgrad=[bf16, bf16, None], pgrad=[],
                              name=f"s5_dglu_{tag}")
    dyg = _mm(dga, w["s5_w_glu_a"], tb=True, name=f"s5_dyg_a_{tag}")
    dyg = _mm(dgb, w["s5_w_glu_b"], tb=True, add=dyg, name=f"s5_dyg_b_{tag}")
    g["s5_w_glu_a"] = _mm(yg, dga, ta=True, name=f"s5_dwa_{tag}")
    g["s5_w_glu_b"] = _mm(yg, dgb, ta=True, name=f"s5_dwb_{tag}")
    (dy, du_skip), (g["s5_d"],) = _rows_bwd(_s5_post, [y, hn], [w["s5_d"]], [dyg], rgrad=[bf16, f32], pgrad=[True],
                                           name=f"s5_dpost_{tag}")
    du, db3, dc3, da_row = _s5_core_bwd(a_row, dy, w["s5_c3"], hs, hn, b3, name=f"s5_dcore_{tag}")
    dabr, dabi = _s5_unstate_row(da_row)
    dbbr, dbbi = _s5_unpack_b(db3)
    g["s5_lam_re"], g["s5_lam_im"], g["s5_log_dt"], g["s5_bt_re"], g["s5_bt_im"] = _s5_disc_bwd(
        disc_in, [dabr, dabi, dbbr, dbbi], name=f"s5_ddisc_{tag}")
    g["s5_c_re"], g["s5_c_im"] = _s5_unpack_c(dc3)
    (dh,), (g["norm_mix"],) = _rows_bwd(_rms_twice, [h], [w["norm_mix"]], [du, du_skip], rgrad=[f32], pgrad=[True],
                                        addends={0: dout}, name=f"s5_dnorm_{tag}")
    return dh, g


def _odd_weights(p, j, layer, dt):
    tr = lambda b: b.transpose(0, 2, 1).reshape(S5_GROUPS, S5_GROUP * S5_STATE)
    return dict(
        norm_mix=p["norm_mix"][layer][None], s5_lam_re=p["s5_lam_re"][j], s5_lam_im=p["s5_lam_im"][j],
        s5_log_dt=p["s5_log_dt"][j][:, None], s5_bt_re=tr(p["s5_b_re"][j]), s5_bt_im=tr(p["s5_b_im"][j]),
        s5_expand=_s5_expand(), s5_c3=_s5_pack_c(p["s5_c_re"][j], p["s5_c_im"][j]).astype(dt),
        **{n: (p[n][j][None] if n == "s5_d" else p[n][j].astype(dt))
           for n in ("s5_d", "s5_w_glu_a", "s5_w_glu_b") if n in p})


def _odd_grads(g):
    tr = lambda b: b.reshape(S5_GROUPS, S5_GROUP, S5_STATE).transpose(0, 2, 1)[None]
    return dict(
        norm_mix=g["norm_mix"], s5_lam_re=g["s5_lam_re"][None], s5_lam_im=g["s5_lam_im"][None],
        s5_log_dt=g["s5_log_dt"][:, 0][None], s5_b_re=tr(g["s5_bt_re"]), s5_b_im=tr(g["s5_bt_im"]),
        s5_c_re=g["s5_c_re"][None], s5_c_im=g["s5_c_im"][None], s5_d=g["s5_d"],
        s5_w_glu_a=g["s5_w_glu_a"][None], s5_w_glu_b=g["s5_w_glu_b"][None])


def _loss_fn(y, t):
    e = y - t
    part = jnp.sum(jnp.sum(e * e, axis=-1, keepdims=True), axis=0, keepdims=True) * (0.5 / y.shape[1])
    return e * (1.0 / y.shape[1]), part


FF_SHARD = 352
FF_SHARD_PAD = 384


def _pad_groups(a, axis):
    axis %= a.ndim
    zeros = jnp.zeros(a.shape[:axis] + (FF_SHARD_PAD - FF_SHARD,) + a.shape[axis + 1:], a.dtype)
    pieces = []
    for g in range(a.shape[axis] // FF_SHARD):
        pieces += [lax.slice_in_dim(a, g * FF_SHARD, (g + 1) * FF_SHARD, axis=axis), zeros]
    return jnp.concatenate(pieces, axis=axis)


def _unpad_groups(a, axis):
    axis %= a.ndim
    pieces = [lax.slice_in_dim(a, g * FF_SHARD_PAD, g * FF_SHARD_PAD + FF_SHARD, axis=axis)
              for g in range(a.shape[axis] // FF_SHARD_PAD)]
    return pieces[0] if len(pieces) == 1 else jnp.concatenate(pieces, axis=axis)


def _layer_weights(p, layer, dt):
    return dict(
        norm_xa=p["norm_xa"][layer][None], norm_mem=p["norm_mem"][layer][None], norm_ffn=p["norm_ffn"][layer][None],
        xa_wq=p["xa_wq"][layer].astype(dt), xa_wk=p["xa_wk"][layer].astype(dt), xa_wv=p["xa_wv"][layer].astype(dt),
        xa_wo=p["xa_wo"][layer].astype(dt), xa_q_norm=p["xa_q_norm"][layer][None],
        xa_k_norm=p["xa_k_norm"][layer][None], ffn_w_up=_pad_groups(p["ffn_w_up"][layer], 1).astype(dt),
        ffn_conv_w=_pad_groups(p["ffn_conv_w"][layer], 1), ffn_conv_b=_pad_groups(p["ffn_conv_b"][layer][None], 1),
        ffn_w_down=_pad_groups(p["ffn_w_down"][layer], 0).astype(dt))


_PER_LAYER = ("norm_xa", "norm_mem", "norm_ffn", "xa_wq", "xa_wk", "xa_wv", "xa_wo", "xa_q_norm", "xa_k_norm",
              "ffn_w_up", "ffn_conv_w", "ffn_conv_b", "ffn_w_down")
_FFN_PADDED = dict(ffn_w_up=1, ffn_conv_w=1, ffn_conv_b=1, ffn_w_down=0)


def _local_step(x, mem, positions, target, p):
    cos, sin = _rope_tables(positions)
    we = _even_weights(p, 0, 0, bf16)
    wo = _odd_weights(p, 0, 1, bf16)
    wl = [_layer_weights(p, layer, bf16) for layer in range(2)]
    loss, dh, g_even, g_odd, gl = _local_core(x, mem, cos, sin, target, we, wo, wl)
    grads = {}
    for n in _PER_LAYER:
        a, b = gl[0][n], gl[1][n]
        if n in _FFN_PADDED:
            a, b = _unpad_groups(a, _FFN_PADDED[n]), _unpad_groups(b, _FFN_PADDED[n])
        grads[n] = jnp.concatenate([a, b], axis=0) if a.shape[0] == 1 else jnp.stack([a, b])
    ge, go = _even_grads(g_even), _odd_grads(g_odd)
    grads["norm_mix"] = jnp.concatenate([ge.pop("norm_mix"), go.pop("norm_mix")], axis=0)
    grads.update(ge)
    grads.update(go)
    return loss, dh, grads


def _local_core(x, mem, cos, sin, target, we, wo, wl):
    h, s_mix0, hx = _mixer_fwd(x, cos, sin, we, "l0", wl[0]["norm_xa"])
    h, s_xa0, hf = _xattn_fwd(h, mem, wl[0], "l0", hx)
    h, s_ff0 = _ffn_fwd(h, wl[0], "l0", hf)
    h, s_mix1, hx = _s5_fwd(h, wo, "l1", wl[1]["norm_xa"])
    h, s_xa1, hf = _xattn_fwd(h, mem, wl[1], "l1", hx)
    h, s_ff1 = _ffn_fwd(h, wl[1], "l1", hf)
    dh, loss = _rows(_loss_fn, [h, target], [], [(h.shape[1], f32)], accs=[(1, 1)], name="loss_head")

    gl = [{}, {}]
    dh, g = _ffn_bwd(dh, wl[1], s_ff1, "l1")
    gl[1].update(g)
    dh, g = _xattn_bwd(dh, mem, wl[1], s_xa1, "l1")
    gl[1].update(g)
    dh, g_odd = _s5_bwd(dh, wo, s_mix1, "l1")
    dh, g = _ffn_bwd(dh, wl[0], s_ff0, "l0")
    gl[0].update(g)
    dh, g = _xattn_bwd(dh, mem, wl[0], s_xa0, "l0")
    gl[0].update(g)
    dh, g_even = _mixer_bwd(dh, cos, sin, we, s_mix0, "l0")
    return loss, dh, g_even, g_odd, gl


_LANES = 1024
_ROW_PAD = 16


_PEER_MASKS = (1, 2, 4, 3, 5, 6, 7)


def _mesh_place():
    x, y, c = lax.axis_index("x"), lax.axis_index("y"), lax.axis_index("c")

    def peer(mask):
        px = 1 - x if mask & 4 else x
        py = 1 - y if mask & 2 else y
        pc = 1 - c if mask & 1 else c
        return (px, py, pc), 4 * px + 2 * py + pc

    return 4 * x + 2 * y + c, peer


class _Exchange:
    def __init__(self, name):
        self.name = name
        self.srcs, self.shapes, self.items, self.where = [], [], [], {}

    def add(self, src, land_shape, src_at, dst_at, key):
        si = next((i for i, s in enumerate(self.srcs) if s is src), None)
        if si is None:
            self.srcs.append(src)
            si = len(self.srcs) - 1
        if key not in self.where:
            self.shapes.append(land_shape)
            self.where[key] = len(self.shapes) - 1
        self.items.append(dict(src=si, dst=self.where[key], src_at=src_at, dst_at=dst_at))

    def _copy(self, k, mask, ins, lands, send_sems, recv_sems, me, peer, arriving):
        it = self.items[k]
        dev, idx = peer(mask)
        s = k * (N_DEV - 1) + _PEER_MASKS.index(mask)
        return pltpu.make_async_remote_copy(
            src_ref=it["src_at"](ins[it["src"]], idx), dst_ref=it["dst_at"](lands[it["dst"]], idx if arriving else me),
            send_sem=send_sems.at[s], recv_sem=recv_sems.at[s], device_id=dev, device_id_type=pl.DeviceIdType.MESH)

    def _own_copy(self, k, ins, lands, own_sems, me):
        it = self.items[k]
        return pltpu.make_async_copy(it["src_at"](ins[it["src"]], me), it["dst_at"](lands[it["dst"]], me), own_sems.at[k])

    def begin(self, own):
        ns, nd, ni = len(self.srcs), len(self.shapes), len(self.items)
        nsem = ni * (N_DEV - 1)
        self.own = own

        nq = 3 if own else 2

        def body(*refs):
            ins, land_refs = refs[:ns], refs[ns:ns + nd]
            sems, token = refs[ns + nd:ns + nd + nq], refs[-1]
            me, peer = _mesh_place()
            for mask in _PEER_MASKS:
                for k in range(ni):
                    self._copy(k, mask, ins, land_refs, sems[0], sems[1], me, peer, False).start()
            if own:
                for k in range(ni):
                    self._own_copy(k, ins, land_refs, sems[2], me).start()
            token[...] = jnp.zeros_like(token)

        hbm = pl.BlockSpec(memory_space=pltpu.HBM)
        sem = pl.BlockSpec(memory_space=pltpu.SEMAPHORE)
        lands = [lax.empty(s.shape, s.dtype) for s in self.shapes]
        sem_shapes = [pltpu.SemaphoreType.DMA((nsem,)), pltpu.SemaphoreType.DMA((nsem,)), pltpu.SemaphoreType.DMA((ni,))]
        res = pl.pallas_call(
            body, in_specs=[hbm] * (ns + nd),
            out_specs=[sem] * nq + [hbm] * nd + [pl.BlockSpec(memory_space=pltpu.VMEM)],
            out_shape=sem_shapes[:nq] + [pltpu.HBM(s.shape, s.dtype) for s in self.shapes]
            + [jax.ShapeDtypeStruct((8, 128), f32)],
            input_output_aliases={ns + j: nq + j for j in range(nd)},
            compiler_params=pltpu.CompilerParams(has_side_effects=pltpu.SideEffectType.DATAFLOW_SIDE_EFFECTING),
            name=self.name + "_start")(*self.srcs, *lands)
        self.token = res[-1]
        return list(res[:nq]), list(res[nq:-1])

    def finish(self, state, after):
        sems, lands = state
        nq = len(sems)
        after = list(after) if isinstance(after, (list, tuple)) else [after]
        ns, nd, ni = len(self.srcs), len(self.shapes), len(self.items)

        def body(*refs):
            ins, land_refs = refs[:ns], refs[ns:ns + nd]
            sem_refs = refs[ns + nd:ns + nd + nq]
            me, peer = _mesh_place()
            for mask in _PEER_MASKS:
                for k in range(ni):
                    cp = self._copy(k, mask, ins, land_refs, sem_refs[0], sem_refs[1], me, peer, True)
                    cp.wait_send()
                    cp.wait_recv()
            if self.own:
                for k in range(ni):
                    self._own_copy(k, ins, land_refs, sem_refs[2], me).wait()

        hbm = pl.BlockSpec(memory_space=pltpu.HBM)
        sem = pl.BlockSpec(memory_space=pltpu.SEMAPHORE)
        res = pl.pallas_call(
            body, in_specs=[hbm] * (ns + nd) + [sem] * nq + [pl.BlockSpec(memory_space=pl.ANY)] * len(after),
            out_specs=[hbm] * nd, out_shape=[pltpu.HBM(s.shape, s.dtype) for s in self.shapes],
            input_output_aliases={ns + j: j for j in range(nd)},
            compiler_params=pltpu.CompilerParams(has_side_effects=pltpu.SideEffectType.DATAFLOW_SIDE_EFFECTING),
            name=self.name + "_wait")(*self.srcs, *lands, *sems, *after)
        return {k: res[i] for k, i in self.where.items()}


def _after(x, *tokens, name):
    def body(*refs):
        del refs

    anyspace = pl.BlockSpec(memory_space=pl.ANY)
    return pl.pallas_call(body, in_specs=[anyspace] * (1 + len(tokens)), out_specs=anyspace,
                          out_shape=jax.ShapeDtypeStruct(x.shape, x.dtype), input_output_aliases={0: 0},
                          name=name)(x, *tokens)


def _rows_of(n):
    return lambda r, i: r.at[pl.ds(pl.multiple_of(i * n, n), n), :]


def _cols_of(n):
    return lambda r, i: r.at[:, pl.ds(pl.multiple_of(i * n, n), n)]


def _whole(r, i):
    return r


def _slot(r, i):
    return r.at[i]


def _at_layer(layer):
    return lambda r, i: r.at[layer]


def _sum_adam(me_index, slots, owns, own_block, w, m, v, *, name):
    layers, rows, cols = w.shape
    tr = _pick(rows, (256, 128, 104, 64, 32, 16, 8))
    bc1 = 1.0 - ADAM_B1 ** ADAM_STEP
    bc2 = 1.0 - ADAM_B2 ** ADAM_STEP
    own_shape, own_map = own_block(tr)
    nl = len(slots)
    assert nl == layers and len(owns) == layers

    def body(me_ref, *refs):
        s_refs, own_refs = refs[:nl], refs[nl:2 * nl]
        w_ref, m_ref, v_ref, g_ref, d_ref, nm_ref, nv_ref = refs[2 * nl:]
        me = me_ref[0]

        def run(s_ref, own_ref):
            mine = own_ref[0] if len(own_shape) == 3 else own_ref[...]
            g = jnp.where(me == 0, mine, s_ref[0])
            for k in range(1, N_DEV):
                g = g + jnp.where(me == k, mine, s_ref[k])
            mm = ADAM_B1 * m_ref[0] + (1.0 - ADAM_B1) * g
            vv = ADAM_B2 * v_ref[0] + (1.0 - ADAM_B2) * (g * g)
            g_ref[0] = g
            nm_ref[0] = mm
            nv_ref[0] = vv
            d_ref[0] = -ADAM_LR * ((mm / bc1) / (jnp.sqrt(vv / bc2) + ADAM_EPS) + ADAM_WD * w_ref[0])

        for layer in range(nl):
            pl.when(pl.program_id(0) == layer)(functools.partial(run, s_refs[layer], own_refs[layer]))

    def of_layer(layer, index_map):
        return lambda lyr, i, me: index_map(jnp.where(lyr == layer, i, 0), me)

    blk = pl.BlockSpec((1, tr, cols), lambda lyr, i, me: (lyr, i, 0))
    sds = jax.ShapeDtypeStruct((layers, rows, cols), f32)
    grid_spec = pltpu.PrefetchScalarGridSpec(
        num_scalar_prefetch=1, grid=(layers, rows // tr),
        in_specs=[pl.BlockSpec((N_DEV, tr, cols), of_layer(layer, lambda i, me: (0, i, 0))) for layer in range(nl)]
        + [pl.BlockSpec(own_shape, of_layer(layer, own_map)) for layer in range(nl)] + [blk, blk, blk],
        out_specs=[blk] * 4)
    return pl.pallas_call(body, grid_spec=grid_spec, out_shape=[sds] * 4,
                          compiler_params=_cparams(("arbitrary", "arbitrary")),
                          name=name)(me_index, *slots, *owns, w, m, v)


_SHARDED = dict(xa_wq=1, xa_wk=1, xa_wv=1, xa_wo=1, ffn_w_up=2, ffn_conv_w=2, ffn_w_down=1, mix_w_in=2, mla_w_uq=2,
                mla_w_ukv=2, mix_w_out=1, s5_d=1, s5_w_glu_a=1, s5_w_glu_b=1)
_EXACT = ("ffn_conv_w", "s5_d")
_WEIGHTS = ("norm_mix", "norm_xa", "norm_mem", "norm_ffn", "xa_wq", "xa_wk", "xa_wv", "xa_wo", "xa_q_norm",
            "xa_k_norm", "ffn_w_up", "ffn_conv_w", "ffn_conv_b", "ffn_w_down", "hg_lb_logits", "mix_w_in",
            "hg_out_norm", "mla_q_a_norm", "mla_w_uq", "mla_kv_a_norm", "mla_w_ukv", "mla_qn_nope", "mla_qn_rope",
            "mla_kn_nope", "mla_kn_rope", "mix_w_out", "s5_lam_re", "s5_lam_im", "s5_log_dt", "s5_b_re", "s5_b_im",
            "s5_c_re", "s5_c_im", "s5_d", "s5_w_glu_a", "s5_w_glu_b")
_BIG = tuple(n for n in _WEIGHTS if n in _SHARDED and n not in _EXACT)
_SHARD_ORDER = tuple(n for n in _WEIGHTS if n in _SHARDED)
_REPL_ORDER = tuple(n for n in _WEIGHTS if n not in _SHARDED)
_REPL_EARLY = tuple(n for n in _REPL_ORDER if n.startswith("s5_"))
_REPL_LATE = tuple(n for n in _REPL_ORDER if n not in _REPL_EARLY)


def _pack(parts, dtype, lead=None):
    nl = 0 if lead is None else 1
    flat = [a.astype(dtype).reshape(a.shape[:nl] + (-1,)) for a in parts]
    cat = jnp.concatenate(flat, axis=nl)
    n = cat.shape[nl]
    unit = _LANES * _ROW_PAD
    total = -(-n // unit) * unit
    cat = jnp.pad(cat, [(0, 0)] * nl + [(0, total - n)])
    return cat.reshape(cat.shape[:nl] + (total // _LANES, _LANES))


def _unpack(packed, shapes, lead=None):
    nl = 0 if lead is None else 1
    flat = packed.reshape(packed.shape[:nl] + (-1,))
    out, off = [], 0
    for s in shapes:
        n = int(np.prod(s))
        piece = flat[..., off:off + n] if nl else flat[off:off + n]
        out.append(piece.reshape(packed.shape[:nl] + tuple(s)))
        off += n
    return out


def _to_full(gathered, axis):
    g = jnp.moveaxis(gathered, 0, axis)
    s = g.shape
    return g.reshape(s[:axis] + (s[axis] * s[axis + 1],) + s[axis + 2:])


def _to_shards(full, axis):
    s = full.shape
    g = full.reshape(s[:axis] + (N_DEV, s[axis] // N_DEV) + s[axis + 1:])
    return jnp.moveaxis(g, axis, 0)


_DIRECT_ROWS = ("xa_wq", "xa_wk", "xa_wv", "xa_wo", "mix_w_out", "s5_w_glu_a", "s5_w_glu_b")
_SMALL16 = ("mix_w_in", "mla_w_uq", "mla_w_ukv")
_SMALL_SHARDED = ("mla_w_uq", "mla_w_ukv") + _EXACT
_SHARD_ROWS = 128


def _exchange_layout(d):
    out = dict(d)
    out["ffn_w_up"] = _pad_groups(d["ffn_w_up"], 2)
    out["ffn_conv_w"] = _pad_groups(d["ffn_conv_w"], 2)
    out["ffn_w_down"] = _pad_groups(d["ffn_w_down"], 1)
    return out


def _train_step(x, mem, positions, target, w, m, v):
    d_model = x.shape[1]
    we_, me_, ve_ = _exchange_layout(w), _exchange_layout(m), _exchange_layout(v)
    sds = jax.ShapeDtypeStruct

    matrices = _DIRECT_ROWS + ("ffn_w_up", "ffn_w_down")
    layer_mats = ("xa_wq", "xa_wk", "xa_wv", "xa_wo", "ffn_w_up", "ffn_w_down")
    shard16 = {n: we_[n].astype(bf16) for n in matrices}
    part_of = {n: _rows_of(_SHARD_ROWS) for n in _DIRECT_ROWS}
    part_of["ffn_w_up"] = _cols_of(we_["ffn_w_up"].shape[2])
    part_of["ffn_w_down"] = _rows_of(we_["ffn_w_down"].shape[1])
    part_shape = {n: we_[n].shape[1:] for n in matrices}

    def full_shape(n):
        r, c = part_shape[n]
        return (r, N_DEV * c) if n == "ffn_w_up" else (N_DEV * r, c)

    def gather(ex, n, layer):
        ex.add(shard16[n], sds(full_shape(n), bf16), _at_layer(layer), part_of[n], (n, layer))

    def scatter(ex, n, layer, grad):
        ex.add(grad, sds((N_DEV,) + part_shape[n], f32), part_of[n], _slot, (n, layer))

    small16 = _pack([we_[n] for n in _SMALL16], bf16)
    exact = _pack([we_[n] for n in _EXACT], f32)
    ga, ga1, gb, gc = _Exchange("gather_a"), _Exchange("gather_a1"), _Exchange("gather_b"), _Exchange("gather_c")
    ga.add(small16, sds((N_DEV,) + small16.shape, bf16), _whole, _slot, "small16")
    ga1.add(exact, sds((N_DEV,) + exact.shape, f32), _whole, _slot, "exact")
    gather(ga1, "mix_w_out", 0)
    for n in layer_mats:
        gather(gb, n, 0)
    gather(gc, "s5_w_glu_a", 0)
    gather(gc, "s5_w_glu_b", 0)
    for n in layer_mats:
        gather(gc, n, 1)
    state_a, state_a1, state_b, state_c = ga.begin(True), ga1.begin(True), gb.begin(True), gc.begin(True)

    p = {n: w[n] for n in _REPL_ORDER}
    cos, sin = _rope_tables(positions)
    wo = _odd_weights(p, 0, 1, bf16)
    conv_b = _pad_groups(w["ffn_conv_b"], 1)
    prepared = [cos, sin, conv_b, wo["s5_c3"], wo["s5_bt_re"], wo["s5_bt_im"]]
    full = ga.finish(state_a, [ga1.token, gb.token, gc.token] + prepared)
    for n, a in zip(_SMALL16, _unpack(full["small16"], [we_[n].shape for n in _SMALL16], lead=True)):
        p[n] = _to_full(a, _SHARDED[n])
    we = _even_weights(p, 0, 0, bf16)
    we["norm_mix"] = _after(we["norm_mix"], ga.token, ga1.token, gb.token, gc.token, name="after_gather_starts")

    def late_mix_w_out(mixin):
        full.update(ga1.finish(state_a1, [mixin]))
        return full[("mix_w_out", 0)]

    we["mix_w_out"] = late_mix_w_out
    h, s_mix0, hx = _mixer_fwd(x, cos, sin, we, "l0", w["norm_xa"][0][None])
    conv_w, s5_d = [_to_full(a, _SHARDED[n]) for n, a in
                    zip(_EXACT, _unpack(full["exact"], [we_[n].shape for n in _EXACT], lead=True))]

    def layer_weights(layer):
        return dict(norm_xa=w["norm_xa"][layer][None], norm_mem=w["norm_mem"][layer][None],
                    norm_ffn=w["norm_ffn"][layer][None], xa_q_norm=w["xa_q_norm"][layer][None],
                    xa_k_norm=w["xa_k_norm"][layer][None], ffn_conv_w=conv_w[layer],
                    ffn_conv_b=conv_b[layer][None], **{n: full[(n, layer)] for n in layer_mats})

    full.update(gb.finish(state_b, h))
    wl = [layer_weights(0)]
    h, s_xa0, hf = _xattn_fwd(h, mem, wl[0], "l0", hx)
    h, s_ff0 = _ffn_fwd(h, wl[0], "l0", hf)
    full.update(gc.finish(state_c, h))
    wl.append(layer_weights(1))
    wo.update(s5_d=s5_d, s5_w_glu_a=full[("s5_w_glu_a", 0)], s5_w_glu_b=full[("s5_w_glu_b", 0)])
    h, s_mix1, hx = _s5_fwd(h, wo, "l1", wl[1]["norm_xa"])
    h, s_xa1, hf = _xattn_fwd(h, mem, wl[1], "l1", hx)
    h, s_ff1 = _ffn_fwd(h, wl[1], "l1", hf)
    dh, loss = _rows(_loss_fn, [h, target], [], [(h.shape[1], f32)], accs=[(1, 1)], name="loss_head")

    gl = [{}, {}]
    reduces = []

    own_grad = {}

    def reduce_start(name, entries, dh):
        ex = _Exchange(name)
        for n, layer, grad in entries.get("matrices", ()):
            scatter(ex, n, layer, grad)
            own_grad[(n, layer)] = grad
        for key, src, shape, src_at in entries.get("packs", ()):
            ex.add(src, shape, src_at, _slot, key)
        reduces.append((ex, ex.begin(False)))
        return _after(dh, ex.token, name="after_" + name)

    dh, gl[1] = _ffn_bwd(dh, wl[1], s_ff1, "l1")
    dh = reduce_start("reduce_ffn1", dict(matrices=[(n, 1, gl[1][n]) for n in ("ffn_w_up", "ffn_w_down")]), dh)
    dh, g = _xattn_bwd(dh, mem, wl[1], s_xa1, "l1")
    gl[1].update(g)
    dh = reduce_start("reduce_xa1", dict(matrices=[(n, 1, g[n]) for n in ("xa_wq", "xa_wk", "xa_wv", "xa_wo")]), dh)
    dh, g_odd = _s5_bwd(dh, wo, s_mix1, "l1")
    go = _odd_grads(g_odd)
    dh, gl[0] = _ffn_bwd(dh, wl[0], s_ff0, "l0")
    send_early = _pack([go[n].reshape(w[n].shape) for n in _REPL_EARLY], f32)
    dh = reduce_start("reduce_ffn0", dict(
        matrices=[(n, 0, g_odd[n]) for n in ("s5_w_glu_a", "s5_w_glu_b")]
        + [(n, 0, gl[0][n]) for n in ("ffn_w_up", "ffn_w_down")],
        packs=[("repl_early", send_early, sds((N_DEV,) + send_early.shape, f32), _whole)]), dh)
    dh, g = _xattn_bwd(dh, mem, wl[0], s_xa0, "l0")
    gl[0].update(g)
    dh = reduce_start("reduce_xa0", dict(matrices=[(n, 0, g[n]) for n in ("xa_wq", "xa_wk", "xa_wv", "xa_wo")]), dh)
    grad_x, g_even = _mixer_bwd(
        dh, cos, sin, we, s_mix0, "l0",
        on_w_out=lambda grad, dmixin: reduce_start("reduce_w_out", dict(matrices=[("mix_w_out", 0, grad)]), dmixin))

    ge = _even_grads(g_even)
    cat = lambda n: jnp.concatenate([gl[0][n], gl[1][n]], axis=0)
    rg = dict(ge)
    rg["norm_mix"] = jnp.concatenate([ge["norm_mix"], go["norm_mix"]], axis=0)
    for n in ("norm_xa", "norm_mem", "norm_ffn", "xa_q_norm", "xa_k_norm"):
        rg[n] = cat(n)
    rg["ffn_conv_b"] = _unpad_groups(cat("ffn_conv_b"), 1)
    sg = dict(mla_w_uq=ge["mla_w_uq"], mla_w_ukv=ge["mla_w_ukv"], s5_d=go["s5_d"],
              ffn_conv_w=jnp.stack([gl[0]["ffn_conv_w"], gl[1]["ffn_conv_w"]]))
    send_small = _pack([_to_shards(sg[n], _SHARDED[n]) for n in _SMALL_SHARDED], f32, lead=True)
    send_late = _pack([rg[n].reshape(w[n].shape) for n in _REPL_LATE], f32)
    w_in_rows = w["mix_w_in"].shape[2]
    last = _Exchange("reduce_last")
    last.add(g_even["mix_w_in_t"], sds((N_DEV, w_in_rows, d_model), f32), _rows_of(w_in_rows), _slot, "mix_w_in")
    last.add(send_small, sds(send_small.shape, f32), _slot, _slot, "small")
    last.add(send_late, sds((N_DEV,) + send_late.shape, f32), _whole, _slot, "repl_late")
    state_last = last.begin(False)
    slots = {}
    for ex, state in reduces:
        slots.update(ex.finish(state, [grad_x, last.token]))

    me_index = (4 * lax.axis_index("x") + 2 * lax.axis_index("y") + lax.axis_index("c")).astype(jnp.int32).reshape(1)

    def rows_block(r, c):
        return lambda tr: ((tr, c), lambda i, me: (me[0] * (r // tr) + i, 0))

    def own_block(n):
        r, c = part_shape[n]
        if n == "ffn_w_up":
            return lambda tr: ((tr, c), lambda i, me: (i, me[0]))
        return rows_block(r, c)

    out = [{}, {}, {}, {}]
    unpad = dict(ffn_w_up=2, ffn_conv_w=2, ffn_w_down=1)
    for n in matrices:
        layers = range(we_[n].shape[0])
        res = _sum_adam(me_index, [slots[(n, layer)] for layer in layers], [own_grad[(n, layer)] for layer in layers],
                        own_block(n), we_[n], me_[n], ve_[n], name=f"adam_{n}")
        for k in range(4):
            out[k][n] = _unpad_groups(res[k], unpad[n]) if n in unpad else res[k]
    pk = lambda d, order: _pack([d[n] for n in order], f32)[None]
    whole_rows = lambda tr: ((tr, _LANES), lambda i, me: (i, 0))
    res_early = _sum_adam(me_index, [slots["repl_early"]], [send_early], whole_rows, pk(w, _REPL_EARLY),
                          pk(m, _REPL_EARLY), pk(v, _REPL_EARLY), name="adam_repl_early")
    for k in range(4):
        out[k].update(zip(_REPL_EARLY, _unpack(res_early[k][0], [w[n].shape for n in _REPL_EARLY])))
    done = [out[k][n] for k in range(4) for n in matrices + _REPL_EARLY]
    slots = last.finish(state_last, done)
    transposed = lambda d: jnp.swapaxes(d["mix_w_in"], 1, 2)
    res_w_in = _sum_adam(me_index, [slots["mix_w_in"]], [g_even["mix_w_in_t"]], rows_block(w_in_rows, d_model),
                         transposed(w), transposed(m), transposed(v), name="adam_mix_w_in")
    res_small = _sum_adam(me_index, [slots["small"]], [send_small],
                          lambda tr: ((1, tr, _LANES), lambda i, me: (me[0], i, 0)),
                          pk(we_, _SMALL_SHARDED), pk(me_, _SMALL_SHARDED), pk(ve_, _SMALL_SHARDED), name="adam_small")
    res_late = _sum_adam(me_index, [slots["repl_late"]], [send_late], whole_rows, pk(w, _REPL_LATE), pk(m, _REPL_LATE),
                         pk(v, _REPL_LATE), name="adam_repl_late")
    for k in range(4):
        out[k]["mix_w_in"] = jnp.swapaxes(res_w_in[k], 1, 2)
        for n, a in zip(_SMALL_SHARDED, _unpack(res_small[k][0], [we_[n].shape for n in _SMALL_SHARDED])):
            out[k][n] = _unpad_groups(a, unpad[n]) if n in unpad else a
        out[k].update(zip(_REPL_LATE, _unpack(res_late[k][0], [w[n].shape for n in _REPL_LATE])))
    return loss, grad_x, out


_INPUTS = tuple("""x, mem, positions, norm_mix, norm_xa, norm_mem, norm_ffn, xa_wq, xa_wk, xa_wv, xa_wo, xa_q_norm, xa_k_norm, ffn_w_up, ffn_conv_w, ffn_conv_b, ffn_w_down, hg_lb_logits, mix_w_in, hg_out_norm, mla_q_a_norm, mla_w_uq, mla_kv_a_norm, mla_w_ukv, mla_qn_nope, mla_qn_rope, mla_kn_nope, mla_kn_rope, mix_w_out, s5_lam_re, s5_lam_im, s5_log_dt, s5_b_re, s5_b_im, s5_c_re, s5_c_im, s5_d, s5_w_glu_a, s5_w_glu_b, loss_target, m_norm_mix, m_norm_xa, m_norm_mem, m_norm_ffn, m_xa_wq, m_xa_wk, m_xa_wv, m_xa_wo, m_xa_q_norm, m_xa_k_norm, m_ffn_w_up, m_ffn_conv_w, m_ffn_conv_b, m_ffn_w_down, m_hg_lb_logits, m_mix_w_in, m_hg_out_norm, m_mla_q_a_norm, m_mla_w_uq, m_mla_kv_a_norm, m_mla_w_ukv, m_mla_qn_nope, m_mla_qn_rope, m_mla_kn_nope, m_mla_kn_rope, m_mix_w_out, m_s5_lam_re, m_s5_lam_im, m_s5_log_dt, m_s5_b_re, m_s5_b_im, m_s5_c_re, m_s5_c_im, m_s5_d, m_s5_w_glu_a, m_s5_w_glu_b, v_norm_mix, v_norm_xa, v_norm_mem, v_norm_ffn, v_xa_wq, v_xa_wk, v_xa_wv, v_xa_wo, v_xa_q_norm, v_xa_k_norm, v_ffn_w_up, v_ffn_conv_w, v_ffn_conv_b, v_ffn_w_down, v_hg_lb_logits, v_mix_w_in, v_hg_out_norm, v_mla_q_a_norm, v_mla_w_uq, v_mla_kv_a_norm, v_mla_w_ukv, v_mla_qn_nope, v_mla_qn_rope, v_mla_kn_nope, v_mla_kn_rope, v_mix_w_out, v_s5_lam_re, v_s5_lam_im, v_s5_log_dt, v_s5_b_re, v_s5_b_im, v_s5_c_re, v_s5_c_im, v_s5_d, v_s5_w_glu_a, v_s5_w_glu_b""".replace(" ", "").split(","))


def kernel(x, mem, positions, norm_mix, norm_xa, norm_mem, norm_ffn, xa_wq, xa_wk, xa_wv, xa_wo, xa_q_norm, xa_k_norm, ffn_w_up, ffn_conv_w, ffn_conv_b, ffn_w_down, hg_lb_logits, mix_w_in, hg_out_norm, mla_q_a_norm, mla_w_uq, mla_kv_a_norm, mla_w_ukv, mla_qn_nope, mla_qn_rope, mla_kn_nope, mla_kn_rope, mix_w_out, s5_lam_re, s5_lam_im, s5_log_dt, s5_b_re, s5_b_im, s5_c_re, s5_c_im, s5_d, s5_w_glu_a, s5_w_glu_b, loss_target, m_norm_mix, m_norm_xa, m_norm_mem, m_norm_ffn, m_xa_wq, m_xa_wk, m_xa_wv, m_xa_wo, m_xa_q_norm, m_xa_k_norm, m_ffn_w_up, m_ffn_conv_w, m_ffn_conv_b, m_ffn_w_down, m_hg_lb_logits, m_mix_w_in, m_hg_out_norm, m_mla_q_a_norm, m_mla_w_uq, m_mla_kv_a_norm, m_mla_w_ukv, m_mla_qn_nope, m_mla_qn_rope, m_mla_kn_nope, m_mla_kn_rope, m_mix_w_out, m_s5_lam_re, m_s5_lam_im, m_s5_log_dt, m_s5_b_re, m_s5_b_im, m_s5_c_re, m_s5_c_im, m_s5_d, m_s5_w_glu_a, m_s5_w_glu_b, v_norm_mix, v_norm_xa, v_norm_mem, v_norm_ffn, v_xa_wq, v_xa_wk, v_xa_wv, v_xa_wo, v_xa_q_norm, v_xa_k_norm, v_ffn_w_up, v_ffn_conv_w, v_ffn_conv_b, v_ffn_w_down, v_hg_lb_logits, v_mix_w_in, v_hg_out_norm, v_mla_q_a_norm, v_mla_w_uq, v_mla_kv_a_norm, v_mla_w_ukv, v_mla_qn_nope, v_mla_qn_rope, v_mla_kn_nope, v_mla_kn_rope, v_mix_w_out, v_s5_lam_re, v_s5_lam_im, v_s5_log_dt, v_s5_b_re, v_s5_b_im, v_s5_c_re, v_s5_c_im, v_s5_d, v_s5_w_glu_a, v_s5_w_glu_b):
    vals = dict(zip(_INPUTS, (x, mem, positions, norm_mix, norm_xa, norm_mem, norm_ffn, xa_wq, xa_wk, xa_wv, xa_wo, xa_q_norm, xa_k_norm, ffn_w_up, ffn_conv_w, ffn_conv_b, ffn_w_down, hg_lb_logits, mix_w_in, hg_out_norm, mla_q_a_norm, mla_w_uq, mla_kv_a_norm, mla_w_ukv, mla_qn_nope, mla_qn_rope, mla_kn_nope, mla_kn_rope, mix_w_out, s5_lam_re, s5_lam_im, s5_log_dt, s5_b_re, s5_b_im, s5_c_re, s5_c_im, s5_d, s5_w_glu_a, s5_w_glu_b, loss_target, m_norm_mix, m_norm_xa, m_norm_mem, m_norm_ffn, m_xa_wq, m_xa_wk, m_xa_wv, m_xa_wo, m_xa_q_norm, m_xa_k_norm, m_ffn_w_up, m_ffn_conv_w, m_ffn_conv_b, m_ffn_w_down, m_hg_lb_logits, m_mix_w_in, m_hg_out_norm, m_mla_q_a_norm, m_mla_w_uq, m_mla_kv_a_norm, m_mla_w_ukv, m_mla_qn_nope, m_mla_qn_rope, m_mla_kn_nope, m_mla_kn_rope, m_mix_w_out, m_s5_lam_re, m_s5_lam_im, m_s5_log_dt, m_s5_b_re, m_s5_b_im, m_s5_c_re, m_s5_c_im, m_s5_d, m_s5_w_glu_a, m_s5_w_glu_b, v_norm_mix, v_norm_xa, v_norm_mem, v_norm_ffn, v_xa_wq, v_xa_wk, v_xa_wv, v_xa_wo, v_xa_q_norm, v_xa_k_norm, v_ffn_w_up, v_ffn_conv_w, v_ffn_conv_b, v_ffn_w_down, v_hg_lb_logits, v_mix_w_in, v_hg_out_norm, v_mla_q_a_norm, v_mla_w_uq, v_mla_kv_a_norm, v_mla_w_ukv, v_mla_qn_nope, v_mla_qn_rope, v_mla_kn_nope, v_mla_kn_rope, v_mix_w_out, v_s5_lam_re, v_s5_lam_im, v_s5_log_dt, v_s5_b_re, v_s5_b_im, v_s5_c_re, v_s5_c_im, v_s5_d, v_s5_w_glu_a, v_s5_w_glu_b)))
    w = {n: vals[n] for n in _WEIGHTS}
    m = {n: vals["m_" + n] for n in _WEIGHTS}
    v = {n: vals["v_" + n] for n in _WEIGHTS}
    loss, grad_x, res = _train_step(vals["x"][0], vals["mem"][0], vals["positions"][0], vals["loss_target"][0],
                                    w, m, v)
    loss = lax.psum(loss[0, 0], ("x", "y", "c"))
    return (loss, grad_x[None], *[r[n] for r in res for n in _WEIGHTS])
```

```python
import functools

import jax
import jax.numpy as jnp
import numpy as np
from jax import lax
from jax.experimental import pallas as pl
from jax.experimental.pallas import tpu as pltpu

f32 = jnp.float32
bf16 = jnp.bfloat16

EPS = 1e-6
N_DEV = 8
VMEM_LIMIT = 52 * 1024 * 1024

HG_HEADS = 4
HG_DIM = 128
HG_WIDTH = HG_HEADS * HG_DIM
HG_CHUNK = 64
HG_SUB = 16
MLA_HEADS = 4
MLA_Q_RANK = 256
MLA_KV_RANK = 128
MLA_NOPE = 128
MLA_ROPE = 64
MLA_V = 128
MLA_QK = MLA_NOPE + MLA_ROPE
MLA_QK_PAD = 256
ROPE_BASE = 10000.0
IN_WIDTH = 4 * HG_WIDTH + MLA_Q_RANK + MLA_KV_RANK + MLA_ROPE
IN_PAD = 2560
XA_HEADS = 4
XA_DIM = 256
S5_GROUP = 16
S5_GROUPS = 64
S5_STATE = 64
CONV_W = 3

ADAM_LR = 0.001
ADAM_B1 = 0.9
ADAM_B2 = 0.999
ADAM_EPS = 1e-08
ADAM_WD = 0.01
ADAM_STEP = 10

_NT = (((1,), (1,)), ((), ()))
_TN = (((0,), (0,)), ((), ()))
_NN = (((1,), (0,)), ((), ()))


def _pick(n, cands):
    for c in cands:
        if n % c == 0:
            return c
    return n


def _cparams(sem):
    return pltpu.CompilerParams(dimension_semantics=sem, vmem_limit_bytes=VMEM_LIMIT)


_MM_BUDGET = 36 * 1024 * 1024
_MM_TILES = ((1024, 1024), (1024, 512), (512, 1024), (512, 512), (512, 256), (256, 512), (256, 256), (256, 128),
             (128, 256), (128, 128))


def _mm(a, b, *, name, ta=False, tb=False, out_dtype=f32, add=None, b2=None, kslab=None, norm=None, rms_bwd=None):
    m, k = (a.shape[1], a.shape[0]) if ta else a.shape
    nb = b.shape[0] if tb else b.shape[1]
    n = nb * (2 if b2 is not None else 1)
    slab, nslab = kslab if kslab is not None else (0, 1)
    assert (b.shape[1] // nslab if tb else b.shape[0]) == k, (a.shape, b.shape, ta, tb)
    assert b2 is None or (not tb and b2.shape == b.shape)
    full_rows = norm is not None or rms_bwd is not None
    assert not (full_rows and b2 is not None) and not (norm is not None and rms_bwd is not None)
    isz = lambda x: jnp.dtype(x.dtype).itemsize
    bm = bn = None
    for cm, cn in _MM_TILES:
        if m % cm or nb % cn or (full_rows and cn != n):
            continue
        need = 2 * (cm * k * isz(a) + cn * k * isz(b) * (2 if b2 is not None else 1)
                    + cm * cn * (jnp.dtype(out_dtype).itemsize + (4 if add is not None else 0)
                                 + (2 if norm is not None else 0) + (8 if rms_bwd is not None else 0)))
        if need <= _MM_BUDGET:
            bm, bn = cm, cn
            break
    assert bm is not None, (name, a.shape, b.shape)
    half = nb // bn
    dims = (((0 if ta else 1,), (1 if tb else 0,)), ((), ()))

    def body(*refs):
        refs = list(refs)
        a_ref, b_ref = refs[0], refs[1]
        b2_ref = refs.pop(2) if b2 is not None else None
        add_ref = refs.pop(2) if add is not None else None
        gain_ref = refs.pop(2) if norm is not None else None
        x_ref, xgain_ref, through_ref = (refs.pop(2), refs.pop(2), refs.pop(2)) if rms_bwd is not None else (None,) * 3
        o_ref = refs[2]
        extra_ref = refs[3] if (norm is not None or rms_bwd is not None) else None

        def run(rhs_ref):
            r = lax.dot_general(a_ref[...].astype(bf16), rhs_ref[...].astype(bf16), dims, preferred_element_type=f32)
            if add_ref is not None:
                r = r + add_ref[...].astype(f32)
            if rms_bwd is None:
                o_ref[...] = r.astype(o_ref.dtype)
            else:
                _, vjp = jax.vjp(_rms, x_ref[...], xgain_ref[...])
                dx, dgain = vjp(r)
                o_ref[...] = (through_ref[...] + dx).astype(o_ref.dtype)

                @pl.when(pl.program_id(0) == 0)
                def _():
                    extra_ref[...] = jnp.zeros_like(extra_ref)
                extra_ref[...] += dgain
            if norm is not None:
                extra_ref[...] = _rms(r, gain_ref[...]).astype(extra_ref.dtype)

        if b2_ref is None:
            run(b_ref)
        else:
            pl.when(pl.program_id(1) < half)(lambda: run(b_ref))
            pl.when(pl.program_id(1) >= half)(lambda: run(b2_ref))

    a_spec = pl.BlockSpec((k, bm), lambda i, j: (0, i)) if ta else pl.BlockSpec((bm, k), lambda i, j: (i, 0))
    if tb:
        b_spec = pl.BlockSpec((bn, k), lambda i, j: (j, slab))
    elif b2 is None:
        b_spec = pl.BlockSpec((k, bn), lambda i, j: (0, j))
    else:
        b_spec = pl.BlockSpec((k, bn), lambda i, j: (0, jnp.minimum(j, half - 1)))
    in_specs = [a_spec, b_spec]
    args = [a, b]
    if b2 is not None:
        in_specs.append(pl.BlockSpec((k, bn), lambda i, j: (0, jnp.maximum(j - half, 0))))
        args.append(b2)
    if add is not None:
        in_specs.append(pl.BlockSpec((bm, bn), lambda i, j: (i, j)))
        args.append(add)
    out_blk = pl.BlockSpec((bm, bn), lambda i, j: (i, j))
    out_specs, out_shape = out_blk, jax.ShapeDtypeStruct((m, n), out_dtype)
    row_blk = pl.BlockSpec((1, bn), lambda i, j: (0, j))
    if norm is not None:
        in_specs.append(row_blk)
        args.append(norm)
        out_specs, out_shape = [out_blk, out_blk], [out_shape, jax.ShapeDtypeStruct((m, n), bf16)]
    if rms_bwd is not None:
        in_specs += [out_blk, row_blk, out_blk]
        args += list(rms_bwd)
        out_specs, out_shape = [out_blk, row_blk], [out_shape, jax.ShapeDtypeStruct((1, n), f32)]
    return pl.pallas_call(
        body, grid=(m // bm, n // bn), in_specs=in_specs, out_specs=out_specs, out_shape=out_shape,
        compiler_params=_cparams(("arbitrary" if rms_bwd is not None else "parallel", "parallel")),
        name=name)(*args)


def _as_tuple(x):
    return tuple(x) if isinstance(x, (tuple, list)) else (x,)


def _full_spec(p):
    nd = p.ndim
    return pl.BlockSpec(p.shape, lambda i, _nd=nd: (0,) * _nd)


def _window(a, start, width):
    assert start % width == 0 and width % 128 == 0
    return (a, start // width, width)


def _row_array(x):
    return x[0] if isinstance(x, tuple) else x


def _row_shape(x):
    return (x[0].shape[0], x[2]) if isinstance(x, tuple) else x.shape


def _row_spec(x, tile):
    if isinstance(x, tuple):
        return pl.BlockSpec((tile, x[2]), lambda i, _b=x[1]: (i, _b))
    return pl.BlockSpec((tile, x.shape[1]), lambda i: (i, 0))


def _rows(fn, rows, params, outs, *, name, tile=256, accs=()):
    length = _row_shape(rows[0])[0]
    tile = min(tile, length)
    nr, npar, no = len(rows), len(params), len(outs)

    def body(*refs):
        r, p, o = refs[:nr], refs[nr:nr + npar], refs[nr + npar:]
        res = _as_tuple(fn(*[x[...].astype(f32) for x in r], *[x[...] for x in p]))
        for kk in range(no):
            o[kk][...] = res[kk].astype(o[kk].dtype)
        if accs:
            @pl.when(pl.program_id(0) == 0)
            def _():
                for kk in range(no, no + len(accs)):
                    o[kk][...] = jnp.zeros_like(o[kk])
            for kk in range(no, no + len(accs)):
                o[kk][...] += res[kk]

    in_specs = [_row_spec(x, tile) for x in rows] + [_full_spec(p) for p in params]
    out_specs = [pl.BlockSpec((tile, w), lambda i: (i, 0)) for w, _ in outs]
    out_shape = [jax.ShapeDtypeStruct((length, w), d) for w, d in outs]
    for s in accs:
        out_specs.append(pl.BlockSpec(s, lambda i, _nd=len(s): (0,) * _nd))
        out_shape.append(jax.ShapeDtypeStruct(s, f32))
    res = pl.pallas_call(body, grid=(length // tile,), in_specs=in_specs, out_specs=out_specs, out_shape=out_shape,
                         compiler_params=_cparams(("arbitrary",)), name=name)(*[_row_array(x) for x in rows], *params)
    return res


def _rows_bwd(fn, rows, params, cts, *, name, rgrad, pgrad, tile=256, addends=None):
    addends = {i: (a if isinstance(a, list) else [(a, 0)]) for i, a in (addends or {}).items()}
    length = _row_shape(rows[0])[0]
    tile = min(tile, length)
    nr, npar, nc = len(rows), len(params), len(cts)
    ridx = [i for i in range(nr) if rgrad[i] is not None]
    pidx = [i for i in range(npar) if pgrad[i]]
    flat_addends = [(i, a, off) for i in sorted(addends) for a, off in addends[i]]
    na = len(flat_addends)

    def body(*refs):
        r, p, c = refs[:nr], refs[nr:nr + npar], refs[nr + npar:nr + npar + nc]
        ad = refs[nr + npar + nc:nr + npar + nc + na]
        o = refs[nr + npar + nc + na:]
        rv = [x[...].astype(f32) for x in r]
        pv = [x[...] for x in p]
        cv = tuple(x[...].astype(f32) for x in c)

        def g(*d):
            rr, pp = list(rv), list(pv)
            for n_, i_ in enumerate(ridx):
                rr[i_] = d[n_]
            for n_, i_ in enumerate(pidx):
                pp[i_] = d[len(ridx) + n_]
            return _as_tuple(fn(*rr, *pp))

        _, vjp = jax.vjp(g, *[rv[i] for i in ridx], *[pv[i] for i in pidx])
        grads = vjp(cv)
        for n_, i_ in enumerate(ridx):
            val = grads[n_]
            for k_, (j_, a_, off) in enumerate(flat_addends):
                if j_ == i_:
                    extra = ad[k_][...].astype(f32)
                    if extra.shape[1] != val.shape[1]:
                        extra = jnp.pad(extra, ((0, 0), (off, val.shape[1] - off - extra.shape[1])))
                    val = val + extra
            o[n_][...] = val.astype(o[n_].dtype)
        if pidx:
            @pl.when(pl.program_id(0) == 0)
            def _():
                for n_ in range(len(pidx)):
                    o[len(ridx) + n_][...] = jnp.zeros_like(o[len(ridx) + n_])
            for n_ in range(len(pidx)):
                o[len(ridx) + n_][...] += grads[len(ridx) + n_]

    plain = lambda shape: pl.BlockSpec((tile, shape[1]), lambda i: (i, 0))
    in_specs = ([_row_spec(x, tile) for x in rows] + [_full_spec(p) for p in params] + [plain(x.shape) for x in cts]
                + [plain(a.shape) for _, a, _ in flat_addends])
    out_specs = [plain(_row_shape(rows[i])) for i in ridx] + [_full_spec(params[i]) for i in pidx]
    out_shape = ([jax.ShapeDtypeStruct(_row_shape(rows[i]), rgrad[i]) for i in ridx]
                 + [jax.ShapeDtypeStruct(params[i].shape, f32) for i in pidx])
    res = pl.pallas_call(body, grid=(length // tile,), in_specs=in_specs, out_specs=out_specs, out_shape=out_shape,
                         compiler_params=_cparams(("arbitrary",)), name=name)(
        *[_row_array(x) for x in rows], *params, *cts, *[a for _, a, _ in flat_addends])
    return list(res[:len(ridx)]), list(res[len(ridx):])


def _rms(x, g):
    return x * lax.rsqrt(jnp.mean(x * x, axis=-1, keepdims=True) + EPS) * g


def _rms_twice(x, g):
    y = _rms(x, g)
    return y, y


def _silu(x):
    return x * jax.nn.sigmoid(x)


_UP_COLS = 512


def _ffn_up_conv(hf, w_up, cw, cb, *, name):
    length, d = hf.shape
    ff = w_up.shape[1] // 2
    bm = _pick(length, (1024, 512, 256))
    bn = _UP_COLS
    nj = ff // bn

    def body(hf_ref, wg_ref, wv_ref, cwg, cwv, cbg, cbv, ug_ref, uv_ref, a_ref, halo_g, halo_v):
        i, j = pl.program_id(0), pl.program_id(1)
        x = hf_ref[...]
        rows = lax.broadcasted_iota(jnp.int32, (bm, bn), 0)
        pad = jnp.zeros((bm - 8, bn), f32)

        def half(w_ref, cw_ref, cb_ref, u_ref, halo):
            u = jnp.dot(x, w_ref[...], preferred_element_type=f32).astype(bf16)
            u_ref[...] = u
            u = u.astype(f32)
            prev = jnp.where(i == 0, 0.0, halo[j])
            x1 = jnp.where(rows >= 1, pltpu.roll(u, 1, axis=0), jnp.concatenate([pltpu.roll(prev, 1, axis=0), pad]))
            x2 = jnp.where(rows >= 2, pltpu.roll(u, 2, axis=0), jnp.concatenate([pltpu.roll(prev, 2, axis=0), pad]))
            halo[j] = u[bm - 8:, :]
            return cw_ref[2:3, :] * u + cw_ref[1:2, :] * x1 + cw_ref[0:1, :] * x2 + cb_ref[...]

        g = half(wg_ref, cwg, cbg, ug_ref, halo_g)
        v = half(wv_ref, cwv, cbv, uv_ref, halo_v)
        a_ref[...] = (_silu(g) * v).astype(a_ref.dtype)

    col = lambda r, off: pl.BlockSpec((r, bn), lambda i, j, _o=off: (0, j + _o))
    out_blk = pl.BlockSpec((bm, bn), lambda i, j: (i, j))
    sds = jax.ShapeDtypeStruct((length, ff), bf16)
    return pl.pallas_call(
        body, grid=(length // bm, nj),
        in_specs=[pl.BlockSpec((bm, d), lambda i, j: (i, 0)), col(d, 0), col(d, nj), col(CONV_W, 0), col(CONV_W, nj),
                  col(1, 0), col(1, nj)],
        out_specs=[out_blk] * 3, out_shape=[sds] * 3,
        scratch_shapes=[pltpu.VMEM((nj, 8, bn), f32), pltpu.VMEM((nj, 8, bn), f32)],
        compiler_params=_cparams(("arbitrary", "arbitrary")), name=name)(hf, w_up, w_up, cw, cw, cb, cb)


def _ffn_da_dconv(dout, w_down, u_g, u_v, cw, cb, *, name):
    length, d = dout.shape
    ff = u_g.shape[1]
    bm = _pick(length, (1024, 512, 256))
    bn = _UP_COLS
    nj, ni = ff // bn, length // bm

    def body(dout_ref, wd_ref, ug_ref, uv_ref, pg_ref, pv_ref, cwg, cwv, cbg, cbv,
             dug_ref, duv_ref, sums_g_ref, sums_v_ref, halo_g, halo_v, acc_g, acc_v):
        s, j = pl.program_id(0), pl.program_id(1)
        rows = lax.broadcasted_iota(jnp.int32, (bm, bn), 0)
        row8 = lax.broadcasted_iota(jnp.int32, (8, bn), 0)
        pad = jnp.zeros((bm - 8, bn), f32)
        da = lax.dot_general(dout_ref[...].astype(bf16), wd_ref[...], _NT, preferred_element_type=f32)

        def conv(u_ref, p_ref, cw_ref, cb_ref):
            x = u_ref[...].astype(f32)
            prev = jnp.where(s == ni - 1, 0.0, p_ref[...].astype(f32))
            x1 = jnp.where(rows >= 1, pltpu.roll(x, 1, axis=0), jnp.concatenate([pltpu.roll(prev, 1, axis=0), pad]))
            x2 = jnp.where(rows >= 2, pltpu.roll(x, 2, axis=0), jnp.concatenate([pltpu.roll(prev, 2, axis=0), pad]))
            return cw_ref[2:3, :] * x + cw_ref[1:2, :] * x1 + cw_ref[0:1, :] * x2 + cb_ref[...], x, x1, x2

        g, xg, xg1, xg2 = conv(ug_ref, pg_ref, cwg, cbg)
        v, xv, xv1, xv2 = conv(uv_ref, pv_ref, cwv, cbv)
        sg = jax.nn.sigmoid(g)
        dg = da * v * (sg * (1.0 + g * (1.0 - sg)))
        dv = da * (g * sg)

        def back(dy, x, x1, x2, cw_ref, du_ref, sums_ref, halo, acc):
            nxt = jnp.where(s == 0, 0.0, halo[j])
            up1 = jnp.where(rows < bm - 1, pltpu.roll(dy, bm - 1, axis=0),
                            jnp.concatenate([pad, pltpu.roll(nxt, 7, axis=0)]))
            up2 = jnp.where(rows < bm - 2, pltpu.roll(dy, bm - 2, axis=0),
                            jnp.concatenate([pad, pltpu.roll(nxt, 6, axis=0)]))
            halo[j] = dy[:8, :]
            du_ref[...] = (cw_ref[2:3, :] * dy + cw_ref[1:2, :] * up1 + cw_ref[0:1, :] * up2).astype(du_ref.dtype)
            col = lambda t: jnp.sum(t, axis=0, keepdims=True)
            part = jnp.where(row8 == 0, col(dy * x2), jnp.where(row8 == 1, col(dy * x1), jnp.where(
                row8 == 2, col(dy * x), jnp.where(row8 == 3, col(dy), 0.0))))
            total = jnp.where(s == 0, part, acc[j] + part)
            acc[j] = total
            sums_ref[...] = total

        back(dg, xg, xg1, xg2, cwg, dug_ref, sums_g_ref, halo_g, acc_g)
        back(dv, xv, xv1, xv2, cwv, duv_ref, sums_v_ref, halo_v, acc_v)

    rb = lambda s: ni - 1 - s
    tile = pl.BlockSpec((bm, bn), lambda s, j: (rb(s), j))
    before = pl.BlockSpec((8, bn), lambda s, j: (jnp.maximum(rb(s) * (bm // 8) - 1, 0), j))
    col = lambda r, off: pl.BlockSpec((r, bn), lambda s, j, _o=off: (0, j + _o))
    sds = jax.ShapeDtypeStruct
    dug, duv, sums_g, sums_v = pl.pallas_call(
        body, grid=(ni, nj),
        in_specs=[pl.BlockSpec((bm, d), lambda s, j: (rb(s), 0)), pl.BlockSpec((bn, d), lambda s, j: (j, 0)),
                  tile, tile, before, before, col(CONV_W, 0), col(CONV_W, nj), col(1, 0), col(1, nj)],
        out_specs=[tile, tile] + [pl.BlockSpec((8, bn), lambda s, j: (s, j))] * 2,
        out_shape=[sds((length, ff), bf16), sds((length, ff), bf16), sds((ni * 8, ff), f32), sds((ni * 8, ff), f32)],
        scratch_shapes=[pltpu.VMEM((nj, 8, bn), f32)] * 4,
        compiler_params=_cparams(("arbitrary", "arbitrary")), name=name)(
        dout, w_down, u_g, u_v, u_g, u_v, cw, cw, cb, cb)
    last = (ni - 1) * 8
    both = lambda lo, hi: jnp.concatenate([sums_g[last + lo:last + hi], sums_v[last + lo:last + hi]], axis=1)
    return dug, duv, both(0, CONV_W), both(CONV_W, CONV_W + 1)


def _ffn_fwd(h, w, tag, hf=None):
    if hf is None:
        hf, = _rows(_rms, [h], [w["norm_ffn"]], [(h.shape[1], bf16)], name=f"ffn_norm_{tag}")
    u_g, u_v, a = _ffn_up_conv(hf, w["ffn_w_up"], w["ffn_conv_w"], w["ffn_conv_b"], name=f"ffn_up_{tag}")
    out = _mm(a, w["ffn_w_down"], add=h, name=f"ffn_down_{tag}")
    return out, (h, hf, u_g, u_v, a)


def _ffn_bwd(dout, w, saved, tag):
    h, hf, u_g, u_v, a = saved
    g = {"ffn_w_down": _mm(a, dout, ta=True, name=f"ffn_dwdown_{tag}")}
    dug, duv, g["ffn_conv_w"], g["ffn_conv_b"] = _ffn_da_dconv(
        dout, w["ffn_w_down"], u_g, u_v, w["ffn_conv_w"], w["ffn_conv_b"], name=f"ffn_dconv_{tag}")
    dhf = _mm(dug, w["ffn_w_up"], tb=True, kslab=(0, 2), name=f"ffn_dhf_g_{tag}")
    dh, g["norm_ffn"] = _mm(duv, w["ffn_w_up"], tb=True, kslab=(1, 2), add=dhf, rms_bwd=(h, w["norm_ffn"], dout),
                            name=f"ffn_dhf_v_{tag}")
    g["ffn_w_up"] = _mm(hf, dug, ta=True, b2=duv, name=f"ffn_dwup_{tag}")
    return dh, g


def _xattn_fn(qx, kx, vx, qg, kg):
    outs = []
    for hh in range(XA_HEADS):
        sl = slice(hh * XA_DIM, (hh + 1) * XA_DIM)
        q = _rms(qx[:, sl], qg).astype(bf16)
        k = _rms(kx[:, sl], kg).astype(bf16)
        s = lax.dot_general(q, k, _NT, preferred_element_type=f32) * (XA_DIM ** -0.5)
        s = s - jnp.max(s, axis=-1, keepdims=True)
        p = jnp.exp(s)
        p = p / jnp.sum(p, axis=-1, keepdims=True)
        outs.append(jnp.dot(p.astype(bf16), vx[:, sl].astype(bf16), preferred_element_type=f32))
    return jnp.concatenate(outs, axis=-1)


def _xattn_fwd(h, mem, w, tag, hx=None):
    d = h.shape[1]
    if hx is None:
        hx, = _rows(_rms, [h], [w["norm_xa"]], [(d, bf16)], name=f"xa_norm_{tag}")
    qx = _mm(hx, w["xa_wq"], name=f"xa_q_{tag}")
    m, = _rows(_rms, [mem], [w["norm_mem"]], [(d, bf16)], name=f"xa_mnorm_{tag}")
    kx = _mm(m, w["xa_wk"], name=f"xa_k_{tag}")
    vx = _mm(m, w["xa_wv"], name=f"xa_v_{tag}")
    o, = _rows(_xattn_fn, [qx], [kx, vx, w["xa_q_norm"], w["xa_k_norm"]], [(d, bf16)], tile=1024,
               name=f"xa_attn_{tag}")
    out, hf = _mm(o, w["xa_wo"], add=h, norm=w["norm_ffn"], name=f"xa_o_{tag}")
    return out, (h, hx, qx, m, kx, vx, o), hf


def _xattn_bwd(dout, mem, w, saved, tag):
    h, hx, qx, m, kx, vx, o = saved
    g = {}
    do = _mm(dout, w["xa_wo"], tb=True, out_dtype=bf16, name=f"xa_do_{tag}")
    g["xa_wo"] = _mm(o, dout, ta=True, name=f"xa_dwo_{tag}")
    (dqx,), (dkx, dvx, g["xa_q_norm"], g["xa_k_norm"]) = _rows_bwd(
        _xattn_fn, [qx], [kx, vx, w["xa_q_norm"], w["xa_k_norm"]], [do], rgrad=[bf16], pgrad=[True] * 4,
        tile=1024, name=f"xa_dattn_{tag}")
    dh, g["norm_xa"] = _mm(dqx, w["xa_wq"], tb=True, rms_bwd=(h, w["norm_xa"], dout), name=f"xa_dhx_{tag}")
    g["xa_wq"] = _mm(hx, dqx, ta=True, name=f"xa_dwq_{tag}")
    dm = _mm(dkx, w["xa_wk"], tb=True, name=f"xa_dm_k_{tag}")
    dm = _mm(dvx, w["xa_wv"], tb=True, add=dm, name=f"xa_dm_v_{tag}")
    g["xa_wk"] = _mm(m, dkx, ta=True, name=f"xa_dwk_{tag}")
    g["xa_wv"] = _mm(m, dvx, ta=True, name=f"xa_dwv_{tag}")
    _, (g["norm_mem"],) = _rows_bwd(_rms, [mem], [w["norm_mem"]], [dm], rgrad=[None], pgrad=[True],
                                    name=f"xa_dmnorm_{tag}")
    return dh, g


_HG_GROUP = 4


def _hg_chunk(q, k, v, g, *sts):
    c = q.shape[0]
    heads = [slice(h * HG_DIM, (h + 1) * HG_DIM) for h in range(len(sts))]
    tri = (lax.broadcasted_iota(jnp.int32, (c, c), 0) >= lax.broadcasted_iota(jnp.int32, (c, c), 1)).astype(f32)
    b = jnp.dot(tri, g, precision=lax.Precision.HIGHEST, preferred_element_type=f32)
    bend = jnp.sum(g, axis=0, keepdims=True)
    qe = (q * jnp.exp(b)).astype(bf16)
    kd = (k * jnp.exp(bend - b)).astype(bf16)
    vb = v.astype(bf16)
    decay = jnp.exp(bend)
    o_inter = [lax.dot_general(qe[:, hs], st.astype(bf16), _NT, preferred_element_type=f32) for hs, st in zip(heads, sts)]
    new = [st * decay[:, hs] + lax.dot_general(vb[:, hs], kd[:, hs], _TN, preferred_element_type=f32)
           for hs, st in zip(heads, sts)]
    outs = []
    for i in range(c // HG_SUB):
        lo, n = HG_SUB * i, HG_SUB * (i + 1)
        ref = jnp.sum(g[:lo], axis=0, keepdims=True) if i else jnp.zeros((1, g.shape[1]), f32)
        qh = (q[lo:n] * jnp.exp(b[lo:n] - ref)).astype(bf16)
        kh = (k[:n] * jnp.exp(ref - b[:n])).astype(bf16)
        keep = (lax.broadcasted_iota(jnp.int32, (HG_SUB, n), 1)
                <= lo + lax.broadcasted_iota(jnp.int32, (HG_SUB, n), 0))
        scores = [lax.dot_general(qh[:, hs], kh[:, hs], _NT, preferred_element_type=f32) for hs in heads]
        scores = [jnp.where(keep, a, 0.0).astype(bf16) for a in scores]
        outs.append(jnp.concatenate([jnp.dot(a, vb[:n, hs], preferred_element_type=f32)
                                     for a, hs in zip(scores, heads)], axis=1))
    return (jnp.concatenate(outs, axis=0) + jnp.concatenate(o_inter, axis=1), *new)


def _hg_fwd(q, k, v, g, *, name):
    length = q.shape[0]
    rows = _HG_GROUP * HG_CHUNK
    ng = length // rows
    nc = length // HG_CHUNK

    def body(q_ref, k_ref, v_ref, g_ref, o_ref, st_ref, state):
        @pl.when(pl.program_id(0) == 0)
        def _():
            state[...] = jnp.zeros_like(state)

        states = [state[h] for h in range(HG_HEADS)]
        for ci in range(_HG_GROUP):
            sl = slice(ci * HG_CHUNK, (ci + 1) * HG_CHUNK)
            for h in range(HG_HEADS):
                st_ref[h, ci] = states[h]
            o, *states = _hg_chunk(q_ref[sl, :], k_ref[sl, :], v_ref[sl, :], g_ref[sl, :], *states)
            o_ref[sl, :] = o
        for h in range(HG_HEADS):
            state[h] = states[h]

    blk = pl.BlockSpec((rows, HG_WIDTH), lambda c: (c, 0))
    return pl.pallas_call(
        body, grid=(ng,), in_specs=[blk] * 4,
        out_specs=[blk, pl.BlockSpec((HG_HEADS, _HG_GROUP, HG_DIM, HG_DIM), lambda c: (0, c, 0, 0))],
        out_shape=[jax.ShapeDtypeStruct((length, HG_WIDTH), f32),
                   jax.ShapeDtypeStruct((HG_HEADS, nc, HG_DIM, HG_DIM), f32)],
        scratch_shapes=[pltpu.VMEM((HG_HEADS, HG_DIM, HG_DIM), f32)],
        compiler_params=_cparams(("arbitrary",)), name=name)(q, k, v, g)


def _hg_bwd(q, k, v, g, states, do, *, name):
    length = q.shape[0]
    rows = _HG_GROUP * HG_CHUNK
    ng = length // rows

    def body(q_ref, k_ref, v_ref, g_ref, st_ref, do_ref, dq_ref, dk_ref, dv_ref, dg_ref, dstate):
        @pl.when(pl.program_id(0) == 0)
        def _():
            dstate[...] = jnp.zeros_like(dstate)

        dstates = [dstate[h] for h in range(HG_HEADS)]
        for ci in reversed(range(_HG_GROUP)):
            sl = slice(ci * HG_CHUNK, (ci + 1) * HG_CHUNK)
            _, vjp = jax.vjp(_hg_chunk, q_ref[sl, :], k_ref[sl, :], v_ref[sl, :], g_ref[sl, :],
                             *[st_ref[h, ci] for h in range(HG_HEADS)])
            dq, dk, dv, dg, *dstates = vjp((do_ref[sl, :], *dstates))
            dq_ref[sl, :] = dq
            dk_ref[sl, :] = dk
            dv_ref[sl, :] = dv
            dg_ref[sl, :] = dg
        for h in range(HG_HEADS):
            dstate[h] = dstates[h]

    blk = pl.BlockSpec((rows, HG_WIDTH), lambda c: (ng - 1 - c, 0))
    sds = jax.ShapeDtypeStruct((length, HG_WIDTH), f32)
    return pl.pallas_call(
        body, grid=(ng,),
        in_specs=[blk] * 4 + [pl.BlockSpec((HG_HEADS, _HG_GROUP, HG_DIM, HG_DIM), lambda c: (0, ng - 1 - c, 0, 0)), blk],
        out_specs=[blk] * 4, out_shape=[sds] * 4,
        scratch_shapes=[pltpu.VMEM((HG_HEADS, HG_DIM, HG_DIM), f32)],
        compiler_params=_cparams(("arbitrary",)), name=name)(q, k, v, g, states, do)


_ATT_BLK = 512
_ATT_SCALE = MLA_QK ** -0.5
_NEG = -1e30


def _att_mask(i, j, t):
    rows = i * t + lax.broadcasted_iota(jnp.int32, (t, t), 0)
    cols = j * t + lax.broadcasted_iota(jnp.int32, (t, t), 1)
    return cols <= rows


def _att_fwd(q, k, v, *, name):
    length = q.shape[0]
    t = min(_ATT_BLK, length)
    nq = length // t
    qw, vw = MLA_QK_PAD, MLA_V
    heads = range(MLA_HEADS)

    def body(q_ref, k_ref, v_ref, o_ref, lse_ref):
        i = pl.program_id(0)
        qbs = [q_ref[:, h * qw:(h + 1) * qw] for h in heads]

        def step(j, carry, diagonal=False):
            off = pl.multiple_of(j * t, t)
            out = []
            for h in heads:
                m, l, acc = carry[h]
                ks = k_ref[pl.ds(off, t), h * qw:(h + 1) * qw]
                vs = v_ref[pl.ds(off, t), h * vw:(h + 1) * vw]
                s = lax.dot_general(qbs[h], ks, _NT, preferred_element_type=f32) * _ATT_SCALE
                if diagonal:
                    s = jnp.where(_att_mask(i, j, t), s, _NEG)
                m_new = jnp.maximum(m, jnp.max(s, axis=-1, keepdims=True))
                alpha = jnp.exp(m - m_new)
                p = jnp.exp(s - m_new)
                l = alpha * l + jnp.sum(p, axis=-1, keepdims=True)
                acc = alpha * acc + jnp.dot(p.astype(bf16), vs, preferred_element_type=f32)
                out.append((m_new, l, acc))
            return tuple(out)

        init = tuple((jnp.full((t, 1), _NEG, f32), jnp.zeros((t, 1), f32), jnp.zeros((t, vw), f32)) for _ in heads)
        res = step(i, lax.fori_loop(0, i, step, init), diagonal=True)
        for h in heads:
            m, l, acc = res[h]
            o_ref[:, h * vw:(h + 1) * vw] = (acc / l).astype(o_ref.dtype)
            lse_ref[:, h * vw:(h + 1) * vw] = jnp.broadcast_to(m + jnp.log(l), (t, vw))

    return pl.pallas_call(
        body, grid=(nq,),
        in_specs=[pl.BlockSpec((t, q.shape[1]), lambda i: (i, 0)), pl.BlockSpec(k.shape, lambda i: (0, 0)),
                  pl.BlockSpec(v.shape, lambda i: (0, 0))],
        out_specs=[pl.BlockSpec((t, v.shape[1]), lambda i: (i, 0))] * 2,
        out_shape=[jax.ShapeDtypeStruct(v.shape, bf16), jax.ShapeDtypeStruct(v.shape, f32)],
        compiler_params=_cparams(("arbitrary",)), name=name)(q, k, v)


def _att_bwd(q, k, v, o, lse, do, *, name):
    length = q.shape[0]
    t = min(_ATT_BLK, length)
    nq = length // t
    qw, vw = MLA_QK_PAD, MLA_V
    heads = range(MLA_HEADS)

    def dq_body(q_ref, k_ref, v_ref, o_ref, lse_ref, do_ref, dq_ref, delta_ref):
        i = pl.program_id(0)
        qbs = [q_ref[:, h * qw:(h + 1) * qw] for h in heads]
        dobs = [do_ref[:, h * vw:(h + 1) * vw] for h in heads]
        lses = [lse_ref[:, h * vw:h * vw + 1] for h in heads]
        deltas = [jnp.sum(dobs[h].astype(f32) * o_ref[:, h * vw:(h + 1) * vw].astype(f32), axis=-1, keepdims=True)
                  for h in heads]

        def step(j, dqs, diagonal=False):
            off = pl.multiple_of(j * t, t)
            out = []
            for h in heads:
                ks = k_ref[pl.ds(off, t), h * qw:(h + 1) * qw]
                vs = v_ref[pl.ds(off, t), h * vw:(h + 1) * vw]
                s = lax.dot_general(qbs[h], ks, _NT, preferred_element_type=f32) * _ATT_SCALE
                p = jnp.exp(s - lses[h])
                if diagonal:
                    p = jnp.where(_att_mask(i, j, t), p, 0.0)
                dp = lax.dot_general(dobs[h], vs, _NT, preferred_element_type=f32)
                ds = p * (dp - deltas[h]) * _ATT_SCALE
                out.append(dqs[h] + jnp.dot(ds.astype(bf16), ks, preferred_element_type=f32))
            return tuple(out)

        dqs = step(i, lax.fori_loop(0, i, step, tuple(jnp.zeros((t, qw), f32) for _ in heads)), diagonal=True)
        for h in heads:
            dq_ref[:, h * qw:(h + 1) * qw] = dqs[h].astype(dq_ref.dtype)
            delta_ref[:, h * vw:(h + 1) * vw] = jnp.broadcast_to(deltas[h], (t, vw))

    qblk = pl.BlockSpec((t, q.shape[1]), lambda i: (i, 0))
    vblk = pl.BlockSpec((t, v.shape[1]), lambda i: (i, 0))
    qfull = pl.BlockSpec(q.shape, lambda i: (0, 0))
    vfull = pl.BlockSpec(v.shape, lambda i: (0, 0))
    dq, delta = pl.pallas_call(
        dq_body, grid=(nq,), in_specs=[qblk, qfull, vfull, vblk, vblk, vblk], out_specs=[qblk, vblk],
        out_shape=[jax.ShapeDtypeStruct(q.shape, bf16), jax.ShapeDtypeStruct(lse.shape, f32)],
        compiler_params=_cparams(("arbitrary",)), name=name + "_dq")(q, k, v, o, lse, do)

    def dkv_body(k_ref, v_ref, q_ref, do_ref, lse_ref, delta_ref, dk_ref, dv_ref):
        j = pl.program_id(0)
        kbs = [k_ref[:, h * qw:(h + 1) * qw] for h in heads]
        vbs = [v_ref[:, h * vw:(h + 1) * vw] for h in heads]

        def step(i, carry, diagonal=False):
            off = pl.multiple_of(i * t, t)
            out = []
            for h in heads:
                dk, dv = carry[h]
                qs = q_ref[pl.ds(off, t), h * qw:(h + 1) * qw]
                dos = do_ref[pl.ds(off, t), h * vw:(h + 1) * vw]
                lse_i = lse_ref[pl.ds(off, t), h * vw:h * vw + 1]
                delta_i = delta_ref[pl.ds(off, t), h * vw:h * vw + 1]
                s = lax.dot_general(qs, kbs[h], _NT, preferred_element_type=f32) * _ATT_SCALE
                p = jnp.exp(s - lse_i)
                if diagonal:
                    p = jnp.where(_att_mask(i, j, t), p, 0.0)
                dv = dv + lax.dot_general(p.astype(bf16), dos, _TN, preferred_element_type=f32)
                dp = lax.dot_general(dos, vbs[h], _NT, preferred_element_type=f32)
                ds = p * (dp - delta_i) * _ATT_SCALE
                dk = dk + lax.dot_general(ds.astype(bf16), qs, _TN, preferred_element_type=f32)
                out.append((dk, dv))
            return tuple(out)

        first = step(j, tuple((jnp.zeros((t, qw), f32), jnp.zeros((t, vw), f32)) for _ in heads), diagonal=True)
        res = lax.fori_loop(j + 1, nq, step, first)
        for h in heads:
            dk_ref[:, h * qw:(h + 1) * qw] = res[h][0].astype(dk_ref.dtype)
            dv_ref[:, h * vw:(h + 1) * vw] = res[h][1].astype(dv_ref.dtype)

    dk, dv = pl.pallas_call(
        dkv_body, grid=(nq,), in_specs=[qblk, vblk, qfull, vfull, vfull, vfull], out_specs=[qblk, vblk],
        out_shape=[jax.ShapeDtypeStruct(k.shape, bf16), jax.ShapeDtypeStruct(v.shape, bf16)],
        compiler_params=_cparams(("arbitrary",)), name=name + "_dkv")(k, v, q, do, lse, delta)
    return dq, dk, dv


_C_Q = 4 * HG_WIDTH
_C_KV = _C_Q + MLA_Q_RANK
_C_KPE = _C_KV + MLA_KV_RANK


def _rms_n(x, g, n):
    return x * lax.rsqrt(jnp.sum(x * x, axis=-1, keepdims=True) * (1.0 / n) + EPS) * g


def _mix_a(proj, l0, l1, q_a_norm, kv_a_norm):
    lb = jax.nn.sigmoid(l0 - l1)
    f = lb + (1.0 - lb) * jax.nn.sigmoid(proj[:, HG_WIDTH:2 * HG_WIDTH])
    qf = _silu(proj[:, :HG_WIDTH])
    v = proj[:, 2 * HG_WIDTH:3 * HG_WIDTH]
    cqn = _rms(proj[:, _C_Q:_C_KV], q_a_norm)
    ckvn = _rms(proj[:, _C_KV:_C_KPE], kv_a_norm)
    return qf, 1.0 - f, v, jnp.log(f), cqn, ckvn


def _mix_b(qraw, kvraw, kpe_raw, cos, sin, qn_nope, qn_rope, kn_nope, kn_rope, perm):
    def rope(x):
        return x * cos + jnp.dot(x, perm, precision=lax.Precision.HIGHEST, preferred_element_type=f32) * sin

    kpe = rope(_rms_n(kpe_raw, kn_rope, MLA_ROPE))
    qs, ks, vs = [], [], []
    for hh in range(MLA_HEADS):
        base = hh * MLA_QK_PAD
        qs.append(_rms(qraw[:, base:base + MLA_NOPE], qn_nope))
        qs.append(rope(_rms_n(qraw[:, base + MLA_NOPE:base + MLA_QK_PAD], qn_rope, MLA_ROPE)))
        ks.append(_rms(kvraw[:, base:base + MLA_NOPE], kn_nope))
        ks.append(kpe)
        vs.append(kvraw[:, base + MLA_NOPE:base + MLA_QK_PAD])
    return jnp.concatenate(qs, axis=-1), jnp.concatenate(ks, axis=-1), jnp.concatenate(vs, axis=-1)


def _mix_c(o_hg, gate, o_mla, hg_out_norm):
    parts = []
    for hh in range(HG_HEADS):
        sl = slice(hh * HG_DIM, (hh + 1) * HG_DIM)
        parts.append(_rms(o_hg[:, sl], hg_out_norm[:, sl]))
    o = jnp.concatenate(parts, axis=-1) * _silu(gate)
    return jnp.concatenate([o, o_mla], axis=-1)


def _rope_perm():
    p = np.zeros((128, 128), np.float32)
    half = MLA_ROPE // 2
    for i in range(half):
        p[i + half, i] = -1.0
        p[i, i + half] = 1.0
    return jnp.asarray(p)


def _mixer_fwd(h, cos, sin, w, tag, next_gain):
    d = h.shape[1]
    hn, = _rows(_rms, [h], [w["norm_mix"]], [(d, bf16)], name=f"mix_norm_{tag}")
    proj = _mm(hn, w["mix_w_in"], name=f"mix_in_{tag}")
    pa = [w["lb0"], w["lb1"], w["mla_q_a_norm"], w["mla_kv_a_norm"]]
    qf, kk, vv, logf, cqn, ckvn = _rows(
        _mix_a, [proj], pa, [(HG_WIDTH, f32)] * 4 + [(MLA_Q_RANK, bf16), (MLA_KV_RANK, bf16)], name=f"mix_a_{tag}")
    o_hg, states = _hg_fwd(qf, kk, vv, logf, name=f"hg_fwd_{tag}")
    qraw = _mm(cqn, w["mla_w_uq"], name=f"mla_uq_{tag}")
    kvraw = _mm(ckvn, w["mla_w_ukv"], name=f"mla_ukv_{tag}")
    pb = [w["mla_qn_nope"], w["mla_qn_rope"], w["mla_kn_nope"], w["mla_kn_rope"], w["rope_perm"]]
    kpe_raw, gate = _window(proj, _C_KPE, IN_PAD - _C_KPE), _window(proj, 3 * HG_WIDTH, HG_WIDTH)
    qfull, kfull, vfull = _rows(_mix_b, [qraw, kvraw, kpe_raw, cos, sin], pb,
                                [(MLA_HEADS * MLA_QK_PAD, bf16)] * 2 + [(MLA_HEADS * MLA_V, bf16)],
                                name=f"mix_b_{tag}")
    o_mla, lse = _att_fwd(qfull, kfull, vfull, name=f"att_fwd_{tag}")
    mixin, = _rows(_mix_c, [o_hg, gate, o_mla], [w["hg_out_norm"]], [(d, bf16)], name=f"mix_c_{tag}")
    if callable(w["mix_w_out"]):
        w["mix_w_out"] = w["mix_w_out"](mixin)
    out, normed = _mm(mixin, w["mix_w_out"], add=h, norm=next_gain, name=f"mix_out_{tag}")
    return out, (h, hn, proj, qf, kk, vv, logf, cqn, ckvn, o_hg, states, qraw, kvraw, qfull, kfull, vfull, o_mla,
                 lse, mixin), normed


def _mixer_bwd(dout, cos, sin, w, saved, tag, on_w_out=None):
    (h, hn, proj, qf, kk, vv, logf, cqn, ckvn, o_hg, states, qraw, kvraw, qfull, kfull, vfull, o_mla, lse,
     mixin) = saved
    g = {}
    dmixin = _mm(dout, w["mix_w_out"], tb=True, name=f"mix_dmixin_{tag}")
    g["mix_w_out"] = _mm(mixin, dout, ta=True, name=f"mix_dwout_{tag}")
    if on_w_out is not None:
        dmixin = on_w_out(g["mix_w_out"], dmixin)
    kpe_raw, gate = _window(proj, _C_KPE, IN_PAD - _C_KPE), _window(proj, 3 * HG_WIDTH, HG_WIDTH)
    (do_hg, dgate, do_mla), (g["hg_out_norm"],) = _rows_bwd(
        _mix_c, [o_hg, gate, o_mla], [w["hg_out_norm"]], [dmixin], rgrad=[f32, f32, bf16], pgrad=[True],
        name=f"mix_dc_{tag}")
    dqfull, dkfull, dvfull = _att_bwd(qfull, kfull, vfull, o_mla, lse, do_mla, name=f"att_bwd_{tag}")
    pb = [w["mla_qn_nope"], w["mla_qn_rope"], w["mla_kn_nope"], w["mla_kn_rope"], w["rope_perm"]]
    (dqraw, dkvraw, dkpe_raw), pg = _rows_bwd(
        _mix_b, [qraw, kvraw, kpe_raw, cos, sin], pb, [dqfull, dkfull, dvfull],
        rgrad=[bf16, bf16, f32, None, None], pgrad=[True, True, True, True, False],
        name=f"mix_db_{tag}")
    g["mla_qn_nope"], g["mla_qn_rope"], g["mla_kn_nope"], g["mla_kn_rope"] = pg
    dcqn = _mm(dqraw, w["mla_w_uq"], tb=True, name=f"mla_dcq_{tag}")
    g["mla_w_uq"] = _mm(cqn, dqraw, ta=True, name=f"mla_dwuq_{tag}")
    dckvn = _mm(dkvraw, w["mla_w_ukv"], tb=True, name=f"mla_dckv_{tag}")
    g["mla_w_ukv"] = _mm(ckvn, dkvraw, ta=True, name=f"mla_dwukv_{tag}")
    dqf, dkk, dvv, dlogf = _hg_bwd(qf, kk, vv, logf, states, do_hg, name=f"hg_bwd_{tag}")
    pa = [w["lb0"], w["lb1"], w["mla_q_a_norm"], w["mla_kv_a_norm"]]
    (dproj,), (g["lb0"], g["lb1"], g["mla_q_a_norm"], g["mla_kv_a_norm"]) = _rows_bwd(
        _mix_a, [proj], pa, [dqf, dkk, dvv, dlogf, dcqn, dckvn], rgrad=[bf16], pgrad=[True] * 4,
        addends={0: [(dgate, 3 * HG_WIDTH), (dkpe_raw, _C_KPE)]}, name=f"mix_da_{tag}")
    dh, g["norm_mix"] = _mm(dproj, w["mix_w_in"], tb=True, rms_bwd=(h, w["norm_mix"], dout), name=f"mix_dhn_{tag}")
    g["mix_w_in_t"] = _mm(dproj, hn, ta=True, name=f"mix_dwin_{tag}")
    return dh, g


def _rope_tables(positions):
    inv_freq = 1.0 / (ROPE_BASE ** (jnp.arange(0, MLA_ROPE, 2, dtype=f32) / MLA_ROPE))
    ang = positions.astype(f32)[:, None] * inv_freq
    z = jnp.zeros((positions.shape[0], 128 - MLA_ROPE), f32)
    return (jnp.concatenate([jnp.cos(ang), jnp.cos(ang), z], axis=1),
            jnp.concatenate([jnp.sin(ang), jnp.sin(ang), z], axis=1))


def _pad_cols(a, n):
    return jnp.pad(a, ((0, 0), (0, n - a.shape[1])))


def _even_weights(p, j, layer, dt):
    w_uq = p["mla_w_uq"][j].reshape(MLA_Q_RANK, MLA_HEADS, MLA_QK)
    w_uq = jnp.pad(w_uq, ((0, 0), (0, 0), (0, MLA_QK_PAD - MLA_QK))).reshape(MLA_Q_RANK, MLA_HEADS * MLA_QK_PAD)
    return dict(
        norm_mix=p["norm_mix"][layer][None], mix_w_in=_pad_cols(p["mix_w_in"][j], IN_PAD).astype(dt),
        lb0=p["hg_lb_logits"][0][None], lb1=p["hg_lb_logits"][1][None],
        mla_q_a_norm=p["mla_q_a_norm"][j][None], mla_kv_a_norm=p["mla_kv_a_norm"][j][None],
        mla_w_uq=w_uq.astype(dt), mla_w_ukv=p["mla_w_ukv"][j].astype(dt),
        mla_qn_nope=p["mla_qn_nope"][j][None], mla_qn_rope=_pad_cols(p["mla_qn_rope"][j][None], 128),
        mla_kn_nope=p["mla_kn_nope"][j][None], mla_kn_rope=_pad_cols(p["mla_kn_rope"][j][None], 128),
        rope_perm=_rope_perm(), hg_out_norm=p["hg_out_norm"][j][None],
        mix_w_out=p["mix_w_out"][j].astype(dt) if "mix_w_out" in p else None)


def _even_grads(g):
    w_uq = g["mla_w_uq"].reshape(MLA_Q_RANK, MLA_HEADS, MLA_QK_PAD)[:, :, :MLA_QK].reshape(MLA_Q_RANK, -1)
    return dict(
        norm_mix=g["norm_mix"], mix_w_in=g["mix_w_in_t"][:IN_WIDTH].T[None],
        hg_lb_logits=jnp.concatenate([g["lb0"], g["lb1"]], axis=0),
        mla_q_a_norm=g["mla_q_a_norm"], mla_kv_a_norm=g["mla_kv_a_norm"], mla_w_uq=w_uq[None],
        mla_w_ukv=g["mla_w_ukv"][None], mla_qn_nope=g["mla_qn_nope"], mla_qn_rope=g["mla_qn_rope"][:, :MLA_ROPE],
        mla_kn_nope=g["mla_kn_nope"], mla_kn_rope=g["mla_kn_rope"][:, :MLA_ROPE],
        hg_out_norm=g["hg_out_norm"], mix_w_out=g["mix_w_out"][None])


_S5_NB = 8
_S5_BW = 1024
_S5_HALF = 512
_S5_UC = 128
_S5_TIME = 512


def _cmul(ar, ai, br, bi):
    return ar * br - ai * bi, ar * bi + ai * br


def _pow_table(ar, ai, descending):
    rows = lax.broadcasted_iota(jnp.int32, (8, ar.shape[1]), 0)
    tr = jnp.zeros((8, ar.shape[1]), f32)
    ti = jnp.zeros((8, ar.shape[1]), f32)
    pr, pi_ = ar, ai
    for r in range(8):
        sel = rows == ((7 - r) if descending else r)
        tr = jnp.where(sel, pr, tr)
        ti = jnp.where(sel, pi_, ti)
        pr, pi_ = _cmul(pr, pi_, ar, ai)
    return tr, ti


def _s5_tile_scan(work, carry, ar, ai, tc, reverse, per_tile=None):
    hw = _S5_HALF
    row8 = lax.broadcasted_iota(jnp.int32, (8, hw), 0)
    powers = [(ar, ai)]
    for _ in range(2):
        powers.append(_cmul(*powers[-1], *powers[-1]))
    steps = []
    for (mr, mi), s in zip(powers, (1, 2, 4)):
        ok = (row8 < 8 - s) if reverse else (row8 >= s)
        steps.append((jnp.where(ok, mr, 0.0), jnp.where(ok, mi, 0.0), 8 - s if reverse else s))
    tr, ti = _pow_table(ar, ai, reverse)
    cr, ci = carry[:, :hw], carry[:, hw:]
    tiles = range(tc // 8)
    for i in (reversed(tiles) if reverse else tiles):
        sl = slice(8 * i, 8 * i + 8)
        xr, xi = work[sl, :hw], work[sl, hw:]
        for mr, mi, shift in steps:
            pr, pi_ = _cmul(mr, mi, pltpu.roll(xr, shift, axis=0), pltpu.roll(xi, shift, axis=0))
            xr, xi = xr + pr, xi + pi_
        pr, pi_ = _cmul(tr, ti, cr, ci)
        xr, xi = xr + pr, xi + pi_
        work[sl, :hw] = xr
        work[sl, hw:] = xi
        if per_tile is not None:
            per_tile(sl, xr, xi, cr, ci)
        edge = 8 * i if reverse else 8 * i + 7
        cr, ci = work[edge:edge + 1, :hw], work[edge:edge + 1, hw:]
    carry[:, :hw] = cr
    carry[:, hw:] = ci


def _s5_core_fwd(a, hn, b3, c3, *, name):
    length = hn.shape[0]
    tc = min(_S5_TIME, length)

    def body(a_ref, hn_ref, b_ref, c_ref, hs_ref, y_ref, work, carry):
        @pl.when(pl.program_id(1) == 0)
        def _():
            carry[...] = jnp.zeros_like(carry)

        work[...] = jnp.dot(hn_ref[...].astype(bf16), b_ref[...], preferred_element_type=f32)
        _s5_tile_scan(work, carry, a_ref[:, :_S5_HALF], a_ref[:, _S5_HALF:], tc, False)
        hs = work[...].astype(bf16)
        hs_ref[...] = hs
        y_ref[...] = jnp.dot(hs, c_ref[...], preferred_element_type=f32)

    return pl.pallas_call(
        body, grid=(_S5_NB, length // tc),
        in_specs=[pl.BlockSpec((1, _S5_BW), lambda j, t: (0, j)), pl.BlockSpec((tc, _S5_UC), lambda j, t: (t, j)),
                  pl.BlockSpec((_S5_UC, _S5_BW), lambda j, t: (j, 0)), pl.BlockSpec((_S5_BW, _S5_UC), lambda j, t: (j, 0))],
        out_specs=[pl.BlockSpec((tc, _S5_BW), lambda j, t: (t, j)), pl.BlockSpec((tc, _S5_UC), lambda j, t: (t, j))],
        out_shape=[jax.ShapeDtypeStruct((length, _S5_NB * _S5_BW), bf16),
                   jax.ShapeDtypeStruct((length, _S5_NB * _S5_UC), f32)],
        scratch_shapes=[pltpu.VMEM((tc, _S5_BW), f32), pltpu.VMEM((1, _S5_BW), f32)],
        compiler_params=_cparams(("parallel", "arbitrary")), name=name)(a, hn, b3, c3)


def _s5_core_bwd(a, dy, c3, hs, hn, b3, *, name):
    length = hn.shape[0]
    tc = min(_S5_TIME, length)
    nt = length // tc
    hw = _S5_HALF

    def body(a_ref, dy_ref, c_ref, hs_ref, hn_ref, b_ref, du_ref, db_ref, dc_ref, da_ref, work, carry, acc):
        @pl.when(pl.program_id(1) == 0)
        def _():
            carry[...] = jnp.zeros_like(carry)
            db_ref[...] = jnp.zeros_like(db_ref)
            dc_ref[...] = jnp.zeros_like(dc_ref)
            da_ref[...] = jnp.zeros_like(da_ref)

        dyb = dy_ref[...].astype(bf16)
        work[...] = lax.dot_general(dyb, c_ref[...], _NT, preferred_element_type=f32)
        acc[...] = jnp.zeros_like(acc)
        row8 = lax.broadcasted_iota(jnp.int32, (8, hw), 0)

        def grad_a(sl, gr, gi, cr, ci):
            gnr = jnp.where(row8 == 7, cr, pltpu.roll(gr, 7, axis=0))
            gni = jnp.where(row8 == 7, ci, pltpu.roll(gi, 7, axis=0))
            hr, hi = hs_ref[sl, :hw].astype(f32), hs_ref[sl, hw:].astype(f32)
            acc[:, :hw] += hr * gnr + hi * gni
            acc[:, hw:] += hr * gni - hi * gnr

        _s5_tile_scan(work, carry, a_ref[:, :hw], -a_ref[:, hw:], tc, True, grad_a)
        da_ref[...] += jnp.sum(acc[...], axis=0, keepdims=True)
        g = work[...].astype(bf16)
        du_ref[...] = lax.dot_general(g, b_ref[...], _NT, preferred_element_type=f32)
        db_ref[...] += lax.dot_general(hn_ref[...].astype(bf16), g, _TN, preferred_element_type=f32)
        dc_ref[...] += lax.dot_general(hs_ref[...], dyb, _TN, preferred_element_type=f32)

    rev = lambda j, t: (nt - 1 - t, j)
    return pl.pallas_call(
        body, grid=(_S5_NB, nt),
        in_specs=[pl.BlockSpec((1, _S5_BW), lambda j, t: (0, j)), pl.BlockSpec((tc, _S5_UC), rev),
                  pl.BlockSpec((_S5_BW, _S5_UC), lambda j, t: (j, 0)), pl.BlockSpec((tc, _S5_BW), rev),
                  pl.BlockSpec((tc, _S5_UC), rev), pl.BlockSpec((_S5_UC, _S5_BW), lambda j, t: (j, 0))],
        out_specs=[pl.BlockSpec((tc, _S5_UC), rev), pl.BlockSpec((_S5_UC, _S5_BW), lambda j, t: (j, 0)),
                   pl.BlockSpec((_S5_BW, _S5_UC), lambda j, t: (j, 0)), pl.BlockSpec((1, _S5_BW), lambda j, t: (0, j))],
        out_shape=[jax.ShapeDtypeStruct((length, _S5_NB * _S5_UC), f32),
                   jax.ShapeDtypeStruct((_S5_NB * _S5_UC, _S5_BW), f32),
                   jax.ShapeDtypeStruct((_S5_NB * _S5_BW, _S5_UC), f32),
                   jax.ShapeDtypeStruct((1, _S5_NB * _S5_BW), f32)],
        scratch_shapes=[pltpu.VMEM((tc, _S5_BW), f32), pltpu.VMEM((1, _S5_BW), f32), pltpu.VMEM((8, _S5_BW), f32)],
        compiler_params=_cparams(("parallel", "arbitrary")), name=name)(a, dy, c3, hs, hn, b3)


def _s5_disc(lr, li, ldt, btr, bti, expand):
    dt = jnp.exp(ldt)
    mag = jnp.exp(lr * dt)
    abr = mag * jnp.cos(li * dt)
    abi = mag * jnp.sin(li * dt)
    den = lr * lr + li * li
    zr = ((abr - 1.0) * lr + abi * li) / den
    zi = (abi * lr - (abr - 1.0) * li) / den
    zr = jnp.dot(zr, expand, precision=lax.Precision.HIGHEST, preferred_element_type=f32)
    zi = jnp.dot(zi, expand, precision=lax.Precision.HIGHEST, preferred_element_type=f32)
    return abr, abi, zr * btr - zi * bti, zr * bti + zi * btr


def _s5_disc_fwd(args, *, name):
    def body(*refs):
        res = _s5_disc(*[r[...] for r in refs[:6]])
        for o, v in zip(refs[6:], res):
            o[...] = v

    sds = jax.ShapeDtypeStruct
    return pl.pallas_call(body, out_shape=[sds(args[0].shape, f32)] * 2 + [sds(args[3].shape, f32)] * 2,
                          name=name)(*args)


def _s5_disc_bwd(args, cts, *, name):
    def body(*refs):
        vals = [r[...] for r in refs[:6]]
        _, vjp = jax.vjp(lambda *d: _s5_disc(*d, vals[5]), *vals[:5])
        grads = vjp(tuple(r[...] for r in refs[6:10]))
        for o, v in zip(refs[10:], grads):
            o[...] = v

    return pl.pallas_call(body, out_shape=[jax.ShapeDtypeStruct(a.shape, f32) for a in args[:5]],
                          name=name)(*args, *cts)


def _gelu_tanh(x):
    return 0.5 * x * (1.0 + jnp.tanh(0.7978845608028654 * (x + 0.044715 * (x * x * x))))


def _s5_post(y, u, d_skip):
    return _gelu_tanh(y + d_skip * u)


def _s5_glu(ga, gb, h):
    return h + ga * jax.nn.sigmoid(gb)


def _s5_glu_norm(ga, gb, h, next_gain):
    out = _s5_glu(ga, gb, h)
    return out, _rms(out, next_gain)


def _s5_expand():
    e = np.zeros((S5_STATE, S5_GROUP * S5_STATE), np.float32)
    for m in range(S5_GROUP):
        e[np.arange(S5_STATE), m * S5_STATE + np.arange(S5_STATE)] = 1.0
    return jnp.asarray(e)


def _s5_pack_b(bbr, bbi):
    eye = jnp.eye(8, dtype=f32)

    def one(bb):
        b5 = bb.reshape(_S5_NB, 8, S5_GROUP, S5_STATE)
        return jnp.einsum("jgmp,gh->jgmhp", b5, eye).reshape(_S5_NB * _S5_UC, _S5_HALF)

    return jnp.concatenate([one(bbr), one(bbi)], axis=1)


def _s5_unpack_b(db3):
    def one(d):
        d5 = d.reshape(_S5_NB, 8, S5_GROUP, 8, S5_STATE)
        return jnp.einsum("jgmgp->jgmp", d5).reshape(S5_GROUPS, S5_GROUP * S5_STATE)

    return one(db3[:, :_S5_HALF]), one(db3[:, _S5_HALF:])


def _s5_pack_c(c_re, c_im):
    eye = jnp.eye(8, dtype=f32)

    def one(c):
        c4 = c.reshape(_S5_NB, 8, S5_GROUP, S5_STATE)
        return jnp.einsum("jgmp,hg->jhpgm", c4, eye).reshape(_S5_NB, _S5_HALF, _S5_UC)

    return jnp.concatenate([one(c_re), -one(c_im)], axis=1).reshape(_S5_NB * _S5_BW, _S5_UC)


def _s5_unpack_c(dc3):
    d = dc3.reshape(_S5_NB, 2, 8, S5_STATE, 8, S5_GROUP)
    dre = jnp.einsum("jgpgm->jgmp", d[:, 0]).reshape(S5_GROUPS, S5_GROUP, S5_STATE)
    dim = -jnp.einsum("jgpgm->jgmp", d[:, 1]).reshape(S5_GROUPS, S5_GROUP, S5_STATE)
    return dre, dim


def _s5_state_row(re, im):
    r = re.reshape(_S5_NB, 1, _S5_HALF)
    i = im.reshape(_S5_NB, 1, _S5_HALF)
    return jnp.concatenate([r, i], axis=2).reshape(1, _S5_NB * _S5_BW)


def _s5_unstate_row(row):
    r = row.reshape(_S5_NB, 2, 8, S5_STATE)
    return r[:, 0].reshape(S5_GROUPS, S5_STATE), r[:, 1].reshape(S5_GROUPS, S5_STATE)


def _s5_fwd(h, w, tag, next_gain):
    d = h.shape[1]
    hn, = _rows(_rms, [h], [w["norm_mix"]], [(d, f32)], name=f"s5_norm_{tag}")
    disc_in = [w["s5_lam_re"], w["s5_lam_im"], w["s5_log_dt"], w["s5_bt_re"], w["s5_bt_im"], w["s5_expand"]]
    abr, abi, bbr, bbi = _s5_disc_fwd(disc_in, name=f"s5_disc_{tag}")
    a_row = _s5_state_row(abr, abi)
    b3 = _s5_pack_b(bbr, bbi).astype(bf16)
    hs, y = _s5_core_fwd(a_row, hn, b3, w["s5_c3"], name=f"s5_core_{tag}")
    yg, = _rows(_s5_post, [y, hn], [w["s5_d"]], [(d, bf16)], name=f"s5_post_{tag}")
    ga = _mm(yg, w["s5_w_glu_a"], name=f"s5_glu_a_{tag}")
    gb = _mm(yg, w["s5_w_glu_b"], name=f"s5_glu_b_{tag}")
    out, normed = _rows(_s5_glu_norm, [ga, gb, h], [next_gain], [(d, f32), (d, bf16)], name=f"s5_glu_{tag}")
    return out, (h, hn, disc_in, a_row, b3, hs, y, yg, ga, gb), normed


def _s5_bwd(dout, w, saved, tag):
    h, hn, disc_in, a_row, b3, hs, y, yg, ga, gb = saved
    g = {}
    (dga, dgb), _ = _rows_bwd(_s5_glu, [ga, gb, h], [], [dout], rgrad=[bf16, bf16, None], pgrad=[],
                              name=f"s5_dglu_{tag}")
    dyg = _mm(dga, w["s5_w_glu_a"], tb=True, name=f"s5_dyg_a_{tag}")
    dyg = _mm(dgb, w["s5_w_glu_b"], tb=True, add=dyg, name=f"s5_dyg_b_{tag}")
    g["s5_w_glu_a"] = _mm(yg, dga, ta=True, name=f"s5_dwa_{tag}")
    g["s5_w_glu_b"] = _mm(yg, dgb, ta=True, name=f"s5_dwb_{tag}")
    (dy, du_skip), (g["s5_d"],) = _rows_bwd(_s5_post, [y, hn], [w["s5_d"]], [dyg], rgrad=[bf16, f32], pgrad=[True],
                                           name=f"s5_dpost_{tag}")
    du, db3, dc3, da_row = _s5_core_bwd(a_row, dy, w["s5_c3"], hs, hn, b3, name=f"s5_dcore_{tag}")
    dabr, dabi = _s5_unstate_row(da_row)
    dbbr, dbbi = _s5_unpack_b(db3)
    g["s5_lam_re"], g["s5_lam_im"], g["s5_log_dt"], g["s5_bt_re"], g["s5_bt_im"] = _s5_disc_bwd(
        disc_in, [dabr, dabi, dbbr, dbbi], name=f"s5_ddisc_{tag}")
    g["s5_c_re"], g["s5_c_im"] = _s5_unpack_c(dc3)
    (dh,), (g["norm_mix"],) = _rows_bwd(_rms_twice, [h], [w["norm_mix"]], [du, du_skip], rgrad=[f32], pgrad=[True],
                                        addends={0: dout}, name=f"s5_dnorm_{tag}")
    return dh, g


def _odd_weights(p, j, layer, dt):
    tr = lambda b: b.transpose(0, 2, 1).reshape(S5_GROUPS, S5_GROUP * S5_STATE)
    return dict(
        norm_mix=p["norm_mix"][layer][None], s5_lam_re=p["s5_lam_re"][j], s5_lam_im=p["s5_lam_im"][j],
        s5_log_dt=p["s5_log_dt"][j][:, None], s5_bt_re=tr(p["s5_b_re"][j]), s5_bt_im=tr(p["s5_b_im"][j]),
        s5_expand=_s5_expand(), s5_c3=_s5_pack_c(p["s5_c_re"][j], p["s5_c_im"][j]).astype(dt),
        **{n: (p[n][j][None] if n == "s5_d" else p[n][j].astype(dt))
           for n in ("s5_d", "s5_w_glu_a", "s5_w_glu_b") if n in p})


def _odd_grads(g):
    tr = lambda b: b.reshape(S5_GROUPS, S5_GROUP, S5_STATE).transpose(0, 2, 1)[None]
    return dict(
        norm_mix=g["norm_mix"], s5_lam_re=g["s5_lam_re"][None], s5_lam_im=g["s5_lam_im"][None],
        s5_log_dt=g["s5_log_dt"][:, 0][None], s5_b_re=tr(g["s5_bt_re"]), s5_b_im=tr(g["s5_bt_im"]),
        s5_c_re=g["s5_c_re"][None], s5_c_im=g["s5_c_im"][None], s5_d=g["s5_d"],
        s5_w_glu_a=g["s5_w_glu_a"][None], s5_w_glu_b=g["s5_w_glu_b"][None])


def _loss_fn(y, t):
    e = y - t
    part = jnp.sum(jnp.sum(e * e, axis=-1, keepdims=True), axis=0, keepdims=True) * (0.5 / y.shape[1])
    return e * (1.0 / y.shape[1]), part


FF_SHARD = 352
FF_SHARD_PAD = 384


def _pad_groups(a, axis):
    axis %= a.ndim
    zeros = jnp.zeros(a.shape[:axis] + (FF_SHARD_PAD - FF_SHARD,) + a.shape[axis + 1:], a.dtype)
    pieces = []
    for g in range(a.shape[axis] // FF_SHARD):
        pieces += [lax.slice_in_dim(a, g * FF_SHARD, (g + 1) * FF_SHARD, axis=axis), zeros]
    return jnp.concatenate(pieces, axis=axis)


def _unpad_groups(a, axis):
    axis %= a.ndim
    pieces = [lax.slice_in_dim(a, g * FF_SHARD_PAD, g * FF_SHARD_PAD + FF_SHARD, axis=axis)
              for g in range(a.shape[axis] // FF_SHARD_PAD)]
    return pieces[0] if len(pieces) == 1 else jnp.concatenate(pieces, axis=axis)


def _layer_weights(p, layer, dt):
    return dict(
        norm_xa=p["norm_xa"][layer][None], norm_mem=p["norm_mem"][layer][None], norm_ffn=p["norm_ffn"][layer][None],
        xa_wq=p["xa_wq"][layer].astype(dt), xa_wk=p["xa_wk"][layer].astype(dt), xa_wv=p["xa_wv"][layer].astype(dt),
        xa_wo=p["xa_wo"][layer].astype(dt), xa_q_norm=p["xa_q_norm"][layer][None],
        xa_k_norm=p["xa_k_norm"][layer][None], ffn_w_up=_pad_groups(p["ffn_w_up"][layer], 1).astype(dt),
        ffn_conv_w=_pad_groups(p["ffn_conv_w"][layer], 1), ffn_conv_b=_pad_groups(p["ffn_conv_b"][layer][None], 1),
        ffn_w_down=_pad_groups(p["ffn_w_down"][layer], 0).astype(dt))


_PER_LAYER = ("norm_xa", "norm_mem", "norm_ffn", "xa_wq", "xa_wk", "xa_wv", "xa_wo", "xa_q_norm", "xa_k_norm",
              "ffn_w_up", "ffn_conv_w", "ffn_conv_b", "ffn_w_down")
_FFN_PADDED = dict(ffn_w_up=1, ffn_conv_w=1, ffn_conv_b=1, ffn_w_down=0)


def _local_step(x, mem, positions, target, p):
    cos, sin = _rope_tables(positions)
    we = _even_weights(p, 0, 0, bf16)
    wo = _odd_weights(p, 0, 1, bf16)
    wl = [_layer_weights(p, layer, bf16) for layer in range(2)]
    loss, dh, g_even, g_odd, gl = _local_core(x, mem, cos, sin, target, we, wo, wl)
    grads = {}
    for n in _PER_LAYER:
        a, b = gl[0][n], gl[1][n]
        if n in _FFN_PADDED:
            a, b = _unpad_groups(a, _FFN_PADDED[n]), _unpad_groups(b, _FFN_PADDED[n])
        grads[n] = jnp.concatenate([a, b], axis=0) if a.shape[0] == 1 else jnp.stack([a, b])
    ge, go = _even_grads(g_even), _odd_grads(g_odd)
    grads["norm_mix"] = jnp.concatenate([ge.pop("norm_mix"), go.pop("norm_mix")], axis=0)
    grads.update(ge)
    grads.update(go)
    return loss, dh, grads


def _local_core(x, mem, cos, sin, target, we, wo, wl):
    h, s_mix0, hx = _mixer_fwd(x, cos, sin, we, "l0", wl[0]["norm_xa"])
    h, s_xa0, hf = _xattn_fwd(h, mem, wl[0], "l0", hx)
    h, s_ff0 = _ffn_fwd(h, wl[0], "l0", hf)
    h, s_mix1, hx = _s5_fwd(h, wo, "l1", wl[1]["norm_xa"])
    h, s_xa1, hf = _xattn_fwd(h, mem, wl[1], "l1", hx)
    h, s_ff1 = _ffn_fwd(h, wl[1], "l1", hf)
    dh, loss = _rows(_loss_fn, [h, target], [], [(h.shape[1], f32)], accs=[(1, 1)], name="loss_head")

    gl = [{}, {}]
    dh, g = _ffn_bwd(dh, wl[1], s_ff1, "l1")
    gl[1].update(g)
    dh, g = _xattn_bwd(dh, mem, wl[1], s_xa1, "l1")
    gl[1].update(g)
    dh, g_odd = _s5_bwd(dh, wo, s_mix1, "l1")
    dh, g = _ffn_bwd(dh, wl[0], s_ff0, "l0")
    gl[0].update(g)
    dh, g = _xattn_bwd(dh, mem, wl[0], s_xa0, "l0")
    gl[0].update(g)
    dh, g_even = _mixer_bwd(dh, cos, sin, we, s_mix0, "l0")
    return loss, dh, g_even, g_odd, gl


_LANES = 1024
_ROW_PAD = 16


_PEER_MASKS = (1, 2, 4, 3, 5, 6, 7)


def _mesh_place():
    x, y, c = lax.axis_index("x"), lax.axis_index("y"), lax.axis_index("c")

    def peer(mask):
        px = 1 - x if mask & 4 else x
        py = 1 - y if mask & 2 else y
        pc = 1 - c if mask & 1 else c
        return (px, py, pc), 4 * px + 2 * py + pc

    return 4 * x + 2 * y + c, peer


class _Exchange:
    def __init__(self, name):
        self.name = name
        self.srcs, self.shapes, self.items, self.where = [], [], [], {}

    def add(self, src, land_shape, src_at, dst_at, key):
        si = next((i for i, s in enumerate(self.srcs) if s is src), None)
        if si is None:
            self.srcs.append(src)
            si = len(self.srcs) - 1
        if key not in self.where:
            self.shapes.append(land_shape)
            self.where[key] = len(self.shapes) - 1
        self.items.append(dict(src=si, dst=self.where[key], src_at=src_at, dst_at=dst_at))

    def _copy(self, k, mask, ins, lands, send_sems, recv_sems, me, peer, arriving):
        it = self.items[k]
        dev, idx = peer(mask)
        s = k * (N_DEV - 1) + _PEER_MASKS.index(mask)
        return pltpu.make_async_remote_copy(
            src_ref=it["src_at"](ins[it["src"]], idx), dst_ref=it["dst_at"](lands[it["dst"]], idx if arriving else me),
            send_sem=send_sems.at[s], recv_sem=recv_sems.at[s], device_id=dev, device_id_type=pl.DeviceIdType.MESH)

    def _own_copy(self, k, ins, lands, own_sems, me):
        it = self.items[k]
        return pltpu.make_async_copy(it["src_at"](ins[it["src"]], me), it["dst_at"](lands[it["dst"]], me), own_sems.at[k])

    def begin(self, own):
        ns, nd, ni = len(self.srcs), len(self.shapes), len(self.items)
        nsem = ni * (N_DEV - 1)
        self.own = own

        nq = 3 if own else 2

        def body(*refs):
            ins, land_refs = refs[:ns], refs[ns:ns + nd]
            sems, token = refs[ns + nd:ns + nd + nq], refs[-1]
            me, peer = _mesh_place()
            for mask in _PEER_MASKS:
                for k in range(ni):
                    self._copy(k, mask, ins, land_refs, sems[0], sems[1], me, peer, False).start()
            if own:
                for k in range(ni):
                    self._own_copy(k, ins, land_refs, sems[2], me).start()
            token[...] = jnp.zeros_like(token)

        hbm = pl.BlockSpec(memory_space=pltpu.HBM)
        sem = pl.BlockSpec(memory_space=pltpu.SEMAPHORE)
        lands = [lax.empty(s.shape, s.dtype) for s in self.shapes]
        sem_shapes = [pltpu.SemaphoreType.DMA((nsem,)), pltpu.SemaphoreType.DMA((nsem,)), pltpu.SemaphoreType.DMA((ni,))]
        res = pl.pallas_call(
            body, in_specs=[hbm] * (ns + nd),
            out_specs=[sem] * nq + [hbm] * nd + [pl.BlockSpec(memory_space=pltpu.VMEM)],
            out_shape=sem_shapes[:nq] + [pltpu.HBM(s.shape, s.dtype) for s in self.shapes]
            + [jax.ShapeDtypeStruct((8, 128), f32)],
            input_output_aliases={ns + j: nq + j for j in range(nd)},
            compiler_params=pltpu.CompilerParams(has_side_effects=pltpu.SideEffectType.DATAFLOW_SIDE_EFFECTING),
            name=self.name + "_start")(*self.srcs, *lands)
        self.token = res[-1]
        return list(res[:nq]), list(res[nq:-1])

    def finish(self, state, after):
        sems, lands = state
        nq = len(sems)
        after = list(after) if isinstance(after, (list, tuple)) else [after]
        ns, nd, ni = len(self.srcs), len(self.shapes), len(self.items)

        def body(*refs):
            ins, land_refs = refs[:ns], refs[ns:ns + nd]
            sem_refs = refs[ns + nd:ns + nd + nq]
            me, peer = _mesh_place()
            for mask in _PEER_MASKS:
                for k in range(ni):
                    cp = self._copy(k, mask, ins, land_refs, sem_refs[0], sem_refs[1], me, peer, True)
                    cp.wait_send()
                    cp.wait_recv()
            if self.own:
                for k in range(ni):
                    self._own_copy(k, ins, land_refs, sem_refs[2], me).wait()

        hbm = pl.BlockSpec(memory_space=pltpu.HBM)
        sem = pl.BlockSpec(memory_space=pltpu.SEMAPHORE)
        res = pl.pallas_call(
            body, in_specs=[hbm] * (ns + nd) + [sem] * nq + [pl.BlockSpec(memory_space=pl.ANY)] * len(after),
            out_specs=[hbm] * nd, out_shape=[pltpu.HBM(s.shape, s.dtype) for s in self.shapes],
            input_output_aliases={ns + j: j for j in range(nd)},
            compiler_params=pltpu.CompilerParams(has_side_effects=pltpu.SideEffectType.DATAFLOW_SIDE_EFFECTING),
            name=self.name + "_wait")(*self.srcs, *lands, *sems, *after)
        return {k: res[i] for k, i in self.where.items()}


def _after(x, *tokens, name):
    def body(*refs):
        del refs

    anyspace = pl.BlockSpec(memory_space=pl.ANY)
    return pl.pallas_call(body, in_specs=[anyspace] * (1 + len(tokens)), out_specs=anyspace,
                          out_shape=jax.ShapeDtypeStruct(x.shape, x.dtype), input_output_aliases={0: 0},
                          name=name)(x, *tokens)


def _rows_of(n):
    return lambda r, i: r.at[pl.ds(pl.multiple_of(i * n, n), n), :]


def _cols_of(n):
    return lambda r, i: r.at[:, pl.ds(pl.multiple_of(i * n, n), n)]


def _whole(r, i):
    return r


def _slot(r, i):
    return r.at[i]


def _at_layer(layer):
    return lambda r, i: r.at[layer]


def _sum_adam(me_index, slots, owns, own_block, w, m, v, *, name):
    layers, rows, cols = w.shape
    tr = _pick(rows, (256, 128, 104, 64, 32, 16, 8))
    bc1 = 1.0 - ADAM_B1 ** ADAM_STEP
    bc2 = 1.0 - ADAM_B2 ** ADAM_STEP
    own_shape, own_map = own_block(tr)
    nl = len(slots)
    assert nl == layers and len(owns) == layers

    def body(me_ref, *refs):
        s_refs, own_refs = refs[:nl], refs[nl:2 * nl]
        w_ref, m_ref, v_ref, g_ref, d_ref, nm_ref, nv_ref = refs[2 * nl:]
        me = me_ref[0]

        def run(s_ref, own_ref):
            mine = own_ref[0] if len(own_shape) == 3 else own_ref[...]
            g = jnp.where(me == 0, mine, s_ref[0])
            for k in range(1, N_DEV):
                g = g + jnp.where(me == k, mine, s_ref[k])
            mm = ADAM_B1 * m_ref[0] + (1.0 - ADAM_B1) * g
            vv = ADAM_B2 * v_ref[0] + (1.0 - ADAM_B2) * (g * g)
            g_ref[0] = g
            nm_ref[0] = mm
            nv_ref[0] = vv
            d_ref[0] = -ADAM_LR * ((mm / bc1) / (jnp.sqrt(vv / bc2) + ADAM_EPS) + ADAM_WD * w_ref[0])

        for layer in range(nl):
            pl.when(pl.program_id(0) == layer)(functools.partial(run, s_refs[layer], own_refs[layer]))

    def of_layer(layer, index_map):
        return lambda lyr, i, me: index_map(jnp.where(lyr == layer, i, 0), me)

    blk = pl.BlockSpec((1, tr, cols), lambda lyr, i, me: (lyr, i, 0))
    sds = jax.ShapeDtypeStruct((layers, rows, cols), f32)
    grid_spec = pltpu.PrefetchScalarGridSpec(
        num_scalar_prefetch=1, grid=(layers, rows // tr),
        in_specs=[pl.BlockSpec((N_DEV, tr, cols), of_layer(layer, lambda i, me: (0, i, 0))) for layer in range(nl)]
        + [pl.BlockSpec(own_shape, of_layer(layer, own_map)) for layer in range(nl)] + [blk, blk, blk],
        out_specs=[blk] * 4)
    return pl.pallas_call(body, grid_spec=grid_spec, out_shape=[sds] * 4,
                          compiler_params=_cparams(("arbitrary", "arbitrary")),
                          name=name)(me_index, *slots, *owns, w, m, v)


_SHARDED = dict(xa_wq=1, xa_wk=1, xa_wv=1, xa_wo=1, ffn_w_up=2, ffn_conv_w=2, ffn_w_down=1, mix_w_in=2, mla_w_uq=2,
                mla_w_ukv=2, mix_w_out=1, s5_d=1, s5_w_glu_a=1, s5_w_glu_b=1)
_EXACT = ("ffn_conv_w", "s5_d")
_WEIGHTS = ("norm_mix", "norm_xa", "norm_mem", "norm_ffn", "xa_wq", "xa_wk", "xa_wv", "xa_wo", "xa_q_norm",
            "xa_k_norm", "ffn_w_up", "ffn_conv_w", "ffn_conv_b", "ffn_w_down", "hg_lb_logits", "mix_w_in",
            "hg_out_norm", "mla_q_a_norm", "mla_w_uq", "mla_kv_a_norm", "mla_w_ukv", "mla_qn_nope", "mla_qn_rope",
            "mla_kn_nope", "mla_kn_rope", "mix_w_out", "s5_lam_re", "s5_lam_im", "s5_log_dt", "s5_b_re", "s5_b_im",
            "s5_c_re", "s5_c_im", "s5_d", "s5_w_glu_a", "s5_w_glu_b")
_BIG = tuple(n for n in _WEIGHTS if n in _SHARDED and n not in _EXACT)
_SHARD_ORDER = tuple(n for n in _WEIGHTS if n in _SHARDED)
_REPL_ORDER = tuple(n for n in _WEIGHTS if n not in _SHARDED)
_REPL_EARLY = tuple(n for n in _REPL_ORDER if n.startswith("s5_"))
_REPL_LATE = tuple(n for n in _REPL_ORDER if n not in _REPL_EARLY)


def _pack(parts, dtype, lead=None):
    nl = 0 if lead is None else 1
    flat = [a.astype(dtype).reshape(a.shape[:nl] + (-1,)) for a in parts]
    cat = jnp.concatenate(flat, axis=nl)
    n = cat.shape[nl]
    unit = _LANES * _ROW_PAD
    total = -(-n // unit) * unit
    cat = jnp.pad(cat, [(0, 0)] * nl + [(0, total - n)])
    return cat.reshape(cat.shape[:nl] + (total // _LANES, _LANES))


def _unpack(packed, shapes, lead=None):
    nl = 0 if lead is None else 1
    flat = packed.reshape(packed.shape[:nl] + (-1,))
    out, off = [], 0
    for s in shapes:
        n = int(np.prod(s))
        piece = flat[..., off:off + n] if nl else flat[off:off + n]
        out.append(piece.reshape(packed.shape[:nl] + tuple(s)))
        off += n
    return out


def _to_full(gathered, axis):
    g = jnp.moveaxis(gathered, 0, axis)
    s = g.shape
    return g.reshape(s[:axis] + (s[axis] * s[axis + 1],) + s[axis + 2:])


def _to_shards(full, axis):
    s = full.shape
    g = full.reshape(s[:axis] + (N_DEV, s[axis] // N_DEV) + s[axis + 1:])
    return jnp.moveaxis(g, axis, 0)


_DIRECT_ROWS = ("xa_wq", "xa_wk", "xa_wv", "xa_wo", "mix_w_out", "s5_w_glu_a", "s5_w_glu_b")
_SMALL16 = ("mix_w_in", "mla_w_uq", "mla_w_ukv")
_SMALL_SHARDED = ("mla_w_uq", "mla_w_ukv") + _EXACT
_SHARD_ROWS = 128


def _exchange_layout(d):
    out = dict(d)
    out["ffn_w_up"] = _pad_groups(d["ffn_w_up"], 2)
    out["ffn_conv_w"] = _pad_groups(d["ffn_conv_w"], 2)
    out["ffn_w_down"] = _pad_groups(d["ffn_w_down"], 1)
    return out


def _train_step(x, mem, positions, target, w, m, v):
    d_model = x.shape[1]
    we_, me_, ve_ = _exchange_layout(w), _exchange_layout(m), _exchange_layout(v)
    sds = jax.ShapeDtypeStruct

    matrices = _DIRECT_ROWS + ("ffn_w_up", "ffn_w_down")
    layer_mats = ("xa_wq", "xa_wk", "xa_wv", "xa_wo", "ffn_w_up", "ffn_w_down")
    shard16 = {n: we_[n].astype(bf16) for n in matrices}
    part_of = {n: _rows_of(_SHARD_ROWS) for n in _DIRECT_ROWS}
    part_of["ffn_w_up"] = _cols_of(we_["ffn_w_up"].shape[2])
    part_of["ffn_w_down"] = _rows_of(we_["ffn_w_down"].shape[1])
    part_shape = {n: we_[n].shape[1:] for n in matrices}

    def full_shape(n):
        r, c = part_shape[n]
        return (r, N_DEV * c) if n == "ffn_w_up" else (N_DEV * r, c)

    def gather(ex, n, layer):
        ex.add(shard16[n], sds(full_shape(n), bf16), _at_layer(layer), part_of[n], (n, layer))

    def scatter(ex, n, layer, grad):
        ex.add(grad, sds((N_DEV,) + part_shape[n], f32), part_of[n], _slot, (n, layer))

    small16 = _pack([we_[n] for n in _SMALL16], bf16)
    exact = _pack([we_[n] for n in _EXACT], f32)
    ga, ga1, gb, gc = _Exchange("gather_a"), _Exchange("gather_a1"), _Exchange("gather_b"), _Exchange("gather_c")
    ga.add(small16, sds((N_DEV,) + small16.shape, bf16), _whole, _slot, "small16")
    ga1.add(exact, sds((N_DEV,) + exact.shape, f32), _whole, _slot, "exact")
    gather(ga1, "mix_w_out", 0)
    for n in layer_mats:
        gather(gb, n, 0)
    gather(gc, "s5_w_glu_a", 0)
    gather(gc, "s5_w_glu_b", 0)
    for n in layer_mats:
        gather(gc, n, 1)
    state_a, state_a1, state_b, state_c = ga.begin(True), ga1.begin(True), gb.begin(True), gc.begin(True)

    p = {n: w[n] for n in _REPL_ORDER}
    cos, sin = _rope_tables(positions)
    wo = _odd_weights(p, 0, 1, bf16)
    conv_b = _pad_groups(w["ffn_conv_b"], 1)
    prepared = [cos, sin, conv_b, wo["s5_c3"], wo["s5_bt_re"], wo["s5_bt_im"]]
    full = ga.finish(state_a, [ga1.token, gb.token, gc.token] + prepared)
    for n, a in zip(_SMALL16, _unpack(full["small16"], [we_[n].shape for n in _SMALL16], lead=True)):
        p[n] = _to_full(a, _SHARDED[n])
    we = _even_weights(p, 0, 0, bf16)
    we["norm_mix"] = _after(we["norm_mix"], ga.token, ga1.token, gb.token, gc.token, name="after_gather_starts")

    def late_mix_w_out(mixin):
        full.update(ga1.finish(state_a1, [mixin]))
        return full[("mix_w_out", 0)]

    we["mix_w_out"] = late_mix_w_out
    h, s_mix0, hx = _mixer_fwd(x, cos, sin, we, "l0", w["norm_xa"][0][None])
    conv_w, s5_d = [_to_full(a, _SHARDED[n]) for n, a in
                    zip(_EXACT, _unpack(full["exact"], [we_[n].shape for n in _EXACT], lead=True))]

    def layer_weights(layer):
        return dict(norm_xa=w["norm_xa"][layer][None], norm_mem=w["norm_mem"][layer][None],
                    norm_ffn=w["norm_ffn"][layer][None], xa_q_norm=w["xa_q_norm"][layer][None],
                    xa_k_norm=w["xa_k_norm"][layer][None], ffn_conv_w=conv_w[layer],
                    ffn_conv_b=conv_b[layer][None], **{n: full[(n, layer)] for n in layer_mats})

    full.update(gb.finish(state_b, h))
    wl = [layer_weights(0)]
    h, s_xa0, hf = _xattn_fwd(h, mem, wl[0], "l0", hx)
    h, s_ff0 = _ffn_fwd(h, wl[0], "l0", hf)
    full.update(gc.finish(state_c, h))
    wl.append(layer_weights(1))
    wo.update(s5_d=s5_d, s5_w_glu_a=full[("s5_w_glu_a", 0)], s5_w_glu_b=full[("s5_w_glu_b", 0)])
    h, s_mix1, hx = _s5_fwd(h, wo, "l1", wl[1]["norm_xa"])
    h, s_xa1, hf = _xattn_fwd(h, mem, wl[1], "l1", hx)
    h, s_ff1 = _ffn_fwd(h, wl[1], "l1", hf)
    dh, loss = _rows(_loss_fn, [h, target], [], [(h.shape[1], f32)], accs=[(1, 1)], name="loss_head")

    gl = [{}, {}]
    reduces = []

    own_grad = {}

    def reduce_start(name, entries, dh):
        ex = _Exchange(name)
        for n, layer, grad in entries.get("matrices", ()):
            scatter(ex, n, layer, grad)
            own_grad[(n, layer)] = grad
        for key, src, shape, src_at in entries.get("packs", ()):
            ex.add(src, shape, src_at, _slot, key)
        reduces.append((ex, ex.begin(False)))
        return _after(dh, ex.token, name="after_" + name)

    dh, gl[1] = _ffn_bwd(dh, wl[1], s_ff1, "l1")
    dh = reduce_start("reduce_ffn1", dict(matrices=[(n, 1, gl[1][n]) for n in ("ffn_w_up", "ffn_w_down")]), dh)
    dh, g = _xattn_bwd(dh, mem, wl[1], s_xa1, "l1")
    gl[1].update(g)
    dh = reduce_start("reduce_xa1", dict(matrices=[(n, 1, g[n]) for n in ("xa_wq", "xa_wk", "xa_wv", "xa_wo")]), dh)
    dh, g_odd = _s5_bwd(dh, wo, s_mix1, "l1")
    go = _odd_grads(g_odd)
    dh, gl[0] = _ffn_bwd(dh, wl[0], s_ff0, "l0")
    send_early = _pack([go[n].reshape(w[n].shape) for n in _REPL_EARLY], f32)
    dh = reduce_start("reduce_ffn0", dict(
        matrices=[(n, 0, g_odd[n]) for n in ("s5_w_glu_a", "s5_w_glu_b")]
        + [(n, 0, gl[0][n]) for n in ("ffn_w_up", "ffn_w_down")],
        packs=[("repl_early", send_early, sds((N_DEV,) + send_early.shape, f32), _whole)]), dh)
    dh, g = _xattn_bwd(dh, mem, wl[0], s_xa0, "l0")
    gl[0].update(g)
    dh = reduce_start("reduce_xa0", dict(matrices=[(n, 0, g[n]) for n in ("xa_wq", "xa_wk", "xa_wv", "xa_wo")]), dh)
    grad_x, g_even = _mixer_bwd(
        dh, cos, sin, we, s_mix0, "l0",
        on_w_out=lambda grad, dmixin: reduce_start("reduce_w_out", dict(matrices=[("mix_w_out", 0, grad)]), dmixin))

    ge = _even_grads(g_even)
    cat = lambda n: jnp.concatenate([gl[0][n], gl[1][n]], axis=0)
    rg = dict(ge)
    rg["norm_mix"] = jnp.concatenate([ge["norm_mix"], go["norm_mix"]], axis=0)
    for n in ("norm_xa", "norm_mem", "norm_ffn", "xa_q_norm", "xa_k_norm"):
        rg[n] = cat(n)
    rg["ffn_conv_b"] = _unpad_groups(cat("ffn_conv_b"), 1)
    sg = dict(mla_w_uq=ge["mla_w_uq"], mla_w_ukv=ge["mla_w_ukv"], s5_d=go["s5_d"],
              ffn_conv_w=jnp.stack([gl[0]["ffn_conv_w"], gl[1]["ffn_conv_w"]]))
    send_small = _pack([_to_shards(sg[n], _SHARDED[n]) for n in _SMALL_SHARDED], f32, lead=True)
    send_late = _pack([rg[n].reshape(w[n].shape) for n in _REPL_LATE], f32)
    w_in_rows = w["mix_w_in"].shape[2]
    last = _Exchange("reduce_last")
    last.add(g_even["mix_w_in_t"], sds((N_DEV, w_in_rows, d_model), f32), _rows_of(w_in_rows), _slot, "mix_w_in")
    last.add(send_small, sds(send_small.shape, f32), _slot, _slot, "small")
    last.add(send_late, sds((N_DEV,) + send_late.shape, f32), _whole, _slot, "repl_late")
    state_last = last.begin(False)
    slots = {}
    for ex, state in reduces:
        slots.update(ex.finish(state, [grad_x, last.token]))

    me_index = (4 * lax.axis_index("x") + 2 * lax.axis_index("y") + lax.axis_index("c")).astype(jnp.int32).reshape(1)

    def rows_block(r, c):
        return lambda tr: ((tr, c), lambda i, me: (me[0] * (r // tr) + i, 0))

    def own_block(n):
        r, c = part_shape[n]
        if n == "ffn_w_up":
            return lambda tr: ((tr, c), lambda i, me: (i, me[0]))
        return rows_block(r, c)

    out = [{}, {}, {}, {}]
    unpad = dict(ffn_w_up=2, ffn_conv_w=2, ffn_w_down=1)
    for n in matrices:
        layers = range(we_[n].shape[0])
        res = _sum_adam(me_index, [slots[(n, layer)] for layer in layers], [own_grad[(n, layer)] for layer in layers],
                        own_block(n), we_[n], me_[n], ve_[n], name=f"adam_{n}")
        for k in range(4):
            out[k][n] = _unpad_groups(res[k], unpad[n]) if n in unpad else res[k]
    pk = lambda d, order: _pack([d[n] for n in order], f32)[None]
    whole_rows = lambda tr: ((tr, _LANES), lambda i, me: (i, 0))
    res_early = _sum_adam(me_index, [slots["repl_early"]], [send_early], whole_rows, pk(w, _REPL_EARLY),
                          pk(m, _REPL_EARLY), pk(v, _REPL_EARLY), name="adam_repl_early")
    for k in range(4):
        out[k].update(zip(_REPL_EARLY, _unpack(res_early[k][0], [w[n].shape for n in _REPL_EARLY])))
    done = [out[k][n] for k in range(4) for n in matrices + _REPL_EARLY]
    slots = last.finish(state_last, done)
    transposed = lambda d: jnp.swapaxes(d["mix_w_in"], 1, 2)
    res_w_in = _sum_adam(me_index, [slots["mix_w_in"]], [g_even["mix_w_in_t"]], rows_block(w_in_rows, d_model),
                         transposed(w), transposed(m), transposed(v), name="adam_mix_w_in")
    res_small = _sum_adam(me_index, [slots["small"]], [send_small],
                          lambda tr: ((1, tr, _LANES), lambda i, me: (me[0], i, 0)),
                          pk(we_, _SMALL_SHARDED), pk(me_, _SMALL_SHARDED), pk(ve_, _SMALL_SHARDED), name="adam_small")
    res_late = _sum_adam(me_index, [slots["repl_late"]], [send_late], whole_rows, pk(w, _REPL_LATE), pk(m, _REPL_LATE),
                         pk(v, _REPL_LATE), name="adam_repl_late")
    for k in range(4):
        out[k]["mix_w_in"] = jnp.swapaxes(res_w_in[k], 1, 2)
        for n, a in zip(_SMALL_SHARDED, _unpack(res_small[k][0], [we_[n].shape for n in _SMALL_SHARDED])):
            out[k][n] = _unpad_groups(a, unpad[n]) if n in unpad else a
        out[k].update(zip(_REPL_LATE, _unpack(res_late[k][0], [w[n].shape for n in _REPL_LATE])))
    return loss, grad_x, out


_INPUTS = tuple("""x, mem, positions, norm_mix, norm_xa, norm_mem, norm_ffn, xa_wq, xa_wk, xa_wv, xa_wo, xa_q_norm, xa_k_norm, ffn_w_up, ffn_conv_w, ffn_conv_b, ffn_w_down, hg_lb_logits, mix_w_in, hg_out_norm, mla_q_a_norm, mla_w_uq, mla_kv_a_norm, mla_w_ukv, mla_qn_nope, mla_qn_rope, mla_kn_nope, mla_kn_rope, mix_w_out, s5_lam_re, s5_lam_im, s5_log_dt, s5_b_re, s5_b_im, s5_c_re, s5_c_im, s5_d, s5_w_glu_a, s5_w_glu_b, loss_target, m_norm_mix, m_norm_xa, m_norm_mem, m_norm_ffn, m_xa_wq, m_xa_wk, m_xa_wv, m_xa_wo, m_xa_q_norm, m_xa_k_norm, m_ffn_w_up, m_ffn_conv_w, m_ffn_conv_b, m_ffn_w_down, m_hg_lb_logits, m_mix_w_in, m_hg_out_norm, m_mla_q_a_norm, m_mla_w_uq, m_mla_kv_a_norm, m_mla_w_ukv, m_mla_qn_nope, m_mla_qn_rope, m_mla_kn_nope, m_mla_kn_rope, m_mix_w_out, m_s5_lam_re, m_s5_lam_im, m_s5_log_dt, m_s5_b_re, m_s5_b_im, m_s5_c_re, m_s5_c_im, m_s5_d, m_s5_w_glu_a, m_s5_w_glu_b, v_norm_mix, v_norm_xa, v_norm_mem, v_norm_ffn, v_xa_wq, v_xa_wk, v_xa_wv, v_xa_wo, v_xa_q_norm, v_xa_k_norm, v_ffn_w_up, v_ffn_conv_w, v_ffn_conv_b, v_ffn_w_down, v_hg_lb_logits, v_mix_w_in, v_hg_out_norm, v_mla_q_a_norm, v_mla_w_uq, v_mla_kv_a_norm, v_mla_w_ukv, v_mla_qn_nope, v_mla_qn_rope, v_mla_kn_nope, v_mla_kn_rope, v_mix_w_out, v_s5_lam_re, v_s5_lam_im, v_s5_log_dt, v_s5_b_re, v_s5_b_im, v_s5_c_re, v_s5_c_im, v_s5_d, v_s5_w_glu_a, v_s5_w_glu_b""".replace(" ", "").split(","))


def kernel(x, mem, positions, norm_mix, norm_xa, norm_mem, norm_ffn, xa_wq, xa_wk, xa_wv, xa_wo, xa_q_norm, xa_k_norm, ffn_w_up, ffn_conv_w, ffn_conv_b, ffn_w_down, hg_lb_logits, mix_w_in, hg_out_norm, mla_q_a_norm, mla_w_uq, mla_kv_a_norm, mla_w_ukv, mla_qn_nope, mla_qn_rope, mla_kn_nope, mla_kn_rope, mix_w_out, s5_lam_re, s5_lam_im, s5_log_dt, s5_b_re, s5_b_im, s5_c_re, s5_c_im, s5_d, s5_w_glu_a, s5_w_glu_b, loss_target, m_norm_mix, m_norm_xa, m_norm_mem, m_norm_ffn, m_xa_wq, m_xa_wk, m_xa_wv, m_xa_wo, m_xa_q_norm, m_xa_k_norm, m_ffn_w_up, m_ffn_conv_w, m_ffn_conv_b, m_ffn_w_down, m_hg_lb_logits, m_mix_w_in, m_hg_out_norm, m_mla_q_a_norm, m_mla_w_uq, m_mla_kv_a_norm, m_mla_w_ukv, m_mla_qn_nope, m_mla_qn_rope, m_mla_kn_nope, m_mla_kn_rope, m_mix_w_out, m_s5_lam_re, m_s5_lam_im, m_s5_log_dt, m_s5_b_re, m_s5_b_im, m_s5_c_re, m_s5_c_im, m_s5_d, m_s5_w_glu_a, m_s5_w_glu_b, v_norm_mix, v_norm_xa, v_norm_mem, v_norm_ffn, v_xa_wq, v_xa_wk, v_xa_wv, v_xa_wo, v_xa_q_norm, v_xa_k_norm, v_ffn_w_up, v_ffn_conv_w, v_ffn_conv_b, v_ffn_w_down, v_hg_lb_logits, v_mix_w_in, v_hg_out_norm, v_mla_q_a_norm, v_mla_w_uq, v_mla_kv_a_norm, v_mla_w_ukv, v_mla_qn_nope, v_mla_qn_rope, v_mla_kn_nope, v_mla_kn_rope, v_mix_w_out, v_s5_lam_re, v_s5_lam_im, v_s5_log_dt, v_s5_b_re, v_s5_b_im, v_s5_c_re, v_s5_c_im, v_s5_d, v_s5_w_glu_a, v_s5_w_glu_b):
    vals = dict(zip(_INPUTS, (x, mem, positions, norm_mix, norm_xa, norm_mem, norm_ffn, xa_wq, xa_wk, xa_wv, xa_wo, xa_q_norm, xa_k_norm, ffn_w_up, ffn_conv_w, ffn_conv_b, ffn_w_down, hg_lb_logits, mix_w_in, hg_out_norm, mla_q_a_norm, mla_w_uq, mla_kv_a_norm, mla_w_ukv, mla_qn_nope, mla_qn_rope, mla_kn_nope, mla_kn_rope, mix_w_out, s5_lam_re, s5_lam_im, s5_log_dt, s5_b_re, s5_b_im, s5_c_re, s5_c_im, s5_d, s5_w_glu_a, s5_w_glu_b, loss_target, m_norm_mix, m_norm_xa, m_norm_mem, m_norm_ffn, m_xa_wq, m_xa_wk, m_xa_wv, m_xa_wo, m_xa_q_norm, m_xa_k_norm, m_ffn_w_up, m_ffn_conv_w, m_ffn_conv_b, m_ffn_w_down, m_hg_lb_logits, m_mix_w_in, m_hg_out_norm, m_mla_q_a_norm, m_mla_w_uq, m_mla_kv_a_norm, m_mla_w_ukv, m_mla_qn_nope, m_mla_qn_rope, m_mla_kn_nope, m_mla_kn_rope, m_mix_w_out, m_s5_lam_re, m_s5_lam_im, m_s5_log_dt, m_s5_b_re, m_s5_b_im, m_s5_c_re, m_s5_c_im, m_s5_d, m_s5_w_glu_a, m_s5_w_glu_b, v_norm_mix, v_norm_xa, v_norm_mem, v_norm_ffn, v_xa_wq, v_xa_wk, v_xa_wv, v_xa_wo, v_xa_q_norm, v_xa_k_norm, v_ffn_w_up, v_ffn_conv_w, v_ffn_conv_b, v_ffn_w_down, v_hg_lb_logits, v_mix_w_in, v_hg_out_norm, v_mla_q_a_norm, v_mla_w_uq, v_mla_kv_a_norm, v_mla_w_ukv, v_mla_qn_nope, v_mla_qn_rope, v_mla_kn_nope, v_mla_kn_rope, v_mix_w_out, v_s5_lam_re, v_s5_lam_im, v_s5_log_dt, v_s5_b_re, v_s5_b_im, v_s5_c_re, v_s5_c_im, v_s5_d, v_s5_w_glu_a, v_s5_w_glu_b)))
    w = {n: vals[n] for n in _WEIGHTS}
    m = {n: vals["m_" + n] for n in _WEIGHTS}
    v = {n: vals["v_" + n] for n in _WEIGHTS}
    loss, grad_x, res = _train_step(vals["x"][0], vals["mem"][0], vals["positions"][0], vals["loss_target"][0],
                                    w, m, v)
    loss = lax.psum(loss[0, 0], ("x", "y", "c"))
    return (loss, grad_x[None], *[r[n] for r in res for n in _WEIGHTS])
```

```python
import functools

import jax
import jax.numpy as jnp
import numpy as np
from jax import lax
from jax.experimental import pallas as pl
from jax.experimental.pallas import tpu as pltpu

f32 = jnp.float32
bf16 = jnp.bfloat16

EPS = 1e-6
N_DEV = 8
VMEM_LIMIT = 52 * 1024 * 1024

HG_HEADS = 4
HG_DIM = 128
HG_WIDTH = HG_HEADS * HG_DIM
HG_CHUNK = 64
HG_SUB = 16
MLA_HEADS = 4
MLA_Q_RANK = 256
MLA_KV_RANK = 128
MLA_NOPE = 128
MLA_ROPE = 64
MLA_V = 128
MLA_QK = MLA_NOPE + MLA_ROPE
MLA_QK_PAD = 256
ROPE_BASE = 10000.0
IN_WIDTH = 4 * HG_WIDTH + MLA_Q_RANK + MLA_KV_RANK + MLA_ROPE
IN_PAD = 2560
XA_HEADS = 4
XA_DIM = 256
S5_GROUP = 16
S5_GROUPS = 64
S5_STATE = 64
CONV_W = 3

ADAM_LR = 0.001
ADAM_B1 = 0.9
ADAM_B2 = 0.999
ADAM_EPS = 1e-08
ADAM_WD = 0.01
ADAM_STEP = 10

_NT = (((1,), (1,)), ((), ()))
_TN = (((0,), (0,)), ((), ()))
_NN = (((1,), (0,)), ((), ()))


def _pick(n, cands):
    for c in cands:
        if n % c == 0:
            return c
    return n


def _cparams(sem):
    return pltpu.CompilerParams(dimension_semantics=sem, vmem_limit_bytes=VMEM_LIMIT)


_MM_BUDGET = 36 * 1024 * 1024
_MM_TILES = ((1024, 1024), (1024, 512), (512, 1024), (512, 512), (512, 256), (256, 512), (256, 256), (256, 128),
             (128, 256), (128, 128))


def _mm(a, b, *, name, ta=False, tb=False, out_dtype=f32, add=None, b2=None, kslab=None, norm=None, rms_bwd=None):
    m, k = (a.shape[1], a.shape[0]) if ta else a.shape
    nb = b.shape[0] if tb else b.shape[1]
    n = nb * (2 if b2 is not None else 1)
    slab, nslab = kslab if kslab is not None else (0, 1)
    assert (b.shape[1] // nslab if tb else b.shape[0]) == k, (a.shape, b.shape, ta, tb)
    assert b2 is None or (not tb and b2.shape == b.shape)
    full_rows = norm is not None or rms_bwd is not None
    assert not (full_rows and b2 is not None) and not (norm is not None and rms_bwd is not None)
    isz = lambda x: jnp.dtype(x.dtype).itemsize
    bm = bn = None
    for cm, cn in _MM_TILES:
        if m % cm or nb % cn or (full_rows and cn != n):
            continue
        need = 2 * (cm * k * isz(a) + cn * k * isz(b) * (2 if b2 is not None else 1)
                    + cm * cn * (jnp.dtype(out_dtype).itemsize + (4 if add is not None else 0)
                                 + (2 if norm is not None else 0) + (8 if rms_bwd is not None else 0)))
        if need <= _MM_BUDGET:
            bm, bn = cm, cn
            break
    assert bm is not None, (name, a.shape, b.shape)
    half = nb // bn
    dims = (((0 if ta else 1,), (1 if tb else 0,)), ((), ()))

    def body(*refs):
        refs = list(refs)
        a_ref, b_ref = refs[0], refs[1]
        b2_ref = refs.pop(2) if b2 is not None else None
        add_ref = refs.pop(2) if add is not None else None
        gain_ref = refs.pop(2) if norm is not None else None
        x_ref, xgain_ref, through_ref = (refs.pop(2), refs.pop(2), refs.pop(2)) if rms_bwd is not None else (None,) * 3
        o_ref = refs[2]
        extra_ref = refs[3] if (norm is not None or rms_bwd is not None) else None

        def run(rhs_ref):
            r = lax.dot_general(a_ref[...].astype(bf16), rhs_ref[...].astype(bf16), dims, preferred_element_type=f32)
            if add_ref is not None:
                r = r + add_ref[...].astype(f32)
            if rms_bwd is None:
                o_ref[...] = r.astype(o_ref.dtype)
            else:
                _, vjp = jax.vjp(_rms, x_ref[...], xgain_ref[...])
                dx, dgain = vjp(r)
                o_ref[...] = (through_ref[...] + dx).astype(o_ref.dtype)

                @pl.when(pl.program_id(0) == 0)
                def _():
                    extra_ref[...] = jnp.zeros_like(extra_ref)
                extra_ref[...] += dgain
            if norm is not None:
                extra_ref[...] = _rms(r, gain_ref[...]).astype(extra_ref.dtype)

        if b2_ref is None:
            run(b_ref)
        else:
            pl.when(pl.program_id(1) < half)(lambda: run(b_ref))
            pl.when(pl.program_id(1) >= half)(lambda: run(b2_ref))

    a_spec = pl.BlockSpec((k, bm), lambda i, j: (0, i)) if ta else pl.BlockSpec((bm, k), lambda i, j: (i, 0))
    if tb:
        b_spec = pl.BlockSpec((bn, k), lambda i, j: (j, slab))
    elif b2 is None:
        b_spec = pl.BlockSpec((k, bn), lambda i, j: (0, j))
    else:
        b_spec = pl.BlockSpec((k, bn), lambda i, j: (0, jnp.minimum(j, half - 1)))
    in_specs = [a_spec, b_spec]
    args = [a, b]
    if b2 is not None:
        in_specs.append(pl.BlockSpec((k, bn), lambda i, j: (0, jnp.maximum(j - half, 0))))
        args.append(b2)
    if add is not None:
        in_specs.append(pl.BlockSpec((bm, bn), lambda i, j: (i, j)))
        args.append(add)
    out_blk = pl.BlockSpec((bm, bn), lambda i, j: (i, j))
    out_specs, out_shape = out_blk, jax.ShapeDtypeStruct((m, n), out_dtype)
    row_blk = pl.BlockSpec((1, bn), lambda i, j: (0, j))
    if norm is not None:
        in_specs.append(row_blk)
        args.append(norm)
        out_specs, out_shape = [out_blk, out_blk], [out_shape, jax.ShapeDtypeStruct((m, n), bf16)]
    if rms_bwd is not None:
        in_specs += [out_blk, row_blk, out_blk]
        args += list(rms_bwd)
        out_specs, out_shape = [out_blk, row_blk], [out_shape, jax.ShapeDtypeStruct((1, n), f32)]
    return pl.pallas_call(
        body, grid=(m // bm, n // bn), in_specs=in_specs, out_specs=out_specs, out_shape=out_shape,
        compiler_params=_cparams(("arbitrary" if rms_bwd is not None else "parallel", "parallel")),
        name=name)(*args)


def _as_tuple(x):
    return tuple(x) if isinstance(x, (tuple, list)) else (x,)


def _full_spec(p):
    nd = p.ndim
    return pl.BlockSpec(p.shape, lambda i, _nd=nd: (0,) * _nd)


def _window(a, start, width):
    assert start % width == 0 and width % 128 == 0
    return (a, start // width, width)


def _row_array(x):
    return x[0] if isinstance(x, tuple) else x


def _row_shape(x):
    return (x[0].shape[0], x[2]) if isinstance(x, tuple) else x.shape


def _row_spec(x, tile):
    if isinstance(x, tuple):
        return pl.BlockSpec((tile, x[2]), lambda i, _b=x[1]: (i, _b))
    return pl.BlockSpec((tile, x.shape[1]), lambda i: (i, 0))


def _rows(fn, rows, params, outs, *, name, tile=256, accs=()):
    length = _row_shape(rows[0])[0]
    tile = min(tile, length)
    nr, npar, no = len(rows), len(params), len(outs)

    def body(*refs):
        r, p, o = refs[:nr], refs[nr:nr + npar], refs[nr + npar:]
        res = _as_tuple(fn(*[x[...].astype(f32) for x in r], *[x[...] for x in p]))
        for kk in range(no):
            o[kk][...] = res[kk].astype(o[kk].dtype)
        if accs:
            @pl.when(pl.program_id(0) == 0)
            def _():
                for kk in range(no, no + len(accs)):
                    o[kk][...] = jnp.zeros_like(o[kk])
            for kk in range(no, no + len(accs)):
                o[kk][...] += res[kk]

    in_specs = [_row_spec(x, tile) for x in rows] + [_full_spec(p) for p in params]
    out_specs = [pl.BlockSpec((tile, w), lambda i: (i, 0)) for w, _ in outs]
    out_shape = [jax.ShapeDtypeStruct((length, w), d) for w, d in outs]
    for s in accs:
        out_specs.append(pl.BlockSpec(s, lambda i, _nd=len(s): (0,) * _nd))
        out_shape.append(jax.ShapeDtypeStruct(s, f32))
    res = pl.pallas_call(body, grid=(length // tile,), in_specs=in_specs, out_specs=out_specs, out_shape=out_shape,
                         compiler_params=_cparams(("arbitrary",)), name=name)(*[_row_array(x) for x in rows], *params)
    return res


def _rows_bwd(fn, rows, params, cts, *, name, rgrad, pgrad, tile=256, addends=None):
    addends = {i: (a if isinstance(a, list) else [(a, 0)]) for i, a in (addends or {}).items()}
    length = _row_shape(rows[0])[0]
    tile = min(tile, length)
    nr, npar, nc = len(rows), len(params), len(cts)
    ridx = [i for i in range(nr) if rgrad[i] is not None]
    pidx = [i for i in range(npar) if pgrad[i]]
    flat_addends = [(i, a, off) for i in sorted(addends) for a, off in addends[i]]
    na = len(flat_addends)

    def body(*refs):
        r, p, c = refs[:nr], refs[nr:nr + npar], refs[nr + npar:nr + npar + nc]
        ad = refs[nr + npar + nc:nr + npar + nc + na]
        o = refs[nr + npar + nc + na:]
        rv = [x[...].astype(f32) for x in r]
        pv = [x[...] for x in p]
        cv = tuple(x[...].astype(f32) for x in c)

        def g(*d):
            rr, pp = list(rv), list(pv)
            for n_, i_ in enumerate(ridx):
                rr[i_] = d[n_]
            for n_, i_ in enumerate(pidx):
                pp[i_] = d[len(ridx) + n_]
            return _as_tuple(fn(*rr, *pp))

        _, vjp = jax.vjp(g, *[rv[i] for i in ridx], *[pv[i] for i in pidx])
        grads = vjp(cv)
        for n_, i_ in enumerate(ridx):
            val = grads[n_]
            for k_, (j_, a_, off) in enumerate(flat_addends):
                if j_ == i_:
                    extra = ad[k_][...].astype(f32)
                    if extra.shape[1] != val.shape[1]:
                        extra = jnp.pad(extra, ((0, 0), (off, val.shape[1] - off - extra.shape[1])))
                    val = val + extra
            o[n_][...] = val.astype(o[n_].dtype)
        if pidx:
            @pl.when(pl.program_id(0) == 0)
            def _():
                for n_ in range(len(pidx)):
                    o[len(ridx) + n_][...] = jnp.zeros_like(o[len(ridx) + n_])
            for n_ in range(len(pidx)):
                o[len(ridx) + n_][...] += grads[len(ridx) + n_]

    plain = lambda shape: pl.BlockSpec((tile, shape[1]), lambda i: (i, 0))
    in_specs = ([_row_spec(x, tile) for x in rows] + [_full_spec(p) for p in params] + [plain(x.shape) for x in cts]
                + [plain(a.shape) for _, a, _ in flat_addends])
    out_specs = [plain(_row_shape(rows[i])) for i in ridx] + [_full_spec(params[i]) for i in pidx]
    out_shape = ([jax.ShapeDtypeStruct(_row_shape(rows[i]), rgrad[i]) for i in ridx]
                 + [jax.ShapeDtypeStruct(params[i].shape, f32) for i in pidx])
    res = pl.pallas_call(body, grid=(length // tile,), in_specs=in_specs, out_specs=out_specs, out_shape=out_shape,
                         compiler_params=_cparams(("arbitrary",)), name=name)(
        *[_row_array(x) for x in rows], *params, *cts, *[a for _, a, _ in flat_addends])
    return list(res[:len(ridx)]), list(res[len(ridx):])


def _rms(x, g):
    return x * lax.rsqrt(jnp.mean(x * x, axis=-1, keepdims=True) + EPS) * g


def _rms_twice(x, g):
    y = _rms(x, g)
    return y, y


def _silu(x):
    return x * jax.nn.sigmoid(x)


_UP_COLS = 512


def _ffn_up_conv(hf, w_up, cw, cb, *, name):
    length, d = hf.shape
    ff = w_up.shape[1] // 2
    bm = _pick(length, (1024, 512, 256))
    bn = _UP_COLS
    nj = ff // bn

    def body(hf_ref, wg_ref, wv_ref, cwg, cwv, cbg, cbv, ug_ref, uv_ref, a_ref, halo_g, halo_v):
        i, j = pl.program_id(0), pl.program_id(1)
        x = hf_ref[...]
        rows = lax.broadcasted_iota(jnp.int32, (bm, bn), 0)
        pad = jnp.zeros((bm - 8, bn), f32)

        def half(w_ref, cw_ref, cb_ref, u_ref, halo):
            u = jnp.dot(x, w_ref[...], preferred_element_type=f32).astype(bf16)
            u_ref[...] = u
            u = u.astype(f32)
            prev = jnp.where(i == 0, 0.0, halo[j])
            x1 = jnp.where(rows >= 1, pltpu.roll(u, 1, axis=0), jnp.concatenate([pltpu.roll(prev, 1, axis=0), pad]))
            x2 = jnp.where(rows >= 2, pltpu.roll(u, 2, axis=0), jnp.concatenate([pltpu.roll(prev, 2, axis=0), pad]))
            halo[j] = u[bm - 8:, :]
            return cw_ref[2:3, :] * u + cw_ref[1:2, :] * x1 + cw_ref[0:1, :] * x2 + cb_ref[...]

        g = half(wg_ref, cwg, cbg, ug_ref, halo_g)
        v = half(wv_ref, cwv, cbv, uv_ref, halo_v)
        a_ref[...] = (_silu(g) * v).astype(a_ref.dtype)

    col = lambda r, off: pl.BlockSpec((r, bn), lambda i, j, _o=off: (0, j + _o))
    out_blk = pl.BlockSpec((bm, bn), lambda i, j: (i, j))
    sds = jax.ShapeDtypeStruct((length, ff), bf16)
    return pl.pallas_call(
        body, grid=(length // bm, nj),
        in_specs=[pl.BlockSpec((bm, d), lambda i, j: (i, 0)), col(d, 0), col(d, nj), col(CONV_W, 0), col(CONV_W, nj),
                  col(1, 0), col(1, nj)],
        out_specs=[out_blk] * 3, out_shape=[sds] * 3,
        scratch_shapes=[pltpu.VMEM((nj, 8, bn), f32), pltpu.VMEM((nj, 8, bn), f32)],
        compiler_params=_cparams(("arbitrary", "arbitrary")), name=name)(hf, w_up, w_up, cw, cw, cb, cb)


def _ffn_da_dconv(dout, w_down, u_g, u_v, cw, cb, *, name):
    length, d = dout.shape
    ff = u_g.shape[1]
    bm = _pick(length, (1024, 512, 256))
    bn = _UP_COLS
    nj, ni = ff // bn, length // bm

    def body(dout_ref, wd_ref, ug_ref, uv_ref, pg_ref, pv_ref, cwg, cwv, cbg, cbv,
             dug_ref, duv_ref, sums_g_ref, sums_v_ref, halo_g, halo_v, acc_g, acc_v):
        s, j = pl.program_id(0), pl.program_id(1)
        rows = lax.broadcasted_iota(jnp.int32, (bm, bn), 0)
        row8 = lax.broadcasted_iota(jnp.int32, (8, bn), 0)
        pad = jnp.zeros((bm - 8, bn), f32)
        da = lax.dot_general(dout_ref[...].astype(bf16), wd_ref[...], _NT, preferred_element_type=f32)

        def conv(u_ref, p_ref, cw_ref, cb_ref):
            x = u_ref[...].astype(f32)
            prev = jnp.where(s == ni - 1, 0.0, p_ref[...].astype(f32))
            x1 = jnp.where(rows >= 1, pltpu.roll(x, 1, axis=0), jnp.concatenate([pltpu.roll(prev, 1, axis=0), pad]))
            x2 = jnp.where(rows >= 2, pltpu.roll(x, 2, axis=0), jnp.concatenate([pltpu.roll(prev, 2, axis=0), pad]))
            return cw_ref[2:3, :] * x + cw_ref[1:2, :] * x1 + cw_ref[0:1, :] * x2 + cb_ref[...], x, x1, x2

        g, xg, xg1, xg2 = conv(ug_ref, pg_ref, cwg, cbg)
        v, xv, xv1, xv2 = conv(uv_ref, pv_ref, cwv, cbv)
        sg = jax.nn.sigmoid(g)
        dg = da * v * (sg * (1.0 + g * (1.0 - sg)))
        dv = da * (g * sg)

        def back(dy, x, x1, x2, cw_ref, du_ref, sums_ref, halo, acc):
            nxt = jnp.where(s == 0, 0.0, halo[j])
            up1 = jnp.where(rows < bm - 1, pltpu.roll(dy, bm - 1, axis=0),
                            jnp.concatenate([pad, pltpu.roll(nxt, 7, axis=0)]))
            up2 = jnp.where(rows < bm - 2, pltpu.roll(dy, bm - 2, axis=0),
                            jnp.concatenate([pad, pltpu.roll(nxt, 6, axis=0)]))
            halo[j] = dy[:8, :]
            du_ref[...] = (cw_ref[2:3, :] * dy + cw_ref[1:2, :] * up1 + cw_ref[0:1, :] * up2).astype(du_ref.dtype)
            col = lambda t: jnp.sum(t, axis=0, keepdims=True)
            part = jnp.where(row8 == 0, col(dy * x2), jnp.where(row8 == 1, col(dy * x1), jnp.where(
                row8 == 2, col(dy * x), jnp.where(row8 == 3, col(dy), 0.0))))
            total = jnp.where(s == 0, part, acc[j] + part)
            acc[j] = total
            sums_ref[...] = total

        back(dg, xg, xg1, xg2, cwg, dug_ref, sums_g_ref, halo_g, acc_g)
        back(dv, xv, xv1, xv2, cwv, duv_ref, sums_v_ref, halo_v, acc_v)

    rb = lambda s: ni - 1 - s
    tile = pl.BlockSpec((bm, bn), lambda s, j: (rb(s), j))
    before = pl.BlockSpec((8, bn), lambda s, j: (jnp.maximum(rb(s) * (bm // 8) - 1, 0), j))
    col = lambda r, off: pl.BlockSpec((r, bn), lambda s, j, _o=off: (0, j + _o))
    sds = jax.ShapeDtypeStruct
    dug, duv, sums_g, sums_v = pl.pallas_call(
        body, grid=(ni, nj),
        in_specs=[pl.BlockSpec((bm, d), lambda s, j: (rb(s), 0)), pl.BlockSpec((bn, d), lambda s, j: (j, 0)),
                  tile, tile, before, before, col(CONV_W, 0), col(CONV_W, nj), col(1, 0), col(1, nj)],
        out_specs=[tile, tile] + [pl.BlockSpec((8, bn), lambda s, j: (s, j))] * 2,
        out_shape=[sds((length, ff), bf16), sds((length, ff), bf16), sds((ni * 8, ff), f32), sds((ni * 8, ff), f32)],
        scratch_shapes=[pltpu.VMEM((nj, 8, bn), f32)] * 4,
        compiler_params=_cparams(("arbitrary", "arbitrary")), name=name)(
        dout, w_down, u_g, u_v, u_g, u_v, cw, cw, cb, cb)
    last = (ni - 1) * 8
    both = lambda lo, hi: jnp.concatenate([sums_g[last + lo:last + hi], sums_v[last + lo:last + hi]], axis=1)
    return dug, duv, both(0, CONV_W), both(CONV_W, CONV_W + 1)


def _ffn_fwd(h, w, tag, hf=None):
    if hf is None:
        hf, = _rows(_rms, [h], [w["norm_ffn"]], [(h.shape[1], bf16)], name=f"ffn_norm_{tag}")
    u_g, u_v, a = _ffn_up_conv(hf, w["ffn_w_up"], w["ffn_conv_w"], w["ffn_conv_b"], name=f"ffn_up_{tag}")
    out = _mm(a, w["ffn_w_down"], add=h, name=f"ffn_down_{tag}")
    return out, (h, hf, u_g, u_v, a)


def _ffn_bwd(dout, w, saved, tag):
    h, hf, u_g, u_v, a = saved
    g = {"ffn_w_down": _mm(a, dout, ta=True, out_dtype=bf16, name=f"ffn_dwdown_{tag}")}
    dug, duv, g["ffn_conv_w"], g["ffn_conv_b"] = _ffn_da_dconv(
        dout, w["ffn_w_down"], u_g, u_v, w["ffn_conv_w"], w["ffn_conv_b"], name=f"ffn_dconv_{tag}")
    dhf = _mm(dug, w["ffn_w_up"], tb=True, kslab=(0, 2), name=f"ffn_dhf_g_{tag}")
    dh, g["norm_ffn"] = _mm(duv, w["ffn_w_up"], tb=True, kslab=(1, 2), add=dhf, rms_bwd=(h, w["norm_ffn"], dout),
                            name=f"ffn_dhf_v_{tag}")
    g["ffn_w_up"] = _mm(hf, dug, ta=True, b2=duv, out_dtype=bf16, name=f"ffn_dwup_{tag}")
    return dh, g


def _xattn_fn(qx, kx, vx, qg, kg):
    outs = []
    for hh in range(XA_HEADS):
        sl = slice(hh * XA_DIM, (hh + 1) * XA_DIM)
        q = _rms(qx[:, sl], qg).astype(bf16)
        k = _rms(kx[:, sl], kg).astype(bf16)
        s = lax.dot_general(q, k, _NT, preferred_element_type=f32) * (XA_DIM ** -0.5)
        s = s - jnp.max(s, axis=-1, keepdims=True)
        p = jnp.exp(s)
        p = p / jnp.sum(p, axis=-1, keepdims=True)
        outs.append(jnp.dot(p.astype(bf16), vx[:, sl].astype(bf16), preferred_element_type=f32))
    return jnp.concatenate(outs, axis=-1)


def _xattn_fwd(h, mem, w, tag, hx=None):
    d = h.shape[1]
    if hx is None:
        hx, = _rows(_rms, [h], [w["norm_xa"]], [(d, bf16)], name=f"xa_norm_{tag}")
    qx = _mm(hx, w["xa_wq"], name=f"xa_q_{tag}")
    m, = _rows(_rms, [mem], [w["norm_mem"]], [(d, bf16)], name=f"xa_mnorm_{tag}")
    kx = _mm(m, w["xa_wk"], name=f"xa_k_{tag}")
    vx = _mm(m, w["xa_wv"], name=f"xa_v_{tag}")
    o, = _rows(_xattn_fn, [qx], [kx, vx, w["xa_q_norm"], w["xa_k_norm"]], [(d, bf16)], tile=1024,
               name=f"xa_attn_{tag}")
    out, hf = _mm(o, w["xa_wo"], add=h, norm=w["norm_ffn"], name=f"xa_o_{tag}")
    return out, (h, hx, qx, m, kx, vx, o), hf


def _xattn_bwd(dout, mem, w, saved, tag):
    h, hx, qx, m, kx, vx, o = saved
    g = {}
    do = _mm(dout, w["xa_wo"], tb=True, out_dtype=bf16, name=f"xa_do_{tag}")
    g["xa_wo"] = _mm(o, dout, ta=True, out_dtype=bf16, name=f"xa_dwo_{tag}")
    (dqx,), (dkx, dvx, g["xa_q_norm"], g["xa_k_norm"]) = _rows_bwd(
        _xattn_fn, [qx], [kx, vx, w["xa_q_norm"], w["xa_k_norm"]], [do], rgrad=[bf16], pgrad=[True] * 4,
        tile=1024, name=f"xa_dattn_{tag}")
    dh, g["norm_xa"] = _mm(dqx, w["xa_wq"], tb=True, rms_bwd=(h, w["norm_xa"], dout), name=f"xa_dhx_{tag}")
    g["xa_wq"] = _mm(hx, dqx, ta=True, out_dtype=bf16, name=f"xa_dwq_{tag}")
    dm = _mm(dkx, w["xa_wk"], tb=True, name=f"xa_dm_k_{tag}")
    dm = _mm(dvx, w["xa_wv"], tb=True, add=dm, name=f"xa_dm_v_{tag}")
    g["xa_wk"] = _mm(m, dkx, ta=True, out_dtype=bf16, name=f"xa_dwk_{tag}")
    g["xa_wv"] = _mm(m, dvx, ta=True, out_dtype=bf16, name=f"xa_dwv_{tag}")
    _, (g["norm_mem"],) = _rows_bwd(_rms, [mem], [w["norm_mem"]], [dm], rgrad=[None], pgrad=[True],
                                    name=f"xa_dmnorm_{tag}")
    return dh, g


_HG_GROUP = 4


def _hg_chunk(q, k, v, g, *sts):
    c = q.shape[0]
    heads = [slice(h * HG_DIM, (h + 1) * HG_DIM) for h in range(len(sts))]
    tri = (lax.broadcasted_iota(jnp.int32, (c, c), 0) >= lax.broadcasted_iota(jnp.int32, (c, c), 1)).astype(f32)
    b = jnp.dot(tri, g, precision=lax.Precision.HIGHEST, preferred_element_type=f32)
    bend = jnp.sum(g, axis=0, keepdims=True)
    qe = (q * jnp.exp(b)).astype(bf16)
    kd = (k * jnp.exp(bend - b)).astype(bf16)
    vb = v.astype(bf16)
    decay = jnp.exp(bend)
    o_inter = [lax.dot_general(qe[:, hs], st.astype(bf16), _NT, preferred_element_type=f32) for hs, st in zip(heads, sts)]
    new = [st * decay[:, hs] + lax.dot_general(vb[:, hs], kd[:, hs], _TN, preferred_element_type=f32)
           for hs, st in zip(heads, sts)]
    outs = []
    for i in range(c // HG_SUB):
        lo, n = HG_SUB * i, HG_SUB * (i + 1)
        ref = jnp.sum(g[:lo], axis=0, keepdims=True) if i else jnp.zeros((1, g.shape[1]), f32)
        qh = (q[lo:n] * jnp.exp(b[lo:n] - ref)).astype(bf16)
        kh = (k[:n] * jnp.exp(ref - b[:n])).astype(bf16)
        keep = (lax.broadcasted_iota(jnp.int32, (HG_SUB, n), 1)
                <= lo + lax.broadcasted_iota(jnp.int32, (HG_SUB, n), 0))
        scores = [lax.dot_general(qh[:, hs], kh[:, hs], _NT, preferred_element_type=f32) for hs in heads]
        scores = [jnp.where(keep, a, 0.0).astype(bf16) for a in scores]
        outs.append(jnp.concatenate([jnp.dot(a, vb[:n, hs], preferred_element_type=f32)
                                     for a, hs in zip(scores, heads)], axis=1))
    return (jnp.concatenate(outs, axis=0) + jnp.concatenate(o_inter, axis=1), *new)


def _hg_fwd(q, k, v, g, *, name):
    length = q.shape[0]
    rows = _HG_GROUP * HG_CHUNK
    ng = length // rows
    nc = length // HG_CHUNK

    def body(q_ref, k_ref, v_ref, g_ref, o_ref, st_ref, state):
        @pl.when(pl.program_id(0) == 0)
        def _():
            state[...] = jnp.zeros_like(state)

        states = [state[h] for h in range(HG_HEADS)]
        for ci in range(_HG_GROUP):
            sl = slice(ci * HG_CHUNK, (ci + 1) * HG_CHUNK)
            for h in range(HG_HEADS):
                st_ref[h, ci] = states[h]
            o, *states = _hg_chunk(q_ref[sl, :], k_ref[sl, :], v_ref[sl, :], g_ref[sl, :], *states)
            o_ref[sl, :] = o
        for h in range(HG_HEADS):
            state[h] = states[h]

    blk = pl.BlockSpec((rows, HG_WIDTH), lambda c: (c, 0))
    return pl.pallas_call(
        body, grid=(ng,), in_specs=[blk] * 4,
        out_specs=[blk, pl.BlockSpec((HG_HEADS, _HG_GROUP, HG_DIM, HG_DIM), lambda c: (0, c, 0, 0))],
        out_shape=[jax.ShapeDtypeStruct((length, HG_WIDTH), f32),
                   jax.ShapeDtypeStruct((HG_HEADS, nc, HG_DIM, HG_DIM), f32)],
        scratch_shapes=[pltpu.VMEM((HG_HEADS, HG_DIM, HG_DIM), f32)],
        compiler_params=_cparams(("arbitrary",)), name=name)(q, k, v, g)


def _hg_bwd(q, k, v, g, states, do, *, name):
    length = q.shape[0]
    rows = _HG_GROUP * HG_CHUNK
    ng = length // rows

    def body(q_ref, k_ref, v_ref, g_ref, st_ref, do_ref, dq_ref, dk_ref, dv_ref, dg_ref, dstate):
        @pl.when(pl.program_id(0) == 0)
        def _():
            dstate[...] = jnp.zeros_like(dstate)

        dstates = [dstate[h] for h in range(HG_HEADS)]
        for ci in reversed(range(_HG_GROUP)):
            sl = slice(ci * HG_CHUNK, (ci + 1) * HG_CHUNK)
            _, vjp = jax.vjp(_hg_chunk, q_ref[sl, :], k_ref[sl, :], v_ref[sl, :], g_ref[sl, :],
                             *[st_ref[h, ci] for h in range(HG_HEADS)])
            dq, dk, dv, dg, *dstates = vjp((do_ref[sl, :], *dstates))
            dq_ref[sl, :] = dq
            dk_ref[sl, :] = dk
            dv_ref[sl, :] = dv
            dg_ref[sl, :] = dg
        for h in range(HG_HEADS):
            dstate[h] = dstates[h]

    blk = pl.BlockSpec((rows, HG_WIDTH), lambda c: (ng - 1 - c, 0))
    sds = jax.ShapeDtypeStruct((length, HG_WIDTH), f32)
    return pl.pallas_call(
        body, grid=(ng,),
        in_specs=[blk] * 4 + [pl.BlockSpec((HG_HEADS, _HG_GROUP, HG_DIM, HG_DIM), lambda c: (0, ng - 1 - c, 0, 0)), blk],
        out_specs=[blk] * 4, out_shape=[sds] * 4,
        scratch_shapes=[pltpu.VMEM((HG_HEADS, HG_DIM, HG_DIM), f32)],
        compiler_params=_cparams(("arbitrary",)), name=name)(q, k, v, g, states, do)


_ATT_BLK = 512
_ATT_SCALE = MLA_QK ** -0.5
_NEG = -1e30


def _att_mask(i, j, t):
    rows = i * t + lax.broadcasted_iota(jnp.int32, (t, t), 0)
    cols = j * t + lax.broadcasted_iota(jnp.int32, (t, t), 1)
    return cols <= rows


def _att_fwd(q, k, v, *, name):
    length = q.shape[0]
    t = min(_ATT_BLK, length)
    nq = length // t
    qw, vw = MLA_QK_PAD, MLA_V
    heads = range(MLA_HEADS)

    def body(q_ref, k_ref, v_ref, o_ref, lse_ref):
        i = pl.program_id(0)
        qbs = [q_ref[:, h * qw:(h + 1) * qw] for h in heads]

        def step(j, carry, diagonal=False):
            off = pl.multiple_of(j * t, t)
            out = []
            for h in heads:
                m, l, acc = carry[h]
                ks = k_ref[pl.ds(off, t), h * qw:(h + 1) * qw]
                vs = v_ref[pl.ds(off, t), h * vw:(h + 1) * vw]
                s = lax.dot_general(qbs[h], ks, _NT, preferred_element_type=f32) * _ATT_SCALE
                if diagonal:
                    s = jnp.where(_att_mask(i, j, t), s, _NEG)
                m_new = jnp.maximum(m, jnp.max(s, axis=-1, keepdims=True))
                alpha = jnp.exp(m - m_new)
                p = jnp.exp(s - m_new)
                l = alpha * l + jnp.sum(p, axis=-1, keepdims=True)
                acc = alpha * acc + jnp.dot(p.astype(bf16), vs, preferred_element_type=f32)
                out.append((m_new, l, acc))
            return tuple(out)

        init = tuple((jnp.full((t, 1), _NEG, f32), jnp.zeros((t, 1), f32), jnp.zeros((t, vw), f32)) for _ in heads)
        res = step(i, lax.fori_loop(0, i, step, init), diagonal=True)
        for h in heads:
            m, l, acc = res[h]
            o_ref[:, h * vw:(h + 1) * vw] = (acc / l).astype(o_ref.dtype)
            lse_ref[:, h * vw:(h + 1) * vw] = jnp.broadcast_to(m + jnp.log(l), (t, vw))

    return pl.pallas_call(
        body, grid=(nq,),
        in_specs=[pl.BlockSpec((t, q.shape[1]), lambda i: (i, 0)), pl.BlockSpec(k.shape, lambda i: (0, 0)),
                  pl.BlockSpec(v.shape, lambda i: (0, 0))],
        out_specs=[pl.BlockSpec((t, v.shape[1]), lambda i: (i, 0))] * 2,
        out_shape=[jax.ShapeDtypeStruct(v.shape, bf16), jax.ShapeDtypeStruct(v.shape, f32)],
        compiler_params=_cparams(("arbitrary",)), name=name)(q, k, v)


def _att_bwd(q, k, v, o, lse, do, *, name):
    length = q.shape[0]
    t = min(_ATT_BLK, length)
    nq = length // t
    qw, vw = MLA_QK_PAD, MLA_V
    heads = range(MLA_HEADS)

    def dq_body(q_ref, k_ref, v_ref, o_ref, lse_ref, do_ref, dq_ref, delta_ref):
        i = pl.program_id(0)
        qbs = [q_ref[:, h * qw:(h + 1) * qw] for h in heads]
        dobs = [do_ref[:, h * vw:(h + 1) * vw] for h in heads]
        lses = [lse_ref[:, h * vw:h * vw + 1] for h in heads]
        deltas = [jnp.sum(dobs[h].astype(f32) * o_ref[:, h * vw:(h + 1) * vw].astype(f32), axis=-1, keepdims=True)
                  for h in heads]

        def step(j, dqs, diagonal=False):
            off = pl.multiple_of(j * t, t)
            out = []
            for h in heads:
                ks = k_ref[pl.ds(off, t), h * qw:(h + 1) * qw]
                vs = v_ref[pl.ds(off, t), h * vw:(h + 1) * vw]
                s = lax.dot_general(qbs[h], ks, _NT, preferred_element_type=f32) * _ATT_SCALE
                p = jnp.exp(s - lses[h])
                if diagonal:
                    p = jnp.where(_att_mask(i, j, t), p, 0.0)
                dp = lax.dot_general(dobs[h], vs, _NT, preferred_element_type=f32)
                ds = p * (dp - deltas[h]) * _ATT_SCALE
                out.append(dqs[h] + jnp.dot(ds.astype(bf16), ks, preferred_element_type=f32))
            return tuple(out)

        dqs = step(i, lax.fori_loop(0, i, step, tuple(jnp.zeros((t, qw), f32) for _ in heads)), diagonal=True)
        for h in heads:
            dq_ref[:, h * qw:(h + 1) * qw] = dqs[h].astype(dq_ref.dtype)
            delta_ref[:, h * vw:(h + 1) * vw] = jnp.broadcast_to(deltas[h], (t, vw))

    qblk = pl.BlockSpec((t, q.shape[1]), lambda i: (i, 0))
    vblk = pl.BlockSpec((t, v.shape[1]), lambda i: (i, 0))
    qfull = pl.BlockSpec(q.shape, lambda i: (0, 0))
    vfull = pl.BlockSpec(v.shape, lambda i: (0, 0))
    dq, delta = pl.pallas_call(
        dq_body, grid=(nq,), in_specs=[qblk, qfull, vfull, vblk, vblk, vblk], out_specs=[qblk, vblk],
        out_shape=[jax.ShapeDtypeStruct(q.shape, bf16), jax.ShapeDtypeStruct(lse.shape, f32)],
        compiler_params=_cparams(("arbitrary",)), name=name + "_dq")(q, k, v, o, lse, do)

    def dkv_body(k_ref, v_ref, q_ref, do_ref, lse_ref, delta_ref, dk_ref, dv_ref):
        j = pl.program_id(0)
        kbs = [k_ref[:, h * qw:(h + 1) * qw] for h in heads]
        vbs = [v_ref[:, h * vw:(h + 1) * vw] for h in heads]

        def step(i, carry, diagonal=False):
            off = pl.multiple_of(i * t, t)
            out = []
            for h in heads:
                dk, dv = carry[h]
                qs = q_ref[pl.ds(off, t), h * qw:(h + 1) * qw]
                dos = do_ref[pl.ds(off, t), h * vw:(h + 1) * vw]
                lse_i = lse_ref[pl.ds(off, t), h * vw:h * vw + 1]
                delta_i = delta_ref[pl.ds(off, t), h * vw:h * vw + 1]
                s = lax.dot_general(qs, kbs[h], _NT, preferred_element_type=f32) * _ATT_SCALE
                p = jnp.exp(s - lse_i)
                if diagonal:
                    p = jnp.where(_att_mask(i, j, t), p, 0.0)
                dv = dv + lax.dot_general(p.astype(bf16), dos, _TN, preferred_element_type=f32)
                dp = lax.dot_general(dos, vbs[h], _NT, preferred_element_type=f32)
                ds = p * (dp - delta_i) * _ATT_SCALE
                dk = dk + lax.dot_general(ds.astype(bf16), qs, _TN, preferred_element_type=f32)
                out.append((dk, dv))
            return tuple(out)

        first = step(j, tuple((jnp.zeros((t, qw), f32), jnp.zeros((t, vw), f32)) for _ in heads), diagonal=True)
        res = lax.fori_loop(j + 1, nq, step, first)
        for h in heads:
            dk_ref[:, h * qw:(h + 1) * qw] = res[h][0].astype(dk_ref.dtype)
            dv_ref[:, h * vw:(h + 1) * vw] = res[h][1].astype(dv_ref.dtype)

    dk, dv = pl.pallas_call(
        dkv_body, grid=(nq,), in_specs=[qblk, vblk, qfull, vfull, vfull, vfull], out_specs=[qblk, vblk],
        out_shape=[jax.ShapeDtypeStruct(k.shape, bf16), jax.ShapeDtypeStruct(v.shape, bf16)],
        compiler_params=_cparams(("arbitrary",)), name=name + "_dkv")(k, v, q, do, lse, delta)
    return dq, dk, dv


_C_Q = 4 * HG_WIDTH
_C_KV = _C_Q + MLA_Q_RANK
_C_KPE = _C_KV + MLA_KV_RANK


def _rms_n(x, g, n):
    return x * lax.rsqrt(jnp.sum(x * x, axis=-1, keepdims=True) * (1.0 / n) + EPS) * g


def _mix_a(proj, l0, l1, q_a_norm, kv_a_norm):
    lb = jax.nn.sigmoid(l0 - l1)
    f = lb + (1.0 - lb) * jax.nn.sigmoid(proj[:, HG_WIDTH:2 * HG_WIDTH])
    qf = _silu(proj[:, :HG_WIDTH])
    v = proj[:, 2 * HG_WIDTH:3 * HG_WIDTH]
    cqn = _rms(proj[:, _C_Q:_C_KV], q_a_norm)
    ckvn = _rms(proj[:, _C_KV:_C_KPE], kv_a_norm)
    return qf, 1.0 - f, v, jnp.log(f), cqn, ckvn


def _mix_b(qraw, kvraw, kpe_raw, cos, sin, qn_nope, qn_rope, kn_nope, kn_rope, perm):
    def rope(x):
        return x * cos + jnp.dot(x, perm, precision=lax.Precision.HIGHEST, preferred_element_type=f32) * sin

    kpe = rope(_rms_n(kpe_raw, kn_rope, MLA_ROPE))
    qs, ks, vs = [], [], []
    for hh in range(MLA_HEADS):
        base = hh * MLA_QK_PAD
        qs.append(_rms(qraw[:, base:base + MLA_NOPE], qn_nope))
        qs.append(rope(_rms_n(qraw[:, base + MLA_NOPE:base + MLA_QK_PAD], qn_rope, MLA_ROPE)))
        ks.append(_rms(kvraw[:, base:base + MLA_NOPE], kn_nope))
        ks.append(kpe)
        vs.append(kvraw[:, base + MLA_NOPE:base + MLA_QK_PAD])
    return jnp.concatenate(qs, axis=-1), jnp.concatenate(ks, axis=-1), jnp.concatenate(vs, axis=-1)


def _mix_c(o_hg, gate, o_mla, hg_out_norm):
    parts = []
    for hh in range(HG_HEADS):
        sl = slice(hh * HG_DIM, (hh + 1) * HG_DIM)
        parts.append(_rms(o_hg[:, sl], hg_out_norm[:, sl]))
    o = jnp.concatenate(parts, axis=-1) * _silu(gate)
    return jnp.concatenate([o, o_mla], axis=-1)


def _rope_perm():
    p = np.zeros((128, 128), np.float32)
    half = MLA_ROPE // 2
    for i in range(half):
        p[i + half, i] = -1.0
        p[i, i + half] = 1.0
    return jnp.asarray(p)


def _mixer_fwd(h, cos, sin, w, tag, next_gain):
    d = h.shape[1]
    hn, = _rows(_rms, [h], [w["norm_mix"]], [(d, bf16)], name=f"mix_norm_{tag}")
    proj = _mm(hn, w["mix_w_in"], name=f"mix_in_{tag}")
    pa = [w["lb0"], w["lb1"], w["mla_q_a_norm"], w["mla_kv_a_norm"]]
    qf, kk, vv, logf, cqn, ckvn = _rows(
        _mix_a, [proj], pa, [(HG_WIDTH, f32)] * 4 + [(MLA_Q_RANK, bf16), (MLA_KV_RANK, bf16)], name=f"mix_a_{tag}")
    o_hg, states = _hg_fwd(qf, kk, vv, logf, name=f"hg_fwd_{tag}")
    qraw = _mm(cqn, w["mla_w_uq"], name=f"mla_uq_{tag}")
    kvraw = _mm(ckvn, w["mla_w_ukv"], name=f"mla_ukv_{tag}")
    pb = [w["mla_qn_nope"], w["mla_qn_rope"], w["mla_kn_nope"], w["mla_kn_rope"], w["rope_perm"]]
    kpe_raw, gate = _window(proj, _C_KPE, IN_PAD - _C_KPE), _window(proj, 3 * HG_WIDTH, HG_WIDTH)
    qfull, kfull, vfull = _rows(_mix_b, [qraw, kvraw, kpe_raw, cos, sin], pb,
                                [(MLA_HEADS * MLA_QK_PAD, bf16)] * 2 + [(MLA_HEADS * MLA_V, bf16)],
                                name=f"mix_b_{tag}")
    o_mla, lse = _att_fwd(qfull, kfull, vfull, name=f"att_fwd_{tag}")
    mixin, = _rows(_mix_c, [o_hg, gate, o_mla], [w["hg_out_norm"]], [(d, bf16)], name=f"mix_c_{tag}")
    if callable(w["mix_w_out"]):
        w["mix_w_out"] = w["mix_w_out"](mixin)
    out, normed = _mm(mixin, w["mix_w_out"], add=h, norm=next_gain, name=f"mix_out_{tag}")
    return out, (h, hn, proj, qf, kk, vv, logf, cqn, ckvn, o_hg, states, qraw, kvraw, qfull, kfull, vfull, o_mla,
                 lse, mixin), normed


def _mixer_bwd(dout, cos, sin, w, saved, tag, on_w_out=None):
    (h, hn, proj, qf, kk, vv, logf, cqn, ckvn, o_hg, states, qraw, kvraw, qfull, kfull, vfull, o_mla, lse,
     mixin) = saved
    g = {}
    dmixin = _mm(dout, w["mix_w_out"], tb=True, name=f"mix_dmixin_{tag}")
    g["mix_w_out"] = _mm(mixin, dout, ta=True, out_dtype=bf16, name=f"mix_dwout_{tag}")
    if on_w_out is not None:
        dmixin = on_w_out(g["mix_w_out"], dmixin)
    kpe_raw, gate = _window(proj, _C_KPE, IN_PAD - _C_KPE), _window(proj, 3 * HG_WIDTH, HG_WIDTH)
    (do_hg, dgate, do_mla), (g["hg_out_norm"],) = _rows_bwd(
        _mix_c, [o_hg, gate, o_mla], [w["hg_out_norm"]], [dmixin], rgrad=[f32, f32, bf16], pgrad=[True],
        name=f"mix_dc_{tag}")
    dqfull, dkfull, dvfull = _att_bwd(qfull, kfull, vfull, o_mla, lse, do_mla, name=f"att_bwd_{tag}")
    pb = [w["mla_qn_nope"], w["mla_qn_rope"], w["mla_kn_nope"], w["mla_kn_rope"], w["rope_perm"]]
    (dqraw, dkvraw, dkpe_raw), pg = _rows_bwd(
        _mix_b, [qraw, kvraw, kpe_raw, cos, sin], pb, [dqfull, dkfull, dvfull],
        rgrad=[bf16, bf16, f32, None, None], pgrad=[True, True, True, True, False],
        name=f"mix_db_{tag}")
    g["mla_qn_nope"], g["mla_qn_rope"], g["mla_kn_nope"], g["mla_kn_rope"] = pg
    dcqn = _mm(dqraw, w["mla_w_uq"], tb=True, name=f"mla_dcq_{tag}")
    g["mla_w_uq"] = _mm(cqn, dqraw, ta=True, name=f"mla_dwuq_{tag}")
    dckvn = _mm(dkvraw, w["mla_w_ukv"], tb=True, name=f"mla_dckv_{tag}")
    g["mla_w_ukv"] = _mm(ckvn, dkvraw, ta=True, name=f"mla_dwukv_{tag}")
    dqf, dkk, dvv, dlogf = _hg_bwd(qf, kk, vv, logf, states, do_hg, name=f"hg_bwd_{tag}")
    pa = [w["lb0"], w["lb1"], w["mla_q_a_norm"], w["mla_kv_a_norm"]]
    (dproj,), (g["lb0"], g["lb1"], g["mla_q_a_norm"], g["mla_kv_a_norm"]) = _rows_bwd(
        _mix_a, [proj], pa, [dqf, dkk, dvv, dlogf, dcqn, dckvn], rgrad=[bf16], pgrad=[True] * 4,
        addends={0: [(dgate, 3 * HG_WIDTH), (dkpe_raw, _C_KPE)]}, name=f"mix_da_{tag}")
    dh, g["norm_mix"] = _mm(dproj, w["mix_w_in"], tb=True, rms_bwd=(h, w["norm_mix"], dout), name=f"mix_dhn_{tag}")
    g["mix_w_in_t"] = _mm(dproj, hn, ta=True, name=f"mix_dwin_{tag}")
    return dh, g


def _rope_tables(positions):
    inv_freq = 1.0 / (ROPE_BASE ** (jnp.arange(0, MLA_ROPE, 2, dtype=f32) / MLA_ROPE))
    ang = positions.astype(f32)[:, None] * inv_freq
    z = jnp.zeros((positions.shape[0], 128 - MLA_ROPE), f32)
    return (jnp.concatenate([jnp.cos(ang), jnp.cos(ang), z], axis=1),
            jnp.concatenate([jnp.sin(ang), jnp.sin(ang), z], axis=1))


def _pad_cols(a, n):
    return jnp.pad(a, ((0, 0), (0, n - a.shape[1])))


def _even_weights(p, j, layer, dt):
    w_uq = p["mla_w_uq"][j].reshape(MLA_Q_RANK, MLA_HEADS, MLA_QK)
    w_uq = jnp.pad(w_uq, ((0, 0), (0, 0), (0, MLA_QK_PAD - MLA_QK))).reshape(MLA_Q_RANK, MLA_HEADS * MLA_QK_PAD)
    return dict(
        norm_mix=p["norm_mix"][layer][None], mix_w_in=_pad_cols(p["mix_w_in"][j], IN_PAD).astype(dt),
        lb0=p["hg_lb_logits"][0][None], lb1=p["hg_lb_logits"][1][None],
        mla_q_a_norm=p["mla_q_a_norm"][j][None], mla_kv_a_norm=p["mla_kv_a_norm"][j][None],
        mla_w_uq=w_uq.astype(dt), mla_w_ukv=p["mla_w_ukv"][j].astype(dt),
        mla_qn_nope=p["mla_qn_nope"][j][None], mla_qn_rope=_pad_cols(p["mla_qn_rope"][j][None], 128),
        mla_kn_nope=p["mla_kn_nope"][j][None], mla_kn_rope=_pad_cols(p["mla_kn_rope"][j][None], 128),
        rope_perm=_rope_perm(), hg_out_norm=p["hg_out_norm"][j][None],
        mix_w_out=p["mix_w_out"][j].astype(dt) if "mix_w_out" in p else None)


def _even_grads(g):
    w_uq = g["mla_w_uq"].reshape(MLA_Q_RANK, MLA_HEADS, MLA_QK_PAD)[:, :, :MLA_QK].reshape(MLA_Q_RANK, -1)
    return dict(
        norm_mix=g["norm_mix"], mix_w_in=g["mix_w_in_t"][:IN_WIDTH].T[None],
        hg_lb_logits=jnp.concatenate([g["lb0"], g["lb1"]], axis=0),
        mla_q_a_norm=g["mla_q_a_norm"], mla_kv_a_norm=g["mla_kv_a_norm"], mla_w_uq=w_uq[None],
        mla_w_ukv=g["mla_w_ukv"][None], mla_qn_nope=g["mla_qn_nope"], mla_qn_rope=g["mla_qn_rope"][:, :MLA_ROPE],
        mla_kn_nope=g["mla_kn_nope"], mla_kn_rope=g["mla_kn_rope"][:, :MLA_ROPE],
        hg_out_norm=g["hg_out_norm"], mix_w_out=g["mix_w_out"][None])


_S5_NB = 8
_S5_BW = 1024
_S5_HALF = 512
_S5_UC = 128
_S5_TIME = 512


def _cmul(ar, ai, br, bi):
    return ar * br - ai * bi, ar * bi + ai * br


def _pow_table(ar, ai, descending):
    rows = lax.broadcasted_iota(jnp.int32, (8, ar.shape[1]), 0)
    tr = jnp.zeros((8, ar.shape[1]), f32)
    ti = jnp.zeros((8, ar.shape[1]), f32)
    pr, pi_ = ar, ai
    for r in range(8):
        sel = rows == ((7 - r) if descending else r)
        tr = jnp.where(sel, pr, tr)
        ti = jnp.where(sel, pi_, ti)
        pr, pi_ = _cmul(pr, pi_, ar, ai)
    return tr, ti


def _s5_tile_scan(work, carry, ar, ai, tc, reverse, per_tile=None):
    hw = _S5_HALF
    row8 = lax.broadcasted_iota(jnp.int32, (8, hw), 0)
    powers = [(ar, ai)]
    for _ in range(2):
        powers.append(_cmul(*powers[-1], *powers[-1]))
    steps = []
    for (mr, mi), s in zip(powers, (1, 2, 4)):
        ok = (row8 < 8 - s) if reverse else (row8 >= s)
        steps.append((jnp.where(ok, mr, 0.0), jnp.where(ok, mi, 0.0), 8 - s if reverse else s))
    tr, ti = _pow_table(ar, ai, reverse)
    cr, ci = carry[:, :hw], carry[:, hw:]
    tiles = range(tc // 8)
    for i in (reversed(tiles) if reverse else tiles):
        sl = slice(8 * i, 8 * i + 8)
        xr, xi = work[sl, :hw], work[sl, hw:]
        for mr, mi, shift in steps:
            pr, pi_ = _cmul(mr, mi, pltpu.roll(xr, shift, axis=0), pltpu.roll(xi, shift, axis=0))
            xr, xi = xr + pr, xi + pi_
        pr, pi_ = _cmul(tr, ti, cr, ci)
        xr, xi = xr + pr, xi + pi_
        work[sl, :hw] = xr
        work[sl, hw:] = xi
        if per_tile is not None:
            per_tile(sl, xr, xi, cr, ci)
        edge = 8 * i if reverse else 8 * i + 7
        cr, ci = work[edge:edge + 1, :hw], work[edge:edge + 1, hw:]
    carry[:, :hw] = cr
    carry[:, hw:] = ci


def _s5_core_fwd(a, hn, b3, c3, *, name):
    length = hn.shape[0]
    tc = min(_S5_TIME, length)

    def body(a_ref, hn_ref, b_ref, c_ref, hs_ref, y_ref, work, carry):
        @pl.when(pl.program_id(1) == 0)
        def _():
            carry[...] = jnp.zeros_like(carry)

        work[...] = jnp.dot(hn_ref[...].astype(bf16), b_ref[...], preferred_element_type=f32)
        _s5_tile_scan(work, carry, a_ref[:, :_S5_HALF], a_ref[:, _S5_HALF:], tc, False)
        hs = work[...].astype(bf16)
        hs_ref[...] = hs
        y_ref[...] = jnp.dot(hs, c_ref[...], preferred_element_type=f32)

    return pl.pallas_call(
        body, grid=(_S5_NB, length // tc),
        in_specs=[pl.BlockSpec((1, _S5_BW), lambda j, t: (0, j)), pl.BlockSpec((tc, _S5_UC), lambda j, t: (t, j)),
                  pl.BlockSpec((_S5_UC, _S5_BW), lambda j, t: (j, 0)), pl.BlockSpec((_S5_BW, _S5_UC), lambda j, t: (j, 0))],
        out_specs=[pl.BlockSpec((tc, _S5_BW), lambda j, t: (t, j)), pl.BlockSpec((tc, _S5_UC), lambda j, t: (t, j))],
        out_shape=[jax.ShapeDtypeStruct((length, _S5_NB * _S5_BW), bf16),
                   jax.ShapeDtypeStruct((length, _S5_NB * _S5_UC), f32)],
        scratch_shapes=[pltpu.VMEM((tc, _S5_BW), f32), pltpu.VMEM((1, _S5_BW), f32)],
        compiler_params=_cparams(("parallel", "arbitrary")), name=name)(a, hn, b3, c3)


def _s5_core_bwd(a, dy, c3, hs, hn, b3, *, name):
    length = hn.shape[0]
    tc = min(_S5_TIME, length)
    nt = length // tc
    hw = _S5_HALF

    def body(a_ref, dy_ref, c_ref, hs_ref, hn_ref, b_ref, du_ref, db_ref, dc_ref, da_ref, work, carry, acc):
        @pl.when(pl.program_id(1) == 0)
        def _():
            carry[...] = jnp.zeros_like(carry)
            db_ref[...] = jnp.zeros_like(db_ref)
            dc_ref[...] = jnp.zeros_like(dc_ref)
            da_ref[...] = jnp.zeros_like(da_ref)

        dyb = dy_ref[...].astype(bf16)
        work[...] = lax.dot_general(dyb, c_ref[...], _NT, preferred_element_type=f32)
        acc[...] = jnp.zeros_like(acc)
        row8 = lax.broadcasted_iota(jnp.int32, (8, hw), 0)

        def grad_a(sl, gr, gi, cr, ci):
            gnr = jnp.where(row8 == 7, cr, pltpu.roll(gr, 7, axis=0))
            gni = jnp.where(row8 == 7, ci, pltpu.roll(gi, 7, axis=0))
            hr, hi = hs_ref[sl, :hw].astype(f32), hs_ref[sl, hw:].astype(f32)
            acc[:, :hw] += hr * gnr + hi * gni
            acc[:, hw:] += hr * gni - hi * gnr

        _s5_tile_scan(work, carry, a_ref[:, :hw], -a_ref[:, hw:], tc, True, grad_a)
        da_ref[...] += jnp.sum(acc[...], axis=0, keepdims=True)
        g = work[...].astype(bf16)
        du_ref[...] = lax.dot_general(g, b_ref[...], _NT, preferred_element_type=f32)
        db_ref[...] += lax.dot_general(hn_ref[...].astype(bf16), g, _TN, preferred_element_type=f32)
        dc_ref[...] += lax.dot_general(hs_ref[...], dyb, _TN, preferred_element_type=f32)

    rev = lambda j, t: (nt - 1 - t, j)
    return pl.pallas_call(
        body, grid=(_S5_NB, nt),
        in_specs=[pl.BlockSpec((1, _S5_BW), lambda j, t: (0, j)), pl.BlockSpec((tc, _S5_UC), rev),
                  pl.BlockSpec((_S5_BW, _S5_UC), lambda j, t: (j, 0)), pl.BlockSpec((tc, _S5_BW), rev),
                  pl.BlockSpec((tc, _S5_UC), rev), pl.BlockSpec((_S5_UC, _S5_BW), lambda j, t: (j, 0))],
        out_specs=[pl.BlockSpec((tc, _S5_UC), rev), pl.BlockSpec((_S5_UC, _S5_BW), lambda j, t: (j, 0)),
                   pl.BlockSpec((_S5_BW, _S5_UC), lambda j, t: (j, 0)), pl.BlockSpec((1, _S5_BW), lambda j, t: (0, j))],
        out_shape=[jax.ShapeDtypeStruct((length, _S5_NB * _S5_UC), f32),
                   jax.ShapeDtypeStruct((_S5_NB * _S5_UC, _S5_BW), f32),
                   jax.ShapeDtypeStruct((_S5_NB * _S5_BW, _S5_UC), f32),
                   jax.ShapeDtypeStruct((1, _S5_NB * _S5_BW), f32)],
        scratch_shapes=[pltpu.VMEM((tc, _S5_BW), f32), pltpu.VMEM((1, _S5_BW), f32), pltpu.VMEM((8, _S5_BW), f32)],
        compiler_params=_cparams(("parallel", "arbitrary")), name=name)(a, dy, c3, hs, hn, b3)


def _s5_disc(lr, li, ldt, btr, bti, expand):
    dt = jnp.exp(ldt)
    mag = jnp.exp(lr * dt)
    abr = mag * jnp.cos(li * dt)
    abi = mag * jnp.sin(li * dt)
    den = lr * lr + li * li
    zr = ((abr - 1.0) * lr + abi * li) / den
    zi = (abi * lr - (abr - 1.0) * li) / den
    zr = jnp.dot(zr, expand, precision=lax.Precision.HIGHEST, preferred_element_type=f32)
    zi = jnp.dot(zi, expand, precision=lax.Precision.HIGHEST, preferred_element_type=f32)
    return abr, abi, zr * btr - zi * bti, zr * bti + zi * btr


def _s5_disc_fwd(args, *, name):
    def body(*refs):
        res = _s5_disc(*[r[...] for r in refs[:6]])
        for o, v in zip(refs[6:], res):
            o[...] = v

    sds = jax.ShapeDtypeStruct
    return pl.pallas_call(body, out_shape=[sds(args[0].shape, f32)] * 2 + [sds(args[3].shape, f32)] * 2,
                          name=name)(*args)


def _s5_disc_bwd(args, cts, *, name):
    def body(*refs):
        vals = [r[...] for r in refs[:6]]
        _, vjp = jax.vjp(lambda *d: _s5_disc(*d, vals[5]), *vals[:5])
        grads = vjp(tuple(r[...] for r in refs[6:10]))
        for o, v in zip(refs[10:], grads):
            o[...] = v

    return pl.pallas_call(body, out_shape=[jax.ShapeDtypeStruct(a.shape, f32) for a in args[:5]],
                          name=name)(*args, *cts)


def _gelu_tanh(x):
    return 0.5 * x * (1.0 + jnp.tanh(0.7978845608028654 * (x + 0.044715 * (x * x * x))))


def _s5_post(y, u, d_skip):
    return _gelu_tanh(y + d_skip * u)


def _s5_glu(ga, gb, h):
    return h + ga * jax.nn.sigmoid(gb)


def _s5_glu_norm(ga, gb, h, next_gain):
    out = _s5_glu(ga, gb, h)
    return out, _rms(out, next_gain)


def _s5_expand():
    e = np.zeros((S5_STATE, S5_GROUP * S5_STATE), np.float32)
    for m in range(S5_GROUP):
        e[np.arange(S5_STATE), m * S5_STATE + np.arange(S5_STATE)] = 1.0
    return jnp.asarray(e)


def _s5_pack_b(bbr, bbi):
    eye = jnp.eye(8, dtype=f32)

    def one(bb):
        b5 = bb.reshape(_S5_NB, 8, S5_GROUP, S5_STATE)
        return jnp.einsum("jgmp,gh->jgmhp", b5, eye).reshape(_S5_NB * _S5_UC, _S5_HALF)

    return jnp.concatenate([one(bbr), one(bbi)], axis=1)


def _s5_unpack_b(db3):
    def one(d):
        d5 = d.reshape(_S5_NB, 8, S5_GROUP, 8, S5_STATE)
        return jnp.einsum("jgmgp->jgmp", d5).reshape(S5_GROUPS, S5_GROUP * S5_STATE)

    return one(db3[:, :_S5_HALF]), one(db3[:, _S5_HALF:])


def _s5_pack_c(c_re, c_im):
    eye = jnp.eye(8, dtype=f32)

    def one(c):
        c4 = c.reshape(_S5_NB, 8, S5_GROUP, S5_STATE)
        return jnp.einsum("jgmp,hg->jhpgm", c4, eye).reshape(_S5_NB, _S5_HALF, _S5_UC)

    return jnp.concatenate([one(c_re), -one(c_im)], axis=1).reshape(_S5_NB * _S5_BW, _S5_UC)


def _s5_unpack_c(dc3):
    d = dc3.reshape(_S5_NB, 2, 8, S5_STATE, 8, S5_GROUP)
    dre = jnp.einsum("jgpgm->jgmp", d[:, 0]).reshape(S5_GROUPS, S5_GROUP, S5_STATE)
    dim = -jnp.einsum("jgpgm->jgmp", d[:, 1]).reshape(S5_GROUPS, S5_GROUP, S5_STATE)
    return dre, dim


def _s5_state_row(re, im):
    r = re.reshape(_S5_NB, 1, _S5_HALF)
    i = im.reshape(_S5_NB, 1, _S5_HALF)
    return jnp.concatenate([r, i], axis=2).reshape(1, _S5_NB * _S5_BW)


def _s5_unstate_row(row):
    r = row.reshape(_S5_NB, 2, 8, S5_STATE)
    return r[:, 0].reshape(S5_GROUPS, S5_STATE), r[:, 1].reshape(S5_GROUPS, S5_STATE)


def _s5_fwd(h, w, tag, next_gain):
    d = h.shape[1]
    hn, = _rows(_rms, [h], [w["norm_mix"]], [(d, f32)], name=f"s5_norm_{tag}")
    disc_in = [w["s5_lam_re"], w["s5_lam_im"], w["s5_log_dt"], w["s5_bt_re"], w["s5_bt_im"], w["s5_expand"]]
    abr, abi, bbr, bbi = _s5_disc_fwd(disc_in, name=f"s5_disc_{tag}")
    a_row = _s5_state_row(abr, abi)
    b3 = _s5_pack_b(bbr, bbi).astype(bf16)
    hs, y = _s5_core_fwd(a_row, hn, b3, w["s5_c3"], name=f"s5_core_{tag}")
    yg, = _rows(_s5_post, [y, hn], [w["s5_d"]], [(d, bf16)], name=f"s5_post_{tag}")
    ga = _mm(yg, w["s5_w_glu_a"], name=f"s5_glu_a_{tag}")
    gb = _mm(yg, w["s5_w_glu_b"], name=f"s5_glu_b_{tag}")
    out, normed = _rows(_s5_glu_norm, [ga, gb, h], [next_gain], [(d, f32), (d, bf16)], name=f"s5_glu_{tag}")
    return out, (h, hn, disc_in, a_row, b3, hs, y, yg, ga, gb), normed


def _s5_bwd(dout, w, saved, tag):
    h, hn, disc_in, a_row, b3, hs, y, yg, ga, gb = saved
    g = {}
    (dga, dgb), _ = _rows_bwd(_s5_glu, [ga, gb, h], [], [dout], rgrad=[bf16, bf16, None], pgrad=[],
                              name=f"s5_dglu_{tag}")
    dyg = _mm(dga, w["s5_w_glu_a"], tb=True, name=f"s5_dyg_a_{tag}")
    dyg = _mm(dgb, w["s5_w_glu_b"], tb=True, add=dyg, name=f"s5_dyg_b_{tag}")
    g["s5_w_glu_a"] = _mm(yg, dga, ta=True, out_dtype=bf16, name=f"s5_dwa_{tag}")
    g["s5_w_glu_b"] = _mm(yg, dgb, ta=True, out_dtype=bf16, name=f"s5_dwb_{tag}")
    (dy, du_skip), (g["s5_d"],) = _rows_bwd(_s5_post, [y, hn], [w["s5_d"]], [dyg], rgrad=[bf16, f32], pgrad=[True],
                                           name=f"s5_dpost_{tag}")
    du, db3, dc3, da_row = _s5_core_bwd(a_row, dy, w["s5_c3"], hs, hn, b3, name=f"s5_dcore_{tag}")
    dabr, dabi = _s5_unstate_row(da_row)
    dbbr, dbbi = _s5_unpack_b(db3)
    g["s5_lam_re"], g["s5_lam_im"], g["s5_log_dt"], g["s5_bt_re"], g["s5_bt_im"] = _s5_disc_bwd(
        disc_in, [dabr, dabi, dbbr, dbbi], name=f"s5_ddisc_{tag}")
    g["s5_c_re"], g["s5_c_im"] = _s5_unpack_c(dc3)
    (dh,), (g["norm_mix"],) = _rows_bwd(_rms_twice, [h], [w["norm_mix"]], [du, du_skip], rgrad=[f32], pgrad=[True],
                                        addends={0: dout}, name=f"s5_dnorm_{tag}")
    return dh, g


def _odd_weights(p, j, layer, dt):
    tr = lambda b: b.transpose(0, 2, 1).reshape(S5_GROUPS, S5_GROUP * S5_STATE)
    return dict(
        norm_mix=p["norm_mix"][layer][None], s5_lam_re=p["s5_lam_re"][j], s5_lam_im=p["s5_lam_im"][j],
        s5_log_dt=p["s5_log_dt"][j][:, None], s5_bt_re=tr(p["s5_b_re"][j]), s5_bt_im=tr(p["s5_b_im"][j]),
        s5_expand=_s5_expand(), s5_c3=_s5_pack_c(p["s5_c_re"][j], p["s5_c_im"][j]).astype(dt),
        **{n: (p[n][j][None] if n == "s5_d" else p[n][j].astype(dt))
           for n in ("s5_d", "s5_w_glu_a", "s5_w_glu_b") if n in p})


def _odd_grads(g):
    tr = lambda b: b.reshape(S5_GROUPS, S5_GROUP, S5_STATE).transpose(0, 2, 1)[None]
    return dict(
        norm_mix=g["norm_mix"], s5_lam_re=g["s5_lam_re"][None], s5_lam_im=g["s5_lam_im"][None],
        s5_log_dt=g["s5_log_dt"][:, 0][None], s5_b_re=tr(g["s5_bt_re"]), s5_b_im=tr(g["s5_bt_im"]),
        s5_c_re=g["s5_c_re"][None], s5_c_im=g["s5_c_im"][None], s5_d=g["s5_d"],
        s5_w_glu_a=g["s5_w_glu_a"][None], s5_w_glu_b=g["s5_w_glu_b"][None])


def _loss_fn(y, t):
    e = y - t
    part = jnp.sum(jnp.sum(e * e, axis=-1, keepdims=True), axis=0, keepdims=True) * (0.5 / y.shape[1])
    return e * (1.0 / y.shape[1]), part


FF_SHARD = 352
FF_SHARD_PAD = 384


def _pad_groups(a, axis):
    axis %= a.ndim
    zeros = jnp.zeros(a.shape[:axis] + (FF_SHARD_PAD - FF_SHARD,) + a.shape[axis + 1:], a.dtype)
    pieces = []
    for g in range(a.shape[axis] // FF_SHARD):
        pieces += [lax.slice_in_dim(a, g * FF_SHARD, (g + 1) * FF_SHARD, axis=axis), zeros]
    return jnp.concatenate(pieces, axis=axis)


def _unpad_groups(a, axis):
    axis %= a.ndim
    pieces = [lax.slice_in_dim(a, g * FF_SHARD_PAD, g * FF_SHARD_PAD + FF_SHARD, axis=axis)
              for g in range(a.shape[axis] // FF_SHARD_PAD)]
    return pieces[0] if len(pieces) == 1 else jnp.concatenate(pieces, axis=axis)


def _layer_weights(p, layer, dt):
    return dict(
        norm_xa=p["norm_xa"][layer][None], norm_mem=p["norm_mem"][layer][None], norm_ffn=p["norm_ffn"][layer][None],
        xa_wq=p["xa_wq"][layer].astype(dt), xa_wk=p["xa_wk"][layer].astype(dt), xa_wv=p["xa_wv"][layer].astype(dt),
        xa_wo=p["xa_wo"][layer].astype(dt), xa_q_norm=p["xa_q_norm"][layer][None],
        xa_k_norm=p["xa_k_norm"][layer][None], ffn_w_up=_pad_groups(p["ffn_w_up"][layer], 1).astype(dt),
        ffn_conv_w=_pad_groups(p["ffn_conv_w"][layer], 1), ffn_conv_b=_pad_groups(p["ffn_conv_b"][layer][None], 1),
        ffn_w_down=_pad_groups(p["ffn_w_down"][layer], 0).astype(dt))


_PER_LAYER = ("norm_xa", "norm_mem", "norm_ffn", "xa_wq", "xa_wk", "xa_wv", "xa_wo", "xa_q_norm", "xa_k_norm",
              "ffn_w_up", "ffn_conv_w", "ffn_conv_b", "ffn_w_down")
_FFN_PADDED = dict(ffn_w_up=1, ffn_conv_w=1, ffn_conv_b=1, ffn_w_down=0)


def _local_step(x, mem, positions, target, p):
    cos, sin = _rope_tables(positions)
    we = _even_weights(p, 0, 0, bf16)
    wo = _odd_weights(p, 0, 1, bf16)
    wl = [_layer_weights(p, layer, bf16) for layer in range(2)]
    loss, dh, g_even, g_odd, gl = _local_core(x, mem, cos, sin, target, we, wo, wl)
    grads = {}
    for n in _PER_LAYER:
        a, b = gl[0][n], gl[1][n]
        if n in _FFN_PADDED:
            a, b = _unpad_groups(a, _FFN_PADDED[n]), _unpad_groups(b, _FFN_PADDED[n])
        grads[n] = jnp.concatenate([a, b], axis=0) if a.shape[0] == 1 else jnp.stack([a, b])
    ge, go = _even_grads(g_even), _odd_grads(g_odd)
    grads["norm_mix"] = jnp.concatenate([ge.pop("norm_mix"), go.pop("norm_mix")], axis=0)
    grads.update(ge)
    grads.update(go)
    return loss, dh, grads


def _local_core(x, mem, cos, sin, target, we, wo, wl):
    h, s_mix0, hx = _mixer_fwd(x, cos, sin, we, "l0", wl[0]["norm_xa"])
    h, s_xa0, hf = _xattn_fwd(h, mem, wl[0], "l0", hx)
    h, s_ff0 = _ffn_fwd(h, wl[0], "l0", hf)
    h, s_mix1, hx = _s5_fwd(h, wo, "l1", wl[1]["norm_xa"])
    h, s_xa1, hf = _xattn_fwd(h, mem, wl[1], "l1", hx)
    h, s_ff1 = _ffn_fwd(h, wl[1], "l1", hf)
    dh, loss = _rows(_loss_fn, [h, target], [], [(h.shape[1], f32)], accs=[(1, 1)], name="loss_head")

    gl = [{}, {}]
    dh, g = _ffn_bwd(dh, wl[1], s_ff1, "l1")
    gl[1].update(g)
    dh, g = _xattn_bwd(dh, mem, wl[1], s_xa1, "l1")
    gl[1].update(g)
    dh, g_odd = _s5_bwd(dh, wo, s_mix1, "l1")
    dh, g = _ffn_bwd(dh, wl[0], s_ff0, "l0")
    gl[0].update(g)
    dh, g = _xattn_bwd(dh, mem, wl[0], s_xa0, "l0")
    gl[0].update(g)
    dh, g_even = _mixer_bwd(dh, cos, sin, we, s_mix0, "l0")
    return loss, dh, g_even, g_odd, gl


_LANES = 1024
_ROW_PAD = 16


_PEER_MASKS = (1, 2, 4, 3, 5, 6, 7)


def _mesh_place():
    x, y, c = lax.axis_index("x"), lax.axis_index("y"), lax.axis_index("c")

    def peer(mask):
        px = 1 - x if mask & 4 else x
        py = 1 - y if mask & 2 else y
        pc = 1 - c if mask & 1 else c
        return (px, py, pc), 4 * px + 2 * py + pc

    return 4 * x + 2 * y + c, peer


class _Exchange:
    def __init__(self, name):
        self.name = name
        self.srcs, self.shapes, self.items, self.where = [], [], [], {}

    def add(self, src, land_shape, src_at, dst_at, key):
        si = next((i for i, s in enumerate(self.srcs) if s is src), None)
        if si is None:
            self.srcs.append(src)
            si = len(self.srcs) - 1
        if key not in self.where:
            self.shapes.append(land_shape)
            self.where[key] = len(self.shapes) - 1
        self.items.append(dict(src=si, dst=self.where[key], src_at=src_at, dst_at=dst_at))

    def _copy(self, k, mask, ins, lands, send_sems, recv_sems, me, peer, arriving):
        it = self.items[k]
        dev, idx = peer(mask)
        s = k * (N_DEV - 1) + _PEER_MASKS.index(mask)
        return pltpu.make_async_remote_copy(
            src_ref=it["src_at"](ins[it["src"]], idx), dst_ref=it["dst_at"](lands[it["dst"]], idx if arriving else me),
            send_sem=send_sems.at[s], recv_sem=recv_sems.at[s], device_id=dev, device_id_type=pl.DeviceIdType.MESH)

    def _own_copy(self, k, ins, lands, own_sems, me):
        it = self.items[k]
        return pltpu.make_async_copy(it["src_at"](ins[it["src"]], me), it["dst_at"](lands[it["dst"]], me), own_sems.at[k])

    def begin(self, own):
        ns, nd, ni = len(self.srcs), len(self.shapes), len(self.items)
        nsem = ni * (N_DEV - 1)
        self.own = own

        nq = 3 if own else 2

        def body(*refs):
            ins, land_refs = refs[:ns], refs[ns:ns + nd]
            sems, token = refs[ns + nd:ns + nd + nq], refs[-1]
            me, peer = _mesh_place()
            for mask in _PEER_MASKS:
                for k in range(ni):
                    self._copy(k, mask, ins, land_refs, sems[0], sems[1], me, peer, False).start()
            if own:
                for k in range(ni):
                    self._own_copy(k, ins, land_refs, sems[2], me).start()
            token[...] = jnp.zeros_like(token)

        hbm = pl.BlockSpec(memory_space=pltpu.HBM)
        sem = pl.BlockSpec(memory_space=pltpu.SEMAPHORE)
        lands = [lax.empty(s.shape, s.dtype) for s in self.shapes]
        sem_shapes = [pltpu.SemaphoreType.DMA((nsem,)), pltpu.SemaphoreType.DMA((nsem,)), pltpu.SemaphoreType.DMA((ni,))]
        res = pl.pallas_call(
            body, in_specs=[hbm] * (ns + nd),
            out_specs=[sem] * nq + [hbm] * nd + [pl.BlockSpec(memory_space=pltpu.VMEM)],
            out_shape=sem_shapes[:nq] + [pltpu.HBM(s.shape, s.dtype) for s in self.shapes]
            + [jax.ShapeDtypeStruct((8, 128), f32)],
            input_output_aliases={ns + j: nq + j for j in range(nd)},
            compiler_params=pltpu.CompilerParams(has_side_effects=pltpu.SideEffectType.DATAFLOW_SIDE_EFFECTING),
            name=self.name + "_start")(*self.srcs, *lands)
        self.token = res[-1]
        return list(res[:nq]), list(res[nq:-1])

    def finish(self, state, after):
        sems, lands = state
        nq = len(sems)
        after = list(after) if isinstance(after, (list, tuple)) else [after]
        ns, nd, ni = len(self.srcs), len(self.shapes), len(self.items)

        def body(*refs):
            ins, land_refs = refs[:ns], refs[ns:ns + nd]
            sem_refs = refs[ns + nd:ns + nd + nq]
            me, peer = _mesh_place()
            for mask in _PEER_MASKS:
                for k in range(ni):
                    cp = self._copy(k, mask, ins, land_refs, sem_refs[0], sem_refs[1], me, peer, True)
                    cp.wait_send()
                    cp.wait_recv()
            if self.own:
                for k in range(ni):
                    self._own_copy(k, ins, land_refs, sem_refs[2], me).wait()

        hbm = pl.BlockSpec(memory_space=pltpu.HBM)
        sem = pl.BlockSpec(memory_space=pltpu.SEMAPHORE)
        res = pl.pallas_call(
            body, in_specs=[hbm] * (ns + nd) + [sem] * nq + [pl.BlockSpec(memory_space=pl.ANY)] * len(after),
            out_specs=[hbm] * nd, out_shape=[pltpu.HBM(s.shape, s.dtype) for s in self.shapes],
            input_output_aliases={ns + j: j for j in range(nd)},
            compiler_params=pltpu.CompilerParams(has_side_effects=pltpu.SideEffectType.DATAFLOW_SIDE_EFFECTING),
            name=self.name + "_wait")(*self.srcs, *lands, *sems, *after)
        return {k: res[i] for k, i in self.where.items()}


def _after(x, *tokens, name):
    def body(*refs):
        del refs

    anyspace = pl.BlockSpec(memory_space=pl.ANY)
    return pl.pallas_call(body, in_specs=[anyspace] * (1 + len(tokens)), out_specs=anyspace,
                          out_shape=jax.ShapeDtypeStruct(x.shape, x.dtype), input_output_aliases={0: 0},
                          name=name)(x, *tokens)


def _rows_of(n):
    return lambda r, i: r.at[pl.ds(pl.multiple_of(i * n, n), n), :]


def _cols_of(n):
    return lambda r, i: r.at[:, pl.ds(pl.multiple_of(i * n, n), n)]


def _whole(r, i):
    return r


def _slot(r, i):
    return r.at[i]


def _at_layer(layer):
    return lambda r, i: r.at[layer]


def _sum_adam(me_index, slots, owns, own_block, w, m, v, *, name):
    layers, rows, cols = w.shape
    tr = _pick(rows, (256, 128, 104, 64, 32, 16, 8))
    bc1 = 1.0 - ADAM_B1 ** ADAM_STEP
    bc2 = 1.0 - ADAM_B2 ** ADAM_STEP
    own_shape, own_map = own_block(tr)
    nl = len(slots)
    assert nl == layers and len(owns) == layers

    def body(me_ref, *refs):
        s_refs, own_refs = refs[:nl], refs[nl:2 * nl]
        w_ref, m_ref, v_ref, g_ref, d_ref, nm_ref, nv_ref = refs[2 * nl:]
        me = me_ref[0]

        def run(s_ref, own_ref):
            mine = (own_ref[0] if len(own_shape) == 3 else own_ref[...]).astype(f32)
            g = jnp.where(me == 0, mine, s_ref[0].astype(f32))
            for k in range(1, N_DEV):
                g = g + jnp.where(me == k, mine, s_ref[k].astype(f32))
            mm = ADAM_B1 * m_ref[0] + (1.0 - ADAM_B1) * g
            vv = ADAM_B2 * v_ref[0] + (1.0 - ADAM_B2) * (g * g)
            g_ref[0] = g
            nm_ref[0] = mm
            nv_ref[0] = vv
            d_ref[0] = -ADAM_LR * ((mm / bc1) / (jnp.sqrt(vv / bc2) + ADAM_EPS) + ADAM_WD * w_ref[0])

        for layer in range(nl):
            pl.when(pl.program_id(0) == layer)(functools.partial(run, s_refs[layer], own_refs[layer]))

    def of_layer(layer, index_map):
        return lambda lyr, i, me: index_map(jnp.where(lyr == layer, i, 0), me)

    blk = pl.BlockSpec((1, tr, cols), lambda lyr, i, me: (lyr, i, 0))
    sds = jax.ShapeDtypeStruct((layers, rows, cols), f32)
    grid_spec = pltpu.PrefetchScalarGridSpec(
        num_scalar_prefetch=1, grid=(layers, rows // tr),
        in_specs=[pl.BlockSpec((N_DEV, tr, cols), of_layer(layer, lambda i, me: (0, i, 0))) for layer in range(nl)]
        + [pl.BlockSpec(own_shape, of_layer(layer, own_map)) for layer in range(nl)] + [blk, blk, blk],
        out_specs=[blk] * 4)
    return pl.pallas_call(body, grid_spec=grid_spec, out_shape=[sds] * 4,
                          compiler_params=_cparams(("arbitrary", "arbitrary")),
                          name=name)(me_index, *slots, *owns, w, m, v)


_SHARDED = dict(xa_wq=1, xa_wk=1, xa_wv=1, xa_wo=1, ffn_w_up=2, ffn_conv_w=2, ffn_w_down=1, mix_w_in=2, mla_w_uq=2,
                mla_w_ukv=2, mix_w_out=1, s5_d=1, s5_w_glu_a=1, s5_w_glu_b=1)
_EXACT = ("ffn_conv_w", "s5_d")
_WEIGHTS = ("norm_mix", "norm_xa", "norm_mem", "norm_ffn", "xa_wq", "xa_wk", "xa_wv", "xa_wo", "xa_q_norm",
            "xa_k_norm", "ffn_w_up", "ffn_conv_w", "ffn_conv_b", "ffn_w_down", "hg_lb_logits", "mix_w_in",
            "hg_out_norm", "mla_q_a_norm", "mla_w_uq", "mla_kv_a_norm", "mla_w_ukv", "mla_qn_nope", "mla_qn_rope",
            "mla_kn_nope", "mla_kn_rope", "mix_w_out", "s5_lam_re", "s5_lam_im", "s5_log_dt", "s5_b_re", "s5_b_im",
            "s5_c_re", "s5_c_im", "s5_d", "s5_w_glu_a", "s5_w_glu_b")
_BIG = tuple(n for n in _WEIGHTS if n in _SHARDED and n not in _EXACT)
_SHARD_ORDER = tuple(n for n in _WEIGHTS if n in _SHARDED)
_REPL_ORDER = tuple(n for n in _WEIGHTS if n not in _SHARDED)
_REPL_EARLY = tuple(n for n in _REPL_ORDER if n.startswith("s5_"))
_REPL_LATE = tuple(n for n in _REPL_ORDER if n not in _REPL_EARLY)


def _pack(parts, dtype, lead=None):
    nl = 0 if lead is None else 1
    flat = [a.astype(dtype).reshape(a.shape[:nl] + (-1,)) for a in parts]
    cat = jnp.concatenate(flat, axis=nl)
    n = cat.shape[nl]
    unit = _LANES * _ROW_PAD
    total = -(-n // unit) * unit
    cat = jnp.pad(cat, [(0, 0)] * nl + [(0, total - n)])
    return cat.reshape(cat.shape[:nl] + (total // _LANES, _LANES))


def _unpack(packed, shapes, lead=None):
    nl = 0 if lead is None else 1
    flat = packed.reshape(packed.shape[:nl] + (-1,))
    out, off = [], 0
    for s in shapes:
        n = int(np.prod(s))
        piece = flat[..., off:off + n] if nl else flat[off:off + n]
        out.append(piece.reshape(packed.shape[:nl] + tuple(s)))
        off += n
    return out


def _to_full(gathered, axis):
    g = jnp.moveaxis(gathered, 0, axis)
    s = g.shape
    return g.reshape(s[:axis] + (s[axis] * s[axis + 1],) + s[axis + 2:])


def _to_shards(full, axis):
    s = full.shape
    g = full.reshape(s[:axis] + (N_DEV, s[axis] // N_DEV) + s[axis + 1:])
    return jnp.moveaxis(g, axis, 0)


_DIRECT_ROWS = ("xa_wq", "xa_wk", "xa_wv", "xa_wo", "mix_w_out", "s5_w_glu_a", "s5_w_glu_b")
_SMALL16 = ("mix_w_in", "mla_w_uq", "mla_w_ukv")
_SMALL_SHARDED = ("mla_w_uq", "mla_w_ukv") + _EXACT
_SHARD_ROWS = 128


def _exchange_layout(d):
    out = dict(d)
    out["ffn_w_up"] = _pad_groups(d["ffn_w_up"], 2)
    out["ffn_conv_w"] = _pad_groups(d["ffn_conv_w"], 2)
    out["ffn_w_down"] = _pad_groups(d["ffn_w_down"], 1)
    return out


def _train_step(x, mem, positions, target, w, m, v):
    d_model = x.shape[1]
    we_, me_, ve_ = _exchange_layout(w), _exchange_layout(m), _exchange_layout(v)
    sds = jax.ShapeDtypeStruct

    matrices = _DIRECT_ROWS + ("ffn_w_up", "ffn_w_down")
    layer_mats = ("xa_wq", "xa_wk", "xa_wv", "xa_wo", "ffn_w_up", "ffn_w_down")
    shard16 = {n: we_[n].astype(bf16) for n in matrices}
    part_of = {n: _rows_of(_SHARD_ROWS) for n in _DIRECT_ROWS}
    part_of["ffn_w_up"] = _cols_of(we_["ffn_w_up"].shape[2])
    part_of["ffn_w_down"] = _rows_of(we_["ffn_w_down"].shape[1])
    part_shape = {n: we_[n].shape[1:] for n in matrices}

    def full_shape(n):
        r, c = part_shape[n]
        return (r, N_DEV * c) if n == "ffn_w_up" else (N_DEV * r, c)

    def gather(ex, n, layer):
        ex.add(shard16[n], sds(full_shape(n), bf16), _at_layer(layer), part_of[n], (n, layer))

    def scatter(ex, n, layer, grad):
        ex.add(grad, sds((N_DEV,) + part_shape[n], grad.dtype), part_of[n], _slot, (n, layer))

    small16 = _pack([we_[n] for n in _SMALL16], bf16)
    exact = _pack([we_[n] for n in _EXACT], f32)
    ga, ga1, gb, gc = _Exchange("gather_a"), _Exchange("gather_a1"), _Exchange("gather_b"), _Exchange("gather_c")
    ga.add(small16, sds((N_DEV,) + small16.shape, bf16), _whole, _slot, "small16")
    ga1.add(exact, sds((N_DEV,) + exact.shape, f32), _whole, _slot, "exact")
    gather(ga1, "mix_w_out", 0)
    for n in layer_mats:
        gather(gb, n, 0)
    gather(gc, "s5_w_glu_a", 0)
    gather(gc, "s5_w_glu_b", 0)
    for n in layer_mats:
        gather(gc, n, 1)
    state_a, state_a1, state_b, state_c = ga.begin(True), ga1.begin(True), gb.begin(True), gc.begin(True)

    p = {n: w[n] for n in _REPL_ORDER}
    cos, sin = _rope_tables(positions)
    wo = _odd_weights(p, 0, 1, bf16)
    conv_b = _pad_groups(w["ffn_conv_b"], 1)
    prepared = [cos, sin, conv_b, wo["s5_c3"], wo["s5_bt_re"], wo["s5_bt_im"]]
    full = ga.finish(state_a, [ga1.token, gb.token, gc.token] + prepared)
    for n, a in zip(_SMALL16, _unpack(full["small16"], [we_[n].shape for n in _SMALL16], lead=True)):
        p[n] = _to_full(a, _SHARDED[n])
    we = _even_weights(p, 0, 0, bf16)
    we["norm_mix"] = _after(we["norm_mix"], ga.token, ga1.token, gb.token, gc.token, name="after_gather_starts")

    def late_mix_w_out(mixin):
        full.update(ga1.finish(state_a1, [mixin]))
        return full[("mix_w_out", 0)]

    we["mix_w_out"] = late_mix_w_out
    h, s_mix0, hx = _mixer_fwd(x, cos, sin, we, "l0", w["norm_xa"][0][None])
    conv_w, s5_d = [_to_full(a, _SHARDED[n]) for n, a in
                    zip(_EXACT, _unpack(full["exact"], [we_[n].shape for n in _EXACT], lead=True))]

    def layer_weights(layer):
        return dict(norm_xa=w["norm_xa"][layer][None], norm_mem=w["norm_mem"][layer][None],
                    norm_ffn=w["norm_ffn"][layer][None], xa_q_norm=w["xa_q_norm"][layer][None],
                    xa_k_norm=w["xa_k_norm"][layer][None], ffn_conv_w=conv_w[layer],
                    ffn_conv_b=conv_b[layer][None], **{n: full[(n, layer)] for n in layer_mats})

    full.update(gb.finish(state_b, h))
    wl = [layer_weights(0)]
    h, s_xa0, hf = _xattn_fwd(h, mem, wl[0], "l0", hx)
    h, s_ff0 = _ffn_fwd(h, wl[0], "l0", hf)
    full.update(gc.finish(state_c, h))
    wl.append(layer_weights(1))
    wo.update(s5_d=s5_d, s5_w_glu_a=full[("s5_w_glu_a", 0)], s5_w_glu_b=full[("s5_w_glu_b", 0)])
    h, s_mix1, hx = _s5_fwd(h, wo, "l1", wl[1]["norm_xa"])
    h, s_xa1, hf = _xattn_fwd(h, mem, wl[1], "l1", hx)
    h, s_ff1 = _ffn_fwd(h, wl[1], "l1", hf)
    dh, loss = _rows(_loss_fn, [h, target], [], [(h.shape[1], f32)], accs=[(1, 1)], name="loss_head")

    gl = [{}, {}]
    reduces = []

    own_grad = {}

    def reduce_start(name, entries, dh):
        ex = _Exchange(name)
        for n, layer, grad in entries.get("matrices", ()):
            scatter(ex, n, layer, grad)
            own_grad[(n, layer)] = grad
        for key, src, shape, src_at in entries.get("packs", ()):
            ex.add(src, shape, src_at, _slot, key)
        reduces.append((ex, ex.begin(False)))
        return _after(dh, ex.token, name="after_" + name)

    dh, gl[1] = _ffn_bwd(dh, wl[1], s_ff1, "l1")
    dh = reduce_start("reduce_ffn1", dict(matrices=[(n, 1, gl[1][n]) for n in ("ffn_w_up", "ffn_w_down")]), dh)
    dh, g = _xattn_bwd(dh, mem, wl[1], s_xa1, "l1")
    gl[1].update(g)
    dh = reduce_start("reduce_xa1", dict(matrices=[(n, 1, g[n]) for n in ("xa_wq", "xa_wk", "xa_wv", "xa_wo")]), dh)
    dh, g_odd = _s5_bwd(dh, wo, s_mix1, "l1")
    go = _odd_grads(g_odd)
    dh, gl[0] = _ffn_bwd(dh, wl[0], s_ff0, "l0")
    send_early = _pack([go[n].reshape(w[n].shape) for n in _REPL_EARLY], f32)
    dh = reduce_start("reduce_ffn0", dict(
        matrices=[(n, 0, g_odd[n]) for n in ("s5_w_glu_a", "s5_w_glu_b")]
        + [(n, 0, gl[0][n]) for n in ("ffn_w_up", "ffn_w_down")],
        packs=[("repl_early", send_early, sds((N_DEV,) + send_early.shape, f32), _whole)]), dh)
    dh, g = _xattn_bwd(dh, mem, wl[0], s_xa0, "l0")
    gl[0].update(g)
    dh = reduce_start("reduce_xa0", dict(matrices=[(n, 0, g[n]) for n in ("xa_wq", "xa_wk", "xa_wv", "xa_wo")]), dh)
    grad_x, g_even = _mixer_bwd(
        dh, cos, sin, we, s_mix0, "l0",
        on_w_out=lambda grad, dmixin: reduce_start("reduce_w_out", dict(matrices=[("mix_w_out", 0, grad)]), dmixin))

    ge = _even_grads(g_even)
    cat = lambda n: jnp.concatenate([gl[0][n], gl[1][n]], axis=0)
    rg = dict(ge)
    rg["norm_mix"] = jnp.concatenate([ge["norm_mix"], go["norm_mix"]], axis=0)
    for n in ("norm_xa", "norm_mem", "norm_ffn", "xa_q_norm", "xa_k_norm"):
        rg[n] = cat(n)
    rg["ffn_conv_b"] = _unpad_groups(cat("ffn_conv_b"), 1)
    sg = dict(mla_w_uq=ge["mla_w_uq"], mla_w_ukv=ge["mla_w_ukv"], s5_d=go["s5_d"],
              ffn_conv_w=jnp.stack([gl[0]["ffn_conv_w"], gl[1]["ffn_conv_w"]]))
    send_small = _pack([_to_shards(sg[n], _SHARDED[n]) for n in _SMALL_SHARDED], f32, lead=True)
    send_late = _pack([rg[n].reshape(w[n].shape) for n in _REPL_LATE], f32)
    w_in_rows = w["mix_w_in"].shape[2]
    last = _Exchange("reduce_last")
    last.add(g_even["mix_w_in_t"], sds((N_DEV, w_in_rows, d_model), f32), _rows_of(w_in_rows), _slot, "mix_w_in")
    last.add(send_small, sds(send_small.shape, f32), _slot, _slot, "small")
    last.add(send_late, sds((N_DEV,) + send_late.shape, f32), _whole, _slot, "repl_late")
    state_last = last.begin(False)
    slots = {}
    for ex, state in reduces:
        slots.update(ex.finish(state, [grad_x, last.token]))

    me_index = (4 * lax.axis_index("x") + 2 * lax.axis_index("y") + lax.axis_index("c")).astype(jnp.int32).reshape(1)

    def rows_block(r, c):
        return lambda tr: ((tr, c), lambda i, me: (me[0] * (r // tr) + i, 0))

    def own_block(n):
        r, c = part_shape[n]
        if n == "ffn_w_up":
            return lambda tr: ((tr, c), lambda i, me: (i, me[0]))
        return rows_block(r, c)

    out = [{}, {}, {}, {}]
    unpad = dict(ffn_w_up=2, ffn_conv_w=2, ffn_w_down=1)
    for n in matrices:
        layers = range(we_[n].shape[0])
        res = _sum_adam(me_index, [slots[(n, layer)] for layer in layers], [own_grad[(n, layer)] for layer in layers],
                        own_block(n), we_[n], me_[n], ve_[n], name=f"adam_{n}")
        for k in range(4):
            out[k][n] = _unpad_groups(res[k], unpad[n]) if n in unpad else res[k]
    pk = lambda d, order: _pack([d[n] for n in order], f32)[None]
    whole_rows = lambda tr: ((tr, _LANES), lambda i, me: (i, 0))
    res_early = _sum_adam(me_index, [slots["repl_early"]], [send_early], whole_rows, pk(w, _REPL_EARLY),
                          pk(m, _REPL_EARLY), pk(v, _REPL_EARLY), name="adam_repl_early")
    for k in range(4):
        out[k].update(zip(_REPL_EARLY, _unpack(res_early[k][0], [w[n].shape for n in _REPL_EARLY])))
    done = [out[k][n] for k in range(4) for n in matrices + _REPL_EARLY]
    slots = last.finish(state_last, done)
    transposed = lambda d: jnp.swapaxes(d["mix_w_in"], 1, 2)
    res_w_in = _sum_adam(me_index, [slots["mix_w_in"]], [g_even["mix_w_in_t"]], rows_block(w_in_rows, d_model),
                         transposed(w), transposed(m), transposed(v), name="adam_mix_w_in")
    res_small = _sum_adam(me_index, [slots["small"]], [send_small],
                          lambda tr: ((1, tr, _LANES), lambda i, me: (me[0], i, 0)),
                          pk(we_, _SMALL_SHARDED), pk(me_, _SMALL_SHARDED), pk(ve_, _SMALL_SHARDED), name="adam_small")
    res_late = _sum_adam(me_index, [slots["repl_late"]], [send_late], whole_rows, pk(w, _REPL_LATE), pk(m, _REPL_LATE),
                         pk(v, _REPL_LATE), name="adam_repl_late")
    for k in range(4):
        out[k]["mix_w_in"] = jnp.swapaxes(res_w_in[k], 1, 2)
        for n, a in zip(_SMALL_SHARDED, _unpack(res_small[k][0], [we_[n].shape for n in _SMALL_SHARDED])):
            out[k][n] = _unpad_groups(a, unpad[n]) if n in unpad else a
        out[k].update(zip(_REPL_LATE, _unpack(res_late[k][0], [w[n].shape for n in _REPL_LATE])))
    return loss, grad_x, out


_INPUTS = tuple("""x, mem, positions, norm_mix, norm_xa, norm_mem, norm_ffn, xa_wq, xa_wk, xa_wv, xa_wo, xa_q_norm, xa_k_norm, ffn_w_up, ffn_conv_w, ffn_conv_b, ffn_w_down, hg_lb_logits, mix_w_in, hg_out_norm, mla_q_a_norm, mla_w_uq, mla_kv_a_norm, mla_w_ukv, mla_qn_nope, mla_qn_rope, mla_kn_nope, mla_kn_rope, mix_w_out, s5_lam_re, s5_lam_im, s5_log_dt, s5_b_re, s5_b_im, s5_c_re, s5_c_im, s5_d, s5_w_glu_a, s5_w_glu_b, loss_target, m_norm_mix, m_norm_xa, m_norm_mem, m_norm_ffn, m_xa_wq, m_xa_wk, m_xa_wv, m_xa_wo, m_xa_q_norm, m_xa_k_norm, m_ffn_w_up, m_ffn_conv_w, m_ffn_conv_b, m_ffn_w_down, m_hg_lb_logits, m_mix_w_in, m_hg_out_norm, m_mla_q_a_norm, m_mla_w_uq, m_mla_kv_a_norm, m_mla_w_ukv, m_mla_qn_nope, m_mla_qn_rope, m_mla_kn_nope, m_mla_kn_rope, m_mix_w_out, m_s5_lam_re, m_s5_lam_im, m_s5_log_dt, m_s5_b_re, m_s5_b_im, m_s5_c_re, m_s5_c_im, m_s5_d, m_s5_w_glu_a, m_s5_w_glu_b, v_norm_mix, v_norm_xa, v_norm_mem, v_norm_ffn, v_xa_wq, v_xa_wk, v_xa_wv, v_xa_wo, v_xa_q_norm, v_xa_k_norm, v_ffn_w_up, v_ffn_conv_w, v_ffn_conv_b, v_ffn_w_down, v_hg_lb_logits, v_mix_w_in, v_hg_out_norm, v_mla_q_a_norm, v_mla_w_uq, v_mla_kv_a_norm, v_mla_w_ukv, v_mla_qn_nope, v_mla_qn_rope, v_mla_kn_nope, v_mla_kn_rope, v_mix_w_out, v_s5_lam_re, v_s5_lam_im, v_s5_log_dt, v_s5_b_re, v_s5_b_im, v_s5_c_re, v_s5_c_im, v_s5_d, v_s5_w_glu_a, v_s5_w_glu_b""".replace(" ", "").split(","))


def kernel(x, mem, positions, norm_mix, norm_xa, norm_mem, norm_ffn, xa_wq, xa_wk, xa_wv, xa_wo, xa_q_norm, xa_k_norm, ffn_w_up, ffn_conv_w, ffn_conv_b, ffn_w_down, hg_lb_logits, mix_w_in, hg_out_norm, mla_q_a_norm, mla_w_uq, mla_kv_a_norm, mla_w_ukv, mla_qn_nope, mla_qn_rope, mla_kn_nope, mla_kn_rope, mix_w_out, s5_lam_re, s5_lam_im, s5_log_dt, s5_b_re, s5_b_im, s5_c_re, s5_c_im, s5_d, s5_w_glu_a, s5_w_glu_b, loss_target, m_norm_mix, m_norm_xa, m_norm_mem, m_norm_ffn, m_xa_wq, m_xa_wk, m_xa_wv, m_xa_wo, m_xa_q_norm, m_xa_k_norm, m_ffn_w_up, m_ffn_conv_w, m_ffn_conv_b, m_ffn_w_down, m_hg_lb_logits, m_mix_w_in, m_hg_out_norm, m_mla_q_a_norm, m_mla_w_uq, m_mla_kv_a_norm, m_mla_w_ukv, m_mla_qn_nope, m_mla_qn_rope, m_mla_kn_nope, m_mla_kn_rope, m_mix_w_out, m_s5_lam_re, m_s5_lam_im, m_s5_log_dt, m_s5_b_re, m_s5_b_im, m_s5_c_re, m_s5_c_im, m_s5_d, m_s5_w_glu_a, m_s5_w_glu_b, v_norm_mix, v_norm_xa, v_norm_mem, v_norm_ffn, v_xa_wq, v_xa_wk, v_xa_wv, v_xa_wo, v_xa_q_norm, v_xa_k_norm, v_ffn_w_up, v_ffn_conv_w, v_ffn_conv_b, v_ffn_w_down, v_hg_lb_logits, v_mix_w_in, v_hg_out_norm, v_mla_q_a_norm, v_mla_w_uq, v_mla_kv_a_norm, v_mla_w_ukv, v_mla_qn_nope, v_mla_qn_rope, v_mla_kn_nope, v_mla_kn_rope, v_mix_w_out, v_s5_lam_re, v_s5_lam_im, v_s5_log_dt, v_s5_b_re, v_s5_b_im, v_s5_c_re, v_s5_c_im, v_s5_d, v_s5_w_glu_a, v_s5_w_glu_b):
    vals = dict(zip(_INPUTS, (x, mem, positions, norm_mix, norm_xa, norm_mem, norm_ffn, xa_wq, xa_wk, xa_wv, xa_wo, xa_q_norm, xa_k_norm, ffn_w_up, ffn_conv_w, ffn_conv_b, ffn_w_down, hg_lb_logits, mix_w_in, hg_out_norm, mla_q_a_norm, mla_w_uq, mla_kv_a_norm, mla_w_ukv, mla_qn_nope, mla_qn_rope, mla_kn_nope, mla_kn_rope, mix_w_out, s5_lam_re, s5_lam_im, s5_log_dt, s5_b_re, s5_b_im, s5_c_re, s5_c_im, s5_d, s5_w_glu_a, s5_w_glu_b, loss_target, m_norm_mix, m_norm_xa, m_norm_mem, m_norm_ffn, m_xa_wq, m_xa_wk, m_xa_wv, m_xa_wo, m_xa_q_norm, m_xa_k_norm, m_ffn_w_up, m_ffn_conv_w, m_ffn_conv_b, m_ffn_w_down, m_hg_lb_logits, m_mix_w_in, m_hg_out_norm, m_mla_q_a_norm, m_mla_w_uq, m_mla_kv_a_norm, m_mla_w_ukv, m_mla_qn_nope, m_mla_qn_rope, m_mla_kn_nope, m_mla_kn_rope, m_mix_w_out, m_s5_lam_re, m_s5_lam_im, m_s5_log_dt, m_s5_b_re, m_s5_b_im, m_s5_c_re, m_s5_c_im, m_s5_d, m_s5_w_glu_a, m_s5_w_glu_b, v_norm_mix, v_norm_xa, v_norm_mem, v_norm_ffn, v_xa_wq, v_xa_wk, v_xa_wv, v_xa_wo, v_xa_q_norm, v_xa_k_norm, v_ffn_w_up, v_ffn_conv_w, v_ffn_conv_b, v_ffn_w_down, v_hg_lb_logits, v_mix_w_in, v_hg_out_norm, v_mla_q_a_norm, v_mla_w_uq, v_mla_kv_a_norm, v_mla_w_ukv, v_mla_qn_nope, v_mla_qn_rope, v_mla_kn_nope, v_mla_kn_rope, v_mix_w_out, v_s5_lam_re, v_s5_lam_im, v_s5_log_dt, v_s5_b_re, v_s5_b_im, v_s5_c_re, v_s5_c_im, v_s5_d, v_s5_w_glu_a, v_s5_w_glu_b)))
    w = {n: vals[n] for n in _WEIGHTS}
    m = {n: vals["m_" + n] for n in _WEIGHTS}
    v = {n: vals["v_" + n] for n in _WEIGHTS}
    loss, grad_x, res = _train_step(vals["x"][0], vals["mem"][0], vals["positions"][0], vals["loss_target"][0],
                                    w, m, v)
    loss = lax.psum(loss[0, 0], ("x", "y", "c"))
    return (loss, grad_x[None], *[r[n] for r in res for n in _WEIGHTS])
```

```python
import functools

import jax
import jax.numpy as jnp
import numpy as np
from jax import lax
from jax.experimental import pallas as pl
from jax.experimental.pallas import tpu as pltpu

f32 = jnp.float32
bf16 = jnp.bfloat16

EPS = 1e-6
N_DEV = 8
VMEM_LIMIT = 52 * 1024 * 1024

HG_HEADS = 4
HG_DIM = 128
HG_WIDTH = HG_HEADS * HG_DIM
HG_CHUNK = 64
HG_SUB = 16
MLA_HEADS = 4
MLA_Q_RANK = 256
MLA_KV_RANK = 128
MLA_NOPE = 128
MLA_ROPE = 64
MLA_V = 128
MLA_QK = MLA_NOPE + MLA_ROPE
MLA_QK_PAD = 256
ROPE_BASE = 10000.0
IN_WIDTH = 4 * HG_WIDTH + MLA_Q_RANK + MLA_KV_RANK + MLA_ROPE
IN_PAD = 2560
XA_HEADS = 4
XA_DIM = 256
S5_GROUP = 16
S5_GROUPS = 64
S5_STATE = 64
CONV_W = 3

ADAM_LR = 0.001
ADAM_B1 = 0.9
ADAM_B2 = 0.999
ADAM_EPS = 1e-08
ADAM_WD = 0.01
ADAM_STEP = 10

_NT = (((1,), (1,)), ((), ()))
_TN = (((0,), (0,)), ((), ()))
_NN = (((1,), (0,)), ((), ()))


def _pick(n, cands):
    for c in cands:
        if n % c == 0:
            return c
    return n


def _cparams(sem):
    return pltpu.CompilerParams(dimension_semantics=sem, vmem_limit_bytes=VMEM_LIMIT)


_MM_BUDGET = 36 * 1024 * 1024
_MM_TILES = ((1024, 1024), (1024, 512), (512, 1024), (512, 512), (512, 256), (256, 512), (256, 256), (256, 128),
             (128, 256), (128, 128))


def _mm(a, b, *, name, ta=False, tb=False, out_dtype=f32, add=None, b2=None, kslab=None, norm=None, rms_bwd=None,
        sq_err=None):
    m, k = (a.shape[1], a.shape[0]) if ta else a.shape
    nb = b.shape[0] if tb else b.shape[1]
    n = nb * (2 if b2 is not None else 1)
    slab, nslab = kslab if kslab is not None else (0, 1)
    assert (b.shape[1] // nslab if tb else b.shape[0]) == k, (a.shape, b.shape, ta, tb)
    assert b2 is None or (not tb and b2.shape == b.shape)
    full_rows = norm is not None or rms_bwd is not None
    assert not (full_rows and b2 is not None) and not (norm is not None and rms_bwd is not None)
    isz = lambda x: jnp.dtype(x.dtype).itemsize
    bm = bn = None
    for cm, cn in _MM_TILES:
        if m % cm or nb % cn or (full_rows and cn != n):
            continue
        need = 2 * (cm * k * isz(a) + cn * k * isz(b) * (2 if b2 is not None else 1)
                    + cm * cn * (jnp.dtype(out_dtype).itemsize + (4 if add is not None else 0)
                                 + (2 if norm is not None else 0) + (8 if rms_bwd is not None else 0)
                                 + (4 if sq_err is not None else 0)))
        if need <= _MM_BUDGET:
            bm, bn = cm, cn
            break
    assert bm is not None, (name, a.shape, b.shape)
    half = nb // bn
    dims = (((0 if ta else 1,), (1 if tb else 0,)), ((), ()))

    def body(*refs):
        refs = list(refs)
        a_ref, b_ref = refs[0], refs[1]
        b2_ref = refs.pop(2) if b2 is not None else None
        add_ref = refs.pop(2) if add is not None else None
        gain_ref = refs.pop(2) if norm is not None else None
        x_ref, xgain_ref, through_ref = (refs.pop(2), refs.pop(2), refs.pop(2)) if rms_bwd is not None else (None,) * 3
        target_ref = refs.pop(2) if sq_err is not None else None
        o_ref = refs[2]
        extra_ref = refs[3] if (norm is not None or rms_bwd is not None or sq_err is not None) else None

        def run(rhs_ref):
            r = lax.dot_general(a_ref[...].astype(bf16), rhs_ref[...].astype(bf16), dims, preferred_element_type=f32)
            if add_ref is not None:
                r = r + add_ref[...].astype(f32)
            if sq_err is not None:
                err = r - target_ref[...]
                o_ref[...] = (err * (1.0 / n)).astype(o_ref.dtype)

                @pl.when((pl.program_id(0) == 0) & (pl.program_id(1) == 0))
                def _():
                    extra_ref[...] = jnp.zeros_like(extra_ref)
                extra_ref[...] += jnp.sum(jnp.sum(err * err, axis=-1, keepdims=True), axis=0, keepdims=True) * (0.5 / n)
            elif rms_bwd is None:
                o_ref[...] = r.astype(o_ref.dtype)
            else:
                _, vjp = jax.vjp(_rms, x_ref[...], xgain_ref[...])
                dx, dgain = vjp(r)
                o_ref[...] = (through_ref[...] + dx).astype(o_ref.dtype)

                @pl.when(pl.program_id(0) == 0)
                def _():
                    extra_ref[...] = jnp.zeros_like(extra_ref)
                extra_ref[...] += dgain
            if norm is not None:
                extra_ref[...] = _rms(r, gain_ref[...]).astype(extra_ref.dtype)

        if b2_ref is None:
            run(b_ref)
        else:
            pl.when(pl.program_id(1) < half)(lambda: run(b_ref))
            pl.when(pl.program_id(1) >= half)(lambda: run(b2_ref))

    a_spec = pl.BlockSpec((k, bm), lambda i, j: (0, i)) if ta else pl.BlockSpec((bm, k), lambda i, j: (i, 0))
    if tb:
        b_spec = pl.BlockSpec((bn, k), lambda i, j: (j, slab))
    elif b2 is None:
        b_spec = pl.BlockSpec((k, bn), lambda i, j: (0, j))
    else:
        b_spec = pl.BlockSpec((k, bn), lambda i, j: (0, jnp.minimum(j, half - 1)))
    in_specs = [a_spec, b_spec]
    args = [a, b]
    if b2 is not None:
        in_specs.append(pl.BlockSpec((k, bn), lambda i, j: (0, jnp.maximum(j - half, 0))))
        args.append(b2)
    if add is not None:
        in_specs.append(pl.BlockSpec((bm, bn), lambda i, j: (i, j)))
        args.append(add)
    out_blk = pl.BlockSpec((bm, bn), lambda i, j: (i, j))
    out_specs, out_shape = out_blk, jax.ShapeDtypeStruct((m, n), out_dtype)
    row_blk = pl.BlockSpec((1, bn), lambda i, j: (0, j))
    if norm is not None:
        in_specs.append(row_blk)
        args.append(norm)
        out_specs, out_shape = [out_blk, out_blk], [out_shape, jax.ShapeDtypeStruct((m, n), bf16)]
    if rms_bwd is not None:
        in_specs += [out_blk, row_blk, out_blk]
        args += list(rms_bwd)
        out_specs, out_shape = [out_blk, row_blk], [out_shape, jax.ShapeDtypeStruct((1, n), f32)]
    if sq_err is not None:
        in_specs.append(out_blk)
        args.append(sq_err)
        out_specs = [out_blk, pl.BlockSpec((1, 1), lambda i, j: (0, 0))]
        out_shape = [out_shape, jax.ShapeDtypeStruct((1, 1), f32)]
    carried = rms_bwd is not None or sq_err is not None
    return pl.pallas_call(
        body, grid=(m // bm, n // bn), in_specs=in_specs, out_specs=out_specs, out_shape=out_shape,
        compiler_params=_cparams(("arbitrary" if carried else "parallel", "arbitrary" if sq_err is not None else "parallel")),
        name=name)(*args)


def _as_tuple(x):
    return tuple(x) if isinstance(x, (tuple, list)) else (x,)


def _full_spec(p):
    nd = p.ndim
    return pl.BlockSpec(p.shape, lambda i, _nd=nd: (0,) * _nd)


def _window(a, start, width):
    assert start % width == 0 and width % 128 == 0
    return (a, start // width, width)


def _row_array(x):
    return x[0] if isinstance(x, tuple) else x


def _row_shape(x):
    return (x[0].shape[0], x[2]) if isinstance(x, tuple) else x.shape


def _row_spec(x, tile):
    if isinstance(x, tuple):
        return pl.BlockSpec((tile, x[2]), lambda i, _b=x[1]: (i, _b))
    return pl.BlockSpec((tile, x.shape[1]), lambda i: (i, 0))


def _rows(fn, rows, params, outs, *, name, tile=256, accs=()):
    length = _row_shape(rows[0])[0]
    tile = min(tile, length)
    nr, npar, no = len(rows), len(params), len(outs)

    def body(*refs):
        r, p, o = refs[:nr], refs[nr:nr + npar], refs[nr + npar:]
        res = _as_tuple(fn(*[x[...].astype(f32) for x in r], *[x[...] for x in p]))
        for kk in range(no):
            o[kk][...] = res[kk].astype(o[kk].dtype)
        if accs:
            @pl.when(pl.program_id(0) == 0)
            def _():
                for kk in range(no, no + len(accs)):
                    o[kk][...] = jnp.zeros_like(o[kk])
            for kk in range(no, no + len(accs)):
                o[kk][...] += res[kk]

    in_specs = [_row_spec(x, tile) for x in rows] + [_full_spec(p) for p in params]
    out_specs = [pl.BlockSpec((tile, w), lambda i: (i, 0)) for w, _ in outs]
    out_shape = [jax.ShapeDtypeStruct((length, w), d) for w, d in outs]
    for s in accs:
        out_specs.append(pl.BlockSpec(s, lambda i, _nd=len(s): (0,) * _nd))
        out_shape.append(jax.ShapeDtypeStruct(s, f32))
    res = pl.pallas_call(body, grid=(length // tile,), in_specs=in_specs, out_specs=out_specs, out_shape=out_shape,
                         compiler_params=_cparams(("arbitrary",)), name=name)(*[_row_array(x) for x in rows], *params)
    return res


def _rows_bwd(fn, rows, params, cts, *, name, rgrad, pgrad, tile=256, addends=None):
    addends = {i: (a if isinstance(a, list) else [(a, 0)]) for i, a in (addends or {}).items()}
    length = _row_shape(rows[0])[0]
    tile = min(tile, length)
    nr, npar, nc = len(rows), len(params), len(cts)
    ridx = [i for i in range(nr) if rgrad[i] is not None]
    pidx = [i for i in range(npar) if pgrad[i]]
    flat_addends = [(i, a, off) for i in sorted(addends) for a, off in addends[i]]
    na = len(flat_addends)

    def body(*refs):
        r, p, c = refs[:nr], refs[nr:nr + npar], refs[nr + npar:nr + npar + nc]
        ad = refs[nr + npar + nc:nr + npar + nc + na]
        o = refs[nr + npar + nc + na:]
        rv = [x[...].astype(f32) for x in r]
        pv = [x[...] for x in p]
        cv = tuple(x[...].astype(f32) for x in c)

        def g(*d):
            rr, pp = list(rv), list(pv)
            for n_, i_ in enumerate(ridx):
                rr[i_] = d[n_]
            for n_, i_ in enumerate(pidx):
                pp[i_] = d[len(ridx) + n_]
            return _as_tuple(fn(*rr, *pp))

        _, vjp = jax.vjp(g, *[rv[i] for i in ridx], *[pv[i] for i in pidx])
        grads = vjp(cv)
        for n_, i_ in enumerate(ridx):
            val = grads[n_]
            for k_, (j_, a_, off) in enumerate(flat_addends):
                if j_ == i_:
                    extra = ad[k_][...].astype(f32)
                    if extra.shape[1] != val.shape[1]:
                        extra = jnp.pad(extra, ((0, 0), (off, val.shape[1] - off - extra.shape[1])))
                    val = val + extra
            o[n_][...] = val.astype(o[n_].dtype)
        if pidx:
            @pl.when(pl.program_id(0) == 0)
            def _():
                for n_ in range(len(pidx)):
                    o[len(ridx) + n_][...] = jnp.zeros_like(o[len(ridx) + n_])
            for n_ in range(len(pidx)):
                o[len(ridx) + n_][...] += grads[len(ridx) + n_]

    plain = lambda shape: pl.BlockSpec((tile, shape[1]), lambda i: (i, 0))
    in_specs = ([_row_spec(x, tile) for x in rows] + [_full_spec(p) for p in params] + [plain(x.shape) for x in cts]
                + [plain(a.shape) for _, a, _ in flat_addends])
    out_specs = [plain(_row_shape(rows[i])) for i in ridx] + [_full_spec(params[i]) for i in pidx]
    out_shape = ([jax.ShapeDtypeStruct(_row_shape(rows[i]), rgrad[i]) for i in ridx]
                 + [jax.ShapeDtypeStruct(params[i].shape, f32) for i in pidx])
    res = pl.pallas_call(body, grid=(length // tile,), in_specs=in_specs, out_specs=out_specs, out_shape=out_shape,
                         compiler_params=_cparams(("arbitrary",)), name=name)(
        *[_row_array(x) for x in rows], *params, *cts, *[a for _, a, _ in flat_addends])
    return list(res[:len(ridx)]), list(res[len(ridx):])


def _rms(x, g):
    return x * lax.rsqrt(jnp.mean(x * x, axis=-1, keepdims=True) + EPS) * g


def _rms_twice(x, g):
    y = _rms(x, g)
    return y, y


def _silu(x):
    return x * jax.nn.sigmoid(x)


_UP_COLS = 512


def _ffn_up_conv(hf, w_up, cw, cb, *, name):
    length, d = hf.shape
    ff = w_up.shape[1] // 2
    bm = _pick(length, (1024, 512, 256))
    bn = _UP_COLS
    nj = ff // bn

    def body(hf_ref, wg_ref, wv_ref, cwg, cwv, cbg, cbv, ug_ref, uv_ref, a_ref, halo_g, halo_v):
        i, j = pl.program_id(0), pl.program_id(1)
        x = hf_ref[...]
        rows = lax.broadcasted_iota(jnp.int32, (bm, bn), 0)
        pad = jnp.zeros((bm - 8, bn), f32)

        def half(w_ref, cw_ref, cb_ref, u_ref, halo):
            u = jnp.dot(x, w_ref[...], preferred_element_type=f32).astype(bf16)
            u_ref[...] = u
            u = u.astype(f32)
            prev = jnp.where(i == 0, 0.0, halo[j])
            x1 = jnp.where(rows >= 1, pltpu.roll(u, 1, axis=0), jnp.concatenate([pltpu.roll(prev, 1, axis=0), pad]))
            x2 = jnp.where(rows >= 2, pltpu.roll(u, 2, axis=0), jnp.concatenate([pltpu.roll(prev, 2, axis=0), pad]))
            halo[j] = u[bm - 8:, :]
            return cw_ref[2:3, :] * u + cw_ref[1:2, :] * x1 + cw_ref[0:1, :] * x2 + cb_ref[...]

        g = half(wg_ref, cwg, cbg, ug_ref, halo_g)
        v = half(wv_ref, cwv, cbv, uv_ref, halo_v)
        a_ref[...] = (_silu(g) * v).astype(a_ref.dtype)

    col = lambda r, off: pl.BlockSpec((r, bn), lambda i, j, _o=off: (0, j + _o))
    out_blk = pl.BlockSpec((bm, bn), lambda i, j: (i, j))
    sds = jax.ShapeDtypeStruct((length, ff), bf16)
    return pl.pallas_call(
        body, grid=(length // bm, nj),
        in_specs=[pl.BlockSpec((bm, d), lambda i, j: (i, 0)), col(d, 0), col(d, nj), col(CONV_W, 0), col(CONV_W, nj),
                  col(1, 0), col(1, nj)],
        out_specs=[out_blk] * 3, out_shape=[sds] * 3,
        scratch_shapes=[pltpu.VMEM((nj, 8, bn), f32), pltpu.VMEM((nj, 8, bn), f32)],
        compiler_params=_cparams(("arbitrary", "arbitrary")), name=name)(hf, w_up, w_up, cw, cw, cb, cb)


def _ffn_da_dconv(dout, w_down, u_g, u_v, cw, cb, *, name):
    length, d = dout.shape
    ff = u_g.shape[1]
    bm = _pick(length, (1024, 512, 256))
    bn = _UP_COLS
    nj, ni = ff // bn, length // bm

    def body(dout_ref, wd_ref, ug_ref, uv_ref, pg_ref, pv_ref, cwg, cwv, cbg, cbv,
             dug_ref, duv_ref, sums_g_ref, sums_v_ref, halo_g, halo_v, acc_g, acc_v):
        s, j = pl.program_id(0), pl.program_id(1)
        rows = lax.broadcasted_iota(jnp.int32, (bm, bn), 0)
        row8 = lax.broadcasted_iota(jnp.int32, (8, bn), 0)
        pad = jnp.zeros((bm - 8, bn), f32)
        da = lax.dot_general(dout_ref[...].astype(bf16), wd_ref[...], _NT, preferred_element_type=f32)

        def conv(u_ref, p_ref, cw_ref, cb_ref):
            x = u_ref[...].astype(f32)
            prev = jnp.where(s == ni - 1, 0.0, p_ref[...].astype(f32))
            x1 = jnp.where(rows >= 1, pltpu.roll(x, 1, axis=0), jnp.concatenate([pltpu.roll(prev, 1, axis=0), pad]))
            x2 = jnp.where(rows >= 2, pltpu.roll(x, 2, axis=0), jnp.concatenate([pltpu.roll(prev, 2, axis=0), pad]))
            return cw_ref[2:3, :] * x + cw_ref[1:2, :] * x1 + cw_ref[0:1, :] * x2 + cb_ref[...], x, x1, x2

        g, xg, xg1, xg2 = conv(ug_ref, pg_ref, cwg, cbg)
        v, xv, xv1, xv2 = conv(uv_ref, pv_ref, cwv, cbv)
        sg = jax.nn.sigmoid(g)
        dg = da * v * (sg * (1.0 + g * (1.0 - sg)))
        dv = da * (g * sg)

        def back(dy, x, x1, x2, cw_ref, du_ref, sums_ref, halo, acc):
            nxt = jnp.where(s == 0, 0.0, halo[j])
            up1 = jnp.where(rows < bm - 1, pltpu.roll(dy, bm - 1, axis=0),
                            jnp.concatenate([pad, pltpu.roll(nxt, 7, axis=0)]))
            up2 = jnp.where(rows < bm - 2, pltpu.roll(dy, bm - 2, axis=0),
                            jnp.concatenate([pad, pltpu.roll(nxt, 6, axis=0)]))
            halo[j] = dy[:8, :]
            du_ref[...] = (cw_ref[2:3, :] * dy + cw_ref[1:2, :] * up1 + cw_ref[0:1, :] * up2).astype(du_ref.dtype)
            col = lambda t: jnp.sum(t, axis=0, keepdims=True)
            part = jnp.where(row8 == 0, col(dy * x2), jnp.where(row8 == 1, col(dy * x1), jnp.where(
                row8 == 2, col(dy * x), jnp.where(row8 == 3, col(dy), 0.0))))
            total = jnp.where(s == 0, part, acc[j] + part)
            acc[j] = total
            sums_ref[...] = total

        back(dg, xg, xg1, xg2, cwg, dug_ref, sums_g_ref, halo_g, acc_g)
        back(dv, xv, xv1, xv2, cwv, duv_ref, sums_v_ref, halo_v, acc_v)

    rb = lambda s: ni - 1 - s
    tile = pl.BlockSpec((bm, bn), lambda s, j: (rb(s), j))
    before = pl.BlockSpec((8, bn), lambda s, j: (jnp.maximum(rb(s) * (bm // 8) - 1, 0), j))
    col = lambda r, off: pl.BlockSpec((r, bn), lambda s, j, _o=off: (0, j + _o))
    sds = jax.ShapeDtypeStruct
    dug, duv, sums_g, sums_v = pl.pallas_call(
        body, grid=(ni, nj),
        in_specs=[pl.BlockSpec((bm, d), lambda s, j: (rb(s), 0)), pl.BlockSpec((bn, d), lambda s, j: (j, 0)),
                  tile, tile, before, before, col(CONV_W, 0), col(CONV_W, nj), col(1, 0), col(1, nj)],
        out_specs=[tile, tile] + [pl.BlockSpec((8, bn), lambda s, j: (s, j))] * 2,
        out_shape=[sds((length, ff), bf16), sds((length, ff), bf16), sds((ni * 8, ff), f32), sds((ni * 8, ff), f32)],
        scratch_shapes=[pltpu.VMEM((nj, 8, bn), f32)] * 4,
        compiler_params=_cparams(("arbitrary", "arbitrary")), name=name)(
        dout, w_down, u_g, u_v, u_g, u_v, cw, cw, cb, cb)
    last = (ni - 1) * 8
    both = lambda lo, hi: jnp.concatenate([sums_g[last + lo:last + hi], sums_v[last + lo:last + hi]], axis=1)
    return dug, duv, both(0, CONV_W), both(CONV_W, CONV_W + 1)


def _ffn_fwd(h, w, tag, hf=None, loss_target=None):
    if hf is None:
        hf, = _rows(_rms, [h], [w["norm_ffn"]], [(h.shape[1], bf16)], name=f"ffn_norm_{tag}")
    u_g, u_v, a = _ffn_up_conv(hf, w["ffn_w_up"], w["ffn_conv_w"], w["ffn_conv_b"], name=f"ffn_up_{tag}")
    out = _mm(a, w["ffn_w_down"], add=h, sq_err=loss_target, name=f"ffn_down_{tag}")
    return out, (h, hf, u_g, u_v, a)


def _ffn_bwd(dout, w, saved, tag):
    h, hf, u_g, u_v, a = saved
    g = {"ffn_w_down": _mm(a, dout, ta=True, out_dtype=bf16, name=f"ffn_dwdown_{tag}")}
    dug, duv, g["ffn_conv_w"], g["ffn_conv_b"] = _ffn_da_dconv(
        dout, w["ffn_w_down"], u_g, u_v, w["ffn_conv_w"], w["ffn_conv_b"], name=f"ffn_dconv_{tag}")
    dhf = _mm(dug, w["ffn_w_up"], tb=True, kslab=(0, 2), name=f"ffn_dhf_g_{tag}")
    dh, g["norm_ffn"] = _mm(duv, w["ffn_w_up"], tb=True, kslab=(1, 2), add=dhf, rms_bwd=(h, w["norm_ffn"], dout),
                            name=f"ffn_dhf_v_{tag}")
    g["ffn_w_up"] = _mm(hf, dug, ta=True, b2=duv, out_dtype=bf16, name=f"ffn_dwup_{tag}")
    return dh, g


def _xattn_fn(qx, kx, vx, qg, kg):
    outs = []
    for hh in range(XA_HEADS):
        sl = slice(hh * XA_DIM, (hh + 1) * XA_DIM)
        q = _rms(qx[:, sl], qg).astype(bf16)
        k = _rms(kx[:, sl], kg).astype(bf16)
        s = lax.dot_general(q, k, _NT, preferred_element_type=f32) * (XA_DIM ** -0.5)
        s = s - jnp.max(s, axis=-1, keepdims=True)
        p = jnp.exp(s)
        p = p / jnp.sum(p, axis=-1, keepdims=True)
        outs.append(jnp.dot(p.astype(bf16), vx[:, sl].astype(bf16), preferred_element_type=f32))
    return jnp.concatenate(outs, axis=-1)


def _xattn_fwd(h, mem, w, tag, hx=None):
    d = h.shape[1]
    if hx is None:
        hx, = _rows(_rms, [h], [w["norm_xa"]], [(d, bf16)], name=f"xa_norm_{tag}")
    qx = _mm(hx, w["xa_wq"], name=f"xa_q_{tag}")
    m, = _rows(_rms, [mem], [w["norm_mem"]], [(d, bf16)], name=f"xa_mnorm_{tag}")
    kx = _mm(m, w["xa_wk"], name=f"xa_k_{tag}")
    vx = _mm(m, w["xa_wv"], name=f"xa_v_{tag}")
    o, = _rows(_xattn_fn, [qx], [kx, vx, w["xa_q_norm"], w["xa_k_norm"]], [(d, bf16)], tile=1024,
               name=f"xa_attn_{tag}")
    out, hf = _mm(o, w["xa_wo"], add=h, norm=w["norm_ffn"], name=f"xa_o_{tag}")
    return out, (h, hx, qx, m, kx, vx, o), hf


def _xattn_bwd(dout, mem, w, saved, tag):
    h, hx, qx, m, kx, vx, o = saved
    g = {}
    do = _mm(dout, w["xa_wo"], tb=True, out_dtype=bf16, name=f"xa_do_{tag}")
    g["xa_wo"] = _mm(o, dout, ta=True, out_dtype=bf16, name=f"xa_dwo_{tag}")
    (dqx,), (dkx, dvx, g["xa_q_norm"], g["xa_k_norm"]) = _rows_bwd(
        _xattn_fn, [qx], [kx, vx, w["xa_q_norm"], w["xa_k_norm"]], [do], rgrad=[bf16], pgrad=[True] * 4,
        tile=1024, name=f"xa_dattn_{tag}")
    dh, g["norm_xa"] = _mm(dqx, w["xa_wq"], tb=True, rms_bwd=(h, w["norm_xa"], dout), name=f"xa_dhx_{tag}")
    g["xa_wq"] = _mm(hx, dqx, ta=True, out_dtype=bf16, name=f"xa_dwq_{tag}")
    dm = _mm(dkx, w["xa_wk"], tb=True, name=f"xa_dm_k_{tag}")
    dm = _mm(dvx, w["xa_wv"], tb=True, add=dm, name=f"xa_dm_v_{tag}")
    g["xa_wk"] = _mm(m, dkx, ta=True, out_dtype=bf16, name=f"xa_dwk_{tag}")
    g["xa_wv"] = _mm(m, dvx, ta=True, out_dtype=bf16, name=f"xa_dwv_{tag}")
    _, (g["norm_mem"],) = _rows_bwd(_rms, [mem], [w["norm_mem"]], [dm], rgrad=[None], pgrad=[True],
                                    name=f"xa_dmnorm_{tag}")
    return dh, g


_HG_GROUP = 8


def _hg_chunk(q, k, v, g, *sts):
    c = q.shape[0]
    heads = [slice(h * HG_DIM, (h + 1) * HG_DIM) for h in range(len(sts))]
    tri = (lax.broadcasted_iota(jnp.int32, (c, c), 0) >= lax.broadcasted_iota(jnp.int32, (c, c), 1)).astype(f32)
    b = jnp.dot(tri, g, precision=lax.Precision.HIGHEST, preferred_element_type=f32)
    bend = jnp.sum(g, axis=0, keepdims=True)
    qe = (q * jnp.exp(b)).astype(bf16)
    kd = (k * jnp.exp(bend - b)).astype(bf16)
    vb = v.astype(bf16)
    decay = jnp.exp(bend)
    o_inter = [lax.dot_general(qe[:, hs], st.astype(bf16), _NT, preferred_element_type=f32) for hs, st in zip(heads, sts)]
    new = [st * decay[:, hs] + lax.dot_general(vb[:, hs], kd[:, hs], _TN, preferred_element_type=f32)
           for hs, st in zip(heads, sts)]
    outs = []
    for i in range(c // HG_SUB):
        lo, n = HG_SUB * i, HG_SUB * (i + 1)
        ref = jnp.sum(g[:lo], axis=0, keepdims=True) if i else jnp.zeros((1, g.shape[1]), f32)
        qh = (q[lo:n] * jnp.exp(b[lo:n] - ref)).astype(bf16)
        kh = (k[:n] * jnp.exp(ref - b[:n])).astype(bf16)
        keep = (lax.broadcasted_iota(jnp.int32, (HG_SUB, n), 1)
                <= lo + lax.broadcasted_iota(jnp.int32, (HG_SUB, n), 0))
        scores = [lax.dot_general(qh[:, hs], kh[:, hs], _NT, preferred_element_type=f32) for hs in heads]
        scores = [jnp.where(keep, a, 0.0).astype(bf16) for a in scores]
        outs.append(jnp.concatenate([jnp.dot(a, vb[:n, hs], preferred_element_type=f32)
                                     for a, hs in zip(scores, heads)], axis=1))
    return (jnp.concatenate(outs, axis=0) + jnp.concatenate(o_inter, axis=1), *new)


def _hg_fwd(q, k, v, g, *, name):
    length = q.shape[0]
    rows = _HG_GROUP * HG_CHUNK
    ng = length // rows
    nc = length // HG_CHUNK

    def body(q_ref, k_ref, v_ref, g_ref, o_ref, st_ref, state):
        @pl.when(pl.program_id(0) == 0)
        def _():
            state[...] = jnp.zeros_like(state)

        states = [state[h] for h in range(HG_HEADS)]
        for ci in range(_HG_GROUP):
            sl = slice(ci * HG_CHUNK, (ci + 1) * HG_CHUNK)
            for h in range(HG_HEADS):
                st_ref[h, ci] = states[h]
            o, *states = _hg_chunk(q_ref[sl, :], k_ref[sl, :], v_ref[sl, :], g_ref[sl, :], *states)
            o_ref[sl, :] = o
        for h in range(HG_HEADS):
            state[h] = states[h]

    blk = pl.BlockSpec((rows, HG_WIDTH), lambda c: (c, 0))
    return pl.pallas_call(
        body, grid=(ng,), in_specs=[blk] * 4,
        out_specs=[blk, pl.BlockSpec((HG_HEADS, _HG_GROUP, HG_DIM, HG_DIM), lambda c: (0, c, 0, 0))],
        out_shape=[jax.ShapeDtypeStruct((length, HG_WIDTH), f32),
                   jax.ShapeDtypeStruct((HG_HEADS, nc, HG_DIM, HG_DIM), f32)],
        scratch_shapes=[pltpu.VMEM((HG_HEADS, HG_DIM, HG_DIM), f32)],
        compiler_params=_cparams(("arbitrary",)), name=name)(q, k, v, g)


def _hg_bwd(q, k, v, g, states, do, *, name):
    length = q.shape[0]
    rows = _HG_GROUP * HG_CHUNK
    ng = length // rows

    def body(q_ref, k_ref, v_ref, g_ref, st_ref, do_ref, dq_ref, dk_ref, dv_ref, dg_ref, dstate):
        @pl.when(pl.program_id(0) == 0)
        def _():
            dstate[...] = jnp.zeros_like(dstate)

        dstates = [dstate[h] for h in range(HG_HEADS)]
        for ci in reversed(range(_HG_GROUP)):
            sl = slice(ci * HG_CHUNK, (ci + 1) * HG_CHUNK)
            _, vjp = jax.vjp(_hg_chunk, q_ref[sl, :], k_ref[sl, :], v_ref[sl, :], g_ref[sl, :],
                             *[st_ref[h, ci] for h in range(HG_HEADS)])
            dq, dk, dv, dg, *dstates = vjp((do_ref[sl, :], *dstates))
            dq_ref[sl, :] = dq
            dk_ref[sl, :] = dk
            dv_ref[sl, :] = dv
            dg_ref[sl, :] = dg
        for h in range(HG_HEADS):
            dstate[h] = dstates[h]

    blk = pl.BlockSpec((rows, HG_WIDTH), lambda c: (ng - 1 - c, 0))
    sds = jax.ShapeDtypeStruct((length, HG_WIDTH), f32)
    return pl.pallas_call(
        body, grid=(ng,),
        in_specs=[blk] * 4 + [pl.BlockSpec((HG_HEADS, _HG_GROUP, HG_DIM, HG_DIM), lambda c: (0, ng - 1 - c, 0, 0)), blk],
        out_specs=[blk] * 4, out_shape=[sds] * 4,
        scratch_shapes=[pltpu.VMEM((HG_HEADS, HG_DIM, HG_DIM), f32)],
        compiler_params=_cparams(("arbitrary",)), name=name)(q, k, v, g, states, do)


_ATT_BLK = 512
_ATT_SCALE = MLA_QK ** -0.5
_NEG = -1e30


def _att_mask(i, j, t):
    rows = i * t + lax.broadcasted_iota(jnp.int32, (t, t), 0)
    cols = j * t + lax.broadcasted_iota(jnp.int32, (t, t), 1)
    return cols <= rows


def _att_fwd(q, k, v, *, name):
    length = q.shape[0]
    t = min(_ATT_BLK, length)
    nq = length // t
    qw, vw = MLA_QK_PAD, MLA_V
    heads = range(MLA_HEADS)

    def body(q_ref, k_ref, v_ref, o_ref, lse_ref):
        i = pl.program_id(0)
        qbs = [q_ref[:, h * qw:(h + 1) * qw] for h in heads]

        def step(j, carry, diagonal=False):
            off = pl.multiple_of(j * t, t)
            out = []
            for h in heads:
                m, l, acc = carry[h]
                ks = k_ref[pl.ds(off, t), h * qw:(h + 1) * qw]
                vs = v_ref[pl.ds(off, t), h * vw:(h + 1) * vw]
                s = lax.dot_general(qbs[h], ks, _NT, preferred_element_type=f32) * _ATT_SCALE
                if diagonal:
                    s = jnp.where(_att_mask(i, j, t), s, _NEG)
                m_new = jnp.maximum(m, jnp.max(s, axis=-1, keepdims=True))
                alpha = jnp.exp(m - m_new)
                p = jnp.exp(s - m_new)
                l = alpha * l + jnp.sum(p, axis=-1, keepdims=True)
                acc = alpha * acc + jnp.dot(p.astype(bf16), vs, preferred_element_type=f32)
                out.append((m_new, l, acc))
            return tuple(out)

        init = tuple((jnp.full((t, 1), _NEG, f32), jnp.zeros((t, 1), f32), jnp.zeros((t, vw), f32)) for _ in heads)
        res = step(i, lax.fori_loop(0, i, step, init), diagonal=True)
        for h in heads:
            m, l, acc = res[h]
            o_ref[:, h * vw:(h + 1) * vw] = (acc / l).astype(o_ref.dtype)
            lse_ref[:, h * vw:(h + 1) * vw] = jnp.broadcast_to(m + jnp.log(l), (t, vw))

    return pl.pallas_call(
        body, grid=(nq,),
        in_specs=[pl.BlockSpec((t, q.shape[1]), lambda i: (i, 0)), pl.BlockSpec(k.shape, lambda i: (0, 0)),
                  pl.BlockSpec(v.shape, lambda i: (0, 0))],
        out_specs=[pl.BlockSpec((t, v.shape[1]), lambda i: (i, 0))] * 2,
        out_shape=[jax.ShapeDtypeStruct(v.shape, bf16), jax.ShapeDtypeStruct(v.shape, f32)],
        compiler_params=_cparams(("arbitrary",)), name=name)(q, k, v)


def _att_bwd(q, k, v, o, lse, do, *, name):
    length = q.shape[0]
    t = min(_ATT_BLK, length)
    nq = length // t
    qw, vw = MLA_QK_PAD, MLA_V
    heads = range(MLA_HEADS)

    def dq_body(q_ref, k_ref, v_ref, o_ref, lse_ref, do_ref, dq_ref, delta_ref):
        i = pl.program_id(0)
        qbs = [q_ref[:, h * qw:(h + 1) * qw] for h in heads]
        dobs = [do_ref[:, h * vw:(h + 1) * vw] for h in heads]
        lses = [lse_ref[:, h * vw:h * vw + 1] for h in heads]
        deltas = [jnp.sum(dobs[h].astype(f32) * o_ref[:, h * vw:(h + 1) * vw].astype(f32), axis=-1, keepdims=True)
                  for h in heads]

        def step(j, dqs, diagonal=False):
            off = pl.multiple_of(j * t, t)
            out = []
            for h in heads:
                ks = k_ref[pl.ds(off, t), h * qw:(h + 1) * qw]
                vs = v_ref[pl.ds(off, t), h * vw:(h + 1) * vw]
                s = lax.dot_general(qbs[h], ks, _NT, preferred_element_type=f32) * _ATT_SCALE
                p = jnp.exp(s - lses[h])
                if diagonal:
                    p = jnp.where(_att_mask(i, j, t), p, 0.0)
                dp = lax.dot_general(dobs[h], vs, _NT, preferred_element_type=f32)
                ds = p * (dp - deltas[h]) * _ATT_SCALE
                out.append(dqs[h] + jnp.dot(ds.astype(bf16), ks, preferred_element_type=f32))
            return tuple(out)

        dqs = step(i, lax.fori_loop(0, i, step, tuple(jnp.zeros((t, qw), f32) for _ in heads)), diagonal=True)
        for h in heads:
            dq_ref[:, h * qw:(h + 1) * qw] = dqs[h].astype(dq_ref.dtype)
            delta_ref[:, h * vw:(h + 1) * vw] = jnp.broadcast_to(deltas[h], (t, vw))

    qblk = pl.BlockSpec((t, q.shape[1]), lambda i: (i, 0))
    vblk = pl.BlockSpec((t, v.shape[1]), lambda i: (i, 0))
    qfull = pl.BlockSpec(q.shape, lambda i: (0, 0))
    vfull = pl.BlockSpec(v.shape, lambda i: (0, 0))
    dq, delta = pl.pallas_call(
        dq_body, grid=(nq,), in_specs=[qblk, qfull, vfull, vblk, vblk, vblk], out_specs=[qblk, vblk],
        out_shape=[jax.ShapeDtypeStruct(q.shape, bf16), jax.ShapeDtypeStruct(lse.shape, f32)],
        compiler_params=_cparams(("arbitrary",)), name=name + "_dq")(q, k, v, o, lse, do)

    def dkv_body(k_ref, v_ref, q_ref, do_ref, lse_ref, delta_ref, dk_ref, dv_ref):
        j = pl.program_id(0)
        kbs = [k_ref[:, h * qw:(h + 1) * qw] for h in heads]
        vbs = [v_ref[:, h * vw:(h + 1) * vw] for h in heads]

        def step(i, carry, diagonal=False):
            off = pl.multiple_of(i * t, t)
            out = []
            for h in heads:
                dk, dv = carry[h]
                qs = q_ref[pl.ds(off, t), h * qw:(h + 1) * qw]
                dos = do_ref[pl.ds(off, t), h * vw:(h + 1) * vw]
                lse_i = lse_ref[pl.ds(off, t), h * vw:h * vw + 1]
                delta_i = delta_ref[pl.ds(off, t), h * vw:h * vw + 1]
                s = lax.dot_general(qs, kbs[h], _NT, preferred_element_type=f32) * _ATT_SCALE
                p = jnp.exp(s - lse_i)
                if diagonal:
                    p = jnp.where(_att_mask(i, j, t), p, 0.0)
                dv = dv + lax.dot_general(p.astype(bf16), dos, _TN, preferred_element_type=f32)
                dp = lax.dot_general(dos, vbs[h], _NT, preferred_element_type=f32)
                ds = p * (dp - delta_i) * _ATT_SCALE
                dk = dk + lax.dot_general(ds.astype(bf16), qs, _TN, preferred_element_type=f32)
                out.append((dk, dv))
            return tuple(out)

        first = step(j, tuple((jnp.zeros((t, qw), f32), jnp.zeros((t, vw), f32)) for _ in heads), diagonal=True)
        res = lax.fori_loop(j + 1, nq, step, first)
        for h in heads:
            dk_ref[:, h * qw:(h + 1) * qw] = res[h][0].astype(dk_ref.dtype)
            dv_ref[:, h * vw:(h + 1) * vw] = res[h][1].astype(dv_ref.dtype)

    dk, dv = pl.pallas_call(
        dkv_body, grid=(nq,), in_specs=[qblk, vblk, qfull, vfull, vfull, vfull], out_specs=[qblk, vblk],
        out_shape=[jax.ShapeDtypeStruct(k.shape, bf16), jax.ShapeDtypeStruct(v.shape, bf16)],
        compiler_params=_cparams(("arbitrary",)), name=name + "_dkv")(k, v, q, do, lse, delta)
    return dq, dk, dv


_C_Q = 4 * HG_WIDTH
_C_KV = _C_Q + MLA_Q_RANK
_C_KPE = _C_KV + MLA_KV_RANK


def _rms_n(x, g, n):
    return x * lax.rsqrt(jnp.sum(x * x, axis=-1, keepdims=True) * (1.0 / n) + EPS) * g


def _mix_a(proj, l0, l1, q_a_norm, kv_a_norm):
    lb = jax.nn.sigmoid(l0 - l1)
    f = lb + (1.0 - lb) * jax.nn.sigmoid(proj[:, HG_WIDTH:2 * HG_WIDTH])
    qf = _silu(proj[:, :HG_WIDTH])
    v = proj[:, 2 * HG_WIDTH:3 * HG_WIDTH]
    cqn = _rms(proj[:, _C_Q:_C_KV], q_a_norm)
    ckvn = _rms(proj[:, _C_KV:_C_KPE], kv_a_norm)
    return qf, 1.0 - f, v, jnp.log(f), cqn, ckvn


def _mix_b(qraw, kvraw, kpe_raw, cos, sin, qn_nope, qn_rope, kn_nope, kn_rope, perm):
    def rope(x):
        return x * cos + jnp.dot(x, perm, precision=lax.Precision.HIGHEST, preferred_element_type=f32) * sin

    kpe = rope(_rms_n(kpe_raw, kn_rope, MLA_ROPE))
    qs, ks, vs = [], [], []
    for hh in range(MLA_HEADS):
        base = hh * MLA_QK_PAD
        qs.append(_rms(qraw[:, base:base + MLA_NOPE], qn_nope))
        qs.append(rope(_rms_n(qraw[:, base + MLA_NOPE:base + MLA_QK_PAD], qn_rope, MLA_ROPE)))
        ks.append(_rms(kvraw[:, base:base + MLA_NOPE], kn_nope))
        ks.append(kpe)
        vs.append(kvraw[:, base + MLA_NOPE:base + MLA_QK_PAD])
    return jnp.concatenate(qs, axis=-1), jnp.concatenate(ks, axis=-1), jnp.concatenate(vs, axis=-1)


def _mix_c(o_hg, gate, o_mla, hg_out_norm):
    parts = []
    for hh in range(HG_HEADS):
        sl = slice(hh * HG_DIM, (hh + 1) * HG_DIM)
        parts.append(_rms(o_hg[:, sl], hg_out_norm[:, sl]))
    o = jnp.concatenate(parts, axis=-1) * _silu(gate)
    return jnp.concatenate([o, o_mla], axis=-1)


def _rope_perm():
    p = np.zeros((128, 128), np.float32)
    half = MLA_ROPE // 2
    for i in range(half):
        p[i + half, i] = -1.0
        p[i, i + half] = 1.0
    return jnp.asarray(p)


def _mixer_fwd(h, cos, sin, w, tag, next_gain):
    d = h.shape[1]
    hn, = _rows(_rms, [h], [w["norm_mix"]], [(d, bf16)], name=f"mix_norm_{tag}")
    proj = _mm(hn, w["mix_w_in"], name=f"mix_in_{tag}")
    pa = [w["lb0"], w["lb1"], w["mla_q_a_norm"], w["mla_kv_a_norm"]]
    qf, kk, vv, logf, cqn, ckvn = _rows(
        _mix_a, [proj], pa, [(HG_WIDTH, f32)] * 4 + [(MLA_Q_RANK, bf16), (MLA_KV_RANK, bf16)], name=f"mix_a_{tag}")
    o_hg, states = _hg_fwd(qf, kk, vv, logf, name=f"hg_fwd_{tag}")
    qraw = _mm(cqn, w["mla_w_uq"], name=f"mla_uq_{tag}")
    kvraw = _mm(ckvn, w["mla_w_ukv"], name=f"mla_ukv_{tag}")
    pb = [w["mla_qn_nope"], w["mla_qn_rope"], w["mla_kn_nope"], w["mla_kn_rope"], w["rope_perm"]]
    kpe_raw, gate = _window(proj, _C_KPE, IN_PAD - _C_KPE), _window(proj, 3 * HG_WIDTH, HG_WIDTH)
    qfull, kfull, vfull = _rows(_mix_b, [qraw, kvraw, kpe_raw, cos, sin], pb,
                                [(MLA_HEADS * MLA_QK_PAD, bf16)] * 2 + [(MLA_HEADS * MLA_V, bf16)],
                                name=f"mix_b_{tag}")
    o_mla, lse = _att_fwd(qfull, kfull, vfull, name=f"att_fwd_{tag}")
    mixin, = _rows(_mix_c, [o_hg, gate, o_mla], [w["hg_out_norm"]], [(d, bf16)], name=f"mix_c_{tag}")
    if callable(w["mix_w_out"]):
        w["mix_w_out"] = w["mix_w_out"](mixin)
    out, normed = _mm(mixin, w["mix_w_out"], add=h, norm=next_gain, name=f"mix_out_{tag}")
    return out, (h, hn, proj, qf, kk, vv, logf, cqn, ckvn, o_hg, states, qraw, kvraw, qfull, kfull, vfull, o_mla,
                 lse, mixin), normed


def _mixer_bwd(dout, cos, sin, w, saved, tag, on_w_out=None):
    (h, hn, proj, qf, kk, vv, logf, cqn, ckvn, o_hg, states, qraw, kvraw, qfull, kfull, vfull, o_mla, lse,
     mixin) = saved
    g = {}
    dmixin = _mm(dout, w["mix_w_out"], tb=True, name=f"mix_dmixin_{tag}")
    g["mix_w_out"] = _mm(mixin, dout, ta=True, out_dtype=bf16, name=f"mix_dwout_{tag}")
    if on_w_out is not None:
        dmixin = on_w_out(g["mix_w_out"], dmixin)
    kpe_raw, gate = _window(proj, _C_KPE, IN_PAD - _C_KPE), _window(proj, 3 * HG_WIDTH, HG_WIDTH)
    (do_hg, dgate, do_mla), (g["hg_out_norm"],) = _rows_bwd(
        _mix_c, [o_hg, gate, o_mla], [w["hg_out_norm"]], [dmixin], rgrad=[f32, f32, bf16], pgrad=[True],
        name=f"mix_dc_{tag}")
    dqfull, dkfull, dvfull = _att_bwd(qfull, kfull, vfull, o_mla, lse, do_mla, name=f"att_bwd_{tag}")
    pb = [w["mla_qn_nope"], w["mla_qn_rope"], w["mla_kn_nope"], w["mla_kn_rope"], w["rope_perm"]]
    (dqraw, dkvraw, dkpe_raw), pg = _rows_bwd(
        _mix_b, [qraw, kvraw, kpe_raw, cos, sin], pb, [dqfull, dkfull, dvfull],
        rgrad=[bf16, bf16, f32, None, None], pgrad=[True, True, True, True, False],
        name=f"mix_db_{tag}")
    g["mla_qn_nope"], g["mla_qn_rope"], g["mla_kn_nope"], g["mla_kn_rope"] = pg
    dcqn = _mm(dqraw, w["mla_w_uq"], tb=True, name=f"mla_dcq_{tag}")
    g["mla_w_uq"] = _mm(cqn, dqraw, ta=True, name=f"mla_dwuq_{tag}")
    dckvn = _mm(dkvraw, w["mla_w_ukv"], tb=True, name=f"mla_dckv_{tag}")
    g["mla_w_ukv"] = _mm(ckvn, dkvraw, ta=True, name=f"mla_dwukv_{tag}")
    dqf, dkk, dvv, dlogf = _hg_bwd(qf, kk, vv, logf, states, do_hg, name=f"hg_bwd_{tag}")
    pa = [w["lb0"], w["lb1"], w["mla_q_a_norm"], w["mla_kv_a_norm"]]
    (dproj,), (g["lb0"], g["lb1"], g["mla_q_a_norm"], g["mla_kv_a_norm"]) = _rows_bwd(
        _mix_a, [proj], pa, [dqf, dkk, dvv, dlogf, dcqn, dckvn], rgrad=[bf16], pgrad=[True] * 4,
        addends={0: [(dgate, 3 * HG_WIDTH), (dkpe_raw, _C_KPE)]}, name=f"mix_da_{tag}")
    dh, g["norm_mix"] = _mm(dproj, w["mix_w_in"], tb=True, rms_bwd=(h, w["norm_mix"], dout), name=f"mix_dhn_{tag}")
    g["mix_w_in_t"] = _mm(dproj, hn, ta=True, name=f"mix_dwin_{tag}")
    return dh, g


def _rope_tables(positions):
    inv_freq = 1.0 / (ROPE_BASE ** (jnp.arange(0, MLA_ROPE, 2, dtype=f32) / MLA_ROPE))
    ang = positions.astype(f32)[:, None] * inv_freq
    z = jnp.zeros((positions.shape[0], 128 - MLA_ROPE), f32)
    return (jnp.concatenate([jnp.cos(ang), jnp.cos(ang), z], axis=1),
            jnp.concatenate([jnp.sin(ang), jnp.sin(ang), z], axis=1))


def _pad_cols(a, n):
    return jnp.pad(a, ((0, 0), (0, n - a.shape[1])))


def _even_weights(p, j, layer, dt):
    w_uq = p["mla_w_uq"][j].reshape(MLA_Q_RANK, MLA_HEADS, MLA_QK)
    w_uq = jnp.pad(w_uq, ((0, 0), (0, 0), (0, MLA_QK_PAD - MLA_QK))).reshape(MLA_Q_RANK, MLA_HEADS * MLA_QK_PAD)
    return dict(
        norm_mix=p["norm_mix"][layer][None], mix_w_in=_pad_cols(p["mix_w_in"][j], IN_PAD).astype(dt),
        lb0=p["hg_lb_logits"][0][None], lb1=p["hg_lb_logits"][1][None],
        mla_q_a_norm=p["mla_q_a_norm"][j][None], mla_kv_a_norm=p["mla_kv_a_norm"][j][None],
        mla_w_uq=w_uq.astype(dt), mla_w_ukv=p["mla_w_ukv"][j].astype(dt),
        mla_qn_nope=p["mla_qn_nope"][j][None], mla_qn_rope=_pad_cols(p["mla_qn_rope"][j][None], 128),
        mla_kn_nope=p["mla_kn_nope"][j][None], mla_kn_rope=_pad_cols(p["mla_kn_rope"][j][None], 128),
        rope_perm=_rope_perm(), hg_out_norm=p["hg_out_norm"][j][None],
        mix_w_out=p["mix_w_out"][j].astype(dt) if "mix_w_out" in p else None)


def _even_grads(g):
    w_uq = g["mla_w_uq"].reshape(MLA_Q_RANK, MLA_HEADS, MLA_QK_PAD)[:, :, :MLA_QK].reshape(MLA_Q_RANK, -1)
    return dict(
        norm_mix=g["norm_mix"], mix_w_in=g["mix_w_in_t"][:IN_WIDTH].T[None],
        hg_lb_logits=jnp.concatenate([g["lb0"], g["lb1"]], axis=0),
        mla_q_a_norm=g["mla_q_a_norm"], mla_kv_a_norm=g["mla_kv_a_norm"], mla_w_uq=w_uq[None],
        mla_w_ukv=g["mla_w_ukv"][None], mla_qn_nope=g["mla_qn_nope"], mla_qn_rope=g["mla_qn_rope"][:, :MLA_ROPE],
        mla_kn_nope=g["mla_kn_nope"], mla_kn_rope=g["mla_kn_rope"][:, :MLA_ROPE],
        hg_out_norm=g["hg_out_norm"], mix_w_out=g["mix_w_out"][None])


_S5_NB = 8
_S5_BW = 1024
_S5_HALF = 512
_S5_UC = 128
_S5_TIME = 1024


def _cmul(ar, ai, br, bi):
    return ar * br - ai * bi, ar * bi + ai * br


def _pow_table(ar, ai, descending):
    rows = lax.broadcasted_iota(jnp.int32, (8, ar.shape[1]), 0)
    tr = jnp.zeros((8, ar.shape[1]), f32)
    ti = jnp.zeros((8, ar.shape[1]), f32)
    pr, pi_ = ar, ai
    for r in range(8):
        sel = rows == ((7 - r) if descending else r)
        tr = jnp.where(sel, pr, tr)
        ti = jnp.where(sel, pi_, ti)
        pr, pi_ = _cmul(pr, pi_, ar, ai)
    return tr, ti


def _s5_tile_scan(work, carry, ar, ai, tc, reverse, per_tile=None):
    hw = _S5_HALF
    row8 = lax.broadcasted_iota(jnp.int32, (8, hw), 0)
    powers = [(ar, ai)]
    for _ in range(2):
        powers.append(_cmul(*powers[-1], *powers[-1]))
    steps = []
    for (mr, mi), s in zip(powers, (1, 2, 4)):
        ok = (row8 < 8 - s) if reverse else (row8 >= s)
        steps.append((jnp.where(ok, mr, 0.0), jnp.where(ok, mi, 0.0), 8 - s if reverse else s))
    tr, ti = _pow_table(ar, ai, reverse)
    cr, ci = carry[:, :hw], carry[:, hw:]
    tiles = range(tc // 8)
    for i in (reversed(tiles) if reverse else tiles):
        sl = slice(8 * i, 8 * i + 8)
        xr, xi = work[sl, :hw], work[sl, hw:]
        for mr, mi, shift in steps:
            pr, pi_ = _cmul(mr, mi, pltpu.roll(xr, shift, axis=0), pltpu.roll(xi, shift, axis=0))
            xr, xi = xr + pr, xi + pi_
        pr, pi_ = _cmul(tr, ti, cr, ci)
        xr, xi = xr + pr, xi + pi_
        work[sl, :hw] = xr
        work[sl, hw:] = xi
        if per_tile is not None:
            per_tile(sl, xr, xi, cr, ci)
        edge = 8 * i if reverse else 8 * i + 7
        cr, ci = work[edge:edge + 1, :hw], work[edge:edge + 1, hw:]
    carry[:, :hw] = cr
    carry[:, hw:] = ci


def _s5_core_fwd(a, hn, b3, c3, *, name):
    length = hn.shape[0]
    tc = min(_S5_TIME, length)

    def body(a_ref, hn_ref, b_ref, c_ref, hs_ref, y_ref, work, carry):
        @pl.when(pl.program_id(1) == 0)
        def _():
            carry[...] = jnp.zeros_like(carry)

        work[...] = jnp.dot(hn_ref[...].astype(bf16), b_ref[...], preferred_element_type=f32)
        _s5_tile_scan(work, carry, a_ref[:, :_S5_HALF], a_ref[:, _S5_HALF:], tc, False)
        hs = work[...].astype(bf16)
        hs_ref[...] = hs
        y_ref[...] = jnp.dot(hs, c_ref[...], preferred_element_type=f32)

    return pl.pallas_call(
        body, grid=(_S5_NB, length // tc),
        in_specs=[pl.BlockSpec((1, _S5_BW), lambda j, t: (0, j)), pl.BlockSpec((tc, _S5_UC), lambda j, t: (t, j)),
                  pl.BlockSpec((_S5_UC, _S5_BW), lambda j, t: (j, 0)), pl.BlockSpec((_S5_BW, _S5_UC), lambda j, t: (j, 0))],
        out_specs=[pl.BlockSpec((tc, _S5_BW), lambda j, t: (t, j)), pl.BlockSpec((tc, _S5_UC), lambda j, t: (t, j))],
        out_shape=[jax.ShapeDtypeStruct((length, _S5_NB * _S5_BW), bf16),
                   jax.ShapeDtypeStruct((length, _S5_NB * _S5_UC), f32)],
        scratch_shapes=[pltpu.VMEM((tc, _S5_BW), f32), pltpu.VMEM((1, _S5_BW), f32)],
        compiler_params=_cparams(("parallel", "arbitrary")), name=name)(a, hn, b3, c3)


def _s5_core_bwd(a, dy, c3, hs, hn, b3, *, name):
    length = hn.shape[0]
    tc = min(_S5_TIME, length)
    nt = length // tc
    hw = _S5_HALF

    def body(a_ref, dy_ref, c_ref, hs_ref, hn_ref, b_ref, du_ref, db_ref, dc_ref, da_ref, work, carry, acc):
        @pl.when(pl.program_id(1) == 0)
        def _():
            carry[...] = jnp.zeros_like(carry)
            db_ref[...] = jnp.zeros_like(db_ref)
            dc_ref[...] = jnp.zeros_like(dc_ref)
            da_ref[...] = jnp.zeros_like(da_ref)

        dyb = dy_ref[...].astype(bf16)
        work[...] = lax.dot_general(dyb, c_ref[...], _NT, preferred_element_type=f32)
        acc[...] = jnp.zeros_like(acc)
        row8 = lax.broadcasted_iota(jnp.int32, (8, hw), 0)

        def grad_a(sl, gr, gi, cr, ci):
            gnr = jnp.where(row8 == 7, cr, pltpu.roll(gr, 7, axis=0))
            gni = jnp.where(row8 == 7, ci, pltpu.roll(gi, 7, axis=0))
            hr, hi = hs_ref[sl, :hw].astype(f32), hs_ref[sl, hw:].astype(f32)
            acc[:, :hw] += hr * gnr + hi * gni
            acc[:, hw:] += hr * gni - hi * gnr

        _s5_tile_scan(work, carry, a_ref[:, :hw], -a_ref[:, hw:], tc, True, grad_a)
        da_ref[...] += jnp.sum(acc[...], axis=0, keepdims=True)
        g = work[...].astype(bf16)
        du_ref[...] = lax.dot_general(g, b_ref[...], _NT, preferred_element_type=f32)
        db_ref[...] += lax.dot_general(hn_ref[...].astype(bf16), g, _TN, preferred_element_type=f32)
        dc_ref[...] += lax.dot_general(hs_ref[...], dyb, _TN, preferred_element_type=f32)

    rev = lambda j, t: (nt - 1 - t, j)
    return pl.pallas_call(
        body, grid=(_S5_NB, nt),
        in_specs=[pl.BlockSpec((1, _S5_BW), lambda j, t: (0, j)), pl.BlockSpec((tc, _S5_UC), rev),
                  pl.BlockSpec((_S5_BW, _S5_UC), lambda j, t: (j, 0)), pl.BlockSpec((tc, _S5_BW), rev),
                  pl.BlockSpec((tc, _S5_UC), rev), pl.BlockSpec((_S5_UC, _S5_BW), lambda j, t: (j, 0))],
        out_specs=[pl.BlockSpec((tc, _S5_UC), rev), pl.BlockSpec((_S5_UC, _S5_BW), lambda j, t: (j, 0)),
                   pl.BlockSpec((_S5_BW, _S5_UC), lambda j, t: (j, 0)), pl.BlockSpec((1, _S5_BW), lambda j, t: (0, j))],
        out_shape=[jax.ShapeDtypeStruct((length, _S5_NB * _S5_UC), f32),
                   jax.ShapeDtypeStruct((_S5_NB * _S5_UC, _S5_BW), f32),
                   jax.ShapeDtypeStruct((_S5_NB * _S5_BW, _S5_UC), f32),
                   jax.ShapeDtypeStruct((1, _S5_NB * _S5_BW), f32)],
        scratch_shapes=[pltpu.VMEM((tc, _S5_BW), f32), pltpu.VMEM((1, _S5_BW), f32), pltpu.VMEM((8, _S5_BW), f32)],
        compiler_params=_cparams(("parallel", "arbitrary")), name=name)(a, dy, c3, hs, hn, b3)


def _s5_disc(lr, li, ldt, btr, bti, expand):
    dt = jnp.exp(ldt)
    mag = jnp.exp(lr * dt)
    abr = mag * jnp.cos(li * dt)
    abi = mag * jnp.sin(li * dt)
    den = lr * lr + li * li
    zr = ((abr - 1.0) * lr + abi * li) / den
    zi = (abi * lr - (abr - 1.0) * li) / den
    zr = jnp.dot(zr, expand, precision=lax.Precision.HIGHEST, preferred_element_type=f32)
    zi = jnp.dot(zi, expand, precision=lax.Precision.HIGHEST, preferred_element_type=f32)
    return abr, abi, zr * btr - zi * bti, zr * bti + zi * btr


def _s5_disc_fwd(args, *, name):
    def body(*refs):
        res = _s5_disc(*[r[...] for r in refs[:6]])
        for o, v in zip(refs[6:], res):
            o[...] = v

    sds = jax.ShapeDtypeStruct
    return pl.pallas_call(body, out_shape=[sds(args[0].shape, f32)] * 2 + [sds(args[3].shape, f32)] * 2,
                          name=name)(*args)


def _s5_disc_bwd(args, cts, *, name):
    def body(*refs):
        vals = [r[...] for r in refs[:6]]
        _, vjp = jax.vjp(lambda *d: _s5_disc(*d, vals[5]), *vals[:5])
        grads = vjp(tuple(r[...] for r in refs[6:10]))
        for o, v in zip(refs[10:], grads):
            o[...] = v

    return pl.pallas_call(body, out_shape=[jax.ShapeDtypeStruct(a.shape, f32) for a in args[:5]],
                          name=name)(*args, *cts)


def _gelu_tanh(x):
    return 0.5 * x * (1.0 + jnp.tanh(0.7978845608028654 * (x + 0.044715 * (x * x * x))))


def _s5_post(y, u, d_skip):
    return _gelu_tanh(y + d_skip * u)


def _s5_glu(ga, gb, h):
    return h + ga * jax.nn.sigmoid(gb)


def _s5_glu_norm(ga, gb, h, next_gain):
    out = _s5_glu(ga, gb, h)
    return out, _rms(out, next_gain)


def _s5_expand():
    e = np.zeros((S5_STATE, S5_GROUP * S5_STATE), np.float32)
    for m in range(S5_GROUP):
        e[np.arange(S5_STATE), m * S5_STATE + np.arange(S5_STATE)] = 1.0
    return jnp.asarray(e)


def _s5_pack_b(bbr, bbi):
    eye = jnp.eye(8, dtype=f32)

    def one(bb):
        b5 = bb.reshape(_S5_NB, 8, S5_GROUP, S5_STATE)
        return jnp.einsum("jgmp,gh->jgmhp", b5, eye).reshape(_S5_NB * _S5_UC, _S5_HALF)

    return jnp.concatenate([one(bbr), one(bbi)], axis=1)


def _s5_unpack_b(db3):
    def one(d):
        d5 = d.reshape(_S5_NB, 8, S5_GROUP, 8, S5_STATE)
        return jnp.einsum("jgmgp->jgmp", d5).reshape(S5_GROUPS, S5_GROUP * S5_STATE)

    return one(db3[:, :_S5_HALF]), one(db3[:, _S5_HALF:])


def _s5_pack_c(c_re, c_im):
    eye = jnp.eye(8, dtype=f32)

    def one(c):
        c4 = c.reshape(_S5_NB, 8, S5_GROUP, S5_STATE)
        return jnp.einsum("jgmp,hg->jhpgm", c4, eye).reshape(_S5_NB, _S5_HALF, _S5_UC)

    return jnp.concatenate([one(c_re), -one(c_im)], axis=1).reshape(_S5_NB * _S5_BW, _S5_UC)


def _s5_unpack_c(dc3):
    d = dc3.reshape(_S5_NB, 2, 8, S5_STATE, 8, S5_GROUP)
    dre = jnp.einsum("jgpgm->jgmp", d[:, 0]).reshape(S5_GROUPS, S5_GROUP, S5_STATE)
    dim = -jnp.einsum("jgpgm->jgmp", d[:, 1]).reshape(S5_GROUPS, S5_GROUP, S5_STATE)
    return dre, dim


def _s5_state_row(re, im):
    r = re.reshape(_S5_NB, 1, _S5_HALF)
    i = im.reshape(_S5_NB, 1, _S5_HALF)
    return jnp.concatenate([r, i], axis=2).reshape(1, _S5_NB * _S5_BW)


def _s5_unstate_row(row):
    r = row.reshape(_S5_NB, 2, 8, S5_STATE)
    return r[:, 0].reshape(S5_GROUPS, S5_STATE), r[:, 1].reshape(S5_GROUPS, S5_STATE)


def _s5_fwd(h, w, tag, next_gain):
    d = h.shape[1]
    hn, = _rows(_rms, [h], [w["norm_mix"]], [(d, f32)], name=f"s5_norm_{tag}")
    disc_in = [w["s5_lam_re"], w["s5_lam_im"], w["s5_log_dt"], w["s5_bt_re"], w["s5_bt_im"], w["s5_expand"]]
    abr, abi, bbr, bbi = _s5_disc_fwd(disc_in, name=f"s5_disc_{tag}")
    a_row = _s5_state_row(abr, abi)
    b3 = _s5_pack_b(bbr, bbi).astype(bf16)
    hs, y = _s5_core_fwd(a_row, hn, b3, w["s5_c3"], name=f"s5_core_{tag}")
    yg, = _rows(_s5_post, [y, hn], [w["s5_d"]], [(d, bf16)], name=f"s5_post_{tag}")
    ga = _mm(yg, w["s5_w_glu_a"], name=f"s5_glu_a_{tag}")
    gb = _mm(yg, w["s5_w_glu_b"], name=f"s5_glu_b_{tag}")
    out, normed = _rows(_s5_glu_norm, [ga, gb, h], [next_gain], [(d, f32), (d, bf16)], name=f"s5_glu_{tag}")
    return out, (h, hn, disc_in, a_row, b3, hs, y, yg, ga, gb), normed


def _s5_bwd(dout, w, saved, tag):
    h, hn, disc_in, a_row, b3, hs, y, yg, ga, gb = saved
    g = {}
    (dga, dgb), _ = _rows_bwd(_s5_glu, [ga, gb, h], [], [dout], rgrad=[bf16, bf16, None], pgrad=[],
                              name=f"s5_dglu_{tag}")
    dyg = _mm(dga, w["s5_w_glu_a"], tb=True, name=f"s5_dyg_a_{tag}")
    dyg = _mm(dgb, w["s5_w_glu_b"], tb=True, add=dyg, name=f"s5_dyg_b_{tag}")
    g["s5_w_glu_a"] = _mm(yg, dga, ta=True, out_dtype=bf16, name=f"s5_dwa_{tag}")
    g["s5_w_glu_b"] = _mm(yg, dgb, ta=True, out_dtype=bf16, name=f"s5_dwb_{tag}")
    (dy, du_skip), (g["s5_d"],) = _rows_bwd(_s5_post, [y, hn], [w["s5_d"]], [dyg], rgrad=[bf16, f32], pgrad=[True],
                                           name=f"s5_dpost_{tag}")
    du, db3, dc3, da_row = _s5_core_bwd(a_row, dy, w["s5_c3"], hs, hn, b3, name=f"s5_dcore_{tag}")
    dabr, dabi = _s5_unstate_row(da_row)
    dbbr, dbbi = _s5_unpack_b(db3)
    g["s5_lam_re"], g["s5_lam_im"], g["s5_log_dt"], g["s5_bt_re"], g["s5_bt_im"] = _s5_disc_bwd(
        disc_in, [dabr, dabi, dbbr, dbbi], name=f"s5_ddisc_{tag}")
    g["s5_c_re"], g["s5_c_im"] = _s5_unpack_c(dc3)
    (dh,), (g["norm_mix"],) = _rows_bwd(_rms_twice, [h], [w["norm_mix"]], [du, du_skip], rgrad=[f32], pgrad=[True],
                                        addends={0: dout}, name=f"s5_dnorm_{tag}")
    return dh, g


def _odd_weights(p, j, layer, dt):
    tr = lambda b: b.transpose(0, 2, 1).reshape(S5_GROUPS, S5_GROUP * S5_STATE)
    return dict(
        norm_mix=p["norm_mix"][layer][None], s5_lam_re=p["s5_lam_re"][j], s5_lam_im=p["s5_lam_im"][j],
        s5_log_dt=p["s5_log_dt"][j][:, None], s5_bt_re=tr(p["s5_b_re"][j]), s5_bt_im=tr(p["s5_b_im"][j]),
        s5_expand=_s5_expand(), s5_c3=_s5_pack_c(p["s5_c_re"][j], p["s5_c_im"][j]).astype(dt),
        **{n: (p[n][j][None] if n == "s5_d" else p[n][j].astype(dt))
           for n in ("s5_d", "s5_w_glu_a", "s5_w_glu_b") if n in p})


def _odd_grads(g):
    tr = lambda b: b.reshape(S5_GROUPS, S5_GROUP, S5_STATE).transpose(0, 2, 1)[None]
    return dict(
        norm_mix=g["norm_mix"], s5_lam_re=g["s5_lam_re"][None], s5_lam_im=g["s5_lam_im"][None],
        s5_log_dt=g["s5_log_dt"][:, 0][None], s5_b_re=tr(g["s5_bt_re"]), s5_b_im=tr(g["s5_bt_im"]),
        s5_c_re=g["s5_c_re"][None], s5_c_im=g["s5_c_im"][None], s5_d=g["s5_d"],
        s5_w_glu_a=g["s5_w_glu_a"][None], s5_w_glu_b=g["s5_w_glu_b"][None])


FF_SHARD = 352
FF_SHARD_PAD = 384


def _pad_groups(a, axis):
    axis %= a.ndim
    zeros = jnp.zeros(a.shape[:axis] + (FF_SHARD_PAD - FF_SHARD,) + a.shape[axis + 1:], a.dtype)
    pieces = []
    for g in range(a.shape[axis] // FF_SHARD):
        pieces += [lax.slice_in_dim(a, g * FF_SHARD, (g + 1) * FF_SHARD, axis=axis), zeros]
    return jnp.concatenate(pieces, axis=axis)


def _unpad_groups(a, axis):
    axis %= a.ndim
    pieces = [lax.slice_in_dim(a, g * FF_SHARD_PAD, g * FF_SHARD_PAD + FF_SHARD, axis=axis)
              for g in range(a.shape[axis] // FF_SHARD_PAD)]
    return pieces[0] if len(pieces) == 1 else jnp.concatenate(pieces, axis=axis)


def _layer_weights(p, layer, dt):
    return dict(
        norm_xa=p["norm_xa"][layer][None], norm_mem=p["norm_mem"][layer][None], norm_ffn=p["norm_ffn"][layer][None],
        xa_wq=p["xa_wq"][layer].astype(dt), xa_wk=p["xa_wk"][layer].astype(dt), xa_wv=p["xa_wv"][layer].astype(dt),
        xa_wo=p["xa_wo"][layer].astype(dt), xa_q_norm=p["xa_q_norm"][layer][None],
        xa_k_norm=p["xa_k_norm"][layer][None], ffn_w_up=_pad_groups(p["ffn_w_up"][layer], 1).astype(dt),
        ffn_conv_w=_pad_groups(p["ffn_conv_w"][layer], 1), ffn_conv_b=_pad_groups(p["ffn_conv_b"][layer][None], 1),
        ffn_w_down=_pad_groups(p["ffn_w_down"][layer], 0).astype(dt))


_PER_LAYER = ("norm_xa", "norm_mem", "norm_ffn", "xa_wq", "xa_wk", "xa_wv", "xa_wo", "xa_q_norm", "xa_k_norm",
              "ffn_w_up", "ffn_conv_w", "ffn_conv_b", "ffn_w_down")
_FFN_PADDED = dict(ffn_w_up=1, ffn_conv_w=1, ffn_conv_b=1, ffn_w_down=0)


def _local_step(x, mem, positions, target, p):
    cos, sin = _rope_tables(positions)
    we = _even_weights(p, 0, 0, bf16)
    wo = _odd_weights(p, 0, 1, bf16)
    wl = [_layer_weights(p, layer, bf16) for layer in range(2)]
    loss, dh, g_even, g_odd, gl = _local_core(x, mem, cos, sin, target, we, wo, wl)
    grads = {}
    for n in _PER_LAYER:
        a, b = gl[0][n], gl[1][n]
        if n in _FFN_PADDED:
            a, b = _unpad_groups(a, _FFN_PADDED[n]), _unpad_groups(b, _FFN_PADDED[n])
        grads[n] = jnp.concatenate([a, b], axis=0) if a.shape[0] == 1 else jnp.stack([a, b])
    ge, go = _even_grads(g_even), _odd_grads(g_odd)
    grads["norm_mix"] = jnp.concatenate([ge.pop("norm_mix"), go.pop("norm_mix")], axis=0)
    grads.update(ge)
    grads.update(go)
    return loss, dh, grads


def _local_core(x, mem, cos, sin, target, we, wo, wl):
    h, s_mix0, hx = _mixer_fwd(x, cos, sin, we, "l0", wl[0]["norm_xa"])
    h, s_xa0, hf = _xattn_fwd(h, mem, wl[0], "l0", hx)
    h, s_ff0 = _ffn_fwd(h, wl[0], "l0", hf)
    h, s_mix1, hx = _s5_fwd(h, wo, "l1", wl[1]["norm_xa"])
    h, s_xa1, hf = _xattn_fwd(h, mem, wl[1], "l1", hx)
    (dh, loss), s_ff1 = _ffn_fwd(h, wl[1], "l1", hf, loss_target=target)

    gl = [{}, {}]
    dh, g = _ffn_bwd(dh, wl[1], s_ff1, "l1")
    gl[1].update(g)
    dh, g = _xattn_bwd(dh, mem, wl[1], s_xa1, "l1")
    gl[1].update(g)
    dh, g_odd = _s5_bwd(dh, wo, s_mix1, "l1")
    dh, g = _ffn_bwd(dh, wl[0], s_ff0, "l0")
    gl[0].update(g)
    dh, g = _xattn_bwd(dh, mem, wl[0], s_xa0, "l0")
    gl[0].update(g)
    dh, g_even = _mixer_bwd(dh, cos, sin, we, s_mix0, "l0")
    return loss, dh, g_even, g_odd, gl


_LANES = 1024
_ROW_PAD = 16


_PEER_MASKS = (1, 2, 4, 3, 5, 6, 7)


def _mesh_place():
    x, y, c = lax.axis_index("x"), lax.axis_index("y"), lax.axis_index("c")

    def peer(mask):
        px = 1 - x if mask & 4 else x
        py = 1 - y if mask & 2 else y
        pc = 1 - c if mask & 1 else c
        return (px, py, pc), 4 * px + 2 * py + pc

    return 4 * x + 2 * y + c, peer


class _Exchange:
    def __init__(self, name):
        self.name = name
        self.srcs, self.shapes, self.items, self.where = [], [], [], {}

    def add(self, src, land_shape, src_at, dst_at, key):
        si = next((i for i, s in enumerate(self.srcs) if s is src), None)
        if si is None:
            self.srcs.append(src)
            si = len(self.srcs) - 1
        if key not in self.where:
            self.shapes.append(land_shape)
            self.where[key] = len(self.shapes) - 1
        self.items.append(dict(src=si, dst=self.where[key], src_at=src_at, dst_at=dst_at))

    def _copy(self, k, mask, ins, lands, send_sems, recv_sems, me, peer, arriving):
        it = self.items[k]
        dev, idx = peer(mask)
        s = k * (N_DEV - 1) + _PEER_MASKS.index(mask)
        return pltpu.make_async_remote_copy(
            src_ref=it["src_at"](ins[it["src"]], idx), dst_ref=it["dst_at"](lands[it["dst"]], idx if arriving else me),
            send_sem=send_sems.at[s], recv_sem=recv_sems.at[s], device_id=dev, device_id_type=pl.DeviceIdType.MESH)

    def _own_copy(self, k, ins, lands, own_sems, me):
        it = self.items[k]
        return pltpu.make_async_copy(it["src_at"](ins[it["src"]], me), it["dst_at"](lands[it["dst"]], me), own_sems.at[k])

    def begin(self, own):
        ns, nd, ni = len(self.srcs), len(self.shapes), len(self.items)
        nsem = ni * (N_DEV - 1)
        self.own = own

        nq = 3 if own else 2

        def body(*refs):
            ins, land_refs = refs[:ns], refs[ns:ns + nd]
            sems, token = refs[ns + nd:ns + nd + nq], refs[-1]
            me, peer = _mesh_place()
            for mask in _PEER_MASKS:
                for k in range(ni):
                    self._copy(k, mask, ins, land_refs, sems[0], sems[1], me, peer, False).start()
            if own:
                for k in range(ni):
                    self._own_copy(k, ins, land_refs, sems[2], me).start()
            token[...] = jnp.zeros_like(token)

        hbm = pl.BlockSpec(memory_space=pltpu.HBM)
        sem = pl.BlockSpec(memory_space=pltpu.SEMAPHORE)
        lands = [lax.empty(s.shape, s.dtype) for s in self.shapes]
        sem_shapes = [pltpu.SemaphoreType.DMA((nsem,)), pltpu.SemaphoreType.DMA((nsem,)), pltpu.SemaphoreType.DMA((ni,))]
        res = pl.pallas_call(
            body, in_specs=[hbm] * (ns + nd),
            out_specs=[sem] * nq + [hbm] * nd + [pl.BlockSpec(memory_space=pltpu.VMEM)],
            out_shape=sem_shapes[:nq] + [pltpu.HBM(s.shape, s.dtype) for s in self.shapes]
            + [jax.ShapeDtypeStruct((8, 128), f32)],
            input_output_aliases={ns + j: nq + j for j in range(nd)},
            compiler_params=pltpu.CompilerParams(has_side_effects=pltpu.SideEffectType.DATAFLOW_SIDE_EFFECTING),
            name=self.name + "_start")(*self.srcs, *lands)
        self.token = res[-1]
        return list(res[:nq]), list(res[nq:-1])

    def finish(self, state, after):
        sems, lands = state
        nq = len(sems)
        after = list(after) if isinstance(after, (list, tuple)) else [after]
        ns, nd, ni = len(self.srcs), len(self.shapes), len(self.items)

        def body(*refs):
            ins, land_refs = refs[:ns], refs[ns:ns + nd]
            sem_refs = refs[ns + nd:ns + nd + nq]
            me, peer = _mesh_place()
            for mask in _PEER_MASKS:
                for k in range(ni):
                    cp = self._copy(k, mask, ins, land_refs, sem_refs[0], sem_refs[1], me, peer, True)
                    cp.wait_send()
                    cp.wait_recv()
            if self.own:
                for k in range(ni):
                    self._own_copy(k, ins, land_refs, sem_refs[2], me).wait()

        hbm = pl.BlockSpec(memory_space=pltpu.HBM)
        sem = pl.BlockSpec(memory_space=pltpu.SEMAPHORE)
        res = pl.pallas_call(
            body, in_specs=[hbm] * (ns + nd) + [sem] * nq + [pl.BlockSpec(memory_space=pl.ANY)] * len(after),
            out_specs=[hbm] * nd, out_shape=[pltpu.HBM(s.shape, s.dtype) for s in self.shapes],
            input_output_aliases={ns + j: j for j in range(nd)},
            compiler_params=pltpu.CompilerParams(has_side_effects=pltpu.SideEffectType.DATAFLOW_SIDE_EFFECTING),
            name=self.name + "_wait")(*self.srcs, *lands, *sems, *after)
        return {k: res[i] for k, i in self.where.items()}


def _after(x, *tokens, name):
    def body(*refs):
        del refs

    anyspace = pl.BlockSpec(memory_space=pl.ANY)
    return pl.pallas_call(body, in_specs=[anyspace] * (1 + len(tokens)), out_specs=anyspace,
                          out_shape=jax.ShapeDtypeStruct(x.shape, x.dtype), input_output_aliases={0: 0},
                          name=name)(x, *tokens)


def _rows_of(n):
    return lambda r, i: r.at[pl.ds(pl.multiple_of(i * n, n), n), :]


def _cols_of(n):
    return lambda r, i: r.at[:, pl.ds(pl.multiple_of(i * n, n), n)]


def _whole(r, i):
    return r


def _slot(r, i):
    return r.at[i]


def _at_layer(layer):
    return lambda r, i: r.at[layer]


def _sum_adam(me_index, slots, owns, own_block, w, m, v, *, name):
    layers, rows, cols = w.shape
    tr = _pick(rows, (256, 128, 104, 64, 32, 16, 8))
    bc1 = 1.0 - ADAM_B1 ** ADAM_STEP
    bc2 = 1.0 - ADAM_B2 ** ADAM_STEP
    own_shape, own_map = own_block(tr)
    nl = len(slots)
    assert nl == layers and len(owns) == layers

    def body(me_ref, *refs):
        s_refs, own_refs = refs[:nl], refs[nl:2 * nl]
        w_ref, m_ref, v_ref, g_ref, d_ref, nm_ref, nv_ref = refs[2 * nl:]
        me = me_ref[0]

        def run(s_ref, own_ref):
            mine = (own_ref[0] if len(own_shape) == 3 else own_ref[...]).astype(f32)
            g = jnp.where(me == 0, mine, s_ref[0].astype(f32))
            for k in range(1, N_DEV):
                g = g + jnp.where(me == k, mine, s_ref[k].astype(f32))
            mm = ADAM_B1 * m_ref[0] + (1.0 - ADAM_B1) * g
            vv = ADAM_B2 * v_ref[0] + (1.0 - ADAM_B2) * (g * g)
            g_ref[0] = g
            nm_ref[0] = mm
            nv_ref[0] = vv
            d_ref[0] = -ADAM_LR * ((mm / bc1) / (jnp.sqrt(vv / bc2) + ADAM_EPS) + ADAM_WD * w_ref[0])

        for layer in range(nl):
            pl.when(pl.program_id(0) == layer)(functools.partial(run, s_refs[layer], own_refs[layer]))

    def of_layer(layer, index_map):
        return lambda lyr, i, me: index_map(jnp.where(lyr == layer, i, 0), me)

    blk = pl.BlockSpec((1, tr, cols), lambda lyr, i, me: (lyr, i, 0))
    sds = jax.ShapeDtypeStruct((layers, rows, cols), f32)
    grid_spec = pltpu.PrefetchScalarGridSpec(
        num_scalar_prefetch=1, grid=(layers, rows // tr),
        in_specs=[pl.BlockSpec((N_DEV, tr, cols), of_layer(layer, lambda i, me: (0, i, 0))) for layer in range(nl)]
        + [pl.BlockSpec(own_shape, of_layer(layer, own_map)) for layer in range(nl)] + [blk, blk, blk],
        out_specs=[blk] * 4)
    return pl.pallas_call(body, grid_spec=grid_spec, out_shape=[sds] * 4,
                          compiler_params=_cparams(("arbitrary", "arbitrary")),
                          name=name)(me_index, *slots, *owns, w, m, v)


_SHARDED = dict(xa_wq=1, xa_wk=1, xa_wv=1, xa_wo=1, ffn_w_up=2, ffn_conv_w=2, ffn_w_down=1, mix_w_in=2, mla_w_uq=2,
                mla_w_ukv=2, mix_w_out=1, s5_d=1, s5_w_glu_a=1, s5_w_glu_b=1)
_EXACT = ("ffn_conv_w", "s5_d")
_WEIGHTS = ("norm_mix", "norm_xa", "norm_mem", "norm_ffn", "xa_wq", "xa_wk", "xa_wv", "xa_wo", "xa_q_norm",
            "xa_k_norm", "ffn_w_up", "ffn_conv_w", "ffn_conv_b", "ffn_w_down", "hg_lb_logits", "mix_w_in",
            "hg_out_norm", "mla_q_a_norm", "mla_w_uq", "mla_kv_a_norm", "mla_w_ukv", "mla_qn_nope", "mla_qn_rope",
            "mla_kn_nope", "mla_kn_rope", "mix_w_out", "s5_lam_re", "s5_lam_im", "s5_log_dt", "s5_b_re", "s5_b_im",
            "s5_c_re", "s5_c_im", "s5_d", "s5_w_glu_a", "s5_w_glu_b")
_BIG = tuple(n for n in _WEIGHTS if n in _SHARDED and n not in _EXACT)
_SHARD_ORDER = tuple(n for n in _WEIGHTS if n in _SHARDED)
_REPL_ORDER = tuple(n for n in _WEIGHTS if n not in _SHARDED)
_REPL_EARLY = tuple(n for n in _REPL_ORDER if n.startswith("s5_"))
_REPL_LATE = tuple(n for n in _REPL_ORDER if n not in _REPL_EARLY)


def _pack(parts, dtype, lead=None):
    nl = 0 if lead is None else 1
    flat = [a.astype(dtype).reshape(a.shape[:nl] + (-1,)) for a in parts]
    cat = jnp.concatenate(flat, axis=nl)
    n = cat.shape[nl]
    unit = _LANES * _ROW_PAD
    total = -(-n // unit) * unit
    cat = jnp.pad(cat, [(0, 0)] * nl + [(0, total - n)])
    return cat.reshape(cat.shape[:nl] + (total // _LANES, _LANES))


def _unpack(packed, shapes, lead=None):
    nl = 0 if lead is None else 1
    flat = packed.reshape(packed.shape[:nl] + (-1,))
    out, off = [], 0
    for s in shapes:
        n = int(np.prod(s))
        piece = flat[..., off:off + n] if nl else flat[off:off + n]
        out.append(piece.reshape(packed.shape[:nl] + tuple(s)))
        off += n
    return out


def _to_full(gathered, axis):
    g = jnp.moveaxis(gathered, 0, axis)
    s = g.shape
    return g.reshape(s[:axis] + (s[axis] * s[axis + 1],) + s[axis + 2:])


def _to_shards(full, axis):
    s = full.shape
    g = full.reshape(s[:axis] + (N_DEV, s[axis] // N_DEV) + s[axis + 1:])
    return jnp.moveaxis(g, axis, 0)


_DIRECT_ROWS = ("xa_wq", "xa_wk", "xa_wv", "xa_wo", "mix_w_out", "s5_w_glu_a", "s5_w_glu_b")
_SMALL16 = ("mix_w_in", "mla_w_uq", "mla_w_ukv")
_SMALL_SHARDED = ("mla_w_uq", "mla_w_ukv") + _EXACT
_SHARD_ROWS = 128


def _exchange_layout(d):
    out = dict(d)
    out["ffn_w_up"] = _pad_groups(d["ffn_w_up"], 2)
    out["ffn_conv_w"] = _pad_groups(d["ffn_conv_w"], 2)
    out["ffn_w_down"] = _pad_groups(d["ffn_w_down"], 1)
    return out


def _train_step(x, mem, positions, target, w, m, v):
    d_model = x.shape[1]
    we_, me_, ve_ = _exchange_layout(w), _exchange_layout(m), _exchange_layout(v)
    sds = jax.ShapeDtypeStruct

    matrices = _DIRECT_ROWS + ("ffn_w_up", "ffn_w_down")
    layer_mats = ("xa_wq", "xa_wk", "xa_wv", "xa_wo", "ffn_w_up", "ffn_w_down")
    shard16 = {n: we_[n].astype(bf16) for n in matrices}
    part_of = {n: _rows_of(_SHARD_ROWS) for n in _DIRECT_ROWS}
    part_of["ffn_w_up"] = _cols_of(we_["ffn_w_up"].shape[2])
    part_of["ffn_w_down"] = _rows_of(we_["ffn_w_down"].shape[1])
    part_shape = {n: we_[n].shape[1:] for n in matrices}

    def full_shape(n):
        r, c = part_shape[n]
        return (r, N_DEV * c) if n == "ffn_w_up" else (N_DEV * r, c)

    def gather(ex, n, layer):
        ex.add(shard16[n], sds(full_shape(n), bf16), _at_layer(layer), part_of[n], (n, layer))

    def scatter(ex, n, layer, grad):
        ex.add(grad, sds((N_DEV,) + part_shape[n], grad.dtype), part_of[n], _slot, (n, layer))

    small16 = _pack([we_[n] for n in _SMALL16], bf16)
    exact = _pack([we_[n] for n in _EXACT], f32)
    ga, ga1, gb, gc = _Exchange("gather_a"), _Exchange("gather_a1"), _Exchange("gather_b"), _Exchange("gather_c")
    ga.add(small16, sds((N_DEV,) + small16.shape, bf16), _whole, _slot, "small16")
    ga1.add(exact, sds((N_DEV,) + exact.shape, f32), _whole, _slot, "exact")
    gather(ga1, "mix_w_out", 0)
    for n in layer_mats:
        gather(gb, n, 0)
    gather(gc, "s5_w_glu_a", 0)
    gather(gc, "s5_w_glu_b", 0)
    for n in layer_mats:
        gather(gc, n, 1)
    state_a, state_a1, state_b, state_c = ga.begin(True), ga1.begin(True), gb.begin(True), gc.begin(True)

    p = {n: w[n] for n in _REPL_ORDER}
    cos, sin = _rope_tables(positions)
    wo = _odd_weights(p, 0, 1, bf16)
    conv_b = _pad_groups(w["ffn_conv_b"], 1)
    prepared = [cos, sin, conv_b, wo["s5_c3"], wo["s5_bt_re"], wo["s5_bt_im"]]
    full = ga.finish(state_a, [ga1.token, gb.token, gc.token] + prepared)
    for n, a in zip(_SMALL16, _unpack(full["small16"], [we_[n].shape for n in _SMALL16], lead=True)):
        p[n] = _to_full(a, _SHARDED[n])
    we = _even_weights(p, 0, 0, bf16)
    we["norm_mix"] = _after(we["norm_mix"], ga.token, ga1.token, gb.token, gc.token, name="after_gather_starts")

    def late_mix_w_out(mixin):
        full.update(ga1.finish(state_a1, [mixin]))
        return full[("mix_w_out", 0)]

    we["mix_w_out"] = late_mix_w_out
    h, s_mix0, hx = _mixer_fwd(x, cos, sin, we, "l0", w["norm_xa"][0][None])
    conv_w, s5_d = [_to_full(a, _SHARDED[n]) for n, a in
                    zip(_EXACT, _unpack(full["exact"], [we_[n].shape for n in _EXACT], lead=True))]

    def layer_weights(layer):
        return dict(norm_xa=w["norm_xa"][layer][None], norm_mem=w["norm_mem"][layer][None],
                    norm_ffn=w["norm_ffn"][layer][None], xa_q_norm=w["xa_q_norm"][layer][None],
                    xa_k_norm=w["xa_k_norm"][layer][None], ffn_conv_w=conv_w[layer],
                    ffn_conv_b=conv_b[layer][None], **{n: full[(n, layer)] for n in layer_mats})

    full.update(gb.finish(state_b, h))
    wl = [layer_weights(0)]
    h, s_xa0, hf = _xattn_fwd(h, mem, wl[0], "l0", hx)
    h, s_ff0 = _ffn_fwd(h, wl[0], "l0", hf)
    full.update(gc.finish(state_c, h))
    wl.append(layer_weights(1))
    wo.update(s5_d=s5_d, s5_w_glu_a=full[("s5_w_glu_a", 0)], s5_w_glu_b=full[("s5_w_glu_b", 0)])
    h, s_mix1, hx = _s5_fwd(h, wo, "l1", wl[1]["norm_xa"])
    h, s_xa1, hf = _xattn_fwd(h, mem, wl[1], "l1", hx)
    (dh, loss), s_ff1 = _ffn_fwd(h, wl[1], "l1", hf, loss_target=target)

    gl = [{}, {}]
    reduces = []

    own_grad = {}

    def reduce_start(name, entries, dh):
        ex = _Exchange(name)
        for n, layer, grad in entries.get("matrices", ()):
            scatter(ex, n, layer, grad)
            own_grad[(n, layer)] = grad
        for key, src, shape, src_at in entries.get("packs", ()):
            ex.add(src, shape, src_at, _slot, key)
        reduces.append((ex, ex.begin(False)))
        return _after(dh, ex.token, name="after_" + name)

    dh, gl[1] = _ffn_bwd(dh, wl[1], s_ff1, "l1")
    dh = reduce_start("reduce_ffn1", dict(matrices=[(n, 1, gl[1][n]) for n in ("ffn_w_up", "ffn_w_down")]), dh)
    dh, g = _xattn_bwd(dh, mem, wl[1], s_xa1, "l1")
    gl[1].update(g)
    dh = reduce_start("reduce_xa1", dict(matrices=[(n, 1, g[n]) for n in ("xa_wq", "xa_wk", "xa_wv", "xa_wo")]), dh)
    dh, g_odd = _s5_bwd(dh, wo, s_mix1, "l1")
    go = _odd_grads(g_odd)
    dh, gl[0] = _ffn_bwd(dh, wl[0], s_ff0, "l0")
    send_early = _pack([go[n].reshape(w[n].shape) for n in _REPL_EARLY], f32)
    dh = reduce_start("reduce_ffn0", dict(
        matrices=[(n, 0, g_odd[n]) for n in ("s5_w_glu_a", "s5_w_glu_b")]
        + [(n, 0, gl[0][n]) for n in ("ffn_w_up", "ffn_w_down")],
        packs=[("repl_early", send_early, sds((N_DEV,) + send_early.shape, f32), _whole)]), dh)
    dh, g = _xattn_bwd(dh, mem, wl[0], s_xa0, "l0")
    gl[0].update(g)
    dh = reduce_start("reduce_xa0", dict(matrices=[(n, 0, g[n]) for n in ("xa_wq", "xa_wk", "xa_wv", "xa_wo")]), dh)
    grad_x, g_even = _mixer_bwd(
        dh, cos, sin, we, s_mix0, "l0",
        on_w_out=lambda grad, dmixin: reduce_start("reduce_w_out", dict(matrices=[("mix_w_out", 0, grad)]), dmixin))

    ge = _even_grads(g_even)
    cat = lambda n: jnp.concatenate([gl[0][n], gl[1][n]], axis=0)
    rg = dict(ge)
    rg["norm_mix"] = jnp.concatenate([ge["norm_mix"], go["norm_mix"]], axis=0)
    for n in ("norm_xa", "norm_mem", "norm_ffn", "xa_q_norm", "xa_k_norm"):
        rg[n] = cat(n)
    rg["ffn_conv_b"] = _unpad_groups(cat("ffn_conv_b"), 1)
    sg = dict(mla_w_uq=ge["mla_w_uq"], mla_w_ukv=ge["mla_w_ukv"], s5_d=go["s5_d"],
              ffn_conv_w=jnp.stack([gl[0]["ffn_conv_w"], gl[1]["ffn_conv_w"]]))
    send_small = _pack([_to_shards(sg[n], _SHARDED[n]) for n in _SMALL_SHARDED], f32, lead=True)
    send_late = _pack([rg[n].reshape(w[n].shape) for n in _REPL_LATE], f32)
    w_in_rows = w["mix_w_in"].shape[2]
    last = _Exchange("reduce_last")
    last.add(g_even["mix_w_in_t"], sds((N_DEV, w_in_rows, d_model), f32), _rows_of(w_in_rows), _slot, "mix_w_in")
    last.add(send_small, sds(send_small.shape, f32), _slot, _slot, "small")
    last.add(send_late, sds((N_DEV,) + send_late.shape, f32), _whole, _slot, "repl_late")
    state_last = last.begin(False)
    slots = {}
    for ex, state in reduces:
        slots.update(ex.finish(state, [grad_x, last.token]))

    me_index = (4 * lax.axis_index("x") + 2 * lax.axis_index("y") + lax.axis_index("c")).astype(jnp.int32).reshape(1)

    def rows_block(r, c):
        return lambda tr: ((tr, c), lambda i, me: (me[0] * (r // tr) + i, 0))

    def own_block(n):
        r, c = part_shape[n]
        if n == "ffn_w_up":
            return lambda tr: ((tr, c), lambda i, me: (i, me[0]))
        return rows_block(r, c)

    out = [{}, {}, {}, {}]
    unpad = dict(ffn_w_up=2, ffn_conv_w=2, ffn_w_down=1)
    for n in matrices:
        layers = range(we_[n].shape[0])
        res = _sum_adam(me_index, [slots[(n, layer)] for layer in layers], [own_grad[(n, layer)] for layer in layers],
                        own_block(n), we_[n], me_[n], ve_[n], name=f"adam_{n}")
        for k in range(4):
            out[k][n] = _unpad_groups(res[k], unpad[n]) if n in unpad else res[k]
    pk = lambda d, order: _pack([d[n] for n in order], f32)[None]
    whole_rows = lambda tr: ((tr, _LANES), lambda i, me: (i, 0))
    res_early = _sum_adam(me_index, [slots["repl_early"]], [send_early], whole_rows, pk(w, _REPL_EARLY),
                          pk(m, _REPL_EARLY), pk(v, _REPL_EARLY), name="adam_repl_early")
    for k in range(4):
        out[k].update(zip(_REPL_EARLY, _unpack(res_early[k][0], [w[n].shape for n in _REPL_EARLY])))
    done = [out[k][n] for k in range(4) for n in matrices + _REPL_EARLY]
    slots = last.finish(state_last, done)
    transposed = lambda d: jnp.swapaxes(d["mix_w_in"], 1, 2)
    res_w_in = _sum_adam(me_index, [slots["mix_w_in"]], [g_even["mix_w_in_t"]], rows_block(w_in_rows, d_model),
                         transposed(w), transposed(m), transposed(v), name="adam_mix_w_in")
    res_small = _sum_adam(me_index, [slots["small"]], [send_small],
                          lambda tr: ((1, tr, _LANES), lambda i, me: (me[0], i, 0)),
                          pk(we_, _SMALL_SHARDED), pk(me_, _SMALL_SHARDED), pk(ve_, _SMALL_SHARDED), name="adam_small")
    res_late = _sum_adam(me_index, [slots["repl_late"]], [send_late], whole_rows, pk(w, _REPL_LATE), pk(m, _REPL_LATE),
                         pk(v, _REPL_LATE), name="adam_repl_late")
    for k in range(4):
        out[k]["mix_w_in"] = jnp.swapaxes(res_w_in[k], 1, 2)
        for n, a in zip(_SMALL_SHARDED, _unpack(res_small[k][0], [we_[n].shape for n in _SMALL_SHARDED])):
            out[k][n] = _unpad_groups(a, unpad[n]) if n in unpad else a
        out[k].update(zip(_REPL_LATE, _unpack(res_late[k][0], [w[n].shape for n in _REPL_LATE])))
    return loss, grad_x, out


_INPUTS = tuple("""x, mem, positions, norm_mix, norm_xa, norm_mem, norm_ffn, xa_wq, xa_wk, xa_wv, xa_wo, xa_q_norm, xa_k_norm, ffn_w_up, ffn_conv_w, ffn_conv_b, ffn_w_down, hg_lb_logits, mix_w_in, hg_out_norm, mla_q_a_norm, mla_w_uq, mla_kv_a_norm, mla_w_ukv, mla_qn_nope, mla_qn_rope, mla_kn_nope, mla_kn_rope, mix_w_out, s5_lam_re, s5_lam_im, s5_log_dt, s5_b_re, s5_b_im, s5_c_re, s5_c_im, s5_d, s5_w_glu_a, s5_w_glu_b, loss_target, m_norm_mix, m_norm_xa, m_norm_mem, m_norm_ffn, m_xa_wq, m_xa_wk, m_xa_wv, m_xa_wo, m_xa_q_norm, m_xa_k_norm, m_ffn_w_up, m_ffn_conv_w, m_ffn_conv_b, m_ffn_w_down, m_hg_lb_logits, m_mix_w_in, m_hg_out_norm, m_mla_q_a_norm, m_mla_w_uq, m_mla_kv_a_norm, m_mla_w_ukv, m_mla_qn_nope, m_mla_qn_rope, m_mla_kn_nope, m_mla_kn_rope, m_mix_w_out, m_s5_lam_re, m_s5_lam_im, m_s5_log_dt, m_s5_b_re, m_s5_b_im, m_s5_c_re, m_s5_c_im, m_s5_d, m_s5_w_glu_a, m_s5_w_glu_b, v_norm_mix, v_norm_xa, v_norm_mem, v_norm_ffn, v_xa_wq, v_xa_wk, v_xa_wv, v_xa_wo, v_xa_q_norm, v_xa_k_norm, v_ffn_w_up, v_ffn_conv_w, v_ffn_conv_b, v_ffn_w_down, v_hg_lb_logits, v_mix_w_in, v_hg_out_norm, v_mla_q_a_norm, v_mla_w_uq, v_mla_kv_a_norm, v_mla_w_ukv, v_mla_qn_nope, v_mla_qn_rope, v_mla_kn_nope, v_mla_kn_rope, v_mix_w_out, v_s5_lam_re, v_s5_lam_im, v_s5_log_dt, v_s5_b_re, v_s5_b_im, v_s5_c_re, v_s5_c_im, v_s5_d, v_s5_w_glu_a, v_s5_w_glu_b""".replace(" ", "").split(","))


def kernel(x, mem, positions, norm_mix, norm_xa, norm_mem, norm_ffn, xa_wq, xa_wk, xa_wv, xa_wo, xa_q_norm, xa_k_norm, ffn_w_up, ffn_conv_w, ffn_conv_b, ffn_w_down, hg_lb_logits, mix_w_in, hg_out_norm, mla_q_a_norm, mla_w_uq, mla_kv_a_norm, mla_w_ukv, mla_qn_nope, mla_qn_rope, mla_kn_nope, mla_kn_rope, mix_w_out, s5_lam_re, s5_lam_im, s5_log_dt, s5_b_re, s5_b_im, s5_c_re, s5_c_im, s5_d, s5_w_glu_a, s5_w_glu_b, loss_target, m_norm_mix, m_norm_xa, m_norm_mem, m_norm_ffn, m_xa_wq, m_xa_wk, m_xa_wv, m_xa_wo, m_xa_q_norm, m_xa_k_norm, m_ffn_w_up, m_ffn_conv_w, m_ffn_conv_b, m_ffn_w_down, m_hg_lb_logits, m_mix_w_in, m_hg_out_norm, m_mla_q_a_norm, m_mla_w_uq, m_mla_kv_a_norm, m_mla_w_ukv, m_mla_qn_nope, m_mla_qn_rope, m_mla_kn_nope, m_mla_kn_rope, m_mix_w_out, m_s5_lam_re, m_s5_lam_im, m_s5_log_dt, m_s5_b_re, m_s5_b_im, m_s5_c_re, m_s5_c_im, m_s5_d, m_s5_w_glu_a, m_s5_w_glu_b, v_norm_mix, v_norm_xa, v_norm_mem, v_norm_ffn, v_xa_wq, v_xa_wk, v_xa_wv, v_xa_wo, v_xa_q_norm, v_xa_k_norm, v_ffn_w_up, v_ffn_conv_w, v_ffn_conv_b, v_ffn_w_down, v_hg_lb_logits, v_mix_w_in, v_hg_out_norm, v_mla_q_a_norm, v_mla_w_uq, v_mla_kv_a_norm, v_mla_w_ukv, v_mla_qn_nope, v_mla_qn_rope, v_mla_kn_nope, v_mla_kn_rope, v_mix_w_out, v_s5_lam_re, v_s5_lam_im, v_s5_log_dt, v_s5_b_re, v_s5_b_im, v_s5_c_re, v_s5_c_im, v_s5_d, v_s5_w_glu_a, v_s5_w_glu_b):
    vals = dict(zip(_INPUTS, (x, mem, positions, norm_mix, norm_xa, norm_mem, norm_ffn, xa_wq, xa_wk, xa_wv, xa_wo, xa_q_norm, xa_k_norm, ffn_w_up, ffn_conv_w, ffn_conv_b, ffn_w_down, hg_lb_logits, mix_w_in, hg_out_norm, mla_q_a_norm, mla_w_uq, mla_kv_a_norm, mla_w_ukv, mla_qn_nope, mla_qn_rope, mla_kn_nope, mla_kn_rope, mix_w_out, s5_lam_re, s5_lam_im, s5_log_dt, s5_b_re, s5_b_im, s5_c_re, s5_c_im, s5_d, s5_w_glu_a, s5_w_glu_b, loss_target, m_norm_mix, m_norm_xa, m_norm_mem, m_norm_ffn, m_xa_wq, m_xa_wk, m_xa_wv, m_xa_wo, m_xa_q_norm, m_xa_k_norm, m_ffn_w_up, m_ffn_conv_w, m_ffn_conv_b, m_ffn_w_down, m_hg_lb_logits, m_mix_w_in, m_hg_out_norm, m_mla_q_a_norm, m_mla_w_uq, m_mla_kv_a_norm, m_mla_w_ukv, m_mla_qn_nope, m_mla_qn_rope, m_mla_kn_nope, m_mla_kn_rope, m_mix_w_out, m_s5_lam_re, m_s5_lam_im, m_s5_log_dt, m_s5_b_re, m_s5_b_im, m_s5_c_re, m_s5_c_im, m_s5_d, m_s5_w_glu_a, m_s5_w_glu_b, v_norm_mix, v_norm_xa, v_norm_mem, v_norm_ffn, v_xa_wq, v_xa_wk, v_xa_wv, v_xa_wo, v_xa_q_norm, v_xa_k_norm, v_ffn_w_up, v_ffn_conv_w, v_ffn_conv_b, v_ffn_w_down, v_hg_lb_logits, v_mix_w_in, v_hg_out_norm, v_mla_q_a_norm, v_mla_w_uq, v_mla_kv_a_norm, v_mla_w_ukv, v_mla_qn_nope, v_mla_qn_rope, v_mla_kn_nope, v_mla_kn_rope, v_mix_w_out, v_s5_lam_re, v_s5_lam_im, v_s5_log_dt, v_s5_b_re, v_s5_b_im, v_s5_c_re, v_s5_c_im, v_s5_d, v_s5_w_glu_a, v_s5_w_glu_b)))
    w = {n: vals[n] for n in _WEIGHTS}
    m = {n: vals["m_" + n] for n in _WEIGHTS}
    v = {n: vals["v_" + n] for n in _WEIGHTS}
    loss, grad_x, res = _train_step(vals["x"][0], vals["mem"][0], vals["positions"][0], vals["loss_target"][0],
                                    w, m, v)
    loss = lax.psum(loss[0, 0], ("x", "y", "c"))
    return (loss, grad_x[None], *[r[n] for r in res for n in _WEIGHTS])
```

```python
import functools

import jax
import jax.numpy as jnp
import numpy as np
from jax import lax
from jax.experimental import pallas as pl
from jax.experimental.pallas import tpu as pltpu

f32 = jnp.float32
bf16 = jnp.bfloat16

EPS = 1e-6
N_DEV = 8
VMEM_LIMIT = 52 * 1024 * 1024

HG_HEADS = 4
HG_DIM = 128
HG_WIDTH = HG_HEADS * HG_DIM
HG_CHUNK = 64
HG_SUB = 16
MLA_HEADS = 4
MLA_Q_RANK = 256
MLA_KV_RANK = 128
MLA_NOPE = 128
MLA_ROPE = 64
MLA_V = 128
MLA_QK = MLA_NOPE + MLA_ROPE
MLA_QK_PAD = 256
ROPE_BASE = 10000.0
IN_WIDTH = 4 * HG_WIDTH + MLA_Q_RANK + MLA_KV_RANK + MLA_ROPE
IN_PAD = 2560
XA_HEADS = 4
XA_DIM = 256
S5_GROUP = 16
S5_GROUPS = 64
S5_STATE = 64
CONV_W = 3

ADAM_LR = 0.001
ADAM_B1 = 0.9
ADAM_B2 = 0.999
ADAM_EPS = 1e-08
ADAM_WD = 0.01
ADAM_STEP = 10

_NT = (((1,), (1,)), ((), ()))
_TN = (((0,), (0,)), ((), ()))


def _pick(n, cands):
    for c in cands:
        if n % c == 0:
            return c
    return n


def _cparams(sem):
    return pltpu.CompilerParams(dimension_semantics=sem, vmem_limit_bytes=VMEM_LIMIT)


_MM_BUDGET = 36 * 1024 * 1024
_MM_TILES = ((1024, 1024), (1024, 512), (512, 1024), (512, 512), (512, 256), (256, 512), (256, 256), (256, 128),
             (128, 256), (128, 128))


def _mm(a, b, *, name, ta=False, tb=False, out_dtype=f32, add=None, b2=None, kslab=None, norm=None, rms_bwd=None,
        sq_err=None):
    m, k = (a.shape[1], a.shape[0]) if ta else a.shape
    nb = b.shape[0] if tb else b.shape[1]
    n = nb * (2 if b2 is not None else 1)
    slab, nslab = kslab if kslab is not None else (0, 1)
    assert (b.shape[1] // nslab if tb else b.shape[0]) == k, (a.shape, b.shape, ta, tb)
    assert b2 is None or (not tb and b2.shape == b.shape)
    full_rows = norm is not None or rms_bwd is not None
    assert not (full_rows and b2 is not None) and not (norm is not None and rms_bwd is not None)
    isz = lambda x: jnp.dtype(x.dtype).itemsize
    bm = bn = None
    for cm, cn in _MM_TILES:
        if m % cm or nb % cn or (full_rows and cn != n):
            continue
        need = 2 * (cm * k * isz(a) + cn * k * isz(b) * (2 if b2 is not None else 1)
                    + cm * cn * (jnp.dtype(out_dtype).itemsize + (4 if add is not None else 0)
                                 + (2 if norm is not None else 0) + (8 if rms_bwd is not None else 0)
                                 + (4 if sq_err is not None else 0)))
        if need <= _MM_BUDGET:
            bm, bn = cm, cn
            break
    assert bm is not None, (name, a.shape, b.shape)
    half = nb // bn
    dims = (((0 if ta else 1,), (1 if tb else 0,)), ((), ()))

    def body(*refs):
        refs = list(refs)
        a_ref, b_ref = refs[0], refs[1]
        b2_ref = refs.pop(2) if b2 is not None else None
        add_ref = refs.pop(2) if add is not None else None
        gain_ref = refs.pop(2) if norm is not None else None
        x_ref, xgain_ref, through_ref = (refs.pop(2), refs.pop(2), refs.pop(2)) if rms_bwd is not None else (None,) * 3
        target_ref = refs.pop(2) if sq_err is not None else None
        o_ref = refs[2]
        extra_ref = refs[3] if (norm is not None or rms_bwd is not None or sq_err is not None) else None

        def run(rhs_ref):
            r = lax.dot_general(a_ref[...].astype(bf16), rhs_ref[...].astype(bf16), dims, preferred_element_type=f32)
            if add_ref is not None:
                r = r + add_ref[...].astype(f32)
            if sq_err is not None:
                err = r - target_ref[...]
                o_ref[...] = (err * (1.0 / n)).astype(o_ref.dtype)

                @pl.when((pl.program_id(0) == 0) & (pl.program_id(1) == 0))
                def _():
                    extra_ref[...] = jnp.zeros_like(extra_ref)
                extra_ref[...] += jnp.sum(jnp.sum(err * err, axis=-1, keepdims=True), axis=0, keepdims=True) * (0.5 / n)
            elif rms_bwd is None:
                o_ref[...] = r.astype(o_ref.dtype)
            else:
                _, vjp = jax.vjp(_rms, x_ref[...], xgain_ref[...])
                dx, dgain = vjp(r)
                o_ref[...] = (through_ref[...] + dx).astype(o_ref.dtype)

                @pl.when(pl.program_id(0) == 0)
                def _():
                    extra_ref[...] = jnp.zeros_like(extra_ref)
                extra_ref[...] += dgain
            if norm is not None:
                extra_ref[...] = _rms(r, gain_ref[...]).astype(extra_ref.dtype)

        if b2_ref is None:
            run(b_ref)
        else:
            pl.when(pl.program_id(1) < half)(lambda: run(b_ref))
            pl.when(pl.program_id(1) >= half)(lambda: run(b2_ref))

    a_spec = pl.BlockSpec((k, bm), lambda i, j: (0, i)) if ta else pl.BlockSpec((bm, k), lambda i, j: (i, 0))
    if tb:
        b_spec = pl.BlockSpec((bn, k), lambda i, j: (j, slab))
    elif b2 is None:
        b_spec = pl.BlockSpec((k, bn), lambda i, j: (0, j))
    else:
        b_spec = pl.BlockSpec((k, bn), lambda i, j: (0, jnp.minimum(j, half - 1)))
    in_specs = [a_spec, b_spec]
    args = [a, b]
    if b2 is not None:
        in_specs.append(pl.BlockSpec((k, bn), lambda i, j: (0, jnp.maximum(j - half, 0))))
        args.append(b2)
    if add is not None:
        in_specs.append(pl.BlockSpec((bm, bn), lambda i, j: (i, j)))
        args.append(add)
    out_blk = pl.BlockSpec((bm, bn), lambda i, j: (i, j))
    out_specs, out_shape = out_blk, jax.ShapeDtypeStruct((m, n), out_dtype)
    row_blk = pl.BlockSpec((1, bn), lambda i, j: (0, j))
    if norm is not None:
        in_specs.append(row_blk)
        args.append(norm)
        out_specs, out_shape = [out_blk, out_blk], [out_shape, jax.ShapeDtypeStruct((m, n), bf16)]
    if rms_bwd is not None:
        in_specs += [out_blk, row_blk, out_blk]
        args += list(rms_bwd)
        out_specs, out_shape = [out_blk, row_blk], [out_shape, jax.ShapeDtypeStruct((1, n), f32)]
    if sq_err is not None:
        in_specs.append(out_blk)
        args.append(sq_err)
        out_specs = [out_blk, pl.BlockSpec((1, 1), lambda i, j: (0, 0))]
        out_shape = [out_shape, jax.ShapeDtypeStruct((1, 1), f32)]
    carried = rms_bwd is not None or sq_err is not None
    return pl.pallas_call(
        body, grid=(m // bm, n // bn), in_specs=in_specs, out_specs=out_specs, out_shape=out_shape,
        compiler_params=_cparams(("arbitrary" if carried else "parallel", "arbitrary" if sq_err is not None else "parallel")),
        name=name)(*args)


def _as_tuple(x):
    return tuple(x) if isinstance(x, (tuple, list)) else (x,)


def _full_spec(p):
    nd = p.ndim
    return pl.BlockSpec(p.shape, lambda i, _nd=nd: (0,) * _nd)


def _window(a, start, width):
    assert start % width == 0 and width % 128 == 0
    return (a, start // width, width)


def _row_array(x):
    return x[0] if isinstance(x, tuple) else x


def _row_shape(x):
    return (x[0].shape[0], x[2]) if isinstance(x, tuple) else x.shape


def _row_spec(x, tile):
    if isinstance(x, tuple):
        return pl.BlockSpec((tile, x[2]), lambda i, _b=x[1]: (i, _b))
    return pl.BlockSpec((tile, x.shape[1]), lambda i: (i, 0))


def _rows(fn, rows, params, outs, *, name, tile=256, accs=()):
    length = _row_shape(rows[0])[0]
    tile = min(tile, length)
    nr, npar, no = len(rows), len(params), len(outs)

    def body(*refs):
        r, p, o = refs[:nr], refs[nr:nr + npar], refs[nr + npar:]
        res = _as_tuple(fn(*[x[...].astype(f32) for x in r], *[x[...] for x in p]))
        for kk in range(no):
            o[kk][...] = res[kk].astype(o[kk].dtype)
        if accs:
            @pl.when(pl.program_id(0) == 0)
            def _():
                for kk in range(no, no + len(accs)):
                    o[kk][...] = jnp.zeros_like(o[kk])
            for kk in range(no, no + len(accs)):
                o[kk][...] += res[kk]

    in_specs = [_row_spec(x, tile) for x in rows] + [_full_spec(p) for p in params]
    out_specs = [pl.BlockSpec((tile, w), lambda i: (i, 0)) for w, _ in outs]
    out_shape = [jax.ShapeDtypeStruct((length, w), d) for w, d in outs]
    for s in accs:
        out_specs.append(pl.BlockSpec(s, lambda i, _nd=len(s): (0,) * _nd))
        out_shape.append(jax.ShapeDtypeStruct(s, f32))
    res = pl.pallas_call(body, grid=(length // tile,), in_specs=in_specs, out_specs=out_specs, out_shape=out_shape,
                         compiler_params=_cparams(("arbitrary",)), name=name)(*[_row_array(x) for x in rows], *params)
    return res


def _rows_bwd(fn, rows, params, cts, *, name, rgrad, pgrad, tile=256, addends=None):
    addends = {i: (a if isinstance(a, list) else [(a, 0)]) for i, a in (addends or {}).items()}
    length = _row_shape(rows[0])[0]
    tile = min(tile, length)
    nr, npar, nc = len(rows), len(params), len(cts)
    ridx = [i for i in range(nr) if rgrad[i] is not None]
    pidx = [i for i in range(npar) if pgrad[i]]
    flat_addends = [(i, a, off) for i in sorted(addends) for a, off in addends[i]]
    na = len(flat_addends)

    def body(*refs):
        r, p, c = refs[:nr], refs[nr:nr + npar], refs[nr + npar:nr + npar + nc]
        ad = refs[nr + npar + nc:nr + npar + nc + na]
        o = refs[nr + npar + nc + na:]
        rv = [x[...].astype(f32) for x in r]
        pv = [x[...] for x in p]
        cv = tuple(x[...].astype(f32) for x in c)

        def g(*d):
            rr, pp = list(rv), list(pv)
            for n_, i_ in enumerate(ridx):
                rr[i_] = d[n_]
            for n_, i_ in enumerate(pidx):
                pp[i_] = d[len(ridx) + n_]
            return _as_tuple(fn(*rr, *pp))

        _, vjp = jax.vjp(g, *[rv[i] for i in ridx], *[pv[i] for i in pidx])
        grads = vjp(cv)
        for n_, i_ in enumerate(ridx):
            val = grads[n_]
            for k_, (j_, a_, off) in enumerate(flat_addends):
                if j_ == i_:
                    extra = ad[k_][...].astype(f32)
                    if extra.shape[1] != val.shape[1]:
                        extra = jnp.pad(extra, ((0, 0), (off, val.shape[1] - off - extra.shape[1])))
                    val = val + extra
            o[n_][...] = val.astype(o[n_].dtype)
        if pidx:
            @pl.when(pl.program_id(0) == 0)
            def _():
                for n_ in range(len(pidx)):
                    o[len(ridx) + n_][...] = jnp.zeros_like(o[len(ridx) + n_])
            for n_ in range(len(pidx)):
                o[len(ridx) + n_][...] += grads[len(ridx) + n_]

    plain = lambda shape: pl.BlockSpec((tile, shape[1]), lambda i: (i, 0))
    in_specs = ([_row_spec(x, tile) for x in rows] + [_full_spec(p) for p in params] + [plain(x.shape) for x in cts]
                + [plain(a.shape) for _, a, _ in flat_addends])
    out_specs = [plain(_row_shape(rows[i])) for i in ridx] + [_full_spec(params[i]) for i in pidx]
    out_shape = ([jax.ShapeDtypeStruct(_row_shape(rows[i]), rgrad[i]) for i in ridx]
                 + [jax.ShapeDtypeStruct(params[i].shape, f32) for i in pidx])
    res = pl.pallas_call(body, grid=(length // tile,), in_specs=in_specs, out_specs=out_specs, out_shape=out_shape,
                         compiler_params=_cparams(("arbitrary",)), name=name)(
        *[_row_array(x) for x in rows], *params, *cts, *[a for _, a, _ in flat_addends])
    return list(res[:len(ridx)]), list(res[len(ridx):])


def _rms(x, g):
    return x * lax.rsqrt(jnp.mean(x * x, axis=-1, keepdims=True) + EPS) * g


def _rms_twice(x, g):
    y = _rms(x, g)
    return y, y


def _silu(x):
    return x * jax.nn.sigmoid(x)


_UP_COLS = 512


def _ffn_up_conv(hf, w_up, cw, cb, *, name):
    length, d = hf.shape
    ff = w_up.shape[1] // 2
    bm = _pick(length, (1024, 512, 256))
    bn = _UP_COLS
    nj = ff // bn

    def body(hf_ref, wg_ref, wv_ref, cwg, cwv, cbg, cbv, ug_ref, uv_ref, a_ref, halo_g, halo_v):
        i, j = pl.program_id(0), pl.program_id(1)
        x = hf_ref[...]
        rows = lax.broadcasted_iota(jnp.int32, (bm, bn), 0)
        pad = jnp.zeros((bm - 8, bn), f32)

        def half(w_ref, cw_ref, cb_ref, u_ref, halo):
            u = jnp.dot(x, w_ref[...], preferred_element_type=f32).astype(bf16)
            u_ref[...] = u
            u = u.astype(f32)
            prev = jnp.where(i == 0, 0.0, halo[j])
            x1 = jnp.where(rows >= 1, pltpu.roll(u, 1, axis=0), jnp.concatenate([pltpu.roll(prev, 1, axis=0), pad]))
            x2 = jnp.where(rows >= 2, pltpu.roll(u, 2, axis=0), jnp.concatenate([pltpu.roll(prev, 2, axis=0), pad]))
            halo[j] = u[bm - 8:, :]
            return cw_ref[2:3, :] * u + cw_ref[1:2, :] * x1 + cw_ref[0:1, :] * x2 + cb_ref[...]

        g = half(wg_ref, cwg, cbg, ug_ref, halo_g)
        v = half(wv_ref, cwv, cbv, uv_ref, halo_v)
        a_ref[...] = (_silu(g) * v).astype(a_ref.dtype)

    col = lambda r, off: pl.BlockSpec((r, bn), lambda i, j, _o=off: (0, j + _o))
    out_blk = pl.BlockSpec((bm, bn), lambda i, j: (i, j))
    sds = jax.ShapeDtypeStruct((length, ff), bf16)
    return pl.pallas_call(
        body, grid=(length // bm, nj),
        in_specs=[pl.BlockSpec((bm, d), lambda i, j: (i, 0)), col(d, 0), col(d, nj), col(CONV_W, 0), col(CONV_W, nj),
                  col(1, 0), col(1, nj)],
        out_specs=[out_blk] * 3, out_shape=[sds] * 3,
        scratch_shapes=[pltpu.VMEM((nj, 8, bn), f32), pltpu.VMEM((nj, 8, bn), f32)],
        compiler_params=_cparams(("arbitrary", "arbitrary")), name=name)(hf, w_up, w_up, cw, cw, cb, cb)


def _ffn_da_dconv(dout, w_down, u_g, u_v, cw, cb, *, name):
    length, d = dout.shape
    ff = u_g.shape[1]
    bm = _pick(length, (1024, 512, 256))
    bn = _UP_COLS
    nj, ni = ff // bn, length // bm

    def body(dout_ref, wd_ref, ug_ref, uv_ref, pg_ref, pv_ref, cwg, cwv, cbg, cbv,
             dug_ref, duv_ref, sums_g_ref, sums_v_ref, halo_g, halo_v, acc_g, acc_v):
        s, j = pl.program_id(0), pl.program_id(1)
        rows = lax.broadcasted_iota(jnp.int32, (bm, bn), 0)
        row8 = lax.broadcasted_iota(jnp.int32, (8, bn), 0)
        pad = jnp.zeros((bm - 8, bn), f32)
        da = lax.dot_general(dout_ref[...].astype(bf16), wd_ref[...], _NT, preferred_element_type=f32)

        def conv(u_ref, p_ref, cw_ref, cb_ref):
            x = u_ref[...].astype(f32)
            prev = jnp.where(s == ni - 1, 0.0, p_ref[...].astype(f32))
            x1 = jnp.where(rows >= 1, pltpu.roll(x, 1, axis=0), jnp.concatenate([pltpu.roll(prev, 1, axis=0), pad]))
            x2 = jnp.where(rows >= 2, pltpu.roll(x, 2, axis=0), jnp.concatenate([pltpu.roll(prev, 2, axis=0), pad]))
            return cw_ref[2:3, :] * x + cw_ref[1:2, :] * x1 + cw_ref[0:1, :] * x2 + cb_ref[...], x, x1, x2

        g, xg, xg1, xg2 = conv(ug_ref, pg_ref, cwg, cbg)
        v, xv, xv1, xv2 = conv(uv_ref, pv_ref, cwv, cbv)
        sg = jax.nn.sigmoid(g)
        dg = da * v * (sg * (1.0 + g * (1.0 - sg)))
        dv = da * (g * sg)

        def back(dy, x, x1, x2, cw_ref, du_ref, sums_ref, halo, acc):
            nxt = jnp.where(s == 0, 0.0, halo[j])
            up1 = jnp.where(rows < bm - 1, pltpu.roll(dy, bm - 1, axis=0),
                            jnp.concatenate([pad, pltpu.roll(nxt, 7, axis=0)]))
            up2 = jnp.where(rows < bm - 2, pltpu.roll(dy, bm - 2, axis=0),
                            jnp.concatenate([pad, pltpu.roll(nxt, 6, axis=0)]))
            halo[j] = dy[:8, :]
            du_ref[...] = (cw_ref[2:3, :] * dy + cw_ref[1:2, :] * up1 + cw_ref[0:1, :] * up2).astype(du_ref.dtype)
            col = lambda t: jnp.sum(t, axis=0, keepdims=True)
            part = jnp.where(row8 == 0, col(dy * x2), jnp.where(row8 == 1, col(dy * x1), jnp.where(
                row8 == 2, col(dy * x), jnp.where(row8 == 3, col(dy), 0.0))))
            total = jnp.where(s == 0, part, acc[j] + part)
            acc[j] = total
            sums_ref[...] = total

        back(dg, xg, xg1, xg2, cwg, dug_ref, sums_g_ref, halo_g, acc_g)
        back(dv, xv, xv1, xv2, cwv, duv_ref, sums_v_ref, halo_v, acc_v)

    rb = lambda s: ni - 1 - s
    tile = pl.BlockSpec((bm, bn), lambda s, j: (rb(s), j))
    before = pl.BlockSpec((8, bn), lambda s, j: (jnp.maximum(rb(s) * (bm // 8) - 1, 0), j))
    col = lambda r, off: pl.BlockSpec((r, bn), lambda s, j, _o=off: (0, j + _o))
    sds = jax.ShapeDtypeStruct
    dug, duv, sums_g, sums_v = pl.pallas_call(
        body, grid=(ni, nj),
        in_specs=[pl.BlockSpec((bm, d), lambda s, j: (rb(s), 0)), pl.BlockSpec((bn, d), lambda s, j: (j, 0)),
                  tile, tile, before, before, col(CONV_W, 0), col(CONV_W, nj), col(1, 0), col(1, nj)],
        out_specs=[tile, tile] + [pl.BlockSpec((8, bn), lambda s, j: (s, j))] * 2,
        out_shape=[sds((length, ff), bf16), sds((length, ff), bf16), sds((ni * 8, ff), f32), sds((ni * 8, ff), f32)],
        scratch_shapes=[pltpu.VMEM((nj, 8, bn), f32)] * 4,
        compiler_params=_cparams(("arbitrary", "arbitrary")), name=name)(
        dout, w_down, u_g, u_v, u_g, u_v, cw, cw, cb, cb)
    last = (ni - 1) * 8
    both = lambda lo, hi: jnp.concatenate([sums_g[last + lo:last + hi], sums_v[last + lo:last + hi]], axis=1)
    return dug, duv, both(0, CONV_W), both(CONV_W, CONV_W + 1)


def _ffn_fwd(h, w, tag, hf=None, loss_target=None):
    if hf is None:
        hf, = _rows(_rms, [h], [w["norm_ffn"]], [(h.shape[1], bf16)], name=f"ffn_norm_{tag}")
    u_g, u_v, a = _ffn_up_conv(hf, w["ffn_w_up"], w["ffn_conv_w"], w["ffn_conv_b"], name=f"ffn_up_{tag}")
    out = _mm(a, w["ffn_w_down"], add=h, sq_err=loss_target, name=f"ffn_down_{tag}")
    return out, (h, hf, u_g, u_v, a)


def _ffn_bwd(dout, w, saved, tag):
    h, hf, u_g, u_v, a = saved
    g = {"ffn_w_down": _mm(a, dout, ta=True, out_dtype=bf16, name=f"ffn_dwdown_{tag}")}
    dug, duv, g["ffn_conv_w"], g["ffn_conv_b"] = _ffn_da_dconv(
        dout, w["ffn_w_down"], u_g, u_v, w["ffn_conv_w"], w["ffn_conv_b"], name=f"ffn_dconv_{tag}")
    dhf = _mm(dug, w["ffn_w_up"], tb=True, kslab=(0, 2), name=f"ffn_dhf_g_{tag}")
    dh, g["norm_ffn"] = _mm(duv, w["ffn_w_up"], tb=True, kslab=(1, 2), add=dhf, rms_bwd=(h, w["norm_ffn"], dout),
                            name=f"ffn_dhf_v_{tag}")
    g["ffn_w_up"] = _mm(hf, dug, ta=True, b2=duv, out_dtype=bf16, name=f"ffn_dwup_{tag}")
    return dh, g


def _xattn_fn(qx, kx, vx, qg, kg):
    outs = []
    for hh in range(XA_HEADS):
        sl = slice(hh * XA_DIM, (hh + 1) * XA_DIM)
        q = _rms(qx[:, sl], qg).astype(bf16)
        k = _rms(kx[:, sl], kg).astype(bf16)
        s = lax.dot_general(q, k, _NT, preferred_element_type=f32) * (XA_DIM ** -0.5)
        s = s - jnp.max(s, axis=-1, keepdims=True)
        p = jnp.exp(s)
        p = p / jnp.sum(p, axis=-1, keepdims=True)
        outs.append(jnp.dot(p.astype(bf16), vx[:, sl].astype(bf16), preferred_element_type=f32))
    return jnp.concatenate(outs, axis=-1)


def _xattn_fwd(h, mem, w, tag, hx=None):
    d = h.shape[1]
    if hx is None:
        hx, = _rows(_rms, [h], [w["norm_xa"]], [(d, bf16)], name=f"xa_norm_{tag}")
    qx = _mm(hx, w["xa_wq"], out_dtype=bf16, name=f"xa_q_{tag}")
    m, = _rows(_rms, [mem], [w["norm_mem"]], [(d, bf16)], name=f"xa_mnorm_{tag}")
    kx = _mm(m, w["xa_wk"], name=f"xa_k_{tag}")
    vx = _mm(m, w["xa_wv"], name=f"xa_v_{tag}")
    o, = _rows(_xattn_fn, [qx], [kx, vx, w["xa_q_norm"], w["xa_k_norm"]], [(d, bf16)], tile=1024,
               name=f"xa_attn_{tag}")
    out, hf = _mm(o, w["xa_wo"], add=h, norm=w["norm_ffn"], name=f"xa_o_{tag}")
    return out, (h, hx, qx, m, kx, vx, o), hf


def _xattn_bwd(dout, mem, w, saved, tag):
    h, hx, qx, m, kx, vx, o = saved
    g = {}
    do = _mm(dout, w["xa_wo"], tb=True, out_dtype=bf16, name=f"xa_do_{tag}")
    g["xa_wo"] = _mm(o, dout, ta=True, out_dtype=bf16, name=f"xa_dwo_{tag}")
    (dqx,), (dkx, dvx, g["xa_q_norm"], g["xa_k_norm"]) = _rows_bwd(
        _xattn_fn, [qx], [kx, vx, w["xa_q_norm"], w["xa_k_norm"]], [do], rgrad=[bf16], pgrad=[True] * 4,
        tile=1024, name=f"xa_dattn_{tag}")
    dh, g["norm_xa"] = _mm(dqx, w["xa_wq"], tb=True, rms_bwd=(h, w["norm_xa"], dout), name=f"xa_dhx_{tag}")
    g["xa_wq"] = _mm(hx, dqx, ta=True, out_dtype=bf16, name=f"xa_dwq_{tag}")
    dm = _mm(dkx, w["xa_wk"], tb=True, name=f"xa_dm_k_{tag}")
    dm = _mm(dvx, w["xa_wv"], tb=True, add=dm, name=f"xa_dm_v_{tag}")
    g["xa_wk"] = _mm(m, dkx, ta=True, out_dtype=bf16, name=f"xa_dwk_{tag}")
    g["xa_wv"] = _mm(m, dvx, ta=True, out_dtype=bf16, name=f"xa_dwv_{tag}")
    _, (g["norm_mem"],) = _rows_bwd(_rms, [mem], [w["norm_mem"]], [dm], rgrad=[None], pgrad=[True],
                                    name=f"xa_dmnorm_{tag}")
    return dh, g


_HG_GROUP = 8


def _hg_chunk(q, k, v, g, *sts):
    c = q.shape[0]
    heads = [slice(h * HG_DIM, (h + 1) * HG_DIM) for h in range(len(sts))]
    tri = (lax.broadcasted_iota(jnp.int32, (c, c), 0) >= lax.broadcasted_iota(jnp.int32, (c, c), 1)).astype(f32)
    b = jnp.dot(tri, g, precision=lax.Precision.HIGHEST, preferred_element_type=f32)
    bend = jnp.sum(g, axis=0, keepdims=True)
    qe = (q * jnp.exp(b)).astype(bf16)
    kd = (k * jnp.exp(bend - b)).astype(bf16)
    vb = v.astype(bf16)
    decay = jnp.exp(bend)
    o_inter = [lax.dot_general(qe[:, hs], st.astype(bf16), _NT, preferred_element_type=f32) for hs, st in zip(heads, sts)]
    new = [st * decay[:, hs] + lax.dot_general(vb[:, hs], kd[:, hs], _TN, preferred_element_type=f32)
           for hs, st in zip(heads, sts)]
    outs = []
    for i in range(c // HG_SUB):
        lo, n = HG_SUB * i, HG_SUB * (i + 1)
        ref = jnp.sum(g[:lo], axis=0, keepdims=True) if i else jnp.zeros((1, g.shape[1]), f32)
        qh = (q[lo:n] * jnp.exp(b[lo:n] - ref)).astype(bf16)
        kh = (k[:n] * jnp.exp(ref - b[:n])).astype(bf16)
        keep = (lax.broadcasted_iota(jnp.int32, (HG_SUB, n), 1)
                <= lo + lax.broadcasted_iota(jnp.int32, (HG_SUB, n), 0))
        scores = [lax.dot_general(qh[:, hs], kh[:, hs], _NT, preferred_element_type=f32) for hs in heads]
        scores = [jnp.where(keep, a, 0.0).astype(bf16) for a in scores]
        outs.append(jnp.concatenate([jnp.dot(a, vb[:n, hs], preferred_element_type=f32)
                                     for a, hs in zip(scores, heads)], axis=1))
    return (jnp.concatenate(outs, axis=0) + jnp.concatenate(o_inter, axis=1), *new)


def _hg_fwd(q, k, v, g, *, name):
    length = q.shape[0]
    rows = _HG_GROUP * HG_CHUNK
    ng = length // rows
    nc = length // HG_CHUNK

    def body(q_ref, k_ref, v_ref, g_ref, o_ref, st_ref, state):
        @pl.when(pl.program_id(0) == 0)
        def _():
            state[...] = jnp.zeros_like(state)

        states = [state[h] for h in range(HG_HEADS)]
        for ci in range(_HG_GROUP):
            sl = slice(ci * HG_CHUNK, (ci + 1) * HG_CHUNK)
            for h in range(HG_HEADS):
                st_ref[h, ci] = states[h]
            o, *states = _hg_chunk(q_ref[sl, :], k_ref[sl, :], v_ref[sl, :], g_ref[sl, :], *states)
            o_ref[sl, :] = o
        for h in range(HG_HEADS):
            state[h] = states[h]

    blk = pl.BlockSpec((rows, HG_WIDTH), lambda c: (c, 0))
    return pl.pallas_call(
        body, grid=(ng,), in_specs=[blk] * 4,
        out_specs=[blk, pl.BlockSpec((HG_HEADS, _HG_GROUP, HG_DIM, HG_DIM), lambda c: (0, c, 0, 0))],
        out_shape=[jax.ShapeDtypeStruct((length, HG_WIDTH), f32),
                   jax.ShapeDtypeStruct((HG_HEADS, nc, HG_DIM, HG_DIM), f32)],
        scratch_shapes=[pltpu.VMEM((HG_HEADS, HG_DIM, HG_DIM), f32)],
        compiler_params=_cparams(("arbitrary",)), name=name)(q, k, v, g)


def _hg_bwd(q, k, v, g, states, do, *, name):
    length = q.shape[0]
    rows = _HG_GROUP * HG_CHUNK
    ng = length // rows

    def body(q_ref, k_ref, v_ref, g_ref, st_ref, do_ref, dq_ref, dk_ref, dv_ref, dg_ref, dstate):
        @pl.when(pl.program_id(0) == 0)
        def _():
            dstate[...] = jnp.zeros_like(dstate)

        dstates = [dstate[h] for h in range(HG_HEADS)]
        for ci in reversed(range(_HG_GROUP)):
            sl = slice(ci * HG_CHUNK, (ci + 1) * HG_CHUNK)
            _, vjp = jax.vjp(_hg_chunk, q_ref[sl, :], k_ref[sl, :], v_ref[sl, :], g_ref[sl, :],
                             *[st_ref[h, ci] for h in range(HG_HEADS)])
            dq, dk, dv, dg, *dstates = vjp((do_ref[sl, :], *dstates))
            dq_ref[sl, :] = dq
            dk_ref[sl, :] = dk
            dv_ref[sl, :] = dv
            dg_ref[sl, :] = dg
        for h in range(HG_HEADS):
            dstate[h] = dstates[h]

    blk = pl.BlockSpec((rows, HG_WIDTH), lambda c: (ng - 1 - c, 0))
    sds = jax.ShapeDtypeStruct((length, HG_WIDTH), f32)
    return pl.pallas_call(
        body, grid=(ng,),
        in_specs=[blk] * 4 + [pl.BlockSpec((HG_HEADS, _HG_GROUP, HG_DIM, HG_DIM), lambda c: (0, ng - 1 - c, 0, 0)), blk],
        out_specs=[blk] * 4, out_shape=[sds] * 4,
        scratch_shapes=[pltpu.VMEM((HG_HEADS, HG_DIM, HG_DIM), f32)],
        compiler_params=_cparams(("arbitrary",)), name=name)(q, k, v, g, states, do)


_ATT_BLK = 512
_ATT_SCALE = MLA_QK ** -0.5
_NEG = -1e30


def _att_mask(i, j, t):
    rows = i * t + lax.broadcasted_iota(jnp.int32, (t, t), 0)
    cols = j * t + lax.broadcasted_iota(jnp.int32, (t, t), 1)
    return cols <= rows


def _att_fwd(q, k, v, *, name):
    length = q.shape[0]
    t = min(_ATT_BLK, length)
    nq = length // t
    qw, vw = MLA_QK_PAD, MLA_V
    heads = range(MLA_HEADS)

    def body(q_ref, k_ref, v_ref, o_ref, lse_ref):
        i = pl.program_id(0)
        qbs = [q_ref[:, h * qw:(h + 1) * qw] for h in heads]

        def step(j, carry, diagonal=False):
            off = pl.multiple_of(j * t, t)
            out = []
            for h in heads:
                m, l, acc = carry[h]
                ks = k_ref[pl.ds(off, t), h * qw:(h + 1) * qw]
                vs = v_ref[pl.ds(off, t), h * vw:(h + 1) * vw]
                s = lax.dot_general(qbs[h], ks, _NT, preferred_element_type=f32) * _ATT_SCALE
                if diagonal:
                    s = jnp.where(_att_mask(i, j, t), s, _NEG)
                m_new = jnp.maximum(m, jnp.max(s, axis=-1, keepdims=True))
                alpha = jnp.exp(m - m_new)
                p = jnp.exp(s - m_new)
                l = alpha * l + jnp.sum(p, axis=-1, keepdims=True)
                acc = alpha * acc + jnp.dot(p.astype(bf16), vs, preferred_element_type=f32)
                out.append((m_new, l, acc))
            return tuple(out)

        init = tuple((jnp.full((t, 1), _NEG, f32), jnp.zeros((t, 1), f32), jnp.zeros((t, vw), f32)) for _ in heads)
        res = step(i, lax.fori_loop(0, i, step, init), diagonal=True)
        for h in heads:
            m, l, acc = res[h]
            o_ref[:, h * vw:(h + 1) * vw] = (acc / l).astype(o_ref.dtype)
            lse_ref[:, h * vw:(h + 1) * vw] = jnp.broadcast_to(m + jnp.log(l), (t, vw))

    return pl.pallas_call(
        body, grid=(nq,),
        in_specs=[pl.BlockSpec((t, q.shape[1]), lambda i: (i, 0)), pl.BlockSpec(k.shape, lambda i: (0, 0)),
                  pl.BlockSpec(v.shape, lambda i: (0, 0))],
        out_specs=[pl.BlockSpec((t, v.shape[1]), lambda i: (i, 0))] * 2,
        out_shape=[jax.ShapeDtypeStruct(v.shape, bf16), jax.ShapeDtypeStruct(v.shape, f32)],
        compiler_params=_cparams(("arbitrary",)), name=name)(q, k, v)


def _att_bwd(q, k, v, o, lse, do, *, name):
    length = q.shape[0]
    t = min(_ATT_BLK, length)
    nq = length // t
    qw, vw = MLA_QK_PAD, MLA_V
    heads = range(MLA_HEADS)

    def dq_body(q_ref, k_ref, v_ref, o_ref, lse_ref, do_ref, dq_ref, delta_ref):
        i = pl.program_id(0)
        qbs = [q_ref[:, h * qw:(h + 1) * qw] for h in heads]
        dobs = [do_ref[:, h * vw:(h + 1) * vw] for h in heads]
        lses = [lse_ref[:, h * vw:h * vw + 1] for h in heads]
        deltas = [jnp.sum(dobs[h].astype(f32) * o_ref[:, h * vw:(h + 1) * vw].astype(f32), axis=-1, keepdims=True)
                  for h in heads]

        def step(j, dqs, diagonal=False):
            off = pl.multiple_of(j * t, t)
            out = []
            for h in heads:
                ks = k_ref[pl.ds(off, t), h * qw:(h + 1) * qw]
                vs = v_ref[pl.ds(off, t), h * vw:(h + 1) * vw]
                s = lax.dot_general(qbs[h], ks, _NT, preferred_element_type=f32) * _ATT_SCALE
                p = jnp.exp(s - lses[h])
                if diagonal:
                    p = jnp.where(_att_mask(i, j, t), p, 0.0)
                dp = lax.dot_general(dobs[h], vs, _NT, preferred_element_type=f32)
                ds = p * (dp - deltas[h]) * _ATT_SCALE
                out.append(dqs[h] + jnp.dot(ds.astype(bf16), ks, preferred_element_type=f32))
            return tuple(out)

        dqs = step(i, lax.fori_loop(0, i, step, tuple(jnp.zeros((t, qw), f32) for _ in heads)), diagonal=True)
        for h in heads:
            dq_ref[:, h * qw:(h + 1) * qw] = dqs[h].astype(dq_ref.dtype)
            delta_ref[:, h * vw:(h + 1) * vw] = jnp.broadcast_to(deltas[h], (t, vw))

    qblk = pl.BlockSpec((t, q.shape[1]), lambda i: (i, 0))
    vblk = pl.BlockSpec((t, v.shape[1]), lambda i: (i, 0))
    qfull = pl.BlockSpec(q.shape, lambda i: (0, 0))
    vfull = pl.BlockSpec(v.shape, lambda i: (0, 0))
    dq, delta = pl.pallas_call(
        dq_body, grid=(nq,), in_specs=[qblk, qfull, vfull, vblk, vblk, vblk], out_specs=[qblk, vblk],
        out_shape=[jax.ShapeDtypeStruct(q.shape, bf16), jax.ShapeDtypeStruct(lse.shape, f32)],
        compiler_params=_cparams(("arbitrary",)), name=name + "_dq")(q, k, v, o, lse, do)

    def dkv_body(k_ref, v_ref, q_ref, do_ref, lse_ref, delta_ref, dk_ref, dv_ref):
        j = pl.program_id(0)
        kbs = [k_ref[:, h * qw:(h + 1) * qw] for h in heads]
        vbs = [v_ref[:, h * vw:(h + 1) * vw] for h in heads]

        def step(i, carry, diagonal=False):
            off = pl.multiple_of(i * t, t)
            out = []
            for h in heads:
                dk, dv = carry[h]
                qs = q_ref[pl.ds(off, t), h * qw:(h + 1) * qw]
                dos = do_ref[pl.ds(off, t), h * vw:(h + 1) * vw]
                lse_i = lse_ref[pl.ds(off, t), h * vw:h * vw + 1]
                delta_i = delta_ref[pl.ds(off, t), h * vw:h * vw + 1]
                s = lax.dot_general(qs, kbs[h], _NT, preferred_element_type=f32) * _ATT_SCALE
                p = jnp.exp(s - lse_i)
                if diagonal:
                    p = jnp.where(_att_mask(i, j, t), p, 0.0)
                dv = dv + lax.dot_general(p.astype(bf16), dos, _TN, preferred_element_type=f32)
                dp = lax.dot_general(dos, vbs[h], _NT, preferred_element_type=f32)
                ds = p * (dp - delta_i) * _ATT_SCALE
                dk = dk + lax.dot_general(ds.astype(bf16), qs, _TN, preferred_element_type=f32)
                out.append((dk, dv))
            return tuple(out)

        first = step(j, tuple((jnp.zeros((t, qw), f32), jnp.zeros((t, vw), f32)) for _ in heads), diagonal=True)
        res = lax.fori_loop(j + 1, nq, step, first)
        for h in heads:
            dk_ref[:, h * qw:(h + 1) * qw] = res[h][0].astype(dk_ref.dtype)
            dv_ref[:, h * vw:(h + 1) * vw] = res[h][1].astype(dv_ref.dtype)

    dk, dv = pl.pallas_call(
        dkv_body, grid=(nq,), in_specs=[qblk, vblk, qfull, vfull, vfull, vfull], out_specs=[qblk, vblk],
        out_shape=[jax.ShapeDtypeStruct(k.shape, bf16), jax.ShapeDtypeStruct(v.shape, bf16)],
        compiler_params=_cparams(("arbitrary",)), name=name + "_dkv")(k, v, q, do, lse, delta)
    return dq, dk, dv


_C_Q = 4 * HG_WIDTH
_C_KV = _C_Q + MLA_Q_RANK
_C_KPE = _C_KV + MLA_KV_RANK


def _rms_n(x, g, n):
    return x * lax.rsqrt(jnp.sum(x * x, axis=-1, keepdims=True) * (1.0 / n) + EPS) * g


def _mix_a(proj, l0, l1, q_a_norm, kv_a_norm):
    lb = jax.nn.sigmoid(l0 - l1)
    f = lb + (1.0 - lb) * jax.nn.sigmoid(proj[:, HG_WIDTH:2 * HG_WIDTH])
    qf = _silu(proj[:, :HG_WIDTH])
    v = proj[:, 2 * HG_WIDTH:3 * HG_WIDTH]
    cqn = _rms(proj[:, _C_Q:_C_KV], q_a_norm)
    ckvn = _rms(proj[:, _C_KV:_C_KPE], kv_a_norm)
    return qf, 1.0 - f, v, jnp.log(f), cqn, ckvn


def _mix_b(qraw, kvraw, kpe_raw, cos, sin, qn_nope, qn_rope, kn_nope, kn_rope, perm):
    def rope(x):
        return x * cos + jnp.dot(x, perm, precision=lax.Precision.HIGHEST, preferred_element_type=f32) * sin

    kpe = rope(_rms_n(kpe_raw, kn_rope, MLA_ROPE))
    qs, ks, vs = [], [], []
    for hh in range(MLA_HEADS):
        base = hh * MLA_QK_PAD
        qs.append(_rms(qraw[:, base:base + MLA_NOPE], qn_nope))
        qs.append(rope(_rms_n(qraw[:, base + MLA_NOPE:base + MLA_QK_PAD], qn_rope, MLA_ROPE)))
        ks.append(_rms(kvraw[:, base:base + MLA_NOPE], kn_nope))
        ks.append(kpe)
        vs.append(kvraw[:, base + MLA_NOPE:base + MLA_QK_PAD])
    return jnp.concatenate(qs, axis=-1), jnp.concatenate(ks, axis=-1), jnp.concatenate(vs, axis=-1)


def _mix_c(o_hg, gate, o_mla, hg_out_norm):
    parts = []
    for hh in range(HG_HEADS):
        sl = slice(hh * HG_DIM, (hh + 1) * HG_DIM)
        parts.append(_rms(o_hg[:, sl], hg_out_norm[:, sl]))
    o = jnp.concatenate(parts, axis=-1) * _silu(gate)
    return jnp.concatenate([o, o_mla], axis=-1)


def _rope_perm():
    p = np.zeros((128, 128), np.float32)
    half = MLA_ROPE // 2
    for i in range(half):
        p[i + half, i] = -1.0
        p[i, i + half] = 1.0
    return jnp.asarray(p)


def _mixer_fwd(h, cos, sin, w, tag, next_gain):
    d = h.shape[1]
    hn, = _rows(_rms, [h], [w["norm_mix"]], [(d, bf16)], name=f"mix_norm_{tag}")
    proj = _mm(hn, w["mix_w_in"], name=f"mix_in_{tag}")
    pa = [w["lb0"], w["lb1"], w["mla_q_a_norm"], w["mla_kv_a_norm"]]
    qf, kk, vv, logf, cqn, ckvn = _rows(
        _mix_a, [proj], pa, [(HG_WIDTH, f32)] * 4 + [(MLA_Q_RANK, bf16), (MLA_KV_RANK, bf16)], name=f"mix_a_{tag}")
    o_hg, states = _hg_fwd(qf, kk, vv, logf, name=f"hg_fwd_{tag}")
    qraw = _mm(cqn, w["mla_w_uq"], out_dtype=bf16, name=f"mla_uq_{tag}")
    kvraw = _mm(ckvn, w["mla_w_ukv"], out_dtype=bf16, name=f"mla_ukv_{tag}")
    pb = [w["mla_qn_nope"], w["mla_qn_rope"], w["mla_kn_nope"], w["mla_kn_rope"], w["rope_perm"]]
    kpe_raw, gate = _window(proj, _C_KPE, IN_PAD - _C_KPE), _window(proj, 3 * HG_WIDTH, HG_WIDTH)
    qfull, kfull, vfull = _rows(_mix_b, [qraw, kvraw, kpe_raw, cos, sin], pb,
                                [(MLA_HEADS * MLA_QK_PAD, bf16)] * 2 + [(MLA_HEADS * MLA_V, bf16)],
                                name=f"mix_b_{tag}")
    o_mla, lse = _att_fwd(qfull, kfull, vfull, name=f"att_fwd_{tag}")
    mixin, = _rows(_mix_c, [o_hg, gate, o_mla], [w["hg_out_norm"]], [(d, bf16)], name=f"mix_c_{tag}")
    if callable(w["mix_w_out"]):
        w["mix_w_out"] = w["mix_w_out"](mixin)
    out, normed = _mm(mixin, w["mix_w_out"], add=h, norm=next_gain, name=f"mix_out_{tag}")
    return out, (h, hn, proj, qf, kk, vv, logf, cqn, ckvn, o_hg, states, qraw, kvraw, qfull, kfull, vfull, o_mla,
                 lse, mixin), normed


def _mixer_bwd(dout, cos, sin, w, saved, tag, on_w_out=None):
    (h, hn, proj, qf, kk, vv, logf, cqn, ckvn, o_hg, states, qraw, kvraw, qfull, kfull, vfull, o_mla, lse,
     mixin) = saved
    g = {}
    dmixin = _mm(dout, w["mix_w_out"], tb=True, name=f"mix_dmixin_{tag}")
    g["mix_w_out"] = _mm(mixin, dout, ta=True, out_dtype=bf16, name=f"mix_dwout_{tag}")
    if on_w_out is not None:
        dmixin = on_w_out(g["mix_w_out"], dmixin)
    kpe_raw, gate = _window(proj, _C_KPE, IN_PAD - _C_KPE), _window(proj, 3 * HG_WIDTH, HG_WIDTH)
    (do_hg, dgate, do_mla), (g["hg_out_norm"],) = _rows_bwd(
        _mix_c, [o_hg, gate, o_mla], [w["hg_out_norm"]], [dmixin], rgrad=[f32, f32, bf16], pgrad=[True],
        name=f"mix_dc_{tag}")
    dqfull, dkfull, dvfull = _att_bwd(qfull, kfull, vfull, o_mla, lse, do_mla, name=f"att_bwd_{tag}")
    pb = [w["mla_qn_nope"], w["mla_qn_rope"], w["mla_kn_nope"], w["mla_kn_rope"], w["rope_perm"]]
    (dqraw, dkvraw, dkpe_raw), pg = _rows_bwd(
        _mix_b, [qraw, kvraw, kpe_raw, cos, sin], pb, [dqfull, dkfull, dvfull],
        rgrad=[bf16, bf16, f32, None, None], pgrad=[True, True, True, True, False],
        name=f"mix_db_{tag}")
    g["mla_qn_nope"], g["mla_qn_rope"], g["mla_kn_nope"], g["mla_kn_rope"] = pg
    dcqn = _mm(dqraw, w["mla_w_uq"], tb=True, name=f"mla_dcq_{tag}")
    g["mla_w_uq"] = _mm(cqn, dqraw, ta=True, name=f"mla_dwuq_{tag}")
    dckvn = _mm(dkvraw, w["mla_w_ukv"], tb=True, name=f"mla_dckv_{tag}")
    g["mla_w_ukv"] = _mm(ckvn, dkvraw, ta=True, name=f"mla_dwukv_{tag}")
    dqf, dkk, dvv, dlogf = _hg_bwd(qf, kk, vv, logf, states, do_hg, name=f"hg_bwd_{tag}")
    pa = [w["lb0"], w["lb1"], w["mla_q_a_norm"], w["mla_kv_a_norm"]]
    (dproj,), (g["lb0"], g["lb1"], g["mla_q_a_norm"], g["mla_kv_a_norm"]) = _rows_bwd(
        _mix_a, [proj], pa, [dqf, dkk, dvv, dlogf, dcqn, dckvn], rgrad=[bf16], pgrad=[True] * 4,
        addends={0: [(dgate, 3 * HG_WIDTH), (dkpe_raw, _C_KPE)]}, name=f"mix_da_{tag}")
    dh, g["norm_mix"] = _mm(dproj, w["mix_w_in"], tb=True, rms_bwd=(h, w["norm_mix"], dout), name=f"mix_dhn_{tag}")
    g["mix_w_in_t"] = _mm(dproj, hn, ta=True, name=f"mix_dwin_{tag}")
    return dh, g


def _rope_tables(positions):
    inv_freq = 1.0 / (ROPE_BASE ** (jnp.arange(0, MLA_ROPE, 2, dtype=f32) / MLA_ROPE))
    ang = positions.astype(f32)[:, None] * inv_freq
    z = jnp.zeros((positions.shape[0], 128 - MLA_ROPE), f32)
    return (jnp.concatenate([jnp.cos(ang), jnp.cos(ang), z], axis=1),
            jnp.concatenate([jnp.sin(ang), jnp.sin(ang), z], axis=1))


def _pad_cols(a, n):
    return jnp.pad(a, ((0, 0), (0, n - a.shape[1])))


def _even_weights(p, j, layer, dt):
    w_uq = p["mla_w_uq"][j].reshape(MLA_Q_RANK, MLA_HEADS, MLA_QK)
    w_uq = jnp.pad(w_uq, ((0, 0), (0, 0), (0, MLA_QK_PAD - MLA_QK))).reshape(MLA_Q_RANK, MLA_HEADS * MLA_QK_PAD)
    return dict(
        norm_mix=p["norm_mix"][layer][None], mix_w_in=_pad_cols(p["mix_w_in"][j], IN_PAD).astype(dt),
        lb0=p["hg_lb_logits"][0][None], lb1=p["hg_lb_logits"][1][None],
        mla_q_a_norm=p["mla_q_a_norm"][j][None], mla_kv_a_norm=p["mla_kv_a_norm"][j][None],
        mla_w_uq=w_uq.astype(dt), mla_w_ukv=p["mla_w_ukv"][j].astype(dt),
        mla_qn_nope=p["mla_qn_nope"][j][None], mla_qn_rope=_pad_cols(p["mla_qn_rope"][j][None], 128),
        mla_kn_nope=p["mla_kn_nope"][j][None], mla_kn_rope=_pad_cols(p["mla_kn_rope"][j][None], 128),
        rope_perm=_rope_perm(), hg_out_norm=p["hg_out_norm"][j][None],
        mix_w_out=p["mix_w_out"][j].astype(dt) if "mix_w_out" in p else None)


def _even_grads(g):
    w_uq = g["mla_w_uq"].reshape(MLA_Q_RANK, MLA_HEADS, MLA_QK_PAD)[:, :, :MLA_QK].reshape(MLA_Q_RANK, -1)
    return dict(
        norm_mix=g["norm_mix"], mix_w_in=g["mix_w_in_t"][:IN_WIDTH].T[None],
        hg_lb_logits=jnp.concatenate([g["lb0"], g["lb1"]], axis=0),
        mla_q_a_norm=g["mla_q_a_norm"], mla_kv_a_norm=g["mla_kv_a_norm"], mla_w_uq=w_uq[None],
        mla_w_ukv=g["mla_w_ukv"][None], mla_qn_nope=g["mla_qn_nope"], mla_qn_rope=g["mla_qn_rope"][:, :MLA_ROPE],
        mla_kn_nope=g["mla_kn_nope"], mla_kn_rope=g["mla_kn_rope"][:, :MLA_ROPE],
        hg_out_norm=g["hg_out_norm"], mix_w_out=g["mix_w_out"][None])


_S5_NB = 8
_S5_BW = 1024
_S5_HALF = 512
_S5_UC = 128
_S5_TIME = 1024


def _cmul(ar, ai, br, bi):
    return ar * br - ai * bi, ar * bi + ai * br


def _pow_table(ar, ai, descending):
    rows = lax.broadcasted_iota(jnp.int32, (8, ar.shape[1]), 0)
    tr = jnp.zeros((8, ar.shape[1]), f32)
    ti = jnp.zeros((8, ar.shape[1]), f32)
    pr, pi_ = ar, ai
    for r in range(8):
        sel = rows == ((7 - r) if descending else r)
        tr = jnp.where(sel, pr, tr)
        ti = jnp.where(sel, pi_, ti)
        pr, pi_ = _cmul(pr, pi_, ar, ai)
    return tr, ti


def _s5_tile_scan(work, carry, ar, ai, tc, reverse, per_tile=None):
    hw = _S5_HALF
    row8 = lax.broadcasted_iota(jnp.int32, (8, hw), 0)
    powers = [(ar, ai)]
    for _ in range(2):
        powers.append(_cmul(*powers[-1], *powers[-1]))
    steps = []
    for (mr, mi), s in zip(powers, (1, 2, 4)):
        ok = (row8 < 8 - s) if reverse else (row8 >= s)
        steps.append((jnp.where(ok, mr, 0.0), jnp.where(ok, mi, 0.0), 8 - s if reverse else s))
    tr, ti = _pow_table(ar, ai, reverse)
    cr, ci = carry[:, :hw], carry[:, hw:]
    tiles = range(tc // 8)
    for i in (reversed(tiles) if reverse else tiles):
        sl = slice(8 * i, 8 * i + 8)
        xr, xi = work[sl, :hw], work[sl, hw:]
        for mr, mi, shift in steps:
            pr, pi_ = _cmul(mr, mi, pltpu.roll(xr, shift, axis=0), pltpu.roll(xi, shift, axis=0))
            xr, xi = xr + pr, xi + pi_
        pr, pi_ = _cmul(tr, ti, cr, ci)
        xr, xi = xr + pr, xi + pi_
        work[sl, :hw] = xr
        work[sl, hw:] = xi
        if per_tile is not None:
            per_tile(sl, xr, xi, cr, ci)
        edge = 8 * i if reverse else 8 * i + 7
        cr, ci = work[edge:edge + 1, :hw], work[edge:edge + 1, hw:]
    carry[:, :hw] = cr
    carry[:, hw:] = ci


def _s5_core_fwd(a, hn, b3, c3, *, name):
    length = hn.shape[0]
    tc = min(_S5_TIME, length)

    def body(a_ref, hn_ref, b_ref, c_ref, hs_ref, y_ref, work, carry):
        @pl.when(pl.program_id(1) == 0)
        def _():
            carry[...] = jnp.zeros_like(carry)

        work[...] = jnp.dot(hn_ref[...].astype(bf16), b_ref[...], preferred_element_type=f32)
        _s5_tile_scan(work, carry, a_ref[:, :_S5_HALF], a_ref[:, _S5_HALF:], tc, False)
        hs = work[...].astype(bf16)
        hs_ref[...] = hs
        y_ref[...] = jnp.dot(hs, c_ref[...], preferred_element_type=f32)

    return pl.pallas_call(
        body, grid=(_S5_NB, length // tc),
        in_specs=[pl.BlockSpec((1, _S5_BW), lambda j, t: (0, j)), pl.BlockSpec((tc, _S5_UC), lambda j, t: (t, j)),
                  pl.BlockSpec((_S5_UC, _S5_BW), lambda j, t: (j, 0)), pl.BlockSpec((_S5_BW, _S5_UC), lambda j, t: (j, 0))],
        out_specs=[pl.BlockSpec((tc, _S5_BW), lambda j, t: (t, j)), pl.BlockSpec((tc, _S5_UC), lambda j, t: (t, j))],
        out_shape=[jax.ShapeDtypeStruct((length, _S5_NB * _S5_BW), bf16),
                   jax.ShapeDtypeStruct((length, _S5_NB * _S5_UC), f32)],
        scratch_shapes=[pltpu.VMEM((tc, _S5_BW), f32), pltpu.VMEM((1, _S5_BW), f32)],
        compiler_params=_cparams(("parallel", "arbitrary")), name=name)(a, hn, b3, c3)


def _s5_core_bwd(a, dy, c3, hs, hn, b3, *, name):
    length = hn.shape[0]
    tc = min(_S5_TIME, length)
    nt = length // tc
    hw = _S5_HALF

    def body(a_ref, dy_ref, c_ref, hs_ref, hn_ref, b_ref, du_ref, db_ref, dc_ref, da_ref, work, carry, acc):
        @pl.when(pl.program_id(1) == 0)
        def _():
            carry[...] = jnp.zeros_like(carry)
            db_ref[...] = jnp.zeros_like(db_ref)
            dc_ref[...] = jnp.zeros_like(dc_ref)
            da_ref[...] = jnp.zeros_like(da_ref)

        dyb = dy_ref[...].astype(bf16)
        work[...] = lax.dot_general(dyb, c_ref[...], _NT, preferred_element_type=f32)
        acc[...] = jnp.zeros_like(acc)
        row8 = lax.broadcasted_iota(jnp.int32, (8, hw), 0)

        def grad_a(sl, gr, gi, cr, ci):
            gnr = jnp.where(row8 == 7, cr, pltpu.roll(gr, 7, axis=0))
            gni = jnp.where(row8 == 7, ci, pltpu.roll(gi, 7, axis=0))
            hr, hi = hs_ref[sl, :hw].astype(f32), hs_ref[sl, hw:].astype(f32)
            acc[:, :hw] += hr * gnr + hi * gni
            acc[:, hw:] += hr * gni - hi * gnr

        _s5_tile_scan(work, carry, a_ref[:, :hw], -a_ref[:, hw:], tc, True, grad_a)
        da_ref[...] += jnp.sum(acc[...], axis=0, keepdims=True)
        g = work[...].astype(bf16)
        du_ref[...] = lax.dot_general(g, b_ref[...], _NT, preferred_element_type=f32)
        db_ref[...] += lax.dot_general(hn_ref[...].astype(bf16), g, _TN, preferred_element_type=f32)
        dc_ref[...] += lax.dot_general(hs_ref[...], dyb, _TN, preferred_element_type=f32)

    rev = lambda j, t: (nt - 1 - t, j)
    return pl.pallas_call(
        body, grid=(_S5_NB, nt),
        in_specs=[pl.BlockSpec((1, _S5_BW), lambda j, t: (0, j)), pl.BlockSpec((tc, _S5_UC), rev),
                  pl.BlockSpec((_S5_BW, _S5_UC), lambda j, t: (j, 0)), pl.BlockSpec((tc, _S5_BW), rev),
                  pl.BlockSpec((tc, _S5_UC), rev), pl.BlockSpec((_S5_UC, _S5_BW), lambda j, t: (j, 0))],
        out_specs=[pl.BlockSpec((tc, _S5_UC), rev), pl.BlockSpec((_S5_UC, _S5_BW), lambda j, t: (j, 0)),
                   pl.BlockSpec((_S5_BW, _S5_UC), lambda j, t: (j, 0)), pl.BlockSpec((1, _S5_BW), lambda j, t: (0, j))],
        out_shape=[jax.ShapeDtypeStruct((length, _S5_NB * _S5_UC), f32),
                   jax.ShapeDtypeStruct((_S5_NB * _S5_UC, _S5_BW), f32),
                   jax.ShapeDtypeStruct((_S5_NB * _S5_BW, _S5_UC), f32),
                   jax.ShapeDtypeStruct((1, _S5_NB * _S5_BW), f32)],
        scratch_shapes=[pltpu.VMEM((tc, _S5_BW), f32), pltpu.VMEM((1, _S5_BW), f32), pltpu.VMEM((8, _S5_BW), f32)],
        compiler_params=_cparams(("parallel", "arbitrary")), name=name)(a, dy, c3, hs, hn, b3)


def _s5_disc(lr, li, ldt, btr, bti, expand):
    dt = jnp.exp(ldt)
    mag = jnp.exp(lr * dt)
    abr = mag * jnp.cos(li * dt)
    abi = mag * jnp.sin(li * dt)
    den = lr * lr + li * li
    zr = ((abr - 1.0) * lr + abi * li) / den
    zi = (abi * lr - (abr - 1.0) * li) / den
    zr = jnp.dot(zr, expand, precision=lax.Precision.HIGHEST, preferred_element_type=f32)
    zi = jnp.dot(zi, expand, precision=lax.Precision.HIGHEST, preferred_element_type=f32)
    return abr, abi, zr * btr - zi * bti, zr * bti + zi * btr


def _s5_disc_fwd(args, *, name):
    def body(*refs):
        res = _s5_disc(*[r[...] for r in refs[:6]])
        for o, v in zip(refs[6:], res):
            o[...] = v

    sds = jax.ShapeDtypeStruct
    return pl.pallas_call(body, out_shape=[sds(args[0].shape, f32)] * 2 + [sds(args[3].shape, f32)] * 2,
                          name=name)(*args)


def _s5_disc_bwd(args, cts, *, name):
    def body(*refs):
        vals = [r[...] for r in refs[:6]]
        _, vjp = jax.vjp(lambda *d: _s5_disc(*d, vals[5]), *vals[:5])
        grads = vjp(tuple(r[...] for r in refs[6:10]))
        for o, v in zip(refs[10:], grads):
            o[...] = v

    return pl.pallas_call(body, out_shape=[jax.ShapeDtypeStruct(a.shape, f32) for a in args[:5]],
                          name=name)(*args, *cts)


def _gelu_tanh(x):
    return 0.5 * x * (1.0 + jnp.tanh(0.7978845608028654 * (x + 0.044715 * (x * x * x))))


def _s5_post(y, u, d_skip):
    return _gelu_tanh(y + d_skip * u)


def _s5_glu(ga, gb, h):
    return h + ga * jax.nn.sigmoid(gb)


def _s5_glu_norm(ga, gb, h, next_gain):
    out = _s5_glu(ga, gb, h)
    return out, _rms(out, next_gain)


def _s5_expand():
    e = np.zeros((S5_STATE, S5_GROUP * S5_STATE), np.float32)
    for m in range(S5_GROUP):
        e[np.arange(S5_STATE), m * S5_STATE + np.arange(S5_STATE)] = 1.0
    return jnp.asarray(e)


def _s5_pack_b(bbr, bbi):
    eye = jnp.eye(8, dtype=f32)

    def one(bb):
        b5 = bb.reshape(_S5_NB, 8, S5_GROUP, S5_STATE)
        return jnp.einsum("jgmp,gh->jgmhp", b5, eye).reshape(_S5_NB * _S5_UC, _S5_HALF)

    return jnp.concatenate([one(bbr), one(bbi)], axis=1)


def _s5_unpack_b(db3):
    def one(d):
        d5 = d.reshape(_S5_NB, 8, S5_GROUP, 8, S5_STATE)
        return jnp.einsum("jgmgp->jgmp", d5).reshape(S5_GROUPS, S5_GROUP * S5_STATE)

    return one(db3[:, :_S5_HALF]), one(db3[:, _S5_HALF:])


def _s5_pack_c(c_re, c_im):
    eye = jnp.eye(8, dtype=f32)

    def one(c):
        c4 = c.reshape(_S5_NB, 8, S5_GROUP, S5_STATE)
        return jnp.einsum("jgmp,hg->jhpgm", c4, eye).reshape(_S5_NB, _S5_HALF, _S5_UC)

    return jnp.concatenate([one(c_re), -one(c_im)], axis=1).reshape(_S5_NB * _S5_BW, _S5_UC)


def _s5_unpack_c(dc3):
    d = dc3.reshape(_S5_NB, 2, 8, S5_STATE, 8, S5_GROUP)
    dre = jnp.einsum("jgpgm->jgmp", d[:, 0]).reshape(S5_GROUPS, S5_GROUP, S5_STATE)
    dim = -jnp.einsum("jgpgm->jgmp", d[:, 1]).reshape(S5_GROUPS, S5_GROUP, S5_STATE)
    return dre, dim


def _s5_state_row(re, im):
    r = re.reshape(_S5_NB, 1, _S5_HALF)
    i = im.reshape(_S5_NB, 1, _S5_HALF)
    return jnp.concatenate([r, i], axis=2).reshape(1, _S5_NB * _S5_BW)


def _s5_unstate_row(row):
    r = row.reshape(_S5_NB, 2, 8, S5_STATE)
    return r[:, 0].reshape(S5_GROUPS, S5_STATE), r[:, 1].reshape(S5_GROUPS, S5_STATE)


def _s5_fwd(h, w, tag, next_gain):
    d = h.shape[1]
    hn, = _rows(_rms, [h], [w["norm_mix"]], [(d, f32)], name=f"s5_norm_{tag}")
    disc_in = [w["s5_lam_re"], w["s5_lam_im"], w["s5_log_dt"], w["s5_bt_re"], w["s5_bt_im"], w["s5_expand"]]
    abr, abi, bbr, bbi = _s5_disc_fwd(disc_in, name=f"s5_disc_{tag}")
    a_row = _s5_state_row(abr, abi)
    b3 = _s5_pack_b(bbr, bbi).astype(bf16)
    hs, y = _s5_core_fwd(a_row, hn, b3, w["s5_c3"], name=f"s5_core_{tag}")
    yg, = _rows(_s5_post, [y, hn], [w["s5_d"]], [(d, bf16)], name=f"s5_post_{tag}")
    ga = _mm(yg, w["s5_w_glu_a"], out_dtype=bf16, name=f"s5_glu_a_{tag}")
    gb = _mm(yg, w["s5_w_glu_b"], out_dtype=bf16, name=f"s5_glu_b_{tag}")
    out, normed = _rows(_s5_glu_norm, [ga, gb, h], [next_gain], [(d, f32), (d, bf16)], name=f"s5_glu_{tag}")
    return out, (h, hn, disc_in, a_row, b3, hs, y, yg, ga, gb), normed


def _s5_bwd(dout, w, saved, tag):
    h, hn, disc_in, a_row, b3, hs, y, yg, ga, gb = saved
    g = {}
    (dga, dgb), _ = _rows_bwd(_s5_glu, [ga, gb, h], [], [dout], rgrad=[bf16, bf16, None], pgrad=[],
                              name=f"s5_dglu_{tag}")
    dyg = _mm(dga, w["s5_w_glu_a"], tb=True, name=f"s5_dyg_a_{tag}")
    dyg = _mm(dgb, w["s5_w_glu_b"], tb=True, add=dyg, name=f"s5_dyg_b_{tag}")
    g["s5_w_glu_a"] = _mm(yg, dga, ta=True, out_dtype=bf16, name=f"s5_dwa_{tag}")
    g["s5_w_glu_b"] = _mm(yg, dgb, ta=True, out_dtype=bf16, name=f"s5_dwb_{tag}")
    (dy, du_skip), (g["s5_d"],) = _rows_bwd(_s5_post, [y, hn], [w["s5_d"]], [dyg], rgrad=[bf16, f32], pgrad=[True],
                                           name=f"s5_dpost_{tag}")
    du, db3, dc3, da_row = _s5_core_bwd(a_row, dy, w["s5_c3"], hs, hn, b3, name=f"s5_dcore_{tag}")
    dabr, dabi = _s5_unstate_row(da_row)
    dbbr, dbbi = _s5_unpack_b(db3)
    g["s5_lam_re"], g["s5_lam_im"], g["s5_log_dt"], g["s5_bt_re"], g["s5_bt_im"] = _s5_disc_bwd(
        disc_in, [dabr, dabi, dbbr, dbbi], name=f"s5_ddisc_{tag}")
    g["s5_c_re"], g["s5_c_im"] = _s5_unpack_c(dc3)
    (dh,), (g["norm_mix"],) = _rows_bwd(_rms_twice, [h], [w["norm_mix"]], [du, du_skip], rgrad=[f32], pgrad=[True],
                                        addends={0: dout}, name=f"s5_dnorm_{tag}")
    return dh, g


def _odd_weights(p, j, layer, dt):
    tr = lambda b: b.transpose(0, 2, 1).reshape(S5_GROUPS, S5_GROUP * S5_STATE)
    return dict(
        norm_mix=p["norm_mix"][layer][None], s5_lam_re=p["s5_lam_re"][j], s5_lam_im=p["s5_lam_im"][j],
        s5_log_dt=p["s5_log_dt"][j][:, None], s5_bt_re=tr(p["s5_b_re"][j]), s5_bt_im=tr(p["s5_b_im"][j]),
        s5_expand=_s5_expand(), s5_c3=_s5_pack_c(p["s5_c_re"][j], p["s5_c_im"][j]).astype(dt),
        **{n: (p[n][j][None] if n == "s5_d" else p[n][j].astype(dt))
           for n in ("s5_d", "s5_w_glu_a", "s5_w_glu_b") if n in p})


def _odd_grads(g):
    tr = lambda b: b.reshape(S5_GROUPS, S5_GROUP, S5_STATE).transpose(0, 2, 1)[None]
    return dict(
        norm_mix=g["norm_mix"], s5_lam_re=g["s5_lam_re"][None], s5_lam_im=g["s5_lam_im"][None],
        s5_log_dt=g["s5_log_dt"][:, 0][None], s5_b_re=tr(g["s5_bt_re"]), s5_b_im=tr(g["s5_bt_im"]),
        s5_c_re=g["s5_c_re"][None], s5_c_im=g["s5_c_im"][None], s5_d=g["s5_d"],
        s5_w_glu_a=g["s5_w_glu_a"][None], s5_w_glu_b=g["s5_w_glu_b"][None])


FF_SHARD = 352
FF_SHARD_PAD = 384


def _pad_groups(a, axis):
    axis %= a.ndim
    zeros = jnp.zeros(a.shape[:axis] + (FF_SHARD_PAD - FF_SHARD,) + a.shape[axis + 1:], a.dtype)
    pieces = []
    for g in range(a.shape[axis] // FF_SHARD):
        pieces += [lax.slice_in_dim(a, g * FF_SHARD, (g + 1) * FF_SHARD, axis=axis), zeros]
    return jnp.concatenate(pieces, axis=axis)


def _unpad_groups(a, axis):
    axis %= a.ndim
    pieces = [lax.slice_in_dim(a, g * FF_SHARD_PAD, g * FF_SHARD_PAD + FF_SHARD, axis=axis)
              for g in range(a.shape[axis] // FF_SHARD_PAD)]
    return pieces[0] if len(pieces) == 1 else jnp.concatenate(pieces, axis=axis)


_LANES = 1024
_ROW_PAD = 16


_PEER_MASKS = (1, 2, 4, 3, 5, 6, 7)


def _mesh_place():
    x, y, c = lax.axis_index("x"), lax.axis_index("y"), lax.axis_index("c")

    def peer(mask):
        px = 1 - x if mask & 4 else x
        py = 1 - y if mask & 2 else y
        pc = 1 - c if mask & 1 else c
        return (px, py, pc), 4 * px + 2 * py + pc

    return 4 * x + 2 * y + c, peer


class _Exchange:
    def __init__(self, name):
        self.name = name
        self.srcs, self.shapes, self.items, self.where = [], [], [], {}

    def add(self, src, land_shape, src_at, dst_at, key):
        si = next((i for i, s in enumerate(self.srcs) if s is src), None)
        if si is None:
            self.srcs.append(src)
            si = len(self.srcs) - 1
        if key not in self.where:
            self.shapes.append(land_shape)
            self.where[key] = len(self.shapes) - 1
        self.items.append(dict(src=si, dst=self.where[key], src_at=src_at, dst_at=dst_at))

    def _copy(self, k, mask, ins, lands, send_sems, recv_sems, me, peer, arriving):
        it = self.items[k]
        dev, idx = peer(mask)
        s = k * (N_DEV - 1) + _PEER_MASKS.index(mask)
        return pltpu.make_async_remote_copy(
            src_ref=it["src_at"](ins[it["src"]], idx), dst_ref=it["dst_at"](lands[it["dst"]], idx if arriving else me),
            send_sem=send_sems.at[s], recv_sem=recv_sems.at[s], device_id=dev, device_id_type=pl.DeviceIdType.MESH)

    def _own_copy(self, k, ins, lands, own_sems, me):
        it = self.items[k]
        return pltpu.make_async_copy(it["src_at"](ins[it["src"]], me), it["dst_at"](lands[it["dst"]], me), own_sems.at[k])

    def begin(self, own):
        ns, nd, ni = len(self.srcs), len(self.shapes), len(self.items)
        nsem = ni * (N_DEV - 1)
        self.own = own

        nq = 3 if own else 2

        def body(*refs):
            ins, land_refs = refs[:ns], refs[ns:ns + nd]
            sems, token = refs[ns + nd:ns + nd + nq], refs[-1]
            me, peer = _mesh_place()
            for mask in _PEER_MASKS:
                for k in range(ni):
                    self._copy(k, mask, ins, land_refs, sems[0], sems[1], me, peer, False).start()
            if own:
                for k in range(ni):
                    self._own_copy(k, ins, land_refs, sems[2], me).start()
            token[...] = jnp.zeros_like(token)

        hbm = pl.BlockSpec(memory_space=pltpu.HBM)
        sem = pl.BlockSpec(memory_space=pltpu.SEMAPHORE)
        lands = [lax.empty(s.shape, s.dtype) for s in self.shapes]
        sem_shapes = [pltpu.SemaphoreType.DMA((nsem,)), pltpu.SemaphoreType.DMA((nsem,)), pltpu.SemaphoreType.DMA((ni,))]
        res = pl.pallas_call(
            body, in_specs=[hbm] * (ns + nd),
            out_specs=[sem] * nq + [hbm] * nd + [pl.BlockSpec(memory_space=pltpu.VMEM)],
            out_shape=sem_shapes[:nq] + [pltpu.HBM(s.shape, s.dtype) for s in self.shapes]
            + [jax.ShapeDtypeStruct((8, 128), f32)],
            input_output_aliases={ns + j: nq + j for j in range(nd)},
            compiler_params=pltpu.CompilerParams(has_side_effects=pltpu.SideEffectType.DATAFLOW_SIDE_EFFECTING),
            name=self.name + "_start")(*self.srcs, *lands)
        self.token = res[-1]
        return list(res[:nq]), list(res[nq:-1])

    def finish(self, state, after):
        sems, lands = state
        nq = len(sems)
        after = list(after) if isinstance(after, (list, tuple)) else [after]
        ns, nd, ni = len(self.srcs), len(self.shapes), len(self.items)

        def body(*refs):
            ins, land_refs = refs[:ns], refs[ns:ns + nd]
            sem_refs = refs[ns + nd:ns + nd + nq]
            me, peer = _mesh_place()
            for mask in _PEER_MASKS:
                for k in range(ni):
                    cp = self._copy(k, mask, ins, land_refs, sem_refs[0], sem_refs[1], me, peer, True)
                    cp.wait_send()
                    cp.wait_recv()
            if self.own:
                for k in range(ni):
                    self._own_copy(k, ins, land_refs, sem_refs[2], me).wait()

        hbm = pl.BlockSpec(memory_space=pltpu.HBM)
        sem = pl.BlockSpec(memory_space=pltpu.SEMAPHORE)
        res = pl.pallas_call(
            body, in_specs=[hbm] * (ns + nd) + [sem] * nq + [pl.BlockSpec(memory_space=pl.ANY)] * len(after),
            out_specs=[hbm] * nd, out_shape=[pltpu.HBM(s.shape, s.dtype) for s in self.shapes],
            input_output_aliases={ns + j: j for j in range(nd)},
            compiler_params=pltpu.CompilerParams(has_side_effects=pltpu.SideEffectType.DATAFLOW_SIDE_EFFECTING),
            name=self.name + "_wait")(*self.srcs, *lands, *sems, *after)
        return {k: res[i] for k, i in self.where.items()}


def _after(x, *tokens, name):
    def body(*refs):
        del refs

    anyspace = pl.BlockSpec(memory_space=pl.ANY)
    return pl.pallas_call(body, in_specs=[anyspace] * (1 + len(tokens)), out_specs=anyspace,
                          out_shape=jax.ShapeDtypeStruct(x.shape, x.dtype), input_output_aliases={0: 0},
                          name=name)(x, *tokens)


def _rows_of(n):
    return lambda r, i: r.at[pl.ds(pl.multiple_of(i * n, n), n), :]


def _cols_of(n):
    return lambda r, i: r.at[:, pl.ds(pl.multiple_of(i * n, n), n)]


def _whole(r, i):
    return r


def _slot(r, i):
    return r.at[i]


def _at_layer(layer):
    return lambda r, i: r.at[layer]


def _sum_adam(me_index, slots, owns, own_block, w, m, v, *, name):
    layers, rows, cols = w.shape
    tr = _pick(rows, (256, 128, 104, 64, 32, 16, 8))
    bc1 = 1.0 - ADAM_B1 ** ADAM_STEP
    bc2 = 1.0 - ADAM_B2 ** ADAM_STEP
    own_shape, own_map = own_block(tr)
    nl = len(slots)
    assert nl == layers and len(owns) == layers

    def body(me_ref, *refs):
        s_refs, own_refs = refs[:nl], refs[nl:2 * nl]
        w_ref, m_ref, v_ref, g_ref, d_ref, nm_ref, nv_ref = refs[2 * nl:]
        me = me_ref[0]

        def run(s_ref, own_ref):
            mine = (own_ref[0] if len(own_shape) == 3 else own_ref[...]).astype(f32)
            g = jnp.where(me == 0, mine, s_ref[0].astype(f32))
            for k in range(1, N_DEV):
                g = g + jnp.where(me == k, mine, s_ref[k].astype(f32))
            mm = ADAM_B1 * m_ref[0] + (1.0 - ADAM_B1) * g
            vv = ADAM_B2 * v_ref[0] + (1.0 - ADAM_B2) * (g * g)
            g_ref[0] = g
            nm_ref[0] = mm
            nv_ref[0] = vv
            d_ref[0] = -ADAM_LR * ((mm / bc1) / (jnp.sqrt(vv / bc2) + ADAM_EPS) + ADAM_WD * w_ref[0])

        for layer in range(nl):
            pl.when(pl.program_id(0) == layer)(functools.partial(run, s_refs[layer], own_refs[layer]))

    def of_layer(layer, index_map):
        return lambda lyr, i, me: index_map(jnp.where(lyr == layer, i, 0), me)

    blk = pl.BlockSpec((1, tr, cols), lambda lyr, i, me: (lyr, i, 0))
    sds = jax.ShapeDtypeStruct((layers, rows, cols), f32)
    grid_spec = pltpu.PrefetchScalarGridSpec(
        num_scalar_prefetch=1, grid=(layers, rows // tr),
        in_specs=[pl.BlockSpec((N_DEV, tr, cols), of_layer(layer, lambda i, me: (0, i, 0))) for layer in range(nl)]
        + [pl.BlockSpec(own_shape, of_layer(layer, own_map)) for layer in range(nl)] + [blk, blk, blk],
        out_specs=[blk] * 4)
    return pl.pallas_call(body, grid_spec=grid_spec, out_shape=[sds] * 4,
                          compiler_params=_cparams(("arbitrary", "arbitrary")),
                          name=name)(me_index, *slots, *owns, w, m, v)


_SHARDED = dict(xa_wq=1, xa_wk=1, xa_wv=1, xa_wo=1, ffn_w_up=2, ffn_conv_w=2, ffn_w_down=1, mix_w_in=2, mla_w_uq=2,
                mla_w_ukv=2, mix_w_out=1, s5_d=1, s5_w_glu_a=1, s5_w_glu_b=1)
_EXACT = ("ffn_conv_w", "s5_d")
_WEIGHTS = ("norm_mix", "norm_xa", "norm_mem", "norm_ffn", "xa_wq", "xa_wk", "xa_wv", "xa_wo", "xa_q_norm",
            "xa_k_norm", "ffn_w_up", "ffn_conv_w", "ffn_conv_b", "ffn_w_down", "hg_lb_logits", "mix_w_in",
            "hg_out_norm", "mla_q_a_norm", "mla_w_uq", "mla_kv_a_norm", "mla_w_ukv", "mla_qn_nope", "mla_qn_rope",
            "mla_kn_nope", "mla_kn_rope", "mix_w_out", "s5_lam_re", "s5_lam_im", "s5_log_dt", "s5_b_re", "s5_b_im",
            "s5_c_re", "s5_c_im", "s5_d", "s5_w_glu_a", "s5_w_glu_b")
_BIG = tuple(n for n in _WEIGHTS if n in _SHARDED and n not in _EXACT)
_SHARD_ORDER = tuple(n for n in _WEIGHTS if n in _SHARDED)
_REPL_ORDER = tuple(n for n in _WEIGHTS if n not in _SHARDED)
_REPL_EARLY = tuple(n for n in _REPL_ORDER if n.startswith("s5_"))
_REPL_LATE = tuple(n for n in _REPL_ORDER if n not in _REPL_EARLY)


def _pack(parts, dtype, lead=None):
    nl = 0 if lead is None else 1
    flat = [a.astype(dtype).reshape(a.shape[:nl] + (-1,)) for a in parts]
    cat = jnp.concatenate(flat, axis=nl)
    n = cat.shape[nl]
    unit = _LANES * _ROW_PAD
    total = -(-n // unit) * unit
    cat = jnp.pad(cat, [(0, 0)] * nl + [(0, total - n)])
    return cat.reshape(cat.shape[:nl] + (total // _LANES, _LANES))


def _unpack(packed, shapes, lead=None):
    nl = 0 if lead is None else 1
    flat = packed.reshape(packed.shape[:nl] + (-1,))
    out, off = [], 0
    for s in shapes:
        n = int(np.prod(s))
        piece = flat[..., off:off + n] if nl else flat[off:off + n]
        out.append(piece.reshape(packed.shape[:nl] + tuple(s)))
        off += n
    return out


def _to_full(gathered, axis):
    g = jnp.moveaxis(gathered, 0, axis)
    s = g.shape
    return g.reshape(s[:axis] + (s[axis] * s[axis + 1],) + s[axis + 2:])


def _to_shards(full, axis):
    s = full.shape
    g = full.reshape(s[:axis] + (N_DEV, s[axis] // N_DEV) + s[axis + 1:])
    return jnp.moveaxis(g, axis, 0)


_DIRECT_ROWS = ("xa_wq", "xa_wk", "xa_wv", "xa_wo", "mix_w_out", "s5_w_glu_a", "s5_w_glu_b")
_SMALL16 = ("mix_w_in", "mla_w_uq", "mla_w_ukv")
_SMALL_SHARDED = ("mla_w_uq", "mla_w_ukv") + _EXACT
_SHARD_ROWS = 128


def _exchange_layout(d):
    out = dict(d)
    out["ffn_w_up"] = _pad_groups(d["ffn_w_up"], 2)
    out["ffn_conv_w"] = _pad_groups(d["ffn_conv_w"], 2)
    out["ffn_w_down"] = _pad_groups(d["ffn_w_down"], 1)
    return out


def _train_step(x, mem, positions, target, w, m, v):
    d_model = x.shape[1]
    we_, me_, ve_ = _exchange_layout(w), _exchange_layout(m), _exchange_layout(v)
    sds = jax.ShapeDtypeStruct

    matrices = _DIRECT_ROWS + ("ffn_w_up", "ffn_w_down")
    layer_mats = ("xa_wq", "xa_wk", "xa_wv", "xa_wo", "ffn_w_up", "ffn_w_down")
    shard16 = {n: we_[n].astype(bf16) for n in matrices}
    part_of = {n: _rows_of(_SHARD_ROWS) for n in _DIRECT_ROWS}
    part_of["ffn_w_up"] = _cols_of(we_["ffn_w_up"].shape[2])
    part_of["ffn_w_down"] = _rows_of(we_["ffn_w_down"].shape[1])
    part_shape = {n: we_[n].shape[1:] for n in matrices}

    def full_shape(n):
        r, c = part_shape[n]
        return (r, N_DEV * c) if n == "ffn_w_up" else (N_DEV * r, c)

    def gather(ex, n, layer):
        ex.add(shard16[n], sds(full_shape(n), bf16), _at_layer(layer), part_of[n], (n, layer))

    def scatter(ex, n, layer, grad):
        ex.add(grad, sds((N_DEV,) + part_shape[n], grad.dtype), part_of[n], _slot, (n, layer))

    small16 = _pack([we_[n] for n in _SMALL16], bf16)
    exact = _pack([we_[n] for n in _EXACT], f32)
    ga, ga1, gb, gc = _Exchange("gather_a"), _Exchange("gather_a1"), _Exchange("gather_b"), _Exchange("gather_c")
    ga.add(small16, sds((N_DEV,) + small16.shape, bf16), _whole, _slot, "small16")
    ga1.add(exact, sds((N_DEV,) + exact.shape, f32), _whole, _slot, "exact")
    gather(ga1, "mix_w_out", 0)
    for n in layer_mats:
        gather(gb, n, 0)
    gather(gc, "s5_w_glu_a", 0)
    gather(gc, "s5_w_glu_b", 0)
    for n in layer_mats:
        gather(gc, n, 1)
    state_a, state_a1, state_b, state_c = ga.begin(True), ga1.begin(True), gb.begin(True), gc.begin(True)

    p = {n: w[n] for n in _REPL_ORDER}
    cos, sin = _rope_tables(positions)
    wo = _odd_weights(p, 0, 1, bf16)
    conv_b = _pad_groups(w["ffn_conv_b"], 1)
    prepared = [cos, sin, conv_b, wo["s5_c3"], wo["s5_bt_re"], wo["s5_bt_im"]]
    full = ga.finish(state_a, [ga1.token, gb.token, gc.token] + prepared)
    for n, a in zip(_SMALL16, _unpack(full["small16"], [we_[n].shape for n in _SMALL16], lead=True)):
        p[n] = _to_full(a, _SHARDED[n])
    we = _even_weights(p, 0, 0, bf16)
    we["norm_mix"] = _after(we["norm_mix"], ga.token, ga1.token, gb.token, gc.token, name="after_gather_starts")

    def late_mix_w_out(mixin):
        full.update(ga1.finish(state_a1, [mixin]))
        return full[("mix_w_out", 0)]

    we["mix_w_out"] = late_mix_w_out
    h, s_mix0, hx = _mixer_fwd(x, cos, sin, we, "l0", w["norm_xa"][0][None])
    conv_w, s5_d = [_to_full(a, _SHARDED[n]) for n, a in
                    zip(_EXACT, _unpack(full["exact"], [we_[n].shape for n in _EXACT], lead=True))]

    def layer_weights(layer):
        return dict(norm_xa=w["norm_xa"][layer][None], norm_mem=w["norm_mem"][layer][None],
                    norm_ffn=w["norm_ffn"][layer][None], xa_q_norm=w["xa_q_norm"][layer][None],
                    xa_k_norm=w["xa_k_norm"][layer][None], ffn_conv_w=conv_w[layer],
                    ffn_conv_b=conv_b[layer][None], **{n: full[(n, layer)] for n in layer_mats})

    full.update(gb.finish(state_b, h))
    wl = [layer_weights(0)]
    h, s_xa0, hf = _xattn_fwd(h, mem, wl[0], "l0", hx)
    h, s_ff0 = _ffn_fwd(h, wl[0], "l0", hf)
    full.update(gc.finish(state_c, h))
    wl.append(layer_weights(1))
    wo.update(s5_d=s5_d, s5_w_glu_a=full[("s5_w_glu_a", 0)], s5_w_glu_b=full[("s5_w_glu_b", 0)])
    h, s_mix1, hx = _s5_fwd(h, wo, "l1", wl[1]["norm_xa"])
    h, s_xa1, hf = _xattn_fwd(h, mem, wl[1], "l1", hx)
    (dh, loss), s_ff1 = _ffn_fwd(h, wl[1], "l1", hf, loss_target=target)

    gl = [{}, {}]
    reduces = []

    own_grad = {}

    def reduce_start(name, entries, dh):
        ex = _Exchange(name)
        for n, layer, grad in entries.get("matrices", ()):
            scatter(ex, n, layer, grad)
            own_grad[(n, layer)] = grad
        for key, src, shape, src_at in entries.get("packs", ()):
            ex.add(src, shape, src_at, _slot, key)
        reduces.append((ex, ex.begin(False)))
        return _after(dh, ex.token, name="after_" + name)

    dh, gl[1] = _ffn_bwd(dh, wl[1], s_ff1, "l1")
    dh = reduce_start("reduce_ffn1", dict(matrices=[(n, 1, gl[1][n]) for n in ("ffn_w_up", "ffn_w_down")]), dh)
    dh, g = _xattn_bwd(dh, mem, wl[1], s_xa1, "l1")
    gl[1].update(g)
    dh = reduce_start("reduce_xa1", dict(matrices=[(n, 1, g[n]) for n in ("xa_wq", "xa_wk", "xa_wv", "xa_wo")]), dh)
    dh, g_odd = _s5_bwd(dh, wo, s_mix1, "l1")
    go = _odd_grads(g_odd)
    dh, gl[0] = _ffn_bwd(dh, wl[0], s_ff0, "l0")
    send_early = _pack([go[n].reshape(w[n].shape) for n in _REPL_EARLY], f32)
    dh = reduce_start("reduce_ffn0", dict(
        matrices=[(n, 0, g_odd[n]) for n in ("s5_w_glu_a", "s5_w_glu_b")]
        + [(n, 0, gl[0][n]) for n in ("ffn_w_up", "ffn_w_down")],
        packs=[("repl_early", send_early, sds((N_DEV,) + send_early.shape, f32), _whole)]), dh)
    dh, g = _xattn_bwd(dh, mem, wl[0], s_xa0, "l0")
    gl[0].update(g)
    dh = reduce_start("reduce_xa0", dict(matrices=[(n, 0, g[n]) for n in ("xa_wq", "xa_wk", "xa_wv", "xa_wo")]), dh)
    grad_x, g_even = _mixer_bwd(
        dh, cos, sin, we, s_mix0, "l0",
        on_w_out=lambda grad, dmixin: reduce_start("reduce_w_out", dict(matrices=[("mix_w_out", 0, grad)]), dmixin))

    ge = _even_grads(g_even)
    cat = lambda n: jnp.concatenate([gl[0][n], gl[1][n]], axis=0)
    rg = dict(ge)
    rg["norm_mix"] = jnp.concatenate([ge["norm_mix"], go["norm_mix"]], axis=0)
    for n in ("norm_xa", "norm_mem", "norm_ffn", "xa_q_norm", "xa_k_norm"):
        rg[n] = cat(n)
    rg["ffn_conv_b"] = _unpad_groups(cat("ffn_conv_b"), 1)
    sg = dict(mla_w_uq=ge["mla_w_uq"], mla_w_ukv=ge["mla_w_ukv"], s5_d=go["s5_d"],
              ffn_conv_w=jnp.stack([gl[0]["ffn_conv_w"], gl[1]["ffn_conv_w"]]))
    send_small = _pack([_to_shards(sg[n], _SHARDED[n]) for n in _SMALL_SHARDED], f32, lead=True)
    send_late = _pack([rg[n].reshape(w[n].shape) for n in _REPL_LATE], f32)
    w_in_rows = w["mix_w_in"].shape[2]
    last = _Exchange("reduce_last")
    last.add(g_even["mix_w_in_t"], sds((N_DEV, w_in_rows, d_model), f32), _rows_of(w_in_rows), _slot, "mix_w_in")
    last.add(send_small, sds(send_small.shape, f32), _slot, _slot, "small")
    last.add(send_late, sds((N_DEV,) + send_late.shape, f32), _whole, _slot, "repl_late")
    state_last = last.begin(False)
    slots = {}
    for ex, state in reduces:
        slots.update(ex.finish(state, [grad_x, last.token]))

    me_index = (4 * lax.axis_index("x") + 2 * lax.axis_index("y") + lax.axis_index("c")).astype(jnp.int32).reshape(1)

    def rows_block(r, c):
        return lambda tr: ((tr, c), lambda i, me: (me[0] * (r // tr) + i, 0))

    def own_block(n):
        r, c = part_shape[n]
        if n == "ffn_w_up":
            return lambda tr: ((tr, c), lambda i, me: (i, me[0]))
        return rows_block(r, c)

    out = [{}, {}, {}, {}]
    unpad = dict(ffn_w_up=2, ffn_conv_w=2, ffn_w_down=1)
    for n in matrices:
        layers = range(we_[n].shape[0])
        res = _sum_adam(me_index, [slots[(n, layer)] for layer in layers], [own_grad[(n, layer)] for layer in layers],
                        own_block(n), we_[n], me_[n], ve_[n], name=f"adam_{n}")
        for k in range(4):
            out[k][n] = _unpad_groups(res[k], unpad[n]) if n in unpad else res[k]
    pk = lambda d, order: _pack([d[n] for n in order], f32)[None]
    whole_rows = lambda tr: ((tr, _LANES), lambda i, me: (i, 0))
    res_early = _sum_adam(me_index, [slots["repl_early"]], [send_early], whole_rows, pk(w, _REPL_EARLY),
                          pk(m, _REPL_EARLY), pk(v, _REPL_EARLY), name="adam_repl_early")
    for k in range(4):
        out[k].update(zip(_REPL_EARLY, _unpack(res_early[k][0], [w[n].shape for n in _REPL_EARLY])))
    done = [out[k][n] for k in range(4) for n in matrices + _REPL_EARLY]
    slots = last.finish(state_last, done)
    transposed = lambda d: jnp.swapaxes(d["mix_w_in"], 1, 2)
    res_w_in = _sum_adam(me_index, [slots["mix_w_in"]], [g_even["mix_w_in_t"]], rows_block(w_in_rows, d_model),
                         transposed(w), transposed(m), transposed(v), name="adam_mix_w_in")
    res_small = _sum_adam(me_index, [slots["small"]], [send_small],
                          lambda tr: ((1, tr, _LANES), lambda i, me: (me[0], i, 0)),
                          pk(we_, _SMALL_SHARDED), pk(me_, _SMALL_SHARDED), pk(ve_, _SMALL_SHARDED), name="adam_small")
    res_late = _sum_adam(me_index, [slots["repl_late"]], [send_late], whole_rows, pk(w, _REPL_LATE), pk(m, _REPL_LATE),
                         pk(v, _REPL_LATE), name="adam_repl_late")
    for k in range(4):
        out[k]["mix_w_in"] = jnp.swapaxes(res_w_in[k], 1, 2)
        for n, a in zip(_SMALL_SHARDED, _unpack(res_small[k][0], [we_[n].shape for n in _SMALL_SHARDED])):
            out[k][n] = _unpad_groups(a, unpad[n]) if n in unpad else a
        out[k].update(zip(_REPL_LATE, _unpack(res_late[k][0], [w[n].shape for n in _REPL_LATE])))
    return loss, grad_x, out


_INPUTS = tuple("""x, mem, positions, norm_mix, norm_xa, norm_mem, norm_ffn, xa_wq, xa_wk, xa_wv, xa_wo, xa_q_norm, xa_k_norm, ffn_w_up, ffn_conv_w, ffn_conv_b, ffn_w_down, hg_lb_logits, mix_w_in, hg_out_norm, mla_q_a_norm, mla_w_uq, mla_kv_a_norm, mla_w_ukv, mla_qn_nope, mla_qn_rope, mla_kn_nope, mla_kn_rope, mix_w_out, s5_lam_re, s5_lam_im, s5_log_dt, s5_b_re, s5_b_im, s5_c_re, s5_c_im, s5_d, s5_w_glu_a, s5_w_glu_b, loss_target, m_norm_mix, m_norm_xa, m_norm_mem, m_norm_ffn, m_xa_wq, m_xa_wk, m_xa_wv, m_xa_wo, m_xa_q_norm, m_xa_k_norm, m_ffn_w_up, m_ffn_conv_w, m_ffn_conv_b, m_ffn_w_down, m_hg_lb_logits, m_mix_w_in, m_hg_out_norm, m_mla_q_a_norm, m_mla_w_uq, m_mla_kv_a_norm, m_mla_w_ukv, m_mla_qn_nope, m_mla_qn_rope, m_mla_kn_nope, m_mla_kn_rope, m_mix_w_out, m_s5_lam_re, m_s5_lam_im, m_s5_log_dt, m_s5_b_re, m_s5_b_im, m_s5_c_re, m_s5_c_im, m_s5_d, m_s5_w_glu_a, m_s5_w_glu_b, v_norm_mix, v_norm_xa, v_norm_mem, v_norm_ffn, v_xa_wq, v_xa_wk, v_xa_wv, v_xa_wo, v_xa_q_norm, v_xa_k_norm, v_ffn_w_up, v_ffn_conv_w, v_ffn_conv_b, v_ffn_w_down, v_hg_lb_logits, v_mix_w_in, v_hg_out_norm, v_mla_q_a_norm, v_mla_w_uq, v_mla_kv_a_norm, v_mla_w_ukv, v_mla_qn_nope, v_mla_qn_rope, v_mla_kn_nope, v_mla_kn_rope, v_mix_w_out, v_s5_lam_re, v_s5_lam_im, v_s5_log_dt, v_s5_b_re, v_s5_b_im, v_s5_c_re, v_s5_c_im, v_s5_d, v_s5_w_glu_a, v_s5_w_glu_b""".replace(" ", "").split(","))


def kernel(x, mem, positions, norm_mix, norm_xa, norm_mem, norm_ffn, xa_wq, xa_wk, xa_wv, xa_wo, xa_q_norm, xa_k_norm, ffn_w_up, ffn_conv_w, ffn_conv_b, ffn_w_down, hg_lb_logits, mix_w_in, hg_out_norm, mla_q_a_norm, mla_w_uq, mla_kv_a_norm, mla_w_ukv, mla_qn_nope, mla_qn_rope, mla_kn_nope, mla_kn_rope, mix_w_out, s5_lam_re, s5_lam_im, s5_log_dt, s5_b_re, s5_b_im, s5_c_re, s5_c_im, s5_d, s5_w_glu_a, s5_w_glu_b, loss_target, m_norm_mix, m_norm_xa, m_norm_mem, m_norm_ffn, m_xa_wq, m_xa_wk, m_xa_wv, m_xa_wo, m_xa_q_norm, m_xa_k_norm, m_ffn_w_up, m_ffn_conv_w, m_ffn_conv_b, m_ffn_w_down, m_hg_lb_logits, m_mix_w_in, m_hg_out_norm, m_mla_q_a_norm, m_mla_w_uq, m_mla_kv_a_norm, m_mla_w_ukv, m_mla_qn_nope, m_mla_qn_rope, m_mla_kn_nope, m_mla_kn_rope, m_mix_w_out, m_s5_lam_re, m_s5_lam_im, m_s5_log_dt, m_s5_b_re, m_s5_b_im, m_s5_c_re, m_s5_c_im, m_s5_d, m_s5_w_glu_a, m_s5_w_glu_b, v_norm_mix, v_norm_xa, v_norm_mem, v_norm_ffn, v_xa_wq, v_xa_wk, v_xa_wv, v_xa_wo, v_xa_q_norm, v_xa_k_norm, v_ffn_w_up, v_ffn_conv_w, v_ffn_conv_b, v_ffn_w_down, v_hg_lb_logits, v_mix_w_in, v_hg_out_norm, v_mla_q_a_norm, v_mla_w_uq, v_mla_kv_a_norm, v_mla_w_ukv, v_mla_qn_nope, v_mla_qn_rope, v_mla_kn_nope, v_mla_kn_rope, v_mix_w_out, v_s5_lam_re, v_s5_lam_im, v_s5_log_dt, v_s5_b_re, v_s5_b_im, v_s5_c_re, v_s5_c_im, v_s5_d, v_s5_w_glu_a, v_s5_w_glu_b):
    vals = dict(zip(_INPUTS, (x, mem, positions, norm_mix, norm_xa, norm_mem, norm_ffn, xa_wq, xa_wk, xa_wv, xa_wo, xa_q_norm, xa_k_norm, ffn_w_up, ffn_conv_w, ffn_conv_b, ffn_w_down, hg_lb_logits, mix_w_in, hg_out_norm, mla_q_a_norm, mla_w_uq, mla_kv_a_norm, mla_w_ukv, mla_qn_nope, mla_qn_rope, mla_kn_nope, mla_kn_rope, mix_w_out, s5_lam_re, s5_lam_im, s5_log_dt, s5_b_re, s5_b_im, s5_c_re, s5_c_im, s5_d, s5_w_glu_a, s5_w_glu_b, loss_target, m_norm_mix, m_norm_xa, m_norm_mem, m_norm_ffn, m_xa_wq, m_xa_wk, m_xa_wv, m_xa_wo, m_xa_q_norm, m_xa_k_norm, m_ffn_w_up, m_ffn_conv_w, m_ffn_conv_b, m_ffn_w_down, m_hg_lb_logits, m_mix_w_in, m_hg_out_norm, m_mla_q_a_norm, m_mla_w_uq, m_mla_kv_a_norm, m_mla_w_ukv, m_mla_qn_nope, m_mla_qn_rope, m_mla_kn_nope, m_mla_kn_rope, m_mix_w_out, m_s5_lam_re, m_s5_lam_im, m_s5_log_dt, m_s5_b_re, m_s5_b_im, m_s5_c_re, m_s5_c_im, m_s5_d, m_s5_w_glu_a, m_s5_w_glu_b, v_norm_mix, v_norm_xa, v_norm_mem, v_norm_ffn, v_xa_wq, v_xa_wk, v_xa_wv, v_xa_wo, v_xa_q_norm, v_xa_k_norm, v_ffn_w_up, v_ffn_conv_w, v_ffn_conv_b, v_ffn_w_down, v_hg_lb_logits, v_mix_w_in, v_hg_out_norm, v_mla_q_a_norm, v_mla_w_uq, v_mla_kv_a_norm, v_mla_w_ukv, v_mla_qn_nope, v_mla_qn_rope, v_mla_kn_nope, v_mla_kn_rope, v_mix_w_out, v_s5_lam_re, v_s5_lam_im, v_s5_log_dt, v_s5_b_re, v_s5_b_im, v_s5_c_re, v_s5_c_im, v_s5_d, v_s5_w_glu_a, v_s5_w_glu_b)))
    w = {n: vals[n] for n in _WEIGHTS}
    m = {n: vals["m_" + n] for n in _WEIGHTS}
    v = {n: vals["v_" + n] for n in _WEIGHTS}
    loss, grad_x, res = _train_step(vals["x"][0], vals["mem"][0], vals["positions"][0], vals["loss_target"][0],
                                    w, m, v)
    loss = lax.psum(loss[0, 0], ("x", "y", "c"))
    return (loss, grad_x[None], *[r[n] for r in res for n in _WEIGHTS])
```

```python
import functools

import jax
import jax.numpy as jnp
import numpy as np
from jax import lax
from jax.experimental import pallas as pl
from jax.experimental.pallas import tpu as pltpu

f32 = jnp.float32
bf16 = jnp.bfloat16

EPS = 1e-6
N_DEV = 8
VMEM_LIMIT = 52 * 1024 * 1024

HG_HEADS = 4
HG_DIM = 128
HG_WIDTH = HG_HEADS * HG_DIM
HG_CHUNK = 64
HG_SUB = 16
MLA_HEADS = 4
MLA_Q_RANK = 256
MLA_KV_RANK = 128
MLA_NOPE = 128
MLA_ROPE = 64
MLA_V = 128
MLA_QK = MLA_NOPE + MLA_ROPE
MLA_QK_PAD = 256
ROPE_BASE = 10000.0
IN_WIDTH = 4 * HG_WIDTH + MLA_Q_RANK + MLA_KV_RANK + MLA_ROPE
IN_PAD = 2560
XA_HEADS = 4
XA_DIM = 256
S5_GROUP = 16
S5_GROUPS = 64
S5_STATE = 64
CONV_W = 3

ADAM_LR = 0.001
ADAM_B1 = 0.9
ADAM_B2 = 0.999
ADAM_EPS = 1e-08
ADAM_WD = 0.01
ADAM_STEP = 10

_NT = (((1,), (1,)), ((), ()))
_TN = (((0,), (0,)), ((), ()))


def _pick(n, cands):
    for c in cands:
        if n % c == 0:
            return c
    return n


def _cparams(sem):
    return pltpu.CompilerParams(dimension_semantics=sem, vmem_limit_bytes=VMEM_LIMIT)


_MM_BUDGET = 36 * 1024 * 1024
_MM_TILES = ((1024, 1024), (1024, 512), (512, 1024), (512, 512), (512, 256), (256, 512), (256, 256), (256, 128),
             (128, 256), (128, 128))


def _mm(a, b, *, name, ta=False, tb=False, out_dtype=f32, add=None, b2=None, kslab=None, norm=None, rms_bwd=None,
        sq_err=None):
    m, k = (a.shape[1], a.shape[0]) if ta else a.shape
    nb = b.shape[0] if tb else b.shape[1]
    n = nb * (2 if b2 is not None else 1)
    slab, nslab = kslab if kslab is not None else (0, 1)
    assert (b.shape[1] // nslab if tb else b.shape[0]) == k, (a.shape, b.shape, ta, tb)
    assert b2 is None or (not tb and b2.shape == b.shape)
    full_rows = norm is not None or rms_bwd is not None
    assert not (full_rows and b2 is not None) and not (norm is not None and rms_bwd is not None)
    isz = lambda x: jnp.dtype(x.dtype).itemsize
    bm = bn = None
    for cm, cn in _MM_TILES:
        if m % cm or nb % cn or (full_rows and cn != n):
            continue
        need = 2 * (cm * k * isz(a) + cn * k * isz(b) * (2 if b2 is not None else 1)
                    + cm * cn * (jnp.dtype(out_dtype).itemsize + (4 if add is not None else 0)
                                 + (2 if norm is not None else 0) + (8 if rms_bwd is not None else 0)
                                 + (4 if sq_err is not None else 0)))
        if need <= _MM_BUDGET:
            bm, bn = cm, cn
            break
    assert bm is not None, (name, a.shape, b.shape)
    half = nb // bn
    dims = (((0 if ta else 1,), (1 if tb else 0,)), ((), ()))

    def body(*refs):
        refs = list(refs)
        a_ref, b_ref = refs[0], refs[1]
        b2_ref = refs.pop(2) if b2 is not None else None
        add_ref = refs.pop(2) if add is not None else None
        gain_ref = refs.pop(2) if norm is not None else None
        x_ref, xgain_ref, through_ref = (refs.pop(2), refs.pop(2), refs.pop(2)) if rms_bwd is not None else (None,) * 3
        target_ref = refs.pop(2) if sq_err is not None else None
        o_ref = refs[2]
        extra_ref = refs[3] if (norm is not None or rms_bwd is not None or sq_err is not None) else None

        def run(rhs_ref):
            r = lax.dot_general(a_ref[...].astype(bf16), rhs_ref[...].astype(bf16), dims, preferred_element_type=f32)
            if add_ref is not None:
                r = r + add_ref[...].astype(f32)
            if sq_err is not None:
                err = r - target_ref[...]
                o_ref[...] = (err * (1.0 / n)).astype(o_ref.dtype)

                @pl.when((pl.program_id(0) == 0) & (pl.program_id(1) == 0))
                def _():
                    extra_ref[...] = jnp.zeros_like(extra_ref)
                extra_ref[...] += jnp.sum(jnp.sum(err * err, axis=-1, keepdims=True), axis=0, keepdims=True) * (0.5 / n)
            elif rms_bwd is None:
                o_ref[...] = r.astype(o_ref.dtype)
            else:
                _, vjp = jax.vjp(_rms, x_ref[...], xgain_ref[...])
                dx, dgain = vjp(r)
                o_ref[...] = (through_ref[...] + dx).astype(o_ref.dtype)

                @pl.when(pl.program_id(0) == 0)
                def _():
                    extra_ref[...] = jnp.zeros_like(extra_ref)
                extra_ref[...] += dgain
            if norm is not None:
                extra_ref[...] = _rms(r, gain_ref[...]).astype(extra_ref.dtype)

        if b2_ref is None:
            run(b_ref)
        else:
            pl.when(pl.program_id(1) < half)(lambda: run(b_ref))
            pl.when(pl.program_id(1) >= half)(lambda: run(b2_ref))

    a_spec = pl.BlockSpec((k, bm), lambda i, j: (0, i)) if ta else pl.BlockSpec((bm, k), lambda i, j: (i, 0))
    if tb:
        b_spec = pl.BlockSpec((bn, k), lambda i, j: (j, slab))
    elif b2 is None:
        b_spec = pl.BlockSpec((k, bn), lambda i, j: (0, j))
    else:
        b_spec = pl.BlockSpec((k, bn), lambda i, j: (0, jnp.minimum(j, half - 1)))
    in_specs = [a_spec, b_spec]
    args = [a, b]
    if b2 is not None:
        in_specs.append(pl.BlockSpec((k, bn), lambda i, j: (0, jnp.maximum(j - half, 0))))
        args.append(b2)
    if add is not None:
        in_specs.append(pl.BlockSpec((bm, bn), lambda i, j: (i, j)))
        args.append(add)
    out_blk = pl.BlockSpec((bm, bn), lambda i, j: (i, j))
    out_specs, out_shape = out_blk, jax.ShapeDtypeStruct((m, n), out_dtype)
    row_blk = pl.BlockSpec((1, bn), lambda i, j: (0, j))
    if norm is not None:
        in_specs.append(row_blk)
        args.append(norm)
        out_specs, out_shape = [out_blk, out_blk], [out_shape, jax.ShapeDtypeStruct((m, n), bf16)]
    if rms_bwd is not None:
        in_specs += [out_blk, row_blk, out_blk]
        args += list(rms_bwd)
        out_specs, out_shape = [out_blk, row_blk], [out_shape, jax.ShapeDtypeStruct((1, n), f32)]
    if sq_err is not None:
        in_specs.append(out_blk)
        args.append(sq_err)
        out_specs = [out_blk, pl.BlockSpec((1, 1), lambda i, j: (0, 0))]
        out_shape = [out_shape, jax.ShapeDtypeStruct((1, 1), f32)]
    carried = rms_bwd is not None or sq_err is not None
    return pl.pallas_call(
        body, grid=(m // bm, n // bn), in_specs=in_specs, out_specs=out_specs, out_shape=out_shape,
        compiler_params=_cparams(("arbitrary" if carried else "parallel", "arbitrary" if sq_err is not None else "parallel")),
        name=name)(*args)


def _as_tuple(x):
    return tuple(x) if isinstance(x, (tuple, list)) else (x,)


def _full_spec(p):
    nd = p.ndim
    return pl.BlockSpec(p.shape, lambda i, _nd=nd: (0,) * _nd)


def _window(a, start, width):
    assert start % width == 0 and width % 128 == 0
    return (a, start // width, width)


def _row_array(x):
    return x[0] if isinstance(x, tuple) else x


def _row_shape(x):
    return (x[0].shape[0], x[2]) if isinstance(x, tuple) else x.shape


def _row_spec(x, tile):
    if isinstance(x, tuple):
        return pl.BlockSpec((tile, x[2]), lambda i, _b=x[1]: (i, _b))
    return pl.BlockSpec((tile, x.shape[1]), lambda i: (i, 0))


def _rows(fn, rows, params, outs, *, name, tile=256, accs=()):
    length = _row_shape(rows[0])[0]
    tile = min(tile, length)
    nr, npar, no = len(rows), len(params), len(outs)

    def body(*refs):
        r, p, o = refs[:nr], refs[nr:nr + npar], refs[nr + npar:]
        res = _as_tuple(fn(*[x[...].astype(f32) for x in r], *[x[...] for x in p]))
        for kk in range(no):
            o[kk][...] = res[kk].astype(o[kk].dtype)
        if accs:
            @pl.when(pl.program_id(0) == 0)
            def _():
                for kk in range(no, no + len(accs)):
                    o[kk][...] = jnp.zeros_like(o[kk])
            for kk in range(no, no + len(accs)):
                o[kk][...] += res[kk]

    in_specs = [_row_spec(x, tile) for x in rows] + [_full_spec(p) for p in params]
    out_specs = [pl.BlockSpec((tile, w), lambda i: (i, 0)) for w, _ in outs]
    out_shape = [jax.ShapeDtypeStruct((length, w), d) for w, d in outs]
    for s in accs:
        out_specs.append(pl.BlockSpec(s, lambda i, _nd=len(s): (0,) * _nd))
        out_shape.append(jax.ShapeDtypeStruct(s, f32))
    res = pl.pallas_call(body, grid=(length // tile,), in_specs=in_specs, out_specs=out_specs, out_shape=out_shape,
                         compiler_params=_cparams(("arbitrary",)), name=name)(*[_row_array(x) for x in rows], *params)
    return res


def _rows_bwd(fn, rows, params, cts, *, name, rgrad, pgrad, tile=256, addends=None):
    addends = {i: (a if isinstance(a, list) else [(a, 0)]) for i, a in (addends or {}).items()}
    length = _row_shape(rows[0])[0]
    tile = min(tile, length)
    nr, npar, nc = len(rows), len(params), len(cts)
    ridx = [i for i in range(nr) if rgrad[i] is not None]
    pidx = [i for i in range(npar) if pgrad[i]]
    flat_addends = [(i, a, off) for i in sorted(addends) for a, off in addends[i]]
    na = len(flat_addends)

    def body(*refs):
        r, p, c = refs[:nr], refs[nr:nr + npar], refs[nr + npar:nr + npar + nc]
        ad = refs[nr + npar + nc:nr + npar + nc + na]
        o = refs[nr + npar + nc + na:]
        rv = [x[...].astype(f32) for x in r]
        pv = [x[...] for x in p]
        cv = tuple(x[...].astype(f32) for x in c)

        def g(*d):
            rr, pp = list(rv), list(pv)
            for n_, i_ in enumerate(ridx):
                rr[i_] = d[n_]
            for n_, i_ in enumerate(pidx):
                pp[i_] = d[len(ridx) + n_]
            return _as_tuple(fn(*rr, *pp))

        _, vjp = jax.vjp(g, *[rv[i] for i in ridx], *[pv[i] for i in pidx])
        grads = vjp(cv)
        for n_, i_ in enumerate(ridx):
            val = grads[n_]
            for k_, (j_, a_, off) in enumerate(flat_addends):
                if j_ == i_:
                    extra = ad[k_][...].astype(f32)
                    if extra.shape[1] != val.shape[1]:
                        extra = jnp.pad(extra, ((0, 0), (off, val.shape[1] - off - extra.shape[1])))
                    val = val + extra
            o[n_][...] = val.astype(o[n_].dtype)
        if pidx:
            @pl.when(pl.program_id(0) == 0)
            def _():
                for n_ in range(len(pidx)):
                    o[len(ridx) + n_][...] = jnp.zeros_like(o[len(ridx) + n_])
            for n_ in range(len(pidx)):
                o[len(ridx) + n_][...] += grads[len(ridx) + n_]

    plain = lambda shape: pl.BlockSpec((tile, shape[1]), lambda i: (i, 0))
    in_specs = ([_row_spec(x, tile) for x in rows] + [_full_spec(p) for p in params] + [plain(x.shape) for x in cts]
                + [plain(a.shape) for _, a, _ in flat_addends])
    out_specs = [plain(_row_shape(rows[i])) for i in ridx] + [_full_spec(params[i]) for i in pidx]
    out_shape = ([jax.ShapeDtypeStruct(_row_shape(rows[i]), rgrad[i]) for i in ridx]
                 + [jax.ShapeDtypeStruct(params[i].shape, f32) for i in pidx])
    res = pl.pallas_call(body, grid=(length // tile,), in_specs=in_specs, out_specs=out_specs, out_shape=out_shape,
                         compiler_params=_cparams(("arbitrary",)), name=name)(
        *[_row_array(x) for x in rows], *params, *cts, *[a for _, a, _ in flat_addends])
    return list(res[:len(ridx)]), list(res[len(ridx):])


def _rms(x, g):
    return x * lax.rsqrt(jnp.mean(x * x, axis=-1, keepdims=True) + EPS) * g


def _rms_twice(x, g):
    y = _rms(x, g)
    return y, y


def _silu(x):
    return x * jax.nn.sigmoid(x)


_UP_COLS = 512


def _ffn_up_conv(hf, w_up, cw, cb, *, name):
    length, d = hf.shape
    ff = w_up.shape[1] // 2
    bm = _pick(length, (1024, 512, 256))
    bn = _UP_COLS
    nj = ff // bn

    def body(hf_ref, wg_ref, wv_ref, cwg, cwv, cbg, cbv, ug_ref, uv_ref, a_ref, halo_g, halo_v):
        i, j = pl.program_id(0), pl.program_id(1)
        x = hf_ref[...]
        rows = lax.broadcasted_iota(jnp.int32, (bm, bn), 0)
        pad = jnp.zeros((bm - 8, bn), f32)

        def half(w_ref, cw_ref, cb_ref, u_ref, halo):
            u = jnp.dot(x, w_ref[...], preferred_element_type=f32).astype(bf16)
            u_ref[...] = u
            u = u.astype(f32)
            prev = jnp.where(i == 0, 0.0, halo[j])
            x1 = jnp.where(rows >= 1, pltpu.roll(u, 1, axis=0), jnp.concatenate([pltpu.roll(prev, 1, axis=0), pad]))
            x2 = jnp.where(rows >= 2, pltpu.roll(u, 2, axis=0), jnp.concatenate([pltpu.roll(prev, 2, axis=0), pad]))
            halo[j] = u[bm - 8:, :]
            return cw_ref[2:3, :] * u + cw_ref[1:2, :] * x1 + cw_ref[0:1, :] * x2 + cb_ref[...]

        g = half(wg_ref, cwg, cbg, ug_ref, halo_g)
        v = half(wv_ref, cwv, cbv, uv_ref, halo_v)
        a_ref[...] = (_silu(g) * v).astype(a_ref.dtype)

    col = lambda r, off: pl.BlockSpec((r, bn), lambda i, j, _o=off: (0, j + _o))
    out_blk = pl.BlockSpec((bm, bn), lambda i, j: (i, j))
    sds = jax.ShapeDtypeStruct((length, ff), bf16)
    return pl.pallas_call(
        body, grid=(length // bm, nj),
        in_specs=[pl.BlockSpec((bm, d), lambda i, j: (i, 0)), col(d, 0), col(d, nj), col(CONV_W, 0), col(CONV_W, nj),
                  col(1, 0), col(1, nj)],
        out_specs=[out_blk] * 3, out_shape=[sds] * 3,
        scratch_shapes=[pltpu.VMEM((nj, 8, bn), f32), pltpu.VMEM((nj, 8, bn), f32)],
        compiler_params=_cparams(("arbitrary", "arbitrary")), name=name)(hf, w_up, w_up, cw, cw, cb, cb)


def _ffn_da_dconv(dout, w_down, u_g, u_v, cw, cb, *, name):
    length, d = dout.shape
    ff = u_g.shape[1]
    bm = _pick(length, (1024, 512, 256))
    bn = _UP_COLS
    nj, ni = ff // bn, length // bm

    def body(dout_ref, wd_ref, ug_ref, uv_ref, pg_ref, pv_ref, cwg, cwv, cbg, cbv,
             dug_ref, duv_ref, sums_g_ref, sums_v_ref, halo_g, halo_v, acc_g, acc_v):
        s, j = pl.program_id(0), pl.program_id(1)
        rows = lax.broadcasted_iota(jnp.int32, (bm, bn), 0)
        row8 = lax.broadcasted_iota(jnp.int32, (8, bn), 0)
        pad = jnp.zeros((bm - 8, bn), f32)
        da = lax.dot_general(dout_ref[...].astype(bf16), wd_ref[...], _NT, preferred_element_type=f32)

        def conv(u_ref, p_ref, cw_ref, cb_ref):
            x = u_ref[...].astype(f32)
            prev = jnp.where(s == ni - 1, 0.0, p_ref[...].astype(f32))
            x1 = jnp.where(rows >= 1, pltpu.roll(x, 1, axis=0), jnp.concatenate([pltpu.roll(prev, 1, axis=0), pad]))
            x2 = jnp.where(rows >= 2, pltpu.roll(x, 2, axis=0), jnp.concatenate([pltpu.roll(prev, 2, axis=0), pad]))
            return cw_ref[2:3, :] * x + cw_ref[1:2, :] * x1 + cw_ref[0:1, :] * x2 + cb_ref[...], x, x1, x2

        g, xg, xg1, xg2 = conv(ug_ref, pg_ref, cwg, cbg)
        v, xv, xv1, xv2 = conv(uv_ref, pv_ref, cwv, cbv)
        sg = jax.nn.sigmoid(g)
        dg = da * v * (sg * (1.0 + g * (1.0 - sg)))
        dv = da * (g * sg)

        def back(dy, x, x1, x2, cw_ref, du_ref, sums_ref, halo, acc):
            nxt = jnp.where(s == 0, 0.0, halo[j])
            up1 = jnp.where(rows < bm - 1, pltpu.roll(dy, bm - 1, axis=0),
                            jnp.concatenate([pad, pltpu.roll(nxt, 7, axis=0)]))
            up2 = jnp.where(rows < bm - 2, pltpu.roll(dy, bm - 2, axis=0),
                            jnp.concatenate([pad, pltpu.roll(nxt, 6, axis=0)]))
            halo[j] = dy[:8, :]
            du_ref[...] = (cw_ref[2:3, :] * dy + cw_ref[1:2, :] * up1 + cw_ref[0:1, :] * up2).astype(du_ref.dtype)
            col = lambda t: jnp.sum(t, axis=0, keepdims=True)
            part = jnp.where(row8 == 0, col(dy * x2), jnp.where(row8 == 1, col(dy * x1), jnp.where(
                row8 == 2, col(dy * x), jnp.where(row8 == 3, col(dy), 0.0))))
            total = jnp.where(s == 0, part, acc[j] + part)
            acc[j] = total
            sums_ref[...] = total

        back(dg, xg, xg1, xg2, cwg, dug_ref, sums_g_ref, halo_g, acc_g)
        back(dv, xv, xv1, xv2, cwv, duv_ref, sums_v_ref, halo_v, acc_v)

    rb = lambda s: ni - 1 - s
    tile = pl.BlockSpec((bm, bn), lambda s, j: (rb(s), j))
    before = pl.BlockSpec((8, bn), lambda s, j: (jnp.maximum(rb(s) * (bm // 8) - 1, 0), j))
    col = lambda r, off: pl.BlockSpec((r, bn), lambda s, j, _o=off: (0, j + _o))
    sds = jax.ShapeDtypeStruct
    dug, duv, sums_g, sums_v = pl.pallas_call(
        body, grid=(ni, nj),
        in_specs=[pl.BlockSpec((bm, d), lambda s, j: (rb(s), 0)), pl.BlockSpec((bn, d), lambda s, j: (j, 0)),
                  tile, tile, before, before, col(CONV_W, 0), col(CONV_W, nj), col(1, 0), col(1, nj)],
        out_specs=[tile, tile] + [pl.BlockSpec((8, bn), lambda s, j: (s, j))] * 2,
        out_shape=[sds((length, ff), bf16), sds((length, ff), bf16), sds((ni * 8, ff), f32), sds((ni * 8, ff), f32)],
        scratch_shapes=[pltpu.VMEM((nj, 8, bn), f32)] * 4,
        compiler_params=_cparams(("arbitrary", "arbitrary")), name=name)(
        dout, w_down, u_g, u_v, u_g, u_v, cw, cw, cb, cb)
    last = (ni - 1) * 8
    both = lambda lo, hi: jnp.concatenate([sums_g[last + lo:last + hi], sums_v[last + lo:last + hi]], axis=1)
    return dug, duv, both(0, CONV_W), both(CONV_W, CONV_W + 1)


def _ffn_fwd(h, w, tag, hf=None, loss_target=None):
    if hf is None:
        hf, = _rows(_rms, [h], [w["norm_ffn"]], [(h.shape[1], bf16)], name=f"ffn_norm_{tag}")
    u_g, u_v, a = _ffn_up_conv(hf, w["ffn_w_up"], w["ffn_conv_w"], w["ffn_conv_b"], name=f"ffn_up_{tag}")
    out = _mm(a, w["ffn_w_down"], add=h, sq_err=loss_target, name=f"ffn_down_{tag}")
    return out, (h, hf, u_g, u_v, a)


def _ffn_bwd(dout, w, saved, tag):
    h, hf, u_g, u_v, a = saved
    g = {"ffn_w_down": _mm(a, dout, ta=True, out_dtype=bf16, name=f"ffn_dwdown_{tag}")}
    dug, duv, g["ffn_conv_w"], g["ffn_conv_b"] = _ffn_da_dconv(
        dout, w["ffn_w_down"], u_g, u_v, w["ffn_conv_w"], w["ffn_conv_b"], name=f"ffn_dconv_{tag}")
    dhf = _mm(dug, w["ffn_w_up"], tb=True, kslab=(0, 2), name=f"ffn_dhf_g_{tag}")
    dh, g["norm_ffn"] = _mm(duv, w["ffn_w_up"], tb=True, kslab=(1, 2), add=dhf, rms_bwd=(h, w["norm_ffn"], dout),
                            name=f"ffn_dhf_v_{tag}")
    g["ffn_w_up"] = _mm(hf, dug, ta=True, b2=duv, out_dtype=bf16, name=f"ffn_dwup_{tag}")
    return dh, g


def _xattn_fn(qx, kx, vx, qg, kg):
    outs = []
    for hh in range(XA_HEADS):
        sl = slice(hh * XA_DIM, (hh + 1) * XA_DIM)
        q = _rms(qx[:, sl], qg).astype(bf16)
        k = _rms(kx[:, sl], kg).astype(bf16)
        s = lax.dot_general(q, k, _NT, preferred_element_type=f32) * (XA_DIM ** -0.5)
        s = s - jnp.max(s, axis=-1, keepdims=True)
        p = jnp.exp(s)
        p = p / jnp.sum(p, axis=-1, keepdims=True)
        outs.append(jnp.dot(p.astype(bf16), vx[:, sl].astype(bf16), preferred_element_type=f32))
    return jnp.concatenate(outs, axis=-1)


def _xattn_fwd(h, mem, w, tag, hx=None):
    d = h.shape[1]
    if hx is None:
        hx, = _rows(_rms, [h], [w["norm_xa"]], [(d, bf16)], name=f"xa_norm_{tag}")
    qx = _mm(hx, w["xa_wq"], out_dtype=bf16, name=f"xa_q_{tag}")
    m, = _rows(_rms, [mem], [w["norm_mem"]], [(d, bf16)], name=f"xa_mnorm_{tag}")
    kx = _mm(m, w["xa_wk"], name=f"xa_k_{tag}")
    vx = _mm(m, w["xa_wv"], name=f"xa_v_{tag}")
    o, = _rows(_xattn_fn, [qx], [kx, vx, w["xa_q_norm"], w["xa_k_norm"]], [(d, bf16)], tile=1024,
               name=f"xa_attn_{tag}")
    out, hf = _mm(o, w["xa_wo"], add=h, norm=w["norm_ffn"], name=f"xa_o_{tag}")
    return out, (h, hx, qx, m, kx, vx, o), hf


def _xattn_bwd(dout, mem, w, saved, tag):
    h, hx, qx, m, kx, vx, o = saved
    g = {}
    do = _mm(dout, w["xa_wo"], tb=True, out_dtype=bf16, name=f"xa_do_{tag}")
    g["xa_wo"] = _mm(o, dout, ta=True, out_dtype=bf16, name=f"xa_dwo_{tag}")
    (dqx,), (dkx, dvx, g["xa_q_norm"], g["xa_k_norm"]) = _rows_bwd(
        _xattn_fn, [qx], [kx, vx, w["xa_q_norm"], w["xa_k_norm"]], [do], rgrad=[bf16], pgrad=[True] * 4,
        tile=1024, name=f"xa_dattn_{tag}")
    dh, g["norm_xa"] = _mm(dqx, w["xa_wq"], tb=True, rms_bwd=(h, w["norm_xa"], dout), name=f"xa_dhx_{tag}")
    g["xa_wq"] = _mm(hx, dqx, ta=True, out_dtype=bf16, name=f"xa_dwq_{tag}")
    dm = _mm(dkx, w["xa_wk"], tb=True, name=f"xa_dm_k_{tag}")
    dm = _mm(dvx, w["xa_wv"], tb=True, add=dm, name=f"xa_dm_v_{tag}")
    g["xa_wk"] = _mm(m, dkx, ta=True, out_dtype=bf16, name=f"xa_dwk_{tag}")
    g["xa_wv"] = _mm(m, dvx, ta=True, out_dtype=bf16, name=f"xa_dwv_{tag}")
    _, (g["norm_mem"],) = _rows_bwd(_rms, [mem], [w["norm_mem"]], [dm], rgrad=[None], pgrad=[True],
                                    name=f"xa_dmnorm_{tag}")
    return dh, g


_HG_GROUP = 8


def _hg_chunk(q, k, v, g, *sts):
    c = q.shape[0]
    heads = [slice(h * HG_DIM, (h + 1) * HG_DIM) for h in range(len(sts))]
    tri = (lax.broadcasted_iota(jnp.int32, (c, c), 0) >= lax.broadcasted_iota(jnp.int32, (c, c), 1)).astype(f32)
    b = jnp.dot(tri, g, precision=lax.Precision.HIGHEST, preferred_element_type=f32)
    bend = jnp.sum(g, axis=0, keepdims=True)
    qe = (q * jnp.exp(b)).astype(bf16)
    kd = (k * jnp.exp(bend - b)).astype(bf16)
    vb = v.astype(bf16)
    decay = jnp.exp(bend)
    o_inter = [lax.dot_general(qe[:, hs], st.astype(bf16), _NT, preferred_element_type=f32) for hs, st in zip(heads, sts)]
    new = [st * decay[:, hs] + lax.dot_general(vb[:, hs], kd[:, hs], _TN, preferred_element_type=f32)
           for hs, st in zip(heads, sts)]
    outs = []
    for i in range(c // HG_SUB):
        lo, n = HG_SUB * i, HG_SUB * (i + 1)
        ref = jnp.sum(g[:lo], axis=0, keepdims=True) if i else jnp.zeros((1, g.shape[1]), f32)
        qh = (q[lo:n] * jnp.exp(b[lo:n] - ref)).astype(bf16)
        kh = (k[:n] * jnp.exp(ref - b[:n])).astype(bf16)
        keep = (lax.broadcasted_iota(jnp.int32, (HG_SUB, n), 1)
                <= lo + lax.broadcasted_iota(jnp.int32, (HG_SUB, n), 0))
        scores = [lax.dot_general(qh[:, hs], kh[:, hs], _NT, preferred_element_type=f32) for hs in heads]
        scores = [jnp.where(keep, a, 0.0).astype(bf16) for a in scores]
        outs.append(jnp.concatenate([jnp.dot(a, vb[:n, hs], preferred_element_type=f32)
                                     for a, hs in zip(scores, heads)], axis=1))
    return (jnp.concatenate(outs, axis=0) + jnp.concatenate(o_inter, axis=1), *new)


def _hg_fwd(q, k, v, g, *, name):
    length = q.shape[0]
    rows = _HG_GROUP * HG_CHUNK
    ng = length // rows
    nc = length // HG_CHUNK

    def body(q_ref, k_ref, v_ref, g_ref, o_ref, st_ref, state):
        @pl.when(pl.program_id(0) == 0)
        def _():
            state[...] = jnp.zeros_like(state)

        states = [state[h] for h in range(HG_HEADS)]
        for ci in range(_HG_GROUP):
            sl = slice(ci * HG_CHUNK, (ci + 1) * HG_CHUNK)
            for h in range(HG_HEADS):
                st_ref[h, ci] = states[h].astype(st_ref.dtype)
            o, *states = _hg_chunk(q_ref[sl, :], k_ref[sl, :], v_ref[sl, :], g_ref[sl, :], *states)
            o_ref[sl, :] = o
        for h in range(HG_HEADS):
            state[h] = states[h]

    blk = pl.BlockSpec((rows, HG_WIDTH), lambda c: (c, 0))
    return pl.pallas_call(
        body, grid=(ng,), in_specs=[blk] * 4,
        out_specs=[blk, pl.BlockSpec((HG_HEADS, _HG_GROUP, HG_DIM, HG_DIM), lambda c: (0, c, 0, 0))],
        out_shape=[jax.ShapeDtypeStruct((length, HG_WIDTH), f32),
                   jax.ShapeDtypeStruct((HG_HEADS, nc, HG_DIM, HG_DIM), bf16)],
        scratch_shapes=[pltpu.VMEM((HG_HEADS, HG_DIM, HG_DIM), f32)],
        compiler_params=_cparams(("arbitrary",)), name=name)(q, k, v, g)


def _hg_bwd(q, k, v, g, states, do, *, name):
    length = q.shape[0]
    rows = _HG_GROUP * HG_CHUNK
    ng = length // rows

    def body(q_ref, k_ref, v_ref, g_ref, st_ref, do_ref, dq_ref, dk_ref, dv_ref, dg_ref, dstate):
        @pl.when(pl.program_id(0) == 0)
        def _():
            dstate[...] = jnp.zeros_like(dstate)

        dstates = [dstate[h] for h in range(HG_HEADS)]
        for ci in reversed(range(_HG_GROUP)):
            sl = slice(ci * HG_CHUNK, (ci + 1) * HG_CHUNK)
            _, vjp = jax.vjp(_hg_chunk, q_ref[sl, :], k_ref[sl, :], v_ref[sl, :], g_ref[sl, :],
                             *[st_ref[h, ci].astype(f32) for h in range(HG_HEADS)])
            dq, dk, dv, dg, *dstates = vjp((do_ref[sl, :], *dstates))
            dq_ref[sl, :] = dq
            dk_ref[sl, :] = dk
            dv_ref[sl, :] = dv
            dg_ref[sl, :] = dg
        for h in range(HG_HEADS):
            dstate[h] = dstates[h]

    blk = pl.BlockSpec((rows, HG_WIDTH), lambda c: (ng - 1 - c, 0))
    sds = jax.ShapeDtypeStruct((length, HG_WIDTH), f32)
    return pl.pallas_call(
        body, grid=(ng,),
        in_specs=[blk] * 4 + [pl.BlockSpec((HG_HEADS, _HG_GROUP, HG_DIM, HG_DIM), lambda c: (0, ng - 1 - c, 0, 0)), blk],
        out_specs=[blk] * 4, out_shape=[sds] * 4,
        scratch_shapes=[pltpu.VMEM((HG_HEADS, HG_DIM, HG_DIM), f32)],
        compiler_params=_cparams(("arbitrary",)), name=name)(q, k, v, g, states, do)


_ATT_BLK = 512
_ATT_SCALE = MLA_QK ** -0.5
_NEG = -1e30


def _att_mask(i, j, t):
    rows = i * t + lax.broadcasted_iota(jnp.int32, (t, t), 0)
    cols = j * t + lax.broadcasted_iota(jnp.int32, (t, t), 1)
    return cols <= rows


def _att_fwd(q, k, v, *, name):
    length = q.shape[0]
    t = min(_ATT_BLK, length)
    nq = length // t
    qw, vw = MLA_QK_PAD, MLA_V
    heads = range(MLA_HEADS)

    def body(q_ref, k_ref, v_ref, o_ref, lse_ref):
        i = pl.program_id(0)
        qbs = [q_ref[:, h * qw:(h + 1) * qw] for h in heads]

        def step(j, carry, diagonal=False):
            off = pl.multiple_of(j * t, t)
            out = []
            for h in heads:
                m, l, acc = carry[h]
                ks = k_ref[pl.ds(off, t), h * qw:(h + 1) * qw]
                vs = v_ref[pl.ds(off, t), h * vw:(h + 1) * vw]
                s = lax.dot_general(qbs[h], ks, _NT, preferred_element_type=f32) * _ATT_SCALE
                if diagonal:
                    s = jnp.where(_att_mask(i, j, t), s, _NEG)
                m_new = jnp.maximum(m, jnp.max(s, axis=-1, keepdims=True))
                alpha = jnp.exp(m - m_new)
                p = jnp.exp(s - m_new)
                l = alpha * l + jnp.sum(p, axis=-1, keepdims=True)
                acc = alpha * acc + jnp.dot(p.astype(bf16), vs, preferred_element_type=f32)
                out.append((m_new, l, acc))
            return tuple(out)

        init = tuple((jnp.full((t, 1), _NEG, f32), jnp.zeros((t, 1), f32), jnp.zeros((t, vw), f32)) for _ in heads)
        res = step(i, lax.fori_loop(0, i, step, init), diagonal=True)
        for h in heads:
            m, l, acc = res[h]
            o_ref[:, h * vw:(h + 1) * vw] = (acc / l).astype(o_ref.dtype)
            lse_ref[:, h * vw:(h + 1) * vw] = jnp.broadcast_to(m + jnp.log(l), (t, vw))

    return pl.pallas_call(
        body, grid=(nq,),
        in_specs=[pl.BlockSpec((t, q.shape[1]), lambda i: (i, 0)), pl.BlockSpec(k.shape, lambda i: (0, 0)),
                  pl.BlockSpec(v.shape, lambda i: (0, 0))],
        out_specs=[pl.BlockSpec((t, v.shape[1]), lambda i: (i, 0))] * 2,
        out_shape=[jax.ShapeDtypeStruct(v.shape, bf16), jax.ShapeDtypeStruct(v.shape, f32)],
        compiler_params=_cparams(("arbitrary",)), name=name)(q, k, v)


def _att_bwd(q, k, v, o, lse, do, *, name):
    length = q.shape[0]
    t = min(_ATT_BLK, length)
    nq = length // t
    qw, vw = MLA_QK_PAD, MLA_V
    heads = range(MLA_HEADS)

    def dq_body(q_ref, k_ref, v_ref, o_ref, lse_ref, do_ref, dq_ref, delta_ref):
        i = pl.program_id(0)
        qbs = [q_ref[:, h * qw:(h + 1) * qw] for h in heads]
        dobs = [do_ref[:, h * vw:(h + 1) * vw] for h in heads]
        lses = [lse_ref[:, h * vw:h * vw + 1] for h in heads]
        deltas = [jnp.sum(dobs[h].astype(f32) * o_ref[:, h * vw:(h + 1) * vw].astype(f32), axis=-1, keepdims=True)
                  for h in heads]

        def step(j, dqs, diagonal=False):
            off = pl.multiple_of(j * t, t)
            out = []
            for h in heads:
                ks = k_ref[pl.ds(off, t), h * qw:(h + 1) * qw]
                vs = v_ref[pl.ds(off, t), h * vw:(h + 1) * vw]
                s = lax.dot_general(qbs[h], ks, _NT, preferred_element_type=f32) * _ATT_SCALE
                p = jnp.exp(s - lses[h])
                if diagonal:
                    p = jnp.where(_att_mask(i, j, t), p, 0.0)
                dp = lax.dot_general(dobs[h], vs, _NT, preferred_element_type=f32)
                ds = p * (dp - deltas[h]) * _ATT_SCALE
                out.append(dqs[h] + jnp.dot(ds.astype(bf16), ks, preferred_element_type=f32))
            return tuple(out)

        dqs = step(i, lax.fori_loop(0, i, step, tuple(jnp.zeros((t, qw), f32) for _ in heads)), diagonal=True)
        for h in heads:
            dq_ref[:, h * qw:(h + 1) * qw] = dqs[h].astype(dq_ref.dtype)
            delta_ref[:, h * vw:(h + 1) * vw] = jnp.broadcast_to(deltas[h], (t, vw))

    qblk = pl.BlockSpec((t, q.shape[1]), lambda i: (i, 0))
    vblk = pl.BlockSpec((t, v.shape[1]), lambda i: (i, 0))
    qfull = pl.BlockSpec(q.shape, lambda i: (0, 0))
    vfull = pl.BlockSpec(v.shape, lambda i: (0, 0))
    dq, delta = pl.pallas_call(
        dq_body, grid=(nq,), in_specs=[qblk, qfull, vfull, vblk, vblk, vblk], out_specs=[qblk, vblk],
        out_shape=[jax.ShapeDtypeStruct(q.shape, bf16), jax.ShapeDtypeStruct(lse.shape, f32)],
        compiler_params=_cparams(("arbitrary",)), name=name + "_dq")(q, k, v, o, lse, do)

    def dkv_body(k_ref, v_ref, q_ref, do_ref, lse_ref, delta_ref, dk_ref, dv_ref):
        j = pl.program_id(0)
        kbs = [k_ref[:, h * qw:(h + 1) * qw] for h in heads]
        vbs = [v_ref[:, h * vw:(h + 1) * vw] for h in heads]

        def step(i, carry, diagonal=False):
            off = pl.multiple_of(i * t, t)
            out = []
            for h in heads:
                dk, dv = carry[h]
                qs = q_ref[pl.ds(off, t), h * qw:(h + 1) * qw]
                dos = do_ref[pl.ds(off, t), h * vw:(h + 1) * vw]
                lse_i = lse_ref[pl.ds(off, t), h * vw:h * vw + 1]
                delta_i = delta_ref[pl.ds(off, t), h * vw:h * vw + 1]
                s = lax.dot_general(qs, kbs[h], _NT, preferred_element_type=f32) * _ATT_SCALE
                p = jnp.exp(s - lse_i)
                if diagonal:
                    p = jnp.where(_att_mask(i, j, t), p, 0.0)
                dv = dv + lax.dot_general(p.astype(bf16), dos, _TN, preferred_element_type=f32)
                dp = lax.dot_general(dos, vbs[h], _NT, preferred_element_type=f32)
                ds = p * (dp - delta_i) * _ATT_SCALE
                dk = dk + lax.dot_general(ds.astype(bf16), qs, _TN, preferred_element_type=f32)
                out.append((dk, dv))
            return tuple(out)

        first = step(j, tuple((jnp.zeros((t, qw), f32), jnp.zeros((t, vw), f32)) for _ in heads), diagonal=True)
        res = lax.fori_loop(j + 1, nq, step, first)
        for h in heads:
            dk_ref[:, h * qw:(h + 1) * qw] = res[h][0].astype(dk_ref.dtype)
            dv_ref[:, h * vw:(h + 1) * vw] = res[h][1].astype(dv_ref.dtype)

    dk, dv = pl.pallas_call(
        dkv_body, grid=(nq,), in_specs=[qblk, vblk, qfull, vfull, vfull, vfull], out_specs=[qblk, vblk],
        out_shape=[jax.ShapeDtypeStruct(k.shape, bf16), jax.ShapeDtypeStruct(v.shape, bf16)],
        compiler_params=_cparams(("arbitrary",)), name=name + "_dkv")(k, v, q, do, lse, delta)
    return dq, dk, dv


_C_Q = 4 * HG_WIDTH
_C_KV = _C_Q + MLA_Q_RANK
_C_KPE = _C_KV + MLA_KV_RANK


def _rms_n(x, g, n):
    return x * lax.rsqrt(jnp.sum(x * x, axis=-1, keepdims=True) * (1.0 / n) + EPS) * g


def _mix_a(proj, l0, l1, q_a_norm, kv_a_norm):
    lb = jax.nn.sigmoid(l0 - l1)
    f = lb + (1.0 - lb) * jax.nn.sigmoid(proj[:, HG_WIDTH:2 * HG_WIDTH])
    qf = _silu(proj[:, :HG_WIDTH])
    v = proj[:, 2 * HG_WIDTH:3 * HG_WIDTH]
    cqn = _rms(proj[:, _C_Q:_C_KV], q_a_norm)
    ckvn = _rms(proj[:, _C_KV:_C_KPE], kv_a_norm)
    return qf, 1.0 - f, v, jnp.log(f), cqn, ckvn


def _mix_b(qraw, kvraw, kpe_raw, cos, sin, qn_nope, qn_rope, kn_nope, kn_rope, perm):
    def rope(x):
        return x * cos + jnp.dot(x, perm, precision=lax.Precision.HIGHEST, preferred_element_type=f32) * sin

    kpe = rope(_rms_n(kpe_raw, kn_rope, MLA_ROPE))
    qs, ks, vs = [], [], []
    for hh in range(MLA_HEADS):
        base = hh * MLA_QK_PAD
        qs.append(_rms(qraw[:, base:base + MLA_NOPE], qn_nope))
        qs.append(rope(_rms_n(qraw[:, base + MLA_NOPE:base + MLA_QK_PAD], qn_rope, MLA_ROPE)))
        ks.append(_rms(kvraw[:, base:base + MLA_NOPE], kn_nope))
        ks.append(kpe)
        vs.append(kvraw[:, base + MLA_NOPE:base + MLA_QK_PAD])
    return jnp.concatenate(qs, axis=-1), jnp.concatenate(ks, axis=-1), jnp.concatenate(vs, axis=-1)


def _mix_c(o_hg, gate, o_mla, hg_out_norm):
    parts = []
    for hh in range(HG_HEADS):
        sl = slice(hh * HG_DIM, (hh + 1) * HG_DIM)
        parts.append(_rms(o_hg[:, sl], hg_out_norm[:, sl]))
    o = jnp.concatenate(parts, axis=-1) * _silu(gate)
    return jnp.concatenate([o, o_mla], axis=-1)


def _rope_perm():
    p = np.zeros((128, 128), np.float32)
    half = MLA_ROPE // 2
    for i in range(half):
        p[i + half, i] = -1.0
        p[i, i + half] = 1.0
    return jnp.asarray(p)


def _mixer_fwd(h, cos, sin, w, tag, next_gain):
    d = h.shape[1]
    hn, = _rows(_rms, [h], [w["norm_mix"]], [(d, bf16)], name=f"mix_norm_{tag}")
    proj = _mm(hn, w["mix_w_in"], name=f"mix_in_{tag}")
    pa = [w["lb0"], w["lb1"], w["mla_q_a_norm"], w["mla_kv_a_norm"]]
    qf, kk, vv, logf, cqn, ckvn = _rows(
        _mix_a, [proj], pa, [(HG_WIDTH, f32)] * 4 + [(MLA_Q_RANK, bf16), (MLA_KV_RANK, bf16)], name=f"mix_a_{tag}")
    o_hg, states = _hg_fwd(qf, kk, vv, logf, name=f"hg_fwd_{tag}")
    qraw = _mm(cqn, w["mla_w_uq"], out_dtype=bf16, name=f"mla_uq_{tag}")
    kvraw = _mm(ckvn, w["mla_w_ukv"], out_dtype=bf16, name=f"mla_ukv_{tag}")
    pb = [w["mla_qn_nope"], w["mla_qn_rope"], w["mla_kn_nope"], w["mla_kn_rope"], w["rope_perm"]]
    kpe_raw, gate = _window(proj, _C_KPE, IN_PAD - _C_KPE), _window(proj, 3 * HG_WIDTH, HG_WIDTH)
    qfull, kfull, vfull = _rows(_mix_b, [qraw, kvraw, kpe_raw, cos, sin], pb,
                                [(MLA_HEADS * MLA_QK_PAD, bf16)] * 2 + [(MLA_HEADS * MLA_V, bf16)],
                                name=f"mix_b_{tag}")
    o_mla, lse = _att_fwd(qfull, kfull, vfull, name=f"att_fwd_{tag}")
    mixin, = _rows(_mix_c, [o_hg, gate, o_mla], [w["hg_out_norm"]], [(d, bf16)], name=f"mix_c_{tag}")
    if callable(w["mix_w_out"]):
        w["mix_w_out"] = w["mix_w_out"](mixin)
    out, normed = _mm(mixin, w["mix_w_out"], add=h, norm=next_gain, name=f"mix_out_{tag}")
    return out, (h, hn, proj, qf, kk, vv, logf, cqn, ckvn, o_hg, states, qraw, kvraw, qfull, kfull, vfull, o_mla,
                 lse, mixin), normed


def _mixer_bwd(dout, cos, sin, w, saved, tag, on_w_out=None):
    (h, hn, proj, qf, kk, vv, logf, cqn, ckvn, o_hg, states, qraw, kvraw, qfull, kfull, vfull, o_mla, lse,
     mixin) = saved
    g = {}
    dmixin = _mm(dout, w["mix_w_out"], tb=True, out_dtype=bf16, name=f"mix_dmixin_{tag}")
    g["mix_w_out"] = _mm(mixin, dout, ta=True, out_dtype=bf16, name=f"mix_dwout_{tag}")
    if on_w_out is not None:
        dmixin = on_w_out(g["mix_w_out"], dmixin)
    kpe_raw, gate = _window(proj, _C_KPE, IN_PAD - _C_KPE), _window(proj, 3 * HG_WIDTH, HG_WIDTH)
    (do_hg, dgate, do_mla), (g["hg_out_norm"],) = _rows_bwd(
        _mix_c, [o_hg, gate, o_mla], [w["hg_out_norm"]], [dmixin], rgrad=[f32, f32, bf16], pgrad=[True],
        name=f"mix_dc_{tag}")
    dqfull, dkfull, dvfull = _att_bwd(qfull, kfull, vfull, o_mla, lse, do_mla, name=f"att_bwd_{tag}")
    pb = [w["mla_qn_nope"], w["mla_qn_rope"], w["mla_kn_nope"], w["mla_kn_rope"], w["rope_perm"]]
    (dqraw, dkvraw, dkpe_raw), pg = _rows_bwd(
        _mix_b, [qraw, kvraw, kpe_raw, cos, sin], pb, [dqfull, dkfull, dvfull],
        rgrad=[bf16, bf16, f32, None, None], pgrad=[True, True, True, True, False],
        name=f"mix_db_{tag}")
    g["mla_qn_nope"], g["mla_qn_rope"], g["mla_kn_nope"], g["mla_kn_rope"] = pg
    dcqn = _mm(dqraw, w["mla_w_uq"], tb=True, name=f"mla_dcq_{tag}")
    g["mla_w_uq"] = _mm(cqn, dqraw, ta=True, name=f"mla_dwuq_{tag}")
    dckvn = _mm(dkvraw, w["mla_w_ukv"], tb=True, name=f"mla_dckv_{tag}")
    g["mla_w_ukv"] = _mm(ckvn, dkvraw, ta=True, name=f"mla_dwukv_{tag}")
    dqf, dkk, dvv, dlogf = _hg_bwd(qf, kk, vv, logf, states, do_hg, name=f"hg_bwd_{tag}")
    pa = [w["lb0"], w["lb1"], w["mla_q_a_norm"], w["mla_kv_a_norm"]]
    (dproj,), (g["lb0"], g["lb1"], g["mla_q_a_norm"], g["mla_kv_a_norm"]) = _rows_bwd(
        _mix_a, [proj], pa, [dqf, dkk, dvv, dlogf, dcqn, dckvn], rgrad=[bf16], pgrad=[True] * 4,
        addends={0: [(dgate, 3 * HG_WIDTH), (dkpe_raw, _C_KPE)]}, name=f"mix_da_{tag}")
    dh, g["norm_mix"] = _mm(dproj, w["mix_w_in"], tb=True, rms_bwd=(h, w["norm_mix"], dout), name=f"mix_dhn_{tag}")
    g["mix_w_in_t"] = _mm(dproj, hn, ta=True, name=f"mix_dwin_{tag}")
    return dh, g


def _rope_tables(positions):
    inv_freq = 1.0 / (ROPE_BASE ** (jnp.arange(0, MLA_ROPE, 2, dtype=f32) / MLA_ROPE))
    ang = positions.astype(f32)[:, None] * inv_freq
    z = jnp.zeros((positions.shape[0], 128 - MLA_ROPE), f32)
    return (jnp.concatenate([jnp.cos(ang), jnp.cos(ang), z], axis=1),
            jnp.concatenate([jnp.sin(ang), jnp.sin(ang), z], axis=1))


def _pad_cols(a, n):
    return jnp.pad(a, ((0, 0), (0, n - a.shape[1])))


def _even_weights(p, j, layer, dt):
    w_uq = p["mla_w_uq"][j].reshape(MLA_Q_RANK, MLA_HEADS, MLA_QK)
    w_uq = jnp.pad(w_uq, ((0, 0), (0, 0), (0, MLA_QK_PAD - MLA_QK))).reshape(MLA_Q_RANK, MLA_HEADS * MLA_QK_PAD)
    return dict(
        norm_mix=p["norm_mix"][layer][None], mix_w_in=_pad_cols(p["mix_w_in"][j], IN_PAD).astype(dt),
        lb0=p["hg_lb_logits"][0][None], lb1=p["hg_lb_logits"][1][None],
        mla_q_a_norm=p["mla_q_a_norm"][j][None], mla_kv_a_norm=p["mla_kv_a_norm"][j][None],
        mla_w_uq=w_uq.astype(dt), mla_w_ukv=p["mla_w_ukv"][j].astype(dt),
        mla_qn_nope=p["mla_qn_nope"][j][None], mla_qn_rope=_pad_cols(p["mla_qn_rope"][j][None], 128),
        mla_kn_nope=p["mla_kn_nope"][j][None], mla_kn_rope=_pad_cols(p["mla_kn_rope"][j][None], 128),
        rope_perm=_rope_perm(), hg_out_norm=p["hg_out_norm"][j][None],
        mix_w_out=p["mix_w_out"][j].astype(dt) if "mix_w_out" in p else None)


def _even_grads(g):
    w_uq = g["mla_w_uq"].reshape(MLA_Q_RANK, MLA_HEADS, MLA_QK_PAD)[:, :, :MLA_QK].reshape(MLA_Q_RANK, -1)
    return dict(
        norm_mix=g["norm_mix"], mix_w_in=g["mix_w_in_t"][:IN_WIDTH].T[None],
        hg_lb_logits=jnp.concatenate([g["lb0"], g["lb1"]], axis=0),
        mla_q_a_norm=g["mla_q_a_norm"], mla_kv_a_norm=g["mla_kv_a_norm"], mla_w_uq=w_uq[None],
        mla_w_ukv=g["mla_w_ukv"][None], mla_qn_nope=g["mla_qn_nope"], mla_qn_rope=g["mla_qn_rope"][:, :MLA_ROPE],
        mla_kn_nope=g["mla_kn_nope"], mla_kn_rope=g["mla_kn_rope"][:, :MLA_ROPE],
        hg_out_norm=g["hg_out_norm"], mix_w_out=g["mix_w_out"][None])


_S5_NB = 8
_S5_BW = 1024
_S5_HALF = 512
_S5_UC = 128
_S5_TIME = 1024


def _cmul(ar, ai, br, bi):
    return ar * br - ai * bi, ar * bi + ai * br


def _pow_table(ar, ai, descending):
    rows = lax.broadcasted_iota(jnp.int32, (8, ar.shape[1]), 0)
    tr = jnp.zeros((8, ar.shape[1]), f32)
    ti = jnp.zeros((8, ar.shape[1]), f32)
    pr, pi_ = ar, ai
    for r in range(8):
        sel = rows == ((7 - r) if descending else r)
        tr = jnp.where(sel, pr, tr)
        ti = jnp.where(sel, pi_, ti)
        pr, pi_ = _cmul(pr, pi_, ar, ai)
    return tr, ti


def _s5_tile_scan(work, carry, ar, ai, tc, reverse, per_tile=None):
    hw = _S5_HALF
    row8 = lax.broadcasted_iota(jnp.int32, (8, hw), 0)
    powers = [(ar, ai)]
    for _ in range(2):
        powers.append(_cmul(*powers[-1], *powers[-1]))
    steps = []
    for (mr, mi), s in zip(powers, (1, 2, 4)):
        ok = (row8 < 8 - s) if reverse else (row8 >= s)
        steps.append((jnp.where(ok, mr, 0.0), jnp.where(ok, mi, 0.0), 8 - s if reverse else s))
    tr, ti = _pow_table(ar, ai, reverse)
    cr, ci = carry[:, :hw], carry[:, hw:]
    tiles = range(tc // 8)
    for i in (reversed(tiles) if reverse else tiles):
        sl = slice(8 * i, 8 * i + 8)
        xr, xi = work[sl, :hw], work[sl, hw:]
        for mr, mi, shift in steps:
            pr, pi_ = _cmul(mr, mi, pltpu.roll(xr, shift, axis=0), pltpu.roll(xi, shift, axis=0))
            xr, xi = xr + pr, xi + pi_
        pr, pi_ = _cmul(tr, ti, cr, ci)
        xr, xi = xr + pr, xi + pi_
        work[sl, :hw] = xr
        work[sl, hw:] = xi
        if per_tile is not None:
            per_tile(sl, xr, xi, cr, ci)
        edge = 8 * i if reverse else 8 * i + 7
        cr, ci = work[edge:edge + 1, :hw], work[edge:edge + 1, hw:]
    carry[:, :hw] = cr
    carry[:, hw:] = ci


def _s5_core_fwd(a, hn, b3, c3, *, name):
    length = hn.shape[0]
    tc = min(_S5_TIME, length)

    def body(a_ref, hn_ref, b_ref, c_ref, hs_ref, y_ref, work, carry):
        @pl.when(pl.program_id(1) == 0)
        def _():
            carry[...] = jnp.zeros_like(carry)

        work[...] = jnp.dot(hn_ref[...].astype(bf16), b_ref[...], preferred_element_type=f32)
        _s5_tile_scan(work, carry, a_ref[:, :_S5_HALF], a_ref[:, _S5_HALF:], tc, False)
        hs = work[...].astype(bf16)
        hs_ref[...] = hs
        y_ref[...] = jnp.dot(hs, c_ref[...], preferred_element_type=f32)

    return pl.pallas_call(
        body, grid=(_S5_NB, length // tc),
        in_specs=[pl.BlockSpec((1, _S5_BW), lambda j, t: (0, j)), pl.BlockSpec((tc, _S5_UC), lambda j, t: (t, j)),
                  pl.BlockSpec((_S5_UC, _S5_BW), lambda j, t: (j, 0)), pl.BlockSpec((_S5_BW, _S5_UC), lambda j, t: (j, 0))],
        out_specs=[pl.BlockSpec((tc, _S5_BW), lambda j, t: (t, j)), pl.BlockSpec((tc, _S5_UC), lambda j, t: (t, j))],
        out_shape=[jax.ShapeDtypeStruct((length, _S5_NB * _S5_BW), bf16),
                   jax.ShapeDtypeStruct((length, _S5_NB * _S5_UC), f32)],
        scratch_shapes=[pltpu.VMEM((tc, _S5_BW), f32), pltpu.VMEM((1, _S5_BW), f32)],
        compiler_params=_cparams(("parallel", "arbitrary")), name=name)(a, hn, b3, c3)


def _s5_core_bwd(a, dy, c3, hs, hn, b3, *, name):
    length = hn.shape[0]
    tc = min(_S5_TIME, length)
    nt = length // tc
    hw = _S5_HALF

    def body(a_ref, dy_ref, c_ref, hs_ref, hn_ref, b_ref, du_ref, db_ref, dc_ref, da_ref, work, carry, acc):
        @pl.when(pl.program_id(1) == 0)
        def _():
            carry[...] = jnp.zeros_like(carry)
            db_ref[...] = jnp.zeros_like(db_ref)
            dc_ref[...] = jnp.zeros_like(dc_ref)
            da_ref[...] = jnp.zeros_like(da_ref)

        dyb = dy_ref[...].astype(bf16)
        work[...] = lax.dot_general(dyb, c_ref[...], _NT, preferred_element_type=f32)
        acc[...] = jnp.zeros_like(acc)
        row8 = lax.broadcasted_iota(jnp.int32, (8, hw), 0)

        def grad_a(sl, gr, gi, cr, ci):
            gnr = jnp.where(row8 == 7, cr, pltpu.roll(gr, 7, axis=0))
            gni = jnp.where(row8 == 7, ci, pltpu.roll(gi, 7, axis=0))
            hr, hi = hs_ref[sl, :hw].astype(f32), hs_ref[sl, hw:].astype(f32)
            acc[:, :hw] += hr * gnr + hi * gni
            acc[:, hw:] += hr * gni - hi * gnr

        _s5_tile_scan(work, carry, a_ref[:, :hw], -a_ref[:, hw:], tc, True, grad_a)
        da_ref[...] += jnp.sum(acc[...], axis=0, keepdims=True)
        g = work[...].astype(bf16)
        du_ref[...] = lax.dot_general(g, b_ref[...], _NT, preferred_element_type=f32)
        db_ref[...] += lax.dot_general(hn_ref[...].astype(bf16), g, _TN, preferred_element_type=f32)
        dc_ref[...] += lax.dot_general(hs_ref[...], dyb, _TN, preferred_element_type=f32)

    rev = lambda j, t: (nt - 1 - t, j)
    return pl.pallas_call(
        body, grid=(_S5_NB, nt),
        in_specs=[pl.BlockSpec((1, _S5_BW), lambda j, t: (0, j)), pl.BlockSpec((tc, _S5_UC), rev),
                  pl.BlockSpec((_S5_BW, _S5_UC), lambda j, t: (j, 0)), pl.BlockSpec((tc, _S5_BW), rev),
                  pl.BlockSpec((tc, _S5_UC), rev), pl.BlockSpec((_S5_UC, _S5_BW), lambda j, t: (j, 0))],
        out_specs=[pl.BlockSpec((tc, _S5_UC), rev), pl.BlockSpec((_S5_UC, _S5_BW), lambda j, t: (j, 0)),
                   pl.BlockSpec((_S5_BW, _S5_UC), lambda j, t: (j, 0)), pl.BlockSpec((1, _S5_BW), lambda j, t: (0, j))],
        out_shape=[jax.ShapeDtypeStruct((length, _S5_NB * _S5_UC), f32),
                   jax.ShapeDtypeStruct((_S5_NB * _S5_UC, _S5_BW), f32),
                   jax.ShapeDtypeStruct((_S5_NB * _S5_BW, _S5_UC), f32),
                   jax.ShapeDtypeStruct((1, _S5_NB * _S5_BW), f32)],
        scratch_shapes=[pltpu.VMEM((tc, _S5_BW), f32), pltpu.VMEM((1, _S5_BW), f32), pltpu.VMEM((8, _S5_BW), f32)],
        compiler_params=_cparams(("parallel", "arbitrary")), name=name)(a, dy, c3, hs, hn, b3)


def _s5_disc(lr, li, ldt, btr, bti, expand):
    dt = jnp.exp(ldt)
    mag = jnp.exp(lr * dt)
    abr = mag * jnp.cos(li * dt)
    abi = mag * jnp.sin(li * dt)
    den = lr * lr + li * li
    zr = ((abr - 1.0) * lr + abi * li) / den
    zi = (abi * lr - (abr - 1.0) * li) / den
    zr = jnp.dot(zr, expand, precision=lax.Precision.HIGHEST, preferred_element_type=f32)
    zi = jnp.dot(zi, expand, precision=lax.Precision.HIGHEST, preferred_element_type=f32)
    return abr, abi, zr * btr - zi * bti, zr * bti + zi * btr


def _s5_disc_fwd(args, *, name):
    def body(*refs):
        res = _s5_disc(*[r[...] for r in refs[:6]])
        for o, v in zip(refs[6:], res):
            o[...] = v

    sds = jax.ShapeDtypeStruct
    return pl.pallas_call(body, out_shape=[sds(args[0].shape, f32)] * 2 + [sds(args[3].shape, f32)] * 2,
                          name=name)(*args)


def _s5_disc_bwd(args, cts, *, name):
    def body(*refs):
        vals = [r[...] for r in refs[:6]]
        _, vjp = jax.vjp(lambda *d: _s5_disc(*d, vals[5]), *vals[:5])
        grads = vjp(tuple(r[...] for r in refs[6:10]))
        for o, v in zip(refs[10:], grads):
            o[...] = v

    return pl.pallas_call(body, out_shape=[jax.ShapeDtypeStruct(a.shape, f32) for a in args[:5]],
                          name=name)(*args, *cts)


def _gelu_tanh(x):
    return 0.5 * x * (1.0 + jnp.tanh(0.7978845608028654 * (x + 0.044715 * (x * x * x))))


def _s5_post(y, u, d_skip):
    return _gelu_tanh(y + d_skip * u)


def _s5_glu(ga, gb, h):
    return h + ga * jax.nn.sigmoid(gb)


def _s5_glu_norm(ga, gb, h, next_gain):
    out = _s5_glu(ga, gb, h)
    return out, _rms(out, next_gain)


def _s5_expand():
    e = np.zeros((S5_STATE, S5_GROUP * S5_STATE), np.float32)
    for m in range(S5_GROUP):
        e[np.arange(S5_STATE), m * S5_STATE + np.arange(S5_STATE)] = 1.0
    return jnp.asarray(e)


def _s5_pack_b(bbr, bbi):
    eye = jnp.eye(8, dtype=f32)

    def one(bb):
        b5 = bb.reshape(_S5_NB, 8, S5_GROUP, S5_STATE)
        return jnp.einsum("jgmp,gh->jgmhp", b5, eye).reshape(_S5_NB * _S5_UC, _S5_HALF)

    return jnp.concatenate([one(bbr), one(bbi)], axis=1)


def _s5_unpack_b(db3):
    def one(d):
        d5 = d.reshape(_S5_NB, 8, S5_GROUP, 8, S5_STATE)
        return jnp.einsum("jgmgp->jgmp", d5).reshape(S5_GROUPS, S5_GROUP * S5_STATE)

    return one(db3[:, :_S5_HALF]), one(db3[:, _S5_HALF:])


def _s5_pack_c(c_re, c_im):
    eye = jnp.eye(8, dtype=f32)

    def one(c):
        c4 = c.reshape(_S5_NB, 8, S5_GROUP, S5_STATE)
        return jnp.einsum("jgmp,hg->jhpgm", c4, eye).reshape(_S5_NB, _S5_HALF, _S5_UC)

    return jnp.concatenate([one(c_re), -one(c_im)], axis=1).reshape(_S5_NB * _S5_BW, _S5_UC)


def _s5_unpack_c(dc3):
    d = dc3.reshape(_S5_NB, 2, 8, S5_STATE, 8, S5_GROUP)
    dre = jnp.einsum("jgpgm->jgmp", d[:, 0]).reshape(S5_GROUPS, S5_GROUP, S5_STATE)
    dim = -jnp.einsum("jgpgm->jgmp", d[:, 1]).reshape(S5_GROUPS, S5_GROUP, S5_STATE)
    return dre, dim


def _s5_state_row(re, im):
    r = re.reshape(_S5_NB, 1, _S5_HALF)
    i = im.reshape(_S5_NB, 1, _S5_HALF)
    return jnp.concatenate([r, i], axis=2).reshape(1, _S5_NB * _S5_BW)


def _s5_unstate_row(row):
    r = row.reshape(_S5_NB, 2, 8, S5_STATE)
    return r[:, 0].reshape(S5_GROUPS, S5_STATE), r[:, 1].reshape(S5_GROUPS, S5_STATE)


def _s5_fwd(h, w, tag, next_gain):
    d = h.shape[1]
    hn, = _rows(_rms, [h], [w["norm_mix"]], [(d, f32)], name=f"s5_norm_{tag}")
    disc_in = [w["s5_lam_re"], w["s5_lam_im"], w["s5_log_dt"], w["s5_bt_re"], w["s5_bt_im"], w["s5_expand"]]
    abr, abi, bbr, bbi = _s5_disc_fwd(disc_in, name=f"s5_disc_{tag}")
    a_row = _s5_state_row(abr, abi)
    b3 = _s5_pack_b(bbr, bbi).astype(bf16)
    hs, y = _s5_core_fwd(a_row, hn, b3, w["s5_c3"], name=f"s5_core_{tag}")
    yg, = _rows(_s5_post, [y, hn], [w["s5_d"]], [(d, bf16)], name=f"s5_post_{tag}")
    ga = _mm(yg, w["s5_w_glu_a"], out_dtype=bf16, name=f"s5_glu_a_{tag}")
    gb = _mm(yg, w["s5_w_glu_b"], out_dtype=bf16, name=f"s5_glu_b_{tag}")
    out, normed = _rows(_s5_glu_norm, [ga, gb, h], [next_gain], [(d, f32), (d, bf16)], name=f"s5_glu_{tag}")
    return out, (h, hn, disc_in, a_row, b3, hs, y, yg, ga, gb), normed


def _s5_bwd(dout, w, saved, tag):
    h, hn, disc_in, a_row, b3, hs, y, yg, ga, gb = saved
    g = {}
    (dga, dgb), _ = _rows_bwd(_s5_glu, [ga, gb, h], [], [dout], rgrad=[bf16, bf16, None], pgrad=[],
                              name=f"s5_dglu_{tag}")
    dyg = _mm(dga, w["s5_w_glu_a"], tb=True, name=f"s5_dyg_a_{tag}")
    dyg = _mm(dgb, w["s5_w_glu_b"], tb=True, add=dyg, out_dtype=bf16, name=f"s5_dyg_b_{tag}")
    g["s5_w_glu_a"] = _mm(yg, dga, ta=True, out_dtype=bf16, name=f"s5_dwa_{tag}")
    g["s5_w_glu_b"] = _mm(yg, dgb, ta=True, out_dtype=bf16, name=f"s5_dwb_{tag}")
    (dy, du_skip), (g["s5_d"],) = _rows_bwd(_s5_post, [y, hn], [w["s5_d"]], [dyg], rgrad=[bf16, f32], pgrad=[True],
                                           name=f"s5_dpost_{tag}")
    du, db3, dc3, da_row = _s5_core_bwd(a_row, dy, w["s5_c3"], hs, hn, b3, name=f"s5_dcore_{tag}")
    dabr, dabi = _s5_unstate_row(da_row)
    dbbr, dbbi = _s5_unpack_b(db3)
    g["s5_lam_re"], g["s5_lam_im"], g["s5_log_dt"], g["s5_bt_re"], g["s5_bt_im"] = _s5_disc_bwd(
        disc_in, [dabr, dabi, dbbr, dbbi], name=f"s5_ddisc_{tag}")
    g["s5_c_re"], g["s5_c_im"] = _s5_unpack_c(dc3)
    (dh,), (g["norm_mix"],) = _rows_bwd(_rms_twice, [h], [w["norm_mix"]], [du, du_skip], rgrad=[f32], pgrad=[True],
                                        addends={0: dout}, name=f"s5_dnorm_{tag}")
    return dh, g


def _odd_weights(p, j, layer, dt):
    tr = lambda b: b.transpose(0, 2, 1).reshape(S5_GROUPS, S5_GROUP * S5_STATE)
    return dict(
        norm_mix=p["norm_mix"][layer][None], s5_lam_re=p["s5_lam_re"][j], s5_lam_im=p["s5_lam_im"][j],
        s5_log_dt=p["s5_log_dt"][j][:, None], s5_bt_re=tr(p["s5_b_re"][j]), s5_bt_im=tr(p["s5_b_im"][j]),
        s5_expand=_s5_expand(), s5_c3=_s5_pack_c(p["s5_c_re"][j], p["s5_c_im"][j]).astype(dt),
        **{n: (p[n][j][None] if n == "s5_d" else p[n][j].astype(dt))
           for n in ("s5_d", "s5_w_glu_a", "s5_w_glu_b") if n in p})


def _odd_grads(g):
    tr = lambda b: b.reshape(S5_GROUPS, S5_GROUP, S5_STATE).transpose(0, 2, 1)[None]
    return dict(
        norm_mix=g["norm_mix"], s5_lam_re=g["s5_lam_re"][None], s5_lam_im=g["s5_lam_im"][None],
        s5_log_dt=g["s5_log_dt"][:, 0][None], s5_b_re=tr(g["s5_bt_re"]), s5_b_im=tr(g["s5_bt_im"]),
        s5_c_re=g["s5_c_re"][None], s5_c_im=g["s5_c_im"][None], s5_d=g["s5_d"],
        s5_w_glu_a=g["s5_w_glu_a"][None], s5_w_glu_b=g["s5_w_glu_b"][None])


FF_SHARD = 352
FF_SHARD_PAD = 384


def _pad_groups(a, axis):
    axis %= a.ndim
    zeros = jnp.zeros(a.shape[:axis] + (FF_SHARD_PAD - FF_SHARD,) + a.shape[axis + 1:], a.dtype)
    pieces = []
    for g in range(a.shape[axis] // FF_SHARD):
        pieces += [lax.slice_in_dim(a, g * FF_SHARD, (g + 1) * FF_SHARD, axis=axis), zeros]
    return jnp.concatenate(pieces, axis=axis)


def _unpad_groups(a, axis):
    axis %= a.ndim
    pieces = [lax.slice_in_dim(a, g * FF_SHARD_PAD, g * FF_SHARD_PAD + FF_SHARD, axis=axis)
              for g in range(a.shape[axis] // FF_SHARD_PAD)]
    return pieces[0] if len(pieces) == 1 else jnp.concatenate(pieces, axis=axis)


_LANES = 1024
_ROW_PAD = 16


_PEER_MASKS = (1, 2, 4, 3, 5, 6, 7)


def _mesh_place():
    x, y, c = lax.axis_index("x"), lax.axis_index("y"), lax.axis_index("c")

    def peer(mask):
        px = 1 - x if mask & 4 else x
        py = 1 - y if mask & 2 else y
        pc = 1 - c if mask & 1 else c
        return (px, py, pc), 4 * px + 2 * py + pc

    return 4 * x + 2 * y + c, peer


class _Exchange:
    def __init__(self, name):
        self.name = name
        self.srcs, self.shapes, self.items, self.where = [], [], [], {}

    def add(self, src, land_shape, src_at, dst_at, key):
        si = next((i for i, s in enumerate(self.srcs) if s is src), None)
        if si is None:
            self.srcs.append(src)
            si = len(self.srcs) - 1
        if key not in self.where:
            self.shapes.append(land_shape)
            self.where[key] = len(self.shapes) - 1
        self.items.append(dict(src=si, dst=self.where[key], src_at=src_at, dst_at=dst_at))

    def _copy(self, k, mask, ins, lands, send_sems, recv_sems, me, peer, arriving):
        it = self.items[k]
        dev, idx = peer(mask)
        s = k * (N_DEV - 1) + _PEER_MASKS.index(mask)
        return pltpu.make_async_remote_copy(
            src_ref=it["src_at"](ins[it["src"]], idx), dst_ref=it["dst_at"](lands[it["dst"]], idx if arriving else me),
            send_sem=send_sems.at[s], recv_sem=recv_sems.at[s], device_id=dev, device_id_type=pl.DeviceIdType.MESH)

    def _own_copy(self, k, ins, lands, own_sems, me):
        it = self.items[k]
        return pltpu.make_async_copy(it["src_at"](ins[it["src"]], me), it["dst_at"](lands[it["dst"]], me), own_sems.at[k])

    def begin(self, own):
        ns, nd, ni = len(self.srcs), len(self.shapes), len(self.items)
        nsem = ni * (N_DEV - 1)
        self.own = own

        nq = 3 if own else 2

        def body(*refs):
            ins, land_refs = refs[:ns], refs[ns:ns + nd]
            sems, token = refs[ns + nd:ns + nd + nq], refs[-1]
            me, peer = _mesh_place()
            for mask in _PEER_MASKS:
                for k in range(ni):
                    self._copy(k, mask, ins, land_refs, sems[0], sems[1], me, peer, False).start()
            if own:
                for k in range(ni):
                    self._own_copy(k, ins, land_refs, sems[2], me).start()
            token[...] = jnp.zeros_like(token)

        hbm = pl.BlockSpec(memory_space=pltpu.HBM)
        sem = pl.BlockSpec(memory_space=pltpu.SEMAPHORE)
        lands = [lax.empty(s.shape, s.dtype) for s in self.shapes]
        sem_shapes = [pltpu.SemaphoreType.DMA((nsem,)), pltpu.SemaphoreType.DMA((nsem,)), pltpu.SemaphoreType.DMA((ni,))]
        res = pl.pallas_call(
            body, in_specs=[hbm] * (ns + nd),
            out_specs=[sem] * nq + [hbm] * nd + [pl.BlockSpec(memory_space=pltpu.VMEM)],
            out_shape=sem_shapes[:nq] + [pltpu.HBM(s.shape, s.dtype) for s in self.shapes]
            + [jax.ShapeDtypeStruct((8, 128), f32)],
            input_output_aliases={ns + j: nq + j for j in range(nd)},
            compiler_params=pltpu.CompilerParams(has_side_effects=pltpu.SideEffectType.DATAFLOW_SIDE_EFFECTING),
            name=self.name + "_start")(*self.srcs, *lands)
        self.token = res[-1]
        return list(res[:nq]), list(res[nq:-1])

    def finish(self, state, after):
        sems, lands = state
        nq = len(sems)
        after = list(after) if isinstance(after, (list, tuple)) else [after]
        ns, nd, ni = len(self.srcs), len(self.shapes), len(self.items)

        def body(*refs):
            ins, land_refs = refs[:ns], refs[ns:ns + nd]
            sem_refs = refs[ns + nd:ns + nd + nq]
            me, peer = _mesh_place()
            for mask in _PEER_MASKS:
                for k in range(ni):
                    cp = self._copy(k, mask, ins, land_refs, sem_refs[0], sem_refs[1], me, peer, True)
                    cp.wait_send()
                    cp.wait_recv()
            if self.own:
                for k in range(ni):
                    self._own_copy(k, ins, land_refs, sem_refs[2], me).wait()

        hbm = pl.BlockSpec(memory_space=pltpu.HBM)
        sem = pl.BlockSpec(memory_space=pltpu.SEMAPHORE)
        res = pl.pallas_call(
            body, in_specs=[hbm] * (ns + nd) + [sem] * nq + [pl.BlockSpec(memory_space=pl.ANY)] * len(after),
            out_specs=[hbm] * nd, out_shape=[pltpu.HBM(s.shape, s.dtype) for s in self.shapes],
            input_output_aliases={ns + j: j for j in range(nd)},
            compiler_params=pltpu.CompilerParams(has_side_effects=pltpu.SideEffectType.DATAFLOW_SIDE_EFFECTING),
            name=self.name + "_wait")(*self.srcs, *lands, *sems, *after)
        return {k: res[i] for k, i in self.where.items()}


def _after(x, *tokens, name):
    def body(*refs):
        del refs

    anyspace = pl.BlockSpec(memory_space=pl.ANY)
    return pl.pallas_call(body, in_specs=[anyspace] * (1 + len(tokens)), out_specs=anyspace,
                          out_shape=jax.ShapeDtypeStruct(x.shape, x.dtype), input_output_aliases={0: 0},
                          name=name)(x, *tokens)


def _rows_of(n):
    return lambda r, i: r.at[pl.ds(pl.multiple_of(i * n, n), n), :]


def _cols_of(n):
    return lambda r, i: r.at[:, pl.ds(pl.multiple_of(i * n, n), n)]


def _whole(r, i):
    return r


def _slot(r, i):
    return r.at[i]


def _at_layer(layer):
    return lambda r, i: r.at[layer]


def _sum_adam(me_index, slots, owns, own_block, w, m, v, *, name):
    layers, rows, cols = w.shape
    tr = _pick(rows, (256, 128, 104, 64, 32, 16, 8))
    bc1 = 1.0 - ADAM_B1 ** ADAM_STEP
    bc2 = 1.0 - ADAM_B2 ** ADAM_STEP
    own_shape, own_map = own_block(tr)
    nl = len(slots)
    assert nl == layers and len(owns) == layers

    def body(me_ref, *refs):
        s_refs, own_refs = refs[:nl], refs[nl:2 * nl]
        w_ref, m_ref, v_ref, g_ref, d_ref, nm_ref, nv_ref = refs[2 * nl:]
        me = me_ref[0]

        def run(s_ref, own_ref):
            mine = (own_ref[0] if len(own_shape) == 3 else own_ref[...]).astype(f32)
            g = jnp.where(me == 0, mine, s_ref[0].astype(f32))
            for k in range(1, N_DEV):
                g = g + jnp.where(me == k, mine, s_ref[k].astype(f32))
            mm = ADAM_B1 * m_ref[0] + (1.0 - ADAM_B1) * g
            vv = ADAM_B2 * v_ref[0] + (1.0 - ADAM_B2) * (g * g)
            g_ref[0] = g
            nm_ref[0] = mm
            nv_ref[0] = vv
            d_ref[0] = -ADAM_LR * ((mm / bc1) / (jnp.sqrt(vv / bc2) + ADAM_EPS) + ADAM_WD * w_ref[0])

        for layer in range(nl):
            pl.when(pl.program_id(0) == layer)(functools.partial(run, s_refs[layer], own_refs[layer]))

    def of_layer(layer, index_map):
        return lambda lyr, i, me: index_map(jnp.where(lyr == layer, i, 0), me)

    blk = pl.BlockSpec((1, tr, cols), lambda lyr, i, me: (lyr, i, 0))
    sds = jax.ShapeDtypeStruct((layers, rows, cols), f32)
    grid_spec = pltpu.PrefetchScalarGridSpec(
        num_scalar_prefetch=1, grid=(layers, rows // tr),
        in_specs=[pl.BlockSpec((N_DEV, tr, cols), of_layer(layer, lambda i, me: (0, i, 0))) for layer in range(nl)]
        + [pl.BlockSpec(own_shape, of_layer(layer, own_map)) for layer in range(nl)] + [blk, blk, blk],
        out_specs=[blk] * 4)
    return pl.pallas_call(body, grid_spec=grid_spec, out_shape=[sds] * 4,
                          compiler_params=_cparams(("arbitrary", "arbitrary")),
                          name=name)(me_index, *slots, *owns, w, m, v)


_SHARDED = dict(xa_wq=1, xa_wk=1, xa_wv=1, xa_wo=1, ffn_w_up=2, ffn_conv_w=2, ffn_w_down=1, mix_w_in=2, mla_w_uq=2,
                mla_w_ukv=2, mix_w_out=1, s5_d=1, s5_w_glu_a=1, s5_w_glu_b=1)
_EXACT = ("ffn_conv_w", "s5_d")
_WEIGHTS = ("norm_mix", "norm_xa", "norm_mem", "norm_ffn", "xa_wq", "xa_wk", "xa_wv", "xa_wo", "xa_q_norm",
            "xa_k_norm", "ffn_w_up", "ffn_conv_w", "ffn_conv_b", "ffn_w_down", "hg_lb_logits", "mix_w_in",
            "hg_out_norm", "mla_q_a_norm", "mla_w_uq", "mla_kv_a_norm", "mla_w_ukv", "mla_qn_nope", "mla_qn_rope",
            "mla_kn_nope", "mla_kn_rope", "mix_w_out", "s5_lam_re", "s5_lam_im", "s5_log_dt", "s5_b_re", "s5_b_im",
            "s5_c_re", "s5_c_im", "s5_d", "s5_w_glu_a", "s5_w_glu_b")
_BIG = tuple(n for n in _WEIGHTS if n in _SHARDED and n not in _EXACT)
_SHARD_ORDER = tuple(n for n in _WEIGHTS if n in _SHARDED)
_REPL_ORDER = tuple(n for n in _WEIGHTS if n not in _SHARDED)
_REPL_EARLY = tuple(n for n in _REPL_ORDER if n.startswith("s5_"))
_REPL_LATE = tuple(n for n in _REPL_ORDER if n not in _REPL_EARLY)


def _pack(parts, dtype, lead=None):
    nl = 0 if lead is None else 1
    flat = [a.astype(dtype).reshape(a.shape[:nl] + (-1,)) for a in parts]
    cat = jnp.concatenate(flat, axis=nl)
    n = cat.shape[nl]
    unit = _LANES * _ROW_PAD
    total = -(-n // unit) * unit
    cat = jnp.pad(cat, [(0, 0)] * nl + [(0, total - n)])
    return cat.reshape(cat.shape[:nl] + (total // _LANES, _LANES))


def _unpack(packed, shapes, lead=None):
    nl = 0 if lead is None else 1
    flat = packed.reshape(packed.shape[:nl] + (-1,))
    out, off = [], 0
    for s in shapes:
        n = int(np.prod(s))
        piece = flat[..., off:off + n] if nl else flat[off:off + n]
        out.append(piece.reshape(packed.shape[:nl] + tuple(s)))
        off += n
    return out


def _to_full(gathered, axis):
    g = jnp.moveaxis(gathered, 0, axis)
    s = g.shape
    return g.reshape(s[:axis] + (s[axis] * s[axis + 1],) + s[axis + 2:])


def _to_shards(full, axis):
    s = full.shape
    g = full.reshape(s[:axis] + (N_DEV, s[axis] // N_DEV) + s[axis + 1:])
    return jnp.moveaxis(g, axis, 0)


_DIRECT_ROWS = ("xa_wq", "xa_wk", "xa_wv", "xa_wo", "mix_w_out", "s5_w_glu_a", "s5_w_glu_b")
_SMALL16 = ("mix_w_in", "mla_w_uq", "mla_w_ukv")
_SMALL_SHARDED = ("mla_w_uq", "mla_w_ukv") + _EXACT
_SHARD_ROWS = 128


def _exchange_layout(d):
    out = dict(d)
    out["ffn_w_up"] = _pad_groups(d["ffn_w_up"], 2)
    out["ffn_conv_w"] = _pad_groups(d["ffn_conv_w"], 2)
    out["ffn_w_down"] = _pad_groups(d["ffn_w_down"], 1)
    return out


def _train_step(x, mem, positions, target, w, m, v):
    d_model = x.shape[1]
    we_, me_, ve_ = _exchange_layout(w), _exchange_layout(m), _exchange_layout(v)
    sds = jax.ShapeDtypeStruct

    matrices = _DIRECT_ROWS + ("ffn_w_up", "ffn_w_down")
    layer_mats = ("xa_wq", "xa_wk", "xa_wv", "xa_wo", "ffn_w_up", "ffn_w_down")
    shard16 = {n: we_[n].astype(bf16) for n in matrices}
    part_of = {n: _rows_of(_SHARD_ROWS) for n in _DIRECT_ROWS}
    part_of["ffn_w_up"] = _cols_of(we_["ffn_w_up"].shape[2])
    part_of["ffn_w_down"] = _rows_of(we_["ffn_w_down"].shape[1])
    part_shape = {n: we_[n].shape[1:] for n in matrices}

    def full_shape(n):
        r, c = part_shape[n]
        return (r, N_DEV * c) if n == "ffn_w_up" else (N_DEV * r, c)

    def gather(ex, n, layer):
        ex.add(shard16[n], sds(full_shape(n), bf16), _at_layer(layer), part_of[n], (n, layer))

    def scatter(ex, n, layer, grad):
        ex.add(grad, sds((N_DEV,) + part_shape[n], grad.dtype), part_of[n], _slot, (n, layer))

    small16 = _pack([we_[n] for n in _SMALL16], bf16)
    exact = _pack([we_[n] for n in _EXACT], f32)
    ga, ga1, gb, gc = _Exchange("gather_a"), _Exchange("gather_a1"), _Exchange("gather_b"), _Exchange("gather_c")
    ga.add(small16, sds((N_DEV,) + small16.shape, bf16), _whole, _slot, "small16")
    ga1.add(exact, sds((N_DEV,) + exact.shape, f32), _whole, _slot, "exact")
    gather(ga1, "mix_w_out", 0)
    for n in layer_mats:
        gather(gb, n, 0)
    gather(gc, "s5_w_glu_a", 0)
    gather(gc, "s5_w_glu_b", 0)
    for n in layer_mats:
        gather(gc, n, 1)
    state_a, state_a1, state_b, state_c = ga.begin(True), ga1.begin(True), gb.begin(True), gc.begin(True)

    p = {n: w[n] for n in _REPL_ORDER}
    cos, sin = _rope_tables(positions)
    wo = _odd_weights(p, 0, 1, bf16)
    conv_b = _pad_groups(w["ffn_conv_b"], 1)
    prepared = [cos, sin, conv_b, wo["s5_c3"], wo["s5_bt_re"], wo["s5_bt_im"]]
    full = ga.finish(state_a, [ga1.token, gb.token, gc.token] + prepared)
    for n, a in zip(_SMALL16, _unpack(full["small16"], [we_[n].shape for n in _SMALL16], lead=True)):
        p[n] = _to_full(a, _SHARDED[n])
    we = _even_weights(p, 0, 0, bf16)
    we["norm_mix"] = _after(we["norm_mix"], ga.token, ga1.token, gb.token, gc.token, name="after_gather_starts")

    def late_mix_w_out(mixin):
        full.update(ga1.finish(state_a1, [mixin]))
        return full[("mix_w_out", 0)]

    we["mix_w_out"] = late_mix_w_out
    h, s_mix0, hx = _mixer_fwd(x, cos, sin, we, "l0", w["norm_xa"][0][None])
    conv_w, s5_d = [_to_full(a, _SHARDED[n]) for n, a in
                    zip(_EXACT, _unpack(full["exact"], [we_[n].shape for n in _EXACT], lead=True))]

    def layer_weights(layer):
        return dict(norm_xa=w["norm_xa"][layer][None], norm_mem=w["norm_mem"][layer][None],
                    norm_ffn=w["norm_ffn"][layer][None], xa_q_norm=w["xa_q_norm"][layer][None],
                    xa_k_norm=w["xa_k_norm"][layer][None], ffn_conv_w=conv_w[layer],
                    ffn_conv_b=conv_b[layer][None], **{n: full[(n, layer)] for n in layer_mats})

    full.update(gb.finish(state_b, h))
    wl = [layer_weights(0)]
    h, s_xa0, hf = _xattn_fwd(h, mem, wl[0], "l0", hx)
    h, s_ff0 = _ffn_fwd(h, wl[0], "l0", hf)
    full.update(gc.finish(state_c, h))
    wl.append(layer_weights(1))
    wo.update(s5_d=s5_d, s5_w_glu_a=full[("s5_w_glu_a", 0)], s5_w_glu_b=full[("s5_w_glu_b", 0)])
    h, s_mix1, hx = _s5_fwd(h, wo, "l1", wl[1]["norm_xa"])
    h, s_xa1, hf = _xattn_fwd(h, mem, wl[1], "l1", hx)
    (dh, loss), s_ff1 = _ffn_fwd(h, wl[1], "l1", hf, loss_target=target)

    gl = [{}, {}]
    reduces = []

    own_grad = {}

    def reduce_start(name, entries, dh):
        ex = _Exchange(name)
        for n, layer, grad in entries.get("matrices", ()):
            scatter(ex, n, layer, grad)
            own_grad[(n, layer)] = grad
        for key, src, shape, src_at in entries.get("packs", ()):
            ex.add(src, shape, src_at, _slot, key)
        reduces.append((ex, ex.begin(False)))
        return _after(dh, ex.token, name="after_" + name)

    dh, gl[1] = _ffn_bwd(dh, wl[1], s_ff1, "l1")
    dh = reduce_start("reduce_ffn1", dict(matrices=[(n, 1, gl[1][n]) for n in ("ffn_w_up", "ffn_w_down")]), dh)
    dh, g = _xattn_bwd(dh, mem, wl[1], s_xa1, "l1")
    gl[1].update(g)
    dh = reduce_start("reduce_xa1", dict(matrices=[(n, 1, g[n]) for n in ("xa_wq", "xa_wk", "xa_wv", "xa_wo")]), dh)
    dh, g_odd = _s5_bwd(dh, wo, s_mix1, "l1")
    go = _odd_grads(g_odd)
    dh, gl[0] = _ffn_bwd(dh, wl[0], s_ff0, "l0")
    send_early = _pack([go[n].reshape(w[n].shape) for n in _REPL_EARLY], f32)
    dh = reduce_start("reduce_ffn0", dict(
        matrices=[(n, 0, g_odd[n]) for n in ("s5_w_glu_a", "s5_w_glu_b")]
        + [(n, 0, gl[0][n]) for n in ("ffn_w_up", "ffn_w_down")],
        packs=[("repl_early", send_early, sds((N_DEV,) + send_early.shape, f32), _whole)]), dh)
    dh, g = _xattn_bwd(dh, mem, wl[0], s_xa0, "l0")
    gl[0].update(g)
    dh = reduce_start("reduce_xa0", dict(matrices=[(n, 0, g[n]) for n in ("xa_wq", "xa_wk", "xa_wv", "xa_wo")]), dh)
    grad_x, g_even = _mixer_bwd(
        dh, cos, sin, we, s_mix0, "l0",
        on_w_out=lambda grad, dmixin: reduce_start("reduce_w_out", dict(matrices=[("mix_w_out", 0, grad)]), dmixin))

    ge = _even_grads(g_even)
    cat = lambda n: jnp.concatenate([gl[0][n], gl[1][n]], axis=0)
    rg = dict(ge)
    rg["norm_mix"] = jnp.concatenate([ge["norm_mix"], go["norm_mix"]], axis=0)
    for n in ("norm_xa", "norm_mem", "norm_ffn", "xa_q_norm", "xa_k_norm"):
        rg[n] = cat(n)
    rg["ffn_conv_b"] = _unpad_groups(cat("ffn_conv_b"), 1)
    sg = dict(mla_w_uq=ge["mla_w_uq"], mla_w_ukv=ge["mla_w_ukv"], s5_d=go["s5_d"],
              ffn_conv_w=jnp.stack([gl[0]["ffn_conv_w"], gl[1]["ffn_conv_w"]]))
    send_small = _pack([_to_shards(sg[n], _SHARDED[n]) for n in _SMALL_SHARDED], f32, lead=True)
    send_late = _pack([rg[n].reshape(w[n].shape) for n in _REPL_LATE], f32)
    w_in_rows = w["mix_w_in"].shape[2]
    last = _Exchange("reduce_last")
    last.add(g_even["mix_w_in_t"], sds((N_DEV, w_in_rows, d_model), f32), _rows_of(w_in_rows), _slot, "mix_w_in")
    last.add(send_small, sds(send_small.shape, f32), _slot, _slot, "small")
    last.add(send_late, sds((N_DEV,) + send_late.shape, f32), _whole, _slot, "repl_late")
    state_last = last.begin(False)
    slots = {}
    for ex, state in reduces:
        slots.update(ex.finish(state, [grad_x, last.token]))

    me_index = (4 * lax.axis_index("x") + 2 * lax.axis_index("y") + lax.axis_index("c")).astype(jnp.int32).reshape(1)

    def rows_block(r, c):
        return lambda tr: ((tr, c), lambda i, me: (me[0] * (r // tr) + i, 0))

    def own_block(n):
        r, c = part_shape[n]
        if n == "ffn_w_up":
            return lambda tr: ((tr, c), lambda i, me: (i, me[0]))
        return rows_block(r, c)

    out = [{}, {}, {}, {}]
    unpad = dict(ffn_w_up=2, ffn_conv_w=2, ffn_w_down=1)
    for n in matrices:
        layers = range(we_[n].shape[0])
        res = _sum_adam(me_index, [slots[(n, layer)] for layer in layers], [own_grad[(n, layer)] for layer in layers],
                        own_block(n), we_[n], me_[n], ve_[n], name=f"adam_{n}")
        for k in range(4):
            out[k][n] = _unpad_groups(res[k], unpad[n]) if n in unpad else res[k]
    pk = lambda d, order: _pack([d[n] for n in order], f32)[None]
    whole_rows = lambda tr: ((tr, _LANES), lambda i, me: (i, 0))
    res_early = _sum_adam(me_index, [slots["repl_early"]], [send_early], whole_rows, pk(w, _REPL_EARLY),
                          pk(m, _REPL_EARLY), pk(v, _REPL_EARLY), name="adam_repl_early")
    for k in range(4):
        out[k].update(zip(_REPL_EARLY, _unpack(res_early[k][0], [w[n].shape for n in _REPL_EARLY])))
    done = [out[k][n] for k in range(4) for n in matrices + _REPL_EARLY]
    slots = last.finish(state_last, done)
    transposed = lambda d: jnp.swapaxes(d["mix_w_in"], 1, 2)
    res_w_in = _sum_adam(me_index, [slots["mix_w_in"]], [g_even["mix_w_in_t"]], rows_block(w_in_rows, d_model),
                         transposed(w), transposed(m), transposed(v), name="adam_mix_w_in")
    res_small = _sum_adam(me_index, [slots["small"]], [send_small],
                          lambda tr: ((1, tr, _LANES), lambda i, me: (me[0], i, 0)),
                          pk(we_, _SMALL_SHARDED), pk(me_, _SMALL_SHARDED), pk(ve_, _SMALL_SHARDED), name="adam_small")
    res_late = _sum_adam(me_index, [slots["repl_late"]], [send_late], whole_rows, pk(w, _REPL_LATE), pk(m, _REPL_LATE),
                         pk(v, _REPL_LATE), name="adam_repl_late")
    for k in range(4):
        out[k]["mix_w_in"] = jnp.swapaxes(res_w_in[k], 1, 2)
        for n, a in zip(_SMALL_SHARDED, _unpack(res_small[k][0], [we_[n].shape for n in _SMALL_SHARDED])):
            out[k][n] = _unpad_groups(a, unpad[n]) if n in unpad else a
        out[k].update(zip(_REPL_LATE, _unpack(res_late[k][0], [w[n].shape for n in _REPL_LATE])))
    return loss, grad_x, out


_INPUTS = tuple("""x, mem, positions, norm_mix, norm_xa, norm_mem, norm_ffn, xa_wq, xa_wk, xa_wv, xa_wo, xa_q_norm, xa_k_norm, ffn_w_up, ffn_conv_w, ffn_conv_b, ffn_w_down, hg_lb_logits, mix_w_in, hg_out_norm, mla_q_a_norm, mla_w_uq, mla_kv_a_norm, mla_w_ukv, mla_qn_nope, mla_qn_rope, mla_kn_nope, mla_kn_rope, mix_w_out, s5_lam_re, s5_lam_im, s5_log_dt, s5_b_re, s5_b_im, s5_c_re, s5_c_im, s5_d, s5_w_glu_a, s5_w_glu_b, loss_target, m_norm_mix, m_norm_xa, m_norm_mem, m_norm_ffn, m_xa_wq, m_xa_wk, m_xa_wv, m_xa_wo, m_xa_q_norm, m_xa_k_norm, m_ffn_w_up, m_ffn_conv_w, m_ffn_conv_b, m_ffn_w_down, m_hg_lb_logits, m_mix_w_in, m_hg_out_norm, m_mla_q_a_norm, m_mla_w_uq, m_mla_kv_a_norm, m_mla_w_ukv, m_mla_qn_nope, m_mla_qn_rope, m_mla_kn_nope, m_mla_kn_rope, m_mix_w_out, m_s5_lam_re, m_s5_lam_im, m_s5_log_dt, m_s5_b_re, m_s5_b_im, m_s5_c_re, m_s5_c_im, m_s5_d, m_s5_w_glu_a, m_s5_w_glu_b, v_norm_mix, v_norm_xa, v_norm_mem, v_norm_ffn, v_xa_wq, v_xa_wk, v_xa_wv, v_xa_wo, v_xa_q_norm, v_xa_k_norm, v_ffn_w_up, v_ffn_conv_w, v_ffn_conv_b, v_ffn_w_down, v_hg_lb_logits, v_mix_w_in, v_hg_out_norm, v_mla_q_a_norm, v_mla_w_uq, v_mla_kv_a_norm, v_mla_w_ukv, v_mla_qn_nope, v_mla_qn_rope, v_mla_kn_nope, v_mla_kn_rope, v_mix_w_out, v_s5_lam_re, v_s5_lam_im, v_s5_log_dt, v_s5_b_re, v_s5_b_im, v_s5_c_re, v_s5_c_im, v_s5_d, v_s5_w_glu_a, v_s5_w_glu_b""".replace(" ", "").split(","))


def kernel(x, mem, positions, norm_mix, norm_xa, norm_mem, norm_ffn, xa_wq, xa_wk, xa_wv, xa_wo, xa_q_norm, xa_k_norm, ffn_w_up, ffn_conv_w, ffn_conv_b, ffn_w_down, hg_lb_logits, mix_w_in, hg_out_norm, mla_q_a_norm, mla_w_uq, mla_kv_a_norm, mla_w_ukv, mla_qn_nope, mla_qn_rope, mla_kn_nope, mla_kn_rope, mix_w_out, s5_lam_re, s5_lam_im, s5_log_dt, s5_b_re, s5_b_im, s5_c_re, s5_c_im, s5_d, s5_w_glu_a, s5_w_glu_b, loss_target, m_norm_mix, m_norm_xa, m_norm_mem, m_norm_ffn, m_xa_wq, m_xa_wk, m_xa_wv, m_xa_wo, m_xa_q_norm, m_xa_k_norm, m_ffn_w_up, m_ffn_conv_w, m_ffn_conv_b, m_ffn_w_down, m_hg_lb_logits, m_mix_w_in, m_hg_out_norm, m_mla_q_a_norm, m_mla_w_uq, m_mla_kv_a_norm, m_mla_w_ukv, m_mla_qn_nope, m_mla_qn_rope, m_mla_kn_nope, m_mla_kn_rope, m_mix_w_out, m_s5_lam_re, m_s5_lam_im, m_s5_log_dt, m_s5_b_re, m_s5_b_im, m_s5_c_re, m_s5_c_im, m_s5_d, m_s5_w_glu_a, m_s5_w_glu_b, v_norm_mix, v_norm_xa, v_norm_mem, v_norm_ffn, v_xa_wq, v_xa_wk, v_xa_wv, v_xa_wo, v_xa_q_norm, v_xa_k_norm, v_ffn_w_up, v_ffn_conv_w, v_ffn_conv_b, v_ffn_w_down, v_hg_lb_logits, v_mix_w_in, v_hg_out_norm, v_mla_q_a_norm, v_mla_w_uq, v_mla_kv_a_norm, v_mla_w_ukv, v_mla_qn_nope, v_mla_qn_rope, v_mla_kn_nope, v_mla_kn_rope, v_mix_w_out, v_s5_lam_re, v_s5_lam_im, v_s5_log_dt, v_s5_b_re, v_s5_b_im, v_s5_c_re, v_s5_c_im, v_s5_d, v_s5_w_glu_a, v_s5_w_glu_b):
    vals = dict(zip(_INPUTS, (x, mem, positions, norm_mix, norm_xa, norm_mem, norm_ffn, xa_wq, xa_wk, xa_wv, xa_wo, xa_q_norm, xa_k_norm, ffn_w_up, ffn_conv_w, ffn_conv_b, ffn_w_down, hg_lb_logits, mix_w_in, hg_out_norm, mla_q_a_norm, mla_w_uq, mla_kv_a_norm, mla_w_ukv, mla_qn_nope, mla_qn_rope, mla_kn_nope, mla_kn_rope, mix_w_out, s5_lam_re, s5_lam_im, s5_log_dt, s5_b_re, s5_b_im, s5_c_re, s5_c_im, s5_d, s5_w_glu_a, s5_w_glu_b, loss_target, m_norm_mix, m_norm_xa, m_norm_mem, m_norm_ffn, m_xa_wq, m_xa_wk, m_xa_wv, m_xa_wo, m_xa_q_norm, m_xa_k_norm, m_ffn_w_up, m_ffn_conv_w, m_ffn_conv_b, m_ffn_w_down, m_hg_lb_logits, m_mix_w_in, m_hg_out_norm, m_mla_q_a_norm, m_mla_w_uq, m_mla_kv_a_norm, m_mla_w_ukv, m_mla_qn_nope, m_mla_qn_rope, m_mla_kn_nope, m_mla_kn_rope, m_mix_w_out, m_s5_lam_re, m_s5_lam_im, m_s5_log_dt, m_s5_b_re, m_s5_b_im, m_s5_c_re, m_s5_c_im, m_s5_d, m_s5_w_glu_a, m_s5_w_glu_b, v_norm_mix, v_norm_xa, v_norm_mem, v_norm_ffn, v_xa_wq, v_xa_wk, v_xa_wv, v_xa_wo, v_xa_q_norm, v_xa_k_norm, v_ffn_w_up, v_ffn_conv_w, v_ffn_conv_b, v_ffn_w_down, v_hg_lb_logits, v_mix_w_in, v_hg_out_norm, v_mla_q_a_norm, v_mla_w_uq, v_mla_kv_a_norm, v_mla_w_ukv, v_mla_qn_nope, v_mla_qn_rope, v_mla_kn_nope, v_mla_kn_rope, v_mix_w_out, v_s5_lam_re, v_s5_lam_im, v_s5_log_dt, v_s5_b_re, v_s5_b_im, v_s5_c_re, v_s5_c_im, v_s5_d, v_s5_w_glu_a, v_s5_w_glu_b)))
    w = {n: vals[n] for n in _WEIGHTS}
    m = {n: vals["m_" + n] for n in _WEIGHTS}
    v = {n: vals["v_" + n] for n in _WEIGHTS}
    loss, grad_x, res = _train_step(vals["x"][0], vals["mem"][0], vals["positions"][0], vals["loss_target"][0],
                                    w, m, v)
    loss = lax.psum(loss[0, 0], ("x", "y", "c"))
    return (loss, grad_x[None], *[r[n] for r in res for n in _WEIGHTS])
```

```python
import functools

import jax
import jax.numpy as jnp
import numpy as np
from jax import lax
from jax.experimental import pallas as pl
from jax.experimental.pallas import tpu as pltpu

f32 = jnp.float32
bf16 = jnp.bfloat16

EPS = 1e-6
N_DEV = 8
VMEM_LIMIT = 52 * 1024 * 1024

HG_HEADS = 4
HG_DIM = 128
HG_WIDTH = HG_HEADS * HG_DIM
HG_CHUNK = 64
HG_SUB = 16
MLA_HEADS = 4
MLA_Q_RANK = 256
MLA_KV_RANK = 128
MLA_NOPE = 128
MLA_ROPE = 64
MLA_V = 128
MLA_QK = MLA_NOPE + MLA_ROPE
MLA_QK_PAD = 256
ROPE_BASE = 10000.0
IN_WIDTH = 4 * HG_WIDTH + MLA_Q_RANK + MLA_KV_RANK + MLA_ROPE
IN_PAD = 2560
XA_HEADS = 4
XA_DIM = 256
S5_GROUP = 16
S5_GROUPS = 64
S5_STATE = 64
CONV_W = 3

ADAM_LR = 0.001
ADAM_B1 = 0.9
ADAM_B2 = 0.999
ADAM_EPS = 1e-08
ADAM_WD = 0.01
ADAM_STEP = 10

_NT = (((1,), (1,)), ((), ()))
_TN = (((0,), (0,)), ((), ()))


def _pick(n, cands):
    for c in cands:
        if n % c == 0:
            return c
    return n


def _cparams(sem):
    return pltpu.CompilerParams(dimension_semantics=sem, vmem_limit_bytes=VMEM_LIMIT)


_MM_BUDGET = 36 * 1024 * 1024
_MM_TILES = ((1024, 1024), (1024, 512), (512, 1024), (512, 512), (512, 256), (256, 512), (256, 256), (256, 128),
             (128, 256), (128, 128))


def _mm(a, b, *, name, ta=False, tb=False, out_dtype=f32, add=None, b2=None, kslab=None, norm=None, rms_bwd=None,
        sq_err=None):
    m, k = (a.shape[1], a.shape[0]) if ta else a.shape
    nb = b.shape[0] if tb else b.shape[1]
    n = nb * (2 if b2 is not None else 1)
    slab, nslab = kslab if kslab is not None else (0, 1)
    assert (b.shape[1] // nslab if tb else b.shape[0]) == k, (a.shape, b.shape, ta, tb)
    assert b2 is None or (not tb and b2.shape == b.shape)
    full_rows = norm is not None or rms_bwd is not None
    assert not (full_rows and b2 is not None) and not (norm is not None and rms_bwd is not None)
    isz = lambda x: jnp.dtype(x.dtype).itemsize
    bm = bn = None
    for cm, cn in _MM_TILES:
        if m % cm or nb % cn or (full_rows and cn != n):
            continue
        need = 2 * (cm * k * isz(a) + cn * k * isz(b) * (2 if b2 is not None else 1)
                    + cm * cn * (jnp.dtype(out_dtype).itemsize + (4 if add is not None else 0)
                                 + (2 if norm is not None else 0) + (8 if rms_bwd is not None else 0)
                                 + (4 if sq_err is not None else 0)))
        if need <= _MM_BUDGET:
            bm, bn = cm, cn
            break
    assert bm is not None, (name, a.shape, b.shape)
    half = nb // bn
    dims = (((0 if ta else 1,), (1 if tb else 0,)), ((), ()))

    def body(*refs):
        refs = list(refs)
        a_ref, b_ref = refs[0], refs[1]
        b2_ref = refs.pop(2) if b2 is not None else None
        add_ref = refs.pop(2) if add is not None else None
        gain_ref = refs.pop(2) if norm is not None else None
        x_ref, xgain_ref, through_ref = (refs.pop(2), refs.pop(2), refs.pop(2)) if rms_bwd is not None else (None,) * 3
        target_ref = refs.pop(2) if sq_err is not None else None
        o_ref = refs[2]
        extra_ref = refs[3] if (norm is not None or rms_bwd is not None or sq_err is not None) else None

        def run(rhs_ref):
            r = lax.dot_general(a_ref[...].astype(bf16), rhs_ref[...].astype(bf16), dims, preferred_element_type=f32)
            if add_ref is not None:
                r = r + add_ref[...].astype(f32)
            if sq_err is not None:
                err = r - target_ref[...]
                o_ref[...] = (err * (1.0 / n)).astype(o_ref.dtype)

                @pl.when((pl.program_id(0) == 0) & (pl.program_id(1) == 0))
                def _():
                    extra_ref[...] = jnp.zeros_like(extra_ref)
                extra_ref[...] += jnp.sum(jnp.sum(err * err, axis=-1, keepdims=True), axis=0, keepdims=True) * (0.5 / n)
            elif rms_bwd is None:
                o_ref[...] = r.astype(o_ref.dtype)
            else:
                _, vjp = jax.vjp(_rms, x_ref[...], xgain_ref[...])
                dx, dgain = vjp(r)
                o_ref[...] = (through_ref[...] + dx).astype(o_ref.dtype)

                @pl.when(pl.program_id(0) == 0)
                def _():
                    extra_ref[...] = jnp.zeros_like(extra_ref)
                extra_ref[...] += dgain
            if norm is not None:
                extra_ref[...] = _rms(r, gain_ref[...]).astype(extra_ref.dtype)

        if b2_ref is None:
            run(b_ref)
        else:
            pl.when(pl.program_id(1) < half)(lambda: run(b_ref))
            pl.when(pl.program_id(1) >= half)(lambda: run(b2_ref))

    a_spec = pl.BlockSpec((k, bm), lambda i, j: (0, i)) if ta else pl.BlockSpec((bm, k), lambda i, j: (i, 0))
    if tb:
        b_spec = pl.BlockSpec((bn, k), lambda i, j: (j, slab))
    elif b2 is None:
        b_spec = pl.BlockSpec((k, bn), lambda i, j: (0, j))
    else:
        b_spec = pl.BlockSpec((k, bn), lambda i, j: (0, jnp.minimum(j, half - 1)))
    in_specs = [a_spec, b_spec]
    args = [a, b]
    if b2 is not None:
        in_specs.append(pl.BlockSpec((k, bn), lambda i, j: (0, jnp.maximum(j - half, 0))))
        args.append(b2)
    if add is not None:
        in_specs.append(pl.BlockSpec((bm, bn), lambda i, j: (i, j)))
        args.append(add)
    out_blk = pl.BlockSpec((bm, bn), lambda i, j: (i, j))
    out_specs, out_shape = out_blk, jax.ShapeDtypeStruct((m, n), out_dtype)
    row_blk = pl.BlockSpec((1, bn), lambda i, j: (0, j))
    if norm is not None:
        in_specs.append(row_blk)
        args.append(norm)
        out_specs, out_shape = [out_blk, out_blk], [out_shape, jax.ShapeDtypeStruct((m, n), bf16)]
    if rms_bwd is not None:
        in_specs += [out_blk, row_blk, out_blk]
        args += list(rms_bwd)
        out_specs, out_shape = [out_blk, row_blk], [out_shape, jax.ShapeDtypeStruct((1, n), f32)]
    if sq_err is not None:
        in_specs.append(out_blk)
        args.append(sq_err)
        out_specs = [out_blk, pl.BlockSpec((1, 1), lambda i, j: (0, 0))]
        out_shape = [out_shape, jax.ShapeDtypeStruct((1, 1), f32)]
    carried = rms_bwd is not None or sq_err is not None
    return pl.pallas_call(
        body, grid=(m // bm, n // bn), in_specs=in_specs, out_specs=out_specs, out_shape=out_shape,
        compiler_params=_cparams(("arbitrary" if carried else "parallel", "arbitrary" if sq_err is not None else "parallel")),
        name=name)(*args)


def _as_tuple(x):
    return tuple(x) if isinstance(x, (tuple, list)) else (x,)


def _full_spec(p):
    nd = p.ndim
    return pl.BlockSpec(p.shape, lambda i, _nd=nd: (0,) * _nd)


def _window(a, start, width):
    assert start % width == 0 and width % 128 == 0
    return (a, start // width, width)


def _row_array(x):
    return x[0] if isinstance(x, tuple) else x


def _row_shape(x):
    return (x[0].shape[0], x[2]) if isinstance(x, tuple) else x.shape


def _row_spec(x, tile):
    if isinstance(x, tuple):
        return pl.BlockSpec((tile, x[2]), lambda i, _b=x[1]: (i, _b))
    return pl.BlockSpec((tile, x.shape[1]), lambda i: (i, 0))


def _rows(fn, rows, params, outs, *, name, tile=256, accs=()):
    length = _row_shape(rows[0])[0]
    tile = min(tile, length)
    nr, npar, no = len(rows), len(params), len(outs)

    def body(*refs):
        r, p, o = refs[:nr], refs[nr:nr + npar], refs[nr + npar:]
        res = _as_tuple(fn(*[x[...].astype(f32) for x in r], *[x[...] for x in p]))
        for kk in range(no):
            o[kk][...] = res[kk].astype(o[kk].dtype)
        if accs:
            @pl.when(pl.program_id(0) == 0)
            def _():
                for kk in range(no, no + len(accs)):
                    o[kk][...] = jnp.zeros_like(o[kk])
            for kk in range(no, no + len(accs)):
                o[kk][...] += res[kk]

    in_specs = [_row_spec(x, tile) for x in rows] + [_full_spec(p) for p in params]
    out_specs = [pl.BlockSpec((tile, w), lambda i: (i, 0)) for w, _ in outs]
    out_shape = [jax.ShapeDtypeStruct((length, w), d) for w, d in outs]
    for s in accs:
        out_specs.append(pl.BlockSpec(s, lambda i, _nd=len(s): (0,) * _nd))
        out_shape.append(jax.ShapeDtypeStruct(s, f32))
    res = pl.pallas_call(body, grid=(length // tile,), in_specs=in_specs, out_specs=out_specs, out_shape=out_shape,
                         compiler_params=_cparams(("arbitrary",)), name=name)(*[_row_array(x) for x in rows], *params)
    return res


def _rows_bwd(fn, rows, params, cts, *, name, rgrad, pgrad, tile=256, addends=None):
    addends = {i: (a if isinstance(a, list) else [(a, 0)]) for i, a in (addends or {}).items()}
    length = _row_shape(rows[0])[0]
    tile = min(tile, length)
    nr, npar, nc = len(rows), len(params), len(cts)
    ridx = [i for i in range(nr) if rgrad[i] is not None]
    pidx = [i for i in range(npar) if pgrad[i]]
    flat_addends = [(i, a, off) for i in sorted(addends) for a, off in addends[i]]
    na = len(flat_addends)

    def body(*refs):
        r, p, c = refs[:nr], refs[nr:nr + npar], refs[nr + npar:nr + npar + nc]
        ad = refs[nr + npar + nc:nr + npar + nc + na]
        o = refs[nr + npar + nc + na:]
        rv = [x[...].astype(f32) for x in r]
        pv = [x[...] for x in p]
        cv = tuple(x[...].astype(f32) for x in c)

        def g(*d):
            rr, pp = list(rv), list(pv)
            for n_, i_ in enumerate(ridx):
                rr[i_] = d[n_]
            for n_, i_ in enumerate(pidx):
                pp[i_] = d[len(ridx) + n_]
            return _as_tuple(fn(*rr, *pp))

        _, vjp = jax.vjp(g, *[rv[i] for i in ridx], *[pv[i] for i in pidx])
        grads = vjp(cv)
        for n_, i_ in enumerate(ridx):
            val = grads[n_]
            for k_, (j_, a_, off) in enumerate(flat_addends):
                if j_ == i_:
                    extra = ad[k_][...].astype(f32)
                    if extra.shape[1] != val.shape[1]:
                        extra = jnp.pad(extra, ((0, 0), (off, val.shape[1] - off - extra.shape[1])))
                    val = val + extra
            o[n_][...] = val.astype(o[n_].dtype)
        if pidx:
            @pl.when(pl.program_id(0) == 0)
            def _():
                for n_ in range(len(pidx)):
                    o[len(ridx) + n_][...] = jnp.zeros_like(o[len(ridx) + n_])
            for n_ in range(len(pidx)):
                o[len(ridx) + n_][...] += grads[len(ridx) + n_]

    plain = lambda shape: pl.BlockSpec((tile, shape[1]), lambda i: (i, 0))
    in_specs = ([_row_spec(x, tile) for x in rows] + [_full_spec(p) for p in params] + [plain(x.shape) for x in cts]
                + [plain(a.shape) for _, a, _ in flat_addends])
    out_specs = [plain(_row_shape(rows[i])) for i in ridx] + [_full_spec(params[i]) for i in pidx]
    out_shape = ([jax.ShapeDtypeStruct(_row_shape(rows[i]), rgrad[i]) for i in ridx]
                 + [jax.ShapeDtypeStruct(params[i].shape, f32) for i in pidx])
    res = pl.pallas_call(body, grid=(length // tile,), in_specs=in_specs, out_specs=out_specs, out_shape=out_shape,
                         compiler_params=_cparams(("arbitrary",)), name=name)(
        *[_row_array(x) for x in rows], *params, *cts, *[a for _, a, _ in flat_addends])
    return list(res[:len(ridx)]), list(res[len(ridx):])


def _rms(x, g):
    return x * lax.rsqrt(jnp.mean(x * x, axis=-1, keepdims=True) + EPS) * g


def _rms_twice(x, g):
    y = _rms(x, g)
    return y, y


def _silu(x):
    return x * jax.nn.sigmoid(x)


_UP_COLS = 1024


def _ffn_up_conv(hf, w_up, cw, cb, *, name):
    length, d = hf.shape
    ff = w_up.shape[1] // 2
    bm = _pick(length, (1024, 512, 256))
    bn = _UP_COLS
    nj = ff // bn

    def body(hf_ref, wg_ref, wv_ref, cwg, cwv, cbg, cbv, ug_ref, uv_ref, a_ref, halo_g, halo_v):
        i, j = pl.program_id(0), pl.program_id(1)
        x = hf_ref[...]
        rows = lax.broadcasted_iota(jnp.int32, (bm, bn), 0)
        pad = jnp.zeros((bm - 8, bn), f32)

        def half(w_ref, cw_ref, cb_ref, u_ref, halo):
            u = jnp.dot(x, w_ref[...], preferred_element_type=f32).astype(bf16)
            u_ref[...] = u
            u = u.astype(f32)
            prev = jnp.where(i == 0, 0.0, halo[j])
            x1 = jnp.where(rows >= 1, pltpu.roll(u, 1, axis=0), jnp.concatenate([pltpu.roll(prev, 1, axis=0), pad]))
            x2 = jnp.where(rows >= 2, pltpu.roll(u, 2, axis=0), jnp.concatenate([pltpu.roll(prev, 2, axis=0), pad]))
            halo[j] = u[bm - 8:, :]
            return cw_ref[2:3, :] * u + cw_ref[1:2, :] * x1 + cw_ref[0:1, :] * x2 + cb_ref[...]

        g = half(wg_ref, cwg, cbg, ug_ref, halo_g)
        v = half(wv_ref, cwv, cbv, uv_ref, halo_v)
        a_ref[...] = (_silu(g) * v).astype(a_ref.dtype)

    col = lambda r, off: pl.BlockSpec((r, bn), lambda i, j, _o=off: (0, j + _o))
    out_blk = pl.BlockSpec((bm, bn), lambda i, j: (i, j))
    sds = jax.ShapeDtypeStruct((length, ff), bf16)
    return pl.pallas_call(
        body, grid=(length // bm, nj),
        in_specs=[pl.BlockSpec((bm, d), lambda i, j: (i, 0)), col(d, 0), col(d, nj), col(CONV_W, 0), col(CONV_W, nj),
                  col(1, 0), col(1, nj)],
        out_specs=[out_blk] * 3, out_shape=[sds] * 3,
        scratch_shapes=[pltpu.VMEM((nj, 8, bn), f32), pltpu.VMEM((nj, 8, bn), f32)],
        compiler_params=_cparams(("arbitrary", "arbitrary")), name=name)(hf, w_up, w_up, cw, cw, cb, cb)


def _ffn_da_dconv(dout, w_down, u_g, u_v, cw, cb, *, name):
    length, d = dout.shape
    ff = u_g.shape[1]
    bm = _pick(length, (1024, 512, 256))
    bn = _UP_COLS
    nj, ni = ff // bn, length // bm

    def body(dout_ref, wd_ref, ug_ref, uv_ref, pg_ref, pv_ref, cwg, cwv, cbg, cbv,
             dug_ref, duv_ref, sums_g_ref, sums_v_ref, halo_g, halo_v, acc_g, acc_v):
        s, j = pl.program_id(0), pl.program_id(1)
        rows = lax.broadcasted_iota(jnp.int32, (bm, bn), 0)
        row8 = lax.broadcasted_iota(jnp.int32, (8, bn), 0)
        pad = jnp.zeros((bm - 8, bn), f32)
        da = lax.dot_general(dout_ref[...].astype(bf16), wd_ref[...], _NT, preferred_element_type=f32)

        def conv(u_ref, p_ref, cw_ref, cb_ref):
            x = u_ref[...].astype(f32)
            prev = jnp.where(s == ni - 1, 0.0, p_ref[...].astype(f32))
            x1 = jnp.where(rows >= 1, pltpu.roll(x, 1, axis=0), jnp.concatenate([pltpu.roll(prev, 1, axis=0), pad]))
            x2 = jnp.where(rows >= 2, pltpu.roll(x, 2, axis=0), jnp.concatenate([pltpu.roll(prev, 2, axis=0), pad]))
            return cw_ref[2:3, :] * x + cw_ref[1:2, :] * x1 + cw_ref[0:1, :] * x2 + cb_ref[...], x, x1, x2

        g, xg, xg1, xg2 = conv(ug_ref, pg_ref, cwg, cbg)
        v, xv, xv1, xv2 = conv(uv_ref, pv_ref, cwv, cbv)
        sg = jax.nn.sigmoid(g)
        dg = da * v * (sg * (1.0 + g * (1.0 - sg)))
        dv = da * (g * sg)

        def back(dy, x, x1, x2, cw_ref, du_ref, sums_ref, halo, acc):
            nxt = jnp.where(s == 0, 0.0, halo[j])
            up1 = jnp.where(rows < bm - 1, pltpu.roll(dy, bm - 1, axis=0),
                            jnp.concatenate([pad, pltpu.roll(nxt, 7, axis=0)]))
            up2 = jnp.where(rows < bm - 2, pltpu.roll(dy, bm - 2, axis=0),
                            jnp.concatenate([pad, pltpu.roll(nxt, 6, axis=0)]))
            halo[j] = dy[:8, :]
            du_ref[...] = (cw_ref[2:3, :] * dy + cw_ref[1:2, :] * up1 + cw_ref[0:1, :] * up2).astype(du_ref.dtype)
            col = lambda t: jnp.sum(t, axis=0, keepdims=True)
            part = jnp.where(row8 == 0, col(dy * x2), jnp.where(row8 == 1, col(dy * x1), jnp.where(
                row8 == 2, col(dy * x), jnp.where(row8 == 3, col(dy), 0.0))))
            total = jnp.where(s == 0, part, acc[j] + part)
            acc[j] = total
            sums_ref[...] = total

        back(dg, xg, xg1, xg2, cwg, dug_ref, sums_g_ref, halo_g, acc_g)
        back(dv, xv, xv1, xv2, cwv, duv_ref, sums_v_ref, halo_v, acc_v)

    rb = lambda s: ni - 1 - s
    tile = pl.BlockSpec((bm, bn), lambda s, j: (rb(s), j))
    before = pl.BlockSpec((8, bn), lambda s, j: (jnp.maximum(rb(s) * (bm // 8) - 1, 0), j))
    col = lambda r, off: pl.BlockSpec((r, bn), lambda s, j, _o=off: (0, j + _o))
    sds = jax.ShapeDtypeStruct
    dug, duv, sums_g, sums_v = pl.pallas_call(
        body, grid=(ni, nj),
        in_specs=[pl.BlockSpec((bm, d), lambda s, j: (rb(s), 0)), pl.BlockSpec((bn, d), lambda s, j: (j, 0)),
                  tile, tile, before, before, col(CONV_W, 0), col(CONV_W, nj), col(1, 0), col(1, nj)],
        out_specs=[tile, tile] + [pl.BlockSpec((8, bn), lambda s, j: (s, j))] * 2,
        out_shape=[sds((length, ff), bf16), sds((length, ff), bf16), sds((ni * 8, ff), f32), sds((ni * 8, ff), f32)],
        scratch_shapes=[pltpu.VMEM((nj, 8, bn), f32)] * 4,
        compiler_params=_cparams(("arbitrary", "arbitrary")), name=name)(
        dout, w_down, u_g, u_v, u_g, u_v, cw, cw, cb, cb)
    last = (ni - 1) * 8
    both = lambda lo, hi: jnp.concatenate([sums_g[last + lo:last + hi], sums_v[last + lo:last + hi]], axis=1)
    return dug, duv, both(0, CONV_W), both(CONV_W, CONV_W + 1)


def _ffn_fwd(h, w, tag, hf=None, loss_target=None):
    if hf is None:
        hf, = _rows(_rms, [h], [w["norm_ffn"]], [(h.shape[1], bf16)], name=f"ffn_norm_{tag}")
    u_g, u_v, a = _ffn_up_conv(hf, w["ffn_w_up"], w["ffn_conv_w"], w["ffn_conv_b"], name=f"ffn_up_{tag}")
    out = _mm(a, w["ffn_w_down"], add=h, sq_err=loss_target, name=f"ffn_down_{tag}")
    return out, (h, hf, u_g, u_v, a)


def _ffn_bwd(dout, w, saved, tag):
    h, hf, u_g, u_v, a = saved
    g = {"ffn_w_down": _mm(a, dout, ta=True, out_dtype=bf16, name=f"ffn_dwdown_{tag}")}
    dug, duv, g["ffn_conv_w"], g["ffn_conv_b"] = _ffn_da_dconv(
        dout, w["ffn_w_down"], u_g, u_v, w["ffn_conv_w"], w["ffn_conv_b"], name=f"ffn_dconv_{tag}")
    dhf = _mm(dug, w["ffn_w_up"], tb=True, kslab=(0, 2), name=f"ffn_dhf_g_{tag}")
    dh, g["norm_ffn"] = _mm(duv, w["ffn_w_up"], tb=True, kslab=(1, 2), add=dhf, rms_bwd=(h, w["norm_ffn"], dout),
                            name=f"ffn_dhf_v_{tag}")
    g["ffn_w_up"] = _mm(hf, dug, ta=True, b2=duv, out_dtype=bf16, name=f"ffn_dwup_{tag}")
    return dh, g


def _xattn_fn(qx, kx, vx, qg, kg):
    outs = []
    for hh in range(XA_HEADS):
        sl = slice(hh * XA_DIM, (hh + 1) * XA_DIM)
        q = _rms(qx[:, sl], qg).astype(bf16)
        k = _rms(kx[:, sl], kg).astype(bf16)
        s = lax.dot_general(q, k, _NT, preferred_element_type=f32) * (XA_DIM ** -0.5)
        s = s - jnp.max(s, axis=-1, keepdims=True)
        p = jnp.exp(s)
        p = p / jnp.sum(p, axis=-1, keepdims=True)
        outs.append(jnp.dot(p.astype(bf16), vx[:, sl].astype(bf16), preferred_element_type=f32))
    return jnp.concatenate(outs, axis=-1)


def _xattn_fwd(h, mem, w, tag, hx=None):
    d = h.shape[1]
    if hx is None:
        hx, = _rows(_rms, [h], [w["norm_xa"]], [(d, bf16)], name=f"xa_norm_{tag}")
    qx = _mm(hx, w["xa_wq"], out_dtype=bf16, name=f"xa_q_{tag}")
    m, = _rows(_rms, [mem], [w["norm_mem"]], [(d, bf16)], name=f"xa_mnorm_{tag}")
    kx = _mm(m, w["xa_wk"], name=f"xa_k_{tag}")
    vx = _mm(m, w["xa_wv"], name=f"xa_v_{tag}")
    o, = _rows(_xattn_fn, [qx], [kx, vx, w["xa_q_norm"], w["xa_k_norm"]], [(d, bf16)], tile=1024,
               name=f"xa_attn_{tag}")
    out, hf = _mm(o, w["xa_wo"], add=h, norm=w["norm_ffn"], name=f"xa_o_{tag}")
    return out, (h, hx, qx, m, kx, vx, o), hf


def _xattn_bwd(dout, mem, w, saved, tag):
    h, hx, qx, m, kx, vx, o = saved
    g = {}
    do = _mm(dout, w["xa_wo"], tb=True, out_dtype=bf16, name=f"xa_do_{tag}")
    g["xa_wo"] = _mm(o, dout, ta=True, out_dtype=bf16, name=f"xa_dwo_{tag}")
    (dqx,), (dkx, dvx, g["xa_q_norm"], g["xa_k_norm"]) = _rows_bwd(
        _xattn_fn, [qx], [kx, vx, w["xa_q_norm"], w["xa_k_norm"]], [do], rgrad=[bf16], pgrad=[True] * 4,
        tile=1024, name=f"xa_dattn_{tag}")
    dh, g["norm_xa"] = _mm(dqx, w["xa_wq"], tb=True, rms_bwd=(h, w["norm_xa"], dout), name=f"xa_dhx_{tag}")
    g["xa_wq"] = _mm(hx, dqx, ta=True, out_dtype=bf16, name=f"xa_dwq_{tag}")
    dm = _mm(dkx, w["xa_wk"], tb=True, name=f"xa_dm_k_{tag}")
    dm = _mm(dvx, w["xa_wv"], tb=True, add=dm, name=f"xa_dm_v_{tag}")
    g["xa_wk"] = _mm(m, dkx, ta=True, out_dtype=bf16, name=f"xa_dwk_{tag}")
    g["xa_wv"] = _mm(m, dvx, ta=True, out_dtype=bf16, name=f"xa_dwv_{tag}")
    _, (g["norm_mem"],) = _rows_bwd(_rms, [mem], [w["norm_mem"]], [dm], rgrad=[None], pgrad=[True],
                                    name=f"xa_dmnorm_{tag}")
    return dh, g


_HG_GROUP = 8


def _hg_chunk(q, k, v, g, *sts):
    c = q.shape[0]
    heads = [slice(h * HG_DIM, (h + 1) * HG_DIM) for h in range(len(sts))]
    tri = (lax.broadcasted_iota(jnp.int32, (c, c), 0) >= lax.broadcasted_iota(jnp.int32, (c, c), 1)).astype(f32)
    b = jnp.dot(tri, g, precision=lax.Precision.HIGHEST, preferred_element_type=f32)
    bend = jnp.sum(g, axis=0, keepdims=True)
    qe = (q * jnp.exp(b)).astype(bf16)
    kd = (k * jnp.exp(bend - b)).astype(bf16)
    vb = v.astype(bf16)
    decay = jnp.exp(bend)
    o_inter = [lax.dot_general(qe[:, hs], st.astype(bf16), _NT, preferred_element_type=f32) for hs, st in zip(heads, sts)]
    new = [st * decay[:, hs] + lax.dot_general(vb[:, hs], kd[:, hs], _TN, preferred_element_type=f32)
           for hs, st in zip(heads, sts)]
    outs = []
    for i in range(c // HG_SUB):
        lo, n = HG_SUB * i, HG_SUB * (i + 1)
        ref = jnp.sum(g[:lo], axis=0, keepdims=True) if i else jnp.zeros((1, g.shape[1]), f32)
        qh = (q[lo:n] * jnp.exp(b[lo:n] - ref)).astype(bf16)
        kh = (k[:n] * jnp.exp(ref - b[:n])).astype(bf16)
        keep = (lax.broadcasted_iota(jnp.int32, (HG_SUB, n), 1)
                <= lo + lax.broadcasted_iota(jnp.int32, (HG_SUB, n), 0))
        scores = [lax.dot_general(qh[:, hs], kh[:, hs], _NT, preferred_element_type=f32) for hs in heads]
        scores = [jnp.where(keep, a, 0.0).astype(bf16) for a in scores]
        outs.append(jnp.concatenate([jnp.dot(a, vb[:n, hs], preferred_element_type=f32)
                                     for a, hs in zip(scores, heads)], axis=1))
    return (jnp.concatenate(outs, axis=0) + jnp.concatenate(o_inter, axis=1), *new)


def _hg_fwd(q, k, v, g, *, name):
    length = q.shape[0]
    rows = _HG_GROUP * HG_CHUNK
    ng = length // rows
    nc = length // HG_CHUNK

    def body(q_ref, k_ref, v_ref, g_ref, o_ref, st_ref, state):
        @pl.when(pl.program_id(0) == 0)
        def _():
            state[...] = jnp.zeros_like(state)

        states = [state[h] for h in range(HG_HEADS)]
        for ci in range(_HG_GROUP):
            sl = slice(ci * HG_CHUNK, (ci + 1) * HG_CHUNK)
            for h in range(HG_HEADS):
                st_ref[h, ci] = states[h].astype(st_ref.dtype)
            o, *states = _hg_chunk(q_ref[sl, :], k_ref[sl, :], v_ref[sl, :], g_ref[sl, :], *states)
            o_ref[sl, :] = o
        for h in range(HG_HEADS):
            state[h] = states[h]

    blk = pl.BlockSpec((rows, HG_WIDTH), lambda c: (c, 0))
    return pl.pallas_call(
        body, grid=(ng,), in_specs=[blk] * 4,
        out_specs=[blk, pl.BlockSpec((HG_HEADS, _HG_GROUP, HG_DIM, HG_DIM), lambda c: (0, c, 0, 0))],
        out_shape=[jax.ShapeDtypeStruct((length, HG_WIDTH), f32),
                   jax.ShapeDtypeStruct((HG_HEADS, nc, HG_DIM, HG_DIM), bf16)],
        scratch_shapes=[pltpu.VMEM((HG_HEADS, HG_DIM, HG_DIM), f32)],
        compiler_params=_cparams(("arbitrary",)), name=name)(q, k, v, g)


def _hg_bwd(q, k, v, g, states, do, *, name):
    length = q.shape[0]
    rows = _HG_GROUP * HG_CHUNK
    ng = length // rows

    def body(q_ref, k_ref, v_ref, g_ref, st_ref, do_ref, dq_ref, dk_ref, dv_ref, dg_ref, dstate):
        @pl.when(pl.program_id(0) == 0)
        def _():
            dstate[...] = jnp.zeros_like(dstate)

        dstates = [dstate[h] for h in range(HG_HEADS)]
        for ci in reversed(range(_HG_GROUP)):
            sl = slice(ci * HG_CHUNK, (ci + 1) * HG_CHUNK)
            _, vjp = jax.vjp(_hg_chunk, q_ref[sl, :], k_ref[sl, :], v_ref[sl, :], g_ref[sl, :],
                             *[st_ref[h, ci].astype(f32) for h in range(HG_HEADS)])
            dq, dk, dv, dg, *dstates = vjp((do_ref[sl, :], *dstates))
            dq_ref[sl, :] = dq
            dk_ref[sl, :] = dk
            dv_ref[sl, :] = dv
            dg_ref[sl, :] = dg
        for h in range(HG_HEADS):
            dstate[h] = dstates[h]

    blk = pl.BlockSpec((rows, HG_WIDTH), lambda c: (ng - 1 - c, 0))
    sds = jax.ShapeDtypeStruct((length, HG_WIDTH), f32)
    return pl.pallas_call(
        body, grid=(ng,),
        in_specs=[blk] * 4 + [pl.BlockSpec((HG_HEADS, _HG_GROUP, HG_DIM, HG_DIM), lambda c: (0, ng - 1 - c, 0, 0)), blk],
        out_specs=[blk] * 4, out_shape=[sds] * 4,
        scratch_shapes=[pltpu.VMEM((HG_HEADS, HG_DIM, HG_DIM), f32)],
        compiler_params=_cparams(("arbitrary",)), name=name)(q, k, v, g, states, do)


_ATT_BLK = 512
_ATT_SCALE = MLA_QK ** -0.5
_NEG = -1e30


def _att_mask(i, j, t):
    rows = i * t + lax.broadcasted_iota(jnp.int32, (t, t), 0)
    cols = j * t + lax.broadcasted_iota(jnp.int32, (t, t), 1)
    return cols <= rows


def _att_fwd(q, k, v, *, name):
    length = q.shape[0]
    t = min(_ATT_BLK, length)
    nq = length // t
    qw, vw = MLA_QK_PAD, MLA_V
    heads = range(MLA_HEADS)

    def body(q_ref, k_ref, v_ref, o_ref, lse_ref):
        i = pl.program_id(0)
        qbs = [q_ref[:, h * qw:(h + 1) * qw] for h in heads]

        def step(j, carry, diagonal=False):
            off = pl.multiple_of(j * t, t)
            out = []
            for h in heads:
                m, l, acc = carry[h]
                ks = k_ref[pl.ds(off, t), h * qw:(h + 1) * qw]
                vs = v_ref[pl.ds(off, t), h * vw:(h + 1) * vw]
                s = lax.dot_general(qbs[h], ks, _NT, preferred_element_type=f32) * _ATT_SCALE
                if diagonal:
                    s = jnp.where(_att_mask(i, j, t), s, _NEG)
                m_new = jnp.maximum(m, jnp.max(s, axis=-1, keepdims=True))
                alpha = jnp.exp(m - m_new)
                p = jnp.exp(s - m_new)
                l = alpha * l + jnp.sum(p, axis=-1, keepdims=True)
                acc = alpha * acc + jnp.dot(p.astype(bf16), vs, preferred_element_type=f32)
                out.append((m_new, l, acc))
            return tuple(out)

        init = tuple((jnp.full((t, 1), _NEG, f32), jnp.zeros((t, 1), f32), jnp.zeros((t, vw), f32)) for _ in heads)
        res = step(i, lax.fori_loop(0, i, step, init), diagonal=True)
        for h in heads:
            m, l, acc = res[h]
            o_ref[:, h * vw:(h + 1) * vw] = (acc / l).astype(o_ref.dtype)
            lse_ref[:, h * vw:(h + 1) * vw] = jnp.broadcast_to(m + jnp.log(l), (t, vw))

    return pl.pallas_call(
        body, grid=(nq,),
        in_specs=[pl.BlockSpec((t, q.shape[1]), lambda i: (i, 0)), pl.BlockSpec(k.shape, lambda i: (0, 0)),
                  pl.BlockSpec(v.shape, lambda i: (0, 0))],
        out_specs=[pl.BlockSpec((t, v.shape[1]), lambda i: (i, 0))] * 2,
        out_shape=[jax.ShapeDtypeStruct(v.shape, bf16), jax.ShapeDtypeStruct(v.shape, f32)],
        compiler_params=_cparams(("arbitrary",)), name=name)(q, k, v)


def _att_bwd(q, k, v, o, lse, do, *, name):
    length = q.shape[0]
    t = min(_ATT_BLK, length)
    nq = length // t
    qw, vw = MLA_QK_PAD, MLA_V
    heads = range(MLA_HEADS)

    def dq_body(q_ref, k_ref, v_ref, o_ref, lse_ref, do_ref, dq_ref, delta_ref):
        i = pl.program_id(0)
        qbs = [q_ref[:, h * qw:(h + 1) * qw] for h in heads]
        dobs = [do_ref[:, h * vw:(h + 1) * vw] for h in heads]
        lses = [lse_ref[:, h * vw:h * vw + 1] for h in heads]
        deltas = [jnp.sum(dobs[h].astype(f32) * o_ref[:, h * vw:(h + 1) * vw].astype(f32), axis=-1, keepdims=True)
                  for h in heads]

        def step(j, dqs, diagonal=False):
            off = pl.multiple_of(j * t, t)
            out = []
            for h in heads:
                ks = k_ref[pl.ds(off, t), h * qw:(h + 1) * qw]
                vs = v_ref[pl.ds(off, t), h * vw:(h + 1) * vw]
                s = lax.dot_general(qbs[h], ks, _NT, preferred_element_type=f32) * _ATT_SCALE
                p = jnp.exp(s - lses[h])
                if diagonal:
                    p = jnp.where(_att_mask(i, j, t), p, 0.0)
                dp = lax.dot_general(dobs[h], vs, _NT, preferred_element_type=f32)
                ds = p * (dp - deltas[h]) * _ATT_SCALE
                out.append(dqs[h] + jnp.dot(ds.astype(bf16), ks, preferred_element_type=f32))
            return tuple(out)

        dqs = step(i, lax.fori_loop(0, i, step, tuple(jnp.zeros((t, qw), f32) for _ in heads)), diagonal=True)
        for h in heads:
            dq_ref[:, h * qw:(h + 1) * qw] = dqs[h].astype(dq_ref.dtype)
            delta_ref[:, h * vw:(h + 1) * vw] = jnp.broadcast_to(deltas[h], (t, vw))

    qblk = pl.BlockSpec((t, q.shape[1]), lambda i: (i, 0))
    vblk = pl.BlockSpec((t, v.shape[1]), lambda i: (i, 0))
    qfull = pl.BlockSpec(q.shape, lambda i: (0, 0))
    vfull = pl.BlockSpec(v.shape, lambda i: (0, 0))
    dq, delta = pl.pallas_call(
        dq_body, grid=(nq,), in_specs=[qblk, qfull, vfull, vblk, vblk, vblk], out_specs=[qblk, vblk],
        out_shape=[jax.ShapeDtypeStruct(q.shape, bf16), jax.ShapeDtypeStruct(lse.shape, f32)],
        compiler_params=_cparams(("arbitrary",)), name=name + "_dq")(q, k, v, o, lse, do)

    def dkv_body(k_ref, v_ref, q_ref, do_ref, lse_ref, delta_ref, dk_ref, dv_ref):
        j = pl.program_id(0)
        kbs = [k_ref[:, h * qw:(h + 1) * qw] for h in heads]
        vbs = [v_ref[:, h * vw:(h + 1) * vw] for h in heads]

        def step(i, carry, diagonal=False):
            off = pl.multiple_of(i * t, t)
            out = []
            for h in heads:
                dk, dv = carry[h]
                qs = q_ref[pl.ds(off, t), h * qw:(h + 1) * qw]
                dos = do_ref[pl.ds(off, t), h * vw:(h + 1) * vw]
                lse_i = lse_ref[pl.ds(off, t), h * vw:h * vw + 1]
                delta_i = delta_ref[pl.ds(off, t), h * vw:h * vw + 1]
                s = lax.dot_general(qs, kbs[h], _NT, preferred_element_type=f32) * _ATT_SCALE
                p = jnp.exp(s - lse_i)
                if diagonal:
                    p = jnp.where(_att_mask(i, j, t), p, 0.0)
                dv = dv + lax.dot_general(p.astype(bf16), dos, _TN, preferred_element_type=f32)
                dp = lax.dot_general(dos, vbs[h], _NT, preferred_element_type=f32)
                ds = p * (dp - delta_i) * _ATT_SCALE
                dk = dk + lax.dot_general(ds.astype(bf16), qs, _TN, preferred_element_type=f32)
                out.append((dk, dv))
            return tuple(out)

        first = step(j, tuple((jnp.zeros((t, qw), f32), jnp.zeros((t, vw), f32)) for _ in heads), diagonal=True)
        res = lax.fori_loop(j + 1, nq, step, first)
        for h in heads:
            dk_ref[:, h * qw:(h + 1) * qw] = res[h][0].astype(dk_ref.dtype)
            dv_ref[:, h * vw:(h + 1) * vw] = res[h][1].astype(dv_ref.dtype)

    dk, dv = pl.pallas_call(
        dkv_body, grid=(nq,), in_specs=[qblk, vblk, qfull, vfull, vfull, vfull], out_specs=[qblk, vblk],
        out_shape=[jax.ShapeDtypeStruct(k.shape, bf16), jax.ShapeDtypeStruct(v.shape, bf16)],
        compiler_params=_cparams(("arbitrary",)), name=name + "_dkv")(k, v, q, do, lse, delta)
    return dq, dk, dv


_C_Q = 4 * HG_WIDTH
_C_KV = _C_Q + MLA_Q_RANK
_C_KPE = _C_KV + MLA_KV_RANK


def _rms_n(x, g, n):
    return x * lax.rsqrt(jnp.sum(x * x, axis=-1, keepdims=True) * (1.0 / n) + EPS) * g


def _mix_a(proj, l0, l1, q_a_norm, kv_a_norm):
    lb = jax.nn.sigmoid(l0 - l1)
    f = lb + (1.0 - lb) * jax.nn.sigmoid(proj[:, HG_WIDTH:2 * HG_WIDTH])
    qf = _silu(proj[:, :HG_WIDTH])
    v = proj[:, 2 * HG_WIDTH:3 * HG_WIDTH]
    cqn = _rms(proj[:, _C_Q:_C_KV], q_a_norm)
    ckvn = _rms(proj[:, _C_KV:_C_KPE], kv_a_norm)
    return qf, 1.0 - f, v, jnp.log(f), cqn, ckvn


def _mix_b(qraw, kvraw, kpe_raw, cos, sin, qn_nope, qn_rope, kn_nope, kn_rope, perm):
    def rope(x):
        return x * cos + jnp.dot(x, perm, precision=lax.Precision.HIGHEST, preferred_element_type=f32) * sin

    kpe = rope(_rms_n(kpe_raw, kn_rope, MLA_ROPE))
    qs, ks, vs = [], [], []
    for hh in range(MLA_HEADS):
        base = hh * MLA_QK_PAD
        qs.append(_rms(qraw[:, base:base + MLA_NOPE], qn_nope))
        qs.append(rope(_rms_n(qraw[:, base + MLA_NOPE:base + MLA_QK_PAD], qn_rope, MLA_ROPE)))
        ks.append(_rms(kvraw[:, base:base + MLA_NOPE], kn_nope))
        ks.append(kpe)
        vs.append(kvraw[:, base + MLA_NOPE:base + MLA_QK_PAD])
    return jnp.concatenate(qs, axis=-1), jnp.concatenate(ks, axis=-1), jnp.concatenate(vs, axis=-1)


def _mix_c(o_hg, gate, o_mla, hg_out_norm):
    parts = []
    for hh in range(HG_HEADS):
        sl = slice(hh * HG_DIM, (hh + 1) * HG_DIM)
        parts.append(_rms(o_hg[:, sl], hg_out_norm[:, sl]))
    o = jnp.concatenate(parts, axis=-1) * _silu(gate)
    return jnp.concatenate([o, o_mla], axis=-1)


def _rope_perm():
    p = np.zeros((128, 128), np.float32)
    half = MLA_ROPE // 2
    for i in range(half):
        p[i + half, i] = -1.0
        p[i, i + half] = 1.0
    return jnp.asarray(p)


def _mixer_fwd(h, cos, sin, w, tag, next_gain):
    d = h.shape[1]
    hn, = _rows(_rms, [h], [w["norm_mix"]], [(d, bf16)], name=f"mix_norm_{tag}")
    proj = _mm(hn, w["mix_w_in"], name=f"mix_in_{tag}")
    pa = [w["lb0"], w["lb1"], w["mla_q_a_norm"], w["mla_kv_a_norm"]]
    qf, kk, vv, logf, cqn, ckvn = _rows(
        _mix_a, [proj], pa, [(HG_WIDTH, f32)] * 4 + [(MLA_Q_RANK, bf16), (MLA_KV_RANK, bf16)], name=f"mix_a_{tag}")
    o_hg, states = _hg_fwd(qf, kk, vv, logf, name=f"hg_fwd_{tag}")
    qraw = _mm(cqn, w["mla_w_uq"], out_dtype=bf16, name=f"mla_uq_{tag}")
    kvraw = _mm(ckvn, w["mla_w_ukv"], out_dtype=bf16, name=f"mla_ukv_{tag}")
    pb = [w["mla_qn_nope"], w["mla_qn_rope"], w["mla_kn_nope"], w["mla_kn_rope"], w["rope_perm"]]
    kpe_raw, gate = _window(proj, _C_KPE, IN_PAD - _C_KPE), _window(proj, 3 * HG_WIDTH, HG_WIDTH)
    qfull, kfull, vfull = _rows(_mix_b, [qraw, kvraw, kpe_raw, cos, sin], pb,
                                [(MLA_HEADS * MLA_QK_PAD, bf16)] * 2 + [(MLA_HEADS * MLA_V, bf16)],
                                name=f"mix_b_{tag}")
    o_mla, lse = _att_fwd(qfull, kfull, vfull, name=f"att_fwd_{tag}")
    mixin, = _rows(_mix_c, [o_hg, gate, o_mla], [w["hg_out_norm"]], [(d, bf16)], name=f"mix_c_{tag}")
    if callable(w["mix_w_out"]):
        w["mix_w_out"] = w["mix_w_out"](mixin)
    out, normed = _mm(mixin, w["mix_w_out"], add=h, norm=next_gain, name=f"mix_out_{tag}")
    return out, (h, hn, proj, qf, kk, vv, logf, cqn, ckvn, o_hg, states, qraw, kvraw, qfull, kfull, vfull, o_mla,
                 lse, mixin), normed


def _mixer_bwd(dout, cos, sin, w, saved, tag, on_w_out=None):
    (h, hn, proj, qf, kk, vv, logf, cqn, ckvn, o_hg, states, qraw, kvraw, qfull, kfull, vfull, o_mla, lse,
     mixin) = saved
    g = {}
    dmixin = _mm(dout, w["mix_w_out"], tb=True, out_dtype=bf16, name=f"mix_dmixin_{tag}")
    g["mix_w_out"] = _mm(mixin, dout, ta=True, out_dtype=bf16, name=f"mix_dwout_{tag}")
    if on_w_out is not None:
        dmixin = on_w_out(g["mix_w_out"], dmixin)
    kpe_raw, gate = _window(proj, _C_KPE, IN_PAD - _C_KPE), _window(proj, 3 * HG_WIDTH, HG_WIDTH)
    (do_hg, dgate, do_mla), (g["hg_out_norm"],) = _rows_bwd(
        _mix_c, [o_hg, gate, o_mla], [w["hg_out_norm"]], [dmixin], rgrad=[f32, f32, bf16], pgrad=[True],
        name=f"mix_dc_{tag}")
    dqfull, dkfull, dvfull = _att_bwd(qfull, kfull, vfull, o_mla, lse, do_mla, name=f"att_bwd_{tag}")
    pb = [w["mla_qn_nope"], w["mla_qn_rope"], w["mla_kn_nope"], w["mla_kn_rope"], w["rope_perm"]]
    (dqraw, dkvraw, dkpe_raw), pg = _rows_bwd(
        _mix_b, [qraw, kvraw, kpe_raw, cos, sin], pb, [dqfull, dkfull, dvfull],
        rgrad=[bf16, bf16, f32, None, None], pgrad=[True, True, True, True, False],
        name=f"mix_db_{tag}")
    g["mla_qn_nope"], g["mla_qn_rope"], g["mla_kn_nope"], g["mla_kn_rope"] = pg
    dcqn = _mm(dqraw, w["mla_w_uq"], tb=True, name=f"mla_dcq_{tag}")
    g["mla_w_uq"] = _mm(cqn, dqraw, ta=True, name=f"mla_dwuq_{tag}")
    dckvn = _mm(dkvraw, w["mla_w_ukv"], tb=True, name=f"mla_dckv_{tag}")
    g["mla_w_ukv"] = _mm(ckvn, dkvraw, ta=True, name=f"mla_dwukv_{tag}")
    dqf, dkk, dvv, dlogf = _hg_bwd(qf, kk, vv, logf, states, do_hg, name=f"hg_bwd_{tag}")
    pa = [w["lb0"], w["lb1"], w["mla_q_a_norm"], w["mla_kv_a_norm"]]
    (dproj,), (g["lb0"], g["lb1"], g["mla_q_a_norm"], g["mla_kv_a_norm"]) = _rows_bwd(
        _mix_a, [proj], pa, [dqf, dkk, dvv, dlogf, dcqn, dckvn], rgrad=[bf16], pgrad=[True] * 4,
        addends={0: [(dgate, 3 * HG_WIDTH), (dkpe_raw, _C_KPE)]}, name=f"mix_da_{tag}")
    dh, g["norm_mix"] = _mm(dproj, w["mix_w_in"], tb=True, rms_bwd=(h, w["norm_mix"], dout), name=f"mix_dhn_{tag}")
    g["mix_w_in_t"] = _mm(dproj, hn, ta=True, name=f"mix_dwin_{tag}")
    return dh, g


def _rope_tables(positions):
    inv_freq = 1.0 / (ROPE_BASE ** (jnp.arange(0, MLA_ROPE, 2, dtype=f32) / MLA_ROPE))
    ang = positions.astype(f32)[:, None] * inv_freq
    z = jnp.zeros((positions.shape[0], 128 - MLA_ROPE), f32)
    return (jnp.concatenate([jnp.cos(ang), jnp.cos(ang), z], axis=1),
            jnp.concatenate([jnp.sin(ang), jnp.sin(ang), z], axis=1))


def _pad_cols(a, n):
    return jnp.pad(a, ((0, 0), (0, n - a.shape[1])))


def _even_weights(p, j, layer, dt):
    w_uq = p["mla_w_uq"][j].reshape(MLA_Q_RANK, MLA_HEADS, MLA_QK)
    w_uq = jnp.pad(w_uq, ((0, 0), (0, 0), (0, MLA_QK_PAD - MLA_QK))).reshape(MLA_Q_RANK, MLA_HEADS * MLA_QK_PAD)
    return dict(
        norm_mix=p["norm_mix"][layer][None], mix_w_in=_pad_cols(p["mix_w_in"][j], IN_PAD).astype(dt),
        lb0=p["hg_lb_logits"][0][None], lb1=p["hg_lb_logits"][1][None],
        mla_q_a_norm=p["mla_q_a_norm"][j][None], mla_kv_a_norm=p["mla_kv_a_norm"][j][None],
        mla_w_uq=w_uq.astype(dt), mla_w_ukv=p["mla_w_ukv"][j].astype(dt),
        mla_qn_nope=p["mla_qn_nope"][j][None], mla_qn_rope=_pad_cols(p["mla_qn_rope"][j][None], 128),
        mla_kn_nope=p["mla_kn_nope"][j][None], mla_kn_rope=_pad_cols(p["mla_kn_rope"][j][None], 128),
        rope_perm=_rope_perm(), hg_out_norm=p["hg_out_norm"][j][None],
        mix_w_out=p["mix_w_out"][j].astype(dt) if "mix_w_out" in p else None)


def _even_grads(g):
    w_uq = g["mla_w_uq"].reshape(MLA_Q_RANK, MLA_HEADS, MLA_QK_PAD)[:, :, :MLA_QK].reshape(MLA_Q_RANK, -1)
    return dict(
        norm_mix=g["norm_mix"], mix_w_in=g["mix_w_in_t"][:IN_WIDTH].T[None],
        hg_lb_logits=jnp.concatenate([g["lb0"], g["lb1"]], axis=0),
        mla_q_a_norm=g["mla_q_a_norm"], mla_kv_a_norm=g["mla_kv_a_norm"], mla_w_uq=w_uq[None],
        mla_w_ukv=g["mla_w_ukv"][None], mla_qn_nope=g["mla_qn_nope"], mla_qn_rope=g["mla_qn_rope"][:, :MLA_ROPE],
        mla_kn_nope=g["mla_kn_nope"], mla_kn_rope=g["mla_kn_rope"][:, :MLA_ROPE],
        hg_out_norm=g["hg_out_norm"], mix_w_out=g["mix_w_out"][None])


_S5_NB = 8
_S5_BW = 1024
_S5_HALF = 512
_S5_UC = 128
_S5_TIME = 1024


def _cmul(ar, ai, br, bi):
    return ar * br - ai * bi, ar * bi + ai * br


def _pow_table(ar, ai, descending):
    rows = lax.broadcasted_iota(jnp.int32, (8, ar.shape[1]), 0)
    tr = jnp.zeros((8, ar.shape[1]), f32)
    ti = jnp.zeros((8, ar.shape[1]), f32)
    pr, pi_ = ar, ai
    for r in range(8):
        sel = rows == ((7 - r) if descending else r)
        tr = jnp.where(sel, pr, tr)
        ti = jnp.where(sel, pi_, ti)
        pr, pi_ = _cmul(pr, pi_, ar, ai)
    return tr, ti


def _s5_tile_scan(work, carry, ar, ai, tc, reverse, per_tile=None):
    hw = _S5_HALF
    row8 = lax.broadcasted_iota(jnp.int32, (8, hw), 0)
    powers = [(ar, ai)]
    for _ in range(2):
        powers.append(_cmul(*powers[-1], *powers[-1]))
    steps = []
    for (mr, mi), s in zip(powers, (1, 2, 4)):
        ok = (row8 < 8 - s) if reverse else (row8 >= s)
        steps.append((jnp.where(ok, mr, 0.0), jnp.where(ok, mi, 0.0), 8 - s if reverse else s))
    tr, ti = _pow_table(ar, ai, reverse)
    cr, ci = carry[:, :hw], carry[:, hw:]
    tiles = range(tc // 8)
    for i in (reversed(tiles) if reverse else tiles):
        sl = slice(8 * i, 8 * i + 8)
        xr, xi = work[sl, :hw], work[sl, hw:]
        for mr, mi, shift in steps:
            pr, pi_ = _cmul(mr, mi, pltpu.roll(xr, shift, axis=0), pltpu.roll(xi, shift, axis=0))
            xr, xi = xr + pr, xi + pi_
        pr, pi_ = _cmul(tr, ti, cr, ci)
        xr, xi = xr + pr, xi + pi_
        work[sl, :hw] = xr
        work[sl, hw:] = xi
        if per_tile is not None:
            per_tile(sl, xr, xi, cr, ci)
        edge = 8 * i if reverse else 8 * i + 7
        cr, ci = work[edge:edge + 1, :hw], work[edge:edge + 1, hw:]
    carry[:, :hw] = cr
    carry[:, hw:] = ci


def _s5_core_fwd(a, hn, b3, c3, *, name):
    length = hn.shape[0]
    tc = min(_S5_TIME, length)

    def body(a_ref, hn_ref, b_ref, c_ref, hs_ref, y_ref, work, carry):
        @pl.when(pl.program_id(1) == 0)
        def _():
            carry[...] = jnp.zeros_like(carry)

        work[...] = jnp.dot(hn_ref[...].astype(bf16), b_ref[...], preferred_element_type=f32)
        _s5_tile_scan(work, carry, a_ref[:, :_S5_HALF], a_ref[:, _S5_HALF:], tc, False)
        hs = work[...].astype(bf16)
        hs_ref[...] = hs
        y_ref[...] = jnp.dot(hs, c_ref[...], preferred_element_type=f32)

    return pl.pallas_call(
        body, grid=(_S5_NB, length // tc),
        in_specs=[pl.BlockSpec((1, _S5_BW), lambda j, t: (0, j)), pl.BlockSpec((tc, _S5_UC), lambda j, t: (t, j)),
                  pl.BlockSpec((_S5_UC, _S5_BW), lambda j, t: (j, 0)), pl.BlockSpec((_S5_BW, _S5_UC), lambda j, t: (j, 0))],
        out_specs=[pl.BlockSpec((tc, _S5_BW), lambda j, t: (t, j)), pl.BlockSpec((tc, _S5_UC), lambda j, t: (t, j))],
        out_shape=[jax.ShapeDtypeStruct((length, _S5_NB * _S5_BW), bf16),
                   jax.ShapeDtypeStruct((length, _S5_NB * _S5_UC), f32)],
        scratch_shapes=[pltpu.VMEM((tc, _S5_BW), f32), pltpu.VMEM((1, _S5_BW), f32)],
        compiler_params=_cparams(("parallel", "arbitrary")), name=name)(a, hn, b3, c3)


def _s5_core_bwd(a, dy, c3, hs, hn, b3, *, name):
    length = hn.shape[0]
    tc = min(_S5_TIME, length)
    nt = length // tc
    hw = _S5_HALF

    def body(a_ref, dy_ref, c_ref, hs_ref, hn_ref, b_ref, du_ref, db_ref, dc_ref, da_ref, work, carry, acc):
        @pl.when(pl.program_id(1) == 0)
        def _():
            carry[...] = jnp.zeros_like(carry)
            db_ref[...] = jnp.zeros_like(db_ref)
            dc_ref[...] = jnp.zeros_like(dc_ref)
            da_ref[...] = jnp.zeros_like(da_ref)

        dyb = dy_ref[...].astype(bf16)
        work[...] = lax.dot_general(dyb, c_ref[...], _NT, preferred_element_type=f32)
        acc[...] = jnp.zeros_like(acc)
        row8 = lax.broadcasted_iota(jnp.int32, (8, hw), 0)

        def grad_a(sl, gr, gi, cr, ci):
            gnr = jnp.where(row8 == 7, cr, pltpu.roll(gr, 7, axis=0))
            gni = jnp.where(row8 == 7, ci, pltpu.roll(gi, 7, axis=0))
            hr, hi = hs_ref[sl, :hw].astype(f32), hs_ref[sl, hw:].astype(f32)
            acc[:, :hw] += hr * gnr + hi * gni
            acc[:, hw:] += hr * gni - hi * gnr

        _s5_tile_scan(work, carry, a_ref[:, :hw], -a_ref[:, hw:], tc, True, grad_a)
        da_ref[...] += jnp.sum(acc[...], axis=0, keepdims=True)
        g = work[...].astype(bf16)
        du_ref[...] = lax.dot_general(g, b_ref[...], _NT, preferred_element_type=f32)
        db_ref[...] += lax.dot_general(hn_ref[...].astype(bf16), g, _TN, preferred_element_type=f32)
        dc_ref[...] += lax.dot_general(hs_ref[...], dyb, _TN, preferred_element_type=f32)

    rev = lambda j, t: (nt - 1 - t, j)
    return pl.pallas_call(
        body, grid=(_S5_NB, nt),
        in_specs=[pl.BlockSpec((1, _S5_BW), lambda j, t: (0, j)), pl.BlockSpec((tc, _S5_UC), rev),
                  pl.BlockSpec((_S5_BW, _S5_UC), lambda j, t: (j, 0)), pl.BlockSpec((tc, _S5_BW), rev),
                  pl.BlockSpec((tc, _S5_UC), rev), pl.BlockSpec((_S5_UC, _S5_BW), lambda j, t: (j, 0))],
        out_specs=[pl.BlockSpec((tc, _S5_UC), rev), pl.BlockSpec((_S5_UC, _S5_BW), lambda j, t: (j, 0)),
                   pl.BlockSpec((_S5_BW, _S5_UC), lambda j, t: (j, 0)), pl.BlockSpec((1, _S5_BW), lambda j, t: (0, j))],
        out_shape=[jax.ShapeDtypeStruct((length, _S5_NB * _S5_UC), f32),
                   jax.ShapeDtypeStruct((_S5_NB * _S5_UC, _S5_BW), f32),
                   jax.ShapeDtypeStruct((_S5_NB * _S5_BW, _S5_UC), f32),
                   jax.ShapeDtypeStruct((1, _S5_NB * _S5_BW), f32)],
        scratch_shapes=[pltpu.VMEM((tc, _S5_BW), f32), pltpu.VMEM((1, _S5_BW), f32), pltpu.VMEM((8, _S5_BW), f32)],
        compiler_params=_cparams(("parallel", "arbitrary")), name=name)(a, dy, c3, hs, hn, b3)


def _s5_disc(lr, li, ldt, btr, bti, expand):
    dt = jnp.exp(ldt)
    mag = jnp.exp(lr * dt)
    abr = mag * jnp.cos(li * dt)
    abi = mag * jnp.sin(li * dt)
    den = lr * lr + li * li
    zr = ((abr - 1.0) * lr + abi * li) / den
    zi = (abi * lr - (abr - 1.0) * li) / den
    zr = jnp.dot(zr, expand, precision=lax.Precision.HIGHEST, preferred_element_type=f32)
    zi = jnp.dot(zi, expand, precision=lax.Precision.HIGHEST, preferred_element_type=f32)
    return abr, abi, zr * btr - zi * bti, zr * bti + zi * btr


def _s5_disc_fwd(args, *, name):
    def body(*refs):
        res = _s5_disc(*[r[...] for r in refs[:6]])
        for o, v in zip(refs[6:], res):
            o[...] = v

    sds = jax.ShapeDtypeStruct
    return pl.pallas_call(body, out_shape=[sds(args[0].shape, f32)] * 2 + [sds(args[3].shape, f32)] * 2,
                          name=name)(*args)


def _s5_disc_bwd(args, cts, *, name):
    def body(*refs):
        vals = [r[...] for r in refs[:6]]
        _, vjp = jax.vjp(lambda *d: _s5_disc(*d, vals[5]), *vals[:5])
        grads = vjp(tuple(r[...] for r in refs[6:10]))
        for o, v in zip(refs[10:], grads):
            o[...] = v

    return pl.pallas_call(body, out_shape=[jax.ShapeDtypeStruct(a.shape, f32) for a in args[:5]],
                          name=name)(*args, *cts)


def _gelu_tanh(x):
    return 0.5 * x * (1.0 + jnp.tanh(0.7978845608028654 * (x + 0.044715 * (x * x * x))))


def _s5_post(y, u, d_skip):
    return _gelu_tanh(y + d_skip * u)


def _s5_glu(ga, gb, h):
    return h + ga * jax.nn.sigmoid(gb)


def _s5_glu_norm(ga, gb, h, next_gain):
    out = _s5_glu(ga, gb, h)
    return out, _rms(out, next_gain)


def _s5_expand():
    e = np.zeros((S5_STATE, S5_GROUP * S5_STATE), np.float32)
    for m in range(S5_GROUP):
        e[np.arange(S5_STATE), m * S5_STATE + np.arange(S5_STATE)] = 1.0
    return jnp.asarray(e)


def _s5_pack_b(bbr, bbi):
    eye = jnp.eye(8, dtype=f32)

    def one(bb):
        b5 = bb.reshape(_S5_NB, 8, S5_GROUP, S5_STATE)
        return jnp.einsum("jgmp,gh->jgmhp", b5, eye).reshape(_S5_NB * _S5_UC, _S5_HALF)

    return jnp.concatenate([one(bbr), one(bbi)], axis=1)


def _s5_unpack_b(db3):
    def one(d):
        d5 = d.reshape(_S5_NB, 8, S5_GROUP, 8, S5_STATE)
        return jnp.einsum("jgmgp->jgmp", d5).reshape(S5_GROUPS, S5_GROUP * S5_STATE)

    return one(db3[:, :_S5_HALF]), one(db3[:, _S5_HALF:])


def _s5_pack_c(c_re, c_im):
    eye = jnp.eye(8, dtype=f32)

    def one(c):
        c4 = c.reshape(_S5_NB, 8, S5_GROUP, S5_STATE)
        return jnp.einsum("jgmp,hg->jhpgm", c4, eye).reshape(_S5_NB, _S5_HALF, _S5_UC)

    return jnp.concatenate([one(c_re), -one(c_im)], axis=1).reshape(_S5_NB * _S5_BW, _S5_UC)


def _s5_unpack_c(dc3):
    d = dc3.reshape(_S5_NB, 2, 8, S5_STATE, 8, S5_GROUP)
    dre = jnp.einsum("jgpgm->jgmp", d[:, 0]).reshape(S5_GROUPS, S5_GROUP, S5_STATE)
    dim = -jnp.einsum("jgpgm->jgmp", d[:, 1]).reshape(S5_GROUPS, S5_GROUP, S5_STATE)
    return dre, dim


def _s5_state_row(re, im):
    r = re.reshape(_S5_NB, 1, _S5_HALF)
    i = im.reshape(_S5_NB, 1, _S5_HALF)
    return jnp.concatenate([r, i], axis=2).reshape(1, _S5_NB * _S5_BW)


def _s5_unstate_row(row):
    r = row.reshape(_S5_NB, 2, 8, S5_STATE)
    return r[:, 0].reshape(S5_GROUPS, S5_STATE), r[:, 1].reshape(S5_GROUPS, S5_STATE)


def _s5_fwd(h, w, tag, next_gain):
    d = h.shape[1]
    hn, = _rows(_rms, [h], [w["norm_mix"]], [(d, f32)], name=f"s5_norm_{tag}")
    disc_in = [w["s5_lam_re"], w["s5_lam_im"], w["s5_log_dt"], w["s5_bt_re"], w["s5_bt_im"], w["s5_expand"]]
    abr, abi, bbr, bbi = _s5_disc_fwd(disc_in, name=f"s5_disc_{tag}")
    a_row = _s5_state_row(abr, abi)
    b3 = _s5_pack_b(bbr, bbi).astype(bf16)
    hs, y = _s5_core_fwd(a_row, hn, b3, w["s5_c3"], name=f"s5_core_{tag}")
    yg, = _rows(_s5_post, [y, hn], [w["s5_d"]], [(d, bf16)], name=f"s5_post_{tag}")
    ga = _mm(yg, w["s5_w_glu_a"], out_dtype=bf16, name=f"s5_glu_a_{tag}")
    gb = _mm(yg, w["s5_w_glu_b"], out_dtype=bf16, name=f"s5_glu_b_{tag}")
    out, normed = _rows(_s5_glu_norm, [ga, gb, h], [next_gain], [(d, f32), (d, bf16)], name=f"s5_glu_{tag}")
    return out, (h, hn, disc_in, a_row, b3, hs, y, yg, ga, gb), normed


def _s5_bwd(dout, w, saved, tag):
    h, hn, disc_in, a_row, b3, hs, y, yg, ga, gb = saved
    g = {}
    (dga, dgb), _ = _rows_bwd(_s5_glu, [ga, gb, h], [], [dout], rgrad=[bf16, bf16, None], pgrad=[],
                              name=f"s5_dglu_{tag}")
    dyg = _mm(dga, w["s5_w_glu_a"], tb=True, name=f"s5_dyg_a_{tag}")
    dyg = _mm(dgb, w["s5_w_glu_b"], tb=True, add=dyg, out_dtype=bf16, name=f"s5_dyg_b_{tag}")
    g["s5_w_glu_a"] = _mm(yg, dga, ta=True, out_dtype=bf16, name=f"s5_dwa_{tag}")
    g["s5_w_glu_b"] = _mm(yg, dgb, ta=True, out_dtype=bf16, name=f"s5_dwb_{tag}")
    (dy, du_skip), (g["s5_d"],) = _rows_bwd(_s5_post, [y, hn], [w["s5_d"]], [dyg], rgrad=[bf16, f32], pgrad=[True],
                                           name=f"s5_dpost_{tag}")
    du, db3, dc3, da_row = _s5_core_bwd(a_row, dy, w["s5_c3"], hs, hn, b3, name=f"s5_dcore_{tag}")
    dabr, dabi = _s5_unstate_row(da_row)
    dbbr, dbbi = _s5_unpack_b(db3)
    g["s5_lam_re"], g["s5_lam_im"], g["s5_log_dt"], g["s5_bt_re"], g["s5_bt_im"] = _s5_disc_bwd(
        disc_in, [dabr, dabi, dbbr, dbbi], name=f"s5_ddisc_{tag}")
    g["s5_c_re"], g["s5_c_im"] = _s5_unpack_c(dc3)
    (dh,), (g["norm_mix"],) = _rows_bwd(_rms_twice, [h], [w["norm_mix"]], [du, du_skip], rgrad=[f32], pgrad=[True],
                                        addends={0: dout}, name=f"s5_dnorm_{tag}")
    return dh, g


def _odd_weights(p, j, layer, dt):
    tr = lambda b: b.transpose(0, 2, 1).reshape(S5_GROUPS, S5_GROUP * S5_STATE)
    return dict(
        norm_mix=p["norm_mix"][layer][None], s5_lam_re=p["s5_lam_re"][j], s5_lam_im=p["s5_lam_im"][j],
        s5_log_dt=p["s5_log_dt"][j][:, None], s5_bt_re=tr(p["s5_b_re"][j]), s5_bt_im=tr(p["s5_b_im"][j]),
        s5_expand=_s5_expand(), s5_c3=_s5_pack_c(p["s5_c_re"][j], p["s5_c_im"][j]).astype(dt),
        **{n: (p[n][j][None] if n == "s5_d" else p[n][j].astype(dt))
           for n in ("s5_d", "s5_w_glu_a", "s5_w_glu_b") if n in p})


def _odd_grads(g):
    tr = lambda b: b.reshape(S5_GROUPS, S5_GROUP, S5_STATE).transpose(0, 2, 1)[None]
    return dict(
        norm_mix=g["norm_mix"], s5_lam_re=g["s5_lam_re"][None], s5_lam_im=g["s5_lam_im"][None],
        s5_log_dt=g["s5_log_dt"][:, 0][None], s5_b_re=tr(g["s5_bt_re"]), s5_b_im=tr(g["s5_bt_im"]),
        s5_c_re=g["s5_c_re"][None], s5_c_im=g["s5_c_im"][None], s5_d=g["s5_d"],
        s5_w_glu_a=g["s5_w_glu_a"][None], s5_w_glu_b=g["s5_w_glu_b"][None])


FF_SHARD = 352
FF_SHARD_PAD = 384


def _pad_groups(a, axis):
    axis %= a.ndim
    zeros = jnp.zeros(a.shape[:axis] + (FF_SHARD_PAD - FF_SHARD,) + a.shape[axis + 1:], a.dtype)
    pieces = []
    for g in range(a.shape[axis] // FF_SHARD):
        pieces += [lax.slice_in_dim(a, g * FF_SHARD, (g + 1) * FF_SHARD, axis=axis), zeros]
    return jnp.concatenate(pieces, axis=axis)


def _unpad_groups(a, axis):
    axis %= a.ndim
    pieces = [lax.slice_in_dim(a, g * FF_SHARD_PAD, g * FF_SHARD_PAD + FF_SHARD, axis=axis)
              for g in range(a.shape[axis] // FF_SHARD_PAD)]
    return pieces[0] if len(pieces) == 1 else jnp.concatenate(pieces, axis=axis)


_LANES = 1024
_ROW_PAD = 16


_PEER_MASKS = (1, 2, 4, 3, 5, 6, 7)


def _mesh_place():
    x, y, c = lax.axis_index("x"), lax.axis_index("y"), lax.axis_index("c")

    def peer(mask):
        px = 1 - x if mask & 4 else x
        py = 1 - y if mask & 2 else y
        pc = 1 - c if mask & 1 else c
        return (px, py, pc), 4 * px + 2 * py + pc

    return 4 * x + 2 * y + c, peer


class _Exchange:
    def __init__(self, name):
        self.name = name
        self.srcs, self.shapes, self.items, self.where = [], [], [], {}

    def add(self, src, land_shape, src_at, dst_at, key):
        si = next((i for i, s in enumerate(self.srcs) if s is src), None)
        if si is None:
            self.srcs.append(src)
            si = len(self.srcs) - 1
        if key not in self.where:
            self.shapes.append(land_shape)
            self.where[key] = len(self.shapes) - 1
        self.items.append(dict(src=si, dst=self.where[key], src_at=src_at, dst_at=dst_at))

    def _copy(self, k, mask, ins, lands, send_sems, recv_sems, me, peer, arriving):
        it = self.items[k]
        dev, idx = peer(mask)
        s = k * (N_DEV - 1) + _PEER_MASKS.index(mask)
        return pltpu.make_async_remote_copy(
            src_ref=it["src_at"](ins[it["src"]], idx), dst_ref=it["dst_at"](lands[it["dst"]], idx if arriving else me),
            send_sem=send_sems.at[s], recv_sem=recv_sems.at[s], device_id=dev, device_id_type=pl.DeviceIdType.MESH)

    def _own_copy(self, k, ins, lands, own_sems, me):
        it = self.items[k]
        return pltpu.make_async_copy(it["src_at"](ins[it["src"]], me), it["dst_at"](lands[it["dst"]], me), own_sems.at[k])

    def begin(self, own):
        ns, nd, ni = len(self.srcs), len(self.shapes), len(self.items)
        nsem = ni * (N_DEV - 1)
        self.own = own

        nq = 3 if own else 2

        def body(*refs):
            ins, land_refs = refs[:ns], refs[ns:ns + nd]
            sems, token = refs[ns + nd:ns + nd + nq], refs[-1]
            me, peer = _mesh_place()
            for mask in _PEER_MASKS:
                for k in range(ni):
                    self._copy(k, mask, ins, land_refs, sems[0], sems[1], me, peer, False).start()
            if own:
                for k in range(ni):
                    self._own_copy(k, ins, land_refs, sems[2], me).start()
            token[...] = jnp.zeros_like(token)

        hbm = pl.BlockSpec(memory_space=pltpu.HBM)
        sem = pl.BlockSpec(memory_space=pltpu.SEMAPHORE)
        lands = [lax.empty(s.shape, s.dtype) for s in self.shapes]
        sem_shapes = [pltpu.SemaphoreType.DMA((nsem,)), pltpu.SemaphoreType.DMA((nsem,)), pltpu.SemaphoreType.DMA((ni,))]
        res = pl.pallas_call(
            body, in_specs=[hbm] * (ns + nd),
            out_specs=[sem] * nq + [hbm] * nd + [pl.BlockSpec(memory_space=pltpu.VMEM)],
            out_shape=sem_shapes[:nq] + [pltpu.HBM(s.shape, s.dtype) for s in self.shapes]
            + [jax.ShapeDtypeStruct((8, 128), f32)],
            input_output_aliases={ns + j: nq + j for j in range(nd)},
            compiler_params=pltpu.CompilerParams(has_side_effects=pltpu.SideEffectType.DATAFLOW_SIDE_EFFECTING),
            name=self.name + "_start")(*self.srcs, *lands)
        self.token = res[-1]
        return list(res[:nq]), list(res[nq:-1])

    def finish(self, state, after):
        sems, lands = state
        nq = len(sems)
        after = list(after) if isinstance(after, (list, tuple)) else [after]
        ns, nd, ni = len(self.srcs), len(self.shapes), len(self.items)

        def body(*refs):
            ins, land_refs = refs[:ns], refs[ns:ns + nd]
            sem_refs = refs[ns + nd:ns + nd + nq]
            me, peer = _mesh_place()
            for mask in _PEER_MASKS:
                for k in range(ni):
                    cp = self._copy(k, mask, ins, land_refs, sem_refs[0], sem_refs[1], me, peer, True)
                    cp.wait_send()
                    cp.wait_recv()
            if self.own:
                for k in range(ni):
                    self._own_copy(k, ins, land_refs, sem_refs[2], me).wait()

        hbm = pl.BlockSpec(memory_space=pltpu.HBM)
        sem = pl.BlockSpec(memory_space=pltpu.SEMAPHORE)
        res = pl.pallas_call(
            body, in_specs=[hbm] * (ns + nd) + [sem] * nq + [pl.BlockSpec(memory_space=pl.ANY)] * len(after),
            out_specs=[hbm] * nd, out_shape=[pltpu.HBM(s.shape, s.dtype) for s in self.shapes],
            input_output_aliases={ns + j: j for j in range(nd)},
            compiler_params=pltpu.CompilerParams(has_side_effects=pltpu.SideEffectType.DATAFLOW_SIDE_EFFECTING),
            name=self.name + "_wait")(*self.srcs, *lands, *sems, *after)
        return {k: res[i] for k, i in self.where.items()}


def _after(x, *tokens, name):
    def body(*refs):
        del refs

    anyspace = pl.BlockSpec(memory_space=pl.ANY)
    return pl.pallas_call(body, in_specs=[anyspace] * (1 + len(tokens)), out_specs=anyspace,
                          out_shape=jax.ShapeDtypeStruct(x.shape, x.dtype), input_output_aliases={0: 0},
                          name=name)(x, *tokens)


def _rows_of(n):
    return lambda r, i: r.at[pl.ds(pl.multiple_of(i * n, n), n), :]


def _cols_of(n):
    return lambda r, i: r.at[:, pl.ds(pl.multiple_of(i * n, n), n)]


def _whole(r, i):
    return r


def _slot(r, i):
    return r.at[i]


def _at_layer(layer):
    return lambda r, i: r.at[layer]


def _sum_adam(me_index, slots, owns, own_block, w, m, v, *, name):
    layers, rows, cols = w.shape
    tr = _pick(rows, (256, 128, 104, 64, 32, 16, 8))
    bc1 = 1.0 - ADAM_B1 ** ADAM_STEP
    bc2 = 1.0 - ADAM_B2 ** ADAM_STEP
    own_shape, own_map = own_block(tr)
    nl = len(slots)
    assert nl == layers and len(owns) == layers

    def body(me_ref, *refs):
        s_refs, own_refs = refs[:nl], refs[nl:2 * nl]
        w_ref, m_ref, v_ref, g_ref, d_ref, nm_ref, nv_ref = refs[2 * nl:]
        me = me_ref[0]

        def run(s_ref, own_ref):
            mine = (own_ref[0] if len(own_shape) == 3 else own_ref[...]).astype(f32)
            g = jnp.where(me == 0, mine, s_ref[0].astype(f32))
            for k in range(1, N_DEV):
                g = g + jnp.where(me == k, mine, s_ref[k].astype(f32))
            mm = ADAM_B1 * m_ref[0] + (1.0 - ADAM_B1) * g
            vv = ADAM_B2 * v_ref[0] + (1.0 - ADAM_B2) * (g * g)
            g_ref[0] = g
            nm_ref[0] = mm
            nv_ref[0] = vv
            d_ref[0] = -ADAM_LR * ((mm / bc1) / (jnp.sqrt(vv / bc2) + ADAM_EPS) + ADAM_WD * w_ref[0])

        for layer in range(nl):
            pl.when(pl.program_id(0) == layer)(functools.partial(run, s_refs[layer], own_refs[layer]))

    def of_layer(layer, index_map):
        return lambda lyr, i, me: index_map(jnp.where(lyr == layer, i, 0), me)

    blk = pl.BlockSpec((1, tr, cols), lambda lyr, i, me: (lyr, i, 0))
    sds = jax.ShapeDtypeStruct((layers, rows, cols), f32)
    grid_spec = pltpu.PrefetchScalarGridSpec(
        num_scalar_prefetch=1, grid=(layers, rows // tr),
        in_specs=[pl.BlockSpec((N_DEV, tr, cols), of_layer(layer, lambda i, me: (0, i, 0))) for layer in range(nl)]
        + [pl.BlockSpec(own_shape, of_layer(layer, own_map)) for layer in range(nl)] + [blk, blk, blk],
        out_specs=[blk] * 4)
    return pl.pallas_call(body, grid_spec=grid_spec, out_shape=[sds] * 4,
                          compiler_params=_cparams(("arbitrary", "arbitrary")),
                          name=name)(me_index, *slots, *owns, w, m, v)


_SHARDED = dict(xa_wq=1, xa_wk=1, xa_wv=1, xa_wo=1, ffn_w_up=2, ffn_conv_w=2, ffn_w_down=1, mix_w_in=2, mla_w_uq=2,
                mla_w_ukv=2, mix_w_out=1, s5_d=1, s5_w_glu_a=1, s5_w_glu_b=1)
_EXACT = ("ffn_conv_w", "s5_d")
_WEIGHTS = ("norm_mix", "norm_xa", "norm_mem", "norm_ffn", "xa_wq", "xa_wk", "xa_wv", "xa_wo", "xa_q_norm",
            "xa_k_norm", "ffn_w_up", "ffn_conv_w", "ffn_conv_b", "ffn_w_down", "hg_lb_logits", "mix_w_in",
            "hg_out_norm", "mla_q_a_norm", "mla_w_uq", "mla_kv_a_norm", "mla_w_ukv", "mla_qn_nope", "mla_qn_rope",
            "mla_kn_nope", "mla_kn_rope", "mix_w_out", "s5_lam_re", "s5_lam_im", "s5_log_dt", "s5_b_re", "s5_b_im",
            "s5_c_re", "s5_c_im", "s5_d", "s5_w_glu_a", "s5_w_glu_b")
_BIG = tuple(n for n in _WEIGHTS if n in _SHARDED and n not in _EXACT)
_SHARD_ORDER = tuple(n for n in _WEIGHTS if n in _SHARDED)
_REPL_ORDER = tuple(n for n in _WEIGHTS if n not in _SHARDED)
_REPL_EARLY = tuple(n for n in _REPL_ORDER if n.startswith("s5_"))
_REPL_LATE = tuple(n for n in _REPL_ORDER if n not in _REPL_EARLY)


def _pack(parts, dtype, lead=None):
    nl = 0 if lead is None else 1
    flat = [a.astype(dtype).reshape(a.shape[:nl] + (-1,)) for a in parts]
    cat = jnp.concatenate(flat, axis=nl)
    n = cat.shape[nl]
    unit = _LANES * _ROW_PAD
    total = -(-n // unit) * unit
    cat = jnp.pad(cat, [(0, 0)] * nl + [(0, total - n)])
    return cat.reshape(cat.shape[:nl] + (total // _LANES, _LANES))


def _unpack(packed, shapes, lead=None):
    nl = 0 if lead is None else 1
    flat = packed.reshape(packed.shape[:nl] + (-1,))
    out, off = [], 0
    for s in shapes:
        n = int(np.prod(s))
        piece = flat[..., off:off + n] if nl else flat[off:off + n]
        out.append(piece.reshape(packed.shape[:nl] + tuple(s)))
        off += n
    return out


def _to_full(gathered, axis):
    g = jnp.moveaxis(gathered, 0, axis)
    s = g.shape
    return g.reshape(s[:axis] + (s[axis] * s[axis + 1],) + s[axis + 2:])


def _to_shards(full, axis):
    s = full.shape
    g = full.reshape(s[:axis] + (N_DEV, s[axis] // N_DEV) + s[axis + 1:])
    return jnp.moveaxis(g, axis, 0)


_DIRECT_ROWS = ("xa_wq", "xa_wk", "xa_wv", "xa_wo", "mix_w_out", "s5_w_glu_a", "s5_w_glu_b")
_SMALL16 = ("mix_w_in", "mla_w_uq", "mla_w_ukv")
_SMALL_SHARDED = ("mla_w_uq", "mla_w_ukv") + _EXACT
_SHARD_ROWS = 128


def _exchange_layout(d):
    out = dict(d)
    out["ffn_w_up"] = _pad_groups(d["ffn_w_up"], 2)
    out["ffn_conv_w"] = _pad_groups(d["ffn_conv_w"], 2)
    out["ffn_w_down"] = _pad_groups(d["ffn_w_down"], 1)
    return out


def _train_step(x, mem, positions, target, w, m, v):
    d_model = x.shape[1]
    we_, me_, ve_ = _exchange_layout(w), _exchange_layout(m), _exchange_layout(v)
    sds = jax.ShapeDtypeStruct

    matrices = _DIRECT_ROWS + ("ffn_w_up", "ffn_w_down")
    layer_mats = ("xa_wq", "xa_wk", "xa_wv", "xa_wo", "ffn_w_up", "ffn_w_down")
    shard16 = {n: we_[n].astype(bf16) for n in matrices}
    part_of = {n: _rows_of(_SHARD_ROWS) for n in _DIRECT_ROWS}
    part_of["ffn_w_up"] = _cols_of(we_["ffn_w_up"].shape[2])
    part_of["ffn_w_down"] = _rows_of(we_["ffn_w_down"].shape[1])
    part_shape = {n: we_[n].shape[1:] for n in matrices}

    def full_shape(n):
        r, c = part_shape[n]
        return (r, N_DEV * c) if n == "ffn_w_up" else (N_DEV * r, c)

    def gather(ex, n, layer):
        ex.add(shard16[n], sds(full_shape(n), bf16), _at_layer(layer), part_of[n], (n, layer))

    def scatter(ex, n, layer, grad):
        ex.add(grad, sds((N_DEV,) + part_shape[n], grad.dtype), part_of[n], _slot, (n, layer))

    small16 = _pack([we_[n] for n in _SMALL16], bf16)
    exact = _pack([we_[n] for n in _EXACT], f32)
    ga, ga1, gb, gc = _Exchange("gather_a"), _Exchange("gather_a1"), _Exchange("gather_b"), _Exchange("gather_c")
    ga.add(small16, sds((N_DEV,) + small16.shape, bf16), _whole, _slot, "small16")
    ga1.add(exact, sds((N_DEV,) + exact.shape, f32), _whole, _slot, "exact")
    gather(ga1, "mix_w_out", 0)
    for n in layer_mats:
        gather(gb, n, 0)
    gather(gc, "s5_w_glu_a", 0)
    gather(gc, "s5_w_glu_b", 0)
    for n in layer_mats:
        gather(gc, n, 1)
    state_a, state_a1, state_b, state_c = ga.begin(True), ga1.begin(True), gb.begin(True), gc.begin(True)

    p = {n: w[n] for n in _REPL_ORDER}
    cos, sin = _rope_tables(positions)
    wo = _odd_weights(p, 0, 1, bf16)
    conv_b = _pad_groups(w["ffn_conv_b"], 1)
    prepared = [cos, sin, conv_b, wo["s5_c3"], wo["s5_bt_re"], wo["s5_bt_im"]]
    full = ga.finish(state_a, [ga1.token, gb.token, gc.token] + prepared)
    for n, a in zip(_SMALL16, _unpack(full["small16"], [we_[n].shape for n in _SMALL16], lead=True)):
        p[n] = _to_full(a, _SHARDED[n])
    we = _even_weights(p, 0, 0, bf16)
    we["norm_mix"] = _after(we["norm_mix"], ga.token, ga1.token, gb.token, gc.token, name="after_gather_starts")

    def late_mix_w_out(mixin):
        full.update(ga1.finish(state_a1, [mixin]))
        return full[("mix_w_out", 0)]

    we["mix_w_out"] = late_mix_w_out
    h, s_mix0, hx = _mixer_fwd(x, cos, sin, we, "l0", w["norm_xa"][0][None])
    conv_w, s5_d = [_to_full(a, _SHARDED[n]) for n, a in
                    zip(_EXACT, _unpack(full["exact"], [we_[n].shape for n in _EXACT], lead=True))]

    def layer_weights(layer):
        return dict(norm_xa=w["norm_xa"][layer][None], norm_mem=w["norm_mem"][layer][None],
                    norm_ffn=w["norm_ffn"][layer][None], xa_q_norm=w["xa_q_norm"][layer][None],
                    xa_k_norm=w["xa_k_norm"][layer][None], ffn_conv_w=conv_w[layer],
                    ffn_conv_b=conv_b[layer][None], **{n: full[(n, layer)] for n in layer_mats})

    full.update(gb.finish(state_b, h))
    wl = [layer_weights(0)]
    h, s_xa0, hf = _xattn_fwd(h, mem, wl[0], "l0", hx)
    h, s_ff0 = _ffn_fwd(h, wl[0], "l0", hf)
    full.update(gc.finish(state_c, h))
    wl.append(layer_weights(1))
    wo.update(s5_d=s5_d, s5_w_glu_a=full[("s5_w_glu_a", 0)], s5_w_glu_b=full[("s5_w_glu_b", 0)])
    h, s_mix1, hx = _s5_fwd(h, wo, "l1", wl[1]["norm_xa"])
    h, s_xa1, hf = _xattn_fwd(h, mem, wl[1], "l1", hx)
    (dh, loss), s_ff1 = _ffn_fwd(h, wl[1], "l1", hf, loss_target=target)

    gl = [{}, {}]
    reduces = []

    own_grad = {}

    def reduce_start(name, entries, dh):
        ex = _Exchange(name)
        for n, layer, grad in entries.get("matrices", ()):
            scatter(ex, n, layer, grad)
            own_grad[(n, layer)] = grad
        for key, src, shape, src_at in entries.get("packs", ()):
            ex.add(src, shape, src_at, _slot, key)
        reduces.append((ex, ex.begin(False)))
        return _after(dh, ex.token, name="after_" + name)

    dh, gl[1] = _ffn_bwd(dh, wl[1], s_ff1, "l1")
    dh = reduce_start("reduce_ffn1", dict(matrices=[(n, 1, gl[1][n]) for n in ("ffn_w_up", "ffn_w_down")]), dh)
    dh, g = _xattn_bwd(dh, mem, wl[1], s_xa1, "l1")
    gl[1].update(g)
    dh = reduce_start("reduce_xa1", dict(matrices=[(n, 1, g[n]) for n in ("xa_wq", "xa_wk", "xa_wv", "xa_wo")]), dh)
    dh, g_odd = _s5_bwd(dh, wo, s_mix1, "l1")
    go = _odd_grads(g_odd)
    dh, gl[0] = _ffn_bwd(dh, wl[0], s_ff0, "l0")
    send_early = _pack([go[n].reshape(w[n].shape) for n in _REPL_EARLY], f32)
    dh = reduce_start("reduce_ffn0", dict(
        matrices=[(n, 0, g_odd[n]) for n in ("s5_w_glu_a", "s5_w_glu_b")]
        + [(n, 0, gl[0][n]) for n in ("ffn_w_up", "ffn_w_down")],
        packs=[("repl_early", send_early, sds((N_DEV,) + send_early.shape, f32), _whole)]), dh)
    dh, g = _xattn_bwd(dh, mem, wl[0], s_xa0, "l0")
    gl[0].update(g)
    dh = reduce_start("reduce_xa0", dict(matrices=[(n, 0, g[n]) for n in ("xa_wq", "xa_wk", "xa_wv", "xa_wo")]), dh)
    grad_x, g_even = _mixer_bwd(
        dh, cos, sin, we, s_mix0, "l0",
        on_w_out=lambda grad, dmixin: reduce_start("reduce_w_out", dict(matrices=[("mix_w_out", 0, grad)]), dmixin))

    ge = _even_grads(g_even)
    cat = lambda n: jnp.concatenate([gl[0][n], gl[1][n]], axis=0)
    rg = dict(ge)
    rg["norm_mix"] = jnp.concatenate([ge["norm_mix"], go["norm_mix"]], axis=0)
    for n in ("norm_xa", "norm_mem", "norm_ffn", "xa_q_norm", "xa_k_norm"):
        rg[n] = cat(n)
    rg["ffn_conv_b"] = _unpad_groups(cat("ffn_conv_b"), 1)
    sg = dict(mla_w_uq=ge["mla_w_uq"], mla_w_ukv=ge["mla_w_ukv"], s5_d=go["s5_d"],
              ffn_conv_w=jnp.stack([gl[0]["ffn_conv_w"], gl[1]["ffn_conv_w"]]))
    send_small = _pack([_to_shards(sg[n], _SHARDED[n]) for n in _SMALL_SHARDED], f32, lead=True)
    send_late = _pack([rg[n].reshape(w[n].shape) for n in _REPL_LATE], f32)
    w_in_rows = w["mix_w_in"].shape[2]
    last = _Exchange("reduce_last")
    last.add(g_even["mix_w_in_t"], sds((N_DEV, w_in_rows, d_model), f32), _rows_of(w_in_rows), _slot, "mix_w_in")
    last.add(send_small, sds(send_small.shape, f32), _slot, _slot, "small")
    last.add(send_late, sds((N_DEV,) + send_late.shape, f32), _whole, _slot, "repl_late")
    state_last = last.begin(False)
    slots = {}
    for ex, state in reduces:
        slots.update(ex.finish(state, [grad_x, last.token]))

    me_index = (4 * lax.axis_index("x") + 2 * lax.axis_index("y") + lax.axis_index("c")).astype(jnp.int32).reshape(1)

    def rows_block(r, c):
        return lambda tr: ((tr, c), lambda i, me: (me[0] * (r // tr) + i, 0))

    def own_block(n):
        r, c = part_shape[n]
        if n == "ffn_w_up":
            return lambda tr: ((tr, c), lambda i, me: (i, me[0]))
        return rows_block(r, c)

    out = [{}, {}, {}, {}]
    unpad = dict(ffn_w_up=2, ffn_conv_w=2, ffn_w_down=1)
    for n in matrices:
        layers = range(we_[n].shape[0])
        res = _sum_adam(me_index, [slots[(n, layer)] for layer in layers], [own_grad[(n, layer)] for layer in layers],
                        own_block(n), we_[n], me_[n], ve_[n], name=f"adam_{n}")
        for k in range(4):
            out[k][n] = _unpad_groups(res[k], unpad[n]) if n in unpad else res[k]
    pk = lambda d, order: _pack([d[n] for n in order], f32)[None]
    whole_rows = lambda tr: ((tr, _LANES), lambda i, me: (i, 0))
    res_early = _sum_adam(me_index, [slots["repl_early"]], [send_early], whole_rows, pk(w, _REPL_EARLY),
                          pk(m, _REPL_EARLY), pk(v, _REPL_EARLY), name="adam_repl_early")
    for k in range(4):
        out[k].update(zip(_REPL_EARLY, _unpack(res_early[k][0], [w[n].shape for n in _REPL_EARLY])))
    done = [out[k][n] for k in range(4) for n in matrices + _REPL_EARLY]
    slots = last.finish(state_last, done)
    transposed = lambda d: jnp.swapaxes(d["mix_w_in"], 1, 2)
    res_w_in = _sum_adam(me_index, [slots["mix_w_in"]], [g_even["mix_w_in_t"]], rows_block(w_in_rows, d_model),
                         transposed(w), transposed(m), transposed(v), name="adam_mix_w_in")
    res_small = _sum_adam(me_index, [slots["small"]], [send_small],
                          lambda tr: ((1, tr, _LANES), lambda i, me: (me[0], i, 0)),
                          pk(we_, _SMALL_SHARDED), pk(me_, _SMALL_SHARDED), pk(ve_, _SMALL_SHARDED), name="adam_small")
    res_late = _sum_adam(me_index, [slots["repl_late"]], [send_late], whole_rows, pk(w, _REPL_LATE), pk(m, _REPL_LATE),
                         pk(v, _REPL_LATE), name="adam_repl_late")
    for k in range(4):
        out[k]["mix_w_in"] = jnp.swapaxes(res_w_in[k], 1, 2)
        for n, a in zip(_SMALL_SHARDED, _unpack(res_small[k][0], [we_[n].shape for n in _SMALL_SHARDED])):
            out[k][n] = _unpad_groups(a, unpad[n]) if n in unpad else a
        out[k].update(zip(_REPL_LATE, _unpack(res_late[k][0], [w[n].shape for n in _REPL_LATE])))
    return loss, grad_x, out


_INPUTS = tuple("""x, mem, positions, norm_mix, norm_xa, norm_mem, norm_ffn, xa_wq, xa_wk, xa_wv, xa_wo, xa_q_norm, xa_k_norm, ffn_w_up, ffn_conv_w, ffn_conv_b, ffn_w_down, hg_lb_logits, mix_w_in, hg_out_norm, mla_q_a_norm, mla_w_uq, mla_kv_a_norm, mla_w_ukv, mla_qn_nope, mla_qn_rope, mla_kn_nope, mla_kn_rope, mix_w_out, s5_lam_re, s5_lam_im, s5_log_dt, s5_b_re, s5_b_im, s5_c_re, s5_c_im, s5_d, s5_w_glu_a, s5_w_glu_b, loss_target, m_norm_mix, m_norm_xa, m_norm_mem, m_norm_ffn, m_xa_wq, m_xa_wk, m_xa_wv, m_xa_wo, m_xa_q_norm, m_xa_k_norm, m_ffn_w_up, m_ffn_conv_w, m_ffn_conv_b, m_ffn_w_down, m_hg_lb_logits, m_mix_w_in, m_hg_out_norm, m_mla_q_a_norm, m_mla_w_uq, m_mla_kv_a_norm, m_mla_w_ukv, m_mla_qn_nope, m_mla_qn_rope, m_mla_kn_nope, m_mla_kn_rope, m_mix_w_out, m_s5_lam_re, m_s5_lam_im, m_s5_log_dt, m_s5_b_re, m_s5_b_im, m_s5_c_re, m_s5_c_im, m_s5_d, m_s5_w_glu_a, m_s5_w_glu_b, v_norm_mix, v_norm_xa, v_norm_mem, v_norm_ffn, v_xa_wq, v_xa_wk, v_xa_wv, v_xa_wo, v_xa_q_norm, v_xa_k_norm, v_ffn_w_up, v_ffn_conv_w, v_ffn_conv_b, v_ffn_w_down, v_hg_lb_logits, v_mix_w_in, v_hg_out_norm, v_mla_q_a_norm, v_mla_w_uq, v_mla_kv_a_norm, v_mla_w_ukv, v_mla_qn_nope, v_mla_qn_rope, v_mla_kn_nope, v_mla_kn_rope, v_mix_w_out, v_s5_lam_re, v_s5_lam_im, v_s5_log_dt, v_s5_b_re, v_s5_b_im, v_s5_c_re, v_s5_c_im, v_s5_d, v_s5_w_glu_a, v_s5_w_glu_b""".replace(" ", "").split(","))


def kernel(x, mem, positions, norm_mix, norm_xa, norm_mem, norm_ffn, xa_wq, xa_wk, xa_wv, xa_wo, xa_q_norm, xa_k_norm, ffn_w_up, ffn_conv_w, ffn_conv_b, ffn_w_down, hg_lb_logits, mix_w_in, hg_out_norm, mla_q_a_norm, mla_w_uq, mla_kv_a_norm, mla_w_ukv, mla_qn_nope, mla_qn_rope, mla_kn_nope, mla_kn_rope, mix_w_out, s5_lam_re, s5_lam_im, s5_log_dt, s5_b_re, s5_b_im, s5_c_re, s5_c_im, s5_d, s5_w_glu_a, s5_w_glu_b, loss_target, m_norm_mix, m_norm_xa, m_norm_mem, m_norm_ffn, m_xa_wq, m_xa_wk, m_xa_wv, m_xa_wo, m_xa_q_norm, m_xa_k_norm, m_ffn_w_up, m_ffn_conv_w, m_ffn_conv_b, m_ffn_w_down, m_hg_lb_logits, m_mix_w_in, m_hg_out_norm, m_mla_q_a_norm, m_mla_w_uq, m_mla_kv_a_norm, m_mla_w_ukv, m_mla_qn_nope, m_mla_qn_rope, m_mla_kn_nope, m_mla_kn_rope, m_mix_w_out, m_s5_lam_re, m_s5_lam_im, m_s5_log_dt, m_s5_b_re, m_s5_b_im, m_s5_c_re, m_s5_c_im, m_s5_d, m_s5_w_glu_a, m_s5_w_glu_b, v_norm_mix, v_norm_xa, v_norm_mem, v_norm_ffn, v_xa_wq, v_xa_wk, v_xa_wv, v_xa_wo, v_xa_q_norm, v_xa_k_norm, v_ffn_w_up, v_ffn_conv_w, v_ffn_conv_b, v_ffn_w_down, v_hg_lb_logits, v_mix_w_in, v_hg_out_norm, v_mla_q_a_norm, v_mla_w_uq, v_mla_kv_a_norm, v_mla_w_ukv, v_mla_qn_nope, v_mla_qn_rope, v_mla_kn_nope, v_mla_kn_rope, v_mix_w_out, v_s5_lam_re, v_s5_lam_im, v_s5_log_dt, v_s5_b_re, v_s5_b_im, v_s5_c_re, v_s5_c_im, v_s5_d, v_s5_w_glu_a, v_s5_w_glu_b):
    vals = dict(zip(_INPUTS, (x, mem, positions, norm_mix, norm_xa, norm_mem, norm_ffn, xa_wq, xa_wk, xa_wv, xa_wo, xa_q_norm, xa_k_norm, ffn_w_up, ffn_conv_w, ffn_conv_b, ffn_w_down, hg_lb_logits, mix_w_in, hg_out_norm, mla_q_a_norm, mla_w_uq, mla_kv_a_norm, mla_w_ukv, mla_qn_nope, mla_qn_rope, mla_kn_nope, mla_kn_rope, mix_w_out, s5_lam_re, s5_lam_im, s5_log_dt, s5_b_re, s5_b_im, s5_c_re, s5_c_im, s5_d, s5_w_glu_a, s5_w_glu_b, loss_target, m_norm_mix, m_norm_xa, m_norm_mem, m_norm_ffn, m_xa_wq, m_xa_wk, m_xa_wv, m_xa_wo, m_xa_q_norm, m_xa_k_norm, m_ffn_w_up, m_ffn_conv_w, m_ffn_conv_b, m_ffn_w_down, m_hg_lb_logits, m_mix_w_in, m_hg_out_norm, m_mla_q_a_norm, m_mla_w_uq, m_mla_kv_a_norm, m_mla_w_ukv, m_mla_qn_nope, m_mla_qn_rope, m_mla_kn_nope, m_mla_kn_rope, m_mix_w_out, m_s5_lam_re, m_s5_lam_im, m_s5_log_dt, m_s5_b_re, m_s5_b_im, m_s5_c_re, m_s5_c_im, m_s5_d, m_s5_w_glu_a, m_s5_w_glu_b, v_norm_mix, v_norm_xa, v_norm_mem, v_norm_ffn, v_xa_wq, v_xa_wk, v_xa_wv, v_xa_wo, v_xa_q_norm, v_xa_k_norm, v_ffn_w_up, v_ffn_conv_w, v_ffn_conv_b, v_ffn_w_down, v_hg_lb_logits, v_mix_w_in, v_hg_out_norm, v_mla_q_a_norm, v_mla_w_uq, v_mla_kv_a_norm, v_mla_w_ukv, v_mla_qn_nope, v_mla_qn_rope, v_mla_kn_nope, v_mla_kn_rope, v_mix_w_out, v_s5_lam_re, v_s5_lam_im, v_s5_log_dt, v_s5_b_re, v_s5_b_im, v_s5_c_re, v_s5_c_im, v_s5_d, v_s5_w_glu_a, v_s5_w_glu_b)))
    w = {n: vals[n] for n in _WEIGHTS}
    m = {n: vals["m_" + n] for n in _WEIGHTS}
    v = {n: vals["v_" + n] for n in _WEIGHTS}
    loss, grad_x, res = _train_step(vals["x"][0], vals["mem"][0], vals["positions"][0], vals["loss_target"][0],
                                    w, m, v)
    loss = lax.psum(loss[0, 0], ("x", "y", "c"))
    return (loss, grad_x[None], *[r[n] for r in res for n in _WEIGHTS])
```

```python
import functools

import jax
import jax.numpy as jnp
import numpy as np
from jax import lax
from jax.experimental import pallas as pl
from jax.experimental.pallas import tpu as pltpu

f32 = jnp.float32
bf16 = jnp.bfloat16

EPS = 1e-6
N_DEV = 8
VMEM_LIMIT = 52 * 1024 * 1024

HG_HEADS = 4
HG_DIM = 128
HG_WIDTH = HG_HEADS * HG_DIM
HG_CHUNK = 64
HG_SUB = 16
MLA_HEADS = 4
MLA_Q_RANK = 256
MLA_KV_RANK = 128
MLA_NOPE = 128
MLA_ROPE = 64
MLA_V = 128
MLA_QK = MLA_NOPE + MLA_ROPE
MLA_QK_PAD = 256
ROPE_BASE = 10000.0
IN_WIDTH = 4 * HG_WIDTH + MLA_Q_RANK + MLA_KV_RANK + MLA_ROPE
IN_PAD = 2560
XA_HEADS = 4
XA_DIM = 256
S5_GROUP = 16
S5_GROUPS = 64
S5_STATE = 64
CONV_W = 3

ADAM_LR = 0.001
ADAM_B1 = 0.9
ADAM_B2 = 0.999
ADAM_EPS = 1e-08
ADAM_WD = 0.01
ADAM_STEP = 10

_NT = (((1,), (1,)), ((), ()))
_TN = (((0,), (0,)), ((), ()))


def _pick(n, cands):
    for c in cands:
        if n % c == 0:
            return c
    return n


def _cparams(sem):
    return pltpu.CompilerParams(dimension_semantics=sem, vmem_limit_bytes=VMEM_LIMIT)


_MM_BUDGET = 36 * 1024 * 1024
_MM_TILES = ((1024, 1024), (1024, 512), (512, 1024), (512, 512), (512, 256), (256, 512), (256, 256), (256, 128),
             (128, 256), (128, 128))


def _mm(a, b, *, name, ta=False, tb=False, out_dtype=f32, add=None, b2=None, kslab=None, norm=None, rms_bwd=None,
        sq_err=None):
    m, k = (a.shape[1], a.shape[0]) if ta else a.shape
    nb = b.shape[0] if tb else b.shape[1]
    n = nb * (2 if b2 is not None else 1)
    slab, nslab = kslab if kslab is not None else (0, 1)
    assert (b.shape[1] // nslab if tb else b.shape[0]) == k, (a.shape, b.shape, ta, tb)
    assert b2 is None or (not tb and b2.shape == b.shape)
    full_rows = norm is not None or rms_bwd is not None
    assert not (full_rows and b2 is not None) and not (norm is not None and rms_bwd is not None)
    isz = lambda x: jnp.dtype(x.dtype).itemsize
    bm = bn = None
    for cm, cn in _MM_TILES:
        if m % cm or nb % cn or (full_rows and cn != n):
            continue
        need = 2 * (cm * k * isz(a) + cn * k * isz(b) * (2 if b2 is not None else 1)
                    + cm * cn * (jnp.dtype(out_dtype).itemsize + (4 if add is not None else 0)
                                 + (2 if norm is not None else 0) + (8 if rms_bwd is not None else 0)
                                 + (4 if sq_err is not None else 0)))
        if need <= _MM_BUDGET:
            bm, bn = cm, cn
            break
    assert bm is not None, (name, a.shape, b.shape)
    half = nb // bn
    dims = (((0 if ta else 1,), (1 if tb else 0,)), ((), ()))

    def body(*refs):
        refs = list(refs)
        a_ref, b_ref = refs[0], refs[1]
        b2_ref = refs.pop(2) if b2 is not None else None
        add_ref = refs.pop(2) if add is not None else None
        gain_ref = refs.pop(2) if norm is not None else None
        x_ref, xgain_ref, through_ref = (refs.pop(2), refs.pop(2), refs.pop(2)) if rms_bwd is not None else (None,) * 3
        target_ref = refs.pop(2) if sq_err is not None else None
        o_ref = refs[2]
        extra_ref = refs[3] if (norm is not None or rms_bwd is not None or sq_err is not None) else None

        def run(rhs_ref):
            r = lax.dot_general(a_ref[...].astype(bf16), rhs_ref[...].astype(bf16), dims, preferred_element_type=f32)
            if add_ref is not None:
                r = r + add_ref[...].astype(f32)
            if sq_err is not None:
                err = r - target_ref[...]
                o_ref[...] = (err * (1.0 / n)).astype(o_ref.dtype)

                @pl.when((pl.program_id(0) == 0) & (pl.program_id(1) == 0))
                def _():
                    extra_ref[...] = jnp.zeros_like(extra_ref)
                extra_ref[...] += jnp.sum(jnp.sum(err * err, axis=-1, keepdims=True), axis=0, keepdims=True) * (0.5 / n)
            elif rms_bwd is None:
                o_ref[...] = r.astype(o_ref.dtype)
            else:
                _, vjp = jax.vjp(_rms, x_ref[...], xgain_ref[...])
                dx, dgain = vjp(r)
                o_ref[...] = (through_ref[...] + dx).astype(o_ref.dtype)

                @pl.when(pl.program_id(0) == 0)
                def _():
                    extra_ref[...] = jnp.zeros_like(extra_ref)
                extra_ref[...] += dgain
            if norm is not None:
                extra_ref[...] = _rms(r, gain_ref[...]).astype(extra_ref.dtype)

        if b2_ref is None:
            run(b_ref)
        else:
            pl.when(pl.program_id(1) < half)(lambda: run(b_ref))
            pl.when(pl.program_id(1) >= half)(lambda: run(b2_ref))

    a_spec = pl.BlockSpec((k, bm), lambda i, j: (0, i)) if ta else pl.BlockSpec((bm, k), lambda i, j: (i, 0))
    if tb:
        b_spec = pl.BlockSpec((bn, k), lambda i, j: (j, slab))
    elif b2 is None:
        b_spec = pl.BlockSpec((k, bn), lambda i, j: (0, j))
    else:
        b_spec = pl.BlockSpec((k, bn), lambda i, j: (0, jnp.minimum(j, half - 1)))
    in_specs = [a_spec, b_spec]
    args = [a, b]
    if b2 is not None:
        in_specs.append(pl.BlockSpec((k, bn), lambda i, j: (0, jnp.maximum(j - half, 0))))
        args.append(b2)
    if add is not None:
        in_specs.append(pl.BlockSpec((bm, bn), lambda i, j: (i, j)))
        args.append(add)
    out_blk = pl.BlockSpec((bm, bn), lambda i, j: (i, j))
    out_specs, out_shape = out_blk, jax.ShapeDtypeStruct((m, n), out_dtype)
    row_blk = pl.BlockSpec((1, bn), lambda i, j: (0, j))
    if norm is not None:
        in_specs.append(row_blk)
        args.append(norm)
        out_specs, out_shape = [out_blk, out_blk], [out_shape, jax.ShapeDtypeStruct((m, n), bf16)]
    if rms_bwd is not None:
        in_specs += [out_blk, row_blk, out_blk]
        args += list(rms_bwd)
        out_specs, out_shape = [out_blk, row_blk], [out_shape, jax.ShapeDtypeStruct((1, n), f32)]
    if sq_err is not None:
        in_specs.append(out_blk)
        args.append(sq_err)
        out_specs = [out_blk, pl.BlockSpec((1, 1), lambda i, j: (0, 0))]
        out_shape = [out_shape, jax.ShapeDtypeStruct((1, 1), f32)]
    carried = rms_bwd is not None or sq_err is not None
    return pl.pallas_call(
        body, grid=(m // bm, n // bn), in_specs=in_specs, out_specs=out_specs, out_shape=out_shape,
        compiler_params=_cparams(("arbitrary" if carried else "parallel", "arbitrary" if sq_err is not None else "parallel")),
        name=name)(*args)


def _as_tuple(x):
    return tuple(x) if isinstance(x, (tuple, list)) else (x,)


def _full_spec(p):
    nd = p.ndim
    return pl.BlockSpec(p.shape, lambda i, _nd=nd: (0,) * _nd)


def _window(a, start, width):
    assert start % width == 0 and width % 128 == 0
    return (a, start // width, width)


def _row_array(x):
    return x[0] if isinstance(x, tuple) else x


def _row_shape(x):
    return (x[0].shape[0], x[2]) if isinstance(x, tuple) else x.shape


def _row_spec(x, tile):
    if isinstance(x, tuple):
        return pl.BlockSpec((tile, x[2]), lambda i, _b=x[1]: (i, _b))
    return pl.BlockSpec((tile, x.shape[1]), lambda i: (i, 0))


def _rows(fn, rows, params, outs, *, name, tile=256, accs=()):
    length = _row_shape(rows[0])[0]
    tile = min(tile, length)
    nr, npar, no = len(rows), len(params), len(outs)

    def body(*refs):
        r, p, o = refs[:nr], refs[nr:nr + npar], refs[nr + npar:]
        res = _as_tuple(fn(*[x[...].astype(f32) for x in r], *[x[...] for x in p]))
        for kk in range(no):
            o[kk][...] = res[kk].astype(o[kk].dtype)
        if accs:
            @pl.when(pl.program_id(0) == 0)
            def _():
                for kk in range(no, no + len(accs)):
                    o[kk][...] = jnp.zeros_like(o[kk])
            for kk in range(no, no + len(accs)):
                o[kk][...] += res[kk]

    in_specs = [_row_spec(x, tile) for x in rows] + [_full_spec(p) for p in params]
    out_specs = [pl.BlockSpec((tile, w), lambda i: (i, 0)) for w, _ in outs]
    out_shape = [jax.ShapeDtypeStruct((length, w), d) for w, d in outs]
    for s in accs:
        out_specs.append(pl.BlockSpec(s, lambda i, _nd=len(s): (0,) * _nd))
        out_shape.append(jax.ShapeDtypeStruct(s, f32))
    res = pl.pallas_call(body, grid=(length // tile,), in_specs=in_specs, out_specs=out_specs, out_shape=out_shape,
                         compiler_params=_cparams(("arbitrary",)), name=name)(*[_row_array(x) for x in rows], *params)
    return res


def _rows_bwd(fn, rows, params, cts, *, name, rgrad, pgrad, tile=256, addends=None):
    addends = {i: (a if isinstance(a, list) else [(a, 0)]) for i, a in (addends or {}).items()}
    length = _row_shape(rows[0])[0]
    tile = min(tile, length)
    nr, npar, nc = len(rows), len(params), len(cts)
    ridx = [i for i in range(nr) if rgrad[i] is not None]
    pidx = [i for i in range(npar) if pgrad[i]]
    flat_addends = [(i, a, off) for i in sorted(addends) for a, off in addends[i]]
    na = len(flat_addends)

    def body(*refs):
        r, p, c = refs[:nr], refs[nr:nr + npar], refs[nr + npar:nr + npar + nc]
        ad = refs[nr + npar + nc:nr + npar + nc + na]
        o = refs[nr + npar + nc + na:]
        rv = [x[...].astype(f32) for x in r]
        pv = [x[...] for x in p]
        cv = tuple(x[...].astype(f32) for x in c)

        def g(*d):
            rr, pp = list(rv), list(pv)
            for n_, i_ in enumerate(ridx):
                rr[i_] = d[n_]
            for n_, i_ in enumerate(pidx):
                pp[i_] = d[len(ridx) + n_]
            return _as_tuple(fn(*rr, *pp))

        _, vjp = jax.vjp(g, *[rv[i] for i in ridx], *[pv[i] for i in pidx])
        grads = vjp(cv)
        for n_, i_ in enumerate(ridx):
            val = grads[n_]
            for k_, (j_, a_, off) in enumerate(flat_addends):
                if j_ == i_:
                    extra = ad[k_][...].astype(f32)
                    if extra.shape[1] != val.shape[1]:
                        extra = jnp.pad(extra, ((0, 0), (off, val.shape[1] - off - extra.shape[1])))
                    val = val + extra
            o[n_][...] = val.astype(o[n_].dtype)
        if pidx:
            @pl.when(pl.program_id(0) == 0)
            def _():
                for n_ in range(len(pidx)):
                    o[len(ridx) + n_][...] = jnp.zeros_like(o[len(ridx) + n_])
            for n_ in range(len(pidx)):
                o[len(ridx) + n_][...] += grads[len(ridx) + n_]

    plain = lambda shape: pl.BlockSpec((tile, shape[1]), lambda i: (i, 0))
    in_specs = ([_row_spec(x, tile) for x in rows] + [_full_spec(p) for p in params] + [plain(x.shape) for x in cts]
                + [plain(a.shape) for _, a, _ in flat_addends])
    out_specs = [plain(_row_shape(rows[i])) for i in ridx] + [_full_spec(params[i]) for i in pidx]
    out_shape = ([jax.ShapeDtypeStruct(_row_shape(rows[i]), rgrad[i]) for i in ridx]
                 + [jax.ShapeDtypeStruct(params[i].shape, f32) for i in pidx])
    res = pl.pallas_call(body, grid=(length // tile,), in_specs=in_specs, out_specs=out_specs, out_shape=out_shape,
                         compiler_params=_cparams(("arbitrary",)), name=name)(
        *[_row_array(x) for x in rows], *params, *cts, *[a for _, a, _ in flat_addends])
    return list(res[:len(ridx)]), list(res[len(ridx):])


def _rms(x, g):
    return x * lax.rsqrt(jnp.mean(x * x, axis=-1, keepdims=True) + EPS) * g


def _rms_twice(x, g):
    y = _rms(x, g)
    return y, y


def _silu(x):
    return x * jax.nn.sigmoid(x)


_UP_COLS = 1024


def _ffn_up_conv(hf, w_up, cw, cb, *, name):
    length, d = hf.shape
    ff = w_up.shape[1] // 2
    bm = _pick(length, (1024, 512, 256))
    bn = _UP_COLS
    nj = ff // bn

    def body(hf_ref, wg_ref, wv_ref, cwg, cwv, cbg, cbv, ug_ref, uv_ref, a_ref, halo_g, halo_v):
        i, j = pl.program_id(0), pl.program_id(1)
        x = hf_ref[...]
        rows = lax.broadcasted_iota(jnp.int32, (bm, bn), 0)
        pad = jnp.zeros((bm - 8, bn), f32)

        def half(w_ref, cw_ref, cb_ref, u_ref, halo):
            u = jnp.dot(x, w_ref[...], preferred_element_type=f32).astype(bf16)
            u_ref[...] = u
            u = u.astype(f32)
            prev = jnp.where(i == 0, 0.0, halo[j])
            x1 = jnp.where(rows >= 1, pltpu.roll(u, 1, axis=0), jnp.concatenate([pltpu.roll(prev, 1, axis=0), pad]))
            x2 = jnp.where(rows >= 2, pltpu.roll(u, 2, axis=0), jnp.concatenate([pltpu.roll(prev, 2, axis=0), pad]))
            halo[j] = u[bm - 8:, :]
            return cw_ref[2:3, :] * u + cw_ref[1:2, :] * x1 + cw_ref[0:1, :] * x2 + cb_ref[...]

        g = half(wg_ref, cwg, cbg, ug_ref, halo_g)
        v = half(wv_ref, cwv, cbv, uv_ref, halo_v)
        a_ref[...] = (_silu(g) * v).astype(a_ref.dtype)

    col = lambda r, off: pl.BlockSpec((r, bn), lambda i, j, _o=off: (0, j + _o))
    out_blk = pl.BlockSpec((bm, bn), lambda i, j: (i, j))
    sds = jax.ShapeDtypeStruct((length, ff), bf16)
    return pl.pallas_call(
        body, grid=(length // bm, nj),
        in_specs=[pl.BlockSpec((bm, d), lambda i, j: (i, 0)), col(d, 0), col(d, nj), col(CONV_W, 0), col(CONV_W, nj),
                  col(1, 0), col(1, nj)],
        out_specs=[out_blk] * 3, out_shape=[sds] * 3,
        scratch_shapes=[pltpu.VMEM((nj, 8, bn), f32), pltpu.VMEM((nj, 8, bn), f32)],
        compiler_params=_cparams(("arbitrary", "arbitrary")), name=name)(hf, w_up, w_up, cw, cw, cb, cb)


def _ffn_da_dconv(dout, w_down, u_g, u_v, cw, cb, *, name):
    length, d = dout.shape
    ff = u_g.shape[1]
    bm = _pick(length, (1024, 512, 256))
    bn = _UP_COLS
    nj, ni = ff // bn, length // bm

    def body(dout_ref, wd_ref, ug_ref, uv_ref, pg_ref, pv_ref, cwg, cwv, cbg, cbv,
             dug_ref, duv_ref, sums_g_ref, sums_v_ref, halo_g, halo_v, acc_g, acc_v):
        s, j = pl.program_id(0), pl.program_id(1)
        rows = lax.broadcasted_iota(jnp.int32, (bm, bn), 0)
        row8 = lax.broadcasted_iota(jnp.int32, (8, bn), 0)
        pad = jnp.zeros((bm - 8, bn), f32)
        da = lax.dot_general(dout_ref[...].astype(bf16), wd_ref[...], _NT, preferred_element_type=f32)

        def conv(u_ref, p_ref, cw_ref, cb_ref):
            x = u_ref[...].astype(f32)
            prev = jnp.where(s == ni - 1, 0.0, p_ref[...].astype(f32))
            x1 = jnp.where(rows >= 1, pltpu.roll(x, 1, axis=0), jnp.concatenate([pltpu.roll(prev, 1, axis=0), pad]))
            x2 = jnp.where(rows >= 2, pltpu.roll(x, 2, axis=0), jnp.concatenate([pltpu.roll(prev, 2, axis=0), pad]))
            return cw_ref[2:3, :] * x + cw_ref[1:2, :] * x1 + cw_ref[0:1, :] * x2 + cb_ref[...], x, x1, x2

        g, xg, xg1, xg2 = conv(ug_ref, pg_ref, cwg, cbg)
        v, xv, xv1, xv2 = conv(uv_ref, pv_ref, cwv, cbv)
        sg = jax.nn.sigmoid(g)
        dg = da * v * (sg * (1.0 + g * (1.0 - sg)))
        dv = da * (g * sg)

        def back(dy, x, x1, x2, cw_ref, du_ref, sums_ref, halo, acc):
            nxt = jnp.where(s == 0, 0.0, halo[j])
            up1 = jnp.where(rows < bm - 1, pltpu.roll(dy, bm - 1, axis=0),
                            jnp.concatenate([pad, pltpu.roll(nxt, 7, axis=0)]))
            up2 = jnp.where(rows < bm - 2, pltpu.roll(dy, bm - 2, axis=0),
                            jnp.concatenate([pad, pltpu.roll(nxt, 6, axis=0)]))
            halo[j] = dy[:8, :]
            du_ref[...] = (cw_ref[2:3, :] * dy + cw_ref[1:2, :] * up1 + cw_ref[0:1, :] * up2).astype(du_ref.dtype)
            col = lambda t: jnp.sum(t, axis=0, keepdims=True)
            part = jnp.where(row8 == 0, col(dy * x2), jnp.where(row8 == 1, col(dy * x1), jnp.where(
                row8 == 2, col(dy * x), jnp.where(row8 == 3, col(dy), 0.0))))
            total = jnp.where(s == 0, part, acc[j] + part)
            acc[j] = total
            sums_ref[...] = total

        back(dg, xg, xg1, xg2, cwg, dug_ref, sums_g_ref, halo_g, acc_g)
        back(dv, xv, xv1, xv2, cwv, duv_ref, sums_v_ref, halo_v, acc_v)

    rb = lambda s: ni - 1 - s
    tile = pl.BlockSpec((bm, bn), lambda s, j: (rb(s), j))
    before = pl.BlockSpec((8, bn), lambda s, j: (jnp.maximum(rb(s) * (bm // 8) - 1, 0), j))
    col = lambda r, off: pl.BlockSpec((r, bn), lambda s, j, _o=off: (0, j + _o))
    sds = jax.ShapeDtypeStruct
    dug, duv, sums_g, sums_v = pl.pallas_call(
        body, grid=(ni, nj),
        in_specs=[pl.BlockSpec((bm, d), lambda s, j: (rb(s), 0)), pl.BlockSpec((bn, d), lambda s, j: (j, 0)),
                  tile, tile, before, before, col(CONV_W, 0), col(CONV_W, nj), col(1, 0), col(1, nj)],
        out_specs=[tile, tile] + [pl.BlockSpec((8, bn), lambda s, j: (s, j))] * 2,
        out_shape=[sds((length, ff), bf16), sds((length, ff), bf16), sds((ni * 8, ff), f32), sds((ni * 8, ff), f32)],
        scratch_shapes=[pltpu.VMEM((nj, 8, bn), f32)] * 4,
        compiler_params=_cparams(("arbitrary", "arbitrary")), name=name)(
        dout, w_down, u_g, u_v, u_g, u_v, cw, cw, cb, cb)
    last = (ni - 1) * 8
    both = lambda lo, hi: jnp.concatenate([sums_g[last + lo:last + hi], sums_v[last + lo:last + hi]], axis=1)
    return dug, duv, both(0, CONV_W), both(CONV_W, CONV_W + 1)


def _ffn_fwd(h, w, tag, hf=None, loss_target=None):
    if hf is None:
        hf, = _rows(_rms, [h], [w["norm_ffn"]], [(h.shape[1], bf16)], name=f"ffn_norm_{tag}")
    u_g, u_v, a = _ffn_up_conv(hf, w["ffn_w_up"], w["ffn_conv_w"], w["ffn_conv_b"], name=f"ffn_up_{tag}")
    out = _mm(a, w["ffn_w_down"], add=h, sq_err=loss_target, name=f"ffn_down_{tag}")
    return out, (h, hf, u_g, u_v, a)


def _ffn_bwd(dout, w, saved, tag):
    h, hf, u_g, u_v, a = saved
    g = {"ffn_w_down": _mm(a, dout, ta=True, out_dtype=bf16, name=f"ffn_dwdown_{tag}")}
    dug, duv, g["ffn_conv_w"], g["ffn_conv_b"] = _ffn_da_dconv(
        dout, w["ffn_w_down"], u_g, u_v, w["ffn_conv_w"], w["ffn_conv_b"], name=f"ffn_dconv_{tag}")
    dhf = _mm(dug, w["ffn_w_up"], tb=True, kslab=(0, 2), name=f"ffn_dhf_g_{tag}")
    dh, g["norm_ffn"] = _mm(duv, w["ffn_w_up"], tb=True, kslab=(1, 2), add=dhf, rms_bwd=(h, w["norm_ffn"], dout),
                            name=f"ffn_dhf_v_{tag}")
    g["ffn_w_up"] = _mm(hf, dug, ta=True, b2=duv, out_dtype=bf16, name=f"ffn_dwup_{tag}")
    return dh, g


def _xa_keys(kx, kg):
    return jnp.concatenate([_rms(kx[:, hh * XA_DIM:(hh + 1) * XA_DIM], kg) for hh in range(XA_HEADS)], axis=-1)


def _xattn_fn(qx, kn, vx, qg):
    outs = []
    for hh in range(XA_HEADS):
        sl = slice(hh * XA_DIM, (hh + 1) * XA_DIM)
        q = _rms(qx[:, sl], qg).astype(bf16)
        k = kn[:, sl].astype(bf16)
        s = lax.dot_general(q, k, _NT, preferred_element_type=f32) * (XA_DIM ** -0.5)
        s = s - jnp.max(s, axis=-1, keepdims=True)
        p = jnp.exp(s)
        p = p / jnp.sum(p, axis=-1, keepdims=True)
        outs.append(jnp.dot(p.astype(bf16), vx[:, sl].astype(bf16), preferred_element_type=f32))
    return jnp.concatenate(outs, axis=-1)


def _xattn_fwd(h, mem, w, tag, hx=None):
    d = h.shape[1]
    if hx is None:
        hx, = _rows(_rms, [h], [w["norm_xa"]], [(d, bf16)], name=f"xa_norm_{tag}")
    qx = _mm(hx, w["xa_wq"], out_dtype=bf16, name=f"xa_q_{tag}")
    m, = _rows(_rms, [mem], [w["norm_mem"]], [(d, bf16)], name=f"xa_mnorm_{tag}")
    kx = _mm(m, w["xa_wk"], name=f"xa_k_{tag}")
    vx = _mm(m, w["xa_wv"], name=f"xa_v_{tag}")
    kn, = _rows(_xa_keys, [kx], [w["xa_k_norm"]], [(d, f32)], name=f"xa_kn_{tag}")
    o, = _rows(_xattn_fn, [qx], [kn, vx, w["xa_q_norm"]], [(d, bf16)], tile=1024, name=f"xa_attn_{tag}")
    out, hf = _mm(o, w["xa_wo"], add=h, norm=w["norm_ffn"], name=f"xa_o_{tag}")
    return out, (h, hx, qx, m, kx, kn, vx, o), hf


def _xattn_bwd(dout, mem, w, saved, tag):
    h, hx, qx, m, kx, kn, vx, o = saved
    g = {}
    do = _mm(dout, w["xa_wo"], tb=True, out_dtype=bf16, name=f"xa_do_{tag}")
    g["xa_wo"] = _mm(o, dout, ta=True, out_dtype=bf16, name=f"xa_dwo_{tag}")
    (dqx,), (dkn, dvx, g["xa_q_norm"]) = _rows_bwd(
        _xattn_fn, [qx], [kn, vx, w["xa_q_norm"]], [do], rgrad=[bf16], pgrad=[True] * 3, tile=1024,
        name=f"xa_dattn_{tag}")
    (dkx,), (g["xa_k_norm"],) = _rows_bwd(_xa_keys, [kx], [w["xa_k_norm"]], [dkn], rgrad=[f32], pgrad=[True],
                                          name=f"xa_dkn_{tag}")
    dh, g["norm_xa"] = _mm(dqx, w["xa_wq"], tb=True, rms_bwd=(h, w["norm_xa"], dout), name=f"xa_dhx_{tag}")
    g["xa_wq"] = _mm(hx, dqx, ta=True, out_dtype=bf16, name=f"xa_dwq_{tag}")
    dm = _mm(dkx, w["xa_wk"], tb=True, name=f"xa_dm_k_{tag}")
    dm = _mm(dvx, w["xa_wv"], tb=True, add=dm, name=f"xa_dm_v_{tag}")
    g["xa_wk"] = _mm(m, dkx, ta=True, out_dtype=bf16, name=f"xa_dwk_{tag}")
    g["xa_wv"] = _mm(m, dvx, ta=True, out_dtype=bf16, name=f"xa_dwv_{tag}")
    _, (g["norm_mem"],) = _rows_bwd(_rms, [mem], [w["norm_mem"]], [dm], rgrad=[None], pgrad=[True],
                                    name=f"xa_dmnorm_{tag}")
    return dh, g


_HG_GROUP = 8


def _hg_chunk(q, k, v, g, *sts):
    c = q.shape[0]
    heads = [slice(h * HG_DIM, (h + 1) * HG_DIM) for h in range(len(sts))]
    tri = (lax.broadcasted_iota(jnp.int32, (c, c), 0) >= lax.broadcasted_iota(jnp.int32, (c, c), 1)).astype(f32)
    b = jnp.dot(tri, g, precision=lax.Precision.HIGHEST, preferred_element_type=f32)
    bend = jnp.sum(g, axis=0, keepdims=True)
    qe = (q * jnp.exp(b)).astype(bf16)
    kd = (k * jnp.exp(bend - b)).astype(bf16)
    vb = v.astype(bf16)
    decay = jnp.exp(bend)
    o_inter = [lax.dot_general(qe[:, hs], st.astype(bf16), _NT, preferred_element_type=f32) for hs, st in zip(heads, sts)]
    new = [st * decay[:, hs] + lax.dot_general(vb[:, hs], kd[:, hs], _TN, preferred_element_type=f32)
           for hs, st in zip(heads, sts)]
    outs = []
    for i in range(c // HG_SUB):
        lo, n = HG_SUB * i, HG_SUB * (i + 1)
        ref = jnp.sum(g[:lo], axis=0, keepdims=True) if i else jnp.zeros((1, g.shape[1]), f32)
        qh = (q[lo:n] * jnp.exp(b[lo:n] - ref)).astype(bf16)
        kh = (k[:n] * jnp.exp(ref - b[:n])).astype(bf16)
        keep = (lax.broadcasted_iota(jnp.int32, (HG_SUB, n), 1)
                <= lo + lax.broadcasted_iota(jnp.int32, (HG_SUB, n), 0))
        scores = [lax.dot_general(qh[:, hs], kh[:, hs], _NT, preferred_element_type=f32) for hs in heads]
        scores = [jnp.where(keep, a, 0.0).astype(bf16) for a in scores]
        outs.append(jnp.concatenate([jnp.dot(a, vb[:n, hs], preferred_element_type=f32)
                                     for a, hs in zip(scores, heads)], axis=1))
    return (jnp.concatenate(outs, axis=0) + jnp.concatenate(o_inter, axis=1), *new)


def _hg_fwd(q, k, v, g, *, name):
    length = q.shape[0]
    rows = _HG_GROUP * HG_CHUNK
    ng = length // rows
    nc = length // HG_CHUNK

    def body(q_ref, k_ref, v_ref, g_ref, o_ref, st_ref, state):
        @pl.when(pl.program_id(0) == 0)
        def _():
            state[...] = jnp.zeros_like(state)

        states = [state[h] for h in range(HG_HEADS)]
        for ci in range(_HG_GROUP):
            sl = slice(ci * HG_CHUNK, (ci + 1) * HG_CHUNK)
            for h in range(HG_HEADS):
                st_ref[h, ci] = states[h].astype(st_ref.dtype)
            o, *states = _hg_chunk(q_ref[sl, :], k_ref[sl, :], v_ref[sl, :], g_ref[sl, :], *states)
            o_ref[sl, :] = o
        for h in range(HG_HEADS):
            state[h] = states[h]

    blk = pl.BlockSpec((rows, HG_WIDTH), lambda c: (c, 0))
    return pl.pallas_call(
        body, grid=(ng,), in_specs=[blk] * 4,
        out_specs=[blk, pl.BlockSpec((HG_HEADS, _HG_GROUP, HG_DIM, HG_DIM), lambda c: (0, c, 0, 0))],
        out_shape=[jax.ShapeDtypeStruct((length, HG_WIDTH), f32),
                   jax.ShapeDtypeStruct((HG_HEADS, nc, HG_DIM, HG_DIM), bf16)],
        scratch_shapes=[pltpu.VMEM((HG_HEADS, HG_DIM, HG_DIM), f32)],
        compiler_params=_cparams(("arbitrary",)), name=name)(q, k, v, g)


def _hg_bwd(q, k, v, g, states, do, *, name):
    length = q.shape[0]
    rows = _HG_GROUP * HG_CHUNK
    ng = length // rows

    def body(q_ref, k_ref, v_ref, g_ref, st_ref, do_ref, dq_ref, dk_ref, dv_ref, dg_ref, dstate):
        @pl.when(pl.program_id(0) == 0)
        def _():
            dstate[...] = jnp.zeros_like(dstate)

        dstates = [dstate[h] for h in range(HG_HEADS)]
        for ci in reversed(range(_HG_GROUP)):
            sl = slice(ci * HG_CHUNK, (ci + 1) * HG_CHUNK)
            _, vjp = jax.vjp(_hg_chunk, q_ref[sl, :], k_ref[sl, :], v_ref[sl, :], g_ref[sl, :],
                             *[st_ref[h, ci].astype(f32) for h in range(HG_HEADS)])
            dq, dk, dv, dg, *dstates = vjp((do_ref[sl, :], *dstates))
            dq_ref[sl, :] = dq
            dk_ref[sl, :] = dk
            dv_ref[sl, :] = dv
            dg_ref[sl, :] = dg
        for h in range(HG_HEADS):
            dstate[h] = dstates[h]

    blk = pl.BlockSpec((rows, HG_WIDTH), lambda c: (ng - 1 - c, 0))
    sds = jax.ShapeDtypeStruct((length, HG_WIDTH), f32)
    return pl.pallas_call(
        body, grid=(ng,),
        in_specs=[blk] * 4 + [pl.BlockSpec((HG_HEADS, _HG_GROUP, HG_DIM, HG_DIM), lambda c: (0, ng - 1 - c, 0, 0)), blk],
        out_specs=[blk] * 4, out_shape=[sds] * 4,
        scratch_shapes=[pltpu.VMEM((HG_HEADS, HG_DIM, HG_DIM), f32)],
        compiler_params=_cparams(("arbitrary",)), name=name)(q, k, v, g, states, do)


_ATT_BLK = 512
_ATT_SCALE = MLA_QK ** -0.5
_NEG = -1e30


def _att_mask(i, j, t):
    rows = i * t + lax.broadcasted_iota(jnp.int32, (t, t), 0)
    cols = j * t + lax.broadcasted_iota(jnp.int32, (t, t), 1)
    return cols <= rows


def _att_fwd(q, k, v, *, name):
    length = q.shape[0]
    t = min(_ATT_BLK, length)
    nq = length // t
    qw, vw = MLA_QK_PAD, MLA_V
    heads = range(MLA_HEADS)

    def body(q_ref, k_ref, v_ref, o_ref, lse_ref):
        i = pl.program_id(0)
        qbs = [q_ref[:, h * qw:(h + 1) * qw] for h in heads]

        def step(j, carry, diagonal=False):
            off = pl.multiple_of(j * t, t)
            out = []
            for h in heads:
                m, l, acc = carry[h]
                ks = k_ref[pl.ds(off, t), h * qw:(h + 1) * qw]
                vs = v_ref[pl.ds(off, t), h * vw:(h + 1) * vw]
                s = lax.dot_general(qbs[h], ks, _NT, preferred_element_type=f32) * _ATT_SCALE
                if diagonal:
                    s = jnp.where(_att_mask(i, j, t), s, _NEG)
                m_new = jnp.maximum(m, jnp.max(s, axis=-1, keepdims=True))
                alpha = jnp.exp(m - m_new)
                p = jnp.exp(s - m_new)
                l = alpha * l + jnp.sum(p, axis=-1, keepdims=True)
                acc = alpha * acc + jnp.dot(p.astype(bf16), vs, preferred_element_type=f32)
                out.append((m_new, l, acc))
            return tuple(out)

        init = tuple((jnp.full((t, 1), _NEG, f32), jnp.zeros((t, 1), f32), jnp.zeros((t, vw), f32)) for _ in heads)
        res = step(i, lax.fori_loop(0, i, step, init), diagonal=True)
        for h in heads:
            m, l, acc = res[h]
            o_ref[:, h * vw:(h + 1) * vw] = (acc / l).astype(o_ref.dtype)
            lse_ref[:, h * vw:(h + 1) * vw] = jnp.broadcast_to(m + jnp.log(l), (t, vw))

    return pl.pallas_call(
        body, grid=(nq,),
        in_specs=[pl.BlockSpec((t, q.shape[1]), lambda i: (i, 0)), pl.BlockSpec(k.shape, lambda i: (0, 0)),
                  pl.BlockSpec(v.shape, lambda i: (0, 0))],
        out_specs=[pl.BlockSpec((t, v.shape[1]), lambda i: (i, 0))] * 2,
        out_shape=[jax.ShapeDtypeStruct(v.shape, bf16), jax.ShapeDtypeStruct(v.shape, f32)],
        compiler_params=_cparams(("arbitrary",)), name=name)(q, k, v)


def _att_bwd(q, k, v, o, lse, do, *, name):
    length = q.shape[0]
    t = min(_ATT_BLK, length)
    nq = length // t
    qw, vw = MLA_QK_PAD, MLA_V
    heads = range(MLA_HEADS)

    def dq_body(q_ref, k_ref, v_ref, o_ref, lse_ref, do_ref, dq_ref, delta_ref):
        i = pl.program_id(0)
        qbs = [q_ref[:, h * qw:(h + 1) * qw] for h in heads]
        dobs = [do_ref[:, h * vw:(h + 1) * vw] for h in heads]
        lses = [lse_ref[:, h * vw:h * vw + 1] for h in heads]
        deltas = [jnp.sum(dobs[h].astype(f32) * o_ref[:, h * vw:(h + 1) * vw].astype(f32), axis=-1, keepdims=True)
                  for h in heads]

        def step(j, dqs, diagonal=False):
            off = pl.multiple_of(j * t, t)
            out = []
            for h in heads:
                ks = k_ref[pl.ds(off, t), h * qw:(h + 1) * qw]
                vs = v_ref[pl.ds(off, t), h * vw:(h + 1) * vw]
                s = lax.dot_general(qbs[h], ks, _NT, preferred_element_type=f32) * _ATT_SCALE
                p = jnp.exp(s - lses[h])
                if diagonal:
                    p = jnp.where(_att_mask(i, j, t), p, 0.0)
                dp = lax.dot_general(dobs[h], vs, _NT, preferred_element_type=f32)
                ds = p * (dp - deltas[h]) * _ATT_SCALE
                out.append(dqs[h] + jnp.dot(ds.astype(bf16), ks, preferred_element_type=f32))
            return tuple(out)

        dqs = step(i, lax.fori_loop(0, i, step, tuple(jnp.zeros((t, qw), f32) for _ in heads)), diagonal=True)
        for h in heads:
            dq_ref[:, h * qw:(h + 1) * qw] = dqs[h].astype(dq_ref.dtype)
            delta_ref[:, h * vw:(h + 1) * vw] = jnp.broadcast_to(deltas[h], (t, vw))

    qblk = pl.BlockSpec((t, q.shape[1]), lambda i: (i, 0))
    vblk = pl.BlockSpec((t, v.shape[1]), lambda i: (i, 0))
    qfull = pl.BlockSpec(q.shape, lambda i: (0, 0))
    vfull = pl.BlockSpec(v.shape, lambda i: (0, 0))
    dq, delta = pl.pallas_call(
        dq_body, grid=(nq,), in_specs=[qblk, qfull, vfull, vblk, vblk, vblk], out_specs=[qblk, vblk],
        out_shape=[jax.ShapeDtypeStruct(q.shape, bf16), jax.ShapeDtypeStruct(lse.shape, f32)],
        compiler_params=_cparams(("arbitrary",)), name=name + "_dq")(q, k, v, o, lse, do)

    def dkv_body(k_ref, v_ref, q_ref, do_ref, lse_ref, delta_ref, dk_ref, dv_ref):
        j = pl.program_id(0)
        kbs = [k_ref[:, h * qw:(h + 1) * qw] for h in heads]
        vbs = [v_ref[:, h * vw:(h + 1) * vw] for h in heads]

        def step(i, carry, diagonal=False):
            off = pl.multiple_of(i * t, t)
            out = []
            for h in heads:
                dk, dv = carry[h]
                qs = q_ref[pl.ds(off, t), h * qw:(h + 1) * qw]
                dos = do_ref[pl.ds(off, t), h * vw:(h + 1) * vw]
                lse_i = lse_ref[pl.ds(off, t), h * vw:h * vw + 1]
                delta_i = delta_ref[pl.ds(off, t), h * vw:h * vw + 1]
                s = lax.dot_general(qs, kbs[h], _NT, preferred_element_type=f32) * _ATT_SCALE
                p = jnp.exp(s - lse_i)
                if diagonal:
                    p = jnp.where(_att_mask(i, j, t), p, 0.0)
                dv = dv + lax.dot_general(p.astype(bf16), dos, _TN, preferred_element_type=f32)
                dp = lax.dot_general(dos, vbs[h], _NT, preferred_element_type=f32)
                ds = p * (dp - delta_i) * _ATT_SCALE
                dk = dk + lax.dot_general(ds.astype(bf16), qs, _TN, preferred_element_type=f32)
                out.append((dk, dv))
            return tuple(out)

        first = step(j, tuple((jnp.zeros((t, qw), f32), jnp.zeros((t, vw), f32)) for _ in heads), diagonal=True)
        res = lax.fori_loop(j + 1, nq, step, first)
        for h in heads:
            dk_ref[:, h * qw:(h + 1) * qw] = res[h][0].astype(dk_ref.dtype)
            dv_ref[:, h * vw:(h + 1) * vw] = res[h][1].astype(dv_ref.dtype)

    dk, dv = pl.pallas_call(
        dkv_body, grid=(nq,), in_specs=[qblk, vblk, qfull, vfull, vfull, vfull], out_specs=[qblk, vblk],
        out_shape=[jax.ShapeDtypeStruct(k.shape, bf16), jax.ShapeDtypeStruct(v.shape, bf16)],
        compiler_params=_cparams(("arbitrary",)), name=name + "_dkv")(k, v, q, do, lse, delta)
    return dq, dk, dv


_C_Q = 4 * HG_WIDTH
_C_KV = _C_Q + MLA_Q_RANK
_C_KPE = _C_KV + MLA_KV_RANK


def _rms_n(x, g, n):
    return x * lax.rsqrt(jnp.sum(x * x, axis=-1, keepdims=True) * (1.0 / n) + EPS) * g


def _mix_a(proj, l0, l1, q_a_norm, kv_a_norm):
    lb = jax.nn.sigmoid(l0 - l1)
    f = lb + (1.0 - lb) * jax.nn.sigmoid(proj[:, HG_WIDTH:2 * HG_WIDTH])
    qf = _silu(proj[:, :HG_WIDTH])
    v = proj[:, 2 * HG_WIDTH:3 * HG_WIDTH]
    cqn = _rms(proj[:, _C_Q:_C_KV], q_a_norm)
    ckvn = _rms(proj[:, _C_KV:_C_KPE], kv_a_norm)
    return qf, 1.0 - f, v, jnp.log(f), cqn, ckvn


def _mix_b(qraw, kvraw, kpe_raw, cos, sin, qn_nope, qn_rope, kn_nope, kn_rope, perm):
    def rope(x):
        return x * cos + jnp.dot(x, perm, precision=lax.Precision.HIGHEST, preferred_element_type=f32) * sin

    kpe = rope(_rms_n(kpe_raw, kn_rope, MLA_ROPE))
    qs, ks, vs = [], [], []
    for hh in range(MLA_HEADS):
        base = hh * MLA_QK_PAD
        qs.append(_rms(qraw[:, base:base + MLA_NOPE], qn_nope))
        qs.append(rope(_rms_n(qraw[:, base + MLA_NOPE:base + MLA_QK_PAD], qn_rope, MLA_ROPE)))
        ks.append(_rms(kvraw[:, base:base + MLA_NOPE], kn_nope))
        ks.append(kpe)
        vs.append(kvraw[:, base + MLA_NOPE:base + MLA_QK_PAD])
    return jnp.concatenate(qs, axis=-1), jnp.concatenate(ks, axis=-1), jnp.concatenate(vs, axis=-1)


def _mix_c(o_hg, gate, o_mla, hg_out_norm):
    parts = []
    for hh in range(HG_HEADS):
        sl = slice(hh * HG_DIM, (hh + 1) * HG_DIM)
        parts.append(_rms(o_hg[:, sl], hg_out_norm[:, sl]))
    o = jnp.concatenate(parts, axis=-1) * _silu(gate)
    return jnp.concatenate([o, o_mla], axis=-1)


def _rope_perm():
    p = np.zeros((128, 128), np.float32)
    half = MLA_ROPE // 2
    for i in range(half):
        p[i + half, i] = -1.0
        p[i, i + half] = 1.0
    return jnp.asarray(p)


def _mixer_fwd(h, cos, sin, w, tag, next_gain):
    d = h.shape[1]
    hn, = _rows(_rms, [h], [w["norm_mix"]], [(d, bf16)], name=f"mix_norm_{tag}")
    proj = _mm(hn, w["mix_w_in"], name=f"mix_in_{tag}")
    pa = [w["lb0"], w["lb1"], w["mla_q_a_norm"], w["mla_kv_a_norm"]]
    qf, kk, vv, logf, cqn, ckvn = _rows(
        _mix_a, [proj], pa, [(HG_WIDTH, f32)] * 4 + [(MLA_Q_RANK, bf16), (MLA_KV_RANK, bf16)], name=f"mix_a_{tag}")
    o_hg, states = _hg_fwd(qf, kk, vv, logf, name=f"hg_fwd_{tag}")
    qraw = _mm(cqn, w["mla_w_uq"], out_dtype=bf16, name=f"mla_uq_{tag}")
    kvraw = _mm(ckvn, w["mla_w_ukv"], out_dtype=bf16, name=f"mla_ukv_{tag}")
    pb = [w["mla_qn_nope"], w["mla_qn_rope"], w["mla_kn_nope"], w["mla_kn_rope"], w["rope_perm"]]
    kpe_raw, gate = _window(proj, _C_KPE, IN_PAD - _C_KPE), _window(proj, 3 * HG_WIDTH, HG_WIDTH)
    qfull, kfull, vfull = _rows(_mix_b, [qraw, kvraw, kpe_raw, cos, sin], pb,
                                [(MLA_HEADS * MLA_QK_PAD, bf16)] * 2 + [(MLA_HEADS * MLA_V, bf16)],
                                name=f"mix_b_{tag}")
    o_mla, lse = _att_fwd(qfull, kfull, vfull, name=f"att_fwd_{tag}")
    mixin, = _rows(_mix_c, [o_hg, gate, o_mla], [w["hg_out_norm"]], [(d, bf16)], name=f"mix_c_{tag}")
    if callable(w["mix_w_out"]):
        w["mix_w_out"] = w["mix_w_out"](mixin)
    out, normed = _mm(mixin, w["mix_w_out"], add=h, norm=next_gain, name=f"mix_out_{tag}")
    return out, (h, hn, proj, qf, kk, vv, logf, cqn, ckvn, o_hg, states, qraw, kvraw, qfull, kfull, vfull, o_mla,
                 lse, mixin), normed


def _mixer_bwd(dout, cos, sin, w, saved, tag, on_w_out=None):
    (h, hn, proj, qf, kk, vv, logf, cqn, ckvn, o_hg, states, qraw, kvraw, qfull, kfull, vfull, o_mla, lse,
     mixin) = saved
    g = {}
    dmixin = _mm(dout, w["mix_w_out"], tb=True, out_dtype=bf16, name=f"mix_dmixin_{tag}")
    g["mix_w_out"] = _mm(mixin, dout, ta=True, out_dtype=bf16, name=f"mix_dwout_{tag}")
    if on_w_out is not None:
        dmixin = on_w_out(g["mix_w_out"], dmixin)
    kpe_raw, gate = _window(proj, _C_KPE, IN_PAD - _C_KPE), _window(proj, 3 * HG_WIDTH, HG_WIDTH)
    (do_hg, dgate, do_mla), (g["hg_out_norm"],) = _rows_bwd(
        _mix_c, [o_hg, gate, o_mla], [w["hg_out_norm"]], [dmixin], rgrad=[f32, f32, bf16], pgrad=[True],
        name=f"mix_dc_{tag}")
    dqfull, dkfull, dvfull = _att_bwd(qfull, kfull, vfull, o_mla, lse, do_mla, name=f"att_bwd_{tag}")
    pb = [w["mla_qn_nope"], w["mla_qn_rope"], w["mla_kn_nope"], w["mla_kn_rope"], w["rope_perm"]]
    (dqraw, dkvraw, dkpe_raw), pg = _rows_bwd(
        _mix_b, [qraw, kvraw, kpe_raw, cos, sin], pb, [dqfull, dkfull, dvfull],
        rgrad=[bf16, bf16, f32, None, None], pgrad=[True, True, True, True, False],
        name=f"mix_db_{tag}")
    g["mla_qn_nope"], g["mla_qn_rope"], g["mla_kn_nope"], g["mla_kn_rope"] = pg
    dcqn = _mm(dqraw, w["mla_w_uq"], tb=True, name=f"mla_dcq_{tag}")
    g["mla_w_uq"] = _mm(cqn, dqraw, ta=True, name=f"mla_dwuq_{tag}")
    dckvn = _mm(dkvraw, w["mla_w_ukv"], tb=True, name=f"mla_dckv_{tag}")
    g["mla_w_ukv"] = _mm(ckvn, dkvraw, ta=True, name=f"mla_dwukv_{tag}")
    dqf, dkk, dvv, dlogf = _hg_bwd(qf, kk, vv, logf, states, do_hg, name=f"hg_bwd_{tag}")
    pa = [w["lb0"], w["lb1"], w["mla_q_a_norm"], w["mla_kv_a_norm"]]
    (dproj,), (g["lb0"], g["lb1"], g["mla_q_a_norm"], g["mla_kv_a_norm"]) = _rows_bwd(
        _mix_a, [proj], pa, [dqf, dkk, dvv, dlogf, dcqn, dckvn], rgrad=[bf16], pgrad=[True] * 4,
        addends={0: [(dgate, 3 * HG_WIDTH), (dkpe_raw, _C_KPE)]}, name=f"mix_da_{tag}")
    dh, g["norm_mix"] = _mm(dproj, w["mix_w_in"], tb=True, rms_bwd=(h, w["norm_mix"], dout), name=f"mix_dhn_{tag}")
    g["mix_w_in_t"] = _mm(dproj, hn, ta=True, name=f"mix_dwin_{tag}")
    return dh, g


def _rope_tables(positions):
    inv_freq = 1.0 / (ROPE_BASE ** (jnp.arange(0, MLA_ROPE, 2, dtype=f32) / MLA_ROPE))
    ang = positions.astype(f32)[:, None] * inv_freq
    z = jnp.zeros((positions.shape[0], 128 - MLA_ROPE), f32)
    return (jnp.concatenate([jnp.cos(ang), jnp.cos(ang), z], axis=1),
            jnp.concatenate([jnp.sin(ang), jnp.sin(ang), z], axis=1))


def _pad_cols(a, n):
    return jnp.pad(a, ((0, 0), (0, n - a.shape[1])))


def _even_weights(p, j, layer, dt):
    w_uq = p["mla_w_uq"][j].reshape(MLA_Q_RANK, MLA_HEADS, MLA_QK)
    w_uq = jnp.pad(w_uq, ((0, 0), (0, 0), (0, MLA_QK_PAD - MLA_QK))).reshape(MLA_Q_RANK, MLA_HEADS * MLA_QK_PAD)
    return dict(
        norm_mix=p["norm_mix"][layer][None], mix_w_in=_pad_cols(p["mix_w_in"][j], IN_PAD).astype(dt),
        lb0=p["hg_lb_logits"][0][None], lb1=p["hg_lb_logits"][1][None],
        mla_q_a_norm=p["mla_q_a_norm"][j][None], mla_kv_a_norm=p["mla_kv_a_norm"][j][None],
        mla_w_uq=w_uq.astype(dt), mla_w_ukv=p["mla_w_ukv"][j].astype(dt),
        mla_qn_nope=p["mla_qn_nope"][j][None], mla_qn_rope=_pad_cols(p["mla_qn_rope"][j][None], 128),
        mla_kn_nope=p["mla_kn_nope"][j][None], mla_kn_rope=_pad_cols(p["mla_kn_rope"][j][None], 128),
        rope_perm=_rope_perm(), hg_out_norm=p["hg_out_norm"][j][None],
        mix_w_out=p["mix_w_out"][j].astype(dt) if "mix_w_out" in p else None)


def _even_grads(g):
    w_uq = g["mla_w_uq"].reshape(MLA_Q_RANK, MLA_HEADS, MLA_QK_PAD)[:, :, :MLA_QK].reshape(MLA_Q_RANK, -1)
    return dict(
        norm_mix=g["norm_mix"], mix_w_in=g["mix_w_in_t"][:IN_WIDTH].T[None],
        hg_lb_logits=jnp.concatenate([g["lb0"], g["lb1"]], axis=0),
        mla_q_a_norm=g["mla_q_a_norm"], mla_kv_a_norm=g["mla_kv_a_norm"], mla_w_uq=w_uq[None],
        mla_w_ukv=g["mla_w_ukv"][None], mla_qn_nope=g["mla_qn_nope"], mla_qn_rope=g["mla_qn_rope"][:, :MLA_ROPE],
        mla_kn_nope=g["mla_kn_nope"], mla_kn_rope=g["mla_kn_rope"][:, :MLA_ROPE],
        hg_out_norm=g["hg_out_norm"], mix_w_out=g["mix_w_out"][None])


_S5_NB = 8
_S5_BW = 1024
_S5_HALF = 512
_S5_UC = 128
_S5_TIME = 1024


def _cmul(ar, ai, br, bi):
    return ar * br - ai * bi, ar * bi + ai * br


def _pow_table(ar, ai, descending):
    rows = lax.broadcasted_iota(jnp.int32, (8, ar.shape[1]), 0)
    tr = jnp.zeros((8, ar.shape[1]), f32)
    ti = jnp.zeros((8, ar.shape[1]), f32)
    pr, pi_ = ar, ai
    for r in range(8):
        sel = rows == ((7 - r) if descending else r)
        tr = jnp.where(sel, pr, tr)
        ti = jnp.where(sel, pi_, ti)
        pr, pi_ = _cmul(pr, pi_, ar, ai)
    return tr, ti


def _s5_tile_scan(work, carry, ar, ai, tc, reverse, per_tile=None):
    hw = _S5_HALF
    row8 = lax.broadcasted_iota(jnp.int32, (8, hw), 0)
    powers = [(ar, ai)]
    for _ in range(2):
        powers.append(_cmul(*powers[-1], *powers[-1]))
    steps = []
    for (mr, mi), s in zip(powers, (1, 2, 4)):
        ok = (row8 < 8 - s) if reverse else (row8 >= s)
        steps.append((jnp.where(ok, mr, 0.0), jnp.where(ok, mi, 0.0), 8 - s if reverse else s))
    tr, ti = _pow_table(ar, ai, reverse)
    cr, ci = carry[:, :hw], carry[:, hw:]
    tiles = range(tc // 8)
    for i in (reversed(tiles) if reverse else tiles):
        sl = slice(8 * i, 8 * i + 8)
        xr, xi = work[sl, :hw], work[sl, hw:]
        for mr, mi, shift in steps:
            pr, pi_ = _cmul(mr, mi, pltpu.roll(xr, shift, axis=0), pltpu.roll(xi, shift, axis=0))
            xr, xi = xr + pr, xi + pi_
        pr, pi_ = _cmul(tr, ti, cr, ci)
        xr, xi = xr + pr, xi + pi_
        work[sl, :hw] = xr
        work[sl, hw:] = xi
        if per_tile is not None:
            per_tile(sl, xr, xi, cr, ci)
        edge = 8 * i if reverse else 8 * i + 7
        cr, ci = work[edge:edge + 1, :hw], work[edge:edge + 1, hw:]
    carry[:, :hw] = cr
    carry[:, hw:] = ci


def _s5_core_fwd(a, hn, b3, c3, *, name):
    length = hn.shape[0]
    tc = min(_S5_TIME, length)

    def body(a_ref, hn_ref, b_ref, c_ref, hs_ref, y_ref, work, carry):
        @pl.when(pl.program_id(1) == 0)
        def _():
            carry[...] = jnp.zeros_like(carry)

        work[...] = jnp.dot(hn_ref[...].astype(bf16), b_ref[...], preferred_element_type=f32)
        _s5_tile_scan(work, carry, a_ref[:, :_S5_HALF], a_ref[:, _S5_HALF:], tc, False)
        hs = work[...].astype(bf16)
        hs_ref[...] = hs
        y_ref[...] = jnp.dot(hs, c_ref[...], preferred_element_type=f32)

    return pl.pallas_call(
        body, grid=(_S5_NB, length // tc),
        in_specs=[pl.BlockSpec((1, _S5_BW), lambda j, t: (0, j)), pl.BlockSpec((tc, _S5_UC), lambda j, t: (t, j)),
                  pl.BlockSpec((_S5_UC, _S5_BW), lambda j, t: (j, 0)), pl.BlockSpec((_S5_BW, _S5_UC), lambda j, t: (j, 0))],
        out_specs=[pl.BlockSpec((tc, _S5_BW), lambda j, t: (t, j)), pl.BlockSpec((tc, _S5_UC), lambda j, t: (t, j))],
        out_shape=[jax.ShapeDtypeStruct((length, _S5_NB * _S5_BW), bf16),
                   jax.ShapeDtypeStruct((length, _S5_NB * _S5_UC), f32)],
        scratch_shapes=[pltpu.VMEM((tc, _S5_BW), f32), pltpu.VMEM((1, _S5_BW), f32)],
        compiler_params=_cparams(("parallel", "arbitrary")), name=name)(a, hn, b3, c3)


def _s5_core_bwd(a, dy, c3, hs, hn, b3, *, name):
    length = hn.shape[0]
    tc = min(_S5_TIME, length)
    nt = length // tc
    hw = _S5_HALF

    def body(a_ref, dy_ref, c_ref, hs_ref, hn_ref, b_ref, du_ref, db_ref, dc_ref, da_ref, work, carry, acc):
        @pl.when(pl.program_id(1) == 0)
        def _():
            carry[...] = jnp.zeros_like(carry)
            db_ref[...] = jnp.zeros_like(db_ref)
            dc_ref[...] = jnp.zeros_like(dc_ref)
            da_ref[...] = jnp.zeros_like(da_ref)

        dyb = dy_ref[...].astype(bf16)
        work[...] = lax.dot_general(dyb, c_ref[...], _NT, preferred_element_type=f32)
        acc[...] = jnp.zeros_like(acc)
        row8 = lax.broadcasted_iota(jnp.int32, (8, hw), 0)

        def grad_a(sl, gr, gi, cr, ci):
            gnr = jnp.where(row8 == 7, cr, pltpu.roll(gr, 7, axis=0))
            gni = jnp.where(row8 == 7, ci, pltpu.roll(gi, 7, axis=0))
            hr, hi = hs_ref[sl, :hw].astype(f32), hs_ref[sl, hw:].astype(f32)
            acc[:, :hw] += hr * gnr + hi * gni
            acc[:, hw:] += hr * gni - hi * gnr

        _s5_tile_scan(work, carry, a_ref[:, :hw], -a_ref[:, hw:], tc, True, grad_a)
        da_ref[...] += jnp.sum(acc[...], axis=0, keepdims=True)
        g = work[...].astype(bf16)
        du_ref[...] = lax.dot_general(g, b_ref[...], _NT, preferred_element_type=f32)
        db_ref[...] += lax.dot_general(hn_ref[...].astype(bf16), g, _TN, preferred_element_type=f32)
        dc_ref[...] += lax.dot_general(hs_ref[...], dyb, _TN, preferred_element_type=f32)

    rev = lambda j, t: (nt - 1 - t, j)
    return pl.pallas_call(
        body, grid=(_S5_NB, nt),
        in_specs=[pl.BlockSpec((1, _S5_BW), lambda j, t: (0, j)), pl.BlockSpec((tc, _S5_UC), rev),
                  pl.BlockSpec((_S5_BW, _S5_UC), lambda j, t: (j, 0)), pl.BlockSpec((tc, _S5_BW), rev),
                  pl.BlockSpec((tc, _S5_UC), rev), pl.BlockSpec((_S5_UC, _S5_BW), lambda j, t: (j, 0))],
        out_specs=[pl.BlockSpec((tc, _S5_UC), rev), pl.BlockSpec((_S5_UC, _S5_BW), lambda j, t: (j, 0)),
                   pl.BlockSpec((_S5_BW, _S5_UC), lambda j, t: (j, 0)), pl.BlockSpec((1, _S5_BW), lambda j, t: (0, j))],
        out_shape=[jax.ShapeDtypeStruct((length, _S5_NB * _S5_UC), f32),
                   jax.ShapeDtypeStruct((_S5_NB * _S5_UC, _S5_BW), f32),
                   jax.ShapeDtypeStruct((_S5_NB * _S5_BW, _S5_UC), f32),
                   jax.ShapeDtypeStruct((1, _S5_NB * _S5_BW), f32)],
        scratch_shapes=[pltpu.VMEM((tc, _S5_BW), f32), pltpu.VMEM((1, _S5_BW), f32), pltpu.VMEM((8, _S5_BW), f32)],
        compiler_params=_cparams(("parallel", "arbitrary")), name=name)(a, dy, c3, hs, hn, b3)


def _s5_disc(lr, li, ldt, btr, bti, expand):
    dt = jnp.exp(ldt)
    mag = jnp.exp(lr * dt)
    abr = mag * jnp.cos(li * dt)
    abi = mag * jnp.sin(li * dt)
    den = lr * lr + li * li
    zr = ((abr - 1.0) * lr + abi * li) / den
    zi = (abi * lr - (abr - 1.0) * li) / den
    zr = jnp.dot(zr, expand, precision=lax.Precision.HIGHEST, preferred_element_type=f32)
    zi = jnp.dot(zi, expand, precision=lax.Precision.HIGHEST, preferred_element_type=f32)
    return abr, abi, zr * btr - zi * bti, zr * bti + zi * btr


def _s5_disc_fwd(args, *, name):
    def body(*refs):
        res = _s5_disc(*[r[...] for r in refs[:6]])
        for o, v in zip(refs[6:], res):
            o[...] = v

    sds = jax.ShapeDtypeStruct
    return pl.pallas_call(body, out_shape=[sds(args[0].shape, f32)] * 2 + [sds(args[3].shape, f32)] * 2,
                          name=name)(*args)


def _s5_disc_bwd(args, cts, *, name):
    def body(*refs):
        vals = [r[...] for r in refs[:6]]
        _, vjp = jax.vjp(lambda *d: _s5_disc(*d, vals[5]), *vals[:5])
        grads = vjp(tuple(r[...] for r in refs[6:10]))
        for o, v in zip(refs[10:], grads):
            o[...] = v

    return pl.pallas_call(body, out_shape=[jax.ShapeDtypeStruct(a.shape, f32) for a in args[:5]],
                          name=name)(*args, *cts)


def _gelu_tanh(x):
    return 0.5 * x * (1.0 + jnp.tanh(0.7978845608028654 * (x + 0.044715 * (x * x * x))))


def _s5_post(y, u, d_skip):
    return _gelu_tanh(y + d_skip * u)


def _s5_glu(ga, gb, h):
    return h + ga * jax.nn.sigmoid(gb)


def _s5_glu_norm(ga, gb, h, next_gain):
    out = _s5_glu(ga, gb, h)
    return out, _rms(out, next_gain)


def _s5_expand():
    e = np.zeros((S5_STATE, S5_GROUP * S5_STATE), np.float32)
    for m in range(S5_GROUP):
        e[np.arange(S5_STATE), m * S5_STATE + np.arange(S5_STATE)] = 1.0
    return jnp.asarray(e)


def _s5_pack_b(bbr, bbi):
    eye = jnp.eye(8, dtype=f32)

    def one(bb):
        b5 = bb.reshape(_S5_NB, 8, S5_GROUP, S5_STATE)
        return jnp.einsum("jgmp,gh->jgmhp", b5, eye).reshape(_S5_NB * _S5_UC, _S5_HALF)

    return jnp.concatenate([one(bbr), one(bbi)], axis=1)


def _s5_unpack_b(db3):
    def one(d):
        d5 = d.reshape(_S5_NB, 8, S5_GROUP, 8, S5_STATE)
        return jnp.einsum("jgmgp->jgmp", d5).reshape(S5_GROUPS, S5_GROUP * S5_STATE)

    return one(db3[:, :_S5_HALF]), one(db3[:, _S5_HALF:])


def _s5_pack_c(c_re, c_im):
    eye = jnp.eye(8, dtype=f32)

    def one(c):
        c4 = c.reshape(_S5_NB, 8, S5_GROUP, S5_STATE)
        return jnp.einsum("jgmp,hg->jhpgm", c4, eye).reshape(_S5_NB, _S5_HALF, _S5_UC)

    return jnp.concatenate([one(c_re), -one(c_im)], axis=1).reshape(_S5_NB * _S5_BW, _S5_UC)


def _s5_unpack_c(dc3):
    d = dc3.reshape(_S5_NB, 2, 8, S5_STATE, 8, S5_GROUP)
    dre = jnp.einsum("jgpgm->jgmp", d[:, 0]).reshape(S5_GROUPS, S5_GROUP, S5_STATE)
    dim = -jnp.einsum("jgpgm->jgmp", d[:, 1]).reshape(S5_GROUPS, S5_GROUP, S5_STATE)
    return dre, dim


def _s5_state_row(re, im):
    r = re.reshape(_S5_NB, 1, _S5_HALF)
    i = im.reshape(_S5_NB, 1, _S5_HALF)
    return jnp.concatenate([r, i], axis=2).reshape(1, _S5_NB * _S5_BW)


def _s5_unstate_row(row):
    r = row.reshape(_S5_NB, 2, 8, S5_STATE)
    return r[:, 0].reshape(S5_GROUPS, S5_STATE), r[:, 1].reshape(S5_GROUPS, S5_STATE)


def _s5_fwd(h, w, tag, next_gain):
    d = h.shape[1]
    hn, = _rows(_rms, [h], [w["norm_mix"]], [(d, f32)], name=f"s5_norm_{tag}")
    disc_in = [w["s5_lam_re"], w["s5_lam_im"], w["s5_log_dt"], w["s5_bt_re"], w["s5_bt_im"], w["s5_expand"]]
    abr, abi, bbr, bbi = _s5_disc_fwd(disc_in, name=f"s5_disc_{tag}")
    a_row = _s5_state_row(abr, abi)
    b3 = _s5_pack_b(bbr, bbi).astype(bf16)
    hs, y = _s5_core_fwd(a_row, hn, b3, w["s5_c3"], name=f"s5_core_{tag}")
    yg, = _rows(_s5_post, [y, hn], [w["s5_d"]], [(d, bf16)], name=f"s5_post_{tag}")
    ga = _mm(yg, w["s5_w_glu_a"], out_dtype=bf16, name=f"s5_glu_a_{tag}")
    gb = _mm(yg, w["s5_w_glu_b"], out_dtype=bf16, name=f"s5_glu_b_{tag}")
    out, normed = _rows(_s5_glu_norm, [ga, gb, h], [next_gain], [(d, f32), (d, bf16)], name=f"s5_glu_{tag}")
    return out, (h, hn, disc_in, a_row, b3, hs, y, yg, ga, gb), normed


def _s5_bwd(dout, w, saved, tag):
    h, hn, disc_in, a_row, b3, hs, y, yg, ga, gb = saved
    g = {}
    (dga, dgb), _ = _rows_bwd(_s5_glu, [ga, gb, h], [], [dout], rgrad=[bf16, bf16, None], pgrad=[],
                              name=f"s5_dglu_{tag}")
    dyg = _mm(dga, w["s5_w_glu_a"], tb=True, name=f"s5_dyg_a_{tag}")
    dyg = _mm(dgb, w["s5_w_glu_b"], tb=True, add=dyg, out_dtype=bf16, name=f"s5_dyg_b_{tag}")
    g["s5_w_glu_a"] = _mm(yg, dga, ta=True, out_dtype=bf16, name=f"s5_dwa_{tag}")
    g["s5_w_glu_b"] = _mm(yg, dgb, ta=True, out_dtype=bf16, name=f"s5_dwb_{tag}")
    (dy, du_skip), (g["s5_d"],) = _rows_bwd(_s5_post, [y, hn], [w["s5_d"]], [dyg], rgrad=[bf16, f32], pgrad=[True],
                                           name=f"s5_dpost_{tag}")
    du, db3, dc3, da_row = _s5_core_bwd(a_row, dy, w["s5_c3"], hs, hn, b3, name=f"s5_dcore_{tag}")
    dabr, dabi = _s5_unstate_row(da_row)
    dbbr, dbbi = _s5_unpack_b(db3)
    g["s5_lam_re"], g["s5_lam_im"], g["s5_log_dt"], g["s5_bt_re"], g["s5_bt_im"] = _s5_disc_bwd(
        disc_in, [dabr, dabi, dbbr, dbbi], name=f"s5_ddisc_{tag}")
    g["s5_c_re"], g["s5_c_im"] = _s5_unpack_c(dc3)
    (dh,), (g["norm_mix"],) = _rows_bwd(_rms_twice, [h], [w["norm_mix"]], [du, du_skip], rgrad=[f32], pgrad=[True],
                                        addends={0: dout}, name=f"s5_dnorm_{tag}")
    return dh, g


def _odd_weights(p, j, layer, dt):
    tr = lambda b: b.transpose(0, 2, 1).reshape(S5_GROUPS, S5_GROUP * S5_STATE)
    return dict(
        norm_mix=p["norm_mix"][layer][None], s5_lam_re=p["s5_lam_re"][j], s5_lam_im=p["s5_lam_im"][j],
        s5_log_dt=p["s5_log_dt"][j][:, None], s5_bt_re=tr(p["s5_b_re"][j]), s5_bt_im=tr(p["s5_b_im"][j]),
        s5_expand=_s5_expand(), s5_c3=_s5_pack_c(p["s5_c_re"][j], p["s5_c_im"][j]).astype(dt),
        **{n: (p[n][j][None] if n == "s5_d" else p[n][j].astype(dt))
           for n in ("s5_d", "s5_w_glu_a", "s5_w_glu_b") if n in p})


def _odd_grads(g):
    tr = lambda b: b.reshape(S5_GROUPS, S5_GROUP, S5_STATE).transpose(0, 2, 1)[None]
    return dict(
        norm_mix=g["norm_mix"], s5_lam_re=g["s5_lam_re"][None], s5_lam_im=g["s5_lam_im"][None],
        s5_log_dt=g["s5_log_dt"][:, 0][None], s5_b_re=tr(g["s5_bt_re"]), s5_b_im=tr(g["s5_bt_im"]),
        s5_c_re=g["s5_c_re"][None], s5_c_im=g["s5_c_im"][None], s5_d=g["s5_d"],
        s5_w_glu_a=g["s5_w_glu_a"][None], s5_w_glu_b=g["s5_w_glu_b"][None])


FF_SHARD = 352
FF_SHARD_PAD = 384


def _pad_groups(a, axis):
    axis %= a.ndim
    zeros = jnp.zeros(a.shape[:axis] + (FF_SHARD_PAD - FF_SHARD,) + a.shape[axis + 1:], a.dtype)
    pieces = []
    for g in range(a.shape[axis] // FF_SHARD):
        pieces += [lax.slice_in_dim(a, g * FF_SHARD, (g + 1) * FF_SHARD, axis=axis), zeros]
    return jnp.concatenate(pieces, axis=axis)


def _unpad_groups(a, axis):
    axis %= a.ndim
    pieces = [lax.slice_in_dim(a, g * FF_SHARD_PAD, g * FF_SHARD_PAD + FF_SHARD, axis=axis)
              for g in range(a.shape[axis] // FF_SHARD_PAD)]
    return pieces[0] if len(pieces) == 1 else jnp.concatenate(pieces, axis=axis)


_LANES = 1024
_ROW_PAD = 16


_PEER_MASKS = (1, 2, 4, 3, 5, 6, 7)


def _mesh_place():
    x, y, c = lax.axis_index("x"), lax.axis_index("y"), lax.axis_index("c")

    def peer(mask):
        px = 1 - x if mask & 4 else x
        py = 1 - y if mask & 2 else y
        pc = 1 - c if mask & 1 else c
        return (px, py, pc), 4 * px + 2 * py + pc

    return 4 * x + 2 * y + c, peer


class _Exchange:
    def __init__(self, name):
        self.name = name
        self.srcs, self.shapes, self.items, self.where = [], [], [], {}

    def add(self, src, land_shape, src_at, dst_at, key):
        si = next((i for i, s in enumerate(self.srcs) if s is src), None)
        if si is None:
            self.srcs.append(src)
            si = len(self.srcs) - 1
        if key not in self.where:
            self.shapes.append(land_shape)
            self.where[key] = len(self.shapes) - 1
        self.items.append(dict(src=si, dst=self.where[key], src_at=src_at, dst_at=dst_at))

    def _copy(self, k, mask, ins, lands, send_sems, recv_sems, me, peer, arriving):
        it = self.items[k]
        dev, idx = peer(mask)
        s = k * (N_DEV - 1) + _PEER_MASKS.index(mask)
        return pltpu.make_async_remote_copy(
            src_ref=it["src_at"](ins[it["src"]], idx), dst_ref=it["dst_at"](lands[it["dst"]], idx if arriving else me),
            send_sem=send_sems.at[s], recv_sem=recv_sems.at[s], device_id=dev, device_id_type=pl.DeviceIdType.MESH)

    def _own_copy(self, k, ins, lands, own_sems, me):
        it = self.items[k]
        return pltpu.make_async_copy(it["src_at"](ins[it["src"]], me), it["dst_at"](lands[it["dst"]], me), own_sems.at[k])

    def begin(self, own):
        ns, nd, ni = len(self.srcs), len(self.shapes), len(self.items)
        nsem = ni * (N_DEV - 1)
        self.own = own

        nq = 3 if own else 2

        def body(*refs):
            ins, land_refs = refs[:ns], refs[ns:ns + nd]
            sems, token = refs[ns + nd:ns + nd + nq], refs[-1]
            me, peer = _mesh_place()
            for mask in _PEER_MASKS:
                for k in range(ni):
                    self._copy(k, mask, ins, land_refs, sems[0], sems[1], me, peer, False).start()
            if own:
                for k in range(ni):
                    self._own_copy(k, ins, land_refs, sems[2], me).start()
            token[...] = jnp.zeros_like(token)

        hbm = pl.BlockSpec(memory_space=pltpu.HBM)
        sem = pl.BlockSpec(memory_space=pltpu.SEMAPHORE)
        lands = [lax.empty(s.shape, s.dtype) for s in self.shapes]
        sem_shapes = [pltpu.SemaphoreType.DMA((nsem,)), pltpu.SemaphoreType.DMA((nsem,)), pltpu.SemaphoreType.DMA((ni,))]
        res = pl.pallas_call(
            body, in_specs=[hbm] * (ns + nd),
            out_specs=[sem] * nq + [hbm] * nd + [pl.BlockSpec(memory_space=pltpu.VMEM)],
            out_shape=sem_shapes[:nq] + [pltpu.HBM(s.shape, s.dtype) for s in self.shapes]
            + [jax.ShapeDtypeStruct((8, 128), f32)],
            input_output_aliases={ns + j: nq + j for j in range(nd)},
            compiler_params=pltpu.CompilerParams(has_side_effects=pltpu.SideEffectType.DATAFLOW_SIDE_EFFECTING),
            name=self.name + "_start")(*self.srcs, *lands)
        self.token = res[-1]
        return list(res[:nq]), list(res[nq:-1])

    def finish(self, state, after):
        sems, lands = state
        nq = len(sems)
        after = list(after) if isinstance(after, (list, tuple)) else [after]
        ns, nd, ni = len(self.srcs), len(self.shapes), len(self.items)

        def body(*refs):
            ins, land_refs = refs[:ns], refs[ns:ns + nd]
            sem_refs = refs[ns + nd:ns + nd + nq]
            me, peer = _mesh_place()
            for mask in _PEER_MASKS:
                for k in range(ni):
                    cp = self._copy(k, mask, ins, land_refs, sem_refs[0], sem_refs[1], me, peer, True)
                    cp.wait_send()
                    cp.wait_recv()
            if self.own:
                for k in range(ni):
                    self._own_copy(k, ins, land_refs, sem_refs[2], me).wait()

        hbm = pl.BlockSpec(memory_space=pltpu.HBM)
        sem = pl.BlockSpec(memory_space=pltpu.SEMAPHORE)
        res = pl.pallas_call(
            body, in_specs=[hbm] * (ns + nd) + [sem] * nq + [pl.BlockSpec(memory_space=pl.ANY)] * len(after),
            out_specs=[hbm] * nd, out_shape=[pltpu.HBM(s.shape, s.dtype) for s in self.shapes],
            input_output_aliases={ns + j: j for j in range(nd)},
            compiler_params=pltpu.CompilerParams(has_side_effects=pltpu.SideEffectType.DATAFLOW_SIDE_EFFECTING),
            name=self.name + "_wait")(*self.srcs, *lands, *sems, *after)
        return {k: res[i] for k, i in self.where.items()}


def _after(x, *tokens, name):
    def body(*refs):
        del refs

    anyspace = pl.BlockSpec(memory_space=pl.ANY)
    return pl.pallas_call(body, in_specs=[anyspace] * (1 + len(tokens)), out_specs=anyspace,
                          out_shape=jax.ShapeDtypeStruct(x.shape, x.dtype), input_output_aliases={0: 0},
                          name=name)(x, *tokens)


def _rows_of(n):
    return lambda r, i: r.at[pl.ds(pl.multiple_of(i * n, n), n), :]


def _cols_of(n):
    return lambda r, i: r.at[:, pl.ds(pl.multiple_of(i * n, n), n)]


def _whole(r, i):
    return r


def _slot(r, i):
    return r.at[i]


def _at_layer(layer):
    return lambda r, i: r.at[layer]


def _sum_adam(me_index, slots, owns, own_block, w, m, v, *, name):
    layers, rows, cols = w.shape
    tr = _pick(rows, (256, 128, 104, 64, 32, 16, 8))
    bc1 = 1.0 - ADAM_B1 ** ADAM_STEP
    bc2 = 1.0 - ADAM_B2 ** ADAM_STEP
    own_shape, own_map = own_block(tr)
    nl = len(slots)
    assert nl == layers and len(owns) == layers

    def body(me_ref, *refs):
        s_refs, own_refs = refs[:nl], refs[nl:2 * nl]
        w_ref, m_ref, v_ref, g_ref, d_ref, nm_ref, nv_ref = refs[2 * nl:]
        me = me_ref[0]

        def run(s_ref, own_ref):
            mine = (own_ref[0] if len(own_shape) == 3 else own_ref[...]).astype(f32)
            g = jnp.where(me == 0, mine, s_ref[0].astype(f32))
            for k in range(1, N_DEV):
                g = g + jnp.where(me == k, mine, s_ref[k].astype(f32))
            mm = ADAM_B1 * m_ref[0] + (1.0 - ADAM_B1) * g
            vv = ADAM_B2 * v_ref[0] + (1.0 - ADAM_B2) * (g * g)
            g_ref[0] = g
            nm_ref[0] = mm
            nv_ref[0] = vv
            d_ref[0] = -ADAM_LR * ((mm / bc1) / (jnp.sqrt(vv / bc2) + ADAM_EPS) + ADAM_WD * w_ref[0])

        for layer in range(nl):
            pl.when(pl.program_id(0) == layer)(functools.partial(run, s_refs[layer], own_refs[layer]))

    def of_layer(layer, index_map):
        return lambda lyr, i, me: index_map(jnp.where(lyr == layer, i, 0), me)

    blk = pl.BlockSpec((1, tr, cols), lambda lyr, i, me: (lyr, i, 0))
    sds = jax.ShapeDtypeStruct((layers, rows, cols), f32)
    grid_spec = pltpu.PrefetchScalarGridSpec(
        num_scalar_prefetch=1, grid=(layers, rows // tr),
        in_specs=[pl.BlockSpec((N_DEV, tr, cols), of_layer(layer, lambda i, me: (0, i, 0))) for layer in range(nl)]
        + [pl.BlockSpec(own_shape, of_layer(layer, own_map)) for layer in range(nl)] + [blk, blk, blk],
        out_specs=[blk] * 4)
    return pl.pallas_call(body, grid_spec=grid_spec, out_shape=[sds] * 4,
                          compiler_params=_cparams(("arbitrary", "arbitrary")),
                          name=name)(me_index, *slots, *owns, w, m, v)


_SHARDED = dict(xa_wq=1, xa_wk=1, xa_wv=1, xa_wo=1, ffn_w_up=2, ffn_conv_w=2, ffn_w_down=1, mix_w_in=2, mla_w_uq=2,
                mla_w_ukv=2, mix_w_out=1, s5_d=1, s5_w_glu_a=1, s5_w_glu_b=1)
_EXACT = ("ffn_conv_w", "s5_d")
_WEIGHTS = ("norm_mix", "norm_xa", "norm_mem", "norm_ffn", "xa_wq", "xa_wk", "xa_wv", "xa_wo", "xa_q_norm",
            "xa_k_norm", "ffn_w_up", "ffn_conv_w", "ffn_conv_b", "ffn_w_down", "hg_lb_logits", "mix_w_in",
            "hg_out_norm", "mla_q_a_norm", "mla_w_uq", "mla_kv_a_norm", "mla_w_ukv", "mla_qn_nope", "mla_qn_rope",
            "mla_kn_nope", "mla_kn_rope", "mix_w_out", "s5_lam_re", "s5_lam_im", "s5_log_dt", "s5_b_re", "s5_b_im",
            "s5_c_re", "s5_c_im", "s5_d", "s5_w_glu_a", "s5_w_glu_b")
_BIG = tuple(n for n in _WEIGHTS if n in _SHARDED and n not in _EXACT)
_SHARD_ORDER = tuple(n for n in _WEIGHTS if n in _SHARDED)
_REPL_ORDER = tuple(n for n in _WEIGHTS if n not in _SHARDED)
_REPL_EARLY = tuple(n for n in _REPL_ORDER if n.startswith("s5_"))
_REPL_LATE = tuple(n for n in _REPL_ORDER if n not in _REPL_EARLY)


def _pack(parts, dtype, lead=None):
    nl = 0 if lead is None else 1
    flat = [a.astype(dtype).reshape(a.shape[:nl] + (-1,)) for a in parts]
    cat = jnp.concatenate(flat, axis=nl)
    n = cat.shape[nl]
    unit = _LANES * _ROW_PAD
    total = -(-n // unit) * unit
    cat = jnp.pad(cat, [(0, 0)] * nl + [(0, total - n)])
    return cat.reshape(cat.shape[:nl] + (total // _LANES, _LANES))


def _unpack(packed, shapes, lead=None):
    nl = 0 if lead is None else 1
    flat = packed.reshape(packed.shape[:nl] + (-1,))
    out, off = [], 0
    for s in shapes:
        n = int(np.prod(s))
        piece = flat[..., off:off + n] if nl else flat[off:off + n]
        out.append(piece.reshape(packed.shape[:nl] + tuple(s)))
        off += n
    return out


def _to_full(gathered, axis):
    g = jnp.moveaxis(gathered, 0, axis)
    s = g.shape
    return g.reshape(s[:axis] + (s[axis] * s[axis + 1],) + s[axis + 2:])


def _to_shards(full, axis):
    s = full.shape
    g = full.reshape(s[:axis] + (N_DEV, s[axis] // N_DEV) + s[axis + 1:])
    return jnp.moveaxis(g, axis, 0)


_DIRECT_ROWS = ("xa_wq", "xa_wk", "xa_wv", "xa_wo", "mix_w_out", "s5_w_glu_a", "s5_w_glu_b")
_SMALL16 = ("mix_w_in", "mla_w_uq", "mla_w_ukv")
_SMALL_SHARDED = ("mla_w_uq", "mla_w_ukv") + _EXACT
_SHARD_ROWS = 128


def _exchange_layout(d):
    out = dict(d)
    out["ffn_w_up"] = _pad_groups(d["ffn_w_up"], 2)
    out["ffn_conv_w"] = _pad_groups(d["ffn_conv_w"], 2)
    out["ffn_w_down"] = _pad_groups(d["ffn_w_down"], 1)
    return out


def _train_step(x, mem, positions, target, w, m, v):
    d_model = x.shape[1]
    we_, me_, ve_ = _exchange_layout(w), _exchange_layout(m), _exchange_layout(v)
    sds = jax.ShapeDtypeStruct

    matrices = _DIRECT_ROWS + ("ffn_w_up", "ffn_w_down")
    layer_mats = ("xa_wq", "xa_wk", "xa_wv", "xa_wo", "ffn_w_up", "ffn_w_down")
    shard16 = {n: we_[n].astype(bf16) for n in matrices}
    part_of = {n: _rows_of(_SHARD_ROWS) for n in _DIRECT_ROWS}
    part_of["ffn_w_up"] = _cols_of(we_["ffn_w_up"].shape[2])
    part_of["ffn_w_down"] = _rows_of(we_["ffn_w_down"].shape[1])
    part_shape = {n: we_[n].shape[1:] for n in matrices}

    def full_shape(n):
        r, c = part_shape[n]
        return (r, N_DEV * c) if n == "ffn_w_up" else (N_DEV * r, c)

    def gather(ex, n, layer):
        ex.add(shard16[n], sds(full_shape(n), bf16), _at_layer(layer), part_of[n], (n, layer))

    def scatter(ex, n, layer, grad):
        ex.add(grad, sds((N_DEV,) + part_shape[n], grad.dtype), part_of[n], _slot, (n, layer))

    small16 = _pack([we_[n] for n in _SMALL16], bf16)
    exact = _pack([we_[n] for n in _EXACT], f32)
    ga, ga1, gb, gc = _Exchange("gather_a"), _Exchange("gather_a1"), _Exchange("gather_b"), _Exchange("gather_c")
    ga.add(small16, sds((N_DEV,) + small16.shape, bf16), _whole, _slot, "small16")
    ga1.add(exact, sds((N_DEV,) + exact.shape, f32), _whole, _slot, "exact")
    gather(ga1, "mix_w_out", 0)
    for n in layer_mats:
        gather(gb, n, 0)
    gather(gc, "s5_w_glu_a", 0)
    gather(gc, "s5_w_glu_b", 0)
    for n in layer_mats:
        gather(gc, n, 1)
    state_a, state_a1, state_b, state_c = ga.begin(True), ga1.begin(True), gb.begin(True), gc.begin(True)

    p = {n: w[n] for n in _REPL_ORDER}
    cos, sin = _rope_tables(positions)
    wo = _odd_weights(p, 0, 1, bf16)
    conv_b = _pad_groups(w["ffn_conv_b"], 1)
    prepared = [cos, sin, conv_b, wo["s5_c3"], wo["s5_bt_re"], wo["s5_bt_im"]]
    full = ga.finish(state_a, [ga1.token, gb.token, gc.token] + prepared)
    for n, a in zip(_SMALL16, _unpack(full["small16"], [we_[n].shape for n in _SMALL16], lead=True)):
        p[n] = _to_full(a, _SHARDED[n])
    we = _even_weights(p, 0, 0, bf16)
    we["norm_mix"] = _after(we["norm_mix"], ga.token, ga1.token, gb.token, gc.token, name="after_gather_starts")

    def late_mix_w_out(mixin):
        full.update(ga1.finish(state_a1, [mixin]))
        return full[("mix_w_out", 0)]

    we["mix_w_out"] = late_mix_w_out
    h, s_mix0, hx = _mixer_fwd(x, cos, sin, we, "l0", w["norm_xa"][0][None])
    conv_w, s5_d = [_to_full(a, _SHARDED[n]) for n, a in
                    zip(_EXACT, _unpack(full["exact"], [we_[n].shape for n in _EXACT], lead=True))]

    def layer_weights(layer):
        return dict(norm_xa=w["norm_xa"][layer][None], norm_mem=w["norm_mem"][layer][None],
                    norm_ffn=w["norm_ffn"][layer][None], xa_q_norm=w["xa_q_norm"][layer][None],
                    xa_k_norm=w["xa_k_norm"][layer][None], ffn_conv_w=conv_w[layer],
                    ffn_conv_b=conv_b[layer][None], **{n: full[(n, layer)] for n in layer_mats})

    full.update(gb.finish(state_b, h))
    wl = [layer_weights(0)]
    h, s_xa0, hf = _xattn_fwd(h, mem, wl[0], "l0", hx)
    h, s_ff0 = _ffn_fwd(h, wl[0], "l0", hf)
    full.update(gc.finish(state_c, h))
    wl.append(layer_weights(1))
    wo.update(s5_d=s5_d, s5_w_glu_a=full[("s5_w_glu_a", 0)], s5_w_glu_b=full[("s5_w_glu_b", 0)])
    h, s_mix1, hx = _s5_fwd(h, wo, "l1", wl[1]["norm_xa"])
    h, s_xa1, hf = _xattn_fwd(h, mem, wl[1], "l1", hx)
    (dh, loss), s_ff1 = _ffn_fwd(h, wl[1], "l1", hf, loss_target=target)

    gl = [{}, {}]
    reduces = []

    own_grad = {}

    def reduce_start(name, entries, dh):
        ex = _Exchange(name)
        for n, layer, grad in entries.get("matrices", ()):
            scatter(ex, n, layer, grad)
            own_grad[(n, layer)] = grad
        for key, src, shape, src_at in entries.get("packs", ()):
            ex.add(src, shape, src_at, _slot, key)
        reduces.append((ex, ex.begin(False)))
        return _after(dh, ex.token, name="after_" + name)

    dh, gl[1] = _ffn_bwd(dh, wl[1], s_ff1, "l1")
    dh = reduce_start("reduce_ffn1", dict(matrices=[(n, 1, gl[1][n]) for n in ("ffn_w_up", "ffn_w_down")]), dh)
    dh, g = _xattn_bwd(dh, mem, wl[1], s_xa1, "l1")
    gl[1].update(g)
    dh = reduce_start("reduce_xa1", dict(matrices=[(n, 1, g[n]) for n in ("xa_wq", "xa_wk", "xa_wv", "xa_wo")]), dh)
    dh, g_odd = _s5_bwd(dh, wo, s_mix1, "l1")
    go = _odd_grads(g_odd)
    dh, gl[0] = _ffn_bwd(dh, wl[0], s_ff0, "l0")
    send_early = _pack([go[n].reshape(w[n].shape) for n in _REPL_EARLY], f32)
    dh = reduce_start("reduce_ffn0", dict(
        matrices=[(n, 0, g_odd[n]) for n in ("s5_w_glu_a", "s5_w_glu_b")]
        + [(n, 0, gl[0][n]) for n in ("ffn_w_up", "ffn_w_down")],
        packs=[("repl_early", send_early, sds((N_DEV,) + send_early.shape, f32), _whole)]), dh)
    dh, g = _xattn_bwd(dh, mem, wl[0], s_xa0, "l0")
    gl[0].update(g)
    dh = reduce_start("reduce_xa0", dict(matrices=[(n, 0, g[n]) for n in ("xa_wq", "xa_wk", "xa_wv", "xa_wo")]), dh)
    grad_x, g_even = _mixer_bwd(
        dh, cos, sin, we, s_mix0, "l0",
        on_w_out=lambda grad, dmixin: reduce_start("reduce_w_out", dict(matrices=[("mix_w_out", 0, grad)]), dmixin))

    ge = _even_grads(g_even)
    cat = lambda n: jnp.concatenate([gl[0][n], gl[1][n]], axis=0)
    rg = dict(ge)
    rg["norm_mix"] = jnp.concatenate([ge["norm_mix"], go["norm_mix"]], axis=0)
    for n in ("norm_xa", "norm_mem", "norm_ffn", "xa_q_norm", "xa_k_norm"):
        rg[n] = cat(n)
    rg["ffn_conv_b"] = _unpad_groups(cat("ffn_conv_b"), 1)
    sg = dict(mla_w_uq=ge["mla_w_uq"], mla_w_ukv=ge["mla_w_ukv"], s5_d=go["s5_d"],
              ffn_conv_w=jnp.stack([gl[0]["ffn_conv_w"], gl[1]["ffn_conv_w"]]))
    send_small = _pack([_to_shards(sg[n], _SHARDED[n]) for n in _SMALL_SHARDED], f32, lead=True)
    send_late = _pack([rg[n].reshape(w[n].shape) for n in _REPL_LATE], f32)
    w_in_rows = w["mix_w_in"].shape[2]
    last = _Exchange("reduce_last")
    last.add(g_even["mix_w_in_t"], sds((N_DEV, w_in_rows, d_model), f32), _rows_of(w_in_rows), _slot, "mix_w_in")
    last.add(send_small, sds(send_small.shape, f32), _slot, _slot, "small")
    last.add(send_late, sds((N_DEV,) + send_late.shape, f32), _whole, _slot, "repl_late")
    state_last = last.begin(False)
    slots = {}
    for ex, state in reduces:
        slots.update(ex.finish(state, [grad_x, last.token]))

    me_index = (4 * lax.axis_index("x") + 2 * lax.axis_index("y") + lax.axis_index("c")).astype(jnp.int32).reshape(1)

    def rows_block(r, c):
        return lambda tr: ((tr, c), lambda i, me: (me[0] * (r // tr) + i, 0))

    def own_block(n):
        r, c = part_shape[n]
        if n == "ffn_w_up":
            return lambda tr: ((tr, c), lambda i, me: (i, me[0]))
        return rows_block(r, c)

    out = [{}, {}, {}, {}]
    unpad = dict(ffn_w_up=2, ffn_conv_w=2, ffn_w_down=1)
    for n in matrices:
        layers = range(we_[n].shape[0])
        res = _sum_adam(me_index, [slots[(n, layer)] for layer in layers], [own_grad[(n, layer)] for layer in layers],
                        own_block(n), we_[n], me_[n], ve_[n], name=f"adam_{n}")
        for k in range(4):
            out[k][n] = _unpad_groups(res[k], unpad[n]) if n in unpad else res[k]
    pk = lambda d, order: _pack([d[n] for n in order], f32)[None]
    whole_rows = lambda tr: ((tr, _LANES), lambda i, me: (i, 0))
    res_early = _sum_adam(me_index, [slots["repl_early"]], [send_early], whole_rows, pk(w, _REPL_EARLY),
                          pk(m, _REPL_EARLY), pk(v, _REPL_EARLY), name="adam_repl_early")
    for k in range(4):
        out[k].update(zip(_REPL_EARLY, _unpack(res_early[k][0], [w[n].shape for n in _REPL_EARLY])))
    done = [out[k][n] for k in range(4) for n in matrices + _REPL_EARLY]
    slots = last.finish(state_last, done)
    transposed = lambda d: jnp.swapaxes(d["mix_w_in"], 1, 2)
    res_w_in = _sum_adam(me_index, [slots["mix_w_in"]], [g_even["mix_w_in_t"]], rows_block(w_in_rows, d_model),
                         transposed(w), transposed(m), transposed(v), name="adam_mix_w_in")
    res_small = _sum_adam(me_index, [slots["small"]], [send_small],
                          lambda tr: ((1, tr, _LANES), lambda i, me: (me[0], i, 0)),
                          pk(we_, _SMALL_SHARDED), pk(me_, _SMALL_SHARDED), pk(ve_, _SMALL_SHARDED), name="adam_small")
    res_late = _sum_adam(me_index, [slots["repl_late"]], [send_late], whole_rows, pk(w, _REPL_LATE), pk(m, _REPL_LATE),
                         pk(v, _REPL_LATE), name="adam_repl_late")
    for k in range(4):
        out[k]["mix_w_in"] = jnp.swapaxes(res_w_in[k], 1, 2)
        for n, a in zip(_SMALL_SHARDED, _unpack(res_small[k][0], [we_[n].shape for n in _SMALL_SHARDED])):
            out[k][n] = _unpad_groups(a, unpad[n]) if n in unpad else a
        out[k].update(zip(_REPL_LATE, _unpack(res_late[k][0], [w[n].shape for n in _REPL_LATE])))
    return loss, grad_x, out


_INPUTS = tuple("""x, mem, positions, norm_mix, norm_xa, norm_mem, norm_ffn, xa_wq, xa_wk, xa_wv, xa_wo, xa_q_norm, xa_k_norm, ffn_w_up, ffn_conv_w, ffn_conv_b, ffn_w_down, hg_lb_logits, mix_w_in, hg_out_norm, mla_q_a_norm, mla_w_uq, mla_kv_a_norm, mla_w_ukv, mla_qn_nope, mla_qn_rope, mla_kn_nope, mla_kn_rope, mix_w_out, s5_lam_re, s5_lam_im, s5_log_dt, s5_b_re, s5_b_im, s5_c_re, s5_c_im, s5_d, s5_w_glu_a, s5_w_glu_b, loss_target, m_norm_mix, m_norm_xa, m_norm_mem, m_norm_ffn, m_xa_wq, m_xa_wk, m_xa_wv, m_xa_wo, m_xa_q_norm, m_xa_k_norm, m_ffn_w_up, m_ffn_conv_w, m_ffn_conv_b, m_ffn_w_down, m_hg_lb_logits, m_mix_w_in, m_hg_out_norm, m_mla_q_a_norm, m_mla_w_uq, m_mla_kv_a_norm, m_mla_w_ukv, m_mla_qn_nope, m_mla_qn_rope, m_mla_kn_nope, m_mla_kn_rope, m_mix_w_out, m_s5_lam_re, m_s5_lam_im, m_s5_log_dt, m_s5_b_re, m_s5_b_im, m_s5_c_re, m_s5_c_im, m_s5_d, m_s5_w_glu_a, m_s5_w_glu_b, v_norm_mix, v_norm_xa, v_norm_mem, v_norm_ffn, v_xa_wq, v_xa_wk, v_xa_wv, v_xa_wo, v_xa_q_norm, v_xa_k_norm, v_ffn_w_up, v_ffn_conv_w, v_ffn_conv_b, v_ffn_w_down, v_hg_lb_logits, v_mix_w_in, v_hg_out_norm, v_mla_q_a_norm, v_mla_w_uq, v_mla_kv_a_norm, v_mla_w_ukv, v_mla_qn_nope, v_mla_qn_rope, v_mla_kn_nope, v_mla_kn_rope, v_mix_w_out, v_s5_lam_re, v_s5_lam_im, v_s5_log_dt, v_s5_b_re, v_s5_b_im, v_s5_c_re, v_s5_c_im, v_s5_d, v_s5_w_glu_a, v_s5_w_glu_b""".replace(" ", "").split(","))


def kernel(x, mem, positions, norm_mix, norm_xa, norm_mem, norm_ffn, xa_wq, xa_wk, xa_wv, xa_wo, xa_q_norm, xa_k_norm, ffn_w_up, ffn_conv_w, ffn_conv_b, ffn_w_down, hg_lb_logits, mix_w_in, hg_out_norm, mla_q_a_norm, mla_w_uq, mla_kv_a_norm, mla_w_ukv, mla_qn_nope, mla_qn_rope, mla_kn_nope, mla_kn_rope, mix_w_out, s5_lam_re, s5_lam_im, s5_log_dt, s5_b_re, s5_b_im, s5_c_re, s5_c_im, s5_d, s5_w_glu_a, s5_w_glu_b, loss_target, m_norm_mix, m_norm_xa, m_norm_mem, m_norm_ffn, m_xa_wq, m_xa_wk, m_xa_wv, m_xa_wo, m_xa_q_norm, m_xa_k_norm, m_ffn_w_up, m_ffn_conv_w, m_ffn_conv_b, m_ffn_w_down, m_hg_lb_logits, m_mix_w_in, m_hg_out_norm, m_mla_q_a_norm, m_mla_w_uq, m_mla_kv_a_norm, m_mla_w_ukv, m_mla_qn_nope, m_mla_qn_rope, m_mla_kn_nope, m_mla_kn_rope, m_mix_w_out, m_s5_lam_re, m_s5_lam_im, m_s5_log_dt, m_s5_b_re, m_s5_b_im, m_s5_c_re, m_s5_c_im, m_s5_d, m_s5_w_glu_a, m_s5_w_glu_b, v_norm_mix, v_norm_xa, v_norm_mem, v_norm_ffn, v_xa_wq, v_xa_wk, v_xa_wv, v_xa_wo, v_xa_q_norm, v_xa_k_norm, v_ffn_w_up, v_ffn_conv_w, v_ffn_conv_b, v_ffn_w_down, v_hg_lb_logits, v_mix_w_in, v_hg_out_norm, v_mla_q_a_norm, v_mla_w_uq, v_mla_kv_a_norm, v_mla_w_ukv, v_mla_qn_nope, v_mla_qn_rope, v_mla_kn_nope, v_mla_kn_rope, v_mix_w_out, v_s5_lam_re, v_s5_lam_im, v_s5_log_dt, v_s5_b_re, v_s5_b_im, v_s5_c_re, v_s5_c_im, v_s5_d, v_s5_w_glu_a, v_s5_w_glu_b):
    vals = dict(zip(_INPUTS, (x, mem, positions, norm_mix, norm_xa, norm_mem, norm_ffn, xa_wq, xa_wk, xa_wv, xa_wo, xa_q_norm, xa_k_norm, ffn_w_up, ffn_conv_w, ffn_conv_b, ffn_w_down, hg_lb_logits, mix_w_in, hg_out_norm, mla_q_a_norm, mla_w_uq, mla_kv_a_norm, mla_w_ukv, mla_qn_nope, mla_qn_rope, mla_kn_nope, mla_kn_rope, mix_w_out, s5_lam_re, s5_lam_im, s5_log_dt, s5_b_re, s5_b_im, s5_c_re, s5_c_im, s5_d, s5_w_glu_a, s5_w_glu_b, loss_target, m_norm_mix, m_norm_xa, m_norm_mem, m_norm_ffn, m_xa_wq, m_xa_wk, m_xa_wv, m_xa_wo, m_xa_q_norm, m_xa_k_norm, m_ffn_w_up, m_ffn_conv_w, m_ffn_conv_b, m_ffn_w_down, m_hg_lb_logits, m_mix_w_in, m_hg_out_norm, m_mla_q_a_norm, m_mla_w_uq, m_mla_kv_a_norm, m_mla_w_ukv, m_mla_qn_nope, m_mla_qn_rope, m_mla_kn_nope, m_mla_kn_rope, m_mix_w_out, m_s5_lam_re, m_s5_lam_im, m_s5_log_dt, m_s5_b_re, m_s5_b_im, m_s5_c_re, m_s5_c_im, m_s5_d, m_s5_w_glu_a, m_s5_w_glu_b, v_norm_mix, v_norm_xa, v_norm_mem, v_norm_ffn, v_xa_wq, v_xa_wk, v_xa_wv, v_xa_wo, v_xa_q_norm, v_xa_k_norm, v_ffn_w_up, v_ffn_conv_w, v_ffn_conv_b, v_ffn_w_down, v_hg_lb_logits, v_mix_w_in, v_hg_out_norm, v_mla_q_a_norm, v_mla_w_uq, v_mla_kv_a_norm, v_mla_w_ukv, v_mla_qn_nope, v_mla_qn_rope, v_mla_kn_nope, v_mla_kn_rope, v_mix_w_out, v_s5_lam_re, v_s5_lam_im, v_s5_log_dt, v_s5_b_re, v_s5_b_im, v_s5_c_re, v_s5_c_im, v_s5_d, v_s5_w_glu_a, v_s5_w_glu_b)))
    w = {n: vals[n] for n in _WEIGHTS}
    m = {n: vals["m_" + n] for n in _WEIGHTS}
    v = {n: vals["v_" + n] for n in _WEIGHTS}
    loss, grad_x, res = _train_step(vals["x"][0], vals["mem"][0], vals["positions"][0], vals["loss_target"][0],
                                    w, m, v)
    loss = lax.psum(loss[0, 0], ("x", "y", "c"))
    return (loss, grad_x[None], *[r[n] for r in res for n in _WEIGHTS])
```

```python
import functools

import jax
import jax.numpy as jnp
import numpy as np
from jax import lax
from jax.experimental import pallas as pl
from jax.experimental.pallas import tpu as pltpu

f32 = jnp.float32
bf16 = jnp.bfloat16

EPS = 1e-6
N_DEV = 8
VMEM_LIMIT = 52 * 1024 * 1024

HG_HEADS = 4
HG_DIM = 128
HG_WIDTH = HG_HEADS * HG_DIM
HG_CHUNK = 64
HG_SUB = 16
MLA_HEADS = 4
MLA_Q_RANK = 256
MLA_KV_RANK = 128
MLA_NOPE = 128
MLA_ROPE = 64
MLA_V = 128
MLA_QK = MLA_NOPE + MLA_ROPE
MLA_QK_PAD = 256
ROPE_BASE = 10000.0
IN_WIDTH = 4 * HG_WIDTH + MLA_Q_RANK + MLA_KV_RANK + MLA_ROPE
IN_PAD = 2560
XA_HEADS = 4
XA_DIM = 256
S5_GROUP = 16
S5_GROUPS = 64
S5_STATE = 64
CONV_W = 3

ADAM_LR = 0.001
ADAM_B1 = 0.9
ADAM_B2 = 0.999
ADAM_EPS = 1e-08
ADAM_WD = 0.01
ADAM_STEP = 10

_NT = (((1,), (1,)), ((), ()))
_TN = (((0,), (0,)), ((), ()))


def _pick(n, cands):
    for c in cands:
        if n % c == 0:
            return c
    return n


def _cparams(sem):
    return pltpu.CompilerParams(dimension_semantics=sem, vmem_limit_bytes=VMEM_LIMIT)


_MM_BUDGET = 36 * 1024 * 1024
_MM_TILES = ((1024, 1024), (1024, 512), (512, 1024), (512, 512), (512, 256), (256, 512), (256, 256), (256, 128),
             (128, 256), (128, 128))


def _mm(a, b, *, name, ta=False, tb=False, out_dtype=f32, add=None, b2=None, kslab=None, norm=None, rms_bwd=None,
        sq_err=None):
    m, k = (a.shape[1], a.shape[0]) if ta else a.shape
    nb = b.shape[0] if tb else b.shape[1]
    n = nb * (2 if b2 is not None else 1)
    slab, nslab = kslab if kslab is not None else (0, 1)
    assert (b.shape[1] // nslab if tb else b.shape[0]) == k, (a.shape, b.shape, ta, tb)
    assert b2 is None or (not tb and b2.shape == b.shape)
    full_rows = norm is not None or rms_bwd is not None
    assert not (full_rows and b2 is not None) and not (norm is not None and rms_bwd is not None)
    isz = lambda x: jnp.dtype(x.dtype).itemsize
    bm = bn = None
    for cm, cn in _MM_TILES:
        if m % cm or nb % cn or (full_rows and cn != n):
            continue
        need = 2 * (cm * k * isz(a) + cn * k * isz(b) * (2 if b2 is not None else 1)
                    + cm * cn * (jnp.dtype(out_dtype).itemsize + (4 if add is not None else 0)
                                 + (2 if norm is not None else 0) + (8 if rms_bwd is not None else 0)
                                 + (4 if sq_err is not None else 0)))
        if need <= _MM_BUDGET:
            bm, bn = cm, cn
            break
    assert bm is not None, (name, a.shape, b.shape)
    half = nb // bn
    dims = (((0 if ta else 1,), (1 if tb else 0,)), ((), ()))

    def body(*refs):
        refs = list(refs)
        a_ref, b_ref = refs[0], refs[1]
        b2_ref = refs.pop(2) if b2 is not None else None
        add_ref = refs.pop(2) if add is not None else None
        gain_ref = refs.pop(2) if norm is not None else None
        x_ref, xgain_ref, through_ref = (refs.pop(2), refs.pop(2), refs.pop(2)) if rms_bwd is not None else (None,) * 3
        target_ref = refs.pop(2) if sq_err is not None else None
        o_ref = refs[2]
        extra_ref = refs[3] if (norm is not None or rms_bwd is not None or sq_err is not None) else None

        def run(rhs_ref):
            r = lax.dot_general(a_ref[...].astype(bf16), rhs_ref[...].astype(bf16), dims, preferred_element_type=f32)
            if add_ref is not None:
                r = r + add_ref[...].astype(f32)
            if sq_err is not None:
                err = r - target_ref[...]
                o_ref[...] = (err * (1.0 / n)).astype(o_ref.dtype)

                @pl.when((pl.program_id(0) == 0) & (pl.program_id(1) == 0))
                def _():
                    extra_ref[...] = jnp.zeros_like(extra_ref)
                extra_ref[...] += jnp.sum(jnp.sum(err * err, axis=-1, keepdims=True), axis=0, keepdims=True) * (0.5 / n)
            elif rms_bwd is None:
                o_ref[...] = r.astype(o_ref.dtype)
            else:
                _, vjp = jax.vjp(_rms, x_ref[...], xgain_ref[...])
                dx, dgain = vjp(r)
                o_ref[...] = (through_ref[...] + dx).astype(o_ref.dtype)

                @pl.when(pl.program_id(0) == 0)
                def _():
                    extra_ref[...] = jnp.zeros_like(extra_ref)
                extra_ref[...] += dgain
            if norm is not None:
                extra_ref[...] = _rms(r, gain_ref[...]).astype(extra_ref.dtype)

        if b2_ref is None:
            run(b_ref)
        else:
            pl.when(pl.program_id(1) < half)(lambda: run(b_ref))
            pl.when(pl.program_id(1) >= half)(lambda: run(b2_ref))

    a_spec = pl.BlockSpec((k, bm), lambda i, j: (0, i)) if ta else pl.BlockSpec((bm, k), lambda i, j: (i, 0))
    if tb:
        b_spec = pl.BlockSpec((bn, k), lambda i, j: (j, slab))
    elif b2 is None:
        b_spec = pl.BlockSpec((k, bn), lambda i, j: (0, j))
    else:
        b_spec = pl.BlockSpec((k, bn), lambda i, j: (0, jnp.minimum(j, half - 1)))
    in_specs = [a_spec, b_spec]
    args = [a, b]
    if b2 is not None:
        in_specs.append(pl.BlockSpec((k, bn), lambda i, j: (0, jnp.maximum(j - half, 0))))
        args.append(b2)
    if add is not None:
        in_specs.append(pl.BlockSpec((bm, bn), lambda i, j: (i, j)))
        args.append(add)
    out_blk = pl.BlockSpec((bm, bn), lambda i, j: (i, j))
    out_specs, out_shape = out_blk, jax.ShapeDtypeStruct((m, n), out_dtype)
    row_blk = pl.BlockSpec((1, bn), lambda i, j: (0, j))
    if norm is not None:
        in_specs.append(row_blk)
        args.append(norm)
        out_specs, out_shape = [out_blk, out_blk], [out_shape, jax.ShapeDtypeStruct((m, n), bf16)]
    if rms_bwd is not None:
        in_specs += [out_blk, row_blk, out_blk]
        args += list(rms_bwd)
        out_specs, out_shape = [out_blk, row_blk], [out_shape, jax.ShapeDtypeStruct((1, n), f32)]
    if sq_err is not None:
        in_specs.append(out_blk)
        args.append(sq_err)
        out_specs = [out_blk, pl.BlockSpec((1, 1), lambda i, j: (0, 0))]
        out_shape = [out_shape, jax.ShapeDtypeStruct((1, 1), f32)]
    carried = rms_bwd is not None or sq_err is not None
    return pl.pallas_call(
        body, grid=(m // bm, n // bn), in_specs=in_specs, out_specs=out_specs, out_shape=out_shape,
        compiler_params=_cparams(("arbitrary" if carried else "parallel", "arbitrary" if sq_err is not None else "parallel")),
        name=name)(*args)


def _as_tuple(x):
    return tuple(x) if isinstance(x, (tuple, list)) else (x,)


def _full_spec(p):
    nd = p.ndim
    return pl.BlockSpec(p.shape, lambda i, _nd=nd: (0,) * _nd)


def _window(a, start, width):
    assert start % width == 0 and width % 128 == 0
    return (a, start // width, width)


def _row_array(x):
    return x[0] if isinstance(x, tuple) else x


def _row_shape(x):
    return (x[0].shape[0], x[2]) if isinstance(x, tuple) else x.shape


def _row_spec(x, tile):
    if isinstance(x, tuple):
        return pl.BlockSpec((tile, x[2]), lambda i, _b=x[1]: (i, _b))
    return pl.BlockSpec((tile, x.shape[1]), lambda i: (i, 0))


def _rows(fn, rows, params, outs, *, name, tile=256, accs=()):
    length = _row_shape(rows[0])[0]
    tile = min(tile, length)
    nr, npar, no = len(rows), len(params), len(outs)

    def body(*refs):
        r, p, o = refs[:nr], refs[nr:nr + npar], refs[nr + npar:]
        res = _as_tuple(fn(*[x[...].astype(f32) for x in r], *[x[...] for x in p]))
        for kk in range(no):
            o[kk][...] = res[kk].astype(o[kk].dtype)
        if accs:
            @pl.when(pl.program_id(0) == 0)
            def _():
                for kk in range(no, no + len(accs)):
                    o[kk][...] = jnp.zeros_like(o[kk])
            for kk in range(no, no + len(accs)):
                o[kk][...] += res[kk]

    in_specs = [_row_spec(x, tile) for x in rows] + [_full_spec(p) for p in params]
    out_specs = [pl.BlockSpec((tile, w), lambda i: (i, 0)) for w, _ in outs]
    out_shape = [jax.ShapeDtypeStruct((length, w), d) for w, d in outs]
    for s in accs:
        out_specs.append(pl.BlockSpec(s, lambda i, _nd=len(s): (0,) * _nd))
        out_shape.append(jax.ShapeDtypeStruct(s, f32))
    res = pl.pallas_call(body, grid=(length // tile,), in_specs=in_specs, out_specs=out_specs, out_shape=out_shape,
                         compiler_params=_cparams(("arbitrary",)), name=name)(*[_row_array(x) for x in rows], *params)
    return res


def _rows_bwd(fn, rows, params, cts, *, name, rgrad, pgrad, tile=256, addends=None):
    addends = {i: (a if isinstance(a, list) else [(a, 0)]) for i, a in (addends or {}).items()}
    length = _row_shape(rows[0])[0]
    tile = min(tile, length)
    nr, npar, nc = len(rows), len(params), len(cts)
    ridx = [i for i in range(nr) if rgrad[i] is not None]
    pidx = [i for i in range(npar) if pgrad[i]]
    flat_addends = [(i, a, off) for i in sorted(addends) for a, off in addends[i]]
    na = len(flat_addends)

    def body(*refs):
        r, p, c = refs[:nr], refs[nr:nr + npar], refs[nr + npar:nr + npar + nc]
        ad = refs[nr + npar + nc:nr + npar + nc + na]
        o = refs[nr + npar + nc + na:]
        rv = [x[...].astype(f32) for x in r]
        pv = [x[...] for x in p]
        cv = tuple(x[...].astype(f32) for x in c)

        def g(*d):
            rr, pp = list(rv), list(pv)
            for n_, i_ in enumerate(ridx):
                rr[i_] = d[n_]
            for n_, i_ in enumerate(pidx):
                pp[i_] = d[len(ridx) + n_]
            return _as_tuple(fn(*rr, *pp))

        _, vjp = jax.vjp(g, *[rv[i] for i in ridx], *[pv[i] for i in pidx])
        grads = vjp(cv)
        for n_, i_ in enumerate(ridx):
            val = grads[n_]
            for k_, (j_, a_, off) in enumerate(flat_addends):
                if j_ == i_:
                    extra = ad[k_][...].astype(f32)
                    if extra.shape[1] != val.shape[1]:
                        extra = jnp.pad(extra, ((0, 0), (off, val.shape[1] - off - extra.shape[1])))
                    val = val + extra
            o[n_][...] = val.astype(o[n_].dtype)
        if pidx:
            @pl.when(pl.program_id(0) == 0)
            def _():
                for n_ in range(len(pidx)):
                    o[len(ridx) + n_][...] = jnp.zeros_like(o[len(ridx) + n_])
            for n_ in range(len(pidx)):
                o[len(ridx) + n_][...] += grads[len(ridx) + n_]

    plain = lambda shape: pl.BlockSpec((tile, shape[1]), lambda i: (i, 0))
    in_specs = ([_row_spec(x, tile) for x in rows] + [_full_spec(p) for p in params] + [plain(x.shape) for x in cts]
                + [plain(a.shape) for _, a, _ in flat_addends])
    out_specs = [plain(_row_shape(rows[i])) for i in ridx] + [_full_spec(params[i]) for i in pidx]
    out_shape = ([jax.ShapeDtypeStruct(_row_shape(rows[i]), rgrad[i]) for i in ridx]
                 + [jax.ShapeDtypeStruct(params[i].shape, f32) for i in pidx])
    res = pl.pallas_call(body, grid=(length // tile,), in_specs=in_specs, out_specs=out_specs, out_shape=out_shape,
                         compiler_params=_cparams(("arbitrary",)), name=name)(
        *[_row_array(x) for x in rows], *params, *cts, *[a for _, a, _ in flat_addends])
    return list(res[:len(ridx)]), list(res[len(ridx):])


def _rms(x, g):
    return x * lax.rsqrt(jnp.mean(x * x, axis=-1, keepdims=True) + EPS) * g


def _rms_twice(x, g):
    y = _rms(x, g)
    return y, y


def _silu(x):
    return x * jax.nn.sigmoid(x)


_UP_COLS = 1024


def _ffn_up_conv(hf, w_up, cw, cb, *, name):
    length, d = hf.shape
    ff = w_up.shape[1] // 2
    bm = _pick(length, (1024, 512, 256))
    bn = _UP_COLS
    nj = ff // bn

    def body(hf_ref, wg_ref, wv_ref, cwg, cwv, cbg, cbv, ug_ref, uv_ref, a_ref, halo_g, halo_v):
        i, j = pl.program_id(0), pl.program_id(1)
        x = hf_ref[...]
        rows = lax.broadcasted_iota(jnp.int32, (bm, bn), 0)
        pad = jnp.zeros((bm - 8, bn), f32)

        def half(w_ref, cw_ref, cb_ref, u_ref, halo):
            u = jnp.dot(x, w_ref[...], preferred_element_type=f32).astype(bf16)
            u_ref[...] = u
            u = u.astype(f32)
            prev = jnp.where(i == 0, 0.0, halo[j])
            x1 = jnp.where(rows >= 1, pltpu.roll(u, 1, axis=0), jnp.concatenate([pltpu.roll(prev, 1, axis=0), pad]))
            x2 = jnp.where(rows >= 2, pltpu.roll(u, 2, axis=0), jnp.concatenate([pltpu.roll(prev, 2, axis=0), pad]))
            halo[j] = u[bm - 8:, :]
            return cw_ref[2:3, :] * u + cw_ref[1:2, :] * x1 + cw_ref[0:1, :] * x2 + cb_ref[...]

        g = half(wg_ref, cwg, cbg, ug_ref, halo_g)
        v = half(wv_ref, cwv, cbv, uv_ref, halo_v)
        a_ref[...] = (_silu(g) * v).astype(a_ref.dtype)

    col = lambda r, off: pl.BlockSpec((r, bn), lambda i, j, _o=off: (0, j + _o))
    out_blk = pl.BlockSpec((bm, bn), lambda i, j: (i, j))
    sds = jax.ShapeDtypeStruct((length, ff), bf16)
    return pl.pallas_call(
        body, grid=(length // bm, nj),
        in_specs=[pl.BlockSpec((bm, d), lambda i, j: (i, 0)), col(d, 0), col(d, nj), col(CONV_W, 0), col(CONV_W, nj),
                  col(1, 0), col(1, nj)],
        out_specs=[out_blk] * 3, out_shape=[sds] * 3,
        scratch_shapes=[pltpu.VMEM((nj, 8, bn), f32), pltpu.VMEM((nj, 8, bn), f32)],
        compiler_params=_cparams(("arbitrary", "arbitrary")), name=name)(hf, w_up, w_up, cw, cw, cb, cb)


def _ffn_da_dconv(dout, w_down, u_g, u_v, cw, cb, *, name):
    length, d = dout.shape
    ff = u_g.shape[1]
    bm = _pick(length, (1024, 512, 256))
    bn = _UP_COLS
    nj, ni = ff // bn, length // bm

    def body(dout_ref, wd_ref, ug_ref, uv_ref, pg_ref, pv_ref, cwg, cwv, cbg, cbv,
             dug_ref, duv_ref, sums_g_ref, sums_v_ref, halo_g, halo_v, acc_g, acc_v):
        s, j = pl.program_id(0), pl.program_id(1)
        rows = lax.broadcasted_iota(jnp.int32, (bm, bn), 0)
        row8 = lax.broadcasted_iota(jnp.int32, (8, bn), 0)
        pad = jnp.zeros((bm - 8, bn), f32)
        da = lax.dot_general(dout_ref[...].astype(bf16), wd_ref[...], _NT, preferred_element_type=f32)

        def conv(u_ref, p_ref, cw_ref, cb_ref):
            x = u_ref[...].astype(f32)
            prev = jnp.where(s == ni - 1, 0.0, p_ref[...].astype(f32))
            x1 = jnp.where(rows >= 1, pltpu.roll(x, 1, axis=0), jnp.concatenate([pltpu.roll(prev, 1, axis=0), pad]))
            x2 = jnp.where(rows >= 2, pltpu.roll(x, 2, axis=0), jnp.concatenate([pltpu.roll(prev, 2, axis=0), pad]))
            return cw_ref[2:3, :] * x + cw_ref[1:2, :] * x1 + cw_ref[0:1, :] * x2 + cb_ref[...], x, x1, x2

        g, xg, xg1, xg2 = conv(ug_ref, pg_ref, cwg, cbg)
        v, xv, xv1, xv2 = conv(uv_ref, pv_ref, cwv, cbv)
        sg = jax.nn.sigmoid(g)
        dg = da * v * (sg * (1.0 + g * (1.0 - sg)))
        dv = da * (g * sg)

        def back(dy, x, x1, x2, cw_ref, du_ref, sums_ref, halo, acc):
            nxt = jnp.where(s == 0, 0.0, halo[j])
            up1 = jnp.where(rows < bm - 1, pltpu.roll(dy, bm - 1, axis=0),
                            jnp.concatenate([pad, pltpu.roll(nxt, 7, axis=0)]))
            up2 = jnp.where(rows < bm - 2, pltpu.roll(dy, bm - 2, axis=0),
                            jnp.concatenate([pad, pltpu.roll(nxt, 6, axis=0)]))
            halo[j] = dy[:8, :]
            du_ref[...] = (cw_ref[2:3, :] * dy + cw_ref[1:2, :] * up1 + cw_ref[0:1, :] * up2).astype(du_ref.dtype)
            col = lambda t: jnp.sum(t, axis=0, keepdims=True)
            part = jnp.where(row8 == 0, col(dy * x2), jnp.where(row8 == 1, col(dy * x1), jnp.where(
                row8 == 2, col(dy * x), jnp.where(row8 == 3, col(dy), 0.0))))
            total = jnp.where(s == 0, part, acc[j] + part)
            acc[j] = total
            sums_ref[...] = total

        back(dg, xg, xg1, xg2, cwg, dug_ref, sums_g_ref, halo_g, acc_g)
        back(dv, xv, xv1, xv2, cwv, duv_ref, sums_v_ref, halo_v, acc_v)

    rb = lambda s: ni - 1 - s
    tile = pl.BlockSpec((bm, bn), lambda s, j: (rb(s), j))
    before = pl.BlockSpec((8, bn), lambda s, j: (jnp.maximum(rb(s) * (bm // 8) - 1, 0), j))
    col = lambda r, off: pl.BlockSpec((r, bn), lambda s, j, _o=off: (0, j + _o))
    sds = jax.ShapeDtypeStruct
    dug, duv, sums_g, sums_v = pl.pallas_call(
        body, grid=(ni, nj),
        in_specs=[pl.BlockSpec((bm, d), lambda s, j: (rb(s), 0)), pl.BlockSpec((bn, d), lambda s, j: (j, 0)),
                  tile, tile, before, before, col(CONV_W, 0), col(CONV_W, nj), col(1, 0), col(1, nj)],
        out_specs=[tile, tile] + [pl.BlockSpec((8, bn), lambda s, j: (s, j))] * 2,
        out_shape=[sds((length, ff), bf16), sds((length, ff), bf16), sds((ni * 8, ff), f32), sds((ni * 8, ff), f32)],
        scratch_shapes=[pltpu.VMEM((nj, 8, bn), f32)] * 4,
        compiler_params=_cparams(("arbitrary", "arbitrary")), name=name)(
        dout, w_down, u_g, u_v, u_g, u_v, cw, cw, cb, cb)
    last = (ni - 1) * 8
    both = lambda lo, hi: jnp.concatenate([sums_g[last + lo:last + hi], sums_v[last + lo:last + hi]], axis=1)
    return dug, duv, both(0, CONV_W), both(CONV_W, CONV_W + 1)


def _ffn_fwd(h, w, tag, hf=None, loss_target=None):
    if hf is None:
        hf, = _rows(_rms, [h], [w["norm_ffn"]], [(h.shape[1], bf16)], name=f"ffn_norm_{tag}")
    u_g, u_v, a = _ffn_up_conv(hf, w["ffn_w_up"], w["ffn_conv_w"], w["ffn_conv_b"], name=f"ffn_up_{tag}")
    out = _mm(a, w["ffn_w_down"], add=h, sq_err=loss_target, name=f"ffn_down_{tag}")
    return out, (h, hf, u_g, u_v, a)


def _ffn_bwd(dout, w, saved, tag):
    h, hf, u_g, u_v, a = saved
    g = {"ffn_w_down": _mm(a, dout, ta=True, out_dtype=bf16, name=f"ffn_dwdown_{tag}")}
    dug, duv, g["ffn_conv_w"], g["ffn_conv_b"] = _ffn_da_dconv(
        dout, w["ffn_w_down"], u_g, u_v, w["ffn_conv_w"], w["ffn_conv_b"], name=f"ffn_dconv_{tag}")
    dhf = _mm(dug, w["ffn_w_up"], tb=True, kslab=(0, 2), name=f"ffn_dhf_g_{tag}")
    dh, g["norm_ffn"] = _mm(duv, w["ffn_w_up"], tb=True, kslab=(1, 2), add=dhf, rms_bwd=(h, w["norm_ffn"], dout),
                            name=f"ffn_dhf_v_{tag}")
    g["ffn_w_up"] = _mm(hf, dug, ta=True, b2=duv, out_dtype=bf16, name=f"ffn_dwup_{tag}")
    return dh, g


def _xa_keys(kx, kg):
    return jnp.concatenate([_rms(kx[:, hh * XA_DIM:(hh + 1) * XA_DIM], kg) for hh in range(XA_HEADS)], axis=-1)


def _xattn_fn(qx, kn, vx, qg):
    outs = []
    for hh in range(XA_HEADS):
        sl = slice(hh * XA_DIM, (hh + 1) * XA_DIM)
        q = _rms(qx[:, sl], qg).astype(bf16)
        k = kn[:, sl].astype(bf16)
        s = lax.dot_general(q, k, _NT, preferred_element_type=f32) * (XA_DIM ** -0.5)
        s = s - jnp.max(s, axis=-1, keepdims=True)
        p = jnp.exp(s)
        p = p / jnp.sum(p, axis=-1, keepdims=True)
        outs.append(jnp.dot(p.astype(bf16), vx[:, sl].astype(bf16), preferred_element_type=f32))
    return jnp.concatenate(outs, axis=-1)


def _xattn_fwd(h, mem, w, tag, hx=None):
    d = h.shape[1]
    if hx is None:
        hx, = _rows(_rms, [h], [w["norm_xa"]], [(d, bf16)], name=f"xa_norm_{tag}")
    qx = _mm(hx, w["xa_wq"], out_dtype=bf16, name=f"xa_q_{tag}")
    m, = _rows(_rms, [mem], [w["norm_mem"]], [(d, bf16)], name=f"xa_mnorm_{tag}")
    kx = _mm(m, w["xa_wk"], name=f"xa_k_{tag}")
    vx = _mm(m, w["xa_wv"], name=f"xa_v_{tag}")
    kn, = _rows(_xa_keys, [kx], [w["xa_k_norm"]], [(d, f32)], name=f"xa_kn_{tag}")
    o, = _rows(_xattn_fn, [qx], [kn, vx, w["xa_q_norm"]], [(d, bf16)], tile=1024, name=f"xa_attn_{tag}")
    out, hf = _mm(o, w["xa_wo"], add=h, norm=w["norm_ffn"], name=f"xa_o_{tag}")
    return out, (h, hx, qx, m, kx, kn, vx, o), hf


def _xattn_bwd(dout, mem, w, saved, tag):
    h, hx, qx, m, kx, kn, vx, o = saved
    g = {}
    do = _mm(dout, w["xa_wo"], tb=True, out_dtype=bf16, name=f"xa_do_{tag}")
    g["xa_wo"] = _mm(o, dout, ta=True, out_dtype=bf16, name=f"xa_dwo_{tag}")
    (dqx,), (dkn, dvx, g["xa_q_norm"]) = _rows_bwd(
        _xattn_fn, [qx], [kn, vx, w["xa_q_norm"]], [do], rgrad=[bf16], pgrad=[True] * 3, tile=1024,
        name=f"xa_dattn_{tag}")
    (dkx,), (g["xa_k_norm"],) = _rows_bwd(_xa_keys, [kx], [w["xa_k_norm"]], [dkn], rgrad=[f32], pgrad=[True],
                                          name=f"xa_dkn_{tag}")
    dh, g["norm_xa"] = _mm(dqx, w["xa_wq"], tb=True, rms_bwd=(h, w["norm_xa"], dout), name=f"xa_dhx_{tag}")
    g["xa_wq"] = _mm(hx, dqx, ta=True, out_dtype=bf16, name=f"xa_dwq_{tag}")
    dm = _mm(dkx, w["xa_wk"], tb=True, name=f"xa_dm_k_{tag}")
    dm = _mm(dvx, w["xa_wv"], tb=True, add=dm, name=f"xa_dm_v_{tag}")
    g["xa_wk"] = _mm(m, dkx, ta=True, out_dtype=bf16, name=f"xa_dwk_{tag}")
    g["xa_wv"] = _mm(m, dvx, ta=True, out_dtype=bf16, name=f"xa_dwv_{tag}")
    _, (g["norm_mem"],) = _rows_bwd(_rms, [mem], [w["norm_mem"]], [dm], rgrad=[None], pgrad=[True],
                                    name=f"xa_dmnorm_{tag}")
    return dh, g


_HG_GROUP = 8


def _hg_chunk(q, k, v, g, *sts):
    c = q.shape[0]
    heads = [slice(h * HG_DIM, (h + 1) * HG_DIM) for h in range(len(sts))]
    tri = (lax.broadcasted_iota(jnp.int32, (c, c), 0) >= lax.broadcasted_iota(jnp.int32, (c, c), 1)).astype(f32)
    b = jnp.dot(tri, g, precision=lax.Precision.HIGHEST, preferred_element_type=f32)
    bend = jnp.sum(g, axis=0, keepdims=True)
    qe = (q * jnp.exp(b)).astype(bf16)
    kd = (k * jnp.exp(bend - b)).astype(bf16)
    vb = v.astype(bf16)
    decay = jnp.exp(bend)
    o_inter = [lax.dot_general(qe[:, hs], st.astype(bf16), _NT, preferred_element_type=f32) for hs, st in zip(heads, sts)]
    new = [st * decay[:, hs] + lax.dot_general(vb[:, hs], kd[:, hs], _TN, preferred_element_type=f32)
           for hs, st in zip(heads, sts)]
    outs = []
    for i in range(c // HG_SUB):
        lo, n = HG_SUB * i, HG_SUB * (i + 1)
        ref = jnp.sum(g[:lo], axis=0, keepdims=True) if i else jnp.zeros((1, g.shape[1]), f32)
        qh = (q[lo:n] * jnp.exp(b[lo:n] - ref)).astype(bf16)
        kh = (k[:n] * jnp.exp(ref - b[:n])).astype(bf16)
        keep = (lax.broadcasted_iota(jnp.int32, (HG_SUB, n), 1)
                <= lo + lax.broadcasted_iota(jnp.int32, (HG_SUB, n), 0))
        scores = [lax.dot_general(qh[:, hs], kh[:, hs], _NT, preferred_element_type=f32) for hs in heads]
        scores = [jnp.where(keep, a, 0.0).astype(bf16) for a in scores]
        outs.append(jnp.concatenate([jnp.dot(a, vb[:n, hs], preferred_element_type=f32)
                                     for a, hs in zip(scores, heads)], axis=1))
    return (jnp.concatenate(outs, axis=0) + jnp.concatenate(o_inter, axis=1), *new)


def _hg_fwd(q, k, v, g, *, name):
    length = q.shape[0]
    rows = _HG_GROUP * HG_CHUNK
    ng = length // rows
    nc = length // HG_CHUNK

    def body(q_ref, k_ref, v_ref, g_ref, o_ref, st_ref, state):
        @pl.when(pl.program_id(0) == 0)
        def _():
            state[...] = jnp.zeros_like(state)

        states = [state[h] for h in range(HG_HEADS)]
        for ci in range(_HG_GROUP):
            sl = slice(ci * HG_CHUNK, (ci + 1) * HG_CHUNK)
            for h in range(HG_HEADS):
                st_ref[h, ci] = states[h].astype(st_ref.dtype)
            o, *states = _hg_chunk(q_ref[sl, :], k_ref[sl, :], v_ref[sl, :], g_ref[sl, :], *states)
            o_ref[sl, :] = o
        for h in range(HG_HEADS):
            state[h] = states[h]

    blk = pl.BlockSpec((rows, HG_WIDTH), lambda c: (c, 0))
    return pl.pallas_call(
        body, grid=(ng,), in_specs=[blk] * 4,
        out_specs=[blk, pl.BlockSpec((HG_HEADS, _HG_GROUP, HG_DIM, HG_DIM), lambda c: (0, c, 0, 0))],
        out_shape=[jax.ShapeDtypeStruct((length, HG_WIDTH), f32),
                   jax.ShapeDtypeStruct((HG_HEADS, nc, HG_DIM, HG_DIM), bf16)],
        scratch_shapes=[pltpu.VMEM((HG_HEADS, HG_DIM, HG_DIM), f32)],
        compiler_params=_cparams(("arbitrary",)), name=name)(q, k, v, g)


def _hg_bwd(q, k, v, g, states, do, *, name):
    length = q.shape[0]
    rows = _HG_GROUP * HG_CHUNK
    ng = length // rows

    def body(q_ref, k_ref, v_ref, g_ref, st_ref, do_ref, dq_ref, dk_ref, dv_ref, dg_ref, dstate):
        @pl.when(pl.program_id(0) == 0)
        def _():
            dstate[...] = jnp.zeros_like(dstate)

        dstates = [dstate[h] for h in range(HG_HEADS)]
        for ci in reversed(range(_HG_GROUP)):
            sl = slice(ci * HG_CHUNK, (ci + 1) * HG_CHUNK)
            _, vjp = jax.vjp(_hg_chunk, q_ref[sl, :], k_ref[sl, :], v_ref[sl, :], g_ref[sl, :],
                             *[st_ref[h, ci].astype(f32) for h in range(HG_HEADS)])
            dq, dk, dv, dg, *dstates = vjp((do_ref[sl, :], *dstates))
            dq_ref[sl, :] = dq
            dk_ref[sl, :] = dk
            dv_ref[sl, :] = dv
            dg_ref[sl, :] = dg
        for h in range(HG_HEADS):
            dstate[h] = dstates[h]

    blk = pl.BlockSpec((rows, HG_WIDTH), lambda c: (ng - 1 - c, 0))
    sds = jax.ShapeDtypeStruct((length, HG_WIDTH), f32)
    return pl.pallas_call(
        body, grid=(ng,),
        in_specs=[blk] * 4 + [pl.BlockSpec((HG_HEADS, _HG_GROUP, HG_DIM, HG_DIM), lambda c: (0, ng - 1 - c, 0, 0)), blk],
        out_specs=[blk] * 4, out_shape=[sds] * 4,
        scratch_shapes=[pltpu.VMEM((HG_HEADS, HG_DIM, HG_DIM), f32)],
        compiler_params=_cparams(("arbitrary",)), name=name)(q, k, v, g, states, do)


_ATT_BLK = 512
_ATT_SCALE = MLA_QK ** -0.5
_NEG = -1e30


def _att_mask(i, j, t):
    rows = i * t + lax.broadcasted_iota(jnp.int32, (t, t), 0)
    cols = j * t + lax.broadcasted_iota(jnp.int32, (t, t), 1)
    return cols <= rows


def _att_fwd(q, k, v, *, name):
    length = q.shape[0]
    t = min(_ATT_BLK, length)
    nq = length // t
    qw, vw = MLA_QK_PAD, MLA_V
    heads = range(MLA_HEADS)

    def body(q_ref, k_ref, v_ref, o_ref, lse_ref):
        i = pl.program_id(0)
        qbs = [q_ref[:, h * qw:(h + 1) * qw] for h in heads]

        def step(j, carry, diagonal=False):
            off = pl.multiple_of(j * t, t)
            out = []
            for h in heads:
                m, l, acc = carry[h]
                ks = k_ref[pl.ds(off, t), h * qw:(h + 1) * qw]
                vs = v_ref[pl.ds(off, t), h * vw:(h + 1) * vw]
                s = lax.dot_general(qbs[h], ks, _NT, preferred_element_type=f32) * _ATT_SCALE
                if diagonal:
                    s = jnp.where(_att_mask(i, j, t), s, _NEG)
                m_new = jnp.maximum(m, jnp.max(s, axis=-1, keepdims=True))
                alpha = jnp.exp(m - m_new)
                p = jnp.exp(s - m_new)
                l = alpha * l + jnp.sum(p, axis=-1, keepdims=True)
                acc = alpha * acc + jnp.dot(p.astype(bf16), vs, preferred_element_type=f32)
                out.append((m_new, l, acc))
            return tuple(out)

        init = tuple((jnp.full((t, 1), _NEG, f32), jnp.zeros((t, 1), f32), jnp.zeros((t, vw), f32)) for _ in heads)
        res = step(i, lax.fori_loop(0, i, step, init), diagonal=True)
        for h in heads:
            m, l, acc = res[h]
            o_ref[:, h * vw:(h + 1) * vw] = (acc / l).astype(o_ref.dtype)
            lse_ref[:, h * vw:(h + 1) * vw] = jnp.broadcast_to(m + jnp.log(l), (t, vw))

    return pl.pallas_call(
        body, grid=(nq,),
        in_specs=[pl.BlockSpec((t, q.shape[1]), lambda i: (i, 0)), pl.BlockSpec(k.shape, lambda i: (0, 0)),
                  pl.BlockSpec(v.shape, lambda i: (0, 0))],
        out_specs=[pl.BlockSpec((t, v.shape[1]), lambda i: (i, 0))] * 2,
        out_shape=[jax.ShapeDtypeStruct(v.shape, bf16), jax.ShapeDtypeStruct(v.shape, f32)],
        compiler_params=_cparams(("arbitrary",)), name=name)(q, k, v)


def _att_bwd(q, k, v, o, lse, do, *, name):
    length = q.shape[0]
    t = min(_ATT_BLK, length)
    nq = length // t
    qw, vw = MLA_QK_PAD, MLA_V
    heads = range(MLA_HEADS)

    def dq_body(q_ref, k_ref, v_ref, o_ref, lse_ref, do_ref, dq_ref, delta_ref):
        i = pl.program_id(0)
        qbs = [q_ref[:, h * qw:(h + 1) * qw] for h in heads]
        dobs = [do_ref[:, h * vw:(h + 1) * vw] for h in heads]
        lses = [lse_ref[:, h * vw:h * vw + 1] for h in heads]
        deltas = [jnp.sum(dobs[h].astype(f32) * o_ref[:, h * vw:(h + 1) * vw].astype(f32), axis=-1, keepdims=True)
                  for h in heads]

        def step(j, dqs, diagonal=False):
            off = pl.multiple_of(j * t, t)
            out = []
            for h in heads:
                ks = k_ref[pl.ds(off, t), h * qw:(h + 1) * qw]
                vs = v_ref[pl.ds(off, t), h * vw:(h + 1) * vw]
                s = lax.dot_general(qbs[h], ks, _NT, preferred_element_type=f32) * _ATT_SCALE
                p = jnp.exp(s - lses[h])
                if diagonal:
                    p = jnp.where(_att_mask(i, j, t), p, 0.0)
                dp = lax.dot_general(dobs[h], vs, _NT, preferred_element_type=f32)
                ds = p * (dp - deltas[h]) * _ATT_SCALE
                out.append(dqs[h] + jnp.dot(ds.astype(bf16), ks, preferred_element_type=f32))
            return tuple(out)

        dqs = step(i, lax.fori_loop(0, i, step, tuple(jnp.zeros((t, qw), f32) for _ in heads)), diagonal=True)
        for h in heads:
            dq_ref[:, h * qw:(h + 1) * qw] = dqs[h].astype(dq_ref.dtype)
            delta_ref[:, h * vw:(h + 1) * vw] = jnp.broadcast_to(deltas[h], (t, vw))

    qblk = pl.BlockSpec((t, q.shape[1]), lambda i: (i, 0))
    vblk = pl.BlockSpec((t, v.shape[1]), lambda i: (i, 0))
    qfull = pl.BlockSpec(q.shape, lambda i: (0, 0))
    vfull = pl.BlockSpec(v.shape, lambda i: (0, 0))
    dq, delta = pl.pallas_call(
        dq_body, grid=(nq,), in_specs=[qblk, qfull, vfull, vblk, vblk, vblk], out_specs=[qblk, vblk],
        out_shape=[jax.ShapeDtypeStruct(q.shape, bf16), jax.ShapeDtypeStruct(lse.shape, f32)],
        compiler_params=_cparams(("arbitrary",)), name=name + "_dq")(q, k, v, o, lse, do)

    def dkv_body(k_ref, v_ref, q_ref, do_ref, lse_ref, delta_ref, dk_ref, dv_ref):
        j = pl.program_id(0)
        kbs = [k_ref[:, h * qw:(h + 1) * qw] for h in heads]
        vbs = [v_ref[:, h * vw:(h + 1) * vw] for h in heads]

        def step(i, carry, diagonal=False):
            off = pl.multiple_of(i * t, t)
            out = []
            for h in heads:
                dk, dv = carry[h]
                qs = q_ref[pl.ds(off, t), h * qw:(h + 1) * qw]
                dos = do_ref[pl.ds(off, t), h * vw:(h + 1) * vw]
                lse_i = lse_ref[pl.ds(off, t), h * vw:h * vw + 1]
                delta_i = delta_ref[pl.ds(off, t), h * vw:h * vw + 1]
                s = lax.dot_general(qs, kbs[h], _NT, preferred_element_type=f32) * _ATT_SCALE
                p = jnp.exp(s - lse_i)
                if diagonal:
                    p = jnp.where(_att_mask(i, j, t), p, 0.0)
                dv = dv + lax.dot_general(p.astype(bf16), dos, _TN, preferred_element_type=f32)
                dp = lax.dot_general(dos, vbs[h], _NT, preferred_element_type=f32)
                ds = p * (dp - delta_i) * _ATT_SCALE
                dk = dk + lax.dot_general(ds.astype(bf16), qs, _TN, preferred_element_type=f32)
                out.append((dk, dv))
            return tuple(out)

        first = step(j, tuple((jnp.zeros((t, qw), f32), jnp.zeros((t, vw), f32)) for _ in heads), diagonal=True)
        res = lax.fori_loop(j + 1, nq, step, first)
        for h in heads:
            dk_ref[:, h * qw:(h + 1) * qw] = res[h][0].astype(dk_ref.dtype)
            dv_ref[:, h * vw:(h + 1) * vw] = res[h][1].astype(dv_ref.dtype)

    dk, dv = pl.pallas_call(
        dkv_body, grid=(nq,), in_specs=[qblk, vblk, qfull, vfull, vfull, vfull], out_specs=[qblk, vblk],
        out_shape=[jax.ShapeDtypeStruct(k.shape, bf16), jax.ShapeDtypeStruct(v.shape, bf16)],
        compiler_params=_cparams(("arbitrary",)), name=name + "_dkv")(k, v, q, do, lse, delta)
    return dq, dk, dv


_C_Q = 4 * HG_WIDTH
_C_KV = _C_Q + MLA_Q_RANK
_C_KPE = _C_KV + MLA_KV_RANK


def _rms_n(x, g, n):
    return x * lax.rsqrt(jnp.sum(x * x, axis=-1, keepdims=True) * (1.0 / n) + EPS) * g


def _mix_a(proj, l0, l1, q_a_norm, kv_a_norm):
    lb = jax.nn.sigmoid(l0 - l1)
    f = lb + (1.0 - lb) * jax.nn.sigmoid(proj[:, HG_WIDTH:2 * HG_WIDTH])
    qf = _silu(proj[:, :HG_WIDTH])
    v = proj[:, 2 * HG_WIDTH:3 * HG_WIDTH]
    cqn = _rms(proj[:, _C_Q:_C_KV], q_a_norm)
    ckvn = _rms(proj[:, _C_KV:_C_KPE], kv_a_norm)
    return qf, 1.0 - f, v, jnp.log(f), cqn, ckvn


def _mix_b(qraw, kvraw, kpe_raw, cos, sin, qn_nope, qn_rope, kn_nope, kn_rope, perm):
    def rope(x):
        return x * cos + jnp.dot(x, perm, precision=lax.Precision.HIGHEST, preferred_element_type=f32) * sin

    kpe = rope(_rms_n(kpe_raw, kn_rope, MLA_ROPE))
    qs, ks, vs = [], [], []
    for hh in range(MLA_HEADS):
        base = hh * MLA_QK_PAD
        qs.append(_rms(qraw[:, base:base + MLA_NOPE], qn_nope))
        qs.append(rope(_rms_n(qraw[:, base + MLA_NOPE:base + MLA_QK_PAD], qn_rope, MLA_ROPE)))
        ks.append(_rms(kvraw[:, base:base + MLA_NOPE], kn_nope))
        ks.append(kpe)
        vs.append(kvraw[:, base + MLA_NOPE:base + MLA_QK_PAD])
    return jnp.concatenate(qs, axis=-1), jnp.concatenate(ks, axis=-1), jnp.concatenate(vs, axis=-1)


def _mix_c(o_hg, gate, o_mla, hg_out_norm):
    parts = []
    for hh in range(HG_HEADS):
        sl = slice(hh * HG_DIM, (hh + 1) * HG_DIM)
        parts.append(_rms(o_hg[:, sl], hg_out_norm[:, sl]))
    o = jnp.concatenate(parts, axis=-1) * _silu(gate)
    return jnp.concatenate([o, o_mla], axis=-1)


def _rope_perm():
    p = np.zeros((128, 128), np.float32)
    half = MLA_ROPE // 2
    for i in range(half):
        p[i + half, i] = -1.0
        p[i, i + half] = 1.0
    return jnp.asarray(p)


def _mixer_fwd(h, cos, sin, w, tag, next_gain):
    d = h.shape[1]
    hn, = _rows(_rms, [h], [w["norm_mix"]], [(d, bf16)], name=f"mix_norm_{tag}")
    proj = _mm(hn, w["mix_w_in"], name=f"mix_in_{tag}")
    pa = [w["lb0"], w["lb1"], w["mla_q_a_norm"], w["mla_kv_a_norm"]]
    qf, kk, vv, logf, cqn, ckvn = _rows(
        _mix_a, [proj], pa, [(HG_WIDTH, f32)] * 4 + [(MLA_Q_RANK, bf16), (MLA_KV_RANK, bf16)], name=f"mix_a_{tag}")
    o_hg, states = _hg_fwd(qf, kk, vv, logf, name=f"hg_fwd_{tag}")
    qraw = _mm(cqn, w["mla_w_uq"], out_dtype=bf16, name=f"mla_uq_{tag}")
    kvraw = _mm(ckvn, w["mla_w_ukv"], out_dtype=bf16, name=f"mla_ukv_{tag}")
    pb = [w["mla_qn_nope"], w["mla_qn_rope"], w["mla_kn_nope"], w["mla_kn_rope"], w["rope_perm"]]
    kpe_raw, gate = _window(proj, _C_KPE, IN_PAD - _C_KPE), _window(proj, 3 * HG_WIDTH, HG_WIDTH)
    qfull, kfull, vfull = _rows(_mix_b, [qraw, kvraw, kpe_raw, cos, sin], pb,
                                [(MLA_HEADS * MLA_QK_PAD, bf16)] * 2 + [(MLA_HEADS * MLA_V, bf16)],
                                name=f"mix_b_{tag}")
    o_mla, lse = _att_fwd(qfull, kfull, vfull, name=f"att_fwd_{tag}")
    mixin, = _rows(_mix_c, [o_hg, gate, o_mla], [w["hg_out_norm"]], [(d, bf16)], name=f"mix_c_{tag}")
    if callable(w["mix_w_out"]):
        w["mix_w_out"] = w["mix_w_out"](mixin)
    out, normed = _mm(mixin, w["mix_w_out"], add=h, norm=next_gain, name=f"mix_out_{tag}")
    return out, (h, hn, proj, qf, kk, vv, logf, cqn, ckvn, o_hg, states, qraw, kvraw, qfull, kfull, vfull, o_mla,
                 lse, mixin), normed


def _mixer_bwd(dout, cos, sin, w, saved, tag, on_w_out=None):
    (h, hn, proj, qf, kk, vv, logf, cqn, ckvn, o_hg, states, qraw, kvraw, qfull, kfull, vfull, o_mla, lse,
     mixin) = saved
    g = {}
    dmixin = _mm(dout, w["mix_w_out"], tb=True, out_dtype=bf16, name=f"mix_dmixin_{tag}")
    g["mix_w_out"] = _mm(mixin, dout, ta=True, out_dtype=bf16, name=f"mix_dwout_{tag}")
    if on_w_out is not None:
        dmixin = on_w_out(g["mix_w_out"], dmixin)
    kpe_raw, gate = _window(proj, _C_KPE, IN_PAD - _C_KPE), _window(proj, 3 * HG_WIDTH, HG_WIDTH)
    (do_hg, dgate, do_mla), (g["hg_out_norm"],) = _rows_bwd(
        _mix_c, [o_hg, gate, o_mla], [w["hg_out_norm"]], [dmixin], rgrad=[f32, f32, bf16], pgrad=[True],
        name=f"mix_dc_{tag}")
    dqfull, dkfull, dvfull = _att_bwd(qfull, kfull, vfull, o_mla, lse, do_mla, name=f"att_bwd_{tag}")
    pb = [w["mla_qn_nope"], w["mla_qn_rope"], w["mla_kn_nope"], w["mla_kn_rope"], w["rope_perm"]]
    (dqraw, dkvraw, dkpe_raw), pg = _rows_bwd(
        _mix_b, [qraw, kvraw, kpe_raw, cos, sin], pb, [dqfull, dkfull, dvfull],
        rgrad=[bf16, bf16, f32, None, None], pgrad=[True, True, True, True, False],
        name=f"mix_db_{tag}")
    g["mla_qn_nope"], g["mla_qn_rope"], g["mla_kn_nope"], g["mla_kn_rope"] = pg
    dcqn = _mm(dqraw, w["mla_w_uq"], tb=True, name=f"mla_dcq_{tag}")
    g["mla_w_uq"] = _mm(cqn, dqraw, ta=True, name=f"mla_dwuq_{tag}")
    dckvn = _mm(dkvraw, w["mla_w_ukv"], tb=True, name=f"mla_dckv_{tag}")
    g["mla_w_ukv"] = _mm(ckvn, dkvraw, ta=True, name=f"mla_dwukv_{tag}")
    dqf, dkk, dvv, dlogf = _hg_bwd(qf, kk, vv, logf, states, do_hg, name=f"hg_bwd_{tag}")
    pa = [w["lb0"], w["lb1"], w["mla_q_a_norm"], w["mla_kv_a_norm"]]
    (dproj,), (g["lb0"], g["lb1"], g["mla_q_a_norm"], g["mla_kv_a_norm"]) = _rows_bwd(
        _mix_a, [proj], pa, [dqf, dkk, dvv, dlogf, dcqn, dckvn], rgrad=[bf16], pgrad=[True] * 4,
        addends={0: [(dgate, 3 * HG_WIDTH), (dkpe_raw, _C_KPE)]}, name=f"mix_da_{tag}")
    dh, g["norm_mix"] = _mm(dproj, w["mix_w_in"], tb=True, rms_bwd=(h, w["norm_mix"], dout), name=f"mix_dhn_{tag}")
    g["mix_w_in_t"] = _mm(dproj, hn, ta=True, name=f"mix_dwin_{tag}")
    return dh, g


def _rope_tables(positions):
    inv_freq = 1.0 / (ROPE_BASE ** (jnp.arange(0, MLA_ROPE, 2, dtype=f32) / MLA_ROPE))
    ang = positions.astype(f32)[:, None] * inv_freq
    z = jnp.zeros((positions.shape[0], 128 - MLA_ROPE), f32)
    return (jnp.concatenate([jnp.cos(ang), jnp.cos(ang), z], axis=1),
            jnp.concatenate([jnp.sin(ang), jnp.sin(ang), z], axis=1))


def _pad_cols(a, n):
    return jnp.pad(a, ((0, 0), (0, n - a.shape[1])))


def _even_weights(p, j, layer, dt):
    w_uq = p["mla_w_uq"][j].reshape(MLA_Q_RANK, MLA_HEADS, MLA_QK)
    w_uq = jnp.pad(w_uq, ((0, 0), (0, 0), (0, MLA_QK_PAD - MLA_QK))).reshape(MLA_Q_RANK, MLA_HEADS * MLA_QK_PAD)
    return dict(
        norm_mix=p["norm_mix"][layer][None], mix_w_in=_pad_cols(p["mix_w_in"][j], IN_PAD).astype(dt),
        lb0=p["hg_lb_logits"][0][None], lb1=p["hg_lb_logits"][1][None],
        mla_q_a_norm=p["mla_q_a_norm"][j][None], mla_kv_a_norm=p["mla_kv_a_norm"][j][None],
        mla_w_uq=w_uq.astype(dt), mla_w_ukv=p["mla_w_ukv"][j].astype(dt),
        mla_qn_nope=p["mla_qn_nope"][j][None], mla_qn_rope=_pad_cols(p["mla_qn_rope"][j][None], 128),
        mla_kn_nope=p["mla_kn_nope"][j][None], mla_kn_rope=_pad_cols(p["mla_kn_rope"][j][None], 128),
        rope_perm=_rope_perm(), hg_out_norm=p["hg_out_norm"][j][None],
        mix_w_out=p["mix_w_out"][j].astype(dt) if "mix_w_out" in p else None)


def _even_grads(g):
    w_uq = g["mla_w_uq"].reshape(MLA_Q_RANK, MLA_HEADS, MLA_QK_PAD)[:, :, :MLA_QK].reshape(MLA_Q_RANK, -1)
    return dict(
        norm_mix=g["norm_mix"], mix_w_in=g["mix_w_in_t"][:IN_WIDTH].T[None],
        hg_lb_logits=jnp.concatenate([g["lb0"], g["lb1"]], axis=0),
        mla_q_a_norm=g["mla_q_a_norm"], mla_kv_a_norm=g["mla_kv_a_norm"], mla_w_uq=w_uq[None],
        mla_w_ukv=g["mla_w_ukv"][None], mla_qn_nope=g["mla_qn_nope"], mla_qn_rope=g["mla_qn_rope"][:, :MLA_ROPE],
        mla_kn_nope=g["mla_kn_nope"], mla_kn_rope=g["mla_kn_rope"][:, :MLA_ROPE],
        hg_out_norm=g["hg_out_norm"], mix_w_out=g["mix_w_out"][None])


_S5_NB = 8
_S5_BW = 1024
_S5_HALF = 512
_S5_UC = 128
_S5_TIME = 2048


def _cmul(ar, ai, br, bi):
    return ar * br - ai * bi, ar * bi + ai * br


def _pow_table(ar, ai, descending):
    rows = lax.broadcasted_iota(jnp.int32, (8, ar.shape[1]), 0)
    tr = jnp.zeros((8, ar.shape[1]), f32)
    ti = jnp.zeros((8, ar.shape[1]), f32)
    pr, pi_ = ar, ai
    for r in range(8):
        sel = rows == ((7 - r) if descending else r)
        tr = jnp.where(sel, pr, tr)
        ti = jnp.where(sel, pi_, ti)
        pr, pi_ = _cmul(pr, pi_, ar, ai)
    return tr, ti


def _s5_tile_scan(work, carry, ar, ai, tc, reverse, per_tile=None):
    hw = _S5_HALF
    row8 = lax.broadcasted_iota(jnp.int32, (8, hw), 0)
    powers = [(ar, ai)]
    for _ in range(2):
        powers.append(_cmul(*powers[-1], *powers[-1]))
    steps = []
    for (mr, mi), s in zip(powers, (1, 2, 4)):
        ok = (row8 < 8 - s) if reverse else (row8 >= s)
        steps.append((jnp.where(ok, mr, 0.0), jnp.where(ok, mi, 0.0), 8 - s if reverse else s))
    tr, ti = _pow_table(ar, ai, reverse)
    cr, ci = carry[:, :hw], carry[:, hw:]
    tiles = range(tc // 8)
    for i in (reversed(tiles) if reverse else tiles):
        sl = slice(8 * i, 8 * i + 8)
        xr, xi = work[sl, :hw], work[sl, hw:]
        for mr, mi, shift in steps:
            pr, pi_ = _cmul(mr, mi, pltpu.roll(xr, shift, axis=0), pltpu.roll(xi, shift, axis=0))
            xr, xi = xr + pr, xi + pi_
        pr, pi_ = _cmul(tr, ti, cr, ci)
        xr, xi = xr + pr, xi + pi_
        work[sl, :hw] = xr
        work[sl, hw:] = xi
        if per_tile is not None:
            per_tile(sl, xr, xi, cr, ci)
        edge = 8 * i if reverse else 8 * i + 7
        cr, ci = work[edge:edge + 1, :hw], work[edge:edge + 1, hw:]
    carry[:, :hw] = cr
    carry[:, hw:] = ci


def _s5_core_fwd(a, hn, b3, c3, *, name):
    length = hn.shape[0]
    tc = min(_S5_TIME, length)

    def body(a_ref, hn_ref, b_ref, c_ref, hs_ref, y_ref, work, carry):
        @pl.when(pl.program_id(1) == 0)
        def _():
            carry[...] = jnp.zeros_like(carry)

        work[...] = jnp.dot(hn_ref[...].astype(bf16), b_ref[...], preferred_element_type=f32)
        _s5_tile_scan(work, carry, a_ref[:, :_S5_HALF], a_ref[:, _S5_HALF:], tc, False)
        hs = work[...].astype(bf16)
        hs_ref[...] = hs
        y_ref[...] = jnp.dot(hs, c_ref[...], preferred_element_type=f32)

    return pl.pallas_call(
        body, grid=(_S5_NB, length // tc),
        in_specs=[pl.BlockSpec((1, _S5_BW), lambda j, t: (0, j)), pl.BlockSpec((tc, _S5_UC), lambda j, t: (t, j)),
                  pl.BlockSpec((_S5_UC, _S5_BW), lambda j, t: (j, 0)), pl.BlockSpec((_S5_BW, _S5_UC), lambda j, t: (j, 0))],
        out_specs=[pl.BlockSpec((tc, _S5_BW), lambda j, t: (t, j)), pl.BlockSpec((tc, _S5_UC), lambda j, t: (t, j))],
        out_shape=[jax.ShapeDtypeStruct((length, _S5_NB * _S5_BW), bf16),
                   jax.ShapeDtypeStruct((length, _S5_NB * _S5_UC), f32)],
        scratch_shapes=[pltpu.VMEM((tc, _S5_BW), f32), pltpu.VMEM((1, _S5_BW), f32)],
        compiler_params=_cparams(("parallel", "arbitrary")), name=name)(a, hn, b3, c3)


def _s5_core_bwd(a, dy, c3, hs, hn, b3, *, name):
    length = hn.shape[0]
    tc = min(_S5_TIME, length)
    nt = length // tc
    hw = _S5_HALF

    def body(a_ref, dy_ref, c_ref, hs_ref, hn_ref, b_ref, du_ref, db_ref, dc_ref, da_ref, work, carry, acc):
        @pl.when(pl.program_id(1) == 0)
        def _():
            carry[...] = jnp.zeros_like(carry)
            db_ref[...] = jnp.zeros_like(db_ref)
            dc_ref[...] = jnp.zeros_like(dc_ref)
            da_ref[...] = jnp.zeros_like(da_ref)

        dyb = dy_ref[...].astype(bf16)
        work[...] = lax.dot_general(dyb, c_ref[...], _NT, preferred_element_type=f32)
        acc[...] = jnp.zeros_like(acc)
        row8 = lax.broadcasted_iota(jnp.int32, (8, hw), 0)

        def grad_a(sl, gr, gi, cr, ci):
            gnr = jnp.where(row8 == 7, cr, pltpu.roll(gr, 7, axis=0))
            gni = jnp.where(row8 == 7, ci, pltpu.roll(gi, 7, axis=0))
            hr, hi = hs_ref[sl, :hw].astype(f32), hs_ref[sl, hw:].astype(f32)
            acc[:, :hw] += hr * gnr + hi * gni
            acc[:, hw:] += hr * gni - hi * gnr

        _s5_tile_scan(work, carry, a_ref[:, :hw], -a_ref[:, hw:], tc, True, grad_a)
        da_ref[...] += jnp.sum(acc[...], axis=0, keepdims=True)
        g = work[...].astype(bf16)
        du_ref[...] = lax.dot_general(g, b_ref[...], _NT, preferred_element_type=f32)
        db_ref[...] += lax.dot_general(hn_ref[...].astype(bf16), g, _TN, preferred_element_type=f32)
        dc_ref[...] += lax.dot_general(hs_ref[...], dyb, _TN, preferred_element_type=f32)

    rev = lambda j, t: (nt - 1 - t, j)
    return pl.pallas_call(
        body, grid=(_S5_NB, nt),
        in_specs=[pl.BlockSpec((1, _S5_BW), lambda j, t: (0, j)), pl.BlockSpec((tc, _S5_UC), rev),
                  pl.BlockSpec((_S5_BW, _S5_UC), lambda j, t: (j, 0)), pl.BlockSpec((tc, _S5_BW), rev),
                  pl.BlockSpec((tc, _S5_UC), rev), pl.BlockSpec((_S5_UC, _S5_BW), lambda j, t: (j, 0))],
        out_specs=[pl.BlockSpec((tc, _S5_UC), rev), pl.BlockSpec((_S5_UC, _S5_BW), lambda j, t: (j, 0)),
                   pl.BlockSpec((_S5_BW, _S5_UC), lambda j, t: (j, 0)), pl.BlockSpec((1, _S5_BW), lambda j, t: (0, j))],
        out_shape=[jax.ShapeDtypeStruct((length, _S5_NB * _S5_UC), f32),
                   jax.ShapeDtypeStruct((_S5_NB * _S5_UC, _S5_BW), f32),
                   jax.ShapeDtypeStruct((_S5_NB * _S5_BW, _S5_UC), f32),
                   jax.ShapeDtypeStruct((1, _S5_NB * _S5_BW), f32)],
        scratch_shapes=[pltpu.VMEM((tc, _S5_BW), f32), pltpu.VMEM((1, _S5_BW), f32), pltpu.VMEM((8, _S5_BW), f32)],
        compiler_params=_cparams(("parallel", "arbitrary")), name=name)(a, dy, c3, hs, hn, b3)


def _s5_disc(lr, li, ldt, btr, bti, expand):
    dt = jnp.exp(ldt)
    mag = jnp.exp(lr * dt)
    abr = mag * jnp.cos(li * dt)
    abi = mag * jnp.sin(li * dt)
    den = lr * lr + li * li
    zr = ((abr - 1.0) * lr + abi * li) / den
    zi = (abi * lr - (abr - 1.0) * li) / den
    zr = jnp.dot(zr, expand, precision=lax.Precision.HIGHEST, preferred_element_type=f32)
    zi = jnp.dot(zi, expand, precision=lax.Precision.HIGHEST, preferred_element_type=f32)
    return abr, abi, zr * btr - zi * bti, zr * bti + zi * btr


def _s5_disc_fwd(args, *, name):
    def body(*refs):
        res = _s5_disc(*[r[...] for r in refs[:6]])
        for o, v in zip(refs[6:], res):
            o[...] = v

    sds = jax.ShapeDtypeStruct
    return pl.pallas_call(body, out_shape=[sds(args[0].shape, f32)] * 2 + [sds(args[3].shape, f32)] * 2,
                          name=name)(*args)


def _s5_disc_bwd(args, cts, *, name):
    def body(*refs):
        vals = [r[...] for r in refs[:6]]
        _, vjp = jax.vjp(lambda *d: _s5_disc(*d, vals[5]), *vals[:5])
        grads = vjp(tuple(r[...] for r in refs[6:10]))
        for o, v in zip(refs[10:], grads):
            o[...] = v

    return pl.pallas_call(body, out_shape=[jax.ShapeDtypeStruct(a.shape, f32) for a in args[:5]],
                          name=name)(*args, *cts)


def _gelu_tanh(x):
    return 0.5 * x * (1.0 + jnp.tanh(0.7978845608028654 * (x + 0.044715 * (x * x * x))))


def _s5_post(y, u, d_skip):
    return _gelu_tanh(y + d_skip * u)


def _s5_glu(ga, gb, h):
    return h + ga * jax.nn.sigmoid(gb)


def _s5_glu_norm(ga, gb, h, next_gain):
    out = _s5_glu(ga, gb, h)
    return out, _rms(out, next_gain)


def _s5_expand():
    e = np.zeros((S5_STATE, S5_GROUP * S5_STATE), np.float32)
    for m in range(S5_GROUP):
        e[np.arange(S5_STATE), m * S5_STATE + np.arange(S5_STATE)] = 1.0
    return jnp.asarray(e)


def _s5_pack_b(bbr, bbi):
    eye = jnp.eye(8, dtype=f32)

    def one(bb):
        b5 = bb.reshape(_S5_NB, 8, S5_GROUP, S5_STATE)
        return jnp.einsum("jgmp,gh->jgmhp", b5, eye).reshape(_S5_NB * _S5_UC, _S5_HALF)

    return jnp.concatenate([one(bbr), one(bbi)], axis=1)


def _s5_unpack_b(db3):
    def one(d):
        d5 = d.reshape(_S5_NB, 8, S5_GROUP, 8, S5_STATE)
        return jnp.einsum("jgmgp->jgmp", d5).reshape(S5_GROUPS, S5_GROUP * S5_STATE)

    return one(db3[:, :_S5_HALF]), one(db3[:, _S5_HALF:])


def _s5_pack_c(c_re, c_im):
    eye = jnp.eye(8, dtype=f32)

    def one(c):
        c4 = c.reshape(_S5_NB, 8, S5_GROUP, S5_STATE)
        return jnp.einsum("jgmp,hg->jhpgm", c4, eye).reshape(_S5_NB, _S5_HALF, _S5_UC)

    return jnp.concatenate([one(c_re), -one(c_im)], axis=1).reshape(_S5_NB * _S5_BW, _S5_UC)


def _s5_unpack_c(dc3):
    d = dc3.reshape(_S5_NB, 2, 8, S5_STATE, 8, S5_GROUP)
    dre = jnp.einsum("jgpgm->jgmp", d[:, 0]).reshape(S5_GROUPS, S5_GROUP, S5_STATE)
    dim = -jnp.einsum("jgpgm->jgmp", d[:, 1]).reshape(S5_GROUPS, S5_GROUP, S5_STATE)
    return dre, dim


def _s5_state_row(re, im):
    r = re.reshape(_S5_NB, 1, _S5_HALF)
    i = im.reshape(_S5_NB, 1, _S5_HALF)
    return jnp.concatenate([r, i], axis=2).reshape(1, _S5_NB * _S5_BW)


def _s5_unstate_row(row):
    r = row.reshape(_S5_NB, 2, 8, S5_STATE)
    return r[:, 0].reshape(S5_GROUPS, S5_STATE), r[:, 1].reshape(S5_GROUPS, S5_STATE)


def _s5_fwd(h, w, tag, next_gain):
    d = h.shape[1]
    hn, = _rows(_rms, [h], [w["norm_mix"]], [(d, f32)], name=f"s5_norm_{tag}")
    disc_in = [w["s5_lam_re"], w["s5_lam_im"], w["s5_log_dt"], w["s5_bt_re"], w["s5_bt_im"], w["s5_expand"]]
    abr, abi, bbr, bbi = _s5_disc_fwd(disc_in, name=f"s5_disc_{tag}")
    a_row = _s5_state_row(abr, abi)
    b3 = _s5_pack_b(bbr, bbi).astype(bf16)
    hs, y = _s5_core_fwd(a_row, hn, b3, w["s5_c3"], name=f"s5_core_{tag}")
    yg, = _rows(_s5_post, [y, hn], [w["s5_d"]], [(d, bf16)], name=f"s5_post_{tag}")
    ga = _mm(yg, w["s5_w_glu_a"], out_dtype=bf16, name=f"s5_glu_a_{tag}")
    gb = _mm(yg, w["s5_w_glu_b"], out_dtype=bf16, name=f"s5_glu_b_{tag}")
    out, normed = _rows(_s5_glu_norm, [ga, gb, h], [next_gain], [(d, f32), (d, bf16)], name=f"s5_glu_{tag}")
    return out, (h, hn, disc_in, a_row, b3, hs, y, yg, ga, gb), normed


def _s5_bwd(dout, w, saved, tag):
    h, hn, disc_in, a_row, b3, hs, y, yg, ga, gb = saved
    g = {}
    (dga, dgb), _ = _rows_bwd(_s5_glu, [ga, gb, h], [], [dout], rgrad=[bf16, bf16, None], pgrad=[],
                              name=f"s5_dglu_{tag}")
    dyg = _mm(dga, w["s5_w_glu_a"], tb=True, name=f"s5_dyg_a_{tag}")
    dyg = _mm(dgb, w["s5_w_glu_b"], tb=True, add=dyg, out_dtype=bf16, name=f"s5_dyg_b_{tag}")
    g["s5_w_glu_a"] = _mm(yg, dga, ta=True, out_dtype=bf16, name=f"s5_dwa_{tag}")
    g["s5_w_glu_b"] = _mm(yg, dgb, ta=True, out_dtype=bf16, name=f"s5_dwb_{tag}")
    (dy, du_skip), (g["s5_d"],) = _rows_bwd(_s5_post, [y, hn], [w["s5_d"]], [dyg], rgrad=[bf16, f32], pgrad=[True],
                                           name=f"s5_dpost_{tag}")
    du, db3, dc3, da_row = _s5_core_bwd(a_row, dy, w["s5_c3"], hs, hn, b3, name=f"s5_dcore_{tag}")
    dabr, dabi = _s5_unstate_row(da_row)
    dbbr, dbbi = _s5_unpack_b(db3)
    g["s5_lam_re"], g["s5_lam_im"], g["s5_log_dt"], g["s5_bt_re"], g["s5_bt_im"] = _s5_disc_bwd(
        disc_in, [dabr, dabi, dbbr, dbbi], name=f"s5_ddisc_{tag}")
    g["s5_c_re"], g["s5_c_im"] = _s5_unpack_c(dc3)
    (dh,), (g["norm_mix"],) = _rows_bwd(_rms_twice, [h], [w["norm_mix"]], [du, du_skip], rgrad=[f32], pgrad=[True],
                                        addends={0: dout}, name=f"s5_dnorm_{tag}")
    return dh, g


def _odd_weights(p, j, layer, dt):
    tr = lambda b: b.transpose(0, 2, 1).reshape(S5_GROUPS, S5_GROUP * S5_STATE)
    return dict(
        norm_mix=p["norm_mix"][layer][None], s5_lam_re=p["s5_lam_re"][j], s5_lam_im=p["s5_lam_im"][j],
        s5_log_dt=p["s5_log_dt"][j][:, None], s5_bt_re=tr(p["s5_b_re"][j]), s5_bt_im=tr(p["s5_b_im"][j]),
        s5_expand=_s5_expand(), s5_c3=_s5_pack_c(p["s5_c_re"][j], p["s5_c_im"][j]).astype(dt),
        **{n: (p[n][j][None] if n == "s5_d" else p[n][j].astype(dt))
           for n in ("s5_d", "s5_w_glu_a", "s5_w_glu_b") if n in p})


def _odd_grads(g):
    tr = lambda b: b.reshape(S5_GROUPS, S5_GROUP, S5_STATE).transpose(0, 2, 1)[None]
    return dict(
        norm_mix=g["norm_mix"], s5_lam_re=g["s5_lam_re"][None], s5_lam_im=g["s5_lam_im"][None],
        s5_log_dt=g["s5_log_dt"][:, 0][None], s5_b_re=tr(g["s5_bt_re"]), s5_b_im=tr(g["s5_bt_im"]),
        s5_c_re=g["s5_c_re"][None], s5_c_im=g["s5_c_im"][None], s5_d=g["s5_d"],
        s5_w_glu_a=g["s5_w_glu_a"][None], s5_w_glu_b=g["s5_w_glu_b"][None])


FF_SHARD = 352
FF_SHARD_PAD = 384


def _pad_groups(a, axis):
    axis %= a.ndim
    zeros = jnp.zeros(a.shape[:axis] + (FF_SHARD_PAD - FF_SHARD,) + a.shape[axis + 1:], a.dtype)
    pieces = []
    for g in range(a.shape[axis] // FF_SHARD):
        pieces += [lax.slice_in_dim(a, g * FF_SHARD, (g + 1) * FF_SHARD, axis=axis), zeros]
    return jnp.concatenate(pieces, axis=axis)


def _unpad_groups(a, axis):
    axis %= a.ndim
    pieces = [lax.slice_in_dim(a, g * FF_SHARD_PAD, g * FF_SHARD_PAD + FF_SHARD, axis=axis)
              for g in range(a.shape[axis] // FF_SHARD_PAD)]
    return pieces[0] if len(pieces) == 1 else jnp.concatenate(pieces, axis=axis)


_LANES = 1024
_ROW_PAD = 16


_PEER_MASKS = (1, 2, 4, 3, 5, 6, 7)


def _mesh_place():
    x, y, c = lax.axis_index("x"), lax.axis_index("y"), lax.axis_index("c")

    def peer(mask):
        px = 1 - x if mask & 4 else x
        py = 1 - y if mask & 2 else y
        pc = 1 - c if mask & 1 else c
        return (px, py, pc), 4 * px + 2 * py + pc

    return 4 * x + 2 * y + c, peer


class _Exchange:
    def __init__(self, name):
        self.name = name
        self.srcs, self.shapes, self.items, self.where = [], [], [], {}

    def add(self, src, land_shape, src_at, dst_at, key):
        si = next((i for i, s in enumerate(self.srcs) if s is src), None)
        if si is None:
            self.srcs.append(src)
            si = len(self.srcs) - 1
        if key not in self.where:
            self.shapes.append(land_shape)
            self.where[key] = len(self.shapes) - 1
        self.items.append(dict(src=si, dst=self.where[key], src_at=src_at, dst_at=dst_at))

    def _copy(self, k, mask, ins, lands, send_sems, recv_sems, me, peer, arriving):
        it = self.items[k]
        dev, idx = peer(mask)
        s = k * (N_DEV - 1) + _PEER_MASKS.index(mask)
        return pltpu.make_async_remote_copy(
            src_ref=it["src_at"](ins[it["src"]], idx), dst_ref=it["dst_at"](lands[it["dst"]], idx if arriving else me),
            send_sem=send_sems.at[s], recv_sem=recv_sems.at[s], device_id=dev, device_id_type=pl.DeviceIdType.MESH)

    def _own_copy(self, k, ins, lands, own_sems, me):
        it = self.items[k]
        return pltpu.make_async_copy(it["src_at"](ins[it["src"]], me), it["dst_at"](lands[it["dst"]], me), own_sems.at[k])

    def begin(self, own):
        ns, nd, ni = len(self.srcs), len(self.shapes), len(self.items)
        nsem = ni * (N_DEV - 1)
        self.own = own

        nq = 3 if own else 2

        def body(*refs):
            ins, land_refs = refs[:ns], refs[ns:ns + nd]
            sems, token = refs[ns + nd:ns + nd + nq], refs[-1]
            me, peer = _mesh_place()
            for mask in _PEER_MASKS:
                for k in range(ni):
                    self._copy(k, mask, ins, land_refs, sems[0], sems[1], me, peer, False).start()
            if own:
                for k in range(ni):
                    self._own_copy(k, ins, land_refs, sems[2], me).start()
            token[...] = jnp.zeros_like(token)

        hbm = pl.BlockSpec(memory_space=pltpu.HBM)
        sem = pl.BlockSpec(memory_space=pltpu.SEMAPHORE)
        lands = [lax.empty(s.shape, s.dtype) for s in self.shapes]
        sem_shapes = [pltpu.SemaphoreType.DMA((nsem,)), pltpu.SemaphoreType.DMA((nsem,)), pltpu.SemaphoreType.DMA((ni,))]
        res = pl.pallas_call(
            body, in_specs=[hbm] * (ns + nd),
            out_specs=[sem] * nq + [hbm] * nd + [pl.BlockSpec(memory_space=pltpu.VMEM)],
            out_shape=sem_shapes[:nq] + [pltpu.HBM(s.shape, s.dtype) for s in self.shapes]
            + [jax.ShapeDtypeStruct((8, 128), f32)],
            input_output_aliases={ns + j: nq + j for j in range(nd)},
            compiler_params=pltpu.CompilerParams(has_side_effects=pltpu.SideEffectType.DATAFLOW_SIDE_EFFECTING),
            name=self.name + "_start")(*self.srcs, *lands)
        self.token = res[-1]
        return list(res[:nq]), list(res[nq:-1])

    def finish(self, state, after):
        sems, lands = state
        nq = len(sems)
        after = list(after) if isinstance(after, (list, tuple)) else [after]
        ns, nd, ni = len(self.srcs), len(self.shapes), len(self.items)

        def body(*refs):
            ins, land_refs = refs[:ns], refs[ns:ns + nd]
            sem_refs = refs[ns + nd:ns + nd + nq]
            me, peer = _mesh_place()
            for mask in _PEER_MASKS:
                for k in range(ni):
                    cp = self._copy(k, mask, ins, land_refs, sem_refs[0], sem_refs[1], me, peer, True)
                    cp.wait_send()
                    cp.wait_recv()
            if self.own:
                for k in range(ni):
                    self._own_copy(k, ins, land_refs, sem_refs[2], me).wait()

        hbm = pl.BlockSpec(memory_space=pltpu.HBM)
        sem = pl.BlockSpec(memory_space=pltpu.SEMAPHORE)
        res = pl.pallas_call(
            body, in_specs=[hbm] * (ns + nd) + [sem] * nq + [pl.BlockSpec(memory_space=pl.ANY)] * len(after),
            out_specs=[hbm] * nd, out_shape=[pltpu.HBM(s.shape, s.dtype) for s in self.shapes],
            input_output_aliases={ns + j: j for j in range(nd)},
            compiler_params=pltpu.CompilerParams(has_side_effects=pltpu.SideEffectType.DATAFLOW_SIDE_EFFECTING),
            name=self.name + "_wait")(*self.srcs, *lands, *sems, *after)
        return {k: res[i] for k, i in self.where.items()}


def _after(x, *tokens, name):
    def body(*refs):
        del refs

    anyspace = pl.BlockSpec(memory_space=pl.ANY)
    return pl.pallas_call(body, in_specs=[anyspace] * (1 + len(tokens)), out_specs=anyspace,
                          out_shape=jax.ShapeDtypeStruct(x.shape, x.dtype), input_output_aliases={0: 0},
                          name=name)(x, *tokens)


def _rows_of(n):
    return lambda r, i: r.at[pl.ds(pl.multiple_of(i * n, n), n), :]


def _cols_of(n):
    return lambda r, i: r.at[:, pl.ds(pl.multiple_of(i * n, n), n)]


def _whole(r, i):
    return r


def _slot(r, i):
    return r.at[i]


def _at_layer(layer):
    return lambda r, i: r.at[layer]


def _sum_adam(me_index, slots, owns, own_block, w, m, v, *, name):
    layers, rows, cols = w.shape
    tr = _pick(rows, (256, 128, 104, 64, 32, 16, 8))
    bc1 = 1.0 - ADAM_B1 ** ADAM_STEP
    bc2 = 1.0 - ADAM_B2 ** ADAM_STEP
    own_shape, own_map = own_block(tr)
    nl = len(slots)
    assert nl == layers and len(owns) == layers

    def body(me_ref, *refs):
        s_refs, own_refs = refs[:nl], refs[nl:2 * nl]
        w_ref, m_ref, v_ref, g_ref, d_ref, nm_ref, nv_ref = refs[2 * nl:]
        me = me_ref[0]

        def run(s_ref, own_ref):
            mine = (own_ref[0] if len(own_shape) == 3 else own_ref[...]).astype(f32)
            g = jnp.where(me == 0, mine, s_ref[0].astype(f32))
            for k in range(1, N_DEV):
                g = g + jnp.where(me == k, mine, s_ref[k].astype(f32))
            mm = ADAM_B1 * m_ref[0] + (1.0 - ADAM_B1) * g
            vv = ADAM_B2 * v_ref[0] + (1.0 - ADAM_B2) * (g * g)
            g_ref[0] = g
            nm_ref[0] = mm
            nv_ref[0] = vv
            d_ref[0] = -ADAM_LR * ((mm / bc1) / (jnp.sqrt(vv / bc2) + ADAM_EPS) + ADAM_WD * w_ref[0])

        for layer in range(nl):
            pl.when(pl.program_id(0) == layer)(functools.partial(run, s_refs[layer], own_refs[layer]))

    def of_layer(layer, index_map):
        return lambda lyr, i, me: index_map(jnp.where(lyr == layer, i, 0), me)

    blk = pl.BlockSpec((1, tr, cols), lambda lyr, i, me: (lyr, i, 0))
    sds = jax.ShapeDtypeStruct((layers, rows, cols), f32)
    grid_spec = pltpu.PrefetchScalarGridSpec(
        num_scalar_prefetch=1, grid=(layers, rows // tr),
        in_specs=[pl.BlockSpec((N_DEV, tr, cols), of_layer(layer, lambda i, me: (0, i, 0))) for layer in range(nl)]
        + [pl.BlockSpec(own_shape, of_layer(layer, own_map)) for layer in range(nl)] + [blk, blk, blk],
        out_specs=[blk] * 4)
    return pl.pallas_call(body, grid_spec=grid_spec, out_shape=[sds] * 4,
                          compiler_params=_cparams(("arbitrary", "arbitrary")),
                          name=name)(me_index, *slots, *owns, w, m, v)


_SHARDED = dict(xa_wq=1, xa_wk=1, xa_wv=1, xa_wo=1, ffn_w_up=2, ffn_conv_w=2, ffn_w_down=1, mix_w_in=2, mla_w_uq=2,
                mla_w_ukv=2, mix_w_out=1, s5_d=1, s5_w_glu_a=1, s5_w_glu_b=1)
_EXACT = ("ffn_conv_w", "s5_d")
_WEIGHTS = ("norm_mix", "norm_xa", "norm_mem", "norm_ffn", "xa_wq", "xa_wk", "xa_wv", "xa_wo", "xa_q_norm",
            "xa_k_norm", "ffn_w_up", "ffn_conv_w", "ffn_conv_b", "ffn_w_down", "hg_lb_logits", "mix_w_in",
            "hg_out_norm", "mla_q_a_norm", "mla_w_uq", "mla_kv_a_norm", "mla_w_ukv", "mla_qn_nope", "mla_qn_rope",
            "mla_kn_nope", "mla_kn_rope", "mix_w_out", "s5_lam_re", "s5_lam_im", "s5_log_dt", "s5_b_re", "s5_b_im",
            "s5_c_re", "s5_c_im", "s5_d", "s5_w_glu_a", "s5_w_glu_b")
_BIG = tuple(n for n in _WEIGHTS if n in _SHARDED and n not in _EXACT)
_SHARD_ORDER = tuple(n for n in _WEIGHTS if n in _SHARDED)
_REPL_ORDER = tuple(n for n in _WEIGHTS if n not in _SHARDED)
_REPL_EARLY = tuple(n for n in _REPL_ORDER if n.startswith("s5_"))
_REPL_LATE = tuple(n for n in _REPL_ORDER if n not in _REPL_EARLY)


def _pack(parts, dtype, lead=None):
    nl = 0 if lead is None else 1
    flat = [a.astype(dtype).reshape(a.shape[:nl] + (-1,)) for a in parts]
    cat = jnp.concatenate(flat, axis=nl)
    n = cat.shape[nl]
    unit = _LANES * _ROW_PAD
    total = -(-n // unit) * unit
    cat = jnp.pad(cat, [(0, 0)] * nl + [(0, total - n)])
    return cat.reshape(cat.shape[:nl] + (total // _LANES, _LANES))


def _unpack(packed, shapes, lead=None):
    nl = 0 if lead is None else 1
    flat = packed.reshape(packed.shape[:nl] + (-1,))
    out, off = [], 0
    for s in shapes:
        n = int(np.prod(s))
        piece = flat[..., off:off + n] if nl else flat[off:off + n]
        out.append(piece.reshape(packed.shape[:nl] + tuple(s)))
        off += n
    return out


def _to_full(gathered, axis):
    g = jnp.moveaxis(gathered, 0, axis)
    s = g.shape
    return g.reshape(s[:axis] + (s[axis] * s[axis + 1],) + s[axis + 2:])


def _to_shards(full, axis):
    s = full.shape
    g = full.reshape(s[:axis] + (N_DEV, s[axis] // N_DEV) + s[axis + 1:])
    return jnp.moveaxis(g, axis, 0)


_DIRECT_ROWS = ("xa_wq", "xa_wk", "xa_wv", "xa_wo", "mix_w_out", "s5_w_glu_a", "s5_w_glu_b")
_SMALL16 = ("mix_w_in", "mla_w_uq", "mla_w_ukv")
_SMALL_SHARDED = ("mla_w_uq", "mla_w_ukv") + _EXACT
_SHARD_ROWS = 128


def _exchange_layout(d):
    out = dict(d)
    out["ffn_w_up"] = _pad_groups(d["ffn_w_up"], 2)
    out["ffn_conv_w"] = _pad_groups(d["ffn_conv_w"], 2)
    out["ffn_w_down"] = _pad_groups(d["ffn_w_down"], 1)
    return out


def _train_step(x, mem, positions, target, w, m, v):
    d_model = x.shape[1]
    we_, me_, ve_ = _exchange_layout(w), _exchange_layout(m), _exchange_layout(v)
    sds = jax.ShapeDtypeStruct

    matrices = _DIRECT_ROWS + ("ffn_w_up", "ffn_w_down")
    layer_mats = ("xa_wq", "xa_wk", "xa_wv", "xa_wo", "ffn_w_up", "ffn_w_down")
    shard16 = {n: we_[n].astype(bf16) for n in matrices}
    part_of = {n: _rows_of(_SHARD_ROWS) for n in _DIRECT_ROWS}
    part_of["ffn_w_up"] = _cols_of(we_["ffn_w_up"].shape[2])
    part_of["ffn_w_down"] = _rows_of(we_["ffn_w_down"].shape[1])
    part_shape = {n: we_[n].shape[1:] for n in matrices}

    def full_shape(n):
        r, c = part_shape[n]
        return (r, N_DEV * c) if n == "ffn_w_up" else (N_DEV * r, c)

    def gather(ex, n, layer):
        ex.add(shard16[n], sds(full_shape(n), bf16), _at_layer(layer), part_of[n], (n, layer))

    def scatter(ex, n, layer, grad):
        ex.add(grad, sds((N_DEV,) + part_shape[n], grad.dtype), part_of[n], _slot, (n, layer))

    small16 = _pack([we_[n] for n in _SMALL16], bf16)
    exact = _pack([we_[n] for n in _EXACT], f32)
    ga, ga1, gb, gc = _Exchange("gather_a"), _Exchange("gather_a1"), _Exchange("gather_b"), _Exchange("gather_c")
    ga.add(small16, sds((N_DEV,) + small16.shape, bf16), _whole, _slot, "small16")
    ga1.add(exact, sds((N_DEV,) + exact.shape, f32), _whole, _slot, "exact")
    gather(ga1, "mix_w_out", 0)
    for n in layer_mats:
        gather(gb, n, 0)
    gather(gc, "s5_w_glu_a", 0)
    gather(gc, "s5_w_glu_b", 0)
    for n in layer_mats:
        gather(gc, n, 1)
    state_a, state_a1, state_b, state_c = ga.begin(True), ga1.begin(True), gb.begin(True), gc.begin(True)

    p = {n: w[n] for n in _REPL_ORDER}
    cos, sin = _rope_tables(positions)
    wo = _odd_weights(p, 0, 1, bf16)
    conv_b = _pad_groups(w["ffn_conv_b"], 1)
    prepared = [cos, sin, conv_b, wo["s5_c3"], wo["s5_bt_re"], wo["s5_bt_im"]]
    full = ga.finish(state_a, [ga1.token, gb.token, gc.token] + prepared)
    for n, a in zip(_SMALL16, _unpack(full["small16"], [we_[n].shape for n in _SMALL16], lead=True)):
        p[n] = _to_full(a, _SHARDED[n])
    we = _even_weights(p, 0, 0, bf16)
    we["norm_mix"] = _after(we["norm_mix"], ga.token, ga1.token, gb.token, gc.token, name="after_gather_starts")

    def late_mix_w_out(mixin):
        full.update(ga1.finish(state_a1, [mixin]))
        return full[("mix_w_out", 0)]

    we["mix_w_out"] = late_mix_w_out
    h, s_mix0, hx = _mixer_fwd(x, cos, sin, we, "l0", w["norm_xa"][0][None])
    conv_w, s5_d = [_to_full(a, _SHARDED[n]) for n, a in
                    zip(_EXACT, _unpack(full["exact"], [we_[n].shape for n in _EXACT], lead=True))]

    def layer_weights(layer):
        return dict(norm_xa=w["norm_xa"][layer][None], norm_mem=w["norm_mem"][layer][None],
                    norm_ffn=w["norm_ffn"][layer][None], xa_q_norm=w["xa_q_norm"][layer][None],
                    xa_k_norm=w["xa_k_norm"][layer][None], ffn_conv_w=conv_w[layer],
                    ffn_conv_b=conv_b[layer][None], **{n: full[(n, layer)] for n in layer_mats})

    full.update(gb.finish(state_b, h))
    wl = [layer_weights(0)]
    h, s_xa0, hf = _xattn_fwd(h, mem, wl[0], "l0", hx)
    h, s_ff0 = _ffn_fwd(h, wl[0], "l0", hf)
    full.update(gc.finish(state_c, h))
    wl.append(layer_weights(1))
    wo.update(s5_d=s5_d, s5_w_glu_a=full[("s5_w_glu_a", 0)], s5_w_glu_b=full[("s5_w_glu_b", 0)])
    h, s_mix1, hx = _s5_fwd(h, wo, "l1", wl[1]["norm_xa"])
    h, s_xa1, hf = _xattn_fwd(h, mem, wl[1], "l1", hx)
    (dh, loss), s_ff1 = _ffn_fwd(h, wl[1], "l1", hf, loss_target=target)

    gl = [{}, {}]
    reduces = []

    own_grad = {}

    def reduce_start(name, entries, dh):
        ex = _Exchange(name)
        for n, layer, grad in entries.get("matrices", ()):
            scatter(ex, n, layer, grad)
            own_grad[(n, layer)] = grad
        for key, src, shape, src_at in entries.get("packs", ()):
            ex.add(src, shape, src_at, _slot, key)
        reduces.append((ex, ex.begin(False)))
        return _after(dh, ex.token, name="after_" + name)

    dh, gl[1] = _ffn_bwd(dh, wl[1], s_ff1, "l1")
    dh = reduce_start("reduce_ffn1", dict(matrices=[(n, 1, gl[1][n]) for n in ("ffn_w_up", "ffn_w_down")]), dh)
    dh, g = _xattn_bwd(dh, mem, wl[1], s_xa1, "l1")
    gl[1].update(g)
    dh = reduce_start("reduce_xa1", dict(matrices=[(n, 1, g[n]) for n in ("xa_wq", "xa_wk", "xa_wv", "xa_wo")]), dh)
    dh, g_odd = _s5_bwd(dh, wo, s_mix1, "l1")
    go = _odd_grads(g_odd)
    dh, gl[0] = _ffn_bwd(dh, wl[0], s_ff0, "l0")
    send_early = _pack([go[n].reshape(w[n].shape) for n in _REPL_EARLY], f32)
    dh = reduce_start("reduce_ffn0", dict(
        matrices=[(n, 0, g_odd[n]) for n in ("s5_w_glu_a", "s5_w_glu_b")]
        + [(n, 0, gl[0][n]) for n in ("ffn_w_up", "ffn_w_down")],
        packs=[("repl_early", send_early, sds((N_DEV,) + send_early.shape, f32), _whole)]), dh)
    dh, g = _xattn_bwd(dh, mem, wl[0], s_xa0, "l0")
    gl[0].update(g)
    dh = reduce_start("reduce_xa0", dict(matrices=[(n, 0, g[n]) for n in ("xa_wq", "xa_wk", "xa_wv", "xa_wo")]), dh)
    grad_x, g_even = _mixer_bwd(
        dh, cos, sin, we, s_mix0, "l0",
        on_w_out=lambda grad, dmixin: reduce_start("reduce_w_out", dict(matrices=[("mix_w_out", 0, grad)]), dmixin))

    ge = _even_grads(g_even)
    cat = lambda n: jnp.concatenate([gl[0][n], gl[1][n]], axis=0)
    rg = dict(ge)
    rg["norm_mix"] = jnp.concatenate([ge["norm_mix"], go["norm_mix"]], axis=0)
    for n in ("norm_xa", "norm_mem", "norm_ffn", "xa_q_norm", "xa_k_norm"):
        rg[n] = cat(n)
    rg["ffn_conv_b"] = _unpad_groups(cat("ffn_conv_b"), 1)
    sg = dict(mla_w_uq=ge["mla_w_uq"], mla_w_ukv=ge["mla_w_ukv"], s5_d=go["s5_d"],
              ffn_conv_w=jnp.stack([gl[0]["ffn_conv_w"], gl[1]["ffn_conv_w"]]))
    send_small = _pack([_to_shards(sg[n], _SHARDED[n]) for n in _SMALL_SHARDED], f32, lead=True)
    send_late = _pack([rg[n].reshape(w[n].shape) for n in _REPL_LATE], f32)
    w_in_rows = w["mix_w_in"].shape[2]
    last = _Exchange("reduce_last")
    last.add(g_even["mix_w_in_t"], sds((N_DEV, w_in_rows, d_model), f32), _rows_of(w_in_rows), _slot, "mix_w_in")
    last.add(send_small, sds(send_small.shape, f32), _slot, _slot, "small")
    last.add(send_late, sds((N_DEV,) + send_late.shape, f32), _whole, _slot, "repl_late")
    state_last = last.begin(False)
    slots = {}
    for ex, state in reduces:
        slots.update(ex.finish(state, [grad_x, last.token]))

    me_index = (4 * lax.axis_index("x") + 2 * lax.axis_index("y") + lax.axis_index("c")).astype(jnp.int32).reshape(1)

    def rows_block(r, c):
        return lambda tr: ((tr, c), lambda i, me: (me[0] * (r // tr) + i, 0))

    def own_block(n):
        r, c = part_shape[n]
        if n == "ffn_w_up":
            return lambda tr: ((tr, c), lambda i, me: (i, me[0]))
        return rows_block(r, c)

    out = [{}, {}, {}, {}]
    unpad = dict(ffn_w_up=2, ffn_conv_w=2, ffn_w_down=1)
    for n in matrices:
        layers = range(we_[n].shape[0])
        res = _sum_adam(me_index, [slots[(n, layer)] for layer in layers], [own_grad[(n, layer)] for layer in layers],
                        own_block(n), we_[n], me_[n], ve_[n], name=f"adam_{n}")
        for k in range(4):
            out[k][n] = _unpad_groups(res[k], unpad[n]) if n in unpad else res[k]
    pk = lambda d, order: _pack([d[n] for n in order], f32)[None]
    whole_rows = lambda tr: ((tr, _LANES), lambda i, me: (i, 0))
    res_early = _sum_adam(me_index, [slots["repl_early"]], [send_early], whole_rows, pk(w, _REPL_EARLY),
                          pk(m, _REPL_EARLY), pk(v, _REPL_EARLY), name="adam_repl_early")
    for k in range(4):
        out[k].update(zip(_REPL_EARLY, _unpack(res_early[k][0], [w[n].shape for n in _REPL_EARLY])))
    done = [out[k][n] for k in range(4) for n in matrices + _REPL_EARLY]
    slots = last.finish(state_last, done)
    transposed = lambda d: jnp.swapaxes(d["mix_w_in"], 1, 2)
    res_w_in = _sum_adam(me_index, [slots["mix_w_in"]], [g_even["mix_w_in_t"]], rows_block(w_in_rows, d_model),
                         transposed(w), transposed(m), transposed(v), name="adam_mix_w_in")
    res_small = _sum_adam(me_index, [slots["small"]], [send_small],
                          lambda tr: ((1, tr, _LANES), lambda i, me: (me[0], i, 0)),
                          pk(we_, _SMALL_SHARDED), pk(me_, _SMALL_SHARDED), pk(ve_, _SMALL_SHARDED), name="adam_small")
    res_late = _sum_adam(me_index, [slots["repl_late"]], [send_late], whole_rows, pk(w, _REPL_LATE), pk(m, _REPL_LATE),
                         pk(v, _REPL_LATE), name="adam_repl_late")
    for k in range(4):
        out[k]["mix_w_in"] = jnp.swapaxes(res_w_in[k], 1, 2)
        for n, a in zip(_SMALL_SHARDED, _unpack(res_small[k][0], [we_[n].shape for n in _SMALL_SHARDED])):
            out[k][n] = _unpad_groups(a, unpad[n]) if n in unpad else a
        out[k].update(zip(_REPL_LATE, _unpack(res_late[k][0], [w[n].shape for n in _REPL_LATE])))
    return loss, grad_x, out


_INPUTS = tuple("""x, mem, positions, norm_mix, norm_xa, norm_mem, norm_ffn, xa_wq, xa_wk, xa_wv, xa_wo, xa_q_norm, xa_k_norm, ffn_w_up, ffn_conv_w, ffn_conv_b, ffn_w_down, hg_lb_logits, mix_w_in, hg_out_norm, mla_q_a_norm, mla_w_uq, mla_kv_a_norm, mla_w_ukv, mla_qn_nope, mla_qn_rope, mla_kn_nope, mla_kn_rope, mix_w_out, s5_lam_re, s5_lam_im, s5_log_dt, s5_b_re, s5_b_im, s5_c_re, s5_c_im, s5_d, s5_w_glu_a, s5_w_glu_b, loss_target, m_norm_mix, m_norm_xa, m_norm_mem, m_norm_ffn, m_xa_wq, m_xa_wk, m_xa_wv, m_xa_wo, m_xa_q_norm, m_xa_k_norm, m_ffn_w_up, m_ffn_conv_w, m_ffn_conv_b, m_ffn_w_down, m_hg_lb_logits, m_mix_w_in, m_hg_out_norm, m_mla_q_a_norm, m_mla_w_uq, m_mla_kv_a_norm, m_mla_w_ukv, m_mla_qn_nope, m_mla_qn_rope, m_mla_kn_nope, m_mla_kn_rope, m_mix_w_out, m_s5_lam_re, m_s5_lam_im, m_s5_log_dt, m_s5_b_re, m_s5_b_im, m_s5_c_re, m_s5_c_im, m_s5_d, m_s5_w_glu_a, m_s5_w_glu_b, v_norm_mix, v_norm_xa, v_norm_mem, v_norm_ffn, v_xa_wq, v_xa_wk, v_xa_wv, v_xa_wo, v_xa_q_norm, v_xa_k_norm, v_ffn_w_up, v_ffn_conv_w, v_ffn_conv_b, v_ffn_w_down, v_hg_lb_logits, v_mix_w_in, v_hg_out_norm, v_mla_q_a_norm, v_mla_w_uq, v_mla_kv_a_norm, v_mla_w_ukv, v_mla_qn_nope, v_mla_qn_rope, v_mla_kn_nope, v_mla_kn_rope, v_mix_w_out, v_s5_lam_re, v_s5_lam_im, v_s5_log_dt, v_s5_b_re, v_s5_b_im, v_s5_c_re, v_s5_c_im, v_s5_d, v_s5_w_glu_a, v_s5_w_glu_b""".replace(" ", "").split(","))


def kernel(x, mem, positions, norm_mix, norm_xa, norm_mem, norm_ffn, xa_wq, xa_wk, xa_wv, xa_wo, xa_q_norm, xa_k_norm, ffn_w_up, ffn_conv_w, ffn_conv_b, ffn_w_down, hg_lb_logits, mix_w_in, hg_out_norm, mla_q_a_norm, mla_w_uq, mla_kv_a_norm, mla_w_ukv, mla_qn_nope, mla_qn_rope, mla_kn_nope, mla_kn_rope, mix_w_out, s5_lam_re, s5_lam_im, s5_log_dt, s5_b_re, s5_b_im, s5_c_re, s5_c_im, s5_d, s5_w_glu_a, s5_w_glu_b, loss_target, m_norm_mix, m_norm_xa, m_norm_mem, m_norm_ffn, m_xa_wq, m_xa_wk, m_xa_wv, m_xa_wo, m_xa_q_norm, m_xa_k_norm, m_ffn_w_up, m_ffn_conv_w, m_ffn_conv_b, m_ffn_w_down, m_hg_lb_logits, m_mix_w_in, m_hg_out_norm, m_mla_q_a_norm, m_mla_w_uq, m_mla_kv_a_norm, m_mla_w_ukv, m_mla_qn_nope, m_mla_qn_rope, m_mla_kn_nope, m_mla_kn_rope, m_mix_w_out, m_s5_lam_re, m_s5_lam_im, m_s5_log_dt, m_s5_b_re, m_s5_b_im, m_s5_c_re, m_s5_c_im, m_s5_d, m_s5_w_glu_a, m_s5_w_glu_b, v_norm_mix, v_norm_xa, v_norm_mem, v_norm_ffn, v_xa_wq, v_xa_wk, v_xa_wv, v_xa_wo, v_xa_q_norm, v_xa_k_norm, v_ffn_w_up, v_ffn_conv_w, v_ffn_conv_b, v_ffn_w_down, v_hg_lb_logits, v_mix_w_in, v_hg_out_norm, v_mla_q_a_norm, v_mla_w_uq, v_mla_kv_a_norm, v_mla_w_ukv, v_mla_qn_nope, v_mla_qn_rope, v_mla_kn_nope, v_mla_kn_rope, v_mix_w_out, v_s5_lam_re, v_s5_lam_im, v_s5_log_dt, v_s5_b_re, v_s5_b_im, v_s5_c_re, v_s5_c_im, v_s5_d, v_s5_w_glu_a, v_s5_w_glu_b):
    vals = dict(zip(_INPUTS, (x, mem, positions, norm_mix, norm_xa, norm_mem, norm_ffn, xa_wq, xa_wk, xa_wv, xa_wo, xa_q_norm, xa_k_norm, ffn_w_up, ffn_conv_w, ffn_conv_b, ffn_w_down, hg_lb_logits, mix_w_in, hg_out_norm, mla_q_a_norm, mla_w_uq, mla_kv_a_norm, mla_w_ukv, mla_qn_nope, mla_qn_rope, mla_kn_nope, mla_kn_rope, mix_w_out, s5_lam_re, s5_lam_im, s5_log_dt, s5_b_re, s5_b_im, s5_c_re, s5_c_im, s5_d, s5_w_glu_a, s5_w_glu_b, loss_target, m_norm_mix, m_norm_xa, m_norm_mem, m_norm_ffn, m_xa_wq, m_xa_wk, m_xa_wv, m_xa_wo, m_xa_q_norm, m_xa_k_norm, m_ffn_w_up, m_ffn_conv_w, m_ffn_conv_b, m_ffn_w_down, m_hg_lb_logits, m_mix_w_in, m_hg_out_norm, m_mla_q_a_norm, m_mla_w_uq, m_mla_kv_a_norm, m_mla_w_ukv, m_mla_qn_nope, m_mla_qn_rope, m_mla_kn_nope, m_mla_kn_rope, m_mix_w_out, m_s5_lam_re, m_s5_lam_im, m_s5_log_dt, m_s5_b_re, m_s5_b_im, m_s5_c_re, m_s5_c_im, m_s5_d, m_s5_w_glu_a, m_s5_w_glu_b, v_norm_mix, v_norm_xa, v_norm_mem, v_norm_ffn, v_xa_wq, v_xa_wk, v_xa_wv, v_xa_wo, v_xa_q_norm, v_xa_k_norm, v_ffn_w_up, v_ffn_conv_w, v_ffn_conv_b, v_ffn_w_down, v_hg_lb_logits, v_mix_w_in, v_hg_out_norm, v_mla_q_a_norm, v_mla_w_uq, v_mla_kv_a_norm, v_mla_w_ukv, v_mla_qn_nope, v_mla_qn_rope, v_mla_kn_nope, v_mla_kn_rope, v_mix_w_out, v_s5_lam_re, v_s5_lam_im, v_s5_log_dt, v_s5_b_re, v_s5_b_im, v_s5_c_re, v_s5_c_im, v_s5_d, v_s5_w_glu_a, v_s5_w_glu_b)))
    w = {n: vals[n] for n in _WEIGHTS}
    m = {n: vals["m_" + n] for n in _WEIGHTS}
    v = {n: vals["v_" + n] for n in _WEIGHTS}
    loss, grad_x, res = _train_step(vals["x"][0], vals["mem"][0], vals["positions"][0], vals["loss_target"][0],
                                    w, m, v)
    loss = lax.psum(loss[0, 0], ("x", "y", "c"))
    return (loss, grad_x[None], *[r[n] for r in res for n in _WEIGHTS])
```
